```python
import jax, jax.numpy as jnp
from jax import lax
import numpy as np

D_MODEL = 1024
BATCH = 8
SEQ = 4096
DEPTH = 1

GRID_W = 64
CTX_LEN = 256
D_SSD = 1024
SSD_HEAD_DIM = 64
SSD_HEADS = D_SSD // SSD_HEAD_DIM
SSD_GROUPS = 2
SSD_HPG = SSD_HEADS // SSD_GROUPS
SSD_STATE = 128
SSD_CONV = 5
SSD_CHUNK = 128
SSD_GN = SSD_GROUPS * SSD_STATE
XBC_DIM = D_SSD + 2 * SSD_GN
D_GM = 1024
GM_GROUPS = 8
GM_GROUP_DIM = D_GM // GM_GROUPS
GM_CHUNK = 128
ROWS_PER_CHUNK = GM_CHUNK // GRID_W
D_FF = -(-8 * D_MODEL // (3 * 256)) * 256
D_PROJ = D_SSD + XBC_DIM + 2 * SSD_HEADS + 2 * D_GM + 2 * D_MODEL
ALPHA = (2 * DEPTH) ** 0.25
BETA = (8 * DEPTH) ** -0.25
LN_EPS = 1e-5

kernel_name = 'hybrid_ssd_gmlp_dit_block'


def layer_norm(x, g, b):
    xf = x.astype(jnp.float32)
    mu = jnp.mean(xf, axis=-1, keepdims=True)
    var = jnp.mean(jnp.square(xf - mu), axis=-1, keepdims=True)
    return ((xf - mu) * lax.rsqrt(var + LN_EPS) * g + b).astype(x.dtype)


def gated_rms_norm(y, z, g):
    h = (y * jax.nn.silu(z)).astype(jnp.float32)
    h = h * lax.rsqrt(jnp.mean(jnp.square(h), axis=-1, keepdims=True) + LN_EPS)
    return (h * g).astype(y.dtype)


def dwconv_centred(x, w, b):
    ch = x.shape[-1]
    pad = SSD_CONV // 2
    y = lax.conv_general_dilated(x, w[:, None, :], window_strides=(1,), padding=[(pad, pad)],
                                 dimension_numbers=('NWC', 'WIO', 'NWC'), feature_group_count=ch)
    return y + b


def split_proj(p):
    o1 = D_SSD
    o2 = o1 + XBC_DIM
    o3 = o2 + 2 * SSD_HEADS
    o4 = o3 + D_GM
    o5 = o4 + D_GM
    return p[..., :o1], p[..., o1:o2], p[..., o2:o3], p[..., o3:o4], p[..., o4:o5], p[..., o5:]


def ssd_scan(xs, dt, A, Bm, Cm, h0):
    bsz, L, _ = xs.shape
    nc = L // SSD_CHUNK
    x = xs.reshape(bsz, nc, SSD_CHUNK, SSD_GROUPS, SSD_HPG, SSD_HEAD_DIM)
    dt = dt.reshape(bsz, nc, SSD_CHUNK, SSD_GROUPS, SSD_HPG)
    Bm = Bm.reshape(bsz, nc, SSD_CHUNK, SSD_GROUPS, SSD_STATE)
    Cm = Cm.reshape(bsz, nc, SSD_CHUNK, SSD_GROUPS, SSD_STATE)
    acum = jnp.cumsum(dt * A.reshape(SSD_GROUPS, SSD_HPG), axis=2)
    seg = acum[:, :, :, None] - acum[:, :, None, :]
    tri = jnp.tril(jnp.ones((SSD_CHUNK, SSD_CHUNK), dtype=bool))[:, :, None, None]
    lmat = jnp.where(tri, jnp.exp(jnp.where(tri, seg, 0.0)), 0.0)
    cb = jnp.einsum('bcign,bcjgn->bcijg', Cm, Bm)
    wgt = cb[..., None] * lmat * dt[:, :, None]
    y_diag = jnp.einsum('bcijgh,bcjghp->bcighp', wgt, x)
    decay_to_end = jnp.exp(acum[:, :, -1:] - acum)
    states = jnp.einsum('bcjgn,bcjgh,bcjghp->bcghpn', Bm, decay_to_end * dt, x)
    chunk_decay = jnp.exp(acum[:, :, -1])

    def step(h, inp):
        dec, st = inp
        h_new = (dec[..., None, None] * h + st).astype(h.dtype)
        return h_new, h

    h_final, h_prev = lax.scan(step, h0.astype(states.dtype),
                               (jnp.moveaxis(chunk_decay, 1, 0), jnp.moveaxis(states, 1, 0)))
    h_prev = jnp.moveaxis(h_prev, 0, 1)
    y_off = jnp.einsum('bcign,bcghpn,bcigh->bcighp', Cm, h_prev, jnp.exp(acum))
    return (y_diag + y_off).reshape(bsz, L, D_SSD), h_final


def ssd_branch(z, xbc, dt_raw, conv_w, conv_b, dt_bias, a_log, d_skip, norm_g, h0_f, h0_b):
    bsz, L, _ = xbc.shape
    xbc = jax.nn.silu(dwconv_centred(xbc, conv_w, conv_b))
    xs, Bm, Cm = xbc[..., :D_SSD], xbc[..., D_SSD:D_SSD + SSD_GN], xbc[..., D_SSD + SSD_GN:]
    dt = jax.nn.softplus(dt_raw.reshape(bsz, L, 2, SSD_HEADS) + dt_bias)
    A = -jnp.exp(a_log)
    y_f, s_f = ssd_scan(xs, dt[:, :, 0], A[0], Bm, Cm, h0_f)
    y_b, s_b = ssd_scan(jnp.flip(xs, 1), jnp.flip(dt[:, :, 1], 1), A[1],
                        jnp.flip(Bm, 1), jnp.flip(Cm, 1), h0_b)
    skip = (xs.reshape(bsz, L, SSD_HEADS, SSD_HEAD_DIM) * (d_skip[0] + d_skip[1])[:, None]).reshape(bsz, L, D_SSD)
    y = y_f + jnp.flip(y_b, 1) + skip
    return gated_rms_norm(y, z, norm_g), s_f, s_b


def spatial_gating(u, v, n_chunks, g, b, w_s, b_s):
    bsz = u.shape[0]
    v = layer_norm(v, g, b).reshape(bsz, n_chunks, GM_CHUNK, GM_GROUPS, GM_GROUP_DIM)
    mixed = jnp.einsum('gpq,bnqgc->bnpgc', w_s, v) + b_s.T[:, :, None]
    return u * mixed.reshape(u.shape)


def branch_merge(y_ssd, u, v, gates, n_chunks, gm_g, gm_b, w_s, b_s, b_gate, w_ssd_proj, w_gm_proj, w_out):
    y_gm = spatial_gating(jax.nn.gelu(u), jax.nn.gelu(v), n_chunks, gm_g, gm_b, w_s, b_s)
    g = jax.nn.sigmoid(gates + b_gate)
    merged = g[..., :D_MODEL] * (y_ssd @ w_ssd_proj) + g[..., D_MODEL:] * (y_gm @ w_gm_proj)
    return merged @ w_out


def swiglu(h, w1, w3, w2):
    return (jax.nn.silu(h @ w1) * (h @ w3)) @ w2


def _fwd_setup_inputs(seed: int = 0) -> dict:
    key = jax.random.key(seed)
    ks = jax.random.split(key, 32)
    f32 = jnp.float32

    def nrm(k, shape, scale=1.0):
        return jax.random.normal(k, shape, f32) * scale

    dt0 = jnp.exp(jax.random.uniform(ks[8], (DEPTH, 2, SSD_HEADS), f32, np.log(1e-3), np.log(1e-1)))
    return {
        'x': nrm(ks[0], (BATCH, SEQ, D_MODEL)),
        'c': nrm(ks[1], (BATCH, D_MODEL)),
        'ctx': nrm(ks[2], (BATCH, CTX_LEN, D_MODEL)),
        'c_ctx': nrm(ks[3], (D_MODEL,)),
        'ln0_g': 1.0 + nrm(ks[4], (D_MODEL,), 0.01),
        'ln0_b': nrm(ks[5], (D_MODEL,), 0.01),
        'w_ada': nrm(ks[6], (DEPTH, D_MODEL, 6 * D_MODEL), 0.3 * D_MODEL ** -0.5),
        'b_ada': nrm(ks[7], (DEPTH, 6 * D_MODEL), 0.01),
        'w_in': nrm(ks[9], (DEPTH, D_MODEL, D_PROJ), D_MODEL ** -0.5),
        'conv_w': nrm(ks[10], (DEPTH, SSD_CONV, XBC_DIM), SSD_CONV ** -0.5),
        'conv_b': nrm(ks[11], (DEPTH, XBC_DIM), 0.01),
        'dt_bias': dt0 + jnp.log(-jnp.expm1(-dt0)),
        'a_log': jnp.log(jax.random.uniform(ks[12], (DEPTH, 2, SSD_HEADS), f32, 1.0, 16.0)),
        'd_skip': 1.0 + nrm(ks[13], (DEPTH, 2, SSD_HEADS), 0.01),
        'ssd_norm_g': 1.0 + nrm(ks[14], (DEPTH, D_SSD), 0.01),
        'gm_norm_g': 1.0 + nrm(ks[15], (DEPTH, D_GM), 0.01),
        'gm_norm_b': nrm(ks[16], (DEPTH, D_GM), 0.01),
        'w_spatial': nrm(ks[17], (DEPTH, GM_GROUPS, GM_CHUNK, GM_CHUNK), GM_CHUNK ** -0.5),
        'b_spatial': 1.0 + nrm(ks[18], (DEPTH, GM_GROUPS, GM_CHUNK), 0.01),
        'b_gate': nrm(ks[19], (DEPTH, 2 * D_MODEL), 0.01),
        'w_ssd_proj': nrm(ks[20], (DEPTH, D_SSD, D_MODEL), BETA * D_SSD ** -0.5),
        'w_gm_proj': nrm(ks[21], (DEPTH, D_GM, D_MODEL), BETA * D_GM ** -0.5),
        'w_out': nrm(ks[22], (DEPTH, D_MODEL, D_MODEL), BETA * D_MODEL ** -0.5),
        'ln1_g': 1.0 + nrm(ks[23], (DEPTH, D_MODEL), 0.01),
        'ln1_b': nrm(ks[24], (DEPTH, D_MODEL), 0.01),
        'w_ff1': nrm(ks[25], (DEPTH, D_MODEL, D_FF), BETA * D_MODEL ** -0.5),
        'w_ff3': nrm(ks[26], (DEPTH, D_MODEL, D_FF), BETA * D_MODEL ** -0.5),
        'w_ff2': nrm(ks[27], (DEPTH, D_FF, D_MODEL), BETA * D_FF ** -0.5),
        'ln2_g': 1.0 + nrm(ks[28], (DEPTH, D_MODEL), 0.01),
        'ln2_b': nrm(ks[29], (DEPTH, D_MODEL), 0.01),
    }


def _fwd_reference(x, c, ctx, c_ctx, ln0_g, ln0_b, w_ada, b_ada, w_in, conv_w, conv_b, dt_bias, a_log, d_skip,
              ssd_norm_g, gm_norm_g, gm_norm_b, w_spatial, b_spatial, b_gate, w_ssd_proj, w_gm_proj, w_out,
              ln1_g, ln1_b, w_ff1, w_ff3, w_ff2, ln2_g, ln2_b):
    x = layer_norm(x, ln0_g, ln0_b)
    ctx_h = layer_norm(ctx, ln0_g, ln0_b)
    rows = x.shape[1] // GRID_W
    n_lat_chunks = rows // ROWS_PER_CHUNK
    n_ctx_chunks = ctx.shape[1] // GM_CHUNK
    zero_state = jnp.zeros((ctx.shape[0], SSD_GROUPS, SSD_HPG, SSD_HEAD_DIM, SSD_STATE), x.dtype)
    for l in range(DEPTH):
        mod_x = (jax.nn.silu(c) @ w_ada[l] + b_ada[l])[:, None, :]
        mod_c = jax.nn.silu(c_ctx) @ w_ada[l] + b_ada[l]
        sh1x, sc1x, g1x, sh2x, sc2x, g2x = jnp.split(mod_x, 6, axis=-1)
        sh1c, sc1c, g1c, sh2c, sc2c, g2c = jnp.split(mod_c, 6, axis=-1)
        ssd_p = (conv_w[l], conv_b[l], dt_bias[l], a_log[l], d_skip[l], ssd_norm_g[l])
        mrg_p = (gm_norm_g[l], gm_norm_b[l], w_spatial[l], b_spatial[l], b_gate[l],
                 w_ssd_proj[l], w_gm_proj[l], w_out[l])
        zc, xbcc, dtc, uc, vc, gc = split_proj((ctx_h * (1.0 + sc1c) + sh1c) @ w_in[l])
        yc, s_f, s_b = ssd_branch(zc, xbcc, dtc, *ssd_p, zero_state, zero_state)
        zx, xbcx, dtx, ux, vx, gx = split_proj((x * (1.0 + sc1x) + sh1x) @ w_in[l])
        yx, _, _ = ssd_branch(zx, xbcx, dtx, *ssd_p, s_f, s_b)
        out_x = branch_merge(yx, ux, vx, gx, n_lat_chunks, *mrg_p)
        x = layer_norm(ALPHA * x + g1x * out_x, ln1_g[l], ln1_b[l])
        x = layer_norm(ALPHA * x + g2x * swiglu(x * (1.0 + sc2x) + sh2x, w_ff1[l], w_ff3[l], w_ff2[l]),
                       ln2_g[l], ln2_b[l])
        if l < DEPTH - 1:
            out_c = branch_merge(yc, uc, vc, gc, n_ctx_chunks, *mrg_p)
            ctx_h = layer_norm(ALPHA * ctx_h + g1c * out_c, ln1_g[l], ln1_b[l])
            ctx_h = layer_norm(ALPHA * ctx_h + g2c * swiglu(ctx_h * (1.0 + sc2c) + sh2c, w_ff1[l], w_ff3[l], w_ff2[l]),
                               ln2_g[l], ln2_b[l])
    return x


import jax as _jax
import jax.numpy as _jnp

TWIN_FORMAT = 'train_step'
FWD_PARAMS = ['x', 'c', 'ctx', 'c_ctx', 'ln0_g', 'ln0_b', 'w_ada', 'b_ada', 'w_in', 'conv_w', 'conv_b', 'dt_bias', 'a_log', 'd_skip', 'ssd_norm_g', 'gm_norm_g', 'gm_norm_b', 'w_spatial', 'b_spatial', 'b_gate', 'w_ssd_proj', 'w_gm_proj', 'w_out', 'ln1_g', 'ln1_b', 'w_ff1', 'w_ff3', 'w_ff2', 'ln2_g', 'ln2_b']
TWIN_WEIGHTS = ['c_ctx', 'ln0_g', 'ln0_b', 'w_ada', 'b_ada', 'w_in', 'conv_w', 'conv_b', 'dt_bias', 'a_log', 'd_skip', 'ssd_norm_g', 'gm_norm_g', 'gm_norm_b', 'w_spatial', 'b_spatial', 'b_gate', 'w_ssd_proj', 'w_gm_proj', 'w_out', 'ln1_g', 'ln1_b', 'w_ff1', 'w_ff3', 'w_ff2', 'ln2_g', 'ln2_b']
TWIN_DIFF_INPUT = 'x'
TWIN_INPUTS = ['x', 'c', 'ctx', 'c_ctx', 'ln0_g', 'ln0_b', 'w_ada', 'b_ada', 'w_in', 'conv_w', 'conv_b', 'dt_bias', 'a_log', 'd_skip', 'ssd_norm_g', 'gm_norm_g', 'gm_norm_b', 'w_spatial', 'b_spatial', 'b_gate', 'w_ssd_proj', 'w_gm_proj', 'w_out', 'ln1_g', 'ln1_b', 'w_ff1', 'w_ff3', 'w_ff2', 'ln2_g', 'ln2_b', 'loss_target', 'm_c_ctx', 'm_ln0_g', 'm_ln0_b', 'm_w_ada', 'm_b_ada', 'm_w_in', 'm_conv_w', 'm_conv_b', 'm_dt_bias', 'm_a_log', 'm_d_skip', 'm_ssd_norm_g', 'm_gm_norm_g', 'm_gm_norm_b', 'm_w_spatial', 'm_b_spatial', 'm_b_gate', 'm_w_ssd_proj', 'm_w_gm_proj', 'm_w_out', 'm_ln1_g', 'm_ln1_b', 'm_w_ff1', 'm_w_ff3', 'm_w_ff2', 'm_ln2_g', 'm_ln2_b', 'v_c_ctx', 'v_ln0_g', 'v_ln0_b', 'v_w_ada', 'v_b_ada', 'v_w_in', 'v_conv_w', 'v_conv_b', 'v_dt_bias', 'v_a_log', 'v_d_skip', 'v_ssd_norm_g', 'v_gm_norm_g', 'v_gm_norm_b', 'v_w_spatial', 'v_b_spatial', 'v_b_gate', 'v_w_ssd_proj', 'v_w_gm_proj', 'v_w_out', 'v_ln1_g', 'v_ln1_b', 'v_w_ff1', 'v_w_ff3', 'v_w_ff2', 'v_ln2_g', 'v_ln2_b']
TWIN_OUTPUTS = ['loss', 'grad_x', 'grad_c_ctx', 'grad_ln0_g', 'grad_ln0_b', 'grad_w_ada', 'grad_b_ada', 'grad_w_in', 'grad_conv_w', 'grad_conv_b', 'grad_dt_bias', 'grad_a_log', 'grad_d_skip', 'grad_ssd_norm_g', 'grad_gm_norm_g', 'grad_gm_norm_b', 'grad_w_spatial', 'grad_b_spatial', 'grad_b_gate', 'grad_w_ssd_proj', 'grad_w_gm_proj', 'grad_w_out', 'grad_ln1_g', 'grad_ln1_b', 'grad_w_ff1', 'grad_w_ff3', 'grad_w_ff2', 'grad_ln2_g', 'grad_ln2_b', 'delta_c_ctx', 'delta_ln0_g', 'delta_ln0_b', 'delta_w_ada', 'delta_b_ada', 'delta_w_in', 'delta_conv_w', 'delta_conv_b', 'delta_dt_bias', 'delta_a_log', 'delta_d_skip', 'delta_ssd_norm_g', 'delta_gm_norm_g', 'delta_gm_norm_b', 'delta_w_spatial', 'delta_b_spatial', 'delta_b_gate', 'delta_w_ssd_proj', 'delta_w_gm_proj', 'delta_w_out', 'delta_ln1_g', 'delta_ln1_b', 'delta_w_ff1', 'delta_w_ff3', 'delta_w_ff2', 'delta_ln2_g', 'delta_ln2_b', 'new_m_c_ctx', 'new_m_ln0_g', 'new_m_ln0_b', 'new_m_w_ada', 'new_m_b_ada', 'new_m_w_in', 'new_m_conv_w', 'new_m_conv_b', 'new_m_dt_bias', 'new_m_a_log', 'new_m_d_skip', 'new_m_ssd_norm_g', 'new_m_gm_norm_g', 'new_m_gm_norm_b', 'new_m_w_spatial', 'new_m_b_spatial', 'new_m_b_gate', 'new_m_w_ssd_proj', 'new_m_w_gm_proj', 'new_m_w_out', 'new_m_ln1_g', 'new_m_ln1_b', 'new_m_w_ff1', 'new_m_w_ff3', 'new_m_w_ff2', 'new_m_ln2_g', 'new_m_ln2_b', 'new_v_c_ctx', 'new_v_ln0_g', 'new_v_ln0_b', 'new_v_w_ada', 'new_v_b_ada', 'new_v_w_in', 'new_v_conv_w', 'new_v_conv_b', 'new_v_dt_bias', 'new_v_a_log', 'new_v_d_skip', 'new_v_ssd_norm_g', 'new_v_gm_norm_g', 'new_v_gm_norm_b', 'new_v_w_spatial', 'new_v_b_spatial', 'new_v_b_gate', 'new_v_w_ssd_proj', 'new_v_w_gm_proj', 'new_v_w_out', 'new_v_ln1_g', 'new_v_ln1_b', 'new_v_w_ff1', 'new_v_w_ff3', 'new_v_w_ff2', 'new_v_ln2_g', 'new_v_ln2_b']
TWIN_LEAF_KINDS = {'loss': 'loss', 'grad_x': 'grad_x', 'grad_c_ctx': 'grad_w', 'grad_ln0_g': 'grad_w', 'grad_ln0_b': 'grad_w', 'grad_w_ada': 'grad_w', 'grad_b_ada': 'grad_w', 'grad_w_in': 'grad_w', 'grad_conv_w': 'grad_w', 'grad_conv_b': 'grad_w', 'grad_dt_bias': 'grad_w', 'grad_a_log': 'grad_w', 'grad_d_skip': 'grad_w', 'grad_ssd_norm_g': 'grad_w', 'grad_gm_norm_g': 'grad_w', 'grad_gm_norm_b': 'grad_w', 'grad_w_spatial': 'grad_w', 'grad_b_spatial': 'grad_w', 'grad_b_gate': 'grad_w', 'grad_w_ssd_proj': 'grad_w', 'grad_w_gm_proj': 'grad_w', 'grad_w_out': 'grad_w', 'grad_ln1_g': 'grad_w', 'grad_ln1_b': 'grad_w', 'grad_w_ff1': 'grad_w', 'grad_w_ff3': 'grad_w', 'grad_w_ff2': 'grad_w', 'grad_ln2_g': 'grad_w', 'grad_ln2_b': 'grad_w', 'delta_c_ctx': 'delta_w', 'delta_ln0_g': 'delta_w', 'delta_ln0_b': 'delta_w', 'delta_w_ada': 'delta_w', 'delta_b_ada': 'delta_w', 'delta_w_in': 'delta_w', 'delta_conv_w': 'delta_w', 'delta_conv_b': 'delta_w', 'delta_dt_bias': 'delta_w', 'delta_a_log': 'delta_w', 'delta_d_skip': 'delta_w', 'delta_ssd_norm_g': 'delta_w', 'delta_gm_norm_g': 'delta_w', 'delta_gm_norm_b': 'delta_w', 'delta_w_spatial': 'delta_w', 'delta_b_spatial': 'delta_w', 'delta_b_gate': 'delta_w', 'delta_w_ssd_proj': 'delta_w', 'delta_w_gm_proj': 'delta_w', 'delta_w_out': 'delta_w', 'delta_ln1_g': 'delta_w', 'delta_ln1_b': 'delta_w', 'delta_w_ff1': 'delta_w', 'delta_w_ff3': 'delta_w', 'delta_w_ff2': 'delta_w', 'delta_ln2_g': 'delta_w', 'delta_ln2_b': 'delta_w', 'new_m_c_ctx': 'new_m', 'new_m_ln0_g': 'new_m', 'new_m_ln0_b': 'new_m', 'new_m_w_ada': 'new_m', 'new_m_b_ada': 'new_m', 'new_m_w_in': 'new_m', 'new_m_conv_w': 'new_m', 'new_m_conv_b': 'new_m', 'new_m_dt_bias': 'new_m', 'new_m_a_log': 'new_m', 'new_m_d_skip': 'new_m', 'new_m_ssd_norm_g': 'new_m', 'new_m_gm_norm_g': 'new_m', 'new_m_gm_norm_b': 'new_m', 'new_m_w_spatial': 'new_m', 'new_m_b_spatial': 'new_m', 'new_m_b_gate': 'new_m', 'new_m_w_ssd_proj': 'new_m', 'new_m_w_gm_proj': 'new_m', 'new_m_w_out': 'new_m', 'new_m_ln1_g': 'new_m', 'new_m_ln1_b': 'new_m', 'new_m_w_ff1': 'new_m', 'new_m_w_ff3': 'new_m', 'new_m_w_ff2': 'new_m', 'new_m_ln2_g': 'new_m', 'new_m_ln2_b': 'new_m', 'new_v_c_ctx': 'new_v', 'new_v_ln0_g': 'new_v', 'new_v_ln0_b': 'new_v', 'new_v_w_ada': 'new_v', 'new_v_b_ada': 'new_v', 'new_v_w_in': 'new_v', 'new_v_conv_w': 'new_v', 'new_v_conv_b': 'new_v', 'new_v_dt_bias': 'new_v', 'new_v_a_log': 'new_v', 'new_v_d_skip': 'new_v', 'new_v_ssd_norm_g': 'new_v', 'new_v_gm_norm_g': 'new_v', 'new_v_gm_norm_b': 'new_v', 'new_v_w_spatial': 'new_v', 'new_v_b_spatial': 'new_v', 'new_v_b_gate': 'new_v', 'new_v_w_ssd_proj': 'new_v', 'new_v_w_gm_proj': 'new_v', 'new_v_w_out': 'new_v', 'new_v_ln1_g': 'new_v', 'new_v_ln1_b': 'new_v', 'new_v_w_ff1': 'new_v', 'new_v_w_ff3': 'new_v', 'new_v_w_ff2': 'new_v', 'new_v_ln2_g': 'new_v', 'new_v_ln2_b': 'new_v'}


def _forward(args):
    return _fwd_reference(*[args[k] for k in FWD_PARAMS])


def _output_shape():
    out = _jax.eval_shape(lambda: _forward(_fwd_setup_inputs(0)))
    return out.shape, out.dtype

N_MICROBATCH = 1
ADAM_LR = 0.001
ADAM_B1 = 0.9
ADAM_B2 = 0.999
ADAM_EPS = 1e-08
ADAM_WD = 0.01
ADAM_STEP = 10
PER_EXAMPLE_BATCH_AXIS = {'x': 0, 'c': 0, 'ctx': 0, 'loss_target': 0}
SHARED_INPUTS = []
_WEIGHT_DTYPES = {'c_ctx': _jnp.float32, 'ln0_g': _jnp.float32, 'ln0_b': _jnp.float32, 'w_ada': _jnp.float32, 'b_ada': _jnp.float32, 'w_in': _jnp.float32, 'conv_w': _jnp.float32, 'conv_b': _jnp.float32, 'dt_bias': _jnp.float32, 'a_log': _jnp.float32, 'd_skip': _jnp.float32, 'ssd_norm_g': _jnp.float32, 'gm_norm_g': _jnp.float32, 'gm_norm_b': _jnp.float32, 'w_spatial': _jnp.float32, 'b_spatial': _jnp.float32, 'b_gate': _jnp.float32, 'w_ssd_proj': _jnp.float32, 'w_gm_proj': _jnp.float32, 'w_out': _jnp.float32, 'ln1_g': _jnp.float32, 'ln1_b': _jnp.float32, 'w_ff1': _jnp.float32, 'w_ff3': _jnp.float32, 'w_ff2': _jnp.float32, 'ln2_g': _jnp.float32, 'ln2_b': _jnp.float32}
MOMENT_SCALE = {'c_ctx': 1.930363e-04, 'ln0_g': 4.939150e-01, 'ln0_b': 2.463020e-01, 'w_ada': 1.241967e-02, 'b_ada': 1.936484e-02, 'w_in': 4.138184e-03, 'conv_w': 4.299778e-03, 'conv_b': 6.235364e-03, 'dt_bias': 7.530791e-03, 'a_log': 1.319079e-02, 'd_skip': 1.743852e-02, 'ssd_norm_g': 5.201156e-03, 'gm_norm_g': 3.974211e-03, 'gm_norm_b': 3.623455e-03, 'w_spatial': 3.578963e-03, 'b_spatial': 3.590985e-03, 'b_gate': 1.969701e-03, 'w_ssd_proj': 9.006961e-03, 'w_gm_proj': 8.669040e-03, 'w_out': 1.253016e-02, 'ln1_g': 4.940941e-01, 'ln1_b': 2.454362e-01, 'w_ff1': 3.606424e-03, 'w_ff3': 3.513613e-03, 'w_ff2': 5.837074e-03, 'ln2_g': 3.197156e+01, 'ln2_b': 4.479832e-01}


def _to_microbatches(a, axis):
    t = _jnp.moveaxis(a, axis, 0)
    t = t.reshape((N_MICROBATCH, t.shape[0] // N_MICROBATCH) + t.shape[1:])
    return _jnp.moveaxis(t, 1, axis + 1)


def setup_inputs(seed: int = 0) -> dict:
    inp = _fwd_setup_inputs(seed)
    key = _jax.random.fold_in(_jax.random.key(seed), 7919)
    shape, _ = _output_shape()
    out = dict(inp)
    out["loss_target"] = _jax.random.normal(_jax.random.fold_in(key, 0), shape, _jnp.float32)
    for i, name in enumerate(TWIN_WEIGHTS):
        w = inp[name].astype(_jnp.float32)
        if MOMENT_SCALE is None:
            s = _jnp.sqrt(_jnp.mean(_jnp.square(w)) + 1e-30)
        else:
            s = MOMENT_SCALE[name]
        km, kv = _jax.random.split(_jax.random.fold_in(key, i + 1))
        out[name] = w
        out["m_" + name] = s * _jax.random.normal(km, w.shape, _jnp.float32)
        out["v_" + name] = (s * s) * _jax.random.uniform(kv, w.shape, _jnp.float32, 0.5, 1.5)
    if N_MICROBATCH > 1:
        for name, axis in PER_EXAMPLE_BATCH_AXIS.items():
            out[name] = _to_microbatches(out[name], axis)
    return {'x': out['x'], 'c': out['c'], 'ctx': out['ctx'], 'c_ctx': out['c_ctx'], 'ln0_g': out['ln0_g'], 'ln0_b': out['ln0_b'], 'w_ada': out['w_ada'], 'b_ada': out['b_ada'], 'w_in': out['w_in'], 'conv_w': out['conv_w'], 'conv_b': out['conv_b'], 'dt_bias': out['dt_bias'], 'a_log': out['a_log'], 'd_skip': out['d_skip'], 'ssd_norm_g': out['ssd_norm_g'], 'gm_norm_g': out['gm_norm_g'], 'gm_norm_b': out['gm_norm_b'], 'w_spatial': out['w_spatial'], 'b_spatial': out['b_spatial'], 'b_gate': out['b_gate'], 'w_ssd_proj': out['w_ssd_proj'], 'w_gm_proj': out['w_gm_proj'], 'w_out': out['w_out'], 'ln1_g': out['ln1_g'], 'ln1_b': out['ln1_b'], 'w_ff1': out['w_ff1'], 'w_ff3': out['w_ff3'], 'w_ff2': out['w_ff2'], 'ln2_g': out['ln2_g'], 'ln2_b': out['ln2_b'], 'loss_target': out['loss_target'], 'm_c_ctx': out['m_c_ctx'], 'm_ln0_g': out['m_ln0_g'], 'm_ln0_b': out['m_ln0_b'], 'm_w_ada': out['m_w_ada'], 'm_b_ada': out['m_b_ada'], 'm_w_in': out['m_w_in'], 'm_conv_w': out['m_conv_w'], 'm_conv_b': out['m_conv_b'], 'm_dt_bias': out['m_dt_bias'], 'm_a_log': out['m_a_log'], 'm_d_skip': out['m_d_skip'], 'm_ssd_norm_g': out['m_ssd_norm_g'], 'm_gm_norm_g': out['m_gm_norm_g'], 'm_gm_norm_b': out['m_gm_norm_b'], 'm_w_spatial': out['m_w_spatial'], 'm_b_spatial': out['m_b_spatial'], 'm_b_gate': out['m_b_gate'], 'm_w_ssd_proj': out['m_w_ssd_proj'], 'm_w_gm_proj': out['m_w_gm_proj'], 'm_w_out': out['m_w_out'], 'm_ln1_g': out['m_ln1_g'], 'm_ln1_b': out['m_ln1_b'], 'm_w_ff1': out['m_w_ff1'], 'm_w_ff3': out['m_w_ff3'], 'm_w_ff2': out['m_w_ff2'], 'm_ln2_g': out['m_ln2_g'], 'm_ln2_b': out['m_ln2_b'], 'v_c_ctx': out['v_c_ctx'], 'v_ln0_g': out['v_ln0_g'], 'v_ln0_b': out['v_ln0_b'], 'v_w_ada': out['v_w_ada'], 'v_b_ada': out['v_b_ada'], 'v_w_in': out['v_w_in'], 'v_conv_w': out['v_conv_w'], 'v_conv_b': out['v_conv_b'], 'v_dt_bias': out['v_dt_bias'], 'v_a_log': out['v_a_log'], 'v_d_skip': out['v_d_skip'], 'v_ssd_norm_g': out['v_ssd_norm_g'], 'v_gm_norm_g': out['v_gm_norm_g'], 'v_gm_norm_b': out['v_gm_norm_b'], 'v_w_spatial': out['v_w_spatial'], 'v_b_spatial': out['v_b_spatial'], 'v_b_gate': out['v_b_gate'], 'v_w_ssd_proj': out['v_w_ssd_proj'], 'v_w_gm_proj': out['v_w_gm_proj'], 'v_w_out': out['v_w_out'], 'v_ln1_g': out['v_ln1_g'], 'v_ln1_b': out['v_ln1_b'], 'v_w_ff1': out['v_w_ff1'], 'v_w_ff3': out['v_w_ff3'], 'v_w_ff2': out['v_w_ff2'], 'v_ln2_g': out['v_ln2_g'], 'v_ln2_b': out['v_ln2_b']}


def _loss(weights, diff, rest, loss_target):
    with _jax.named_scope("forward"):
        args = {**rest, TWIN_DIFF_INPUT: diff, **{k: w.astype(_WEIGHT_DTYPES[k]) for k, w in weights.items()}}
        y = _forward(args)
    with _jax.named_scope("loss_head"):
        err = _jnp.square(y.astype(_jnp.float32) - loss_target)
        return 0.5 * _jnp.sum(_jnp.mean(err, axis=-1)) if err.ndim else 0.5 * err


def _adamw(w, g, m, v):
    m = ADAM_B1 * m + (1.0 - ADAM_B1) * g
    v = ADAM_B2 * v + (1.0 - ADAM_B2) * _jnp.square(g)
    m_hat = m / (1.0 - ADAM_B1 ** ADAM_STEP)
    v_hat = v / (1.0 - ADAM_B2 ** ADAM_STEP)
    delta = -ADAM_LR * (m_hat / (_jnp.sqrt(v_hat) + ADAM_EPS) + ADAM_WD * w)
    return delta, m, v


def reference(x, c, ctx, c_ctx, ln0_g, ln0_b, w_ada, b_ada, w_in, conv_w, conv_b, dt_bias, a_log, d_skip, ssd_norm_g, gm_norm_g, gm_norm_b, w_spatial, b_spatial, b_gate, w_ssd_proj, w_gm_proj, w_out, ln1_g, ln1_b, w_ff1, w_ff3, w_ff2, ln2_g, ln2_b, loss_target, m_c_ctx, m_ln0_g, m_ln0_b, m_w_ada, m_b_ada, m_w_in, m_conv_w, m_conv_b, m_dt_bias, m_a_log, m_d_skip, m_ssd_norm_g, m_gm_norm_g, m_gm_norm_b, m_w_spatial, m_b_spatial, m_b_gate, m_w_ssd_proj, m_w_gm_proj, m_w_out, m_ln1_g, m_ln1_b, m_w_ff1, m_w_ff3, m_w_ff2, m_ln2_g, m_ln2_b, v_c_ctx, v_ln0_g, v_ln0_b, v_w_ada, v_b_ada, v_w_in, v_conv_w, v_conv_b, v_dt_bias, v_a_log, v_d_skip, v_ssd_norm_g, v_gm_norm_g, v_gm_norm_b, v_w_spatial, v_b_spatial, v_b_gate, v_w_ssd_proj, v_w_gm_proj, v_w_out, v_ln1_g, v_ln1_b, v_w_ff1, v_w_ff3, v_w_ff2, v_ln2_g, v_ln2_b):
    given = dict(x=x, c=c, ctx=ctx, c_ctx=c_ctx, ln0_g=ln0_g, ln0_b=ln0_b, w_ada=w_ada, b_ada=b_ada, w_in=w_in, conv_w=conv_w, conv_b=conv_b, dt_bias=dt_bias, a_log=a_log, d_skip=d_skip, ssd_norm_g=ssd_norm_g, gm_norm_g=gm_norm_g, gm_norm_b=gm_norm_b, w_spatial=w_spatial, b_spatial=b_spatial, b_gate=b_gate, w_ssd_proj=w_ssd_proj, w_gm_proj=w_gm_proj, w_out=w_out, ln1_g=ln1_g, ln1_b=ln1_b, w_ff1=w_ff1, w_ff3=w_ff3, w_ff2=w_ff2, ln2_g=ln2_g, ln2_b=ln2_b, loss_target=loss_target, m_c_ctx=m_c_ctx, m_ln0_g=m_ln0_g, m_ln0_b=m_ln0_b, m_w_ada=m_w_ada, m_b_ada=m_b_ada, m_w_in=m_w_in, m_conv_w=m_conv_w, m_conv_b=m_conv_b, m_dt_bias=m_dt_bias, m_a_log=m_a_log, m_d_skip=m_d_skip, m_ssd_norm_g=m_ssd_norm_g, m_gm_norm_g=m_gm_norm_g, m_gm_norm_b=m_gm_norm_b, m_w_spatial=m_w_spatial, m_b_spatial=m_b_spatial, m_b_gate=m_b_gate, m_w_ssd_proj=m_w_ssd_proj, m_w_gm_proj=m_w_gm_proj, m_w_out=m_w_out, m_ln1_g=m_ln1_g, m_ln1_b=m_ln1_b, m_w_ff1=m_w_ff1, m_w_ff3=m_w_ff3, m_w_ff2=m_w_ff2, m_ln2_g=m_ln2_g, m_ln2_b=m_ln2_b, v_c_ctx=v_c_ctx, v_ln0_g=v_ln0_g, v_ln0_b=v_ln0_b, v_w_ada=v_w_ada, v_b_ada=v_b_ada, v_w_in=v_w_in, v_conv_w=v_conv_w, v_conv_b=v_conv_b, v_dt_bias=v_dt_bias, v_a_log=v_a_log, v_d_skip=v_d_skip, v_ssd_norm_g=v_ssd_norm_g, v_gm_norm_g=v_gm_norm_g, v_gm_norm_b=v_gm_norm_b, v_w_spatial=v_w_spatial, v_b_spatial=v_b_spatial, v_b_gate=v_b_gate, v_w_ssd_proj=v_w_ssd_proj, v_w_gm_proj=v_w_gm_proj, v_w_out=v_w_out, v_ln1_g=v_ln1_g, v_ln1_b=v_ln1_b, v_w_ff1=v_w_ff1, v_w_ff3=v_w_ff3, v_w_ff2=v_w_ff2, v_ln2_g=v_ln2_g, v_ln2_b=v_ln2_b)
    weights = {n: given[n] for n in TWIN_WEIGHTS}
    shared = {n: given[n] for n in SHARED_INPUTS}
    per_example = {n: given[n] for n in ['x', 'c', 'ctx']}
    grad_fn = _jax.value_and_grad(_loss, argnums=(0, 1))

    def one_microbatch(ex, loss_target):
        ex = dict(ex)
        diff = ex.pop(TWIN_DIFF_INPUT)
        return grad_fn(weights, diff, {**shared, **ex}, loss_target)

    if N_MICROBATCH == 1:
        loss, (grad_w, grad_x) = one_microbatch(per_example, given["loss_target"])
    else:
        def body(carry, xs):
            loss_sum, grad_sum = carry
            l_k, (gw_k, gx_k) = one_microbatch(xs[0], xs[1])
            with _jax.named_scope("update"):
                return (loss_sum + l_k, _jax.tree.map(_jnp.add, grad_sum, gw_k)), gx_k

        init = (_jnp.zeros((), _jnp.float32), _jax.tree.map(_jnp.zeros_like, weights))
        (loss, grad_w), grad_x = _jax.lax.scan(body, init, (per_example, given["loss_target"]))
    with _jax.named_scope("update"):
        delta_w, new_m, new_v = {}, {}, {}
        for n in TWIN_WEIGHTS:
            delta_w[n], new_m[n], new_v[n] = _adamw(weights[n], grad_w[n], given["m_" + n], given["v_" + n])
    return (loss, grad_x, *[grad_w[n] for n in TWIN_WEIGHTS], *[delta_w[n] for n in TWIN_WEIGHTS],
            *[new_m[n] for n in TWIN_WEIGHTS], *[new_v[n] for n in TWIN_WEIGHTS])
```

```python
import functools

import jax
import jax.numpy as jnp
from jax import lax
from jax.experimental import pallas as pl
from jax.experimental.pallas import tpu as pltpu

_MXU = jnp.bfloat16
F32 = jnp.float32
D = 1024
TL = 256
Q = 128
NH, HP, NS, HPG = 16, 64, 128, 8
DFF = 2816
ALPHA = 2.0 ** 0.25
EPS = 1e-5
OZ, OU, OV, OXS, OG, OB, OC, ODT, NPJ = 0, 1024, 2048, 3072, 4096, 6144, 6400, 6656, 6912
NNAT = 6688
NDEV = 8
ADAM_LR, ADAM_B1, ADAM_B2, ADAM_EPS, ADAM_WD, ADAM_STEP = 1e-3, 0.9, 0.999, 1e-8, 0.01, 10
VMEM_LIMIT = 48 * 1024 * 1024

NN = ((1,), (0,))
NT = ((1,), (1,))
TN = ((0,), (0,))
MESH = pl.DeviceIdType.MESH


def _dot(a, b, dims):
    return lax.dot_general(a.astype(_MXU), b.astype(_MXU), (dims, ((), ())),
                           preferred_element_type=F32)


def _tile(n, cands):
    for c in cands:
        if n % c == 0:
            return c
    return n


def _params(sem):
    return pltpu.CompilerParams(dimension_semantics=sem, vmem_limit_bytes=VMEM_LIMIT)


def _cst(shape):
    nd = len(shape)
    return pl.BlockSpec(shape, lambda *_: (0,) * nd)


def _rt(w, cb=0, rows=TL):
    return pl.BlockSpec((rows, w), lambda i: (i, cb))


def _rtc(w, nt, cb=0):
    return pl.BlockSpec((TL, w), lambda i: (jnp.minimum(i, nt - 1), cb))


def _sig(x):
    return jax.nn.sigmoid(x)


def _softplus(x):
    return jnp.maximum(x, 0.0) + jnp.log1p(jnp.exp(-jnp.abs(x)))


_G0, _G1 = 0.7978845608028654, 0.044715


def _gelu(x):
    t = jnp.tanh(_G0 * (x + _G1 * x * x * x))
    return 0.5 * x * (1.0 + t), t


def _gelu_grad(x, t):
    return 0.5 * (1.0 + t) + 0.5 * x * (1.0 - t * t) * _G0 * (1.0 + 3.0 * _G1 * x * x)


def _ln(r):
    mu = jnp.mean(r, axis=-1, keepdims=True)
    xc = r - mu
    var = jnp.mean(xc * xc, axis=-1, keepdims=True)
    rstd = lax.rsqrt(var + EPS)
    return xc * rstd, rstd


def _ln_bwd(dyh, xhat, rstd):
    return rstd * (dyh - jnp.mean(dyh, axis=-1, keepdims=True)
                   - xhat * jnp.mean(dyh * xhat, axis=-1, keepdims=True))


def _colsum(v):
    return jnp.sum(v, axis=0, keepdims=True)


def _sum11(v):
    return jnp.sum(jnp.sum(v, axis=1, keepdims=True), axis=0, keepdims=True)


def _cumsum_rows(a, rev):
    n = a.shape[0]
    row = lax.broadcasted_iota(jnp.int32, a.shape, 0)
    s = 1
    while s < n:
        if rev:
            a = a + jnp.where(row < n - s, pltpu.roll(a, n - s, 0), 0.0)
        else:
            a = a + jnp.where(row >= s, pltpu.roll(a, s, 0), 0.0)
        s *= 2
    return a


def _mm(a, b, mode, out_dtype, name):
    if mode == "tn":
        K, M = a.shape
    else:
        M, K = a.shape
    N = b.shape[0] if mode == "nt" else b.shape[1]
    tm = _tile(M, (512, 256, 128))
    tn = _tile(N, (1024, 768, 512, 256, 128))
    tk = _tile(K, (1024, 512, 256, 128))
    nk = K // tk
    dims = {"nn": NN, "nt": NT, "tn": TN}[mode]

    def body(a_ref, b_ref, o_ref, acc_ref):
        k = pl.program_id(2)

        @pl.when(k == 0)
        def _():
            acc_ref[...] = jnp.zeros_like(acc_ref)

        acc_ref[...] += _dot(a_ref[...], b_ref[...], dims)

        @pl.when(k == nk - 1)
        def _():
            o_ref[...] = acc_ref[...].astype(o_ref.dtype)

    if mode == "tn":
        a_spec = pl.BlockSpec((tk, tm), lambda i, j, k: (k, i))
    else:
        a_spec = pl.BlockSpec((tm, tk), lambda i, j, k: (i, k))
    if mode == "nt":
        b_spec = pl.BlockSpec((tn, tk), lambda i, j, k: (j, k))
    else:
        b_spec = pl.BlockSpec((tk, tn), lambda i, j, k: (k, j))
    return pl.pallas_call(
        body, name=name, grid=(M // tm, N // tn, nk),
        in_specs=[a_spec, b_spec],
        out_specs=pl.BlockSpec((tm, tn), lambda i, j, k: (i, j)),
        out_shape=jax.ShapeDtypeStruct((M, N), out_dtype),
        scratch_shapes=[pltpu.VMEM((tm, tn), F32)],
        compiler_params=_params(("parallel", "parallel", "arbitrary")),
    )(a, b)


def _all_gather(x, name):
    def body(x_ref, out_ref, send_sems, recv_sems, local_sem):
        mx, my, mc = lax.axis_index("x"), lax.axis_index("y"), lax.axis_index("c")
        me, sibling = (mx, my, mc), (mx, my, 1 - mc)
        chips = [(1 - mx, my), (mx, 1 - my), (1 - mx, 1 - my)]

        def slot(px, py, pc):
            return out_ref.at[4 * px + 2 * py + pc]

        def copy(k, block, to, src=None):
            return pltpu.make_async_remote_copy(
                src_ref=slot(*block) if src is None else src, dst_ref=slot(*block),
                send_sem=send_sems.at[k], recv_sem=recv_sems.at[k],
                device_id=to, device_id_type=MESH)

        mine = pltpu.make_async_copy(x_ref, slot(*me), local_sem)
        mine.start()
        first = [copy(0, me, sibling, src=x_ref)]
        first += [copy(1 + j, me, (*chip, mc), src=x_ref) for j, chip in enumerate(chips)]
        for cp in first:
            cp.start()
        passed = [copy(4 + j, (*chip, mc), sibling) for j, chip in enumerate(chips)]
        for j, chip in enumerate(chips):
            copy(1 + j, (*chip, mc), me).wait_recv()
            passed[j].start()
        copy(0, sibling, me).wait_recv()
        for j, chip in enumerate(chips):
            copy(4 + j, (*chip, 1 - mc), me).wait_recv()
        for cp in first + passed:
            cp.wait_send()
        mine.wait()

    return pl.pallas_call(
        body, name=name,
        out_shape=jax.ShapeDtypeStruct((NDEV,) + x.shape, x.dtype),
        in_specs=[pl.BlockSpec(memory_space=pl.ANY)],
        out_specs=pl.BlockSpec(memory_space=pl.ANY),
        scratch_shapes=[pltpu.SemaphoreType.DMA((7,)), pltpu.SemaphoreType.DMA((7,)),
                        pltpu.SemaphoreType.DMA],
    )(x)


def _owner_exchange(g, name):
    def body(g_ref, out_ref, send_sems, recv_sems, local_sem):
        mx, my, mc = lax.axis_index("x"), lax.axis_index("y"), lax.axis_index("c")
        local = pltpu.make_async_copy(g_ref.at[4 * mx + 2 * my + mc], out_ref.at[0], local_sem)
        local.start()
        copies = []
        for f in range(1, NDEV):
            px = 1 - mx if (f >> 2) & 1 else mx
            py = 1 - my if (f >> 1) & 1 else my
            pc = 1 - mc if f & 1 else mc
            cp = pltpu.make_async_remote_copy(
                src_ref=g_ref.at[4 * px + 2 * py + pc], dst_ref=out_ref.at[f],
                send_sem=send_sems.at[f - 1], recv_sem=recv_sems.at[f - 1],
                device_id=(px, py, pc), device_id_type=MESH)
            cp.start()
            copies.append(cp)
        for cp in copies:
            cp.wait_recv()
        for cp in copies:
            cp.wait_send()
        local.wait()

    return pl.pallas_call(
        body, name=name,
        out_shape=jax.ShapeDtypeStruct(g.shape, g.dtype),
        in_specs=[pl.BlockSpec(memory_space=pl.ANY)],
        out_specs=pl.BlockSpec(memory_space=pl.ANY),
        scratch_shapes=[pltpu.SemaphoreType.DMA((7,)), pltpu.SemaphoreType.DMA((7,)),
                        pltpu.SemaphoreType.DMA],
    )(g)


def _sum8(r, name):
    _, R, C = r.shape
    tr = _tile(R, (256, 160, 128, 64, 32, 16, 8))

    def body(r_ref, o_ref):
        acc = r_ref[0].astype(F32)
        for k in range(1, NDEV):
            acc = acc + r_ref[k].astype(F32)
        o_ref[...] = acc

    return pl.pallas_call(
        body, name=name, grid=(R // tr,),
        in_specs=[pl.BlockSpec((NDEV, tr, C), lambda i: (0, i, 0))],
        out_specs=pl.BlockSpec((tr, C), lambda i: (i, 0)),
        out_shape=jax.ShapeDtypeStruct((R, C), F32),
        compiler_params=_params(("parallel",)),
    )(r)


def _adamw(g, w, m, v, name):
    R, C = g.shape
    tr = _tile(R, (256, 160, 128, 64, 32, 16, 8))
    bc1 = 1.0 - ADAM_B1 ** ADAM_STEP
    bc2 = 1.0 - ADAM_B2 ** ADAM_STEP

    def body(g_ref, w_ref, m_ref, v_ref, d_ref, mo_ref, vo_ref):
        gg = g_ref[...]
        mn = ADAM_B1 * m_ref[...] + (1.0 - ADAM_B1) * gg
        vn = ADAM_B2 * v_ref[...] + (1.0 - ADAM_B2) * (gg * gg)
        mh = mn / bc1
        vh = vn / bc2
        d_ref[...] = -ADAM_LR * (mh / (jnp.sqrt(vh) + ADAM_EPS) + ADAM_WD * w_ref[...])
        mo_ref[...] = mn
        vo_ref[...] = vn

    spec = pl.BlockSpec((tr, C), lambda i: (i, 0))
    sh = jax.ShapeDtypeStruct((R, C), F32)
    return pl.pallas_call(
        body, name=name, grid=(R // tr,), in_specs=[spec] * 4, out_specs=[spec] * 3,
        out_shape=[sh] * 3, compiler_params=_params(("parallel",)),
    )(g, w, m, v)


def _ada_fwd(c16, w_sh, b_sh):
    def body(c_ref, w_ref, b_ref, o_ref):
        c = c_ref[...]
        o_ref[...] = _dot(c * _sig(c), w_ref[...], NN) + b_ref[...]

    return pl.pallas_call(
        body, name="ada_fwd", out_shape=jax.ShapeDtypeStruct((16, w_sh.shape[1]), F32),
        compiler_params=pltpu.CompilerParams(vmem_limit_bytes=VMEM_LIMIT),
    )(c16, w_sh, b_sh)


def _ada_bwd(c16, g16, g16_sh, w_sh):
    ncol = w_sh.shape[1]

    def body(c_ref, g_ref, gs_ref, w_ref, dw_ref, db_ref, dc_ref):
        c = c_ref[...]
        s = _sig(c)
        gs = gs_ref[...]
        dw_ref[...] = _dot(c * s, gs, TN)
        db_ref[...] = jnp.broadcast_to(_colsum(g_ref[...]), db_ref.shape)
        odd = lax.broadcasted_iota(jnp.int32, gs.shape, 0) % 2 == 1
        gc = _colsum(jnp.where(odd, gs, 0.0))
        ds = _dot(jnp.broadcast_to(gc, (8, ncol)), w_ref[...], NT)
        c1 = c[1:2, :]
        s1 = s[1:2, :]
        dc_ref[...] = ds * (s1 * (1.0 + c1 * (1.0 - s1)))

    return pl.pallas_call(
        body, name="ada_bwd",
        out_shape=[jax.ShapeDtypeStruct(w_sh.shape, F32),
                   jax.ShapeDtypeStruct((8, g16.shape[1]), F32),
                   jax.ShapeDtypeStruct((8, D), F32)],
        compiler_params=pltpu.CompilerParams(vmem_limit_bytes=VMEM_LIMIT),
    )(c16, g16, g16_sh, w_sh)


def _ln0_fwd(x, ctx, g, b, modx, modc):
    L = x.shape[0]
    nt = L // TL

    def body(x_ref, c_ref, g_ref, b_ref, mx_ref, mc_ref, xn_ref, h_ref):
        isc = pl.program_id(0) == nt
        xin = jnp.where(isc, c_ref[...], x_ref[...])
        sh = jnp.where(isc, mc_ref[0:1, :], mx_ref[0:1, :])
        sc = jnp.where(isc, mc_ref[1:2, :], mx_ref[1:2, :])
        xhat, _ = _ln(xin)
        xn = xhat * g_ref[...] + b_ref[...]
        xn_ref[...] = xn
        h_ref[...] = (xn * (1.0 + sc) + sh).astype(h_ref.dtype)

    return pl.pallas_call(
        body, name="ln0_fwd", grid=(nt + 1,),
        in_specs=[_rtc(D, nt), _cst((TL, D)), _cst((1, D)), _cst((1, D)), _cst((8, D)), _cst((8, D))],
        out_specs=[_rt(D), _rt(D)],
        out_shape=[jax.ShapeDtypeStruct((L + TL, D), F32), jax.ShapeDtypeStruct((L + TL, D), _MXU)],
        compiler_params=_params(("parallel",)),
    )(x, ctx, g, b, modx, modc)


def _xbc_colblk(j):
    return jnp.where(j < 8, OXS // 128 + j, OB // 128 + j - 8)


def _conv_taps(p_ref, r0, first, last):
    main = p_ref[pl.ds(r0, TL), :]
    zero = jnp.zeros((8, main.shape[1]), F32)
    prev = zero if first else p_ref[pl.ds(r0 - 8, 8), :]
    nxt = zero if last else p_ref[pl.ds(r0 + TL, 8), :]
    ext = jnp.concatenate([prev, main, nxt], axis=0)
    n = TL + 16
    return [pltpu.roll(ext, (2 - k) % n, 0)[8:8 + TL] for k in range(5)]


def _seq_chunks(L):
    nt = L // TL
    return [(r * TL, r == 0, r == nt - 1) for r in range(nt)] + [(L, True, True)]


def _conv_fwd(p, conv_w8, conv_b):
    RT = p.shape[0]
    L = RT - TL
    chunks = _seq_chunks(L)

    def body(p_ref, w_ref, b_ref, o_ref):
        w = w_ref[...]
        bias = b_ref[...]
        for r0, first, last in chunks:
            taps = _conv_taps(p_ref, r0, first, last)
            pre = bias + sum(w[k:k + 1, :] * taps[k] for k in range(5))
            o_ref[pl.ds(r0, TL), :] = pre * _sig(pre)

    return pl.pallas_call(
        body, name="conv_fwd", grid=(12,),
        in_specs=[pl.BlockSpec((RT, 128), lambda j: (0, _xbc_colblk(j))),
                  pl.BlockSpec((8, 128), lambda j: (0, j)),
                  pl.BlockSpec((1, 128), lambda j: (0, j))],
        out_specs=pl.BlockSpec((RT, 128), lambda j: (0, j)),
        out_shape=jax.ShapeDtypeStruct((RT, 1536), F32),
        compiler_params=_params(("parallel",)),
    )(p, conv_w8, conv_b)


def _ssd_common(dtraw, dtb, a32, rev):
    dt = _softplus(dtraw + dtb)
    acum = _cumsum_rows(dt * a32, rev)
    ii = lax.broadcasted_iota(jnp.int32, (Q, Q), 0)
    jj = lax.broadcasted_iota(jnp.int32, (Q, Q), 1)
    mask = (ii <= jj) if rev else (ii >= jj)
    return dt, acum, acum.T, dt.T, mask


def _ssd_orders(ncl, ncc):
    nc = ncl + ncc

    def cf(s):
        return jnp.where(s < ncc, ncl + s, s - ncc)

    def cb(s):
        return nc - 1 - s

    return cf, cb


def _ssd_fwd(xbc, p, prm):
    RT = xbc.shape[0]
    nc = RT // Q
    ncc = TL // Q
    cf, cb = _ssd_orders(nc - ncc, ncc)

    def one_dir(x_ref, dt_ref, prm_ref, y_ref, hp_ref, H_ref, d):
        rev = d == 1
        a32 = -jnp.exp(prm_ref[1:2, :])
        dt, acum, acumT, dtT, mask = _ssd_common(dt_ref[...], prm_ref[0:1, :], a32, rev)
        end = 0 if rev else Q - 1
        for g in range(2):
            Bg = x_ref[:, D + g * NS:D + (g + 1) * NS]
            Cg = x_ref[:, D + 2 * NS + g * NS:D + 2 * NS + (g + 1) * NS]
            CB = _dot(Cg, Bg, NT)
            for hh in range(HPG):
                h = g * HPG + hh
                ln = 16 * d + h
                col = acum[:, ln:ln + 1]
                rowv = acumT[ln:ln + 1, :]
                a_end = rowv[:, end:end + 1]
                Lm = jnp.exp(jnp.where(mask, col - rowv, -1e30))
                W = CB * Lm * dtT[ln:ln + 1, :]
                Xh = x_ref[:, h * HP:(h + 1) * HP]
                Hp = H_ref[h * HP:(h + 1) * HP, :]
                y = _dot(W, Xh, NN) + jnp.exp(col) * _dot(Cg, Hp, NT)
                y_ref[:, h * HP:(h + 1) * HP] = y
                dcol = jnp.exp(a_end - col) * dt[:, ln:ln + 1]
                hp_ref[0, h * HP:(h + 1) * HP, :] = Hp
                H_ref[h * HP:(h + 1) * HP, :] = jnp.exp(a_end) * Hp + _dot(Xh * dcol, Bg, TN)

    def body(xf_ref, xb_ref, df_ref, db_ref, prm_ref, yf_ref, yb_ref, hf_ref, hb_ref, Hf, Hb):
        @pl.when(pl.program_id(0) == 0)
        def _():
            Hf[...] = jnp.zeros_like(Hf)
            Hb[...] = jnp.zeros_like(Hb)

        one_dir(xf_ref, df_ref, prm_ref, yf_ref, hf_ref, Hf, 0)
        one_dir(xb_ref, db_ref, prm_ref, yb_ref, hb_ref, Hb, 1)

    ysh = jax.ShapeDtypeStruct((RT, D), F32)
    hsh = jax.ShapeDtypeStruct((nc, NH * HP, NS), F32)
    hspec = pl.BlockSpec((1, NH * HP, NS), lambda s: (s, 0, 0))
    return pl.pallas_call(
        body, name="ssd_fwd", grid=(nc,),
        in_specs=[pl.BlockSpec((Q, 1536), lambda s: (cf(s), 0)),
                  pl.BlockSpec((Q, 1536), lambda s: (cb(s), 0)),
                  pl.BlockSpec((Q, 128), lambda s: (cf(s), ODT // 128)),
                  pl.BlockSpec((Q, 128), lambda s: (cb(s), ODT // 128)),
                  _cst((8, 128))],
        out_specs=[pl.BlockSpec((Q, D), lambda s: (cf(s), 0)),
                   pl.BlockSpec((Q, D), lambda s: (cb(s), 0)), hspec, hspec],
        out_shape=[ysh, ysh, hsh, hsh],
        scratch_shapes=[pltpu.VMEM((NH * HP, NS), F32), pltpu.VMEM((NH * HP, NS), F32)],
        compiler_params=_params(("arbitrary",)),
    )(xbc, xbc, p, p, prm)


def _ssd_bwd(xbc, p, prm, dsk, dyd, hpf, hpb):
    RT = xbc.shape[0]
    nc = RT // Q
    ncc = TL // Q
    ncl = nc - ncc
    cf, cb = _ssd_orders(ncl, ncc)

    def rs(t):
        return nc - 1 - t

    def one_dir(x_ref, dt_ref, prm_ref, dsk_ref, dy_ref, is_ctx, hp_ref, dH_ref,
                dx_ref, ddt_ref, st_ref, d):
        rev = d == 1
        a32 = -jnp.exp(prm_ref[1:2, :])
        dtraw = dt_ref[...]
        dtb = prm_ref[0:1, :]
        dt, acum, acumT, dtT, mask = _ssd_common(dtraw, dtb, a32, rev)
        end = 0 if rev else Q - 1
        lane = lax.broadcasted_iota(jnp.int32, (Q, 128), 1)
        srow = lax.broadcasted_iota(jnp.int32, (Q, 128), 0)
        dyscale = jnp.where(is_ctx, 0.0, 1.0)
        c_dacum = jnp.zeros((Q, 128), F32)
        r_dacum = jnp.zeros((Q, 128), F32)
        c_ddt = jnp.zeros((Q, 128), F32)
        r_ddt = jnp.zeros((Q, 128), F32)
        dskacc = jnp.zeros((1, 128), F32)
        for g in range(2):
            Bg = x_ref[:, D + g * NS:D + (g + 1) * NS]
            Cg = x_ref[:, D + 2 * NS + g * NS:D + 2 * NS + (g + 1) * NS]
            CB = _dot(Cg, Bg, NT)
            dCB = jnp.zeros((Q, Q), F32)
            dBg = jnp.zeros((Q, NS), F32)
            dCg = jnp.zeros((Q, NS), F32)
            for hh in range(HPG):
                h = g * HPG + hh
                ln = 16 * d + h
                hs = slice(h * HP, (h + 1) * HP)
                col = acum[:, ln:ln + 1]
                rowv = acumT[ln:ln + 1, :]
                dtr = dtT[ln:ln + 1, :]
                dtc = dt[:, ln:ln + 1]
                a_end = rowv[:, end:end + 1]
                Lm = jnp.exp(jnp.where(mask, col - rowv, -1e30))
                E = jnp.exp(col)
                ecol = jnp.exp(a_end - col)
                dcol = ecol * dtc
                Xh = x_ref[:, hs]
                dY = dy_ref[:, hs] * dyscale
                Hp = hp_ref[0, hs, :]
                dHn = dH_ref[hs, :]
                W = CB * Lm * dtr
                dW = _dot(dY, Xh, NT)
                Mm = dW * CB * Lm
                T = Mm * dtr
                dCB = dCB + dW * Lm * dtr
                BdH = _dot(Bg, dHn, NT)
                dX = _dot(W, dY, TN) + dcol * BdH
                if d == 0:
                    dX = dX + dY * dsk_ref[:, hs]
                    dskacc = dskacc + jnp.where(lane[0:1, :] == h, _sum11(dY * Xh), 0.0)
                dx_ref[:, hs] = dX
                xb = jnp.sum(Xh * BdH, axis=1, keepdims=True)
                scol = dcol * xb
                G = _dot(dY, Hp, NN)
                dCg = dCg + E * G
                qcol = E * jnp.sum(G * Cg, axis=1, keepdims=True)
                dBg = dBg + _dot(Xh * dcol, dHn, NN)
                dH_ref[hs, :] = jnp.exp(a_end) * dHn + _dot(dY * E, Cg, TN)
                eterm = jnp.exp(a_end) * _sum11(dHn * Hp) + _sum11(scol)
                cvec = jnp.sum(T, axis=1, keepdims=True) + qcol - scol
                cvec = cvec + jnp.where(srow[:, 0:1] == end, eterm, 0.0)
                c_dacum = c_dacum + jnp.where(lane == ln, cvec, 0.0)
                r_dacum = r_dacum - jnp.where(srow == ln, _colsum(T), 0.0)
                c_ddt = c_ddt + jnp.where(lane == ln, ecol * xb, 0.0)
                r_ddt = r_ddt + jnp.where(srow == ln, _colsum(Mm), 0.0)
            dBg = dBg + _dot(dCB, Cg, TN)
            dCg = dCg + _dot(dCB, Bg, NN)
            dx_ref[:, D + g * NS:D + (g + 1) * NS] = dBg
            dx_ref[:, D + 2 * NS + g * NS:D + 2 * NS + (g + 1) * NS] = dCg
        dacum = c_dacum + r_dacum.T
        da = _cumsum_rows(dacum, not rev)
        mine = (lane >= 16 * d) & (lane < 16 * d + 16)
        ddt = jnp.where(mine, c_ddt + r_ddt.T + da * a32, 0.0)
        ddt_ref[...] = ddt * _sig(dtraw + dtb)
        st_ref[0:1, :] += _colsum(jnp.where(mine, da * dt, 0.0))
        if d == 0:
            st_ref[1:2, :] += dskacc

    def body(xf_ref, xb_ref, df_ref, db_ref, prm_ref, dsk_ref, dyf_ref, dyb_ref, hf_ref, hb_ref,
             dxf_ref, dxb_ref, ddf_ref, ddb_ref, st_ref, dHf, dHb):
        t = pl.program_id(0)

        @pl.when(t == 0)
        def _():
            dHf[...] = jnp.zeros_like(dHf)
            dHb[...] = jnp.zeros_like(dHb)
            st_ref[...] = jnp.zeros_like(st_ref)

        s = rs(t)
        one_dir(xf_ref, df_ref, prm_ref, dsk_ref, dyf_ref, cf(s) >= ncl, hf_ref, dHf,
                dxf_ref, ddf_ref, st_ref, 0)
        one_dir(xb_ref, db_ref, prm_ref, dsk_ref, dyb_ref, cb(s) >= ncl, hb_ref, dHb,
                dxb_ref, ddb_ref, st_ref, 1)

        @pl.when(t == nc - 1)
        def _():
            st_ref[0:1, :] = -jnp.exp(prm_ref[1:2, :]) * st_ref[0:1, :]

    def lat(c):
        return jnp.minimum(c, ncl - 1)

    xsh = jax.ShapeDtypeStruct((RT, 1536), F32)
    dsh = jax.ShapeDtypeStruct((RT, 128), F32)
    hspec = pl.BlockSpec((1, NH * HP, NS), lambda t: (rs(t), 0, 0))
    return pl.pallas_call(
        body, name="ssd_bwd", grid=(nc,),
        in_specs=[pl.BlockSpec((Q, 1536), lambda t: (cf(rs(t)), 0)),
                  pl.BlockSpec((Q, 1536), lambda t: (cb(rs(t)), 0)),
                  pl.BlockSpec((Q, 128), lambda t: (cf(rs(t)), ODT // 128)),
                  pl.BlockSpec((Q, 128), lambda t: (cb(rs(t)), ODT // 128)),
                  _cst((8, 128)), _cst((1, D)),
                  pl.BlockSpec((Q, D), lambda t: (lat(cf(rs(t))), 0)),
                  pl.BlockSpec((Q, D), lambda t: (lat(cb(rs(t))), 0)),
                  hspec, hspec],
        out_specs=[pl.BlockSpec((Q, 1536), lambda t: (cf(rs(t)), 0)),
                   pl.BlockSpec((Q, 1536), lambda t: (cb(rs(t)), 0)),
                   pl.BlockSpec((Q, 128), lambda t: (cf(rs(t)), 0)),
                   pl.BlockSpec((Q, 128), lambda t: (cb(rs(t)), 0)),
                   _cst((8, 128))],
        out_shape=[xsh, xsh, dsh, dsh, jax.ShapeDtypeStruct((8, 128), F32)],
        scratch_shapes=[pltpu.VMEM((NH * HP, NS), F32), pltpu.VMEM((NH * HP, NS), F32)],
        compiler_params=_params(("arbitrary",)),
    )(xbc, xbc, p, p, prm, dsk, dyd, dyd, hpf, hpb)


def _mix_fwd_vals(yf, yb, z, xs, u, v, dsk, sg, gg, gb):
    y = yf + yb + xs * dsk
    sz = _sig(z)
    hh = y * z * sz
    r = lax.rsqrt(jnp.mean(hh * hh, axis=-1, keepdims=True) + EPS)
    nh = hh * r
    ug, tu = _gelu(u)
    vg, tv = _gelu(v)
    vhat, vrstd = _ln(vg)
    vn = vhat * gg + gb
    return y, sz, r, nh, ug, tu, vg, tv, vhat, vrstd, vn


def _mix_fwd(yf, yb, p, xbc, dsk, sg, gg, gb, ws, bsT):
    L = yf.shape[0] - TL
    nt = L // TL

    def body(yf_ref, yb_ref, z_ref, xs_ref, u_ref, v_ref, dsk_ref, sg_ref, gg_ref, gb_ref,
             ws_ref, bs_ref, ys_ref, ym_ref):
        _, _, _, nh, ug, _, _, _, _, _, vn = _mix_fwd_vals(
            yf_ref[...], yb_ref[...], z_ref[...], xs_ref[...], u_ref[...], v_ref[...],
            dsk_ref[...], sg_ref[...], gg_ref[...], gb_ref[...])
        ys_ref[...] = (nh * sg_ref[...]).astype(ys_ref.dtype)
        for n in range(TL // Q):
            rs_ = slice(n * Q, (n + 1) * Q)
            for g in range(8):
                cs = slice(g * 128, (g + 1) * 128)
                mixed = _dot(ws_ref[g], vn[rs_, cs], NN) + bs_ref[:, g:g + 1]
                ym_ref[rs_, cs] = (ug[rs_, cs] * mixed).astype(ym_ref.dtype)

    return pl.pallas_call(
        body, name="mix_fwd", grid=(nt,),
        in_specs=[_rt(D), _rt(D), _rt(D, OZ // D), _rt(D, 0), _rt(D, OU // D), _rt(D, OV // D),
                  _cst((1, D)), _cst((1, D)), _cst((1, D)), _cst((1, D)),
                  _cst((8, 128, 128)), _cst((128, 128))],
        out_specs=[_rt(D), _rt(D)],
        out_shape=[jax.ShapeDtypeStruct((L, D), _MXU), jax.ShapeDtypeStruct((L, D), _MXU)],
        compiler_params=_params(("parallel",)),
    )(yf, yb, p, xbc, p, p, dsk, sg, gg, gb, ws, bsT)


def _mix_bwd(dys, dym, yf, yb, p, xbc, dp, dsk, sg, gg, gb, ws, bsT):
    L = dys.shape[0]
    nt = L // TL

    def body(dys_ref, dym_ref, yf_ref, yb_ref, z_ref, xs_ref, u_ref, v_ref, dsk_ref, sg_ref,
             gg_ref, gb_ref, ws_ref, bs_ref, dp_any, dzuv_ref, dy_ref, st_ref,
             dws_ref, dbs_ref, dvn_s):
        del dp_any
        dz_ref = dzuv_ref.at[:, OZ:OZ + D]
        du_ref = dzuv_ref.at[:, OU:OU + D]
        dv_ref = dzuv_ref.at[:, OV:OV + D]

        @pl.when(pl.program_id(0) == 0)
        def _():
            st_ref[...] = jnp.zeros_like(st_ref)
            dws_ref[...] = jnp.zeros_like(dws_ref)
            dbs_ref[...] = jnp.zeros_like(dbs_ref)

        z = z_ref[...]
        u = u_ref[...]
        v = v_ref[...]
        y, sz, r, nh, ug, tu, vg, tv, vhat, vrstd, vn = _mix_fwd_vals(
            yf_ref[...], yb_ref[...], z, xs_ref[...], u, v,
            dsk_ref[...], sg_ref[...], gg_ref[...], gb_ref[...])
        dys = dys_ref[...]
        st_ref[0:1, :] += _colsum(dys * nh)
        dn = dys * sg_ref[...]
        dhh = r * (dn - nh * jnp.mean(dn * nh, axis=-1, keepdims=True))
        dy_ref[...] = dhh * z * sz
        dz_ref[...] = (dhh * y * (sz * (1.0 + z * (1.0 - sz)))).astype(dz_ref.dtype)
        dym = dym_ref[...]
        lane = lax.broadcasted_iota(jnp.int32, (Q, 128), 1)
        dbs = jnp.zeros((Q, 128), F32)
        gu = _gelu_grad(u, tu)
        for n in range(TL // Q):
            rs_ = slice(n * Q, (n + 1) * Q)
            for g in range(8):
                cs = slice(g * 128, (g + 1) * 128)
                vb = vn[rs_, cs]
                mixed = _dot(ws_ref[g], vb, NN) + bs_ref[:, g:g + 1]
                dyb = dym[rs_, cs]
                dmx = dyb * ug[rs_, cs]
                du_ref[rs_, cs] = (dyb * mixed * gu[rs_, cs]).astype(du_ref.dtype)
                dvn_s[rs_, cs] = _dot(ws_ref[g], dmx, TN)
                dws_ref[g] += _dot(dmx, vb, NT)
                dbs = dbs + jnp.where(lane == g, jnp.sum(dmx, axis=1, keepdims=True), 0.0)
        dbs_ref[...] += dbs
        dvn = dvn_s[...]
        st_ref[1:2, :] += _colsum(dvn * vhat)
        st_ref[2:3, :] += _colsum(dvn)
        dvg = _ln_bwd(dvn * gg_ref[...], vhat, vrstd)
        dv_ref[...] = (dvg * _gelu_grad(v, tv)).astype(dv_ref.dtype)

    outs = pl.pallas_call(
        body, name="mix_bwd", grid=(nt,),
        in_specs=[_rt(D), _rt(D), _rt(D), _rt(D), _rt(D, OZ // D), _rt(D, 0), _rt(D, OU // D),
                  _rt(D, OV // D), _cst((1, D)), _cst((1, D)), _cst((1, D)), _cst((1, D)),
                  _cst((8, 128, 128)), _cst((128, 128)), pl.BlockSpec(memory_space=pl.ANY)],
        out_specs=[_rt(3 * D, 0), _rt(D), _cst((8, D)),
                   _cst((8, 128, 128)), _cst((128, 128))],
        out_shape=[jax.ShapeDtypeStruct(dp.shape, dp.dtype),
                   jax.ShapeDtypeStruct((L, D), F32), jax.ShapeDtypeStruct((8, D), F32),
                   jax.ShapeDtypeStruct((8, 128, 128), F32), jax.ShapeDtypeStruct((128, 128), F32)],
        scratch_shapes=[pltpu.VMEM((TL, D), F32)],
        input_output_aliases={14: 0},
        compiler_params=_params(("arbitrary",)),
    )(dys, dym, yf, yb, p, xbc, p, p, dsk, sg, gg, gb, ws, bsT, dp)
    return outs


def _gate_fwd(a1, a2, p, bg):
    L = a1.shape[0]

    def body(a1_ref, a2_ref, g_ref, bg_ref, m_ref):
        gt = _sig(g_ref[...] + bg_ref[...])
        m_ref[...] = (gt[:, :D] * a1_ref[...] + gt[:, D:] * a2_ref[...]).astype(m_ref.dtype)

    return pl.pallas_call(
        body, name="gate_fwd", grid=(L // TL,),
        in_specs=[_rt(D), _rt(D), _rt(2 * D, OG // (2 * D)), _cst((1, 2 * D))],
        out_specs=_rt(D), out_shape=jax.ShapeDtypeStruct((L, D), _MXU),
        compiler_params=_params(("parallel",)),
    )(a1, a2, p, bg)


def _gate_bwd(dmg, a1, a2, p, bg, dp):
    L = a1.shape[0]

    def body(dm_ref, a1_ref, a2_ref, g_ref, bg_ref, dp_any, dg_ref, da1_ref, da2_ref, st_ref):
        del dp_any

        @pl.when(pl.program_id(0) == 0)
        def _():
            st_ref[...] = jnp.zeros_like(st_ref)

        gt = _sig(g_ref[...] + bg_ref[...])
        g1 = gt[:, :D]
        g2 = gt[:, D:]
        dm = dm_ref[...]
        da1_ref[...] = (dm * g1).astype(da1_ref.dtype)
        da2_ref[...] = (dm * g2).astype(da2_ref.dtype)
        dg1 = dm * a1_ref[...] * g1 * (1.0 - g1)
        dg2 = dm * a2_ref[...] * g2 * (1.0 - g2)
        st_ref[0:1, 0:D] += _colsum(dg1)
        st_ref[0:1, D:2 * D] += _colsum(dg2)
        dg_ref[:, 0:D] = dg1.astype(dg_ref.dtype)
        dg_ref[:, D:2 * D] = dg2.astype(dg_ref.dtype)

    return pl.pallas_call(
        body, name="gate_bwd", grid=(L // TL,),
        in_specs=[_rt(D), _rt(D), _rt(D), _rt(2 * D, OG // (2 * D)), _cst((1, 2 * D)),
                  pl.BlockSpec(memory_space=pl.ANY)],
        out_specs=[_rt(2 * D, OG // (2 * D)), _rt(D), _rt(D), _cst((8, 2 * D))],
        out_shape=[jax.ShapeDtypeStruct(dp.shape, dp.dtype), jax.ShapeDtypeStruct((L, D), _MXU),
                   jax.ShapeDtypeStruct((L, D), _MXU), jax.ShapeDtypeStruct((8, 2 * D), F32)],
        input_output_aliases={5: 0},
        compiler_params=_params(("arbitrary",)),
    )(dmg, a1, a2, p, bg, dp)


def _res1_fwd(xn, out, modx, g, b):
    L = out.shape[0]

    def body(xn_ref, o_ref, mx_ref, g_ref, b_ref, r1_ref, h2_ref):
        r1 = ALPHA * xn_ref[...] + mx_ref[2:3, :] * o_ref[...]
        xhat, _ = _ln(r1)
        x1 = xhat * g_ref[...] + b_ref[...]
        r1_ref[...] = r1
        h2_ref[...] = (x1 * (1.0 + mx_ref[4:5, :]) + mx_ref[3:4, :]).astype(h2_ref.dtype)

    return pl.pallas_call(
        body, name="res1_fwd", grid=(L // TL,),
        in_specs=[_rt(D), _rt(D), _cst((8, D)), _cst((1, D)), _cst((1, D))],
        out_specs=[_rt(D), _rt(D)],
        out_shape=[jax.ShapeDtypeStruct((L, D), F32), jax.ShapeDtypeStruct((L, D), _MXU)],
        compiler_params=_params(("parallel",)),
    )(xn, out, modx, g, b)


def _glu_fwd(f13):
    L = f13.shape[0]

    def body(f1_ref, f3_ref, o_ref):
        f1 = f1_ref[...]
        o_ref[...] = (f1 * _sig(f1) * f3_ref[...]).astype(o_ref.dtype)

    return pl.pallas_call(
        body, name="glu_fwd", grid=(L // TL,),
        in_specs=[_rt(DFF, 0), _rt(DFF, 1)], out_specs=_rt(DFF),
        out_shape=jax.ShapeDtypeStruct((L, DFF), _MXU),
        compiler_params=_params(("parallel",)),
    )(f13, f13)


def _glu_bwd(dff, f13):
    L = f13.shape[0]

    def body(d_ref, f1_ref, f3_ref, o_ref):
        f1 = f1_ref[...]
        s = _sig(f1)
        d = d_ref[...]
        o_ref[:, 0:DFF] = (d * f3_ref[...] * (s * (1.0 + f1 * (1.0 - s)))).astype(o_ref.dtype)
        o_ref[:, DFF:2 * DFF] = (d * f1 * s).astype(o_ref.dtype)

    return pl.pallas_call(
        body, name="glu_bwd", grid=(L // TL,),
        in_specs=[_rt(DFF), _rt(DFF, 0), _rt(DFF, 1)], out_specs=_rt(2 * DFF),
        out_shape=jax.ShapeDtypeStruct((L, 2 * DFF), _MXU),
        compiler_params=_params(("parallel",)),
    )(dff, f13, f13)


def _res2(r1, o2, tgt, modx, g1, b1, g2, b2):
    L = r1.shape[0]

    def body(r1_ref, o2_ref, t_ref, mx_ref, g1_ref, b1_ref, g2_ref, b2_ref,
             dr2_ref, do2_ref, st_ref, loss_ref):
        @pl.when(pl.program_id(0) == 0)
        def _():
            st_ref[...] = jnp.zeros_like(st_ref)
            loss_ref[...] = jnp.zeros_like(loss_ref)

        xh1, _ = _ln(r1_ref[...])
        x1 = xh1 * g1_ref[...] + b1_ref[...]
        o2 = o2_ref[...]
        g2x = mx_ref[5:6, :]
        xh2, rstd2 = _ln(ALPHA * x1 + g2x * o2)
        err = xh2 * g2_ref[...] + b2_ref[...] - t_ref[...]
        per_tok = jnp.mean(err * err, axis=-1, keepdims=True)
        loss_ref[...] += 0.5 * jnp.sum(per_tok, axis=0, keepdims=True)
        dy = err * (1.0 / D)
        st_ref[0:1, :] += _colsum(dy * xh2)
        st_ref[1:2, :] += _colsum(dy)
        dr2 = _ln_bwd(dy * g2_ref[...], xh2, rstd2)
        st_ref[2:3, :] += _colsum(dr2 * o2)
        dr2_ref[...] = dr2
        do2_ref[...] = (g2x * dr2).astype(do2_ref.dtype)

    return pl.pallas_call(
        body, name="res2", grid=(L // TL,),
        in_specs=[_rt(D), _rt(D), _rt(D), _cst((8, D))] + [_cst((1, D))] * 4,
        out_specs=[_rt(D), _rt(D), _cst((8, D)), _cst((8, 128))],
        out_shape=[jax.ShapeDtypeStruct((L, D), F32), jax.ShapeDtypeStruct((L, D), _MXU),
                   jax.ShapeDtypeStruct((8, D), F32), jax.ShapeDtypeStruct((8, 128), F32)],
        compiler_params=_params(("arbitrary",)),
    )(r1, o2, tgt, modx, g1, b1, g2, b2)


def _res1_bwd(dr2, dh2, r1, out, modx, g1, b1):
    L = r1.shape[0]

    def body(dr2_ref, dh2_ref, r1_ref, o_ref, mx_ref, g_ref, b_ref, dr1_ref, do_ref, st_ref):
        @pl.when(pl.program_id(0) == 0)
        def _():
            st_ref[...] = jnp.zeros_like(st_ref)

        xh1, rstd1 = _ln(r1_ref[...])
        x1 = xh1 * g_ref[...] + b_ref[...]
        dh2 = dh2_ref[...]
        dx1 = ALPHA * dr2_ref[...] + dh2 * (1.0 + mx_ref[4:5, :])
        st_ref[0:1, :] += _colsum(dh2 * x1)
        st_ref[1:2, :] += _colsum(dh2)
        st_ref[2:3, :] += _colsum(dx1 * xh1)
        st_ref[3:4, :] += _colsum(dx1)
        dr1 = _ln_bwd(dx1 * g_ref[...], xh1, rstd1)
        st_ref[4:5, :] += _colsum(dr1 * o_ref[...])
        dr1_ref[...] = dr1
        do_ref[...] = (mx_ref[2:3, :] * dr1).astype(do_ref.dtype)

    return pl.pallas_call(
        body, name="res1_bwd", grid=(L // TL,),
        in_specs=[_rt(D), _rt(D), _rt(D), _rt(D), _cst((8, D)), _cst((1, D)), _cst((1, D))],
        out_specs=[_rt(D), _rt(D), _cst((8, D))],
        out_shape=[jax.ShapeDtypeStruct((L, D), F32), jax.ShapeDtypeStruct((L, D), _MXU),
                   jax.ShapeDtypeStruct((8, D), F32)],
        compiler_params=_params(("arbitrary",)),
    )(dr2, dh2, r1, out, modx, g1, b1)


def _conv_bwd(dxf, dxb, p, conv_w8, conv_b, dp):
    RT = p.shape[0]
    chunks = _seq_chunks(RT - TL)

    def body(df_ref, db_ref, p_ref, w_ref, b_ref, dp_any, o_ref, dw_ref, dbias_ref, dpre_s):
        del dp_any
        w = w_ref[...]
        bias = b_ref[...]
        srow = lax.broadcasted_iota(jnp.int32, (8, 128), 0)
        dwacc = jnp.zeros((8, 128), F32)
        dbacc = jnp.zeros((1, 128), F32)
        for r0, first, last in chunks:
            taps = _conv_taps(p_ref, r0, first, last)
            pre = bias + sum(w[k:k + 1, :] * taps[k] for k in range(5))
            s = _sig(pre)
            dpre = (df_ref[pl.ds(r0, TL), :] + db_ref[pl.ds(r0, TL), :]) * (s * (1.0 + pre * (1.0 - s)))
            dpre_s[pl.ds(r0, TL), :] = dpre
            dbacc = dbacc + _colsum(dpre)
            for k in range(5):
                dwacc = dwacc + jnp.where(srow == k, _colsum(dpre * taps[k]), 0.0)
        for r0, first, last in chunks:
            taps = _conv_taps(dpre_s, r0, first, last)
            dx = sum(w[k:k + 1, :] * taps[4 - k] for k in range(5))
            o_ref[pl.ds(r0, TL), :] = dx.astype(o_ref.dtype)
        dw_ref[...] = dwacc
        dbias_ref[...] = jnp.broadcast_to(dbacc, (8, 128))

    cspec = pl.BlockSpec((RT, 128), lambda j: (0, j))
    wspec = pl.BlockSpec((8, 128), lambda j: (0, j))
    return pl.pallas_call(
        body, name="conv_bwd", grid=(12,),
        in_specs=[cspec, cspec, pl.BlockSpec((RT, 128), lambda j: (0, _xbc_colblk(j))),
                  wspec, pl.BlockSpec((1, 128), lambda j: (0, j)), pl.BlockSpec(memory_space=pl.ANY)],
        out_specs=[pl.BlockSpec((RT, 128), lambda j: (0, _xbc_colblk(j))), wspec, wspec],
        out_shape=[jax.ShapeDtypeStruct(dp.shape, dp.dtype), jax.ShapeDtypeStruct((8, 1536), F32),
                   jax.ShapeDtypeStruct((8, 1536), F32)],
        scratch_shapes=[pltpu.VMEM((RT, 128), F32)],
        input_output_aliases={5: 0},
        compiler_params=_params(("parallel",)),
    )(dxf, dxb, p, conv_w8, conv_b, dp)


def _dt_bwd(ddf, ddb, dp):
    RT = ddf.shape[0]

    def body(f_ref, b_ref, dp_any, o_ref, st_ref):
        del dp_any

        @pl.when(pl.program_id(0) == 0)
        def _():
            st_ref[...] = jnp.zeros_like(st_ref)

        s = f_ref[...] + b_ref[...]
        o_ref[...] = s.astype(o_ref.dtype)
        st_ref[0:1, :] += _colsum(s)

    return pl.pallas_call(
        body, name="dt_bwd", grid=(RT // TL,),
        in_specs=[_rt(128), _rt(128), pl.BlockSpec(memory_space=pl.ANY)],
        out_specs=[_rt(128, ODT // 128), _cst((8, 128))],
        out_shape=[jax.ShapeDtypeStruct(dp.shape, dp.dtype), jax.ShapeDtypeStruct((8, 128), F32)],
        input_output_aliases={2: 0},
        compiler_params=_params(("arbitrary",)),
    )(ddf, ddb, dp)


def _ln0_bwd(dh1, dr1, x, ctx, g, b, modx, modc):
    L = x.shape[0]
    nt = L // TL

    def body(dh_ref, dr1_ref, x_ref, c_ref, g_ref, b_ref, mx_ref, mc_ref, gx_ref, st_ref):
        i = pl.program_id(0)
        isc = i == nt

        @pl.when(i == 0)
        def _():
            st_ref[...] = jnp.zeros_like(st_ref)

        xin = jnp.where(isc, c_ref[...], x_ref[...])
        xhat, rstd = _ln(xin)
        xn = xhat * g_ref[...] + b_ref[...]
        sc = jnp.where(isc, mc_ref[1:2, :], mx_ref[1:2, :])
        dh = dh_ref[...]
        lat = jnp.where(isc, 0.0, 1.0)
        dxn = dh * (1.0 + sc) + (lat * ALPHA) * dr1_ref[...]
        tsh = _colsum(dh)
        tsc = _colsum(dh * xn)
        st_ref[0:1, :] += lat * tsh
        st_ref[1:2, :] += lat * tsc
        st_ref[2:3, :] += (1.0 - lat) * tsh
        st_ref[3:4, :] += (1.0 - lat) * tsc
        st_ref[4:5, :] += _colsum(dxn * xhat)
        st_ref[5:6, :] += _colsum(dxn)

        @pl.when(i < nt)
        def _():
            gx_ref[...] = _ln_bwd(dxn * g_ref[...], xhat, rstd)

    return pl.pallas_call(
        body, name="ln0_bwd", grid=(nt + 1,),
        in_specs=[_rt(D), _rtc(D, nt), _rtc(D, nt), _cst((TL, D)), _cst((1, D)), _cst((1, D)),
                  _cst((8, D)), _cst((8, D))],
        out_specs=[_rtc(D, nt), _cst((8, D))],
        out_shape=[jax.ShapeDtypeStruct((L, D), F32), jax.ShapeDtypeStruct((8, D), F32)],
        compiler_params=_params(("arbitrary",)),
    )(dh1, dr1, x, ctx, g, b, modx, modc)


def _perm_cols(w):
    pad = jnp.zeros((w.shape[0], NPJ - NNAT), w.dtype)
    return jnp.concatenate([w[:, 0:1024], w[:, 2592:3616], w[:, 3616:4640], w[:, 1024:2048],
                            w[:, 4640:6688], w[:, 2048:2304], w[:, 2304:2560], w[:, 2560:2592], pad],
                           axis=1)


def _unperm_cols(g):
    return jnp.concatenate([g[:, OZ:OZ + D], g[:, OXS:OXS + D], g[:, OB:OB + 256], g[:, OC:OC + 256],
                            g[:, ODT:ODT + 32], g[:, OU:OU + D], g[:, OV:OV + D], g[:, OG:OG + 2 * D]],
                           axis=1)


def _padded(n, row_align):
    unit = row_align * D
    return -(-n // unit) * unit if row_align else n


def _slab(arrs, rows, row_align=0):
    parts = []
    for a in arrs:
        f = a.reshape(-1)
        parts.append(jnp.pad(f, (0, _padded(f.shape[0], row_align) - f.shape[0])))
    flat = jnp.concatenate(parts)
    flat = jnp.pad(flat, (0, rows * D - flat.shape[0]))
    return flat.reshape(rows, D)


def _unslab(slab, shapes, row_align=0):
    flat = slab.reshape(-1)
    out, off = [], 0
    for shp in shapes:
        n = 1
        for s in shp:
            n *= s
        out.append(flat[off:off + n].reshape(shp))
        off += _padded(n, row_align)
    return out


def _row(v):
    return v.reshape(1, -1)


def _pad_rows(a, rows):
    return jnp.pad(a, ((0, rows - a.shape[0]), (0, 0)))


BIG = ["w_in", "w_ssd_proj", "w_gm_proj", "w_out", "w_ff1", "w_ff3", "w_ff2"]
BIG_ROWS = 2304
BIG_ALIGN = 16
REPL = ["c_ctx", "ln0_g", "ln0_b", "b_ada", "conv_b", "dt_bias", "a_log", "d_skip", "ssd_norm_g",
        "gm_norm_g", "gm_norm_b", "w_spatial", "b_spatial", "b_gate", "ln1_g", "ln1_b", "ln2_g", "ln2_b"]
SMALL_ROWS = 160
WEIGHTS = ["c_ctx", "ln0_g", "ln0_b", "w_ada", "b_ada", "w_in", "conv_w", "conv_b", "dt_bias", "a_log",
           "d_skip", "ssd_norm_g", "gm_norm_g", "gm_norm_b", "w_spatial", "b_spatial", "b_gate",
           "w_ssd_proj", "w_gm_proj", "w_out", "ln1_g", "ln1_b", "w_ff1", "w_ff3", "w_ff2", "ln2_g", "ln2_b"]


def kernel(x, c, ctx, c_ctx, ln0_g, ln0_b, w_ada, b_ada, w_in, conv_w, conv_b, dt_bias, a_log, d_skip, ssd_norm_g, gm_norm_g, gm_norm_b, w_spatial, b_spatial, b_gate, w_ssd_proj, w_gm_proj, w_out, ln1_g, ln1_b, w_ff1, w_ff3, w_ff2, ln2_g, ln2_b, loss_target, m_c_ctx, m_ln0_g, m_ln0_b, m_w_ada, m_b_ada, m_w_in, m_conv_w, m_conv_b, m_dt_bias, m_a_log, m_d_skip, m_ssd_norm_g, m_gm_norm_g, m_gm_norm_b, m_w_spatial, m_b_spatial, m_b_gate, m_w_ssd_proj, m_w_gm_proj, m_w_out, m_ln1_g, m_ln1_b, m_w_ff1, m_w_ff3, m_w_ff2, m_ln2_g, m_ln2_b, v_c_ctx, v_ln0_g, v_ln0_b, v_w_ada, v_b_ada, v_w_in, v_conv_w, v_conv_b, v_dt_bias, v_a_log, v_d_skip, v_ssd_norm_g, v_gm_norm_g, v_gm_norm_b, v_w_spatial, v_b_spatial, v_b_gate, v_w_ssd_proj, v_w_gm_proj, v_w_out, v_ln1_g, v_ln1_b, v_w_ff1, v_w_ff3, v_w_ff2, v_ln2_g, v_ln2_b):
    W = dict(c_ctx=c_ctx, ln0_g=ln0_g, ln0_b=ln0_b, w_ada=w_ada, b_ada=b_ada, w_in=w_in, conv_w=conv_w,
             conv_b=conv_b, dt_bias=dt_bias, a_log=a_log, d_skip=d_skip, ssd_norm_g=ssd_norm_g,
             gm_norm_g=gm_norm_g, gm_norm_b=gm_norm_b, w_spatial=w_spatial, b_spatial=b_spatial,
             b_gate=b_gate, w_ssd_proj=w_ssd_proj, w_gm_proj=w_gm_proj, w_out=w_out, ln1_g=ln1_g,
             ln1_b=ln1_b, w_ff1=w_ff1, w_ff3=w_ff3, w_ff2=w_ff2, ln2_g=ln2_g, ln2_b=ln2_b)
    M = dict(c_ctx=m_c_ctx, ln0_g=m_ln0_g, ln0_b=m_ln0_b, w_ada=m_w_ada, b_ada=m_b_ada, w_in=m_w_in,
             conv_w=m_conv_w, conv_b=m_conv_b, dt_bias=m_dt_bias, a_log=m_a_log, d_skip=m_d_skip,
             ssd_norm_g=m_ssd_norm_g, gm_norm_g=m_gm_norm_g, gm_norm_b=m_gm_norm_b,
             w_spatial=m_w_spatial, b_spatial=m_b_spatial, b_gate=m_b_gate, w_ssd_proj=m_w_ssd_proj,
             w_gm_proj=m_w_gm_proj, w_out=m_w_out, ln1_g=m_ln1_g, ln1_b=m_ln1_b, w_ff1=m_w_ff1,
             w_ff3=m_w_ff3, w_ff2=m_w_ff2, ln2_g=m_ln2_g, ln2_b=m_ln2_b)
    V = dict(c_ctx=v_c_ctx, ln0_g=v_ln0_g, ln0_b=v_ln0_b, w_ada=v_w_ada, b_ada=v_b_ada, w_in=v_w_in,
             conv_w=v_conv_w, conv_b=v_conv_b, dt_bias=v_dt_bias, a_log=v_a_log, d_skip=v_d_skip,
             ssd_norm_g=v_ssd_norm_g, gm_norm_g=v_gm_norm_g, gm_norm_b=v_gm_norm_b,
             w_spatial=v_w_spatial, b_spatial=v_b_spatial, b_gate=v_b_gate, w_ssd_proj=v_w_ssd_proj,
             w_gm_proj=v_w_gm_proj, w_out=v_w_out, ln1_g=v_ln1_g, ln1_b=v_ln1_b, w_ff1=v_w_ff1,
             w_ff3=v_w_ff3, w_ff2=v_w_ff2, ln2_g=v_ln2_g, ln2_b=v_ln2_b)

    me = 4 * lax.axis_index("x") + 2 * lax.axis_index("y") + lax.axis_index("c")
    xl, cx, tgt = x[0], ctx[0], loss_target[0]
    L = xl.shape[0]
    assert cx.shape[0] == TL and L % TL == 0
    ada_n = w_ada.shape[2]
    cw_n = conv_w.shape[2]

    small1 = _pad_rows(jnp.concatenate([c, _slab([conv_w[0]], 1)], axis=0), 8)
    g1 = _all_gather(small1, "ag_small")
    c_all = g1[:, 0, :]
    conv_w_full = g1[:, 1, :5 * cw_n].reshape(NDEV, 5, cw_n).transpose(1, 0, 2).reshape(5, NDEV * cw_n)
    big_local = _slab([W[n][0] for n in BIG], BIG_ROWS, BIG_ALIGN).astype(_MXU)
    gw = _all_gather(big_local, "ag_weights")

    def rows_of(n):
        return W[n][0].size // D

    offs, o = {}, 0
    for n in BIG:
        offs[n] = o
        o += _padded(rows_of(n) * D, BIG_ALIGN) // D

    def col_sharded(n):
        k, nc = W[n].shape[1], W[n].shape[2]
        blk = gw[:, offs[n]:offs[n] + rows_of(n), :].reshape(NDEV, k, nc)
        return blk.transpose(1, 0, 2).reshape(k, NDEV * nc)

    def row_sharded(n):
        return gw[:, offs[n]:offs[n] + rows_of(n), :].reshape(-1, D)

    w_in_p = _perm_cols(col_sharded("w_in"))
    w_ssd_f, w_gm_f, w_out_f = row_sharded("w_ssd_proj"), row_sharded("w_gm_proj"), row_sharded("w_out")
    w13 = jnp.concatenate([col_sharded("w_ff1"), col_sharded("w_ff3")], axis=1)
    w_ff2_f = row_sharded("w_ff2")

    c16 = _pad_rows(jnp.concatenate([c_all, _row(c_ctx)], axis=0), 16)
    b_ada_sh = lax.dynamic_slice(b_ada, (0, ada_n * me), (1, ada_n))
    modp = _ada_fwd(c16, w_ada[0], b_ada_sh)
    mod16 = _all_gather(modp, "ag_mod").transpose(1, 0, 2).reshape(16, NDEV * ada_n)
    modx = _pad_rows(lax.dynamic_slice(mod16, (me, 0), (1, 6 * D)).reshape(6, D), 8)
    modc = _pad_rows(mod16[8].reshape(6, D), 8)

    g0, b0 = _row(ln0_g), _row(ln0_b)
    xn, h1 = _ln0_fwd(xl, cx, g0, b0, modx, modc)
    p = _mm(h1, w_in_p, "nn", F32, "mm_p")
    conv_w8 = _pad_rows(conv_w_full, 8)
    xbc = _conv_fwd(p, conv_w8, conv_b)
    prm = _pad_rows(jnp.pad(jnp.stack([dt_bias.reshape(32), a_log.reshape(32)]), ((0, 0), (0, 96))), 8)
    yf, yb, hpf, hpb = _ssd_fwd(xbc, p, prm)
    dsk = _row(jnp.repeat(d_skip[0, 0] + d_skip[0, 1], HP))
    ws_m = w_spatial[0].astype(_MXU)
    bsT = jnp.pad(b_spatial[0].T, ((0, 0), (0, 120)))
    mixp = (dsk, ssd_norm_g, gm_norm_g, gm_norm_b, ws_m, bsT)
    yssd, ygm = _mix_fwd(yf, yb, p, xbc, *mixp)
    a1 = _mm(yssd, w_ssd_f, "nn", F32, "mm_a1")
    a2 = _mm(ygm, w_gm_f, "nn", F32, "mm_a2")
    merged = _gate_fwd(a1, a2, p, b_gate)
    out = _mm(merged, w_out_f, "nn", F32, "mm_out")
    r1, h2 = _res1_fwd(xn, out, modx, ln1_g, ln1_b)
    f13 = _mm(h2, w13, "nn", F32, "mm_f13")
    ff = _glu_fwd(f13)
    o2 = _mm(ff, w_ff2_f, "nn", F32, "mm_o2")

    dr2, do2, st2, loss_slab = _res2(r1, o2, tgt, modx, ln1_g, ln1_b, ln2_g, ln2_b)
    loss = lax.psum(loss_slab[0, 0], ("x", "y", "c"))
    dff = _mm(do2, w_ff2_f, "nt", F32, "mm_dff")
    df13 = _glu_bwd(dff, f13)
    dh2 = _mm(df13, w13, "nt", F32, "mm_dh2")
    dw_ff2 = _mm(ff, do2, "tn", F32, "mm_dw_ff2")
    dw13 = _mm(h2, df13, "tn", F32, "mm_dw13")
    dr1, dout, st1 = _res1_bwd(dr2, dh2, r1, out, modx, ln1_g, ln1_b)
    dmg = _mm(dout, w_out_f, "nt", F32, "mm_dmerged")
    dw_out = _mm(merged, dout, "tn", F32, "mm_dw_out")
    dp = jnp.zeros((L + TL, NPJ), _MXU)
    dp, da1, da2, stg = _gate_bwd(dmg, a1, a2, p, b_gate, dp)
    dys = _mm(da1, w_ssd_f, "nt", F32, "mm_dyssd")
    dym = _mm(da2, w_gm_f, "nt", F32, "mm_dygm")
    dw_ssd = _mm(yssd, da1, "tn", F32, "mm_dw_ssd")
    dw_gm = _mm(ygm, da2, "tn", F32, "mm_dw_gm")
    dp, dyd, stm, dws, dbsT = _mix_bwd(dys, dym, yf, yb, p, xbc, dp, *mixp)
    dxf, dxb, ddf, ddb, sts = _ssd_bwd(xbc, p, prm, dsk, dyd, hpf, hpb)
    dp, dcw, dcb = _conv_bwd(dxf, dxb, p, conv_w8, conv_b, dp)
    dp, std = _dt_bwd(ddf, ddb, dp)
    dh1 = _mm(dp, w_in_p, "nt", F32, "mm_dh1")
    dw_in_p = _mm(h1, dp, "tn", F32, "mm_dw_in")
    grad_x, st0 = _ln0_bwd(dh1, dr1, xl, cx, g0, b0, modx, modc)

    zero = jnp.zeros((D,), F32)
    dmod = jnp.stack([jnp.concatenate([st0[0], st0[1], st1[4], st1[1], st1[0], st2[2]]),
                      jnp.concatenate([st0[2], st0[3], zero, zero, zero, zero])])
    g16 = _all_gather(_pad_rows(dmod, 8), "ag_dmod")[:, 0:2, :].reshape(16, 6 * D)
    g16_sh = lax.dynamic_slice(g16, (0, ada_n * me), (16, ada_n))
    c16b = jnp.stack([c_all, jnp.broadcast_to(_row(c_ctx), (NDEV, D))], axis=1).reshape(16, D)
    dw_ada, db_ada8, dcc8 = _ada_bwd(c16b, g16, g16_sh, w_ada[0])

    part = dict(
        c_ctx=dcc8[0], ln0_g=st0[4], ln0_b=st0[5], conv_w=dcw[0:5], conv_b=dcb[0],
        dt_bias=std[0, 0:32], a_log=sts[0, 0:32], d_skip=jnp.tile(sts[1, 0:16], 2),
        ssd_norm_g=stm[0], gm_norm_g=stm[1], gm_norm_b=stm[2], w_spatial=dws,
        b_spatial=dbsT[:, 0:8].T, b_gate=stg[0], ln1_g=st1[2], ln1_b=st1[3], ln2_g=st2[0], ln2_b=st2[1])
    pnames = list(part)
    psum8 = _sum8(_all_gather(_slab([part[n] for n in pnames], SMALL_ROWS), "ag_smallgrads"), "sum_smallgrads")
    small = dict(zip(pnames, _unslab(psum8, [part[n].shape for n in pnames])))
    grads = {n: small[n].reshape(W[n].shape) for n in pnames if n != "conv_w"}
    grads["conv_w"] = lax.dynamic_slice(small["conv_w"], (0, cw_n * me), (5, cw_n)).reshape(conv_w.shape)
    grads["b_ada"] = db_ada8[0:1]
    grads["w_ada"] = dw_ada.reshape(w_ada.shape)

    def aligned(blocks):
        return jnp.pad(blocks, ((0, 0), (0, -blocks.shape[1] % BIG_ALIGN), (0, 0)))

    def to_owner_cols(gfull):
        k, n = gfull.shape
        return aligned(gfull.reshape(k, NDEV, n // NDEV).transpose(1, 0, 2).reshape(NDEV, -1, D))

    def to_owner_rows(gfull):
        return aligned(gfull.reshape(NDEV, -1, D))

    gbig = jnp.concatenate(
        [to_owner_cols(_unperm_cols(dw_in_p)), to_owner_rows(dw_ssd), to_owner_rows(dw_gm),
         to_owner_rows(dw_out), to_owner_cols(dw13[:, :DFF]), to_owner_cols(dw13[:, DFF:]),
         to_owner_rows(dw_ff2)], axis=1)
    gbig = jnp.pad(gbig, ((0, 0), (0, BIG_ROWS - gbig.shape[1]), (0, 0))).astype(_MXU)
    gsum = _sum8(_owner_exchange(gbig, "xchg_grads"), "sum_grads")
    for n, gshard in zip(BIG, _unslab(gsum, [W[n].shape for n in BIG], BIG_ALIGN)):
        grads[n] = gshard

    delta, new_m, new_v = {}, {}, {}

    def adam_group(names, rows, tag, align=0):
        shapes = [W[n].shape for n in names]
        outs = _adamw(*[_slab([src[n] for n in names], rows, align) for src in (grads, W, M, V)], tag)
        for res, slab in zip((delta, new_m, new_v), outs):
            for n, a in zip(names, _unslab(slab, shapes, align)):
                res[n] = a

    adam_group(BIG, BIG_ROWS, "adamw_big", BIG_ALIGN)
    adam_group(["w_ada", "conv_w"], ada_n + 256, "adamw_shard")
    adam_group(REPL, SMALL_ROWS, "adamw_repl")

    return (loss, grad_x[None], *[grads[n] for n in WEIGHTS], *[delta[n] for n in WEIGHTS],
            *[new_m[n] for n in WEIGHTS], *[new_v[n] for n in WEIGHTS])
```

```python
import functools

import jax
import jax.numpy as jnp
from jax import lax
from jax.experimental import pallas as pl
from jax.experimental.pallas import tpu as pltpu

_MXU = jnp.bfloat16
F32 = jnp.float32
D = 1024
TL = 256
Q = 128
NH, HP, NS, HPG = 16, 64, 128, 8
DFF = 2816
ALPHA = 2.0 ** 0.25
EPS = 1e-5
OZ, OU, OV, OXS, OG, OB, OC, ODT, NPJ = 0, 1024, 2048, 3072, 4096, 6144, 6400, 6656, 6912
NNAT = 6688
NDEV = 8
ADAM_LR, ADAM_B1, ADAM_B2, ADAM_EPS, ADAM_WD, ADAM_STEP = 1e-3, 0.9, 0.999, 1e-8, 0.01, 10
VMEM_LIMIT = 48 * 1024 * 1024

NN = ((1,), (0,))
NT = ((1,), (1,))
TN = ((0,), (0,))
MESH = pl.DeviceIdType.MESH


def _dot(a, b, dims):
    return lax.dot_general(a.astype(_MXU), b.astype(_MXU), (dims, ((), ())),
                           preferred_element_type=F32)


def _tile(n, cands):
    for c in cands:
        if n % c == 0:
            return c
    return n


def _divisor_tile(n, cap, mult):
    best = n
    for t in range(mult, min(n, cap) + 1, mult):
        if n % t == 0:
            best = t
    return best


def _params(sem):
    return pltpu.CompilerParams(dimension_semantics=sem, vmem_limit_bytes=VMEM_LIMIT)


def _cst(shape):
    nd = len(shape)
    return pl.BlockSpec(shape, lambda *_: (0,) * nd)


def _rt(w, cb=0, rows=TL):
    return pl.BlockSpec((rows, w), lambda i: (i, cb))


def _rtc(w, nt, cb=0):
    return pl.BlockSpec((TL, w), lambda i: (jnp.minimum(i, nt - 1), cb))


def _sig(x):
    return jax.nn.sigmoid(x)


def _softplus(x):
    return jnp.maximum(x, 0.0) + jnp.log1p(jnp.exp(-jnp.abs(x)))


_G0, _G1 = 0.7978845608028654, 0.044715


def _gelu(x):
    t = jnp.tanh(_G0 * (x + _G1 * x * x * x))
    return 0.5 * x * (1.0 + t), t


def _gelu_grad(x, t):
    return 0.5 * (1.0 + t) + 0.5 * x * (1.0 - t * t) * _G0 * (1.0 + 3.0 * _G1 * x * x)


def _ln(r):
    mu = jnp.mean(r, axis=-1, keepdims=True)
    xc = r - mu
    var = jnp.mean(xc * xc, axis=-1, keepdims=True)
    rstd = lax.rsqrt(var + EPS)
    return xc * rstd, rstd


def _ln_bwd(dyh, xhat, rstd):
    return rstd * (dyh - jnp.mean(dyh, axis=-1, keepdims=True)
                   - xhat * jnp.mean(dyh * xhat, axis=-1, keepdims=True))


def _colsum(v):
    return jnp.sum(v, axis=0, keepdims=True)


def _sum11(v):
    return jnp.sum(jnp.sum(v, axis=1, keepdims=True), axis=0, keepdims=True)


def _cumsum_rows(a, rev):
    n = a.shape[0]
    row = lax.broadcasted_iota(jnp.int32, a.shape, 0)
    s = 1
    while s < n:
        if rev:
            a = a + jnp.where(row < n - s, pltpu.roll(a, n - s, 0), 0.0)
        else:
            a = a + jnp.where(row >= s, pltpu.roll(a, s, 0), 0.0)
        s *= 2
    return a


def _mm(a, b, mode, out_dtype, name):
    if mode == "tn":
        K, M = a.shape
    else:
        M, K = a.shape
    N = b.shape[0] if mode == "nt" else b.shape[1]
    tm = _divisor_tile(M, 1408, 128) if mode == "tn" else _divisor_tile(M, 1088, 16)
    tn = _divisor_tile(N, 1408, 128)
    tk = _divisor_tile(K, 2304, 128)
    nk = K // tk
    dims = {"nn": NN, "nt": NT, "tn": TN}[mode]
    assert out_dtype == F32

    def body(a_ref, b_ref, o_ref):
        prod = _dot(a_ref[...], b_ref[...], dims)
        if nk == 1:
            o_ref[...] = prod
            return
        k = pl.program_id(2)

        @pl.when(k == 0)
        def _():
            o_ref[...] = prod

        @pl.when(k > 0)
        def _():
            o_ref[...] += prod

    if mode == "tn":
        a_spec = pl.BlockSpec((tk, tm), lambda i, j, k: (k, i))
    else:
        a_spec = pl.BlockSpec((tm, tk), lambda i, j, k: (i, k))
    if mode == "nt":
        b_spec = pl.BlockSpec((tn, tk), lambda i, j, k: (j, k))
    else:
        b_spec = pl.BlockSpec((tk, tn), lambda i, j, k: (k, j))
    return pl.pallas_call(
        body, name=name, grid=(M // tm, N // tn, nk),
        in_specs=[a_spec, b_spec],
        out_specs=pl.BlockSpec((tm, tn), lambda i, j, k: (i, j)),
        out_shape=jax.ShapeDtypeStruct((M, N), out_dtype),
        compiler_params=_params(("parallel", "parallel", "arbitrary")),
    )(a, b)


def _all_gather(x, name):
    def body(x_ref, out_ref, send_sems, recv_sems, local_sem):
        mx, my, mc = lax.axis_index("x"), lax.axis_index("y"), lax.axis_index("c")
        me, sibling = (mx, my, mc), (mx, my, 1 - mc)
        chips = [(1 - mx, my), (mx, 1 - my), (1 - mx, 1 - my)]

        def slot(px, py, pc):
            return out_ref.at[4 * px + 2 * py + pc]

        def copy(k, block, to, src=None):
            return pltpu.make_async_remote_copy(
                src_ref=slot(*block) if src is None else src, dst_ref=slot(*block),
                send_sem=send_sems.at[k], recv_sem=recv_sems.at[k],
                device_id=to, device_id_type=MESH)

        mine = pltpu.make_async_copy(x_ref, slot(*me), local_sem)
        mine.start()
        first = [copy(0, me, sibling, src=x_ref)]
        first += [copy(1 + j, me, (*chip, mc), src=x_ref) for j, chip in enumerate(chips)]
        for cp in first:
            cp.start()
        passed = [copy(4 + j, (*chip, mc), sibling) for j, chip in enumerate(chips)]
        for j, chip in enumerate(chips):
            copy(1 + j, (*chip, mc), me).wait_recv()
            passed[j].start()
        copy(0, sibling, me).wait_recv()
        for j, chip in enumerate(chips):
            copy(4 + j, (*chip, 1 - mc), me).wait_recv()
        for cp in first + passed:
            cp.wait_send()
        mine.wait()

    return pl.pallas_call(
        body, name=name,
        out_shape=jax.ShapeDtypeStruct((NDEV,) + x.shape, x.dtype),
        in_specs=[pl.BlockSpec(memory_space=pl.ANY)],
        out_specs=pl.BlockSpec(memory_space=pl.ANY),
        scratch_shapes=[pltpu.SemaphoreType.DMA((7,)), pltpu.SemaphoreType.DMA((7,)),
                        pltpu.SemaphoreType.DMA],
    )(x)


def _owner_exchange(g, name):
    def body(g_ref, out_ref, send_sems, recv_sems, local_sem):
        mx, my, mc = lax.axis_index("x"), lax.axis_index("y"), lax.axis_index("c")
        local = pltpu.make_async_copy(g_ref.at[4 * mx + 2 * my + mc], out_ref.at[0], local_sem)
        local.start()
        copies = []
        for f in range(1, NDEV):
            px = 1 - mx if (f >> 2) & 1 else mx
            py = 1 - my if (f >> 1) & 1 else my
            pc = 1 - mc if f & 1 else mc
            cp = pltpu.make_async_remote_copy(
                src_ref=g_ref.at[4 * px + 2 * py + pc], dst_ref=out_ref.at[f],
                send_sem=send_sems.at[f - 1], recv_sem=recv_sems.at[f - 1],
                device_id=(px, py, pc), device_id_type=MESH)
            cp.start()
            copies.append(cp)
        for cp in copies:
            cp.wait_recv()
        for cp in copies:
            cp.wait_send()
        local.wait()

    return pl.pallas_call(
        body, name=name,
        out_shape=jax.ShapeDtypeStruct(g.shape, g.dtype),
        in_specs=[pl.BlockSpec(memory_space=pl.ANY)],
        out_specs=pl.BlockSpec(memory_space=pl.ANY),
        scratch_shapes=[pltpu.SemaphoreType.DMA((7,)), pltpu.SemaphoreType.DMA((7,)),
                        pltpu.SemaphoreType.DMA],
    )(g)


def _sum8(r, name):
    _, R, C = r.shape
    tr = _tile(R, (256, 160, 128, 64, 32, 16, 8))

    def body(r_ref, o_ref):
        acc = r_ref[0].astype(F32)
        for k in range(1, NDEV):
            acc = acc + r_ref[k].astype(F32)
        o_ref[...] = acc

    return pl.pallas_call(
        body, name=name, grid=(R // tr,),
        in_specs=[pl.BlockSpec((NDEV, tr, C), lambda i: (0, i, 0))],
        out_specs=pl.BlockSpec((tr, C), lambda i: (i, 0)),
        out_shape=jax.ShapeDtypeStruct((R, C), F32),
        compiler_params=_params(("parallel",)),
    )(r)


def _adamw(g, w, m, v, name):
    R, C = g.shape
    tr = _tile(R, (256, 160, 128, 64, 32, 16, 8))
    bc1 = 1.0 - ADAM_B1 ** ADAM_STEP
    bc2 = 1.0 - ADAM_B2 ** ADAM_STEP

    def body(g_ref, w_ref, m_ref, v_ref, d_ref, mo_ref, vo_ref):
        gg = g_ref[...]
        mn = ADAM_B1 * m_ref[...] + (1.0 - ADAM_B1) * gg
        vn = ADAM_B2 * v_ref[...] + (1.0 - ADAM_B2) * (gg * gg)
        mh = mn / bc1
        vh = vn / bc2
        d_ref[...] = -ADAM_LR * (mh / (jnp.sqrt(vh) + ADAM_EPS) + ADAM_WD * w_ref[...])
        mo_ref[...] = mn
        vo_ref[...] = vn

    spec = pl.BlockSpec((tr, C), lambda i: (i, 0))
    sh = jax.ShapeDtypeStruct((R, C), F32)
    return pl.pallas_call(
        body, name=name, grid=(R // tr,), in_specs=[spec] * 4, out_specs=[spec] * 3,
        out_shape=[sh] * 3, compiler_params=_params(("parallel",)),
    )(g, w, m, v)


def _ada_fwd(c16, w_sh, b_sh):
    def body(c_ref, w_ref, b_ref, o_ref):
        c = c_ref[...]
        o_ref[...] = _dot(c * _sig(c), w_ref[...], NN) + b_ref[...]

    return pl.pallas_call(
        body, name="ada_fwd", out_shape=jax.ShapeDtypeStruct((16, w_sh.shape[1]), F32),
        compiler_params=pltpu.CompilerParams(vmem_limit_bytes=VMEM_LIMIT),
    )(c16, w_sh, b_sh)


def _ada_bwd(c16, g16, g16_sh, w_sh):
    ncol = w_sh.shape[1]

    def body(c_ref, g_ref, gs_ref, w_ref, dw_ref, db_ref, dc_ref):
        c = c_ref[...]
        s = _sig(c)
        gs = gs_ref[...]
        dw_ref[...] = _dot(c * s, gs, TN)
        db_ref[...] = jnp.broadcast_to(_colsum(g_ref[...]), db_ref.shape)
        odd = lax.broadcasted_iota(jnp.int32, gs.shape, 0) % 2 == 1
        gc = _colsum(jnp.where(odd, gs, 0.0))
        ds = _dot(jnp.broadcast_to(gc, (8, ncol)), w_ref[...], NT)
        c1 = c[1:2, :]
        s1 = s[1:2, :]
        dc_ref[...] = ds * (s1 * (1.0 + c1 * (1.0 - s1)))

    return pl.pallas_call(
        body, name="ada_bwd",
        out_shape=[jax.ShapeDtypeStruct(w_sh.shape, F32),
                   jax.ShapeDtypeStruct((8, g16.shape[1]), F32),
                   jax.ShapeDtypeStruct((8, D), F32)],
        compiler_params=pltpu.CompilerParams(vmem_limit_bytes=VMEM_LIMIT),
    )(c16, g16, g16_sh, w_sh)


def _ln0_fwd(x, ctx, g, b, modx, modc):
    L = x.shape[0]
    nt = L // TL

    def body(x_ref, c_ref, g_ref, b_ref, mx_ref, mc_ref, xn_ref, h_ref):
        isc = pl.program_id(0) == nt
        xin = jnp.where(isc, c_ref[...], x_ref[...])
        sh = jnp.where(isc, mc_ref[0:1, :], mx_ref[0:1, :])
        sc = jnp.where(isc, mc_ref[1:2, :], mx_ref[1:2, :])
        xhat, _ = _ln(xin)
        xn = xhat * g_ref[...] + b_ref[...]
        xn_ref[...] = xn
        h_ref[...] = (xn * (1.0 + sc) + sh).astype(h_ref.dtype)

    return pl.pallas_call(
        body, name="ln0_fwd", grid=(nt + 1,),
        in_specs=[_rtc(D, nt), _cst((TL, D)), _cst((1, D)), _cst((1, D)), _cst((8, D)), _cst((8, D))],
        out_specs=[_rt(D), _rt(D)],
        out_shape=[jax.ShapeDtypeStruct((L + TL, D), F32), jax.ShapeDtypeStruct((L + TL, D), _MXU)],
        compiler_params=_params(("parallel",)),
    )(x, ctx, g, b, modx, modc)


def _xbc_colblk(j):
    return jnp.where(j < 8, OXS // 128 + j, OB // 128 + j - 8)


def _conv_taps(p_ref, r0, first, last):
    main = p_ref[pl.ds(r0, TL), :]
    zero = jnp.zeros((8, main.shape[1]), F32)
    prev = zero if first else p_ref[pl.ds(r0 - 8, 8), :]
    nxt = zero if last else p_ref[pl.ds(r0 + TL, 8), :]
    ext = jnp.concatenate([prev, main, nxt], axis=0)
    n = TL + 16
    return [pltpu.roll(ext, (2 - k) % n, 0)[8:8 + TL] for k in range(5)]


def _seq_chunks(L):
    nt = L // TL
    return [(r * TL, r == 0, r == nt - 1) for r in range(nt)] + [(L, True, True)]


def _conv_fwd(p, conv_w8, conv_b):
    RT = p.shape[0]
    L = RT - TL
    chunks = _seq_chunks(L)

    def body(p_ref, w_ref, b_ref, o_ref):
        w = w_ref[...]
        bias = b_ref[...]
        for r0, first, last in chunks:
            taps = _conv_taps(p_ref, r0, first, last)
            pre = bias + sum(w[k:k + 1, :] * taps[k] for k in range(5))
            o_ref[pl.ds(r0, TL), :] = pre * _sig(pre)

    return pl.pallas_call(
        body, name="conv_fwd", grid=(12,),
        in_specs=[pl.BlockSpec((RT, 128), lambda j: (0, _xbc_colblk(j))),
                  pl.BlockSpec((8, 128), lambda j: (0, j)),
                  pl.BlockSpec((1, 128), lambda j: (0, j))],
        out_specs=pl.BlockSpec((RT, 128), lambda j: (0, j)),
        out_shape=jax.ShapeDtypeStruct((RT, 1536), F32),
        compiler_params=_params(("parallel",)),
    )(p, conv_w8, conv_b)


def _ssd_common(dtraw, dtb, a32, rev):
    dt = _softplus(dtraw + dtb)
    acum = _cumsum_rows(dt * a32, rev)
    ii = lax.broadcasted_iota(jnp.int32, (Q, Q), 0)
    jj = lax.broadcasted_iota(jnp.int32, (Q, Q), 1)
    mask = (ii <= jj) if rev else (ii >= jj)
    return dt, acum, acum.T, dt.T, mask


def _ssd_orders(ncl, ncc):
    nc = ncl + ncc

    def cf(s):
        return jnp.where(s < ncc, ncl + s, s - ncc)

    def cb(s):
        return nc - 1 - s

    return cf, cb


def _ssd_fwd(xbc, p, prm):
    RT = xbc.shape[0]
    nc = RT // Q
    ncc = TL // Q
    cf, cb = _ssd_orders(nc - ncc, ncc)

    def one_dir(x_ref, dt_ref, prm_ref, y_ref, hp_ref, H_ref, d):
        rev = d == 1
        a32 = -jnp.exp(prm_ref[1:2, :])
        dt, acum, acumT, dtT, mask = _ssd_common(dt_ref[...], prm_ref[0:1, :], a32, rev)
        end = 0 if rev else Q - 1
        for g in range(2):
            Bg = x_ref[:, D + g * NS:D + (g + 1) * NS]
            Cg = x_ref[:, D + 2 * NS + g * NS:D + 2 * NS + (g + 1) * NS]
            CB = _dot(Cg, Bg, NT)
            for hh in range(HPG):
                h = g * HPG + hh
                ln = 16 * d + h
                col = acum[:, ln:ln + 1]
                rowv = acumT[ln:ln + 1, :]
                a_end = rowv[:, end:end + 1]
                Lm = jnp.exp(jnp.where(mask, col - rowv, -1e30))
                W = CB * Lm * dtT[ln:ln + 1, :]
                Xh = x_ref[:, h * HP:(h + 1) * HP]
                Hp = H_ref[h * HP:(h + 1) * HP, :]
                y = _dot(W, Xh, NN) + jnp.exp(col) * _dot(Cg, Hp, NT)
                y_ref[:, h * HP:(h + 1) * HP] = y
                dcol = jnp.exp(a_end - col) * dt[:, ln:ln + 1]
                hp_ref[0, h * HP:(h + 1) * HP, :] = Hp
                H_ref[h * HP:(h + 1) * HP, :] = jnp.exp(a_end) * Hp + _dot(Xh * dcol, Bg, TN)

    def body(xf_ref, xb_ref, df_ref, db_ref, prm_ref, yf_ref, yb_ref, hf_ref, hb_ref, Hf, Hb):
        @pl.when(pl.program_id(0) == 0)
        def _():
            Hf[...] = jnp.zeros_like(Hf)
            Hb[...] = jnp.zeros_like(Hb)

        one_dir(xf_ref, df_ref, prm_ref, yf_ref, hf_ref, Hf, 0)
        one_dir(xb_ref, db_ref, prm_ref, yb_ref, hb_ref, Hb, 1)

    ysh = jax.ShapeDtypeStruct((RT, D), F32)
    hsh = jax.ShapeDtypeStruct((nc, NH * HP, NS), F32)
    hspec = pl.BlockSpec((1, NH * HP, NS), lambda s: (s, 0, 0))
    return pl.pallas_call(
        body, name="ssd_fwd", grid=(nc,),
        in_specs=[pl.BlockSpec((Q, 1536), lambda s: (cf(s), 0)),
                  pl.BlockSpec((Q, 1536), lambda s: (cb(s), 0)),
                  pl.BlockSpec((Q, 128), lambda s: (cf(s), ODT // 128)),
                  pl.BlockSpec((Q, 128), lambda s: (cb(s), ODT // 128)),
                  _cst((8, 128))],
        out_specs=[pl.BlockSpec((Q, D), lambda s: (cf(s), 0)),
                   pl.BlockSpec((Q, D), lambda s: (cb(s), 0)), hspec, hspec],
        out_shape=[ysh, ysh, hsh, hsh],
        scratch_shapes=[pltpu.VMEM((NH * HP, NS), F32), pltpu.VMEM((NH * HP, NS), F32)],
        compiler_params=_params(("arbitrary",)),
    )(xbc, xbc, p, p, prm)


def _ssd_bwd(xbc, p, prm, dsk, dyd, hpf, hpb):
    RT = xbc.shape[0]
    nc = RT // Q
    ncc = TL // Q
    ncl = nc - ncc
    cf, cb = _ssd_orders(ncl, ncc)

    def rs(t):
        return nc - 1 - t

    def one_dir(x_ref, dt_ref, prm_ref, dsk_ref, dy_ref, is_ctx, hp_ref, dH_ref,
                dx_ref, ddt_ref, st_ref, d):
        rev = d == 1
        a32 = -jnp.exp(prm_ref[1:2, :])
        dtraw = dt_ref[...]
        dtb = prm_ref[0:1, :]
        dt, acum, acumT, dtT, mask = _ssd_common(dtraw, dtb, a32, rev)
        end = 0 if rev else Q - 1
        lane = lax.broadcasted_iota(jnp.int32, (Q, 128), 1)
        srow = lax.broadcasted_iota(jnp.int32, (Q, 128), 0)
        dyscale = jnp.where(is_ctx, 0.0, 1.0)
        c_dacum = jnp.zeros((Q, 128), F32)
        r_dacum = jnp.zeros((Q, 128), F32)
        c_ddt = jnp.zeros((Q, 128), F32)
        r_ddt = jnp.zeros((Q, 128), F32)
        dskacc = jnp.zeros((1, 128), F32)
        for g in range(2):
            Bg = x_ref[:, D + g * NS:D + (g + 1) * NS]
            Cg = x_ref[:, D + 2 * NS + g * NS:D + 2 * NS + (g + 1) * NS]
            CB = _dot(Cg, Bg, NT)
            dCB = jnp.zeros((Q, Q), F32)
            dBg = jnp.zeros((Q, NS), F32)
            dCg = jnp.zeros((Q, NS), F32)
            for hh in range(HPG):
                h = g * HPG + hh
                ln = 16 * d + h
                hs = slice(h * HP, (h + 1) * HP)
                col = acum[:, ln:ln + 1]
                rowv = acumT[ln:ln + 1, :]
                dtr = dtT[ln:ln + 1, :]
                dtc = dt[:, ln:ln + 1]
                a_end = rowv[:, end:end + 1]
                Lm = jnp.exp(jnp.where(mask, col - rowv, -1e30))
                E = jnp.exp(col)
                ecol = jnp.exp(a_end - col)
                dcol = ecol * dtc
                Xh = x_ref[:, hs]
                dY = dy_ref[:, hs] * dyscale
                Hp = hp_ref[0, hs, :]
                dHn = dH_ref[hs, :]
                W = CB * Lm * dtr
                dW = _dot(dY, Xh, NT)
                Mm = dW * CB * Lm
                T = Mm * dtr
                dCB = dCB + dW * Lm * dtr
                BdH = _dot(Bg, dHn, NT)
                dX = _dot(W, dY, TN) + dcol * BdH
                if d == 0:
                    dX = dX + dY * dsk_ref[:, hs]
                    dskacc = dskacc + jnp.where(lane[0:1, :] == h, _sum11(dY * Xh), 0.0)
                dx_ref[:, hs] = dX
                xb = jnp.sum(Xh * BdH, axis=1, keepdims=True)
                scol = dcol * xb
                G = _dot(dY, Hp, NN)
                dCg = dCg + E * G
                qcol = E * jnp.sum(G * Cg, axis=1, keepdims=True)
                dBg = dBg + _dot(Xh * dcol, dHn, NN)
                dH_ref[hs, :] = jnp.exp(a_end) * dHn + _dot(dY * E, Cg, TN)
                eterm = jnp.exp(a_end) * _sum11(dHn * Hp) + _sum11(scol)
                cvec = jnp.sum(T, axis=1, keepdims=True) + qcol - scol
                cvec = cvec + jnp.where(srow[:, 0:1] == end, eterm, 0.0)
                c_dacum = c_dacum + jnp.where(lane == ln, cvec, 0.0)
                r_dacum = r_dacum - jnp.where(srow == ln, _colsum(T), 0.0)
                c_ddt = c_ddt + jnp.where(lane == ln, ecol * xb, 0.0)
                r_ddt = r_ddt + jnp.where(srow == ln, _colsum(Mm), 0.0)
            dBg = dBg + _dot(dCB, Cg, TN)
            dCg = dCg + _dot(dCB, Bg, NN)
            dx_ref[:, D + g * NS:D + (g + 1) * NS] = dBg
            dx_ref[:, D + 2 * NS + g * NS:D + 2 * NS + (g + 1) * NS] = dCg
        dacum = c_dacum + r_dacum.T
        da = _cumsum_rows(dacum, not rev)
        mine = (lane >= 16 * d) & (lane < 16 * d + 16)
        ddt = jnp.where(mine, c_ddt + r_ddt.T + da * a32, 0.0)
        ddt_ref[...] = ddt * _sig(dtraw + dtb)
        st_ref[0:1, :] += _colsum(jnp.where(mine, da * dt, 0.0))
        if d == 0:
            st_ref[1:2, :] += dskacc

    def body(xf_ref, xb_ref, df_ref, db_ref, prm_ref, dsk_ref, dyf_ref, dyb_ref, hf_ref, hb_ref,
             dxf_ref, dxb_ref, ddf_ref, ddb_ref, st_ref, dHf, dHb):
        t = pl.program_id(0)

        @pl.when(t == 0)
        def _():
            dHf[...] = jnp.zeros_like(dHf)
            dHb[...] = jnp.zeros_like(dHb)
            st_ref[...] = jnp.zeros_like(st_ref)

        s = rs(t)
        one_dir(xf_ref, df_ref, prm_ref, dsk_ref, dyf_ref, cf(s) >= ncl, hf_ref, dHf,
                dxf_ref, ddf_ref, st_ref, 0)
        one_dir(xb_ref, db_ref, prm_ref, dsk_ref, dyb_ref, cb(s) >= ncl, hb_ref, dHb,
                dxb_ref, ddb_ref, st_ref, 1)

        @pl.when(t == nc - 1)
        def _():
            st_ref[0:1, :] = -jnp.exp(prm_ref[1:2, :]) * st_ref[0:1, :]

    def lat(c):
        return jnp.minimum(c, ncl - 1)

    xsh = jax.ShapeDtypeStruct((RT, 1536), F32)
    dsh = jax.ShapeDtypeStruct((RT, 128), F32)
    hspec = pl.BlockSpec((1, NH * HP, NS), lambda t: (rs(t), 0, 0))
    return pl.pallas_call(
        body, name="ssd_bwd", grid=(nc,),
        in_specs=[pl.BlockSpec((Q, 1536), lambda t: (cf(rs(t)), 0)),
                  pl.BlockSpec((Q, 1536), lambda t: (cb(rs(t)), 0)),
                  pl.BlockSpec((Q, 128), lambda t: (cf(rs(t)), ODT // 128)),
                  pl.BlockSpec((Q, 128), lambda t: (cb(rs(t)), ODT // 128)),
                  _cst((8, 128)), _cst((1, D)),
                  pl.BlockSpec((Q, D), lambda t: (lat(cf(rs(t))), 0)),
                  pl.BlockSpec((Q, D), lambda t: (lat(cb(rs(t))), 0)),
                  hspec, hspec],
        out_specs=[pl.BlockSpec((Q, 1536), lambda t: (cf(rs(t)), 0)),
                   pl.BlockSpec((Q, 1536), lambda t: (cb(rs(t)), 0)),
                   pl.BlockSpec((Q, 128), lambda t: (cf(rs(t)), 0)),
                   pl.BlockSpec((Q, 128), lambda t: (cb(rs(t)), 0)),
                   _cst((8, 128))],
        out_shape=[xsh, xsh, dsh, dsh, jax.ShapeDtypeStruct((8, 128), F32)],
        scratch_shapes=[pltpu.VMEM((NH * HP, NS), F32), pltpu.VMEM((NH * HP, NS), F32)],
        compiler_params=_params(("arbitrary",)),
    )(xbc, xbc, p, p, prm, dsk, dyd, dyd, hpf, hpb)


def _mix_fwd_vals(yf, yb, z, xs, u, v, dsk, sg, gg, gb):
    y = yf + yb + xs * dsk
    sz = _sig(z)
    hh = y * z * sz
    r = lax.rsqrt(jnp.mean(hh * hh, axis=-1, keepdims=True) + EPS)
    nh = hh * r
    ug, tu = _gelu(u)
    vg, tv = _gelu(v)
    vhat, vrstd = _ln(vg)
    vn = vhat * gg + gb
    return y, sz, r, nh, ug, tu, vg, tv, vhat, vrstd, vn


def _mix_fwd(yf, yb, p, xbc, dsk, sg, gg, gb, ws, bsT):
    L = yf.shape[0] - TL
    nt = L // TL

    def body(yf_ref, yb_ref, z_ref, xs_ref, u_ref, v_ref, dsk_ref, sg_ref, gg_ref, gb_ref,
             ws_ref, bs_ref, ys_ref, ym_ref):
        _, _, _, nh, ug, _, _, _, _, _, vn = _mix_fwd_vals(
            yf_ref[...], yb_ref[...], z_ref[...], xs_ref[...], u_ref[...], v_ref[...],
            dsk_ref[...], sg_ref[...], gg_ref[...], gb_ref[...])
        ys_ref[...] = (nh * sg_ref[...]).astype(ys_ref.dtype)
        for n in range(TL // Q):
            rs_ = slice(n * Q, (n + 1) * Q)
            for g in range(8):
                cs = slice(g * 128, (g + 1) * 128)
                mixed = _dot(ws_ref[g], vn[rs_, cs], NN) + bs_ref[:, g:g + 1]
                ym_ref[rs_, cs] = (ug[rs_, cs] * mixed).astype(ym_ref.dtype)

    return pl.pallas_call(
        body, name="mix_fwd", grid=(nt,),
        in_specs=[_rt(D), _rt(D), _rt(D, OZ // D), _rt(D, 0), _rt(D, OU // D), _rt(D, OV // D),
                  _cst((1, D)), _cst((1, D)), _cst((1, D)), _cst((1, D)),
                  _cst((8, 128, 128)), _cst((128, 128))],
        out_specs=[_rt(D), _rt(D)],
        out_shape=[jax.ShapeDtypeStruct((L, D), _MXU), jax.ShapeDtypeStruct((L, D), _MXU)],
        compiler_params=_params(("parallel",)),
    )(yf, yb, p, xbc, p, p, dsk, sg, gg, gb, ws, bsT)


def _mix_bwd(dys, dym, yf, yb, p, xbc, dp, dsk, sg, gg, gb, ws, bsT):
    L = dys.shape[0]
    nt = L // TL

    def body(dys_ref, dym_ref, yf_ref, yb_ref, z_ref, xs_ref, u_ref, v_ref, dsk_ref, sg_ref,
             gg_ref, gb_ref, ws_ref, bs_ref, dp_any, dzuv_ref, dy_ref, st_ref,
             dws_ref, dbs_ref, dvn_s):
        del dp_any
        dz_ref = dzuv_ref.at[:, OZ:OZ + D]
        du_ref = dzuv_ref.at[:, OU:OU + D]
        dv_ref = dzuv_ref.at[:, OV:OV + D]

        @pl.when(pl.program_id(0) == 0)
        def _():
            st_ref[...] = jnp.zeros_like(st_ref)
            dws_ref[...] = jnp.zeros_like(dws_ref)
            dbs_ref[...] = jnp.zeros_like(dbs_ref)

        z = z_ref[...]
        u = u_ref[...]
        v = v_ref[...]
        y, sz, r, nh, ug, tu, vg, tv, vhat, vrstd, vn = _mix_fwd_vals(
            yf_ref[...], yb_ref[...], z, xs_ref[...], u, v,
            dsk_ref[...], sg_ref[...], gg_ref[...], gb_ref[...])
        dys = dys_ref[...]
        st_ref[0:1, :] += _colsum(dys * nh)
        dn = dys * sg_ref[...]
        dhh = r * (dn - nh * jnp.mean(dn * nh, axis=-1, keepdims=True))
        dy_ref[...] = dhh * z * sz
        dz_ref[...] = (dhh * y * (sz * (1.0 + z * (1.0 - sz)))).astype(dz_ref.dtype)
        dym = dym_ref[...]
        lane = lax.broadcasted_iota(jnp.int32, (Q, 128), 1)
        dbs = jnp.zeros((Q, 128), F32)
        gu = _gelu_grad(u, tu)
        for n in range(TL // Q):
            rs_ = slice(n * Q, (n + 1) * Q)
            for g in range(8):
                cs = slice(g * 128, (g + 1) * 128)
                vb = vn[rs_, cs]
                mixed = _dot(ws_ref[g], vb, NN) + bs_ref[:, g:g + 1]
                dyb = dym[rs_, cs]
                dmx = dyb * ug[rs_, cs]
                du_ref[rs_, cs] = (dyb * mixed * gu[rs_, cs]).astype(du_ref.dtype)
                dvn_s[rs_, cs] = _dot(ws_ref[g], dmx, TN)
                dws_ref[g] += _dot(dmx, vb, NT)
                dbs = dbs + jnp.where(lane == g, jnp.sum(dmx, axis=1, keepdims=True), 0.0)
        dbs_ref[...] += dbs
        dvn = dvn_s[...]
        st_ref[1:2, :] += _colsum(dvn * vhat)
        st_ref[2:3, :] += _colsum(dvn)
        dvg = _ln_bwd(dvn * gg_ref[...], vhat, vrstd)
        dv_ref[...] = (dvg * _gelu_grad(v, tv)).astype(dv_ref.dtype)

    outs = pl.pallas_call(
        body, name="mix_bwd", grid=(nt,),
        in_specs=[_rt(D), _rt(D), _rt(D), _rt(D), _rt(D, OZ // D), _rt(D, 0), _rt(D, OU // D),
                  _rt(D, OV // D), _cst((1, D)), _cst((1, D)), _cst((1, D)), _cst((1, D)),
                  _cst((8, 128, 128)), _cst((128, 128)), pl.BlockSpec(memory_space=pl.ANY)],
        out_specs=[_rt(3 * D, 0), _rt(D), _cst((8, D)),
                   _cst((8, 128, 128)), _cst((128, 128))],
        out_shape=[jax.ShapeDtypeStruct(dp.shape, dp.dtype),
                   jax.ShapeDtypeStruct((L, D), F32), jax.ShapeDtypeStruct((8, D), F32),
                   jax.ShapeDtypeStruct((8, 128, 128), F32), jax.ShapeDtypeStruct((128, 128), F32)],
        scratch_shapes=[pltpu.VMEM((TL, D), F32)],
        input_output_aliases={14: 0},
        compiler_params=_params(("arbitrary",)),
    )(dys, dym, yf, yb, p, xbc, p, p, dsk, sg, gg, gb, ws, bsT, dp)
    return outs


def _gate_fwd(a1, a2, p, bg):
    L = a1.shape[0]

    def body(a1_ref, a2_ref, g_ref, bg_ref, m_ref):
        gt = _sig(g_ref[...] + bg_ref[...])
        m_ref[...] = (gt[:, :D] * a1_ref[...] + gt[:, D:] * a2_ref[...]).astype(m_ref.dtype)

    return pl.pallas_call(
        body, name="gate_fwd", grid=(L // TL,),
        in_specs=[_rt(D), _rt(D), _rt(2 * D, OG // (2 * D)), _cst((1, 2 * D))],
        out_specs=_rt(D), out_shape=jax.ShapeDtypeStruct((L, D), _MXU),
        compiler_params=_params(("parallel",)),
    )(a1, a2, p, bg)


def _gate_bwd(dmg, a1, a2, p, bg, dp):
    L = a1.shape[0]

    def body(dm_ref, a1_ref, a2_ref, g_ref, bg_ref, dp_any, dg_ref, da1_ref, da2_ref, st_ref):
        del dp_any

        @pl.when(pl.program_id(0) == 0)
        def _():
            st_ref[...] = jnp.zeros_like(st_ref)

        gt = _sig(g_ref[...] + bg_ref[...])
        g1 = gt[:, :D]
        g2 = gt[:, D:]
        dm = dm_ref[...]
        da1_ref[...] = (dm * g1).astype(da1_ref.dtype)
        da2_ref[...] = (dm * g2).astype(da2_ref.dtype)
        dg1 = dm * a1_ref[...] * g1 * (1.0 - g1)
        dg2 = dm * a2_ref[...] * g2 * (1.0 - g2)
        st_ref[0:1, 0:D] += _colsum(dg1)
        st_ref[0:1, D:2 * D] += _colsum(dg2)
        dg_ref[:, 0:D] = dg1.astype(dg_ref.dtype)
        dg_ref[:, D:2 * D] = dg2.astype(dg_ref.dtype)

    return pl.pallas_call(
        body, name="gate_bwd", grid=(L // TL,),
        in_specs=[_rt(D), _rt(D), _rt(D), _rt(2 * D, OG // (2 * D)), _cst((1, 2 * D)),
                  pl.BlockSpec(memory_space=pl.ANY)],
        out_specs=[_rt(2 * D, OG // (2 * D)), _rt(D), _rt(D), _cst((8, 2 * D))],
        out_shape=[jax.ShapeDtypeStruct(dp.shape, dp.dtype), jax.ShapeDtypeStruct((L, D), _MXU),
                   jax.ShapeDtypeStruct((L, D), _MXU), jax.ShapeDtypeStruct((8, 2 * D), F32)],
        input_output_aliases={5: 0},
        compiler_params=_params(("arbitrary",)),
    )(dmg, a1, a2, p, bg, dp)


def _res1_fwd(xn, out, modx, g, b):
    L = out.shape[0]

    def body(xn_ref, o_ref, mx_ref, g_ref, b_ref, r1_ref, h2_ref):
        r1 = ALPHA * xn_ref[...] + mx_ref[2:3, :] * o_ref[...]
        xhat, _ = _ln(r1)
        x1 = xhat * g_ref[...] + b_ref[...]
        r1_ref[...] = r1
        h2_ref[...] = (x1 * (1.0 + mx_ref[4:5, :]) + mx_ref[3:4, :]).astype(h2_ref.dtype)

    return pl.pallas_call(
        body, name="res1_fwd", grid=(L // TL,),
        in_specs=[_rt(D), _rt(D), _cst((8, D)), _cst((1, D)), _cst((1, D))],
        out_specs=[_rt(D), _rt(D)],
        out_shape=[jax.ShapeDtypeStruct((L, D), F32), jax.ShapeDtypeStruct((L, D), _MXU)],
        compiler_params=_params(("parallel",)),
    )(xn, out, modx, g, b)


def _glu_fwd(f13):
    L = f13.shape[0]

    def body(f1_ref, f3_ref, o_ref):
        f1 = f1_ref[...]
        o_ref[...] = (f1 * _sig(f1) * f3_ref[...]).astype(o_ref.dtype)

    return pl.pallas_call(
        body, name="glu_fwd", grid=(L // TL,),
        in_specs=[_rt(DFF, 0), _rt(DFF, 1)], out_specs=_rt(DFF),
        out_shape=jax.ShapeDtypeStruct((L, DFF), _MXU),
        compiler_params=_params(("parallel",)),
    )(f13, f13)


def _glu_bwd(dff, f13):
    L = f13.shape[0]

    def body(d_ref, f1_ref, f3_ref, o_ref):
        f1 = f1_ref[...]
        s = _sig(f1)
        d = d_ref[...]
        o_ref[:, 0:DFF] = (d * f3_ref[...] * (s * (1.0 + f1 * (1.0 - s)))).astype(o_ref.dtype)
        o_ref[:, DFF:2 * DFF] = (d * f1 * s).astype(o_ref.dtype)

    return pl.pallas_call(
        body, name="glu_bwd", grid=(L // TL,),
        in_specs=[_rt(DFF), _rt(DFF, 0), _rt(DFF, 1)], out_specs=_rt(2 * DFF),
        out_shape=jax.ShapeDtypeStruct((L, 2 * DFF), _MXU),
        compiler_params=_params(("parallel",)),
    )(dff, f13, f13)


def _res2(r1, o2, tgt, modx, g1, b1, g2, b2):
    L = r1.shape[0]

    def body(r1_ref, o2_ref, t_ref, mx_ref, g1_ref, b1_ref, g2_ref, b2_ref,
             dr2_ref, do2_ref, st_ref, loss_ref):
        @pl.when(pl.program_id(0) == 0)
        def _():
            st_ref[...] = jnp.zeros_like(st_ref)
            loss_ref[...] = jnp.zeros_like(loss_ref)

        xh1, _ = _ln(r1_ref[...])
        x1 = xh1 * g1_ref[...] + b1_ref[...]
        o2 = o2_ref[...]
        g2x = mx_ref[5:6, :]
        xh2, rstd2 = _ln(ALPHA * x1 + g2x * o2)
        err = xh2 * g2_ref[...] + b2_ref[...] - t_ref[...]
        per_tok = jnp.mean(err * err, axis=-1, keepdims=True)
        loss_ref[...] += 0.5 * jnp.sum(per_tok, axis=0, keepdims=True)
        dy = err * (1.0 / D)
        st_ref[0:1, :] += _colsum(dy * xh2)
        st_ref[1:2, :] += _colsum(dy)
        dr2 = _ln_bwd(dy * g2_ref[...], xh2, rstd2)
        st_ref[2:3, :] += _colsum(dr2 * o2)
        dr2_ref[...] = dr2
        do2_ref[...] = (g2x * dr2).astype(do2_ref.dtype)

    return pl.pallas_call(
        body, name="res2", grid=(L // TL,),
        in_specs=[_rt(D), _rt(D), _rt(D), _cst((8, D))] + [_cst((1, D))] * 4,
        out_specs=[_rt(D), _rt(D), _cst((8, D)), _cst((8, 128))],
        out_shape=[jax.ShapeDtypeStruct((L, D), F32), jax.ShapeDtypeStruct((L, D), _MXU),
                   jax.ShapeDtypeStruct((8, D), F32), jax.ShapeDtypeStruct((8, 128), F32)],
        compiler_params=_params(("arbitrary",)),
    )(r1, o2, tgt, modx, g1, b1, g2, b2)


def _res1_bwd(dr2, dh2, r1, out, modx, g1, b1):
    L = r1.shape[0]

    def body(dr2_ref, dh2_ref, r1_ref, o_ref, mx_ref, g_ref, b_ref, dr1_ref, do_ref, st_ref):
        @pl.when(pl.program_id(0) == 0)
        def _():
            st_ref[...] = jnp.zeros_like(st_ref)

        xh1, rstd1 = _ln(r1_ref[...])
        x1 = xh1 * g_ref[...] + b_ref[...]
        dh2 = dh2_ref[...]
        dx1 = ALPHA * dr2_ref[...] + dh2 * (1.0 + mx_ref[4:5, :])
        st_ref[0:1, :] += _colsum(dh2 * x1)
        st_ref[1:2, :] += _colsum(dh2)
        st_ref[2:3, :] += _colsum(dx1 * xh1)
        st_ref[3:4, :] += _colsum(dx1)
        dr1 = _ln_bwd(dx1 * g_ref[...], xh1, rstd1)
        st_ref[4:5, :] += _colsum(dr1 * o_ref[...])
        dr1_ref[...] = dr1
        do_ref[...] = (mx_ref[2:3, :] * dr1).astype(do_ref.dtype)

    return pl.pallas_call(
        body, name="res1_bwd", grid=(L // TL,),
        in_specs=[_rt(D), _rt(D), _rt(D), _rt(D), _cst((8, D)), _cst((1, D)), _cst((1, D))],
        out_specs=[_rt(D), _rt(D), _cst((8, D))],
        out_shape=[jax.ShapeDtypeStruct((L, D), F32), jax.ShapeDtypeStruct((L, D), _MXU),
                   jax.ShapeDtypeStruct((8, D), F32)],
        compiler_params=_params(("arbitrary",)),
    )(dr2, dh2, r1, out, modx, g1, b1)


def _conv_bwd(dxf, dxb, p, conv_w8, conv_b, dp):
    RT = p.shape[0]
    chunks = _seq_chunks(RT - TL)

    def body(df_ref, db_ref, p_ref, w_ref, b_ref, dp_any, o_ref, dw_ref, dbias_ref, dpre_s):
        del dp_any
        w = w_ref[...]
        bias = b_ref[...]
        srow = lax.broadcasted_iota(jnp.int32, (8, 128), 0)
        dwacc = jnp.zeros((8, 128), F32)
        dbacc = jnp.zeros((1, 128), F32)
        for r0, first, last in chunks:
            taps = _conv_taps(p_ref, r0, first, last)
            pre = bias + sum(w[k:k + 1, :] * taps[k] for k in range(5))
            s = _sig(pre)
            dpre = (df_ref[pl.ds(r0, TL), :] + db_ref[pl.ds(r0, TL), :]) * (s * (1.0 + pre * (1.0 - s)))
            dpre_s[pl.ds(r0, TL), :] = dpre
            dbacc = dbacc + _colsum(dpre)
            for k in range(5):
                dwacc = dwacc + jnp.where(srow == k, _colsum(dpre * taps[k]), 0.0)
        for r0, first, last in chunks:
            taps = _conv_taps(dpre_s, r0, first, last)
            dx = sum(w[k:k + 1, :] * taps[4 - k] for k in range(5))
            o_ref[pl.ds(r0, TL), :] = dx.astype(o_ref.dtype)
        dw_ref[...] = dwacc
        dbias_ref[...] = jnp.broadcast_to(dbacc, (8, 128))

    cspec = pl.BlockSpec((RT, 128), lambda j: (0, j))
    wspec = pl.BlockSpec((8, 128), lambda j: (0, j))
    return pl.pallas_call(
        body, name="conv_bwd", grid=(12,),
        in_specs=[cspec, cspec, pl.BlockSpec((RT, 128), lambda j: (0, _xbc_colblk(j))),
                  wspec, pl.BlockSpec((1, 128), lambda j: (0, j)), pl.BlockSpec(memory_space=pl.ANY)],
        out_specs=[pl.BlockSpec((RT, 128), lambda j: (0, _xbc_colblk(j))), wspec, wspec],
        out_shape=[jax.ShapeDtypeStruct(dp.shape, dp.dtype), jax.ShapeDtypeStruct((8, 1536), F32),
                   jax.ShapeDtypeStruct((8, 1536), F32)],
        scratch_shapes=[pltpu.VMEM((RT, 128), F32)],
        input_output_aliases={5: 0},
        compiler_params=_params(("parallel",)),
    )(dxf, dxb, p, conv_w8, conv_b, dp)


def _dt_bwd(ddf, ddb, dp):
    RT = ddf.shape[0]

    def body(f_ref, b_ref, dp_any, o_ref, st_ref):
        del dp_any

        @pl.when(pl.program_id(0) == 0)
        def _():
            st_ref[...] = jnp.zeros_like(st_ref)

        s = f_ref[...] + b_ref[...]
        o_ref[...] = s.astype(o_ref.dtype)
        st_ref[0:1, :] += _colsum(s)

    return pl.pallas_call(
        body, name="dt_bwd", grid=(RT // TL,),
        in_specs=[_rt(128), _rt(128), pl.BlockSpec(memory_space=pl.ANY)],
        out_specs=[_rt(128, ODT // 128), _cst((8, 128))],
        out_shape=[jax.ShapeDtypeStruct(dp.shape, dp.dtype), jax.ShapeDtypeStruct((8, 128), F32)],
        input_output_aliases={2: 0},
        compiler_params=_params(("arbitrary",)),
    )(ddf, ddb, dp)


def _ln0_bwd(dh1, dr1, x, ctx, g, b, modx, modc):
    L = x.shape[0]
    nt = L // TL

    def body(dh_ref, dr1_ref, x_ref, c_ref, g_ref, b_ref, mx_ref, mc_ref, gx_ref, st_ref):
        i = pl.program_id(0)
        isc = i == nt

        @pl.when(i == 0)
        def _():
            st_ref[...] = jnp.zeros_like(st_ref)

        xin = jnp.where(isc, c_ref[...], x_ref[...])
        xhat, rstd = _ln(xin)
        xn = xhat * g_ref[...] + b_ref[...]
        sc = jnp.where(isc, mc_ref[1:2, :], mx_ref[1:2, :])
        dh = dh_ref[...]
        lat = jnp.where(isc, 0.0, 1.0)
        dxn = dh * (1.0 + sc) + (lat * ALPHA) * dr1_ref[...]
        tsh = _colsum(dh)
        tsc = _colsum(dh * xn)
        st_ref[0:1, :] += lat * tsh
        st_ref[1:2, :] += lat * tsc
        st_ref[2:3, :] += (1.0 - lat) * tsh
        st_ref[3:4, :] += (1.0 - lat) * tsc
        st_ref[4:5, :] += _colsum(dxn * xhat)
        st_ref[5:6, :] += _colsum(dxn)

        @pl.when(i < nt)
        def _():
            gx_ref[...] = _ln_bwd(dxn * g_ref[...], xhat, rstd)

    return pl.pallas_call(
        body, name="ln0_bwd", grid=(nt + 1,),
        in_specs=[_rt(D), _rtc(D, nt), _rtc(D, nt), _cst((TL, D)), _cst((1, D)), _cst((1, D)),
                  _cst((8, D)), _cst((8, D))],
        out_specs=[_rtc(D, nt), _cst((8, D))],
        out_shape=[jax.ShapeDtypeStruct((L, D), F32), jax.ShapeDtypeStruct((8, D), F32)],
        compiler_params=_params(("arbitrary",)),
    )(dh1, dr1, x, ctx, g, b, modx, modc)


def _perm_cols(w):
    pad = jnp.zeros((w.shape[0], NPJ - NNAT), w.dtype)
    return jnp.concatenate([w[:, 0:1024], w[:, 2592:3616], w[:, 3616:4640], w[:, 1024:2048],
                            w[:, 4640:6688], w[:, 2048:2304], w[:, 2304:2560], w[:, 2560:2592], pad],
                           axis=1)


def _unperm_cols(g):
    return jnp.concatenate([g[:, OZ:OZ + D], g[:, OXS:OXS + D], g[:, OB:OB + 256], g[:, OC:OC + 256],
                            g[:, ODT:ODT + 32], g[:, OU:OU + D], g[:, OV:OV + D], g[:, OG:OG + 2 * D]],
                           axis=1)


def _padded(n, row_align):
    unit = row_align * D
    return -(-n // unit) * unit if row_align else n


def _slab(arrs, rows, row_align=0):
    parts = []
    for a in arrs:
        f = a.reshape(-1)
        parts.append(jnp.pad(f, (0, _padded(f.shape[0], row_align) - f.shape[0])))
    flat = jnp.concatenate(parts)
    flat = jnp.pad(flat, (0, rows * D - flat.shape[0]))
    return flat.reshape(rows, D)


def _unslab(slab, shapes, row_align=0):
    flat = slab.reshape(-1)
    out, off = [], 0
    for shp in shapes:
        n = 1
        for s in shp:
            n *= s
        out.append(flat[off:off + n].reshape(shp))
        off += _padded(n, row_align)
    return out


def _row(v):
    return v.reshape(1, -1)


def _pad_rows(a, rows):
    return jnp.pad(a, ((0, rows - a.shape[0]), (0, 0)))


BIG = ["w_in", "w_ssd_proj", "w_gm_proj", "w_out", "w_ff1", "w_ff3", "w_ff2"]
BIG_ROWS = 2304
BIG_ALIGN = 16
REPL = ["c_ctx", "ln0_g", "ln0_b", "b_ada", "conv_b", "dt_bias", "a_log", "d_skip", "ssd_norm_g",
        "gm_norm_g", "gm_norm_b", "w_spatial", "b_spatial", "b_gate", "ln1_g", "ln1_b", "ln2_g", "ln2_b"]
SMALL_ROWS = 160
WEIGHTS = ["c_ctx", "ln0_g", "ln0_b", "w_ada", "b_ada", "w_in", "conv_w", "conv_b", "dt_bias", "a_log",
           "d_skip", "ssd_norm_g", "gm_norm_g", "gm_norm_b", "w_spatial", "b_spatial", "b_gate",
           "w_ssd_proj", "w_gm_proj", "w_out", "ln1_g", "ln1_b", "w_ff1", "w_ff3", "w_ff2", "ln2_g", "ln2_b"]


def kernel(x, c, ctx, c_ctx, ln0_g, ln0_b, w_ada, b_ada, w_in, conv_w, conv_b, dt_bias, a_log, d_skip, ssd_norm_g, gm_norm_g, gm_norm_b, w_spatial, b_spatial, b_gate, w_ssd_proj, w_gm_proj, w_out, ln1_g, ln1_b, w_ff1, w_ff3, w_ff2, ln2_g, ln2_b, loss_target, m_c_ctx, m_ln0_g, m_ln0_b, m_w_ada, m_b_ada, m_w_in, m_conv_w, m_conv_b, m_dt_bias, m_a_log, m_d_skip, m_ssd_norm_g, m_gm_norm_g, m_gm_norm_b, m_w_spatial, m_b_spatial, m_b_gate, m_w_ssd_proj, m_w_gm_proj, m_w_out, m_ln1_g, m_ln1_b, m_w_ff1, m_w_ff3, m_w_ff2, m_ln2_g, m_ln2_b, v_c_ctx, v_ln0_g, v_ln0_b, v_w_ada, v_b_ada, v_w_in, v_conv_w, v_conv_b, v_dt_bias, v_a_log, v_d_skip, v_ssd_norm_g, v_gm_norm_g, v_gm_norm_b, v_w_spatial, v_b_spatial, v_b_gate, v_w_ssd_proj, v_w_gm_proj, v_w_out, v_ln1_g, v_ln1_b, v_w_ff1, v_w_ff3, v_w_ff2, v_ln2_g, v_ln2_b):
    W = dict(c_ctx=c_ctx, ln0_g=ln0_g, ln0_b=ln0_b, w_ada=w_ada, b_ada=b_ada, w_in=w_in, conv_w=conv_w,
             conv_b=conv_b, dt_bias=dt_bias, a_log=a_log, d_skip=d_skip, ssd_norm_g=ssd_norm_g,
             gm_norm_g=gm_norm_g, gm_norm_b=gm_norm_b, w_spatial=w_spatial, b_spatial=b_spatial,
             b_gate=b_gate, w_ssd_proj=w_ssd_proj, w_gm_proj=w_gm_proj, w_out=w_out, ln1_g=ln1_g,
             ln1_b=ln1_b, w_ff1=w_ff1, w_ff3=w_ff3, w_ff2=w_ff2, ln2_g=ln2_g, ln2_b=ln2_b)
    M = dict(c_ctx=m_c_ctx, ln0_g=m_ln0_g, ln0_b=m_ln0_b, w_ada=m_w_ada, b_ada=m_b_ada, w_in=m_w_in,
             conv_w=m_conv_w, conv_b=m_conv_b, dt_bias=m_dt_bias, a_log=m_a_log, d_skip=m_d_skip,
             ssd_norm_g=m_ssd_norm_g, gm_norm_g=m_gm_norm_g, gm_norm_b=m_gm_norm_b,
             w_spatial=m_w_spatial, b_spatial=m_b_spatial, b_gate=m_b_gate, w_ssd_proj=m_w_ssd_proj,
             w_gm_proj=m_w_gm_proj, w_out=m_w_out, ln1_g=m_ln1_g, ln1_b=m_ln1_b, w_ff1=m_w_ff1,
             w_ff3=m_w_ff3, w_ff2=m_w_ff2, ln2_g=m_ln2_g, ln2_b=m_ln2_b)
    V = dict(c_ctx=v_c_ctx, ln0_g=v_ln0_g, ln0_b=v_ln0_b, w_ada=v_w_ada, b_ada=v_b_ada, w_in=v_w_in,
             conv_w=v_conv_w, conv_b=v_conv_b, dt_bias=v_dt_bias, a_log=v_a_log, d_skip=v_d_skip,
             ssd_norm_g=v_ssd_norm_g, gm_norm_g=v_gm_norm_g, gm_norm_b=v_gm_norm_b,
             w_spatial=v_w_spatial, b_spatial=v_b_spatial, b_gate=v_b_gate, w_ssd_proj=v_w_ssd_proj,
             w_gm_proj=v_w_gm_proj, w_out=v_w_out, ln1_g=v_ln1_g, ln1_b=v_ln1_b, w_ff1=v_w_ff1,
             w_ff3=v_w_ff3, w_ff2=v_w_ff2, ln2_g=v_ln2_g, ln2_b=v_ln2_b)

    me = 4 * lax.axis_index("x") + 2 * lax.axis_index("y") + lax.axis_index("c")
    xl, cx, tgt = x[0], ctx[0], loss_target[0]
    L = xl.shape[0]
    assert cx.shape[0] == TL and L % TL == 0
    ada_n = w_ada.shape[2]
    cw_n = conv_w.shape[2]

    small1 = _pad_rows(jnp.concatenate([c, _slab([conv_w[0]], 1)], axis=0), 8)
    g1 = _all_gather(small1, "ag_small")
    c_all = g1[:, 0, :]
    conv_w_full = g1[:, 1, :5 * cw_n].reshape(NDEV, 5, cw_n).transpose(1, 0, 2).reshape(5, NDEV * cw_n)
    big_local = _slab([W[n][0] for n in BIG], BIG_ROWS, BIG_ALIGN).astype(_MXU)
    gw = _all_gather(big_local, "ag_weights")

    def rows_of(n):
        return W[n][0].size // D

    offs, o = {}, 0
    for n in BIG:
        offs[n] = o
        o += _padded(rows_of(n) * D, BIG_ALIGN) // D

    def col_sharded(n):
        k, nc = W[n].shape[1], W[n].shape[2]
        blk = gw[:, offs[n]:offs[n] + rows_of(n), :].reshape(NDEV, k, nc)
        return blk.transpose(1, 0, 2).reshape(k, NDEV * nc)

    def row_sharded(n):
        return gw[:, offs[n]:offs[n] + rows_of(n), :].reshape(-1, D)

    w_in_p = _perm_cols(col_sharded("w_in"))
    w_ssd_f, w_gm_f, w_out_f = row_sharded("w_ssd_proj"), row_sharded("w_gm_proj"), row_sharded("w_out")
    w13 = jnp.concatenate([col_sharded("w_ff1"), col_sharded("w_ff3")], axis=1)
    w_ff2_f = row_sharded("w_ff2")

    c16 = _pad_rows(jnp.concatenate([c_all, _row(c_ctx)], axis=0), 16)
    b_ada_sh = lax.dynamic_slice(b_ada, (0, ada_n * me), (1, ada_n))
    modp = _ada_fwd(c16, w_ada[0], b_ada_sh)
    mod16 = _all_gather(modp, "ag_mod").transpose(1, 0, 2).reshape(16, NDEV * ada_n)
    modx = _pad_rows(lax.dynamic_slice(mod16, (me, 0), (1, 6 * D)).reshape(6, D), 8)
    modc = _pad_rows(mod16[8].reshape(6, D), 8)

    g0, b0 = _row(ln0_g), _row(ln0_b)
    xn, h1 = _ln0_fwd(xl, cx, g0, b0, modx, modc)
    p = _mm(h1, w_in_p, "nn", F32, "mm_p")
    conv_w8 = _pad_rows(conv_w_full, 8)
    xbc = _conv_fwd(p, conv_w8, conv_b)
    prm = _pad_rows(jnp.pad(jnp.stack([dt_bias.reshape(32), a_log.reshape(32)]), ((0, 0), (0, 96))), 8)
    yf, yb, hpf, hpb = _ssd_fwd(xbc, p, prm)
    dsk = _row(jnp.repeat(d_skip[0, 0] + d_skip[0, 1], HP))
    ws_m = w_spatial[0].astype(_MXU)
    bsT = jnp.pad(b_spatial[0].T, ((0, 0), (0, 120)))
    mixp = (dsk, ssd_norm_g, gm_norm_g, gm_norm_b, ws_m, bsT)
    yssd, ygm = _mix_fwd(yf, yb, p, xbc, *mixp)
    a1 = _mm(yssd, w_ssd_f, "nn", F32, "mm_a1")
    a2 = _mm(ygm, w_gm_f, "nn", F32, "mm_a2")
    merged = _gate_fwd(a1, a2, p, b_gate)
    out = _mm(merged, w_out_f, "nn", F32, "mm_out")
    r1, h2 = _res1_fwd(xn, out, modx, ln1_g, ln1_b)
    f13 = _mm(h2, w13, "nn", F32, "mm_f13")
    ff = _glu_fwd(f13)
    o2 = _mm(ff, w_ff2_f, "nn", F32, "mm_o2")

    dr2, do2, st2, loss_slab = _res2(r1, o2, tgt, modx, ln1_g, ln1_b, ln2_g, ln2_b)
    loss = lax.psum(loss_slab[0, 0], ("x", "y", "c"))
    dff = _mm(do2, w_ff2_f, "nt", F32, "mm_dff")
    df13 = _glu_bwd(dff, f13)
    dh2 = _mm(df13, w13, "nt", F32, "mm_dh2")
    dw_ff2 = _mm(ff, do2, "tn", F32, "mm_dw_ff2")
    dw13 = _mm(h2, df13, "tn", F32, "mm_dw13")
    dr1, dout, st1 = _res1_bwd(dr2, dh2, r1, out, modx, ln1_g, ln1_b)
    dmg = _mm(dout, w_out_f, "nt", F32, "mm_dmerged")
    dw_out = _mm(merged, dout, "tn", F32, "mm_dw_out")
    dp = jnp.zeros((L + TL, NPJ), _MXU)
    dp, da1, da2, stg = _gate_bwd(dmg, a1, a2, p, b_gate, dp)
    dys = _mm(da1, w_ssd_f, "nt", F32, "mm_dyssd")
    dym = _mm(da2, w_gm_f, "nt", F32, "mm_dygm")
    dw_ssd = _mm(yssd, da1, "tn", F32, "mm_dw_ssd")
    dw_gm = _mm(ygm, da2, "tn", F32, "mm_dw_gm")
    dp, dyd, stm, dws, dbsT = _mix_bwd(dys, dym, yf, yb, p, xbc, dp, *mixp)
    dxf, dxb, ddf, ddb, sts = _ssd_bwd(xbc, p, prm, dsk, dyd, hpf, hpb)
    dp, dcw, dcb = _conv_bwd(dxf, dxb, p, conv_w8, conv_b, dp)
    dp, std = _dt_bwd(ddf, ddb, dp)
    dh1 = _mm(dp, w_in_p, "nt", F32, "mm_dh1")
    dw_in_p = _mm(h1, dp, "tn", F32, "mm_dw_in")
    grad_x, st0 = _ln0_bwd(dh1, dr1, xl, cx, g0, b0, modx, modc)

    zero = jnp.zeros((D,), F32)
    dmod = jnp.stack([jnp.concatenate([st0[0], st0[1], st1[4], st1[1], st1[0], st2[2]]),
                      jnp.concatenate([st0[2], st0[3], zero, zero, zero, zero])])
    g16 = _all_gather(_pad_rows(dmod, 8), "ag_dmod")[:, 0:2, :].reshape(16, 6 * D)
    g16_sh = lax.dynamic_slice(g16, (0, ada_n * me), (16, ada_n))
    c16b = jnp.stack([c_all, jnp.broadcast_to(_row(c_ctx), (NDEV, D))], axis=1).reshape(16, D)
    dw_ada, db_ada8, dcc8 = _ada_bwd(c16b, g16, g16_sh, w_ada[0])

    part = dict(
        c_ctx=dcc8[0], ln0_g=st0[4], ln0_b=st0[5], conv_w=dcw[0:5], conv_b=dcb[0],
        dt_bias=std[0, 0:32], a_log=sts[0, 0:32], d_skip=jnp.tile(sts[1, 0:16], 2),
        ssd_norm_g=stm[0], gm_norm_g=stm[1], gm_norm_b=stm[2], w_spatial=dws,
        b_spatial=dbsT[:, 0:8].T, b_gate=stg[0], ln1_g=st1[2], ln1_b=st1[3], ln2_g=st2[0], ln2_b=st2[1])
    pnames = list(part)
    psum8 = _sum8(_all_gather(_slab([part[n] for n in pnames], SMALL_ROWS), "ag_smallgrads"), "sum_smallgrads")
    small = dict(zip(pnames, _unslab(psum8, [part[n].shape for n in pnames])))
    grads = {n: small[n].reshape(W[n].shape) for n in pnames if n != "conv_w"}
    grads["conv_w"] = lax.dynamic_slice(small["conv_w"], (0, cw_n * me), (5, cw_n)).reshape(conv_w.shape)
    grads["b_ada"] = db_ada8[0:1]
    grads["w_ada"] = dw_ada.reshape(w_ada.shape)

    def aligned(blocks):
        return jnp.pad(blocks, ((0, 0), (0, -blocks.shape[1] % BIG_ALIGN), (0, 0)))

    def to_owner_cols(gfull):
        k, n = gfull.shape
        return aligned(gfull.reshape(k, NDEV, n // NDEV).transpose(1, 0, 2).reshape(NDEV, -1, D))

    def to_owner_rows(gfull):
        return aligned(gfull.reshape(NDEV, -1, D))

    gbig = jnp.concatenate(
        [to_owner_cols(_unperm_cols(dw_in_p)), to_owner_rows(dw_ssd), to_owner_rows(dw_gm),
         to_owner_rows(dw_out), to_owner_cols(dw13[:, :DFF]), to_owner_cols(dw13[:, DFF:]),
         to_owner_rows(dw_ff2)], axis=1)
    gbig = jnp.pad(gbig, ((0, 0), (0, BIG_ROWS - gbig.shape[1]), (0, 0))).astype(_MXU)
    gsum = _sum8(_owner_exchange(gbig, "xchg_grads"), "sum_grads")
    for n, gshard in zip(BIG, _unslab(gsum, [W[n].shape for n in BIG], BIG_ALIGN)):
        grads[n] = gshard

    delta, new_m, new_v = {}, {}, {}

    def adam_group(names, rows, tag, align=0):
        shapes = [W[n].shape for n in names]
        outs = _adamw(*[_slab([src[n] for n in names], rows, align) for src in (grads, W, M, V)], tag)
        for res, slab in zip((delta, new_m, new_v), outs):
            for n, a in zip(names, _unslab(slab, shapes, align)):
                res[n] = a

    adam_group(BIG, BIG_ROWS, "adamw_big", BIG_ALIGN)
    adam_group(["w_ada", "conv_w"], ada_n + 256, "adamw_shard")
    adam_group(REPL, SMALL_ROWS, "adamw_repl")

    return (loss, grad_x[None], *[grads[n] for n in WEIGHTS], *[delta[n] for n in WEIGHTS],
            *[new_m[n] for n in WEIGHTS], *[new_v[n] for n in WEIGHTS])
```

```python
import functools

import jax
import jax.numpy as jnp
from jax import lax
from jax.experimental import pallas as pl
from jax.experimental.pallas import tpu as pltpu

_MXU = jnp.bfloat16
F32 = jnp.float32
D = 1024
TL = 256
Q = 128
NH, HP, NS, HPG = 16, 64, 128, 8
DFF = 2816
ALPHA = 2.0 ** 0.25
EPS = 1e-5
OZ, OU, OV, OXS, OG, OB, OC, ODT, NPJ = 0, 1024, 2048, 3072, 4096, 6144, 6400, 6656, 6912
NNAT = 6688
NDEV = 8
ADAM_LR, ADAM_B1, ADAM_B2, ADAM_EPS, ADAM_WD, ADAM_STEP = 1e-3, 0.9, 0.999, 1e-8, 0.01, 10
VMEM_LIMIT = 48 * 1024 * 1024

NN = ((1,), (0,))
NT = ((1,), (1,))
TN = ((0,), (0,))
MESH = pl.DeviceIdType.MESH


def _dot(a, b, dims):
    return lax.dot_general(a.astype(_MXU), b.astype(_MXU), (dims, ((), ())),
                           preferred_element_type=F32)


def _tile(n, cands):
    for c in cands:
        if n % c == 0:
            return c
    return n


def _divisor_tile(n, cap, mult):
    best = n
    for t in range(mult, min(n, cap) + 1, mult):
        if n % t == 0:
            best = t
    return best


def _params(sem):
    return pltpu.CompilerParams(dimension_semantics=sem, vmem_limit_bytes=VMEM_LIMIT)


def _cst(shape):
    nd = len(shape)
    return pl.BlockSpec(shape, lambda *_: (0,) * nd)


def _rt(w, cb=0, rows=TL):
    return pl.BlockSpec((rows, w), lambda i: (i, cb))


def _rtc(w, nt, cb=0):
    return pl.BlockSpec((TL, w), lambda i: (jnp.minimum(i, nt - 1), cb))


def _sig(x):
    return jax.nn.sigmoid(x)


def _softplus(x):
    return jnp.maximum(x, 0.0) + jnp.log1p(jnp.exp(-jnp.abs(x)))


_G0, _G1 = 0.7978845608028654, 0.044715


def _gelu(x):
    t = jnp.tanh(_G0 * (x + _G1 * x * x * x))
    return 0.5 * x * (1.0 + t), t


def _gelu_grad(x, t):
    return 0.5 * (1.0 + t) + 0.5 * x * (1.0 - t * t) * _G0 * (1.0 + 3.0 * _G1 * x * x)


def _ln(r):
    mu = jnp.mean(r, axis=-1, keepdims=True)
    xc = r - mu
    var = jnp.mean(xc * xc, axis=-1, keepdims=True)
    rstd = lax.rsqrt(var + EPS)
    return xc * rstd, rstd


def _ln_bwd(dyh, xhat, rstd):
    return rstd * (dyh - jnp.mean(dyh, axis=-1, keepdims=True)
                   - xhat * jnp.mean(dyh * xhat, axis=-1, keepdims=True))


def _colsum(v):
    return jnp.sum(v, axis=0, keepdims=True)


def _sum11(v):
    return jnp.sum(jnp.sum(v, axis=1, keepdims=True), axis=0, keepdims=True)


def _cumsum_rows(a, rev):
    n = a.shape[0]
    row = lax.broadcasted_iota(jnp.int32, a.shape, 0)
    s = 1
    while s < n:
        if rev:
            a = a + jnp.where(row < n - s, pltpu.roll(a, n - s, 0), 0.0)
        else:
            a = a + jnp.where(row >= s, pltpu.roll(a, s, 0), 0.0)
        s *= 2
    return a


def _mm(a, b, mode, out_dtype, name):
    if mode == "tn":
        K, M = a.shape
    else:
        M, K = a.shape
    N = b.shape[0] if mode == "nt" else b.shape[1]
    tm = _divisor_tile(M, 1408, 128) if mode == "tn" else _divisor_tile(M, 1088, 16)
    tn = _divisor_tile(N, 1408, 128)
    tk = _divisor_tile(K, 2304, 128)
    nk = K // tk
    dims = {"nn": NN, "nt": NT, "tn": TN}[mode]
    assert out_dtype == F32

    def body(a_ref, b_ref, o_ref):
        prod = _dot(a_ref[...], b_ref[...], dims)
        if nk == 1:
            o_ref[...] = prod
            return
        k = pl.program_id(2)

        @pl.when(k == 0)
        def _():
            o_ref[...] = prod

        @pl.when(k > 0)
        def _():
            o_ref[...] += prod

    if mode == "tn":
        a_spec = pl.BlockSpec((tk, tm), lambda i, j, k: (k, i))
    else:
        a_spec = pl.BlockSpec((tm, tk), lambda i, j, k: (i, k))
    if mode == "nt":
        b_spec = pl.BlockSpec((tn, tk), lambda i, j, k: (j, k))
    else:
        b_spec = pl.BlockSpec((tk, tn), lambda i, j, k: (k, j))
    return pl.pallas_call(
        body, name=name, grid=(M // tm, N // tn, nk),
        in_specs=[a_spec, b_spec],
        out_specs=pl.BlockSpec((tm, tn), lambda i, j, k: (i, j)),
        out_shape=jax.ShapeDtypeStruct((M, N), out_dtype),
        compiler_params=_params(("parallel", "parallel", "arbitrary")),
    )(a, b)


def _all_gather(x, name):
    def body(x_ref, out_ref, send_sems, recv_sems, local_sem):
        mx, my, mc = lax.axis_index("x"), lax.axis_index("y"), lax.axis_index("c")
        me, sibling = (mx, my, mc), (mx, my, 1 - mc)
        chips = [(1 - mx, my), (mx, 1 - my), (1 - mx, 1 - my)]

        def slot(px, py, pc):
            return out_ref.at[4 * px + 2 * py + pc]

        def copy(k, block, to, src=None):
            return pltpu.make_async_remote_copy(
                src_ref=slot(*block) if src is None else src, dst_ref=slot(*block),
                send_sem=send_sems.at[k], recv_sem=recv_sems.at[k],
                device_id=to, device_id_type=MESH)

        mine = pltpu.make_async_copy(x_ref, slot(*me), local_sem)
        mine.start()
        first = [copy(0, me, sibling, src=x_ref)]
        first += [copy(1 + j, me, (*chip, mc), src=x_ref) for j, chip in enumerate(chips)]
        for cp in first:
            cp.start()
        passed = [copy(4 + j, (*chip, mc), sibling) for j, chip in enumerate(chips)]
        for j, chip in enumerate(chips):
            copy(1 + j, (*chip, mc), me).wait_recv()
            passed[j].start()
        copy(0, sibling, me).wait_recv()
        for j, chip in enumerate(chips):
            copy(4 + j, (*chip, 1 - mc), me).wait_recv()
        for cp in first + passed:
            cp.wait_send()
        mine.wait()

    return pl.pallas_call(
        body, name=name,
        out_shape=jax.ShapeDtypeStruct((NDEV,) + x.shape, x.dtype),
        in_specs=[pl.BlockSpec(memory_space=pl.ANY)],
        out_specs=pl.BlockSpec(memory_space=pl.ANY),
        scratch_shapes=[pltpu.SemaphoreType.DMA((7,)), pltpu.SemaphoreType.DMA((7,)),
                        pltpu.SemaphoreType.DMA],
    )(x)


def _owner_exchange(g, name):
    def body(g_ref, out_ref, send_sems, recv_sems, local_sem):
        mx, my, mc = lax.axis_index("x"), lax.axis_index("y"), lax.axis_index("c")
        local = pltpu.make_async_copy(g_ref.at[4 * mx + 2 * my + mc], out_ref.at[0], local_sem)
        local.start()
        copies = []
        for f in range(1, NDEV):
            px = 1 - mx if (f >> 2) & 1 else mx
            py = 1 - my if (f >> 1) & 1 else my
            pc = 1 - mc if f & 1 else mc
            cp = pltpu.make_async_remote_copy(
                src_ref=g_ref.at[4 * px + 2 * py + pc], dst_ref=out_ref.at[f],
                send_sem=send_sems.at[f - 1], recv_sem=recv_sems.at[f - 1],
                device_id=(px, py, pc), device_id_type=MESH)
            cp.start()
            copies.append(cp)
        for cp in copies:
            cp.wait_recv()
        for cp in copies:
            cp.wait_send()
        local.wait()

    return pl.pallas_call(
        body, name=name,
        out_shape=jax.ShapeDtypeStruct(g.shape, g.dtype),
        in_specs=[pl.BlockSpec(memory_space=pl.ANY)],
        out_specs=pl.BlockSpec(memory_space=pl.ANY),
        scratch_shapes=[pltpu.SemaphoreType.DMA((7,)), pltpu.SemaphoreType.DMA((7,)),
                        pltpu.SemaphoreType.DMA],
    )(g)


def _sum8(r, name):
    _, R, C = r.shape
    tr = _tile(R, (256, 160, 128, 64, 32, 16, 8))

    def body(r_ref, o_ref):
        acc = r_ref[0].astype(F32)
        for k in range(1, NDEV):
            acc = acc + r_ref[k].astype(F32)
        o_ref[...] = acc

    return pl.pallas_call(
        body, name=name, grid=(R // tr,),
        in_specs=[pl.BlockSpec((NDEV, tr, C), lambda i: (0, i, 0))],
        out_specs=pl.BlockSpec((tr, C), lambda i: (i, 0)),
        out_shape=jax.ShapeDtypeStruct((R, C), F32),
        compiler_params=_params(("parallel",)),
    )(r)


def _adamw(g, w, m, v, name):
    R, C = g.shape
    tr = _tile(R, (256, 160, 128, 64, 32, 16, 8))
    bc1 = 1.0 - ADAM_B1 ** ADAM_STEP
    bc2 = 1.0 - ADAM_B2 ** ADAM_STEP

    def body(g_ref, w_ref, m_ref, v_ref, d_ref, mo_ref, vo_ref):
        gg = g_ref[...]
        mn = ADAM_B1 * m_ref[...] + (1.0 - ADAM_B1) * gg
        vn = ADAM_B2 * v_ref[...] + (1.0 - ADAM_B2) * (gg * gg)
        mh = mn / bc1
        vh = vn / bc2
        d_ref[...] = -ADAM_LR * (mh / (jnp.sqrt(vh) + ADAM_EPS) + ADAM_WD * w_ref[...])
        mo_ref[...] = mn
        vo_ref[...] = vn

    spec = pl.BlockSpec((tr, C), lambda i: (i, 0))
    sh = jax.ShapeDtypeStruct((R, C), F32)
    return pl.pallas_call(
        body, name=name, grid=(R // tr,), in_specs=[spec] * 4, out_specs=[spec] * 3,
        out_shape=[sh] * 3, compiler_params=_params(("parallel",)),
    )(g, w, m, v)


def _ada_fwd(c16, w_sh, b_sh):
    def body(c_ref, w_ref, b_ref, o_ref):
        c = c_ref[...]
        o_ref[...] = _dot(c * _sig(c), w_ref[...], NN) + b_ref[...]

    return pl.pallas_call(
        body, name="ada_fwd", out_shape=jax.ShapeDtypeStruct((16, w_sh.shape[1]), F32),
        compiler_params=pltpu.CompilerParams(vmem_limit_bytes=VMEM_LIMIT),
    )(c16, w_sh, b_sh)


def _ada_bwd(c16, g16, g16_sh, w_sh):
    ncol = w_sh.shape[1]

    def body(c_ref, g_ref, gs_ref, w_ref, dw_ref, db_ref, dc_ref):
        c = c_ref[...]
        s = _sig(c)
        gs = gs_ref[...]
        dw_ref[...] = _dot(c * s, gs, TN)
        db_ref[...] = jnp.broadcast_to(_colsum(g_ref[...]), db_ref.shape)
        odd = lax.broadcasted_iota(jnp.int32, gs.shape, 0) % 2 == 1
        gc = _colsum(jnp.where(odd, gs, 0.0))
        ds = _dot(jnp.broadcast_to(gc, (8, ncol)), w_ref[...], NT)
        c1 = c[1:2, :]
        s1 = s[1:2, :]
        dc_ref[...] = ds * (s1 * (1.0 + c1 * (1.0 - s1)))

    return pl.pallas_call(
        body, name="ada_bwd",
        out_shape=[jax.ShapeDtypeStruct(w_sh.shape, F32),
                   jax.ShapeDtypeStruct((8, g16.shape[1]), F32),
                   jax.ShapeDtypeStruct((8, D), F32)],
        compiler_params=pltpu.CompilerParams(vmem_limit_bytes=VMEM_LIMIT),
    )(c16, g16, g16_sh, w_sh)


def _ln0_fwd(x, ctx, g, b, modx, modc):
    L = x.shape[0]
    nt = L // TL

    def body(x_ref, c_ref, g_ref, b_ref, mx_ref, mc_ref, xn_ref, h_ref):
        isc = pl.program_id(0) == nt
        xin = jnp.where(isc, c_ref[...], x_ref[...])
        sh = jnp.where(isc, mc_ref[0:1, :], mx_ref[0:1, :])
        sc = jnp.where(isc, mc_ref[1:2, :], mx_ref[1:2, :])
        xhat, _ = _ln(xin)
        xn = xhat * g_ref[...] + b_ref[...]
        xn_ref[...] = xn
        h_ref[...] = (xn * (1.0 + sc) + sh).astype(h_ref.dtype)

    return pl.pallas_call(
        body, name="ln0_fwd", grid=(nt + 1,),
        in_specs=[_rtc(D, nt), _cst((TL, D)), _cst((1, D)), _cst((1, D)), _cst((8, D)), _cst((8, D))],
        out_specs=[_rt(D), _rt(D)],
        out_shape=[jax.ShapeDtypeStruct((L + TL, D), F32), jax.ShapeDtypeStruct((L + TL, D), _MXU)],
        compiler_params=_params(("parallel",)),
    )(x, ctx, g, b, modx, modc)


def _xbc_colblk(j):
    return jnp.where(j < 8, OXS // 128 + j, OB // 128 + j - 8)


def _conv_taps(p_ref, r0, first, last):
    main = p_ref[pl.ds(r0, TL), :]
    zero = jnp.zeros((8, main.shape[1]), F32)
    prev = zero if first else p_ref[pl.ds(r0 - 8, 8), :]
    nxt = zero if last else p_ref[pl.ds(r0 + TL, 8), :]
    ext = jnp.concatenate([prev, main, nxt], axis=0)
    n = TL + 16
    return [pltpu.roll(ext, (2 - k) % n, 0)[8:8 + TL] for k in range(5)]


def _seq_chunks(L):
    nt = L // TL
    return [(r * TL, r == 0, r == nt - 1) for r in range(nt)] + [(L, True, True)]


def _conv_fwd(p, conv_w8, conv_b):
    RT = p.shape[0]
    L = RT - TL
    chunks = _seq_chunks(L)

    def body(p_ref, w_ref, b_ref, o_ref):
        w = w_ref[...]
        bias = b_ref[...]
        for r0, first, last in chunks:
            taps = _conv_taps(p_ref, r0, first, last)
            pre = bias + sum(w[k:k + 1, :] * taps[k] for k in range(5))
            o_ref[pl.ds(r0, TL), :] = pre * _sig(pre)

    return pl.pallas_call(
        body, name="conv_fwd", grid=(12,),
        in_specs=[pl.BlockSpec((RT, 128), lambda j: (0, _xbc_colblk(j))),
                  pl.BlockSpec((8, 128), lambda j: (0, j)),
                  pl.BlockSpec((1, 128), lambda j: (0, j))],
        out_specs=pl.BlockSpec((RT, 128), lambda j: (0, j)),
        out_shape=jax.ShapeDtypeStruct((RT, 1536), F32),
        compiler_params=_params(("parallel",)),
    )(p, conv_w8, conv_b)


def _ssd_common(dtraw, dtb, a32, rev):
    dt = _softplus(dtraw + dtb)
    acum = _cumsum_rows(dt * a32, rev)
    ii = lax.broadcasted_iota(jnp.int32, (Q, Q), 0)
    jj = lax.broadcasted_iota(jnp.int32, (Q, Q), 1)
    mask = (ii <= jj) if rev else (ii >= jj)
    return dt, acum, acum.T, dt.T, mask


def _ssd_orders(ncl, ncc):
    nc = ncl + ncc

    def cf(s):
        return jnp.where(s < ncc, ncl + s, s - ncc)

    def cb(s):
        return nc - 1 - s

    return cf, cb


def _ssd_fwd(xbc, p, prm):
    RT = xbc.shape[0]
    nc = RT // Q
    ncc = TL // Q
    cf, cb = _ssd_orders(nc - ncc, ncc)

    def one_dir(x_ref, dt_ref, prm_ref, y_ref, hp_ref, H_ref, d):
        rev = d == 1
        a32 = -jnp.exp(prm_ref[1:2, :])
        dt, acum, acumT, dtT, mask = _ssd_common(dt_ref[...], prm_ref[0:1, :], a32, rev)
        end = 0 if rev else Q - 1
        for g in range(2):
            Bg = x_ref[:, D + g * NS:D + (g + 1) * NS]
            Cg = x_ref[:, D + 2 * NS + g * NS:D + 2 * NS + (g + 1) * NS]
            CB = _dot(Cg, Bg, NT)
            for hh in range(HPG):
                h = g * HPG + hh
                ln = 16 * d + h
                col = acum[:, ln:ln + 1]
                rowv = acumT[ln:ln + 1, :]
                a_end = rowv[:, end:end + 1]
                Lm = jnp.exp(jnp.where(mask, col - rowv, -1e30))
                W = CB * Lm * dtT[ln:ln + 1, :]
                Xh = x_ref[:, h * HP:(h + 1) * HP]
                Hp = H_ref[h * HP:(h + 1) * HP, :]
                y = _dot(W, Xh, NN) + jnp.exp(col) * _dot(Cg, Hp, NT)
                y_ref[:, h * HP:(h + 1) * HP] = y
                dcol = jnp.exp(a_end - col) * dt[:, ln:ln + 1]
                hp_ref[0, h * HP:(h + 1) * HP, :] = Hp
                H_ref[h * HP:(h + 1) * HP, :] = jnp.exp(a_end) * Hp + _dot(Xh * dcol, Bg, TN)

    def body(xf_ref, xb_ref, df_ref, db_ref, prm_ref, yf_ref, yb_ref, hf_ref, hb_ref, Hf, Hb):
        @pl.when(pl.program_id(0) == 0)
        def _():
            Hf[...] = jnp.zeros_like(Hf)
            Hb[...] = jnp.zeros_like(Hb)

        one_dir(xf_ref, df_ref, prm_ref, yf_ref, hf_ref, Hf, 0)
        one_dir(xb_ref, db_ref, prm_ref, yb_ref, hb_ref, Hb, 1)

    ysh = jax.ShapeDtypeStruct((RT, D), F32)
    hsh = jax.ShapeDtypeStruct((nc, NH * HP, NS), F32)
    hspec = pl.BlockSpec((1, NH * HP, NS), lambda s: (s, 0, 0))
    return pl.pallas_call(
        body, name="ssd_fwd", grid=(nc,),
        in_specs=[pl.BlockSpec((Q, 1536), lambda s: (cf(s), 0)),
                  pl.BlockSpec((Q, 1536), lambda s: (cb(s), 0)),
                  pl.BlockSpec((Q, 128), lambda s: (cf(s), ODT // 128)),
                  pl.BlockSpec((Q, 128), lambda s: (cb(s), ODT // 128)),
                  _cst((8, 128))],
        out_specs=[pl.BlockSpec((Q, D), lambda s: (cf(s), 0)),
                   pl.BlockSpec((Q, D), lambda s: (cb(s), 0)), hspec, hspec],
        out_shape=[ysh, ysh, hsh, hsh],
        scratch_shapes=[pltpu.VMEM((NH * HP, NS), F32), pltpu.VMEM((NH * HP, NS), F32)],
        compiler_params=_params(("arbitrary",)),
    )(xbc, xbc, p, p, prm)


def _ssd_bwd(xbc, p, prm, dsk, dyd, hpf, hpb):
    RT = xbc.shape[0]
    nc = RT // Q
    ncc = TL // Q
    ncl = nc - ncc
    cf, cb = _ssd_orders(ncl, ncc)

    def rs(t):
        return nc - 1 - t

    def one_dir(x_ref, dt_ref, prm_ref, dsk_ref, dy_ref, is_ctx, hp_ref, dH_ref,
                dx_ref, ddt_ref, st_ref, d):
        rev = d == 1
        a32 = -jnp.exp(prm_ref[1:2, :])
        dtraw = dt_ref[...]
        dtb = prm_ref[0:1, :]
        dt, acum, acumT, dtT, mask = _ssd_common(dtraw, dtb, a32, rev)
        end = 0 if rev else Q - 1
        lane = lax.broadcasted_iota(jnp.int32, (Q, 128), 1)
        srow = lax.broadcasted_iota(jnp.int32, (Q, 128), 0)
        dyscale = jnp.where(is_ctx, 0.0, 1.0)
        c_dacum = jnp.zeros((Q, 128), F32)
        r_dacum = jnp.zeros((Q, 128), F32)
        c_ddt = jnp.zeros((Q, 128), F32)
        r_ddt = jnp.zeros((Q, 128), F32)
        dskacc = jnp.zeros((1, 128), F32)
        for g in range(2):
            Bg = x_ref[:, D + g * NS:D + (g + 1) * NS]
            Cg = x_ref[:, D + 2 * NS + g * NS:D + 2 * NS + (g + 1) * NS]
            CB = _dot(Cg, Bg, NT)
            dCB = jnp.zeros((Q, Q), F32)
            dBg = jnp.zeros((Q, NS), F32)
            dCg = jnp.zeros((Q, NS), F32)
            for hh in range(HPG):
                h = g * HPG + hh
                ln = 16 * d + h
                hs = slice(h * HP, (h + 1) * HP)
                col = acum[:, ln:ln + 1]
                rowv = acumT[ln:ln + 1, :]
                dtr = dtT[ln:ln + 1, :]
                dtc = dt[:, ln:ln + 1]
                a_end = rowv[:, end:end + 1]
                Lm = jnp.exp(jnp.where(mask, col - rowv, -1e30))
                E = jnp.exp(col)
                ecol = jnp.exp(a_end - col)
                dcol = ecol * dtc
                Xh = x_ref[:, hs]
                dY = dy_ref[:, hs] * dyscale
                Hp = hp_ref[0, hs, :]
                dHn = dH_ref[hs, :]
                W = CB * Lm * dtr
                dW = _dot(dY, Xh, NT)
                Mm = dW * CB * Lm
                T = Mm * dtr
                dCB = dCB + dW * Lm * dtr
                BdH = _dot(Bg, dHn, NT)
                dX = _dot(W, dY, TN) + dcol * BdH
                if d == 0:
                    dX = dX + dY * dsk_ref[:, hs]
                    dskacc = dskacc + jnp.where(lane[0:1, :] == h, _sum11(dY * Xh), 0.0)
                dx_ref[:, hs] = dX
                xb = jnp.sum(Xh * BdH, axis=1, keepdims=True)
                scol = dcol * xb
                G = _dot(dY, Hp, NN)
                dCg = dCg + E * G
                qcol = E * jnp.sum(G * Cg, axis=1, keepdims=True)
                dBg = dBg + _dot(Xh * dcol, dHn, NN)
                dH_ref[hs, :] = jnp.exp(a_end) * dHn + _dot(dY * E, Cg, TN)
                eterm = jnp.exp(a_end) * _sum11(dHn * Hp) + _sum11(scol)
                cvec = jnp.sum(T, axis=1, keepdims=True) + qcol - scol
                cvec = cvec + jnp.where(srow[:, 0:1] == end, eterm, 0.0)
                c_dacum = c_dacum + jnp.where(lane == ln, cvec, 0.0)
                r_dacum = r_dacum - jnp.where(srow == ln, _colsum(T), 0.0)
                c_ddt = c_ddt + jnp.where(lane == ln, ecol * xb, 0.0)
                r_ddt = r_ddt + jnp.where(srow == ln, _colsum(Mm), 0.0)
            dBg = dBg + _dot(dCB, Cg, TN)
            dCg = dCg + _dot(dCB, Bg, NN)
            dx_ref[:, D + g * NS:D + (g + 1) * NS] = dBg
            dx_ref[:, D + 2 * NS + g * NS:D + 2 * NS + (g + 1) * NS] = dCg
        dacum = c_dacum + r_dacum.T
        da = _cumsum_rows(dacum, not rev)
        mine = (lane >= 16 * d) & (lane < 16 * d + 16)
        ddt = jnp.where(mine, c_ddt + r_ddt.T + da * a32, 0.0)
        ddt_ref[...] = ddt * _sig(dtraw + dtb)
        st_ref[0:1, :] += _colsum(jnp.where(mine, da * dt, 0.0))
        if d == 0:
            st_ref[1:2, :] += dskacc

    def body(xf_ref, xb_ref, df_ref, db_ref, prm_ref, dsk_ref, dyf_ref, dyb_ref, hf_ref, hb_ref,
             dxf_ref, dxb_ref, ddf_ref, ddb_ref, st_ref, dHf, dHb):
        t = pl.program_id(0)

        @pl.when(t == 0)
        def _():
            dHf[...] = jnp.zeros_like(dHf)
            dHb[...] = jnp.zeros_like(dHb)
            st_ref[...] = jnp.zeros_like(st_ref)

        s = rs(t)
        one_dir(xf_ref, df_ref, prm_ref, dsk_ref, dyf_ref, cf(s) >= ncl, hf_ref, dHf,
                dxf_ref, ddf_ref, st_ref, 0)
        one_dir(xb_ref, db_ref, prm_ref, dsk_ref, dyb_ref, cb(s) >= ncl, hb_ref, dHb,
                dxb_ref, ddb_ref, st_ref, 1)

        @pl.when(t == nc - 1)
        def _():
            st_ref[0:1, :] = -jnp.exp(prm_ref[1:2, :]) * st_ref[0:1, :]

    def lat(c):
        return jnp.minimum(c, ncl - 1)

    xsh = jax.ShapeDtypeStruct((RT, 1536), F32)
    dsh = jax.ShapeDtypeStruct((RT, 128), F32)
    hspec = pl.BlockSpec((1, NH * HP, NS), lambda t: (rs(t), 0, 0))
    return pl.pallas_call(
        body, name="ssd_bwd", grid=(nc,),
        in_specs=[pl.BlockSpec((Q, 1536), lambda t: (cf(rs(t)), 0)),
                  pl.BlockSpec((Q, 1536), lambda t: (cb(rs(t)), 0)),
                  pl.BlockSpec((Q, 128), lambda t: (cf(rs(t)), ODT // 128)),
                  pl.BlockSpec((Q, 128), lambda t: (cb(rs(t)), ODT // 128)),
                  _cst((8, 128)), _cst((1, D)),
                  pl.BlockSpec((Q, D), lambda t: (lat(cf(rs(t))), 0)),
                  pl.BlockSpec((Q, D), lambda t: (lat(cb(rs(t))), 0)),
                  hspec, hspec],
        out_specs=[pl.BlockSpec((Q, 1536), lambda t: (cf(rs(t)), 0)),
                   pl.BlockSpec((Q, 1536), lambda t: (cb(rs(t)), 0)),
                   pl.BlockSpec((Q, 128), lambda t: (cf(rs(t)), 0)),
                   pl.BlockSpec((Q, 128), lambda t: (cb(rs(t)), 0)),
                   _cst((8, 128))],
        out_shape=[xsh, xsh, dsh, dsh, jax.ShapeDtypeStruct((8, 128), F32)],
        scratch_shapes=[pltpu.VMEM((NH * HP, NS), F32), pltpu.VMEM((NH * HP, NS), F32)],
        compiler_params=_params(("arbitrary",)),
    )(xbc, xbc, p, p, prm, dsk, dyd, dyd, hpf, hpb)


def _lane_bcast(v, ln):
    return jnp.broadcast_to(v[:, ln:ln + 1], v.shape)


def _halves(v, lo, axis):
    return jnp.concatenate([jnp.where(lo, v, 0.0), jnp.where(lo, 0.0, v)], axis=axis)


def _ssd2_fwd(xbc, p, prm):
    RT = xbc.shape[0]
    nc = RT // Q
    ncc = TL // Q
    cf, cb = _ssd_orders(nc - ncc, ncc)

    def one_dir(x_ref, dt_ref, prm_ref, y_ref, hp_ref, HT_ref, d):
        rev = d == 1
        a32 = -jnp.exp(prm_ref[1:2, :])
        dt, acum, acumT, dtT, mask = _ssd_common(dt_ref[...], prm_ref[0:1, :], a32, rev)
        end = 0 if rev else Q - 1
        lo = lax.broadcasted_iota(jnp.int32, (Q, 128), 1) < HP
        for g in range(2):
            Bg = x_ref[:, D + g * NS:D + (g + 1) * NS]
            Cg = x_ref[:, D + 2 * NS + g * NS:D + 2 * NS + (g + 1) * NS]
            CB = _dot(Cg, Bg, NT)
            xds, svs = [], []
            for q in range(HPG // 2):
                pi = g * (HPG // 2) + q
                ps = slice(pi * 128, (pi + 1) * 128)
                Xp = x_ref[:, ps]
                HTp = HT_ref[:, ps]
                lhs, dcs, sv = [], [], []
                ces = []
                for h in (2 * pi, 2 * pi + 1):
                    ln = 16 * d + h
                    colB = _lane_bcast(acum, ln)
                    rowv = acumT[ln:ln + 1, :]
                    aend = colB[end:end + 1, :]
                    Lm = jnp.exp(jnp.where(mask, colB - rowv, -1e30))
                    lhs.append(CB * Lm * dtT[ln:ln + 1, :])
                    ces.append(Cg * jnp.exp(colB))
                    dcs.append(jnp.exp(aend - colB) * _lane_bcast(dt, ln))
                    sv.append(jnp.exp(aend))
                lhs = jnp.concatenate(lhs + ces, axis=1)
                rhs = jnp.concatenate([_halves(Xp, lo, 0), _halves(HTp, lo, 0)], axis=0)
                y_ref[:, ps] = _dot(lhs, rhs, NN)
                xds.append(Xp * jnp.where(lo, dcs[0], dcs[1]))
                svs.append(jnp.where(lo[0:1, :], sv[0], sv[1]))
            gs = slice(g * 512, (g + 1) * 512)
            HTg = HT_ref[:, gs]
            hp_ref[0, :, gs] = HTg
            st = _dot(Bg.T, jnp.concatenate(xds, axis=1), NN)
            HT_ref[:, gs] = jnp.concatenate(svs, axis=1) * HTg + st

    def body(xf_ref, xb_ref, df_ref, db_ref, prm_ref, yf_ref, yb_ref, hf_ref, hb_ref, Hf, Hb):
        @pl.when(pl.program_id(0) == 0)
        def _():
            Hf[...] = jnp.zeros_like(Hf)
            Hb[...] = jnp.zeros_like(Hb)

        one_dir(xf_ref, df_ref, prm_ref, yf_ref, hf_ref, Hf, 0)
        one_dir(xb_ref, db_ref, prm_ref, yb_ref, hb_ref, Hb, 1)

    ysh = jax.ShapeDtypeStruct((RT, D), F32)
    hsh = jax.ShapeDtypeStruct((nc, NS, NH * HP), F32)
    hspec = pl.BlockSpec((1, NS, NH * HP), lambda s: (s, 0, 0))
    return pl.pallas_call(
        body, name="ssd_fwd", grid=(nc,),
        in_specs=[pl.BlockSpec((Q, 1536), lambda s: (cf(s), 0)),
                  pl.BlockSpec((Q, 1536), lambda s: (cb(s), 0)),
                  pl.BlockSpec((Q, 128), lambda s: (cf(s), ODT // 128)),
                  pl.BlockSpec((Q, 128), lambda s: (cb(s), ODT // 128)),
                  _cst((8, 128))],
        out_specs=[pl.BlockSpec((Q, D), lambda s: (cf(s), 0)),
                   pl.BlockSpec((Q, D), lambda s: (cb(s), 0)), hspec, hspec],
        out_shape=[ysh, ysh, hsh, hsh],
        scratch_shapes=[pltpu.VMEM((NS, NH * HP), F32), pltpu.VMEM((NS, NH * HP), F32)],
        compiler_params=_params(("arbitrary",)),
    )(xbc, xbc, p, p, prm)


def _ssd2_bwd(xbc, p, prm, dsk, dyd, hpf, hpb):
    RT = xbc.shape[0]
    nc = RT // Q
    ncc = TL // Q
    ncl = nc - ncc
    cf, cb = _ssd_orders(ncl, ncc)

    def rs(t):
        return nc - 1 - t

    def one_dir(x_ref, dt_ref, prm_ref, dsk_ref, dy_ref, is_ctx, hp_ref, dHT_ref,
                dx_ref, ddt_ref, st_ref, d):
        rev = d == 1
        a32 = -jnp.exp(prm_ref[1:2, :])
        dtraw = dt_ref[...]
        dtb = prm_ref[0:1, :]
        dt, acum, acumT, _, _ = _ssd_common(dtraw, dtb, a32, rev)
        end = 0 if rev else Q - 1
        lane = lax.broadcasted_iota(jnp.int32, (Q, 128), 1)
        srow = lax.broadcasted_iota(jnp.int32, (Q, 128), 0)
        maskT = (lane <= srow) if rev else (lane >= srow)
        lo = lane < HP
        lo1 = lo[0:1, :]
        dyscale = jnp.where(is_ctx, 0.0, 1.0)
        c_dacum = jnp.zeros((Q, 128), F32)
        r_dacum = jnp.zeros((Q, 128), F32)
        c_ddt = jnp.zeros((Q, 128), F32)
        dskacc = jnp.zeros((1, 128), F32)
        for g in range(2):
            gs = slice(g * 512, (g + 1) * 512)
            Bg = x_ref[:, D + g * NS:D + (g + 1) * NS]
            Cg = x_ref[:, D + 2 * NS + g * NS:D + 2 * NS + (g + 1) * NS]
            CBT = _dot(Bg, Cg, NT)
            HTg = hp_ref[0, :, gs]
            dHTg = dHT_ref[:, gs]
            BdHg = _dot(Bg, dHTg, NN)
            dCBT = jnp.zeros((Q, Q), F32)
            dCg = jnp.zeros((Q, NS), F32)
            xds, dyes, svs = [], [], []
            for q in range(HPG // 2):
                pi = g * (HPG // 2) + q
                ps = slice(pi * 128, (pi + 1) * 128)
                qs = slice(q * 128, (q + 1) * 128)
                Xp = x_ref[:, ps]
                dYp = dy_ref[:, ps] * dyscale
                HTp = HTg[:, qs]
                BdHp = BdHg[:, qs]
                dY2 = _halves(dYp, lo, 0)
                dWT2 = _dot(_halves(Xp, lo, 0), dYp.T, NN)
                G2 = _dot(dY2, HTp, NT)
                XB = Xp * BdHp
                hh = _colsum(dHTg[:, qs] * HTp)
                yx = _colsum(dYp * Xp)
                wts, dcs, ebs, sv = [], [], [], []
                for k, h in enumerate((2 * pi, 2 * pi + 1)):
                    ln = 16 * d + h
                    half = lo if k == 0 else jnp.logical_not(lo)
                    half1 = half[0:1, :]
                    colB = _lane_bcast(acum, ln)
                    dtcB = _lane_bcast(dt, ln)
                    rowv = acumT[ln:ln + 1, :]
                    aend = colB[end:end + 1, :]
                    LmT = jnp.exp(jnp.where(maskT, rowv - colB, -1e30))
                    WT = CBT * LmT * dtcB
                    dWT = dWT2[k * Q:(k + 1) * Q, :]
                    MT = dWT * CBT * LmT
                    rM = jnp.sum(MT, axis=1, keepdims=True)
                    rT = _colsum(MT * dtcB)
                    dCBT = dCBT + dWT * LmT * dtcB
                    ecol = jnp.exp(aend - colB)
                    EB = jnp.exp(colB)
                    Gk = G2[k * Q:(k + 1) * Q, :]
                    dCg = dCg + EB * Gk
                    qcol = jnp.sum(EB * Gk * Cg, axis=1, keepdims=True)
                    xb = jnp.sum(jnp.where(half, XB, 0.0), axis=1, keepdims=True)
                    e1 = ecol[:, 0:1]
                    dt1 = dtcB[:, 0:1]
                    scol = e1 * dt1 * xb
                    sA = jnp.exp(aend)
                    eterm = sA[:, 0:1] * jnp.sum(jnp.where(half1, hh, 0.0), axis=1, keepdims=True) \
                        + _colsum(scol)
                    cvec = qcol - dt1 * rM - scol + jnp.where(srow[:, 0:1] == end, eterm, 0.0)
                    c_dacum = c_dacum + jnp.where(lane == ln, cvec, 0.0)
                    r_dacum = r_dacum + jnp.where(srow == ln, rT, 0.0)
                    c_ddt = c_ddt + jnp.where(lane == ln, rM + e1 * xb, 0.0)
                    if d == 0:
                        dskacc = dskacc + jnp.where(
                            lane[0:1, :] == h, jnp.sum(jnp.where(half1, yx, 0.0), axis=1, keepdims=True), 0.0)
                    wts.append(WT)
                    dcs.append(ecol * dtcB)
                    ebs.append(EB)
                    sv.append(sA)
                dcp = jnp.where(lo, dcs[0], dcs[1])
                dX = _dot(jnp.concatenate(wts, axis=1), dY2, NN) + dcp * BdHp
                if d == 0:
                    dX = dX + dYp * dsk_ref[:, ps]
                dx_ref[:, ps] = dX
                xds.append(Xp * dcp)
                dyes.append(dYp * jnp.where(lo, ebs[0], ebs[1]))
                svs.append(jnp.where(lo1, sv[0], sv[1]))
            dx_ref[:, D + g * NS:D + (g + 1) * NS] = (
                _dot(jnp.concatenate(xds, axis=1), dHTg, NT) + _dot(dCBT, Cg, NN))
            dx_ref[:, D + 2 * NS + g * NS:D + 2 * NS + (g + 1) * NS] = dCg + _dot(dCBT, Bg, TN)
            dHT_ref[:, gs] = (jnp.concatenate(svs, axis=1) * dHTg
                              + _dot(Cg.T, jnp.concatenate(dyes, axis=1), NN))
        dacum = c_dacum + r_dacum.T
        da = _cumsum_rows(dacum, not rev)
        mine = (lane >= 16 * d) & (lane < 16 * d + 16)
        ddt = jnp.where(mine, c_ddt + da * a32, 0.0)
        ddt_ref[...] = ddt * _sig(dtraw + dtb)
        st_ref[0:1, :] += _colsum(jnp.where(mine, da * dt, 0.0))
        if d == 0:
            st_ref[1:2, :] += dskacc

    def body(xf_ref, xb_ref, df_ref, db_ref, prm_ref, dsk_ref, dyf_ref, dyb_ref, hf_ref, hb_ref,
             dxf_ref, dxb_ref, ddf_ref, ddb_ref, st_ref, dHf, dHb):
        t = pl.program_id(0)

        @pl.when(t == 0)
        def _():
            dHf[...] = jnp.zeros_like(dHf)
            dHb[...] = jnp.zeros_like(dHb)
            st_ref[...] = jnp.zeros_like(st_ref)

        s = rs(t)
        one_dir(xf_ref, df_ref, prm_ref, dsk_ref, dyf_ref, cf(s) >= ncl, hf_ref, dHf,
                dxf_ref, ddf_ref, st_ref, 0)
        one_dir(xb_ref, db_ref, prm_ref, dsk_ref, dyb_ref, cb(s) >= ncl, hb_ref, dHb,
                dxb_ref, ddb_ref, st_ref, 1)

        @pl.when(t == nc - 1)
        def _():
            st_ref[0:1, :] = -jnp.exp(prm_ref[1:2, :]) * st_ref[0:1, :]

    def lat(c):
        return jnp.minimum(c, ncl - 1)

    xsh = jax.ShapeDtypeStruct((RT, 1536), F32)
    dsh = jax.ShapeDtypeStruct((RT, 128), F32)
    hspec = pl.BlockSpec((1, NS, NH * HP), lambda t: (rs(t), 0, 0))
    return pl.pallas_call(
        body, name="ssd_bwd", grid=(nc,),
        in_specs=[pl.BlockSpec((Q, 1536), lambda t: (cf(rs(t)), 0)),
                  pl.BlockSpec((Q, 1536), lambda t: (cb(rs(t)), 0)),
                  pl.BlockSpec((Q, 128), lambda t: (cf(rs(t)), ODT // 128)),
                  pl.BlockSpec((Q, 128), lambda t: (cb(rs(t)), ODT // 128)),
                  _cst((8, 128)), _cst((1, D)),
                  pl.BlockSpec((Q, D), lambda t: (lat(cf(rs(t))), 0)),
                  pl.BlockSpec((Q, D), lambda t: (lat(cb(rs(t))), 0)),
                  hspec, hspec],
        out_specs=[pl.BlockSpec((Q, 1536), lambda t: (cf(rs(t)), 0)),
                   pl.BlockSpec((Q, 1536), lambda t: (cb(rs(t)), 0)),
                   pl.BlockSpec((Q, 128), lambda t: (cf(rs(t)), 0)),
                   pl.BlockSpec((Q, 128), lambda t: (cb(rs(t)), 0)),
                   _cst((8, 128))],
        out_shape=[xsh, xsh, dsh, dsh, jax.ShapeDtypeStruct((8, 128), F32)],
        scratch_shapes=[pltpu.VMEM((NS, NH * HP), F32), pltpu.VMEM((NS, NH * HP), F32)],
        compiler_params=_params(("arbitrary",)),
    )(xbc, xbc, p, p, prm, dsk, dyd, dyd, hpf, hpb)


def _mix_fwd_vals(yf, yb, z, xs, u, v, dsk, sg, gg, gb):
    y = yf + yb + xs * dsk
    sz = _sig(z)
    hh = y * z * sz
    r = lax.rsqrt(jnp.mean(hh * hh, axis=-1, keepdims=True) + EPS)
    nh = hh * r
    ug, tu = _gelu(u)
    vg, tv = _gelu(v)
    vhat, vrstd = _ln(vg)
    vn = vhat * gg + gb
    return y, sz, r, nh, ug, tu, vg, tv, vhat, vrstd, vn


def _mix_fwd(yf, yb, p, xbc, dsk, sg, gg, gb, ws, bsT):
    L = yf.shape[0] - TL
    nt = L // TL

    def body(yf_ref, yb_ref, z_ref, xs_ref, u_ref, v_ref, dsk_ref, sg_ref, gg_ref, gb_ref,
             ws_ref, bs_ref, ys_ref, ym_ref):
        _, _, _, nh, ug, _, _, _, _, _, vn = _mix_fwd_vals(
            yf_ref[...], yb_ref[...], z_ref[...], xs_ref[...], u_ref[...], v_ref[...],
            dsk_ref[...], sg_ref[...], gg_ref[...], gb_ref[...])
        ys_ref[...] = (nh * sg_ref[...]).astype(ys_ref.dtype)
        for n in range(TL // Q):
            rs_ = slice(n * Q, (n + 1) * Q)
            for g in range(8):
                cs = slice(g * 128, (g + 1) * 128)
                mixed = _dot(ws_ref[g], vn[rs_, cs], NN) + bs_ref[:, g:g + 1]
                ym_ref[rs_, cs] = (ug[rs_, cs] * mixed).astype(ym_ref.dtype)

    return pl.pallas_call(
        body, name="mix_fwd", grid=(nt,),
        in_specs=[_rt(D), _rt(D), _rt(D, OZ // D), _rt(D, 0), _rt(D, OU // D), _rt(D, OV // D),
                  _cst((1, D)), _cst((1, D)), _cst((1, D)), _cst((1, D)),
                  _cst((8, 128, 128)), _cst((128, 128))],
        out_specs=[_rt(D), _rt(D)],
        out_shape=[jax.ShapeDtypeStruct((L, D), _MXU), jax.ShapeDtypeStruct((L, D), _MXU)],
        compiler_params=_params(("parallel",)),
    )(yf, yb, p, xbc, p, p, dsk, sg, gg, gb, ws, bsT)


def _mix_bwd(dys, dym, yf, yb, p, xbc, dp, dsk, sg, gg, gb, ws, bsT):
    L = dys.shape[0]
    nt = L // TL

    def body(dys_ref, dym_ref, yf_ref, yb_ref, z_ref, xs_ref, u_ref, v_ref, dsk_ref, sg_ref,
             gg_ref, gb_ref, ws_ref, bs_ref, dp_any, dzuv_ref, dy_ref, st_ref,
             dws_ref, dbs_ref, dvn_s):
        del dp_any
        dz_ref = dzuv_ref.at[:, OZ:OZ + D]
        du_ref = dzuv_ref.at[:, OU:OU + D]
        dv_ref = dzuv_ref.at[:, OV:OV + D]

        @pl.when(pl.program_id(0) == 0)
        def _():
            st_ref[...] = jnp.zeros_like(st_ref)
            dws_ref[...] = jnp.zeros_like(dws_ref)
            dbs_ref[...] = jnp.zeros_like(dbs_ref)

        z = z_ref[...]
        u = u_ref[...]
        v = v_ref[...]
        y, sz, r, nh, ug, tu, vg, tv, vhat, vrstd, vn = _mix_fwd_vals(
            yf_ref[...], yb_ref[...], z, xs_ref[...], u, v,
            dsk_ref[...], sg_ref[...], gg_ref[...], gb_ref[...])
        dys = dys_ref[...]
        st_ref[0:1, :] += _colsum(dys * nh)
        dn = dys * sg_ref[...]
        dhh = r * (dn - nh * jnp.mean(dn * nh, axis=-1, keepdims=True))
        dy_ref[...] = dhh * z * sz
        dz_ref[...] = (dhh * y * (sz * (1.0 + z * (1.0 - sz)))).astype(dz_ref.dtype)
        dym = dym_ref[...]
        lane = lax.broadcasted_iota(jnp.int32, (Q, 128), 1)
        dbs = jnp.zeros((Q, 128), F32)
        gu = _gelu_grad(u, tu)
        for n in range(TL // Q):
            rs_ = slice(n * Q, (n + 1) * Q)
            for g in range(8):
                cs = slice(g * 128, (g + 1) * 128)
                vb = vn[rs_, cs]
                mixed = _dot(ws_ref[g], vb, NN) + bs_ref[:, g:g + 1]
                dyb = dym[rs_, cs]
                dmx = dyb * ug[rs_, cs]
                du_ref[rs_, cs] = (dyb * mixed * gu[rs_, cs]).astype(du_ref.dtype)
                dvn_s[rs_, cs] = _dot(ws_ref[g], dmx, TN)
                dws_ref[g] += _dot(dmx, vb, NT)
                dbs = dbs + jnp.where(lane == g, jnp.sum(dmx, axis=1, keepdims=True), 0.0)
        dbs_ref[...] += dbs
        dvn = dvn_s[...]
        st_ref[1:2, :] += _colsum(dvn * vhat)
        st_ref[2:3, :] += _colsum(dvn)
        dvg = _ln_bwd(dvn * gg_ref[...], vhat, vrstd)
        dv_ref[...] = (dvg * _gelu_grad(v, tv)).astype(dv_ref.dtype)

    outs = pl.pallas_call(
        body, name="mix_bwd", grid=(nt,),
        in_specs=[_rt(D), _rt(D), _rt(D), _rt(D), _rt(D, OZ // D), _rt(D, 0), _rt(D, OU // D),
                  _rt(D, OV // D), _cst((1, D)), _cst((1, D)), _cst((1, D)), _cst((1, D)),
                  _cst((8, 128, 128)), _cst((128, 128)), pl.BlockSpec(memory_space=pl.ANY)],
        out_specs=[_rt(3 * D, 0), _rt(D), _cst((8, D)),
                   _cst((8, 128, 128)), _cst((128, 128))],
        out_shape=[jax.ShapeDtypeStruct(dp.shape, dp.dtype),
                   jax.ShapeDtypeStruct((L, D), F32), jax.ShapeDtypeStruct((8, D), F32),
                   jax.ShapeDtypeStruct((8, 128, 128), F32), jax.ShapeDtypeStruct((128, 128), F32)],
        scratch_shapes=[pltpu.VMEM((TL, D), F32)],
        input_output_aliases={14: 0},
        compiler_params=_params(("arbitrary",)),
    )(dys, dym, yf, yb, p, xbc, p, p, dsk, sg, gg, gb, ws, bsT, dp)
    return outs


def _gate_fwd(a1, a2, p, bg):
    L = a1.shape[0]

    def body(a1_ref, a2_ref, g_ref, bg_ref, m_ref):
        gt = _sig(g_ref[...] + bg_ref[...])
        m_ref[...] = (gt[:, :D] * a1_ref[...] + gt[:, D:] * a2_ref[...]).astype(m_ref.dtype)

    return pl.pallas_call(
        body, name="gate_fwd", grid=(L // TL,),
        in_specs=[_rt(D), _rt(D), _rt(2 * D, OG // (2 * D)), _cst((1, 2 * D))],
        out_specs=_rt(D), out_shape=jax.ShapeDtypeStruct((L, D), _MXU),
        compiler_params=_params(("parallel",)),
    )(a1, a2, p, bg)


def _gate_bwd(dmg, a1, a2, p, bg, dp):
    L = a1.shape[0]

    def body(dm_ref, a1_ref, a2_ref, g_ref, bg_ref, dp_any, dg_ref, da1_ref, da2_ref, st_ref):
        del dp_any

        @pl.when(pl.program_id(0) == 0)
        def _():
            st_ref[...] = jnp.zeros_like(st_ref)

        gt = _sig(g_ref[...] + bg_ref[...])
        g1 = gt[:, :D]
        g2 = gt[:, D:]
        dm = dm_ref[...]
        da1_ref[...] = (dm * g1).astype(da1_ref.dtype)
        da2_ref[...] = (dm * g2).astype(da2_ref.dtype)
        dg1 = dm * a1_ref[...] * g1 * (1.0 - g1)
        dg2 = dm * a2_ref[...] * g2 * (1.0 - g2)
        st_ref[0:1, 0:D] += _colsum(dg1)
        st_ref[0:1, D:2 * D] += _colsum(dg2)
        dg_ref[:, 0:D] = dg1.astype(dg_ref.dtype)
        dg_ref[:, D:2 * D] = dg2.astype(dg_ref.dtype)

    return pl.pallas_call(
        body, name="gate_bwd", grid=(L // TL,),
        in_specs=[_rt(D), _rt(D), _rt(D), _rt(2 * D, OG // (2 * D)), _cst((1, 2 * D)),
                  pl.BlockSpec(memory_space=pl.ANY)],
        out_specs=[_rt(2 * D, OG // (2 * D)), _rt(D), _rt(D), _cst((8, 2 * D))],
        out_shape=[jax.ShapeDtypeStruct(dp.shape, dp.dtype), jax.ShapeDtypeStruct((L, D), _MXU),
                   jax.ShapeDtypeStruct((L, D), _MXU), jax.ShapeDtypeStruct((8, 2 * D), F32)],
        input_output_aliases={5: 0},
        compiler_params=_params(("arbitrary",)),
    )(dmg, a1, a2, p, bg, dp)


def _res1_fwd(xn, out, modx, g, b):
    L = out.shape[0]

    def body(xn_ref, o_ref, mx_ref, g_ref, b_ref, r1_ref, h2_ref):
        r1 = ALPHA * xn_ref[...] + mx_ref[2:3, :] * o_ref[...]
        xhat, _ = _ln(r1)
        x1 = xhat * g_ref[...] + b_ref[...]
        r1_ref[...] = r1
        h2_ref[...] = (x1 * (1.0 + mx_ref[4:5, :]) + mx_ref[3:4, :]).astype(h2_ref.dtype)

    return pl.pallas_call(
        body, name="res1_fwd", grid=(L // TL,),
        in_specs=[_rt(D), _rt(D), _cst((8, D)), _cst((1, D)), _cst((1, D))],
        out_specs=[_rt(D), _rt(D)],
        out_shape=[jax.ShapeDtypeStruct((L, D), F32), jax.ShapeDtypeStruct((L, D), _MXU)],
        compiler_params=_params(("parallel",)),
    )(xn, out, modx, g, b)


def _glu_fwd(f13):
    L = f13.shape[0]

    def body(f1_ref, f3_ref, o_ref):
        f1 = f1_ref[...]
        o_ref[...] = (f1 * _sig(f1) * f3_ref[...]).astype(o_ref.dtype)

    return pl.pallas_call(
        body, name="glu_fwd", grid=(L // TL,),
        in_specs=[_rt(DFF, 0), _rt(DFF, 1)], out_specs=_rt(DFF),
        out_shape=jax.ShapeDtypeStruct((L, DFF), _MXU),
        compiler_params=_params(("parallel",)),
    )(f13, f13)


def _glu_bwd(dff, f13):
    L = f13.shape[0]

    def body(d_ref, f1_ref, f3_ref, o_ref):
        f1 = f1_ref[...]
        s = _sig(f1)
        d = d_ref[...]
        o_ref[:, 0:DFF] = (d * f3_ref[...] * (s * (1.0 + f1 * (1.0 - s)))).astype(o_ref.dtype)
        o_ref[:, DFF:2 * DFF] = (d * f1 * s).astype(o_ref.dtype)

    return pl.pallas_call(
        body, name="glu_bwd", grid=(L // TL,),
        in_specs=[_rt(DFF), _rt(DFF, 0), _rt(DFF, 1)], out_specs=_rt(2 * DFF),
        out_shape=jax.ShapeDtypeStruct((L, 2 * DFF), _MXU),
        compiler_params=_params(("parallel",)),
    )(dff, f13, f13)


def _res2(r1, o2, tgt, modx, g1, b1, g2, b2):
    L = r1.shape[0]

    def body(r1_ref, o2_ref, t_ref, mx_ref, g1_ref, b1_ref, g2_ref, b2_ref,
             dr2_ref, do2_ref, st_ref, loss_ref):
        @pl.when(pl.program_id(0) == 0)
        def _():
            st_ref[...] = jnp.zeros_like(st_ref)
            loss_ref[...] = jnp.zeros_like(loss_ref)

        xh1, _ = _ln(r1_ref[...])
        x1 = xh1 * g1_ref[...] + b1_ref[...]
        o2 = o2_ref[...]
        g2x = mx_ref[5:6, :]
        xh2, rstd2 = _ln(ALPHA * x1 + g2x * o2)
        err = xh2 * g2_ref[...] + b2_ref[...] - t_ref[...]
        per_tok = jnp.mean(err * err, axis=-1, keepdims=True)
        loss_ref[...] += 0.5 * jnp.sum(per_tok, axis=0, keepdims=True)
        dy = err * (1.0 / D)
        st_ref[0:1, :] += _colsum(dy * xh2)
        st_ref[1:2, :] += _colsum(dy)
        dr2 = _ln_bwd(dy * g2_ref[...], xh2, rstd2)
        st_ref[2:3, :] += _colsum(dr2 * o2)
        dr2_ref[...] = dr2
        do2_ref[...] = (g2x * dr2).astype(do2_ref.dtype)

    return pl.pallas_call(
        body, name="res2", grid=(L // TL,),
        in_specs=[_rt(D), _rt(D), _rt(D), _cst((8, D))] + [_cst((1, D))] * 4,
        out_specs=[_rt(D), _rt(D), _cst((8, D)), _cst((8, 128))],
        out_shape=[jax.ShapeDtypeStruct((L, D), F32), jax.ShapeDtypeStruct((L, D), _MXU),
                   jax.ShapeDtypeStruct((8, D), F32), jax.ShapeDtypeStruct((8, 128), F32)],
        compiler_params=_params(("arbitrary",)),
    )(r1, o2, tgt, modx, g1, b1, g2, b2)


def _res1_bwd(dr2, dh2, r1, out, modx, g1, b1):
    L = r1.shape[0]

    def body(dr2_ref, dh2_ref, r1_ref, o_ref, mx_ref, g_ref, b_ref, dr1_ref, do_ref, st_ref):
        @pl.when(pl.program_id(0) == 0)
        def _():
            st_ref[...] = jnp.zeros_like(st_ref)

        xh1, rstd1 = _ln(r1_ref[...])
        x1 = xh1 * g_ref[...] + b_ref[...]
        dh2 = dh2_ref[...]
        dx1 = ALPHA * dr2_ref[...] + dh2 * (1.0 + mx_ref[4:5, :])
        st_ref[0:1, :] += _colsum(dh2 * x1)
        st_ref[1:2, :] += _colsum(dh2)
        st_ref[2:3, :] += _colsum(dx1 * xh1)
        st_ref[3:4, :] += _colsum(dx1)
        dr1 = _ln_bwd(dx1 * g_ref[...], xh1, rstd1)
        st_ref[4:5, :] += _colsum(dr1 * o_ref[...])
        dr1_ref[...] = dr1
        do_ref[...] = (mx_ref[2:3, :] * dr1).astype(do_ref.dtype)

    return pl.pallas_call(
        body, name="res1_bwd", grid=(L // TL,),
        in_specs=[_rt(D), _rt(D), _rt(D), _rt(D), _cst((8, D)), _cst((1, D)), _cst((1, D))],
        out_specs=[_rt(D), _rt(D), _cst((8, D))],
        out_shape=[jax.ShapeDtypeStruct((L, D), F32), jax.ShapeDtypeStruct((L, D), _MXU),
                   jax.ShapeDtypeStruct((8, D), F32)],
        compiler_params=_params(("arbitrary",)),
    )(dr2, dh2, r1, out, modx, g1, b1)


def _conv_bwd(dxf, dxb, p, conv_w8, conv_b, dp):
    RT = p.shape[0]
    chunks = _seq_chunks(RT - TL)

    def body(df_ref, db_ref, p_ref, w_ref, b_ref, dp_any, o_ref, dw_ref, dbias_ref, dpre_s):
        del dp_any
        w = w_ref[...]
        bias = b_ref[...]
        srow = lax.broadcasted_iota(jnp.int32, (8, 128), 0)
        dwacc = jnp.zeros((8, 128), F32)
        dbacc = jnp.zeros((1, 128), F32)
        for r0, first, last in chunks:
            taps = _conv_taps(p_ref, r0, first, last)
            pre = bias + sum(w[k:k + 1, :] * taps[k] for k in range(5))
            s = _sig(pre)
            dpre = (df_ref[pl.ds(r0, TL), :] + db_ref[pl.ds(r0, TL), :]) * (s * (1.0 + pre * (1.0 - s)))
            dpre_s[pl.ds(r0, TL), :] = dpre
            dbacc = dbacc + _colsum(dpre)
            for k in range(5):
                dwacc = dwacc + jnp.where(srow == k, _colsum(dpre * taps[k]), 0.0)
        for r0, first, last in chunks:
            taps = _conv_taps(dpre_s, r0, first, last)
            dx = sum(w[k:k + 1, :] * taps[4 - k] for k in range(5))
            o_ref[pl.ds(r0, TL), :] = dx.astype(o_ref.dtype)
        dw_ref[...] = dwacc
        dbias_ref[...] = jnp.broadcast_to(dbacc, (8, 128))

    cspec = pl.BlockSpec((RT, 128), lambda j: (0, j))
    wspec = pl.BlockSpec((8, 128), lambda j: (0, j))
    return pl.pallas_call(
        body, name="conv_bwd", grid=(12,),
        in_specs=[cspec, cspec, pl.BlockSpec((RT, 128), lambda j: (0, _xbc_colblk(j))),
                  wspec, pl.BlockSpec((1, 128), lambda j: (0, j)), pl.BlockSpec(memory_space=pl.ANY)],
        out_specs=[pl.BlockSpec((RT, 128), lambda j: (0, _xbc_colblk(j))), wspec, wspec],
        out_shape=[jax.ShapeDtypeStruct(dp.shape, dp.dtype), jax.ShapeDtypeStruct((8, 1536), F32),
                   jax.ShapeDtypeStruct((8, 1536), F32)],
        scratch_shapes=[pltpu.VMEM((RT, 128), F32)],
        input_output_aliases={5: 0},
        compiler_params=_params(("parallel",)),
    )(dxf, dxb, p, conv_w8, conv_b, dp)


def _dt_bwd(ddf, ddb, dp):
    RT = ddf.shape[0]

    def body(f_ref, b_ref, dp_any, o_ref, st_ref):
        del dp_any

        @pl.when(pl.program_id(0) == 0)
        def _():
            st_ref[...] = jnp.zeros_like(st_ref)

        s = f_ref[...] + b_ref[...]
        o_ref[...] = s.astype(o_ref.dtype)
        st_ref[0:1, :] += _colsum(s)

    return pl.pallas_call(
        body, name="dt_bwd", grid=(RT // TL,),
        in_specs=[_rt(128), _rt(128), pl.BlockSpec(memory_space=pl.ANY)],
        out_specs=[_rt(128, ODT // 128), _cst((8, 128))],
        out_shape=[jax.ShapeDtypeStruct(dp.shape, dp.dtype), jax.ShapeDtypeStruct((8, 128), F32)],
        input_output_aliases={2: 0},
        compiler_params=_params(("arbitrary",)),
    )(ddf, ddb, dp)


def _ln0_bwd(dh1, dr1, x, ctx, g, b, modx, modc):
    L = x.shape[0]
    nt = L // TL

    def body(dh_ref, dr1_ref, x_ref, c_ref, g_ref, b_ref, mx_ref, mc_ref, gx_ref, st_ref):
        i = pl.program_id(0)
        isc = i == nt

        @pl.when(i == 0)
        def _():
            st_ref[...] = jnp.zeros_like(st_ref)

        xin = jnp.where(isc, c_ref[...], x_ref[...])
        xhat, rstd = _ln(xin)
        xn = xhat * g_ref[...] + b_ref[...]
        sc = jnp.where(isc, mc_ref[1:2, :], mx_ref[1:2, :])
        dh = dh_ref[...]
        lat = jnp.where(isc, 0.0, 1.0)
        dxn = dh * (1.0 + sc) + (lat * ALPHA) * dr1_ref[...]
        tsh = _colsum(dh)
        tsc = _colsum(dh * xn)
        st_ref[0:1, :] += lat * tsh
        st_ref[1:2, :] += lat * tsc
        st_ref[2:3, :] += (1.0 - lat) * tsh
        st_ref[3:4, :] += (1.0 - lat) * tsc
        st_ref[4:5, :] += _colsum(dxn * xhat)
        st_ref[5:6, :] += _colsum(dxn)

        @pl.when(i < nt)
        def _():
            gx_ref[...] = _ln_bwd(dxn * g_ref[...], xhat, rstd)

    return pl.pallas_call(
        body, name="ln0_bwd", grid=(nt + 1,),
        in_specs=[_rt(D), _rtc(D, nt), _rtc(D, nt), _cst((TL, D)), _cst((1, D)), _cst((1, D)),
                  _cst((8, D)), _cst((8, D))],
        out_specs=[_rtc(D, nt), _cst((8, D))],
        out_shape=[jax.ShapeDtypeStruct((L, D), F32), jax.ShapeDtypeStruct((8, D), F32)],
        compiler_params=_params(("arbitrary",)),
    )(dh1, dr1, x, ctx, g, b, modx, modc)


def _perm_cols(w):
    pad = jnp.zeros((w.shape[0], NPJ - NNAT), w.dtype)
    return jnp.concatenate([w[:, 0:1024], w[:, 2592:3616], w[:, 3616:4640], w[:, 1024:2048],
                            w[:, 4640:6688], w[:, 2048:2304], w[:, 2304:2560], w[:, 2560:2592], pad],
                           axis=1)


def _unperm_cols(g):
    return jnp.concatenate([g[:, OZ:OZ + D], g[:, OXS:OXS + D], g[:, OB:OB + 256], g[:, OC:OC + 256],
                            g[:, ODT:ODT + 32], g[:, OU:OU + D], g[:, OV:OV + D], g[:, OG:OG + 2 * D]],
                           axis=1)


def _padded(n, row_align):
    unit = row_align * D
    return -(-n // unit) * unit if row_align else n


def _slab(arrs, rows, row_align=0):
    parts = []
    for a in arrs:
        f = a.reshape(-1)
        parts.append(jnp.pad(f, (0, _padded(f.shape[0], row_align) - f.shape[0])))
    flat = jnp.concatenate(parts)
    flat = jnp.pad(flat, (0, rows * D - flat.shape[0]))
    return flat.reshape(rows, D)


def _unslab(slab, shapes, row_align=0):
    flat = slab.reshape(-1)
    out, off = [], 0
    for shp in shapes:
        n = 1
        for s in shp:
            n *= s
        out.append(flat[off:off + n].reshape(shp))
        off += _padded(n, row_align)
    return out


def _row(v):
    return v.reshape(1, -1)


def _pad_rows(a, rows):
    return jnp.pad(a, ((0, rows - a.shape[0]), (0, 0)))


BIG = ["w_in", "w_ssd_proj", "w_gm_proj", "w_out", "w_ff1", "w_ff3", "w_ff2"]
BIG_ROWS = 2304
BIG_ALIGN = 16
REPL = ["c_ctx", "ln0_g", "ln0_b", "b_ada", "conv_b", "dt_bias", "a_log", "d_skip", "ssd_norm_g",
        "gm_norm_g", "gm_norm_b", "w_spatial", "b_spatial", "b_gate", "ln1_g", "ln1_b", "ln2_g", "ln2_b"]
SMALL_ROWS = 160
WEIGHTS = ["c_ctx", "ln0_g", "ln0_b", "w_ada", "b_ada", "w_in", "conv_w", "conv_b", "dt_bias", "a_log",
           "d_skip", "ssd_norm_g", "gm_norm_g", "gm_norm_b", "w_spatial", "b_spatial", "b_gate",
           "w_ssd_proj", "w_gm_proj", "w_out", "ln1_g", "ln1_b", "w_ff1", "w_ff3", "w_ff2", "ln2_g", "ln2_b"]


def kernel(x, c, ctx, c_ctx, ln0_g, ln0_b, w_ada, b_ada, w_in, conv_w, conv_b, dt_bias, a_log, d_skip, ssd_norm_g, gm_norm_g, gm_norm_b, w_spatial, b_spatial, b_gate, w_ssd_proj, w_gm_proj, w_out, ln1_g, ln1_b, w_ff1, w_ff3, w_ff2, ln2_g, ln2_b, loss_target, m_c_ctx, m_ln0_g, m_ln0_b, m_w_ada, m_b_ada, m_w_in, m_conv_w, m_conv_b, m_dt_bias, m_a_log, m_d_skip, m_ssd_norm_g, m_gm_norm_g, m_gm_norm_b, m_w_spatial, m_b_spatial, m_b_gate, m_w_ssd_proj, m_w_gm_proj, m_w_out, m_ln1_g, m_ln1_b, m_w_ff1, m_w_ff3, m_w_ff2, m_ln2_g, m_ln2_b, v_c_ctx, v_ln0_g, v_ln0_b, v_w_ada, v_b_ada, v_w_in, v_conv_w, v_conv_b, v_dt_bias, v_a_log, v_d_skip, v_ssd_norm_g, v_gm_norm_g, v_gm_norm_b, v_w_spatial, v_b_spatial, v_b_gate, v_w_ssd_proj, v_w_gm_proj, v_w_out, v_ln1_g, v_ln1_b, v_w_ff1, v_w_ff3, v_w_ff2, v_ln2_g, v_ln2_b):
    W = dict(c_ctx=c_ctx, ln0_g=ln0_g, ln0_b=ln0_b, w_ada=w_ada, b_ada=b_ada, w_in=w_in, conv_w=conv_w,
             conv_b=conv_b, dt_bias=dt_bias, a_log=a_log, d_skip=d_skip, ssd_norm_g=ssd_norm_g,
             gm_norm_g=gm_norm_g, gm_norm_b=gm_norm_b, w_spatial=w_spatial, b_spatial=b_spatial,
             b_gate=b_gate, w_ssd_proj=w_ssd_proj, w_gm_proj=w_gm_proj, w_out=w_out, ln1_g=ln1_g,
             ln1_b=ln1_b, w_ff1=w_ff1, w_ff3=w_ff3, w_ff2=w_ff2, ln2_g=ln2_g, ln2_b=ln2_b)
    M = dict(c_ctx=m_c_ctx, ln0_g=m_ln0_g, ln0_b=m_ln0_b, w_ada=m_w_ada, b_ada=m_b_ada, w_in=m_w_in,
             conv_w=m_conv_w, conv_b=m_conv_b, dt_bias=m_dt_bias, a_log=m_a_log, d_skip=m_d_skip,
             ssd_norm_g=m_ssd_norm_g, gm_norm_g=m_gm_norm_g, gm_norm_b=m_gm_norm_b,
             w_spatial=m_w_spatial, b_spatial=m_b_spatial, b_gate=m_b_gate, w_ssd_proj=m_w_ssd_proj,
             w_gm_proj=m_w_gm_proj, w_out=m_w_out, ln1_g=m_ln1_g, ln1_b=m_ln1_b, w_ff1=m_w_ff1,
             w_ff3=m_w_ff3, w_ff2=m_w_ff2, ln2_g=m_ln2_g, ln2_b=m_ln2_b)
    V = dict(c_ctx=v_c_ctx, ln0_g=v_ln0_g, ln0_b=v_ln0_b, w_ada=v_w_ada, b_ada=v_b_ada, w_in=v_w_in,
             conv_w=v_conv_w, conv_b=v_conv_b, dt_bias=v_dt_bias, a_log=v_a_log, d_skip=v_d_skip,
             ssd_norm_g=v_ssd_norm_g, gm_norm_g=v_gm_norm_g, gm_norm_b=v_gm_norm_b,
             w_spatial=v_w_spatial, b_spatial=v_b_spatial, b_gate=v_b_gate, w_ssd_proj=v_w_ssd_proj,
             w_gm_proj=v_w_gm_proj, w_out=v_w_out, ln1_g=v_ln1_g, ln1_b=v_ln1_b, w_ff1=v_w_ff1,
             w_ff3=v_w_ff3, w_ff2=v_w_ff2, ln2_g=v_ln2_g, ln2_b=v_ln2_b)

    me = 4 * lax.axis_index("x") + 2 * lax.axis_index("y") + lax.axis_index("c")
    xl, cx, tgt = x[0], ctx[0], loss_target[0]
    L = xl.shape[0]
    assert cx.shape[0] == TL and L % TL == 0
    ada_n = w_ada.shape[2]
    cw_n = conv_w.shape[2]

    small1 = _pad_rows(jnp.concatenate([c, _slab([conv_w[0]], 1)], axis=0), 8)
    g1 = _all_gather(small1, "ag_small")
    c_all = g1[:, 0, :]
    conv_w_full = g1[:, 1, :5 * cw_n].reshape(NDEV, 5, cw_n).transpose(1, 0, 2).reshape(5, NDEV * cw_n)
    big_local = _slab([W[n][0] for n in BIG], BIG_ROWS, BIG_ALIGN).astype(_MXU)
    gw = _all_gather(big_local, "ag_weights")

    def rows_of(n):
        return W[n][0].size // D

    offs, o = {}, 0
    for n in BIG:
        offs[n] = o
        o += _padded(rows_of(n) * D, BIG_ALIGN) // D

    def col_sharded(n):
        k, nc = W[n].shape[1], W[n].shape[2]
        blk = gw[:, offs[n]:offs[n] + rows_of(n), :].reshape(NDEV, k, nc)
        return blk.transpose(1, 0, 2).reshape(k, NDEV * nc)

    def row_sharded(n):
        return gw[:, offs[n]:offs[n] + rows_of(n), :].reshape(-1, D)

    w_in_p = _perm_cols(col_sharded("w_in"))
    w_ssd_f, w_gm_f, w_out_f = row_sharded("w_ssd_proj"), row_sharded("w_gm_proj"), row_sharded("w_out")
    w13 = jnp.concatenate([col_sharded("w_ff1"), col_sharded("w_ff3")], axis=1)
    w_ff2_f = row_sharded("w_ff2")

    c16 = _pad_rows(jnp.concatenate([c_all, _row(c_ctx)], axis=0), 16)
    b_ada_sh = lax.dynamic_slice(b_ada, (0, ada_n * me), (1, ada_n))
    modp = _ada_fwd(c16, w_ada[0], b_ada_sh)
    mod16 = _all_gather(modp, "ag_mod").transpose(1, 0, 2).reshape(16, NDEV * ada_n)
    modx = _pad_rows(lax.dynamic_slice(mod16, (me, 0), (1, 6 * D)).reshape(6, D), 8)
    modc = _pad_rows(mod16[8].reshape(6, D), 8)

    g0, b0 = _row(ln0_g), _row(ln0_b)
    xn, h1 = _ln0_fwd(xl, cx, g0, b0, modx, modc)
    p = _mm(h1, w_in_p, "nn", F32, "mm_p")
    conv_w8 = _pad_rows(conv_w_full, 8)
    xbc = _conv_fwd(p, conv_w8, conv_b)
    prm = _pad_rows(jnp.pad(jnp.stack([dt_bias.reshape(32), a_log.reshape(32)]), ((0, 0), (0, 96))), 8)
    yf, yb, hpf, hpb = _ssd2_fwd(xbc, p, prm)
    dsk = _row(jnp.repeat(d_skip[0, 0] + d_skip[0, 1], HP))
    ws_m = w_spatial[0].astype(_MXU)
    bsT = jnp.pad(b_spatial[0].T, ((0, 0), (0, 120)))
    mixp = (dsk, ssd_norm_g, gm_norm_g, gm_norm_b, ws_m, bsT)
    yssd, ygm = _mix_fwd(yf, yb, p, xbc, *mixp)
    a1 = _mm(yssd, w_ssd_f, "nn", F32, "mm_a1")
    a2 = _mm(ygm, w_gm_f, "nn", F32, "mm_a2")
    merged = _gate_fwd(a1, a2, p, b_gate)
    out = _mm(merged, w_out_f, "nn", F32, "mm_out")
    r1, h2 = _res1_fwd(xn, out, modx, ln1_g, ln1_b)
    f13 = _mm(h2, w13, "nn", F32, "mm_f13")
    ff = _glu_fwd(f13)
    o2 = _mm(ff, w_ff2_f, "nn", F32, "mm_o2")

    dr2, do2, st2, loss_slab = _res2(r1, o2, tgt, modx, ln1_g, ln1_b, ln2_g, ln2_b)
    loss = lax.psum(loss_slab[0, 0], ("x", "y", "c"))
    dff = _mm(do2, w_ff2_f, "nt", F32, "mm_dff")
    df13 = _glu_bwd(dff, f13)
    dh2 = _mm(df13, w13, "nt", F32, "mm_dh2")
    dw_ff2 = _mm(ff, do2, "tn", F32, "mm_dw_ff2")
    dw13 = _mm(h2, df13, "tn", F32, "mm_dw13")
    dr1, dout, st1 = _res1_bwd(dr2, dh2, r1, out, modx, ln1_g, ln1_b)
    dmg = _mm(dout, w_out_f, "nt", F32, "mm_dmerged")
    dw_out = _mm(merged, dout, "tn", F32, "mm_dw_out")
    dp = jnp.zeros((L + TL, NPJ), _MXU)
    dp, da1, da2, stg = _gate_bwd(dmg, a1, a2, p, b_gate, dp)
    dys = _mm(da1, w_ssd_f, "nt", F32, "mm_dyssd")
    dym = _mm(da2, w_gm_f, "nt", F32, "mm_dygm")
    dw_ssd = _mm(yssd, da1, "tn", F32, "mm_dw_ssd")
    dw_gm = _mm(ygm, da2, "tn", F32, "mm_dw_gm")
    dp, dyd, stm, dws, dbsT = _mix_bwd(dys, dym, yf, yb, p, xbc, dp, *mixp)
    dxf, dxb, ddf, ddb, sts = _ssd2_bwd(xbc, p, prm, dsk, dyd, hpf, hpb)
    dp, dcw, dcb = _conv_bwd(dxf, dxb, p, conv_w8, conv_b, dp)
    dp, std = _dt_bwd(ddf, ddb, dp)
    dh1 = _mm(dp, w_in_p, "nt", F32, "mm_dh1")
    dw_in_p = _mm(h1, dp, "tn", F32, "mm_dw_in")
    grad_x, st0 = _ln0_bwd(dh1, dr1, xl, cx, g0, b0, modx, modc)

    zero = jnp.zeros((D,), F32)
    dmod = jnp.stack([jnp.concatenate([st0[0], st0[1], st1[4], st1[1], st1[0], st2[2]]),
                      jnp.concatenate([st0[2], st0[3], zero, zero, zero, zero])])
    g16 = _all_gather(_pad_rows(dmod, 8), "ag_dmod")[:, 0:2, :].reshape(16, 6 * D)
    g16_sh = lax.dynamic_slice(g16, (0, ada_n * me), (16, ada_n))
    c16b = jnp.stack([c_all, jnp.broadcast_to(_row(c_ctx), (NDEV, D))], axis=1).reshape(16, D)
    dw_ada, db_ada8, dcc8 = _ada_bwd(c16b, g16, g16_sh, w_ada[0])

    part = dict(
        c_ctx=dcc8[0], ln0_g=st0[4], ln0_b=st0[5], conv_w=dcw[0:5], conv_b=dcb[0],
        dt_bias=std[0, 0:32], a_log=sts[0, 0:32], d_skip=jnp.tile(sts[1, 0:16], 2),
        ssd_norm_g=stm[0], gm_norm_g=stm[1], gm_norm_b=stm[2], w_spatial=dws,
        b_spatial=dbsT[:, 0:8].T, b_gate=stg[0], ln1_g=st1[2], ln1_b=st1[3], ln2_g=st2[0], ln2_b=st2[1])
    pnames = list(part)
    psum8 = _sum8(_all_gather(_slab([part[n] for n in pnames], SMALL_ROWS), "ag_smallgrads"), "sum_smallgrads")
    small = dict(zip(pnames, _unslab(psum8, [part[n].shape for n in pnames])))
    grads = {n: small[n].reshape(W[n].shape) for n in pnames if n != "conv_w"}
    grads["conv_w"] = lax.dynamic_slice(small["conv_w"], (0, cw_n * me), (5, cw_n)).reshape(conv_w.shape)
    grads["b_ada"] = db_ada8[0:1]
    grads["w_ada"] = dw_ada.reshape(w_ada.shape)

    def aligned(blocks):
        return jnp.pad(blocks, ((0, 0), (0, -blocks.shape[1] % BIG_ALIGN), (0, 0)))

    def to_owner_cols(gfull):
        k, n = gfull.shape
        return aligned(gfull.reshape(k, NDEV, n // NDEV).transpose(1, 0, 2).reshape(NDEV, -1, D))

    def to_owner_rows(gfull):
        return aligned(gfull.reshape(NDEV, -1, D))

    gbig = jnp.concatenate(
        [to_owner_cols(_unperm_cols(dw_in_p)), to_owner_rows(dw_ssd), to_owner_rows(dw_gm),
         to_owner_rows(dw_out), to_owner_cols(dw13[:, :DFF]), to_owner_cols(dw13[:, DFF:]),
         to_owner_rows(dw_ff2)], axis=1)
    gbig = jnp.pad(gbig, ((0, 0), (0, BIG_ROWS - gbig.shape[1]), (0, 0))).astype(_MXU)
    gsum = _sum8(_owner_exchange(gbig, "xchg_grads"), "sum_grads")
    for n, gshard in zip(BIG, _unslab(gsum, [W[n].shape for n in BIG], BIG_ALIGN)):
        grads[n] = gshard

    delta, new_m, new_v = {}, {}, {}

    def adam_group(names, rows, tag, align=0):
        shapes = [W[n].shape for n in names]
        outs = _adamw(*[_slab([src[n] for n in names], rows, align) for src in (grads, W, M, V)], tag)
        for res, slab in zip((delta, new_m, new_v), outs):
            for n, a in zip(names, _unslab(slab, shapes, align)):
                res[n] = a

    adam_group(BIG, BIG_ROWS, "adamw_big", BIG_ALIGN)
    adam_group(["w_ada", "conv_w"], ada_n + 256, "adamw_shard")
    adam_group(REPL, SMALL_ROWS, "adamw_repl")

    return (loss, grad_x[None], *[grads[n] for n in WEIGHTS], *[delta[n] for n in WEIGHTS],
            *[new_m[n] for n in WEIGHTS], *[new_v[n] for n in WEIGHTS])
```

```python
import functools

import jax
import jax.numpy as jnp
from jax import lax
from jax.experimental import pallas as pl
from jax.experimental.pallas import tpu as pltpu

_MXU = jnp.bfloat16
F32 = jnp.float32
D = 1024
TL = 256
Q = 128
NH, HP, NS, HPG = 16, 64, 128, 8
DFF = 2816
ALPHA = 2.0 ** 0.25
EPS = 1e-5
OZ, OU, OV, OXS, OG, OB, OC, ODT, NPJ = 0, 1024, 2048, 3072, 4096, 6144, 6400, 6656, 6912
NNAT = 6688
NDEV = 8
ADAM_LR, ADAM_B1, ADAM_B2, ADAM_EPS, ADAM_WD, ADAM_STEP = 1e-3, 0.9, 0.999, 1e-8, 0.01, 10
VMEM_LIMIT = 48 * 1024 * 1024

NN = ((1,), (0,))
NT = ((1,), (1,))
TN = ((0,), (0,))
MESH = pl.DeviceIdType.MESH


def _dot(a, b, dims):
    return lax.dot_general(a.astype(_MXU), b.astype(_MXU), (dims, ((), ())),
                           preferred_element_type=F32)


def _tile(n, cands):
    for c in cands:
        if n % c == 0:
            return c
    return n


def _divisor_tile(n, cap, mult):
    best = n
    for t in range(mult, min(n, cap) + 1, mult):
        if n % t == 0:
            best = t
    return best


def _params(sem):
    return pltpu.CompilerParams(dimension_semantics=sem, vmem_limit_bytes=VMEM_LIMIT)


def _cst(shape):
    nd = len(shape)
    return pl.BlockSpec(shape, lambda *_: (0,) * nd)


def _rt(w, cb=0, rows=TL):
    return pl.BlockSpec((rows, w), lambda i: (i, cb))


def _rtc(w, nt, cb=0):
    return pl.BlockSpec((TL, w), lambda i: (jnp.minimum(i, nt - 1), cb))


def _sig(x):
    return jax.nn.sigmoid(x)


def _softplus(x):
    return jnp.maximum(x, 0.0) + jnp.log1p(jnp.exp(-jnp.abs(x)))


_G0, _G1 = 0.7978845608028654, 0.044715


def _gelu(x):
    t = jnp.tanh(_G0 * (x + _G1 * x * x * x))
    return 0.5 * x * (1.0 + t), t


def _gelu_grad(x, t):
    return 0.5 * (1.0 + t) + 0.5 * x * (1.0 - t * t) * _G0 * (1.0 + 3.0 * _G1 * x * x)


def _ln(r):
    mu = jnp.mean(r, axis=-1, keepdims=True)
    xc = r - mu
    var = jnp.mean(xc * xc, axis=-1, keepdims=True)
    rstd = lax.rsqrt(var + EPS)
    return xc * rstd, rstd


def _ln_bwd(dyh, xhat, rstd):
    return rstd * (dyh - jnp.mean(dyh, axis=-1, keepdims=True)
                   - xhat * jnp.mean(dyh * xhat, axis=-1, keepdims=True))


def _colsum(v):
    return jnp.sum(v, axis=0, keepdims=True)


def _sum11(v):
    return jnp.sum(jnp.sum(v, axis=1, keepdims=True), axis=0, keepdims=True)


def _cumsum_rows(a, rev):
    n = a.shape[0]
    row = lax.broadcasted_iota(jnp.int32, a.shape, 0)
    s = 1
    while s < n:
        if rev:
            a = a + jnp.where(row < n - s, pltpu.roll(a, n - s, 0), 0.0)
        else:
            a = a + jnp.where(row >= s, pltpu.roll(a, s, 0), 0.0)
        s *= 2
    return a


def _mm(a, b, mode, out_dtype, name):
    if mode == "tn":
        K, M = a.shape
    else:
        M, K = a.shape
    N = b.shape[0] if mode == "nt" else b.shape[1]
    tm = _divisor_tile(M, 1408, 128) if mode == "tn" else _divisor_tile(M, 1088, 16)
    tn = _divisor_tile(N, 1408, 128)
    tk = _divisor_tile(K, 2304, 128)
    nk = K // tk
    dims = {"nn": NN, "nt": NT, "tn": TN}[mode]
    use_acc = nk > 1 and out_dtype != F32

    def body(a_ref, b_ref, o_ref, *acc):
        prod = _dot(a_ref[...], b_ref[...], dims)
        if nk == 1:
            o_ref[...] = prod.astype(o_ref.dtype)
            return
        acc_ref = acc[0] if use_acc else o_ref
        k = pl.program_id(2)

        @pl.when(k == 0)
        def _():
            acc_ref[...] = prod

        if use_acc:
            @pl.when((k > 0) & (k < nk - 1))
            def _():
                acc_ref[...] += prod

            @pl.when(k == nk - 1)
            def _():
                o_ref[...] = (acc_ref[...] + prod).astype(o_ref.dtype)
        else:
            @pl.when(k > 0)
            def _():
                o_ref[...] += prod

    if mode == "tn":
        a_spec = pl.BlockSpec((tk, tm), lambda i, j, k: (k, i))
    else:
        a_spec = pl.BlockSpec((tm, tk), lambda i, j, k: (i, k))
    if mode == "nt":
        b_spec = pl.BlockSpec((tn, tk), lambda i, j, k: (j, k))
    else:
        b_spec = pl.BlockSpec((tk, tn), lambda i, j, k: (k, j))
    return pl.pallas_call(
        body, name=name, grid=(M // tm, N // tn, nk),
        in_specs=[a_spec, b_spec],
        out_specs=pl.BlockSpec((tm, tn), lambda i, j, k: (i, j)),
        out_shape=jax.ShapeDtypeStruct((M, N), out_dtype),
        scratch_shapes=[pltpu.VMEM((tm, tn), F32)] if use_acc else [],
        compiler_params=_params(("parallel", "parallel", "arbitrary")),
    )(a, b)


def _all_gather(x, name):
    def body(x_ref, out_ref, send_sems, recv_sems, local_sem):
        mx, my, mc = lax.axis_index("x"), lax.axis_index("y"), lax.axis_index("c")
        me, sibling = (mx, my, mc), (mx, my, 1 - mc)
        chips = [(1 - mx, my), (mx, 1 - my), (1 - mx, 1 - my)]

        def slot(px, py, pc):
            return out_ref.at[4 * px + 2 * py + pc]

        def copy(k, block, to, src=None):
            return pltpu.make_async_remote_copy(
                src_ref=slot(*block) if src is None else src, dst_ref=slot(*block),
                send_sem=send_sems.at[k], recv_sem=recv_sems.at[k],
                device_id=to, device_id_type=MESH)

        mine = pltpu.make_async_copy(x_ref, slot(*me), local_sem)
        mine.start()
        first = [copy(0, me, sibling, src=x_ref)]
        first += [copy(1 + j, me, (*chip, mc), src=x_ref) for j, chip in enumerate(chips)]
        for cp in first:
            cp.start()
        passed = [copy(4 + j, (*chip, mc), sibling) for j, chip in enumerate(chips)]
        for j, chip in enumerate(chips):
            copy(1 + j, (*chip, mc), me).wait_recv()
            passed[j].start()
        copy(0, sibling, me).wait_recv()
        for j, chip in enumerate(chips):
            copy(4 + j, (*chip, 1 - mc), me).wait_recv()
        for cp in first + passed:
            cp.wait_send()
        mine.wait()

    return pl.pallas_call(
        body, name=name,
        out_shape=jax.ShapeDtypeStruct((NDEV,) + x.shape, x.dtype),
        in_specs=[pl.BlockSpec(memory_space=pl.ANY)],
        out_specs=pl.BlockSpec(memory_space=pl.ANY),
        scratch_shapes=[pltpu.SemaphoreType.DMA((7,)), pltpu.SemaphoreType.DMA((7,)),
                        pltpu.SemaphoreType.DMA],
    )(x)


def _owner_exchange(g, name):
    def body(g_ref, out_ref, send_sems, recv_sems, local_sem):
        mx, my, mc = lax.axis_index("x"), lax.axis_index("y"), lax.axis_index("c")
        local = pltpu.make_async_copy(g_ref.at[4 * mx + 2 * my + mc], out_ref.at[0], local_sem)
        local.start()
        copies = []
        for f in range(1, NDEV):
            px = 1 - mx if (f >> 2) & 1 else mx
            py = 1 - my if (f >> 1) & 1 else my
            pc = 1 - mc if f & 1 else mc
            cp = pltpu.make_async_remote_copy(
                src_ref=g_ref.at[4 * px + 2 * py + pc], dst_ref=out_ref.at[f],
                send_sem=send_sems.at[f - 1], recv_sem=recv_sems.at[f - 1],
                device_id=(px, py, pc), device_id_type=MESH)
            cp.start()
            copies.append(cp)
        for cp in copies:
            cp.wait_recv()
        for cp in copies:
            cp.wait_send()
        local.wait()

    return pl.pallas_call(
        body, name=name,
        out_shape=jax.ShapeDtypeStruct(g.shape, g.dtype),
        in_specs=[pl.BlockSpec(memory_space=pl.ANY)],
        out_specs=pl.BlockSpec(memory_space=pl.ANY),
        scratch_shapes=[pltpu.SemaphoreType.DMA((7,)), pltpu.SemaphoreType.DMA((7,)),
                        pltpu.SemaphoreType.DMA],
    )(g)


def _any_specs(n):
    return [pl.BlockSpec(memory_space=pl.ANY)] * n


def _all_gather_multi(xs, name):
    na = len(xs)

    def body(*refs):
        x_refs, out_refs = refs[:na], refs[na:2 * na]
        send_sems, recv_sems, local_sems = refs[2 * na:]
        mx, my, mc = lax.axis_index("x"), lax.axis_index("y"), lax.axis_index("c")
        me, sibling = (mx, my, mc), (mx, my, 1 - mc)
        chips = [(1 - mx, my), (mx, 1 - my), (1 - mx, 1 - my)]

        def copy(a, k, block, to, src=None):
            slot = out_refs[a].at[4 * block[0] + 2 * block[1] + block[2]]
            return pltpu.make_async_remote_copy(
                src_ref=slot if src is None else src, dst_ref=slot,
                send_sem=send_sems.at[7 * a + k], recv_sem=recv_sems.at[7 * a + k],
                device_id=to, device_id_type=MESH)

        mine = [pltpu.make_async_copy(x_refs[a], out_refs[a].at[4 * mx + 2 * my + mc], local_sems.at[a])
                for a in range(na)]
        for cp in mine:
            cp.start()
        first = []
        for a in range(na):
            first.append(copy(a, 0, me, sibling, src=x_refs[a]))
            first += [copy(a, 1 + j, me, (*chip, mc), src=x_refs[a]) for j, chip in enumerate(chips)]
        for cp in first:
            cp.start()
        passed = []
        for a in range(na):
            for j, chip in enumerate(chips):
                copy(a, 1 + j, (*chip, mc), me).wait_recv()
                fwd = copy(a, 4 + j, (*chip, mc), sibling)
                fwd.start()
                passed.append(fwd)
        for a in range(na):
            copy(a, 0, sibling, me).wait_recv()
            for j, chip in enumerate(chips):
                copy(a, 4 + j, (*chip, 1 - mc), me).wait_recv()
        for cp in first + passed:
            cp.wait_send()
        for cp in mine:
            cp.wait()

    return pl.pallas_call(
        body, name=name,
        out_shape=[jax.ShapeDtypeStruct((NDEV,) + x.shape, x.dtype) for x in xs],
        in_specs=_any_specs(na), out_specs=_any_specs(na),
        scratch_shapes=[pltpu.SemaphoreType.DMA((7 * na,)), pltpu.SemaphoreType.DMA((7 * na,)),
                        pltpu.SemaphoreType.DMA((na,))],
    )(*xs)


def _owner_exchange_multi(gs, name):
    na = len(gs)

    def body(*refs):
        g_refs, out_refs = refs[:na], refs[na:2 * na]
        send_sems, recv_sems, local_sems = refs[2 * na:]
        mx, my, mc = lax.axis_index("x"), lax.axis_index("y"), lax.axis_index("c")
        locals_ = [pltpu.make_async_copy(g_refs[a].at[4 * mx + 2 * my + mc], out_refs[a].at[0], local_sems.at[a])
                   for a in range(na)]
        for cp in locals_:
            cp.start()
        copies = []
        for a in range(na):
            for f in range(1, NDEV):
                px = 1 - mx if (f >> 2) & 1 else mx
                py = 1 - my if (f >> 1) & 1 else my
                pc = 1 - mc if f & 1 else mc
                cp = pltpu.make_async_remote_copy(
                    src_ref=g_refs[a].at[4 * px + 2 * py + pc], dst_ref=out_refs[a].at[f],
                    send_sem=send_sems.at[7 * a + f - 1], recv_sem=recv_sems.at[7 * a + f - 1],
                    device_id=(px, py, pc), device_id_type=MESH)
                cp.start()
                copies.append(cp)
        for cp in copies:
            cp.wait_recv()
        for cp in copies:
            cp.wait_send()
        for cp in locals_:
            cp.wait()

    return pl.pallas_call(
        body, name=name,
        out_shape=[jax.ShapeDtypeStruct(g.shape, g.dtype) for g in gs],
        in_specs=_any_specs(na), out_specs=_any_specs(na),
        scratch_shapes=[pltpu.SemaphoreType.DMA((7 * na,)), pltpu.SemaphoreType.DMA((7 * na,)),
                        pltpu.SemaphoreType.DMA((na,))],
    )(*gs)


def _adamw_sum(r8, w, m, v, row0, tr, name):
    R, C = w.shape
    assert row0 % tr == 0
    blk0 = row0 // tr
    bc1 = 1.0 - ADAM_B1 ** ADAM_STEP
    bc2 = 1.0 - ADAM_B2 ** ADAM_STEP

    def body(r_ref, w_ref, m_ref, v_ref, g_ref, d_ref, mo_ref, vo_ref):
        gg = r_ref[0].astype(F32)
        for k in range(1, NDEV):
            gg = gg + r_ref[k].astype(F32)
        mn = ADAM_B1 * m_ref[...] + (1.0 - ADAM_B1) * gg
        vn = ADAM_B2 * v_ref[...] + (1.0 - ADAM_B2) * (gg * gg)
        mh = mn / bc1
        vh = vn / bc2
        g_ref[...] = gg
        d_ref[...] = -ADAM_LR * (mh / (jnp.sqrt(vh) + ADAM_EPS) + ADAM_WD * w_ref[...])
        mo_ref[...] = mn
        vo_ref[...] = vn

    spec = pl.BlockSpec((tr, C), lambda i: (i, 0))
    sh = jax.ShapeDtypeStruct((R, C), F32)
    return pl.pallas_call(
        body, name=name, grid=(R // tr,),
        in_specs=[pl.BlockSpec((NDEV, tr, C), lambda i: (0, i + blk0, 0)), spec, spec, spec],
        out_specs=[spec] * 4, out_shape=[sh] * 4, compiler_params=_params(("parallel",)),
    )(r8, w, m, v)


def _sum8(r, name):
    _, R, C = r.shape
    tr = _tile(R, (256, 160, 128, 64, 32, 16, 8))

    def body(r_ref, o_ref):
        acc = r_ref[0].astype(F32)
        for k in range(1, NDEV):
            acc = acc + r_ref[k].astype(F32)
        o_ref[...] = acc

    return pl.pallas_call(
        body, name=name, grid=(R // tr,),
        in_specs=[pl.BlockSpec((NDEV, tr, C), lambda i: (0, i, 0))],
        out_specs=pl.BlockSpec((tr, C), lambda i: (i, 0)),
        out_shape=jax.ShapeDtypeStruct((R, C), F32),
        compiler_params=_params(("parallel",)),
    )(r)


def _adamw(g, w, m, v, name):
    R, C = g.shape
    tr = _tile(R, (256, 160, 128, 64, 32, 16, 8))
    bc1 = 1.0 - ADAM_B1 ** ADAM_STEP
    bc2 = 1.0 - ADAM_B2 ** ADAM_STEP

    def body(g_ref, w_ref, m_ref, v_ref, d_ref, mo_ref, vo_ref):
        gg = g_ref[...]
        mn = ADAM_B1 * m_ref[...] + (1.0 - ADAM_B1) * gg
        vn = ADAM_B2 * v_ref[...] + (1.0 - ADAM_B2) * (gg * gg)
        mh = mn / bc1
        vh = vn / bc2
        d_ref[...] = -ADAM_LR * (mh / (jnp.sqrt(vh) + ADAM_EPS) + ADAM_WD * w_ref[...])
        mo_ref[...] = mn
        vo_ref[...] = vn

    spec = pl.BlockSpec((tr, C), lambda i: (i, 0))
    sh = jax.ShapeDtypeStruct((R, C), F32)
    return pl.pallas_call(
        body, name=name, grid=(R // tr,), in_specs=[spec] * 4, out_specs=[spec] * 3,
        out_shape=[sh] * 3, compiler_params=_params(("parallel",)),
    )(g, w, m, v)


def _ada_fwd(c16, w_sh, b_sh):
    def body(c_ref, w_ref, b_ref, o_ref):
        c = c_ref[...]
        o_ref[...] = _dot(c * _sig(c), w_ref[...], NN) + b_ref[...]

    return pl.pallas_call(
        body, name="ada_fwd", out_shape=jax.ShapeDtypeStruct((16, w_sh.shape[1]), F32),
        compiler_params=pltpu.CompilerParams(vmem_limit_bytes=VMEM_LIMIT),
    )(c16, w_sh, b_sh)


def _ada_bwd(c16, g16, g16_sh, w_sh):
    ncol = w_sh.shape[1]

    def body(c_ref, g_ref, gs_ref, w_ref, dw_ref, db_ref, dc_ref):
        c = c_ref[...]
        s = _sig(c)
        gs = gs_ref[...]
        dw_ref[...] = _dot(c * s, gs, TN)
        db_ref[...] = jnp.broadcast_to(_colsum(g_ref[...]), db_ref.shape)
        odd = lax.broadcasted_iota(jnp.int32, gs.shape, 0) % 2 == 1
        gc = _colsum(jnp.where(odd, gs, 0.0))
        ds = _dot(jnp.broadcast_to(gc, (8, ncol)), w_ref[...], NT)
        c1 = c[1:2, :]
        s1 = s[1:2, :]
        dc_ref[...] = ds * (s1 * (1.0 + c1 * (1.0 - s1)))

    return pl.pallas_call(
        body, name="ada_bwd",
        out_shape=[jax.ShapeDtypeStruct(w_sh.shape, F32),
                   jax.ShapeDtypeStruct((8, g16.shape[1]), F32),
                   jax.ShapeDtypeStruct((8, D), F32)],
        compiler_params=pltpu.CompilerParams(vmem_limit_bytes=VMEM_LIMIT),
    )(c16, g16, g16_sh, w_sh)


def _ln0_fwd(x, ctx, g, b, modx, modc):
    L = x.shape[0]
    nt = L // TL

    def body(x_ref, c_ref, g_ref, b_ref, mx_ref, mc_ref, xn_ref, h_ref):
        isc = pl.program_id(0) == nt
        xin = jnp.where(isc, c_ref[...], x_ref[...])
        sh = jnp.where(isc, mc_ref[0:1, :], mx_ref[0:1, :])
        sc = jnp.where(isc, mc_ref[1:2, :], mx_ref[1:2, :])
        xhat, _ = _ln(xin)
        xn = xhat * g_ref[...] + b_ref[...]
        xn_ref[...] = xn
        h_ref[...] = (xn * (1.0 + sc) + sh).astype(h_ref.dtype)

    return pl.pallas_call(
        body, name="ln0_fwd", grid=(nt + 1,),
        in_specs=[_rtc(D, nt), _cst((TL, D)), _cst((1, D)), _cst((1, D)), _cst((8, D)), _cst((8, D))],
        out_specs=[_rt(D), _rt(D)],
        out_shape=[jax.ShapeDtypeStruct((L + TL, D), F32), jax.ShapeDtypeStruct((L + TL, D), _MXU)],
        compiler_params=_params(("parallel",)),
    )(x, ctx, g, b, modx, modc)


def _xbc_colblk(j):
    return jnp.where(j < 8, OXS // 128 + j, OB // 128 + j - 8)


def _conv_taps(p_ref, r0, first, last):
    main = p_ref[pl.ds(r0, TL), :]
    zero = jnp.zeros((8, main.shape[1]), F32)
    prev = zero if first else p_ref[pl.ds(r0 - 8, 8), :]
    nxt = zero if last else p_ref[pl.ds(r0 + TL, 8), :]
    ext = jnp.concatenate([prev, main, nxt], axis=0)
    n = TL + 16
    return [pltpu.roll(ext, (2 - k) % n, 0)[8:8 + TL] for k in range(5)]


def _seq_chunks(L):
    nt = L // TL
    return [(r * TL, r == 0, r == nt - 1) for r in range(nt)] + [(L, True, True)]


def _conv_fwd(p, conv_w8, conv_b):
    RT = p.shape[0]
    L = RT - TL
    chunks = _seq_chunks(L)

    def body(p_ref, w_ref, b_ref, o_ref):
        w = w_ref[...]
        bias = b_ref[...]
        for r0, first, last in chunks:
            taps = _conv_taps(p_ref, r0, first, last)
            pre = bias + sum(w[k:k + 1, :] * taps[k] for k in range(5))
            o_ref[pl.ds(r0, TL), :] = pre * _sig(pre)

    return pl.pallas_call(
        body, name="conv_fwd", grid=(12,),
        in_specs=[pl.BlockSpec((RT, 128), lambda j: (0, _xbc_colblk(j))),
                  pl.BlockSpec((8, 128), lambda j: (0, j)),
                  pl.BlockSpec((1, 128), lambda j: (0, j))],
        out_specs=pl.BlockSpec((RT, 128), lambda j: (0, j)),
        out_shape=jax.ShapeDtypeStruct((RT, 1536), F32),
        compiler_params=_params(("parallel",)),
    )(p, conv_w8, conv_b)


def _ssd_common(dtraw, dtb, a32, rev):
    dt = _softplus(dtraw + dtb)
    acum = _cumsum_rows(dt * a32, rev)
    ii = lax.broadcasted_iota(jnp.int32, (Q, Q), 0)
    jj = lax.broadcasted_iota(jnp.int32, (Q, Q), 1)
    mask = (ii <= jj) if rev else (ii >= jj)
    return dt, acum, acum.T, dt.T, mask


def _ssd_orders(ncl, ncc):
    nc = ncl + ncc

    def cf(s):
        return jnp.where(s < ncc, ncl + s, s - ncc)

    def cb(s):
        return nc - 1 - s

    return cf, cb


def _ssd_fwd(xbc, p, prm):
    RT = xbc.shape[0]
    nc = RT // Q
    ncc = TL // Q
    cf, cb = _ssd_orders(nc - ncc, ncc)

    def one_dir(x_ref, dt_ref, prm_ref, y_ref, hp_ref, H_ref, d):
        rev = d == 1
        a32 = -jnp.exp(prm_ref[1:2, :])
        dt, acum, acumT, dtT, mask = _ssd_common(dt_ref[...], prm_ref[0:1, :], a32, rev)
        end = 0 if rev else Q - 1
        for g in range(2):
            Bg = x_ref[:, D + g * NS:D + (g + 1) * NS]
            Cg = x_ref[:, D + 2 * NS + g * NS:D + 2 * NS + (g + 1) * NS]
            CB = _dot(Cg, Bg, NT)
            for hh in range(HPG):
                h = g * HPG + hh
                ln = 16 * d + h
                col = acum[:, ln:ln + 1]
                rowv = acumT[ln:ln + 1, :]
                a_end = rowv[:, end:end + 1]
                Lm = jnp.exp(jnp.where(mask, col - rowv, -1e30))
                W = CB * Lm * dtT[ln:ln + 1, :]
                Xh = x_ref[:, h * HP:(h + 1) * HP]
                Hp = H_ref[h * HP:(h + 1) * HP, :]
                y = _dot(W, Xh, NN) + jnp.exp(col) * _dot(Cg, Hp, NT)
                y_ref[:, h * HP:(h + 1) * HP] = y
                dcol = jnp.exp(a_end - col) * dt[:, ln:ln + 1]
                hp_ref[0, h * HP:(h + 1) * HP, :] = Hp
                H_ref[h * HP:(h + 1) * HP, :] = jnp.exp(a_end) * Hp + _dot(Xh * dcol, Bg, TN)

    def body(xf_ref, xb_ref, df_ref, db_ref, prm_ref, yf_ref, yb_ref, hf_ref, hb_ref, Hf, Hb):
        @pl.when(pl.program_id(0) == 0)
        def _():
            Hf[...] = jnp.zeros_like(Hf)
            Hb[...] = jnp.zeros_like(Hb)

        one_dir(xf_ref, df_ref, prm_ref, yf_ref, hf_ref, Hf, 0)
        one_dir(xb_ref, db_ref, prm_ref, yb_ref, hb_ref, Hb, 1)

    ysh = jax.ShapeDtypeStruct((RT, D), F32)
    hsh = jax.ShapeDtypeStruct((nc, NH * HP, NS), F32)
    hspec = pl.BlockSpec((1, NH * HP, NS), lambda s: (s, 0, 0))
    return pl.pallas_call(
        body, name="ssd_fwd", grid=(nc,),
        in_specs=[pl.BlockSpec((Q, 1536), lambda s: (cf(s), 0)),
                  pl.BlockSpec((Q, 1536), lambda s: (cb(s), 0)),
                  pl.BlockSpec((Q, 128), lambda s: (cf(s), ODT // 128)),
                  pl.BlockSpec((Q, 128), lambda s: (cb(s), ODT // 128)),
                  _cst((8, 128))],
        out_specs=[pl.BlockSpec((Q, D), lambda s: (cf(s), 0)),
                   pl.BlockSpec((Q, D), lambda s: (cb(s), 0)), hspec, hspec],
        out_shape=[ysh, ysh, hsh, hsh],
        scratch_shapes=[pltpu.VMEM((NH * HP, NS), F32), pltpu.VMEM((NH * HP, NS), F32)],
        compiler_params=_params(("arbitrary",)),
    )(xbc, xbc, p, p, prm)


def _ssd_bwd(xbc, p, prm, dsk, dyd, hpf, hpb):
    RT = xbc.shape[0]
    nc = RT // Q
    ncc = TL // Q
    ncl = nc - ncc
    cf, cb = _ssd_orders(ncl, ncc)

    def rs(t):
        return nc - 1 - t

    def one_dir(x_ref, dt_ref, prm_ref, dsk_ref, dy_ref, is_ctx, hp_ref, dH_ref,
                dx_ref, ddt_ref, st_ref, d):
        rev = d == 1
        a32 = -jnp.exp(prm_ref[1:2, :])
        dtraw = dt_ref[...]
        dtb = prm_ref[0:1, :]
        dt, acum, acumT, dtT, mask = _ssd_common(dtraw, dtb, a32, rev)
        end = 0 if rev else Q - 1
        lane = lax.broadcasted_iota(jnp.int32, (Q, 128), 1)
        srow = lax.broadcasted_iota(jnp.int32, (Q, 128), 0)
        dyscale = jnp.where(is_ctx, 0.0, 1.0)
        c_dacum = jnp.zeros((Q, 128), F32)
        r_dacum = jnp.zeros((Q, 128), F32)
        c_ddt = jnp.zeros((Q, 128), F32)
        r_ddt = jnp.zeros((Q, 128), F32)
        dskacc = jnp.zeros((1, 128), F32)
        for g in range(2):
            Bg = x_ref[:, D + g * NS:D + (g + 1) * NS]
            Cg = x_ref[:, D + 2 * NS + g * NS:D + 2 * NS + (g + 1) * NS]
            CB = _dot(Cg, Bg, NT)
            dCB = jnp.zeros((Q, Q), F32)
            dBg = jnp.zeros((Q, NS), F32)
            dCg = jnp.zeros((Q, NS), F32)
            for hh in range(HPG):
                h = g * HPG + hh
                ln = 16 * d + h
                hs = slice(h * HP, (h + 1) * HP)
                col = acum[:, ln:ln + 1]
                rowv = acumT[ln:ln + 1, :]
                dtr = dtT[ln:ln + 1, :]
                dtc = dt[:, ln:ln + 1]
                a_end = rowv[:, end:end + 1]
                Lm = jnp.exp(jnp.where(mask, col - rowv, -1e30))
                E = jnp.exp(col)
                ecol = jnp.exp(a_end - col)
                dcol = ecol * dtc
                Xh = x_ref[:, hs]
                dY = dy_ref[:, hs] * dyscale
                Hp = hp_ref[0, hs, :]
                dHn = dH_ref[hs, :]
                W = CB * Lm * dtr
                dW = _dot(dY, Xh, NT)
                Mm = dW * CB * Lm
                T = Mm * dtr
                dCB = dCB + dW * Lm * dtr
                BdH = _dot(Bg, dHn, NT)
                dX = _dot(W, dY, TN) + dcol * BdH
                if d == 0:
                    dX = dX + dY * dsk_ref[:, hs]
                    dskacc = dskacc + jnp.where(lane[0:1, :] == h, _sum11(dY * Xh), 0.0)
                dx_ref[:, hs] = dX
                xb = jnp.sum(Xh * BdH, axis=1, keepdims=True)
                scol = dcol * xb
                G = _dot(dY, Hp, NN)
                dCg = dCg + E * G
                qcol = E * jnp.sum(G * Cg, axis=1, keepdims=True)
                dBg = dBg + _dot(Xh * dcol, dHn, NN)
                dH_ref[hs, :] = jnp.exp(a_end) * dHn + _dot(dY * E, Cg, TN)
                eterm = jnp.exp(a_end) * _sum11(dHn * Hp) + _sum11(scol)
                cvec = jnp.sum(T, axis=1, keepdims=True) + qcol - scol
                cvec = cvec + jnp.where(srow[:, 0:1] == end, eterm, 0.0)
                c_dacum = c_dacum + jnp.where(lane == ln, cvec, 0.0)
                r_dacum = r_dacum - jnp.where(srow == ln, _colsum(T), 0.0)
                c_ddt = c_ddt + jnp.where(lane == ln, ecol * xb, 0.0)
                r_ddt = r_ddt + jnp.where(srow == ln, _colsum(Mm), 0.0)
            dBg = dBg + _dot(dCB, Cg, TN)
            dCg = dCg + _dot(dCB, Bg, NN)
            dx_ref[:, D + g * NS:D + (g + 1) * NS] = dBg
            dx_ref[:, D + 2 * NS + g * NS:D + 2 * NS + (g + 1) * NS] = dCg
        dacum = c_dacum + r_dacum.T
        da = _cumsum_rows(dacum, not rev)
        mine = (lane >= 16 * d) & (lane < 16 * d + 16)
        ddt = jnp.where(mine, c_ddt + r_ddt.T + da * a32, 0.0)
        ddt_ref[...] = ddt * _sig(dtraw + dtb)
        st_ref[0:1, :] += _colsum(jnp.where(mine, da * dt, 0.0))
        if d == 0:
            st_ref[1:2, :] += dskacc

    def body(xf_ref, xb_ref, df_ref, db_ref, prm_ref, dsk_ref, dyf_ref, dyb_ref, hf_ref, hb_ref,
             dxf_ref, dxb_ref, ddf_ref, ddb_ref, st_ref, dHf, dHb):
        t = pl.program_id(0)

        @pl.when(t == 0)
        def _():
            dHf[...] = jnp.zeros_like(dHf)
            dHb[...] = jnp.zeros_like(dHb)
            st_ref[...] = jnp.zeros_like(st_ref)

        s = rs(t)
        one_dir(xf_ref, df_ref, prm_ref, dsk_ref, dyf_ref, cf(s) >= ncl, hf_ref, dHf,
                dxf_ref, ddf_ref, st_ref, 0)
        one_dir(xb_ref, db_ref, prm_ref, dsk_ref, dyb_ref, cb(s) >= ncl, hb_ref, dHb,
                dxb_ref, ddb_ref, st_ref, 1)

        @pl.when(t == nc - 1)
        def _():
            st_ref[0:1, :] = -jnp.exp(prm_ref[1:2, :]) * st_ref[0:1, :]

    def lat(c):
        return jnp.minimum(c, ncl - 1)

    xsh = jax.ShapeDtypeStruct((RT, 1536), F32)
    dsh = jax.ShapeDtypeStruct((RT, 128), F32)
    hspec = pl.BlockSpec((1, NH * HP, NS), lambda t: (rs(t), 0, 0))
    return pl.pallas_call(
        body, name="ssd_bwd", grid=(nc,),
        in_specs=[pl.BlockSpec((Q, 1536), lambda t: (cf(rs(t)), 0)),
                  pl.BlockSpec((Q, 1536), lambda t: (cb(rs(t)), 0)),
                  pl.BlockSpec((Q, 128), lambda t: (cf(rs(t)), ODT // 128)),
                  pl.BlockSpec((Q, 128), lambda t: (cb(rs(t)), ODT // 128)),
                  _cst((8, 128)), _cst((1, D)),
                  pl.BlockSpec((Q, D), lambda t: (lat(cf(rs(t))), 0)),
                  pl.BlockSpec((Q, D), lambda t: (lat(cb(rs(t))), 0)),
                  hspec, hspec],
        out_specs=[pl.BlockSpec((Q, 1536), lambda t: (cf(rs(t)), 0)),
                   pl.BlockSpec((Q, 1536), lambda t: (cb(rs(t)), 0)),
                   pl.BlockSpec((Q, 128), lambda t: (cf(rs(t)), 0)),
                   pl.BlockSpec((Q, 128), lambda t: (cb(rs(t)), 0)),
                   _cst((8, 128))],
        out_shape=[xsh, xsh, dsh, dsh, jax.ShapeDtypeStruct((8, 128), F32)],
        scratch_shapes=[pltpu.VMEM((NH * HP, NS), F32), pltpu.VMEM((NH * HP, NS), F32)],
        compiler_params=_params(("arbitrary",)),
    )(xbc, xbc, p, p, prm, dsk, dyd, dyd, hpf, hpb)


def _lane_bcast(v, ln):
    return jnp.broadcast_to(v[:, ln:ln + 1], v.shape)


def _halves(v, lo, axis):
    return jnp.concatenate([jnp.where(lo, v, 0.0), jnp.where(lo, 0.0, v)], axis=axis)


def _ssd2_fwd(xbc, p, prm):
    RT = xbc.shape[0]
    nc = RT // Q
    ncc = TL // Q
    cf, cb = _ssd_orders(nc - ncc, ncc)

    def one_dir(x_ref, dt_ref, prm_ref, y_ref, hp_ref, HT_ref, d):
        rev = d == 1
        a32 = -jnp.exp(prm_ref[1:2, :])
        dt, acum, acumT, dtT, mask = _ssd_common(dt_ref[...], prm_ref[0:1, :], a32, rev)
        end = 0 if rev else Q - 1
        lo = lax.broadcasted_iota(jnp.int32, (Q, 128), 1) < HP
        for g in range(2):
            Bg = x_ref[:, D + g * NS:D + (g + 1) * NS]
            Cg = x_ref[:, D + 2 * NS + g * NS:D + 2 * NS + (g + 1) * NS]
            CB = _dot(Cg, Bg, NT)
            xds, svs = [], []
            for q in range(HPG // 2):
                pi = g * (HPG // 2) + q
                ps = slice(pi * 128, (pi + 1) * 128)
                Xp = x_ref[:, ps]
                HTp = HT_ref[:, ps]
                lhs, dcs, sv = [], [], []
                ces = []
                for h in (2 * pi, 2 * pi + 1):
                    ln = 16 * d + h
                    colB = _lane_bcast(acum, ln)
                    rowv = acumT[ln:ln + 1, :]
                    aend = colB[end:end + 1, :]
                    Lm = jnp.exp(jnp.where(mask, colB - rowv, -1e30))
                    lhs.append(CB * Lm * dtT[ln:ln + 1, :])
                    ces.append(Cg * jnp.exp(colB))
                    dcs.append(jnp.exp(aend - colB) * _lane_bcast(dt, ln))
                    sv.append(jnp.exp(aend))
                lhs = jnp.concatenate(lhs + ces, axis=1)
                rhs = jnp.concatenate([_halves(Xp, lo, 0), _halves(HTp, lo, 0)], axis=0)
                y_ref[:, ps] = _dot(lhs, rhs, NN)
                xds.append(Xp * jnp.where(lo, dcs[0], dcs[1]))
                svs.append(jnp.where(lo[0:1, :], sv[0], sv[1]))
            gs = slice(g * 512, (g + 1) * 512)
            HTg = HT_ref[:, gs]
            hp_ref[0, :, gs] = HTg
            st = _dot(Bg.T, jnp.concatenate(xds, axis=1), NN)
            HT_ref[:, gs] = jnp.concatenate(svs, axis=1) * HTg + st

    def body(xf_ref, xb_ref, df_ref, db_ref, prm_ref, yf_ref, yb_ref, hf_ref, hb_ref, Hf, Hb):
        @pl.when(pl.program_id(0) == 0)
        def _():
            Hf[...] = jnp.zeros_like(Hf)
            Hb[...] = jnp.zeros_like(Hb)

        one_dir(xf_ref, df_ref, prm_ref, yf_ref, hf_ref, Hf, 0)
        one_dir(xb_ref, db_ref, prm_ref, yb_ref, hb_ref, Hb, 1)

    ysh = jax.ShapeDtypeStruct((RT, D), F32)
    hsh = jax.ShapeDtypeStruct((nc, NS, NH * HP), F32)
    hspec = pl.BlockSpec((1, NS, NH * HP), lambda s: (s, 0, 0))
    return pl.pallas_call(
        body, name="ssd_fwd", grid=(nc,),
        in_specs=[pl.BlockSpec((Q, 1536), lambda s: (cf(s), 0)),
                  pl.BlockSpec((Q, 1536), lambda s: (cb(s), 0)),
                  pl.BlockSpec((Q, 128), lambda s: (cf(s), ODT // 128)),
                  pl.BlockSpec((Q, 128), lambda s: (cb(s), ODT // 128)),
                  _cst((8, 128))],
        out_specs=[pl.BlockSpec((Q, D), lambda s: (cf(s), 0)),
                   pl.BlockSpec((Q, D), lambda s: (cb(s), 0)), hspec, hspec],
        out_shape=[ysh, ysh, hsh, hsh],
        scratch_shapes=[pltpu.VMEM((NS, NH * HP), F32), pltpu.VMEM((NS, NH * HP), F32)],
        compiler_params=_params(("arbitrary",)),
    )(xbc, xbc, p, p, prm)


def _ssd2_bwd(xbc, p, prm, dsk, dyd, hpf, hpb):
    RT = xbc.shape[0]
    nc = RT // Q
    ncc = TL // Q
    ncl = nc - ncc
    cf, cb = _ssd_orders(ncl, ncc)

    def rs(t):
        return nc - 1 - t

    def one_dir(x_ref, dt_ref, prm_ref, dsk_ref, dy_ref, is_ctx, hp_ref, dHT_ref,
                dx_ref, ddt_ref, st_ref, d):
        rev = d == 1
        a32 = -jnp.exp(prm_ref[1:2, :])
        dtraw = dt_ref[...]
        dtb = prm_ref[0:1, :]
        dt, acum, acumT, _, _ = _ssd_common(dtraw, dtb, a32, rev)
        end = 0 if rev else Q - 1
        lane = lax.broadcasted_iota(jnp.int32, (Q, 128), 1)
        srow = lax.broadcasted_iota(jnp.int32, (Q, 128), 0)
        maskT = (lane <= srow) if rev else (lane >= srow)
        lo = lane < HP
        lo1 = lo[0:1, :]
        dyscale = jnp.where(is_ctx, 0.0, 1.0)
        c_dacum = jnp.zeros((Q, 128), F32)
        r_dacum = jnp.zeros((Q, 128), F32)
        c_ddt = jnp.zeros((Q, 128), F32)
        dskacc = jnp.zeros((1, 128), F32)
        for g in range(2):
            gs = slice(g * 512, (g + 1) * 512)
            Bg = x_ref[:, D + g * NS:D + (g + 1) * NS]
            Cg = x_ref[:, D + 2 * NS + g * NS:D + 2 * NS + (g + 1) * NS]
            CBT = _dot(Bg, Cg, NT)
            HTg = hp_ref[0, :, gs]
            dHTg = dHT_ref[:, gs]
            BdHg = _dot(Bg, dHTg, NN)
            dCBT = jnp.zeros((Q, Q), F32)
            dCg = jnp.zeros((Q, NS), F32)
            xds, dyes, svs = [], [], []
            for q in range(HPG // 2):
                pi = g * (HPG // 2) + q
                ps = slice(pi * 128, (pi + 1) * 128)
                qs = slice(q * 128, (q + 1) * 128)
                Xp = x_ref[:, ps]
                dYp = dy_ref[:, ps] * dyscale
                HTp = HTg[:, qs]
                BdHp = BdHg[:, qs]
                dY2 = _halves(dYp, lo, 0)
                dWT2 = _dot(_halves(Xp, lo, 0), dYp.T, NN)
                G2 = _dot(dY2, HTp, NT)
                XB = Xp * BdHp
                hh = _colsum(dHTg[:, qs] * HTp)
                yx = _colsum(dYp * Xp)
                wts, dcs, ebs, sv = [], [], [], []
                for k, h in enumerate((2 * pi, 2 * pi + 1)):
                    ln = 16 * d + h
                    half = lo if k == 0 else jnp.logical_not(lo)
                    half1 = half[0:1, :]
                    colB = _lane_bcast(acum, ln)
                    dtcB = _lane_bcast(dt, ln)
                    rowv = acumT[ln:ln + 1, :]
                    aend = colB[end:end + 1, :]
                    LmT = jnp.exp(jnp.where(maskT, rowv - colB, -1e30))
                    WT = CBT * LmT * dtcB
                    dWT = dWT2[k * Q:(k + 1) * Q, :]
                    MT = dWT * CBT * LmT
                    rM = jnp.sum(MT, axis=1, keepdims=True)
                    rT = _colsum(MT * dtcB)
                    dCBT = dCBT + dWT * LmT * dtcB
                    ecol = jnp.exp(aend - colB)
                    EB = jnp.exp(colB)
                    Gk = G2[k * Q:(k + 1) * Q, :]
                    dCg = dCg + EB * Gk
                    qcol = jnp.sum(EB * Gk * Cg, axis=1, keepdims=True)
                    xb = jnp.sum(jnp.where(half, XB, 0.0), axis=1, keepdims=True)
                    e1 = ecol[:, 0:1]
                    dt1 = dtcB[:, 0:1]
                    scol = e1 * dt1 * xb
                    sA = jnp.exp(aend)
                    eterm = sA[:, 0:1] * jnp.sum(jnp.where(half1, hh, 0.0), axis=1, keepdims=True) \
                        + _colsum(scol)
                    cvec = qcol - dt1 * rM - scol + jnp.where(srow[:, 0:1] == end, eterm, 0.0)
                    c_dacum = c_dacum + jnp.where(lane == ln, cvec, 0.0)
                    r_dacum = r_dacum + jnp.where(srow == ln, rT, 0.0)
                    c_ddt = c_ddt + jnp.where(lane == ln, rM + e1 * xb, 0.0)
                    if d == 0:
                        dskacc = dskacc + jnp.where(
                            lane[0:1, :] == h, jnp.sum(jnp.where(half1, yx, 0.0), axis=1, keepdims=True), 0.0)
                    wts.append(WT)
                    dcs.append(ecol * dtcB)
                    ebs.append(EB)
                    sv.append(sA)
                dcp = jnp.where(lo, dcs[0], dcs[1])
                dX = _dot(jnp.concatenate(wts, axis=1), dY2, NN) + dcp * BdHp
                if d == 0:
                    dX = dX + dYp * dsk_ref[:, ps]
                dx_ref[:, ps] = dX
                xds.append(Xp * dcp)
                dyes.append(dYp * jnp.where(lo, ebs[0], ebs[1]))
                svs.append(jnp.where(lo1, sv[0], sv[1]))
            dx_ref[:, D + g * NS:D + (g + 1) * NS] = (
                _dot(jnp.concatenate(xds, axis=1), dHTg, NT) + _dot(dCBT, Cg, NN))
            dx_ref[:, D + 2 * NS + g * NS:D + 2 * NS + (g + 1) * NS] = dCg + _dot(dCBT, Bg, TN)
            dHT_ref[:, gs] = (jnp.concatenate(svs, axis=1) * dHTg
                              + _dot(Cg.T, jnp.concatenate(dyes, axis=1), NN))
        dacum = c_dacum + r_dacum.T
        da = _cumsum_rows(dacum, not rev)
        mine = (lane >= 16 * d) & (lane < 16 * d + 16)
        ddt = jnp.where(mine, c_ddt + da * a32, 0.0)
        ddt_ref[...] = ddt * _sig(dtraw + dtb)
        st_ref[0:1, :] += _colsum(jnp.where(mine, da * dt, 0.0))
        if d == 0:
            st_ref[1:2, :] += dskacc

    def body(xf_ref, xb_ref, df_ref, db_ref, prm_ref, dsk_ref, dyf_ref, dyb_ref, hf_ref, hb_ref,
             dxf_ref, dxb_ref, ddf_ref, ddb_ref, st_ref, dHf, dHb):
        t = pl.program_id(0)

        @pl.when(t == 0)
        def _():
            dHf[...] = jnp.zeros_like(dHf)
            dHb[...] = jnp.zeros_like(dHb)
            st_ref[...] = jnp.zeros_like(st_ref)

        s = rs(t)
        one_dir(xf_ref, df_ref, prm_ref, dsk_ref, dyf_ref, cf(s) >= ncl, hf_ref, dHf,
                dxf_ref, ddf_ref, st_ref, 0)
        one_dir(xb_ref, db_ref, prm_ref, dsk_ref, dyb_ref, cb(s) >= ncl, hb_ref, dHb,
                dxb_ref, ddb_ref, st_ref, 1)

        @pl.when(t == nc - 1)
        def _():
            st_ref[0:1, :] = -jnp.exp(prm_ref[1:2, :]) * st_ref[0:1, :]

    def lat(c):
        return jnp.minimum(c, ncl - 1)

    xsh = jax.ShapeDtypeStruct((RT, 1536), F32)
    dsh = jax.ShapeDtypeStruct((RT, 128), F32)
    hspec = pl.BlockSpec((1, NS, NH * HP), lambda t: (rs(t), 0, 0))
    return pl.pallas_call(
        body, name="ssd_bwd", grid=(nc,),
        in_specs=[pl.BlockSpec((Q, 1536), lambda t: (cf(rs(t)), 0)),
                  pl.BlockSpec((Q, 1536), lambda t: (cb(rs(t)), 0)),
                  pl.BlockSpec((Q, 128), lambda t: (cf(rs(t)), ODT // 128)),
                  pl.BlockSpec((Q, 128), lambda t: (cb(rs(t)), ODT // 128)),
                  _cst((8, 128)), _cst((1, D)),
                  pl.BlockSpec((Q, D), lambda t: (lat(cf(rs(t))), 0)),
                  pl.BlockSpec((Q, D), lambda t: (lat(cb(rs(t))), 0)),
                  hspec, hspec],
        out_specs=[pl.BlockSpec((Q, 1536), lambda t: (cf(rs(t)), 0)),
                   pl.BlockSpec((Q, 1536), lambda t: (cb(rs(t)), 0)),
                   pl.BlockSpec((Q, 128), lambda t: (cf(rs(t)), 0)),
                   pl.BlockSpec((Q, 128), lambda t: (cb(rs(t)), 0)),
                   _cst((8, 128))],
        out_shape=[xsh, xsh, dsh, dsh, jax.ShapeDtypeStruct((8, 128), F32)],
        scratch_shapes=[pltpu.VMEM((NS, NH * HP), F32), pltpu.VMEM((NS, NH * HP), F32)],
        compiler_params=_params(("arbitrary",)),
    )(xbc, xbc, p, p, prm, dsk, dyd, dyd, hpf, hpb)


def _mix_fwd_vals(yf, yb, z, xs, u, v, dsk, sg, gg, gb):
    y = yf + yb + xs * dsk
    sz = _sig(z)
    hh = y * z * sz
    r = lax.rsqrt(jnp.mean(hh * hh, axis=-1, keepdims=True) + EPS)
    nh = hh * r
    ug, tu = _gelu(u)
    vg, tv = _gelu(v)
    vhat, vrstd = _ln(vg)
    vn = vhat * gg + gb
    return y, sz, r, nh, ug, tu, vg, tv, vhat, vrstd, vn


def _mix_fwd(yf, yb, p, xbc, dsk, sg, gg, gb, ws, bsT):
    L = yf.shape[0] - TL
    nt = L // TL

    def body(yf_ref, yb_ref, z_ref, xs_ref, u_ref, v_ref, dsk_ref, sg_ref, gg_ref, gb_ref,
             ws_ref, bs_ref, ys_ref, ym_ref):
        _, _, _, nh, ug, _, _, _, _, _, vn = _mix_fwd_vals(
            yf_ref[...], yb_ref[...], z_ref[...], xs_ref[...], u_ref[...], v_ref[...],
            dsk_ref[...], sg_ref[...], gg_ref[...], gb_ref[...])
        ys_ref[...] = (nh * sg_ref[...]).astype(ys_ref.dtype)
        for n in range(TL // Q):
            rs_ = slice(n * Q, (n + 1) * Q)
            for g in range(8):
                cs = slice(g * 128, (g + 1) * 128)
                mixed = _dot(ws_ref[g], vn[rs_, cs], NN) + bs_ref[:, g:g + 1]
                ym_ref[rs_, cs] = (ug[rs_, cs] * mixed).astype(ym_ref.dtype)

    return pl.pallas_call(
        body, name="mix_fwd", grid=(nt,),
        in_specs=[_rt(D), _rt(D), _rt(D, OZ // D), _rt(D, 0), _rt(D, OU // D), _rt(D, OV // D),
                  _cst((1, D)), _cst((1, D)), _cst((1, D)), _cst((1, D)),
                  _cst((8, 128, 128)), _cst((128, 128))],
        out_specs=[_rt(D), _rt(D)],
        out_shape=[jax.ShapeDtypeStruct((L, D), _MXU), jax.ShapeDtypeStruct((L, D), _MXU)],
        compiler_params=_params(("parallel",)),
    )(yf, yb, p, xbc, p, p, dsk, sg, gg, gb, ws, bsT)


def _mix_bwd(dys, dym, yf, yb, p, xbc, dp, dsk, sg, gg, gb, ws, bsT):
    L = dys.shape[0]
    nt = L // TL

    def body(dys_ref, dym_ref, yf_ref, yb_ref, z_ref, xs_ref, u_ref, v_ref, dsk_ref, sg_ref,
             gg_ref, gb_ref, ws_ref, bs_ref, dp_any, dzuv_ref, dy_ref, st_ref,
             dws_ref, dbs_ref, dvn_s):
        del dp_any
        dz_ref = dzuv_ref.at[:, OZ:OZ + D]
        du_ref = dzuv_ref.at[:, OU:OU + D]
        dv_ref = dzuv_ref.at[:, OV:OV + D]

        @pl.when(pl.program_id(0) == 0)
        def _():
            st_ref[...] = jnp.zeros_like(st_ref)
            dws_ref[...] = jnp.zeros_like(dws_ref)
            dbs_ref[...] = jnp.zeros_like(dbs_ref)

        z = z_ref[...]
        u = u_ref[...]
        v = v_ref[...]
        y, sz, r, nh, ug, tu, vg, tv, vhat, vrstd, vn = _mix_fwd_vals(
            yf_ref[...], yb_ref[...], z, xs_ref[...], u, v,
            dsk_ref[...], sg_ref[...], gg_ref[...], gb_ref[...])
        dys = dys_ref[...]
        st_ref[0:1, :] += _colsum(dys * nh)
        dn = dys * sg_ref[...]
        dhh = r * (dn - nh * jnp.mean(dn * nh, axis=-1, keepdims=True))
        dy_ref[...] = dhh * z * sz
        dz_ref[...] = (dhh * y * (sz * (1.0 + z * (1.0 - sz)))).astype(dz_ref.dtype)
        dym = dym_ref[...]
        lane = lax.broadcasted_iota(jnp.int32, (Q, 128), 1)
        dbs = jnp.zeros((Q, 128), F32)
        gu = _gelu_grad(u, tu)
        for n in range(TL // Q):
            rs_ = slice(n * Q, (n + 1) * Q)
            for g in range(8):
                cs = slice(g * 128, (g + 1) * 128)
                vb = vn[rs_, cs]
                mixed = _dot(ws_ref[g], vb, NN) + bs_ref[:, g:g + 1]
                dyb = dym[rs_, cs]
                dmx = dyb * ug[rs_, cs]
                du_ref[rs_, cs] = (dyb * mixed * gu[rs_, cs]).astype(du_ref.dtype)
                dvn_s[rs_, cs] = _dot(ws_ref[g], dmx, TN)
                dws_ref[g] += _dot(dmx, vb, NT)
                dbs = dbs + jnp.where(lane == g, jnp.sum(dmx, axis=1, keepdims=True), 0.0)
        dbs_ref[...] += dbs
        dvn = dvn_s[...]
        st_ref[1:2, :] += _colsum(dvn * vhat)
        st_ref[2:3, :] += _colsum(dvn)
        dvg = _ln_bwd(dvn * gg_ref[...], vhat, vrstd)
        dv_ref[...] = (dvg * _gelu_grad(v, tv)).astype(dv_ref.dtype)

    outs = pl.pallas_call(
        body, name="mix_bwd", grid=(nt,),
        in_specs=[_rt(D), _rt(D), _rt(D), _rt(D), _rt(D, OZ // D), _rt(D, 0), _rt(D, OU // D),
                  _rt(D, OV // D), _cst((1, D)), _cst((1, D)), _cst((1, D)), _cst((1, D)),
                  _cst((8, 128, 128)), _cst((128, 128)), pl.BlockSpec(memory_space=pl.ANY)],
        out_specs=[_rt(3 * D, 0), _rt(D), _cst((8, D)),
                   _cst((8, 128, 128)), _cst((128, 128))],
        out_shape=[jax.ShapeDtypeStruct(dp.shape, dp.dtype),
                   jax.ShapeDtypeStruct((L, D), F32), jax.ShapeDtypeStruct((8, D), F32),
                   jax.ShapeDtypeStruct((8, 128, 128), F32), jax.ShapeDtypeStruct((128, 128), F32)],
        scratch_shapes=[pltpu.VMEM((TL, D), F32)],
        input_output_aliases={14: 0},
        compiler_params=_params(("arbitrary",)),
    )(dys, dym, yf, yb, p, xbc, p, p, dsk, sg, gg, gb, ws, bsT, dp)
    return outs


def _gate_fwd(a1, a2, p, bg):
    L = a1.shape[0]

    def body(a1_ref, a2_ref, g_ref, bg_ref, m_ref):
        gt = _sig(g_ref[...] + bg_ref[...])
        m_ref[...] = (gt[:, :D] * a1_ref[...] + gt[:, D:] * a2_ref[...]).astype(m_ref.dtype)

    return pl.pallas_call(
        body, name="gate_fwd", grid=(L // TL,),
        in_specs=[_rt(D), _rt(D), _rt(2 * D, OG // (2 * D)), _cst((1, 2 * D))],
        out_specs=_rt(D), out_shape=jax.ShapeDtypeStruct((L, D), _MXU),
        compiler_params=_params(("parallel",)),
    )(a1, a2, p, bg)


def _gate_bwd(dmg, a1, a2, p, bg, dp):
    L = a1.shape[0]

    def body(dm_ref, a1_ref, a2_ref, g_ref, bg_ref, dp_any, dg_ref, da1_ref, da2_ref, st_ref):
        del dp_any

        @pl.when(pl.program_id(0) == 0)
        def _():
            st_ref[...] = jnp.zeros_like(st_ref)

        gt = _sig(g_ref[...] + bg_ref[...])
        g1 = gt[:, :D]
        g2 = gt[:, D:]
        dm = dm_ref[...]
        da1_ref[...] = (dm * g1).astype(da1_ref.dtype)
        da2_ref[...] = (dm * g2).astype(da2_ref.dtype)
        dg1 = dm * a1_ref[...] * g1 * (1.0 - g1)
        dg2 = dm * a2_ref[...] * g2 * (1.0 - g2)
        st_ref[0:1, 0:D] += _colsum(dg1)
        st_ref[0:1, D:2 * D] += _colsum(dg2)
        dg_ref[:, 0:D] = dg1.astype(dg_ref.dtype)
        dg_ref[:, D:2 * D] = dg2.astype(dg_ref.dtype)

    return pl.pallas_call(
        body, name="gate_bwd", grid=(L // TL,),
        in_specs=[_rt(D), _rt(D), _rt(D), _rt(2 * D, OG // (2 * D)), _cst((1, 2 * D)),
                  pl.BlockSpec(memory_space=pl.ANY)],
        out_specs=[_rt(2 * D, OG // (2 * D)), _rt(D), _rt(D), _cst((8, 2 * D))],
        out_shape=[jax.ShapeDtypeStruct(dp.shape, dp.dtype), jax.ShapeDtypeStruct((L, D), _MXU),
                   jax.ShapeDtypeStruct((L, D), _MXU), jax.ShapeDtypeStruct((8, 2 * D), F32)],
        input_output_aliases={5: 0},
        compiler_params=_params(("arbitrary",)),
    )(dmg, a1, a2, p, bg, dp)


def _res1_fwd(xn, out, modx, g, b):
    L = out.shape[0]

    def body(xn_ref, o_ref, mx_ref, g_ref, b_ref, r1_ref, h2_ref):
        r1 = ALPHA * xn_ref[...] + mx_ref[2:3, :] * o_ref[...]
        xhat, _ = _ln(r1)
        x1 = xhat * g_ref[...] + b_ref[...]
        r1_ref[...] = r1
        h2_ref[...] = (x1 * (1.0 + mx_ref[4:5, :]) + mx_ref[3:4, :]).astype(h2_ref.dtype)

    return pl.pallas_call(
        body, name="res1_fwd", grid=(L // TL,),
        in_specs=[_rt(D), _rt(D), _cst((8, D)), _cst((1, D)), _cst((1, D))],
        out_specs=[_rt(D), _rt(D)],
        out_shape=[jax.ShapeDtypeStruct((L, D), F32), jax.ShapeDtypeStruct((L, D), _MXU)],
        compiler_params=_params(("parallel",)),
    )(xn, out, modx, g, b)


def _glu_fwd(f13):
    L = f13.shape[0]

    def body(f1_ref, f3_ref, o_ref):
        f1 = f1_ref[...]
        o_ref[...] = (f1 * _sig(f1) * f3_ref[...]).astype(o_ref.dtype)

    return pl.pallas_call(
        body, name="glu_fwd", grid=(L // TL,),
        in_specs=[_rt(DFF, 0), _rt(DFF, 1)], out_specs=_rt(DFF),
        out_shape=jax.ShapeDtypeStruct((L, DFF), _MXU),
        compiler_params=_params(("parallel",)),
    )(f13, f13)


def _glu_bwd(dff, f13):
    L = f13.shape[0]

    def body(d_ref, f1_ref, f3_ref, o_ref):
        f1 = f1_ref[...]
        s = _sig(f1)
        d = d_ref[...]
        o_ref[:, 0:DFF] = (d * f3_ref[...] * (s * (1.0 + f1 * (1.0 - s)))).astype(o_ref.dtype)
        o_ref[:, DFF:2 * DFF] = (d * f1 * s).astype(o_ref.dtype)

    return pl.pallas_call(
        body, name="glu_bwd", grid=(L // TL,),
        in_specs=[_rt(DFF), _rt(DFF, 0), _rt(DFF, 1)], out_specs=_rt(2 * DFF),
        out_shape=jax.ShapeDtypeStruct((L, 2 * DFF), _MXU),
        compiler_params=_params(("parallel",)),
    )(dff, f13, f13)


def _res2(r1, o2, tgt, modx, g1, b1, g2, b2):
    L = r1.shape[0]

    def body(r1_ref, o2_ref, t_ref, mx_ref, g1_ref, b1_ref, g2_ref, b2_ref,
             dr2_ref, do2_ref, st_ref, loss_ref):
        @pl.when(pl.program_id(0) == 0)
        def _():
            st_ref[...] = jnp.zeros_like(st_ref)
            loss_ref[...] = jnp.zeros_like(loss_ref)

        xh1, _ = _ln(r1_ref[...])
        x1 = xh1 * g1_ref[...] + b1_ref[...]
        o2 = o2_ref[...]
        g2x = mx_ref[5:6, :]
        xh2, rstd2 = _ln(ALPHA * x1 + g2x * o2)
        err = xh2 * g2_ref[...] + b2_ref[...] - t_ref[...]
        per_tok = jnp.mean(err * err, axis=-1, keepdims=True)
        loss_ref[...] += 0.5 * jnp.sum(per_tok, axis=0, keepdims=True)
        dy = err * (1.0 / D)
        st_ref[0:1, :] += _colsum(dy * xh2)
        st_ref[1:2, :] += _colsum(dy)
        dr2 = _ln_bwd(dy * g2_ref[...], xh2, rstd2)
        st_ref[2:3, :] += _colsum(dr2 * o2)
        dr2_ref[...] = dr2
        do2_ref[...] = (g2x * dr2).astype(do2_ref.dtype)

    return pl.pallas_call(
        body, name="res2", grid=(L // TL,),
        in_specs=[_rt(D), _rt(D), _rt(D), _cst((8, D))] + [_cst((1, D))] * 4,
        out_specs=[_rt(D), _rt(D), _cst((8, D)), _cst((8, 128))],
        out_shape=[jax.ShapeDtypeStruct((L, D), F32), jax.ShapeDtypeStruct((L, D), _MXU),
                   jax.ShapeDtypeStruct((8, D), F32), jax.ShapeDtypeStruct((8, 128), F32)],
        compiler_params=_params(("arbitrary",)),
    )(r1, o2, tgt, modx, g1, b1, g2, b2)


def _res1_bwd(dr2, dh2, r1, out, modx, g1, b1):
    L = r1.shape[0]

    def body(dr2_ref, dh2_ref, r1_ref, o_ref, mx_ref, g_ref, b_ref, dr1_ref, do_ref, st_ref):
        @pl.when(pl.program_id(0) == 0)
        def _():
            st_ref[...] = jnp.zeros_like(st_ref)

        xh1, rstd1 = _ln(r1_ref[...])
        x1 = xh1 * g_ref[...] + b_ref[...]
        dh2 = dh2_ref[...]
        dx1 = ALPHA * dr2_ref[...] + dh2 * (1.0 + mx_ref[4:5, :])
        st_ref[0:1, :] += _colsum(dh2 * x1)
        st_ref[1:2, :] += _colsum(dh2)
        st_ref[2:3, :] += _colsum(dx1 * xh1)
        st_ref[3:4, :] += _colsum(dx1)
        dr1 = _ln_bwd(dx1 * g_ref[...], xh1, rstd1)
        st_ref[4:5, :] += _colsum(dr1 * o_ref[...])
        dr1_ref[...] = dr1
        do_ref[...] = (mx_ref[2:3, :] * dr1).astype(do_ref.dtype)

    return pl.pallas_call(
        body, name="res1_bwd", grid=(L // TL,),
        in_specs=[_rt(D), _rt(D), _rt(D), _rt(D), _cst((8, D)), _cst((1, D)), _cst((1, D))],
        out_specs=[_rt(D), _rt(D), _cst((8, D))],
        out_shape=[jax.ShapeDtypeStruct((L, D), F32), jax.ShapeDtypeStruct((L, D), _MXU),
                   jax.ShapeDtypeStruct((8, D), F32)],
        compiler_params=_params(("arbitrary",)),
    )(dr2, dh2, r1, out, modx, g1, b1)


def _conv_bwd(dxf, dxb, p, conv_w8, conv_b, dp):
    RT = p.shape[0]
    chunks = _seq_chunks(RT - TL)

    def body(df_ref, db_ref, p_ref, w_ref, b_ref, dp_any, o_ref, dw_ref, dbias_ref, dpre_s):
        del dp_any
        w = w_ref[...]
        bias = b_ref[...]
        srow = lax.broadcasted_iota(jnp.int32, (8, 128), 0)
        dwacc = jnp.zeros((8, 128), F32)
        dbacc = jnp.zeros((1, 128), F32)
        for r0, first, last in chunks:
            taps = _conv_taps(p_ref, r0, first, last)
            pre = bias + sum(w[k:k + 1, :] * taps[k] for k in range(5))
            s = _sig(pre)
            dpre = (df_ref[pl.ds(r0, TL), :] + db_ref[pl.ds(r0, TL), :]) * (s * (1.0 + pre * (1.0 - s)))
            dpre_s[pl.ds(r0, TL), :] = dpre
            dbacc = dbacc + _colsum(dpre)
            for k in range(5):
                dwacc = dwacc + jnp.where(srow == k, _colsum(dpre * taps[k]), 0.0)
        for r0, first, last in chunks:
            taps = _conv_taps(dpre_s, r0, first, last)
            dx = sum(w[k:k + 1, :] * taps[4 - k] for k in range(5))
            o_ref[pl.ds(r0, TL), :] = dx.astype(o_ref.dtype)
        dw_ref[...] = dwacc
        dbias_ref[...] = jnp.broadcast_to(dbacc, (8, 128))

    cspec = pl.BlockSpec((RT, 128), lambda j: (0, j))
    wspec = pl.BlockSpec((8, 128), lambda j: (0, j))
    return pl.pallas_call(
        body, name="conv_bwd", grid=(12,),
        in_specs=[cspec, cspec, pl.BlockSpec((RT, 128), lambda j: (0, _xbc_colblk(j))),
                  wspec, pl.BlockSpec((1, 128), lambda j: (0, j)), pl.BlockSpec(memory_space=pl.ANY)],
        out_specs=[pl.BlockSpec((RT, 128), lambda j: (0, _xbc_colblk(j))), wspec, wspec],
        out_shape=[jax.ShapeDtypeStruct(dp.shape, dp.dtype), jax.ShapeDtypeStruct((8, 1536), F32),
                   jax.ShapeDtypeStruct((8, 1536), F32)],
        scratch_shapes=[pltpu.VMEM((RT, 128), F32)],
        input_output_aliases={5: 0},
        compiler_params=_params(("parallel",)),
    )(dxf, dxb, p, conv_w8, conv_b, dp)


def _dt_bwd(ddf, ddb, dp):
    RT = ddf.shape[0]

    def body(f_ref, b_ref, dp_any, o_ref, st_ref):
        del dp_any

        @pl.when(pl.program_id(0) == 0)
        def _():
            st_ref[...] = jnp.zeros_like(st_ref)

        s = f_ref[...] + b_ref[...]
        o_ref[...] = s.astype(o_ref.dtype)
        st_ref[0:1, :] += _colsum(s)

    return pl.pallas_call(
        body, name="dt_bwd", grid=(RT // TL,),
        in_specs=[_rt(128), _rt(128), pl.BlockSpec(memory_space=pl.ANY)],
        out_specs=[_rt(128, ODT // 128), _cst((8, 128))],
        out_shape=[jax.ShapeDtypeStruct(dp.shape, dp.dtype), jax.ShapeDtypeStruct((8, 128), F32)],
        input_output_aliases={2: 0},
        compiler_params=_params(("arbitrary",)),
    )(ddf, ddb, dp)


def _ln0_bwd(dh1, dr1, x, ctx, g, b, modx, modc):
    L = x.shape[0]
    nt = L // TL

    def body(dh_ref, dr1_ref, x_ref, c_ref, g_ref, b_ref, mx_ref, mc_ref, gx_ref, st_ref):
        i = pl.program_id(0)
        isc = i == nt

        @pl.when(i == 0)
        def _():
            st_ref[...] = jnp.zeros_like(st_ref)

        xin = jnp.where(isc, c_ref[...], x_ref[...])
        xhat, rstd = _ln(xin)
        xn = xhat * g_ref[...] + b_ref[...]
        sc = jnp.where(isc, mc_ref[1:2, :], mx_ref[1:2, :])
        dh = dh_ref[...]
        lat = jnp.where(isc, 0.0, 1.0)
        dxn = dh * (1.0 + sc) + (lat * ALPHA) * dr1_ref[...]
        tsh = _colsum(dh)
        tsc = _colsum(dh * xn)
        st_ref[0:1, :] += lat * tsh
        st_ref[1:2, :] += lat * tsc
        st_ref[2:3, :] += (1.0 - lat) * tsh
        st_ref[3:4, :] += (1.0 - lat) * tsc
        st_ref[4:5, :] += _colsum(dxn * xhat)
        st_ref[5:6, :] += _colsum(dxn)

        @pl.when(i < nt)
        def _():
            gx_ref[...] = _ln_bwd(dxn * g_ref[...], xhat, rstd)

    return pl.pallas_call(
        body, name="ln0_bwd", grid=(nt + 1,),
        in_specs=[_rt(D), _rtc(D, nt), _rtc(D, nt), _cst((TL, D)), _cst((1, D)), _cst((1, D)),
                  _cst((8, D)), _cst((8, D))],
        out_specs=[_rtc(D, nt), _cst((8, D))],
        out_shape=[jax.ShapeDtypeStruct((L, D), F32), jax.ShapeDtypeStruct((8, D), F32)],
        compiler_params=_params(("arbitrary",)),
    )(dh1, dr1, x, ctx, g, b, modx, modc)


def _perm_cols(w):
    pad = jnp.zeros((w.shape[0], NPJ - NNAT), w.dtype)
    return jnp.concatenate([w[:, 0:1024], w[:, 2592:3616], w[:, 3616:4640], w[:, 1024:2048],
                            w[:, 4640:6688], w[:, 2048:2304], w[:, 2304:2560], w[:, 2560:2592], pad],
                           axis=1)


SECTIONS = ((0, 1024, OZ), (1024, 2048, OXS), (2048, 2304, OB), (2304, 2560, OC), (2560, 2592, ODT),
            (2592, 3616, OU), (3616, 4640, OV), (4640, 6688, OG))


def _perm_from_blocks(ga):
    n = ga.shape[2]
    pieces = []
    for na, nb, _ in sorted(SECTIONS, key=lambda sec: sec[2]):
        for k in range(NDEV):
            lo, hi = max(na, k * n), min(nb, (k + 1) * n)
            if lo < hi:
                pieces.append(ga[k][:, lo - k * n:hi - k * n])
    pieces.append(jnp.zeros((ga.shape[1], NPJ - NNAT), ga.dtype))
    return jnp.concatenate(pieces, axis=1)


def _blocks_from_perm(gp, n):
    blocks = []
    for k in range(NDEV):
        pieces = []
        for na, nb, po in SECTIONS:
            lo, hi = max(na, k * n), min(nb, (k + 1) * n)
            if lo < hi:
                pieces.append(gp[:, po + lo - na:po + hi - na])
        blocks.append(jnp.concatenate(pieces, axis=1))
    return jnp.stack(blocks)


def _padded(n, row_align):
    unit = row_align * D
    return -(-n // unit) * unit if row_align else n


def _slab(arrs, rows, row_align=0):
    parts = []
    for a in arrs:
        f = a.reshape(-1)
        parts.append(jnp.pad(f, (0, _padded(f.shape[0], row_align) - f.shape[0])))
    flat = jnp.concatenate(parts)
    flat = jnp.pad(flat, (0, rows * D - flat.shape[0]))
    return flat.reshape(rows, D)


def _unslab(slab, shapes, row_align=0):
    flat = slab.reshape(-1)
    out, off = [], 0
    for shp in shapes:
        n = 1
        for s in shp:
            n *= s
        out.append(flat[off:off + n].reshape(shp))
        off += _padded(n, row_align)
    return out


def _row(v):
    return v.reshape(1, -1)


def _pad_rows(a, rows):
    return jnp.pad(a, ((0, rows - a.shape[0]), (0, 0)))


BIG = ["w_in", "w_ssd_proj", "w_gm_proj", "w_out", "w_ff1", "w_ff3", "w_ff2"]
BIG_ROWS = 2304
BIG_ALIGN = 16
REPL = ["c_ctx", "ln0_g", "ln0_b", "b_ada", "conv_b", "dt_bias", "a_log", "d_skip", "ssd_norm_g",
        "gm_norm_g", "gm_norm_b", "w_spatial", "b_spatial", "b_gate", "ln1_g", "ln1_b", "ln2_g", "ln2_b"]
SMALL_ROWS = 160
WEIGHTS = ["c_ctx", "ln0_g", "ln0_b", "w_ada", "b_ada", "w_in", "conv_w", "conv_b", "dt_bias", "a_log",
           "d_skip", "ssd_norm_g", "gm_norm_g", "gm_norm_b", "w_spatial", "b_spatial", "b_gate",
           "w_ssd_proj", "w_gm_proj", "w_out", "ln1_g", "ln1_b", "w_ff1", "w_ff3", "w_ff2", "ln2_g", "ln2_b"]


def kernel(x, c, ctx, c_ctx, ln0_g, ln0_b, w_ada, b_ada, w_in, conv_w, conv_b, dt_bias, a_log, d_skip, ssd_norm_g, gm_norm_g, gm_norm_b, w_spatial, b_spatial, b_gate, w_ssd_proj, w_gm_proj, w_out, ln1_g, ln1_b, w_ff1, w_ff3, w_ff2, ln2_g, ln2_b, loss_target, m_c_ctx, m_ln0_g, m_ln0_b, m_w_ada, m_b_ada, m_w_in, m_conv_w, m_conv_b, m_dt_bias, m_a_log, m_d_skip, m_ssd_norm_g, m_gm_norm_g, m_gm_norm_b, m_w_spatial, m_b_spatial, m_b_gate, m_w_ssd_proj, m_w_gm_proj, m_w_out, m_ln1_g, m_ln1_b, m_w_ff1, m_w_ff3, m_w_ff2, m_ln2_g, m_ln2_b, v_c_ctx, v_ln0_g, v_ln0_b, v_w_ada, v_b_ada, v_w_in, v_conv_w, v_conv_b, v_dt_bias, v_a_log, v_d_skip, v_ssd_norm_g, v_gm_norm_g, v_gm_norm_b, v_w_spatial, v_b_spatial, v_b_gate, v_w_ssd_proj, v_w_gm_proj, v_w_out, v_ln1_g, v_ln1_b, v_w_ff1, v_w_ff3, v_w_ff2, v_ln2_g, v_ln2_b):
    W = dict(c_ctx=c_ctx, ln0_g=ln0_g, ln0_b=ln0_b, w_ada=w_ada, b_ada=b_ada, w_in=w_in, conv_w=conv_w,
             conv_b=conv_b, dt_bias=dt_bias, a_log=a_log, d_skip=d_skip, ssd_norm_g=ssd_norm_g,
             gm_norm_g=gm_norm_g, gm_norm_b=gm_norm_b, w_spatial=w_spatial, b_spatial=b_spatial,
             b_gate=b_gate, w_ssd_proj=w_ssd_proj, w_gm_proj=w_gm_proj, w_out=w_out, ln1_g=ln1_g,
             ln1_b=ln1_b, w_ff1=w_ff1, w_ff3=w_ff3, w_ff2=w_ff2, ln2_g=ln2_g, ln2_b=ln2_b)
    M = dict(c_ctx=m_c_ctx, ln0_g=m_ln0_g, ln0_b=m_ln0_b, w_ada=m_w_ada, b_ada=m_b_ada, w_in=m_w_in,
             conv_w=m_conv_w, conv_b=m_conv_b, dt_bias=m_dt_bias, a_log=m_a_log, d_skip=m_d_skip,
             ssd_norm_g=m_ssd_norm_g, gm_norm_g=m_gm_norm_g, gm_norm_b=m_gm_norm_b,
             w_spatial=m_w_spatial, b_spatial=m_b_spatial, b_gate=m_b_gate, w_ssd_proj=m_w_ssd_proj,
             w_gm_proj=m_w_gm_proj, w_out=m_w_out, ln1_g=m_ln1_g, ln1_b=m_ln1_b, w_ff1=m_w_ff1,
             w_ff3=m_w_ff3, w_ff2=m_w_ff2, ln2_g=m_ln2_g, ln2_b=m_ln2_b)
    V = dict(c_ctx=v_c_ctx, ln0_g=v_ln0_g, ln0_b=v_ln0_b, w_ada=v_w_ada, b_ada=v_b_ada, w_in=v_w_in,
             conv_w=v_conv_w, conv_b=v_conv_b, dt_bias=v_dt_bias, a_log=v_a_log, d_skip=v_d_skip,
             ssd_norm_g=v_ssd_norm_g, gm_norm_g=v_gm_norm_g, gm_norm_b=v_gm_norm_b,
             w_spatial=v_w_spatial, b_spatial=v_b_spatial, b_gate=v_b_gate, w_ssd_proj=v_w_ssd_proj,
             w_gm_proj=v_w_gm_proj, w_out=v_w_out, ln1_g=v_ln1_g, ln1_b=v_ln1_b, w_ff1=v_w_ff1,
             w_ff3=v_w_ff3, w_ff2=v_w_ff2, ln2_g=v_ln2_g, ln2_b=v_ln2_b)

    me = 4 * lax.axis_index("x") + 2 * lax.axis_index("y") + lax.axis_index("c")
    xl, cx, tgt = x[0], ctx[0], loss_target[0]
    L = xl.shape[0]
    assert cx.shape[0] == TL and L % TL == 0
    ada_n = w_ada.shape[2]
    cw_n = conv_w.shape[2]

    small1 = _pad_rows(jnp.concatenate([c, _slab([conv_w[0]], 1)], axis=0), 8)
    g1 = _all_gather(small1, "ag_small")
    c_all = g1[:, 0, :]
    conv_w_full = g1[:, 1, :5 * cw_n].reshape(NDEV, 5, cw_n).transpose(1, 0, 2).reshape(5, NDEV * cw_n)
    sq = w_ssd_proj.shape[1]
    ffr = w_ff2.shape[1]
    ffc = w_ff1.shape[2]
    rows_local = jnp.concatenate([w_ssd_proj[0], w_gm_proj[0], w_out[0], w_ff2[0]], axis=0)
    ga, gb, gc1, gc2 = _all_gather_multi(
        [w_in[0].astype(_MXU), rows_local.astype(_MXU), w_ff1[0].astype(_MXU), w_ff3[0].astype(_MXU)],
        "ag_weights")
    w_in_p = _perm_from_blocks(ga)
    w_ssd_f = gb[:, 0:sq].reshape(NDEV * sq, D)
    w_gm_f = gb[:, sq:2 * sq].reshape(NDEV * sq, D)
    w_out_f = gb[:, 2 * sq:3 * sq].reshape(NDEV * sq, D)
    w_ff2_f = gb[:, 3 * sq:3 * sq + ffr].reshape(NDEV * ffr, D)
    w13 = jnp.concatenate([gc1[k] for k in range(NDEV)] + [gc2[k] for k in range(NDEV)], axis=1)

    c16 = _pad_rows(jnp.concatenate([c_all, _row(c_ctx)], axis=0), 16)
    b_ada_sh = lax.dynamic_slice(b_ada, (0, ada_n * me), (1, ada_n))
    modp = _ada_fwd(c16, w_ada[0], b_ada_sh)
    mod16 = _all_gather(modp, "ag_mod").transpose(1, 0, 2).reshape(16, NDEV * ada_n)
    modx = _pad_rows(lax.dynamic_slice(mod16, (me, 0), (1, 6 * D)).reshape(6, D), 8)
    modc = _pad_rows(mod16[8].reshape(6, D), 8)

    g0, b0 = _row(ln0_g), _row(ln0_b)
    xn, h1 = _ln0_fwd(xl, cx, g0, b0, modx, modc)
    p = _mm(h1, w_in_p, "nn", F32, "mm_p")
    conv_w8 = _pad_rows(conv_w_full, 8)
    xbc = _conv_fwd(p, conv_w8, conv_b)
    prm = _pad_rows(jnp.pad(jnp.stack([dt_bias.reshape(32), a_log.reshape(32)]), ((0, 0), (0, 96))), 8)
    yf, yb, hpf, hpb = _ssd2_fwd(xbc, p, prm)
    dsk = _row(jnp.repeat(d_skip[0, 0] + d_skip[0, 1], HP))
    ws_m = w_spatial[0].astype(_MXU)
    bsT = jnp.pad(b_spatial[0].T, ((0, 0), (0, 120)))
    mixp = (dsk, ssd_norm_g, gm_norm_g, gm_norm_b, ws_m, bsT)
    yssd, ygm = _mix_fwd(yf, yb, p, xbc, *mixp)
    a1 = _mm(yssd, w_ssd_f, "nn", F32, "mm_a1")
    a2 = _mm(ygm, w_gm_f, "nn", F32, "mm_a2")
    merged = _gate_fwd(a1, a2, p, b_gate)
    out = _mm(merged, w_out_f, "nn", F32, "mm_out")
    r1, h2 = _res1_fwd(xn, out, modx, ln1_g, ln1_b)
    f13 = _mm(h2, w13, "nn", F32, "mm_f13")
    ff = _glu_fwd(f13)
    o2 = _mm(ff, w_ff2_f, "nn", F32, "mm_o2")

    dr2, do2, st2, loss_slab = _res2(r1, o2, tgt, modx, ln1_g, ln1_b, ln2_g, ln2_b)
    loss = lax.psum(loss_slab[0, 0], ("x", "y", "c"))
    dff = _mm(do2, w_ff2_f, "nt", F32, "mm_dff")
    df13 = _glu_bwd(dff, f13)
    dh2 = _mm(df13, w13, "nt", F32, "mm_dh2")
    dw_ff2 = _mm(ff, do2, "tn", _MXU, "mm_dw_ff2")
    dw13 = _mm(h2, df13, "tn", _MXU, "mm_dw13")
    dr1, dout, st1 = _res1_bwd(dr2, dh2, r1, out, modx, ln1_g, ln1_b)
    dmg = _mm(dout, w_out_f, "nt", F32, "mm_dmerged")
    dw_out = _mm(merged, dout, "tn", _MXU, "mm_dw_out")
    dp = jnp.zeros((L + TL, NPJ), _MXU)
    dp, da1, da2, stg = _gate_bwd(dmg, a1, a2, p, b_gate, dp)
    dys = _mm(da1, w_ssd_f, "nt", F32, "mm_dyssd")
    dym = _mm(da2, w_gm_f, "nt", F32, "mm_dygm")
    dw_ssd = _mm(yssd, da1, "tn", _MXU, "mm_dw_ssd")
    dw_gm = _mm(ygm, da2, "tn", _MXU, "mm_dw_gm")
    dp, dyd, stm, dws, dbsT = _mix_bwd(dys, dym, yf, yb, p, xbc, dp, *mixp)
    dxf, dxb, ddf, ddb, sts = _ssd2_bwd(xbc, p, prm, dsk, dyd, hpf, hpb)
    dp, dcw, dcb = _conv_bwd(dxf, dxb, p, conv_w8, conv_b, dp)
    dp, std = _dt_bwd(ddf, ddb, dp)
    dh1 = _mm(dp, w_in_p, "nt", F32, "mm_dh1")
    dw_in_p = _mm(h1, dp, "tn", _MXU, "mm_dw_in")
    grad_x, st0 = _ln0_bwd(dh1, dr1, xl, cx, g0, b0, modx, modc)

    zero = jnp.zeros((D,), F32)
    dmod = jnp.stack([jnp.concatenate([st0[0], st0[1], st1[4], st1[1], st1[0], st2[2]]),
                      jnp.concatenate([st0[2], st0[3], zero, zero, zero, zero])])
    g16 = _all_gather(_pad_rows(dmod, 8), "ag_dmod")[:, 0:2, :].reshape(16, 6 * D)
    g16_sh = lax.dynamic_slice(g16, (0, ada_n * me), (16, ada_n))
    c16b = jnp.stack([c_all, jnp.broadcast_to(_row(c_ctx), (NDEV, D))], axis=1).reshape(16, D)
    dw_ada, db_ada8, dcc8 = _ada_bwd(c16b, g16, g16_sh, w_ada[0])

    part = dict(
        c_ctx=dcc8[0], ln0_g=st0[4], ln0_b=st0[5], conv_w=dcw[0:5], conv_b=dcb[0],
        dt_bias=std[0, 0:32], a_log=sts[0, 0:32], d_skip=jnp.tile(sts[1, 0:16], 2),
        ssd_norm_g=stm[0], gm_norm_g=stm[1], gm_norm_b=stm[2], w_spatial=dws,
        b_spatial=dbsT[:, 0:8].T, b_gate=stg[0], ln1_g=st1[2], ln1_b=st1[3], ln2_g=st2[0], ln2_b=st2[1])
    pnames = list(part)
    psum8 = _sum8(_all_gather(_slab([part[n] for n in pnames], SMALL_ROWS), "ag_smallgrads"), "sum_smallgrads")
    small = dict(zip(pnames, _unslab(psum8, [part[n].shape for n in pnames])))
    grads = {n: small[n].reshape(W[n].shape) for n in pnames if n != "conv_w"}
    grads["conv_w"] = lax.dynamic_slice(small["conv_w"], (0, cw_n * me), (5, cw_n)).reshape(conv_w.shape)
    grads["b_ada"] = db_ada8[0:1]
    grads["w_ada"] = dw_ada.reshape(w_ada.shape)

    xa = _blocks_from_perm(dw_in_p, w_in.shape[2])
    xb = jnp.concatenate([dw_ssd.reshape(NDEV, sq, D), dw_gm.reshape(NDEV, sq, D),
                          dw_out.reshape(NDEV, sq, D), dw_ff2.reshape(NDEV, ffr, D)], axis=1)
    xc1 = jnp.stack([dw13[:, k * ffc:(k + 1) * ffc] for k in range(NDEV)])
    xc3 = jnp.stack([dw13[:, DFF + k * ffc:DFF + (k + 1) * ffc] for k in range(NDEV)])
    ra, rb, rc1, rc3 = _owner_exchange_multi([xa, xb, xc1, xc3], "xchg_grads")

    delta, new_m, new_v = {}, {}, {}
    for n, r8, row0, tr in (("w_in", ra, 0, 256), ("w_ssd_proj", rb, 0, sq), ("w_gm_proj", rb, sq, sq),
                            ("w_out", rb, 2 * sq, sq), ("w_ff2", rb, 3 * sq, 32),
                            ("w_ff1", rc1, 0, 256), ("w_ff3", rc3, 0, 256)):
        res = _adamw_sum(r8, W[n][0], M[n][0], V[n][0], row0, tr, "adamw_" + n)
        grads[n], delta[n], new_m[n], new_v[n] = [a[None] for a in res]

    def adam_group(names, rows, tag, align=0):
        shapes = [W[n].shape for n in names]
        outs = _adamw(*[_slab([src[n] for n in names], rows, align) for src in (grads, W, M, V)], tag)
        for res, slab in zip((delta, new_m, new_v), outs):
            for n, a in zip(names, _unslab(slab, shapes, align)):
                res[n] = a

    adam_group(["w_ada", "conv_w"], ada_n + 256, "adamw_shard")
    adam_group(REPL, SMALL_ROWS, "adamw_repl")

    return (loss, grad_x[None], *[grads[n] for n in WEIGHTS], *[delta[n] for n in WEIGHTS],
            *[new_m[n] for n in WEIGHTS], *[new_v[n] for n in WEIGHTS])
```

```python
import functools

import jax
import jax.numpy as jnp
from jax import lax
from jax.experimental import pallas as pl
from jax.experimental.pallas import tpu as pltpu

_MXU = jnp.bfloat16
F32 = jnp.float32
D = 1024
TL = 256
Q = 128
NH, HP, NS, HPG = 16, 64, 128, 8
DFF = 2816
ALPHA = 2.0 ** 0.25
EPS = 1e-5
OZ, OU, OV, OXS, OG, OB, OC, ODT, NPJ = 0, 1024, 2048, 3072, 4096, 6144, 6400, 6656, 6912
NNAT = 6688
NDEV = 8
ADAM_LR, ADAM_B1, ADAM_B2, ADAM_EPS, ADAM_WD, ADAM_STEP = 1e-3, 0.9, 0.999, 1e-8, 0.01, 10
VMEM_LIMIT = 48 * 1024 * 1024

NN = ((1,), (0,))
NT = ((1,), (1,))
TN = ((0,), (0,))
MESH = pl.DeviceIdType.MESH


def _dot(a, b, dims):
    return lax.dot_general(a.astype(_MXU), b.astype(_MXU), (dims, ((), ())),
                           preferred_element_type=F32)


def _tile(n, cands):
    for c in cands:
        if n % c == 0:
            return c
    return n


def _divisor_tile(n, cap, mult):
    best = n
    for t in range(mult, min(n, cap) + 1, mult):
        if n % t == 0:
            best = t
    return best


def _params(sem):
    return pltpu.CompilerParams(dimension_semantics=sem, vmem_limit_bytes=VMEM_LIMIT)


def _cst(shape):
    nd = len(shape)
    return pl.BlockSpec(shape, lambda *_: (0,) * nd)


def _rt(w, cb=0, rows=TL):
    return pl.BlockSpec((rows, w), lambda i: (i, cb))


def _rtc(w, nt, cb=0):
    return pl.BlockSpec((TL, w), lambda i: (jnp.minimum(i, nt - 1), cb))


def _sig(x):
    return jax.nn.sigmoid(x)


def _softplus(x):
    return jnp.maximum(x, 0.0) + jnp.log1p(jnp.exp(-jnp.abs(x)))


_G0, _G1 = 0.7978845608028654, 0.044715


def _gelu(x):
    t = jnp.tanh(_G0 * (x + _G1 * x * x * x))
    return 0.5 * x * (1.0 + t), t


def _gelu_grad(x, t):
    return 0.5 * (1.0 + t) + 0.5 * x * (1.0 - t * t) * _G0 * (1.0 + 3.0 * _G1 * x * x)


def _ln(r):
    mu = jnp.mean(r, axis=-1, keepdims=True)
    xc = r - mu
    var = jnp.mean(xc * xc, axis=-1, keepdims=True)
    rstd = lax.rsqrt(var + EPS)
    return xc * rstd, rstd


def _ln_bwd(dyh, xhat, rstd):
    return rstd * (dyh - jnp.mean(dyh, axis=-1, keepdims=True)
                   - xhat * jnp.mean(dyh * xhat, axis=-1, keepdims=True))


def _colsum(v):
    return jnp.sum(v, axis=0, keepdims=True)


def _sum11(v):
    return jnp.sum(jnp.sum(v, axis=1, keepdims=True), axis=0, keepdims=True)


def _cumsum_rows(a, rev):
    n = a.shape[0]
    row = lax.broadcasted_iota(jnp.int32, a.shape, 0)
    s = 1
    while s < n:
        if rev:
            a = a + jnp.where(row < n - s, pltpu.roll(a, n - s, 0), 0.0)
        else:
            a = a + jnp.where(row >= s, pltpu.roll(a, s, 0), 0.0)
        s *= 2
    return a


def _mm(a, b, mode, out_dtype, name):
    if mode == "tn":
        K, M = a.shape
    else:
        M, K = a.shape
    N = b.shape[0] if mode == "nt" else b.shape[1]
    tm = _divisor_tile(M, 1408, 128) if mode == "tn" else _divisor_tile(M, 1088, 16)
    tn = _divisor_tile(N, 1408, 128)
    tk = _divisor_tile(K, 2304, 128)
    nk = K // tk
    dims = {"nn": NN, "nt": NT, "tn": TN}[mode]
    use_acc = nk > 1 and out_dtype != F32

    def body(a_ref, b_ref, o_ref, *acc):
        prod = _dot(a_ref[...], b_ref[...], dims)
        if nk == 1:
            o_ref[...] = prod.astype(o_ref.dtype)
            return
        acc_ref = acc[0] if use_acc else o_ref
        k = pl.program_id(2)

        @pl.when(k == 0)
        def _():
            acc_ref[...] = prod

        if use_acc:
            @pl.when((k > 0) & (k < nk - 1))
            def _():
                acc_ref[...] += prod

            @pl.when(k == nk - 1)
            def _():
                o_ref[...] = (acc_ref[...] + prod).astype(o_ref.dtype)
        else:
            @pl.when(k > 0)
            def _():
                o_ref[...] += prod

    if mode == "tn":
        a_spec = pl.BlockSpec((tk, tm), lambda i, j, k: (k, i))
    else:
        a_spec = pl.BlockSpec((tm, tk), lambda i, j, k: (i, k))
    if mode == "nt":
        b_spec = pl.BlockSpec((tn, tk), lambda i, j, k: (j, k))
    else:
        b_spec = pl.BlockSpec((tk, tn), lambda i, j, k: (k, j))
    return pl.pallas_call(
        body, name=name, grid=(M // tm, N // tn, nk),
        in_specs=[a_spec, b_spec],
        out_specs=pl.BlockSpec((tm, tn), lambda i, j, k: (i, j)),
        out_shape=jax.ShapeDtypeStruct((M, N), out_dtype),
        scratch_shapes=[pltpu.VMEM((tm, tn), F32)] if use_acc else [],
        compiler_params=_params(("parallel", "parallel", "arbitrary")),
    )(a, b)


def _all_gather(x, name):
    def body(x_ref, out_ref, send_sems, recv_sems, local_sem):
        mx, my, mc = lax.axis_index("x"), lax.axis_index("y"), lax.axis_index("c")
        me, sibling = (mx, my, mc), (mx, my, 1 - mc)
        chips = [(1 - mx, my), (mx, 1 - my), (1 - mx, 1 - my)]

        def slot(px, py, pc):
            return out_ref.at[4 * px + 2 * py + pc]

        def copy(k, block, to, src=None):
            return pltpu.make_async_remote_copy(
                src_ref=slot(*block) if src is None else src, dst_ref=slot(*block),
                send_sem=send_sems.at[k], recv_sem=recv_sems.at[k],
                device_id=to, device_id_type=MESH)

        mine = pltpu.make_async_copy(x_ref, slot(*me), local_sem)
        mine.start()
        first = [copy(0, me, sibling, src=x_ref)]
        first += [copy(1 + j, me, (*chip, mc), src=x_ref) for j, chip in enumerate(chips)]
        for cp in first:
            cp.start()
        passed = [copy(4 + j, (*chip, mc), sibling) for j, chip in enumerate(chips)]
        for j, chip in enumerate(chips):
            copy(1 + j, (*chip, mc), me).wait_recv()
            passed[j].start()
        copy(0, sibling, me).wait_recv()
        for j, chip in enumerate(chips):
            copy(4 + j, (*chip, 1 - mc), me).wait_recv()
        for cp in first + passed:
            cp.wait_send()
        mine.wait()

    return pl.pallas_call(
        body, name=name,
        out_shape=jax.ShapeDtypeStruct((NDEV,) + x.shape, x.dtype),
        in_specs=[pl.BlockSpec(memory_space=pl.ANY)],
        out_specs=pl.BlockSpec(memory_space=pl.ANY),
        scratch_shapes=[pltpu.SemaphoreType.DMA((7,)), pltpu.SemaphoreType.DMA((7,)),
                        pltpu.SemaphoreType.DMA],
    )(x)


def _owner_exchange(g, name):
    def body(g_ref, out_ref, send_sems, recv_sems, local_sem):
        mx, my, mc = lax.axis_index("x"), lax.axis_index("y"), lax.axis_index("c")
        local = pltpu.make_async_copy(g_ref.at[4 * mx + 2 * my + mc], out_ref.at[0], local_sem)
        local.start()
        copies = []
        for f in range(1, NDEV):
            px = 1 - mx if (f >> 2) & 1 else mx
            py = 1 - my if (f >> 1) & 1 else my
            pc = 1 - mc if f & 1 else mc
            cp = pltpu.make_async_remote_copy(
                src_ref=g_ref.at[4 * px + 2 * py + pc], dst_ref=out_ref.at[f],
                send_sem=send_sems.at[f - 1], recv_sem=recv_sems.at[f - 1],
                device_id=(px, py, pc), device_id_type=MESH)
            cp.start()
            copies.append(cp)
        for cp in copies:
            cp.wait_recv()
        for cp in copies:
            cp.wait_send()
        local.wait()

    return pl.pallas_call(
        body, name=name,
        out_shape=jax.ShapeDtypeStruct(g.shape, g.dtype),
        in_specs=[pl.BlockSpec(memory_space=pl.ANY)],
        out_specs=pl.BlockSpec(memory_space=pl.ANY),
        scratch_shapes=[pltpu.SemaphoreType.DMA((7,)), pltpu.SemaphoreType.DMA((7,)),
                        pltpu.SemaphoreType.DMA],
    )(g)


def _any_specs(n):
    return [pl.BlockSpec(memory_space=pl.ANY)] * n


def _all_gather_multi(xs, name):
    na = len(xs)

    def body(*refs):
        x_refs, out_refs = refs[:na], refs[na:2 * na]
        send_sems, recv_sems, local_sems = refs[2 * na:]
        mx, my, mc = lax.axis_index("x"), lax.axis_index("y"), lax.axis_index("c")
        me, sibling = (mx, my, mc), (mx, my, 1 - mc)
        chips = [(1 - mx, my), (mx, 1 - my), (1 - mx, 1 - my)]

        def copy(a, k, block, to, src=None):
            slot = out_refs[a].at[4 * block[0] + 2 * block[1] + block[2]]
            return pltpu.make_async_remote_copy(
                src_ref=slot if src is None else src, dst_ref=slot,
                send_sem=send_sems.at[7 * a + k], recv_sem=recv_sems.at[7 * a + k],
                device_id=to, device_id_type=MESH)

        mine = [pltpu.make_async_copy(x_refs[a], out_refs[a].at[4 * mx + 2 * my + mc], local_sems.at[a])
                for a in range(na)]
        for cp in mine:
            cp.start()
        first = []
        for a in range(na):
            first.append(copy(a, 0, me, sibling, src=x_refs[a]))
            first += [copy(a, 1 + j, me, (*chip, mc), src=x_refs[a]) for j, chip in enumerate(chips)]
        for cp in first:
            cp.start()
        passed = []
        for a in range(na):
            for j, chip in enumerate(chips):
                copy(a, 1 + j, (*chip, mc), me).wait_recv()
                fwd = copy(a, 4 + j, (*chip, mc), sibling)
                fwd.start()
                passed.append(fwd)
        for a in range(na):
            copy(a, 0, sibling, me).wait_recv()
            for j, chip in enumerate(chips):
                copy(a, 4 + j, (*chip, 1 - mc), me).wait_recv()
        for cp in first + passed:
            cp.wait_send()
        for cp in mine:
            cp.wait()

    return pl.pallas_call(
        body, name=name,
        out_shape=[jax.ShapeDtypeStruct((NDEV,) + x.shape, x.dtype) for x in xs],
        in_specs=_any_specs(na), out_specs=_any_specs(na),
        scratch_shapes=[pltpu.SemaphoreType.DMA((7 * na,)), pltpu.SemaphoreType.DMA((7 * na,)),
                        pltpu.SemaphoreType.DMA((na,))],
    )(*xs)


def _owner_exchange_multi(gs, name):
    na = len(gs)

    def body(*refs):
        g_refs, out_refs = refs[:na], refs[na:2 * na]
        send_sems, recv_sems, local_sems = refs[2 * na:]
        mx, my, mc = lax.axis_index("x"), lax.axis_index("y"), lax.axis_index("c")
        locals_ = [pltpu.make_async_copy(g_refs[a].at[4 * mx + 2 * my + mc], out_refs[a].at[0], local_sems.at[a])
                   for a in range(na)]
        for cp in locals_:
            cp.start()
        copies = []
        for a in range(na):
            for f in range(1, NDEV):
                px = 1 - mx if (f >> 2) & 1 else mx
                py = 1 - my if (f >> 1) & 1 else my
                pc = 1 - mc if f & 1 else mc
                cp = pltpu.make_async_remote_copy(
                    src_ref=g_refs[a].at[4 * px + 2 * py + pc], dst_ref=out_refs[a].at[f],
                    send_sem=send_sems.at[7 * a + f - 1], recv_sem=recv_sems.at[7 * a + f - 1],
                    device_id=(px, py, pc), device_id_type=MESH)
                cp.start()
                copies.append(cp)
        for cp in copies:
            cp.wait_recv()
        for cp in copies:
            cp.wait_send()
        for cp in locals_:
            cp.wait()

    return pl.pallas_call(
        body, name=name,
        out_shape=[jax.ShapeDtypeStruct(g.shape, g.dtype) for g in gs],
        in_specs=_any_specs(na), out_specs=_any_specs(na),
        scratch_shapes=[pltpu.SemaphoreType.DMA((7 * na,)), pltpu.SemaphoreType.DMA((7 * na,)),
                        pltpu.SemaphoreType.DMA((na,))],
    )(*gs)


def _adamw_sum(r8, own, w, m, v, row0, tr, name):
    R, C = w.shape
    assert row0 % tr == 0
    blk0 = row0 // tr
    bc1 = 1.0 - ADAM_B1 ** ADAM_STEP
    bc2 = 1.0 - ADAM_B2 ** ADAM_STEP

    def body(r_ref, *refs):
        if own is None:
            gg = r_ref[0].astype(F32)
        else:
            gg = refs[0][...].astype(F32)
            refs = refs[1:]
        w_ref, m_ref, v_ref, g_ref, d_ref, mo_ref, vo_ref = refs
        for k in range(1, NDEV):
            gg = gg + r_ref[k].astype(F32)
        mn = ADAM_B1 * m_ref[...] + (1.0 - ADAM_B1) * gg
        vn = ADAM_B2 * v_ref[...] + (1.0 - ADAM_B2) * (gg * gg)
        mh = mn / bc1
        vh = vn / bc2
        g_ref[...] = gg
        d_ref[...] = -ADAM_LR * (mh / (jnp.sqrt(vh) + ADAM_EPS) + ADAM_WD * w_ref[...])
        mo_ref[...] = mn
        vo_ref[...] = vn

    spec = pl.BlockSpec((tr, C), lambda i: (i, 0))
    sh = jax.ShapeDtypeStruct((R, C), F32)
    own_ops = [] if own is None else [own]
    own_specs = [] if own is None else [pl.BlockSpec((tr, C), lambda i: (i + blk0, 0))]
    return pl.pallas_call(
        body, name=name, grid=(R // tr,),
        in_specs=[pl.BlockSpec((NDEV, tr, C), lambda i: (0, i + blk0, 0))] + own_specs + [spec, spec, spec],
        out_specs=[spec] * 4, out_shape=[sh] * 4, compiler_params=_params(("parallel",)),
    )(r8, *own_ops, w, m, v)


_HBM = pl.BlockSpec(memory_space=pltpu.HBM)
_SEM = pl.BlockSpec(memory_space=pltpu.SEMAPHORE)
_EFFECT = pltpu.SideEffectType.DATAFLOW_SIDE_EFFECTING


def _exchange_copies(g_refs, land_refs, send_sems, recv_sems):
    mx, my, mc = lax.axis_index("x"), lax.axis_index("y"), lax.axis_index("c")
    copies = []
    for a in range(len(g_refs)):
        for f in range(1, NDEV):
            px = 1 - mx if (f >> 2) & 1 else mx
            py = 1 - my if (f >> 1) & 1 else my
            pc = 1 - mc if f & 1 else mc
            copies.append(pltpu.make_async_remote_copy(
                src_ref=g_refs[a].at[4 * px + 2 * py + pc], dst_ref=land_refs[a].at[f],
                send_sem=send_sems.at[7 * a + f - 1], recv_sem=recv_sems.at[7 * a + f - 1],
                device_id=(px, py, pc), device_id_type=MESH))
    return copies


def _exchange_start(gs, name):
    na = len(gs)

    def body(*refs):
        for cp in _exchange_copies(refs[:na], refs[na:2 * na], refs[2 * na], refs[2 * na + 1]):
            cp.start()
        refs[-1][...] = jnp.zeros_like(refs[-1])

    hbm = [pltpu.HBM(g.shape, g.dtype) for g in gs]
    lands = [pltpu.with_memory_space_constraint(lax.empty(g.shape, g.dtype), pltpu.HBM) for g in gs]
    outs = pl.pallas_call(
        body, name=name,
        out_shape=(pltpu.SemaphoreType.DMA((7 * na,)), pltpu.SemaphoreType.DMA((7 * na,)), *hbm, *hbm,
                   jax.ShapeDtypeStruct((8, 128), F32)),
        in_specs=[_HBM] * (2 * na),
        out_specs=(_SEM, _SEM, *([_HBM] * (2 * na)), pl.BlockSpec(memory_space=pltpu.VMEM)),
        input_output_aliases={i: 2 + i for i in range(2 * na)},
        compiler_params=pltpu.CompilerParams(has_side_effects=_EFFECT),
    )(*[pltpu.with_memory_space_constraint(g, pltpu.HBM) for g in gs], *lands)
    return outs[0], outs[1], outs[2:2 + na], outs[2 + na:2 + 2 * na], outs[-1]


def _exchange_wait(send_sems, recv_sems, g_thru, land_thru, after, name):
    na = len(g_thru)

    def body(*refs):
        for cp in _exchange_copies(refs[:na], refs[na:2 * na], refs[2 * na], refs[2 * na + 1]):
            cp.wait_send()
            cp.wait_recv()

    outs = pl.pallas_call(
        body, name=name,
        out_shape=tuple(pltpu.HBM(g.shape, g.dtype) for g in list(g_thru) + list(land_thru)),
        in_specs=[_HBM] * (2 * na) + [_SEM, _SEM, pl.BlockSpec(memory_space=pl.ANY)],
        out_specs=tuple([_HBM] * (2 * na)),
        input_output_aliases={i: i for i in range(2 * na)},
        compiler_params=pltpu.CompilerParams(has_side_effects=_EFFECT),
    )(*g_thru, *land_thru, send_sems, recv_sems, after)
    return outs[na:]


def _sum8(r, name):
    _, R, C = r.shape
    tr = _tile(R, (256, 160, 128, 64, 32, 16, 8))

    def body(r_ref, o_ref):
        acc = r_ref[0].astype(F32)
        for k in range(1, NDEV):
            acc = acc + r_ref[k].astype(F32)
        o_ref[...] = acc

    return pl.pallas_call(
        body, name=name, grid=(R // tr,),
        in_specs=[pl.BlockSpec((NDEV, tr, C), lambda i: (0, i, 0))],
        out_specs=pl.BlockSpec((tr, C), lambda i: (i, 0)),
        out_shape=jax.ShapeDtypeStruct((R, C), F32),
        compiler_params=_params(("parallel",)),
    )(r)


def _adamw(g, w, m, v, name):
    R, C = g.shape
    tr = _tile(R, (256, 160, 128, 64, 32, 16, 8))
    bc1 = 1.0 - ADAM_B1 ** ADAM_STEP
    bc2 = 1.0 - ADAM_B2 ** ADAM_STEP

    def body(g_ref, w_ref, m_ref, v_ref, d_ref, mo_ref, vo_ref):
        gg = g_ref[...]
        mn = ADAM_B1 * m_ref[...] + (1.0 - ADAM_B1) * gg
        vn = ADAM_B2 * v_ref[...] + (1.0 - ADAM_B2) * (gg * gg)
        mh = mn / bc1
        vh = vn / bc2
        d_ref[...] = -ADAM_LR * (mh / (jnp.sqrt(vh) + ADAM_EPS) + ADAM_WD * w_ref[...])
        mo_ref[...] = mn
        vo_ref[...] = vn

    spec = pl.BlockSpec((tr, C), lambda i: (i, 0))
    sh = jax.ShapeDtypeStruct((R, C), F32)
    return pl.pallas_call(
        body, name=name, grid=(R // tr,), in_specs=[spec] * 4, out_specs=[spec] * 3,
        out_shape=[sh] * 3, compiler_params=_params(("parallel",)),
    )(g, w, m, v)


def _ada_fwd(c16, w_sh, b_sh):
    def body(c_ref, w_ref, b_ref, o_ref):
        c = c_ref[...]
        o_ref[...] = _dot(c * _sig(c), w_ref[...], NN) + b_ref[...]

    return pl.pallas_call(
        body, name="ada_fwd", out_shape=jax.ShapeDtypeStruct((16, w_sh.shape[1]), F32),
        compiler_params=pltpu.CompilerParams(vmem_limit_bytes=VMEM_LIMIT),
    )(c16, w_sh, b_sh)


def _ada_bwd(c16, g16, g16_sh, w_sh):
    ncol = w_sh.shape[1]

    def body(c_ref, g_ref, gs_ref, w_ref, dw_ref, db_ref, dc_ref):
        c = c_ref[...]
        s = _sig(c)
        gs = gs_ref[...]
        dw_ref[...] = _dot(c * s, gs, TN)
        db_ref[...] = jnp.broadcast_to(_colsum(g_ref[...]), db_ref.shape)
        odd = lax.broadcasted_iota(jnp.int32, gs.shape, 0) % 2 == 1
        gc = _colsum(jnp.where(odd, gs, 0.0))
        ds = _dot(jnp.broadcast_to(gc, (8, ncol)), w_ref[...], NT)
        c1 = c[1:2, :]
        s1 = s[1:2, :]
        dc_ref[...] = ds * (s1 * (1.0 + c1 * (1.0 - s1)))

    return pl.pallas_call(
        body, name="ada_bwd",
        out_shape=[jax.ShapeDtypeStruct(w_sh.shape, F32),
                   jax.ShapeDtypeStruct((8, g16.shape[1]), F32),
                   jax.ShapeDtypeStruct((8, D), F32)],
        compiler_params=pltpu.CompilerParams(vmem_limit_bytes=VMEM_LIMIT),
    )(c16, g16, g16_sh, w_sh)


def _ln0_fwd(x, ctx, g, b, modx, modc):
    L = x.shape[0]
    nt = L // TL

    def body(x_ref, c_ref, g_ref, b_ref, mx_ref, mc_ref, xn_ref, h_ref):
        isc = pl.program_id(0) == nt
        xin = jnp.where(isc, c_ref[...], x_ref[...])
        sh = jnp.where(isc, mc_ref[0:1, :], mx_ref[0:1, :])
        sc = jnp.where(isc, mc_ref[1:2, :], mx_ref[1:2, :])
        xhat, _ = _ln(xin)
        xn = xhat * g_ref[...] + b_ref[...]
        xn_ref[...] = xn
        h_ref[...] = (xn * (1.0 + sc) + sh).astype(h_ref.dtype)

    return pl.pallas_call(
        body, name="ln0_fwd", grid=(nt + 1,),
        in_specs=[_rtc(D, nt), _cst((TL, D)), _cst((1, D)), _cst((1, D)), _cst((8, D)), _cst((8, D))],
        out_specs=[_rt(D), _rt(D)],
        out_shape=[jax.ShapeDtypeStruct((L + TL, D), F32), jax.ShapeDtypeStruct((L + TL, D), _MXU)],
        compiler_params=_params(("parallel",)),
    )(x, ctx, g, b, modx, modc)


def _xbc_colblk(j):
    return jnp.where(j < 8, OXS // 128 + j, OB // 128 + j - 8)


def _conv_taps(p_ref, r0, first, last):
    main = p_ref[pl.ds(r0, TL), :]
    zero = jnp.zeros((8, main.shape[1]), F32)
    prev = zero if first else p_ref[pl.ds(r0 - 8, 8), :]
    nxt = zero if last else p_ref[pl.ds(r0 + TL, 8), :]
    ext = jnp.concatenate([prev, main, nxt], axis=0)
    n = TL + 16
    return [pltpu.roll(ext, (2 - k) % n, 0)[8:8 + TL] for k in range(5)]


def _seq_chunks(L):
    nt = L // TL
    return [(r * TL, r == 0, r == nt - 1) for r in range(nt)] + [(L, True, True)]


def _conv_fwd(p, conv_w8, conv_b):
    RT = p.shape[0]
    L = RT - TL
    chunks = _seq_chunks(L)

    def body(p_ref, w_ref, b_ref, o_ref):
        w = w_ref[...]
        bias = b_ref[...]
        for r0, first, last in chunks:
            taps = _conv_taps(p_ref, r0, first, last)
            pre = bias + sum(w[k:k + 1, :] * taps[k] for k in range(5))
            o_ref[pl.ds(r0, TL), :] = pre * _sig(pre)

    return pl.pallas_call(
        body, name="conv_fwd", grid=(12,),
        in_specs=[pl.BlockSpec((RT, 128), lambda j: (0, _xbc_colblk(j))),
                  pl.BlockSpec((8, 128), lambda j: (0, j)),
                  pl.BlockSpec((1, 128), lambda j: (0, j))],
        out_specs=pl.BlockSpec((RT, 128), lambda j: (0, j)),
        out_shape=jax.ShapeDtypeStruct((RT, 1536), F32),
        compiler_params=_params(("parallel",)),
    )(p, conv_w8, conv_b)


def _ssd_common(dtraw, dtb, a32, rev):
    dt = _softplus(dtraw + dtb)
    acum = _cumsum_rows(dt * a32, rev)
    ii = lax.broadcasted_iota(jnp.int32, (Q, Q), 0)
    jj = lax.broadcasted_iota(jnp.int32, (Q, Q), 1)
    mask = (ii <= jj) if rev else (ii >= jj)
    return dt, acum, acum.T, dt.T, mask


def _ssd_orders(ncl, ncc):
    nc = ncl + ncc

    def cf(s):
        return jnp.where(s < ncc, ncl + s, s - ncc)

    def cb(s):
        return nc - 1 - s

    return cf, cb


def _ssd_fwd(xbc, p, prm):
    RT = xbc.shape[0]
    nc = RT // Q
    ncc = TL // Q
    cf, cb = _ssd_orders(nc - ncc, ncc)

    def one_dir(x_ref, dt_ref, prm_ref, y_ref, hp_ref, H_ref, d):
        rev = d == 1
        a32 = -jnp.exp(prm_ref[1:2, :])
        dt, acum, acumT, dtT, mask = _ssd_common(dt_ref[...], prm_ref[0:1, :], a32, rev)
        end = 0 if rev else Q - 1
        for g in range(2):
            Bg = x_ref[:, D + g * NS:D + (g + 1) * NS]
            Cg = x_ref[:, D + 2 * NS + g * NS:D + 2 * NS + (g + 1) * NS]
            CB = _dot(Cg, Bg, NT)
            for hh in range(HPG):
                h = g * HPG + hh
                ln = 16 * d + h
                col = acum[:, ln:ln + 1]
                rowv = acumT[ln:ln + 1, :]
                a_end = rowv[:, end:end + 1]
                Lm = jnp.exp(jnp.where(mask, col - rowv, -1e30))
                W = CB * Lm * dtT[ln:ln + 1, :]
                Xh = x_ref[:, h * HP:(h + 1) * HP]
                Hp = H_ref[h * HP:(h + 1) * HP, :]
                y = _dot(W, Xh, NN) + jnp.exp(col) * _dot(Cg, Hp, NT)
                y_ref[:, h * HP:(h + 1) * HP] = y
                dcol = jnp.exp(a_end - col) * dt[:, ln:ln + 1]
                hp_ref[0, h * HP:(h + 1) * HP, :] = Hp
                H_ref[h * HP:(h + 1) * HP, :] = jnp.exp(a_end) * Hp + _dot(Xh * dcol, Bg, TN)

    def body(xf_ref, xb_ref, df_ref, db_ref, prm_ref, yf_ref, yb_ref, hf_ref, hb_ref, Hf, Hb):
        @pl.when(pl.program_id(0) == 0)
        def _():
            Hf[...] = jnp.zeros_like(Hf)
            Hb[...] = jnp.zeros_like(Hb)

        one_dir(xf_ref, df_ref, prm_ref, yf_ref, hf_ref, Hf, 0)
        one_dir(xb_ref, db_ref, prm_ref, yb_ref, hb_ref, Hb, 1)

    ysh = jax.ShapeDtypeStruct((RT, D), F32)
    hsh = jax.ShapeDtypeStruct((nc, NH * HP, NS), F32)
    hspec = pl.BlockSpec((1, NH * HP, NS), lambda s: (s, 0, 0))
    return pl.pallas_call(
        body, name="ssd_fwd", grid=(nc,),
        in_specs=[pl.BlockSpec((Q, 1536), lambda s: (cf(s), 0)),
                  pl.BlockSpec((Q, 1536), lambda s: (cb(s), 0)),
                  pl.BlockSpec((Q, 128), lambda s: (cf(s), ODT // 128)),
                  pl.BlockSpec((Q, 128), lambda s: (cb(s), ODT // 128)),
                  _cst((8, 128))],
        out_specs=[pl.BlockSpec((Q, D), lambda s: (cf(s), 0)),
                   pl.BlockSpec((Q, D), lambda s: (cb(s), 0)), hspec, hspec],
        out_shape=[ysh, ysh, hsh, hsh],
        scratch_shapes=[pltpu.VMEM((NH * HP, NS), F32), pltpu.VMEM((NH * HP, NS), F32)],
        compiler_params=_params(("arbitrary",)),
    )(xbc, xbc, p, p, prm)


def _ssd_bwd(xbc, p, prm, dsk, dyd, hpf, hpb):
    RT = xbc.shape[0]
    nc = RT // Q
    ncc = TL // Q
    ncl = nc - ncc
    cf, cb = _ssd_orders(ncl, ncc)

    def rs(t):
        return nc - 1 - t

    def one_dir(x_ref, dt_ref, prm_ref, dsk_ref, dy_ref, is_ctx, hp_ref, dH_ref,
                dx_ref, ddt_ref, st_ref, d):
        rev = d == 1
        a32 = -jnp.exp(prm_ref[1:2, :])
        dtraw = dt_ref[...]
        dtb = prm_ref[0:1, :]
        dt, acum, acumT, dtT, mask = _ssd_common(dtraw, dtb, a32, rev)
        end = 0 if rev else Q - 1
        lane = lax.broadcasted_iota(jnp.int32, (Q, 128), 1)
        srow = lax.broadcasted_iota(jnp.int32, (Q, 128), 0)
        dyscale = jnp.where(is_ctx, 0.0, 1.0)
        c_dacum = jnp.zeros((Q, 128), F32)
        r_dacum = jnp.zeros((Q, 128), F32)
        c_ddt = jnp.zeros((Q, 128), F32)
        r_ddt = jnp.zeros((Q, 128), F32)
        dskacc = jnp.zeros((1, 128), F32)
        for g in range(2):
            Bg = x_ref[:, D + g * NS:D + (g + 1) * NS]
            Cg = x_ref[:, D + 2 * NS + g * NS:D + 2 * NS + (g + 1) * NS]
            CB = _dot(Cg, Bg, NT)
            dCB = jnp.zeros((Q, Q), F32)
            dBg = jnp.zeros((Q, NS), F32)
            dCg = jnp.zeros((Q, NS), F32)
            for hh in range(HPG):
                h = g * HPG + hh
                ln = 16 * d + h
                hs = slice(h * HP, (h + 1) * HP)
                col = acum[:, ln:ln + 1]
                rowv = acumT[ln:ln + 1, :]
                dtr = dtT[ln:ln + 1, :]
                dtc = dt[:, ln:ln + 1]
                a_end = rowv[:, end:end + 1]
                Lm = jnp.exp(jnp.where(mask, col - rowv, -1e30))
                E = jnp.exp(col)
                ecol = jnp.exp(a_end - col)
                dcol = ecol * dtc
                Xh = x_ref[:, hs]
                dY = dy_ref[:, hs] * dyscale
                Hp = hp_ref[0, hs, :]
                dHn = dH_ref[hs, :]
                W = CB * Lm * dtr
                dW = _dot(dY, Xh, NT)
                Mm = dW * CB * Lm
                T = Mm * dtr
                dCB = dCB + dW * Lm * dtr
                BdH = _dot(Bg, dHn, NT)
                dX = _dot(W, dY, TN) + dcol * BdH
                if d == 0:
                    dX = dX + dY * dsk_ref[:, hs]
                    dskacc = dskacc + jnp.where(lane[0:1, :] == h, _sum11(dY * Xh), 0.0)
                dx_ref[:, hs] = dX
                xb = jnp.sum(Xh * BdH, axis=1, keepdims=True)
                scol = dcol * xb
                G = _dot(dY, Hp, NN)
                dCg = dCg + E * G
                qcol = E * jnp.sum(G * Cg, axis=1, keepdims=True)
                dBg = dBg + _dot(Xh * dcol, dHn, NN)
                dH_ref[hs, :] = jnp.exp(a_end) * dHn + _dot(dY * E, Cg, TN)
                eterm = jnp.exp(a_end) * _sum11(dHn * Hp) + _sum11(scol)
                cvec = jnp.sum(T, axis=1, keepdims=True) + qcol - scol
                cvec = cvec + jnp.where(srow[:, 0:1] == end, eterm, 0.0)
                c_dacum = c_dacum + jnp.where(lane == ln, cvec, 0.0)
                r_dacum = r_dacum - jnp.where(srow == ln, _colsum(T), 0.0)
                c_ddt = c_ddt + jnp.where(lane == ln, ecol * xb, 0.0)
                r_ddt = r_ddt + jnp.where(srow == ln, _colsum(Mm), 0.0)
            dBg = dBg + _dot(dCB, Cg, TN)
            dCg = dCg + _dot(dCB, Bg, NN)
            dx_ref[:, D + g * NS:D + (g + 1) * NS] = dBg
            dx_ref[:, D + 2 * NS + g * NS:D + 2 * NS + (g + 1) * NS] = dCg
        dacum = c_dacum + r_dacum.T
        da = _cumsum_rows(dacum, not rev)
        mine = (lane >= 16 * d) & (lane < 16 * d + 16)
        ddt = jnp.where(mine, c_ddt + r_ddt.T + da * a32, 0.0)
        ddt_ref[...] = ddt * _sig(dtraw + dtb)
        st_ref[0:1, :] += _colsum(jnp.where(mine, da * dt, 0.0))
        if d == 0:
            st_ref[1:2, :] += dskacc

    def body(xf_ref, xb_ref, df_ref, db_ref, prm_ref, dsk_ref, dyf_ref, dyb_ref, hf_ref, hb_ref,
             dxf_ref, dxb_ref, ddf_ref, ddb_ref, st_ref, dHf, dHb):
        t = pl.program_id(0)

        @pl.when(t == 0)
        def _():
            dHf[...] = jnp.zeros_like(dHf)
            dHb[...] = jnp.zeros_like(dHb)
            st_ref[...] = jnp.zeros_like(st_ref)

        s = rs(t)
        one_dir(xf_ref, df_ref, prm_ref, dsk_ref, dyf_ref, cf(s) >= ncl, hf_ref, dHf,
                dxf_ref, ddf_ref, st_ref, 0)
        one_dir(xb_ref, db_ref, prm_ref, dsk_ref, dyb_ref, cb(s) >= ncl, hb_ref, dHb,
                dxb_ref, ddb_ref, st_ref, 1)

        @pl.when(t == nc - 1)
        def _():
            st_ref[0:1, :] = -jnp.exp(prm_ref[1:2, :]) * st_ref[0:1, :]

    def lat(c):
        return jnp.minimum(c, ncl - 1)

    xsh = jax.ShapeDtypeStruct((RT, 1536), F32)
    dsh = jax.ShapeDtypeStruct((RT, 128), F32)
    hspec = pl.BlockSpec((1, NH * HP, NS), lambda t: (rs(t), 0, 0))
    return pl.pallas_call(
        body, name="ssd_bwd", grid=(nc,),
        in_specs=[pl.BlockSpec((Q, 1536), lambda t: (cf(rs(t)), 0)),
                  pl.BlockSpec((Q, 1536), lambda t: (cb(rs(t)), 0)),
                  pl.BlockSpec((Q, 128), lambda t: (cf(rs(t)), ODT // 128)),
                  pl.BlockSpec((Q, 128), lambda t: (cb(rs(t)), ODT // 128)),
                  _cst((8, 128)), _cst((1, D)),
                  pl.BlockSpec((Q, D), lambda t: (lat(cf(rs(t))), 0)),
                  pl.BlockSpec((Q, D), lambda t: (lat(cb(rs(t))), 0)),
                  hspec, hspec],
        out_specs=[pl.BlockSpec((Q, 1536), lambda t: (cf(rs(t)), 0)),
                   pl.BlockSpec((Q, 1536), lambda t: (cb(rs(t)), 0)),
                   pl.BlockSpec((Q, 128), lambda t: (cf(rs(t)), 0)),
                   pl.BlockSpec((Q, 128), lambda t: (cb(rs(t)), 0)),
                   _cst((8, 128))],
        out_shape=[xsh, xsh, dsh, dsh, jax.ShapeDtypeStruct((8, 128), F32)],
        scratch_shapes=[pltpu.VMEM((NH * HP, NS), F32), pltpu.VMEM((NH * HP, NS), F32)],
        compiler_params=_params(("arbitrary",)),
    )(xbc, xbc, p, p, prm, dsk, dyd, dyd, hpf, hpb)


def _lane_bcast(v, ln):
    return jnp.broadcast_to(v[:, ln:ln + 1], v.shape)


def _halves(v, lo, axis):
    return jnp.concatenate([jnp.where(lo, v, 0.0), jnp.where(lo, 0.0, v)], axis=axis)


def _ssd2_fwd(xbc, p, prm):
    RT = xbc.shape[0]
    nc = RT // Q
    ncc = TL // Q
    cf, cb = _ssd_orders(nc - ncc, ncc)

    def one_dir(x_ref, dt_ref, prm_ref, y_ref, hp_ref, HT_ref, d):
        rev = d == 1
        a32 = -jnp.exp(prm_ref[1:2, :])
        dt, acum, acumT, dtT, mask = _ssd_common(dt_ref[...], prm_ref[0:1, :], a32, rev)
        end = 0 if rev else Q - 1
        lo = lax.broadcasted_iota(jnp.int32, (Q, 128), 1) < HP
        for g in range(2):
            Bg = x_ref[:, D + g * NS:D + (g + 1) * NS]
            Cg = x_ref[:, D + 2 * NS + g * NS:D + 2 * NS + (g + 1) * NS]
            CB = _dot(Cg, Bg, NT)
            xds, svs = [], []
            for q in range(HPG // 2):
                pi = g * (HPG // 2) + q
                ps = slice(pi * 128, (pi + 1) * 128)
                Xp = x_ref[:, ps]
                HTp = HT_ref[:, ps]
                lhs, dcs, sv = [], [], []
                ces = []
                for h in (2 * pi, 2 * pi + 1):
                    ln = 16 * d + h
                    colB = _lane_bcast(acum, ln)
                    rowv = acumT[ln:ln + 1, :]
                    aend = colB[end:end + 1, :]
                    Lm = jnp.exp(jnp.where(mask, colB - rowv, -1e30))
                    lhs.append(CB * Lm * dtT[ln:ln + 1, :])
                    ces.append(Cg * jnp.exp(colB))
                    dcs.append(jnp.exp(aend - colB) * _lane_bcast(dt, ln))
                    sv.append(jnp.exp(aend))
                lhs = jnp.concatenate(lhs + ces, axis=1)
                rhs = jnp.concatenate([_halves(Xp, lo, 0), _halves(HTp, lo, 0)], axis=0)
                y_ref[:, ps] = _dot(lhs, rhs, NN)
                xds.append(Xp * jnp.where(lo, dcs[0], dcs[1]))
                svs.append(jnp.where(lo[0:1, :], sv[0], sv[1]))
            gs = slice(g * 512, (g + 1) * 512)
            HTg = HT_ref[:, gs]
            hp_ref[0, :, gs] = HTg
            st = _dot(Bg.T, jnp.concatenate(xds, axis=1), NN)
            HT_ref[:, gs] = jnp.concatenate(svs, axis=1) * HTg + st

    def body(xf_ref, xb_ref, df_ref, db_ref, prm_ref, yf_ref, yb_ref, hf_ref, hb_ref, Hf, Hb):
        @pl.when(pl.program_id(0) == 0)
        def _():
            Hf[...] = jnp.zeros_like(Hf)
            Hb[...] = jnp.zeros_like(Hb)

        one_dir(xf_ref, df_ref, prm_ref, yf_ref, hf_ref, Hf, 0)
        one_dir(xb_ref, db_ref, prm_ref, yb_ref, hb_ref, Hb, 1)

    ysh = jax.ShapeDtypeStruct((RT, D), F32)
    hsh = jax.ShapeDtypeStruct((nc, NS, NH * HP), F32)
    hspec = pl.BlockSpec((1, NS, NH * HP), lambda s: (s, 0, 0))
    return pl.pallas_call(
        body, name="ssd_fwd", grid=(nc,),
        in_specs=[pl.BlockSpec((Q, 1536), lambda s: (cf(s), 0)),
                  pl.BlockSpec((Q, 1536), lambda s: (cb(s), 0)),
                  pl.BlockSpec((Q, 128), lambda s: (cf(s), ODT // 128)),
                  pl.BlockSpec((Q, 128), lambda s: (cb(s), ODT // 128)),
                  _cst((8, 128))],
        out_specs=[pl.BlockSpec((Q, D), lambda s: (cf(s), 0)),
                   pl.BlockSpec((Q, D), lambda s: (cb(s), 0)), hspec, hspec],
        out_shape=[ysh, ysh, hsh, hsh],
        scratch_shapes=[pltpu.VMEM((NS, NH * HP), F32), pltpu.VMEM((NS, NH * HP), F32)],
        compiler_params=_params(("arbitrary",)),
    )(xbc, xbc, p, p, prm)


def _ssd2_bwd(xbc, p, prm, dsk, dyd, hpf, hpb):
    RT = xbc.shape[0]
    nc = RT // Q
    ncc = TL // Q
    ncl = nc - ncc
    cf, cb = _ssd_orders(ncl, ncc)

    def rs(t):
        return nc - 1 - t

    def one_dir(x_ref, dt_ref, prm_ref, dsk_ref, dy_ref, is_ctx, hp_ref, dHT_ref,
                dx_ref, ddt_ref, st_ref, d):
        rev = d == 1
        a32 = -jnp.exp(prm_ref[1:2, :])
        dtraw = dt_ref[...]
        dtb = prm_ref[0:1, :]
        dt, acum, acumT, _, _ = _ssd_common(dtraw, dtb, a32, rev)
        end = 0 if rev else Q - 1
        lane = lax.broadcasted_iota(jnp.int32, (Q, 128), 1)
        srow = lax.broadcasted_iota(jnp.int32, (Q, 128), 0)
        maskT = (lane <= srow) if rev else (lane >= srow)
        lo = lane < HP
        lo1 = lo[0:1, :]
        dyscale = jnp.where(is_ctx, 0.0, 1.0)
        c_dacum = jnp.zeros((Q, 128), F32)
        r_dacum = jnp.zeros((Q, 128), F32)
        c_ddt = jnp.zeros((Q, 128), F32)
        dskacc = jnp.zeros((1, 128), F32)
        for g in range(2):
            gs = slice(g * 512, (g + 1) * 512)
            Bg = x_ref[:, D + g * NS:D + (g + 1) * NS]
            Cg = x_ref[:, D + 2 * NS + g * NS:D + 2 * NS + (g + 1) * NS]
            CBT = _dot(Bg, Cg, NT)
            HTg = hp_ref[0, :, gs]
            dHTg = dHT_ref[:, gs]
            BdHg = _dot(Bg, dHTg, NN)
            dCBT = jnp.zeros((Q, Q), F32)
            dCg = jnp.zeros((Q, NS), F32)
            xds, dyes, svs = [], [], []
            for q in range(HPG // 2):
                pi = g * (HPG // 2) + q
                ps = slice(pi * 128, (pi + 1) * 128)
                qs = slice(q * 128, (q + 1) * 128)
                Xp = x_ref[:, ps]
                dYp = dy_ref[:, ps] * dyscale
                HTp = HTg[:, qs]
                BdHp = BdHg[:, qs]
                dY2 = _halves(dYp, lo, 0)
                dWT2 = _dot(_halves(Xp, lo, 0), dYp.T, NN)
                G2 = _dot(dY2, HTp, NT)
                XB = Xp * BdHp
                hh = _colsum(dHTg[:, qs] * HTp)
                yx = _colsum(dYp * Xp)
                wts, dcs, ebs, sv = [], [], [], []
                for k, h in enumerate((2 * pi, 2 * pi + 1)):
                    ln = 16 * d + h
                    half = lo if k == 0 else jnp.logical_not(lo)
                    half1 = half[0:1, :]
                    colB = _lane_bcast(acum, ln)
                    dtcB = _lane_bcast(dt, ln)
                    rowv = acumT[ln:ln + 1, :]
                    aend = colB[end:end + 1, :]
                    LmT = jnp.exp(jnp.where(maskT, rowv - colB, -1e30))
                    WT = CBT * LmT * dtcB
                    dWT = dWT2[k * Q:(k + 1) * Q, :]
                    MT = dWT * CBT * LmT
                    rM = jnp.sum(MT, axis=1, keepdims=True)
                    rT = _colsum(MT * dtcB)
                    dCBT = dCBT + dWT * LmT * dtcB
                    ecol = jnp.exp(aend - colB)
                    EB = jnp.exp(colB)
                    Gk = G2[k * Q:(k + 1) * Q, :]
                    dCg = dCg + EB * Gk
                    qcol = jnp.sum(EB * Gk * Cg, axis=1, keepdims=True)
                    xb = jnp.sum(jnp.where(half, XB, 0.0), axis=1, keepdims=True)
                    e1 = ecol[:, 0:1]
                    dt1 = dtcB[:, 0:1]
                    scol = e1 * dt1 * xb
                    sA = jnp.exp(aend)
                    eterm = sA[:, 0:1] * jnp.sum(jnp.where(half1, hh, 0.0), axis=1, keepdims=True) \
                        + _colsum(scol)
                    cvec = qcol - dt1 * rM - scol + jnp.where(srow[:, 0:1] == end, eterm, 0.0)
                    c_dacum = c_dacum + jnp.where(lane == ln, cvec, 0.0)
                    r_dacum = r_dacum + jnp.where(srow == ln, rT, 0.0)
                    c_ddt = c_ddt + jnp.where(lane == ln, rM + e1 * xb, 0.0)
                    if d == 0:
                        dskacc = dskacc + jnp.where(
                            lane[0:1, :] == h, jnp.sum(jnp.where(half1, yx, 0.0), axis=1, keepdims=True), 0.0)
                    wts.append(WT)
                    dcs.append(ecol * dtcB)
                    ebs.append(EB)
                    sv.append(sA)
                dcp = jnp.where(lo, dcs[0], dcs[1])
                dX = _dot(jnp.concatenate(wts, axis=1), dY2, NN) + dcp * BdHp
                if d == 0:
                    dX = dX + dYp * dsk_ref[:, ps]
                dx_ref[:, ps] = dX
                xds.append(Xp * dcp)
                dyes.append(dYp * jnp.where(lo, ebs[0], ebs[1]))
                svs.append(jnp.where(lo1, sv[0], sv[1]))
            dx_ref[:, D + g * NS:D + (g + 1) * NS] = (
                _dot(jnp.concatenate(xds, axis=1), dHTg, NT) + _dot(dCBT, Cg, NN))
            dx_ref[:, D + 2 * NS + g * NS:D + 2 * NS + (g + 1) * NS] = dCg + _dot(dCBT, Bg, TN)
            dHT_ref[:, gs] = (jnp.concatenate(svs, axis=1) * dHTg
                              + _dot(Cg.T, jnp.concatenate(dyes, axis=1), NN))
        dacum = c_dacum + r_dacum.T
        da = _cumsum_rows(dacum, not rev)
        mine = (lane >= 16 * d) & (lane < 16 * d + 16)
        ddt = jnp.where(mine, c_ddt + da * a32, 0.0)
        ddt_ref[...] = ddt * _sig(dtraw + dtb)
        st_ref[0:1, :] += _colsum(jnp.where(mine, da * dt, 0.0))
        if d == 0:
            st_ref[1:2, :] += dskacc

    def body(xf_ref, xb_ref, df_ref, db_ref, prm_ref, dsk_ref, dyf_ref, dyb_ref, hf_ref, hb_ref,
             dxf_ref, dxb_ref, ddf_ref, ddb_ref, st_ref, dHf, dHb):
        t = pl.program_id(0)

        @pl.when(t == 0)
        def _():
            dHf[...] = jnp.zeros_like(dHf)
            dHb[...] = jnp.zeros_like(dHb)
            st_ref[...] = jnp.zeros_like(st_ref)

        s = rs(t)
        one_dir(xf_ref, df_ref, prm_ref, dsk_ref, dyf_ref, cf(s) >= ncl, hf_ref, dHf,
                dxf_ref, ddf_ref, st_ref, 0)
        one_dir(xb_ref, db_ref, prm_ref, dsk_ref, dyb_ref, cb(s) >= ncl, hb_ref, dHb,
                dxb_ref, ddb_ref, st_ref, 1)

        @pl.when(t == nc - 1)
        def _():
            st_ref[0:1, :] = -jnp.exp(prm_ref[1:2, :]) * st_ref[0:1, :]

    def lat(c):
        return jnp.minimum(c, ncl - 1)

    xsh = jax.ShapeDtypeStruct((RT, 1536), F32)
    dsh = jax.ShapeDtypeStruct((RT, 128), F32)
    hspec = pl.BlockSpec((1, NS, NH * HP), lambda t: (rs(t), 0, 0))
    return pl.pallas_call(
        body, name="ssd_bwd", grid=(nc,),
        in_specs=[pl.BlockSpec((Q, 1536), lambda t: (cf(rs(t)), 0)),
                  pl.BlockSpec((Q, 1536), lambda t: (cb(rs(t)), 0)),
                  pl.BlockSpec((Q, 128), lambda t: (cf(rs(t)), ODT // 128)),
                  pl.BlockSpec((Q, 128), lambda t: (cb(rs(t)), ODT // 128)),
                  _cst((8, 128)), _cst((1, D)),
                  pl.BlockSpec((Q, D), lambda t: (lat(cf(rs(t))), 0)),
                  pl.BlockSpec((Q, D), lambda t: (lat(cb(rs(t))), 0)),
                  hspec, hspec],
        out_specs=[pl.BlockSpec((Q, 1536), lambda t: (cf(rs(t)), 0)),
                   pl.BlockSpec((Q, 1536), lambda t: (cb(rs(t)), 0)),
                   pl.BlockSpec((Q, 128), lambda t: (cf(rs(t)), 0)),
                   pl.BlockSpec((Q, 128), lambda t: (cb(rs(t)), 0)),
                   _cst((8, 128))],
        out_shape=[xsh, xsh, dsh, dsh, jax.ShapeDtypeStruct((8, 128), F32)],
        scratch_shapes=[pltpu.VMEM((NS, NH * HP), F32), pltpu.VMEM((NS, NH * HP), F32)],
        compiler_params=_params(("arbitrary",)),
    )(xbc, xbc, p, p, prm, dsk, dyd, dyd, hpf, hpb)


def _mix_fwd_vals(yf, yb, z, xs, u, v, dsk, sg, gg, gb):
    y = yf + yb + xs * dsk
    sz = _sig(z)
    hh = y * z * sz
    r = lax.rsqrt(jnp.mean(hh * hh, axis=-1, keepdims=True) + EPS)
    nh = hh * r
    ug, tu = _gelu(u)
    vg, tv = _gelu(v)
    vhat, vrstd = _ln(vg)
    vn = vhat * gg + gb
    return y, sz, r, nh, ug, tu, vg, tv, vhat, vrstd, vn


def _mix_fwd(yf, yb, p, xbc, dsk, sg, gg, gb, ws, bsT):
    L = yf.shape[0] - TL
    nt = L // TL

    def body(yf_ref, yb_ref, z_ref, xs_ref, u_ref, v_ref, dsk_ref, sg_ref, gg_ref, gb_ref,
             ws_ref, bs_ref, ys_ref, ym_ref):
        _, _, _, nh, ug, _, _, _, _, _, vn = _mix_fwd_vals(
            yf_ref[...], yb_ref[...], z_ref[...], xs_ref[...], u_ref[...], v_ref[...],
            dsk_ref[...], sg_ref[...], gg_ref[...], gb_ref[...])
        ys_ref[...] = (nh * sg_ref[...]).astype(ys_ref.dtype)
        for n in range(TL // Q):
            rs_ = slice(n * Q, (n + 1) * Q)
            for g in range(8):
                cs = slice(g * 128, (g + 1) * 128)
                mixed = _dot(ws_ref[g], vn[rs_, cs], NN) + bs_ref[:, g:g + 1]
                ym_ref[rs_, cs] = (ug[rs_, cs] * mixed).astype(ym_ref.dtype)

    return pl.pallas_call(
        body, name="mix_fwd", grid=(nt,),
        in_specs=[_rt(D), _rt(D), _rt(D, OZ // D), _rt(D, 0), _rt(D, OU // D), _rt(D, OV // D),
                  _cst((1, D)), _cst((1, D)), _cst((1, D)), _cst((1, D)),
                  _cst((8, 128, 128)), _cst((128, 128))],
        out_specs=[_rt(D), _rt(D)],
        out_shape=[jax.ShapeDtypeStruct((L, D), _MXU), jax.ShapeDtypeStruct((L, D), _MXU)],
        compiler_params=_params(("parallel",)),
    )(yf, yb, p, xbc, p, p, dsk, sg, gg, gb, ws, bsT)


def _mix_bwd(dys, dym, yf, yb, p, xbc, dp, dsk, sg, gg, gb, ws, bsT):
    L = dys.shape[0]
    nt = L // TL

    def body(dys_ref, dym_ref, yf_ref, yb_ref, z_ref, xs_ref, u_ref, v_ref, dsk_ref, sg_ref,
             gg_ref, gb_ref, ws_ref, bs_ref, dp_any, dzuv_ref, dy_ref, st_ref,
             dws_ref, dbs_ref, dvn_s):
        del dp_any
        dz_ref = dzuv_ref.at[:, OZ:OZ + D]
        du_ref = dzuv_ref.at[:, OU:OU + D]
        dv_ref = dzuv_ref.at[:, OV:OV + D]

        @pl.when(pl.program_id(0) == 0)
        def _():
            st_ref[...] = jnp.zeros_like(st_ref)
            dws_ref[...] = jnp.zeros_like(dws_ref)
            dbs_ref[...] = jnp.zeros_like(dbs_ref)

        z = z_ref[...]
        u = u_ref[...]
        v = v_ref[...]
        y, sz, r, nh, ug, tu, vg, tv, vhat, vrstd, vn = _mix_fwd_vals(
            yf_ref[...], yb_ref[...], z, xs_ref[...], u, v,
            dsk_ref[...], sg_ref[...], gg_ref[...], gb_ref[...])
        dys = dys_ref[...]
        st_ref[0:1, :] += _colsum(dys * nh)
        dn = dys * sg_ref[...]
        dhh = r * (dn - nh * jnp.mean(dn * nh, axis=-1, keepdims=True))
        dy_ref[...] = dhh * z * sz
        dz_ref[...] = (dhh * y * (sz * (1.0 + z * (1.0 - sz)))).astype(dz_ref.dtype)
        dym = dym_ref[...]
        lane = lax.broadcasted_iota(jnp.int32, (Q, 128), 1)
        dbs = jnp.zeros((Q, 128), F32)
        gu = _gelu_grad(u, tu)
        for n in range(TL // Q):
            rs_ = slice(n * Q, (n + 1) * Q)
            for g in range(8):
                cs = slice(g * 128, (g + 1) * 128)
                vb = vn[rs_, cs]
                mixed = _dot(ws_ref[g], vb, NN) + bs_ref[:, g:g + 1]
                dyb = dym[rs_, cs]
                dmx = dyb * ug[rs_, cs]
                du_ref[rs_, cs] = (dyb * mixed * gu[rs_, cs]).astype(du_ref.dtype)
                dvn_s[rs_, cs] = _dot(ws_ref[g], dmx, TN)
                dws_ref[g] += _dot(dmx, vb, NT)
                dbs = dbs + jnp.where(lane == g, jnp.sum(dmx, axis=1, keepdims=True), 0.0)
        dbs_ref[...] += dbs
        dvn = dvn_s[...]
        st_ref[1:2, :] += _colsum(dvn * vhat)
        st_ref[2:3, :] += _colsum(dvn)
        dvg = _ln_bwd(dvn * gg_ref[...], vhat, vrstd)
        dv_ref[...] = (dvg * _gelu_grad(v, tv)).astype(dv_ref.dtype)

    outs = pl.pallas_call(
        body, name="mix_bwd", grid=(nt,),
        in_specs=[_rt(D), _rt(D), _rt(D), _rt(D), _rt(D, OZ // D), _rt(D, 0), _rt(D, OU // D),
                  _rt(D, OV // D), _cst((1, D)), _cst((1, D)), _cst((1, D)), _cst((1, D)),
                  _cst((8, 128, 128)), _cst((128, 128)), pl.BlockSpec(memory_space=pl.ANY)],
        out_specs=[_rt(3 * D, 0), _rt(D), _cst((8, D)),
                   _cst((8, 128, 128)), _cst((128, 128))],
        out_shape=[jax.ShapeDtypeStruct(dp.shape, dp.dtype),
                   jax.ShapeDtypeStruct((L, D), F32), jax.ShapeDtypeStruct((8, D), F32),
                   jax.ShapeDtypeStruct((8, 128, 128), F32), jax.ShapeDtypeStruct((128, 128), F32)],
        scratch_shapes=[pltpu.VMEM((TL, D), F32)],
        input_output_aliases={14: 0},
        compiler_params=_params(("arbitrary",)),
    )(dys, dym, yf, yb, p, xbc, p, p, dsk, sg, gg, gb, ws, bsT, dp)
    return outs


def _gate_fwd(a1, a2, p, bg):
    L = a1.shape[0]

    def body(a1_ref, a2_ref, g_ref, bg_ref, m_ref):
        gt = _sig(g_ref[...] + bg_ref[...])
        m_ref[...] = (gt[:, :D] * a1_ref[...] + gt[:, D:] * a2_ref[...]).astype(m_ref.dtype)

    return pl.pallas_call(
        body, name="gate_fwd", grid=(L // TL,),
        in_specs=[_rt(D), _rt(D), _rt(2 * D, OG // (2 * D)), _cst((1, 2 * D))],
        out_specs=_rt(D), out_shape=jax.ShapeDtypeStruct((L, D), _MXU),
        compiler_params=_params(("parallel",)),
    )(a1, a2, p, bg)


def _gate_bwd(dmg, a1, a2, p, bg, dp):
    L = a1.shape[0]

    def body(dm_ref, a1_ref, a2_ref, g_ref, bg_ref, dp_any, dg_ref, da1_ref, da2_ref, st_ref):
        del dp_any

        @pl.when(pl.program_id(0) == 0)
        def _():
            st_ref[...] = jnp.zeros_like(st_ref)

        gt = _sig(g_ref[...] + bg_ref[...])
        g1 = gt[:, :D]
        g2 = gt[:, D:]
        dm = dm_ref[...]
        da1_ref[...] = (dm * g1).astype(da1_ref.dtype)
        da2_ref[...] = (dm * g2).astype(da2_ref.dtype)
        dg1 = dm * a1_ref[...] * g1 * (1.0 - g1)
        dg2 = dm * a2_ref[...] * g2 * (1.0 - g2)
        st_ref[0:1, 0:D] += _colsum(dg1)
        st_ref[0:1, D:2 * D] += _colsum(dg2)
        dg_ref[:, 0:D] = dg1.astype(dg_ref.dtype)
        dg_ref[:, D:2 * D] = dg2.astype(dg_ref.dtype)

    return pl.pallas_call(
        body, name="gate_bwd", grid=(L // TL,),
        in_specs=[_rt(D), _rt(D), _rt(D), _rt(2 * D, OG // (2 * D)), _cst((1, 2 * D)),
                  pl.BlockSpec(memory_space=pl.ANY)],
        out_specs=[_rt(2 * D, OG // (2 * D)), _rt(D), _rt(D), _cst((8, 2 * D))],
        out_shape=[jax.ShapeDtypeStruct(dp.shape, dp.dtype), jax.ShapeDtypeStruct((L, D), _MXU),
                   jax.ShapeDtypeStruct((L, D), _MXU), jax.ShapeDtypeStruct((8, 2 * D), F32)],
        input_output_aliases={5: 0},
        compiler_params=_params(("arbitrary",)),
    )(dmg, a1, a2, p, bg, dp)


def _res1_fwd(xn, out, modx, g, b):
    L = out.shape[0]

    def body(xn_ref, o_ref, mx_ref, g_ref, b_ref, r1_ref, h2_ref):
        r1 = ALPHA * xn_ref[...] + mx_ref[2:3, :] * o_ref[...]
        xhat, _ = _ln(r1)
        x1 = xhat * g_ref[...] + b_ref[...]
        r1_ref[...] = r1
        h2_ref[...] = (x1 * (1.0 + mx_ref[4:5, :]) + mx_ref[3:4, :]).astype(h2_ref.dtype)

    return pl.pallas_call(
        body, name="res1_fwd", grid=(L // TL,),
        in_specs=[_rt(D), _rt(D), _cst((8, D)), _cst((1, D)), _cst((1, D))],
        out_specs=[_rt(D), _rt(D)],
        out_shape=[jax.ShapeDtypeStruct((L, D), F32), jax.ShapeDtypeStruct((L, D), _MXU)],
        compiler_params=_params(("parallel",)),
    )(xn, out, modx, g, b)


def _glu_fwd(f13):
    L = f13.shape[0]

    def body(f1_ref, f3_ref, o_ref):
        f1 = f1_ref[...]
        o_ref[...] = (f1 * _sig(f1) * f3_ref[...]).astype(o_ref.dtype)

    return pl.pallas_call(
        body, name="glu_fwd", grid=(L // TL,),
        in_specs=[_rt(DFF, 0), _rt(DFF, 1)], out_specs=_rt(DFF),
        out_shape=jax.ShapeDtypeStruct((L, DFF), _MXU),
        compiler_params=_params(("parallel",)),
    )(f13, f13)


def _glu_bwd(dff, f13):
    L = f13.shape[0]

    def body(d_ref, f1_ref, f3_ref, o_ref):
        f1 = f1_ref[...]
        s = _sig(f1)
        d = d_ref[...]
        o_ref[:, 0:DFF] = (d * f3_ref[...] * (s * (1.0 + f1 * (1.0 - s)))).astype(o_ref.dtype)
        o_ref[:, DFF:2 * DFF] = (d * f1 * s).astype(o_ref.dtype)

    return pl.pallas_call(
        body, name="glu_bwd", grid=(L // TL,),
        in_specs=[_rt(DFF), _rt(DFF, 0), _rt(DFF, 1)], out_specs=_rt(2 * DFF),
        out_shape=jax.ShapeDtypeStruct((L, 2 * DFF), _MXU),
        compiler_params=_params(("parallel",)),
    )(dff, f13, f13)


def _res2(r1, o2, tgt, modx, g1, b1, g2, b2):
    L = r1.shape[0]

    def body(r1_ref, o2_ref, t_ref, mx_ref, g1_ref, b1_ref, g2_ref, b2_ref,
             dr2_ref, do2_ref, st_ref, loss_ref):
        @pl.when(pl.program_id(0) == 0)
        def _():
            st_ref[...] = jnp.zeros_like(st_ref)
            loss_ref[...] = jnp.zeros_like(loss_ref)

        xh1, _ = _ln(r1_ref[...])
        x1 = xh1 * g1_ref[...] + b1_ref[...]
        o2 = o2_ref[...]
        g2x = mx_ref[5:6, :]
        xh2, rstd2 = _ln(ALPHA * x1 + g2x * o2)
        err = xh2 * g2_ref[...] + b2_ref[...] - t_ref[...]
        per_tok = jnp.mean(err * err, axis=-1, keepdims=True)
        loss_ref[...] += 0.5 * jnp.sum(per_tok, axis=0, keepdims=True)
        dy = err * (1.0 / D)
        st_ref[0:1, :] += _colsum(dy * xh2)
        st_ref[1:2, :] += _colsum(dy)
        dr2 = _ln_bwd(dy * g2_ref[...], xh2, rstd2)
        st_ref[2:3, :] += _colsum(dr2 * o2)
        dr2_ref[...] = dr2
        do2_ref[...] = (g2x * dr2).astype(do2_ref.dtype)

    return pl.pallas_call(
        body, name="res2", grid=(L // TL,),
        in_specs=[_rt(D), _rt(D), _rt(D), _cst((8, D))] + [_cst((1, D))] * 4,
        out_specs=[_rt(D), _rt(D), _cst((8, D)), _cst((8, 128))],
        out_shape=[jax.ShapeDtypeStruct((L, D), F32), jax.ShapeDtypeStruct((L, D), _MXU),
                   jax.ShapeDtypeStruct((8, D), F32), jax.ShapeDtypeStruct((8, 128), F32)],
        compiler_params=_params(("arbitrary",)),
    )(r1, o2, tgt, modx, g1, b1, g2, b2)


def _res1_bwd(dr2, dh2, r1, out, modx, g1, b1):
    L = r1.shape[0]

    def body(dr2_ref, dh2_ref, r1_ref, o_ref, mx_ref, g_ref, b_ref, dr1_ref, do_ref, st_ref):
        @pl.when(pl.program_id(0) == 0)
        def _():
            st_ref[...] = jnp.zeros_like(st_ref)

        xh1, rstd1 = _ln(r1_ref[...])
        x1 = xh1 * g_ref[...] + b_ref[...]
        dh2 = dh2_ref[...]
        dx1 = ALPHA * dr2_ref[...] + dh2 * (1.0 + mx_ref[4:5, :])
        st_ref[0:1, :] += _colsum(dh2 * x1)
        st_ref[1:2, :] += _colsum(dh2)
        st_ref[2:3, :] += _colsum(dx1 * xh1)
        st_ref[3:4, :] += _colsum(dx1)
        dr1 = _ln_bwd(dx1 * g_ref[...], xh1, rstd1)
        st_ref[4:5, :] += _colsum(dr1 * o_ref[...])
        dr1_ref[...] = dr1
        do_ref[...] = (mx_ref[2:3, :] * dr1).astype(do_ref.dtype)

    return pl.pallas_call(
        body, name="res1_bwd", grid=(L // TL,),
        in_specs=[_rt(D), _rt(D), _rt(D), _rt(D), _cst((8, D)), _cst((1, D)), _cst((1, D))],
        out_specs=[_rt(D), _rt(D), _cst((8, D))],
        out_shape=[jax.ShapeDtypeStruct((L, D), F32), jax.ShapeDtypeStruct((L, D), _MXU),
                   jax.ShapeDtypeStruct((8, D), F32)],
        compiler_params=_params(("arbitrary",)),
    )(dr2, dh2, r1, out, modx, g1, b1)


def _conv_bwd(dxf, dxb, p, conv_w8, conv_b, dp):
    RT = p.shape[0]
    chunks = _seq_chunks(RT - TL)

    def body(df_ref, db_ref, p_ref, w_ref, b_ref, dp_any, o_ref, dw_ref, dbias_ref, dpre_s):
        del dp_any
        w = w_ref[...]
        bias = b_ref[...]
        srow = lax.broadcasted_iota(jnp.int32, (8, 128), 0)
        dwacc = jnp.zeros((8, 128), F32)
        dbacc = jnp.zeros((1, 128), F32)
        for r0, first, last in chunks:
            taps = _conv_taps(p_ref, r0, first, last)
            pre = bias + sum(w[k:k + 1, :] * taps[k] for k in range(5))
            s = _sig(pre)
            dpre = (df_ref[pl.ds(r0, TL), :] + db_ref[pl.ds(r0, TL), :]) * (s * (1.0 + pre * (1.0 - s)))
            dpre_s[pl.ds(r0, TL), :] = dpre
            dbacc = dbacc + _colsum(dpre)
            for k in range(5):
                dwacc = dwacc + jnp.where(srow == k, _colsum(dpre * taps[k]), 0.0)
        for r0, first, last in chunks:
            taps = _conv_taps(dpre_s, r0, first, last)
            dx = sum(w[k:k + 1, :] * taps[4 - k] for k in range(5))
            o_ref[pl.ds(r0, TL), :] = dx.astype(o_ref.dtype)
        dw_ref[...] = dwacc
        dbias_ref[...] = jnp.broadcast_to(dbacc, (8, 128))

    cspec = pl.BlockSpec((RT, 128), lambda j: (0, j))
    wspec = pl.BlockSpec((8, 128), lambda j: (0, j))
    return pl.pallas_call(
        body, name="conv_bwd", grid=(12,),
        in_specs=[cspec, cspec, pl.BlockSpec((RT, 128), lambda j: (0, _xbc_colblk(j))),
                  wspec, pl.BlockSpec((1, 128), lambda j: (0, j)), pl.BlockSpec(memory_space=pl.ANY)],
        out_specs=[pl.BlockSpec((RT, 128), lambda j: (0, _xbc_colblk(j))), wspec, wspec],
        out_shape=[jax.ShapeDtypeStruct(dp.shape, dp.dtype), jax.ShapeDtypeStruct((8, 1536), F32),
                   jax.ShapeDtypeStruct((8, 1536), F32)],
        scratch_shapes=[pltpu.VMEM((RT, 128), F32)],
        input_output_aliases={5: 0},
        compiler_params=_params(("parallel",)),
    )(dxf, dxb, p, conv_w8, conv_b, dp)


def _dt_bwd(ddf, ddb, dp):
    RT = ddf.shape[0]

    def body(f_ref, b_ref, dp_any, o_ref, st_ref):
        del dp_any

        @pl.when(pl.program_id(0) == 0)
        def _():
            st_ref[...] = jnp.zeros_like(st_ref)

        s = f_ref[...] + b_ref[...]
        o_ref[...] = s.astype(o_ref.dtype)
        st_ref[0:1, :] += _colsum(s)

    return pl.pallas_call(
        body, name="dt_bwd", grid=(RT // TL,),
        in_specs=[_rt(128), _rt(128), pl.BlockSpec(memory_space=pl.ANY)],
        out_specs=[_rt(128, ODT // 128), _cst((8, 128))],
        out_shape=[jax.ShapeDtypeStruct(dp.shape, dp.dtype), jax.ShapeDtypeStruct((8, 128), F32)],
        input_output_aliases={2: 0},
        compiler_params=_params(("arbitrary",)),
    )(ddf, ddb, dp)


def _ln0_bwd(dh1, dr1, x, ctx, g, b, modx, modc):
    L = x.shape[0]
    nt = L // TL

    def body(dh_ref, dr1_ref, x_ref, c_ref, g_ref, b_ref, mx_ref, mc_ref, gx_ref, st_ref):
        i = pl.program_id(0)
        isc = i == nt

        @pl.when(i == 0)
        def _():
            st_ref[...] = jnp.zeros_like(st_ref)

        xin = jnp.where(isc, c_ref[...], x_ref[...])
        xhat, rstd = _ln(xin)
        xn = xhat * g_ref[...] + b_ref[...]
        sc = jnp.where(isc, mc_ref[1:2, :], mx_ref[1:2, :])
        dh = dh_ref[...]
        lat = jnp.where(isc, 0.0, 1.0)
        dxn = dh * (1.0 + sc) + (lat * ALPHA) * dr1_ref[...]
        tsh = _colsum(dh)
        tsc = _colsum(dh * xn)
        st_ref[0:1, :] += lat * tsh
        st_ref[1:2, :] += lat * tsc
        st_ref[2:3, :] += (1.0 - lat) * tsh
        st_ref[3:4, :] += (1.0 - lat) * tsc
        st_ref[4:5, :] += _colsum(dxn * xhat)
        st_ref[5:6, :] += _colsum(dxn)

        @pl.when(i < nt)
        def _():
            gx_ref[...] = _ln_bwd(dxn * g_ref[...], xhat, rstd)

    return pl.pallas_call(
        body, name="ln0_bwd", grid=(nt + 1,),
        in_specs=[_rt(D), _rtc(D, nt), _rtc(D, nt), _cst((TL, D)), _cst((1, D)), _cst((1, D)),
                  _cst((8, D)), _cst((8, D))],
        out_specs=[_rtc(D, nt), _cst((8, D))],
        out_shape=[jax.ShapeDtypeStruct((L, D), F32), jax.ShapeDtypeStruct((8, D), F32)],
        compiler_params=_params(("arbitrary",)),
    )(dh1, dr1, x, ctx, g, b, modx, modc)


def _perm_cols(w):
    pad = jnp.zeros((w.shape[0], NPJ - NNAT), w.dtype)
    return jnp.concatenate([w[:, 0:1024], w[:, 2592:3616], w[:, 3616:4640], w[:, 1024:2048],
                            w[:, 4640:6688], w[:, 2048:2304], w[:, 2304:2560], w[:, 2560:2592], pad],
                           axis=1)


SECTIONS = ((0, 1024, OZ), (1024, 2048, OXS), (2048, 2304, OB), (2304, 2560, OC), (2560, 2592, ODT),
            (2592, 3616, OU), (3616, 4640, OV), (4640, 6688, OG))


def _perm_from_blocks(ga):
    n = ga.shape[2]
    pieces = []
    for na, nb, _ in sorted(SECTIONS, key=lambda sec: sec[2]):
        for k in range(NDEV):
            lo, hi = max(na, k * n), min(nb, (k + 1) * n)
            if lo < hi:
                pieces.append(ga[k][:, lo - k * n:hi - k * n])
    pieces.append(jnp.zeros((ga.shape[1], NPJ - NNAT), ga.dtype))
    return jnp.concatenate(pieces, axis=1)


def _blocks_from_perm(gp, n):
    blocks = []
    for k in range(NDEV):
        pieces = []
        for na, nb, po in SECTIONS:
            lo, hi = max(na, k * n), min(nb, (k + 1) * n)
            if lo < hi:
                pieces.append(gp[:, po + lo - na:po + hi - na])
        blocks.append(jnp.concatenate(pieces, axis=1))
    return jnp.stack(blocks)


def _padded(n, row_align):
    unit = row_align * D
    return -(-n // unit) * unit if row_align else n


def _slab(arrs, rows, row_align=0):
    parts = []
    for a in arrs:
        f = a.reshape(-1)
        parts.append(jnp.pad(f, (0, _padded(f.shape[0], row_align) - f.shape[0])))
    flat = jnp.concatenate(parts)
    flat = jnp.pad(flat, (0, rows * D - flat.shape[0]))
    return flat.reshape(rows, D)


def _unslab(slab, shapes, row_align=0):
    flat = slab.reshape(-1)
    out, off = [], 0
    for shp in shapes:
        n = 1
        for s in shp:
            n *= s
        out.append(flat[off:off + n].reshape(shp))
        off += _padded(n, row_align)
    return out


def _row(v):
    return v.reshape(1, -1)


def _pad_rows(a, rows):
    return jnp.pad(a, ((0, rows - a.shape[0]), (0, 0)))


BIG = ["w_in", "w_ssd_proj", "w_gm_proj", "w_out", "w_ff1", "w_ff3", "w_ff2"]
BIG_ROWS = 2304
BIG_ALIGN = 16
REPL = ["c_ctx", "ln0_g", "ln0_b", "b_ada", "conv_b", "dt_bias", "a_log", "d_skip", "ssd_norm_g",
        "gm_norm_g", "gm_norm_b", "w_spatial", "b_spatial", "b_gate", "ln1_g", "ln1_b", "ln2_g", "ln2_b"]
SMALL_ROWS = 160
WEIGHTS = ["c_ctx", "ln0_g", "ln0_b", "w_ada", "b_ada", "w_in", "conv_w", "conv_b", "dt_bias", "a_log",
           "d_skip", "ssd_norm_g", "gm_norm_g", "gm_norm_b", "w_spatial", "b_spatial", "b_gate",
           "w_ssd_proj", "w_gm_proj", "w_out", "ln1_g", "ln1_b", "w_ff1", "w_ff3", "w_ff2", "ln2_g", "ln2_b"]


def kernel(x, c, ctx, c_ctx, ln0_g, ln0_b, w_ada, b_ada, w_in, conv_w, conv_b, dt_bias, a_log, d_skip, ssd_norm_g, gm_norm_g, gm_norm_b, w_spatial, b_spatial, b_gate, w_ssd_proj, w_gm_proj, w_out, ln1_g, ln1_b, w_ff1, w_ff3, w_ff2, ln2_g, ln2_b, loss_target, m_c_ctx, m_ln0_g, m_ln0_b, m_w_ada, m_b_ada, m_w_in, m_conv_w, m_conv_b, m_dt_bias, m_a_log, m_d_skip, m_ssd_norm_g, m_gm_norm_g, m_gm_norm_b, m_w_spatial, m_b_spatial, m_b_gate, m_w_ssd_proj, m_w_gm_proj, m_w_out, m_ln1_g, m_ln1_b, m_w_ff1, m_w_ff3, m_w_ff2, m_ln2_g, m_ln2_b, v_c_ctx, v_ln0_g, v_ln0_b, v_w_ada, v_b_ada, v_w_in, v_conv_w, v_conv_b, v_dt_bias, v_a_log, v_d_skip, v_ssd_norm_g, v_gm_norm_g, v_gm_norm_b, v_w_spatial, v_b_spatial, v_b_gate, v_w_ssd_proj, v_w_gm_proj, v_w_out, v_ln1_g, v_ln1_b, v_w_ff1, v_w_ff3, v_w_ff2, v_ln2_g, v_ln2_b):
    W = dict(c_ctx=c_ctx, ln0_g=ln0_g, ln0_b=ln0_b, w_ada=w_ada, b_ada=b_ada, w_in=w_in, conv_w=conv_w,
             conv_b=conv_b, dt_bias=dt_bias, a_log=a_log, d_skip=d_skip, ssd_norm_g=ssd_norm_g,
             gm_norm_g=gm_norm_g, gm_norm_b=gm_norm_b, w_spatial=w_spatial, b_spatial=b_spatial,
             b_gate=b_gate, w_ssd_proj=w_ssd_proj, w_gm_proj=w_gm_proj, w_out=w_out, ln1_g=ln1_g,
             ln1_b=ln1_b, w_ff1=w_ff1, w_ff3=w_ff3, w_ff2=w_ff2, ln2_g=ln2_g, ln2_b=ln2_b)
    M = dict(c_ctx=m_c_ctx, ln0_g=m_ln0_g, ln0_b=m_ln0_b, w_ada=m_w_ada, b_ada=m_b_ada, w_in=m_w_in,
             conv_w=m_conv_w, conv_b=m_conv_b, dt_bias=m_dt_bias, a_log=m_a_log, d_skip=m_d_skip,
             ssd_norm_g=m_ssd_norm_g, gm_norm_g=m_gm_norm_g, gm_norm_b=m_gm_norm_b,
             w_spatial=m_w_spatial, b_spatial=m_b_spatial, b_gate=m_b_gate, w_ssd_proj=m_w_ssd_proj,
             w_gm_proj=m_w_gm_proj, w_out=m_w_out, ln1_g=m_ln1_g, ln1_b=m_ln1_b, w_ff1=m_w_ff1,
             w_ff3=m_w_ff3, w_ff2=m_w_ff2, ln2_g=m_ln2_g, ln2_b=m_ln2_b)
    V = dict(c_ctx=v_c_ctx, ln0_g=v_ln0_g, ln0_b=v_ln0_b, w_ada=v_w_ada, b_ada=v_b_ada, w_in=v_w_in,
             conv_w=v_conv_w, conv_b=v_conv_b, dt_bias=v_dt_bias, a_log=v_a_log, d_skip=v_d_skip,
             ssd_norm_g=v_ssd_norm_g, gm_norm_g=v_gm_norm_g, gm_norm_b=v_gm_norm_b,
             w_spatial=v_w_spatial, b_spatial=v_b_spatial, b_gate=v_b_gate, w_ssd_proj=v_w_ssd_proj,
             w_gm_proj=v_w_gm_proj, w_out=v_w_out, ln1_g=v_ln1_g, ln1_b=v_ln1_b, w_ff1=v_w_ff1,
             w_ff3=v_w_ff3, w_ff2=v_w_ff2, ln2_g=v_ln2_g, ln2_b=v_ln2_b)

    me = 4 * lax.axis_index("x") + 2 * lax.axis_index("y") + lax.axis_index("c")
    xl, cx, tgt = x[0], ctx[0], loss_target[0]
    L = xl.shape[0]
    assert cx.shape[0] == TL and L % TL == 0
    ada_n = w_ada.shape[2]
    cw_n = conv_w.shape[2]

    small1 = _pad_rows(jnp.concatenate([c, _slab([conv_w[0]], 1)], axis=0), 8)
    g1 = _all_gather(small1, "ag_small")
    c_all = g1[:, 0, :]
    conv_w_full = g1[:, 1, :5 * cw_n].reshape(NDEV, 5, cw_n).transpose(1, 0, 2).reshape(5, NDEV * cw_n)
    sq = w_ssd_proj.shape[1]
    ffr = w_ff2.shape[1]
    ffc = w_ff1.shape[2]
    rows_local = jnp.concatenate([w_ssd_proj[0], w_gm_proj[0], w_out[0], w_ff2[0]], axis=0)
    ga, gb, gc1, gc2 = _all_gather_multi(
        [w_in[0].astype(_MXU), rows_local.astype(_MXU), w_ff1[0].astype(_MXU), w_ff3[0].astype(_MXU)],
        "ag_weights")
    w_in_p = _perm_from_blocks(ga)
    w_ssd_f = gb[:, 0:sq].reshape(NDEV * sq, D)
    w_gm_f = gb[:, sq:2 * sq].reshape(NDEV * sq, D)
    w_out_f = gb[:, 2 * sq:3 * sq].reshape(NDEV * sq, D)
    w_ff2_f = gb[:, 3 * sq:3 * sq + ffr].reshape(NDEV * ffr, D)
    w13 = jnp.concatenate([gc1[k] for k in range(NDEV)] + [gc2[k] for k in range(NDEV)], axis=1)

    c16 = _pad_rows(jnp.concatenate([c_all, _row(c_ctx)], axis=0), 16)
    b_ada_sh = lax.dynamic_slice(b_ada, (0, ada_n * me), (1, ada_n))
    modp = _ada_fwd(c16, w_ada[0], b_ada_sh)
    mod16 = _all_gather(modp, "ag_mod").transpose(1, 0, 2).reshape(16, NDEV * ada_n)
    modx = _pad_rows(lax.dynamic_slice(mod16, (me, 0), (1, 6 * D)).reshape(6, D), 8)
    modc = _pad_rows(mod16[8].reshape(6, D), 8)

    g0, b0 = _row(ln0_g), _row(ln0_b)
    xn, h1 = _ln0_fwd(xl, cx, g0, b0, modx, modc)
    p = _mm(h1, w_in_p, "nn", F32, "mm_p")
    conv_w8 = _pad_rows(conv_w_full, 8)
    xbc = _conv_fwd(p, conv_w8, conv_b)
    prm = _pad_rows(jnp.pad(jnp.stack([dt_bias.reshape(32), a_log.reshape(32)]), ((0, 0), (0, 96))), 8)
    yf, yb, hpf, hpb = _ssd2_fwd(xbc, p, prm)
    dsk = _row(jnp.repeat(d_skip[0, 0] + d_skip[0, 1], HP))
    ws_m = w_spatial[0].astype(_MXU)
    bsT = jnp.pad(b_spatial[0].T, ((0, 0), (0, 120)))
    mixp = (dsk, ssd_norm_g, gm_norm_g, gm_norm_b, ws_m, bsT)
    yssd, ygm = _mix_fwd(yf, yb, p, xbc, *mixp)
    a1 = _mm(yssd, w_ssd_f, "nn", F32, "mm_a1")
    a2 = _mm(ygm, w_gm_f, "nn", F32, "mm_a2")
    merged = _gate_fwd(a1, a2, p, b_gate)
    out = _mm(merged, w_out_f, "nn", F32, "mm_out")
    r1, h2 = _res1_fwd(xn, out, modx, ln1_g, ln1_b)
    f13 = _mm(h2, w13, "nn", F32, "mm_f13")
    ff = _glu_fwd(f13)
    o2 = _mm(ff, w_ff2_f, "nn", F32, "mm_o2")

    dr2, do2, st2, loss_slab = _res2(r1, o2, tgt, modx, ln1_g, ln1_b, ln2_g, ln2_b)
    loss = lax.psum(loss_slab[0, 0], ("x", "y", "c"))
    dff = _mm(do2, w_ff2_f, "nt", F32, "mm_dff")
    df13 = _glu_bwd(dff, f13)
    dh2 = _mm(df13, w13, "nt", F32, "mm_dh2")
    dw_ff2 = _mm(ff, do2, "tn", _MXU, "mm_dw_ff2")
    dw13 = _mm(h2, df13, "tn", _MXU, "mm_dw13")
    xff = [dw_ff2.reshape(NDEV, ffr, D),
           jnp.stack([dw13[:, k * ffc:(k + 1) * ffc] for k in range(NDEV)]),
           jnp.stack([dw13[:, DFF + k * ffc:DFF + (k + 1) * ffc] for k in range(NDEV)])]
    ff_send, ff_recv, ff_src, ff_land, ff_token = _exchange_start(xff, "xchg_ff_start")
    modx = modx + ff_token[0, 0]
    dr1, dout, st1 = _res1_bwd(dr2, dh2, r1, out, modx, ln1_g, ln1_b)
    dmg = _mm(dout, w_out_f, "nt", F32, "mm_dmerged")
    dw_out = _mm(merged, dout, "tn", _MXU, "mm_dw_out")
    dp = jnp.zeros((L + TL, NPJ), _MXU)
    dp, da1, da2, stg = _gate_bwd(dmg, a1, a2, p, b_gate, dp)
    dys = _mm(da1, w_ssd_f, "nt", F32, "mm_dyssd")
    dym = _mm(da2, w_gm_f, "nt", F32, "mm_dygm")
    dw_ssd = _mm(yssd, da1, "tn", _MXU, "mm_dw_ssd")
    dw_gm = _mm(ygm, da2, "tn", _MXU, "mm_dw_gm")
    dp, dyd, stm, dws, dbsT = _mix_bwd(dys, dym, yf, yb, p, xbc, dp, *mixp)
    dxf, dxb, ddf, ddb, sts = _ssd2_bwd(xbc, p, prm, dsk, dyd, hpf, hpb)
    dp, dcw, dcb = _conv_bwd(dxf, dxb, p, conv_w8, conv_b, dp)
    dp, std = _dt_bwd(ddf, ddb, dp)
    dh1 = _mm(dp, w_in_p, "nt", F32, "mm_dh1")
    dw_in_p = _mm(h1, dp, "tn", _MXU, "mm_dw_in")
    grad_x, st0 = _ln0_bwd(dh1, dr1, xl, cx, g0, b0, modx, modc)

    zero = jnp.zeros((D,), F32)
    dmod = jnp.stack([jnp.concatenate([st0[0], st0[1], st1[4], st1[1], st1[0], st2[2]]),
                      jnp.concatenate([st0[2], st0[3], zero, zero, zero, zero])])
    g16 = _all_gather(_pad_rows(dmod, 8), "ag_dmod")[:, 0:2, :].reshape(16, 6 * D)
    g16_sh = lax.dynamic_slice(g16, (0, ada_n * me), (16, ada_n))
    c16b = jnp.stack([c_all, jnp.broadcast_to(_row(c_ctx), (NDEV, D))], axis=1).reshape(16, D)
    dw_ada, db_ada8, dcc8 = _ada_bwd(c16b, g16, g16_sh, w_ada[0])

    part = dict(
        c_ctx=dcc8[0], ln0_g=st0[4], ln0_b=st0[5], conv_w=dcw[0:5], conv_b=dcb[0],
        dt_bias=std[0, 0:32], a_log=sts[0, 0:32], d_skip=jnp.tile(sts[1, 0:16], 2),
        ssd_norm_g=stm[0], gm_norm_g=stm[1], gm_norm_b=stm[2], w_spatial=dws,
        b_spatial=dbsT[:, 0:8].T, b_gate=stg[0], ln1_g=st1[2], ln1_b=st1[3], ln2_g=st2[0], ln2_b=st2[1])
    pnames = list(part)
    psum8 = _sum8(_all_gather(_slab([part[n] for n in pnames], SMALL_ROWS), "ag_smallgrads"), "sum_smallgrads")
    small = dict(zip(pnames, _unslab(psum8, [part[n].shape for n in pnames])))
    grads = {n: small[n].reshape(W[n].shape) for n in pnames if n != "conv_w"}
    grads["conv_w"] = lax.dynamic_slice(small["conv_w"], (0, cw_n * me), (5, cw_n)).reshape(conv_w.shape)
    grads["b_ada"] = db_ada8[0:1]
    grads["w_ada"] = dw_ada.reshape(w_ada.shape)

    xa = _blocks_from_perm(dw_in_p, w_in.shape[2])
    xb = jnp.concatenate([dw_ssd.reshape(NDEV, sq, D), dw_gm.reshape(NDEV, sq, D),
                          dw_out.reshape(NDEV, sq, D)], axis=1)
    ra, rb = _owner_exchange_multi([xa, xb], "xchg_grads")
    rff = _exchange_wait(ff_send, ff_recv, ff_src, ff_land, ra, "xchg_ff_wait")
    own_ff = [lax.dynamic_index_in_dim(g, me, 0, keepdims=False) for g in xff]

    delta, new_m, new_v = {}, {}, {}
    for n, r8, own, row0, tr in (
            ("w_in", ra, None, 0, 256), ("w_ssd_proj", rb, None, 0, sq), ("w_gm_proj", rb, None, sq, sq),
            ("w_out", rb, None, 2 * sq, sq), ("w_ff2", rff[0], own_ff[0], 0, ffr // 2),
            ("w_ff1", rff[1], own_ff[1], 0, 256), ("w_ff3", rff[2], own_ff[2], 0, 256)):
        res = _adamw_sum(r8, own, W[n][0], M[n][0], V[n][0], row0, tr, "adamw_" + n)
        grads[n], delta[n], new_m[n], new_v[n] = [a[None] for a in res]

    def adam_group(names, rows, tag, align=0):
        shapes = [W[n].shape for n in names]
        outs = _adamw(*[_slab([src[n] for n in names], rows, align) for src in (grads, W, M, V)], tag)
        for res, slab in zip((delta, new_m, new_v), outs):
            for n, a in zip(names, _unslab(slab, shapes, align)):
                res[n] = a

    adam_group(["w_ada", "conv_w"], ada_n + 256, "adamw_shard")
    adam_group(REPL, SMALL_ROWS, "adamw_repl")

    return (loss, grad_x[None], *[grads[n] for n in WEIGHTS], *[delta[n] for n in WEIGHTS],
            *[new_m[n] for n in WEIGHTS], *[new_v[n] for n in WEIGHTS])
```

```python
import functools

import jax
import jax.numpy as jnp
from jax import lax
from jax.experimental import pallas as pl
from jax.experimental.pallas import tpu as pltpu

_MXU = jnp.bfloat16
F32 = jnp.float32
D = 1024
TL = 256
Q = 128
NH, HP, NS, HPG = 16, 64, 128, 8
DFF = 2816
ALPHA = 2.0 ** 0.25
EPS = 1e-5
OZ, OU, OV, OXS, OG, OB, OC, ODT, NPJ = 0, 1024, 2048, 3072, 4096, 6144, 6400, 6656, 6912
NNAT = 6688
NDEV = 8
ADAM_LR, ADAM_B1, ADAM_B2, ADAM_EPS, ADAM_WD, ADAM_STEP = 1e-3, 0.9, 0.999, 1e-8, 0.01, 10
VMEM_LIMIT = 48 * 1024 * 1024

NN = ((1,), (0,))
NT = ((1,), (1,))
TN = ((0,), (0,))
MESH = pl.DeviceIdType.MESH


def _dot(a, b, dims):
    return lax.dot_general(a.astype(_MXU), b.astype(_MXU), (dims, ((), ())),
                           preferred_element_type=F32)


def _tile(n, cands):
    for c in cands:
        if n % c == 0:
            return c
    return n


def _divisor_tile(n, cap, mult):
    best = n
    for t in range(mult, min(n, cap) + 1, mult):
        if n % t == 0:
            best = t
    return best


def _params(sem):
    return pltpu.CompilerParams(dimension_semantics=sem, vmem_limit_bytes=VMEM_LIMIT)


def _cst(shape):
    nd = len(shape)
    return pl.BlockSpec(shape, lambda *_: (0,) * nd)


def _rt(w, cb=0, rows=TL):
    return pl.BlockSpec((rows, w), lambda i: (i, cb))


def _rtc(w, nt, cb=0):
    return pl.BlockSpec((TL, w), lambda i: (jnp.minimum(i, nt - 1), cb))


def _sig(x):
    return jax.nn.sigmoid(x)


def _softplus(x):
    return jnp.maximum(x, 0.0) + jnp.log1p(jnp.exp(-jnp.abs(x)))


_G0, _G1 = 0.7978845608028654, 0.044715


def _gelu(x):
    t = jnp.tanh(_G0 * (x + _G1 * x * x * x))
    return 0.5 * x * (1.0 + t), t


def _gelu_grad(x, t):
    return 0.5 * (1.0 + t) + 0.5 * x * (1.0 - t * t) * _G0 * (1.0 + 3.0 * _G1 * x * x)


def _ln(r):
    mu = jnp.mean(r, axis=-1, keepdims=True)
    xc = r - mu
    var = jnp.mean(xc * xc, axis=-1, keepdims=True)
    rstd = lax.rsqrt(var + EPS)
    return xc * rstd, rstd


def _ln_bwd(dyh, xhat, rstd):
    return rstd * (dyh - jnp.mean(dyh, axis=-1, keepdims=True)
                   - xhat * jnp.mean(dyh * xhat, axis=-1, keepdims=True))


def _colsum(v):
    return jnp.sum(v, axis=0, keepdims=True)


def _sum11(v):
    return jnp.sum(jnp.sum(v, axis=1, keepdims=True), axis=0, keepdims=True)


def _cumsum_rows(a, rev):
    n = a.shape[0]
    row = lax.broadcasted_iota(jnp.int32, a.shape, 0)
    s = 1
    while s < n:
        if rev:
            a = a + jnp.where(row < n - s, pltpu.roll(a, n - s, 0), 0.0)
        else:
            a = a + jnp.where(row >= s, pltpu.roll(a, s, 0), 0.0)
        s *= 2
    return a


def _mm(a, b, mode, out_dtype, name):
    if mode == "tn":
        K, M = a.shape
    else:
        M, K = a.shape
    N = b.shape[0] if mode == "nt" else b.shape[1]
    tm = _divisor_tile(M, 1408, 128) if mode == "tn" else _divisor_tile(M, 1088, 16)
    tn = _divisor_tile(N, 1408, 128)
    tk = _divisor_tile(K, 2304, 128)
    nk = K // tk
    dims = {"nn": NN, "nt": NT, "tn": TN}[mode]
    use_acc = nk > 1 and out_dtype != F32

    def body(a_ref, b_ref, o_ref, *acc):
        prod = _dot(a_ref[...], b_ref[...], dims)
        if nk == 1:
            o_ref[...] = prod.astype(o_ref.dtype)
            return
        acc_ref = acc[0] if use_acc else o_ref
        k = pl.program_id(2)

        @pl.when(k == 0)
        def _():
            acc_ref[...] = prod

        if use_acc:
            @pl.when((k > 0) & (k < nk - 1))
            def _():
                acc_ref[...] += prod

            @pl.when(k == nk - 1)
            def _():
                o_ref[...] = (acc_ref[...] + prod).astype(o_ref.dtype)
        else:
            @pl.when(k > 0)
            def _():
                o_ref[...] += prod

    if mode == "tn":
        a_spec = pl.BlockSpec((tk, tm), lambda i, j, k: (k, i))
    else:
        a_spec = pl.BlockSpec((tm, tk), lambda i, j, k: (i, k))
    if mode == "nt":
        b_spec = pl.BlockSpec((tn, tk), lambda i, j, k: (j, k))
    else:
        b_spec = pl.BlockSpec((tk, tn), lambda i, j, k: (k, j))
    return pl.pallas_call(
        body, name=name, grid=(M // tm, N // tn, nk),
        in_specs=[a_spec, b_spec],
        out_specs=pl.BlockSpec((tm, tn), lambda i, j, k: (i, j)),
        out_shape=jax.ShapeDtypeStruct((M, N), out_dtype),
        scratch_shapes=[pltpu.VMEM((tm, tn), F32)] if use_acc else [],
        compiler_params=_params(("parallel", "parallel", "arbitrary")),
    )(a, b)


def _all_gather(x, name):
    def body(x_ref, out_ref, send_sems, recv_sems, local_sem):
        mx, my, mc = lax.axis_index("x"), lax.axis_index("y"), lax.axis_index("c")
        me, sibling = (mx, my, mc), (mx, my, 1 - mc)
        chips = [(1 - mx, my), (mx, 1 - my), (1 - mx, 1 - my)]

        def slot(px, py, pc):
            return out_ref.at[4 * px + 2 * py + pc]

        def copy(k, block, to, src=None):
            return pltpu.make_async_remote_copy(
                src_ref=slot(*block) if src is None else src, dst_ref=slot(*block),
                send_sem=send_sems.at[k], recv_sem=recv_sems.at[k],
                device_id=to, device_id_type=MESH)

        mine = pltpu.make_async_copy(x_ref, slot(*me), local_sem)
        mine.start()
        first = [copy(0, me, sibling, src=x_ref)]
        first += [copy(1 + j, me, (*chip, mc), src=x_ref) for j, chip in enumerate(chips)]
        for cp in first:
            cp.start()
        passed = [copy(4 + j, (*chip, mc), sibling) for j, chip in enumerate(chips)]
        for j, chip in enumerate(chips):
            copy(1 + j, (*chip, mc), me).wait_recv()
            passed[j].start()
        copy(0, sibling, me).wait_recv()
        for j, chip in enumerate(chips):
            copy(4 + j, (*chip, 1 - mc), me).wait_recv()
        for cp in first + passed:
            cp.wait_send()
        mine.wait()

    return pl.pallas_call(
        body, name=name,
        out_shape=jax.ShapeDtypeStruct((NDEV,) + x.shape, x.dtype),
        in_specs=[pl.BlockSpec(memory_space=pl.ANY)],
        out_specs=pl.BlockSpec(memory_space=pl.ANY),
        scratch_shapes=[pltpu.SemaphoreType.DMA((7,)), pltpu.SemaphoreType.DMA((7,)),
                        pltpu.SemaphoreType.DMA],
    )(x)


def _owner_exchange(g, name):
    def body(g_ref, out_ref, send_sems, recv_sems, local_sem):
        mx, my, mc = lax.axis_index("x"), lax.axis_index("y"), lax.axis_index("c")
        local = pltpu.make_async_copy(g_ref.at[4 * mx + 2 * my + mc], out_ref.at[0], local_sem)
        local.start()
        copies = []
        for f in range(1, NDEV):
            px = 1 - mx if (f >> 2) & 1 else mx
            py = 1 - my if (f >> 1) & 1 else my
            pc = 1 - mc if f & 1 else mc
            cp = pltpu.make_async_remote_copy(
                src_ref=g_ref.at[4 * px + 2 * py + pc], dst_ref=out_ref.at[f],
                send_sem=send_sems.at[f - 1], recv_sem=recv_sems.at[f - 1],
                device_id=(px, py, pc), device_id_type=MESH)
            cp.start()
            copies.append(cp)
        for cp in copies:
            cp.wait_recv()
        for cp in copies:
            cp.wait_send()
        local.wait()

    return pl.pallas_call(
        body, name=name,
        out_shape=jax.ShapeDtypeStruct(g.shape, g.dtype),
        in_specs=[pl.BlockSpec(memory_space=pl.ANY)],
        out_specs=pl.BlockSpec(memory_space=pl.ANY),
        scratch_shapes=[pltpu.SemaphoreType.DMA((7,)), pltpu.SemaphoreType.DMA((7,)),
                        pltpu.SemaphoreType.DMA],
    )(g)


def _any_specs(n):
    return [pl.BlockSpec(memory_space=pl.ANY)] * n


def _all_gather_multi(xs, name):
    na = len(xs)

    def body(*refs):
        x_refs, out_refs = refs[:na], refs[na:2 * na]
        send_sems, recv_sems, local_sems = refs[2 * na:]
        mx, my, mc = lax.axis_index("x"), lax.axis_index("y"), lax.axis_index("c")
        me, sibling = (mx, my, mc), (mx, my, 1 - mc)
        chips = [(1 - mx, my), (mx, 1 - my), (1 - mx, 1 - my)]

        def copy(a, k, block, to, src=None):
            slot = out_refs[a].at[4 * block[0] + 2 * block[1] + block[2]]
            return pltpu.make_async_remote_copy(
                src_ref=slot if src is None else src, dst_ref=slot,
                send_sem=send_sems.at[7 * a + k], recv_sem=recv_sems.at[7 * a + k],
                device_id=to, device_id_type=MESH)

        mine = [pltpu.make_async_copy(x_refs[a], out_refs[a].at[4 * mx + 2 * my + mc], local_sems.at[a])
                for a in range(na)]
        for cp in mine:
            cp.start()
        first = []
        for a in range(na):
            first.append(copy(a, 0, me, sibling, src=x_refs[a]))
            first += [copy(a, 1 + j, me, (*chip, mc), src=x_refs[a]) for j, chip in enumerate(chips)]
        for cp in first:
            cp.start()
        passed = []
        for a in range(na):
            for j, chip in enumerate(chips):
                copy(a, 1 + j, (*chip, mc), me).wait_recv()
                fwd = copy(a, 4 + j, (*chip, mc), sibling)
                fwd.start()
                passed.append(fwd)
        for a in range(na):
            copy(a, 0, sibling, me).wait_recv()
            for j, chip in enumerate(chips):
                copy(a, 4 + j, (*chip, 1 - mc), me).wait_recv()
        for cp in first + passed:
            cp.wait_send()
        for cp in mine:
            cp.wait()

    return pl.pallas_call(
        body, name=name,
        out_shape=[jax.ShapeDtypeStruct((NDEV,) + x.shape, x.dtype) for x in xs],
        in_specs=_any_specs(na), out_specs=_any_specs(na),
        scratch_shapes=[pltpu.SemaphoreType.DMA((7 * na,)), pltpu.SemaphoreType.DMA((7 * na,)),
                        pltpu.SemaphoreType.DMA((na,))],
    )(*xs)


def _owner_exchange_multi(gs, name):
    na = len(gs)

    def body(*refs):
        g_refs, out_refs = refs[:na], refs[na:2 * na]
        send_sems, recv_sems, local_sems = refs[2 * na:]
        mx, my, mc = lax.axis_index("x"), lax.axis_index("y"), lax.axis_index("c")
        locals_ = [pltpu.make_async_copy(g_refs[a].at[4 * mx + 2 * my + mc], out_refs[a].at[0], local_sems.at[a])
                   for a in range(na)]
        for cp in locals_:
            cp.start()
        copies = []
        for a in range(na):
            for f in range(1, NDEV):
                px = 1 - mx if (f >> 2) & 1 else mx
                py = 1 - my if (f >> 1) & 1 else my
                pc = 1 - mc if f & 1 else mc
                cp = pltpu.make_async_remote_copy(
                    src_ref=g_refs[a].at[4 * px + 2 * py + pc], dst_ref=out_refs[a].at[f],
                    send_sem=send_sems.at[7 * a + f - 1], recv_sem=recv_sems.at[7 * a + f - 1],
                    device_id=(px, py, pc), device_id_type=MESH)
                cp.start()
                copies.append(cp)
        for cp in copies:
            cp.wait_recv()
        for cp in copies:
            cp.wait_send()
        for cp in locals_:
            cp.wait()

    return pl.pallas_call(
        body, name=name,
        out_shape=[jax.ShapeDtypeStruct(g.shape, g.dtype) for g in gs],
        in_specs=_any_specs(na), out_specs=_any_specs(na),
        scratch_shapes=[pltpu.SemaphoreType.DMA((7 * na,)), pltpu.SemaphoreType.DMA((7 * na,)),
                        pltpu.SemaphoreType.DMA((na,))],
    )(*gs)


def _adamw_sum(r8, own, w, m, v, row0, tr, name):
    R, C = w.shape
    assert row0 % tr == 0
    blk0 = row0 // tr
    bc1 = 1.0 - ADAM_B1 ** ADAM_STEP
    bc2 = 1.0 - ADAM_B2 ** ADAM_STEP

    def body(r_ref, *refs):
        if own is None:
            gg = r_ref[0].astype(F32)
        else:
            gg = refs[0][...].astype(F32)
            refs = refs[1:]
        w_ref, m_ref, v_ref, g_ref, d_ref, mo_ref, vo_ref = refs
        for k in range(1, NDEV):
            gg = gg + r_ref[k].astype(F32)
        mn = ADAM_B1 * m_ref[...] + (1.0 - ADAM_B1) * gg
        vn = ADAM_B2 * v_ref[...] + (1.0 - ADAM_B2) * (gg * gg)
        mh = mn / bc1
        vh = vn / bc2
        g_ref[...] = gg
        d_ref[...] = -ADAM_LR * (mh / (jnp.sqrt(vh) + ADAM_EPS) + ADAM_WD * w_ref[...])
        mo_ref[...] = mn
        vo_ref[...] = vn

    spec = pl.BlockSpec((tr, C), lambda i: (i, 0))
    sh = jax.ShapeDtypeStruct((R, C), F32)
    own_ops = [] if own is None else [own]
    own_specs = [] if own is None else [pl.BlockSpec((tr, C), lambda i: (i + blk0, 0))]
    return pl.pallas_call(
        body, name=name, grid=(R // tr,),
        in_specs=[pl.BlockSpec((NDEV, tr, C), lambda i: (0, i + blk0, 0))] + own_specs + [spec, spec, spec],
        out_specs=[spec] * 4, out_shape=[sh] * 4, compiler_params=_params(("parallel",)),
    )(r8, *own_ops, w, m, v)


_HBM = pl.BlockSpec(memory_space=pltpu.HBM)
_SEM = pl.BlockSpec(memory_space=pltpu.SEMAPHORE)
_EFFECT = pltpu.SideEffectType.DATAFLOW_SIDE_EFFECTING


def _exchange_copies(g_refs, land_refs, send_sems, recv_sems, gather):
    mx, my, mc = lax.axis_index("x"), lax.axis_index("y"), lax.axis_index("c")
    copies = []
    for a in range(len(g_refs)):
        for f in range(1, NDEV):
            px = 1 - mx if (f >> 2) & 1 else mx
            py = 1 - my if (f >> 1) & 1 else my
            pc = 1 - mc if f & 1 else mc
            src = g_refs[a] if gather else g_refs[a].at[4 * px + 2 * py + pc]
            dst = land_refs[a].at[4 * mx + 2 * my + mc] if gather else land_refs[a].at[f]
            copies.append(pltpu.make_async_remote_copy(
                src_ref=src, dst_ref=dst,
                send_sem=send_sems.at[7 * a + f - 1], recv_sem=recv_sems.at[7 * a + f - 1],
                device_id=(px, py, pc), device_id_type=MESH))
    return copies


def _exchange_start(gs, name, gather=False):
    na = len(gs)

    def body(*refs):
        for cp in _exchange_copies(refs[:na], refs[na:2 * na], refs[2 * na], refs[2 * na + 1], gather):
            cp.start()
        refs[-1][...] = jnp.zeros_like(refs[-1])

    hbm = [pltpu.HBM(g.shape, g.dtype) for g in gs]
    land_shapes = [((NDEV,) + g.shape) if gather else g.shape for g in gs]
    lands = [pltpu.with_memory_space_constraint(lax.empty(shp, g.dtype), pltpu.HBM)
             for shp, g in zip(land_shapes, gs)]
    hbm_land = [pltpu.HBM(shp, g.dtype) for shp, g in zip(land_shapes, gs)]
    outs = pl.pallas_call(
        body, name=name,
        out_shape=(pltpu.SemaphoreType.DMA((7 * na,)), pltpu.SemaphoreType.DMA((7 * na,)), *hbm, *hbm_land,
                   jax.ShapeDtypeStruct((8, 128), F32)),
        in_specs=[_HBM] * (2 * na),
        out_specs=(_SEM, _SEM, *([_HBM] * (2 * na)), pl.BlockSpec(memory_space=pltpu.VMEM)),
        input_output_aliases={i: 2 + i for i in range(2 * na)},
        compiler_params=pltpu.CompilerParams(has_side_effects=_EFFECT),
    )(*[pltpu.with_memory_space_constraint(g, pltpu.HBM) for g in gs], *lands)
    return outs[0], outs[1], outs[2:2 + na], outs[2 + na:2 + 2 * na], outs[-1]


def _exchange_wait(send_sems, recv_sems, g_thru, land_thru, after, name, gather=False):
    na = len(g_thru)

    def body(*refs):
        for cp in _exchange_copies(refs[:na], refs[na:2 * na], refs[2 * na], refs[2 * na + 1], gather):
            cp.wait_send()
            cp.wait_recv()

    outs = pl.pallas_call(
        body, name=name,
        out_shape=tuple(pltpu.HBM(g.shape, g.dtype) for g in list(g_thru) + list(land_thru)),
        in_specs=[_HBM] * (2 * na) + [_SEM, _SEM, pl.BlockSpec(memory_space=pl.ANY)],
        out_specs=tuple([_HBM] * (2 * na)),
        input_output_aliases={i: i for i in range(2 * na)},
        compiler_params=pltpu.CompilerParams(has_side_effects=_EFFECT),
    )(*g_thru, *land_thru, send_sems, recv_sems, after)
    return outs[na:]


def _sum8(r, name):
    _, R, C = r.shape
    tr = _tile(R, (256, 160, 128, 64, 32, 16, 8))

    def body(r_ref, o_ref):
        acc = r_ref[0].astype(F32)
        for k in range(1, NDEV):
            acc = acc + r_ref[k].astype(F32)
        o_ref[...] = acc

    return pl.pallas_call(
        body, name=name, grid=(R // tr,),
        in_specs=[pl.BlockSpec((NDEV, tr, C), lambda i: (0, i, 0))],
        out_specs=pl.BlockSpec((tr, C), lambda i: (i, 0)),
        out_shape=jax.ShapeDtypeStruct((R, C), F32),
        compiler_params=_params(("parallel",)),
    )(r)


def _adamw(g, w, m, v, name):
    R, C = g.shape
    tr = _tile(R, (256, 160, 128, 64, 32, 16, 8))
    bc1 = 1.0 - ADAM_B1 ** ADAM_STEP
    bc2 = 1.0 - ADAM_B2 ** ADAM_STEP

    def body(g_ref, w_ref, m_ref, v_ref, d_ref, mo_ref, vo_ref):
        gg = g_ref[...]
        mn = ADAM_B1 * m_ref[...] + (1.0 - ADAM_B1) * gg
        vn = ADAM_B2 * v_ref[...] + (1.0 - ADAM_B2) * (gg * gg)
        mh = mn / bc1
        vh = vn / bc2
        d_ref[...] = -ADAM_LR * (mh / (jnp.sqrt(vh) + ADAM_EPS) + ADAM_WD * w_ref[...])
        mo_ref[...] = mn
        vo_ref[...] = vn

    spec = pl.BlockSpec((tr, C), lambda i: (i, 0))
    sh = jax.ShapeDtypeStruct((R, C), F32)
    return pl.pallas_call(
        body, name=name, grid=(R // tr,), in_specs=[spec] * 4, out_specs=[spec] * 3,
        out_shape=[sh] * 3, compiler_params=_params(("parallel",)),
    )(g, w, m, v)


def _ada_fwd(c16, w_sh, b_sh):
    def body(c_ref, w_ref, b_ref, o_ref):
        c = c_ref[...]
        o_ref[...] = _dot(c * _sig(c), w_ref[...], NN) + b_ref[...]

    return pl.pallas_call(
        body, name="ada_fwd", out_shape=jax.ShapeDtypeStruct((16, w_sh.shape[1]), F32),
        compiler_params=pltpu.CompilerParams(vmem_limit_bytes=VMEM_LIMIT),
    )(c16, w_sh, b_sh)


def _ada_bwd(c16, g16, g16_sh, w_sh):
    ncol = w_sh.shape[1]

    def body(c_ref, g_ref, gs_ref, w_ref, dw_ref, db_ref, dc_ref):
        c = c_ref[...]
        s = _sig(c)
        gs = gs_ref[...]
        dw_ref[...] = _dot(c * s, gs, TN)
        db_ref[...] = jnp.broadcast_to(_colsum(g_ref[...]), db_ref.shape)
        odd = lax.broadcasted_iota(jnp.int32, gs.shape, 0) % 2 == 1
        gc = _colsum(jnp.where(odd, gs, 0.0))
        ds = _dot(jnp.broadcast_to(gc, (8, ncol)), w_ref[...], NT)
        c1 = c[1:2, :]
        s1 = s[1:2, :]
        dc_ref[...] = ds * (s1 * (1.0 + c1 * (1.0 - s1)))

    return pl.pallas_call(
        body, name="ada_bwd",
        out_shape=[jax.ShapeDtypeStruct(w_sh.shape, F32),
                   jax.ShapeDtypeStruct((8, g16.shape[1]), F32),
                   jax.ShapeDtypeStruct((8, D), F32)],
        compiler_params=pltpu.CompilerParams(vmem_limit_bytes=VMEM_LIMIT),
    )(c16, g16, g16_sh, w_sh)


def _ln0_fwd(x, ctx, g, b, modx, modc):
    L = x.shape[0]
    nt = L // TL

    def body(x_ref, c_ref, g_ref, b_ref, mx_ref, mc_ref, xn_ref, h_ref):
        isc = pl.program_id(0) == nt
        xin = jnp.where(isc, c_ref[...], x_ref[...])
        sh = jnp.where(isc, mc_ref[0:1, :], mx_ref[0:1, :])
        sc = jnp.where(isc, mc_ref[1:2, :], mx_ref[1:2, :])
        xhat, _ = _ln(xin)
        xn = xhat * g_ref[...] + b_ref[...]
        xn_ref[...] = xn
        h_ref[...] = (xn * (1.0 + sc) + sh).astype(h_ref.dtype)

    return pl.pallas_call(
        body, name="ln0_fwd", grid=(nt + 1,),
        in_specs=[_rtc(D, nt), _cst((TL, D)), _cst((1, D)), _cst((1, D)), _cst((8, D)), _cst((8, D))],
        out_specs=[_rt(D), _rt(D)],
        out_shape=[jax.ShapeDtypeStruct((L + TL, D), F32), jax.ShapeDtypeStruct((L + TL, D), _MXU)],
        compiler_params=_params(("parallel",)),
    )(x, ctx, g, b, modx, modc)


def _xbc_colblk(j):
    return jnp.where(j < 8, OXS // 128 + j, OB // 128 + j - 8)


def _conv_taps(p_ref, r0, first, last):
    main = p_ref[pl.ds(r0, TL), :]
    zero = jnp.zeros((8, main.shape[1]), F32)
    prev = zero if first else p_ref[pl.ds(r0 - 8, 8), :]
    nxt = zero if last else p_ref[pl.ds(r0 + TL, 8), :]
    ext = jnp.concatenate([prev, main, nxt], axis=0)
    n = TL + 16
    return [pltpu.roll(ext, (2 - k) % n, 0)[8:8 + TL] for k in range(5)]


def _seq_chunks(L):
    nt = L // TL
    return [(r * TL, r == 0, r == nt - 1) for r in range(nt)] + [(L, True, True)]


def _conv_fwd(p, conv_w8, conv_b):
    RT = p.shape[0]
    L = RT - TL
    chunks = _seq_chunks(L)

    def body(p_ref, w_ref, b_ref, o_ref):
        w = w_ref[...]
        bias = b_ref[...]
        for r0, first, last in chunks:
            taps = _conv_taps(p_ref, r0, first, last)
            pre = bias + sum(w[k:k + 1, :] * taps[k] for k in range(5))
            o_ref[pl.ds(r0, TL), :] = pre * _sig(pre)

    return pl.pallas_call(
        body, name="conv_fwd", grid=(12,),
        in_specs=[pl.BlockSpec((RT, 128), lambda j: (0, _xbc_colblk(j))),
                  pl.BlockSpec((8, 128), lambda j: (0, j)),
                  pl.BlockSpec((1, 128), lambda j: (0, j))],
        out_specs=pl.BlockSpec((RT, 128), lambda j: (0, j)),
        out_shape=jax.ShapeDtypeStruct((RT, 1536), F32),
        compiler_params=_params(("parallel",)),
    )(p, conv_w8, conv_b)


def _ssd_common(dtraw, dtb, a32, rev):
    dt = _softplus(dtraw + dtb)
    acum = _cumsum_rows(dt * a32, rev)
    ii = lax.broadcasted_iota(jnp.int32, (Q, Q), 0)
    jj = lax.broadcasted_iota(jnp.int32, (Q, Q), 1)
    mask = (ii <= jj) if rev else (ii >= jj)
    return dt, acum, acum.T, dt.T, mask


def _ssd_orders(ncl, ncc):
    nc = ncl + ncc

    def cf(s):
        return jnp.where(s < ncc, ncl + s, s - ncc)

    def cb(s):
        return nc - 1 - s

    return cf, cb


def _ssd_fwd(xbc, p, prm):
    RT = xbc.shape[0]
    nc = RT // Q
    ncc = TL // Q
    cf, cb = _ssd_orders(nc - ncc, ncc)

    def one_dir(x_ref, dt_ref, prm_ref, y_ref, hp_ref, H_ref, d):
        rev = d == 1
        a32 = -jnp.exp(prm_ref[1:2, :])
        dt, acum, acumT, dtT, mask = _ssd_common(dt_ref[...], prm_ref[0:1, :], a32, rev)
        end = 0 if rev else Q - 1
        for g in range(2):
            Bg = x_ref[:, D + g * NS:D + (g + 1) * NS]
            Cg = x_ref[:, D + 2 * NS + g * NS:D + 2 * NS + (g + 1) * NS]
            CB = _dot(Cg, Bg, NT)
            for hh in range(HPG):
                h = g * HPG + hh
                ln = 16 * d + h
                col = acum[:, ln:ln + 1]
                rowv = acumT[ln:ln + 1, :]
                a_end = rowv[:, end:end + 1]
                Lm = jnp.exp(jnp.where(mask, col - rowv, -1e30))
                W = CB * Lm * dtT[ln:ln + 1, :]
                Xh = x_ref[:, h * HP:(h + 1) * HP]
                Hp = H_ref[h * HP:(h + 1) * HP, :]
                y = _dot(W, Xh, NN) + jnp.exp(col) * _dot(Cg, Hp, NT)
                y_ref[:, h * HP:(h + 1) * HP] = y
                dcol = jnp.exp(a_end - col) * dt[:, ln:ln + 1]
                hp_ref[0, h * HP:(h + 1) * HP, :] = Hp
                H_ref[h * HP:(h + 1) * HP, :] = jnp.exp(a_end) * Hp + _dot(Xh * dcol, Bg, TN)

    def body(xf_ref, xb_ref, df_ref, db_ref, prm_ref, yf_ref, yb_ref, hf_ref, hb_ref, Hf, Hb):
        @pl.when(pl.program_id(0) == 0)
        def _():
            Hf[...] = jnp.zeros_like(Hf)
            Hb[...] = jnp.zeros_like(Hb)

        one_dir(xf_ref, df_ref, prm_ref, yf_ref, hf_ref, Hf, 0)
        one_dir(xb_ref, db_ref, prm_ref, yb_ref, hb_ref, Hb, 1)

    ysh = jax.ShapeDtypeStruct((RT, D), F32)
    hsh = jax.ShapeDtypeStruct((nc, NH * HP, NS), F32)
    hspec = pl.BlockSpec((1, NH * HP, NS), lambda s: (s, 0, 0))
    return pl.pallas_call(
        body, name="ssd_fwd", grid=(nc,),
        in_specs=[pl.BlockSpec((Q, 1536), lambda s: (cf(s), 0)),
                  pl.BlockSpec((Q, 1536), lambda s: (cb(s), 0)),
                  pl.BlockSpec((Q, 128), lambda s: (cf(s), ODT // 128)),
                  pl.BlockSpec((Q, 128), lambda s: (cb(s), ODT // 128)),
                  _cst((8, 128))],
        out_specs=[pl.BlockSpec((Q, D), lambda s: (cf(s), 0)),
                   pl.BlockSpec((Q, D), lambda s: (cb(s), 0)), hspec, hspec],
        out_shape=[ysh, ysh, hsh, hsh],
        scratch_shapes=[pltpu.VMEM((NH * HP, NS), F32), pltpu.VMEM((NH * HP, NS), F32)],
        compiler_params=_params(("arbitrary",)),
    )(xbc, xbc, p, p, prm)


def _ssd_bwd(xbc, p, prm, dsk, dyd, hpf, hpb):
    RT = xbc.shape[0]
    nc = RT // Q
    ncc = TL // Q
    ncl = nc - ncc
    cf, cb = _ssd_orders(ncl, ncc)

    def rs(t):
        return nc - 1 - t

    def one_dir(x_ref, dt_ref, prm_ref, dsk_ref, dy_ref, is_ctx, hp_ref, dH_ref,
                dx_ref, ddt_ref, st_ref, d):
        rev = d == 1
        a32 = -jnp.exp(prm_ref[1:2, :])
        dtraw = dt_ref[...]
        dtb = prm_ref[0:1, :]
        dt, acum, acumT, dtT, mask = _ssd_common(dtraw, dtb, a32, rev)
        end = 0 if rev else Q - 1
        lane = lax.broadcasted_iota(jnp.int32, (Q, 128), 1)
        srow = lax.broadcasted_iota(jnp.int32, (Q, 128), 0)
        dyscale = jnp.where(is_ctx, 0.0, 1.0)
        c_dacum = jnp.zeros((Q, 128), F32)
        r_dacum = jnp.zeros((Q, 128), F32)
        c_ddt = jnp.zeros((Q, 128), F32)
        r_ddt = jnp.zeros((Q, 128), F32)
        dskacc = jnp.zeros((1, 128), F32)
        for g in range(2):
            Bg = x_ref[:, D + g * NS:D + (g + 1) * NS]
            Cg = x_ref[:, D + 2 * NS + g * NS:D + 2 * NS + (g + 1) * NS]
            CB = _dot(Cg, Bg, NT)
            dCB = jnp.zeros((Q, Q), F32)
            dBg = jnp.zeros((Q, NS), F32)
            dCg = jnp.zeros((Q, NS), F32)
            for hh in range(HPG):
                h = g * HPG + hh
                ln = 16 * d + h
                hs = slice(h * HP, (h + 1) * HP)
                col = acum[:, ln:ln + 1]
                rowv = acumT[ln:ln + 1, :]
                dtr = dtT[ln:ln + 1, :]
                dtc = dt[:, ln:ln + 1]
                a_end = rowv[:, end:end + 1]
                Lm = jnp.exp(jnp.where(mask, col - rowv, -1e30))
                E = jnp.exp(col)
                ecol = jnp.exp(a_end - col)
                dcol = ecol * dtc
                Xh = x_ref[:, hs]
                dY = dy_ref[:, hs] * dyscale
                Hp = hp_ref[0, hs, :]
                dHn = dH_ref[hs, :]
                W = CB * Lm * dtr
                dW = _dot(dY, Xh, NT)
                Mm = dW * CB * Lm
                T = Mm * dtr
                dCB = dCB + dW * Lm * dtr
                BdH = _dot(Bg, dHn, NT)
                dX = _dot(W, dY, TN) + dcol * BdH
                if d == 0:
                    dX = dX + dY * dsk_ref[:, hs]
                    dskacc = dskacc + jnp.where(lane[0:1, :] == h, _sum11(dY * Xh), 0.0)
                dx_ref[:, hs] = dX
                xb = jnp.sum(Xh * BdH, axis=1, keepdims=True)
                scol = dcol * xb
                G = _dot(dY, Hp, NN)
                dCg = dCg + E * G
                qcol = E * jnp.sum(G * Cg, axis=1, keepdims=True)
                dBg = dBg + _dot(Xh * dcol, dHn, NN)
                dH_ref[hs, :] = jnp.exp(a_end) * dHn + _dot(dY * E, Cg, TN)
                eterm = jnp.exp(a_end) * _sum11(dHn * Hp) + _sum11(scol)
                cvec = jnp.sum(T, axis=1, keepdims=True) + qcol - scol
                cvec = cvec + jnp.where(srow[:, 0:1] == end, eterm, 0.0)
                c_dacum = c_dacum + jnp.where(lane == ln, cvec, 0.0)
                r_dacum = r_dacum - jnp.where(srow == ln, _colsum(T), 0.0)
                c_ddt = c_ddt + jnp.where(lane == ln, ecol * xb, 0.0)
                r_ddt = r_ddt + jnp.where(srow == ln, _colsum(Mm), 0.0)
            dBg = dBg + _dot(dCB, Cg, TN)
            dCg = dCg + _dot(dCB, Bg, NN)
            dx_ref[:, D + g * NS:D + (g + 1) * NS] = dBg
            dx_ref[:, D + 2 * NS + g * NS:D + 2 * NS + (g + 1) * NS] = dCg
        dacum = c_dacum + r_dacum.T
        da = _cumsum_rows(dacum, not rev)
        mine = (lane >= 16 * d) & (lane < 16 * d + 16)
        ddt = jnp.where(mine, c_ddt + r_ddt.T + da * a32, 0.0)
        ddt_ref[...] = ddt * _sig(dtraw + dtb)
        st_ref[0:1, :] += _colsum(jnp.where(mine, da * dt, 0.0))
        if d == 0:
            st_ref[1:2, :] += dskacc

    def body(xf_ref, xb_ref, df_ref, db_ref, prm_ref, dsk_ref, dyf_ref, dyb_ref, hf_ref, hb_ref,
             dxf_ref, dxb_ref, ddf_ref, ddb_ref, st_ref, dHf, dHb):
        t = pl.program_id(0)

        @pl.when(t == 0)
        def _():
            dHf[...] = jnp.zeros_like(dHf)
            dHb[...] = jnp.zeros_like(dHb)
            st_ref[...] = jnp.zeros_like(st_ref)

        s = rs(t)
        one_dir(xf_ref, df_ref, prm_ref, dsk_ref, dyf_ref, cf(s) >= ncl, hf_ref, dHf,
                dxf_ref, ddf_ref, st_ref, 0)
        one_dir(xb_ref, db_ref, prm_ref, dsk_ref, dyb_ref, cb(s) >= ncl, hb_ref, dHb,
                dxb_ref, ddb_ref, st_ref, 1)

        @pl.when(t == nc - 1)
        def _():
            st_ref[0:1, :] = -jnp.exp(prm_ref[1:2, :]) * st_ref[0:1, :]

    def lat(c):
        return jnp.minimum(c, ncl - 1)

    xsh = jax.ShapeDtypeStruct((RT, 1536), F32)
    dsh = jax.ShapeDtypeStruct((RT, 128), F32)
    hspec = pl.BlockSpec((1, NH * HP, NS), lambda t: (rs(t), 0, 0))
    return pl.pallas_call(
        body, name="ssd_bwd", grid=(nc,),
        in_specs=[pl.BlockSpec((Q, 1536), lambda t: (cf(rs(t)), 0)),
                  pl.BlockSpec((Q, 1536), lambda t: (cb(rs(t)), 0)),
                  pl.BlockSpec((Q, 128), lambda t: (cf(rs(t)), ODT // 128)),
                  pl.BlockSpec((Q, 128), lambda t: (cb(rs(t)), ODT // 128)),
                  _cst((8, 128)), _cst((1, D)),
                  pl.BlockSpec((Q, D), lambda t: (lat(cf(rs(t))), 0)),
                  pl.BlockSpec((Q, D), lambda t: (lat(cb(rs(t))), 0)),
                  hspec, hspec],
        out_specs=[pl.BlockSpec((Q, 1536), lambda t: (cf(rs(t)), 0)),
                   pl.BlockSpec((Q, 1536), lambda t: (cb(rs(t)), 0)),
                   pl.BlockSpec((Q, 128), lambda t: (cf(rs(t)), 0)),
                   pl.BlockSpec((Q, 128), lambda t: (cb(rs(t)), 0)),
                   _cst((8, 128))],
        out_shape=[xsh, xsh, dsh, dsh, jax.ShapeDtypeStruct((8, 128), F32)],
        scratch_shapes=[pltpu.VMEM((NH * HP, NS), F32), pltpu.VMEM((NH * HP, NS), F32)],
        compiler_params=_params(("arbitrary",)),
    )(xbc, xbc, p, p, prm, dsk, dyd, dyd, hpf, hpb)


def _lane_bcast(v, ln):
    return jnp.broadcast_to(v[:, ln:ln + 1], v.shape)


def _halves(v, lo, axis):
    return jnp.concatenate([jnp.where(lo, v, 0.0), jnp.where(lo, 0.0, v)], axis=axis)


def _ssd2_fwd(xbc, p, prm):
    RT = xbc.shape[0]
    nc = RT // Q
    ncc = TL // Q
    cf, cb = _ssd_orders(nc - ncc, ncc)

    def one_dir(x_ref, dt_ref, prm_ref, y_ref, hp_ref, HT_ref, d):
        rev = d == 1
        a32 = -jnp.exp(prm_ref[1:2, :])
        dt, acum, acumT, dtT, mask = _ssd_common(dt_ref[...], prm_ref[0:1, :], a32, rev)
        end = 0 if rev else Q - 1
        lo = lax.broadcasted_iota(jnp.int32, (Q, 128), 1) < HP
        for g in range(2):
            Bg = x_ref[:, D + g * NS:D + (g + 1) * NS]
            Cg = x_ref[:, D + 2 * NS + g * NS:D + 2 * NS + (g + 1) * NS]
            CB = _dot(Cg, Bg, NT)
            xds, svs = [], []
            for q in range(HPG // 2):
                pi = g * (HPG // 2) + q
                ps = slice(pi * 128, (pi + 1) * 128)
                Xp = x_ref[:, ps]
                HTp = HT_ref[:, ps]
                lhs, dcs, sv = [], [], []
                ces = []
                for h in (2 * pi, 2 * pi + 1):
                    ln = 16 * d + h
                    colB = _lane_bcast(acum, ln)
                    rowv = acumT[ln:ln + 1, :]
                    aend = colB[end:end + 1, :]
                    Lm = jnp.exp(jnp.where(mask, colB - rowv, -1e30))
                    lhs.append(CB * Lm * dtT[ln:ln + 1, :])
                    ces.append(Cg * jnp.exp(colB))
                    dcs.append(jnp.exp(aend - colB) * _lane_bcast(dt, ln))
                    sv.append(jnp.exp(aend))
                lhs = jnp.concatenate(lhs + ces, axis=1)
                rhs = jnp.concatenate([_halves(Xp, lo, 0), _halves(HTp, lo, 0)], axis=0)
                y_ref[:, ps] = _dot(lhs, rhs, NN)
                xds.append(Xp * jnp.where(lo, dcs[0], dcs[1]))
                svs.append(jnp.where(lo[0:1, :], sv[0], sv[1]))
            gs = slice(g * 512, (g + 1) * 512)
            HTg = HT_ref[:, gs]
            hp_ref[0, :, gs] = HTg
            st = _dot(Bg.T, jnp.concatenate(xds, axis=1), NN)
            HT_ref[:, gs] = jnp.concatenate(svs, axis=1) * HTg + st

    def body(xf_ref, xb_ref, df_ref, db_ref, prm_ref, yf_ref, yb_ref, hf_ref, hb_ref, Hf, Hb):
        @pl.when(pl.program_id(0) == 0)
        def _():
            Hf[...] = jnp.zeros_like(Hf)
            Hb[...] = jnp.zeros_like(Hb)

        one_dir(xf_ref, df_ref, prm_ref, yf_ref, hf_ref, Hf, 0)
        one_dir(xb_ref, db_ref, prm_ref, yb_ref, hb_ref, Hb, 1)

    ysh = jax.ShapeDtypeStruct((RT, D), F32)
    hsh = jax.ShapeDtypeStruct((nc, NS, NH * HP), F32)
    hspec = pl.BlockSpec((1, NS, NH * HP), lambda s: (s, 0, 0))
    return pl.pallas_call(
        body, name="ssd_fwd", grid=(nc,),
        in_specs=[pl.BlockSpec((Q, 1536), lambda s: (cf(s), 0)),
                  pl.BlockSpec((Q, 1536), lambda s: (cb(s), 0)),
                  pl.BlockSpec((Q, 128), lambda s: (cf(s), ODT // 128)),
                  pl.BlockSpec((Q, 128), lambda s: (cb(s), ODT // 128)),
                  _cst((8, 128))],
        out_specs=[pl.BlockSpec((Q, D), lambda s: (cf(s), 0)),
                   pl.BlockSpec((Q, D), lambda s: (cb(s), 0)), hspec, hspec],
        out_shape=[ysh, ysh, hsh, hsh],
        scratch_shapes=[pltpu.VMEM((NS, NH * HP), F32), pltpu.VMEM((NS, NH * HP), F32)],
        compiler_params=_params(("arbitrary",)),
    )(xbc, xbc, p, p, prm)


def _ssd2_bwd(xbc, p, prm, dsk, dyd, hpf, hpb):
    RT = xbc.shape[0]
    nc = RT // Q
    ncc = TL // Q
    ncl = nc - ncc
    cf, cb = _ssd_orders(ncl, ncc)

    def rs(t):
        return nc - 1 - t

    def one_dir(x_ref, dt_ref, prm_ref, dsk_ref, dy_ref, is_ctx, hp_ref, dHT_ref,
                dx_ref, ddt_ref, st_ref, d):
        rev = d == 1
        a32 = -jnp.exp(prm_ref[1:2, :])
        dtraw = dt_ref[...]
        dtb = prm_ref[0:1, :]
        dt, acum, acumT, _, _ = _ssd_common(dtraw, dtb, a32, rev)
        end = 0 if rev else Q - 1
        lane = lax.broadcasted_iota(jnp.int32, (Q, 128), 1)
        srow = lax.broadcasted_iota(jnp.int32, (Q, 128), 0)
        maskT = (lane <= srow) if rev else (lane >= srow)
        lo = lane < HP
        lo1 = lo[0:1, :]
        dyscale = jnp.where(is_ctx, 0.0, 1.0)
        c_dacum = jnp.zeros((Q, 128), F32)
        r_dacum = jnp.zeros((Q, 128), F32)
        c_ddt = jnp.zeros((Q, 128), F32)
        dskacc = jnp.zeros((1, 128), F32)
        for g in range(2):
            gs = slice(g * 512, (g + 1) * 512)
            Bg = x_ref[:, D + g * NS:D + (g + 1) * NS]
            Cg = x_ref[:, D + 2 * NS + g * NS:D + 2 * NS + (g + 1) * NS]
            CBT = _dot(Bg, Cg, NT)
            HTg = hp_ref[0, :, gs]
            dHTg = dHT_ref[:, gs]
            BdHg = _dot(Bg, dHTg, NN)
            dCBT = jnp.zeros((Q, Q), F32)
            dCg = jnp.zeros((Q, NS), F32)
            xds, dyes, svs = [], [], []
            for q in range(HPG // 2):
                pi = g * (HPG // 2) + q
                ps = slice(pi * 128, (pi + 1) * 128)
                qs = slice(q * 128, (q + 1) * 128)
                Xp = x_ref[:, ps]
                dYp = dy_ref[:, ps] * dyscale
                HTp = HTg[:, qs]
                BdHp = BdHg[:, qs]
                dY2 = _halves(dYp, lo, 0)
                dWT2 = _dot(_halves(Xp, lo, 0), dYp.T, NN)
                G2 = _dot(dY2, HTp, NT)
                XB = Xp * BdHp
                hh = _colsum(dHTg[:, qs] * HTp)
                yx = _colsum(dYp * Xp)
                wts, dcs, ebs, sv = [], [], [], []
                for k, h in enumerate((2 * pi, 2 * pi + 1)):
                    ln = 16 * d + h
                    half = lo if k == 0 else jnp.logical_not(lo)
                    half1 = half[0:1, :]
                    colB = _lane_bcast(acum, ln)
                    dtcB = _lane_bcast(dt, ln)
                    rowv = acumT[ln:ln + 1, :]
                    aend = colB[end:end + 1, :]
                    LmT = jnp.exp(jnp.where(maskT, rowv - colB, -1e30))
                    WT = CBT * LmT * dtcB
                    dWT = dWT2[k * Q:(k + 1) * Q, :]
                    MT = dWT * CBT * LmT
                    rM = jnp.sum(MT, axis=1, keepdims=True)
                    rT = _colsum(MT * dtcB)
                    dCBT = dCBT + dWT * LmT * dtcB
                    ecol = jnp.exp(aend - colB)
                    EB = jnp.exp(colB)
                    Gk = G2[k * Q:(k + 1) * Q, :]
                    dCg = dCg + EB * Gk
                    qcol = jnp.sum(EB * Gk * Cg, axis=1, keepdims=True)
                    xb = jnp.sum(jnp.where(half, XB, 0.0), axis=1, keepdims=True)
                    e1 = ecol[:, 0:1]
                    dt1 = dtcB[:, 0:1]
                    scol = e1 * dt1 * xb
                    sA = jnp.exp(aend)
                    eterm = sA[:, 0:1] * jnp.sum(jnp.where(half1, hh, 0.0), axis=1, keepdims=True) \
                        + _colsum(scol)
                    cvec = qcol - dt1 * rM - scol + jnp.where(srow[:, 0:1] == end, eterm, 0.0)
                    c_dacum = c_dacum + jnp.where(lane == ln, cvec, 0.0)
                    r_dacum = r_dacum + jnp.where(srow == ln, rT, 0.0)
                    c_ddt = c_ddt + jnp.where(lane == ln, rM + e1 * xb, 0.0)
                    if d == 0:
                        dskacc = dskacc + jnp.where(
                            lane[0:1, :] == h, jnp.sum(jnp.where(half1, yx, 0.0), axis=1, keepdims=True), 0.0)
                    wts.append(WT)
                    dcs.append(ecol * dtcB)
                    ebs.append(EB)
                    sv.append(sA)
                dcp = jnp.where(lo, dcs[0], dcs[1])
                dX = _dot(jnp.concatenate(wts, axis=1), dY2, NN) + dcp * BdHp
                if d == 0:
                    dX = dX + dYp * dsk_ref[:, ps]
                dx_ref[:, ps] = dX
                xds.append(Xp * dcp)
                dyes.append(dYp * jnp.where(lo, ebs[0], ebs[1]))
                svs.append(jnp.where(lo1, sv[0], sv[1]))
            dx_ref[:, D + g * NS:D + (g + 1) * NS] = (
                _dot(jnp.concatenate(xds, axis=1), dHTg, NT) + _dot(dCBT, Cg, NN))
            dx_ref[:, D + 2 * NS + g * NS:D + 2 * NS + (g + 1) * NS] = dCg + _dot(dCBT, Bg, TN)
            dHT_ref[:, gs] = (jnp.concatenate(svs, axis=1) * dHTg
                              + _dot(Cg.T, jnp.concatenate(dyes, axis=1), NN))
        dacum = c_dacum + r_dacum.T
        da = _cumsum_rows(dacum, not rev)
        mine = (lane >= 16 * d) & (lane < 16 * d + 16)
        ddt = jnp.where(mine, c_ddt + da * a32, 0.0)
        ddt_ref[...] = ddt * _sig(dtraw + dtb)
        st_ref[0:1, :] += _colsum(jnp.where(mine, da * dt, 0.0))
        if d == 0:
            st_ref[1:2, :] += dskacc

    def body(xf_ref, xb_ref, df_ref, db_ref, prm_ref, dsk_ref, dyf_ref, dyb_ref, hf_ref, hb_ref,
             dxf_ref, dxb_ref, ddf_ref, ddb_ref, st_ref, dHf, dHb):
        t = pl.program_id(0)

        @pl.when(t == 0)
        def _():
            dHf[...] = jnp.zeros_like(dHf)
            dHb[...] = jnp.zeros_like(dHb)
            st_ref[...] = jnp.zeros_like(st_ref)

        s = rs(t)
        one_dir(xf_ref, df_ref, prm_ref, dsk_ref, dyf_ref, cf(s) >= ncl, hf_ref, dHf,
                dxf_ref, ddf_ref, st_ref, 0)
        one_dir(xb_ref, db_ref, prm_ref, dsk_ref, dyb_ref, cb(s) >= ncl, hb_ref, dHb,
                dxb_ref, ddb_ref, st_ref, 1)

        @pl.when(t == nc - 1)
        def _():
            st_ref[0:1, :] = -jnp.exp(prm_ref[1:2, :]) * st_ref[0:1, :]

    def lat(c):
        return jnp.minimum(c, ncl - 1)

    xsh = jax.ShapeDtypeStruct((RT, 1536), F32)
    dsh = jax.ShapeDtypeStruct((RT, 128), F32)
    hspec = pl.BlockSpec((1, NS, NH * HP), lambda t: (rs(t), 0, 0))
    return pl.pallas_call(
        body, name="ssd_bwd", grid=(nc,),
        in_specs=[pl.BlockSpec((Q, 1536), lambda t: (cf(rs(t)), 0)),
                  pl.BlockSpec((Q, 1536), lambda t: (cb(rs(t)), 0)),
                  pl.BlockSpec((Q, 128), lambda t: (cf(rs(t)), ODT // 128)),
                  pl.BlockSpec((Q, 128), lambda t: (cb(rs(t)), ODT // 128)),
                  _cst((8, 128)), _cst((1, D)),
                  pl.BlockSpec((Q, D), lambda t: (lat(cf(rs(t))), 0)),
                  pl.BlockSpec((Q, D), lambda t: (lat(cb(rs(t))), 0)),
                  hspec, hspec],
        out_specs=[pl.BlockSpec((Q, 1536), lambda t: (cf(rs(t)), 0)),
                   pl.BlockSpec((Q, 1536), lambda t: (cb(rs(t)), 0)),
                   pl.BlockSpec((Q, 128), lambda t: (cf(rs(t)), 0)),
                   pl.BlockSpec((Q, 128), lambda t: (cb(rs(t)), 0)),
                   _cst((8, 128))],
        out_shape=[xsh, xsh, dsh, dsh, jax.ShapeDtypeStruct((8, 128), F32)],
        scratch_shapes=[pltpu.VMEM((NS, NH * HP), F32), pltpu.VMEM((NS, NH * HP), F32)],
        compiler_params=_params(("arbitrary",)),
    )(xbc, xbc, p, p, prm, dsk, dyd, dyd, hpf, hpb)


def _mix_fwd_vals(yf, yb, z, xs, u, v, dsk, sg, gg, gb):
    y = yf + yb + xs * dsk
    sz = _sig(z)
    hh = y * z * sz
    r = lax.rsqrt(jnp.mean(hh * hh, axis=-1, keepdims=True) + EPS)
    nh = hh * r
    ug, tu = _gelu(u)
    vg, tv = _gelu(v)
    vhat, vrstd = _ln(vg)
    vn = vhat * gg + gb
    return y, sz, r, nh, ug, tu, vg, tv, vhat, vrstd, vn


def _mix_fwd(yf, yb, p, xbc, dsk, sg, gg, gb, ws, bsT):
    L = yf.shape[0] - TL
    nt = L // TL

    def body(yf_ref, yb_ref, z_ref, xs_ref, u_ref, v_ref, dsk_ref, sg_ref, gg_ref, gb_ref,
             ws_ref, bs_ref, ys_ref, ym_ref):
        _, _, _, nh, ug, _, _, _, _, _, vn = _mix_fwd_vals(
            yf_ref[...], yb_ref[...], z_ref[...], xs_ref[...], u_ref[...], v_ref[...],
            dsk_ref[...], sg_ref[...], gg_ref[...], gb_ref[...])
        ys_ref[...] = (nh * sg_ref[...]).astype(ys_ref.dtype)
        for n in range(TL // Q):
            rs_ = slice(n * Q, (n + 1) * Q)
            for g in range(8):
                cs = slice(g * 128, (g + 1) * 128)
                mixed = _dot(ws_ref[g], vn[rs_, cs], NN) + bs_ref[:, g:g + 1]
                ym_ref[rs_, cs] = (ug[rs_, cs] * mixed).astype(ym_ref.dtype)

    return pl.pallas_call(
        body, name="mix_fwd", grid=(nt,),
        in_specs=[_rt(D), _rt(D), _rt(D, OZ // D), _rt(D, 0), _rt(D, OU // D), _rt(D, OV // D),
                  _cst((1, D)), _cst((1, D)), _cst((1, D)), _cst((1, D)),
                  _cst((8, 128, 128)), _cst((128, 128))],
        out_specs=[_rt(D), _rt(D)],
        out_shape=[jax.ShapeDtypeStruct((L, D), _MXU), jax.ShapeDtypeStruct((L, D), _MXU)],
        compiler_params=_params(("parallel",)),
    )(yf, yb, p, xbc, p, p, dsk, sg, gg, gb, ws, bsT)


def _mix_bwd(dys, dym, yf, yb, p, xbc, dp, dsk, sg, gg, gb, ws, bsT):
    L = dys.shape[0]
    nt = L // TL

    def body(dys_ref, dym_ref, yf_ref, yb_ref, z_ref, xs_ref, u_ref, v_ref, dsk_ref, sg_ref,
             gg_ref, gb_ref, ws_ref, bs_ref, dp_any, dzuv_ref, dy_ref, st_ref,
             dws_ref, dbs_ref, dvn_s):
        del dp_any
        dz_ref = dzuv_ref.at[:, OZ:OZ + D]
        du_ref = dzuv_ref.at[:, OU:OU + D]
        dv_ref = dzuv_ref.at[:, OV:OV + D]

        @pl.when(pl.program_id(0) == 0)
        def _():
            st_ref[...] = jnp.zeros_like(st_ref)
            dws_ref[...] = jnp.zeros_like(dws_ref)
            dbs_ref[...] = jnp.zeros_like(dbs_ref)

        z = z_ref[...]
        u = u_ref[...]
        v = v_ref[...]
        y, sz, r, nh, ug, tu, vg, tv, vhat, vrstd, vn = _mix_fwd_vals(
            yf_ref[...], yb_ref[...], z, xs_ref[...], u, v,
            dsk_ref[...], sg_ref[...], gg_ref[...], gb_ref[...])
        dys = dys_ref[...]
        st_ref[0:1, :] += _colsum(dys * nh)
        dn = dys * sg_ref[...]
        dhh = r * (dn - nh * jnp.mean(dn * nh, axis=-1, keepdims=True))
        dy_ref[...] = dhh * z * sz
        dz_ref[...] = (dhh * y * (sz * (1.0 + z * (1.0 - sz)))).astype(dz_ref.dtype)
        dym = dym_ref[...]
        lane = lax.broadcasted_iota(jnp.int32, (Q, 128), 1)
        dbs = jnp.zeros((Q, 128), F32)
        gu = _gelu_grad(u, tu)
        for n in range(TL // Q):
            rs_ = slice(n * Q, (n + 1) * Q)
            for g in range(8):
                cs = slice(g * 128, (g + 1) * 128)
                vb = vn[rs_, cs]
                mixed = _dot(ws_ref[g], vb, NN) + bs_ref[:, g:g + 1]
                dyb = dym[rs_, cs]
                dmx = dyb * ug[rs_, cs]
                du_ref[rs_, cs] = (dyb * mixed * gu[rs_, cs]).astype(du_ref.dtype)
                dvn_s[rs_, cs] = _dot(ws_ref[g], dmx, TN)
                dws_ref[g] += _dot(dmx, vb, NT)
                dbs = dbs + jnp.where(lane == g, jnp.sum(dmx, axis=1, keepdims=True), 0.0)
        dbs_ref[...] += dbs
        dvn = dvn_s[...]
        st_ref[1:2, :] += _colsum(dvn * vhat)
        st_ref[2:3, :] += _colsum(dvn)
        dvg = _ln_bwd(dvn * gg_ref[...], vhat, vrstd)
        dv_ref[...] = (dvg * _gelu_grad(v, tv)).astype(dv_ref.dtype)

    outs = pl.pallas_call(
        body, name="mix_bwd", grid=(nt,),
        in_specs=[_rt(D), _rt(D), _rt(D), _rt(D), _rt(D, OZ // D), _rt(D, 0), _rt(D, OU // D),
                  _rt(D, OV // D), _cst((1, D)), _cst((1, D)), _cst((1, D)), _cst((1, D)),
                  _cst((8, 128, 128)), _cst((128, 128)), pl.BlockSpec(memory_space=pl.ANY)],
        out_specs=[_rt(3 * D, 0), _rt(D), _cst((8, D)),
                   _cst((8, 128, 128)), _cst((128, 128))],
        out_shape=[jax.ShapeDtypeStruct(dp.shape, dp.dtype),
                   jax.ShapeDtypeStruct((L, D), F32), jax.ShapeDtypeStruct((8, D), F32),
                   jax.ShapeDtypeStruct((8, 128, 128), F32), jax.ShapeDtypeStruct((128, 128), F32)],
        scratch_shapes=[pltpu.VMEM((TL, D), F32)],
        input_output_aliases={14: 0},
        compiler_params=_params(("arbitrary",)),
    )(dys, dym, yf, yb, p, xbc, p, p, dsk, sg, gg, gb, ws, bsT, dp)
    return outs


def _gate_fwd(a1, a2, p, bg):
    L = a1.shape[0]

    def body(a1_ref, a2_ref, g_ref, bg_ref, m_ref):
        gt = _sig(g_ref[...] + bg_ref[...])
        m_ref[...] = (gt[:, :D] * a1_ref[...] + gt[:, D:] * a2_ref[...]).astype(m_ref.dtype)

    return pl.pallas_call(
        body, name="gate_fwd", grid=(L // TL,),
        in_specs=[_rt(D), _rt(D), _rt(2 * D, OG // (2 * D)), _cst((1, 2 * D))],
        out_specs=_rt(D), out_shape=jax.ShapeDtypeStruct((L, D), _MXU),
        compiler_params=_params(("parallel",)),
    )(a1, a2, p, bg)


def _gate_bwd(dmg, a1, a2, p, bg, dp):
    L = a1.shape[0]

    def body(dm_ref, a1_ref, a2_ref, g_ref, bg_ref, dp_any, dg_ref, da1_ref, da2_ref, st_ref):
        del dp_any

        @pl.when(pl.program_id(0) == 0)
        def _():
            st_ref[...] = jnp.zeros_like(st_ref)

        gt = _sig(g_ref[...] + bg_ref[...])
        g1 = gt[:, :D]
        g2 = gt[:, D:]
        dm = dm_ref[...]
        da1_ref[...] = (dm * g1).astype(da1_ref.dtype)
        da2_ref[...] = (dm * g2).astype(da2_ref.dtype)
        dg1 = dm * a1_ref[...] * g1 * (1.0 - g1)
        dg2 = dm * a2_ref[...] * g2 * (1.0 - g2)
        st_ref[0:1, 0:D] += _colsum(dg1)
        st_ref[0:1, D:2 * D] += _colsum(dg2)
        dg_ref[:, 0:D] = dg1.astype(dg_ref.dtype)
        dg_ref[:, D:2 * D] = dg2.astype(dg_ref.dtype)

    return pl.pallas_call(
        body, name="gate_bwd", grid=(L // TL,),
        in_specs=[_rt(D), _rt(D), _rt(D), _rt(2 * D, OG // (2 * D)), _cst((1, 2 * D)),
                  pl.BlockSpec(memory_space=pl.ANY)],
        out_specs=[_rt(2 * D, OG // (2 * D)), _rt(D), _rt(D), _cst((8, 2 * D))],
        out_shape=[jax.ShapeDtypeStruct(dp.shape, dp.dtype), jax.ShapeDtypeStruct((L, D), _MXU),
                   jax.ShapeDtypeStruct((L, D), _MXU), jax.ShapeDtypeStruct((8, 2 * D), F32)],
        input_output_aliases={5: 0},
        compiler_params=_params(("arbitrary",)),
    )(dmg, a1, a2, p, bg, dp)


def _res1_fwd(xn, out, modx, g, b):
    L = out.shape[0]

    def body(xn_ref, o_ref, mx_ref, g_ref, b_ref, r1_ref, h2_ref):
        r1 = ALPHA * xn_ref[...] + mx_ref[2:3, :] * o_ref[...]
        xhat, _ = _ln(r1)
        x1 = xhat * g_ref[...] + b_ref[...]
        r1_ref[...] = r1
        h2_ref[...] = (x1 * (1.0 + mx_ref[4:5, :]) + mx_ref[3:4, :]).astype(h2_ref.dtype)

    return pl.pallas_call(
        body, name="res1_fwd", grid=(L // TL,),
        in_specs=[_rt(D), _rt(D), _cst((8, D)), _cst((1, D)), _cst((1, D))],
        out_specs=[_rt(D), _rt(D)],
        out_shape=[jax.ShapeDtypeStruct((L, D), F32), jax.ShapeDtypeStruct((L, D), _MXU)],
        compiler_params=_params(("parallel",)),
    )(xn, out, modx, g, b)


def _glu_fwd(f13):
    L = f13.shape[0]

    def body(f1_ref, f3_ref, o_ref):
        f1 = f1_ref[...]
        o_ref[...] = (f1 * _sig(f1) * f3_ref[...]).astype(o_ref.dtype)

    return pl.pallas_call(
        body, name="glu_fwd", grid=(L // TL,),
        in_specs=[_rt(DFF, 0), _rt(DFF, 1)], out_specs=_rt(DFF),
        out_shape=jax.ShapeDtypeStruct((L, DFF), _MXU),
        compiler_params=_params(("parallel",)),
    )(f13, f13)


def _glu_bwd(dff, f13):
    L = f13.shape[0]

    def body(d_ref, f1_ref, f3_ref, o_ref):
        f1 = f1_ref[...]
        s = _sig(f1)
        d = d_ref[...]
        o_ref[:, 0:DFF] = (d * f3_ref[...] * (s * (1.0 + f1 * (1.0 - s)))).astype(o_ref.dtype)
        o_ref[:, DFF:2 * DFF] = (d * f1 * s).astype(o_ref.dtype)

    return pl.pallas_call(
        body, name="glu_bwd", grid=(L // TL,),
        in_specs=[_rt(DFF), _rt(DFF, 0), _rt(DFF, 1)], out_specs=_rt(2 * DFF),
        out_shape=jax.ShapeDtypeStruct((L, 2 * DFF), _MXU),
        compiler_params=_params(("parallel",)),
    )(dff, f13, f13)


def _res2(r1, o2, tgt, modx, g1, b1, g2, b2):
    L = r1.shape[0]

    def body(r1_ref, o2_ref, t_ref, mx_ref, g1_ref, b1_ref, g2_ref, b2_ref,
             dr2_ref, do2_ref, st_ref, loss_ref):
        @pl.when(pl.program_id(0) == 0)
        def _():
            st_ref[...] = jnp.zeros_like(st_ref)
            loss_ref[...] = jnp.zeros_like(loss_ref)

        xh1, _ = _ln(r1_ref[...])
        x1 = xh1 * g1_ref[...] + b1_ref[...]
        o2 = o2_ref[...]
        g2x = mx_ref[5:6, :]
        xh2, rstd2 = _ln(ALPHA * x1 + g2x * o2)
        err = xh2 * g2_ref[...] + b2_ref[...] - t_ref[...]
        per_tok = jnp.mean(err * err, axis=-1, keepdims=True)
        loss_ref[...] += 0.5 * jnp.sum(per_tok, axis=0, keepdims=True)
        dy = err * (1.0 / D)
        st_ref[0:1, :] += _colsum(dy * xh2)
        st_ref[1:2, :] += _colsum(dy)
        dr2 = _ln_bwd(dy * g2_ref[...], xh2, rstd2)
        st_ref[2:3, :] += _colsum(dr2 * o2)
        dr2_ref[...] = dr2
        do2_ref[...] = (g2x * dr2).astype(do2_ref.dtype)

    return pl.pallas_call(
        body, name="res2", grid=(L // TL,),
        in_specs=[_rt(D), _rt(D), _rt(D), _cst((8, D))] + [_cst((1, D))] * 4,
        out_specs=[_rt(D), _rt(D), _cst((8, D)), _cst((8, 128))],
        out_shape=[jax.ShapeDtypeStruct((L, D), F32), jax.ShapeDtypeStruct((L, D), _MXU),
                   jax.ShapeDtypeStruct((8, D), F32), jax.ShapeDtypeStruct((8, 128), F32)],
        compiler_params=_params(("arbitrary",)),
    )(r1, o2, tgt, modx, g1, b1, g2, b2)


def _res1_bwd(dr2, dh2, r1, out, modx, g1, b1):
    L = r1.shape[0]

    def body(dr2_ref, dh2_ref, r1_ref, o_ref, mx_ref, g_ref, b_ref, dr1_ref, do_ref, st_ref):
        @pl.when(pl.program_id(0) == 0)
        def _():
            st_ref[...] = jnp.zeros_like(st_ref)

        xh1, rstd1 = _ln(r1_ref[...])
        x1 = xh1 * g_ref[...] + b_ref[...]
        dh2 = dh2_ref[...]
        dx1 = ALPHA * dr2_ref[...] + dh2 * (1.0 + mx_ref[4:5, :])
        st_ref[0:1, :] += _colsum(dh2 * x1)
        st_ref[1:2, :] += _colsum(dh2)
        st_ref[2:3, :] += _colsum(dx1 * xh1)
        st_ref[3:4, :] += _colsum(dx1)
        dr1 = _ln_bwd(dx1 * g_ref[...], xh1, rstd1)
        st_ref[4:5, :] += _colsum(dr1 * o_ref[...])
        dr1_ref[...] = dr1
        do_ref[...] = (mx_ref[2:3, :] * dr1).astype(do_ref.dtype)

    return pl.pallas_call(
        body, name="res1_bwd", grid=(L // TL,),
        in_specs=[_rt(D), _rt(D), _rt(D), _rt(D), _cst((8, D)), _cst((1, D)), _cst((1, D))],
        out_specs=[_rt(D), _rt(D), _cst((8, D))],
        out_shape=[jax.ShapeDtypeStruct((L, D), F32), jax.ShapeDtypeStruct((L, D), _MXU),
                   jax.ShapeDtypeStruct((8, D), F32)],
        compiler_params=_params(("arbitrary",)),
    )(dr2, dh2, r1, out, modx, g1, b1)


def _conv_bwd(dxf, dxb, p, conv_w8, conv_b, dp):
    RT = p.shape[0]
    chunks = _seq_chunks(RT - TL)

    def body(df_ref, db_ref, p_ref, w_ref, b_ref, dp_any, o_ref, dw_ref, dbias_ref, dpre_s):
        del dp_any
        w = w_ref[...]
        bias = b_ref[...]
        srow = lax.broadcasted_iota(jnp.int32, (8, 128), 0)
        dwacc = jnp.zeros((8, 128), F32)
        dbacc = jnp.zeros((1, 128), F32)
        for r0, first, last in chunks:
            taps = _conv_taps(p_ref, r0, first, last)
            pre = bias + sum(w[k:k + 1, :] * taps[k] for k in range(5))
            s = _sig(pre)
            dpre = (df_ref[pl.ds(r0, TL), :] + db_ref[pl.ds(r0, TL), :]) * (s * (1.0 + pre * (1.0 - s)))
            dpre_s[pl.ds(r0, TL), :] = dpre
            dbacc = dbacc + _colsum(dpre)
            for k in range(5):
                dwacc = dwacc + jnp.where(srow == k, _colsum(dpre * taps[k]), 0.0)
        for r0, first, last in chunks:
            taps = _conv_taps(dpre_s, r0, first, last)
            dx = sum(w[k:k + 1, :] * taps[4 - k] for k in range(5))
            o_ref[pl.ds(r0, TL), :] = dx.astype(o_ref.dtype)
        dw_ref[...] = dwacc
        dbias_ref[...] = jnp.broadcast_to(dbacc, (8, 128))

    cspec = pl.BlockSpec((RT, 128), lambda j: (0, j))
    wspec = pl.BlockSpec((8, 128), lambda j: (0, j))
    return pl.pallas_call(
        body, name="conv_bwd", grid=(12,),
        in_specs=[cspec, cspec, pl.BlockSpec((RT, 128), lambda j: (0, _xbc_colblk(j))),
                  wspec, pl.BlockSpec((1, 128), lambda j: (0, j)), pl.BlockSpec(memory_space=pl.ANY)],
        out_specs=[pl.BlockSpec((RT, 128), lambda j: (0, _xbc_colblk(j))), wspec, wspec],
        out_shape=[jax.ShapeDtypeStruct(dp.shape, dp.dtype), jax.ShapeDtypeStruct((8, 1536), F32),
                   jax.ShapeDtypeStruct((8, 1536), F32)],
        scratch_shapes=[pltpu.VMEM((RT, 128), F32)],
        input_output_aliases={5: 0},
        compiler_params=_params(("parallel",)),
    )(dxf, dxb, p, conv_w8, conv_b, dp)


def _dt_bwd(ddf, ddb, dp):
    RT = ddf.shape[0]

    def body(f_ref, b_ref, dp_any, o_ref, st_ref):
        del dp_any

        @pl.when(pl.program_id(0) == 0)
        def _():
            st_ref[...] = jnp.zeros_like(st_ref)

        s = f_ref[...] + b_ref[...]
        o_ref[...] = s.astype(o_ref.dtype)
        st_ref[0:1, :] += _colsum(s)

    return pl.pallas_call(
        body, name="dt_bwd", grid=(RT // TL,),
        in_specs=[_rt(128), _rt(128), pl.BlockSpec(memory_space=pl.ANY)],
        out_specs=[_rt(128, ODT // 128), _cst((8, 128))],
        out_shape=[jax.ShapeDtypeStruct(dp.shape, dp.dtype), jax.ShapeDtypeStruct((8, 128), F32)],
        input_output_aliases={2: 0},
        compiler_params=_params(("arbitrary",)),
    )(ddf, ddb, dp)


def _ln0_bwd(dh1, dr1, x, ctx, g, b, modx, modc):
    L = x.shape[0]
    nt = L // TL

    def body(dh_ref, dr1_ref, x_ref, c_ref, g_ref, b_ref, mx_ref, mc_ref, gx_ref, st_ref):
        i = pl.program_id(0)
        isc = i == nt

        @pl.when(i == 0)
        def _():
            st_ref[...] = jnp.zeros_like(st_ref)

        xin = jnp.where(isc, c_ref[...], x_ref[...])
        xhat, rstd = _ln(xin)
        xn = xhat * g_ref[...] + b_ref[...]
        sc = jnp.where(isc, mc_ref[1:2, :], mx_ref[1:2, :])
        dh = dh_ref[...]
        lat = jnp.where(isc, 0.0, 1.0)
        dxn = dh * (1.0 + sc) + (lat * ALPHA) * dr1_ref[...]
        tsh = _colsum(dh)
        tsc = _colsum(dh * xn)
        st_ref[0:1, :] += lat * tsh
        st_ref[1:2, :] += lat * tsc
        st_ref[2:3, :] += (1.0 - lat) * tsh
        st_ref[3:4, :] += (1.0 - lat) * tsc
        st_ref[4:5, :] += _colsum(dxn * xhat)
        st_ref[5:6, :] += _colsum(dxn)

        @pl.when(i < nt)
        def _():
            gx_ref[...] = _ln_bwd(dxn * g_ref[...], xhat, rstd)

    return pl.pallas_call(
        body, name="ln0_bwd", grid=(nt + 1,),
        in_specs=[_rt(D), _rtc(D, nt), _rtc(D, nt), _cst((TL, D)), _cst((1, D)), _cst((1, D)),
                  _cst((8, D)), _cst((8, D))],
        out_specs=[_rtc(D, nt), _cst((8, D))],
        out_shape=[jax.ShapeDtypeStruct((L, D), F32), jax.ShapeDtypeStruct((8, D), F32)],
        compiler_params=_params(("arbitrary",)),
    )(dh1, dr1, x, ctx, g, b, modx, modc)


def _perm_cols(w):
    pad = jnp.zeros((w.shape[0], NPJ - NNAT), w.dtype)
    return jnp.concatenate([w[:, 0:1024], w[:, 2592:3616], w[:, 3616:4640], w[:, 1024:2048],
                            w[:, 4640:6688], w[:, 2048:2304], w[:, 2304:2560], w[:, 2560:2592], pad],
                           axis=1)


SECTIONS = ((0, 1024, OZ), (1024, 2048, OXS), (2048, 2304, OB), (2304, 2560, OC), (2560, 2592, ODT),
            (2592, 3616, OU), (3616, 4640, OV), (4640, 6688, OG))


def _perm_from_blocks(ga):
    n = ga.shape[2]
    pieces = []
    for na, nb, _ in sorted(SECTIONS, key=lambda sec: sec[2]):
        for k in range(NDEV):
            lo, hi = max(na, k * n), min(nb, (k + 1) * n)
            if lo < hi:
                pieces.append(ga[k][:, lo - k * n:hi - k * n])
    pieces.append(jnp.zeros((ga.shape[1], NPJ - NNAT), ga.dtype))
    return jnp.concatenate(pieces, axis=1)


def _blocks_from_perm(gp, n):
    blocks = []
    for k in range(NDEV):
        pieces = []
        for na, nb, po in SECTIONS:
            lo, hi = max(na, k * n), min(nb, (k + 1) * n)
            if lo < hi:
                pieces.append(gp[:, po + lo - na:po + hi - na])
        blocks.append(jnp.concatenate(pieces, axis=1))
    return jnp.stack(blocks)


def _padded(n, row_align):
    unit = row_align * D
    return -(-n // unit) * unit if row_align else n


def _slab(arrs, rows, row_align=0):
    parts = []
    for a in arrs:
        f = a.reshape(-1)
        parts.append(jnp.pad(f, (0, _padded(f.shape[0], row_align) - f.shape[0])))
    flat = jnp.concatenate(parts)
    flat = jnp.pad(flat, (0, rows * D - flat.shape[0]))
    return flat.reshape(rows, D)


def _unslab(slab, shapes, row_align=0):
    flat = slab.reshape(-1)
    out, off = [], 0
    for shp in shapes:
        n = 1
        for s in shp:
            n *= s
        out.append(flat[off:off + n].reshape(shp))
        off += _padded(n, row_align)
    return out


def _row(v):
    return v.reshape(1, -1)


def _pad_rows(a, rows):
    return jnp.pad(a, ((0, rows - a.shape[0]), (0, 0)))


BIG = ["w_in", "w_ssd_proj", "w_gm_proj", "w_out", "w_ff1", "w_ff3", "w_ff2"]
BIG_ROWS = 2304
BIG_ALIGN = 16
REPL = ["c_ctx", "ln0_g", "ln0_b", "b_ada", "conv_b", "dt_bias", "a_log", "d_skip", "ssd_norm_g",
        "gm_norm_g", "gm_norm_b", "w_spatial", "b_spatial", "b_gate", "ln1_g", "ln1_b", "ln2_g", "ln2_b"]
SMALL_ROWS = 160
WEIGHTS = ["c_ctx", "ln0_g", "ln0_b", "w_ada", "b_ada", "w_in", "conv_w", "conv_b", "dt_bias", "a_log",
           "d_skip", "ssd_norm_g", "gm_norm_g", "gm_norm_b", "w_spatial", "b_spatial", "b_gate",
           "w_ssd_proj", "w_gm_proj", "w_out", "ln1_g", "ln1_b", "w_ff1", "w_ff3", "w_ff2", "ln2_g", "ln2_b"]


def kernel(x, c, ctx, c_ctx, ln0_g, ln0_b, w_ada, b_ada, w_in, conv_w, conv_b, dt_bias, a_log, d_skip, ssd_norm_g, gm_norm_g, gm_norm_b, w_spatial, b_spatial, b_gate, w_ssd_proj, w_gm_proj, w_out, ln1_g, ln1_b, w_ff1, w_ff3, w_ff2, ln2_g, ln2_b, loss_target, m_c_ctx, m_ln0_g, m_ln0_b, m_w_ada, m_b_ada, m_w_in, m_conv_w, m_conv_b, m_dt_bias, m_a_log, m_d_skip, m_ssd_norm_g, m_gm_norm_g, m_gm_norm_b, m_w_spatial, m_b_spatial, m_b_gate, m_w_ssd_proj, m_w_gm_proj, m_w_out, m_ln1_g, m_ln1_b, m_w_ff1, m_w_ff3, m_w_ff2, m_ln2_g, m_ln2_b, v_c_ctx, v_ln0_g, v_ln0_b, v_w_ada, v_b_ada, v_w_in, v_conv_w, v_conv_b, v_dt_bias, v_a_log, v_d_skip, v_ssd_norm_g, v_gm_norm_g, v_gm_norm_b, v_w_spatial, v_b_spatial, v_b_gate, v_w_ssd_proj, v_w_gm_proj, v_w_out, v_ln1_g, v_ln1_b, v_w_ff1, v_w_ff3, v_w_ff2, v_ln2_g, v_ln2_b):
    W = dict(c_ctx=c_ctx, ln0_g=ln0_g, ln0_b=ln0_b, w_ada=w_ada, b_ada=b_ada, w_in=w_in, conv_w=conv_w,
             conv_b=conv_b, dt_bias=dt_bias, a_log=a_log, d_skip=d_skip, ssd_norm_g=ssd_norm_g,
             gm_norm_g=gm_norm_g, gm_norm_b=gm_norm_b, w_spatial=w_spatial, b_spatial=b_spatial,
             b_gate=b_gate, w_ssd_proj=w_ssd_proj, w_gm_proj=w_gm_proj, w_out=w_out, ln1_g=ln1_g,
             ln1_b=ln1_b, w_ff1=w_ff1, w_ff3=w_ff3, w_ff2=w_ff2, ln2_g=ln2_g, ln2_b=ln2_b)
    M = dict(c_ctx=m_c_ctx, ln0_g=m_ln0_g, ln0_b=m_ln0_b, w_ada=m_w_ada, b_ada=m_b_ada, w_in=m_w_in,
             conv_w=m_conv_w, conv_b=m_conv_b, dt_bias=m_dt_bias, a_log=m_a_log, d_skip=m_d_skip,
             ssd_norm_g=m_ssd_norm_g, gm_norm_g=m_gm_norm_g, gm_norm_b=m_gm_norm_b,
             w_spatial=m_w_spatial, b_spatial=m_b_spatial, b_gate=m_b_gate, w_ssd_proj=m_w_ssd_proj,
             w_gm_proj=m_w_gm_proj, w_out=m_w_out, ln1_g=m_ln1_g, ln1_b=m_ln1_b, w_ff1=m_w_ff1,
             w_ff3=m_w_ff3, w_ff2=m_w_ff2, ln2_g=m_ln2_g, ln2_b=m_ln2_b)
    V = dict(c_ctx=v_c_ctx, ln0_g=v_ln0_g, ln0_b=v_ln0_b, w_ada=v_w_ada, b_ada=v_b_ada, w_in=v_w_in,
             conv_w=v_conv_w, conv_b=v_conv_b, dt_bias=v_dt_bias, a_log=v_a_log, d_skip=v_d_skip,
             ssd_norm_g=v_ssd_norm_g, gm_norm_g=v_gm_norm_g, gm_norm_b=v_gm_norm_b,
             w_spatial=v_w_spatial, b_spatial=v_b_spatial, b_gate=v_b_gate, w_ssd_proj=v_w_ssd_proj,
             w_gm_proj=v_w_gm_proj, w_out=v_w_out, ln1_g=v_ln1_g, ln1_b=v_ln1_b, w_ff1=v_w_ff1,
             w_ff3=v_w_ff3, w_ff2=v_w_ff2, ln2_g=v_ln2_g, ln2_b=v_ln2_b)

    me = 4 * lax.axis_index("x") + 2 * lax.axis_index("y") + lax.axis_index("c")
    xl, cx, tgt = x[0], ctx[0], loss_target[0]
    L = xl.shape[0]
    assert cx.shape[0] == TL and L % TL == 0
    ada_n = w_ada.shape[2]
    cw_n = conv_w.shape[2]

    small1 = _pad_rows(jnp.concatenate([c, _slab([conv_w[0]], 1)], axis=0), 8)
    g1 = _all_gather(small1, "ag_small")
    c_all = g1[:, 0, :]
    conv_w_full = g1[:, 1, :5 * cw_n].reshape(NDEV, 5, cw_n).transpose(1, 0, 2).reshape(5, NDEV * cw_n)
    sq = w_ssd_proj.shape[1]
    ffr = w_ff2.shape[1]
    ffc = w_ff1.shape[2]
    late = [jnp.concatenate([w_ssd_proj[0], w_gm_proj[0], w_out[0], w_ff2[0]], axis=0).astype(_MXU),
            w_ff1[0].astype(_MXU), w_ff3[0].astype(_MXU)]
    lw_send, lw_recv, lw_src, lw_land, lw_token = _exchange_start(late, "ag_late_start", gather=True)
    ga, = _all_gather_multi([(w_in[0] + lw_token[0, 0]).astype(_MXU)], "ag_w_in")
    w_in_p = _perm_from_blocks(ga)

    c16 = _pad_rows(jnp.concatenate([c_all, _row(c_ctx)], axis=0), 16)
    b_ada_sh = lax.dynamic_slice(b_ada, (0, ada_n * me), (1, ada_n))
    modp = _ada_fwd(c16, w_ada[0], b_ada_sh)
    mod16 = _all_gather(modp, "ag_mod").transpose(1, 0, 2).reshape(16, NDEV * ada_n)
    modx = _pad_rows(lax.dynamic_slice(mod16, (me, 0), (1, 6 * D)).reshape(6, D), 8)
    modc = _pad_rows(mod16[8].reshape(6, D), 8)

    g0, b0 = _row(ln0_g), _row(ln0_b)
    xn, h1 = _ln0_fwd(xl, cx, g0, b0, modx, modc)
    p = _mm(h1, w_in_p, "nn", F32, "mm_p")
    conv_w8 = _pad_rows(conv_w_full, 8)
    xbc = _conv_fwd(p, conv_w8, conv_b)
    prm = _pad_rows(jnp.pad(jnp.stack([dt_bias.reshape(32), a_log.reshape(32)]), ((0, 0), (0, 96))), 8)
    yf, yb, hpf, hpb = _ssd2_fwd(xbc, p, prm)
    dsk = _row(jnp.repeat(d_skip[0, 0] + d_skip[0, 1], HP))
    ws_m = w_spatial[0].astype(_MXU)
    bsT = jnp.pad(b_spatial[0].T, ((0, 0), (0, 120)))
    mixp = (dsk, ssd_norm_g, gm_norm_g, gm_norm_b, ws_m, bsT)
    yssd, ygm = _mix_fwd(yf, yb, p, xbc, *mixp)
    gb, gc1, gc2 = _exchange_wait(lw_send, lw_recv, lw_src, lw_land, yssd, "ag_late_wait", gather=True)

    def with_own(g, mine, k):
        return jnp.where(me == k, mine, g[k])

    gb = jnp.stack([with_own(gb, late[0], k) for k in range(NDEV)])
    w_ssd_f = gb[:, 0:sq].reshape(NDEV * sq, D)
    w_gm_f = gb[:, sq:2 * sq].reshape(NDEV * sq, D)
    w_out_f = gb[:, 2 * sq:3 * sq].reshape(NDEV * sq, D)
    w_ff2_f = gb[:, 3 * sq:3 * sq + ffr].reshape(NDEV * ffr, D)
    w13 = jnp.concatenate([with_own(gc1, late[1], k) for k in range(NDEV)]
                          + [with_own(gc2, late[2], k) for k in range(NDEV)], axis=1)
    a1 = _mm(yssd, w_ssd_f, "nn", F32, "mm_a1")
    a2 = _mm(ygm, w_gm_f, "nn", F32, "mm_a2")
    merged = _gate_fwd(a1, a2, p, b_gate)
    out = _mm(merged, w_out_f, "nn", F32, "mm_out")
    r1, h2 = _res1_fwd(xn, out, modx, ln1_g, ln1_b)
    f13 = _mm(h2, w13, "nn", F32, "mm_f13")
    ff = _glu_fwd(f13)
    o2 = _mm(ff, w_ff2_f, "nn", F32, "mm_o2")

    dr2, do2, st2, loss_slab = _res2(r1, o2, tgt, modx, ln1_g, ln1_b, ln2_g, ln2_b)
    loss = lax.psum(loss_slab[0, 0], ("x", "y", "c"))
    dff = _mm(do2, w_ff2_f, "nt", F32, "mm_dff")
    df13 = _glu_bwd(dff, f13)
    dh2 = _mm(df13, w13, "nt", F32, "mm_dh2")
    dw_ff2 = _mm(ff, do2, "tn", _MXU, "mm_dw_ff2")
    dw13 = _mm(h2, df13, "tn", _MXU, "mm_dw13")
    xff = [dw_ff2.reshape(NDEV, ffr, D),
           jnp.stack([dw13[:, k * ffc:(k + 1) * ffc] for k in range(NDEV)]),
           jnp.stack([dw13[:, DFF + k * ffc:DFF + (k + 1) * ffc] for k in range(NDEV)])]
    ff_send, ff_recv, ff_src, ff_land, ff_token = _exchange_start(xff, "xchg_ff_start")
    modx = modx + ff_token[0, 0]
    dr1, dout, st1 = _res1_bwd(dr2, dh2, r1, out, modx, ln1_g, ln1_b)
    dmg = _mm(dout, w_out_f, "nt", F32, "mm_dmerged")
    dw_out = _mm(merged, dout, "tn", _MXU, "mm_dw_out")
    dp = jnp.zeros((L + TL, NPJ), _MXU)
    dp, da1, da2, stg = _gate_bwd(dmg, a1, a2, p, b_gate, dp)
    dys = _mm(da1, w_ssd_f, "nt", F32, "mm_dyssd")
    dym = _mm(da2, w_gm_f, "nt", F32, "mm_dygm")
    dw_ssd = _mm(yssd, da1, "tn", _MXU, "mm_dw_ssd")
    dw_gm = _mm(ygm, da2, "tn", _MXU, "mm_dw_gm")
    xsq = [jnp.concatenate([dw_ssd.reshape(NDEV, sq, D), dw_gm.reshape(NDEV, sq, D),
                            dw_out.reshape(NDEV, sq, D)], axis=1)]
    sq_send, sq_recv, sq_src, sq_land, sq_token = _exchange_start(xsq, "xchg_sq_start")
    mixp = (dsk + sq_token[0:1, 0:1],) + mixp[1:]
    dp, dyd, stm, dws, dbsT = _mix_bwd(dys, dym, yf, yb, p, xbc, dp, *mixp)
    dxf, dxb, ddf, ddb, sts = _ssd2_bwd(xbc, p, prm, dsk, dyd, hpf, hpb)
    dp, dcw, dcb = _conv_bwd(dxf, dxb, p, conv_w8, conv_b, dp)
    dp, std = _dt_bwd(ddf, ddb, dp)
    dh1 = _mm(dp, w_in_p, "nt", F32, "mm_dh1")
    dw_in_p = _mm(h1, dp, "tn", _MXU, "mm_dw_in")
    xin = [_blocks_from_perm(dw_in_p, w_in.shape[2])]
    in_send, in_recv, in_src, in_land, in_token = _exchange_start(xin, "xchg_in_start")
    modx = modx + in_token[0, 0]
    grad_x, st0 = _ln0_bwd(dh1, dr1, xl, cx, g0, b0, modx, modc)

    zero = jnp.zeros((D,), F32)
    dmod = jnp.stack([jnp.concatenate([st0[0], st0[1], st1[4], st1[1], st1[0], st2[2]]),
                      jnp.concatenate([st0[2], st0[3], zero, zero, zero, zero])])
    g16 = _all_gather(_pad_rows(dmod, 8), "ag_dmod")[:, 0:2, :].reshape(16, 6 * D)
    g16_sh = lax.dynamic_slice(g16, (0, ada_n * me), (16, ada_n))
    c16b = jnp.stack([c_all, jnp.broadcast_to(_row(c_ctx), (NDEV, D))], axis=1).reshape(16, D)
    dw_ada, db_ada8, dcc8 = _ada_bwd(c16b, g16, g16_sh, w_ada[0])

    part = dict(
        c_ctx=dcc8[0], ln0_g=st0[4], ln0_b=st0[5], conv_w=dcw[0:5], conv_b=dcb[0],
        dt_bias=std[0, 0:32], a_log=sts[0, 0:32], d_skip=jnp.tile(sts[1, 0:16], 2),
        ssd_norm_g=stm[0], gm_norm_g=stm[1], gm_norm_b=stm[2], w_spatial=dws,
        b_spatial=dbsT[:, 0:8].T, b_gate=stg[0], ln1_g=st1[2], ln1_b=st1[3], ln2_g=st2[0], ln2_b=st2[1])
    pnames = list(part)
    psum8 = _sum8(_all_gather(_slab([part[n] for n in pnames], SMALL_ROWS), "ag_smallgrads"), "sum_smallgrads")
    small = dict(zip(pnames, _unslab(psum8, [part[n].shape for n in pnames])))
    grads = {n: small[n].reshape(W[n].shape) for n in pnames if n != "conv_w"}
    grads["conv_w"] = lax.dynamic_slice(small["conv_w"], (0, cw_n * me), (5, cw_n)).reshape(conv_w.shape)
    grads["b_ada"] = db_ada8[0:1]
    grads["w_ada"] = dw_ada.reshape(w_ada.shape)

    delta, new_m, new_v = {}, {}, {}

    def adam_group(names, rows, tag, align=0):
        shapes = [W[n].shape for n in names]
        outs = _adamw(*[_slab([src[n] for n in names], rows, align) for src in (grads, W, M, V)], tag)
        for res, slab in zip((delta, new_m, new_v), outs):
            for n, a in zip(names, _unslab(slab, shapes, align)):
                res[n] = a

    adam_group(["w_ada", "conv_w"], ada_n + 256, "adamw_shard")
    adam_group(REPL, SMALL_ROWS, "adamw_repl")

    rff = _exchange_wait(ff_send, ff_recv, ff_src, ff_land, st0, "xchg_ff_wait")
    rsq = _exchange_wait(sq_send, sq_recv, sq_src, sq_land, rff[0], "xchg_sq_wait")
    rin = _exchange_wait(in_send, in_recv, in_src, in_land, delta["ln2_b"], "xchg_in_wait")

    def own(blocks):
        return lax.dynamic_index_in_dim(blocks, me, 0, keepdims=False)

    for n, r8, mine, row0, tr in (
            ("w_ff2", rff[0], own(xff[0]), 0, ffr // 2), ("w_ff1", rff[1], own(xff[1]), 0, 256),
            ("w_ff3", rff[2], own(xff[2]), 0, 256), ("w_ssd_proj", rsq[0], own(xsq[0]), 0, sq),
            ("w_gm_proj", rsq[0], own(xsq[0]), sq, sq), ("w_out", rsq[0], own(xsq[0]), 2 * sq, sq),
            ("w_in", rin[0], own(xin[0]), 0, 256)):
        res = _adamw_sum(r8, mine, W[n][0], M[n][0], V[n][0], row0, tr, "adamw_" + n)
        grads[n], delta[n], new_m[n], new_v[n] = [a[None] for a in res]

    return (loss, grad_x[None], *[grads[n] for n in WEIGHTS], *[delta[n] for n in WEIGHTS],
            *[new_m[n] for n in WEIGHTS], *[new_v[n] for n in WEIGHTS])
```

```python
import functools

import jax
import jax.numpy as jnp
from jax import lax
from jax.experimental import pallas as pl
from jax.experimental.pallas import tpu as pltpu

_MXU = jnp.bfloat16
F32 = jnp.float32
D = 1024
TL = 256
Q = 128
NH, HP, NS, HPG = 16, 64, 128, 8
DFF = 2816
ALPHA = 2.0 ** 0.25
EPS = 1e-5
OZ, OU, OV, OXS, OG, OB, OC, ODT, NPJ = 0, 1024, 2048, 3072, 4096, 6144, 6400, 6656, 6912
NNAT = 6688
NDEV = 8
ADAM_LR, ADAM_B1, ADAM_B2, ADAM_EPS, ADAM_WD, ADAM_STEP = 1e-3, 0.9, 0.999, 1e-8, 0.01, 10
VMEM_LIMIT = 48 * 1024 * 1024

NN = ((1,), (0,))
NT = ((1,), (1,))
TN = ((0,), (0,))
MESH = pl.DeviceIdType.MESH


def _dot(a, b, dims):
    return lax.dot_general(a.astype(_MXU), b.astype(_MXU), (dims, ((), ())),
                           preferred_element_type=F32)


def _tile(n, cands):
    for c in cands:
        if n % c == 0:
            return c
    return n


def _divisor_tile(n, cap, mult):
    best = n
    for t in range(mult, min(n, cap) + 1, mult):
        if n % t == 0:
            best = t
    return best


def _params(sem):
    return pltpu.CompilerParams(dimension_semantics=sem, vmem_limit_bytes=VMEM_LIMIT)


def _cst(shape):
    nd = len(shape)
    return pl.BlockSpec(shape, lambda *_: (0,) * nd)


def _rt(w, cb=0, rows=TL):
    return pl.BlockSpec((rows, w), lambda i: (i, cb))


def _rtc(w, nt, cb=0):
    return pl.BlockSpec((TL, w), lambda i: (jnp.minimum(i, nt - 1), cb))


def _sig(x):
    return jax.nn.sigmoid(x)


def _softplus(x):
    return jnp.maximum(x, 0.0) + jnp.log1p(jnp.exp(-jnp.abs(x)))


_G0, _G1 = 0.7978845608028654, 0.044715


def _gelu(x):
    t = jnp.tanh(_G0 * (x + _G1 * x * x * x))
    return 0.5 * x * (1.0 + t), t


def _gelu_grad(x, t):
    return 0.5 * (1.0 + t) + 0.5 * x * (1.0 - t * t) * _G0 * (1.0 + 3.0 * _G1 * x * x)


def _ln(r):
    mu = jnp.mean(r, axis=-1, keepdims=True)
    xc = r - mu
    var = jnp.mean(xc * xc, axis=-1, keepdims=True)
    rstd = lax.rsqrt(var + EPS)
    return xc * rstd, rstd


def _ln_bwd(dyh, xhat, rstd):
    return rstd * (dyh - jnp.mean(dyh, axis=-1, keepdims=True)
                   - xhat * jnp.mean(dyh * xhat, axis=-1, keepdims=True))


def _colsum(v):
    return jnp.sum(v, axis=0, keepdims=True)


def _sum11(v):
    return jnp.sum(jnp.sum(v, axis=1, keepdims=True), axis=0, keepdims=True)


def _cumsum_rows(a, rev):
    n = a.shape[0]
    row = lax.broadcasted_iota(jnp.int32, a.shape, 0)
    s = 1
    while s < n:
        if rev:
            a = a + jnp.where(row < n - s, pltpu.roll(a, n - s, 0), 0.0)
        else:
            a = a + jnp.where(row >= s, pltpu.roll(a, s, 0), 0.0)
        s *= 2
    return a


def _mm(a, b, mode, out_dtype, name):
    if mode == "tn":
        K, M = a.shape
    else:
        M, K = a.shape
    N = b.shape[0] if mode == "nt" else b.shape[1]
    tm = _divisor_tile(M, 1408, 128) if mode == "tn" else _divisor_tile(M, 1088, 16)
    tn = _divisor_tile(N, 1408, 128)
    tk = _divisor_tile(K, 2304, 128)
    nk = K // tk
    dims = {"nn": NN, "nt": NT, "tn": TN}[mode]
    use_acc = nk > 1 and out_dtype != F32

    def body(a_ref, b_ref, o_ref, *acc):
        prod = _dot(a_ref[...], b_ref[...], dims)
        if nk == 1:
            o_ref[...] = prod.astype(o_ref.dtype)
            return
        acc_ref = acc[0] if use_acc else o_ref
        k = pl.program_id(2)

        @pl.when(k == 0)
        def _():
            acc_ref[...] = prod

        if use_acc:
            @pl.when((k > 0) & (k < nk - 1))
            def _():
                acc_ref[...] += prod

            @pl.when(k == nk - 1)
            def _():
                o_ref[...] = (acc_ref[...] + prod).astype(o_ref.dtype)
        else:
            @pl.when(k > 0)
            def _():
                o_ref[...] += prod

    if mode == "tn":
        a_spec = pl.BlockSpec((tk, tm), lambda i, j, k: (k, i))
    else:
        a_spec = pl.BlockSpec((tm, tk), lambda i, j, k: (i, k))
    if mode == "nt":
        b_spec = pl.BlockSpec((tn, tk), lambda i, j, k: (j, k))
    else:
        b_spec = pl.BlockSpec((tk, tn), lambda i, j, k: (k, j))
    return pl.pallas_call(
        body, name=name, grid=(M // tm, N // tn, nk),
        in_specs=[a_spec, b_spec],
        out_specs=pl.BlockSpec((tm, tn), lambda i, j, k: (i, j)),
        out_shape=jax.ShapeDtypeStruct((M, N), out_dtype),
        scratch_shapes=[pltpu.VMEM((tm, tn), F32)] if use_acc else [],
        compiler_params=_params(("parallel", "parallel", "arbitrary")),
    )(a, b)


def _all_gather(x, name):
    def body(x_ref, out_ref, send_sems, recv_sems, local_sem):
        mx, my, mc = lax.axis_index("x"), lax.axis_index("y"), lax.axis_index("c")
        me, sibling = (mx, my, mc), (mx, my, 1 - mc)
        chips = [(1 - mx, my), (mx, 1 - my), (1 - mx, 1 - my)]

        def slot(px, py, pc):
            return out_ref.at[4 * px + 2 * py + pc]

        def copy(k, block, to, src=None):
            return pltpu.make_async_remote_copy(
                src_ref=slot(*block) if src is None else src, dst_ref=slot(*block),
                send_sem=send_sems.at[k], recv_sem=recv_sems.at[k],
                device_id=to, device_id_type=MESH)

        mine = pltpu.make_async_copy(x_ref, slot(*me), local_sem)
        mine.start()
        first = [copy(0, me, sibling, src=x_ref)]
        first += [copy(1 + j, me, (*chip, mc), src=x_ref) for j, chip in enumerate(chips)]
        for cp in first:
            cp.start()
        passed = [copy(4 + j, (*chip, mc), sibling) for j, chip in enumerate(chips)]
        for j, chip in enumerate(chips):
            copy(1 + j, (*chip, mc), me).wait_recv()
            passed[j].start()
        copy(0, sibling, me).wait_recv()
        for j, chip in enumerate(chips):
            copy(4 + j, (*chip, 1 - mc), me).wait_recv()
        for cp in first + passed:
            cp.wait_send()
        mine.wait()

    return pl.pallas_call(
        body, name=name,
        out_shape=jax.ShapeDtypeStruct((NDEV,) + x.shape, x.dtype),
        in_specs=[pl.BlockSpec(memory_space=pl.ANY)],
        out_specs=pl.BlockSpec(memory_space=pl.ANY),
        scratch_shapes=[pltpu.SemaphoreType.DMA((7,)), pltpu.SemaphoreType.DMA((7,)),
                        pltpu.SemaphoreType.DMA],
    )(x)


def _owner_exchange(g, name):
    def body(g_ref, out_ref, send_sems, recv_sems, local_sem):
        mx, my, mc = lax.axis_index("x"), lax.axis_index("y"), lax.axis_index("c")
        local = pltpu.make_async_copy(g_ref.at[4 * mx + 2 * my + mc], out_ref.at[0], local_sem)
        local.start()
        copies = []
        for f in range(1, NDEV):
            px = 1 - mx if (f >> 2) & 1 else mx
            py = 1 - my if (f >> 1) & 1 else my
            pc = 1 - mc if f & 1 else mc
            cp = pltpu.make_async_remote_copy(
                src_ref=g_ref.at[4 * px + 2 * py + pc], dst_ref=out_ref.at[f],
                send_sem=send_sems.at[f - 1], recv_sem=recv_sems.at[f - 1],
                device_id=(px, py, pc), device_id_type=MESH)
            cp.start()
            copies.append(cp)
        for cp in copies:
            cp.wait_recv()
        for cp in copies:
            cp.wait_send()
        local.wait()

    return pl.pallas_call(
        body, name=name,
        out_shape=jax.ShapeDtypeStruct(g.shape, g.dtype),
        in_specs=[pl.BlockSpec(memory_space=pl.ANY)],
        out_specs=pl.BlockSpec(memory_space=pl.ANY),
        scratch_shapes=[pltpu.SemaphoreType.DMA((7,)), pltpu.SemaphoreType.DMA((7,)),
                        pltpu.SemaphoreType.DMA],
    )(g)


def _any_specs(n):
    return [pl.BlockSpec(memory_space=pl.ANY)] * n


def _all_gather_multi(xs, name):
    na = len(xs)

    def body(*refs):
        x_refs, out_refs = refs[:na], refs[na:2 * na]
        send_sems, recv_sems, local_sems = refs[2 * na:]
        mx, my, mc = lax.axis_index("x"), lax.axis_index("y"), lax.axis_index("c")
        me, sibling = (mx, my, mc), (mx, my, 1 - mc)
        chips = [(1 - mx, my), (mx, 1 - my), (1 - mx, 1 - my)]

        def copy(a, k, block, to, src=None):
            slot = out_refs[a].at[4 * block[0] + 2 * block[1] + block[2]]
            return pltpu.make_async_remote_copy(
                src_ref=slot if src is None else src, dst_ref=slot,
                send_sem=send_sems.at[7 * a + k], recv_sem=recv_sems.at[7 * a + k],
                device_id=to, device_id_type=MESH)

        mine = [pltpu.make_async_copy(x_refs[a], out_refs[a].at[4 * mx + 2 * my + mc], local_sems.at[a])
                for a in range(na)]
        for cp in mine:
            cp.start()
        first = []
        for a in range(na):
            first.append(copy(a, 0, me, sibling, src=x_refs[a]))
            first += [copy(a, 1 + j, me, (*chip, mc), src=x_refs[a]) for j, chip in enumerate(chips)]
        for cp in first:
            cp.start()
        passed = []
        for a in range(na):
            for j, chip in enumerate(chips):
                copy(a, 1 + j, (*chip, mc), me).wait_recv()
                fwd = copy(a, 4 + j, (*chip, mc), sibling)
                fwd.start()
                passed.append(fwd)
        for a in range(na):
            copy(a, 0, sibling, me).wait_recv()
            for j, chip in enumerate(chips):
                copy(a, 4 + j, (*chip, 1 - mc), me).wait_recv()
        for cp in first + passed:
            cp.wait_send()
        for cp in mine:
            cp.wait()

    return pl.pallas_call(
        body, name=name,
        out_shape=[jax.ShapeDtypeStruct((NDEV,) + x.shape, x.dtype) for x in xs],
        in_specs=_any_specs(na), out_specs=_any_specs(na),
        scratch_shapes=[pltpu.SemaphoreType.DMA((7 * na,)), pltpu.SemaphoreType.DMA((7 * na,)),
                        pltpu.SemaphoreType.DMA((na,))],
    )(*xs)


def _owner_exchange_multi(gs, name):
    na = len(gs)

    def body(*refs):
        g_refs, out_refs = refs[:na], refs[na:2 * na]
        send_sems, recv_sems, local_sems = refs[2 * na:]
        mx, my, mc = lax.axis_index("x"), lax.axis_index("y"), lax.axis_index("c")
        locals_ = [pltpu.make_async_copy(g_refs[a].at[4 * mx + 2 * my + mc], out_refs[a].at[0], local_sems.at[a])
                   for a in range(na)]
        for cp in locals_:
            cp.start()
        copies = []
        for a in range(na):
            for f in range(1, NDEV):
                px = 1 - mx if (f >> 2) & 1 else mx
                py = 1 - my if (f >> 1) & 1 else my
                pc = 1 - mc if f & 1 else mc
                cp = pltpu.make_async_remote_copy(
                    src_ref=g_refs[a].at[4 * px + 2 * py + pc], dst_ref=out_refs[a].at[f],
                    send_sem=send_sems.at[7 * a + f - 1], recv_sem=recv_sems.at[7 * a + f - 1],
                    device_id=(px, py, pc), device_id_type=MESH)
                cp.start()
                copies.append(cp)
        for cp in copies:
            cp.wait_recv()
        for cp in copies:
            cp.wait_send()
        for cp in locals_:
            cp.wait()

    return pl.pallas_call(
        body, name=name,
        out_shape=[jax.ShapeDtypeStruct(g.shape, g.dtype) for g in gs],
        in_specs=_any_specs(na), out_specs=_any_specs(na),
        scratch_shapes=[pltpu.SemaphoreType.DMA((7 * na,)), pltpu.SemaphoreType.DMA((7 * na,)),
                        pltpu.SemaphoreType.DMA((na,))],
    )(*gs)


def _adamw_sum(r8, own, w, m, v, row0, tr, name):
    R, C = w.shape
    assert row0 % tr == 0
    blk0 = row0 // tr
    bc1 = 1.0 - ADAM_B1 ** ADAM_STEP
    bc2 = 1.0 - ADAM_B2 ** ADAM_STEP

    def body(r_ref, *refs):
        if own is None:
            gg = r_ref[0].astype(F32)
        else:
            gg = refs[0][...].astype(F32)
            refs = refs[1:]
        w_ref, m_ref, v_ref, g_ref, d_ref, mo_ref, vo_ref = refs
        for k in range(1, NDEV):
            gg = gg + r_ref[k].astype(F32)
        mn = ADAM_B1 * m_ref[...] + (1.0 - ADAM_B1) * gg
        vn = ADAM_B2 * v_ref[...] + (1.0 - ADAM_B2) * (gg * gg)
        mh = mn / bc1
        vh = vn / bc2
        g_ref[...] = gg
        d_ref[...] = -ADAM_LR * (mh / (jnp.sqrt(vh) + ADAM_EPS) + ADAM_WD * w_ref[...])
        mo_ref[...] = mn
        vo_ref[...] = vn

    spec = pl.BlockSpec((tr, C), lambda i: (i, 0))
    sh = jax.ShapeDtypeStruct((R, C), F32)
    own_ops = [] if own is None else [own]
    own_specs = [] if own is None else [pl.BlockSpec((tr, C), lambda i: (i + blk0, 0))]
    return pl.pallas_call(
        body, name=name, grid=(R // tr,),
        in_specs=[pl.BlockSpec((NDEV, tr, C), lambda i: (0, i + blk0, 0))] + own_specs + [spec, spec, spec],
        out_specs=[spec] * 4, out_shape=[sh] * 4, compiler_params=_params(("parallel",)),
    )(r8, *own_ops, w, m, v)


_HBM = pl.BlockSpec(memory_space=pltpu.HBM)
_SEM = pl.BlockSpec(memory_space=pltpu.SEMAPHORE)
_EFFECT = pltpu.SideEffectType.DATAFLOW_SIDE_EFFECTING


def _exchange_copies(g_refs, land_refs, send_sems, recv_sems, gather):
    mx, my, mc = lax.axis_index("x"), lax.axis_index("y"), lax.axis_index("c")
    copies = []
    for a in range(len(g_refs)):
        for f in ((1, 2, 4, 6) if gather else range(1, NDEV)):
            px = 1 - mx if (f >> 2) & 1 else mx
            py = 1 - my if (f >> 1) & 1 else my
            pc = 1 - mc if f & 1 else mc
            src = g_refs[a] if gather else g_refs[a].at[4 * px + 2 * py + pc]
            dst = land_refs[a].at[4 * mx + 2 * my + mc] if gather else land_refs[a].at[f]
            copies.append(pltpu.make_async_remote_copy(
                src_ref=src, dst_ref=dst,
                send_sem=send_sems.at[7 * a + f - 1], recv_sem=recv_sems.at[7 * a + f - 1],
                device_id=(px, py, pc), device_id_type=MESH))
    return copies


def _exchange_start(gs, name, gather=False):
    na = len(gs)

    def body(*refs):
        for cp in _exchange_copies(refs[:na], refs[na:2 * na], refs[2 * na], refs[2 * na + 1], gather):
            cp.start()
        refs[-1][...] = jnp.zeros_like(refs[-1])

    hbm = [pltpu.HBM(g.shape, g.dtype) for g in gs]
    land_shapes = [((NDEV,) + g.shape) if gather else g.shape for g in gs]
    lands = [pltpu.with_memory_space_constraint(lax.empty(shp, g.dtype), pltpu.HBM)
             for shp, g in zip(land_shapes, gs)]
    hbm_land = [pltpu.HBM(shp, g.dtype) for shp, g in zip(land_shapes, gs)]
    outs = pl.pallas_call(
        body, name=name,
        out_shape=(pltpu.SemaphoreType.DMA((7 * na,)), pltpu.SemaphoreType.DMA((7 * na,)), *hbm, *hbm_land,
                   jax.ShapeDtypeStruct((8, 128), F32)),
        in_specs=[_HBM] * (2 * na),
        out_specs=(_SEM, _SEM, *([_HBM] * (2 * na)), pl.BlockSpec(memory_space=pltpu.VMEM)),
        input_output_aliases={i: 2 + i for i in range(2 * na)},
        compiler_params=pltpu.CompilerParams(has_side_effects=_EFFECT),
    )(*[pltpu.with_memory_space_constraint(g, pltpu.HBM) for g in gs], *lands)
    return outs[0], outs[1], outs[2:2 + na], outs[2 + na:2 + 2 * na], outs[-1]


def _forward_copies(land_refs, send_sems, recv_sems):
    mx, my, mc = lax.axis_index("x"), lax.axis_index("y"), lax.axis_index("c")
    copies = []
    for a in range(len(land_refs)):
        for j, (fx, fy) in enumerate(((0, 1), (1, 0), (1, 1))):
            px = 1 - mx if fx else mx
            py = 1 - my if fy else my
            blk = land_refs[a].at[4 * px + 2 * py + mc]
            copies.append(pltpu.make_async_remote_copy(
                src_ref=blk, dst_ref=blk, send_sem=send_sems.at[3 * a + j], recv_sem=recv_sems.at[3 * a + j],
                device_id=(mx, my, 1 - mc), device_id_type=MESH))
    return copies


def _forward_start(lands, name):
    na = len(lands)

    def body(*refs):
        for cp in _forward_copies(refs[:na], refs[na], refs[na + 1]):
            cp.start()
        refs[-1][...] = jnp.zeros_like(refs[-1])

    outs = pl.pallas_call(
        body, name=name,
        out_shape=(pltpu.SemaphoreType.DMA((3 * na,)), pltpu.SemaphoreType.DMA((3 * na,)),
                   *[pltpu.HBM(g.shape, g.dtype) for g in lands], jax.ShapeDtypeStruct((8, 128), F32)),
        in_specs=[_HBM] * na,
        out_specs=(_SEM, _SEM, *([_HBM] * na), pl.BlockSpec(memory_space=pltpu.VMEM)),
        input_output_aliases={i: 2 + i for i in range(na)},
        compiler_params=pltpu.CompilerParams(has_side_effects=_EFFECT),
    )(*lands)
    return outs[0], outs[1], outs[2:2 + na], outs[-1]


def _forward_wait(send_sems, recv_sems, lands, after, name):
    na = len(lands)

    def body(*refs):
        for cp in _forward_copies(refs[:na], refs[na], refs[na + 1]):
            cp.wait_send()
            cp.wait_recv()

    return pl.pallas_call(
        body, name=name,
        out_shape=tuple(pltpu.HBM(g.shape, g.dtype) for g in lands),
        in_specs=[_HBM] * na + [_SEM, _SEM, pl.BlockSpec(memory_space=pl.ANY)],
        out_specs=tuple([_HBM] * na),
        input_output_aliases={i: i for i in range(na)},
        compiler_params=pltpu.CompilerParams(has_side_effects=_EFFECT),
    )(*lands, send_sems, recv_sems, after)


def _exchange_wait(send_sems, recv_sems, g_thru, land_thru, after, name, gather=False):
    na = len(g_thru)

    def body(*refs):
        for cp in _exchange_copies(refs[:na], refs[na:2 * na], refs[2 * na], refs[2 * na + 1], gather):
            cp.wait_send()
            cp.wait_recv()

    outs = pl.pallas_call(
        body, name=name,
        out_shape=tuple(pltpu.HBM(g.shape, g.dtype) for g in list(g_thru) + list(land_thru)),
        in_specs=[_HBM] * (2 * na) + [_SEM, _SEM, pl.BlockSpec(memory_space=pl.ANY)],
        out_specs=tuple([_HBM] * (2 * na)),
        input_output_aliases={i: i for i in range(2 * na)},
        compiler_params=pltpu.CompilerParams(has_side_effects=_EFFECT),
    )(*g_thru, *land_thru, send_sems, recv_sems, after)
    return outs[na:]


def _sum8(r, name):
    _, R, C = r.shape
    tr = _tile(R, (256, 160, 128, 64, 32, 16, 8))

    def body(r_ref, o_ref):
        acc = r_ref[0].astype(F32)
        for k in range(1, NDEV):
            acc = acc + r_ref[k].astype(F32)
        o_ref[...] = acc

    return pl.pallas_call(
        body, name=name, grid=(R // tr,),
        in_specs=[pl.BlockSpec((NDEV, tr, C), lambda i: (0, i, 0))],
        out_specs=pl.BlockSpec((tr, C), lambda i: (i, 0)),
        out_shape=jax.ShapeDtypeStruct((R, C), F32),
        compiler_params=_params(("parallel",)),
    )(r)


def _adamw(g, w, m, v, name):
    R, C = g.shape
    tr = _tile(R, (256, 160, 128, 64, 32, 16, 8))
    bc1 = 1.0 - ADAM_B1 ** ADAM_STEP
    bc2 = 1.0 - ADAM_B2 ** ADAM_STEP

    def body(g_ref, w_ref, m_ref, v_ref, d_ref, mo_ref, vo_ref):
        gg = g_ref[...]
        mn = ADAM_B1 * m_ref[...] + (1.0 - ADAM_B1) * gg
        vn = ADAM_B2 * v_ref[...] + (1.0 - ADAM_B2) * (gg * gg)
        mh = mn / bc1
        vh = vn / bc2
        d_ref[...] = -ADAM_LR * (mh / (jnp.sqrt(vh) + ADAM_EPS) + ADAM_WD * w_ref[...])
        mo_ref[...] = mn
        vo_ref[...] = vn

    spec = pl.BlockSpec((tr, C), lambda i: (i, 0))
    sh = jax.ShapeDtypeStruct((R, C), F32)
    return pl.pallas_call(
        body, name=name, grid=(R // tr,), in_specs=[spec] * 4, out_specs=[spec] * 3,
        out_shape=[sh] * 3, compiler_params=_params(("parallel",)),
    )(g, w, m, v)


def _ada_fwd(c16, w_sh, b_sh):
    def body(c_ref, w_ref, b_ref, o_ref):
        c = c_ref[...]
        o_ref[...] = _dot(c * _sig(c), w_ref[...], NN) + b_ref[...]

    return pl.pallas_call(
        body, name="ada_fwd", out_shape=jax.ShapeDtypeStruct((16, w_sh.shape[1]), F32),
        compiler_params=pltpu.CompilerParams(vmem_limit_bytes=VMEM_LIMIT),
    )(c16, w_sh, b_sh)


def _ada_bwd(c16, g16, g16_sh, w_sh):
    ncol = w_sh.shape[1]

    def body(c_ref, g_ref, gs_ref, w_ref, dw_ref, db_ref, dc_ref):
        c = c_ref[...]
        s = _sig(c)
        gs = gs_ref[...]
        dw_ref[...] = _dot(c * s, gs, TN)
        db_ref[...] = jnp.broadcast_to(_colsum(g_ref[...]), db_ref.shape)
        odd = lax.broadcasted_iota(jnp.int32, gs.shape, 0) % 2 == 1
        gc = _colsum(jnp.where(odd, gs, 0.0))
        ds = _dot(jnp.broadcast_to(gc, (8, ncol)), w_ref[...], NT)
        c1 = c[1:2, :]
        s1 = s[1:2, :]
        dc_ref[...] = ds * (s1 * (1.0 + c1 * (1.0 - s1)))

    return pl.pallas_call(
        body, name="ada_bwd",
        out_shape=[jax.ShapeDtypeStruct(w_sh.shape, F32),
                   jax.ShapeDtypeStruct((8, g16.shape[1]), F32),
                   jax.ShapeDtypeStruct((8, D), F32)],
        compiler_params=pltpu.CompilerParams(vmem_limit_bytes=VMEM_LIMIT),
    )(c16, g16, g16_sh, w_sh)


def _ln0_fwd(x, ctx, g, b, modx, modc):
    L = x.shape[0]
    nt = L // TL

    def body(x_ref, c_ref, g_ref, b_ref, mx_ref, mc_ref, xn_ref, h_ref):
        isc = pl.program_id(0) == nt
        xin = jnp.where(isc, c_ref[...], x_ref[...])
        sh = jnp.where(isc, mc_ref[0:1, :], mx_ref[0:1, :])
        sc = jnp.where(isc, mc_ref[1:2, :], mx_ref[1:2, :])
        xhat, _ = _ln(xin)
        xn = xhat * g_ref[...] + b_ref[...]
        xn_ref[...] = xn
        h_ref[...] = (xn * (1.0 + sc) + sh).astype(h_ref.dtype)

    return pl.pallas_call(
        body, name="ln0_fwd", grid=(nt + 1,),
        in_specs=[_rtc(D, nt), _cst((TL, D)), _cst((1, D)), _cst((1, D)), _cst((8, D)), _cst((8, D))],
        out_specs=[_rt(D), _rt(D)],
        out_shape=[jax.ShapeDtypeStruct((L + TL, D), F32), jax.ShapeDtypeStruct((L + TL, D), _MXU)],
        compiler_params=_params(("parallel",)),
    )(x, ctx, g, b, modx, modc)


def _xbc_colblk(j):
    return jnp.where(j < 8, OXS // 128 + j, OB // 128 + j - 8)


def _conv_taps(p_ref, r0, first, last):
    main = p_ref[pl.ds(r0, TL), :]
    zero = jnp.zeros((8, main.shape[1]), F32)
    prev = zero if first else p_ref[pl.ds(r0 - 8, 8), :]
    nxt = zero if last else p_ref[pl.ds(r0 + TL, 8), :]
    ext = jnp.concatenate([prev, main, nxt], axis=0)
    n = TL + 16
    return [pltpu.roll(ext, (2 - k) % n, 0)[8:8 + TL] for k in range(5)]


def _seq_chunks(L):
    nt = L // TL
    return [(r * TL, r == 0, r == nt - 1) for r in range(nt)] + [(L, True, True)]


def _conv_fwd(p, conv_w8, conv_b):
    RT = p.shape[0]
    L = RT - TL
    chunks = _seq_chunks(L)

    def body(p_ref, w_ref, b_ref, o_ref):
        w = w_ref[...]
        bias = b_ref[...]
        for r0, first, last in chunks:
            taps = _conv_taps(p_ref, r0, first, last)
            pre = bias + sum(w[k:k + 1, :] * taps[k] for k in range(5))
            o_ref[pl.ds(r0, TL), :] = pre * _sig(pre)

    return pl.pallas_call(
        body, name="conv_fwd", grid=(12,),
        in_specs=[pl.BlockSpec((RT, 128), lambda j: (0, _xbc_colblk(j))),
                  pl.BlockSpec((8, 128), lambda j: (0, j)),
                  pl.BlockSpec((1, 128), lambda j: (0, j))],
        out_specs=pl.BlockSpec((RT, 128), lambda j: (0, j)),
        out_shape=jax.ShapeDtypeStruct((RT, 1536), F32),
        compiler_params=_params(("parallel",)),
    )(p, conv_w8, conv_b)


def _ssd_common(dtraw, dtb, a32, rev):
    dt = _softplus(dtraw + dtb)
    acum = _cumsum_rows(dt * a32, rev)
    ii = lax.broadcasted_iota(jnp.int32, (Q, Q), 0)
    jj = lax.broadcasted_iota(jnp.int32, (Q, Q), 1)
    mask = (ii <= jj) if rev else (ii >= jj)
    return dt, acum, acum.T, dt.T, mask


def _ssd_orders(ncl, ncc):
    nc = ncl + ncc

    def cf(s):
        return jnp.where(s < ncc, ncl + s, s - ncc)

    def cb(s):
        return nc - 1 - s

    return cf, cb


def _ssd_fwd(xbc, p, prm):
    RT = xbc.shape[0]
    nc = RT // Q
    ncc = TL // Q
    cf, cb = _ssd_orders(nc - ncc, ncc)

    def one_dir(x_ref, dt_ref, prm_ref, y_ref, hp_ref, H_ref, d):
        rev = d == 1
        a32 = -jnp.exp(prm_ref[1:2, :])
        dt, acum, acumT, dtT, mask = _ssd_common(dt_ref[...], prm_ref[0:1, :], a32, rev)
        end = 0 if rev else Q - 1
        for g in range(2):
            Bg = x_ref[:, D + g * NS:D + (g + 1) * NS]
            Cg = x_ref[:, D + 2 * NS + g * NS:D + 2 * NS + (g + 1) * NS]
            CB = _dot(Cg, Bg, NT)
            for hh in range(HPG):
                h = g * HPG + hh
                ln = 16 * d + h
                col = acum[:, ln:ln + 1]
                rowv = acumT[ln:ln + 1, :]
                a_end = rowv[:, end:end + 1]
                Lm = jnp.exp(jnp.where(mask, col - rowv, -1e30))
                W = CB * Lm * dtT[ln:ln + 1, :]
                Xh = x_ref[:, h * HP:(h + 1) * HP]
                Hp = H_ref[h * HP:(h + 1) * HP, :]
                y = _dot(W, Xh, NN) + jnp.exp(col) * _dot(Cg, Hp, NT)
                y_ref[:, h * HP:(h + 1) * HP] = y
                dcol = jnp.exp(a_end - col) * dt[:, ln:ln + 1]
                hp_ref[0, h * HP:(h + 1) * HP, :] = Hp
                H_ref[h * HP:(h + 1) * HP, :] = jnp.exp(a_end) * Hp + _dot(Xh * dcol, Bg, TN)

    def body(xf_ref, xb_ref, df_ref, db_ref, prm_ref, yf_ref, yb_ref, hf_ref, hb_ref, Hf, Hb):
        @pl.when(pl.program_id(0) == 0)
        def _():
            Hf[...] = jnp.zeros_like(Hf)
            Hb[...] = jnp.zeros_like(Hb)

        one_dir(xf_ref, df_ref, prm_ref, yf_ref, hf_ref, Hf, 0)
        one_dir(xb_ref, db_ref, prm_ref, yb_ref, hb_ref, Hb, 1)

    ysh = jax.ShapeDtypeStruct((RT, D), F32)
    hsh = jax.ShapeDtypeStruct((nc, NH * HP, NS), F32)
    hspec = pl.BlockSpec((1, NH * HP, NS), lambda s: (s, 0, 0))
    return pl.pallas_call(
        body, name="ssd_fwd", grid=(nc,),
        in_specs=[pl.BlockSpec((Q, 1536), lambda s: (cf(s), 0)),
                  pl.BlockSpec((Q, 1536), lambda s: (cb(s), 0)),
                  pl.BlockSpec((Q, 128), lambda s: (cf(s), ODT // 128)),
                  pl.BlockSpec((Q, 128), lambda s: (cb(s), ODT // 128)),
                  _cst((8, 128))],
        out_specs=[pl.BlockSpec((Q, D), lambda s: (cf(s), 0)),
                   pl.BlockSpec((Q, D), lambda s: (cb(s), 0)), hspec, hspec],
        out_shape=[ysh, ysh, hsh, hsh],
        scratch_shapes=[pltpu.VMEM((NH * HP, NS), F32), pltpu.VMEM((NH * HP, NS), F32)],
        compiler_params=_params(("arbitrary",)),
    )(xbc, xbc, p, p, prm)


def _ssd_bwd(xbc, p, prm, dsk, dyd, hpf, hpb):
    RT = xbc.shape[0]
    nc = RT // Q
    ncc = TL // Q
    ncl = nc - ncc
    cf, cb = _ssd_orders(ncl, ncc)

    def rs(t):
        return nc - 1 - t

    def one_dir(x_ref, dt_ref, prm_ref, dsk_ref, dy_ref, is_ctx, hp_ref, dH_ref,
                dx_ref, ddt_ref, st_ref, d):
        rev = d == 1
        a32 = -jnp.exp(prm_ref[1:2, :])
        dtraw = dt_ref[...]
        dtb = prm_ref[0:1, :]
        dt, acum, acumT, dtT, mask = _ssd_common(dtraw, dtb, a32, rev)
        end = 0 if rev else Q - 1
        lane = lax.broadcasted_iota(jnp.int32, (Q, 128), 1)
        srow = lax.broadcasted_iota(jnp.int32, (Q, 128), 0)
        dyscale = jnp.where(is_ctx, 0.0, 1.0)
        c_dacum = jnp.zeros((Q, 128), F32)
        r_dacum = jnp.zeros((Q, 128), F32)
        c_ddt = jnp.zeros((Q, 128), F32)
        r_ddt = jnp.zeros((Q, 128), F32)
        dskacc = jnp.zeros((1, 128), F32)
        for g in range(2):
            Bg = x_ref[:, D + g * NS:D + (g + 1) * NS]
            Cg = x_ref[:, D + 2 * NS + g * NS:D + 2 * NS + (g + 1) * NS]
            CB = _dot(Cg, Bg, NT)
            dCB = jnp.zeros((Q, Q), F32)
            dBg = jnp.zeros((Q, NS), F32)
            dCg = jnp.zeros((Q, NS), F32)
            for hh in range(HPG):
                h = g * HPG + hh
                ln = 16 * d + h
                hs = slice(h * HP, (h + 1) * HP)
                col = acum[:, ln:ln + 1]
                rowv = acumT[ln:ln + 1, :]
                dtr = dtT[ln:ln + 1, :]
                dtc = dt[:, ln:ln + 1]
                a_end = rowv[:, end:end + 1]
                Lm = jnp.exp(jnp.where(mask, col - rowv, -1e30))
                E = jnp.exp(col)
                ecol = jnp.exp(a_end - col)
                dcol = ecol * dtc
                Xh = x_ref[:, hs]
                dY = dy_ref[:, hs] * dyscale
                Hp = hp_ref[0, hs, :]
                dHn = dH_ref[hs, :]
                W = CB * Lm * dtr
                dW = _dot(dY, Xh, NT)
                Mm = dW * CB * Lm
                T = Mm * dtr
                dCB = dCB + dW * Lm * dtr
                BdH = _dot(Bg, dHn, NT)
                dX = _dot(W, dY, TN) + dcol * BdH
                if d == 0:
                    dX = dX + dY * dsk_ref[:, hs]
                    dskacc = dskacc + jnp.where(lane[0:1, :] == h, _sum11(dY * Xh), 0.0)
                dx_ref[:, hs] = dX
                xb = jnp.sum(Xh * BdH, axis=1, keepdims=True)
                scol = dcol * xb
                G = _dot(dY, Hp, NN)
                dCg = dCg + E * G
                qcol = E * jnp.sum(G * Cg, axis=1, keepdims=True)
                dBg = dBg + _dot(Xh * dcol, dHn, NN)
                dH_ref[hs, :] = jnp.exp(a_end) * dHn + _dot(dY * E, Cg, TN)
                eterm = jnp.exp(a_end) * _sum11(dHn * Hp) + _sum11(scol)
                cvec = jnp.sum(T, axis=1, keepdims=True) + qcol - scol
                cvec = cvec + jnp.where(srow[:, 0:1] == end, eterm, 0.0)
                c_dacum = c_dacum + jnp.where(lane == ln, cvec, 0.0)
                r_dacum = r_dacum - jnp.where(srow == ln, _colsum(T), 0.0)
                c_ddt = c_ddt + jnp.where(lane == ln, ecol * xb, 0.0)
                r_ddt = r_ddt + jnp.where(srow == ln, _colsum(Mm), 0.0)
            dBg = dBg + _dot(dCB, Cg, TN)
            dCg = dCg + _dot(dCB, Bg, NN)
            dx_ref[:, D + g * NS:D + (g + 1) * NS] = dBg
            dx_ref[:, D + 2 * NS + g * NS:D + 2 * NS + (g + 1) * NS] = dCg
        dacum = c_dacum + r_dacum.T
        da = _cumsum_rows(dacum, not rev)
        mine = (lane >= 16 * d) & (lane < 16 * d + 16)
        ddt = jnp.where(mine, c_ddt + r_ddt.T + da * a32, 0.0)
        ddt_ref[...] = ddt * _sig(dtraw + dtb)
        st_ref[0:1, :] += _colsum(jnp.where(mine, da * dt, 0.0))
        if d == 0:
            st_ref[1:2, :] += dskacc

    def body(xf_ref, xb_ref, df_ref, db_ref, prm_ref, dsk_ref, dyf_ref, dyb_ref, hf_ref, hb_ref,
             dxf_ref, dxb_ref, ddf_ref, ddb_ref, st_ref, dHf, dHb):
        t = pl.program_id(0)

        @pl.when(t == 0)
        def _():
            dHf[...] = jnp.zeros_like(dHf)
            dHb[...] = jnp.zeros_like(dHb)
            st_ref[...] = jnp.zeros_like(st_ref)

        s = rs(t)
        one_dir(xf_ref, df_ref, prm_ref, dsk_ref, dyf_ref, cf(s) >= ncl, hf_ref, dHf,
                dxf_ref, ddf_ref, st_ref, 0)
        one_dir(xb_ref, db_ref, prm_ref, dsk_ref, dyb_ref, cb(s) >= ncl, hb_ref, dHb,
                dxb_ref, ddb_ref, st_ref, 1)

        @pl.when(t == nc - 1)
        def _():
            st_ref[0:1, :] = -jnp.exp(prm_ref[1:2, :]) * st_ref[0:1, :]

    def lat(c):
        return jnp.minimum(c, ncl - 1)

    xsh = jax.ShapeDtypeStruct((RT, 1536), F32)
    dsh = jax.ShapeDtypeStruct((RT, 128), F32)
    hspec = pl.BlockSpec((1, NH * HP, NS), lambda t: (rs(t), 0, 0))
    return pl.pallas_call(
        body, name="ssd_bwd", grid=(nc,),
        in_specs=[pl.BlockSpec((Q, 1536), lambda t: (cf(rs(t)), 0)),
                  pl.BlockSpec((Q, 1536), lambda t: (cb(rs(t)), 0)),
                  pl.BlockSpec((Q, 128), lambda t: (cf(rs(t)), ODT // 128)),
                  pl.BlockSpec((Q, 128), lambda t: (cb(rs(t)), ODT // 128)),
                  _cst((8, 128)), _cst((1, D)),
                  pl.BlockSpec((Q, D), lambda t: (lat(cf(rs(t))), 0)),
                  pl.BlockSpec((Q, D), lambda t: (lat(cb(rs(t))), 0)),
                  hspec, hspec],
        out_specs=[pl.BlockSpec((Q, 1536), lambda t: (cf(rs(t)), 0)),
                   pl.BlockSpec((Q, 1536), lambda t: (cb(rs(t)), 0)),
                   pl.BlockSpec((Q, 128), lambda t: (cf(rs(t)), 0)),
                   pl.BlockSpec((Q, 128), lambda t: (cb(rs(t)), 0)),
                   _cst((8, 128))],
        out_shape=[xsh, xsh, dsh, dsh, jax.ShapeDtypeStruct((8, 128), F32)],
        scratch_shapes=[pltpu.VMEM((NH * HP, NS), F32), pltpu.VMEM((NH * HP, NS), F32)],
        compiler_params=_params(("arbitrary",)),
    )(xbc, xbc, p, p, prm, dsk, dyd, dyd, hpf, hpb)


def _lane_bcast(v, ln):
    return jnp.broadcast_to(v[:, ln:ln + 1], v.shape)


def _halves(v, lo, axis):
    return jnp.concatenate([jnp.where(lo, v, 0.0), jnp.where(lo, 0.0, v)], axis=axis)


def _ssd2_fwd(xbc, p, prm):
    RT = xbc.shape[0]
    nc = RT // Q
    ncc = TL // Q
    cf, cb = _ssd_orders(nc - ncc, ncc)

    def one_dir(x_ref, dt_ref, prm_ref, y_ref, hp_ref, HT_ref, d):
        rev = d == 1
        a32 = -jnp.exp(prm_ref[1:2, :])
        dt, acum, acumT, dtT, mask = _ssd_common(dt_ref[...], prm_ref[0:1, :], a32, rev)
        end = 0 if rev else Q - 1
        lo = lax.broadcasted_iota(jnp.int32, (Q, 128), 1) < HP
        for g in range(2):
            Bg = x_ref[:, D + g * NS:D + (g + 1) * NS]
            Cg = x_ref[:, D + 2 * NS + g * NS:D + 2 * NS + (g + 1) * NS]
            CB = _dot(Cg, Bg, NT)
            xds, svs = [], []
            for q in range(HPG // 2):
                pi = g * (HPG // 2) + q
                ps = slice(pi * 128, (pi + 1) * 128)
                Xp = x_ref[:, ps]
                HTp = HT_ref[:, ps]
                lhs, dcs, sv = [], [], []
                ces = []
                for h in (2 * pi, 2 * pi + 1):
                    ln = 16 * d + h
                    colB = _lane_bcast(acum, ln)
                    rowv = acumT[ln:ln + 1, :]
                    aend = colB[end:end + 1, :]
                    Lm = jnp.exp(jnp.where(mask, colB - rowv, -1e30))
                    lhs.append(CB * Lm * dtT[ln:ln + 1, :])
                    ces.append(Cg * jnp.exp(colB))
                    dcs.append(jnp.exp(aend - colB) * _lane_bcast(dt, ln))
                    sv.append(jnp.exp(aend))
                lhs = jnp.concatenate(lhs + ces, axis=1)
                rhs = jnp.concatenate([_halves(Xp, lo, 0), _halves(HTp, lo, 0)], axis=0)
                y_ref[:, ps] = _dot(lhs, rhs, NN)
                xds.append(Xp * jnp.where(lo, dcs[0], dcs[1]))
                svs.append(jnp.where(lo[0:1, :], sv[0], sv[1]))
            gs = slice(g * 512, (g + 1) * 512)
            HTg = HT_ref[:, gs]
            hp_ref[0, :, gs] = HTg
            st = _dot(Bg.T, jnp.concatenate(xds, axis=1), NN)
            HT_ref[:, gs] = jnp.concatenate(svs, axis=1) * HTg + st

    def body(xf_ref, xb_ref, df_ref, db_ref, prm_ref, yf_ref, yb_ref, hf_ref, hb_ref, Hf, Hb):
        @pl.when(pl.program_id(0) == 0)
        def _():
            Hf[...] = jnp.zeros_like(Hf)
            Hb[...] = jnp.zeros_like(Hb)

        one_dir(xf_ref, df_ref, prm_ref, yf_ref, hf_ref, Hf, 0)
        one_dir(xb_ref, db_ref, prm_ref, yb_ref, hb_ref, Hb, 1)

    ysh = jax.ShapeDtypeStruct((RT, D), F32)
    hsh = jax.ShapeDtypeStruct((nc, NS, NH * HP), F32)
    hspec = pl.BlockSpec((1, NS, NH * HP), lambda s: (s, 0, 0))
    return pl.pallas_call(
        body, name="ssd_fwd", grid=(nc,),
        in_specs=[pl.BlockSpec((Q, 1536), lambda s: (cf(s), 0)),
                  pl.BlockSpec((Q, 1536), lambda s: (cb(s), 0)),
                  pl.BlockSpec((Q, 128), lambda s: (cf(s), ODT // 128)),
                  pl.BlockSpec((Q, 128), lambda s: (cb(s), ODT // 128)),
                  _cst((8, 128))],
        out_specs=[pl.BlockSpec((Q, D), lambda s: (cf(s), 0)),
                   pl.BlockSpec((Q, D), lambda s: (cb(s), 0)), hspec, hspec],
        out_shape=[ysh, ysh, hsh, hsh],
        scratch_shapes=[pltpu.VMEM((NS, NH * HP), F32), pltpu.VMEM((NS, NH * HP), F32)],
        compiler_params=_params(("arbitrary",)),
    )(xbc, xbc, p, p, prm)


def _ssd2_bwd(xbc, p, prm, dsk, dyd, hpf, hpb):
    RT = xbc.shape[0]
    nc = RT // Q
    ncc = TL // Q
    ncl = nc - ncc
    cf, cb = _ssd_orders(ncl, ncc)

    def rs(t):
        return nc - 1 - t

    def one_dir(x_ref, dt_ref, prm_ref, dsk_ref, dy_ref, is_ctx, hp_ref, dHT_ref,
                dx_ref, ddt_ref, st_ref, d):
        rev = d == 1
        a32 = -jnp.exp(prm_ref[1:2, :])
        dtraw = dt_ref[...]
        dtb = prm_ref[0:1, :]
        dt, acum, acumT, _, _ = _ssd_common(dtraw, dtb, a32, rev)
        end = 0 if rev else Q - 1
        lane = lax.broadcasted_iota(jnp.int32, (Q, 128), 1)
        srow = lax.broadcasted_iota(jnp.int32, (Q, 128), 0)
        maskT = (lane <= srow) if rev else (lane >= srow)
        lo = lane < HP
        lo1 = lo[0:1, :]
        dyscale = jnp.where(is_ctx, 0.0, 1.0)
        c_dacum = jnp.zeros((Q, 128), F32)
        r_dacum = jnp.zeros((Q, 128), F32)
        c_ddt = jnp.zeros((Q, 128), F32)
        dskacc = jnp.zeros((1, 128), F32)
        for g in range(2):
            gs = slice(g * 512, (g + 1) * 512)
            Bg = x_ref[:, D + g * NS:D + (g + 1) * NS]
            Cg = x_ref[:, D + 2 * NS + g * NS:D + 2 * NS + (g + 1) * NS]
            CBT = _dot(Bg, Cg, NT)
            HTg = hp_ref[0, :, gs]
            dHTg = dHT_ref[:, gs]
            BdHg = _dot(Bg, dHTg, NN)
            dCBT = jnp.zeros((Q, Q), F32)
            dCg = jnp.zeros((Q, NS), F32)
            xds, dyes, svs = [], [], []
            for q in range(HPG // 2):
                pi = g * (HPG // 2) + q
                ps = slice(pi * 128, (pi + 1) * 128)
                qs = slice(q * 128, (q + 1) * 128)
                Xp = x_ref[:, ps]
                dYp = dy_ref[:, ps] * dyscale
                HTp = HTg[:, qs]
                BdHp = BdHg[:, qs]
                dY2 = _halves(dYp, lo, 0)
                dWT2 = _dot(_halves(Xp, lo, 0), dYp.T, NN)
                G2 = _dot(dY2, HTp, NT)
                XB = Xp * BdHp
                hh = _colsum(dHTg[:, qs] * HTp)
                yx = _colsum(dYp * Xp)
                wts, dcs, ebs, sv = [], [], [], []
                for k, h in enumerate((2 * pi, 2 * pi + 1)):
                    ln = 16 * d + h
                    half = lo if k == 0 else jnp.logical_not(lo)
                    half1 = half[0:1, :]
                    colB = _lane_bcast(acum, ln)
                    dtcB = _lane_bcast(dt, ln)
                    rowv = acumT[ln:ln + 1, :]
                    aend = colB[end:end + 1, :]
                    LmT = jnp.exp(jnp.where(maskT, rowv - colB, -1e30))
                    WT = CBT * LmT * dtcB
                    dWT = dWT2[k * Q:(k + 1) * Q, :]
                    MT = dWT * CBT * LmT
                    rM = jnp.sum(MT, axis=1, keepdims=True)
                    rT = _colsum(MT * dtcB)
                    dCBT = dCBT + dWT * LmT * dtcB
                    ecol = jnp.exp(aend - colB)
                    EB = jnp.exp(colB)
                    Gk = G2[k * Q:(k + 1) * Q, :]
                    dCg = dCg + EB * Gk
                    qcol = jnp.sum(EB * Gk * Cg, axis=1, keepdims=True)
                    xb = jnp.sum(jnp.where(half, XB, 0.0), axis=1, keepdims=True)
                    e1 = ecol[:, 0:1]
                    dt1 = dtcB[:, 0:1]
                    scol = e1 * dt1 * xb
                    sA = jnp.exp(aend)
                    eterm = sA[:, 0:1] * jnp.sum(jnp.where(half1, hh, 0.0), axis=1, keepdims=True) \
                        + _colsum(scol)
                    cvec = qcol - dt1 * rM - scol + jnp.where(srow[:, 0:1] == end, eterm, 0.0)
                    c_dacum = c_dacum + jnp.where(lane == ln, cvec, 0.0)
                    r_dacum = r_dacum + jnp.where(srow == ln, rT, 0.0)
                    c_ddt = c_ddt + jnp.where(lane == ln, rM + e1 * xb, 0.0)
                    if d == 0:
                        dskacc = dskacc + jnp.where(
                            lane[0:1, :] == h, jnp.sum(jnp.where(half1, yx, 0.0), axis=1, keepdims=True), 0.0)
                    wts.append(WT)
                    dcs.append(ecol * dtcB)
                    ebs.append(EB)
                    sv.append(sA)
                dcp = jnp.where(lo, dcs[0], dcs[1])
                dX = _dot(jnp.concatenate(wts, axis=1), dY2, NN) + dcp * BdHp
                if d == 0:
                    dX = dX + dYp * dsk_ref[:, ps]
                dx_ref[:, ps] = dX
                xds.append(Xp * dcp)
                dyes.append(dYp * jnp.where(lo, ebs[0], ebs[1]))
                svs.append(jnp.where(lo1, sv[0], sv[1]))
            dx_ref[:, D + g * NS:D + (g + 1) * NS] = (
                _dot(jnp.concatenate(xds, axis=1), dHTg, NT) + _dot(dCBT, Cg, NN))
            dx_ref[:, D + 2 * NS + g * NS:D + 2 * NS + (g + 1) * NS] = dCg + _dot(dCBT, Bg, TN)
            dHT_ref[:, gs] = (jnp.concatenate(svs, axis=1) * dHTg
                              + _dot(Cg.T, jnp.concatenate(dyes, axis=1), NN))
        dacum = c_dacum + r_dacum.T
        da = _cumsum_rows(dacum, not rev)
        mine = (lane >= 16 * d) & (lane < 16 * d + 16)
        ddt = jnp.where(mine, c_ddt + da * a32, 0.0)
        ddt_ref[...] = ddt * _sig(dtraw + dtb)
        st_ref[0:1, :] += _colsum(jnp.where(mine, da * dt, 0.0))
        if d == 0:
            st_ref[1:2, :] += dskacc

    def body(xf_ref, xb_ref, df_ref, db_ref, prm_ref, dsk_ref, dyf_ref, dyb_ref, hf_ref, hb_ref,
             dxf_ref, dxb_ref, ddf_ref, ddb_ref, st_ref, dHf, dHb):
        t = pl.program_id(0)

        @pl.when(t == 0)
        def _():
            dHf[...] = jnp.zeros_like(dHf)
            dHb[...] = jnp.zeros_like(dHb)
            st_ref[...] = jnp.zeros_like(st_ref)

        s = rs(t)
        one_dir(xf_ref, df_ref, prm_ref, dsk_ref, dyf_ref, cf(s) >= ncl, hf_ref, dHf,
                dxf_ref, ddf_ref, st_ref, 0)
        one_dir(xb_ref, db_ref, prm_ref, dsk_ref, dyb_ref, cb(s) >= ncl, hb_ref, dHb,
                dxb_ref, ddb_ref, st_ref, 1)

        @pl.when(t == nc - 1)
        def _():
            st_ref[0:1, :] = -jnp.exp(prm_ref[1:2, :]) * st_ref[0:1, :]

    def lat(c):
        return jnp.minimum(c, ncl - 1)

    xsh = jax.ShapeDtypeStruct((RT, 1536), F32)
    dsh = jax.ShapeDtypeStruct((RT, 128), F32)
    hspec = pl.BlockSpec((1, NS, NH * HP), lambda t: (rs(t), 0, 0))
    return pl.pallas_call(
        body, name="ssd_bwd", grid=(nc,),
        in_specs=[pl.BlockSpec((Q, 1536), lambda t: (cf(rs(t)), 0)),
                  pl.BlockSpec((Q, 1536), lambda t: (cb(rs(t)), 0)),
                  pl.BlockSpec((Q, 128), lambda t: (cf(rs(t)), ODT // 128)),
                  pl.BlockSpec((Q, 128), lambda t: (cb(rs(t)), ODT // 128)),
                  _cst((8, 128)), _cst((1, D)),
                  pl.BlockSpec((Q, D), lambda t: (lat(cf(rs(t))), 0)),
                  pl.BlockSpec((Q, D), lambda t: (lat(cb(rs(t))), 0)),
                  hspec, hspec],
        out_specs=[pl.BlockSpec((Q, 1536), lambda t: (cf(rs(t)), 0)),
                   pl.BlockSpec((Q, 1536), lambda t: (cb(rs(t)), 0)),
                   pl.BlockSpec((Q, 128), lambda t: (cf(rs(t)), 0)),
                   pl.BlockSpec((Q, 128), lambda t: (cb(rs(t)), 0)),
                   _cst((8, 128))],
        out_shape=[xsh, xsh, dsh, dsh, jax.ShapeDtypeStruct((8, 128), F32)],
        scratch_shapes=[pltpu.VMEM((NS, NH * HP), F32), pltpu.VMEM((NS, NH * HP), F32)],
        compiler_params=_params(("arbitrary",)),
    )(xbc, xbc, p, p, prm, dsk, dyd, dyd, hpf, hpb)


def _mix_fwd_vals(yf, yb, z, xs, u, v, dsk, sg, gg, gb):
    y = yf + yb + xs * dsk
    sz = _sig(z)
    hh = y * z * sz
    r = lax.rsqrt(jnp.mean(hh * hh, axis=-1, keepdims=True) + EPS)
    nh = hh * r
    ug, tu = _gelu(u)
    vg, tv = _gelu(v)
    vhat, vrstd = _ln(vg)
    vn = vhat * gg + gb
    return y, sz, r, nh, ug, tu, vg, tv, vhat, vrstd, vn


def _mix_fwd(yf, yb, p, xbc, dsk, sg, gg, gb, ws, bsT):
    L = yf.shape[0] - TL
    nt = L // TL

    def body(yf_ref, yb_ref, z_ref, xs_ref, u_ref, v_ref, dsk_ref, sg_ref, gg_ref, gb_ref,
             ws_ref, bs_ref, ys_ref, ym_ref):
        _, _, _, nh, ug, _, _, _, _, _, vn = _mix_fwd_vals(
            yf_ref[...], yb_ref[...], z_ref[...], xs_ref[...], u_ref[...], v_ref[...],
            dsk_ref[...], sg_ref[...], gg_ref[...], gb_ref[...])
        ys_ref[...] = (nh * sg_ref[...]).astype(ys_ref.dtype)
        for n in range(TL // Q):
            rs_ = slice(n * Q, (n + 1) * Q)
            for g in range(8):
                cs = slice(g * 128, (g + 1) * 128)
                mixed = _dot(ws_ref[g], vn[rs_, cs], NN) + bs_ref[:, g:g + 1]
                ym_ref[rs_, cs] = (ug[rs_, cs] * mixed).astype(ym_ref.dtype)

    return pl.pallas_call(
        body, name="mix_fwd", grid=(nt,),
        in_specs=[_rt(D), _rt(D), _rt(D, OZ // D), _rt(D, 0), _rt(D, OU // D), _rt(D, OV // D),
                  _cst((1, D)), _cst((1, D)), _cst((1, D)), _cst((1, D)),
                  _cst((8, 128, 128)), _cst((128, 128))],
        out_specs=[_rt(D), _rt(D)],
        out_shape=[jax.ShapeDtypeStruct((L, D), _MXU), jax.ShapeDtypeStruct((L, D), _MXU)],
        compiler_params=_params(("parallel",)),
    )(yf, yb, p, xbc, p, p, dsk, sg, gg, gb, ws, bsT)


def _mix_bwd(dys, dym, yf, yb, p, xbc, dp, dsk, sg, gg, gb, ws, bsT):
    L = dys.shape[0]
    nt = L // TL

    def body(dys_ref, dym_ref, yf_ref, yb_ref, z_ref, xs_ref, u_ref, v_ref, dsk_ref, sg_ref,
             gg_ref, gb_ref, ws_ref, bs_ref, dp_any, dzuv_ref, dy_ref, st_ref,
             dws_ref, dbs_ref, dvn_s):
        del dp_any
        dz_ref = dzuv_ref.at[:, OZ:OZ + D]
        du_ref = dzuv_ref.at[:, OU:OU + D]
        dv_ref = dzuv_ref.at[:, OV:OV + D]

        @pl.when(pl.program_id(0) == 0)
        def _():
            st_ref[...] = jnp.zeros_like(st_ref)
            dws_ref[...] = jnp.zeros_like(dws_ref)
            dbs_ref[...] = jnp.zeros_like(dbs_ref)

        z = z_ref[...]
        u = u_ref[...]
        v = v_ref[...]
        y, sz, r, nh, ug, tu, vg, tv, vhat, vrstd, vn = _mix_fwd_vals(
            yf_ref[...], yb_ref[...], z, xs_ref[...], u, v,
            dsk_ref[...], sg_ref[...], gg_ref[...], gb_ref[...])
        dys = dys_ref[...]
        st_ref[0:1, :] += _colsum(dys * nh)
        dn = dys * sg_ref[...]
        dhh = r * (dn - nh * jnp.mean(dn * nh, axis=-1, keepdims=True))
        dy_ref[...] = dhh * z * sz
        dz_ref[...] = (dhh * y * (sz * (1.0 + z * (1.0 - sz)))).astype(dz_ref.dtype)
        dym = dym_ref[...]
        lane = lax.broadcasted_iota(jnp.int32, (Q, 128), 1)
        dbs = jnp.zeros((Q, 128), F32)
        gu = _gelu_grad(u, tu)
        for n in range(TL // Q):
            rs_ = slice(n * Q, (n + 1) * Q)
            for g in range(8):
                cs = slice(g * 128, (g + 1) * 128)
                vb = vn[rs_, cs]
                mixed = _dot(ws_ref[g], vb, NN) + bs_ref[:, g:g + 1]
                dyb = dym[rs_, cs]
                dmx = dyb * ug[rs_, cs]
                du_ref[rs_, cs] = (dyb * mixed * gu[rs_, cs]).astype(du_ref.dtype)
                dvn_s[rs_, cs] = _dot(ws_ref[g], dmx, TN)
                dws_ref[g] += _dot(dmx, vb, NT)
                dbs = dbs + jnp.where(lane == g, jnp.sum(dmx, axis=1, keepdims=True), 0.0)
        dbs_ref[...] += dbs
        dvn = dvn_s[...]
        st_ref[1:2, :] += _colsum(dvn * vhat)
        st_ref[2:3, :] += _colsum(dvn)
        dvg = _ln_bwd(dvn * gg_ref[...], vhat, vrstd)
        dv_ref[...] = (dvg * _gelu_grad(v, tv)).astype(dv_ref.dtype)

    outs = pl.pallas_call(
        body, name="mix_bwd", grid=(nt,),
        in_specs=[_rt(D), _rt(D), _rt(D), _rt(D), _rt(D, OZ // D), _rt(D, 0), _rt(D, OU // D),
                  _rt(D, OV // D), _cst((1, D)), _cst((1, D)), _cst((1, D)), _cst((1, D)),
                  _cst((8, 128, 128)), _cst((128, 128)), pl.BlockSpec(memory_space=pl.ANY)],
        out_specs=[_rt(3 * D, 0), _rt(D), _cst((8, D)),
                   _cst((8, 128, 128)), _cst((128, 128))],
        out_shape=[jax.ShapeDtypeStruct(dp.shape, dp.dtype),
                   jax.ShapeDtypeStruct((L, D), F32), jax.ShapeDtypeStruct((8, D), F32),
                   jax.ShapeDtypeStruct((8, 128, 128), F32), jax.ShapeDtypeStruct((128, 128), F32)],
        scratch_shapes=[pltpu.VMEM((TL, D), F32)],
        input_output_aliases={14: 0},
        compiler_params=_params(("arbitrary",)),
    )(dys, dym, yf, yb, p, xbc, p, p, dsk, sg, gg, gb, ws, bsT, dp)
    return outs


def _gate_fwd(a1, a2, p, bg):
    L = a1.shape[0]

    def body(a1_ref, a2_ref, g_ref, bg_ref, m_ref):
        gt = _sig(g_ref[...] + bg_ref[...])
        m_ref[...] = (gt[:, :D] * a1_ref[...] + gt[:, D:] * a2_ref[...]).astype(m_ref.dtype)

    return pl.pallas_call(
        body, name="gate_fwd", grid=(L // TL,),
        in_specs=[_rt(D), _rt(D), _rt(2 * D, OG // (2 * D)), _cst((1, 2 * D))],
        out_specs=_rt(D), out_shape=jax.ShapeDtypeStruct((L, D), _MXU),
        compiler_params=_params(("parallel",)),
    )(a1, a2, p, bg)


def _gate_bwd(dmg, a1, a2, p, bg, dp):
    L = a1.shape[0]

    def body(dm_ref, a1_ref, a2_ref, g_ref, bg_ref, dp_any, dg_ref, da1_ref, da2_ref, st_ref):
        del dp_any

        @pl.when(pl.program_id(0) == 0)
        def _():
            st_ref[...] = jnp.zeros_like(st_ref)

        gt = _sig(g_ref[...] + bg_ref[...])
        g1 = gt[:, :D]
        g2 = gt[:, D:]
        dm = dm_ref[...]
        da1_ref[...] = (dm * g1).astype(da1_ref.dtype)
        da2_ref[...] = (dm * g2).astype(da2_ref.dtype)
        dg1 = dm * a1_ref[...] * g1 * (1.0 - g1)
        dg2 = dm * a2_ref[...] * g2 * (1.0 - g2)
        st_ref[0:1, 0:D] += _colsum(dg1)
        st_ref[0:1, D:2 * D] += _colsum(dg2)
        dg_ref[:, 0:D] = dg1.astype(dg_ref.dtype)
        dg_ref[:, D:2 * D] = dg2.astype(dg_ref.dtype)

    return pl.pallas_call(
        body, name="gate_bwd", grid=(L // TL,),
        in_specs=[_rt(D), _rt(D), _rt(D), _rt(2 * D, OG // (2 * D)), _cst((1, 2 * D)),
                  pl.BlockSpec(memory_space=pl.ANY)],
        out_specs=[_rt(2 * D, OG // (2 * D)), _rt(D), _rt(D), _cst((8, 2 * D))],
        out_shape=[jax.ShapeDtypeStruct(dp.shape, dp.dtype), jax.ShapeDtypeStruct((L, D), _MXU),
                   jax.ShapeDtypeStruct((L, D), _MXU), jax.ShapeDtypeStruct((8, 2 * D), F32)],
        input_output_aliases={5: 0},
        compiler_params=_params(("arbitrary",)),
    )(dmg, a1, a2, p, bg, dp)


def _res1_fwd(xn, out, modx, g, b):
    L = out.shape[0]

    def body(xn_ref, o_ref, mx_ref, g_ref, b_ref, r1_ref, h2_ref):
        r1 = ALPHA * xn_ref[...] + mx_ref[2:3, :] * o_ref[...]
        xhat, _ = _ln(r1)
        x1 = xhat * g_ref[...] + b_ref[...]
        r1_ref[...] = r1
        h2_ref[...] = (x1 * (1.0 + mx_ref[4:5, :]) + mx_ref[3:4, :]).astype(h2_ref.dtype)

    return pl.pallas_call(
        body, name="res1_fwd", grid=(L // TL,),
        in_specs=[_rt(D), _rt(D), _cst((8, D)), _cst((1, D)), _cst((1, D))],
        out_specs=[_rt(D), _rt(D)],
        out_shape=[jax.ShapeDtypeStruct((L, D), F32), jax.ShapeDtypeStruct((L, D), _MXU)],
        compiler_params=_params(("parallel",)),
    )(xn, out, modx, g, b)


def _glu_fwd(f13):
    L = f13.shape[0]

    def body(f1_ref, f3_ref, o_ref):
        f1 = f1_ref[...]
        o_ref[...] = (f1 * _sig(f1) * f3_ref[...]).astype(o_ref.dtype)

    return pl.pallas_call(
        body, name="glu_fwd", grid=(L // TL,),
        in_specs=[_rt(DFF, 0), _rt(DFF, 1)], out_specs=_rt(DFF),
        out_shape=jax.ShapeDtypeStruct((L, DFF), _MXU),
        compiler_params=_params(("parallel",)),
    )(f13, f13)


def _glu_bwd(dff, f13):
    L = f13.shape[0]

    def body(d_ref, f1_ref, f3_ref, o_ref):
        f1 = f1_ref[...]
        s = _sig(f1)
        d = d_ref[...]
        o_ref[:, 0:DFF] = (d * f3_ref[...] * (s * (1.0 + f1 * (1.0 - s)))).astype(o_ref.dtype)
        o_ref[:, DFF:2 * DFF] = (d * f1 * s).astype(o_ref.dtype)

    return pl.pallas_call(
        body, name="glu_bwd", grid=(L // TL,),
        in_specs=[_rt(DFF), _rt(DFF, 0), _rt(DFF, 1)], out_specs=_rt(2 * DFF),
        out_shape=jax.ShapeDtypeStruct((L, 2 * DFF), _MXU),
        compiler_params=_params(("parallel",)),
    )(dff, f13, f13)


def _res2(r1, o2, tgt, modx, g1, b1, g2, b2):
    L = r1.shape[0]

    def body(r1_ref, o2_ref, t_ref, mx_ref, g1_ref, b1_ref, g2_ref, b2_ref,
             dr2_ref, do2_ref, st_ref, loss_ref):
        @pl.when(pl.program_id(0) == 0)
        def _():
            st_ref[...] = jnp.zeros_like(st_ref)
            loss_ref[...] = jnp.zeros_like(loss_ref)

        xh1, _ = _ln(r1_ref[...])
        x1 = xh1 * g1_ref[...] + b1_ref[...]
        o2 = o2_ref[...]
        g2x = mx_ref[5:6, :]
        xh2, rstd2 = _ln(ALPHA * x1 + g2x * o2)
        err = xh2 * g2_ref[...] + b2_ref[...] - t_ref[...]
        per_tok = jnp.mean(err * err, axis=-1, keepdims=True)
        loss_ref[...] += 0.5 * jnp.sum(per_tok, axis=0, keepdims=True)
        dy = err * (1.0 / D)
        st_ref[0:1, :] += _colsum(dy * xh2)
        st_ref[1:2, :] += _colsum(dy)
        dr2 = _ln_bwd(dy * g2_ref[...], xh2, rstd2)
        st_ref[2:3, :] += _colsum(dr2 * o2)
        dr2_ref[...] = dr2
        do2_ref[...] = (g2x * dr2).astype(do2_ref.dtype)

    return pl.pallas_call(
        body, name="res2", grid=(L // TL,),
        in_specs=[_rt(D), _rt(D), _rt(D), _cst((8, D))] + [_cst((1, D))] * 4,
        out_specs=[_rt(D), _rt(D), _cst((8, D)), _cst((8, 128))],
        out_shape=[jax.ShapeDtypeStruct((L, D), F32), jax.ShapeDtypeStruct((L, D), _MXU),
                   jax.ShapeDtypeStruct((8, D), F32), jax.ShapeDtypeStruct((8, 128), F32)],
        compiler_params=_params(("arbitrary",)),
    )(r1, o2, tgt, modx, g1, b1, g2, b2)


def _res1_bwd(dr2, dh2, r1, out, modx, g1, b1):
    L = r1.shape[0]

    def body(dr2_ref, dh2_ref, r1_ref, o_ref, mx_ref, g_ref, b_ref, dr1_ref, do_ref, st_ref):
        @pl.when(pl.program_id(0) == 0)
        def _():
            st_ref[...] = jnp.zeros_like(st_ref)

        xh1, rstd1 = _ln(r1_ref[...])
        x1 = xh1 * g_ref[...] + b_ref[...]
        dh2 = dh2_ref[...]
        dx1 = ALPHA * dr2_ref[...] + dh2 * (1.0 + mx_ref[4:5, :])
        st_ref[0:1, :] += _colsum(dh2 * x1)
        st_ref[1:2, :] += _colsum(dh2)
        st_ref[2:3, :] += _colsum(dx1 * xh1)
        st_ref[3:4, :] += _colsum(dx1)
        dr1 = _ln_bwd(dx1 * g_ref[...], xh1, rstd1)
        st_ref[4:5, :] += _colsum(dr1 * o_ref[...])
        dr1_ref[...] = dr1
        do_ref[...] = (mx_ref[2:3, :] * dr1).astype(do_ref.dtype)

    return pl.pallas_call(
        body, name="res1_bwd", grid=(L // TL,),
        in_specs=[_rt(D), _rt(D), _rt(D), _rt(D), _cst((8, D)), _cst((1, D)), _cst((1, D))],
        out_specs=[_rt(D), _rt(D), _cst((8, D))],
        out_shape=[jax.ShapeDtypeStruct((L, D), F32), jax.ShapeDtypeStruct((L, D), _MXU),
                   jax.ShapeDtypeStruct((8, D), F32)],
        compiler_params=_params(("arbitrary",)),
    )(dr2, dh2, r1, out, modx, g1, b1)


def _conv_bwd(dxf, dxb, p, conv_w8, conv_b, dp):
    RT = p.shape[0]
    chunks = _seq_chunks(RT - TL)

    def body(df_ref, db_ref, p_ref, w_ref, b_ref, dp_any, o_ref, dw_ref, dbias_ref, dpre_s):
        del dp_any
        w = w_ref[...]
        bias = b_ref[...]
        srow = lax.broadcasted_iota(jnp.int32, (8, 128), 0)
        dwacc = jnp.zeros((8, 128), F32)
        dbacc = jnp.zeros((1, 128), F32)
        for r0, first, last in chunks:
            taps = _conv_taps(p_ref, r0, first, last)
            pre = bias + sum(w[k:k + 1, :] * taps[k] for k in range(5))
            s = _sig(pre)
            dpre = (df_ref[pl.ds(r0, TL), :] + db_ref[pl.ds(r0, TL), :]) * (s * (1.0 + pre * (1.0 - s)))
            dpre_s[pl.ds(r0, TL), :] = dpre
            dbacc = dbacc + _colsum(dpre)
            for k in range(5):
                dwacc = dwacc + jnp.where(srow == k, _colsum(dpre * taps[k]), 0.0)
        for r0, first, last in chunks:
            taps = _conv_taps(dpre_s, r0, first, last)
            dx = sum(w[k:k + 1, :] * taps[4 - k] for k in range(5))
            o_ref[pl.ds(r0, TL), :] = dx.astype(o_ref.dtype)
        dw_ref[...] = dwacc
        dbias_ref[...] = jnp.broadcast_to(dbacc, (8, 128))

    cspec = pl.BlockSpec((RT, 128), lambda j: (0, j))
    wspec = pl.BlockSpec((8, 128), lambda j: (0, j))
    return pl.pallas_call(
        body, name="conv_bwd", grid=(12,),
        in_specs=[cspec, cspec, pl.BlockSpec((RT, 128), lambda j: (0, _xbc_colblk(j))),
                  wspec, pl.BlockSpec((1, 128), lambda j: (0, j)), pl.BlockSpec(memory_space=pl.ANY)],
        out_specs=[pl.BlockSpec((RT, 128), lambda j: (0, _xbc_colblk(j))), wspec, wspec],
        out_shape=[jax.ShapeDtypeStruct(dp.shape, dp.dtype), jax.ShapeDtypeStruct((8, 1536), F32),
                   jax.ShapeDtypeStruct((8, 1536), F32)],
        scratch_shapes=[pltpu.VMEM((RT, 128), F32)],
        input_output_aliases={5: 0},
        compiler_params=_params(("parallel",)),
    )(dxf, dxb, p, conv_w8, conv_b, dp)


def _dt_bwd(ddf, ddb, dp):
    RT = ddf.shape[0]

    def body(f_ref, b_ref, dp_any, o_ref, st_ref):
        del dp_any

        @pl.when(pl.program_id(0) == 0)
        def _():
            st_ref[...] = jnp.zeros_like(st_ref)

        s = f_ref[...] + b_ref[...]
        o_ref[...] = s.astype(o_ref.dtype)
        st_ref[0:1, :] += _colsum(s)

    return pl.pallas_call(
        body, name="dt_bwd", grid=(RT // TL,),
        in_specs=[_rt(128), _rt(128), pl.BlockSpec(memory_space=pl.ANY)],
        out_specs=[_rt(128, ODT // 128), _cst((8, 128))],
        out_shape=[jax.ShapeDtypeStruct(dp.shape, dp.dtype), jax.ShapeDtypeStruct((8, 128), F32)],
        input_output_aliases={2: 0},
        compiler_params=_params(("arbitrary",)),
    )(ddf, ddb, dp)


def _ln0_bwd(dh1, dr1, x, ctx, g, b, modx, modc):
    L = x.shape[0]
    nt = L // TL

    def body(dh_ref, dr1_ref, x_ref, c_ref, g_ref, b_ref, mx_ref, mc_ref, gx_ref, st_ref):
        i = pl.program_id(0)
        isc = i == nt

        @pl.when(i == 0)
        def _():
            st_ref[...] = jnp.zeros_like(st_ref)

        xin = jnp.where(isc, c_ref[...], x_ref[...])
        xhat, rstd = _ln(xin)
        xn = xhat * g_ref[...] + b_ref[...]
        sc = jnp.where(isc, mc_ref[1:2, :], mx_ref[1:2, :])
        dh = dh_ref[...]
        lat = jnp.where(isc, 0.0, 1.0)
        dxn = dh * (1.0 + sc) + (lat * ALPHA) * dr1_ref[...]
        tsh = _colsum(dh)
        tsc = _colsum(dh * xn)
        st_ref[0:1, :] += lat * tsh
        st_ref[1:2, :] += lat * tsc
        st_ref[2:3, :] += (1.0 - lat) * tsh
        st_ref[3:4, :] += (1.0 - lat) * tsc
        st_ref[4:5, :] += _colsum(dxn * xhat)
        st_ref[5:6, :] += _colsum(dxn)

        @pl.when(i < nt)
        def _():
            gx_ref[...] = _ln_bwd(dxn * g_ref[...], xhat, rstd)

    return pl.pallas_call(
        body, name="ln0_bwd", grid=(nt + 1,),
        in_specs=[_rt(D), _rtc(D, nt), _rtc(D, nt), _cst((TL, D)), _cst((1, D)), _cst((1, D)),
                  _cst((8, D)), _cst((8, D))],
        out_specs=[_rtc(D, nt), _cst((8, D))],
        out_shape=[jax.ShapeDtypeStruct((L, D), F32), jax.ShapeDtypeStruct((8, D), F32)],
        compiler_params=_params(("arbitrary",)),
    )(dh1, dr1, x, ctx, g, b, modx, modc)


def _perm_cols(w):
    pad = jnp.zeros((w.shape[0], NPJ - NNAT), w.dtype)
    return jnp.concatenate([w[:, 0:1024], w[:, 2592:3616], w[:, 3616:4640], w[:, 1024:2048],
                            w[:, 4640:6688], w[:, 2048:2304], w[:, 2304:2560], w[:, 2560:2592], pad],
                           axis=1)


SECTIONS = ((0, 1024, OZ), (1024, 2048, OXS), (2048, 2304, OB), (2304, 2560, OC), (2560, 2592, ODT),
            (2592, 3616, OU), (3616, 4640, OV), (4640, 6688, OG))


def _perm_from_blocks(ga):
    n = ga.shape[2]
    pieces = []
    for na, nb, _ in sorted(SECTIONS, key=lambda sec: sec[2]):
        for k in range(NDEV):
            lo, hi = max(na, k * n), min(nb, (k + 1) * n)
            if lo < hi:
                pieces.append(ga[k][:, lo - k * n:hi - k * n])
    pieces.append(jnp.zeros((ga.shape[1], NPJ - NNAT), ga.dtype))
    return jnp.concatenate(pieces, axis=1)


def _blocks_from_perm(gp, n):
    blocks = []
    for k in range(NDEV):
        pieces = []
        for na, nb, po in SECTIONS:
            lo, hi = max(na, k * n), min(nb, (k + 1) * n)
            if lo < hi:
                pieces.append(gp[:, po + lo - na:po + hi - na])
        blocks.append(jnp.concatenate(pieces, axis=1))
    return jnp.stack(blocks)


def _padded(n, row_align):
    unit = row_align * D
    return -(-n // unit) * unit if row_align else n


def _slab(arrs, rows, row_align=0):
    parts = []
    for a in arrs:
        f = a.reshape(-1)
        parts.append(jnp.pad(f, (0, _padded(f.shape[0], row_align) - f.shape[0])))
    flat = jnp.concatenate(parts)
    flat = jnp.pad(flat, (0, rows * D - flat.shape[0]))
    return flat.reshape(rows, D)


def _unslab(slab, shapes, row_align=0):
    flat = slab.reshape(-1)
    out, off = [], 0
    for shp in shapes:
        n = 1
        for s in shp:
            n *= s
        out.append(flat[off:off + n].reshape(shp))
        off += _padded(n, row_align)
    return out


def _row(v):
    return v.reshape(1, -1)


def _pad_rows(a, rows):
    return jnp.pad(a, ((0, rows - a.shape[0]), (0, 0)))


BIG = ["w_in", "w_ssd_proj", "w_gm_proj", "w_out", "w_ff1", "w_ff3", "w_ff2"]
BIG_ROWS = 2304
BIG_ALIGN = 16
REPL = ["c_ctx", "ln0_g", "ln0_b", "b_ada", "conv_b", "dt_bias", "a_log", "d_skip", "ssd_norm_g",
        "gm_norm_g", "gm_norm_b", "w_spatial", "b_spatial", "b_gate", "ln1_g", "ln1_b", "ln2_g", "ln2_b"]
SMALL_ROWS = 160
WEIGHTS = ["c_ctx", "ln0_g", "ln0_b", "w_ada", "b_ada", "w_in", "conv_w", "conv_b", "dt_bias", "a_log",
           "d_skip", "ssd_norm_g", "gm_norm_g", "gm_norm_b", "w_spatial", "b_spatial", "b_gate",
           "w_ssd_proj", "w_gm_proj", "w_out", "ln1_g", "ln1_b", "w_ff1", "w_ff3", "w_ff2", "ln2_g", "ln2_b"]


def kernel(x, c, ctx, c_ctx, ln0_g, ln0_b, w_ada, b_ada, w_in, conv_w, conv_b, dt_bias, a_log, d_skip, ssd_norm_g, gm_norm_g, gm_norm_b, w_spatial, b_spatial, b_gate, w_ssd_proj, w_gm_proj, w_out, ln1_g, ln1_b, w_ff1, w_ff3, w_ff2, ln2_g, ln2_b, loss_target, m_c_ctx, m_ln0_g, m_ln0_b, m_w_ada, m_b_ada, m_w_in, m_conv_w, m_conv_b, m_dt_bias, m_a_log, m_d_skip, m_ssd_norm_g, m_gm_norm_g, m_gm_norm_b, m_w_spatial, m_b_spatial, m_b_gate, m_w_ssd_proj, m_w_gm_proj, m_w_out, m_ln1_g, m_ln1_b, m_w_ff1, m_w_ff3, m_w_ff2, m_ln2_g, m_ln2_b, v_c_ctx, v_ln0_g, v_ln0_b, v_w_ada, v_b_ada, v_w_in, v_conv_w, v_conv_b, v_dt_bias, v_a_log, v_d_skip, v_ssd_norm_g, v_gm_norm_g, v_gm_norm_b, v_w_spatial, v_b_spatial, v_b_gate, v_w_ssd_proj, v_w_gm_proj, v_w_out, v_ln1_g, v_ln1_b, v_w_ff1, v_w_ff3, v_w_ff2, v_ln2_g, v_ln2_b):
    W = dict(c_ctx=c_ctx, ln0_g=ln0_g, ln0_b=ln0_b, w_ada=w_ada, b_ada=b_ada, w_in=w_in, conv_w=conv_w,
             conv_b=conv_b, dt_bias=dt_bias, a_log=a_log, d_skip=d_skip, ssd_norm_g=ssd_norm_g,
             gm_norm_g=gm_norm_g, gm_norm_b=gm_norm_b, w_spatial=w_spatial, b_spatial=b_spatial,
             b_gate=b_gate, w_ssd_proj=w_ssd_proj, w_gm_proj=w_gm_proj, w_out=w_out, ln1_g=ln1_g,
             ln1_b=ln1_b, w_ff1=w_ff1, w_ff3=w_ff3, w_ff2=w_ff2, ln2_g=ln2_g, ln2_b=ln2_b)
    M = dict(c_ctx=m_c_ctx, ln0_g=m_ln0_g, ln0_b=m_ln0_b, w_ada=m_w_ada, b_ada=m_b_ada, w_in=m_w_in,
             conv_w=m_conv_w, conv_b=m_conv_b, dt_bias=m_dt_bias, a_log=m_a_log, d_skip=m_d_skip,
             ssd_norm_g=m_ssd_norm_g, gm_norm_g=m_gm_norm_g, gm_norm_b=m_gm_norm_b,
             w_spatial=m_w_spatial, b_spatial=m_b_spatial, b_gate=m_b_gate, w_ssd_proj=m_w_ssd_proj,
             w_gm_proj=m_w_gm_proj, w_out=m_w_out, ln1_g=m_ln1_g, ln1_b=m_ln1_b, w_ff1=m_w_ff1,
             w_ff3=m_w_ff3, w_ff2=m_w_ff2, ln2_g=m_ln2_g, ln2_b=m_ln2_b)
    V = dict(c_ctx=v_c_ctx, ln0_g=v_ln0_g, ln0_b=v_ln0_b, w_ada=v_w_ada, b_ada=v_b_ada, w_in=v_w_in,
             conv_w=v_conv_w, conv_b=v_conv_b, dt_bias=v_dt_bias, a_log=v_a_log, d_skip=v_d_skip,
             ssd_norm_g=v_ssd_norm_g, gm_norm_g=v_gm_norm_g, gm_norm_b=v_gm_norm_b,
             w_spatial=v_w_spatial, b_spatial=v_b_spatial, b_gate=v_b_gate, w_ssd_proj=v_w_ssd_proj,
             w_gm_proj=v_w_gm_proj, w_out=v_w_out, ln1_g=v_ln1_g, ln1_b=v_ln1_b, w_ff1=v_w_ff1,
             w_ff3=v_w_ff3, w_ff2=v_w_ff2, ln2_g=v_ln2_g, ln2_b=v_ln2_b)

    me = 4 * lax.axis_index("x") + 2 * lax.axis_index("y") + lax.axis_index("c")
    xl, cx, tgt = x[0], ctx[0], loss_target[0]
    L = xl.shape[0]
    assert cx.shape[0] == TL and L % TL == 0
    ada_n = w_ada.shape[2]
    cw_n = conv_w.shape[2]

    small1 = _pad_rows(jnp.concatenate([c, _slab([conv_w[0]], 1)], axis=0), 8)
    g1 = _all_gather(small1, "ag_small")
    c_all = g1[:, 0, :]
    conv_w_full = g1[:, 1, :5 * cw_n].reshape(NDEV, 5, cw_n).transpose(1, 0, 2).reshape(5, NDEV * cw_n)
    sq = w_ssd_proj.shape[1]
    ffr = w_ff2.shape[1]
    ffc = w_ff1.shape[2]
    late = [jnp.concatenate([w_ssd_proj[0], w_gm_proj[0], w_out[0], w_ff2[0]], axis=0).astype(_MXU),
            w_ff1[0].astype(_MXU), w_ff3[0].astype(_MXU)]
    ga, = _all_gather_multi([w_in[0].astype(_MXU)], "ag_w_in")
    ga, late = lax.optimization_barrier((ga, late))
    lw_send, lw_recv, lw_src, lw_land, lw_token = _exchange_start(late, "ag_late_start", gather=True)
    w_in_p = _perm_from_blocks(ga)

    c16 = _pad_rows(jnp.concatenate([c_all, _row(c_ctx)], axis=0), 16) + lw_token[0, 0]
    b_ada_sh = lax.dynamic_slice(b_ada, (0, ada_n * me), (1, ada_n))
    modp = _ada_fwd(c16, w_ada[0], b_ada_sh)
    mod16 = _all_gather(modp, "ag_mod").transpose(1, 0, 2).reshape(16, NDEV * ada_n)
    modx = _pad_rows(lax.dynamic_slice(mod16, (me, 0), (1, 6 * D)).reshape(6, D), 8)
    modc = _pad_rows(mod16[8].reshape(6, D), 8)

    g0, b0 = _row(ln0_g), _row(ln0_b)
    xn, h1 = _ln0_fwd(xl, cx, g0, b0, modx, modc)
    p = _mm(h1, w_in_p, "nn", F32, "mm_p")
    conv_w8 = _pad_rows(conv_w_full, 8)
    xbc = _conv_fwd(p, conv_w8, conv_b)
    prm = _pad_rows(jnp.pad(jnp.stack([dt_bias.reshape(32), a_log.reshape(32)]), ((0, 0), (0, 96))), 8)
    yf, yb, hpf, hpb = _ssd2_fwd(xbc, p, prm)
    lw_land = _exchange_wait(lw_send, lw_recv, lw_src, lw_land, yf, "ag_late_wait", gather=True)
    fw_send, fw_recv, lw_land, fw_token = _forward_start(lw_land, "ag_fwd_start")
    dsk = _row(jnp.repeat(d_skip[0, 0] + d_skip[0, 1], HP)) + fw_token[0:1, 0:1]
    ws_m = w_spatial[0].astype(_MXU)
    bsT = jnp.pad(b_spatial[0].T, ((0, 0), (0, 120)))
    mixp = (dsk, ssd_norm_g, gm_norm_g, gm_norm_b, ws_m, bsT)
    yssd, ygm = _mix_fwd(yf, yb, p, xbc, *mixp)
    gb, gc1, gc2 = _forward_wait(fw_send, fw_recv, lw_land, yssd, "ag_fwd_wait")

    def with_own(g, mine, k):
        return jnp.where(me == k, mine, g[k])

    gb = jnp.stack([with_own(gb, late[0], k) for k in range(NDEV)])
    w_ssd_f = gb[:, 0:sq].reshape(NDEV * sq, D)
    w_gm_f = gb[:, sq:2 * sq].reshape(NDEV * sq, D)
    w_out_f = gb[:, 2 * sq:3 * sq].reshape(NDEV * sq, D)
    w_ff2_f = gb[:, 3 * sq:3 * sq + ffr].reshape(NDEV * ffr, D)
    w13 = jnp.concatenate([with_own(gc1, late[1], k) for k in range(NDEV)]
                          + [with_own(gc2, late[2], k) for k in range(NDEV)], axis=1)
    a1 = _mm(yssd, w_ssd_f, "nn", F32, "mm_a1")
    a2 = _mm(ygm, w_gm_f, "nn", F32, "mm_a2")
    merged = _gate_fwd(a1, a2, p, b_gate)
    out = _mm(merged, w_out_f, "nn", F32, "mm_out")
    r1, h2 = _res1_fwd(xn, out, modx, ln1_g, ln1_b)
    f13 = _mm(h2, w13, "nn", F32, "mm_f13")
    ff = _glu_fwd(f13)
    o2 = _mm(ff, w_ff2_f, "nn", F32, "mm_o2")

    dr2, do2, st2, loss_slab = _res2(r1, o2, tgt, modx, ln1_g, ln1_b, ln2_g, ln2_b)
    loss = lax.psum(loss_slab[0, 0], ("x", "y", "c"))
    dff = _mm(do2, w_ff2_f, "nt", F32, "mm_dff")
    df13 = _glu_bwd(dff, f13)
    dh2 = _mm(df13, w13, "nt", F32, "mm_dh2")
    dw_ff2 = _mm(ff, do2, "tn", _MXU, "mm_dw_ff2")
    dw13 = _mm(h2, df13, "tn", _MXU, "mm_dw13")
    xff = [dw_ff2.reshape(NDEV, ffr, D),
           jnp.stack([dw13[:, k * ffc:(k + 1) * ffc] for k in range(NDEV)]),
           jnp.stack([dw13[:, DFF + k * ffc:DFF + (k + 1) * ffc] for k in range(NDEV)])]
    ff_send, ff_recv, ff_src, ff_land, ff_token = _exchange_start(xff, "xchg_ff_start")
    modx = modx + ff_token[0, 0]
    dr1, dout, st1 = _res1_bwd(dr2, dh2, r1, out, modx, ln1_g, ln1_b)
    dmg = _mm(dout, w_out_f, "nt", F32, "mm_dmerged")
    dw_out = _mm(merged, dout, "tn", _MXU, "mm_dw_out")
    dp = jnp.zeros((L + TL, NPJ), _MXU)
    dp, da1, da2, stg = _gate_bwd(dmg, a1, a2, p, b_gate, dp)
    dys = _mm(da1, w_ssd_f, "nt", F32, "mm_dyssd")
    dym = _mm(da2, w_gm_f, "nt", F32, "mm_dygm")
    dw_ssd = _mm(yssd, da1, "tn", _MXU, "mm_dw_ssd")
    dw_gm = _mm(ygm, da2, "tn", _MXU, "mm_dw_gm")
    xsq = [jnp.concatenate([dw_ssd.reshape(NDEV, sq, D), dw_gm.reshape(NDEV, sq, D),
                            dw_out.reshape(NDEV, sq, D)], axis=1)]
    sq_send, sq_recv, sq_src, sq_land, sq_token = _exchange_start(xsq, "xchg_sq_start")
    mixp = (dsk + sq_token[0:1, 0:1],) + mixp[1:]
    dp, dyd, stm, dws, dbsT = _mix_bwd(dys, dym, yf, yb, p, xbc, dp, *mixp)
    dxf, dxb, ddf, ddb, sts = _ssd2_bwd(xbc, p, prm, dsk, dyd, hpf, hpb)
    dp, dcw, dcb = _conv_bwd(dxf, dxb, p, conv_w8, conv_b, dp)
    dp, std = _dt_bwd(ddf, ddb, dp)
    dh1 = _mm(dp, w_in_p, "nt", F32, "mm_dh1")
    dw_in_p = _mm(h1, dp, "tn", _MXU, "mm_dw_in")
    xin = [_blocks_from_perm(dw_in_p, w_in.shape[2])]
    in_send, in_recv, in_src, in_land, in_token = _exchange_start(xin, "xchg_in_start")
    modx = modx + in_token[0, 0]
    grad_x, st0 = _ln0_bwd(dh1, dr1, xl, cx, g0, b0, modx, modc)

    zero = jnp.zeros((D,), F32)
    dmod = jnp.stack([jnp.concatenate([st0[0], st0[1], st1[4], st1[1], st1[0], st2[2]]),
                      jnp.concatenate([st0[2], st0[3], zero, zero, zero, zero])])
    g16 = _all_gather(_pad_rows(dmod, 8), "ag_dmod")[:, 0:2, :].reshape(16, 6 * D)
    g16_sh = lax.dynamic_slice(g16, (0, ada_n * me), (16, ada_n))
    c16b = jnp.stack([c_all, jnp.broadcast_to(_row(c_ctx), (NDEV, D))], axis=1).reshape(16, D)
    dw_ada, db_ada8, dcc8 = _ada_bwd(c16b, g16, g16_sh, w_ada[0])

    part = dict(
        c_ctx=dcc8[0], ln0_g=st0[4], ln0_b=st0[5], conv_w=dcw[0:5], conv_b=dcb[0],
        dt_bias=std[0, 0:32], a_log=sts[0, 0:32], d_skip=jnp.tile(sts[1, 0:16], 2),
        ssd_norm_g=stm[0], gm_norm_g=stm[1], gm_norm_b=stm[2], w_spatial=dws,
        b_spatial=dbsT[:, 0:8].T, b_gate=stg[0], ln1_g=st1[2], ln1_b=st1[3], ln2_g=st2[0], ln2_b=st2[1])
    pnames = list(part)
    psum8 = _sum8(_all_gather(_slab([part[n] for n in pnames], SMALL_ROWS), "ag_smallgrads"), "sum_smallgrads")
    small = dict(zip(pnames, _unslab(psum8, [part[n].shape for n in pnames])))
    grads = {n: small[n].reshape(W[n].shape) for n in pnames if n != "conv_w"}
    grads["conv_w"] = lax.dynamic_slice(small["conv_w"], (0, cw_n * me), (5, cw_n)).reshape(conv_w.shape)
    grads["b_ada"] = db_ada8[0:1]
    grads["w_ada"] = dw_ada.reshape(w_ada.shape)

    delta, new_m, new_v = {}, {}, {}

    def adam_group(names, rows, tag, align=0):
        shapes = [W[n].shape for n in names]
        outs = _adamw(*[_slab([src[n] for n in names], rows, align) for src in (grads, W, M, V)], tag)
        for res, slab in zip((delta, new_m, new_v), outs):
            for n, a in zip(names, _unslab(slab, shapes, align)):
                res[n] = a

    adam_group(["w_ada", "conv_w"], ada_n + 256, "adamw_shard")
    adam_group(REPL, SMALL_ROWS, "adamw_repl")

    rff = _exchange_wait(ff_send, ff_recv, ff_src, ff_land, st0, "xchg_ff_wait")
    rsq = _exchange_wait(sq_send, sq_recv, sq_src, sq_land, rff[0], "xchg_sq_wait")
    rin = _exchange_wait(in_send, in_recv, in_src, in_land, delta["ln2_b"], "xchg_in_wait")

    def own(blocks):
        return lax.dynamic_index_in_dim(blocks, me, 0, keepdims=False)

    for n, r8, mine, row0, tr in (
            ("w_ff2", rff[0], own(xff[0]), 0, ffr // 2), ("w_ff1", rff[1], own(xff[1]), 0, 256),
            ("w_ff3", rff[2], own(xff[2]), 0, 256), ("w_ssd_proj", rsq[0], own(xsq[0]), 0, sq),
            ("w_gm_proj", rsq[0], own(xsq[0]), sq, sq), ("w_out", rsq[0], own(xsq[0]), 2 * sq, sq),
            ("w_in", rin[0], own(xin[0]), 0, 256)):
        res = _adamw_sum(r8, mine, W[n][0], M[n][0], V[n][0], row0, tr, "adamw_" + n)
        grads[n], delta[n], new_m[n], new_v[n] = [a[None] for a in res]

    return (loss, grad_x[None], *[grads[n] for n in WEIGHTS], *[delta[n] for n in WEIGHTS],
            *[new_m[n] for n in WEIGHTS], *[new_v[n] for n in WEIGHTS])
```

```python
import functools

import jax
import jax.numpy as jnp
from jax import lax
from jax.experimental import pallas as pl
from jax.experimental.pallas import tpu as pltpu

_MXU = jnp.bfloat16
F32 = jnp.float32
D = 1024
TL = 256
Q = 128
NH, HP, NS, HPG = 16, 64, 128, 8
DFF = 2816
ALPHA = 2.0 ** 0.25
EPS = 1e-5
OZ, OU, OV, OXS, OG, OB, OC, ODT, NPJ = 0, 1024, 2048, 3072, 4096, 6144, 6400, 6656, 6912
NNAT = 6688
NDEV = 8
ADAM_LR, ADAM_B1, ADAM_B2, ADAM_EPS, ADAM_WD, ADAM_STEP = 1e-3, 0.9, 0.999, 1e-8, 0.01, 10
VMEM_LIMIT = 48 * 1024 * 1024

NN = ((1,), (0,))
NT = ((1,), (1,))
TN = ((0,), (0,))
MESH = pl.DeviceIdType.MESH


def _dot(a, b, dims):
    return lax.dot_general(a.astype(_MXU), b.astype(_MXU), (dims, ((), ())),
                           preferred_element_type=F32)


def _tile(n, cands):
    for c in cands:
        if n % c == 0:
            return c
    return n


def _divisor_tile(n, cap, mult):
    best = n
    for t in range(mult, min(n, cap) + 1, mult):
        if n % t == 0:
            best = t
    return best


def _params(sem):
    return pltpu.CompilerParams(dimension_semantics=sem, vmem_limit_bytes=VMEM_LIMIT)


def _cst(shape):
    nd = len(shape)
    return pl.BlockSpec(shape, lambda *_: (0,) * nd)


def _rt(w, cb=0, rows=TL):
    return pl.BlockSpec((rows, w), lambda i: (i, cb))


def _rtc(w, nt, cb=0):
    return pl.BlockSpec((TL, w), lambda i: (jnp.minimum(i, nt - 1), cb))


def _sig(x):
    return jax.nn.sigmoid(x)


def _softplus(x):
    return jnp.maximum(x, 0.0) + jnp.log1p(jnp.exp(-jnp.abs(x)))


_G0, _G1 = 0.7978845608028654, 0.044715


def _gelu(x):
    t = jnp.tanh(_G0 * (x + _G1 * x * x * x))
    return 0.5 * x * (1.0 + t), t


def _gelu_grad(x, t):
    return 0.5 * (1.0 + t) + 0.5 * x * (1.0 - t * t) * _G0 * (1.0 + 3.0 * _G1 * x * x)


def _ln(r):
    mu = jnp.mean(r, axis=-1, keepdims=True)
    xc = r - mu
    var = jnp.mean(xc * xc, axis=-1, keepdims=True)
    rstd = lax.rsqrt(var + EPS)
    return xc * rstd, rstd


def _ln_bwd(dyh, xhat, rstd):
    return rstd * (dyh - jnp.mean(dyh, axis=-1, keepdims=True)
                   - xhat * jnp.mean(dyh * xhat, axis=-1, keepdims=True))


def _colsum(v):
    return jnp.sum(v, axis=0, keepdims=True)


def _sum11(v):
    return jnp.sum(jnp.sum(v, axis=1, keepdims=True), axis=0, keepdims=True)


def _cumsum_rows(a, rev):
    n = a.shape[0]
    row = lax.broadcasted_iota(jnp.int32, a.shape, 0)
    s = 1
    while s < n:
        if rev:
            a = a + jnp.where(row < n - s, pltpu.roll(a, n - s, 0), 0.0)
        else:
            a = a + jnp.where(row >= s, pltpu.roll(a, s, 0), 0.0)
        s *= 2
    return a


def _mm(a, b, mode, out_dtype, name):
    if mode == "tn":
        K, M = a.shape
    else:
        M, K = a.shape
    N = b.shape[0] if mode == "nt" else b.shape[1]
    tm = _divisor_tile(M, 1408, 128) if mode == "tn" else _divisor_tile(M, 1088, 16)
    tn = _divisor_tile(N, 1408, 128)
    tk = _divisor_tile(K, 2304, 128)
    nk = K // tk
    dims = {"nn": NN, "nt": NT, "tn": TN}[mode]
    use_acc = nk > 1 and out_dtype != F32

    def body(a_ref, b_ref, o_ref, *acc):
        prod = _dot(a_ref[...], b_ref[...], dims)
        if nk == 1:
            o_ref[...] = prod.astype(o_ref.dtype)
            return
        acc_ref = acc[0] if use_acc else o_ref
        k = pl.program_id(2)

        @pl.when(k == 0)
        def _():
            acc_ref[...] = prod

        if use_acc:
            @pl.when((k > 0) & (k < nk - 1))
            def _():
                acc_ref[...] += prod

            @pl.when(k == nk - 1)
            def _():
                o_ref[...] = (acc_ref[...] + prod).astype(o_ref.dtype)
        else:
            @pl.when(k > 0)
            def _():
                o_ref[...] += prod

    if mode == "tn":
        a_spec = pl.BlockSpec((tk, tm), lambda i, j, k: (k, i))
    else:
        a_spec = pl.BlockSpec((tm, tk), lambda i, j, k: (i, k))
    if mode == "nt":
        b_spec = pl.BlockSpec((tn, tk), lambda i, j, k: (j, k))
    else:
        b_spec = pl.BlockSpec((tk, tn), lambda i, j, k: (k, j))
    return pl.pallas_call(
        body, name=name, grid=(M // tm, N // tn, nk),
        in_specs=[a_spec, b_spec],
        out_specs=pl.BlockSpec((tm, tn), lambda i, j, k: (i, j)),
        out_shape=jax.ShapeDtypeStruct((M, N), out_dtype),
        scratch_shapes=[pltpu.VMEM((tm, tn), F32)] if use_acc else [],
        compiler_params=_params(("parallel", "parallel", "arbitrary")),
    )(a, b)


def _all_gather(x, name):
    def body(x_ref, out_ref, send_sems, recv_sems, local_sem):
        mx, my, mc = lax.axis_index("x"), lax.axis_index("y"), lax.axis_index("c")
        me, sibling = (mx, my, mc), (mx, my, 1 - mc)
        chips = [(1 - mx, my), (mx, 1 - my), (1 - mx, 1 - my)]

        def slot(px, py, pc):
            return out_ref.at[4 * px + 2 * py + pc]

        def copy(k, block, to, src=None):
            return pltpu.make_async_remote_copy(
                src_ref=slot(*block) if src is None else src, dst_ref=slot(*block),
                send_sem=send_sems.at[k], recv_sem=recv_sems.at[k],
                device_id=to, device_id_type=MESH)

        mine = pltpu.make_async_copy(x_ref, slot(*me), local_sem)
        mine.start()
        first = [copy(0, me, sibling, src=x_ref)]
        first += [copy(1 + j, me, (*chip, mc), src=x_ref) for j, chip in enumerate(chips)]
        for cp in first:
            cp.start()
        passed = [copy(4 + j, (*chip, mc), sibling) for j, chip in enumerate(chips)]
        for j, chip in enumerate(chips):
            copy(1 + j, (*chip, mc), me).wait_recv()
            passed[j].start()
        copy(0, sibling, me).wait_recv()
        for j, chip in enumerate(chips):
            copy(4 + j, (*chip, 1 - mc), me).wait_recv()
        for cp in first + passed:
            cp.wait_send()
        mine.wait()

    return pl.pallas_call(
        body, name=name,
        out_shape=jax.ShapeDtypeStruct((NDEV,) + x.shape, x.dtype),
        in_specs=[pl.BlockSpec(memory_space=pl.ANY)],
        out_specs=pl.BlockSpec(memory_space=pl.ANY),
        scratch_shapes=[pltpu.SemaphoreType.DMA((7,)), pltpu.SemaphoreType.DMA((7,)),
                        pltpu.SemaphoreType.DMA],
    )(x)


def _owner_exchange(g, name):
    def body(g_ref, out_ref, send_sems, recv_sems, local_sem):
        mx, my, mc = lax.axis_index("x"), lax.axis_index("y"), lax.axis_index("c")
        local = pltpu.make_async_copy(g_ref.at[4 * mx + 2 * my + mc], out_ref.at[0], local_sem)
        local.start()
        copies = []
        for f in range(1, NDEV):
            px = 1 - mx if (f >> 2) & 1 else mx
            py = 1 - my if (f >> 1) & 1 else my
            pc = 1 - mc if f & 1 else mc
            cp = pltpu.make_async_remote_copy(
                src_ref=g_ref.at[4 * px + 2 * py + pc], dst_ref=out_ref.at[f],
                send_sem=send_sems.at[f - 1], recv_sem=recv_sems.at[f - 1],
                device_id=(px, py, pc), device_id_type=MESH)
            cp.start()
            copies.append(cp)
        for cp in copies:
            cp.wait_recv()
        for cp in copies:
            cp.wait_send()
        local.wait()

    return pl.pallas_call(
        body, name=name,
        out_shape=jax.ShapeDtypeStruct(g.shape, g.dtype),
        in_specs=[pl.BlockSpec(memory_space=pl.ANY)],
        out_specs=pl.BlockSpec(memory_space=pl.ANY),
        scratch_shapes=[pltpu.SemaphoreType.DMA((7,)), pltpu.SemaphoreType.DMA((7,)),
                        pltpu.SemaphoreType.DMA],
    )(g)


def _any_specs(n):
    return [pl.BlockSpec(memory_space=pl.ANY)] * n


def _all_gather_multi(xs, name):
    na = len(xs)

    def body(*refs):
        x_refs, out_refs = refs[:na], refs[na:2 * na]
        send_sems, recv_sems, local_sems = refs[2 * na:]
        mx, my, mc = lax.axis_index("x"), lax.axis_index("y"), lax.axis_index("c")
        me, sibling = (mx, my, mc), (mx, my, 1 - mc)
        chips = [(1 - mx, my), (mx, 1 - my), (1 - mx, 1 - my)]

        def copy(a, k, block, to, src=None):
            slot = out_refs[a].at[4 * block[0] + 2 * block[1] + block[2]]
            return pltpu.make_async_remote_copy(
                src_ref=slot if src is None else src, dst_ref=slot,
                send_sem=send_sems.at[7 * a + k], recv_sem=recv_sems.at[7 * a + k],
                device_id=to, device_id_type=MESH)

        mine = [pltpu.make_async_copy(x_refs[a], out_refs[a].at[4 * mx + 2 * my + mc], local_sems.at[a])
                for a in range(na)]
        for cp in mine:
            cp.start()
        first = []
        for a in range(na):
            first.append(copy(a, 0, me, sibling, src=x_refs[a]))
            first += [copy(a, 1 + j, me, (*chip, mc), src=x_refs[a]) for j, chip in enumerate(chips)]
        for cp in first:
            cp.start()
        passed = []
        for a in range(na):
            for j, chip in enumerate(chips):
                copy(a, 1 + j, (*chip, mc), me).wait_recv()
                fwd = copy(a, 4 + j, (*chip, mc), sibling)
                fwd.start()
                passed.append(fwd)
        for a in range(na):
            copy(a, 0, sibling, me).wait_recv()
            for j, chip in enumerate(chips):
                copy(a, 4 + j, (*chip, 1 - mc), me).wait_recv()
        for cp in first + passed:
            cp.wait_send()
        for cp in mine:
            cp.wait()

    return pl.pallas_call(
        body, name=name,
        out_shape=[jax.ShapeDtypeStruct((NDEV,) + x.shape, x.dtype) for x in xs],
        in_specs=_any_specs(na), out_specs=_any_specs(na),
        scratch_shapes=[pltpu.SemaphoreType.DMA((7 * na,)), pltpu.SemaphoreType.DMA((7 * na,)),
                        pltpu.SemaphoreType.DMA((na,))],
    )(*xs)


def _owner_exchange_multi(gs, name):
    na = len(gs)

    def body(*refs):
        g_refs, out_refs = refs[:na], refs[na:2 * na]
        send_sems, recv_sems, local_sems = refs[2 * na:]
        mx, my, mc = lax.axis_index("x"), lax.axis_index("y"), lax.axis_index("c")
        locals_ = [pltpu.make_async_copy(g_refs[a].at[4 * mx + 2 * my + mc], out_refs[a].at[0], local_sems.at[a])
                   for a in range(na)]
        for cp in locals_:
            cp.start()
        copies = []
        for a in range(na):
            for f in range(1, NDEV):
                px = 1 - mx if (f >> 2) & 1 else mx
                py = 1 - my if (f >> 1) & 1 else my
                pc = 1 - mc if f & 1 else mc
                cp = pltpu.make_async_remote_copy(
                    src_ref=g_refs[a].at[4 * px + 2 * py + pc], dst_ref=out_refs[a].at[f],
                    send_sem=send_sems.at[7 * a + f - 1], recv_sem=recv_sems.at[7 * a + f - 1],
                    device_id=(px, py, pc), device_id_type=MESH)
                cp.start()
                copies.append(cp)
        for cp in copies:
            cp.wait_recv()
        for cp in copies:
            cp.wait_send()
        for cp in locals_:
            cp.wait()

    return pl.pallas_call(
        body, name=name,
        out_shape=[jax.ShapeDtypeStruct(g.shape, g.dtype) for g in gs],
        in_specs=_any_specs(na), out_specs=_any_specs(na),
        scratch_shapes=[pltpu.SemaphoreType.DMA((7 * na,)), pltpu.SemaphoreType.DMA((7 * na,)),
                        pltpu.SemaphoreType.DMA((na,))],
    )(*gs)


def _adamw_sum(r8, own, w, m, v, row0, tr, name):
    R, C = w.shape
    assert row0 % tr == 0
    blk0 = row0 // tr
    bc1 = 1.0 - ADAM_B1 ** ADAM_STEP
    bc2 = 1.0 - ADAM_B2 ** ADAM_STEP

    def body(r_ref, *refs):
        if own is None:
            gg = r_ref[0].astype(F32)
        else:
            gg = refs[0][...].astype(F32)
            refs = refs[1:]
        w_ref, m_ref, v_ref, g_ref, d_ref, mo_ref, vo_ref = refs
        for k in range(1, NDEV):
            gg = gg + r_ref[k].astype(F32)
        mn = ADAM_B1 * m_ref[...] + (1.0 - ADAM_B1) * gg
        vn = ADAM_B2 * v_ref[...] + (1.0 - ADAM_B2) * (gg * gg)
        mh = mn / bc1
        vh = vn / bc2
        g_ref[...] = gg
        d_ref[...] = -ADAM_LR * (mh / (jnp.sqrt(vh) + ADAM_EPS) + ADAM_WD * w_ref[...])
        mo_ref[...] = mn
        vo_ref[...] = vn

    spec = pl.BlockSpec((tr, C), lambda i: (i, 0))
    sh = jax.ShapeDtypeStruct((R, C), F32)
    own_ops = [] if own is None else [own]
    own_specs = [] if own is None else [pl.BlockSpec((tr, C), lambda i: (i + blk0, 0))]
    return pl.pallas_call(
        body, name=name, grid=(R // tr,),
        in_specs=[pl.BlockSpec((NDEV, tr, C), lambda i: (0, i + blk0, 0))] + own_specs + [spec, spec, spec],
        out_specs=[spec] * 4, out_shape=[sh] * 4, compiler_params=_params(("parallel",)),
    )(r8, *own_ops, w, m, v)


_HBM = pl.BlockSpec(memory_space=pltpu.HBM)
_SEM = pl.BlockSpec(memory_space=pltpu.SEMAPHORE)
_EFFECT = pltpu.SideEffectType.DATAFLOW_SIDE_EFFECTING


def _exchange_copies(g_refs, land_refs, send_sems, recv_sems, gather):
    mx, my, mc = lax.axis_index("x"), lax.axis_index("y"), lax.axis_index("c")
    copies = []
    for a in range(len(g_refs)):
        for f in ((1, 2, 4, 6) if gather else range(1, NDEV)):
            px = 1 - mx if (f >> 2) & 1 else mx
            py = 1 - my if (f >> 1) & 1 else my
            pc = 1 - mc if f & 1 else mc
            src = g_refs[a] if gather else g_refs[a].at[4 * px + 2 * py + pc]
            dst = land_refs[a].at[4 * mx + 2 * my + mc] if gather else land_refs[a].at[f]
            copies.append(pltpu.make_async_remote_copy(
                src_ref=src, dst_ref=dst,
                send_sem=send_sems.at[7 * a + f - 1], recv_sem=recv_sems.at[7 * a + f - 1],
                device_id=(px, py, pc), device_id_type=MESH))
    return copies


def _exchange_start(gs, name, gather=False):
    na = len(gs)

    def body(*refs):
        for cp in _exchange_copies(refs[:na], refs[na:2 * na], refs[2 * na], refs[2 * na + 1], gather):
            cp.start()
        refs[-1][...] = jnp.zeros_like(refs[-1])

    hbm = [pltpu.HBM(g.shape, g.dtype) for g in gs]
    land_shapes = [((NDEV,) + g.shape) if gather else g.shape for g in gs]
    lands = [pltpu.with_memory_space_constraint(lax.empty(shp, g.dtype), pltpu.HBM)
             for shp, g in zip(land_shapes, gs)]
    hbm_land = [pltpu.HBM(shp, g.dtype) for shp, g in zip(land_shapes, gs)]
    outs = pl.pallas_call(
        body, name=name,
        out_shape=(pltpu.SemaphoreType.DMA((7 * na,)), pltpu.SemaphoreType.DMA((7 * na,)), *hbm, *hbm_land,
                   jax.ShapeDtypeStruct((8, 128), F32)),
        in_specs=[_HBM] * (2 * na),
        out_specs=(_SEM, _SEM, *([_HBM] * (2 * na)), pl.BlockSpec(memory_space=pltpu.VMEM)),
        input_output_aliases={i: 2 + i for i in range(2 * na)},
        compiler_params=pltpu.CompilerParams(has_side_effects=_EFFECT),
    )(*[pltpu.with_memory_space_constraint(g, pltpu.HBM) for g in gs], *lands)
    return outs[0], outs[1], outs[2:2 + na], outs[2 + na:2 + 2 * na], outs[-1]


def _forward_copies(land_refs, send_sems, recv_sems):
    mx, my, mc = lax.axis_index("x"), lax.axis_index("y"), lax.axis_index("c")
    copies = []
    for a in range(len(land_refs)):
        for j, (fx, fy) in enumerate(((0, 1), (1, 0), (1, 1))):
            px = 1 - mx if fx else mx
            py = 1 - my if fy else my
            blk = land_refs[a].at[4 * px + 2 * py + mc]
            copies.append(pltpu.make_async_remote_copy(
                src_ref=blk, dst_ref=blk, send_sem=send_sems.at[3 * a + j], recv_sem=recv_sems.at[3 * a + j],
                device_id=(mx, my, 1 - mc), device_id_type=MESH))
    return copies


def _forward_start(lands, name):
    na = len(lands)

    def body(*refs):
        for cp in _forward_copies(refs[:na], refs[na], refs[na + 1]):
            cp.start()
        refs[-1][...] = jnp.zeros_like(refs[-1])

    outs = pl.pallas_call(
        body, name=name,
        out_shape=(pltpu.SemaphoreType.DMA((3 * na,)), pltpu.SemaphoreType.DMA((3 * na,)),
                   *[pltpu.HBM(g.shape, g.dtype) for g in lands], jax.ShapeDtypeStruct((8, 128), F32)),
        in_specs=[_HBM] * na,
        out_specs=(_SEM, _SEM, *([_HBM] * na), pl.BlockSpec(memory_space=pltpu.VMEM)),
        input_output_aliases={i: 2 + i for i in range(na)},
        compiler_params=pltpu.CompilerParams(has_side_effects=_EFFECT),
    )(*lands)
    return outs[0], outs[1], outs[2:2 + na], outs[-1]


def _forward_wait(send_sems, recv_sems, lands, after, name):
    na = len(lands)

    def body(*refs):
        for cp in _forward_copies(refs[:na], refs[na], refs[na + 1]):
            cp.wait_send()
            cp.wait_recv()

    return pl.pallas_call(
        body, name=name,
        out_shape=tuple(pltpu.HBM(g.shape, g.dtype) for g in lands),
        in_specs=[_HBM] * na + [_SEM, _SEM, pl.BlockSpec(memory_space=pl.ANY)],
        out_specs=tuple([_HBM] * na),
        input_output_aliases={i: i for i in range(na)},
        compiler_params=pltpu.CompilerParams(has_side_effects=_EFFECT),
    )(*lands, send_sems, recv_sems, after)


def _exchange_wait(send_sems, recv_sems, g_thru, land_thru, after, name, gather=False):
    na = len(g_thru)

    def body(*refs):
        for cp in _exchange_copies(refs[:na], refs[na:2 * na], refs[2 * na], refs[2 * na + 1], gather):
            cp.wait_send()
            cp.wait_recv()

    outs = pl.pallas_call(
        body, name=name,
        out_shape=tuple(pltpu.HBM(g.shape, g.dtype) for g in list(g_thru) + list(land_thru)),
        in_specs=[_HBM] * (2 * na) + [_SEM, _SEM, pl.BlockSpec(memory_space=pl.ANY)],
        out_specs=tuple([_HBM] * (2 * na)),
        input_output_aliases={i: i for i in range(2 * na)},
        compiler_params=pltpu.CompilerParams(has_side_effects=_EFFECT),
    )(*g_thru, *land_thru, send_sems, recv_sems, after)
    return outs[na:]


def _sum8(r, name):
    _, R, C = r.shape
    tr = _tile(R, (256, 160, 128, 64, 32, 16, 8))

    def body(r_ref, o_ref):
        acc = r_ref[0].astype(F32)
        for k in range(1, NDEV):
            acc = acc + r_ref[k].astype(F32)
        o_ref[...] = acc

    return pl.pallas_call(
        body, name=name, grid=(R // tr,),
        in_specs=[pl.BlockSpec((NDEV, tr, C), lambda i: (0, i, 0))],
        out_specs=pl.BlockSpec((tr, C), lambda i: (i, 0)),
        out_shape=jax.ShapeDtypeStruct((R, C), F32),
        compiler_params=_params(("parallel",)),
    )(r)


def _adamw(g, w, m, v, name):
    R, C = g.shape
    tr = _tile(R, (256, 160, 128, 64, 32, 16, 8))
    bc1 = 1.0 - ADAM_B1 ** ADAM_STEP
    bc2 = 1.0 - ADAM_B2 ** ADAM_STEP

    def body(g_ref, w_ref, m_ref, v_ref, d_ref, mo_ref, vo_ref):
        gg = g_ref[...]
        mn = ADAM_B1 * m_ref[...] + (1.0 - ADAM_B1) * gg
        vn = ADAM_B2 * v_ref[...] + (1.0 - ADAM_B2) * (gg * gg)
        mh = mn / bc1
        vh = vn / bc2
        d_ref[...] = -ADAM_LR * (mh / (jnp.sqrt(vh) + ADAM_EPS) + ADAM_WD * w_ref[...])
        mo_ref[...] = mn
        vo_ref[...] = vn

    spec = pl.BlockSpec((tr, C), lambda i: (i, 0))
    sh = jax.ShapeDtypeStruct((R, C), F32)
    return pl.pallas_call(
        body, name=name, grid=(R // tr,), in_specs=[spec] * 4, out_specs=[spec] * 3,
        out_shape=[sh] * 3, compiler_params=_params(("parallel",)),
    )(g, w, m, v)


def _ada_fwd(c16, w_sh, b_sh):
    def body(c_ref, w_ref, b_ref, o_ref):
        c = c_ref[...]
        o_ref[...] = _dot(c * _sig(c), w_ref[...], NN) + b_ref[...]

    return pl.pallas_call(
        body, name="ada_fwd", out_shape=jax.ShapeDtypeStruct((16, w_sh.shape[1]), F32),
        compiler_params=pltpu.CompilerParams(vmem_limit_bytes=VMEM_LIMIT),
    )(c16, w_sh, b_sh)


def _ada_bwd(c16, g16, g16_sh, w_sh):
    ncol = w_sh.shape[1]

    def body(c_ref, g_ref, gs_ref, w_ref, dw_ref, db_ref, dc_ref):
        c = c_ref[...]
        s = _sig(c)
        gs = gs_ref[...]
        dw_ref[...] = _dot(c * s, gs, TN)
        db_ref[...] = jnp.broadcast_to(_colsum(g_ref[...]), db_ref.shape)
        odd = lax.broadcasted_iota(jnp.int32, gs.shape, 0) % 2 == 1
        gc = _colsum(jnp.where(odd, gs, 0.0))
        ds = _dot(jnp.broadcast_to(gc, (8, ncol)), w_ref[...], NT)
        c1 = c[1:2, :]
        s1 = s[1:2, :]
        dc_ref[...] = ds * (s1 * (1.0 + c1 * (1.0 - s1)))

    return pl.pallas_call(
        body, name="ada_bwd",
        out_shape=[jax.ShapeDtypeStruct(w_sh.shape, F32),
                   jax.ShapeDtypeStruct((8, g16.shape[1]), F32),
                   jax.ShapeDtypeStruct((8, D), F32)],
        compiler_params=pltpu.CompilerParams(vmem_limit_bytes=VMEM_LIMIT),
    )(c16, g16, g16_sh, w_sh)


def _ln0_fwd(x, ctx, g, b, modx, modc):
    L = x.shape[0]
    nt = L // TL

    def body(x_ref, c_ref, g_ref, b_ref, mx_ref, mc_ref, xn_ref, h_ref):
        isc = pl.program_id(0) == nt
        xin = jnp.where(isc, c_ref[...], x_ref[...])
        sh = jnp.where(isc, mc_ref[0:1, :], mx_ref[0:1, :])
        sc = jnp.where(isc, mc_ref[1:2, :], mx_ref[1:2, :])
        xhat, _ = _ln(xin)
        xn = xhat * g_ref[...] + b_ref[...]
        xn_ref[...] = xn
        h_ref[...] = (xn * (1.0 + sc) + sh).astype(h_ref.dtype)

    return pl.pallas_call(
        body, name="ln0_fwd", grid=(nt + 1,),
        in_specs=[_rtc(D, nt), _cst((TL, D)), _cst((1, D)), _cst((1, D)), _cst((8, D)), _cst((8, D))],
        out_specs=[_rt(D), _rt(D)],
        out_shape=[jax.ShapeDtypeStruct((L + TL, D), F32), jax.ShapeDtypeStruct((L + TL, D), _MXU)],
        compiler_params=_params(("parallel",)),
    )(x, ctx, g, b, modx, modc)


def _xbc_colblk(j):
    return jnp.where(j < 8, OXS // 128 + j, OB // 128 + j - 8)


def _conv_taps(p_ref, r0, first, last):
    main = p_ref[pl.ds(r0, TL), :]
    zero = jnp.zeros((8, main.shape[1]), F32)
    prev = zero if first else p_ref[pl.ds(r0 - 8, 8), :]
    nxt = zero if last else p_ref[pl.ds(r0 + TL, 8), :]
    ext = jnp.concatenate([prev, main, nxt], axis=0)
    n = TL + 16
    return [pltpu.roll(ext, (2 - k) % n, 0)[8:8 + TL] for k in range(5)]


def _seq_chunks(L):
    nt = L // TL
    return [(r * TL, r == 0, r == nt - 1) for r in range(nt)] + [(L, True, True)]


def _conv_fwd(p, conv_w8, conv_b):
    RT = p.shape[0]
    L = RT - TL
    chunks = _seq_chunks(L)

    def body(p_ref, w_ref, b_ref, o_ref):
        w = w_ref[...]
        bias = b_ref[...]
        for r0, first, last in chunks:
            taps = _conv_taps(p_ref, r0, first, last)
            pre = bias + sum(w[k:k + 1, :] * taps[k] for k in range(5))
            o_ref[pl.ds(r0, TL), :] = pre * _sig(pre)

    return pl.pallas_call(
        body, name="conv_fwd", grid=(12,),
        in_specs=[pl.BlockSpec((RT, 128), lambda j: (0, _xbc_colblk(j))),
                  pl.BlockSpec((8, 128), lambda j: (0, j)),
                  pl.BlockSpec((1, 128), lambda j: (0, j))],
        out_specs=pl.BlockSpec((RT, 128), lambda j: (0, j)),
        out_shape=jax.ShapeDtypeStruct((RT, 1536), F32),
        compiler_params=_params(("parallel",)),
    )(p, conv_w8, conv_b)


def _ssd_common(dtraw, dtb, a32, rev):
    dt = _softplus(dtraw + dtb)
    acum = _cumsum_rows(dt * a32, rev)
    ii = lax.broadcasted_iota(jnp.int32, (Q, Q), 0)
    jj = lax.broadcasted_iota(jnp.int32, (Q, Q), 1)
    mask = (ii <= jj) if rev else (ii >= jj)
    return dt, acum, acum.T, dt.T, mask


def _ssd_orders(ncl, ncc):
    nc = ncl + ncc

    def cf(s):
        return jnp.where(s < ncc, ncl + s, s - ncc)

    def cb(s):
        return nc - 1 - s

    return cf, cb


def _ssd_fwd(xbc, p, prm):
    RT = xbc.shape[0]
    nc = RT // Q
    ncc = TL // Q
    cf, cb = _ssd_orders(nc - ncc, ncc)

    def one_dir(x_ref, dt_ref, prm_ref, y_ref, hp_ref, H_ref, d):
        rev = d == 1
        a32 = -jnp.exp(prm_ref[1:2, :])
        dt, acum, acumT, dtT, mask = _ssd_common(dt_ref[...], prm_ref[0:1, :], a32, rev)
        end = 0 if rev else Q - 1
        for g in range(2):
            Bg = x_ref[:, D + g * NS:D + (g + 1) * NS]
            Cg = x_ref[:, D + 2 * NS + g * NS:D + 2 * NS + (g + 1) * NS]
            CB = _dot(Cg, Bg, NT)
            for hh in range(HPG):
                h = g * HPG + hh
                ln = 16 * d + h
                col = acum[:, ln:ln + 1]
                rowv = acumT[ln:ln + 1, :]
                a_end = rowv[:, end:end + 1]
                Lm = jnp.exp(jnp.where(mask, col - rowv, -1e30))
                W = CB * Lm * dtT[ln:ln + 1, :]
                Xh = x_ref[:, h * HP:(h + 1) * HP]
                Hp = H_ref[h * HP:(h + 1) * HP, :]
                y = _dot(W, Xh, NN) + jnp.exp(col) * _dot(Cg, Hp, NT)
                y_ref[:, h * HP:(h + 1) * HP] = y
                dcol = jnp.exp(a_end - col) * dt[:, ln:ln + 1]
                hp_ref[0, h * HP:(h + 1) * HP, :] = Hp
                H_ref[h * HP:(h + 1) * HP, :] = jnp.exp(a_end) * Hp + _dot(Xh * dcol, Bg, TN)

    def body(xf_ref, xb_ref, df_ref, db_ref, prm_ref, yf_ref, yb_ref, hf_ref, hb_ref, Hf, Hb):
        @pl.when(pl.program_id(0) == 0)
        def _():
            Hf[...] = jnp.zeros_like(Hf)
            Hb[...] = jnp.zeros_like(Hb)

        one_dir(xf_ref, df_ref, prm_ref, yf_ref, hf_ref, Hf, 0)
        one_dir(xb_ref, db_ref, prm_ref, yb_ref, hb_ref, Hb, 1)

    ysh = jax.ShapeDtypeStruct((RT, D), F32)
    hsh = jax.ShapeDtypeStruct((nc, NH * HP, NS), F32)
    hspec = pl.BlockSpec((1, NH * HP, NS), lambda s: (s, 0, 0))
    return pl.pallas_call(
        body, name="ssd_fwd", grid=(nc,),
        in_specs=[pl.BlockSpec((Q, 1536), lambda s: (cf(s), 0)),
                  pl.BlockSpec((Q, 1536), lambda s: (cb(s), 0)),
                  pl.BlockSpec((Q, 128), lambda s: (cf(s), ODT // 128)),
                  pl.BlockSpec((Q, 128), lambda s: (cb(s), ODT // 128)),
                  _cst((8, 128))],
        out_specs=[pl.BlockSpec((Q, D), lambda s: (cf(s), 0)),
                   pl.BlockSpec((Q, D), lambda s: (cb(s), 0)), hspec, hspec],
        out_shape=[ysh, ysh, hsh, hsh],
        scratch_shapes=[pltpu.VMEM((NH * HP, NS), F32), pltpu.VMEM((NH * HP, NS), F32)],
        compiler_params=_params(("arbitrary",)),
    )(xbc, xbc, p, p, prm)


def _ssd_bwd(xbc, p, prm, dsk, dyd, hpf, hpb):
    RT = xbc.shape[0]
    nc = RT // Q
    ncc = TL // Q
    ncl = nc - ncc
    cf, cb = _ssd_orders(ncl, ncc)

    def rs(t):
        return nc - 1 - t

    def one_dir(x_ref, dt_ref, prm_ref, dsk_ref, dy_ref, is_ctx, hp_ref, dH_ref,
                dx_ref, ddt_ref, st_ref, d):
        rev = d == 1
        a32 = -jnp.exp(prm_ref[1:2, :])
        dtraw = dt_ref[...]
        dtb = prm_ref[0:1, :]
        dt, acum, acumT, dtT, mask = _ssd_common(dtraw, dtb, a32, rev)
        end = 0 if rev else Q - 1
        lane = lax.broadcasted_iota(jnp.int32, (Q, 128), 1)
        srow = lax.broadcasted_iota(jnp.int32, (Q, 128), 0)
        dyscale = jnp.where(is_ctx, 0.0, 1.0)
        c_dacum = jnp.zeros((Q, 128), F32)
        r_dacum = jnp.zeros((Q, 128), F32)
        c_ddt = jnp.zeros((Q, 128), F32)
        r_ddt = jnp.zeros((Q, 128), F32)
        dskacc = jnp.zeros((1, 128), F32)
        for g in range(2):
            Bg = x_ref[:, D + g * NS:D + (g + 1) * NS]
            Cg = x_ref[:, D + 2 * NS + g * NS:D + 2 * NS + (g + 1) * NS]
            CB = _dot(Cg, Bg, NT)
            dCB = jnp.zeros((Q, Q), F32)
            dBg = jnp.zeros((Q, NS), F32)
            dCg = jnp.zeros((Q, NS), F32)
            for hh in range(HPG):
                h = g * HPG + hh
                ln = 16 * d + h
                hs = slice(h * HP, (h + 1) * HP)
                col = acum[:, ln:ln + 1]
                rowv = acumT[ln:ln + 1, :]
                dtr = dtT[ln:ln + 1, :]
                dtc = dt[:, ln:ln + 1]
                a_end = rowv[:, end:end + 1]
                Lm = jnp.exp(jnp.where(mask, col - rowv, -1e30))
                E = jnp.exp(col)
                ecol = jnp.exp(a_end - col)
                dcol = ecol * dtc
                Xh = x_ref[:, hs]
                dY = dy_ref[:, hs] * dyscale
                Hp = hp_ref[0, hs, :]
                dHn = dH_ref[hs, :]
                W = CB * Lm * dtr
                dW = _dot(dY, Xh, NT)
                Mm = dW * CB * Lm
                T = Mm * dtr
                dCB = dCB + dW * Lm * dtr
                BdH = _dot(Bg, dHn, NT)
                dX = _dot(W, dY, TN) + dcol * BdH
                if d == 0:
                    dX = dX + dY * dsk_ref[:, hs]
                    dskacc = dskacc + jnp.where(lane[0:1, :] == h, _sum11(dY * Xh), 0.0)
                dx_ref[:, hs] = dX
                xb = jnp.sum(Xh * BdH, axis=1, keepdims=True)
                scol = dcol * xb
                G = _dot(dY, Hp, NN)
                dCg = dCg + E * G
                qcol = E * jnp.sum(G * Cg, axis=1, keepdims=True)
                dBg = dBg + _dot(Xh * dcol, dHn, NN)
                dH_ref[hs, :] = jnp.exp(a_end) * dHn + _dot(dY * E, Cg, TN)
                eterm = jnp.exp(a_end) * _sum11(dHn * Hp) + _sum11(scol)
                cvec = jnp.sum(T, axis=1, keepdims=True) + qcol - scol
                cvec = cvec + jnp.where(srow[:, 0:1] == end, eterm, 0.0)
                c_dacum = c_dacum + jnp.where(lane == ln, cvec, 0.0)
                r_dacum = r_dacum - jnp.where(srow == ln, _colsum(T), 0.0)
                c_ddt = c_ddt + jnp.where(lane == ln, ecol * xb, 0.0)
                r_ddt = r_ddt + jnp.where(srow == ln, _colsum(Mm), 0.0)
            dBg = dBg + _dot(dCB, Cg, TN)
            dCg = dCg + _dot(dCB, Bg, NN)
            dx_ref[:, D + g * NS:D + (g + 1) * NS] = dBg
            dx_ref[:, D + 2 * NS + g * NS:D + 2 * NS + (g + 1) * NS] = dCg
        dacum = c_dacum + r_dacum.T
        da = _cumsum_rows(dacum, not rev)
        mine = (lane >= 16 * d) & (lane < 16 * d + 16)
        ddt = jnp.where(mine, c_ddt + r_ddt.T + da * a32, 0.0)
        ddt_ref[...] = ddt * _sig(dtraw + dtb)
        st_ref[0:1, :] += _colsum(jnp.where(mine, da * dt, 0.0))
        if d == 0:
            st_ref[1:2, :] += dskacc

    def body(xf_ref, xb_ref, df_ref, db_ref, prm_ref, dsk_ref, dyf_ref, dyb_ref, hf_ref, hb_ref,
             dxf_ref, dxb_ref, ddf_ref, ddb_ref, st_ref, dHf, dHb):
        t = pl.program_id(0)

        @pl.when(t == 0)
        def _():
            dHf[...] = jnp.zeros_like(dHf)
            dHb[...] = jnp.zeros_like(dHb)
            st_ref[...] = jnp.zeros_like(st_ref)

        s = rs(t)
        one_dir(xf_ref, df_ref, prm_ref, dsk_ref, dyf_ref, cf(s) >= ncl, hf_ref, dHf,
                dxf_ref, ddf_ref, st_ref, 0)
        one_dir(xb_ref, db_ref, prm_ref, dsk_ref, dyb_ref, cb(s) >= ncl, hb_ref, dHb,
                dxb_ref, ddb_ref, st_ref, 1)

        @pl.when(t == nc - 1)
        def _():
            st_ref[0:1, :] = -jnp.exp(prm_ref[1:2, :]) * st_ref[0:1, :]

    def lat(c):
        return jnp.minimum(c, ncl - 1)

    xsh = jax.ShapeDtypeStruct((RT, 1536), F32)
    dsh = jax.ShapeDtypeStruct((RT, 128), F32)
    hspec = pl.BlockSpec((1, NH * HP, NS), lambda t: (rs(t), 0, 0))
    return pl.pallas_call(
        body, name="ssd_bwd", grid=(nc,),
        in_specs=[pl.BlockSpec((Q, 1536), lambda t: (cf(rs(t)), 0)),
                  pl.BlockSpec((Q, 1536), lambda t: (cb(rs(t)), 0)),
                  pl.BlockSpec((Q, 128), lambda t: (cf(rs(t)), ODT // 128)),
                  pl.BlockSpec((Q, 128), lambda t: (cb(rs(t)), ODT // 128)),
                  _cst((8, 128)), _cst((1, D)),
                  pl.BlockSpec((Q, D), lambda t: (lat(cf(rs(t))), 0)),
                  pl.BlockSpec((Q, D), lambda t: (lat(cb(rs(t))), 0)),
                  hspec, hspec],
        out_specs=[pl.BlockSpec((Q, 1536), lambda t: (cf(rs(t)), 0)),
                   pl.BlockSpec((Q, 1536), lambda t: (cb(rs(t)), 0)),
                   pl.BlockSpec((Q, 128), lambda t: (cf(rs(t)), 0)),
                   pl.BlockSpec((Q, 128), lambda t: (cb(rs(t)), 0)),
                   _cst((8, 128))],
        out_shape=[xsh, xsh, dsh, dsh, jax.ShapeDtypeStruct((8, 128), F32)],
        scratch_shapes=[pltpu.VMEM((NH * HP, NS), F32), pltpu.VMEM((NH * HP, NS), F32)],
        compiler_params=_params(("arbitrary",)),
    )(xbc, xbc, p, p, prm, dsk, dyd, dyd, hpf, hpb)


def _lane_bcast(v, ln):
    return jnp.broadcast_to(v[:, ln:ln + 1], v.shape)


def _halves(v, lo, axis):
    return jnp.concatenate([jnp.where(lo, v, 0.0), jnp.where(lo, 0.0, v)], axis=axis)


def _ssd2_fwd(xbc, p, prm):
    RT = xbc.shape[0]
    nc = RT // Q
    ncc = TL // Q
    cf, cb = _ssd_orders(nc - ncc, ncc)

    def one_dir(x_ref, dt_ref, prm_ref, y_ref, hp_ref, HT_ref, d):
        rev = d == 1
        a32 = -jnp.exp(prm_ref[1:2, :])
        dt, acum, acumT, dtT, mask = _ssd_common(dt_ref[...], prm_ref[0:1, :], a32, rev)
        end = 0 if rev else Q - 1
        lo = lax.broadcasted_iota(jnp.int32, (Q, 128), 1) < HP
        for g in range(2):
            Bg = x_ref[:, D + g * NS:D + (g + 1) * NS]
            Cg = x_ref[:, D + 2 * NS + g * NS:D + 2 * NS + (g + 1) * NS]
            CB = _dot(Cg, Bg, NT)
            xds, svs = [], []
            for q in range(HPG // 2):
                pi = g * (HPG // 2) + q
                ps = slice(pi * 128, (pi + 1) * 128)
                Xp = x_ref[:, ps]
                HTp = HT_ref[:, ps]
                lhs, dcs, sv = [], [], []
                ces = []
                for h in (2 * pi, 2 * pi + 1):
                    ln = 16 * d + h
                    colB = _lane_bcast(acum, ln)
                    rowv = acumT[ln:ln + 1, :]
                    aend = colB[end:end + 1, :]
                    Lm = jnp.exp(jnp.where(mask, colB - rowv, -1e30))
                    lhs.append(CB * Lm * dtT[ln:ln + 1, :])
                    ces.append(Cg * jnp.exp(colB))
                    dcs.append(jnp.exp(aend - colB) * _lane_bcast(dt, ln))
                    sv.append(jnp.exp(aend))
                lhs = jnp.concatenate(lhs + ces, axis=1)
                rhs = jnp.concatenate([_halves(Xp, lo, 0), _halves(HTp, lo, 0)], axis=0)
                y_ref[:, ps] = _dot(lhs, rhs, NN)
                xds.append(Xp * jnp.where(lo, dcs[0], dcs[1]))
                svs.append(jnp.where(lo[0:1, :], sv[0], sv[1]))
            gs = slice(g * 512, (g + 1) * 512)
            HTg = HT_ref[:, gs]
            hp_ref[0, :, gs] = HTg
            st = _dot(Bg.T, jnp.concatenate(xds, axis=1), NN)
            HT_ref[:, gs] = jnp.concatenate(svs, axis=1) * HTg + st

    def body(xf_ref, xb_ref, df_ref, db_ref, prm_ref, yf_ref, yb_ref, hf_ref, hb_ref, Hf, Hb):
        @pl.when(pl.program_id(0) == 0)
        def _():
            Hf[...] = jnp.zeros_like(Hf)
            Hb[...] = jnp.zeros_like(Hb)

        one_dir(xf_ref, df_ref, prm_ref, yf_ref, hf_ref, Hf, 0)
        one_dir(xb_ref, db_ref, prm_ref, yb_ref, hb_ref, Hb, 1)

    ysh = jax.ShapeDtypeStruct((RT, D), F32)
    hsh = jax.ShapeDtypeStruct((nc, NS, NH * HP), F32)
    hspec = pl.BlockSpec((1, NS, NH * HP), lambda s: (s, 0, 0))
    return pl.pallas_call(
        body, name="ssd_fwd", grid=(nc,),
        in_specs=[pl.BlockSpec((Q, 1536), lambda s: (cf(s), 0)),
                  pl.BlockSpec((Q, 1536), lambda s: (cb(s), 0)),
                  pl.BlockSpec((Q, 128), lambda s: (cf(s), ODT // 128)),
                  pl.BlockSpec((Q, 128), lambda s: (cb(s), ODT // 128)),
                  _cst((8, 128))],
        out_specs=[pl.BlockSpec((Q, D), lambda s: (cf(s), 0)),
                   pl.BlockSpec((Q, D), lambda s: (cb(s), 0)), hspec, hspec],
        out_shape=[ysh, ysh, hsh, hsh],
        scratch_shapes=[pltpu.VMEM((NS, NH * HP), F32), pltpu.VMEM((NS, NH * HP), F32)],
        compiler_params=_params(("arbitrary",)),
    )(xbc, xbc, p, p, prm)


def _ssd2_bwd(xbc, p, prm, dsk, dyd, hpf, hpb):
    RT = xbc.shape[0]
    nc = RT // Q
    ncc = TL // Q
    ncl = nc - ncc
    cf, cb = _ssd_orders(ncl, ncc)

    def rs(t):
        return nc - 1 - t

    def one_dir(x_ref, dt_ref, prm_ref, dsk_ref, dy_ref, is_ctx, hp_ref, dHT_ref,
                dx_ref, ddt_ref, st_ref, d):
        rev = d == 1
        a32 = -jnp.exp(prm_ref[1:2, :])
        dtraw = dt_ref[...]
        dtb = prm_ref[0:1, :]
        dt, acum, acumT, _, _ = _ssd_common(dtraw, dtb, a32, rev)
        end = 0 if rev else Q - 1
        lane = lax.broadcasted_iota(jnp.int32, (Q, 128), 1)
        srow = lax.broadcasted_iota(jnp.int32, (Q, 128), 0)
        maskT = (lane <= srow) if rev else (lane >= srow)
        lo = lane < HP
        lo1 = lo[0:1, :]
        dyscale = jnp.where(is_ctx, 0.0, 1.0)
        c_dacum = jnp.zeros((Q, 128), F32)
        r_dacum = jnp.zeros((Q, 128), F32)
        c_ddt = jnp.zeros((Q, 128), F32)
        dskacc = jnp.zeros((1, 128), F32)
        for g in range(2):
            gs = slice(g * 512, (g + 1) * 512)
            Bg = x_ref[:, D + g * NS:D + (g + 1) * NS]
            Cg = x_ref[:, D + 2 * NS + g * NS:D + 2 * NS + (g + 1) * NS]
            CBT = _dot(Bg, Cg, NT)
            HTg = hp_ref[0, :, gs]
            dHTg = dHT_ref[:, gs]
            BdHg = _dot(Bg, dHTg, NN)
            dCBT = jnp.zeros((Q, Q), F32)
            dCg = jnp.zeros((Q, NS), F32)
            xds, dyes, svs = [], [], []
            for q in range(HPG // 2):
                pi = g * (HPG // 2) + q
                ps = slice(pi * 128, (pi + 1) * 128)
                qs = slice(q * 128, (q + 1) * 128)
                Xp = x_ref[:, ps]
                dYp = dy_ref[:, ps] * dyscale
                HTp = HTg[:, qs]
                BdHp = BdHg[:, qs]
                dY2 = _halves(dYp, lo, 0)
                dWT2 = _dot(_halves(Xp, lo, 0), dYp.T, NN)
                G2 = _dot(dY2, HTp, NT)
                XB = Xp * BdHp
                hh = _colsum(dHTg[:, qs] * HTp)
                yx = _colsum(dYp * Xp)
                wts, dcs, ebs, sv = [], [], [], []
                for k, h in enumerate((2 * pi, 2 * pi + 1)):
                    ln = 16 * d + h
                    half = lo if k == 0 else jnp.logical_not(lo)
                    half1 = half[0:1, :]
                    colB = _lane_bcast(acum, ln)
                    dtcB = _lane_bcast(dt, ln)
                    rowv = acumT[ln:ln + 1, :]
                    aend = colB[end:end + 1, :]
                    LmT = jnp.exp(jnp.where(maskT, rowv - colB, -1e30))
                    WT = CBT * LmT * dtcB
                    dWT = dWT2[k * Q:(k + 1) * Q, :]
                    MT = dWT * CBT * LmT
                    rM = jnp.sum(MT, axis=1, keepdims=True)
                    rT = _colsum(MT * dtcB)
                    dCBT = dCBT + dWT * LmT * dtcB
                    ecol = jnp.exp(aend - colB)
                    EB = jnp.exp(colB)
                    Gk = G2[k * Q:(k + 1) * Q, :]
                    dCg = dCg + EB * Gk
                    qcol = jnp.sum(EB * Gk * Cg, axis=1, keepdims=True)
                    xb = jnp.sum(jnp.where(half, XB, 0.0), axis=1, keepdims=True)
                    e1 = ecol[:, 0:1]
                    dt1 = dtcB[:, 0:1]
                    scol = e1 * dt1 * xb
                    sA = jnp.exp(aend)
                    eterm = sA[:, 0:1] * jnp.sum(jnp.where(half1, hh, 0.0), axis=1, keepdims=True) \
                        + _colsum(scol)
                    cvec = qcol - dt1 * rM - scol + jnp.where(srow[:, 0:1] == end, eterm, 0.0)
                    c_dacum = c_dacum + jnp.where(lane == ln, cvec, 0.0)
                    r_dacum = r_dacum + jnp.where(srow == ln, rT, 0.0)
                    c_ddt = c_ddt + jnp.where(lane == ln, rM + e1 * xb, 0.0)
                    if d == 0:
                        dskacc = dskacc + jnp.where(
                            lane[0:1, :] == h, jnp.sum(jnp.where(half1, yx, 0.0), axis=1, keepdims=True), 0.0)
                    wts.append(WT)
                    dcs.append(ecol * dtcB)
                    ebs.append(EB)
                    sv.append(sA)
                dcp = jnp.where(lo, dcs[0], dcs[1])
                dX = _dot(jnp.concatenate(wts, axis=1), dY2, NN) + dcp * BdHp
                if d == 0:
                    dX = dX + dYp * dsk_ref[:, ps]
                dx_ref[:, ps] = dX
                xds.append(Xp * dcp)
                dyes.append(dYp * jnp.where(lo, ebs[0], ebs[1]))
                svs.append(jnp.where(lo1, sv[0], sv[1]))
            dx_ref[:, D + g * NS:D + (g + 1) * NS] = (
                _dot(jnp.concatenate(xds, axis=1), dHTg, NT) + _dot(dCBT, Cg, NN))
            dx_ref[:, D + 2 * NS + g * NS:D + 2 * NS + (g + 1) * NS] = dCg + _dot(dCBT, Bg, TN)
            dHT_ref[:, gs] = (jnp.concatenate(svs, axis=1) * dHTg
                              + _dot(Cg.T, jnp.concatenate(dyes, axis=1), NN))
        dacum = c_dacum + r_dacum.T
        da = _cumsum_rows(dacum, not rev)
        mine = (lane >= 16 * d) & (lane < 16 * d + 16)
        ddt = jnp.where(mine, c_ddt + da * a32, 0.0)
        ddt_ref[...] = ddt * _sig(dtraw + dtb)
        st_ref[0:1, :] += _colsum(jnp.where(mine, da * dt, 0.0))
        if d == 0:
            st_ref[1:2, :] += dskacc

    def body(xf_ref, xb_ref, df_ref, db_ref, prm_ref, dsk_ref, dyf_ref, dyb_ref, hf_ref, hb_ref,
             dxf_ref, dxb_ref, ddf_ref, ddb_ref, st_ref, dHf, dHb):
        t = pl.program_id(0)

        @pl.when(t == 0)
        def _():
            dHf[...] = jnp.zeros_like(dHf)
            dHb[...] = jnp.zeros_like(dHb)
            st_ref[...] = jnp.zeros_like(st_ref)

        s = rs(t)
        one_dir(xf_ref, df_ref, prm_ref, dsk_ref, dyf_ref, cf(s) >= ncl, hf_ref, dHf,
                dxf_ref, ddf_ref, st_ref, 0)
        one_dir(xb_ref, db_ref, prm_ref, dsk_ref, dyb_ref, cb(s) >= ncl, hb_ref, dHb,
                dxb_ref, ddb_ref, st_ref, 1)

        @pl.when(t == nc - 1)
        def _():
            st_ref[0:1, :] = -jnp.exp(prm_ref[1:2, :]) * st_ref[0:1, :]

    def lat(c):
        return jnp.minimum(c, ncl - 1)

    xsh = jax.ShapeDtypeStruct((RT, 1536), F32)
    dsh = jax.ShapeDtypeStruct((RT, 128), F32)
    hspec = pl.BlockSpec((1, NS, NH * HP), lambda t: (rs(t), 0, 0))
    return pl.pallas_call(
        body, name="ssd_bwd", grid=(nc,),
        in_specs=[pl.BlockSpec((Q, 1536), lambda t: (cf(rs(t)), 0)),
                  pl.BlockSpec((Q, 1536), lambda t: (cb(rs(t)), 0)),
                  pl.BlockSpec((Q, 128), lambda t: (cf(rs(t)), ODT // 128)),
                  pl.BlockSpec((Q, 128), lambda t: (cb(rs(t)), ODT // 128)),
                  _cst((8, 128)), _cst((1, D)),
                  pl.BlockSpec((Q, D), lambda t: (lat(cf(rs(t))), 0)),
                  pl.BlockSpec((Q, D), lambda t: (lat(cb(rs(t))), 0)),
                  hspec, hspec],
        out_specs=[pl.BlockSpec((Q, 1536), lambda t: (cf(rs(t)), 0)),
                   pl.BlockSpec((Q, 1536), lambda t: (cb(rs(t)), 0)),
                   pl.BlockSpec((Q, 128), lambda t: (cf(rs(t)), 0)),
                   pl.BlockSpec((Q, 128), lambda t: (cb(rs(t)), 0)),
                   _cst((8, 128))],
        out_shape=[xsh, xsh, dsh, dsh, jax.ShapeDtypeStruct((8, 128), F32)],
        scratch_shapes=[pltpu.VMEM((NS, NH * HP), F32), pltpu.VMEM((NS, NH * HP), F32)],
        compiler_params=_params(("arbitrary",)),
    )(xbc, xbc, p, p, prm, dsk, dyd, dyd, hpf, hpb)


def _mix_fwd_vals(yf, yb, z, xs, u, v, dsk, sg, gg, gb):
    y = yf + yb + xs * dsk
    sz = _sig(z)
    hh = y * z * sz
    r = lax.rsqrt(jnp.mean(hh * hh, axis=-1, keepdims=True) + EPS)
    nh = hh * r
    ug, tu = _gelu(u)
    vg, tv = _gelu(v)
    vhat, vrstd = _ln(vg)
    vn = vhat * gg + gb
    return y, sz, r, nh, ug, tu, vg, tv, vhat, vrstd, vn


def _mix_fwd(yf, yb, p, xbc, dsk, sg, gg, gb, ws, bsT):
    L = yf.shape[0] - TL
    nt = L // TL

    def body(yf_ref, yb_ref, z_ref, xs_ref, u_ref, v_ref, dsk_ref, sg_ref, gg_ref, gb_ref,
             ws_ref, bs_ref, ys_ref, ym_ref):
        _, _, _, nh, ug, _, _, _, _, _, vn = _mix_fwd_vals(
            yf_ref[...], yb_ref[...], z_ref[...], xs_ref[...], u_ref[...], v_ref[...],
            dsk_ref[...], sg_ref[...], gg_ref[...], gb_ref[...])
        ys_ref[...] = (nh * sg_ref[...]).astype(ys_ref.dtype)
        for n in range(TL // Q):
            rs_ = slice(n * Q, (n + 1) * Q)
            for g in range(8):
                cs = slice(g * 128, (g + 1) * 128)
                mixed = _dot(ws_ref[g], vn[rs_, cs], NN) + bs_ref[:, g:g + 1]
                ym_ref[rs_, cs] = (ug[rs_, cs] * mixed).astype(ym_ref.dtype)

    return pl.pallas_call(
        body, name="mix_fwd", grid=(nt,),
        in_specs=[_rt(D), _rt(D), _rt(D, OZ // D), _rt(D, 0), _rt(D, OU // D), _rt(D, OV // D),
                  _cst((1, D)), _cst((1, D)), _cst((1, D)), _cst((1, D)),
                  _cst((8, 128, 128)), _cst((128, 128))],
        out_specs=[_rt(D), _rt(D)],
        out_shape=[jax.ShapeDtypeStruct((L, D), _MXU), jax.ShapeDtypeStruct((L, D), _MXU)],
        compiler_params=_params(("parallel",)),
    )(yf, yb, p, xbc, p, p, dsk, sg, gg, gb, ws, bsT)


def _mix_bwd(dys, dym, yf, yb, p, xbc, dp, dsk, sg, gg, gb, ws, bsT):
    L = dys.shape[0]
    nt = L // TL

    def body(dys_ref, dym_ref, yf_ref, yb_ref, z_ref, xs_ref, u_ref, v_ref, dsk_ref, sg_ref,
             gg_ref, gb_ref, ws_ref, bs_ref, dp_any, dzuv_ref, dy_ref, st_ref,
             dws_ref, dbs_ref, dvn_s):
        del dp_any
        dz_ref = dzuv_ref.at[:, OZ:OZ + D]
        du_ref = dzuv_ref.at[:, OU:OU + D]
        dv_ref = dzuv_ref.at[:, OV:OV + D]

        @pl.when(pl.program_id(0) == 0)
        def _():
            st_ref[...] = jnp.zeros_like(st_ref)
            dws_ref[...] = jnp.zeros_like(dws_ref)
            dbs_ref[...] = jnp.zeros_like(dbs_ref)

        z = z_ref[...]
        u = u_ref[...]
        v = v_ref[...]
        y, sz, r, nh, ug, tu, vg, tv, vhat, vrstd, vn = _mix_fwd_vals(
            yf_ref[...], yb_ref[...], z, xs_ref[...], u, v,
            dsk_ref[...], sg_ref[...], gg_ref[...], gb_ref[...])
        dys = dys_ref[...]
        st_ref[0:1, :] += _colsum(dys * nh)
        dn = dys * sg_ref[...]
        dhh = r * (dn - nh * jnp.mean(dn * nh, axis=-1, keepdims=True))
        dy_ref[...] = dhh * z * sz
        dz_ref[...] = (dhh * y * (sz * (1.0 + z * (1.0 - sz)))).astype(dz_ref.dtype)
        dym = dym_ref[...]
        lane = lax.broadcasted_iota(jnp.int32, (Q, 128), 1)
        dbs = jnp.zeros((Q, 128), F32)
        gu = _gelu_grad(u, tu)
        for n in range(TL // Q):
            rs_ = slice(n * Q, (n + 1) * Q)
            for g in range(8):
                cs = slice(g * 128, (g + 1) * 128)
                vb = vn[rs_, cs]
                mixed = _dot(ws_ref[g], vb, NN) + bs_ref[:, g:g + 1]
                dyb = dym[rs_, cs]
                dmx = dyb * ug[rs_, cs]
                du_ref[rs_, cs] = (dyb * mixed * gu[rs_, cs]).astype(du_ref.dtype)
                dvn_s[rs_, cs] = _dot(ws_ref[g], dmx, TN)
                dws_ref[g] += _dot(dmx, vb, NT)
                dbs = dbs + jnp.where(lane == g, jnp.sum(dmx, axis=1, keepdims=True), 0.0)
        dbs_ref[...] += dbs
        dvn = dvn_s[...]
        st_ref[1:2, :] += _colsum(dvn * vhat)
        st_ref[2:3, :] += _colsum(dvn)
        dvg = _ln_bwd(dvn * gg_ref[...], vhat, vrstd)
        dv_ref[...] = (dvg * _gelu_grad(v, tv)).astype(dv_ref.dtype)

    outs = pl.pallas_call(
        body, name="mix_bwd", grid=(nt,),
        in_specs=[_rt(D), _rt(D), _rt(D), _rt(D), _rt(D, OZ // D), _rt(D, 0), _rt(D, OU // D),
                  _rt(D, OV // D), _cst((1, D)), _cst((1, D)), _cst((1, D)), _cst((1, D)),
                  _cst((8, 128, 128)), _cst((128, 128)), pl.BlockSpec(memory_space=pl.ANY)],
        out_specs=[_rt(3 * D, 0), _rt(D), _cst((8, D)),
                   _cst((8, 128, 128)), _cst((128, 128))],
        out_shape=[jax.ShapeDtypeStruct(dp.shape, dp.dtype),
                   jax.ShapeDtypeStruct((L, D), F32), jax.ShapeDtypeStruct((8, D), F32),
                   jax.ShapeDtypeStruct((8, 128, 128), F32), jax.ShapeDtypeStruct((128, 128), F32)],
        scratch_shapes=[pltpu.VMEM((TL, D), F32)],
        input_output_aliases={14: 0},
        compiler_params=_params(("arbitrary",)),
    )(dys, dym, yf, yb, p, xbc, p, p, dsk, sg, gg, gb, ws, bsT, dp)
    return outs


def _gate_fwd(a1, a2, p, bg):
    L = a1.shape[0]

    def body(a1_ref, a2_ref, g_ref, bg_ref, m_ref):
        gt = _sig(g_ref[...] + bg_ref[...])
        m_ref[...] = (gt[:, :D] * a1_ref[...] + gt[:, D:] * a2_ref[...]).astype(m_ref.dtype)

    return pl.pallas_call(
        body, name="gate_fwd", grid=(L // TL,),
        in_specs=[_rt(D), _rt(D), _rt(2 * D, OG // (2 * D)), _cst((1, 2 * D))],
        out_specs=_rt(D), out_shape=jax.ShapeDtypeStruct((L, D), _MXU),
        compiler_params=_params(("parallel",)),
    )(a1, a2, p, bg)


def _gate_bwd(dmg, a1, a2, p, bg, dp):
    L = a1.shape[0]

    def body(dm_ref, a1_ref, a2_ref, g_ref, bg_ref, dp_any, dg_ref, da1_ref, da2_ref, st_ref):
        del dp_any

        @pl.when(pl.program_id(0) == 0)
        def _():
            st_ref[...] = jnp.zeros_like(st_ref)

        gt = _sig(g_ref[...] + bg_ref[...])
        g1 = gt[:, :D]
        g2 = gt[:, D:]
        dm = dm_ref[...]
        da1_ref[...] = (dm * g1).astype(da1_ref.dtype)
        da2_ref[...] = (dm * g2).astype(da2_ref.dtype)
        dg1 = dm * a1_ref[...] * g1 * (1.0 - g1)
        dg2 = dm * a2_ref[...] * g2 * (1.0 - g2)
        st_ref[0:1, 0:D] += _colsum(dg1)
        st_ref[0:1, D:2 * D] += _colsum(dg2)
        dg_ref[:, 0:D] = dg1.astype(dg_ref.dtype)
        dg_ref[:, D:2 * D] = dg2.astype(dg_ref.dtype)

    return pl.pallas_call(
        body, name="gate_bwd", grid=(L // TL,),
        in_specs=[_rt(D), _rt(D), _rt(D), _rt(2 * D, OG // (2 * D)), _cst((1, 2 * D)),
                  pl.BlockSpec(memory_space=pl.ANY)],
        out_specs=[_rt(2 * D, OG // (2 * D)), _rt(D), _rt(D), _cst((8, 2 * D))],
        out_shape=[jax.ShapeDtypeStruct(dp.shape, dp.dtype), jax.ShapeDtypeStruct((L, D), _MXU),
                   jax.ShapeDtypeStruct((L, D), _MXU), jax.ShapeDtypeStruct((8, 2 * D), F32)],
        input_output_aliases={5: 0},
        compiler_params=_params(("arbitrary",)),
    )(dmg, a1, a2, p, bg, dp)


def _res1_fwd(xn, out, modx, g, b):
    L = out.shape[0]

    def body(xn_ref, o_ref, mx_ref, g_ref, b_ref, r1_ref, h2_ref):
        r1 = ALPHA * xn_ref[...] + mx_ref[2:3, :] * o_ref[...]
        xhat, _ = _ln(r1)
        x1 = xhat * g_ref[...] + b_ref[...]
        r1_ref[...] = r1
        h2_ref[...] = (x1 * (1.0 + mx_ref[4:5, :]) + mx_ref[3:4, :]).astype(h2_ref.dtype)

    return pl.pallas_call(
        body, name="res1_fwd", grid=(L // TL,),
        in_specs=[_rt(D), _rt(D), _cst((8, D)), _cst((1, D)), _cst((1, D))],
        out_specs=[_rt(D), _rt(D)],
        out_shape=[jax.ShapeDtypeStruct((L, D), F32), jax.ShapeDtypeStruct((L, D), _MXU)],
        compiler_params=_params(("parallel",)),
    )(xn, out, modx, g, b)


def _glu_fwd(f13):
    L = f13.shape[0]

    def body(f1_ref, f3_ref, o_ref):
        f1 = f1_ref[...]
        o_ref[...] = (f1 * _sig(f1) * f3_ref[...]).astype(o_ref.dtype)

    return pl.pallas_call(
        body, name="glu_fwd", grid=(L // TL,),
        in_specs=[_rt(DFF, 0), _rt(DFF, 1)], out_specs=_rt(DFF),
        out_shape=jax.ShapeDtypeStruct((L, DFF), _MXU),
        compiler_params=_params(("parallel",)),
    )(f13, f13)


def _glu_bwd(dff, f13):
    L = f13.shape[0]

    def body(d_ref, f1_ref, f3_ref, o_ref):
        f1 = f1_ref[...]
        s = _sig(f1)
        d = d_ref[...]
        o_ref[:, 0:DFF] = (d * f3_ref[...] * (s * (1.0 + f1 * (1.0 - s)))).astype(o_ref.dtype)
        o_ref[:, DFF:2 * DFF] = (d * f1 * s).astype(o_ref.dtype)

    return pl.pallas_call(
        body, name="glu_bwd", grid=(L // TL,),
        in_specs=[_rt(DFF), _rt(DFF, 0), _rt(DFF, 1)], out_specs=_rt(2 * DFF),
        out_shape=jax.ShapeDtypeStruct((L, 2 * DFF), _MXU),
        compiler_params=_params(("parallel",)),
    )(dff, f13, f13)


def _res2(r1, o2, tgt, modx, g1, b1, g2, b2):
    L = r1.shape[0]

    def body(r1_ref, o2_ref, t_ref, mx_ref, g1_ref, b1_ref, g2_ref, b2_ref,
             dr2_ref, do2_ref, st_ref, loss_ref):
        @pl.when(pl.program_id(0) == 0)
        def _():
            st_ref[...] = jnp.zeros_like(st_ref)
            loss_ref[...] = jnp.zeros_like(loss_ref)

        xh1, _ = _ln(r1_ref[...])
        x1 = xh1 * g1_ref[...] + b1_ref[...]
        o2 = o2_ref[...]
        g2x = mx_ref[5:6, :]
        xh2, rstd2 = _ln(ALPHA * x1 + g2x * o2)
        err = xh2 * g2_ref[...] + b2_ref[...] - t_ref[...]
        per_tok = jnp.mean(err * err, axis=-1, keepdims=True)
        loss_ref[...] += 0.5 * jnp.sum(per_tok, axis=0, keepdims=True)
        dy = err * (1.0 / D)
        st_ref[0:1, :] += _colsum(dy * xh2)
        st_ref[1:2, :] += _colsum(dy)
        dr2 = _ln_bwd(dy * g2_ref[...], xh2, rstd2)
        st_ref[2:3, :] += _colsum(dr2 * o2)
        dr2_ref[...] = dr2
        do2_ref[...] = (g2x * dr2).astype(do2_ref.dtype)

    return pl.pallas_call(
        body, name="res2", grid=(L // TL,),
        in_specs=[_rt(D), _rt(D), _rt(D), _cst((8, D))] + [_cst((1, D))] * 4,
        out_specs=[_rt(D), _rt(D), _cst((8, D)), _cst((8, 128))],
        out_shape=[jax.ShapeDtypeStruct((L, D), F32), jax.ShapeDtypeStruct((L, D), _MXU),
                   jax.ShapeDtypeStruct((8, D), F32), jax.ShapeDtypeStruct((8, 128), F32)],
        compiler_params=_params(("arbitrary",)),
    )(r1, o2, tgt, modx, g1, b1, g2, b2)


def _res1_bwd(dr2, dh2, r1, out, modx, g1, b1):
    L = r1.shape[0]

    def body(dr2_ref, dh2_ref, r1_ref, o_ref, mx_ref, g_ref, b_ref, dr1_ref, do_ref, st_ref):
        @pl.when(pl.program_id(0) == 0)
        def _():
            st_ref[...] = jnp.zeros_like(st_ref)

        xh1, rstd1 = _ln(r1_ref[...])
        x1 = xh1 * g_ref[...] + b_ref[...]
        dh2 = dh2_ref[...]
        dx1 = ALPHA * dr2_ref[...] + dh2 * (1.0 + mx_ref[4:5, :])
        st_ref[0:1, :] += _colsum(dh2 * x1)
        st_ref[1:2, :] += _colsum(dh2)
        st_ref[2:3, :] += _colsum(dx1 * xh1)
        st_ref[3:4, :] += _colsum(dx1)
        dr1 = _ln_bwd(dx1 * g_ref[...], xh1, rstd1)
        st_ref[4:5, :] += _colsum(dr1 * o_ref[...])
        dr1_ref[...] = dr1
        do_ref[...] = (mx_ref[2:3, :] * dr1).astype(do_ref.dtype)

    return pl.pallas_call(
        body, name="res1_bwd", grid=(L // TL,),
        in_specs=[_rt(D), _rt(D), _rt(D), _rt(D), _cst((8, D)), _cst((1, D)), _cst((1, D))],
        out_specs=[_rt(D), _rt(D), _cst((8, D))],
        out_shape=[jax.ShapeDtypeStruct((L, D), F32), jax.ShapeDtypeStruct((L, D), _MXU),
                   jax.ShapeDtypeStruct((8, D), F32)],
        compiler_params=_params(("arbitrary",)),
    )(dr2, dh2, r1, out, modx, g1, b1)


def _conv_bwd(dxf, dxb, p, conv_w8, conv_b, dp):
    RT = p.shape[0]
    chunks = _seq_chunks(RT - TL)

    def body(df_ref, db_ref, p_ref, w_ref, b_ref, dp_any, o_ref, dw_ref, dbias_ref, dpre_s):
        del dp_any
        w = w_ref[...]
        bias = b_ref[...]
        srow = lax.broadcasted_iota(jnp.int32, (8, 128), 0)
        dwacc = jnp.zeros((8, 128), F32)
        dbacc = jnp.zeros((1, 128), F32)
        for r0, first, last in chunks:
            taps = _conv_taps(p_ref, r0, first, last)
            pre = bias + sum(w[k:k + 1, :] * taps[k] for k in range(5))
            s = _sig(pre)
            dpre = (df_ref[pl.ds(r0, TL), :] + db_ref[pl.ds(r0, TL), :]) * (s * (1.0 + pre * (1.0 - s)))
            dpre_s[pl.ds(r0, TL), :] = dpre
            dbacc = dbacc + _colsum(dpre)
            for k in range(5):
                dwacc = dwacc + jnp.where(srow == k, _colsum(dpre * taps[k]), 0.0)
        for r0, first, last in chunks:
            taps = _conv_taps(dpre_s, r0, first, last)
            dx = sum(w[k:k + 1, :] * taps[4 - k] for k in range(5))
            o_ref[pl.ds(r0, TL), :] = dx.astype(o_ref.dtype)
        dw_ref[...] = dwacc
        dbias_ref[...] = jnp.broadcast_to(dbacc, (8, 128))

    cspec = pl.BlockSpec((RT, 128), lambda j: (0, j))
    wspec = pl.BlockSpec((8, 128), lambda j: (0, j))
    return pl.pallas_call(
        body, name="conv_bwd", grid=(12,),
        in_specs=[cspec, cspec, pl.BlockSpec((RT, 128), lambda j: (0, _xbc_colblk(j))),
                  wspec, pl.BlockSpec((1, 128), lambda j: (0, j)), pl.BlockSpec(memory_space=pl.ANY)],
        out_specs=[pl.BlockSpec((RT, 128), lambda j: (0, _xbc_colblk(j))), wspec, wspec],
        out_shape=[jax.ShapeDtypeStruct(dp.shape, dp.dtype), jax.ShapeDtypeStruct((8, 1536), F32),
                   jax.ShapeDtypeStruct((8, 1536), F32)],
        scratch_shapes=[pltpu.VMEM((RT, 128), F32)],
        input_output_aliases={5: 0},
        compiler_params=_params(("parallel",)),
    )(dxf, dxb, p, conv_w8, conv_b, dp)


def _dt_bwd(ddf, ddb, dp):
    RT = ddf.shape[0]

    def body(f_ref, b_ref, dp_any, o_ref, st_ref):
        del dp_any

        @pl.when(pl.program_id(0) == 0)
        def _():
            st_ref[...] = jnp.zeros_like(st_ref)

        s = f_ref[...] + b_ref[...]
        o_ref[...] = s.astype(o_ref.dtype)
        st_ref[0:1, :] += _colsum(s)

    return pl.pallas_call(
        body, name="dt_bwd", grid=(RT // TL,),
        in_specs=[_rt(128), _rt(128), pl.BlockSpec(memory_space=pl.ANY)],
        out_specs=[_rt(128, ODT // 128), _cst((8, 128))],
        out_shape=[jax.ShapeDtypeStruct(dp.shape, dp.dtype), jax.ShapeDtypeStruct((8, 128), F32)],
        input_output_aliases={2: 0},
        compiler_params=_params(("arbitrary",)),
    )(ddf, ddb, dp)


def _ln0_bwd(dh1, dr1, x, ctx, g, b, modx, modc):
    L = x.shape[0]
    nt = L // TL

    def body(dh_ref, dr1_ref, x_ref, c_ref, g_ref, b_ref, mx_ref, mc_ref, gx_ref, st_ref):
        i = pl.program_id(0)
        isc = i == nt

        @pl.when(i == 0)
        def _():
            st_ref[...] = jnp.zeros_like(st_ref)

        xin = jnp.where(isc, c_ref[...], x_ref[...])
        xhat, rstd = _ln(xin)
        xn = xhat * g_ref[...] + b_ref[...]
        sc = jnp.where(isc, mc_ref[1:2, :], mx_ref[1:2, :])
        dh = dh_ref[...]
        lat = jnp.where(isc, 0.0, 1.0)
        dxn = dh * (1.0 + sc) + (lat * ALPHA) * dr1_ref[...]
        tsh = _colsum(dh)
        tsc = _colsum(dh * xn)
        st_ref[0:1, :] += lat * tsh
        st_ref[1:2, :] += lat * tsc
        st_ref[2:3, :] += (1.0 - lat) * tsh
        st_ref[3:4, :] += (1.0 - lat) * tsc
        st_ref[4:5, :] += _colsum(dxn * xhat)
        st_ref[5:6, :] += _colsum(dxn)

        @pl.when(i < nt)
        def _():
            gx_ref[...] = _ln_bwd(dxn * g_ref[...], xhat, rstd)

    return pl.pallas_call(
        body, name="ln0_bwd", grid=(nt + 1,),
        in_specs=[_rt(D), _rtc(D, nt), _rtc(D, nt), _cst((TL, D)), _cst((1, D)), _cst((1, D)),
                  _cst((8, D)), _cst((8, D))],
        out_specs=[_rtc(D, nt), _cst((8, D))],
        out_shape=[jax.ShapeDtypeStruct((L, D), F32), jax.ShapeDtypeStruct((8, D), F32)],
        compiler_params=_params(("arbitrary",)),
    )(dh1, dr1, x, ctx, g, b, modx, modc)


def _perm_cols(w):
    pad = jnp.zeros((w.shape[0], NPJ - NNAT), w.dtype)
    return jnp.concatenate([w[:, 0:1024], w[:, 2592:3616], w[:, 3616:4640], w[:, 1024:2048],
                            w[:, 4640:6688], w[:, 2048:2304], w[:, 2304:2560], w[:, 2560:2592], pad],
                           axis=1)


SECTIONS = ((0, 1024, OZ), (1024, 2048, OXS), (2048, 2304, OB), (2304, 2560, OC), (2560, 2592, ODT),
            (2592, 3616, OU), (3616, 4640, OV), (4640, 6688, OG))


def _perm_from_blocks(ga):
    n = ga.shape[2]
    pieces = []
    for na, nb, _ in sorted(SECTIONS, key=lambda sec: sec[2]):
        for k in range(NDEV):
            lo, hi = max(na, k * n), min(nb, (k + 1) * n)
            if lo < hi:
                pieces.append(ga[k][:, lo - k * n:hi - k * n])
    pieces.append(jnp.zeros((ga.shape[1], NPJ - NNAT), ga.dtype))
    return jnp.concatenate(pieces, axis=1)


def _blocks_from_perm(gp, n):
    blocks = []
    for k in range(NDEV):
        pieces = []
        for na, nb, po in SECTIONS:
            lo, hi = max(na, k * n), min(nb, (k + 1) * n)
            if lo < hi:
                pieces.append(gp[:, po + lo - na:po + hi - na])
        blocks.append(jnp.concatenate(pieces, axis=1))
    return jnp.stack(blocks)


def _padded(n, row_align):
    unit = row_align * D
    return -(-n // unit) * unit if row_align else n


def _slab(arrs, rows, row_align=0):
    parts = []
    for a in arrs:
        f = a.reshape(-1)
        parts.append(jnp.pad(f, (0, _padded(f.shape[0], row_align) - f.shape[0])))
    flat = jnp.concatenate(parts)
    flat = jnp.pad(flat, (0, rows * D - flat.shape[0]))
    return flat.reshape(rows, D)


def _unslab(slab, shapes, row_align=0):
    flat = slab.reshape(-1)
    out, off = [], 0
    for shp in shapes:
        n = 1
        for s in shp:
            n *= s
        out.append(flat[off:off + n].reshape(shp))
        off += _padded(n, row_align)
    return out


def _row(v):
    return v.reshape(1, -1)


def _pad_rows(a, rows):
    return jnp.pad(a, ((0, rows - a.shape[0]), (0, 0)))


BIG = ["w_in", "w_ssd_proj", "w_gm_proj", "w_out", "w_ff1", "w_ff3", "w_ff2"]
BIG_ROWS = 2304
BIG_ALIGN = 16
REPL = ["c_ctx", "ln0_g", "ln0_b", "b_ada", "conv_b", "dt_bias", "a_log", "d_skip", "ssd_norm_g",
        "gm_norm_g", "gm_norm_b", "w_spatial", "b_spatial", "b_gate", "ln1_g", "ln1_b", "ln2_g", "ln2_b"]
SMALL_ROWS = 160
WEIGHTS = ["c_ctx", "ln0_g", "ln0_b", "w_ada", "b_ada", "w_in", "conv_w", "conv_b", "dt_bias", "a_log",
           "d_skip", "ssd_norm_g", "gm_norm_g", "gm_norm_b", "w_spatial", "b_spatial", "b_gate",
           "w_ssd_proj", "w_gm_proj", "w_out", "ln1_g", "ln1_b", "w_ff1", "w_ff3", "w_ff2", "ln2_g", "ln2_b"]


def kernel(x, c, ctx, c_ctx, ln0_g, ln0_b, w_ada, b_ada, w_in, conv_w, conv_b, dt_bias, a_log, d_skip, ssd_norm_g, gm_norm_g, gm_norm_b, w_spatial, b_spatial, b_gate, w_ssd_proj, w_gm_proj, w_out, ln1_g, ln1_b, w_ff1, w_ff3, w_ff2, ln2_g, ln2_b, loss_target, m_c_ctx, m_ln0_g, m_ln0_b, m_w_ada, m_b_ada, m_w_in, m_conv_w, m_conv_b, m_dt_bias, m_a_log, m_d_skip, m_ssd_norm_g, m_gm_norm_g, m_gm_norm_b, m_w_spatial, m_b_spatial, m_b_gate, m_w_ssd_proj, m_w_gm_proj, m_w_out, m_ln1_g, m_ln1_b, m_w_ff1, m_w_ff3, m_w_ff2, m_ln2_g, m_ln2_b, v_c_ctx, v_ln0_g, v_ln0_b, v_w_ada, v_b_ada, v_w_in, v_conv_w, v_conv_b, v_dt_bias, v_a_log, v_d_skip, v_ssd_norm_g, v_gm_norm_g, v_gm_norm_b, v_w_spatial, v_b_spatial, v_b_gate, v_w_ssd_proj, v_w_gm_proj, v_w_out, v_ln1_g, v_ln1_b, v_w_ff1, v_w_ff3, v_w_ff2, v_ln2_g, v_ln2_b):
    W = dict(c_ctx=c_ctx, ln0_g=ln0_g, ln0_b=ln0_b, w_ada=w_ada, b_ada=b_ada, w_in=w_in, conv_w=conv_w,
             conv_b=conv_b, dt_bias=dt_bias, a_log=a_log, d_skip=d_skip, ssd_norm_g=ssd_norm_g,
             gm_norm_g=gm_norm_g, gm_norm_b=gm_norm_b, w_spatial=w_spatial, b_spatial=b_spatial,
             b_gate=b_gate, w_ssd_proj=w_ssd_proj, w_gm_proj=w_gm_proj, w_out=w_out, ln1_g=ln1_g,
             ln1_b=ln1_b, w_ff1=w_ff1, w_ff3=w_ff3, w_ff2=w_ff2, ln2_g=ln2_g, ln2_b=ln2_b)
    M = dict(c_ctx=m_c_ctx, ln0_g=m_ln0_g, ln0_b=m_ln0_b, w_ada=m_w_ada, b_ada=m_b_ada, w_in=m_w_in,
             conv_w=m_conv_w, conv_b=m_conv_b, dt_bias=m_dt_bias, a_log=m_a_log, d_skip=m_d_skip,
             ssd_norm_g=m_ssd_norm_g, gm_norm_g=m_gm_norm_g, gm_norm_b=m_gm_norm_b,
             w_spatial=m_w_spatial, b_spatial=m_b_spatial, b_gate=m_b_gate, w_ssd_proj=m_w_ssd_proj,
             w_gm_proj=m_w_gm_proj, w_out=m_w_out, ln1_g=m_ln1_g, ln1_b=m_ln1_b, w_ff1=m_w_ff1,
             w_ff3=m_w_ff3, w_ff2=m_w_ff2, ln2_g=m_ln2_g, ln2_b=m_ln2_b)
    V = dict(c_ctx=v_c_ctx, ln0_g=v_ln0_g, ln0_b=v_ln0_b, w_ada=v_w_ada, b_ada=v_b_ada, w_in=v_w_in,
             conv_w=v_conv_w, conv_b=v_conv_b, dt_bias=v_dt_bias, a_log=v_a_log, d_skip=v_d_skip,
             ssd_norm_g=v_ssd_norm_g, gm_norm_g=v_gm_norm_g, gm_norm_b=v_gm_norm_b,
             w_spatial=v_w_spatial, b_spatial=v_b_spatial, b_gate=v_b_gate, w_ssd_proj=v_w_ssd_proj,
             w_gm_proj=v_w_gm_proj, w_out=v_w_out, ln1_g=v_ln1_g, ln1_b=v_ln1_b, w_ff1=v_w_ff1,
             w_ff3=v_w_ff3, w_ff2=v_w_ff2, ln2_g=v_ln2_g, ln2_b=v_ln2_b)

    me = 4 * lax.axis_index("x") + 2 * lax.axis_index("y") + lax.axis_index("c")
    xl, cx, tgt = x[0], ctx[0], loss_target[0]
    L = xl.shape[0]
    assert cx.shape[0] == TL and L % TL == 0
    ada_n = w_ada.shape[2]
    cw_n = conv_w.shape[2]

    small1 = _pad_rows(jnp.concatenate([c, _slab([conv_w[0]], 1)], axis=0), 8)
    g1 = _all_gather(small1, "ag_small")
    c_all = g1[:, 0, :]
    conv_w_full = g1[:, 1, :5 * cw_n].reshape(NDEV, 5, cw_n).transpose(1, 0, 2).reshape(5, NDEV * cw_n)
    sq = w_ssd_proj.shape[1]
    ffr = w_ff2.shape[1]
    ffc = w_ff1.shape[2]
    late = [jnp.concatenate([w_ssd_proj[0], w_gm_proj[0], w_out[0], w_ff2[0]], axis=0).astype(_MXU),
            w_ff1[0].astype(_MXU), w_ff3[0].astype(_MXU)]

    c16 = _pad_rows(jnp.concatenate([c_all, _row(c_ctx)], axis=0), 16)
    b_ada_sh = lax.dynamic_slice(b_ada, (0, ada_n * me), (1, ada_n))
    modp = _ada_fwd(c16, w_ada[0], b_ada_sh)
    mod16 = _all_gather(modp, "ag_mod").transpose(1, 0, 2).reshape(16, NDEV * ada_n)

    ga, = _all_gather_multi([w_in[0].astype(_MXU)], "ag_w_in")
    ga, late, mod16 = lax.optimization_barrier((ga, late, mod16))
    lw_send, lw_recv, lw_src, lw_land, lw_token = _exchange_start(late, "ag_late_start", gather=True)
    w_in_p = _perm_from_blocks(ga)
    modx = _pad_rows(lax.dynamic_slice(mod16, (me, 0), (1, 6 * D)).reshape(6, D), 8) + lw_token[0, 0]
    modc = _pad_rows(mod16[8].reshape(6, D), 8)

    g0, b0 = _row(ln0_g), _row(ln0_b)
    xn, h1 = _ln0_fwd(xl, cx, g0, b0, modx, modc)
    p = _mm(h1, w_in_p, "nn", F32, "mm_p")
    conv_w8 = _pad_rows(conv_w_full, 8)
    xbc = _conv_fwd(p, conv_w8, conv_b)
    prm = _pad_rows(jnp.pad(jnp.stack([dt_bias.reshape(32), a_log.reshape(32)]), ((0, 0), (0, 96))), 8)
    yf, yb, hpf, hpb = _ssd2_fwd(xbc, p, prm)
    lw_land = _exchange_wait(lw_send, lw_recv, lw_src, lw_land, yf, "ag_late_wait", gather=True)
    fw_send, fw_recv, lw_land, fw_token = _forward_start(lw_land, "ag_fwd_start")
    dsk = _row(jnp.repeat(d_skip[0, 0] + d_skip[0, 1], HP)) + fw_token[0:1, 0:1]
    ws_m = w_spatial[0].astype(_MXU)
    bsT = jnp.pad(b_spatial[0].T, ((0, 0), (0, 120)))
    mixp = (dsk, ssd_norm_g, gm_norm_g, gm_norm_b, ws_m, bsT)
    yssd, ygm = _mix_fwd(yf, yb, p, xbc, *mixp)
    gb, gc1, gc2 = _forward_wait(fw_send, fw_recv, lw_land, yssd, "ag_fwd_wait")

    def with_own(g, mine, k):
        return jnp.where(me == k, mine, g[k])

    gb = jnp.stack([with_own(gb, late[0], k) for k in range(NDEV)])
    w_ssd_f = gb[:, 0:sq].reshape(NDEV * sq, D)
    w_gm_f = gb[:, sq:2 * sq].reshape(NDEV * sq, D)
    w_out_f = gb[:, 2 * sq:3 * sq].reshape(NDEV * sq, D)
    w_ff2_f = gb[:, 3 * sq:3 * sq + ffr].reshape(NDEV * ffr, D)
    w13 = jnp.concatenate([with_own(gc1, late[1], k) for k in range(NDEV)]
                          + [with_own(gc2, late[2], k) for k in range(NDEV)], axis=1)
    a1 = _mm(yssd, w_ssd_f, "nn", F32, "mm_a1")
    a2 = _mm(ygm, w_gm_f, "nn", F32, "mm_a2")
    merged = _gate_fwd(a1, a2, p, b_gate)
    out = _mm(merged, w_out_f, "nn", F32, "mm_out")
    r1, h2 = _res1_fwd(xn, out, modx, ln1_g, ln1_b)
    f13 = _mm(h2, w13, "nn", F32, "mm_f13")
    ff = _glu_fwd(f13)
    o2 = _mm(ff, w_ff2_f, "nn", F32, "mm_o2")

    dr2, do2, st2, loss_slab = _res2(r1, o2, tgt, modx, ln1_g, ln1_b, ln2_g, ln2_b)
    loss = lax.psum(loss_slab[0, 0], ("x", "y", "c"))
    dff = _mm(do2, w_ff2_f, "nt", F32, "mm_dff")
    df13 = _glu_bwd(dff, f13)
    dh2 = _mm(df13, w13, "nt", F32, "mm_dh2")
    dw_ff2 = _mm(ff, do2, "tn", _MXU, "mm_dw_ff2")
    dw13 = _mm(h2, df13, "tn", _MXU, "mm_dw13")
    xff = [dw_ff2.reshape(NDEV, ffr, D),
           jnp.stack([dw13[:, k * ffc:(k + 1) * ffc] for k in range(NDEV)]),
           jnp.stack([dw13[:, DFF + k * ffc:DFF + (k + 1) * ffc] for k in range(NDEV)])]
    ff_send, ff_recv, ff_src, ff_land, ff_token = _exchange_start(xff, "xchg_ff_start")
    modx = modx + ff_token[0, 0]
    dr1, dout, st1 = _res1_bwd(dr2, dh2, r1, out, modx, ln1_g, ln1_b)
    dmg = _mm(dout, w_out_f, "nt", F32, "mm_dmerged")
    dw_out = _mm(merged, dout, "tn", _MXU, "mm_dw_out")
    dp = jnp.zeros((L + TL, NPJ), _MXU)
    dp, da1, da2, stg = _gate_bwd(dmg, a1, a2, p, b_gate, dp)
    dys = _mm(da1, w_ssd_f, "nt", F32, "mm_dyssd")
    dym = _mm(da2, w_gm_f, "nt", F32, "mm_dygm")
    dw_ssd = _mm(yssd, da1, "tn", _MXU, "mm_dw_ssd")
    dw_gm = _mm(ygm, da2, "tn", _MXU, "mm_dw_gm")
    xsq = [jnp.concatenate([dw_ssd.reshape(NDEV, sq, D), dw_gm.reshape(NDEV, sq, D),
                            dw_out.reshape(NDEV, sq, D)], axis=1)]
    sq_send, sq_recv, sq_src, sq_land, sq_token = _exchange_start(xsq, "xchg_sq_start")
    mixp = (dsk + sq_token[0:1, 0:1],) + mixp[1:]
    dp, dyd, stm, dws, dbsT = _mix_bwd(dys, dym, yf, yb, p, xbc, dp, *mixp)
    dxf, dxb, ddf, ddb, sts = _ssd2_bwd(xbc, p, prm, dsk, dyd, hpf, hpb)
    dp, dcw, dcb = _conv_bwd(dxf, dxb, p, conv_w8, conv_b, dp)
    dp, std = _dt_bwd(ddf, ddb, dp)
    dh1 = _mm(dp, w_in_p, "nt", F32, "mm_dh1")
    dw_in_p = _mm(h1, dp, "tn", _MXU, "mm_dw_in")
    xin = [_blocks_from_perm(dw_in_p, w_in.shape[2])]
    in_send, in_recv, in_src, in_land, in_token = _exchange_start(xin, "xchg_in_start")
    modx = modx + in_token[0, 0]
    grad_x, st0 = _ln0_bwd(dh1, dr1, xl, cx, g0, b0, modx, modc)

    zero = jnp.zeros((D,), F32)
    dmod = jnp.stack([jnp.concatenate([st0[0], st0[1], st1[4], st1[1], st1[0], st2[2]]),
                      jnp.concatenate([st0[2], st0[3], zero, zero, zero, zero])])
    g16 = _all_gather(_pad_rows(dmod, 8), "ag_dmod")[:, 0:2, :].reshape(16, 6 * D)
    g16_sh = lax.dynamic_slice(g16, (0, ada_n * me), (16, ada_n))
    c16b = jnp.stack([c_all, jnp.broadcast_to(_row(c_ctx), (NDEV, D))], axis=1).reshape(16, D)
    dw_ada, db_ada8, dcc8 = _ada_bwd(c16b, g16, g16_sh, w_ada[0])

    part = dict(
        c_ctx=dcc8[0], ln0_g=st0[4], ln0_b=st0[5], conv_w=dcw[0:5], conv_b=dcb[0],
        dt_bias=std[0, 0:32], a_log=sts[0, 0:32], d_skip=jnp.tile(sts[1, 0:16], 2),
        ssd_norm_g=stm[0], gm_norm_g=stm[1], gm_norm_b=stm[2], w_spatial=dws,
        b_spatial=dbsT[:, 0:8].T, b_gate=stg[0], ln1_g=st1[2], ln1_b=st1[3], ln2_g=st2[0], ln2_b=st2[1])
    pnames = list(part)
    psum8 = _sum8(_all_gather(_slab([part[n] for n in pnames], SMALL_ROWS), "ag_smallgrads"), "sum_smallgrads")
    small = dict(zip(pnames, _unslab(psum8, [part[n].shape for n in pnames])))
    grads = {n: small[n].reshape(W[n].shape) for n in pnames if n != "conv_w"}
    grads["conv_w"] = lax.dynamic_slice(small["conv_w"], (0, cw_n * me), (5, cw_n)).reshape(conv_w.shape)
    grads["b_ada"] = db_ada8[0:1]
    grads["w_ada"] = dw_ada.reshape(w_ada.shape)

    delta, new_m, new_v = {}, {}, {}

    def adam_group(names, rows, tag, align=0):
        shapes = [W[n].shape for n in names]
        outs = _adamw(*[_slab([src[n] for n in names], rows, align) for src in (grads, W, M, V)], tag)
        for res, slab in zip((delta, new_m, new_v), outs):
            for n, a in zip(names, _unslab(slab, shapes, align)):
                res[n] = a

    adam_group(["w_ada", "conv_w"], ada_n + 256, "adamw_shard")
    adam_group(REPL, SMALL_ROWS, "adamw_repl")

    rff = _exchange_wait(ff_send, ff_recv, ff_src, ff_land, st0, "xchg_ff_wait")
    rsq = _exchange_wait(sq_send, sq_recv, sq_src, sq_land, rff[0], "xchg_sq_wait")
    rin = _exchange_wait(in_send, in_recv, in_src, in_land, delta["ln2_b"], "xchg_in_wait")

    def own(blocks):
        return lax.dynamic_index_in_dim(blocks, me, 0, keepdims=False)

    for n, r8, mine, row0, tr in (
            ("w_ff2", rff[0], own(xff[0]), 0, ffr // 2), ("w_ff1", rff[1], own(xff[1]), 0, 256),
            ("w_ff3", rff[2], own(xff[2]), 0, 256), ("w_ssd_proj", rsq[0], own(xsq[0]), 0, sq),
            ("w_gm_proj", rsq[0], own(xsq[0]), sq, sq), ("w_out", rsq[0], own(xsq[0]), 2 * sq, sq),
            ("w_in", rin[0], own(xin[0]), 0, 256)):
        res = _adamw_sum(r8, mine, W[n][0], M[n][0], V[n][0], row0, tr, "adamw_" + n)
        grads[n], delta[n], new_m[n], new_v[n] = [a[None] for a in res]

    return (loss, grad_x[None], *[grads[n] for n in WEIGHTS], *[delta[n] for n in WEIGHTS],
            *[new_m[n] for n in WEIGHTS], *[new_v[n] for n in WEIGHTS])
```

```python
import functools

import jax
import jax.numpy as jnp
from jax import lax
from jax.experimental import pallas as pl
from jax.experimental.pallas import tpu as pltpu

_MXU = jnp.bfloat16
F32 = jnp.float32
D = 1024
TL = 256
Q = 128
NH, HP, NS, HPG = 16, 64, 128, 8
DFF = 2816
ALPHA = 2.0 ** 0.25
EPS = 1e-5
OZ, OU, OV, OXS, OG, OB, OC, ODT, NPJ = 0, 1024, 2048, 3072, 4096, 6144, 6400, 6656, 6912
NNAT = 6688
NDEV = 8
ADAM_LR, ADAM_B1, ADAM_B2, ADAM_EPS, ADAM_WD, ADAM_STEP = 1e-3, 0.9, 0.999, 1e-8, 0.01, 10
VMEM_LIMIT = 48 * 1024 * 1024

NN = ((1,), (0,))
NT = ((1,), (1,))
TN = ((0,), (0,))
MESH = pl.DeviceIdType.MESH


def _dot(a, b, dims):
    return lax.dot_general(a.astype(_MXU), b.astype(_MXU), (dims, ((), ())),
                           preferred_element_type=F32)


def _tile(n, cands):
    for c in cands:
        if n % c == 0:
            return c
    return n


def _divisor_tile(n, cap, mult):
    best = n
    for t in range(mult, min(n, cap) + 1, mult):
        if n % t == 0:
            best = t
    return best


def _params(sem):
    return pltpu.CompilerParams(dimension_semantics=sem, vmem_limit_bytes=VMEM_LIMIT)


def _cst(shape):
    nd = len(shape)
    return pl.BlockSpec(shape, lambda *_: (0,) * nd)


def _rt(w, cb=0, rows=TL):
    return pl.BlockSpec((rows, w), lambda i: (i, cb))


def _rtc(w, nt, cb=0):
    return pl.BlockSpec((TL, w), lambda i: (jnp.minimum(i, nt - 1), cb))


def _sig(x):
    return jax.nn.sigmoid(x)


def _softplus(x):
    return jnp.maximum(x, 0.0) + jnp.log1p(jnp.exp(-jnp.abs(x)))


_G0, _G1 = 0.7978845608028654, 0.044715


def _gelu(x):
    t = jnp.tanh(_G0 * (x + _G1 * x * x * x))
    return 0.5 * x * (1.0 + t), t


def _gelu_grad(x, t):
    return 0.5 * (1.0 + t) + 0.5 * x * (1.0 - t * t) * _G0 * (1.0 + 3.0 * _G1 * x * x)


def _ln(r):
    mu = jnp.mean(r, axis=-1, keepdims=True)
    xc = r - mu
    var = jnp.mean(xc * xc, axis=-1, keepdims=True)
    rstd = lax.rsqrt(var + EPS)
    return xc * rstd, rstd


def _ln_bwd(dyh, xhat, rstd):
    return rstd * (dyh - jnp.mean(dyh, axis=-1, keepdims=True)
                   - xhat * jnp.mean(dyh * xhat, axis=-1, keepdims=True))


def _colsum(v):
    return jnp.sum(v, axis=0, keepdims=True)


def _sum11(v):
    return jnp.sum(jnp.sum(v, axis=1, keepdims=True), axis=0, keepdims=True)


def _cumsum_rows(a, rev):
    n = a.shape[0]
    row = lax.broadcasted_iota(jnp.int32, a.shape, 0)
    s = 1
    while s < n:
        if rev:
            a = a + jnp.where(row < n - s, pltpu.roll(a, n - s, 0), 0.0)
        else:
            a = a + jnp.where(row >= s, pltpu.roll(a, s, 0), 0.0)
        s *= 2
    return a


def _mm(a, b, mode, out_dtype, name):
    if mode == "tn":
        K, M = a.shape
    else:
        M, K = a.shape
    N = b.shape[0] if mode == "nt" else b.shape[1]
    tm = _divisor_tile(M, 1408, 128) if mode == "tn" else _divisor_tile(M, 1088, 16)
    tn = _divisor_tile(N, 1408, 128)
    tk = _divisor_tile(K, 2304, 128)
    nk = K // tk
    dims = {"nn": NN, "nt": NT, "tn": TN}[mode]
    use_acc = nk > 1 and out_dtype != F32

    def body(a_ref, b_ref, o_ref, *acc):
        prod = _dot(a_ref[...], b_ref[...], dims)
        if nk == 1:
            o_ref[...] = prod.astype(o_ref.dtype)
            return
        acc_ref = acc[0] if use_acc else o_ref
        k = pl.program_id(2)

        @pl.when(k == 0)
        def _():
            acc_ref[...] = prod

        if use_acc:
            @pl.when((k > 0) & (k < nk - 1))
            def _():
                acc_ref[...] += prod

            @pl.when(k == nk - 1)
            def _():
                o_ref[...] = (acc_ref[...] + prod).astype(o_ref.dtype)
        else:
            @pl.when(k > 0)
            def _():
                o_ref[...] += prod

    if mode == "tn":
        a_spec = pl.BlockSpec((tk, tm), lambda i, j, k: (k, i))
    else:
        a_spec = pl.BlockSpec((tm, tk), lambda i, j, k: (i, k))
    if mode == "nt":
        b_spec = pl.BlockSpec((tn, tk), lambda i, j, k: (j, k))
    else:
        b_spec = pl.BlockSpec((tk, tn), lambda i, j, k: (k, j))
    return pl.pallas_call(
        body, name=name, grid=(M // tm, N // tn, nk),
        in_specs=[a_spec, b_spec],
        out_specs=pl.BlockSpec((tm, tn), lambda i, j, k: (i, j)),
        out_shape=jax.ShapeDtypeStruct((M, N), out_dtype),
        scratch_shapes=[pltpu.VMEM((tm, tn), F32)] if use_acc else [],
        compiler_params=_params(("parallel", "parallel", "arbitrary")),
    )(a, b)


def _all_gather(x, name):
    def body(x_ref, out_ref, send_sems, recv_sems, local_sem):
        mx, my, mc = lax.axis_index("x"), lax.axis_index("y"), lax.axis_index("c")
        me, sibling = (mx, my, mc), (mx, my, 1 - mc)
        chips = [(1 - mx, my), (mx, 1 - my), (1 - mx, 1 - my)]

        def slot(px, py, pc):
            return out_ref.at[4 * px + 2 * py + pc]

        def copy(k, block, to, src=None):
            return pltpu.make_async_remote_copy(
                src_ref=slot(*block) if src is None else src, dst_ref=slot(*block),
                send_sem=send_sems.at[k], recv_sem=recv_sems.at[k],
                device_id=to, device_id_type=MESH)

        mine = pltpu.make_async_copy(x_ref, slot(*me), local_sem)
        mine.start()
        first = [copy(0, me, sibling, src=x_ref)]
        first += [copy(1 + j, me, (*chip, mc), src=x_ref) for j, chip in enumerate(chips)]
        for cp in first:
            cp.start()
        passed = [copy(4 + j, (*chip, mc), sibling) for j, chip in enumerate(chips)]
        for j, chip in enumerate(chips):
            copy(1 + j, (*chip, mc), me).wait_recv()
            passed[j].start()
        copy(0, sibling, me).wait_recv()
        for j, chip in enumerate(chips):
            copy(4 + j, (*chip, 1 - mc), me).wait_recv()
        for cp in first + passed:
            cp.wait_send()
        mine.wait()

    return pl.pallas_call(
        body, name=name,
        out_shape=jax.ShapeDtypeStruct((NDEV,) + x.shape, x.dtype),
        in_specs=[pl.BlockSpec(memory_space=pl.ANY)],
        out_specs=pl.BlockSpec(memory_space=pl.ANY),
        scratch_shapes=[pltpu.SemaphoreType.DMA((7,)), pltpu.SemaphoreType.DMA((7,)),
                        pltpu.SemaphoreType.DMA],
    )(x)


def _owner_exchange(g, name):
    def body(g_ref, out_ref, send_sems, recv_sems, local_sem):
        mx, my, mc = lax.axis_index("x"), lax.axis_index("y"), lax.axis_index("c")
        local = pltpu.make_async_copy(g_ref.at[4 * mx + 2 * my + mc], out_ref.at[0], local_sem)
        local.start()
        copies = []
        for f in range(1, NDEV):
            px = 1 - mx if (f >> 2) & 1 else mx
            py = 1 - my if (f >> 1) & 1 else my
            pc = 1 - mc if f & 1 else mc
            cp = pltpu.make_async_remote_copy(
                src_ref=g_ref.at[4 * px + 2 * py + pc], dst_ref=out_ref.at[f],
                send_sem=send_sems.at[f - 1], recv_sem=recv_sems.at[f - 1],
                device_id=(px, py, pc), device_id_type=MESH)
            cp.start()
            copies.append(cp)
        for cp in copies:
            cp.wait_recv()
        for cp in copies:
            cp.wait_send()
        local.wait()

    return pl.pallas_call(
        body, name=name,
        out_shape=jax.ShapeDtypeStruct(g.shape, g.dtype),
        in_specs=[pl.BlockSpec(memory_space=pl.ANY)],
        out_specs=pl.BlockSpec(memory_space=pl.ANY),
        scratch_shapes=[pltpu.SemaphoreType.DMA((7,)), pltpu.SemaphoreType.DMA((7,)),
                        pltpu.SemaphoreType.DMA],
    )(g)


def _any_specs(n):
    return [pl.BlockSpec(memory_space=pl.ANY)] * n


def _all_gather_multi(xs, name):
    na = len(xs)

    def body(*refs):
        x_refs, out_refs = refs[:na], refs[na:2 * na]
        send_sems, recv_sems, local_sems = refs[2 * na:]
        mx, my, mc = lax.axis_index("x"), lax.axis_index("y"), lax.axis_index("c")
        me, sibling = (mx, my, mc), (mx, my, 1 - mc)
        chips = [(1 - mx, my), (mx, 1 - my), (1 - mx, 1 - my)]

        def copy(a, k, block, to, src=None):
            slot = out_refs[a].at[4 * block[0] + 2 * block[1] + block[2]]
            return pltpu.make_async_remote_copy(
                src_ref=slot if src is None else src, dst_ref=slot,
                send_sem=send_sems.at[7 * a + k], recv_sem=recv_sems.at[7 * a + k],
                device_id=to, device_id_type=MESH)

        mine = [pltpu.make_async_copy(x_refs[a], out_refs[a].at[4 * mx + 2 * my + mc], local_sems.at[a])
                for a in range(na)]
        for cp in mine:
            cp.start()
        first = []
        for a in range(na):
            first.append(copy(a, 0, me, sibling, src=x_refs[a]))
            first += [copy(a, 1 + j, me, (*chip, mc), src=x_refs[a]) for j, chip in enumerate(chips)]
        for cp in first:
            cp.start()
        passed = []
        for a in range(na):
            for j, chip in enumerate(chips):
                copy(a, 1 + j, (*chip, mc), me).wait_recv()
                fwd = copy(a, 4 + j, (*chip, mc), sibling)
                fwd.start()
                passed.append(fwd)
        for a in range(na):
            copy(a, 0, sibling, me).wait_recv()
            for j, chip in enumerate(chips):
                copy(a, 4 + j, (*chip, 1 - mc), me).wait_recv()
        for cp in first + passed:
            cp.wait_send()
        for cp in mine:
            cp.wait()

    return pl.pallas_call(
        body, name=name,
        out_shape=[jax.ShapeDtypeStruct((NDEV,) + x.shape, x.dtype) for x in xs],
        in_specs=_any_specs(na), out_specs=_any_specs(na),
        scratch_shapes=[pltpu.SemaphoreType.DMA((7 * na,)), pltpu.SemaphoreType.DMA((7 * na,)),
                        pltpu.SemaphoreType.DMA((na,))],
    )(*xs)


def _owner_exchange_multi(gs, name):
    na = len(gs)

    def body(*refs):
        g_refs, out_refs = refs[:na], refs[na:2 * na]
        send_sems, recv_sems, local_sems = refs[2 * na:]
        mx, my, mc = lax.axis_index("x"), lax.axis_index("y"), lax.axis_index("c")
        locals_ = [pltpu.make_async_copy(g_refs[a].at[4 * mx + 2 * my + mc], out_refs[a].at[0], local_sems.at[a])
                   for a in range(na)]
        for cp in locals_:
            cp.start()
        copies = []
        for a in range(na):
            for f in range(1, NDEV):
                px = 1 - mx if (f >> 2) & 1 else mx
                py = 1 - my if (f >> 1) & 1 else my
                pc = 1 - mc if f & 1 else mc
                cp = pltpu.make_async_remote_copy(
                    src_ref=g_refs[a].at[4 * px + 2 * py + pc], dst_ref=out_refs[a].at[f],
                    send_sem=send_sems.at[7 * a + f - 1], recv_sem=recv_sems.at[7 * a + f - 1],
                    device_id=(px, py, pc), device_id_type=MESH)
                cp.start()
                copies.append(cp)
        for cp in copies:
            cp.wait_recv()
        for cp in copies:
            cp.wait_send()
        for cp in locals_:
            cp.wait()

    return pl.pallas_call(
        body, name=name,
        out_shape=[jax.ShapeDtypeStruct(g.shape, g.dtype) for g in gs],
        in_specs=_any_specs(na), out_specs=_any_specs(na),
        scratch_shapes=[pltpu.SemaphoreType.DMA((7 * na,)), pltpu.SemaphoreType.DMA((7 * na,)),
                        pltpu.SemaphoreType.DMA((na,))],
    )(*gs)


def _adamw_sum(r8, own, w, m, v, row0, tr, name):
    R, C = w.shape
    assert row0 % tr == 0
    blk0 = row0 // tr
    bc1 = 1.0 - ADAM_B1 ** ADAM_STEP
    bc2 = 1.0 - ADAM_B2 ** ADAM_STEP

    def body(r_ref, *refs):
        if own is None:
            gg = r_ref[0].astype(F32)
        else:
            gg = refs[0][...].astype(F32)
            refs = refs[1:]
        w_ref, m_ref, v_ref, g_ref, d_ref, mo_ref, vo_ref = refs
        for k in range(1, NDEV):
            gg = gg + r_ref[k].astype(F32)
        mn = ADAM_B1 * m_ref[...] + (1.0 - ADAM_B1) * gg
        vn = ADAM_B2 * v_ref[...] + (1.0 - ADAM_B2) * (gg * gg)
        mh = mn / bc1
        vh = vn / bc2
        g_ref[...] = gg
        d_ref[...] = -ADAM_LR * (mh / (jnp.sqrt(vh) + ADAM_EPS) + ADAM_WD * w_ref[...])
        mo_ref[...] = mn
        vo_ref[...] = vn

    spec = pl.BlockSpec((tr, C), lambda i: (i, 0))
    sh = jax.ShapeDtypeStruct((R, C), F32)
    own_ops = [] if own is None else [own]
    own_specs = [] if own is None else [pl.BlockSpec((tr, C), lambda i: (i + blk0, 0))]
    return pl.pallas_call(
        body, name=name, grid=(R // tr,),
        in_specs=[pl.BlockSpec((NDEV, tr, C), lambda i: (0, i + blk0, 0))] + own_specs + [spec, spec, spec],
        out_specs=[spec] * 4, out_shape=[sh] * 4, compiler_params=_params(("parallel",)),
    )(r8, *own_ops, w, m, v)


_HBM = pl.BlockSpec(memory_space=pltpu.HBM)
_SEM = pl.BlockSpec(memory_space=pltpu.SEMAPHORE)
_EFFECT = pltpu.SideEffectType.DATAFLOW_SIDE_EFFECTING


def _exchange_copies(g_refs, land_refs, send_sems, recv_sems, gather):
    mx, my, mc = lax.axis_index("x"), lax.axis_index("y"), lax.axis_index("c")
    copies = []
    for a in range(len(g_refs)):
        for f in ((1, 2, 4, 6) if gather else range(1, NDEV)):
            px = 1 - mx if (f >> 2) & 1 else mx
            py = 1 - my if (f >> 1) & 1 else my
            pc = 1 - mc if f & 1 else mc
            src = g_refs[a] if gather else g_refs[a].at[4 * px + 2 * py + pc]
            dst = land_refs[a].at[4 * mx + 2 * my + mc] if gather else land_refs[a].at[f]
            copies.append(pltpu.make_async_remote_copy(
                src_ref=src, dst_ref=dst,
                send_sem=send_sems.at[7 * a + f - 1], recv_sem=recv_sems.at[7 * a + f - 1],
                device_id=(px, py, pc), device_id_type=MESH))
    return copies


def _exchange_start(gs, name, gather=False):
    na = len(gs)

    def body(*refs):
        for cp in _exchange_copies(refs[:na], refs[na:2 * na], refs[2 * na], refs[2 * na + 1], gather):
            cp.start()
        refs[-1][...] = jnp.zeros_like(refs[-1])

    hbm = [pltpu.HBM(g.shape, g.dtype) for g in gs]
    land_shapes = [((NDEV,) + g.shape) if gather else g.shape for g in gs]
    lands = [pltpu.with_memory_space_constraint(lax.empty(shp, g.dtype), pltpu.HBM)
             for shp, g in zip(land_shapes, gs)]
    hbm_land = [pltpu.HBM(shp, g.dtype) for shp, g in zip(land_shapes, gs)]
    outs = pl.pallas_call(
        body, name=name,
        out_shape=(pltpu.SemaphoreType.DMA((7 * na,)), pltpu.SemaphoreType.DMA((7 * na,)), *hbm, *hbm_land,
                   jax.ShapeDtypeStruct((8, 128), F32)),
        in_specs=[_HBM] * (2 * na),
        out_specs=(_SEM, _SEM, *([_HBM] * (2 * na)), pl.BlockSpec(memory_space=pltpu.VMEM)),
        input_output_aliases={i: 2 + i for i in range(2 * na)},
        compiler_params=pltpu.CompilerParams(has_side_effects=_EFFECT),
    )(*[pltpu.with_memory_space_constraint(g, pltpu.HBM) for g in gs], *lands)
    return outs[0], outs[1], outs[2:2 + na], outs[2 + na:2 + 2 * na], outs[-1]


def _forward_copies(land_refs, send_sems, recv_sems):
    mx, my, mc = lax.axis_index("x"), lax.axis_index("y"), lax.axis_index("c")
    copies = []
    for a in range(len(land_refs)):
        for j, (fx, fy) in enumerate(((0, 1), (1, 0), (1, 1))):
            px = 1 - mx if fx else mx
            py = 1 - my if fy else my
            blk = land_refs[a].at[4 * px + 2 * py + mc]
            copies.append(pltpu.make_async_remote_copy(
                src_ref=blk, dst_ref=blk, send_sem=send_sems.at[3 * a + j], recv_sem=recv_sems.at[3 * a + j],
                device_id=(mx, my, 1 - mc), device_id_type=MESH))
    return copies


def _forward_start(lands, name):
    na = len(lands)

    def body(*refs):
        for cp in _forward_copies(refs[:na], refs[na], refs[na + 1]):
            cp.start()
        refs[-1][...] = jnp.zeros_like(refs[-1])

    outs = pl.pallas_call(
        body, name=name,
        out_shape=(pltpu.SemaphoreType.DMA((3 * na,)), pltpu.SemaphoreType.DMA((3 * na,)),
                   *[pltpu.HBM(g.shape, g.dtype) for g in lands], jax.ShapeDtypeStruct((8, 128), F32)),
        in_specs=[_HBM] * na,
        out_specs=(_SEM, _SEM, *([_HBM] * na), pl.BlockSpec(memory_space=pltpu.VMEM)),
        input_output_aliases={i: 2 + i for i in range(na)},
        compiler_params=pltpu.CompilerParams(has_side_effects=_EFFECT),
    )(*lands)
    return outs[0], outs[1], outs[2:2 + na], outs[-1]


def _forward_wait(send_sems, recv_sems, lands, after, name):
    na = len(lands)

    def body(*refs):
        for cp in _forward_copies(refs[:na], refs[na], refs[na + 1]):
            cp.wait_send()
            cp.wait_recv()

    return pl.pallas_call(
        body, name=name,
        out_shape=tuple(pltpu.HBM(g.shape, g.dtype) for g in lands),
        in_specs=[_HBM] * na + [_SEM, _SEM, pl.BlockSpec(memory_space=pl.ANY)],
        out_specs=tuple([_HBM] * na),
        input_output_aliases={i: i for i in range(na)},
        compiler_params=pltpu.CompilerParams(has_side_effects=_EFFECT),
    )(*lands, send_sems, recv_sems, after)


def _exchange_wait(send_sems, recv_sems, g_thru, land_thru, after, name, gather=False):
    na = len(g_thru)

    def body(*refs):
        for cp in _exchange_copies(refs[:na], refs[na:2 * na], refs[2 * na], refs[2 * na + 1], gather):
            cp.wait_send()
            cp.wait_recv()

    outs = pl.pallas_call(
        body, name=name,
        out_shape=tuple(pltpu.HBM(g.shape, g.dtype) for g in list(g_thru) + list(land_thru)),
        in_specs=[_HBM] * (2 * na) + [_SEM, _SEM, pl.BlockSpec(memory_space=pl.ANY)],
        out_specs=tuple([_HBM] * (2 * na)),
        input_output_aliases={i: i for i in range(2 * na)},
        compiler_params=pltpu.CompilerParams(has_side_effects=_EFFECT),
    )(*g_thru, *land_thru, send_sems, recv_sems, after)
    return outs[na:]


def _sum8(r, name):
    _, R, C = r.shape
    tr = _tile(R, (256, 160, 128, 64, 32, 16, 8))

    def body(r_ref, o_ref):
        acc = r_ref[0].astype(F32)
        for k in range(1, NDEV):
            acc = acc + r_ref[k].astype(F32)
        o_ref[...] = acc

    return pl.pallas_call(
        body, name=name, grid=(R // tr,),
        in_specs=[pl.BlockSpec((NDEV, tr, C), lambda i: (0, i, 0))],
        out_specs=pl.BlockSpec((tr, C), lambda i: (i, 0)),
        out_shape=jax.ShapeDtypeStruct((R, C), F32),
        compiler_params=_params(("parallel",)),
    )(r)


def _adamw(g, w, m, v, name):
    R, C = g.shape
    tr = _tile(R, (256, 160, 128, 64, 32, 16, 8))
    bc1 = 1.0 - ADAM_B1 ** ADAM_STEP
    bc2 = 1.0 - ADAM_B2 ** ADAM_STEP

    def body(g_ref, w_ref, m_ref, v_ref, d_ref, mo_ref, vo_ref):
        gg = g_ref[...]
        mn = ADAM_B1 * m_ref[...] + (1.0 - ADAM_B1) * gg
        vn = ADAM_B2 * v_ref[...] + (1.0 - ADAM_B2) * (gg * gg)
        mh = mn / bc1
        vh = vn / bc2
        d_ref[...] = -ADAM_LR * (mh / (jnp.sqrt(vh) + ADAM_EPS) + ADAM_WD * w_ref[...])
        mo_ref[...] = mn
        vo_ref[...] = vn

    spec = pl.BlockSpec((tr, C), lambda i: (i, 0))
    sh = jax.ShapeDtypeStruct((R, C), F32)
    return pl.pallas_call(
        body, name=name, grid=(R // tr,), in_specs=[spec] * 4, out_specs=[spec] * 3,
        out_shape=[sh] * 3, compiler_params=_params(("parallel",)),
    )(g, w, m, v)


def _ada_fwd(c16, w_sh, b_sh):
    def body(c_ref, w_ref, b_ref, o_ref):
        c = c_ref[...]
        o_ref[...] = _dot(c * _sig(c), w_ref[...], NN) + b_ref[...]

    return pl.pallas_call(
        body, name="ada_fwd", out_shape=jax.ShapeDtypeStruct((16, w_sh.shape[1]), F32),
        compiler_params=pltpu.CompilerParams(vmem_limit_bytes=VMEM_LIMIT),
    )(c16, w_sh, b_sh)


def _ada_bwd(c16, g16, g16_sh, w_sh):
    ncol = w_sh.shape[1]

    def body(c_ref, g_ref, gs_ref, w_ref, dw_ref, db_ref, dc_ref):
        c = c_ref[...]
        s = _sig(c)
        gs = gs_ref[...]
        dw_ref[...] = _dot(c * s, gs, TN)
        db_ref[...] = jnp.broadcast_to(_colsum(g_ref[...]), db_ref.shape)
        odd = lax.broadcasted_iota(jnp.int32, gs.shape, 0) % 2 == 1
        gc = _colsum(jnp.where(odd, gs, 0.0))
        ds = _dot(jnp.broadcast_to(gc, (8, ncol)), w_ref[...], NT)
        c1 = c[1:2, :]
        s1 = s[1:2, :]
        dc_ref[...] = ds * (s1 * (1.0 + c1 * (1.0 - s1)))

    return pl.pallas_call(
        body, name="ada_bwd",
        out_shape=[jax.ShapeDtypeStruct(w_sh.shape, F32),
                   jax.ShapeDtypeStruct((8, g16.shape[1]), F32),
                   jax.ShapeDtypeStruct((8, D), F32)],
        compiler_params=pltpu.CompilerParams(vmem_limit_bytes=VMEM_LIMIT),
    )(c16, g16, g16_sh, w_sh)


def _ln0_fwd(x, ctx, g, b, modx, modc):
    L = x.shape[0]
    nt = L // TL

    def body(x_ref, c_ref, g_ref, b_ref, mx_ref, mc_ref, xn_ref, h_ref):
        isc = pl.program_id(0) == nt
        xin = jnp.where(isc, c_ref[...], x_ref[...])
        sh = jnp.where(isc, mc_ref[0:1, :], mx_ref[0:1, :])
        sc = jnp.where(isc, mc_ref[1:2, :], mx_ref[1:2, :])
        xhat, _ = _ln(xin)
        xn = xhat * g_ref[...] + b_ref[...]
        xn_ref[...] = xn
        h_ref[...] = (xn * (1.0 + sc) + sh).astype(h_ref.dtype)

    return pl.pallas_call(
        body, name="ln0_fwd", grid=(nt + 1,),
        in_specs=[_rtc(D, nt), _cst((TL, D)), _cst((1, D)), _cst((1, D)), _cst((8, D)), _cst((8, D))],
        out_specs=[_rt(D), _rt(D)],
        out_shape=[jax.ShapeDtypeStruct((L + TL, D), F32), jax.ShapeDtypeStruct((L + TL, D), _MXU)],
        compiler_params=_params(("parallel",)),
    )(x, ctx, g, b, modx, modc)


def _xbc_colblk(j):
    return jnp.where(j < 8, OXS // 128 + j, OB // 128 + j - 8)


def _conv_taps(p_ref, r0, first, last):
    main = p_ref[pl.ds(r0, TL), :]
    zero = jnp.zeros((8, main.shape[1]), F32)
    prev = zero if first else p_ref[pl.ds(r0 - 8, 8), :]
    nxt = zero if last else p_ref[pl.ds(r0 + TL, 8), :]
    ext = jnp.concatenate([prev, main, nxt], axis=0)
    n = TL + 16
    return [pltpu.roll(ext, (2 - k) % n, 0)[8:8 + TL] for k in range(5)]


def _seq_chunks(L):
    nt = L // TL
    return [(r * TL, r == 0, r == nt - 1) for r in range(nt)] + [(L, True, True)]


def _conv_fwd(p, conv_w8, conv_b):
    RT = p.shape[0]
    L = RT - TL
    chunks = _seq_chunks(L)

    def body(p_ref, w_ref, b_ref, o_ref):
        w = w_ref[...]
        bias = b_ref[...]
        for r0, first, last in chunks:
            taps = _conv_taps(p_ref, r0, first, last)
            pre = bias + sum(w[k:k + 1, :] * taps[k] for k in range(5))
            o_ref[pl.ds(r0, TL), :] = pre * _sig(pre)

    return pl.pallas_call(
        body, name="conv_fwd", grid=(12,),
        in_specs=[pl.BlockSpec((RT, 128), lambda j: (0, _xbc_colblk(j))),
                  pl.BlockSpec((8, 128), lambda j: (0, j)),
                  pl.BlockSpec((1, 128), lambda j: (0, j))],
        out_specs=pl.BlockSpec((RT, 128), lambda j: (0, j)),
        out_shape=jax.ShapeDtypeStruct((RT, 1536), F32),
        compiler_params=_params(("parallel",)),
    )(p, conv_w8, conv_b)


def _ssd_common(dtraw, dtb, a32, rev):
    dt = _softplus(dtraw + dtb)
    acum = _cumsum_rows(dt * a32, rev)
    ii = lax.broadcasted_iota(jnp.int32, (Q, Q), 0)
    jj = lax.broadcasted_iota(jnp.int32, (Q, Q), 1)
    mask = (ii <= jj) if rev else (ii >= jj)
    return dt, acum, acum.T, dt.T, mask


def _ssd_orders(ncl, ncc):
    nc = ncl + ncc

    def cf(s):
        return jnp.where(s < ncc, ncl + s, s - ncc)

    def cb(s):
        return nc - 1 - s

    return cf, cb


def _ssd_fwd(xbc, p, prm):
    RT = xbc.shape[0]
    nc = RT // Q
    ncc = TL // Q
    cf, cb = _ssd_orders(nc - ncc, ncc)

    def one_dir(x_ref, dt_ref, prm_ref, y_ref, hp_ref, H_ref, d):
        rev = d == 1
        a32 = -jnp.exp(prm_ref[1:2, :])
        dt, acum, acumT, dtT, mask = _ssd_common(dt_ref[...], prm_ref[0:1, :], a32, rev)
        end = 0 if rev else Q - 1
        for g in range(2):
            Bg = x_ref[:, D + g * NS:D + (g + 1) * NS]
            Cg = x_ref[:, D + 2 * NS + g * NS:D + 2 * NS + (g + 1) * NS]
            CB = _dot(Cg, Bg, NT)
            for hh in range(HPG):
                h = g * HPG + hh
                ln = 16 * d + h
                col = acum[:, ln:ln + 1]
                rowv = acumT[ln:ln + 1, :]
                a_end = rowv[:, end:end + 1]
                Lm = jnp.exp(jnp.where(mask, col - rowv, -1e30))
                W = CB * Lm * dtT[ln:ln + 1, :]
                Xh = x_ref[:, h * HP:(h + 1) * HP]
                Hp = H_ref[h * HP:(h + 1) * HP, :]
                y = _dot(W, Xh, NN) + jnp.exp(col) * _dot(Cg, Hp, NT)
                y_ref[:, h * HP:(h + 1) * HP] = y
                dcol = jnp.exp(a_end - col) * dt[:, ln:ln + 1]
                hp_ref[0, h * HP:(h + 1) * HP, :] = Hp
                H_ref[h * HP:(h + 1) * HP, :] = jnp.exp(a_end) * Hp + _dot(Xh * dcol, Bg, TN)

    def body(xf_ref, xb_ref, df_ref, db_ref, prm_ref, yf_ref, yb_ref, hf_ref, hb_ref, Hf, Hb):
        @pl.when(pl.program_id(0) == 0)
        def _():
            Hf[...] = jnp.zeros_like(Hf)
            Hb[...] = jnp.zeros_like(Hb)

        one_dir(xf_ref, df_ref, prm_ref, yf_ref, hf_ref, Hf, 0)
        one_dir(xb_ref, db_ref, prm_ref, yb_ref, hb_ref, Hb, 1)

    ysh = jax.ShapeDtypeStruct((RT, D), F32)
    hsh = jax.ShapeDtypeStruct((nc, NH * HP, NS), F32)
    hspec = pl.BlockSpec((1, NH * HP, NS), lambda s: (s, 0, 0))
    return pl.pallas_call(
        body, name="ssd_fwd", grid=(nc,),
        in_specs=[pl.BlockSpec((Q, 1536), lambda s: (cf(s), 0)),
                  pl.BlockSpec((Q, 1536), lambda s: (cb(s), 0)),
                  pl.BlockSpec((Q, 128), lambda s: (cf(s), ODT // 128)),
                  pl.BlockSpec((Q, 128), lambda s: (cb(s), ODT // 128)),
                  _cst((8, 128))],
        out_specs=[pl.BlockSpec((Q, D), lambda s: (cf(s), 0)),
                   pl.BlockSpec((Q, D), lambda s: (cb(s), 0)), hspec, hspec],
        out_shape=[ysh, ysh, hsh, hsh],
        scratch_shapes=[pltpu.VMEM((NH * HP, NS), F32), pltpu.VMEM((NH * HP, NS), F32)],
        compiler_params=_params(("arbitrary",)),
    )(xbc, xbc, p, p, prm)


def _ssd_bwd(xbc, p, prm, dsk, dyd, hpf, hpb):
    RT = xbc.shape[0]
    nc = RT // Q
    ncc = TL // Q
    ncl = nc - ncc
    cf, cb = _ssd_orders(ncl, ncc)

    def rs(t):
        return nc - 1 - t

    def one_dir(x_ref, dt_ref, prm_ref, dsk_ref, dy_ref, is_ctx, hp_ref, dH_ref,
                dx_ref, ddt_ref, st_ref, d):
        rev = d == 1
        a32 = -jnp.exp(prm_ref[1:2, :])
        dtraw = dt_ref[...]
        dtb = prm_ref[0:1, :]
        dt, acum, acumT, dtT, mask = _ssd_common(dtraw, dtb, a32, rev)
        end = 0 if rev else Q - 1
        lane = lax.broadcasted_iota(jnp.int32, (Q, 128), 1)
        srow = lax.broadcasted_iota(jnp.int32, (Q, 128), 0)
        dyscale = jnp.where(is_ctx, 0.0, 1.0)
        c_dacum = jnp.zeros((Q, 128), F32)
        r_dacum = jnp.zeros((Q, 128), F32)
        c_ddt = jnp.zeros((Q, 128), F32)
        r_ddt = jnp.zeros((Q, 128), F32)
        dskacc = jnp.zeros((1, 128), F32)
        for g in range(2):
            Bg = x_ref[:, D + g * NS:D + (g + 1) * NS]
            Cg = x_ref[:, D + 2 * NS + g * NS:D + 2 * NS + (g + 1) * NS]
            CB = _dot(Cg, Bg, NT)
            dCB = jnp.zeros((Q, Q), F32)
            dBg = jnp.zeros((Q, NS), F32)
            dCg = jnp.zeros((Q, NS), F32)
            for hh in range(HPG):
                h = g * HPG + hh
                ln = 16 * d + h
                hs = slice(h * HP, (h + 1) * HP)
                col = acum[:, ln:ln + 1]
                rowv = acumT[ln:ln + 1, :]
                dtr = dtT[ln:ln + 1, :]
                dtc = dt[:, ln:ln + 1]
                a_end = rowv[:, end:end + 1]
                Lm = jnp.exp(jnp.where(mask, col - rowv, -1e30))
                E = jnp.exp(col)
                ecol = jnp.exp(a_end - col)
                dcol = ecol * dtc
                Xh = x_ref[:, hs]
                dY = dy_ref[:, hs] * dyscale
                Hp = hp_ref[0, hs, :]
                dHn = dH_ref[hs, :]
                W = CB * Lm * dtr
                dW = _dot(dY, Xh, NT)
                Mm = dW * CB * Lm
                T = Mm * dtr
                dCB = dCB + dW * Lm * dtr
                BdH = _dot(Bg, dHn, NT)
                dX = _dot(W, dY, TN) + dcol * BdH
                if d == 0:
                    dX = dX + dY * dsk_ref[:, hs]
                    dskacc = dskacc + jnp.where(lane[0:1, :] == h, _sum11(dY * Xh), 0.0)
                dx_ref[:, hs] = dX
                xb = jnp.sum(Xh * BdH, axis=1, keepdims=True)
                scol = dcol * xb
                G = _dot(dY, Hp, NN)
                dCg = dCg + E * G
                qcol = E * jnp.sum(G * Cg, axis=1, keepdims=True)
                dBg = dBg + _dot(Xh * dcol, dHn, NN)
                dH_ref[hs, :] = jnp.exp(a_end) * dHn + _dot(dY * E, Cg, TN)
                eterm = jnp.exp(a_end) * _sum11(dHn * Hp) + _sum11(scol)
                cvec = jnp.sum(T, axis=1, keepdims=True) + qcol - scol
                cvec = cvec + jnp.where(srow[:, 0:1] == end, eterm, 0.0)
                c_dacum = c_dacum + jnp.where(lane == ln, cvec, 0.0)
                r_dacum = r_dacum - jnp.where(srow == ln, _colsum(T), 0.0)
                c_ddt = c_ddt + jnp.where(lane == ln, ecol * xb, 0.0)
                r_ddt = r_ddt + jnp.where(srow == ln, _colsum(Mm), 0.0)
            dBg = dBg + _dot(dCB, Cg, TN)
            dCg = dCg + _dot(dCB, Bg, NN)
            dx_ref[:, D + g * NS:D + (g + 1) * NS] = dBg
            dx_ref[:, D + 2 * NS + g * NS:D + 2 * NS + (g + 1) * NS] = dCg
        dacum = c_dacum + r_dacum.T
        da = _cumsum_rows(dacum, not rev)
        mine = (lane >= 16 * d) & (lane < 16 * d + 16)
        ddt = jnp.where(mine, c_ddt + r_ddt.T + da * a32, 0.0)
        ddt_ref[...] = ddt * _sig(dtraw + dtb)
        st_ref[0:1, :] += _colsum(jnp.where(mine, da * dt, 0.0))
        if d == 0:
            st_ref[1:2, :] += dskacc

    def body(xf_ref, xb_ref, df_ref, db_ref, prm_ref, dsk_ref, dyf_ref, dyb_ref, hf_ref, hb_ref,
             dxf_ref, dxb_ref, ddf_ref, ddb_ref, st_ref, dHf, dHb):
        t = pl.program_id(0)

        @pl.when(t == 0)
        def _():
            dHf[...] = jnp.zeros_like(dHf)
            dHb[...] = jnp.zeros_like(dHb)
            st_ref[...] = jnp.zeros_like(st_ref)

        s = rs(t)
        one_dir(xf_ref, df_ref, prm_ref, dsk_ref, dyf_ref, cf(s) >= ncl, hf_ref, dHf,
                dxf_ref, ddf_ref, st_ref, 0)
        one_dir(xb_ref, db_ref, prm_ref, dsk_ref, dyb_ref, cb(s) >= ncl, hb_ref, dHb,
                dxb_ref, ddb_ref, st_ref, 1)

        @pl.when(t == nc - 1)
        def _():
            st_ref[0:1, :] = -jnp.exp(prm_ref[1:2, :]) * st_ref[0:1, :]

    def lat(c):
        return jnp.minimum(c, ncl - 1)

    xsh = jax.ShapeDtypeStruct((RT, 1536), F32)
    dsh = jax.ShapeDtypeStruct((RT, 128), F32)
    hspec = pl.BlockSpec((1, NH * HP, NS), lambda t: (rs(t), 0, 0))
    return pl.pallas_call(
        body, name="ssd_bwd", grid=(nc,),
        in_specs=[pl.BlockSpec((Q, 1536), lambda t: (cf(rs(t)), 0)),
                  pl.BlockSpec((Q, 1536), lambda t: (cb(rs(t)), 0)),
                  pl.BlockSpec((Q, 128), lambda t: (cf(rs(t)), ODT // 128)),
                  pl.BlockSpec((Q, 128), lambda t: (cb(rs(t)), ODT // 128)),
                  _cst((8, 128)), _cst((1, D)),
                  pl.BlockSpec((Q, D), lambda t: (lat(cf(rs(t))), 0)),
                  pl.BlockSpec((Q, D), lambda t: (lat(cb(rs(t))), 0)),
                  hspec, hspec],
        out_specs=[pl.BlockSpec((Q, 1536), lambda t: (cf(rs(t)), 0)),
                   pl.BlockSpec((Q, 1536), lambda t: (cb(rs(t)), 0)),
                   pl.BlockSpec((Q, 128), lambda t: (cf(rs(t)), 0)),
                   pl.BlockSpec((Q, 128), lambda t: (cb(rs(t)), 0)),
                   _cst((8, 128))],
        out_shape=[xsh, xsh, dsh, dsh, jax.ShapeDtypeStruct((8, 128), F32)],
        scratch_shapes=[pltpu.VMEM((NH * HP, NS), F32), pltpu.VMEM((NH * HP, NS), F32)],
        compiler_params=_params(("arbitrary",)),
    )(xbc, xbc, p, p, prm, dsk, dyd, dyd, hpf, hpb)


def _lane_bcast(v, ln):
    return jnp.broadcast_to(v[:, ln:ln + 1], v.shape)


def _halves(v, lo, axis):
    return jnp.concatenate([jnp.where(lo, v, 0.0), jnp.where(lo, 0.0, v)], axis=axis)


def _ssd2_fwd(xbc, p, prm):
    RT = xbc.shape[0]
    nc = RT // Q
    ncc = TL // Q
    cf, cb = _ssd_orders(nc - ncc, ncc)

    def one_dir(x_ref, dt_ref, prm_ref, y_ref, hp_ref, HT_ref, d):
        rev = d == 1
        a32 = -jnp.exp(prm_ref[1:2, :])
        dt, acum, acumT, dtT, mask = _ssd_common(dt_ref[...], prm_ref[0:1, :], a32, rev)
        end = 0 if rev else Q - 1
        lo = lax.broadcasted_iota(jnp.int32, (Q, 128), 1) < HP
        for g in range(2):
            Bg = x_ref[:, D + g * NS:D + (g + 1) * NS]
            Cg = x_ref[:, D + 2 * NS + g * NS:D + 2 * NS + (g + 1) * NS]
            CB = _dot(Cg, Bg, NT)
            xds, svs = [], []
            for q in range(HPG // 2):
                pi = g * (HPG // 2) + q
                ps = slice(pi * 128, (pi + 1) * 128)
                Xp = x_ref[:, ps]
                HTp = HT_ref[:, ps]
                lhs, dcs, sv = [], [], []
                ces = []
                for h in (2 * pi, 2 * pi + 1):
                    ln = 16 * d + h
                    colB = _lane_bcast(acum, ln)
                    rowv = acumT[ln:ln + 1, :]
                    aend = colB[end:end + 1, :]
                    Lm = jnp.exp(jnp.where(mask, colB - rowv, -1e30))
                    lhs.append(CB * Lm * dtT[ln:ln + 1, :])
                    ces.append(Cg * jnp.exp(colB))
                    dcs.append(jnp.exp(aend - colB) * _lane_bcast(dt, ln))
                    sv.append(jnp.exp(aend))
                lhs = jnp.concatenate(lhs + ces, axis=1)
                rhs = jnp.concatenate([_halves(Xp, lo, 0), _halves(HTp, lo, 0)], axis=0)
                y_ref[:, ps] = _dot(lhs, rhs, NN)
                xds.append(Xp * jnp.where(lo, dcs[0], dcs[1]))
                svs.append(jnp.where(lo[0:1, :], sv[0], sv[1]))
            gs = slice(g * 512, (g + 1) * 512)
            HTg = HT_ref[:, gs]
            hp_ref[0, :, gs] = HTg
            st = _dot(Bg.T, jnp.concatenate(xds, axis=1), NN)
            HT_ref[:, gs] = jnp.concatenate(svs, axis=1) * HTg + st

    def body(xf_ref, xb_ref, df_ref, db_ref, prm_ref, yf_ref, yb_ref, hf_ref, hb_ref, Hf, Hb):
        @pl.when(pl.program_id(0) == 0)
        def _():
            Hf[...] = jnp.zeros_like(Hf)
            Hb[...] = jnp.zeros_like(Hb)

        one_dir(xf_ref, df_ref, prm_ref, yf_ref, hf_ref, Hf, 0)
        one_dir(xb_ref, db_ref, prm_ref, yb_ref, hb_ref, Hb, 1)

    ysh = jax.ShapeDtypeStruct((RT, D), F32)
    hsh = jax.ShapeDtypeStruct((nc, NS, NH * HP), F32)
    hspec = pl.BlockSpec((1, NS, NH * HP), lambda s: (s, 0, 0))
    return pl.pallas_call(
        body, name="ssd_fwd", grid=(nc,),
        in_specs=[pl.BlockSpec((Q, 1536), lambda s: (cf(s), 0)),
                  pl.BlockSpec((Q, 1536), lambda s: (cb(s), 0)),
                  pl.BlockSpec((Q, 128), lambda s: (cf(s), ODT // 128)),
                  pl.BlockSpec((Q, 128), lambda s: (cb(s), ODT // 128)),
                  _cst((8, 128))],
        out_specs=[pl.BlockSpec((Q, D), lambda s: (cf(s), 0)),
                   pl.BlockSpec((Q, D), lambda s: (cb(s), 0)), hspec, hspec],
        out_shape=[ysh, ysh, hsh, hsh],
        scratch_shapes=[pltpu.VMEM((NS, NH * HP), F32), pltpu.VMEM((NS, NH * HP), F32)],
        compiler_params=_params(("arbitrary",)),
    )(xbc, xbc, p, p, prm)


def _ssd2_bwd(xbc, p, prm, dsk, dyd, hpf, hpb):
    RT = xbc.shape[0]
    nc = RT // Q
    ncc = TL // Q
    ncl = nc - ncc
    cf, cb = _ssd_orders(ncl, ncc)

    def rs(t):
        return nc - 1 - t

    def one_dir(x_ref, dt_ref, prm_ref, dsk_ref, dy_ref, is_ctx, hp_ref, dHT_ref,
                dx_ref, ddt_ref, st_ref, d):
        rev = d == 1
        a32 = -jnp.exp(prm_ref[1:2, :])
        dtraw = dt_ref[...]
        dtb = prm_ref[0:1, :]
        dt, acum, acumT, _, _ = _ssd_common(dtraw, dtb, a32, rev)
        end = 0 if rev else Q - 1
        lane = lax.broadcasted_iota(jnp.int32, (Q, 128), 1)
        srow = lax.broadcasted_iota(jnp.int32, (Q, 128), 0)
        maskT = (lane <= srow) if rev else (lane >= srow)
        lo = lane < HP
        lo1 = lo[0:1, :]
        dyscale = jnp.where(is_ctx, 0.0, 1.0)
        c_dacum = jnp.zeros((Q, 128), F32)
        r_dacum = jnp.zeros((Q, 128), F32)
        c_ddt = jnp.zeros((Q, 128), F32)
        dskacc = jnp.zeros((1, 128), F32)
        for g in range(2):
            gs = slice(g * 512, (g + 1) * 512)
            Bg = x_ref[:, D + g * NS:D + (g + 1) * NS]
            Cg = x_ref[:, D + 2 * NS + g * NS:D + 2 * NS + (g + 1) * NS]
            CBT = _dot(Bg, Cg, NT)
            HTg = hp_ref[0, :, gs]
            dHTg = dHT_ref[:, gs]
            BdHg = _dot(Bg, dHTg, NN)
            dCBT = jnp.zeros((Q, Q), F32)
            dCg = jnp.zeros((Q, NS), F32)
            xds, dyes, svs = [], [], []
            for q in range(HPG // 2):
                pi = g * (HPG // 2) + q
                ps = slice(pi * 128, (pi + 1) * 128)
                qs = slice(q * 128, (q + 1) * 128)
                Xp = x_ref[:, ps]
                dYp = dy_ref[:, ps] * dyscale
                HTp = HTg[:, qs]
                BdHp = BdHg[:, qs]
                dY2 = _halves(dYp, lo, 0)
                dWT2 = _dot(_halves(Xp, lo, 0), dYp.T, NN)
                G2 = _dot(dY2, HTp, NT)
                XB = Xp * BdHp
                hh = _colsum(dHTg[:, qs] * HTp)
                yx = _colsum(dYp * Xp)
                wts, dcs, ebs, sv = [], [], [], []
                for k, h in enumerate((2 * pi, 2 * pi + 1)):
                    ln = 16 * d + h
                    half = lo if k == 0 else jnp.logical_not(lo)
                    half1 = half[0:1, :]
                    colB = _lane_bcast(acum, ln)
                    dtcB = _lane_bcast(dt, ln)
                    rowv = acumT[ln:ln + 1, :]
                    aend = colB[end:end + 1, :]
                    LmT = jnp.exp(jnp.where(maskT, rowv - colB, -1e30))
                    WT = CBT * LmT * dtcB
                    dWT = dWT2[k * Q:(k + 1) * Q, :]
                    U = dWT * LmT
                    MT = U * CBT
                    rM = jnp.sum(MT, axis=1, keepdims=True)
                    rT = _colsum(MT * dtcB)
                    dCBT = dCBT + U * dtcB
                    ecol = jnp.exp(aend - colB)
                    EB = jnp.exp(colB)
                    Gk = G2[k * Q:(k + 1) * Q, :]
                    dCg = dCg + EB * Gk
                    qcol = jnp.sum(EB * Gk * Cg, axis=1, keepdims=True)
                    xb = jnp.sum(jnp.where(half, XB, 0.0), axis=1, keepdims=True)
                    e1 = ecol[:, 0:1]
                    dt1 = dtcB[:, 0:1]
                    scol = e1 * dt1 * xb
                    sA = jnp.exp(aend)
                    eterm = sA[:, 0:1] * jnp.sum(jnp.where(half1, hh, 0.0), axis=1, keepdims=True) \
                        + _colsum(scol)
                    cvec = qcol - dt1 * rM - scol + jnp.where(srow[:, 0:1] == end, eterm, 0.0)
                    c_dacum = jnp.where(lane == ln, cvec, c_dacum)
                    r_dacum = jnp.where(srow == ln, rT, r_dacum)
                    c_ddt = jnp.where(lane == ln, rM + e1 * xb, c_ddt)
                    if d == 0:
                        dskacc = dskacc + jnp.where(
                            lane[0:1, :] == h, jnp.sum(jnp.where(half1, yx, 0.0), axis=1, keepdims=True), 0.0)
                    wts.append(WT)
                    dcs.append(ecol * dtcB)
                    ebs.append(EB)
                    sv.append(sA)
                dcp = jnp.where(lo, dcs[0], dcs[1])
                dX = _dot(jnp.concatenate(wts, axis=1), dY2, NN) + dcp * BdHp
                if d == 0:
                    dX = dX + dYp * dsk_ref[:, ps]
                dx_ref[:, ps] = dX
                xds.append(Xp * dcp)
                dyes.append(dYp * jnp.where(lo, ebs[0], ebs[1]))
                svs.append(jnp.where(lo1, sv[0], sv[1]))
            dx_ref[:, D + g * NS:D + (g + 1) * NS] = (
                _dot(jnp.concatenate(xds, axis=1), dHTg, NT) + _dot(dCBT, Cg, NN))
            dx_ref[:, D + 2 * NS + g * NS:D + 2 * NS + (g + 1) * NS] = dCg + _dot(dCBT, Bg, TN)
            dHT_ref[:, gs] = (jnp.concatenate(svs, axis=1) * dHTg
                              + _dot(Cg.T, jnp.concatenate(dyes, axis=1), NN))
        dacum = c_dacum + r_dacum.T
        da = _cumsum_rows(dacum, not rev)
        mine = (lane >= 16 * d) & (lane < 16 * d + 16)
        ddt = jnp.where(mine, c_ddt + da * a32, 0.0)
        ddt_ref[...] = ddt * _sig(dtraw + dtb)
        st_ref[0:1, :] += _colsum(jnp.where(mine, da * dt, 0.0))
        if d == 0:
            st_ref[1:2, :] += dskacc

    def body(xf_ref, xb_ref, df_ref, db_ref, prm_ref, dsk_ref, dyf_ref, dyb_ref, hf_ref, hb_ref,
             dxf_ref, dxb_ref, ddf_ref, ddb_ref, st_ref, dHf, dHb):
        t = pl.program_id(0)

        @pl.when(t == 0)
        def _():
            dHf[...] = jnp.zeros_like(dHf)
            dHb[...] = jnp.zeros_like(dHb)
            st_ref[...] = jnp.zeros_like(st_ref)

        s = rs(t)
        one_dir(xf_ref, df_ref, prm_ref, dsk_ref, dyf_ref, cf(s) >= ncl, hf_ref, dHf,
                dxf_ref, ddf_ref, st_ref, 0)
        one_dir(xb_ref, db_ref, prm_ref, dsk_ref, dyb_ref, cb(s) >= ncl, hb_ref, dHb,
                dxb_ref, ddb_ref, st_ref, 1)

        @pl.when(t == nc - 1)
        def _():
            st_ref[0:1, :] = -jnp.exp(prm_ref[1:2, :]) * st_ref[0:1, :]

    def lat(c):
        return jnp.minimum(c, ncl - 1)

    xsh = jax.ShapeDtypeStruct((RT, 1536), F32)
    dsh = jax.ShapeDtypeStruct((RT, 128), F32)
    hspec = pl.BlockSpec((1, NS, NH * HP), lambda t: (rs(t), 0, 0))
    return pl.pallas_call(
        body, name="ssd_bwd", grid=(nc,),
        in_specs=[pl.BlockSpec((Q, 1536), lambda t: (cf(rs(t)), 0)),
                  pl.BlockSpec((Q, 1536), lambda t: (cb(rs(t)), 0)),
                  pl.BlockSpec((Q, 128), lambda t: (cf(rs(t)), ODT // 128)),
                  pl.BlockSpec((Q, 128), lambda t: (cb(rs(t)), ODT // 128)),
                  _cst((8, 128)), _cst((1, D)),
                  pl.BlockSpec((Q, D), lambda t: (lat(cf(rs(t))), 0)),
                  pl.BlockSpec((Q, D), lambda t: (lat(cb(rs(t))), 0)),
                  hspec, hspec],
        out_specs=[pl.BlockSpec((Q, 1536), lambda t: (cf(rs(t)), 0)),
                   pl.BlockSpec((Q, 1536), lambda t: (cb(rs(t)), 0)),
                   pl.BlockSpec((Q, 128), lambda t: (cf(rs(t)), 0)),
                   pl.BlockSpec((Q, 128), lambda t: (cb(rs(t)), 0)),
                   _cst((8, 128))],
        out_shape=[xsh, xsh, dsh, dsh, jax.ShapeDtypeStruct((8, 128), F32)],
        scratch_shapes=[pltpu.VMEM((NS, NH * HP), F32), pltpu.VMEM((NS, NH * HP), F32)],
        compiler_params=_params(("arbitrary",)),
    )(xbc, xbc, p, p, prm, dsk, dyd, dyd, hpf, hpb)


def _mix_fwd_vals(yf, yb, z, xs, u, v, dsk, sg, gg, gb):
    y = yf + yb + xs * dsk
    sz = _sig(z)
    hh = y * z * sz
    r = lax.rsqrt(jnp.mean(hh * hh, axis=-1, keepdims=True) + EPS)
    nh = hh * r
    ug, tu = _gelu(u)
    vg, tv = _gelu(v)
    vhat, vrstd = _ln(vg)
    vn = vhat * gg + gb
    return y, sz, r, nh, ug, tu, vg, tv, vhat, vrstd, vn


def _mix_fwd(yf, yb, p, xbc, dsk, sg, gg, gb, ws, bsT):
    L = yf.shape[0] - TL
    nt = L // TL

    def body(yf_ref, yb_ref, z_ref, xs_ref, u_ref, v_ref, dsk_ref, sg_ref, gg_ref, gb_ref,
             ws_ref, bs_ref, ys_ref, ym_ref):
        _, _, _, nh, ug, _, _, _, _, _, vn = _mix_fwd_vals(
            yf_ref[...], yb_ref[...], z_ref[...], xs_ref[...], u_ref[...], v_ref[...],
            dsk_ref[...], sg_ref[...], gg_ref[...], gb_ref[...])
        ys_ref[...] = (nh * sg_ref[...]).astype(ys_ref.dtype)
        for n in range(TL // Q):
            rs_ = slice(n * Q, (n + 1) * Q)
            for g in range(8):
                cs = slice(g * 128, (g + 1) * 128)
                mixed = _dot(ws_ref[g], vn[rs_, cs], NN) + bs_ref[:, g:g + 1]
                ym_ref[rs_, cs] = (ug[rs_, cs] * mixed).astype(ym_ref.dtype)

    return pl.pallas_call(
        body, name="mix_fwd", grid=(nt,),
        in_specs=[_rt(D), _rt(D), _rt(D, OZ // D), _rt(D, 0), _rt(D, OU // D), _rt(D, OV // D),
                  _cst((1, D)), _cst((1, D)), _cst((1, D)), _cst((1, D)),
                  _cst((8, 128, 128)), _cst((128, 128))],
        out_specs=[_rt(D), _rt(D)],
        out_shape=[jax.ShapeDtypeStruct((L, D), _MXU), jax.ShapeDtypeStruct((L, D), _MXU)],
        compiler_params=_params(("parallel",)),
    )(yf, yb, p, xbc, p, p, dsk, sg, gg, gb, ws, bsT)


def _mix_bwd(dys, dym, yf, yb, p, xbc, dp, dsk, sg, gg, gb, ws, bsT):
    L = dys.shape[0]
    nt = L // TL

    def body(dys_ref, dym_ref, yf_ref, yb_ref, z_ref, xs_ref, u_ref, v_ref, dsk_ref, sg_ref,
             gg_ref, gb_ref, ws_ref, bs_ref, dp_any, dzuv_ref, dy_ref, st_ref,
             dws_ref, dbs_ref, dvn_s):
        del dp_any
        dz_ref = dzuv_ref.at[:, OZ:OZ + D]
        du_ref = dzuv_ref.at[:, OU:OU + D]
        dv_ref = dzuv_ref.at[:, OV:OV + D]

        @pl.when(pl.program_id(0) == 0)
        def _():
            st_ref[...] = jnp.zeros_like(st_ref)
            dws_ref[...] = jnp.zeros_like(dws_ref)
            dbs_ref[...] = jnp.zeros_like(dbs_ref)

        z = z_ref[...]
        u = u_ref[...]
        v = v_ref[...]
        y, sz, r, nh, ug, tu, vg, tv, vhat, vrstd, vn = _mix_fwd_vals(
            yf_ref[...], yb_ref[...], z, xs_ref[...], u, v,
            dsk_ref[...], sg_ref[...], gg_ref[...], gb_ref[...])
        dys = dys_ref[...]
        st_ref[0:1, :] += _colsum(dys * nh)
        dn = dys * sg_ref[...]
        dhh = r * (dn - nh * jnp.mean(dn * nh, axis=-1, keepdims=True))
        dy_ref[...] = dhh * z * sz
        dz_ref[...] = (dhh * y * (sz * (1.0 + z * (1.0 - sz)))).astype(dz_ref.dtype)
        dym = dym_ref[...]
        lane = lax.broadcasted_iota(jnp.int32, (Q, 128), 1)
        dbs = jnp.zeros((Q, 128), F32)
        gu = _gelu_grad(u, tu)
        for n in range(TL // Q):
            rs_ = slice(n * Q, (n + 1) * Q)
            for g in range(8):
                cs = slice(g * 128, (g + 1) * 128)
                vb = vn[rs_, cs]
                mixed = _dot(ws_ref[g], vb, NN) + bs_ref[:, g:g + 1]
                dyb = dym[rs_, cs]
                dmx = dyb * ug[rs_, cs]
                du_ref[rs_, cs] = (dyb * mixed * gu[rs_, cs]).astype(du_ref.dtype)
                dvn_s[rs_, cs] = _dot(ws_ref[g], dmx, TN)
                dws_ref[g] += _dot(dmx, vb, NT)
                dbs = dbs + jnp.where(lane == g, jnp.sum(dmx, axis=1, keepdims=True), 0.0)
        dbs_ref[...] += dbs
        dvn = dvn_s[...]
        st_ref[1:2, :] += _colsum(dvn * vhat)
        st_ref[2:3, :] += _colsum(dvn)
        dvg = _ln_bwd(dvn * gg_ref[...], vhat, vrstd)
        dv_ref[...] = (dvg * _gelu_grad(v, tv)).astype(dv_ref.dtype)

    outs = pl.pallas_call(
        body, name="mix_bwd", grid=(nt,),
        in_specs=[_rt(D), _rt(D), _rt(D), _rt(D), _rt(D, OZ // D), _rt(D, 0), _rt(D, OU // D),
                  _rt(D, OV // D), _cst((1, D)), _cst((1, D)), _cst((1, D)), _cst((1, D)),
                  _cst((8, 128, 128)), _cst((128, 128)), pl.BlockSpec(memory_space=pl.ANY)],
        out_specs=[_rt(3 * D, 0), _rt(D), _cst((8, D)),
                   _cst((8, 128, 128)), _cst((128, 128))],
        out_shape=[jax.ShapeDtypeStruct(dp.shape, dp.dtype),
                   jax.ShapeDtypeStruct((L, D), F32), jax.ShapeDtypeStruct((8, D), F32),
                   jax.ShapeDtypeStruct((8, 128, 128), F32), jax.ShapeDtypeStruct((128, 128), F32)],
        scratch_shapes=[pltpu.VMEM((TL, D), F32)],
        input_output_aliases={14: 0},
        compiler_params=_params(("arbitrary",)),
    )(dys, dym, yf, yb, p, xbc, p, p, dsk, sg, gg, gb, ws, bsT, dp)
    return outs


def _gate_fwd(a1, a2, p, bg):
    L = a1.shape[0]

    def body(a1_ref, a2_ref, g_ref, bg_ref, m_ref):
        gt = _sig(g_ref[...] + bg_ref[...])
        m_ref[...] = (gt[:, :D] * a1_ref[...] + gt[:, D:] * a2_ref[...]).astype(m_ref.dtype)

    return pl.pallas_call(
        body, name="gate_fwd", grid=(L // TL,),
        in_specs=[_rt(D), _rt(D), _rt(2 * D, OG // (2 * D)), _cst((1, 2 * D))],
        out_specs=_rt(D), out_shape=jax.ShapeDtypeStruct((L, D), _MXU),
        compiler_params=_params(("parallel",)),
    )(a1, a2, p, bg)


def _gate_bwd(dmg, a1, a2, p, bg, dp):
    L = a1.shape[0]

    def body(dm_ref, a1_ref, a2_ref, g_ref, bg_ref, dp_any, dg_ref, da1_ref, da2_ref, st_ref):
        del dp_any

        @pl.when(pl.program_id(0) == 0)
        def _():
            st_ref[...] = jnp.zeros_like(st_ref)

        gt = _sig(g_ref[...] + bg_ref[...])
        g1 = gt[:, :D]
        g2 = gt[:, D:]
        dm = dm_ref[...]
        da1_ref[...] = (dm * g1).astype(da1_ref.dtype)
        da2_ref[...] = (dm * g2).astype(da2_ref.dtype)
        dg1 = dm * a1_ref[...] * g1 * (1.0 - g1)
        dg2 = dm * a2_ref[...] * g2 * (1.0 - g2)
        st_ref[0:1, 0:D] += _colsum(dg1)
        st_ref[0:1, D:2 * D] += _colsum(dg2)
        dg_ref[:, 0:D] = dg1.astype(dg_ref.dtype)
        dg_ref[:, D:2 * D] = dg2.astype(dg_ref.dtype)

    return pl.pallas_call(
        body, name="gate_bwd", grid=(L // TL,),
        in_specs=[_rt(D), _rt(D), _rt(D), _rt(2 * D, OG // (2 * D)), _cst((1, 2 * D)),
                  pl.BlockSpec(memory_space=pl.ANY)],
        out_specs=[_rt(2 * D, OG // (2 * D)), _rt(D), _rt(D), _cst((8, 2 * D))],
        out_shape=[jax.ShapeDtypeStruct(dp.shape, dp.dtype), jax.ShapeDtypeStruct((L, D), _MXU),
                   jax.ShapeDtypeStruct((L, D), _MXU), jax.ShapeDtypeStruct((8, 2 * D), F32)],
        input_output_aliases={5: 0},
        compiler_params=_params(("arbitrary",)),
    )(dmg, a1, a2, p, bg, dp)


def _res1_fwd(xn, out, modx, g, b):
    L = out.shape[0]

    def body(xn_ref, o_ref, mx_ref, g_ref, b_ref, r1_ref, h2_ref):
        r1 = ALPHA * xn_ref[...] + mx_ref[2:3, :] * o_ref[...]
        xhat, _ = _ln(r1)
        x1 = xhat * g_ref[...] + b_ref[...]
        r1_ref[...] = r1
        h2_ref[...] = (x1 * (1.0 + mx_ref[4:5, :]) + mx_ref[3:4, :]).astype(h2_ref.dtype)

    return pl.pallas_call(
        body, name="res1_fwd", grid=(L // TL,),
        in_specs=[_rt(D), _rt(D), _cst((8, D)), _cst((1, D)), _cst((1, D))],
        out_specs=[_rt(D), _rt(D)],
        out_shape=[jax.ShapeDtypeStruct((L, D), F32), jax.ShapeDtypeStruct((L, D), _MXU)],
        compiler_params=_params(("parallel",)),
    )(xn, out, modx, g, b)


def _glu_fwd(f13):
    L = f13.shape[0]

    def body(f1_ref, f3_ref, o_ref):
        f1 = f1_ref[...]
        o_ref[...] = (f1 * _sig(f1) * f3_ref[...]).astype(o_ref.dtype)

    return pl.pallas_call(
        body, name="glu_fwd", grid=(L // TL,),
        in_specs=[_rt(DFF, 0), _rt(DFF, 1)], out_specs=_rt(DFF),
        out_shape=jax.ShapeDtypeStruct((L, DFF), _MXU),
        compiler_params=_params(("parallel",)),
    )(f13, f13)


def _glu_bwd(dff, f13):
    L = f13.shape[0]

    def body(d_ref, f1_ref, f3_ref, o_ref):
        f1 = f1_ref[...]
        s = _sig(f1)
        d = d_ref[...]
        o_ref[:, 0:DFF] = (d * f3_ref[...] * (s * (1.0 + f1 * (1.0 - s)))).astype(o_ref.dtype)
        o_ref[:, DFF:2 * DFF] = (d * f1 * s).astype(o_ref.dtype)

    return pl.pallas_call(
        body, name="glu_bwd", grid=(L // TL,),
        in_specs=[_rt(DFF), _rt(DFF, 0), _rt(DFF, 1)], out_specs=_rt(2 * DFF),
        out_shape=jax.ShapeDtypeStruct((L, 2 * DFF), _MXU),
        compiler_params=_params(("parallel",)),
    )(dff, f13, f13)


def _res2(r1, o2, tgt, modx, g1, b1, g2, b2):
    L = r1.shape[0]

    def body(r1_ref, o2_ref, t_ref, mx_ref, g1_ref, b1_ref, g2_ref, b2_ref,
             dr2_ref, do2_ref, st_ref, loss_ref):
        @pl.when(pl.program_id(0) == 0)
        def _():
            st_ref[...] = jnp.zeros_like(st_ref)
            loss_ref[...] = jnp.zeros_like(loss_ref)

        xh1, _ = _ln(r1_ref[...])
        x1 = xh1 * g1_ref[...] + b1_ref[...]
        o2 = o2_ref[...]
        g2x = mx_ref[5:6, :]
        xh2, rstd2 = _ln(ALPHA * x1 + g2x * o2)
        err = xh2 * g2_ref[...] + b2_ref[...] - t_ref[...]
        per_tok = jnp.mean(err * err, axis=-1, keepdims=True)
        loss_ref[...] += 0.5 * jnp.sum(per_tok, axis=0, keepdims=True)
        dy = err * (1.0 / D)
        st_ref[0:1, :] += _colsum(dy * xh2)
        st_ref[1:2, :] += _colsum(dy)
        dr2 = _ln_bwd(dy * g2_ref[...], xh2, rstd2)
        st_ref[2:3, :] += _colsum(dr2 * o2)
        dr2_ref[...] = dr2
        do2_ref[...] = (g2x * dr2).astype(do2_ref.dtype)

    return pl.pallas_call(
        body, name="res2", grid=(L // TL,),
        in_specs=[_rt(D), _rt(D), _rt(D), _cst((8, D))] + [_cst((1, D))] * 4,
        out_specs=[_rt(D), _rt(D), _cst((8, D)), _cst((8, 128))],
        out_shape=[jax.ShapeDtypeStruct((L, D), F32), jax.ShapeDtypeStruct((L, D), _MXU),
                   jax.ShapeDtypeStruct((8, D), F32), jax.ShapeDtypeStruct((8, 128), F32)],
        compiler_params=_params(("arbitrary",)),
    )(r1, o2, tgt, modx, g1, b1, g2, b2)


def _res1_bwd(dr2, dh2, r1, out, modx, g1, b1):
    L = r1.shape[0]

    def body(dr2_ref, dh2_ref, r1_ref, o_ref, mx_ref, g_ref, b_ref, dr1_ref, do_ref, st_ref):
        @pl.when(pl.program_id(0) == 0)
        def _():
            st_ref[...] = jnp.zeros_like(st_ref)

        xh1, rstd1 = _ln(r1_ref[...])
        x1 = xh1 * g_ref[...] + b_ref[...]
        dh2 = dh2_ref[...]
        dx1 = ALPHA * dr2_ref[...] + dh2 * (1.0 + mx_ref[4:5, :])
        st_ref[0:1, :] += _colsum(dh2 * x1)
        st_ref[1:2, :] += _colsum(dh2)
        st_ref[2:3, :] += _colsum(dx1 * xh1)
        st_ref[3:4, :] += _colsum(dx1)
        dr1 = _ln_bwd(dx1 * g_ref[...], xh1, rstd1)
        st_ref[4:5, :] += _colsum(dr1 * o_ref[...])
        dr1_ref[...] = dr1
        do_ref[...] = (mx_ref[2:3, :] * dr1).astype(do_ref.dtype)

    return pl.pallas_call(
        body, name="res1_bwd", grid=(L // TL,),
        in_specs=[_rt(D), _rt(D), _rt(D), _rt(D), _cst((8, D)), _cst((1, D)), _cst((1, D))],
        out_specs=[_rt(D), _rt(D), _cst((8, D))],
        out_shape=[jax.ShapeDtypeStruct((L, D), F32), jax.ShapeDtypeStruct((L, D), _MXU),
                   jax.ShapeDtypeStruct((8, D), F32)],
        compiler_params=_params(("arbitrary",)),
    )(dr2, dh2, r1, out, modx, g1, b1)


def _conv_bwd(dxf, dxb, p, conv_w8, conv_b, dp):
    RT = p.shape[0]
    chunks = _seq_chunks(RT - TL)

    def body(df_ref, db_ref, p_ref, w_ref, b_ref, dp_any, o_ref, dw_ref, dbias_ref, dpre_s):
        del dp_any
        w = w_ref[...]
        bias = b_ref[...]
        srow = lax.broadcasted_iota(jnp.int32, (8, 128), 0)
        dwacc = jnp.zeros((8, 128), F32)
        dbacc = jnp.zeros((1, 128), F32)
        for r0, first, last in chunks:
            taps = _conv_taps(p_ref, r0, first, last)
            pre = bias + sum(w[k:k + 1, :] * taps[k] for k in range(5))
            s = _sig(pre)
            dpre = (df_ref[pl.ds(r0, TL), :] + db_ref[pl.ds(r0, TL), :]) * (s * (1.0 + pre * (1.0 - s)))
            dpre_s[pl.ds(r0, TL), :] = dpre
            dbacc = dbacc + _colsum(dpre)
            for k in range(5):
                dwacc = dwacc + jnp.where(srow == k, _colsum(dpre * taps[k]), 0.0)
        for r0, first, last in chunks:
            taps = _conv_taps(dpre_s, r0, first, last)
            dx = sum(w[k:k + 1, :] * taps[4 - k] for k in range(5))
            o_ref[pl.ds(r0, TL), :] = dx.astype(o_ref.dtype)
        dw_ref[...] = dwacc
        dbias_ref[...] = jnp.broadcast_to(dbacc, (8, 128))

    cspec = pl.BlockSpec((RT, 128), lambda j: (0, j))
    wspec = pl.BlockSpec((8, 128), lambda j: (0, j))
    return pl.pallas_call(
        body, name="conv_bwd", grid=(12,),
        in_specs=[cspec, cspec, pl.BlockSpec((RT, 128), lambda j: (0, _xbc_colblk(j))),
                  wspec, pl.BlockSpec((1, 128), lambda j: (0, j)), pl.BlockSpec(memory_space=pl.ANY)],
        out_specs=[pl.BlockSpec((RT, 128), lambda j: (0, _xbc_colblk(j))), wspec, wspec],
        out_shape=[jax.ShapeDtypeStruct(dp.shape, dp.dtype), jax.ShapeDtypeStruct((8, 1536), F32),
                   jax.ShapeDtypeStruct((8, 1536), F32)],
        scratch_shapes=[pltpu.VMEM((RT, 128), F32)],
        input_output_aliases={5: 0},
        compiler_params=_params(("parallel",)),
    )(dxf, dxb, p, conv_w8, conv_b, dp)


def _dt_bwd(ddf, ddb, dp):
    RT = ddf.shape[0]

    def body(f_ref, b_ref, dp_any, o_ref, st_ref):
        del dp_any

        @pl.when(pl.program_id(0) == 0)
        def _():
            st_ref[...] = jnp.zeros_like(st_ref)

        s = f_ref[...] + b_ref[...]
        o_ref[...] = s.astype(o_ref.dtype)
        st_ref[0:1, :] += _colsum(s)

    return pl.pallas_call(
        body, name="dt_bwd", grid=(RT // TL,),
        in_specs=[_rt(128), _rt(128), pl.BlockSpec(memory_space=pl.ANY)],
        out_specs=[_rt(128, ODT // 128), _cst((8, 128))],
        out_shape=[jax.ShapeDtypeStruct(dp.shape, dp.dtype), jax.ShapeDtypeStruct((8, 128), F32)],
        input_output_aliases={2: 0},
        compiler_params=_params(("arbitrary",)),
    )(ddf, ddb, dp)


def _ln0_bwd(dh1, dr1, x, ctx, g, b, modx, modc):
    L = x.shape[0]
    nt = L // TL

    def body(dh_ref, dr1_ref, x_ref, c_ref, g_ref, b_ref, mx_ref, mc_ref, gx_ref, st_ref):
        i = pl.program_id(0)
        isc = i == nt

        @pl.when(i == 0)
        def _():
            st_ref[...] = jnp.zeros_like(st_ref)

        xin = jnp.where(isc, c_ref[...], x_ref[...])
        xhat, rstd = _ln(xin)
        xn = xhat * g_ref[...] + b_ref[...]
        sc = jnp.where(isc, mc_ref[1:2, :], mx_ref[1:2, :])
        dh = dh_ref[...]
        lat = jnp.where(isc, 0.0, 1.0)
        dxn = dh * (1.0 + sc) + (lat * ALPHA) * dr1_ref[...]
        tsh = _colsum(dh)
        tsc = _colsum(dh * xn)
        st_ref[0:1, :] += lat * tsh
        st_ref[1:2, :] += lat * tsc
        st_ref[2:3, :] += (1.0 - lat) * tsh
        st_ref[3:4, :] += (1.0 - lat) * tsc
        st_ref[4:5, :] += _colsum(dxn * xhat)
        st_ref[5:6, :] += _colsum(dxn)

        @pl.when(i < nt)
        def _():
            gx_ref[...] = _ln_bwd(dxn * g_ref[...], xhat, rstd)

    return pl.pallas_call(
        body, name="ln0_bwd", grid=(nt + 1,),
        in_specs=[_rt(D), _rtc(D, nt), _rtc(D, nt), _cst((TL, D)), _cst((1, D)), _cst((1, D)),
                  _cst((8, D)), _cst((8, D))],
        out_specs=[_rtc(D, nt), _cst((8, D))],
        out_shape=[jax.ShapeDtypeStruct((L, D), F32), jax.ShapeDtypeStruct((8, D), F32)],
        compiler_params=_params(("arbitrary",)),
    )(dh1, dr1, x, ctx, g, b, modx, modc)


def _perm_cols(w):
    pad = jnp.zeros((w.shape[0], NPJ - NNAT), w.dtype)
    return jnp.concatenate([w[:, 0:1024], w[:, 2592:3616], w[:, 3616:4640], w[:, 1024:2048],
                            w[:, 4640:6688], w[:, 2048:2304], w[:, 2304:2560], w[:, 2560:2592], pad],
                           axis=1)


SECTIONS = ((0, 1024, OZ), (1024, 2048, OXS), (2048, 2304, OB), (2304, 2560, OC), (2560, 2592, ODT),
            (2592, 3616, OU), (3616, 4640, OV), (4640, 6688, OG))


def _perm_from_blocks(ga):
    n = ga.shape[2]
    pieces = []
    for na, nb, _ in sorted(SECTIONS, key=lambda sec: sec[2]):
        for k in range(NDEV):
            lo, hi = max(na, k * n), min(nb, (k + 1) * n)
            if lo < hi:
                pieces.append(ga[k][:, lo - k * n:hi - k * n])
    pieces.append(jnp.zeros((ga.shape[1], NPJ - NNAT), ga.dtype))
    return jnp.concatenate(pieces, axis=1)


def _blocks_from_perm(gp, n):
    blocks = []
    for k in range(NDEV):
        pieces = []
        for na, nb, po in SECTIONS:
            lo, hi = max(na, k * n), min(nb, (k + 1) * n)
            if lo < hi:
                pieces.append(gp[:, po + lo - na:po + hi - na])
        blocks.append(jnp.concatenate(pieces, axis=1))
    return jnp.stack(blocks)


def _padded(n, row_align):
    unit = row_align * D
    return -(-n // unit) * unit if row_align else n


def _slab(arrs, rows, row_align=0):
    parts = []
    for a in arrs:
        f = a.reshape(-1)
        parts.append(jnp.pad(f, (0, _padded(f.shape[0], row_align) - f.shape[0])))
    flat = jnp.concatenate(parts)
    flat = jnp.pad(flat, (0, rows * D - flat.shape[0]))
    return flat.reshape(rows, D)


def _unslab(slab, shapes, row_align=0):
    out, off = [], 0
    for shp in shapes:
        n = 1
        for s in shp:
            n *= s
        r0, r1 = off // D, -(-(off + n) // D)
        out.append(slab[r0:r1].reshape(-1)[off - r0 * D:off - r0 * D + n].reshape(shp))
        off += _padded(n, row_align)
    return out


def _row(v):
    return v.reshape(1, -1)


def _pad_rows(a, rows):
    return jnp.pad(a, ((0, rows - a.shape[0]), (0, 0)))


BIG = ["w_in", "w_ssd_proj", "w_gm_proj", "w_out", "w_ff1", "w_ff3", "w_ff2"]
BIG_ROWS = 2304
BIG_ALIGN = 16
REPL = ["c_ctx", "ln0_g", "ln0_b", "b_ada", "conv_b", "dt_bias", "a_log", "d_skip", "ssd_norm_g",
        "gm_norm_g", "gm_norm_b", "w_spatial", "b_spatial", "b_gate", "ln1_g", "ln1_b", "ln2_g", "ln2_b"]
SMALL_ROWS = 160
WEIGHTS = ["c_ctx", "ln0_g", "ln0_b", "w_ada", "b_ada", "w_in", "conv_w", "conv_b", "dt_bias", "a_log",
           "d_skip", "ssd_norm_g", "gm_norm_g", "gm_norm_b", "w_spatial", "b_spatial", "b_gate",
           "w_ssd_proj", "w_gm_proj", "w_out", "ln1_g", "ln1_b", "w_ff1", "w_ff3", "w_ff2", "ln2_g", "ln2_b"]


def kernel(x, c, ctx, c_ctx, ln0_g, ln0_b, w_ada, b_ada, w_in, conv_w, conv_b, dt_bias, a_log, d_skip, ssd_norm_g, gm_norm_g, gm_norm_b, w_spatial, b_spatial, b_gate, w_ssd_proj, w_gm_proj, w_out, ln1_g, ln1_b, w_ff1, w_ff3, w_ff2, ln2_g, ln2_b, loss_target, m_c_ctx, m_ln0_g, m_ln0_b, m_w_ada, m_b_ada, m_w_in, m_conv_w, m_conv_b, m_dt_bias, m_a_log, m_d_skip, m_ssd_norm_g, m_gm_norm_g, m_gm_norm_b, m_w_spatial, m_b_spatial, m_b_gate, m_w_ssd_proj, m_w_gm_proj, m_w_out, m_ln1_g, m_ln1_b, m_w_ff1, m_w_ff3, m_w_ff2, m_ln2_g, m_ln2_b, v_c_ctx, v_ln0_g, v_ln0_b, v_w_ada, v_b_ada, v_w_in, v_conv_w, v_conv_b, v_dt_bias, v_a_log, v_d_skip, v_ssd_norm_g, v_gm_norm_g, v_gm_norm_b, v_w_spatial, v_b_spatial, v_b_gate, v_w_ssd_proj, v_w_gm_proj, v_w_out, v_ln1_g, v_ln1_b, v_w_ff1, v_w_ff3, v_w_ff2, v_ln2_g, v_ln2_b):
    W = dict(c_ctx=c_ctx, ln0_g=ln0_g, ln0_b=ln0_b, w_ada=w_ada, b_ada=b_ada, w_in=w_in, conv_w=conv_w,
             conv_b=conv_b, dt_bias=dt_bias, a_log=a_log, d_skip=d_skip, ssd_norm_g=ssd_norm_g,
             gm_norm_g=gm_norm_g, gm_norm_b=gm_norm_b, w_spatial=w_spatial, b_spatial=b_spatial,
             b_gate=b_gate, w_ssd_proj=w_ssd_proj, w_gm_proj=w_gm_proj, w_out=w_out, ln1_g=ln1_g,
             ln1_b=ln1_b, w_ff1=w_ff1, w_ff3=w_ff3, w_ff2=w_ff2, ln2_g=ln2_g, ln2_b=ln2_b)
    M = dict(c_ctx=m_c_ctx, ln0_g=m_ln0_g, ln0_b=m_ln0_b, w_ada=m_w_ada, b_ada=m_b_ada, w_in=m_w_in,
             conv_w=m_conv_w, conv_b=m_conv_b, dt_bias=m_dt_bias, a_log=m_a_log, d_skip=m_d_skip,
             ssd_norm_g=m_ssd_norm_g, gm_norm_g=m_gm_norm_g, gm_norm_b=m_gm_norm_b,
             w_spatial=m_w_spatial, b_spatial=m_b_spatial, b_gate=m_b_gate, w_ssd_proj=m_w_ssd_proj,
             w_gm_proj=m_w_gm_proj, w_out=m_w_out, ln1_g=m_ln1_g, ln1_b=m_ln1_b, w_ff1=m_w_ff1,
             w_ff3=m_w_ff3, w_ff2=m_w_ff2, ln2_g=m_ln2_g, ln2_b=m_ln2_b)
    V = dict(c_ctx=v_c_ctx, ln0_g=v_ln0_g, ln0_b=v_ln0_b, w_ada=v_w_ada, b_ada=v_b_ada, w_in=v_w_in,
             conv_w=v_conv_w, conv_b=v_conv_b, dt_bias=v_dt_bias, a_log=v_a_log, d_skip=v_d_skip,
             ssd_norm_g=v_ssd_norm_g, gm_norm_g=v_gm_norm_g, gm_norm_b=v_gm_norm_b,
             w_spatial=v_w_spatial, b_spatial=v_b_spatial, b_gate=v_b_gate, w_ssd_proj=v_w_ssd_proj,
             w_gm_proj=v_w_gm_proj, w_out=v_w_out, ln1_g=v_ln1_g, ln1_b=v_ln1_b, w_ff1=v_w_ff1,
             w_ff3=v_w_ff3, w_ff2=v_w_ff2, ln2_g=v_ln2_g, ln2_b=v_ln2_b)

    me = 4 * lax.axis_index("x") + 2 * lax.axis_index("y") + lax.axis_index("c")
    xl, cx, tgt = x[0], ctx[0], loss_target[0]
    L = xl.shape[0]
    assert cx.shape[0] == TL and L % TL == 0
    ada_n = w_ada.shape[2]
    cw_n = conv_w.shape[2]

    small1 = _pad_rows(jnp.concatenate([c, _slab([conv_w[0]], 1)], axis=0), 8)
    g1 = _all_gather(small1, "ag_small")
    c_all = g1[:, 0, :]
    conv_w_full = g1[:, 1, :5 * cw_n].reshape(NDEV, 5, cw_n).transpose(1, 0, 2).reshape(5, NDEV * cw_n)
    sq = w_ssd_proj.shape[1]
    ffr = w_ff2.shape[1]
    ffc = w_ff1.shape[2]
    late = [jnp.concatenate([w_ssd_proj[0], w_gm_proj[0], w_out[0], w_ff2[0]], axis=0).astype(_MXU),
            w_ff1[0].astype(_MXU), w_ff3[0].astype(_MXU)]

    c16 = _pad_rows(jnp.concatenate([c_all, _row(c_ctx)], axis=0), 16)
    b_ada_sh = lax.dynamic_slice(b_ada, (0, ada_n * me), (1, ada_n))
    modp = _ada_fwd(c16, w_ada[0], b_ada_sh)
    mod16 = _all_gather(modp, "ag_mod").transpose(1, 0, 2).reshape(16, NDEV * ada_n)

    ga, = _all_gather_multi([w_in[0].astype(_MXU)], "ag_w_in")
    ga, late, mod16 = lax.optimization_barrier((ga, late, mod16))
    lw_send, lw_recv, lw_src, lw_land, lw_token = _exchange_start(late, "ag_late_start", gather=True)
    w_in_p = _perm_from_blocks(ga)
    modx = _pad_rows(lax.dynamic_slice(mod16, (me, 0), (1, 6 * D)).reshape(6, D), 8) + lw_token[0, 0]
    modc = _pad_rows(mod16[8].reshape(6, D), 8)

    g0, b0 = _row(ln0_g), _row(ln0_b)
    xn, h1 = _ln0_fwd(xl, cx, g0, b0, modx, modc)
    p = _mm(h1, w_in_p, "nn", F32, "mm_p")
    conv_w8 = _pad_rows(conv_w_full, 8)
    xbc = _conv_fwd(p, conv_w8, conv_b)
    prm = _pad_rows(jnp.pad(jnp.stack([dt_bias.reshape(32), a_log.reshape(32)]), ((0, 0), (0, 96))), 8)
    yf, yb, hpf, hpb = _ssd2_fwd(xbc, p, prm)
    lw_land = _exchange_wait(lw_send, lw_recv, lw_src, lw_land, yf, "ag_late_wait", gather=True)
    fw_send, fw_recv, lw_land, fw_token = _forward_start(lw_land, "ag_fwd_start")
    dsk = _row(jnp.repeat(d_skip[0, 0] + d_skip[0, 1], HP)) + fw_token[0:1, 0:1]
    ws_m = w_spatial[0].astype(_MXU)
    bsT = jnp.pad(b_spatial[0].T, ((0, 0), (0, 120)))
    mixp = (dsk, ssd_norm_g, gm_norm_g, gm_norm_b, ws_m, bsT)
    yssd, ygm = _mix_fwd(yf, yb, p, xbc, *mixp)
    gb, gc1, gc2 = _forward_wait(fw_send, fw_recv, lw_land, yssd, "ag_fwd_wait")

    def with_own(g, mine, k):
        return jnp.where(me == k, mine, g[k])

    gb = jnp.stack([with_own(gb, late[0], k) for k in range(NDEV)])
    w_ssd_f = gb[:, 0:sq].reshape(NDEV * sq, D)
    w_gm_f = gb[:, sq:2 * sq].reshape(NDEV * sq, D)
    w_out_f = gb[:, 2 * sq:3 * sq].reshape(NDEV * sq, D)
    w_ff2_f = gb[:, 3 * sq:3 * sq + ffr].reshape(NDEV * ffr, D)
    w13 = jnp.concatenate([with_own(gc1, late[1], k) for k in range(NDEV)]
                          + [with_own(gc2, late[2], k) for k in range(NDEV)], axis=1)
    a1 = _mm(yssd, w_ssd_f, "nn", F32, "mm_a1")
    a2 = _mm(ygm, w_gm_f, "nn", F32, "mm_a2")
    merged = _gate_fwd(a1, a2, p, b_gate)
    out = _mm(merged, w_out_f, "nn", F32, "mm_out")
    r1, h2 = _res1_fwd(xn, out, modx, ln1_g, ln1_b)
    f13 = _mm(h2, w13, "nn", F32, "mm_f13")
    ff = _glu_fwd(f13)
    o2 = _mm(ff, w_ff2_f, "nn", F32, "mm_o2")

    dr2, do2, st2, loss_slab = _res2(r1, o2, tgt, modx, ln1_g, ln1_b, ln2_g, ln2_b)
    loss = lax.psum(loss_slab[0, 0], ("x", "y", "c"))
    dff = _mm(do2, w_ff2_f, "nt", F32, "mm_dff")
    df13 = _glu_bwd(dff, f13)
    dh2 = _mm(df13, w13, "nt", F32, "mm_dh2")
    dw_ff2 = _mm(ff, do2, "tn", _MXU, "mm_dw_ff2")
    dw13 = _mm(h2, df13, "tn", _MXU, "mm_dw13")
    xff = [dw_ff2.reshape(NDEV, ffr, D),
           jnp.stack([dw13[:, k * ffc:(k + 1) * ffc] for k in range(NDEV)]),
           jnp.stack([dw13[:, DFF + k * ffc:DFF + (k + 1) * ffc] for k in range(NDEV)])]
    ff_send, ff_recv, ff_src, ff_land, ff_token = _exchange_start(xff, "xchg_ff_start")
    modx = modx + ff_token[0, 0]
    dr1, dout, st1 = _res1_bwd(dr2, dh2, r1, out, modx, ln1_g, ln1_b)
    dmg = _mm(dout, w_out_f, "nt", F32, "mm_dmerged")
    dw_out = _mm(merged, dout, "tn", _MXU, "mm_dw_out")
    dp = jnp.zeros((L + TL, NPJ), _MXU)
    dp, da1, da2, stg = _gate_bwd(dmg, a1, a2, p, b_gate, dp)
    dys = _mm(da1, w_ssd_f, "nt", F32, "mm_dyssd")
    dym = _mm(da2, w_gm_f, "nt", F32, "mm_dygm")
    dw_ssd = _mm(yssd, da1, "tn", _MXU, "mm_dw_ssd")
    dw_gm = _mm(ygm, da2, "tn", _MXU, "mm_dw_gm")
    xsq = [jnp.concatenate([dw_ssd.reshape(NDEV, sq, D), dw_gm.reshape(NDEV, sq, D),
                            dw_out.reshape(NDEV, sq, D)], axis=1)]
    sq_send, sq_recv, sq_src, sq_land, sq_token = _exchange_start(xsq, "xchg_sq_start")
    mixp = (dsk + sq_token[0:1, 0:1],) + mixp[1:]
    dp, dyd, stm, dws, dbsT = _mix_bwd(dys, dym, yf, yb, p, xbc, dp, *mixp)
    dxf, dxb, ddf, ddb, sts = _ssd2_bwd(xbc, p, prm, dsk, dyd, hpf, hpb)
    dp, dcw, dcb = _conv_bwd(dxf, dxb, p, conv_w8, conv_b, dp)
    dp, std = _dt_bwd(ddf, ddb, dp)
    dh1 = _mm(dp, w_in_p, "nt", F32, "mm_dh1")
    dw_in_p = _mm(h1, dp, "tn", _MXU, "mm_dw_in")
    xin = [_blocks_from_perm(dw_in_p, w_in.shape[2])]
    in_send, in_recv, in_src, in_land, in_token = _exchange_start(xin, "xchg_in_start")
    modx = modx + in_token[0, 0]
    grad_x, st0 = _ln0_bwd(dh1, dr1, xl, cx, g0, b0, modx, modc)

    zero = jnp.zeros((D,), F32)
    dmod = jnp.stack([jnp.concatenate([st0[0], st0[1], st1[4], st1[1], st1[0], st2[2]]),
                      jnp.concatenate([st0[2], st0[3], zero, zero, zero, zero])])
    g16 = _all_gather(_pad_rows(dmod, 8), "ag_dmod")[:, 0:2, :].reshape(16, 6 * D)
    g16_sh = lax.dynamic_slice(g16, (0, ada_n * me), (16, ada_n))
    c16b = jnp.stack([c_all, jnp.broadcast_to(_row(c_ctx), (NDEV, D))], axis=1).reshape(16, D)
    dw_ada, db_ada8, dcc8 = _ada_bwd(c16b, g16, g16_sh, w_ada[0])

    part = dict(
        c_ctx=dcc8[0], ln0_g=st0[4], ln0_b=st0[5], conv_w=dcw[0:5], conv_b=dcb[0],
        dt_bias=std[0, 0:32], a_log=sts[0, 0:32], d_skip=jnp.tile(sts[1, 0:16], 2),
        ssd_norm_g=stm[0], gm_norm_g=stm[1], gm_norm_b=stm[2], w_spatial=dws,
        b_spatial=dbsT[:, 0:8].T, b_gate=stg[0], ln1_g=st1[2], ln1_b=st1[3], ln2_g=st2[0], ln2_b=st2[1])
    pnames = list(part)
    psum8 = _sum8(_all_gather(_slab([part[n] for n in pnames], SMALL_ROWS), "ag_smallgrads"), "sum_smallgrads")
    small = dict(zip(pnames, _unslab(psum8, [part[n].shape for n in pnames])))
    grads = {n: small[n].reshape(W[n].shape) for n in pnames if n != "conv_w"}
    grads["conv_w"] = lax.dynamic_slice(small["conv_w"], (0, cw_n * me), (5, cw_n)).reshape(conv_w.shape)
    grads["b_ada"] = db_ada8[0:1]
    grads["w_ada"] = dw_ada.reshape(w_ada.shape)

    delta, new_m, new_v = {}, {}, {}

    def adam_group(names, rows, tag, align=0):
        shapes = [W[n].shape for n in names]
        outs = _adamw(*[_slab([src[n] for n in names], rows, align) for src in (grads, W, M, V)], tag)
        for res, slab in zip((delta, new_m, new_v), outs):
            for n, a in zip(names, _unslab(slab, shapes, align)):
                res[n] = a

    adam_group(REPL + ["conv_w"], SMALL_ROWS, "adamw_small")
    res = _adamw(grads["w_ada"][0], w_ada[0], m_w_ada[0], v_w_ada[0], "adamw_w_ada")
    delta["w_ada"], new_m["w_ada"], new_v["w_ada"] = [a[None] for a in res]

    rff = _exchange_wait(ff_send, ff_recv, ff_src, ff_land, st0, "xchg_ff_wait")
    rsq = _exchange_wait(sq_send, sq_recv, sq_src, sq_land, rff[0], "xchg_sq_wait")
    rin = _exchange_wait(in_send, in_recv, in_src, in_land, delta["ln2_b"], "xchg_in_wait")

    def own(blocks):
        return lax.dynamic_index_in_dim(blocks, me, 0, keepdims=False)

    for n, r8, mine, row0, tr in (
            ("w_ff2", rff[0], own(xff[0]), 0, ffr // 2), ("w_ff1", rff[1], own(xff[1]), 0, 256),
            ("w_ff3", rff[2], own(xff[2]), 0, 256), ("w_ssd_proj", rsq[0], own(xsq[0]), 0, sq),
            ("w_gm_proj", rsq[0], own(xsq[0]), sq, sq), ("w_out", rsq[0], own(xsq[0]), 2 * sq, sq),
            ("w_in", rin[0], own(xin[0]), 0, 256)):
        res = _adamw_sum(r8, mine, W[n][0], M[n][0], V[n][0], row0, tr, "adamw_" + n)
        grads[n], delta[n], new_m[n], new_v[n] = [a[None] for a in res]

    return (loss, grad_x[None], *[grads[n] for n in WEIGHTS], *[delta[n] for n in WEIGHTS],
            *[new_m[n] for n in WEIGHTS], *[new_v[n] for n in WEIGHTS])
```

```python
import functools

import jax
import jax.numpy as jnp
from jax import lax
from jax.experimental import pallas as pl
from jax.experimental.pallas import tpu as pltpu

_MXU = jnp.bfloat16
F32 = jnp.float32
D = 1024
TL = 256
Q = 128
NH, HP, NS, HPG = 16, 64, 128, 8
DFF = 2816
ALPHA = 2.0 ** 0.25
EPS = 1e-5
OZ, OU, OV, OXS, OG, OB, OC, ODT, NPJ = 0, 1024, 2048, 3072, 4096, 6144, 6400, 6656, 6912
NNAT = 6688
NDEV = 8
ADAM_LR, ADAM_B1, ADAM_B2, ADAM_EPS, ADAM_WD, ADAM_STEP = 1e-3, 0.9, 0.999, 1e-8, 0.01, 10
VMEM_LIMIT = 48 * 1024 * 1024

NN = ((1,), (0,))
NT = ((1,), (1,))
TN = ((0,), (0,))
MESH = pl.DeviceIdType.MESH


def _dot(a, b, dims):
    return lax.dot_general(a.astype(_MXU), b.astype(_MXU), (dims, ((), ())),
                           preferred_element_type=F32)


def _tile(n, cands):
    for c in cands:
        if n % c == 0:
            return c
    return n


def _divisor_tile(n, cap, mult):
    best = n
    for t in range(mult, min(n, cap) + 1, mult):
        if n % t == 0:
            best = t
    return best


def _params(sem):
    return pltpu.CompilerParams(dimension_semantics=sem, vmem_limit_bytes=VMEM_LIMIT)


def _cst(shape):
    nd = len(shape)
    return pl.BlockSpec(shape, lambda *_: (0,) * nd)


def _rt(w, cb=0, rows=TL):
    return pl.BlockSpec((rows, w), lambda i: (i, cb))


def _rtc(w, nt, cb=0):
    return pl.BlockSpec((TL, w), lambda i: (jnp.minimum(i, nt - 1), cb))


def _sig(x):
    return jax.nn.sigmoid(x)


def _softplus(x):
    return jnp.maximum(x, 0.0) + jnp.log1p(jnp.exp(-jnp.abs(x)))


_G0, _G1 = 0.7978845608028654, 0.044715


def _gelu(x):
    t = jnp.tanh(_G0 * (x + _G1 * x * x * x))
    return 0.5 * x * (1.0 + t), t


def _gelu_grad(x, t):
    return 0.5 * (1.0 + t) + 0.5 * x * (1.0 - t * t) * _G0 * (1.0 + 3.0 * _G1 * x * x)


def _ln(r):
    mu = jnp.mean(r, axis=-1, keepdims=True)
    xc = r - mu
    var = jnp.mean(xc * xc, axis=-1, keepdims=True)
    rstd = lax.rsqrt(var + EPS)
    return xc * rstd, rstd


def _ln_bwd(dyh, xhat, rstd):
    return rstd * (dyh - jnp.mean(dyh, axis=-1, keepdims=True)
                   - xhat * jnp.mean(dyh * xhat, axis=-1, keepdims=True))


def _colsum(v):
    return jnp.sum(v, axis=0, keepdims=True)


def _sum11(v):
    return jnp.sum(jnp.sum(v, axis=1, keepdims=True), axis=0, keepdims=True)


def _cumsum_rows(a, rev):
    n = a.shape[0]
    row = lax.broadcasted_iota(jnp.int32, a.shape, 0)
    s = 1
    while s < n:
        if rev:
            a = a + jnp.where(row < n - s, pltpu.roll(a, n - s, 0), 0.0)
        else:
            a = a + jnp.where(row >= s, pltpu.roll(a, s, 0), 0.0)
        s *= 2
    return a


def _mm(a, b, mode, out_dtype, name):
    if mode == "tn":
        K, M = a.shape
    else:
        M, K = a.shape
    N = b.shape[0] if mode == "nt" else b.shape[1]
    tm = _divisor_tile(M, 1408, 128) if mode == "tn" else _divisor_tile(M, 1088, 16)
    tn = _divisor_tile(N, 1408, 128)
    tk = _divisor_tile(K, 2304, 128)
    nk = K // tk
    dims = {"nn": NN, "nt": NT, "tn": TN}[mode]
    use_acc = nk > 1 and out_dtype != F32

    def body(a_ref, b_ref, o_ref, *acc):
        prod = _dot(a_ref[...], b_ref[...], dims)
        if nk == 1:
            o_ref[...] = prod.astype(o_ref.dtype)
            return
        acc_ref = acc[0] if use_acc else o_ref
        k = pl.program_id(2)

        @pl.when(k == 0)
        def _():
            acc_ref[...] = prod

        if use_acc:
            @pl.when((k > 0) & (k < nk - 1))
            def _():
                acc_ref[...] += prod

            @pl.when(k == nk - 1)
            def _():
                o_ref[...] = (acc_ref[...] + prod).astype(o_ref.dtype)
        else:
            @pl.when(k > 0)
            def _():
                o_ref[...] += prod

    if mode == "tn":
        a_spec = pl.BlockSpec((tk, tm), lambda i, j, k: (k, i))
    else:
        a_spec = pl.BlockSpec((tm, tk), lambda i, j, k: (i, k))
    if mode == "nt":
        b_spec = pl.BlockSpec((tn, tk), lambda i, j, k: (j, k))
    else:
        b_spec = pl.BlockSpec((tk, tn), lambda i, j, k: (k, j))
    return pl.pallas_call(
        body, name=name, grid=(M // tm, N // tn, nk),
        in_specs=[a_spec, b_spec],
        out_specs=pl.BlockSpec((tm, tn), lambda i, j, k: (i, j)),
        out_shape=jax.ShapeDtypeStruct((M, N), out_dtype),
        scratch_shapes=[pltpu.VMEM((tm, tn), F32)] if use_acc else [],
        compiler_params=_params(("parallel", "parallel", "arbitrary")),
    )(a, b)


def _all_gather(x, name):
    def body(x_ref, out_ref, send_sems, recv_sems, local_sem):
        mx, my, mc = lax.axis_index("x"), lax.axis_index("y"), lax.axis_index("c")
        me, sibling = (mx, my, mc), (mx, my, 1 - mc)
        chips = [(1 - mx, my), (mx, 1 - my), (1 - mx, 1 - my)]

        def slot(px, py, pc):
            return out_ref.at[4 * px + 2 * py + pc]

        def copy(k, block, to, src=None):
            return pltpu.make_async_remote_copy(
                src_ref=slot(*block) if src is None else src, dst_ref=slot(*block),
                send_sem=send_sems.at[k], recv_sem=recv_sems.at[k],
                device_id=to, device_id_type=MESH)

        mine = pltpu.make_async_copy(x_ref, slot(*me), local_sem)
        mine.start()
        first = [copy(0, me, sibling, src=x_ref)]
        first += [copy(1 + j, me, (*chip, mc), src=x_ref) for j, chip in enumerate(chips)]
        for cp in first:
            cp.start()
        passed = [copy(4 + j, (*chip, mc), sibling) for j, chip in enumerate(chips)]
        for j, chip in enumerate(chips):
            copy(1 + j, (*chip, mc), me).wait_recv()
            passed[j].start()
        copy(0, sibling, me).wait_recv()
        for j, chip in enumerate(chips):
            copy(4 + j, (*chip, 1 - mc), me).wait_recv()
        for cp in first + passed:
            cp.wait_send()
        mine.wait()

    return pl.pallas_call(
        body, name=name,
        out_shape=jax.ShapeDtypeStruct((NDEV,) + x.shape, x.dtype),
        in_specs=[pl.BlockSpec(memory_space=pl.ANY)],
        out_specs=pl.BlockSpec(memory_space=pl.ANY),
        scratch_shapes=[pltpu.SemaphoreType.DMA((7,)), pltpu.SemaphoreType.DMA((7,)),
                        pltpu.SemaphoreType.DMA],
    )(x)


def _owner_exchange(g, name):
    def body(g_ref, out_ref, send_sems, recv_sems, local_sem):
        mx, my, mc = lax.axis_index("x"), lax.axis_index("y"), lax.axis_index("c")
        local = pltpu.make_async_copy(g_ref.at[4 * mx + 2 * my + mc], out_ref.at[0], local_sem)
        local.start()
        copies = []
        for f in range(1, NDEV):
            px = 1 - mx if (f >> 2) & 1 else mx
            py = 1 - my if (f >> 1) & 1 else my
            pc = 1 - mc if f & 1 else mc
            cp = pltpu.make_async_remote_copy(
                src_ref=g_ref.at[4 * px + 2 * py + pc], dst_ref=out_ref.at[f],
                send_sem=send_sems.at[f - 1], recv_sem=recv_sems.at[f - 1],
                device_id=(px, py, pc), device_id_type=MESH)
            cp.start()
            copies.append(cp)
        for cp in copies:
            cp.wait_recv()
        for cp in copies:
            cp.wait_send()
        local.wait()

    return pl.pallas_call(
        body, name=name,
        out_shape=jax.ShapeDtypeStruct(g.shape, g.dtype),
        in_specs=[pl.BlockSpec(memory_space=pl.ANY)],
        out_specs=pl.BlockSpec(memory_space=pl.ANY),
        scratch_shapes=[pltpu.SemaphoreType.DMA((7,)), pltpu.SemaphoreType.DMA((7,)),
                        pltpu.SemaphoreType.DMA],
    )(g)


def _any_specs(n):
    return [pl.BlockSpec(memory_space=pl.ANY)] * n


def _all_gather_multi(xs, name):
    na = len(xs)

    def body(*refs):
        x_refs, out_refs = refs[:na], refs[na:2 * na]
        send_sems, recv_sems, local_sems = refs[2 * na:]
        mx, my, mc = lax.axis_index("x"), lax.axis_index("y"), lax.axis_index("c")
        me, sibling = (mx, my, mc), (mx, my, 1 - mc)
        chips = [(1 - mx, my), (mx, 1 - my), (1 - mx, 1 - my)]

        def copy(a, k, block, to, src=None):
            slot = out_refs[a].at[4 * block[0] + 2 * block[1] + block[2]]
            return pltpu.make_async_remote_copy(
                src_ref=slot if src is None else src, dst_ref=slot,
                send_sem=send_sems.at[7 * a + k], recv_sem=recv_sems.at[7 * a + k],
                device_id=to, device_id_type=MESH)

        mine = [pltpu.make_async_copy(x_refs[a], out_refs[a].at[4 * mx + 2 * my + mc], local_sems.at[a])
                for a in range(na)]
        for cp in mine:
            cp.start()
        first = []
        for a in range(na):
            first.append(copy(a, 0, me, sibling, src=x_refs[a]))
            first += [copy(a, 1 + j, me, (*chip, mc), src=x_refs[a]) for j, chip in enumerate(chips)]
        for cp in first:
            cp.start()
        passed = []
        for a in range(na):
            for j, chip in enumerate(chips):
                copy(a, 1 + j, (*chip, mc), me).wait_recv()
                fwd = copy(a, 4 + j, (*chip, mc), sibling)
                fwd.start()
                passed.append(fwd)
        for a in range(na):
            copy(a, 0, sibling, me).wait_recv()
            for j, chip in enumerate(chips):
                copy(a, 4 + j, (*chip, 1 - mc), me).wait_recv()
        for cp in first + passed:
            cp.wait_send()
        for cp in mine:
            cp.wait()

    return pl.pallas_call(
        body, name=name,
        out_shape=[jax.ShapeDtypeStruct((NDEV,) + x.shape, x.dtype) for x in xs],
        in_specs=_any_specs(na), out_specs=_any_specs(na),
        scratch_shapes=[pltpu.SemaphoreType.DMA((7 * na,)), pltpu.SemaphoreType.DMA((7 * na,)),
                        pltpu.SemaphoreType.DMA((na,))],
    )(*xs)


def _owner_exchange_multi(gs, name):
    na = len(gs)

    def body(*refs):
        g_refs, out_refs = refs[:na], refs[na:2 * na]
        send_sems, recv_sems, local_sems = refs[2 * na:]
        mx, my, mc = lax.axis_index("x"), lax.axis_index("y"), lax.axis_index("c")
        locals_ = [pltpu.make_async_copy(g_refs[a].at[4 * mx + 2 * my + mc], out_refs[a].at[0], local_sems.at[a])
                   for a in range(na)]
        for cp in locals_:
            cp.start()
        copies = []
        for a in range(na):
            for f in range(1, NDEV):
                px = 1 - mx if (f >> 2) & 1 else mx
                py = 1 - my if (f >> 1) & 1 else my
                pc = 1 - mc if f & 1 else mc
                cp = pltpu.make_async_remote_copy(
                    src_ref=g_refs[a].at[4 * px + 2 * py + pc], dst_ref=out_refs[a].at[f],
                    send_sem=send_sems.at[7 * a + f - 1], recv_sem=recv_sems.at[7 * a + f - 1],
                    device_id=(px, py, pc), device_id_type=MESH)
                cp.start()
                copies.append(cp)
        for cp in copies:
            cp.wait_recv()
        for cp in copies:
            cp.wait_send()
        for cp in locals_:
            cp.wait()

    return pl.pallas_call(
        body, name=name,
        out_shape=[jax.ShapeDtypeStruct(g.shape, g.dtype) for g in gs],
        in_specs=_any_specs(na), out_specs=_any_specs(na),
        scratch_shapes=[pltpu.SemaphoreType.DMA((7 * na,)), pltpu.SemaphoreType.DMA((7 * na,)),
                        pltpu.SemaphoreType.DMA((na,))],
    )(*gs)


def _adamw_sum(r8, own, w, m, v, row0, tr, name):
    R, C = w.shape
    assert row0 % tr == 0
    blk0 = row0 // tr
    bc1 = 1.0 - ADAM_B1 ** ADAM_STEP
    bc2 = 1.0 - ADAM_B2 ** ADAM_STEP

    def body(r_ref, *refs):
        if own is None:
            gg = r_ref[0].astype(F32)
        else:
            gg = refs[0][...].astype(F32)
            refs = refs[1:]
        w_ref, m_ref, v_ref, g_ref, d_ref, mo_ref, vo_ref = refs
        for k in range(1, NDEV):
            gg = gg + r_ref[k].astype(F32)
        mn = ADAM_B1 * m_ref[...] + (1.0 - ADAM_B1) * gg
        vn = ADAM_B2 * v_ref[...] + (1.0 - ADAM_B2) * (gg * gg)
        mh = mn / bc1
        vh = vn / bc2
        g_ref[...] = gg
        d_ref[...] = -ADAM_LR * (mh / (jnp.sqrt(vh) + ADAM_EPS) + ADAM_WD * w_ref[...])
        mo_ref[...] = mn
        vo_ref[...] = vn

    spec = pl.BlockSpec((tr, C), lambda i: (i, 0))
    sh = jax.ShapeDtypeStruct((R, C), F32)
    own_ops = [] if own is None else [own]
    own_specs = [] if own is None else [pl.BlockSpec((tr, C), lambda i: (i + blk0, 0))]
    return pl.pallas_call(
        body, name=name, grid=(R // tr,),
        in_specs=[pl.BlockSpec((NDEV, tr, C), lambda i: (0, i + blk0, 0))] + own_specs + [spec, spec, spec],
        out_specs=[spec] * 4, out_shape=[sh] * 4, compiler_params=_params(("parallel",)),
    )(r8, *own_ops, w, m, v)


_HBM = pl.BlockSpec(memory_space=pltpu.HBM)
_SEM = pl.BlockSpec(memory_space=pltpu.SEMAPHORE)
_EFFECT = pltpu.SideEffectType.DATAFLOW_SIDE_EFFECTING


def _exchange_copies(g_refs, land_refs, send_sems, recv_sems, gather):
    mx, my, mc = lax.axis_index("x"), lax.axis_index("y"), lax.axis_index("c")
    copies = []
    for a in range(len(g_refs)):
        for f in ((1, 2, 4, 6) if gather else range(1, NDEV)):
            px = 1 - mx if (f >> 2) & 1 else mx
            py = 1 - my if (f >> 1) & 1 else my
            pc = 1 - mc if f & 1 else mc
            src = g_refs[a] if gather else g_refs[a].at[4 * px + 2 * py + pc]
            dst = land_refs[a].at[4 * mx + 2 * my + mc] if gather else land_refs[a].at[f]
            copies.append(pltpu.make_async_remote_copy(
                src_ref=src, dst_ref=dst,
                send_sem=send_sems.at[7 * a + f - 1], recv_sem=recv_sems.at[7 * a + f - 1],
                device_id=(px, py, pc), device_id_type=MESH))
    return copies


def _exchange_start(gs, name, gather=False):
    na = len(gs)

    def body(*refs):
        for cp in _exchange_copies(refs[:na], refs[na:2 * na], refs[2 * na], refs[2 * na + 1], gather):
            cp.start()
        refs[-1][...] = jnp.zeros_like(refs[-1])

    hbm = [pltpu.HBM(g.shape, g.dtype) for g in gs]
    land_shapes = [((NDEV,) + g.shape) if gather else g.shape for g in gs]
    lands = [pltpu.with_memory_space_constraint(lax.empty(shp, g.dtype), pltpu.HBM)
             for shp, g in zip(land_shapes, gs)]
    hbm_land = [pltpu.HBM(shp, g.dtype) for shp, g in zip(land_shapes, gs)]
    outs = pl.pallas_call(
        body, name=name,
        out_shape=(pltpu.SemaphoreType.DMA((7 * na,)), pltpu.SemaphoreType.DMA((7 * na,)), *hbm, *hbm_land,
                   jax.ShapeDtypeStruct((8, 128), F32)),
        in_specs=[_HBM] * (2 * na),
        out_specs=(_SEM, _SEM, *([_HBM] * (2 * na)), pl.BlockSpec(memory_space=pltpu.VMEM)),
        input_output_aliases={i: 2 + i for i in range(2 * na)},
        compiler_params=pltpu.CompilerParams(has_side_effects=_EFFECT),
    )(*[pltpu.with_memory_space_constraint(g, pltpu.HBM) for g in gs], *lands)
    return outs[0], outs[1], outs[2:2 + na], outs[2 + na:2 + 2 * na], outs[-1]


def _forward_copies(land_refs, send_sems, recv_sems):
    mx, my, mc = lax.axis_index("x"), lax.axis_index("y"), lax.axis_index("c")
    copies = []
    for a in range(len(land_refs)):
        for j, (fx, fy) in enumerate(((0, 1), (1, 0), (1, 1))):
            px = 1 - mx if fx else mx
            py = 1 - my if fy else my
            blk = land_refs[a].at[4 * px + 2 * py + mc]
            copies.append(pltpu.make_async_remote_copy(
                src_ref=blk, dst_ref=blk, send_sem=send_sems.at[3 * a + j], recv_sem=recv_sems.at[3 * a + j],
                device_id=(mx, my, 1 - mc), device_id_type=MESH))
    return copies


def _forward_start(lands, name):
    na = len(lands)

    def body(*refs):
        for cp in _forward_copies(refs[:na], refs[na], refs[na + 1]):
            cp.start()
        refs[-1][...] = jnp.zeros_like(refs[-1])

    outs = pl.pallas_call(
        body, name=name,
        out_shape=(pltpu.SemaphoreType.DMA((3 * na,)), pltpu.SemaphoreType.DMA((3 * na,)),
                   *[pltpu.HBM(g.shape, g.dtype) for g in lands], jax.ShapeDtypeStruct((8, 128), F32)),
        in_specs=[_HBM] * na,
        out_specs=(_SEM, _SEM, *([_HBM] * na), pl.BlockSpec(memory_space=pltpu.VMEM)),
        input_output_aliases={i: 2 + i for i in range(na)},
        compiler_params=pltpu.CompilerParams(has_side_effects=_EFFECT),
    )(*lands)
    return outs[0], outs[1], outs[2:2 + na], outs[-1]


def _forward_wait(send_sems, recv_sems, lands, after, name):
    na = len(lands)

    def body(*refs):
        for cp in _forward_copies(refs[:na], refs[na], refs[na + 1]):
            cp.wait_send()
            cp.wait_recv()

    return pl.pallas_call(
        body, name=name,
        out_shape=tuple(pltpu.HBM(g.shape, g.dtype) for g in lands),
        in_specs=[_HBM] * na + [_SEM, _SEM, pl.BlockSpec(memory_space=pl.ANY)],
        out_specs=tuple([_HBM] * na),
        input_output_aliases={i: i for i in range(na)},
        compiler_params=pltpu.CompilerParams(has_side_effects=_EFFECT),
    )(*lands, send_sems, recv_sems, after)


def _exchange_wait(send_sems, recv_sems, g_thru, land_thru, after, name, gather=False):
    na = len(g_thru)

    def body(*refs):
        for cp in _exchange_copies(refs[:na], refs[na:2 * na], refs[2 * na], refs[2 * na + 1], gather):
            cp.wait_send()
            cp.wait_recv()

    outs = pl.pallas_call(
        body, name=name,
        out_shape=tuple(pltpu.HBM(g.shape, g.dtype) for g in list(g_thru) + list(land_thru)),
        in_specs=[_HBM] * (2 * na) + [_SEM, _SEM, pl.BlockSpec(memory_space=pl.ANY)],
        out_specs=tuple([_HBM] * (2 * na)),
        input_output_aliases={i: i for i in range(2 * na)},
        compiler_params=pltpu.CompilerParams(has_side_effects=_EFFECT),
    )(*g_thru, *land_thru, send_sems, recv_sems, after)
    return outs[na:]


def _sum8(r, name):
    _, R, C = r.shape
    tr = _tile(R, (256, 160, 128, 64, 32, 16, 8))

    def body(r_ref, o_ref):
        acc = r_ref[0].astype(F32)
        for k in range(1, NDEV):
            acc = acc + r_ref[k].astype(F32)
        o_ref[...] = acc

    return pl.pallas_call(
        body, name=name, grid=(R // tr,),
        in_specs=[pl.BlockSpec((NDEV, tr, C), lambda i: (0, i, 0))],
        out_specs=pl.BlockSpec((tr, C), lambda i: (i, 0)),
        out_shape=jax.ShapeDtypeStruct((R, C), F32),
        compiler_params=_params(("parallel",)),
    )(r)


def _adamw(g, w, m, v, name):
    R, C = g.shape
    tr = _tile(R, (256, 160, 128, 64, 32, 16, 8))
    bc1 = 1.0 - ADAM_B1 ** ADAM_STEP
    bc2 = 1.0 - ADAM_B2 ** ADAM_STEP

    def body(g_ref, w_ref, m_ref, v_ref, d_ref, mo_ref, vo_ref):
        gg = g_ref[...]
        mn = ADAM_B1 * m_ref[...] + (1.0 - ADAM_B1) * gg
        vn = ADAM_B2 * v_ref[...] + (1.0 - ADAM_B2) * (gg * gg)
        mh = mn / bc1
        vh = vn / bc2
        d_ref[...] = -ADAM_LR * (mh / (jnp.sqrt(vh) + ADAM_EPS) + ADAM_WD * w_ref[...])
        mo_ref[...] = mn
        vo_ref[...] = vn

    spec = pl.BlockSpec((tr, C), lambda i: (i, 0))
    sh = jax.ShapeDtypeStruct((R, C), F32)
    return pl.pallas_call(
        body, name=name, grid=(R // tr,), in_specs=[spec] * 4, out_specs=[spec] * 3,
        out_shape=[sh] * 3, compiler_params=_params(("parallel",)),
    )(g, w, m, v)


def _ada_fwd(c16, w_sh, b_sh):
    def body(c_ref, w_ref, b_ref, o_ref):
        c = c_ref[...]
        o_ref[...] = _dot(c * _sig(c), w_ref[...], NN) + b_ref[...]

    return pl.pallas_call(
        body, name="ada_fwd", out_shape=jax.ShapeDtypeStruct((16, w_sh.shape[1]), F32),
        compiler_params=pltpu.CompilerParams(vmem_limit_bytes=VMEM_LIMIT),
    )(c16, w_sh, b_sh)


def _ada_bwd(c16, g16, g16_sh, w_sh):
    ncol = w_sh.shape[1]

    def body(c_ref, g_ref, gs_ref, w_ref, dw_ref, db_ref, dc_ref):
        c = c_ref[...]
        s = _sig(c)
        gs = gs_ref[...]
        dw_ref[...] = _dot(c * s, gs, TN)
        db_ref[...] = jnp.broadcast_to(_colsum(g_ref[...]), db_ref.shape)
        odd = lax.broadcasted_iota(jnp.int32, gs.shape, 0) % 2 == 1
        gc = _colsum(jnp.where(odd, gs, 0.0))
        ds = _dot(jnp.broadcast_to(gc, (8, ncol)), w_ref[...], NT)
        c1 = c[1:2, :]
        s1 = s[1:2, :]
        dc_ref[...] = ds * (s1 * (1.0 + c1 * (1.0 - s1)))

    return pl.pallas_call(
        body, name="ada_bwd",
        out_shape=[jax.ShapeDtypeStruct(w_sh.shape, F32),
                   jax.ShapeDtypeStruct((8, g16.shape[1]), F32),
                   jax.ShapeDtypeStruct((8, D), F32)],
        compiler_params=pltpu.CompilerParams(vmem_limit_bytes=VMEM_LIMIT),
    )(c16, g16, g16_sh, w_sh)


def _ln0_fwd(x, ctx, g, b, modx, modc):
    L = x.shape[0]
    nt = L // TL

    def body(x_ref, c_ref, g_ref, b_ref, mx_ref, mc_ref, xn_ref, h_ref):
        isc = pl.program_id(0) == nt
        xin = jnp.where(isc, c_ref[...], x_ref[...])
        sh = jnp.where(isc, mc_ref[0:1, :], mx_ref[0:1, :])
        sc = jnp.where(isc, mc_ref[1:2, :], mx_ref[1:2, :])
        xhat, _ = _ln(xin)
        xn = xhat * g_ref[...] + b_ref[...]
        xn_ref[...] = xn
        h_ref[...] = (xn * (1.0 + sc) + sh).astype(h_ref.dtype)

    return pl.pallas_call(
        body, name="ln0_fwd", grid=(nt + 1,),
        in_specs=[_rtc(D, nt), _cst((TL, D)), _cst((1, D)), _cst((1, D)), _cst((8, D)), _cst((8, D))],
        out_specs=[_rt(D), _rt(D)],
        out_shape=[jax.ShapeDtypeStruct((L + TL, D), F32), jax.ShapeDtypeStruct((L + TL, D), _MXU)],
        compiler_params=_params(("parallel",)),
    )(x, ctx, g, b, modx, modc)


def _xbc_colblk(j):
    return jnp.where(j < 8, OXS // 128 + j, OB // 128 + j - 8)


def _conv_taps(p_ref, r0, first, last):
    main = p_ref[pl.ds(r0, TL), :]
    zero = jnp.zeros((8, main.shape[1]), F32)
    prev = zero if first else p_ref[pl.ds(r0 - 8, 8), :]
    nxt = zero if last else p_ref[pl.ds(r0 + TL, 8), :]
    ext = jnp.concatenate([prev, main, nxt], axis=0)
    n = TL + 16
    return [pltpu.roll(ext, (2 - k) % n, 0)[8:8 + TL] for k in range(5)]


def _seq_chunks(L):
    nt = L // TL
    return [(r * TL, r == 0, r == nt - 1) for r in range(nt)] + [(L, True, True)]


def _conv_fwd(p, conv_w8, conv_b):
    RT = p.shape[0]
    L = RT - TL
    chunks = _seq_chunks(L)

    def body(p_ref, w_ref, b_ref, o_ref):
        w = w_ref[...]
        bias = b_ref[...]
        for r0, first, last in chunks:
            taps = _conv_taps(p_ref, r0, first, last)
            pre = bias + sum(w[k:k + 1, :] * taps[k] for k in range(5))
            o_ref[pl.ds(r0, TL), :] = pre * _sig(pre)

    return pl.pallas_call(
        body, name="conv_fwd", grid=(12,),
        in_specs=[pl.BlockSpec((RT, 128), lambda j: (0, _xbc_colblk(j))),
                  pl.BlockSpec((8, 128), lambda j: (0, j)),
                  pl.BlockSpec((1, 128), lambda j: (0, j))],
        out_specs=pl.BlockSpec((RT, 128), lambda j: (0, j)),
        out_shape=jax.ShapeDtypeStruct((RT, 1536), F32),
        compiler_params=_params(("parallel",)),
    )(p, conv_w8, conv_b)


def _ssd_common(dtraw, dtb, a32, rev):
    dt = _softplus(dtraw + dtb)
    acum = _cumsum_rows(dt * a32, rev)
    ii = lax.broadcasted_iota(jnp.int32, (Q, Q), 0)
    jj = lax.broadcasted_iota(jnp.int32, (Q, Q), 1)
    mask = (ii <= jj) if rev else (ii >= jj)
    return dt, acum, acum.T, dt.T, mask


def _ssd_orders(ncl, ncc):
    nc = ncl + ncc

    def cf(s):
        return jnp.where(s < ncc, ncl + s, s - ncc)

    def cb(s):
        return nc - 1 - s

    return cf, cb


def _ssd_fwd(xbc, p, prm):
    RT = xbc.shape[0]
    nc = RT // Q
    ncc = TL // Q
    cf, cb = _ssd_orders(nc - ncc, ncc)

    def one_dir(x_ref, dt_ref, prm_ref, y_ref, hp_ref, H_ref, d):
        rev = d == 1
        a32 = -jnp.exp(prm_ref[1:2, :])
        dt, acum, acumT, dtT, mask = _ssd_common(dt_ref[...], prm_ref[0:1, :], a32, rev)
        end = 0 if rev else Q - 1
        for g in range(2):
            Bg = x_ref[:, D + g * NS:D + (g + 1) * NS]
            Cg = x_ref[:, D + 2 * NS + g * NS:D + 2 * NS + (g + 1) * NS]
            CB = _dot(Cg, Bg, NT)
            for hh in range(HPG):
                h = g * HPG + hh
                ln = 16 * d + h
                col = acum[:, ln:ln + 1]
                rowv = acumT[ln:ln + 1, :]
                a_end = rowv[:, end:end + 1]
                Lm = jnp.exp(jnp.where(mask, col - rowv, -1e30))
                W = CB * Lm * dtT[ln:ln + 1, :]
                Xh = x_ref[:, h * HP:(h + 1) * HP]
                Hp = H_ref[h * HP:(h + 1) * HP, :]
                y = _dot(W, Xh, NN) + jnp.exp(col) * _dot(Cg, Hp, NT)
                y_ref[:, h * HP:(h + 1) * HP] = y
                dcol = jnp.exp(a_end - col) * dt[:, ln:ln + 1]
                hp_ref[0, h * HP:(h + 1) * HP, :] = Hp
                H_ref[h * HP:(h + 1) * HP, :] = jnp.exp(a_end) * Hp + _dot(Xh * dcol, Bg, TN)

    def body(xf_ref, xb_ref, df_ref, db_ref, prm_ref, yf_ref, yb_ref, hf_ref, hb_ref, Hf, Hb):
        @pl.when(pl.program_id(0) == 0)
        def _():
            Hf[...] = jnp.zeros_like(Hf)
            Hb[...] = jnp.zeros_like(Hb)

        one_dir(xf_ref, df_ref, prm_ref, yf_ref, hf_ref, Hf, 0)
        one_dir(xb_ref, db_ref, prm_ref, yb_ref, hb_ref, Hb, 1)

    ysh = jax.ShapeDtypeStruct((RT, D), F32)
    hsh = jax.ShapeDtypeStruct((nc, NH * HP, NS), F32)
    hspec = pl.BlockSpec((1, NH * HP, NS), lambda s: (s, 0, 0))
    return pl.pallas_call(
        body, name="ssd_fwd", grid=(nc,),
        in_specs=[pl.BlockSpec((Q, 1536), lambda s: (cf(s), 0)),
                  pl.BlockSpec((Q, 1536), lambda s: (cb(s), 0)),
                  pl.BlockSpec((Q, 128), lambda s: (cf(s), ODT // 128)),
                  pl.BlockSpec((Q, 128), lambda s: (cb(s), ODT // 128)),
                  _cst((8, 128))],
        out_specs=[pl.BlockSpec((Q, D), lambda s: (cf(s), 0)),
                   pl.BlockSpec((Q, D), lambda s: (cb(s), 0)), hspec, hspec],
        out_shape=[ysh, ysh, hsh, hsh],
        scratch_shapes=[pltpu.VMEM((NH * HP, NS), F32), pltpu.VMEM((NH * HP, NS), F32)],
        compiler_params=_params(("arbitrary",)),
    )(xbc, xbc, p, p, prm)


def _ssd_bwd(xbc, p, prm, dsk, dyd, hpf, hpb):
    RT = xbc.shape[0]
    nc = RT // Q
    ncc = TL // Q
    ncl = nc - ncc
    cf, cb = _ssd_orders(ncl, ncc)

    def rs(t):
        return nc - 1 - t

    def one_dir(x_ref, dt_ref, prm_ref, dsk_ref, dy_ref, is_ctx, hp_ref, dH_ref,
                dx_ref, ddt_ref, st_ref, d):
        rev = d == 1
        a32 = -jnp.exp(prm_ref[1:2, :])
        dtraw = dt_ref[...]
        dtb = prm_ref[0:1, :]
        dt, acum, acumT, dtT, mask = _ssd_common(dtraw, dtb, a32, rev)
        end = 0 if rev else Q - 1
        lane = lax.broadcasted_iota(jnp.int32, (Q, 128), 1)
        srow = lax.broadcasted_iota(jnp.int32, (Q, 128), 0)
        dyscale = jnp.where(is_ctx, 0.0, 1.0)
        c_dacum = jnp.zeros((Q, 128), F32)
        r_dacum = jnp.zeros((Q, 128), F32)
        c_ddt = jnp.zeros((Q, 128), F32)
        r_ddt = jnp.zeros((Q, 128), F32)
        dskacc = jnp.zeros((1, 128), F32)
        for g in range(2):
            Bg = x_ref[:, D + g * NS:D + (g + 1) * NS]
            Cg = x_ref[:, D + 2 * NS + g * NS:D + 2 * NS + (g + 1) * NS]
            CB = _dot(Cg, Bg, NT)
            dCB = jnp.zeros((Q, Q), F32)
            dBg = jnp.zeros((Q, NS), F32)
            dCg = jnp.zeros((Q, NS), F32)
            for hh in range(HPG):
                h = g * HPG + hh
                ln = 16 * d + h
                hs = slice(h * HP, (h + 1) * HP)
                col = acum[:, ln:ln + 1]
                rowv = acumT[ln:ln + 1, :]
                dtr = dtT[ln:ln + 1, :]
                dtc = dt[:, ln:ln + 1]
                a_end = rowv[:, end:end + 1]
                Lm = jnp.exp(jnp.where(mask, col - rowv, -1e30))
                E = jnp.exp(col)
                ecol = jnp.exp(a_end - col)
                dcol = ecol * dtc
                Xh = x_ref[:, hs]
                dY = dy_ref[:, hs] * dyscale
                Hp = hp_ref[0, hs, :]
                dHn = dH_ref[hs, :]
                W = CB * Lm * dtr
                dW = _dot(dY, Xh, NT)
                Mm = dW * CB * Lm
                T = Mm * dtr
                dCB = dCB + dW * Lm * dtr
                BdH = _dot(Bg, dHn, NT)
                dX = _dot(W, dY, TN) + dcol * BdH
                if d == 0:
                    dX = dX + dY * dsk_ref[:, hs]
                    dskacc = dskacc + jnp.where(lane[0:1, :] == h, _sum11(dY * Xh), 0.0)
                dx_ref[:, hs] = dX
                xb = jnp.sum(Xh * BdH, axis=1, keepdims=True)
                scol = dcol * xb
                G = _dot(dY, Hp, NN)
                dCg = dCg + E * G
                qcol = E * jnp.sum(G * Cg, axis=1, keepdims=True)
                dBg = dBg + _dot(Xh * dcol, dHn, NN)
                dH_ref[hs, :] = jnp.exp(a_end) * dHn + _dot(dY * E, Cg, TN)
                eterm = jnp.exp(a_end) * _sum11(dHn * Hp) + _sum11(scol)
                cvec = jnp.sum(T, axis=1, keepdims=True) + qcol - scol
                cvec = cvec + jnp.where(srow[:, 0:1] == end, eterm, 0.0)
                c_dacum = c_dacum + jnp.where(lane == ln, cvec, 0.0)
                r_dacum = r_dacum - jnp.where(srow == ln, _colsum(T), 0.0)
                c_ddt = c_ddt + jnp.where(lane == ln, ecol * xb, 0.0)
                r_ddt = r_ddt + jnp.where(srow == ln, _colsum(Mm), 0.0)
            dBg = dBg + _dot(dCB, Cg, TN)
            dCg = dCg + _dot(dCB, Bg, NN)
            dx_ref[:, D + g * NS:D + (g + 1) * NS] = dBg
            dx_ref[:, D + 2 * NS + g * NS:D + 2 * NS + (g + 1) * NS] = dCg
        dacum = c_dacum + r_dacum.T
        da = _cumsum_rows(dacum, not rev)
        mine = (lane >= 16 * d) & (lane < 16 * d + 16)
        ddt = jnp.where(mine, c_ddt + r_ddt.T + da * a32, 0.0)
        ddt_ref[...] = ddt * _sig(dtraw + dtb)
        st_ref[0:1, :] += _colsum(jnp.where(mine, da * dt, 0.0))
        if d == 0:
            st_ref[1:2, :] += dskacc

    def body(xf_ref, xb_ref, df_ref, db_ref, prm_ref, dsk_ref, dyf_ref, dyb_ref, hf_ref, hb_ref,
             dxf_ref, dxb_ref, ddf_ref, ddb_ref, st_ref, dHf, dHb):
        t = pl.program_id(0)

        @pl.when(t == 0)
        def _():
            dHf[...] = jnp.zeros_like(dHf)
            dHb[...] = jnp.zeros_like(dHb)
            st_ref[...] = jnp.zeros_like(st_ref)

        s = rs(t)
        one_dir(xf_ref, df_ref, prm_ref, dsk_ref, dyf_ref, cf(s) >= ncl, hf_ref, dHf,
                dxf_ref, ddf_ref, st_ref, 0)
        one_dir(xb_ref, db_ref, prm_ref, dsk_ref, dyb_ref, cb(s) >= ncl, hb_ref, dHb,
                dxb_ref, ddb_ref, st_ref, 1)

        @pl.when(t == nc - 1)
        def _():
            st_ref[0:1, :] = -jnp.exp(prm_ref[1:2, :]) * st_ref[0:1, :]

    def lat(c):
        return jnp.minimum(c, ncl - 1)

    xsh = jax.ShapeDtypeStruct((RT, 1536), F32)
    dsh = jax.ShapeDtypeStruct((RT, 128), F32)
    hspec = pl.BlockSpec((1, NH * HP, NS), lambda t: (rs(t), 0, 0))
    return pl.pallas_call(
        body, name="ssd_bwd", grid=(nc,),
        in_specs=[pl.BlockSpec((Q, 1536), lambda t: (cf(rs(t)), 0)),
                  pl.BlockSpec((Q, 1536), lambda t: (cb(rs(t)), 0)),
                  pl.BlockSpec((Q, 128), lambda t: (cf(rs(t)), ODT // 128)),
                  pl.BlockSpec((Q, 128), lambda t: (cb(rs(t)), ODT // 128)),
                  _cst((8, 128)), _cst((1, D)),
                  pl.BlockSpec((Q, D), lambda t: (lat(cf(rs(t))), 0)),
                  pl.BlockSpec((Q, D), lambda t: (lat(cb(rs(t))), 0)),
                  hspec, hspec],
        out_specs=[pl.BlockSpec((Q, 1536), lambda t: (cf(rs(t)), 0)),
                   pl.BlockSpec((Q, 1536), lambda t: (cb(rs(t)), 0)),
                   pl.BlockSpec((Q, 128), lambda t: (cf(rs(t)), 0)),
                   pl.BlockSpec((Q, 128), lambda t: (cb(rs(t)), 0)),
                   _cst((8, 128))],
        out_shape=[xsh, xsh, dsh, dsh, jax.ShapeDtypeStruct((8, 128), F32)],
        scratch_shapes=[pltpu.VMEM((NH * HP, NS), F32), pltpu.VMEM((NH * HP, NS), F32)],
        compiler_params=_params(("arbitrary",)),
    )(xbc, xbc, p, p, prm, dsk, dyd, dyd, hpf, hpb)


def _lane_bcast(v, ln):
    return jnp.broadcast_to(v[:, ln:ln + 1], v.shape)


def _halves(v, lo, axis):
    return jnp.concatenate([jnp.where(lo, v, 0.0), jnp.where(lo, 0.0, v)], axis=axis)


def _ssd2_fwd(xbc, p, prm):
    RT = xbc.shape[0]
    nc = RT // Q
    ncc = TL // Q
    cf, cb = _ssd_orders(nc - ncc, ncc)

    def one_dir(x_ref, dt_ref, prm_ref, y_ref, hp_ref, HT_ref, d):
        rev = d == 1
        a32 = -jnp.exp(prm_ref[1:2, :])
        dt, acum, acumT, dtT, mask = _ssd_common(dt_ref[...], prm_ref[0:1, :], a32, rev)
        end = 0 if rev else Q - 1
        lo = lax.broadcasted_iota(jnp.int32, (Q, 128), 1) < HP
        for g in range(2):
            Bg = x_ref[:, D + g * NS:D + (g + 1) * NS]
            Cg = x_ref[:, D + 2 * NS + g * NS:D + 2 * NS + (g + 1) * NS]
            CB = _dot(Cg, Bg, NT)
            xds, svs = [], []
            for q in range(HPG // 2):
                pi = g * (HPG // 2) + q
                ps = slice(pi * 128, (pi + 1) * 128)
                Xp = x_ref[:, ps]
                HTp = HT_ref[:, ps]
                lhs, dcs, sv = [], [], []
                ces = []
                for h in (2 * pi, 2 * pi + 1):
                    ln = 16 * d + h
                    colB = _lane_bcast(acum, ln)
                    rowv = acumT[ln:ln + 1, :]
                    aend = colB[end:end + 1, :]
                    Lm = jnp.exp(jnp.where(mask, colB - rowv, -1e30))
                    lhs.append(CB * Lm * dtT[ln:ln + 1, :])
                    ces.append(Cg * jnp.exp(colB))
                    dcs.append(jnp.exp(aend - colB) * _lane_bcast(dt, ln))
                    sv.append(jnp.exp(aend))
                lhs = jnp.concatenate(lhs + ces, axis=1)
                rhs = jnp.concatenate([_halves(Xp, lo, 0), _halves(HTp, lo, 0)], axis=0)
                y_ref[:, ps] = _dot(lhs, rhs, NN)
                xds.append(Xp * jnp.where(lo, dcs[0], dcs[1]))
                svs.append(jnp.where(lo[0:1, :], sv[0], sv[1]))
            gs = slice(g * 512, (g + 1) * 512)
            HTg = HT_ref[:, gs]
            hp_ref[0, :, gs] = HTg
            st = _dot(Bg.T, jnp.concatenate(xds, axis=1), NN)
            HT_ref[:, gs] = jnp.concatenate(svs, axis=1) * HTg + st

    def body(xf_ref, xb_ref, df_ref, db_ref, prm_ref, yf_ref, yb_ref, hf_ref, hb_ref, Hf, Hb):
        @pl.when(pl.program_id(0) == 0)
        def _():
            Hf[...] = jnp.zeros_like(Hf)
            Hb[...] = jnp.zeros_like(Hb)

        one_dir(xf_ref, df_ref, prm_ref, yf_ref, hf_ref, Hf, 0)
        one_dir(xb_ref, db_ref, prm_ref, yb_ref, hb_ref, Hb, 1)

    ysh = jax.ShapeDtypeStruct((RT, D), F32)
    hsh = jax.ShapeDtypeStruct((nc, NS, NH * HP), F32)
    hspec = pl.BlockSpec((1, NS, NH * HP), lambda s: (s, 0, 0))
    return pl.pallas_call(
        body, name="ssd_fwd", grid=(nc,),
        in_specs=[pl.BlockSpec((Q, 1536), lambda s: (cf(s), 0)),
                  pl.BlockSpec((Q, 1536), lambda s: (cb(s), 0)),
                  pl.BlockSpec((Q, 128), lambda s: (cf(s), ODT // 128)),
                  pl.BlockSpec((Q, 128), lambda s: (cb(s), ODT // 128)),
                  _cst((8, 128))],
        out_specs=[pl.BlockSpec((Q, D), lambda s: (cf(s), 0)),
                   pl.BlockSpec((Q, D), lambda s: (cb(s), 0)), hspec, hspec],
        out_shape=[ysh, ysh, hsh, hsh],
        scratch_shapes=[pltpu.VMEM((NS, NH * HP), F32), pltpu.VMEM((NS, NH * HP), F32)],
        compiler_params=_params(("arbitrary",)),
    )(xbc, xbc, p, p, prm)


def _ssd2_bwd(xbc, p, prm, dsk, dyd, hpf, hpb):
    RT = xbc.shape[0]
    nc = RT // Q
    ncc = TL // Q
    ncl = nc - ncc
    cf, cb = _ssd_orders(ncl, ncc)

    def rs(t):
        return nc - 1 - t

    def one_dir(x_ref, dt_ref, prm_ref, dsk_ref, dy_ref, is_ctx, hp_ref, dHT_ref,
                dx_ref, ddt_ref, st_ref, d):
        rev = d == 1
        a32 = -jnp.exp(prm_ref[1:2, :])
        dtraw = dt_ref[...]
        dtb = prm_ref[0:1, :]
        dt, acum, acumT, _, _ = _ssd_common(dtraw, dtb, a32, rev)
        end = 0 if rev else Q - 1
        lane = lax.broadcasted_iota(jnp.int32, (Q, 128), 1)
        srow = lax.broadcasted_iota(jnp.int32, (Q, 128), 0)
        maskT = (lane <= srow) if rev else (lane >= srow)
        lo = lane < HP
        lo1 = lo[0:1, :]
        dyscale = jnp.where(is_ctx, 0.0, 1.0)
        c_dacum = jnp.zeros((Q, 128), F32)
        r_dacum = jnp.zeros((Q, 128), F32)
        c_ddt = jnp.zeros((Q, 128), F32)
        dskacc = jnp.zeros((1, 128), F32)
        for g in range(2):
            gs = slice(g * 512, (g + 1) * 512)
            Bg = x_ref[:, D + g * NS:D + (g + 1) * NS]
            Cg = x_ref[:, D + 2 * NS + g * NS:D + 2 * NS + (g + 1) * NS]
            CBT = _dot(Bg, Cg, NT)
            HTg = hp_ref[0, :, gs]
            dHTg = dHT_ref[:, gs]
            BdHg = _dot(Bg, dHTg, NN)
            dCBT = jnp.zeros((Q, Q), F32)
            dCg = jnp.zeros((Q, NS), F32)
            xds, dyes, svs = [], [], []
            for q in range(HPG // 2):
                pi = g * (HPG // 2) + q
                ps = slice(pi * 128, (pi + 1) * 128)
                qs = slice(q * 128, (q + 1) * 128)
                Xp = x_ref[:, ps]
                dYp = dy_ref[:, ps] * dyscale
                HTp = HTg[:, qs]
                BdHp = BdHg[:, qs]
                dY2 = _halves(dYp, lo, 0)
                dWT2 = _dot(_halves(Xp, lo, 0), dYp.T, NN)
                G2 = _dot(dY2, HTp, NT)
                XB = Xp * BdHp
                hh = _colsum(dHTg[:, qs] * HTp)
                yx = _colsum(dYp * Xp)
                wts, dcs, ebs, sv = [], [], [], []
                for k, h in enumerate((2 * pi, 2 * pi + 1)):
                    ln = 16 * d + h
                    half = lo if k == 0 else jnp.logical_not(lo)
                    half1 = half[0:1, :]
                    colB = _lane_bcast(acum, ln)
                    dtcB = _lane_bcast(dt, ln)
                    rowv = acumT[ln:ln + 1, :]
                    aend = colB[end:end + 1, :]
                    LmT = jnp.exp(jnp.where(maskT, rowv - colB, -1e30))
                    WT = CBT * LmT * dtcB
                    dWT = dWT2[k * Q:(k + 1) * Q, :]
                    U = dWT * LmT
                    MT = U * CBT
                    rM = jnp.sum(MT, axis=1, keepdims=True)
                    rT = _colsum(MT * dtcB)
                    dCBT = dCBT + U * dtcB
                    ecol = jnp.exp(aend - colB)
                    EB = jnp.exp(colB)
                    Gk = G2[k * Q:(k + 1) * Q, :]
                    dCg = dCg + EB * Gk
                    qcol = jnp.sum(EB * Gk * Cg, axis=1, keepdims=True)
                    xb = jnp.sum(jnp.where(half, XB, 0.0), axis=1, keepdims=True)
                    e1 = ecol[:, 0:1]
                    dt1 = dtcB[:, 0:1]
                    scol = e1 * dt1 * xb
                    sA = jnp.exp(aend)
                    eterm = sA[:, 0:1] * jnp.sum(jnp.where(half1, hh, 0.0), axis=1, keepdims=True) \
                        + _colsum(scol)
                    cvec = qcol - dt1 * rM - scol + jnp.where(srow[:, 0:1] == end, eterm, 0.0)
                    c_dacum = jnp.where(lane == ln, cvec, c_dacum)
                    r_dacum = jnp.where(srow == ln, rT, r_dacum)
                    c_ddt = jnp.where(lane == ln, rM + e1 * xb, c_ddt)
                    if d == 0:
                        dskacc = dskacc + jnp.where(
                            lane[0:1, :] == h, jnp.sum(jnp.where(half1, yx, 0.0), axis=1, keepdims=True), 0.0)
                    wts.append(WT)
                    dcs.append(ecol * dtcB)
                    ebs.append(EB)
                    sv.append(sA)
                dcp = jnp.where(lo, dcs[0], dcs[1])
                dX = _dot(jnp.concatenate(wts, axis=1), dY2, NN) + dcp * BdHp
                if d == 0:
                    dX = dX + dYp * dsk_ref[:, ps]
                dx_ref[:, ps] = dX
                xds.append(Xp * dcp)
                dyes.append(dYp * jnp.where(lo, ebs[0], ebs[1]))
                svs.append(jnp.where(lo1, sv[0], sv[1]))
            dx_ref[:, D + g * NS:D + (g + 1) * NS] = (
                _dot(jnp.concatenate(xds, axis=1), dHTg, NT) + _dot(dCBT, Cg, NN))
            dx_ref[:, D + 2 * NS + g * NS:D + 2 * NS + (g + 1) * NS] = dCg + _dot(dCBT, Bg, TN)
            dHT_ref[:, gs] = (jnp.concatenate(svs, axis=1) * dHTg
                              + _dot(Cg.T, jnp.concatenate(dyes, axis=1), NN))
        dacum = c_dacum + r_dacum.T
        da = _cumsum_rows(dacum, not rev)
        mine = (lane >= 16 * d) & (lane < 16 * d + 16)
        ddt = jnp.where(mine, c_ddt + da * a32, 0.0)
        ddt_ref[...] = ddt * _sig(dtraw + dtb)
        st_ref[0:1, :] += _colsum(jnp.where(mine, da * dt, 0.0))
        if d == 0:
            st_ref[1:2, :] += dskacc

    def body(xf_ref, xb_ref, df_ref, db_ref, prm_ref, dsk_ref, dyf_ref, dyb_ref, hf_ref, hb_ref,
             dxf_ref, dxb_ref, ddf_ref, ddb_ref, st_ref, dHf, dHb):
        t = pl.program_id(0)

        @pl.when(t == 0)
        def _():
            dHf[...] = jnp.zeros_like(dHf)
            dHb[...] = jnp.zeros_like(dHb)
            st_ref[...] = jnp.zeros_like(st_ref)

        s = rs(t)
        one_dir(xf_ref, df_ref, prm_ref, dsk_ref, dyf_ref, cf(s) >= ncl, hf_ref, dHf,
                dxf_ref, ddf_ref, st_ref, 0)
        one_dir(xb_ref, db_ref, prm_ref, dsk_ref, dyb_ref, cb(s) >= ncl, hb_ref, dHb,
                dxb_ref, ddb_ref, st_ref, 1)

        @pl.when(t == nc - 1)
        def _():
            st_ref[0:1, :] = -jnp.exp(prm_ref[1:2, :]) * st_ref[0:1, :]

    def lat(c):
        return jnp.minimum(c, ncl - 1)

    xsh = jax.ShapeDtypeStruct((RT, 1536), F32)
    dsh = jax.ShapeDtypeStruct((RT, 128), F32)
    hspec = pl.BlockSpec((1, NS, NH * HP), lambda t: (rs(t), 0, 0))
    return pl.pallas_call(
        body, name="ssd_bwd", grid=(nc,),
        in_specs=[pl.BlockSpec((Q, 1536), lambda t: (cf(rs(t)), 0)),
                  pl.BlockSpec((Q, 1536), lambda t: (cb(rs(t)), 0)),
                  pl.BlockSpec((Q, 128), lambda t: (cf(rs(t)), ODT // 128)),
                  pl.BlockSpec((Q, 128), lambda t: (cb(rs(t)), ODT // 128)),
                  _cst((8, 128)), _cst((1, D)),
                  pl.BlockSpec((Q, D), lambda t: (lat(cf(rs(t))), 0)),
                  pl.BlockSpec((Q, D), lambda t: (lat(cb(rs(t))), 0)),
                  hspec, hspec],
        out_specs=[pl.BlockSpec((Q, 1536), lambda t: (cf(rs(t)), 0)),
                   pl.BlockSpec((Q, 1536), lambda t: (cb(rs(t)), 0)),
                   pl.BlockSpec((Q, 128), lambda t: (cf(rs(t)), 0)),
                   pl.BlockSpec((Q, 128), lambda t: (cb(rs(t)), 0)),
                   _cst((8, 128))],
        out_shape=[xsh, xsh, dsh, dsh, jax.ShapeDtypeStruct((8, 128), F32)],
        scratch_shapes=[pltpu.VMEM((NS, NH * HP), F32), pltpu.VMEM((NS, NH * HP), F32)],
        compiler_params=_params(("arbitrary",)),
    )(xbc, xbc, p, p, prm, dsk, dyd, dyd, hpf, hpb)


def _mix_fwd_vals(yf, yb, z, xs, u, v, dsk, sg, gg, gb):
    y = yf + yb + xs * dsk
    sz = _sig(z)
    hh = y * z * sz
    r = lax.rsqrt(jnp.mean(hh * hh, axis=-1, keepdims=True) + EPS)
    nh = hh * r
    ug, tu = _gelu(u)
    vg, tv = _gelu(v)
    vhat, vrstd = _ln(vg)
    vn = vhat * gg + gb
    return y, sz, r, nh, ug, tu, vg, tv, vhat, vrstd, vn


def _mix_fwd(yf, yb, p, xbc, dsk, sg, gg, gb, ws, bsT):
    L = yf.shape[0] - TL
    nt = L // TL

    def body(yf_ref, yb_ref, z_ref, xs_ref, u_ref, v_ref, dsk_ref, sg_ref, gg_ref, gb_ref,
             ws_ref, bs_ref, ys_ref, ym_ref):
        _, _, _, nh, ug, _, _, _, _, _, vn = _mix_fwd_vals(
            yf_ref[...], yb_ref[...], z_ref[...], xs_ref[...], u_ref[...], v_ref[...],
            dsk_ref[...], sg_ref[...], gg_ref[...], gb_ref[...])
        ys_ref[...] = (nh * sg_ref[...]).astype(ys_ref.dtype)
        for n in range(TL // Q):
            rs_ = slice(n * Q, (n + 1) * Q)
            for g in range(8):
                cs = slice(g * 128, (g + 1) * 128)
                mixed = _dot(ws_ref[g], vn[rs_, cs], NN) + bs_ref[:, g:g + 1]
                ym_ref[rs_, cs] = (ug[rs_, cs] * mixed).astype(ym_ref.dtype)

    return pl.pallas_call(
        body, name="mix_fwd", grid=(nt,),
        in_specs=[_rt(D), _rt(D), _rt(D, OZ // D), _rt(D, 0), _rt(D, OU // D), _rt(D, OV // D),
                  _cst((1, D)), _cst((1, D)), _cst((1, D)), _cst((1, D)),
                  _cst((8, 128, 128)), _cst((128, 128))],
        out_specs=[_rt(D), _rt(D)],
        out_shape=[jax.ShapeDtypeStruct((L, D), _MXU), jax.ShapeDtypeStruct((L, D), _MXU)],
        compiler_params=_params(("parallel",)),
    )(yf, yb, p, xbc, p, p, dsk, sg, gg, gb, ws, bsT)


def _mix_bwd(dys, dym, yf, yb, p, xbc, dp, dsk, sg, gg, gb, ws, bsT):
    L = dys.shape[0]
    nt = L // TL

    def body(dys_ref, dym_ref, yf_ref, yb_ref, z_ref, xs_ref, u_ref, v_ref, dsk_ref, sg_ref,
             gg_ref, gb_ref, ws_ref, bs_ref, dp_any, dzuv_ref, dy_ref, st_ref,
             dws_ref, dbs_ref, dvn_s):
        del dp_any
        dz_ref = dzuv_ref.at[:, OZ:OZ + D]
        du_ref = dzuv_ref.at[:, OU:OU + D]
        dv_ref = dzuv_ref.at[:, OV:OV + D]

        @pl.when(pl.program_id(0) == 0)
        def _():
            st_ref[...] = jnp.zeros_like(st_ref)
            dws_ref[...] = jnp.zeros_like(dws_ref)
            dbs_ref[...] = jnp.zeros_like(dbs_ref)

        z = z_ref[...]
        u = u_ref[...]
        v = v_ref[...]
        y, sz, r, nh, ug, tu, vg, tv, vhat, vrstd, vn = _mix_fwd_vals(
            yf_ref[...], yb_ref[...], z, xs_ref[...], u, v,
            dsk_ref[...], sg_ref[...], gg_ref[...], gb_ref[...])
        dys = dys_ref[...]
        st_ref[0:1, :] += _colsum(dys * nh)
        dn = dys * sg_ref[...]
        dhh = r * (dn - nh * jnp.mean(dn * nh, axis=-1, keepdims=True))
        dy_ref[...] = dhh * z * sz
        dz_ref[...] = (dhh * y * (sz * (1.0 + z * (1.0 - sz)))).astype(dz_ref.dtype)
        dym = dym_ref[...]
        lane = lax.broadcasted_iota(jnp.int32, (Q, 128), 1)
        dbs = jnp.zeros((Q, 128), F32)
        gu = _gelu_grad(u, tu)
        for n in range(TL // Q):
            rs_ = slice(n * Q, (n + 1) * Q)
            for g in range(8):
                cs = slice(g * 128, (g + 1) * 128)
                vb = vn[rs_, cs]
                mixed = _dot(ws_ref[g], vb, NN) + bs_ref[:, g:g + 1]
                dyb = dym[rs_, cs]
                dmx = dyb * ug[rs_, cs]
                du_ref[rs_, cs] = (dyb * mixed * gu[rs_, cs]).astype(du_ref.dtype)
                dvn_s[rs_, cs] = _dot(ws_ref[g], dmx, TN)
                dws_ref[g] += _dot(dmx, vb, NT)
                dbs = dbs + jnp.where(lane == g, jnp.sum(dmx, axis=1, keepdims=True), 0.0)
        dbs_ref[...] += dbs
        dvn = dvn_s[...]
        st_ref[1:2, :] += _colsum(dvn * vhat)
        st_ref[2:3, :] += _colsum(dvn)
        dvg = _ln_bwd(dvn * gg_ref[...], vhat, vrstd)
        dv_ref[...] = (dvg * _gelu_grad(v, tv)).astype(dv_ref.dtype)

    outs = pl.pallas_call(
        body, name="mix_bwd", grid=(nt,),
        in_specs=[_rt(D), _rt(D), _rt(D), _rt(D), _rt(D, OZ // D), _rt(D, 0), _rt(D, OU // D),
                  _rt(D, OV // D), _cst((1, D)), _cst((1, D)), _cst((1, D)), _cst((1, D)),
                  _cst((8, 128, 128)), _cst((128, 128)), pl.BlockSpec(memory_space=pl.ANY)],
        out_specs=[_rt(3 * D, 0), _rt(D), _cst((8, D)),
                   _cst((8, 128, 128)), _cst((128, 128))],
        out_shape=[jax.ShapeDtypeStruct(dp.shape, dp.dtype),
                   jax.ShapeDtypeStruct((L, D), F32), jax.ShapeDtypeStruct((8, D), F32),
                   jax.ShapeDtypeStruct((8, 128, 128), F32), jax.ShapeDtypeStruct((128, 128), F32)],
        scratch_shapes=[pltpu.VMEM((TL, D), F32)],
        input_output_aliases={14: 0},
        compiler_params=_params(("arbitrary",)),
    )(dys, dym, yf, yb, p, xbc, p, p, dsk, sg, gg, gb, ws, bsT, dp)
    return outs


def _gate_fwd(a1, a2, p, bg):
    L = a1.shape[0]

    def body(a1_ref, a2_ref, g_ref, bg_ref, m_ref):
        gt = _sig(g_ref[...] + bg_ref[...])
        m_ref[...] = (gt[:, :D] * a1_ref[...] + gt[:, D:] * a2_ref[...]).astype(m_ref.dtype)

    return pl.pallas_call(
        body, name="gate_fwd", grid=(L // TL,),
        in_specs=[_rt(D), _rt(D), _rt(2 * D, OG // (2 * D)), _cst((1, 2 * D))],
        out_specs=_rt(D), out_shape=jax.ShapeDtypeStruct((L, D), _MXU),
        compiler_params=_params(("parallel",)),
    )(a1, a2, p, bg)


def _gate_bwd(dmg, a1, a2, p, bg, dp):
    L = a1.shape[0]

    def body(dm_ref, a1_ref, a2_ref, g_ref, bg_ref, dp_any, dg_ref, da1_ref, da2_ref, st_ref):
        del dp_any

        @pl.when(pl.program_id(0) == 0)
        def _():
            st_ref[...] = jnp.zeros_like(st_ref)

        gt = _sig(g_ref[...] + bg_ref[...])
        g1 = gt[:, :D]
        g2 = gt[:, D:]
        dm = dm_ref[...]
        da1_ref[...] = (dm * g1).astype(da1_ref.dtype)
        da2_ref[...] = (dm * g2).astype(da2_ref.dtype)
        dg1 = dm * a1_ref[...] * g1 * (1.0 - g1)
        dg2 = dm * a2_ref[...] * g2 * (1.0 - g2)
        st_ref[0:1, 0:D] += _colsum(dg1)
        st_ref[0:1, D:2 * D] += _colsum(dg2)
        dg_ref[:, 0:D] = dg1.astype(dg_ref.dtype)
        dg_ref[:, D:2 * D] = dg2.astype(dg_ref.dtype)

    return pl.pallas_call(
        body, name="gate_bwd", grid=(L // TL,),
        in_specs=[_rt(D), _rt(D), _rt(D), _rt(2 * D, OG // (2 * D)), _cst((1, 2 * D)),
                  pl.BlockSpec(memory_space=pl.ANY)],
        out_specs=[_rt(2 * D, OG // (2 * D)), _rt(D), _rt(D), _cst((8, 2 * D))],
        out_shape=[jax.ShapeDtypeStruct(dp.shape, dp.dtype), jax.ShapeDtypeStruct((L, D), _MXU),
                   jax.ShapeDtypeStruct((L, D), _MXU), jax.ShapeDtypeStruct((8, 2 * D), F32)],
        input_output_aliases={5: 0},
        compiler_params=_params(("arbitrary",)),
    )(dmg, a1, a2, p, bg, dp)


def _res1_fwd(xn, out, modx, g, b):
    L = out.shape[0]

    def body(xn_ref, o_ref, mx_ref, g_ref, b_ref, r1_ref, h2_ref):
        r1 = ALPHA * xn_ref[...] + mx_ref[2:3, :] * o_ref[...]
        xhat, _ = _ln(r1)
        x1 = xhat * g_ref[...] + b_ref[...]
        r1_ref[...] = r1
        h2_ref[...] = (x1 * (1.0 + mx_ref[4:5, :]) + mx_ref[3:4, :]).astype(h2_ref.dtype)

    return pl.pallas_call(
        body, name="res1_fwd", grid=(L // TL,),
        in_specs=[_rt(D), _rt(D), _cst((8, D)), _cst((1, D)), _cst((1, D))],
        out_specs=[_rt(D), _rt(D)],
        out_shape=[jax.ShapeDtypeStruct((L, D), F32), jax.ShapeDtypeStruct((L, D), _MXU)],
        compiler_params=_params(("parallel",)),
    )(xn, out, modx, g, b)


def _glu_fwd(f13):
    L = f13.shape[0]

    def body(f1_ref, f3_ref, o_ref):
        f1 = f1_ref[...]
        o_ref[...] = (f1 * _sig(f1) * f3_ref[...]).astype(o_ref.dtype)

    return pl.pallas_call(
        body, name="glu_fwd", grid=(L // TL,),
        in_specs=[_rt(DFF, 0), _rt(DFF, 1)], out_specs=_rt(DFF),
        out_shape=jax.ShapeDtypeStruct((L, DFF), _MXU),
        compiler_params=_params(("parallel",)),
    )(f13, f13)


def _glu_bwd(dff, f13):
    L = f13.shape[0]

    def body(d_ref, f1_ref, f3_ref, o_ref):
        f1 = f1_ref[...]
        s = _sig(f1)
        d = d_ref[...]
        o_ref[:, 0:DFF] = (d * f3_ref[...] * (s * (1.0 + f1 * (1.0 - s)))).astype(o_ref.dtype)
        o_ref[:, DFF:2 * DFF] = (d * f1 * s).astype(o_ref.dtype)

    return pl.pallas_call(
        body, name="glu_bwd", grid=(L // TL,),
        in_specs=[_rt(DFF), _rt(DFF, 0), _rt(DFF, 1)], out_specs=_rt(2 * DFF),
        out_shape=jax.ShapeDtypeStruct((L, 2 * DFF), _MXU),
        compiler_params=_params(("parallel",)),
    )(dff, f13, f13)


def _res2(r1, o2, tgt, modx, g1, b1, g2, b2):
    L = r1.shape[0]

    def body(r1_ref, o2_ref, t_ref, mx_ref, g1_ref, b1_ref, g2_ref, b2_ref,
             dr2_ref, do2_ref, st_ref, loss_ref):
        @pl.when(pl.program_id(0) == 0)
        def _():
            st_ref[...] = jnp.zeros_like(st_ref)
            loss_ref[...] = jnp.zeros_like(loss_ref)

        xh1, _ = _ln(r1_ref[...])
        x1 = xh1 * g1_ref[...] + b1_ref[...]
        o2 = o2_ref[...]
        g2x = mx_ref[5:6, :]
        xh2, rstd2 = _ln(ALPHA * x1 + g2x * o2)
        err = xh2 * g2_ref[...] + b2_ref[...] - t_ref[...]
        per_tok = jnp.mean(err * err, axis=-1, keepdims=True)
        loss_ref[...] += 0.5 * jnp.sum(per_tok, axis=0, keepdims=True)
        dy = err * (1.0 / D)
        st_ref[0:1, :] += _colsum(dy * xh2)
        st_ref[1:2, :] += _colsum(dy)
        dr2 = _ln_bwd(dy * g2_ref[...], xh2, rstd2)
        st_ref[2:3, :] += _colsum(dr2 * o2)
        dr2_ref[...] = dr2
        do2_ref[...] = (g2x * dr2).astype(do2_ref.dtype)

    return pl.pallas_call(
        body, name="res2", grid=(L // TL,),
        in_specs=[_rt(D), _rt(D), _rt(D), _cst((8, D))] + [_cst((1, D))] * 4,
        out_specs=[_rt(D), _rt(D), _cst((8, D)), _cst((8, 128))],
        out_shape=[jax.ShapeDtypeStruct((L, D), F32), jax.ShapeDtypeStruct((L, D), _MXU),
                   jax.ShapeDtypeStruct((8, D), F32), jax.ShapeDtypeStruct((8, 128), F32)],
        compiler_params=_params(("arbitrary",)),
    )(r1, o2, tgt, modx, g1, b1, g2, b2)


def _res1_bwd(dr2, dh2, r1, out, modx, g1, b1):
    L = r1.shape[0]

    def body(dr2_ref, dh2_ref, r1_ref, o_ref, mx_ref, g_ref, b_ref, dr1_ref, do_ref, st_ref):
        @pl.when(pl.program_id(0) == 0)
        def _():
            st_ref[...] = jnp.zeros_like(st_ref)

        xh1, rstd1 = _ln(r1_ref[...])
        x1 = xh1 * g_ref[...] + b_ref[...]
        dh2 = dh2_ref[...]
        dx1 = ALPHA * dr2_ref[...] + dh2 * (1.0 + mx_ref[4:5, :])
        st_ref[0:1, :] += _colsum(dh2 * x1)
        st_ref[1:2, :] += _colsum(dh2)
        st_ref[2:3, :] += _colsum(dx1 * xh1)
        st_ref[3:4, :] += _colsum(dx1)
        dr1 = _ln_bwd(dx1 * g_ref[...], xh1, rstd1)
        st_ref[4:5, :] += _colsum(dr1 * o_ref[...])
        dr1_ref[...] = dr1
        do_ref[...] = (mx_ref[2:3, :] * dr1).astype(do_ref.dtype)

    return pl.pallas_call(
        body, name="res1_bwd", grid=(L // TL,),
        in_specs=[_rt(D), _rt(D), _rt(D), _rt(D), _cst((8, D)), _cst((1, D)), _cst((1, D))],
        out_specs=[_rt(D), _rt(D), _cst((8, D))],
        out_shape=[jax.ShapeDtypeStruct((L, D), F32), jax.ShapeDtypeStruct((L, D), _MXU),
                   jax.ShapeDtypeStruct((8, D), F32)],
        compiler_params=_params(("arbitrary",)),
    )(dr2, dh2, r1, out, modx, g1, b1)


def _conv_bwd(dxf, dxb, p, conv_w8, conv_b, dp):
    RT = p.shape[0]
    chunks = _seq_chunks(RT - TL)

    def body(df_ref, db_ref, p_ref, w_ref, b_ref, dp_any, o_ref, dw_ref, dbias_ref, dpre_s):
        del dp_any
        w = w_ref[...]
        bias = b_ref[...]
        srow = lax.broadcasted_iota(jnp.int32, (8, 128), 0)
        dwacc = jnp.zeros((8, 128), F32)
        dbacc = jnp.zeros((1, 128), F32)
        for r0, first, last in chunks:
            taps = _conv_taps(p_ref, r0, first, last)
            pre = bias + sum(w[k:k + 1, :] * taps[k] for k in range(5))
            s = _sig(pre)
            dpre = (df_ref[pl.ds(r0, TL), :] + db_ref[pl.ds(r0, TL), :]) * (s * (1.0 + pre * (1.0 - s)))
            dpre_s[pl.ds(r0, TL), :] = dpre
            dbacc = dbacc + _colsum(dpre)
            for k in range(5):
                dwacc = dwacc + jnp.where(srow == k, _colsum(dpre * taps[k]), 0.0)
        for r0, first, last in chunks:
            taps = _conv_taps(dpre_s, r0, first, last)
            dx = sum(w[k:k + 1, :] * taps[4 - k] for k in range(5))
            o_ref[pl.ds(r0, TL), :] = dx.astype(o_ref.dtype)
        dw_ref[...] = dwacc
        dbias_ref[...] = jnp.broadcast_to(dbacc, (8, 128))

    cspec = pl.BlockSpec((RT, 128), lambda j: (0, j))
    wspec = pl.BlockSpec((8, 128), lambda j: (0, j))
    return pl.pallas_call(
        body, name="conv_bwd", grid=(12,),
        in_specs=[cspec, cspec, pl.BlockSpec((RT, 128), lambda j: (0, _xbc_colblk(j))),
                  wspec, pl.BlockSpec((1, 128), lambda j: (0, j)), pl.BlockSpec(memory_space=pl.ANY)],
        out_specs=[pl.BlockSpec((RT, 128), lambda j: (0, _xbc_colblk(j))), wspec, wspec],
        out_shape=[jax.ShapeDtypeStruct(dp.shape, dp.dtype), jax.ShapeDtypeStruct((8, 1536), F32),
                   jax.ShapeDtypeStruct((8, 1536), F32)],
        scratch_shapes=[pltpu.VMEM((RT, 128), F32)],
        input_output_aliases={5: 0},
        compiler_params=_params(("parallel",)),
    )(dxf, dxb, p, conv_w8, conv_b, dp)


def _dt_bwd(ddf, ddb, dp):
    RT = ddf.shape[0]

    def body(f_ref, b_ref, dp_any, o_ref, st_ref):
        del dp_any

        @pl.when(pl.program_id(0) == 0)
        def _():
            st_ref[...] = jnp.zeros_like(st_ref)

        s = f_ref[...] + b_ref[...]
        o_ref[...] = s.astype(o_ref.dtype)
        st_ref[0:1, :] += _colsum(s)

    return pl.pallas_call(
        body, name="dt_bwd", grid=(RT // TL,),
        in_specs=[_rt(128), _rt(128), pl.BlockSpec(memory_space=pl.ANY)],
        out_specs=[_rt(128, ODT // 128), _cst((8, 128))],
        out_shape=[jax.ShapeDtypeStruct(dp.shape, dp.dtype), jax.ShapeDtypeStruct((8, 128), F32)],
        input_output_aliases={2: 0},
        compiler_params=_params(("arbitrary",)),
    )(ddf, ddb, dp)


def _ln0_bwd(dh1, dr1, x, ctx, g, b, modx, modc):
    L = x.shape[0]
    nt = L // TL

    def body(dh_ref, dr1_ref, x_ref, c_ref, g_ref, b_ref, mx_ref, mc_ref, gx_ref, st_ref):
        i = pl.program_id(0)
        isc = i == nt

        @pl.when(i == 0)
        def _():
            st_ref[...] = jnp.zeros_like(st_ref)

        xin = jnp.where(isc, c_ref[...], x_ref[...])
        xhat, rstd = _ln(xin)
        xn = xhat * g_ref[...] + b_ref[...]
        sc = jnp.where(isc, mc_ref[1:2, :], mx_ref[1:2, :])
        dh = dh_ref[...]
        lat = jnp.where(isc, 0.0, 1.0)
        dxn = dh * (1.0 + sc) + (lat * ALPHA) * dr1_ref[...]
        tsh = _colsum(dh)
        tsc = _colsum(dh * xn)
        st_ref[0:1, :] += lat * tsh
        st_ref[1:2, :] += lat * tsc
        st_ref[2:3, :] += (1.0 - lat) * tsh
        st_ref[3:4, :] += (1.0 - lat) * tsc
        st_ref[4:5, :] += _colsum(dxn * xhat)
        st_ref[5:6, :] += _colsum(dxn)

        @pl.when(i < nt)
        def _():
            gx_ref[...] = _ln_bwd(dxn * g_ref[...], xhat, rstd)

    return pl.pallas_call(
        body, name="ln0_bwd", grid=(nt + 1,),
        in_specs=[_rt(D), _rtc(D, nt), _rtc(D, nt), _cst((TL, D)), _cst((1, D)), _cst((1, D)),
                  _cst((8, D)), _cst((8, D))],
        out_specs=[_rtc(D, nt), _cst((8, D))],
        out_shape=[jax.ShapeDtypeStruct((L, D), F32), jax.ShapeDtypeStruct((8, D), F32)],
        compiler_params=_params(("arbitrary",)),
    )(dh1, dr1, x, ctx, g, b, modx, modc)


def _perm_cols(w):
    pad = jnp.zeros((w.shape[0], NPJ - NNAT), w.dtype)
    return jnp.concatenate([w[:, 0:1024], w[:, 2592:3616], w[:, 3616:4640], w[:, 1024:2048],
                            w[:, 4640:6688], w[:, 2048:2304], w[:, 2304:2560], w[:, 2560:2592], pad],
                           axis=1)


SECTIONS = ((0, 1024, OZ), (1024, 2048, OXS), (2048, 2304, OB), (2304, 2560, OC), (2560, 2592, ODT),
            (2592, 3616, OU), (3616, 4640, OV), (4640, 6688, OG))


def _perm_from_blocks(ga):
    n = ga.shape[2]
    pieces = []
    for na, nb, _ in sorted(SECTIONS, key=lambda sec: sec[2]):
        for k in range(NDEV):
            lo, hi = max(na, k * n), min(nb, (k + 1) * n)
            if lo < hi:
                pieces.append(ga[k][:, lo - k * n:hi - k * n])
    pieces.append(jnp.zeros((ga.shape[1], NPJ - NNAT), ga.dtype))
    return jnp.concatenate(pieces, axis=1)


def _blocks_from_perm(gp, n):
    blocks = []
    for k in range(NDEV):
        pieces = []
        for na, nb, po in SECTIONS:
            lo, hi = max(na, k * n), min(nb, (k + 1) * n)
            if lo < hi:
                pieces.append(gp[:, po + lo - na:po + hi - na])
        blocks.append(jnp.concatenate(pieces, axis=1))
    return jnp.stack(blocks)


def _padded(n, row_align):
    unit = row_align * D
    return -(-n // unit) * unit if row_align else n


def _slab(arrs, rows, row_align=0):
    parts = []
    for a in arrs:
        f = a.reshape(-1)
        parts.append(jnp.pad(f, (0, _padded(f.shape[0], row_align) - f.shape[0])))
    flat = jnp.concatenate(parts)
    flat = jnp.pad(flat, (0, rows * D - flat.shape[0]))
    return flat.reshape(rows, D)


def _unslab(slab, shapes, row_align=0):
    out, off = [], 0
    for shp in shapes:
        n = 1
        for s in shp:
            n *= s
        r0, r1 = off // D, -(-(off + n) // D)
        out.append(slab[r0:r1].reshape(-1)[off - r0 * D:off - r0 * D + n].reshape(shp))
        off += _padded(n, row_align)
    return out


def _row(v):
    return v.reshape(1, -1)


def _t(a):
    return jnp.swapaxes(a, 0, 1)


def _pad_rows(a, rows):
    return jnp.pad(a, ((0, rows - a.shape[0]), (0, 0)))


BIG = ["w_in", "w_ssd_proj", "w_gm_proj", "w_out", "w_ff1", "w_ff3", "w_ff2"]
BIG_ROWS = 2304
BIG_ALIGN = 16
REPL = ["c_ctx", "ln0_g", "ln0_b", "b_ada", "conv_b", "dt_bias", "a_log", "d_skip", "ssd_norm_g",
        "gm_norm_g", "gm_norm_b", "w_spatial", "b_spatial", "b_gate", "ln1_g", "ln1_b", "ln2_g", "ln2_b"]
SMALL_ROWS = 160
WEIGHTS = ["c_ctx", "ln0_g", "ln0_b", "w_ada", "b_ada", "w_in", "conv_w", "conv_b", "dt_bias", "a_log",
           "d_skip", "ssd_norm_g", "gm_norm_g", "gm_norm_b", "w_spatial", "b_spatial", "b_gate",
           "w_ssd_proj", "w_gm_proj", "w_out", "ln1_g", "ln1_b", "w_ff1", "w_ff3", "w_ff2", "ln2_g", "ln2_b"]


def kernel(x, c, ctx, c_ctx, ln0_g, ln0_b, w_ada, b_ada, w_in, conv_w, conv_b, dt_bias, a_log, d_skip, ssd_norm_g, gm_norm_g, gm_norm_b, w_spatial, b_spatial, b_gate, w_ssd_proj, w_gm_proj, w_out, ln1_g, ln1_b, w_ff1, w_ff3, w_ff2, ln2_g, ln2_b, loss_target, m_c_ctx, m_ln0_g, m_ln0_b, m_w_ada, m_b_ada, m_w_in, m_conv_w, m_conv_b, m_dt_bias, m_a_log, m_d_skip, m_ssd_norm_g, m_gm_norm_g, m_gm_norm_b, m_w_spatial, m_b_spatial, m_b_gate, m_w_ssd_proj, m_w_gm_proj, m_w_out, m_ln1_g, m_ln1_b, m_w_ff1, m_w_ff3, m_w_ff2, m_ln2_g, m_ln2_b, v_c_ctx, v_ln0_g, v_ln0_b, v_w_ada, v_b_ada, v_w_in, v_conv_w, v_conv_b, v_dt_bias, v_a_log, v_d_skip, v_ssd_norm_g, v_gm_norm_g, v_gm_norm_b, v_w_spatial, v_b_spatial, v_b_gate, v_w_ssd_proj, v_w_gm_proj, v_w_out, v_ln1_g, v_ln1_b, v_w_ff1, v_w_ff3, v_w_ff2, v_ln2_g, v_ln2_b):
    W = dict(c_ctx=c_ctx, ln0_g=ln0_g, ln0_b=ln0_b, w_ada=w_ada, b_ada=b_ada, w_in=w_in, conv_w=conv_w,
             conv_b=conv_b, dt_bias=dt_bias, a_log=a_log, d_skip=d_skip, ssd_norm_g=ssd_norm_g,
             gm_norm_g=gm_norm_g, gm_norm_b=gm_norm_b, w_spatial=w_spatial, b_spatial=b_spatial,
             b_gate=b_gate, w_ssd_proj=w_ssd_proj, w_gm_proj=w_gm_proj, w_out=w_out, ln1_g=ln1_g,
             ln1_b=ln1_b, w_ff1=w_ff1, w_ff3=w_ff3, w_ff2=w_ff2, ln2_g=ln2_g, ln2_b=ln2_b)
    M = dict(c_ctx=m_c_ctx, ln0_g=m_ln0_g, ln0_b=m_ln0_b, w_ada=m_w_ada, b_ada=m_b_ada, w_in=m_w_in,
             conv_w=m_conv_w, conv_b=m_conv_b, dt_bias=m_dt_bias, a_log=m_a_log, d_skip=m_d_skip,
             ssd_norm_g=m_ssd_norm_g, gm_norm_g=m_gm_norm_g, gm_norm_b=m_gm_norm_b,
             w_spatial=m_w_spatial, b_spatial=m_b_spatial, b_gate=m_b_gate, w_ssd_proj=m_w_ssd_proj,
             w_gm_proj=m_w_gm_proj, w_out=m_w_out, ln1_g=m_ln1_g, ln1_b=m_ln1_b, w_ff1=m_w_ff1,
             w_ff3=m_w_ff3, w_ff2=m_w_ff2, ln2_g=m_ln2_g, ln2_b=m_ln2_b)
    V = dict(c_ctx=v_c_ctx, ln0_g=v_ln0_g, ln0_b=v_ln0_b, w_ada=v_w_ada, b_ada=v_b_ada, w_in=v_w_in,
             conv_w=v_conv_w, conv_b=v_conv_b, dt_bias=v_dt_bias, a_log=v_a_log, d_skip=v_d_skip,
             ssd_norm_g=v_ssd_norm_g, gm_norm_g=v_gm_norm_g, gm_norm_b=v_gm_norm_b,
             w_spatial=v_w_spatial, b_spatial=v_b_spatial, b_gate=v_b_gate, w_ssd_proj=v_w_ssd_proj,
             w_gm_proj=v_w_gm_proj, w_out=v_w_out, ln1_g=v_ln1_g, ln1_b=v_ln1_b, w_ff1=v_w_ff1,
             w_ff3=v_w_ff3, w_ff2=v_w_ff2, ln2_g=v_ln2_g, ln2_b=v_ln2_b)

    me = 4 * lax.axis_index("x") + 2 * lax.axis_index("y") + lax.axis_index("c")
    xl, cx, tgt = x[0], ctx[0], loss_target[0]
    L = xl.shape[0]
    assert cx.shape[0] == TL and L % TL == 0
    ada_n = w_ada.shape[2]
    cw_n = conv_w.shape[2]

    small1 = _pad_rows(jnp.concatenate([c, _slab([conv_w[0]], 1)], axis=0), 8)
    g1 = _all_gather(small1, "ag_small")
    c_all = g1[:, 0, :]
    conv_w_full = g1[:, 1, :5 * cw_n].reshape(NDEV, 5, cw_n).transpose(1, 0, 2).reshape(5, NDEV * cw_n)
    sq = w_ssd_proj.shape[1]
    ffr = w_ff2.shape[1]
    ffc = w_ff1.shape[2]
    late = [jnp.concatenate([w_ssd_proj[0], w_gm_proj[0], w_out[0], w_ff2[0]], axis=0).astype(_MXU),
            _t(w_ff1[0]).astype(_MXU), _t(w_ff3[0]).astype(_MXU)]

    c16 = _pad_rows(jnp.concatenate([c_all, _row(c_ctx)], axis=0), 16)
    b_ada_sh = lax.dynamic_slice(b_ada, (0, ada_n * me), (1, ada_n))
    modp = _ada_fwd(c16, w_ada[0], b_ada_sh)
    mod16 = _all_gather(modp, "ag_mod").transpose(1, 0, 2).reshape(16, NDEV * ada_n)

    ga, = _all_gather_multi([w_in[0].astype(_MXU)], "ag_w_in")
    ga, late, mod16 = lax.optimization_barrier((ga, late, mod16))
    lw_send, lw_recv, lw_src, lw_land, lw_token = _exchange_start(late, "ag_late_start", gather=True)
    w_in_p = _perm_from_blocks(ga)
    modx = _pad_rows(lax.dynamic_slice(mod16, (me, 0), (1, 6 * D)).reshape(6, D), 8) + lw_token[0, 0]
    modc = _pad_rows(mod16[8].reshape(6, D), 8)

    g0, b0 = _row(ln0_g), _row(ln0_b)
    xn, h1 = _ln0_fwd(xl, cx, g0, b0, modx, modc)
    p = _mm(h1, w_in_p, "nn", F32, "mm_p")
    conv_w8 = _pad_rows(conv_w_full, 8)
    xbc = _conv_fwd(p, conv_w8, conv_b)
    prm = _pad_rows(jnp.pad(jnp.stack([dt_bias.reshape(32), a_log.reshape(32)]), ((0, 0), (0, 96))), 8)
    yf, yb, hpf, hpb = _ssd2_fwd(xbc, p, prm)
    lw_land = _exchange_wait(lw_send, lw_recv, lw_src, lw_land, yf, "ag_late_wait", gather=True)
    fw_send, fw_recv, lw_land, fw_token = _forward_start(lw_land, "ag_fwd_start")
    dsk = _row(jnp.repeat(d_skip[0, 0] + d_skip[0, 1], HP)) + fw_token[0:1, 0:1]
    ws_m = w_spatial[0].astype(_MXU)
    bsT = jnp.pad(b_spatial[0].T, ((0, 0), (0, 120)))
    mixp = (dsk, ssd_norm_g, gm_norm_g, gm_norm_b, ws_m, bsT)
    yssd, ygm = _mix_fwd(yf, yb, p, xbc, *mixp)
    gb, gc1, gc2 = _forward_wait(fw_send, fw_recv, lw_land, yssd, "ag_fwd_wait")

    def with_own(g, mine, k):
        return jnp.where(me == k, mine, g[k])

    gb = jnp.stack([with_own(gb, late[0], k) for k in range(NDEV)])
    w_ssd_f = gb[:, 0:sq].reshape(NDEV * sq, D)
    w_gm_f = gb[:, sq:2 * sq].reshape(NDEV * sq, D)
    w_out_f = gb[:, 2 * sq:3 * sq].reshape(NDEV * sq, D)
    w_ff2_f = gb[:, 3 * sq:3 * sq + ffr].reshape(NDEV * ffr, D)
    w13t = jnp.concatenate([with_own(gc1, late[1], k) for k in range(NDEV)]
                           + [with_own(gc2, late[2], k) for k in range(NDEV)], axis=0)
    a1 = _mm(yssd, w_ssd_f, "nn", F32, "mm_a1")
    a2 = _mm(ygm, w_gm_f, "nn", F32, "mm_a2")
    merged = _gate_fwd(a1, a2, p, b_gate)
    out = _mm(merged, w_out_f, "nn", F32, "mm_out")
    r1, h2 = _res1_fwd(xn, out, modx, ln1_g, ln1_b)
    f13 = _mm(h2, w13t, "nt", F32, "mm_f13")
    ff = _glu_fwd(f13)
    o2 = _mm(ff, w_ff2_f, "nn", F32, "mm_o2")

    dr2, do2, st2, loss_slab = _res2(r1, o2, tgt, modx, ln1_g, ln1_b, ln2_g, ln2_b)
    loss = lax.psum(loss_slab[0, 0], ("x", "y", "c"))
    dff = _mm(do2, w_ff2_f, "nt", F32, "mm_dff")
    df13 = _glu_bwd(dff, f13)
    dh2 = _mm(df13, w13t, "nn", F32, "mm_dh2")
    dw_ff2 = _mm(ff, do2, "tn", _MXU, "mm_dw_ff2")
    dw13t = _mm(df13, h2, "tn", _MXU, "mm_dw13")
    xff = [dw_ff2.reshape(NDEV, ffr, D), dw13t[:DFF].reshape(NDEV, ffc, D), dw13t[DFF:].reshape(NDEV, ffc, D)]
    ff_send, ff_recv, ff_src, ff_land, ff_token = _exchange_start(xff, "xchg_ff_start")
    modx = modx + ff_token[0, 0]
    dr1, dout, st1 = _res1_bwd(dr2, dh2, r1, out, modx, ln1_g, ln1_b)
    dmg = _mm(dout, w_out_f, "nt", F32, "mm_dmerged")
    dw_out = _mm(merged, dout, "tn", _MXU, "mm_dw_out")
    dp = jnp.zeros((L + TL, NPJ), _MXU)
    dp, da1, da2, stg = _gate_bwd(dmg, a1, a2, p, b_gate, dp)
    dys = _mm(da1, w_ssd_f, "nt", F32, "mm_dyssd")
    dym = _mm(da2, w_gm_f, "nt", F32, "mm_dygm")
    dw_ssd = _mm(yssd, da1, "tn", _MXU, "mm_dw_ssd")
    dw_gm = _mm(ygm, da2, "tn", _MXU, "mm_dw_gm")
    xsq = [jnp.concatenate([dw_ssd.reshape(NDEV, sq, D), dw_gm.reshape(NDEV, sq, D),
                            dw_out.reshape(NDEV, sq, D)], axis=1)]
    sq_send, sq_recv, sq_src, sq_land, sq_token = _exchange_start(xsq, "xchg_sq_start")
    mixp = (dsk + sq_token[0:1, 0:1],) + mixp[1:]
    dp, dyd, stm, dws, dbsT = _mix_bwd(dys, dym, yf, yb, p, xbc, dp, *mixp)
    dxf, dxb, ddf, ddb, sts = _ssd2_bwd(xbc, p, prm, dsk, dyd, hpf, hpb)
    dp, dcw, dcb = _conv_bwd(dxf, dxb, p, conv_w8, conv_b, dp)
    dp, std = _dt_bwd(ddf, ddb, dp)
    dh1 = _mm(dp, w_in_p, "nt", F32, "mm_dh1")
    dw_in_p = _mm(h1, dp, "tn", _MXU, "mm_dw_in")
    xin = [_blocks_from_perm(dw_in_p, w_in.shape[2])]
    in_send, in_recv, in_src, in_land, in_token = _exchange_start(xin, "xchg_in_start")
    modx = modx + in_token[0, 0]
    grad_x, st0 = _ln0_bwd(dh1, dr1, xl, cx, g0, b0, modx, modc)

    zero = jnp.zeros((D,), F32)
    dmod = jnp.stack([jnp.concatenate([st0[0], st0[1], st1[4], st1[1], st1[0], st2[2]]),
                      jnp.concatenate([st0[2], st0[3], zero, zero, zero, zero])])
    g16 = _all_gather(_pad_rows(dmod, 8), "ag_dmod")[:, 0:2, :].reshape(16, 6 * D)
    g16_sh = lax.dynamic_slice(g16, (0, ada_n * me), (16, ada_n))
    c16b = jnp.stack([c_all, jnp.broadcast_to(_row(c_ctx), (NDEV, D))], axis=1).reshape(16, D)
    dw_ada, db_ada8, dcc8 = _ada_bwd(c16b, g16, g16_sh, w_ada[0])

    part = dict(
        c_ctx=dcc8[0], ln0_g=st0[4], ln0_b=st0[5], conv_w=dcw[0:5], conv_b=dcb[0],
        dt_bias=std[0, 0:32], a_log=sts[0, 0:32], d_skip=jnp.tile(sts[1, 0:16], 2),
        ssd_norm_g=stm[0], gm_norm_g=stm[1], gm_norm_b=stm[2], w_spatial=dws,
        b_spatial=dbsT[:, 0:8].T, b_gate=stg[0], ln1_g=st1[2], ln1_b=st1[3], ln2_g=st2[0], ln2_b=st2[1])
    pnames = list(part)
    psum8 = _sum8(_all_gather(_slab([part[n] for n in pnames], SMALL_ROWS), "ag_smallgrads"), "sum_smallgrads")
    small = dict(zip(pnames, _unslab(psum8, [part[n].shape for n in pnames])))
    grads = {n: small[n].reshape(W[n].shape) for n in pnames if n != "conv_w"}
    grads["conv_w"] = lax.dynamic_slice(small["conv_w"], (0, cw_n * me), (5, cw_n)).reshape(conv_w.shape)
    grads["b_ada"] = db_ada8[0:1]
    grads["w_ada"] = dw_ada.reshape(w_ada.shape)

    delta, new_m, new_v = {}, {}, {}

    def adam_group(names, rows, tag, align=0):
        shapes = [W[n].shape for n in names]
        outs = _adamw(*[_slab([src[n] for n in names], rows, align) for src in (grads, W, M, V)], tag)
        for res, slab in zip((delta, new_m, new_v), outs):
            for n, a in zip(names, _unslab(slab, shapes, align)):
                res[n] = a

    adam_group(REPL + ["conv_w"], SMALL_ROWS, "adamw_small")
    res = _adamw(grads["w_ada"][0], w_ada[0], m_w_ada[0], v_w_ada[0], "adamw_w_ada")
    delta["w_ada"], new_m["w_ada"], new_v["w_ada"] = [a[None] for a in res]

    rff = _exchange_wait(ff_send, ff_recv, ff_src, ff_land, st0, "xchg_ff_wait")
    rsq = _exchange_wait(sq_send, sq_recv, sq_src, sq_land, rff[0], "xchg_sq_wait")
    rin = _exchange_wait(in_send, in_recv, in_src, in_land, delta["ln2_b"], "xchg_in_wait")

    def own(blocks):
        return lax.dynamic_index_in_dim(blocks, me, 0, keepdims=False)

    for n, r8, mine, row0, tr in (
            ("w_ff2", rff[0], own(xff[0]), 0, ffr // 2), ("w_ssd_proj", rsq[0], own(xsq[0]), 0, sq),
            ("w_gm_proj", rsq[0], own(xsq[0]), sq, sq), ("w_out", rsq[0], own(xsq[0]), 2 * sq, sq),
            ("w_in", rin[0], own(xin[0]), 0, 256)):
        res = _adamw_sum(r8, mine, W[n][0], M[n][0], V[n][0], row0, tr, "adamw_" + n)
        grads[n], delta[n], new_m[n], new_v[n] = [a[None] for a in res]
    for n, r8, mine in (("w_ff1", rff[1], own(xff[1])), ("w_ff3", rff[2], own(xff[2]))):
        res = _adamw_sum(r8, mine, _t(W[n][0]), _t(M[n][0]), _t(V[n][0]), 0, ffc // 2, "adamw_" + n)
        grads[n], delta[n], new_m[n], new_v[n] = [_t(a)[None] for a in res]

    return (loss, grad_x[None], *[grads[n] for n in WEIGHTS], *[delta[n] for n in WEIGHTS],
            *[new_m[n] for n in WEIGHTS], *[new_v[n] for n in WEIGHTS])
```

```python
import functools

import jax
import jax.numpy as jnp
from jax import lax
from jax.experimental import pallas as pl
from jax.experimental.pallas import tpu as pltpu

_MXU = jnp.bfloat16
F32 = jnp.float32
D = 1024
TL = 256
Q = 128
NH, HP, NS, HPG = 16, 64, 128, 8
DFF = 2816
ALPHA = 2.0 ** 0.25
EPS = 1e-5
OZ, OU, OV, OXS, OG, OB, OC, ODT, NPJ = 0, 1024, 2048, 3072, 4096, 6144, 6400, 6656, 6912
NNAT = 6688
NDEV = 8
ADAM_LR, ADAM_B1, ADAM_B2, ADAM_EPS, ADAM_WD, ADAM_STEP = 1e-3, 0.9, 0.999, 1e-8, 0.01, 10
VMEM_LIMIT = 48 * 1024 * 1024

NN = ((1,), (0,))
NT = ((1,), (1,))
TN = ((0,), (0,))
MESH = pl.DeviceIdType.MESH


def _dot(a, b, dims):
    return lax.dot_general(a.astype(_MXU), b.astype(_MXU), (dims, ((), ())),
                           preferred_element_type=F32)


def _tile(n, cands):
    for c in cands:
        if n % c == 0:
            return c
    return n


def _divisor_tile(n, cap, mult):
    best = n
    for t in range(mult, min(n, cap) + 1, mult):
        if n % t == 0:
            best = t
    return best


def _params(sem):
    return pltpu.CompilerParams(dimension_semantics=sem, vmem_limit_bytes=VMEM_LIMIT)


def _cst(shape):
    nd = len(shape)
    return pl.BlockSpec(shape, lambda *_: (0,) * nd)


def _rt(w, cb=0, rows=TL):
    return pl.BlockSpec((rows, w), lambda i: (i, cb))


def _rtc(w, nt, cb=0):
    return pl.BlockSpec((TL, w), lambda i: (jnp.minimum(i, nt - 1), cb))


def _sig(x):
    return jax.nn.sigmoid(x)


def _softplus(x):
    return jnp.maximum(x, 0.0) + jnp.log1p(jnp.exp(-jnp.abs(x)))


_G0, _G1 = 0.7978845608028654, 0.044715


def _gelu(x):
    t = jnp.tanh(_G0 * (x + _G1 * x * x * x))
    return 0.5 * x * (1.0 + t), t


def _gelu_grad(x, t):
    return 0.5 * (1.0 + t) + 0.5 * x * (1.0 - t * t) * _G0 * (1.0 + 3.0 * _G1 * x * x)


def _ln(r):
    mu = jnp.mean(r, axis=-1, keepdims=True)
    xc = r - mu
    var = jnp.mean(xc * xc, axis=-1, keepdims=True)
    rstd = lax.rsqrt(var + EPS)
    return xc * rstd, rstd


def _ln_bwd(dyh, xhat, rstd):
    return rstd * (dyh - jnp.mean(dyh, axis=-1, keepdims=True)
                   - xhat * jnp.mean(dyh * xhat, axis=-1, keepdims=True))


def _colsum(v):
    return jnp.sum(v, axis=0, keepdims=True)


def _sum11(v):
    return jnp.sum(jnp.sum(v, axis=1, keepdims=True), axis=0, keepdims=True)


def _cumsum_rows(a, rev):
    n = a.shape[0]
    row = lax.broadcasted_iota(jnp.int32, a.shape, 0)
    s = 1
    while s < n:
        if rev:
            a = a + jnp.where(row < n - s, pltpu.roll(a, n - s, 0), 0.0)
        else:
            a = a + jnp.where(row >= s, pltpu.roll(a, s, 0), 0.0)
        s *= 2
    return a


def _mm(a, b, mode, out_dtype, name):
    if mode == "tn":
        K, M = a.shape
    else:
        M, K = a.shape
    N = b.shape[0] if mode == "nt" else b.shape[1]
    tm = _divisor_tile(M, 1408, 128) if mode == "tn" else _divisor_tile(M, 1088, 16)
    tn = _divisor_tile(N, 1408, 128)
    tk = _divisor_tile(K, 2304, 128)
    nk = K // tk
    dims = {"nn": NN, "nt": NT, "tn": TN}[mode]
    use_acc = nk > 1 and out_dtype != F32

    def body(a_ref, b_ref, o_ref, *acc):
        prod = _dot(a_ref[...], b_ref[...], dims)
        if nk == 1:
            o_ref[...] = prod.astype(o_ref.dtype)
            return
        acc_ref = acc[0] if use_acc else o_ref
        k = pl.program_id(2)

        @pl.when(k == 0)
        def _():
            acc_ref[...] = prod

        if use_acc:
            @pl.when((k > 0) & (k < nk - 1))
            def _():
                acc_ref[...] += prod

            @pl.when(k == nk - 1)
            def _():
                o_ref[...] = (acc_ref[...] + prod).astype(o_ref.dtype)
        else:
            @pl.when(k > 0)
            def _():
                o_ref[...] += prod

    if mode == "tn":
        a_spec = pl.BlockSpec((tk, tm), lambda i, j, k: (k, i))
    else:
        a_spec = pl.BlockSpec((tm, tk), lambda i, j, k: (i, k))
    if mode == "nt":
        b_spec = pl.BlockSpec((tn, tk), lambda i, j, k: (j, k))
    else:
        b_spec = pl.BlockSpec((tk, tn), lambda i, j, k: (k, j))
    return pl.pallas_call(
        body, name=name, grid=(M // tm, N // tn, nk),
        in_specs=[a_spec, b_spec],
        out_specs=pl.BlockSpec((tm, tn), lambda i, j, k: (i, j)),
        out_shape=jax.ShapeDtypeStruct((M, N), out_dtype),
        scratch_shapes=[pltpu.VMEM((tm, tn), F32)] if use_acc else [],
        compiler_params=_params(("parallel", "parallel", "arbitrary")),
    )(a, b)


def _all_gather(x, name):
    def body(x_ref, out_ref, send_sems, recv_sems, local_sem):
        mx, my, mc = lax.axis_index("x"), lax.axis_index("y"), lax.axis_index("c")
        me, sibling = (mx, my, mc), (mx, my, 1 - mc)
        chips = [(1 - mx, my), (mx, 1 - my), (1 - mx, 1 - my)]

        def slot(px, py, pc):
            return out_ref.at[4 * px + 2 * py + pc]

        def copy(k, block, to, src=None):
            return pltpu.make_async_remote_copy(
                src_ref=slot(*block) if src is None else src, dst_ref=slot(*block),
                send_sem=send_sems.at[k], recv_sem=recv_sems.at[k],
                device_id=to, device_id_type=MESH)

        mine = pltpu.make_async_copy(x_ref, slot(*me), local_sem)
        mine.start()
        first = [copy(0, me, sibling, src=x_ref)]
        first += [copy(1 + j, me, (*chip, mc), src=x_ref) for j, chip in enumerate(chips)]
        for cp in first:
            cp.start()
        passed = [copy(4 + j, (*chip, mc), sibling) for j, chip in enumerate(chips)]
        for j, chip in enumerate(chips):
            copy(1 + j, (*chip, mc), me).wait_recv()
            passed[j].start()
        copy(0, sibling, me).wait_recv()
        for j, chip in enumerate(chips):
            copy(4 + j, (*chip, 1 - mc), me).wait_recv()
        for cp in first + passed:
            cp.wait_send()
        mine.wait()

    return pl.pallas_call(
        body, name=name,
        out_shape=jax.ShapeDtypeStruct((NDEV,) + x.shape, x.dtype),
        in_specs=[pl.BlockSpec(memory_space=pl.ANY)],
        out_specs=pl.BlockSpec(memory_space=pl.ANY),
        scratch_shapes=[pltpu.SemaphoreType.DMA((7,)), pltpu.SemaphoreType.DMA((7,)),
                        pltpu.SemaphoreType.DMA],
    )(x)


def _owner_exchange(g, name):
    def body(g_ref, out_ref, send_sems, recv_sems, local_sem):
        mx, my, mc = lax.axis_index("x"), lax.axis_index("y"), lax.axis_index("c")
        local = pltpu.make_async_copy(g_ref.at[4 * mx + 2 * my + mc], out_ref.at[0], local_sem)
        local.start()
        copies = []
        for f in range(1, NDEV):
            px = 1 - mx if (f >> 2) & 1 else mx
            py = 1 - my if (f >> 1) & 1 else my
            pc = 1 - mc if f & 1 else mc
            cp = pltpu.make_async_remote_copy(
                src_ref=g_ref.at[4 * px + 2 * py + pc], dst_ref=out_ref.at[f],
                send_sem=send_sems.at[f - 1], recv_sem=recv_sems.at[f - 1],
                device_id=(px, py, pc), device_id_type=MESH)
            cp.start()
            copies.append(cp)
        for cp in copies:
            cp.wait_recv()
        for cp in copies:
            cp.wait_send()
        local.wait()

    return pl.pallas_call(
        body, name=name,
        out_shape=jax.ShapeDtypeStruct(g.shape, g.dtype),
        in_specs=[pl.BlockSpec(memory_space=pl.ANY)],
        out_specs=pl.BlockSpec(memory_space=pl.ANY),
        scratch_shapes=[pltpu.SemaphoreType.DMA((7,)), pltpu.SemaphoreType.DMA((7,)),
                        pltpu.SemaphoreType.DMA],
    )(g)


def _any_specs(n):
    return [pl.BlockSpec(memory_space=pl.ANY)] * n


def _all_gather_multi(xs, name):
    na = len(xs)

    def body(*refs):
        x_refs, out_refs = refs[:na], refs[na:2 * na]
        send_sems, recv_sems, local_sems = refs[2 * na:]
        mx, my, mc = lax.axis_index("x"), lax.axis_index("y"), lax.axis_index("c")
        me, sibling = (mx, my, mc), (mx, my, 1 - mc)
        chips = [(1 - mx, my), (mx, 1 - my), (1 - mx, 1 - my)]

        def copy(a, k, block, to, src=None):
            slot = out_refs[a].at[4 * block[0] + 2 * block[1] + block[2]]
            return pltpu.make_async_remote_copy(
                src_ref=slot if src is None else src, dst_ref=slot,
                send_sem=send_sems.at[7 * a + k], recv_sem=recv_sems.at[7 * a + k],
                device_id=to, device_id_type=MESH)

        mine = [pltpu.make_async_copy(x_refs[a], out_refs[a].at[4 * mx + 2 * my + mc], local_sems.at[a])
                for a in range(na)]
        for cp in mine:
            cp.start()
        first = []
        for a in range(na):
            first.append(copy(a, 0, me, sibling, src=x_refs[a]))
            first += [copy(a, 1 + j, me, (*chip, mc), src=x_refs[a]) for j, chip in enumerate(chips)]
        for cp in first:
            cp.start()
        passed = []
        for a in range(na):
            for j, chip in enumerate(chips):
                copy(a, 1 + j, (*chip, mc), me).wait_recv()
                fwd = copy(a, 4 + j, (*chip, mc), sibling)
                fwd.start()
                passed.append(fwd)
        for a in range(na):
            copy(a, 0, sibling, me).wait_recv()
            for j, chip in enumerate(chips):
                copy(a, 4 + j, (*chip, 1 - mc), me).wait_recv()
        for cp in first + passed:
            cp.wait_send()
        for cp in mine:
            cp.wait()

    return pl.pallas_call(
        body, name=name,
        out_shape=[jax.ShapeDtypeStruct((NDEV,) + x.shape, x.dtype) for x in xs],
        in_specs=_any_specs(na), out_specs=_any_specs(na),
        scratch_shapes=[pltpu.SemaphoreType.DMA((7 * na,)), pltpu.SemaphoreType.DMA((7 * na,)),
                        pltpu.SemaphoreType.DMA((na,))],
    )(*xs)


def _owner_exchange_multi(gs, name):
    na = len(gs)

    def body(*refs):
        g_refs, out_refs = refs[:na], refs[na:2 * na]
        send_sems, recv_sems, local_sems = refs[2 * na:]
        mx, my, mc = lax.axis_index("x"), lax.axis_index("y"), lax.axis_index("c")
        locals_ = [pltpu.make_async_copy(g_refs[a].at[4 * mx + 2 * my + mc], out_refs[a].at[0], local_sems.at[a])
                   for a in range(na)]
        for cp in locals_:
            cp.start()
        copies = []
        for a in range(na):
            for f in range(1, NDEV):
                px = 1 - mx if (f >> 2) & 1 else mx
                py = 1 - my if (f >> 1) & 1 else my
                pc = 1 - mc if f & 1 else mc
                cp = pltpu.make_async_remote_copy(
                    src_ref=g_refs[a].at[4 * px + 2 * py + pc], dst_ref=out_refs[a].at[f],
                    send_sem=send_sems.at[7 * a + f - 1], recv_sem=recv_sems.at[7 * a + f - 1],
                    device_id=(px, py, pc), device_id_type=MESH)
                cp.start()
                copies.append(cp)
        for cp in copies:
            cp.wait_recv()
        for cp in copies:
            cp.wait_send()
        for cp in locals_:
            cp.wait()

    return pl.pallas_call(
        body, name=name,
        out_shape=[jax.ShapeDtypeStruct(g.shape, g.dtype) for g in gs],
        in_specs=_any_specs(na), out_specs=_any_specs(na),
        scratch_shapes=[pltpu.SemaphoreType.DMA((7 * na,)), pltpu.SemaphoreType.DMA((7 * na,)),
                        pltpu.SemaphoreType.DMA((na,))],
    )(*gs)


def _adamw_sum(r8, own, w, m, v, row0, tr, name):
    R, C = w.shape
    assert row0 % tr == 0
    blk0 = row0 // tr
    bc1 = 1.0 - ADAM_B1 ** ADAM_STEP
    bc2 = 1.0 - ADAM_B2 ** ADAM_STEP

    def body(r_ref, *refs):
        if own is None:
            gg = r_ref[0].astype(F32)
        else:
            gg = refs[0][...].astype(F32)
            refs = refs[1:]
        w_ref, m_ref, v_ref, g_ref, d_ref, mo_ref, vo_ref = refs
        for k in range(1, NDEV):
            gg = gg + r_ref[k].astype(F32)
        mn = ADAM_B1 * m_ref[...] + (1.0 - ADAM_B1) * gg
        vn = ADAM_B2 * v_ref[...] + (1.0 - ADAM_B2) * (gg * gg)
        mh = mn / bc1
        vh = vn / bc2
        g_ref[...] = gg
        d_ref[...] = -ADAM_LR * (mh / (jnp.sqrt(vh) + ADAM_EPS) + ADAM_WD * w_ref[...])
        mo_ref[...] = mn
        vo_ref[...] = vn

    spec = pl.BlockSpec((tr, C), lambda i: (i, 0))
    sh = jax.ShapeDtypeStruct((R, C), F32)
    own_ops = [] if own is None else [own]
    own_specs = [] if own is None else [pl.BlockSpec((tr, C), lambda i: (i + blk0, 0))]
    return pl.pallas_call(
        body, name=name, grid=(R // tr,),
        in_specs=[pl.BlockSpec((NDEV, tr, C), lambda i: (0, i + blk0, 0))] + own_specs + [spec, spec, spec],
        out_specs=[spec] * 4, out_shape=[sh] * 4, compiler_params=_params(("parallel",)),
    )(r8, *own_ops, w, m, v)


_HBM = pl.BlockSpec(memory_space=pltpu.HBM)
_SEM = pl.BlockSpec(memory_space=pltpu.SEMAPHORE)
_EFFECT = pltpu.SideEffectType.DATAFLOW_SIDE_EFFECTING


def _exchange_copies(g_refs, land_refs, send_sems, recv_sems, gather):
    mx, my, mc = lax.axis_index("x"), lax.axis_index("y"), lax.axis_index("c")
    copies = []
    for a in range(len(g_refs)):
        for f in ((1, 2, 4, 6) if gather else range(1, NDEV)):
            px = 1 - mx if (f >> 2) & 1 else mx
            py = 1 - my if (f >> 1) & 1 else my
            pc = 1 - mc if f & 1 else mc
            src = g_refs[a] if gather else g_refs[a].at[4 * px + 2 * py + pc]
            dst = land_refs[a].at[4 * mx + 2 * my + mc] if gather else land_refs[a].at[f]
            copies.append(pltpu.make_async_remote_copy(
                src_ref=src, dst_ref=dst,
                send_sem=send_sems.at[7 * a + f - 1], recv_sem=recv_sems.at[7 * a + f - 1],
                device_id=(px, py, pc), device_id_type=MESH))
    return copies


def _exchange_start(gs, name, gather=False):
    na = len(gs)

    def body(*refs):
        for cp in _exchange_copies(refs[:na], refs[na:2 * na], refs[2 * na], refs[2 * na + 1], gather):
            cp.start()
        refs[-1][...] = jnp.zeros_like(refs[-1])

    hbm = [pltpu.HBM(g.shape, g.dtype) for g in gs]
    land_shapes = [((NDEV,) + g.shape) if gather else g.shape for g in gs]
    lands = [pltpu.with_memory_space_constraint(lax.empty(shp, g.dtype), pltpu.HBM)
             for shp, g in zip(land_shapes, gs)]
    hbm_land = [pltpu.HBM(shp, g.dtype) for shp, g in zip(land_shapes, gs)]
    outs = pl.pallas_call(
        body, name=name,
        out_shape=(pltpu.SemaphoreType.DMA((7 * na,)), pltpu.SemaphoreType.DMA((7 * na,)), *hbm, *hbm_land,
                   jax.ShapeDtypeStruct((8, 128), F32)),
        in_specs=[_HBM] * (2 * na),
        out_specs=(_SEM, _SEM, *([_HBM] * (2 * na)), pl.BlockSpec(memory_space=pltpu.VMEM)),
        input_output_aliases={i: 2 + i for i in range(2 * na)},
        compiler_params=pltpu.CompilerParams(has_side_effects=_EFFECT),
    )(*[pltpu.with_memory_space_constraint(g, pltpu.HBM) for g in gs], *lands)
    return outs[0], outs[1], outs[2:2 + na], outs[2 + na:2 + 2 * na], outs[-1]


def _forward_copies(land_refs, send_sems, recv_sems):
    mx, my, mc = lax.axis_index("x"), lax.axis_index("y"), lax.axis_index("c")
    copies = []
    for a in range(len(land_refs)):
        for j, (fx, fy) in enumerate(((0, 1), (1, 0), (1, 1))):
            px = 1 - mx if fx else mx
            py = 1 - my if fy else my
            blk = land_refs[a].at[4 * px + 2 * py + mc]
            copies.append(pltpu.make_async_remote_copy(
                src_ref=blk, dst_ref=blk, send_sem=send_sems.at[3 * a + j], recv_sem=recv_sems.at[3 * a + j],
                device_id=(mx, my, 1 - mc), device_id_type=MESH))
    return copies


def _forward_start(lands, name):
    na = len(lands)

    def body(*refs):
        for cp in _forward_copies(refs[:na], refs[na], refs[na + 1]):
            cp.start()
        refs[-1][...] = jnp.zeros_like(refs[-1])

    outs = pl.pallas_call(
        body, name=name,
        out_shape=(pltpu.SemaphoreType.DMA((3 * na,)), pltpu.SemaphoreType.DMA((3 * na,)),
                   *[pltpu.HBM(g.shape, g.dtype) for g in lands], jax.ShapeDtypeStruct((8, 128), F32)),
        in_specs=[_HBM] * na,
        out_specs=(_SEM, _SEM, *([_HBM] * na), pl.BlockSpec(memory_space=pltpu.VMEM)),
        input_output_aliases={i: 2 + i for i in range(na)},
        compiler_params=pltpu.CompilerParams(has_side_effects=_EFFECT),
    )(*lands)
    return outs[0], outs[1], outs[2:2 + na], outs[-1]


def _forward_wait(send_sems, recv_sems, lands, after, name):
    na = len(lands)

    def body(*refs):
        for cp in _forward_copies(refs[:na], refs[na], refs[na + 1]):
            cp.wait_send()
            cp.wait_recv()

    return pl.pallas_call(
        body, name=name,
        out_shape=tuple(pltpu.HBM(g.shape, g.dtype) for g in lands),
        in_specs=[_HBM] * na + [_SEM, _SEM, pl.BlockSpec(memory_space=pl.ANY)],
        out_specs=tuple([_HBM] * na),
        input_output_aliases={i: i for i in range(na)},
        compiler_params=pltpu.CompilerParams(has_side_effects=_EFFECT),
    )(*lands, send_sems, recv_sems, after)


def _exchange_wait(send_sems, recv_sems, g_thru, land_thru, after, name, gather=False):
    na = len(g_thru)

    def body(*refs):
        for cp in _exchange_copies(refs[:na], refs[na:2 * na], refs[2 * na], refs[2 * na + 1], gather):
            cp.wait_send()
            cp.wait_recv()

    outs = pl.pallas_call(
        body, name=name,
        out_shape=tuple(pltpu.HBM(g.shape, g.dtype) for g in list(g_thru) + list(land_thru)),
        in_specs=[_HBM] * (2 * na) + [_SEM, _SEM, pl.BlockSpec(memory_space=pl.ANY)],
        out_specs=tuple([_HBM] * (2 * na)),
        input_output_aliases={i: i for i in range(2 * na)},
        compiler_params=pltpu.CompilerParams(has_side_effects=_EFFECT),
    )(*g_thru, *land_thru, send_sems, recv_sems, after)
    return outs[na:]


def _sum8(r, name):
    _, R, C = r.shape
    tr = _tile(R, (256, 160, 128, 64, 32, 16, 8))

    def body(r_ref, o_ref):
        acc = r_ref[0].astype(F32)
        for k in range(1, NDEV):
            acc = acc + r_ref[k].astype(F32)
        o_ref[...] = acc

    return pl.pallas_call(
        body, name=name, grid=(R // tr,),
        in_specs=[pl.BlockSpec((NDEV, tr, C), lambda i: (0, i, 0))],
        out_specs=pl.BlockSpec((tr, C), lambda i: (i, 0)),
        out_shape=jax.ShapeDtypeStruct((R, C), F32),
        compiler_params=_params(("parallel",)),
    )(r)


def _adamw(g, w, m, v, name):
    R, C = g.shape
    tr = _tile(R, (256, 160, 128, 64, 32, 16, 8))
    bc1 = 1.0 - ADAM_B1 ** ADAM_STEP
    bc2 = 1.0 - ADAM_B2 ** ADAM_STEP

    def body(g_ref, w_ref, m_ref, v_ref, d_ref, mo_ref, vo_ref):
        gg = g_ref[...]
        mn = ADAM_B1 * m_ref[...] + (1.0 - ADAM_B1) * gg
        vn = ADAM_B2 * v_ref[...] + (1.0 - ADAM_B2) * (gg * gg)
        mh = mn / bc1
        vh = vn / bc2
        d_ref[...] = -ADAM_LR * (mh / (jnp.sqrt(vh) + ADAM_EPS) + ADAM_WD * w_ref[...])
        mo_ref[...] = mn
        vo_ref[...] = vn

    spec = pl.BlockSpec((tr, C), lambda i: (i, 0))
    sh = jax.ShapeDtypeStruct((R, C), F32)
    return pl.pallas_call(
        body, name=name, grid=(R // tr,), in_specs=[spec] * 4, out_specs=[spec] * 3,
        out_shape=[sh] * 3, compiler_params=_params(("parallel",)),
    )(g, w, m, v)


def _ada_fwd(c16, w_sh, b_sh):
    def body(c_ref, w_ref, b_ref, o_ref):
        c = c_ref[...]
        o_ref[...] = _dot(c * _sig(c), w_ref[...], NN) + b_ref[...]

    return pl.pallas_call(
        body, name="ada_fwd", out_shape=jax.ShapeDtypeStruct((16, w_sh.shape[1]), F32),
        compiler_params=pltpu.CompilerParams(vmem_limit_bytes=VMEM_LIMIT),
    )(c16, w_sh, b_sh)


def _ada_bwd(c16, g16, g16_sh, w_sh):
    ncol = w_sh.shape[1]

    def body(c_ref, g_ref, gs_ref, w_ref, dw_ref, db_ref, dc_ref):
        c = c_ref[...]
        s = _sig(c)
        gs = gs_ref[...]
        dw_ref[...] = _dot(c * s, gs, TN)
        db_ref[...] = jnp.broadcast_to(_colsum(g_ref[...]), db_ref.shape)
        odd = lax.broadcasted_iota(jnp.int32, gs.shape, 0) % 2 == 1
        gc = _colsum(jnp.where(odd, gs, 0.0))
        ds = _dot(jnp.broadcast_to(gc, (8, ncol)), w_ref[...], NT)
        c1 = c[1:2, :]
        s1 = s[1:2, :]
        dc_ref[...] = ds * (s1 * (1.0 + c1 * (1.0 - s1)))

    return pl.pallas_call(
        body, name="ada_bwd",
        out_shape=[jax.ShapeDtypeStruct(w_sh.shape, F32),
                   jax.ShapeDtypeStruct((8, g16.shape[1]), F32),
                   jax.ShapeDtypeStruct((8, D), F32)],
        compiler_params=pltpu.CompilerParams(vmem_limit_bytes=VMEM_LIMIT),
    )(c16, g16, g16_sh, w_sh)


def _ln0_fwd(x, ctx, g, b, modx, modc):
    L = x.shape[0]
    nt = L // TL

    def body(x_ref, c_ref, g_ref, b_ref, mx_ref, mc_ref, xn_ref, h_ref):
        isc = pl.program_id(0) == nt
        xin = jnp.where(isc, c_ref[...], x_ref[...])
        sh = jnp.where(isc, mc_ref[0:1, :], mx_ref[0:1, :])
        sc = jnp.where(isc, mc_ref[1:2, :], mx_ref[1:2, :])
        xhat, _ = _ln(xin)
        xn = xhat * g_ref[...] + b_ref[...]
        xn_ref[...] = xn
        h_ref[...] = (xn * (1.0 + sc) + sh).astype(h_ref.dtype)

    return pl.pallas_call(
        body, name="ln0_fwd", grid=(nt + 1,),
        in_specs=[_rtc(D, nt), _cst((TL, D)), _cst((1, D)), _cst((1, D)), _cst((8, D)), _cst((8, D))],
        out_specs=[_rt(D), _rt(D)],
        out_shape=[jax.ShapeDtypeStruct((L + TL, D), F32), jax.ShapeDtypeStruct((L + TL, D), _MXU)],
        compiler_params=_params(("parallel",)),
    )(x, ctx, g, b, modx, modc)


def _xbc_colblk(j):
    return jnp.where(j < 8, OXS // 128 + j, OB // 128 + j - 8)


def _conv_taps(p_ref, r0, first, last):
    main = p_ref[pl.ds(r0, TL), :]
    zero = jnp.zeros((8, main.shape[1]), F32)
    prev = zero if first else p_ref[pl.ds(r0 - 8, 8), :]
    nxt = zero if last else p_ref[pl.ds(r0 + TL, 8), :]
    ext = jnp.concatenate([prev, main, nxt], axis=0)
    n = TL + 16
    return [pltpu.roll(ext, (2 - k) % n, 0)[8:8 + TL] for k in range(5)]


def _seq_chunks(L):
    nt = L // TL
    return [(r * TL, r == 0, r == nt - 1) for r in range(nt)] + [(L, True, True)]


def _conv_fwd(p, conv_w8, conv_b):
    RT = p.shape[0]
    L = RT - TL
    chunks = _seq_chunks(L)

    def body(p_ref, w_ref, b_ref, o_ref):
        w = w_ref[...]
        bias = b_ref[...]
        for r0, first, last in chunks:
            taps = _conv_taps(p_ref, r0, first, last)
            pre = bias + sum(w[k:k + 1, :] * taps[k] for k in range(5))
            o_ref[pl.ds(r0, TL), :] = pre * _sig(pre)

    return pl.pallas_call(
        body, name="conv_fwd", grid=(12,),
        in_specs=[pl.BlockSpec((RT, 128), lambda j: (0, _xbc_colblk(j))),
                  pl.BlockSpec((8, 128), lambda j: (0, j)),
                  pl.BlockSpec((1, 128), lambda j: (0, j))],
        out_specs=pl.BlockSpec((RT, 128), lambda j: (0, j)),
        out_shape=jax.ShapeDtypeStruct((RT, 1536), F32),
        compiler_params=_params(("parallel",)),
    )(p, conv_w8, conv_b)


def _ssd_common(dtraw, dtb, a32, rev):
    dt = _softplus(dtraw + dtb)
    acum = _cumsum_rows(dt * a32, rev)
    ii = lax.broadcasted_iota(jnp.int32, (Q, Q), 0)
    jj = lax.broadcasted_iota(jnp.int32, (Q, Q), 1)
    mask = (ii <= jj) if rev else (ii >= jj)
    return dt, acum, acum.T, dt.T, mask


def _ssd_orders(ncl, ncc):
    nc = ncl + ncc

    def cf(s):
        return jnp.where(s < ncc, ncl + s, s - ncc)

    def cb(s):
        return nc - 1 - s

    return cf, cb


def _ssd_fwd(xbc, p, prm):
    RT = xbc.shape[0]
    nc = RT // Q
    ncc = TL // Q
    cf, cb = _ssd_orders(nc - ncc, ncc)

    def one_dir(x_ref, dt_ref, prm_ref, y_ref, hp_ref, H_ref, d):
        rev = d == 1
        a32 = -jnp.exp(prm_ref[1:2, :])
        dt, acum, acumT, dtT, mask = _ssd_common(dt_ref[...], prm_ref[0:1, :], a32, rev)
        end = 0 if rev else Q - 1
        for g in range(2):
            Bg = x_ref[:, D + g * NS:D + (g + 1) * NS]
            Cg = x_ref[:, D + 2 * NS + g * NS:D + 2 * NS + (g + 1) * NS]
            CB = _dot(Cg, Bg, NT)
            for hh in range(HPG):
                h = g * HPG + hh
                ln = 16 * d + h
                col = acum[:, ln:ln + 1]
                rowv = acumT[ln:ln + 1, :]
                a_end = rowv[:, end:end + 1]
                Lm = jnp.exp(jnp.where(mask, col - rowv, -1e30))
                W = CB * Lm * dtT[ln:ln + 1, :]
                Xh = x_ref[:, h * HP:(h + 1) * HP]
                Hp = H_ref[h * HP:(h + 1) * HP, :]
                y = _dot(W, Xh, NN) + jnp.exp(col) * _dot(Cg, Hp, NT)
                y_ref[:, h * HP:(h + 1) * HP] = y
                dcol = jnp.exp(a_end - col) * dt[:, ln:ln + 1]
                hp_ref[0, h * HP:(h + 1) * HP, :] = Hp
                H_ref[h * HP:(h + 1) * HP, :] = jnp.exp(a_end) * Hp + _dot(Xh * dcol, Bg, TN)

    def body(xf_ref, xb_ref, df_ref, db_ref, prm_ref, yf_ref, yb_ref, hf_ref, hb_ref, Hf, Hb):
        @pl.when(pl.program_id(0) == 0)
        def _():
            Hf[...] = jnp.zeros_like(Hf)
            Hb[...] = jnp.zeros_like(Hb)

        one_dir(xf_ref, df_ref, prm_ref, yf_ref, hf_ref, Hf, 0)
        one_dir(xb_ref, db_ref, prm_ref, yb_ref, hb_ref, Hb, 1)

    ysh = jax.ShapeDtypeStruct((RT, D), F32)
    hsh = jax.ShapeDtypeStruct((nc, NH * HP, NS), F32)
    hspec = pl.BlockSpec((1, NH * HP, NS), lambda s: (s, 0, 0))
    return pl.pallas_call(
        body, name="ssd_fwd", grid=(nc,),
        in_specs=[pl.BlockSpec((Q, 1536), lambda s: (cf(s), 0)),
                  pl.BlockSpec((Q, 1536), lambda s: (cb(s), 0)),
                  pl.BlockSpec((Q, 128), lambda s: (cf(s), ODT // 128)),
                  pl.BlockSpec((Q, 128), lambda s: (cb(s), ODT // 128)),
                  _cst((8, 128))],
        out_specs=[pl.BlockSpec((Q, D), lambda s: (cf(s), 0)),
                   pl.BlockSpec((Q, D), lambda s: (cb(s), 0)), hspec, hspec],
        out_shape=[ysh, ysh, hsh, hsh],
        scratch_shapes=[pltpu.VMEM((NH * HP, NS), F32), pltpu.VMEM((NH * HP, NS), F32)],
        compiler_params=_params(("arbitrary",)),
    )(xbc, xbc, p, p, prm)


def _ssd_bwd(xbc, p, prm, dsk, dyd, hpf, hpb):
    RT = xbc.shape[0]
    nc = RT // Q
    ncc = TL // Q
    ncl = nc - ncc
    cf, cb = _ssd_orders(ncl, ncc)

    def rs(t):
        return nc - 1 - t

    def one_dir(x_ref, dt_ref, prm_ref, dsk_ref, dy_ref, is_ctx, hp_ref, dH_ref,
                dx_ref, ddt_ref, st_ref, d):
        rev = d == 1
        a32 = -jnp.exp(prm_ref[1:2, :])
        dtraw = dt_ref[...]
        dtb = prm_ref[0:1, :]
        dt, acum, acumT, dtT, mask = _ssd_common(dtraw, dtb, a32, rev)
        end = 0 if rev else Q - 1
        lane = lax.broadcasted_iota(jnp.int32, (Q, 128), 1)
        srow = lax.broadcasted_iota(jnp.int32, (Q, 128), 0)
        dyscale = jnp.where(is_ctx, 0.0, 1.0)
        c_dacum = jnp.zeros((Q, 128), F32)
        r_dacum = jnp.zeros((Q, 128), F32)
        c_ddt = jnp.zeros((Q, 128), F32)
        r_ddt = jnp.zeros((Q, 128), F32)
        dskacc = jnp.zeros((1, 128), F32)
        for g in range(2):
            Bg = x_ref[:, D + g * NS:D + (g + 1) * NS]
            Cg = x_ref[:, D + 2 * NS + g * NS:D + 2 * NS + (g + 1) * NS]
            CB = _dot(Cg, Bg, NT)
            dCB = jnp.zeros((Q, Q), F32)
            dBg = jnp.zeros((Q, NS), F32)
            dCg = jnp.zeros((Q, NS), F32)
            for hh in range(HPG):
                h = g * HPG + hh
                ln = 16 * d + h
                hs = slice(h * HP, (h + 1) * HP)
                col = acum[:, ln:ln + 1]
                rowv = acumT[ln:ln + 1, :]
                dtr = dtT[ln:ln + 1, :]
                dtc = dt[:, ln:ln + 1]
                a_end = rowv[:, end:end + 1]
                Lm = jnp.exp(jnp.where(mask, col - rowv, -1e30))
                E = jnp.exp(col)
                ecol = jnp.exp(a_end - col)
                dcol = ecol * dtc
                Xh = x_ref[:, hs]
                dY = dy_ref[:, hs] * dyscale
                Hp = hp_ref[0, hs, :]
                dHn = dH_ref[hs, :]
                W = CB * Lm * dtr
                dW = _dot(dY, Xh, NT)
                Mm = dW * CB * Lm
                T = Mm * dtr
                dCB = dCB + dW * Lm * dtr
                BdH = _dot(Bg, dHn, NT)
                dX = _dot(W, dY, TN) + dcol * BdH
                if d == 0:
                    dX = dX + dY * dsk_ref[:, hs]
                    dskacc = dskacc + jnp.where(lane[0:1, :] == h, _sum11(dY * Xh), 0.0)
                dx_ref[:, hs] = dX
                xb = jnp.sum(Xh * BdH, axis=1, keepdims=True)
                scol = dcol * xb
                G = _dot(dY, Hp, NN)
                dCg = dCg + E * G
                qcol = E * jnp.sum(G * Cg, axis=1, keepdims=True)
                dBg = dBg + _dot(Xh * dcol, dHn, NN)
                dH_ref[hs, :] = jnp.exp(a_end) * dHn + _dot(dY * E, Cg, TN)
                eterm = jnp.exp(a_end) * _sum11(dHn * Hp) + _sum11(scol)
                cvec = jnp.sum(T, axis=1, keepdims=True) + qcol - scol
                cvec = cvec + jnp.where(srow[:, 0:1] == end, eterm, 0.0)
                c_dacum = c_dacum + jnp.where(lane == ln, cvec, 0.0)
                r_dacum = r_dacum - jnp.where(srow == ln, _colsum(T), 0.0)
                c_ddt = c_ddt + jnp.where(lane == ln, ecol * xb, 0.0)
                r_ddt = r_ddt + jnp.where(srow == ln, _colsum(Mm), 0.0)
            dBg = dBg + _dot(dCB, Cg, TN)
            dCg = dCg + _dot(dCB, Bg, NN)
            dx_ref[:, D + g * NS:D + (g + 1) * NS] = dBg
            dx_ref[:, D + 2 * NS + g * NS:D + 2 * NS + (g + 1) * NS] = dCg
        dacum = c_dacum + r_dacum.T
        da = _cumsum_rows(dacum, not rev)
        mine = (lane >= 16 * d) & (lane < 16 * d + 16)
        ddt = jnp.where(mine, c_ddt + r_ddt.T + da * a32, 0.0)
        ddt_ref[...] = ddt * _sig(dtraw + dtb)
        st_ref[0:1, :] += _colsum(jnp.where(mine, da * dt, 0.0))
        if d == 0:
            st_ref[1:2, :] += dskacc

    def body(xf_ref, xb_ref, df_ref, db_ref, prm_ref, dsk_ref, dyf_ref, dyb_ref, hf_ref, hb_ref,
             dxf_ref, dxb_ref, ddf_ref, ddb_ref, st_ref, dHf, dHb):
        t = pl.program_id(0)

        @pl.when(t == 0)
        def _():
            dHf[...] = jnp.zeros_like(dHf)
            dHb[...] = jnp.zeros_like(dHb)
            st_ref[...] = jnp.zeros_like(st_ref)

        s = rs(t)
        one_dir(xf_ref, df_ref, prm_ref, dsk_ref, dyf_ref, cf(s) >= ncl, hf_ref, dHf,
                dxf_ref, ddf_ref, st_ref, 0)
        one_dir(xb_ref, db_ref, prm_ref, dsk_ref, dyb_ref, cb(s) >= ncl, hb_ref, dHb,
                dxb_ref, ddb_ref, st_ref, 1)

        @pl.when(t == nc - 1)
        def _():
            st_ref[0:1, :] = -jnp.exp(prm_ref[1:2, :]) * st_ref[0:1, :]

    def lat(c):
        return jnp.minimum(c, ncl - 1)

    xsh = jax.ShapeDtypeStruct((RT, 1536), F32)
    dsh = jax.ShapeDtypeStruct((RT, 128), F32)
    hspec = pl.BlockSpec((1, NH * HP, NS), lambda t: (rs(t), 0, 0))
    return pl.pallas_call(
        body, name="ssd_bwd", grid=(nc,),
        in_specs=[pl.BlockSpec((Q, 1536), lambda t: (cf(rs(t)), 0)),
                  pl.BlockSpec((Q, 1536), lambda t: (cb(rs(t)), 0)),
                  pl.BlockSpec((Q, 128), lambda t: (cf(rs(t)), ODT // 128)),
                  pl.BlockSpec((Q, 128), lambda t: (cb(rs(t)), ODT // 128)),
                  _cst((8, 128)), _cst((1, D)),
                  pl.BlockSpec((Q, D), lambda t: (lat(cf(rs(t))), 0)),
                  pl.BlockSpec((Q, D), lambda t: (lat(cb(rs(t))), 0)),
                  hspec, hspec],
        out_specs=[pl.BlockSpec((Q, 1536), lambda t: (cf(rs(t)), 0)),
                   pl.BlockSpec((Q, 1536), lambda t: (cb(rs(t)), 0)),
                   pl.BlockSpec((Q, 128), lambda t: (cf(rs(t)), 0)),
                   pl.BlockSpec((Q, 128), lambda t: (cb(rs(t)), 0)),
                   _cst((8, 128))],
        out_shape=[xsh, xsh, dsh, dsh, jax.ShapeDtypeStruct((8, 128), F32)],
        scratch_shapes=[pltpu.VMEM((NH * HP, NS), F32), pltpu.VMEM((NH * HP, NS), F32)],
        compiler_params=_params(("arbitrary",)),
    )(xbc, xbc, p, p, prm, dsk, dyd, dyd, hpf, hpb)


def _lane_bcast(v, ln):
    return jnp.broadcast_to(v[:, ln:ln + 1], v.shape)


def _halves(v, lo, axis):
    return jnp.concatenate([jnp.where(lo, v, 0.0), jnp.where(lo, 0.0, v)], axis=axis)


def _ssd2_fwd(xbc, p, prm):
    RT = xbc.shape[0]
    nc = RT // Q
    ncc = TL // Q
    cf, cb = _ssd_orders(nc - ncc, ncc)

    def one_dir(x_ref, dt_ref, prm_ref, y_ref, hp_ref, HT_ref, d):
        rev = d == 1
        a32 = -jnp.exp(prm_ref[1:2, :])
        dt, acum, acumT, dtT, mask = _ssd_common(dt_ref[...], prm_ref[0:1, :], a32, rev)
        end = 0 if rev else Q - 1
        lo = lax.broadcasted_iota(jnp.int32, (Q, 128), 1) < HP
        for g in range(2):
            Bg = x_ref[:, D + g * NS:D + (g + 1) * NS]
            Cg = x_ref[:, D + 2 * NS + g * NS:D + 2 * NS + (g + 1) * NS]
            CB = _dot(Cg, Bg, NT)
            xds, svs = [], []
            for q in range(HPG // 2):
                pi = g * (HPG // 2) + q
                ps = slice(pi * 128, (pi + 1) * 128)
                Xp = x_ref[:, ps]
                HTp = HT_ref[:, ps]
                lhs, dcs, sv = [], [], []
                ces = []
                for h in (2 * pi, 2 * pi + 1):
                    ln = 16 * d + h
                    colB = _lane_bcast(acum, ln)
                    rowv = acumT[ln:ln + 1, :]
                    aend = colB[end:end + 1, :]
                    Lm = jnp.exp(jnp.where(mask, colB - rowv, -1e30))
                    lhs.append(CB * Lm * dtT[ln:ln + 1, :])
                    ces.append(Cg * jnp.exp(colB))
                    dcs.append(jnp.exp(aend - colB) * _lane_bcast(dt, ln))
                    sv.append(jnp.exp(aend))
                lhs = jnp.concatenate(lhs + ces, axis=1)
                rhs = jnp.concatenate([_halves(Xp, lo, 0), _halves(HTp, lo, 0)], axis=0)
                y_ref[:, ps] = _dot(lhs, rhs, NN)
                xds.append(Xp * jnp.where(lo, dcs[0], dcs[1]))
                svs.append(jnp.where(lo[0:1, :], sv[0], sv[1]))
            gs = slice(g * 512, (g + 1) * 512)
            HTg = HT_ref[:, gs]
            hp_ref[0, :, gs] = HTg
            st = _dot(Bg.T, jnp.concatenate(xds, axis=1), NN)
            HT_ref[:, gs] = jnp.concatenate(svs, axis=1) * HTg + st

    def body(xf_ref, xb_ref, df_ref, db_ref, prm_ref, yf_ref, yb_ref, hf_ref, hb_ref, Hf, Hb):
        @pl.when(pl.program_id(0) == 0)
        def _():
            Hf[...] = jnp.zeros_like(Hf)
            Hb[...] = jnp.zeros_like(Hb)

        one_dir(xf_ref, df_ref, prm_ref, yf_ref, hf_ref, Hf, 0)
        one_dir(xb_ref, db_ref, prm_ref, yb_ref, hb_ref, Hb, 1)

    ysh = jax.ShapeDtypeStruct((RT, D), F32)
    hsh = jax.ShapeDtypeStruct((nc, NS, NH * HP), F32)
    hspec = pl.BlockSpec((1, NS, NH * HP), lambda s: (s, 0, 0))
    return pl.pallas_call(
        body, name="ssd_fwd", grid=(nc,),
        in_specs=[pl.BlockSpec((Q, 1536), lambda s: (cf(s), 0)),
                  pl.BlockSpec((Q, 1536), lambda s: (cb(s), 0)),
                  pl.BlockSpec((Q, 128), lambda s: (cf(s), ODT // 128)),
                  pl.BlockSpec((Q, 128), lambda s: (cb(s), ODT // 128)),
                  _cst((8, 128))],
        out_specs=[pl.BlockSpec((Q, D), lambda s: (cf(s), 0)),
                   pl.BlockSpec((Q, D), lambda s: (cb(s), 0)), hspec, hspec],
        out_shape=[ysh, ysh, hsh, hsh],
        scratch_shapes=[pltpu.VMEM((NS, NH * HP), F32), pltpu.VMEM((NS, NH * HP), F32)],
        compiler_params=_params(("arbitrary",)),
    )(xbc, xbc, p, p, prm)


def _ssd2_bwd(xbc, p, prm, dsk, dyd, hpf, hpb):
    RT = xbc.shape[0]
    nc = RT // Q
    ncc = TL // Q
    ncl = nc - ncc
    cf, cb = _ssd_orders(ncl, ncc)

    def rs(t):
        return nc - 1 - t

    def one_dir(x_ref, dt_ref, prm_ref, dsk_ref, dy_ref, is_ctx, hp_ref, dHT_ref,
                dx_ref, ddt_ref, st_ref, d):
        rev = d == 1
        a32 = -jnp.exp(prm_ref[1:2, :])
        dtraw = dt_ref[...]
        dtb = prm_ref[0:1, :]
        dt, acum, acumT, _, _ = _ssd_common(dtraw, dtb, a32, rev)
        end = 0 if rev else Q - 1
        lane = lax.broadcasted_iota(jnp.int32, (Q, 128), 1)
        srow = lax.broadcasted_iota(jnp.int32, (Q, 128), 0)
        maskT = (lane <= srow) if rev else (lane >= srow)
        lo = lane < HP
        lo1 = lo[0:1, :]
        dyscale = jnp.where(is_ctx, 0.0, 1.0)
        c_dacum = jnp.zeros((Q, 128), F32)
        r_dacum = jnp.zeros((Q, 128), F32)
        c_ddt = jnp.zeros((Q, 128), F32)
        dskacc = jnp.zeros((1, 128), F32)
        for g in range(2):
            gs = slice(g * 512, (g + 1) * 512)
            Bg = x_ref[:, D + g * NS:D + (g + 1) * NS]
            Cg = x_ref[:, D + 2 * NS + g * NS:D + 2 * NS + (g + 1) * NS]
            CBT = _dot(Bg, Cg, NT)
            HTg = hp_ref[0, :, gs]
            dHTg = dHT_ref[:, gs]
            BdHg = _dot(Bg, dHTg, NN)
            dCBT = jnp.zeros((Q, Q), F32)
            dCg = jnp.zeros((Q, NS), F32)
            xds, dyes, svs = [], [], []
            for q in range(HPG // 2):
                pi = g * (HPG // 2) + q
                ps = slice(pi * 128, (pi + 1) * 128)
                qs = slice(q * 128, (q + 1) * 128)
                Xp = x_ref[:, ps]
                dYp = dy_ref[:, ps] * dyscale
                HTp = HTg[:, qs]
                BdHp = BdHg[:, qs]
                dY2 = _halves(dYp, lo, 0)
                dWT2 = _dot(_halves(Xp, lo, 0), dYp.T, NN)
                G2 = _dot(dY2, HTp, NT)
                XB = Xp * BdHp
                hh = _colsum(dHTg[:, qs] * HTp)
                yx = _colsum(dYp * Xp)
                wts, dcs, ebs, sv = [], [], [], []
                for k, h in enumerate((2 * pi, 2 * pi + 1)):
                    ln = 16 * d + h
                    half = lo if k == 0 else jnp.logical_not(lo)
                    half1 = half[0:1, :]
                    colB = _lane_bcast(acum, ln)
                    dtcB = _lane_bcast(dt, ln)
                    rowv = acumT[ln:ln + 1, :]
                    aend = colB[end:end + 1, :]
                    LmT = jnp.exp(jnp.where(maskT, rowv - colB, -1e30))
                    WT = CBT * LmT * dtcB
                    dWT = dWT2[k * Q:(k + 1) * Q, :]
                    U = dWT * LmT
                    MT = U * CBT
                    rM = jnp.sum(MT, axis=1, keepdims=True)
                    rT = _colsum(MT * dtcB)
                    dCBT = dCBT + U * dtcB
                    ecol = jnp.exp(aend - colB)
                    EB = jnp.exp(colB)
                    Gk = G2[k * Q:(k + 1) * Q, :]
                    dCg = dCg + EB * Gk
                    qcol = jnp.sum(EB * Gk * Cg, axis=1, keepdims=True)
                    xb = jnp.sum(jnp.where(half, XB, 0.0), axis=1, keepdims=True)
                    e1 = ecol[:, 0:1]
                    dt1 = dtcB[:, 0:1]
                    scol = e1 * dt1 * xb
                    sA = jnp.exp(aend)
                    eterm = sA[:, 0:1] * jnp.sum(jnp.where(half1, hh, 0.0), axis=1, keepdims=True) \
                        + _colsum(scol)
                    cvec = qcol - dt1 * rM - scol + jnp.where(srow[:, 0:1] == end, eterm, 0.0)
                    c_dacum = jnp.where(lane == ln, cvec, c_dacum)
                    r_dacum = jnp.where(srow == ln, rT, r_dacum)
                    c_ddt = jnp.where(lane == ln, rM + e1 * xb, c_ddt)
                    if d == 0:
                        dskacc = dskacc + jnp.where(
                            lane[0:1, :] == h, jnp.sum(jnp.where(half1, yx, 0.0), axis=1, keepdims=True), 0.0)
                    wts.append(WT)
                    dcs.append(ecol * dtcB)
                    ebs.append(EB)
                    sv.append(sA)
                dcp = jnp.where(lo, dcs[0], dcs[1])
                dX = _dot(jnp.concatenate(wts, axis=1), dY2, NN) + dcp * BdHp
                if d == 0:
                    dX = dX + dYp * dsk_ref[:, ps]
                dx_ref[:, ps] = dX
                xds.append(Xp * dcp)
                dyes.append(dYp * jnp.where(lo, ebs[0], ebs[1]))
                svs.append(jnp.where(lo1, sv[0], sv[1]))
            dx_ref[:, D + g * NS:D + (g + 1) * NS] = (
                _dot(jnp.concatenate(xds, axis=1), dHTg, NT) + _dot(dCBT, Cg, NN))
            dx_ref[:, D + 2 * NS + g * NS:D + 2 * NS + (g + 1) * NS] = dCg + _dot(dCBT, Bg, TN)
            dHT_ref[:, gs] = (jnp.concatenate(svs, axis=1) * dHTg
                              + _dot(Cg.T, jnp.concatenate(dyes, axis=1), NN))
        dacum = c_dacum + r_dacum.T
        da = _cumsum_rows(dacum, not rev)
        mine = (lane >= 16 * d) & (lane < 16 * d + 16)
        ddt = jnp.where(mine, c_ddt + da * a32, 0.0)
        ddt_ref[...] = ddt * _sig(dtraw + dtb)
        st_ref[0:1, :] += _colsum(jnp.where(mine, da * dt, 0.0))
        if d == 0:
            st_ref[1:2, :] += dskacc

    def body(xf_ref, xb_ref, df_ref, db_ref, prm_ref, dsk_ref, dyf_ref, dyb_ref, hf_ref, hb_ref,
             dxf_ref, dxb_ref, ddf_ref, ddb_ref, st_ref, dHf, dHb):
        t = pl.program_id(0)

        @pl.when(t == 0)
        def _():
            dHf[...] = jnp.zeros_like(dHf)
            dHb[...] = jnp.zeros_like(dHb)
            st_ref[...] = jnp.zeros_like(st_ref)

        s = rs(t)
        one_dir(xf_ref, df_ref, prm_ref, dsk_ref, dyf_ref, cf(s) >= ncl, hf_ref, dHf,
                dxf_ref, ddf_ref, st_ref, 0)
        one_dir(xb_ref, db_ref, prm_ref, dsk_ref, dyb_ref, cb(s) >= ncl, hb_ref, dHb,
                dxb_ref, ddb_ref, st_ref, 1)

        @pl.when(t == nc - 1)
        def _():
            st_ref[0:1, :] = -jnp.exp(prm_ref[1:2, :]) * st_ref[0:1, :]

    def lat(c):
        return jnp.minimum(c, ncl - 1)

    xsh = jax.ShapeDtypeStruct((RT, 1536), F32)
    dsh = jax.ShapeDtypeStruct((RT, 128), F32)
    hspec = pl.BlockSpec((1, NS, NH * HP), lambda t: (rs(t), 0, 0))
    return pl.pallas_call(
        body, name="ssd_bwd", grid=(nc,),
        in_specs=[pl.BlockSpec((Q, 1536), lambda t: (cf(rs(t)), 0)),
                  pl.BlockSpec((Q, 1536), lambda t: (cb(rs(t)), 0)),
                  pl.BlockSpec((Q, 128), lambda t: (cf(rs(t)), ODT // 128)),
                  pl.BlockSpec((Q, 128), lambda t: (cb(rs(t)), ODT // 128)),
                  _cst((8, 128)), _cst((1, D)),
                  pl.BlockSpec((Q, D), lambda t: (lat(cf(rs(t))), 0)),
                  pl.BlockSpec((Q, D), lambda t: (lat(cb(rs(t))), 0)),
                  hspec, hspec],
        out_specs=[pl.BlockSpec((Q, 1536), lambda t: (cf(rs(t)), 0)),
                   pl.BlockSpec((Q, 1536), lambda t: (cb(rs(t)), 0)),
                   pl.BlockSpec((Q, 128), lambda t: (cf(rs(t)), 0)),
                   pl.BlockSpec((Q, 128), lambda t: (cb(rs(t)), 0)),
                   _cst((8, 128))],
        out_shape=[xsh, xsh, dsh, dsh, jax.ShapeDtypeStruct((8, 128), F32)],
        scratch_shapes=[pltpu.VMEM((NS, NH * HP), F32), pltpu.VMEM((NS, NH * HP), F32)],
        compiler_params=_params(("arbitrary",)),
    )(xbc, xbc, p, p, prm, dsk, dyd, dyd, hpf, hpb)


RB = 32


def _ssd3_bwd(xbc, p, prm, dsk, dyd, hpf, hpb):
    RT = xbc.shape[0]
    nc = RT // Q
    ncc = TL // Q
    ncl = nc - ncc
    cf, cb = _ssd_orders(ncl, ncc)
    npair = HPG // 2

    def rs(t):
        return nc - 1 - t

    def one_dir(x_ref, dt_ref, prm_ref, dsk_ref, dy_ref, is_ctx, hp_ref, dHT_ref,
                dx_ref, ddt_ref, st_ref, s_dwt, s_g, s_wt, s_xd, s_dye, s_dcbt, s_dcg, s_cd, s_cdt, d):
        rev = d == 1
        a32 = -jnp.exp(prm_ref[1:2, :])
        dtraw = dt_ref[...]
        dtb = prm_ref[0:1, :]
        dt, acum, acumT, _, _ = _ssd_common(dtraw, dtb, a32, rev)
        end = 0 if rev else Q - 1
        lane = lax.broadcasted_iota(jnp.int32, (RB, 128), 1)
        srow0 = lax.broadcasted_iota(jnp.int32, (RB, 128), 0)
        lo = lane < HP
        lo1 = lo[0:1, :]
        lane1 = lane[0:1, :]
        dyscale = jnp.where(is_ctx, 0.0, 1.0)
        aend_row = acum[end:end + 1, :]
        s_cd[...] = jnp.zeros_like(s_cd)
        s_cdt[...] = jnp.zeros_like(s_cdt)
        r_rows = jnp.zeros((Q, 128), F32)
        srowQ = lax.broadcasted_iota(jnp.int32, (Q, 128), 0)
        dskacc = jnp.zeros((1, 128), F32)
        for g in range(2):
            gs = slice(g * 512, (g + 1) * 512)
            Bg = x_ref[:, D + g * NS:D + (g + 1) * NS]
            Cg = x_ref[:, D + 2 * NS + g * NS:D + 2 * NS + (g + 1) * NS]
            CBT = _dot(Bg, Cg, NT)
            HTg = hp_ref[0, :, gs]
            dHTg = dHT_ref[:, gs]
            BdHg = _dot(Bg, dHTg, NN)
            hhs, yxs = [], []
            for q in range(npair):
                pi = g * npair + q
                ps = slice(pi * 128, (pi + 1) * 128)
                qs = slice(q * 128, (q + 1) * 128)
                Xp = x_ref[:, ps]
                dYp = dy_ref[:, ps] * dyscale
                s_dwt[q] = _dot(_halves(Xp, lo_full(), 0), dYp.T, NN)
                s_g[q] = _dot(_halves(dYp, lo_full(), 0), HTg[:, qs], NT)
                hhs.append(_colsum(dHTg[:, qs] * HTg[:, qs]))
                yxs.append(_colsum(dYp * Xp))
            rparts = [jnp.zeros((8, 128), F32) for _ in range(HPG)]
            ssum = [jnp.zeros((1, 1), F32) for _ in range(HPG)]
            for rb in range(Q // RB):
                r0 = rb * RB
                rows = slice(r0, r0 + RB)
                srow = srow0 + r0
                maskT = (lane <= srow) if rev else (lane >= srow)
                acum_rb = acum[rows, :]
                dt_rb = dt[rows, :]
                CBT_rb = CBT[rows, :]
                Cg_rb = Cg[rows, :]
                dcbt = jnp.zeros((RB, Q), F32)
                dcg = jnp.zeros((RB, NS), F32)
                cd = s_cd[rows, :]
                cdt = s_cdt[rows, :]
                for q in range(npair):
                    pi = g * npair + q
                    ps = slice(pi * 128, (pi + 1) * 128)
                    qs = slice(q * 128, (q + 1) * 128)
                    Xp = x_ref[rows, ps]
                    dYp = dy_ref[rows, ps] * dyscale
                    BdHp = BdHg[rows, qs]
                    XB = Xp * BdHp
                    dcs, ebs = [], []
                    for k in range(2):
                        hh = 2 * q + k
                        ln = 16 * d + g * HPG + hh
                        half = lo if k == 0 else jnp.logical_not(lo)
                        colB = _lane_bcast(acum_rb, ln)
                        dtcB = _lane_bcast(dt_rb, ln)
                        rowv = acumT[ln:ln + 1, :]
                        aend = _lane_bcast(aend_row, ln)
                        LmT = jnp.exp(jnp.where(maskT, rowv - colB, -1e30))
                        s_wt[q, rows, k * Q:(k + 1) * Q] = (CBT_rb * LmT * dtcB).astype(s_wt.dtype)
                        U = s_dwt[q, k * Q + r0:k * Q + r0 + RB, :] * LmT
                        MT = U * CBT_rb
                        rM = jnp.sum(MT, axis=1, keepdims=True)
                        TT = MT * dtcB
                        rparts[hh] = rparts[hh] + (TT[0:8] + TT[8:16] + TT[16:24] + TT[24:32])
                        dcbt = dcbt + U * dtcB
                        ecol = jnp.exp(aend - colB)
                        EB = jnp.exp(colB)
                        EG = EB * s_g[q, k * Q + r0:k * Q + r0 + RB, :]
                        dcg = dcg + EG
                        qcol = jnp.sum(EG * Cg_rb, axis=1, keepdims=True)
                        xb = jnp.sum(jnp.where(half, XB, 0.0), axis=1, keepdims=True)
                        e1 = ecol[:, 0:1]
                        dt1 = dtcB[:, 0:1]
                        scol = e1 * dt1 * xb
                        ssum[hh] = ssum[hh] + _colsum(scol)
                        cd = jnp.where(lane == ln, qcol - dt1 * rM - scol, cd)
                        cdt = jnp.where(lane == ln, rM + e1 * xb, cdt)
                        dcs.append(ecol * dtcB)
                        ebs.append(EB)
                    dcp = jnp.where(lo, dcs[0], dcs[1])
                    dxo = dcp * BdHp
                    if d == 0:
                        dxo = dxo + dYp * dsk_ref[:, ps]
                    dx_ref[rows, ps] = dxo
                    s_xd[rows, qs] = (Xp * dcp).astype(s_xd.dtype)
                    s_dye[rows, qs] = (dYp * jnp.where(lo, ebs[0], ebs[1])).astype(s_dye.dtype)
                s_dcbt[rows, :] = dcbt
                s_dcg[rows, :] = dcg
                s_cd[rows, :] = cd
                s_cdt[rows, :] = cdt
            erow = jnp.zeros((1, 128), F32)
            svs = []
            for q in range(npair):
                pi = g * npair + q
                ps = slice(pi * 128, (pi + 1) * 128)
                sv = []
                for k in range(2):
                    hh = 2 * q + k
                    h = g * HPG + hh
                    ln = 16 * d + h
                    half1 = lo1 if k == 0 else jnp.logical_not(lo1)
                    sA = jnp.exp(_lane_bcast(aend_row, ln))
                    hsum = jnp.sum(jnp.where(half1, hhs[q], 0.0), axis=1, keepdims=True)
                    erow = erow + jnp.where(lane1 == ln, sA[:, 0:1] * hsum + ssum[hh], 0.0)
                    rp = rparts[hh]
                    r_rows = jnp.where(srowQ == ln, _colsum(rp), r_rows)
                    if d == 0:
                        dskacc = dskacc + jnp.where(
                            lane1 == h, jnp.sum(jnp.where(half1, yxs[q], 0.0), axis=1, keepdims=True), 0.0)
                    sv.append(sA)
                svs.append(jnp.where(lo1, sv[0], sv[1]))
                dY2 = _halves(dy_ref[:, ps] * dyscale, lo_full(), 0)
                dx_ref[:, ps] += _dot(s_wt[q], dY2, NN)
            s_cd[end:end + 1, :] += erow
            dcbt_g = s_dcbt[...]
            dx_ref[:, D + g * NS:D + (g + 1) * NS] = _dot(s_xd[...], dHTg, NT) + _dot(dcbt_g, Cg, NN)
            dx_ref[:, D + 2 * NS + g * NS:D + 2 * NS + (g + 1) * NS] = s_dcg[...] + _dot(dcbt_g, Bg, TN)
            dHT_ref[:, gs] = jnp.concatenate(svs, axis=1) * dHTg + _dot(Cg.T, s_dye[...], NN)
        dacum = s_cd[...] + r_rows.T
        da = _cumsum_rows(dacum, not rev)
        laneQ = lax.broadcasted_iota(jnp.int32, (Q, 128), 1)
        mine = (laneQ >= 16 * d) & (laneQ < 16 * d + 16)
        ddt = jnp.where(mine, s_cdt[...] + da * a32, 0.0)
        ddt_ref[...] = ddt * _sig(dtraw + dtb)
        st_ref[0:1, :] += _colsum(jnp.where(mine, da * dt, 0.0))
        if d == 0:
            st_ref[1:2, :] += dskacc

    def lo_full():
        return lax.broadcasted_iota(jnp.int32, (Q, 128), 1) < HP

    def body(xf_ref, xb_ref, df_ref, db_ref, prm_ref, dsk_ref, dyf_ref, dyb_ref, hf_ref, hb_ref,
             dxf_ref, dxb_ref, ddf_ref, ddb_ref, st_ref, dHf, dHb, *scr):
        t = pl.program_id(0)

        @pl.when(t == 0)
        def _():
            dHf[...] = jnp.zeros_like(dHf)
            dHb[...] = jnp.zeros_like(dHb)
            st_ref[...] = jnp.zeros_like(st_ref)

        s = rs(t)
        one_dir(xf_ref, df_ref, prm_ref, dsk_ref, dyf_ref, cf(s) >= ncl, hf_ref, dHf,
                dxf_ref, ddf_ref, st_ref, *scr, 0)
        one_dir(xb_ref, db_ref, prm_ref, dsk_ref, dyb_ref, cb(s) >= ncl, hb_ref, dHb,
                dxb_ref, ddb_ref, st_ref, *scr, 1)

        @pl.when(t == nc - 1)
        def _():
            st_ref[0:1, :] = -jnp.exp(prm_ref[1:2, :]) * st_ref[0:1, :]

    def lat(c):
        return jnp.minimum(c, ncl - 1)

    xsh = jax.ShapeDtypeStruct((RT, 1536), F32)
    dsh = jax.ShapeDtypeStruct((RT, 128), F32)
    hspec = pl.BlockSpec((1, NS, NH * HP), lambda t: (rs(t), 0, 0))
    return pl.pallas_call(
        body, name="ssd_bwd", grid=(nc,),
        in_specs=[pl.BlockSpec((Q, 1536), lambda t: (cf(rs(t)), 0)),
                  pl.BlockSpec((Q, 1536), lambda t: (cb(rs(t)), 0)),
                  pl.BlockSpec((Q, 128), lambda t: (cf(rs(t)), ODT // 128)),
                  pl.BlockSpec((Q, 128), lambda t: (cb(rs(t)), ODT // 128)),
                  _cst((8, 128)), _cst((1, D)),
                  pl.BlockSpec((Q, D), lambda t: (lat(cf(rs(t))), 0)),
                  pl.BlockSpec((Q, D), lambda t: (lat(cb(rs(t))), 0)),
                  hspec, hspec],
        out_specs=[pl.BlockSpec((Q, 1536), lambda t: (cf(rs(t)), 0)),
                   pl.BlockSpec((Q, 1536), lambda t: (cb(rs(t)), 0)),
                   pl.BlockSpec((Q, 128), lambda t: (cf(rs(t)), 0)),
                   pl.BlockSpec((Q, 128), lambda t: (cb(rs(t)), 0)),
                   _cst((8, 128))],
        out_shape=[xsh, xsh, dsh, dsh, jax.ShapeDtypeStruct((8, 128), F32)],
        scratch_shapes=[pltpu.VMEM((NS, NH * HP), F32), pltpu.VMEM((NS, NH * HP), F32),
                        pltpu.VMEM((npair, 2 * Q, Q), F32), pltpu.VMEM((npair, 2 * Q, NS), F32),
                        pltpu.VMEM((npair, Q, 2 * Q), _MXU), pltpu.VMEM((Q, 512), _MXU),
                        pltpu.VMEM((Q, 512), _MXU), pltpu.VMEM((Q, Q), F32), pltpu.VMEM((Q, NS), F32),
                        pltpu.VMEM((Q, 128), F32), pltpu.VMEM((Q, 128), F32)],
        compiler_params=_params(("arbitrary",)),
    )(xbc, xbc, p, p, prm, dsk, dyd, dyd, hpf, hpb)


def _mix_fwd_vals(yf, yb, z, xs, u, v, dsk, sg, gg, gb):
    y = yf + yb + xs * dsk
    sz = _sig(z)
    hh = y * z * sz
    r = lax.rsqrt(jnp.mean(hh * hh, axis=-1, keepdims=True) + EPS)
    nh = hh * r
    ug, tu = _gelu(u)
    vg, tv = _gelu(v)
    vhat, vrstd = _ln(vg)
    vn = vhat * gg + gb
    return y, sz, r, nh, ug, tu, vg, tv, vhat, vrstd, vn


def _mix_fwd(yf, yb, p, xbc, dsk, sg, gg, gb, ws, bsT):
    L = yf.shape[0] - TL
    nt = L // TL

    def body(yf_ref, yb_ref, z_ref, xs_ref, u_ref, v_ref, dsk_ref, sg_ref, gg_ref, gb_ref,
             ws_ref, bs_ref, ys_ref, ym_ref):
        _, _, _, nh, ug, _, _, _, _, _, vn = _mix_fwd_vals(
            yf_ref[...], yb_ref[...], z_ref[...], xs_ref[...], u_ref[...], v_ref[...],
            dsk_ref[...], sg_ref[...], gg_ref[...], gb_ref[...])
        ys_ref[...] = (nh * sg_ref[...]).astype(ys_ref.dtype)
        for n in range(TL // Q):
            rs_ = slice(n * Q, (n + 1) * Q)
            for g in range(8):
                cs = slice(g * 128, (g + 1) * 128)
                mixed = _dot(ws_ref[g], vn[rs_, cs], NN) + bs_ref[:, g:g + 1]
                ym_ref[rs_, cs] = (ug[rs_, cs] * mixed).astype(ym_ref.dtype)

    return pl.pallas_call(
        body, name="mix_fwd", grid=(nt,),
        in_specs=[_rt(D), _rt(D), _rt(D, OZ // D), _rt(D, 0), _rt(D, OU // D), _rt(D, OV // D),
                  _cst((1, D)), _cst((1, D)), _cst((1, D)), _cst((1, D)),
                  _cst((8, 128, 128)), _cst((128, 128))],
        out_specs=[_rt(D), _rt(D)],
        out_shape=[jax.ShapeDtypeStruct((L, D), _MXU), jax.ShapeDtypeStruct((L, D), _MXU)],
        compiler_params=_params(("parallel",)),
    )(yf, yb, p, xbc, p, p, dsk, sg, gg, gb, ws, bsT)


def _mix_bwd(dys, dym, yf, yb, p, xbc, dp, dsk, sg, gg, gb, ws, bsT):
    L = dys.shape[0]
    nt = L // TL

    def body(dys_ref, dym_ref, yf_ref, yb_ref, z_ref, xs_ref, u_ref, v_ref, dsk_ref, sg_ref,
             gg_ref, gb_ref, ws_ref, bs_ref, dp_any, dzuv_ref, dy_ref, st_ref,
             dws_ref, dbs_ref, dvn_s):
        del dp_any
        dz_ref = dzuv_ref.at[:, OZ:OZ + D]
        du_ref = dzuv_ref.at[:, OU:OU + D]
        dv_ref = dzuv_ref.at[:, OV:OV + D]

        @pl.when(pl.program_id(0) == 0)
        def _():
            st_ref[...] = jnp.zeros_like(st_ref)
            dws_ref[...] = jnp.zeros_like(dws_ref)
            dbs_ref[...] = jnp.zeros_like(dbs_ref)

        z = z_ref[...]
        u = u_ref[...]
        v = v_ref[...]
        y, sz, r, nh, ug, tu, vg, tv, vhat, vrstd, vn = _mix_fwd_vals(
            yf_ref[...], yb_ref[...], z, xs_ref[...], u, v,
            dsk_ref[...], sg_ref[...], gg_ref[...], gb_ref[...])
        dys = dys_ref[...]
        st_ref[0:1, :] += _colsum(dys * nh)
        dn = dys * sg_ref[...]
        dhh = r * (dn - nh * jnp.mean(dn * nh, axis=-1, keepdims=True))
        dy_ref[...] = dhh * z * sz
        dz_ref[...] = (dhh * y * (sz * (1.0 + z * (1.0 - sz)))).astype(dz_ref.dtype)
        dym = dym_ref[...]
        lane = lax.broadcasted_iota(jnp.int32, (Q, 128), 1)
        dbs = jnp.zeros((Q, 128), F32)
        gu = _gelu_grad(u, tu)
        for n in range(TL // Q):
            rs_ = slice(n * Q, (n + 1) * Q)
            for g in range(8):
                cs = slice(g * 128, (g + 1) * 128)
                vb = vn[rs_, cs]
                mixed = _dot(ws_ref[g], vb, NN) + bs_ref[:, g:g + 1]
                dyb = dym[rs_, cs]
                dmx = dyb * ug[rs_, cs]
                du_ref[rs_, cs] = (dyb * mixed * gu[rs_, cs]).astype(du_ref.dtype)
                dvn_s[rs_, cs] = _dot(ws_ref[g], dmx, TN)
                dws_ref[g] += _dot(dmx, vb, NT)
                dbs = dbs + jnp.where(lane == g, jnp.sum(dmx, axis=1, keepdims=True), 0.0)
        dbs_ref[...] += dbs
        dvn = dvn_s[...]
        st_ref[1:2, :] += _colsum(dvn * vhat)
        st_ref[2:3, :] += _colsum(dvn)
        dvg = _ln_bwd(dvn * gg_ref[...], vhat, vrstd)
        dv_ref[...] = (dvg * _gelu_grad(v, tv)).astype(dv_ref.dtype)

    outs = pl.pallas_call(
        body, name="mix_bwd", grid=(nt,),
        in_specs=[_rt(D), _rt(D), _rt(D), _rt(D), _rt(D, OZ // D), _rt(D, 0), _rt(D, OU // D),
                  _rt(D, OV // D), _cst((1, D)), _cst((1, D)), _cst((1, D)), _cst((1, D)),
                  _cst((8, 128, 128)), _cst((128, 128)), pl.BlockSpec(memory_space=pl.ANY)],
        out_specs=[_rt(3 * D, 0), _rt(D), _cst((8, D)),
                   _cst((8, 128, 128)), _cst((128, 128))],
        out_shape=[jax.ShapeDtypeStruct(dp.shape, dp.dtype),
                   jax.ShapeDtypeStruct((L, D), F32), jax.ShapeDtypeStruct((8, D), F32),
                   jax.ShapeDtypeStruct((8, 128, 128), F32), jax.ShapeDtypeStruct((128, 128), F32)],
        scratch_shapes=[pltpu.VMEM((TL, D), F32)],
        input_output_aliases={14: 0},
        compiler_params=_params(("arbitrary",)),
    )(dys, dym, yf, yb, p, xbc, p, p, dsk, sg, gg, gb, ws, bsT, dp)
    return outs


def _gate_fwd(a1, a2, p, bg):
    L = a1.shape[0]

    def body(a1_ref, a2_ref, g_ref, bg_ref, m_ref):
        gt = _sig(g_ref[...] + bg_ref[...])
        m_ref[...] = (gt[:, :D] * a1_ref[...] + gt[:, D:] * a2_ref[...]).astype(m_ref.dtype)

    return pl.pallas_call(
        body, name="gate_fwd", grid=(L // TL,),
        in_specs=[_rt(D), _rt(D), _rt(2 * D, OG // (2 * D)), _cst((1, 2 * D))],
        out_specs=_rt(D), out_shape=jax.ShapeDtypeStruct((L, D), _MXU),
        compiler_params=_params(("parallel",)),
    )(a1, a2, p, bg)


def _gate_bwd(dmg, a1, a2, p, bg, dp):
    L = a1.shape[0]

    def body(dm_ref, a1_ref, a2_ref, g_ref, bg_ref, dp_any, dg_ref, da1_ref, da2_ref, st_ref):
        del dp_any

        @pl.when(pl.program_id(0) == 0)
        def _():
            st_ref[...] = jnp.zeros_like(st_ref)

        gt = _sig(g_ref[...] + bg_ref[...])
        g1 = gt[:, :D]
        g2 = gt[:, D:]
        dm = dm_ref[...]
        da1_ref[...] = (dm * g1).astype(da1_ref.dtype)
        da2_ref[...] = (dm * g2).astype(da2_ref.dtype)
        dg1 = dm * a1_ref[...] * g1 * (1.0 - g1)
        dg2 = dm * a2_ref[...] * g2 * (1.0 - g2)
        st_ref[0:1, 0:D] += _colsum(dg1)
        st_ref[0:1, D:2 * D] += _colsum(dg2)
        dg_ref[:, 0:D] = dg1.astype(dg_ref.dtype)
        dg_ref[:, D:2 * D] = dg2.astype(dg_ref.dtype)

    return pl.pallas_call(
        body, name="gate_bwd", grid=(L // TL,),
        in_specs=[_rt(D), _rt(D), _rt(D), _rt(2 * D, OG // (2 * D)), _cst((1, 2 * D)),
                  pl.BlockSpec(memory_space=pl.ANY)],
        out_specs=[_rt(2 * D, OG // (2 * D)), _rt(D), _rt(D), _cst((8, 2 * D))],
        out_shape=[jax.ShapeDtypeStruct(dp.shape, dp.dtype), jax.ShapeDtypeStruct((L, D), _MXU),
                   jax.ShapeDtypeStruct((L, D), _MXU), jax.ShapeDtypeStruct((8, 2 * D), F32)],
        input_output_aliases={5: 0},
        compiler_params=_params(("arbitrary",)),
    )(dmg, a1, a2, p, bg, dp)


def _res1_fwd(xn, out, modx, g, b):
    L = out.shape[0]

    def body(xn_ref, o_ref, mx_ref, g_ref, b_ref, r1_ref, h2_ref):
        r1 = ALPHA * xn_ref[...] + mx_ref[2:3, :] * o_ref[...]
        xhat, _ = _ln(r1)
        x1 = xhat * g_ref[...] + b_ref[...]
        r1_ref[...] = r1
        h2_ref[...] = (x1 * (1.0 + mx_ref[4:5, :]) + mx_ref[3:4, :]).astype(h2_ref.dtype)

    return pl.pallas_call(
        body, name="res1_fwd", grid=(L // TL,),
        in_specs=[_rt(D), _rt(D), _cst((8, D)), _cst((1, D)), _cst((1, D))],
        out_specs=[_rt(D), _rt(D)],
        out_shape=[jax.ShapeDtypeStruct((L, D), F32), jax.ShapeDtypeStruct((L, D), _MXU)],
        compiler_params=_params(("parallel",)),
    )(xn, out, modx, g, b)


HFF = DFF // 2


def _mm_f13_glu(h2, w13i):
    L = h2.shape[0]
    tm = 512

    def body(a_ref, b_ref, f_ref, ff_ref):
        f = _dot(a_ref[...], b_ref[...], NT)
        f_ref[...] = f
        f1 = f[:, :HFF]
        ff_ref[...] = (f1 * _sig(f1) * f[:, HFF:]).astype(ff_ref.dtype)

    return pl.pallas_call(
        body, name="mm_f13_glu", grid=(DFF // HFF, L // tm),
        in_specs=[pl.BlockSpec((tm, D), lambda j, i: (i, 0)), pl.BlockSpec((2 * HFF, D), lambda j, i: (j, 0))],
        out_specs=[pl.BlockSpec((tm, 2 * HFF), lambda j, i: (i, j)), pl.BlockSpec((tm, HFF), lambda j, i: (i, j))],
        out_shape=[jax.ShapeDtypeStruct((L, 2 * DFF), F32), jax.ShapeDtypeStruct((L, DFF), _MXU)],
        compiler_params=_params(("parallel", "parallel")),
    )(h2, w13i)


def _mm_dff_glu(do2, w_ff2_f, f13i):
    L = do2.shape[0]
    tm = 512

    def body(a_ref, b_ref, f_ref, o_ref):
        d = _dot(a_ref[...], b_ref[...], NT)
        f1 = f_ref[:, :HFF]
        s = _sig(f1)
        o_ref[:, :HFF] = (d * f_ref[:, HFF:] * (s * (1.0 + f1 * (1.0 - s)))).astype(o_ref.dtype)
        o_ref[:, HFF:] = (d * f1 * s).astype(o_ref.dtype)

    return pl.pallas_call(
        body, name="mm_dff_glu", grid=(DFF // HFF, L // tm),
        in_specs=[pl.BlockSpec((tm, D), lambda j, i: (i, 0)), pl.BlockSpec((HFF, D), lambda j, i: (j, 0)),
                  pl.BlockSpec((tm, 2 * HFF), lambda j, i: (i, j))],
        out_specs=pl.BlockSpec((tm, 2 * HFF), lambda j, i: (i, j)),
        out_shape=jax.ShapeDtypeStruct((L, 2 * DFF), _MXU),
        compiler_params=_params(("parallel", "parallel")),
    )(do2, w_ff2_f, f13i)


def _glu_fwd(f13):
    L = f13.shape[0]

    def body(f1_ref, f3_ref, o_ref):
        f1 = f1_ref[...]
        o_ref[...] = (f1 * _sig(f1) * f3_ref[...]).astype(o_ref.dtype)

    return pl.pallas_call(
        body, name="glu_fwd", grid=(L // TL,),
        in_specs=[_rt(DFF, 0), _rt(DFF, 1)], out_specs=_rt(DFF),
        out_shape=jax.ShapeDtypeStruct((L, DFF), _MXU),
        compiler_params=_params(("parallel",)),
    )(f13, f13)


def _glu_bwd(dff, f13):
    L = f13.shape[0]

    def body(d_ref, f1_ref, f3_ref, o_ref):
        f1 = f1_ref[...]
        s = _sig(f1)
        d = d_ref[...]
        o_ref[:, 0:DFF] = (d * f3_ref[...] * (s * (1.0 + f1 * (1.0 - s)))).astype(o_ref.dtype)
        o_ref[:, DFF:2 * DFF] = (d * f1 * s).astype(o_ref.dtype)

    return pl.pallas_call(
        body, name="glu_bwd", grid=(L // TL,),
        in_specs=[_rt(DFF), _rt(DFF, 0), _rt(DFF, 1)], out_specs=_rt(2 * DFF),
        out_shape=jax.ShapeDtypeStruct((L, 2 * DFF), _MXU),
        compiler_params=_params(("parallel",)),
    )(dff, f13, f13)


def _res2(r1, o2, tgt, modx, g1, b1, g2, b2):
    L = r1.shape[0]

    def body(r1_ref, o2_ref, t_ref, mx_ref, g1_ref, b1_ref, g2_ref, b2_ref,
             dr2_ref, do2_ref, st_ref, loss_ref):
        @pl.when(pl.program_id(0) == 0)
        def _():
            st_ref[...] = jnp.zeros_like(st_ref)
            loss_ref[...] = jnp.zeros_like(loss_ref)

        xh1, _ = _ln(r1_ref[...])
        x1 = xh1 * g1_ref[...] + b1_ref[...]
        o2 = o2_ref[...]
        g2x = mx_ref[5:6, :]
        xh2, rstd2 = _ln(ALPHA * x1 + g2x * o2)
        err = xh2 * g2_ref[...] + b2_ref[...] - t_ref[...]
        per_tok = jnp.mean(err * err, axis=-1, keepdims=True)
        loss_ref[...] += 0.5 * jnp.sum(per_tok, axis=0, keepdims=True)
        dy = err * (1.0 / D)
        st_ref[0:1, :] += _colsum(dy * xh2)
        st_ref[1:2, :] += _colsum(dy)
        dr2 = _ln_bwd(dy * g2_ref[...], xh2, rstd2)
        st_ref[2:3, :] += _colsum(dr2 * o2)
        dr2_ref[...] = dr2
        do2_ref[...] = (g2x * dr2).astype(do2_ref.dtype)

    return pl.pallas_call(
        body, name="res2", grid=(L // TL,),
        in_specs=[_rt(D), _rt(D), _rt(D), _cst((8, D))] + [_cst((1, D))] * 4,
        out_specs=[_rt(D), _rt(D), _cst((8, D)), _cst((8, 128))],
        out_shape=[jax.ShapeDtypeStruct((L, D), F32), jax.ShapeDtypeStruct((L, D), _MXU),
                   jax.ShapeDtypeStruct((8, D), F32), jax.ShapeDtypeStruct((8, 128), F32)],
        compiler_params=_params(("arbitrary",)),
    )(r1, o2, tgt, modx, g1, b1, g2, b2)


def _res1_bwd(dr2, dh2, r1, out, modx, g1, b1):
    L = r1.shape[0]

    def body(dr2_ref, dh2_ref, r1_ref, o_ref, mx_ref, g_ref, b_ref, dr1_ref, do_ref, st_ref):
        @pl.when(pl.program_id(0) == 0)
        def _():
            st_ref[...] = jnp.zeros_like(st_ref)

        xh1, rstd1 = _ln(r1_ref[...])
        x1 = xh1 * g_ref[...] + b_ref[...]
        dh2 = dh2_ref[...]
        dx1 = ALPHA * dr2_ref[...] + dh2 * (1.0 + mx_ref[4:5, :])
        st_ref[0:1, :] += _colsum(dh2 * x1)
        st_ref[1:2, :] += _colsum(dh2)
        st_ref[2:3, :] += _colsum(dx1 * xh1)
        st_ref[3:4, :] += _colsum(dx1)
        dr1 = _ln_bwd(dx1 * g_ref[...], xh1, rstd1)
        st_ref[4:5, :] += _colsum(dr1 * o_ref[...])
        dr1_ref[...] = dr1
        do_ref[...] = (mx_ref[2:3, :] * dr1).astype(do_ref.dtype)

    return pl.pallas_call(
        body, name="res1_bwd", grid=(L // TL,),
        in_specs=[_rt(D), _rt(D), _rt(D), _rt(D), _cst((8, D)), _cst((1, D)), _cst((1, D))],
        out_specs=[_rt(D), _rt(D), _cst((8, D))],
        out_shape=[jax.ShapeDtypeStruct((L, D), F32), jax.ShapeDtypeStruct((L, D), _MXU),
                   jax.ShapeDtypeStruct((8, D), F32)],
        compiler_params=_params(("arbitrary",)),
    )(dr2, dh2, r1, out, modx, g1, b1)


def _conv_bwd(dxf, dxb, p, conv_w8, conv_b, dp):
    RT = p.shape[0]
    chunks = _seq_chunks(RT - TL)

    def body(df_ref, db_ref, p_ref, w_ref, b_ref, dp_any, o_ref, dw_ref, dbias_ref, dpre_s):
        del dp_any
        w = w_ref[...]
        bias = b_ref[...]
        srow = lax.broadcasted_iota(jnp.int32, (8, 128), 0)
        dwacc = jnp.zeros((8, 128), F32)
        dbacc = jnp.zeros((1, 128), F32)
        for r0, first, last in chunks:
            taps = _conv_taps(p_ref, r0, first, last)
            pre = bias + sum(w[k:k + 1, :] * taps[k] for k in range(5))
            s = _sig(pre)
            dpre = (df_ref[pl.ds(r0, TL), :] + db_ref[pl.ds(r0, TL), :]) * (s * (1.0 + pre * (1.0 - s)))
            dpre_s[pl.ds(r0, TL), :] = dpre
            dbacc = dbacc + _colsum(dpre)
            for k in range(5):
                dwacc = dwacc + jnp.where(srow == k, _colsum(dpre * taps[k]), 0.0)
        for r0, first, last in chunks:
            taps = _conv_taps(dpre_s, r0, first, last)
            dx = sum(w[k:k + 1, :] * taps[4 - k] for k in range(5))
            o_ref[pl.ds(r0, TL), :] = dx.astype(o_ref.dtype)
        dw_ref[...] = dwacc
        dbias_ref[...] = jnp.broadcast_to(dbacc, (8, 128))

    cspec = pl.BlockSpec((RT, 128), lambda j: (0, j))
    wspec = pl.BlockSpec((8, 128), lambda j: (0, j))
    return pl.pallas_call(
        body, name="conv_bwd", grid=(12,),
        in_specs=[cspec, cspec, pl.BlockSpec((RT, 128), lambda j: (0, _xbc_colblk(j))),
                  wspec, pl.BlockSpec((1, 128), lambda j: (0, j)), pl.BlockSpec(memory_space=pl.ANY)],
        out_specs=[pl.BlockSpec((RT, 128), lambda j: (0, _xbc_colblk(j))), wspec, wspec],
        out_shape=[jax.ShapeDtypeStruct(dp.shape, dp.dtype), jax.ShapeDtypeStruct((8, 1536), F32),
                   jax.ShapeDtypeStruct((8, 1536), F32)],
        scratch_shapes=[pltpu.VMEM((RT, 128), F32)],
        input_output_aliases={5: 0},
        compiler_params=_params(("parallel",)),
    )(dxf, dxb, p, conv_w8, conv_b, dp)


def _dt_bwd(ddf, ddb, dp):
    RT = ddf.shape[0]

    def body(f_ref, b_ref, dp_any, o_ref, st_ref):
        del dp_any

        @pl.when(pl.program_id(0) == 0)
        def _():
            st_ref[...] = jnp.zeros_like(st_ref)

        s = f_ref[...] + b_ref[...]
        o_ref[...] = s.astype(o_ref.dtype)
        st_ref[0:1, :] += _colsum(s)

    return pl.pallas_call(
        body, name="dt_bwd", grid=(RT // TL,),
        in_specs=[_rt(128), _rt(128), pl.BlockSpec(memory_space=pl.ANY)],
        out_specs=[_rt(128, ODT // 128), _cst((8, 128))],
        out_shape=[jax.ShapeDtypeStruct(dp.shape, dp.dtype), jax.ShapeDtypeStruct((8, 128), F32)],
        input_output_aliases={2: 0},
        compiler_params=_params(("arbitrary",)),
    )(ddf, ddb, dp)


def _ln0_bwd(dh1, dr1, x, ctx, g, b, modx, modc):
    L = x.shape[0]
    nt = L // TL

    def body(dh_ref, dr1_ref, x_ref, c_ref, g_ref, b_ref, mx_ref, mc_ref, gx_ref, st_ref):
        i = pl.program_id(0)
        isc = i == nt

        @pl.when(i == 0)
        def _():
            st_ref[...] = jnp.zeros_like(st_ref)

        xin = jnp.where(isc, c_ref[...], x_ref[...])
        xhat, rstd = _ln(xin)
        xn = xhat * g_ref[...] + b_ref[...]
        sc = jnp.where(isc, mc_ref[1:2, :], mx_ref[1:2, :])
        dh = dh_ref[...]
        lat = jnp.where(isc, 0.0, 1.0)
        dxn = dh * (1.0 + sc) + (lat * ALPHA) * dr1_ref[...]
        tsh = _colsum(dh)
        tsc = _colsum(dh * xn)
        st_ref[0:1, :] += lat * tsh
        st_ref[1:2, :] += lat * tsc
        st_ref[2:3, :] += (1.0 - lat) * tsh
        st_ref[3:4, :] += (1.0 - lat) * tsc
        st_ref[4:5, :] += _colsum(dxn * xhat)
        st_ref[5:6, :] += _colsum(dxn)

        @pl.when(i < nt)
        def _():
            gx_ref[...] = _ln_bwd(dxn * g_ref[...], xhat, rstd)

    return pl.pallas_call(
        body, name="ln0_bwd", grid=(nt + 1,),
        in_specs=[_rt(D), _rtc(D, nt), _rtc(D, nt), _cst((TL, D)), _cst((1, D)), _cst((1, D)),
                  _cst((8, D)), _cst((8, D))],
        out_specs=[_rtc(D, nt), _cst((8, D))],
        out_shape=[jax.ShapeDtypeStruct((L, D), F32), jax.ShapeDtypeStruct((8, D), F32)],
        compiler_params=_params(("arbitrary",)),
    )(dh1, dr1, x, ctx, g, b, modx, modc)


def _perm_cols(w):
    pad = jnp.zeros((w.shape[0], NPJ - NNAT), w.dtype)
    return jnp.concatenate([w[:, 0:1024], w[:, 2592:3616], w[:, 3616:4640], w[:, 1024:2048],
                            w[:, 4640:6688], w[:, 2048:2304], w[:, 2304:2560], w[:, 2560:2592], pad],
                           axis=1)


SECTIONS = ((0, 1024, OZ), (1024, 2048, OXS), (2048, 2304, OB), (2304, 2560, OC), (2560, 2592, ODT),
            (2592, 3616, OU), (3616, 4640, OV), (4640, 6688, OG))


def _perm_from_blocks(ga):
    n = ga.shape[2]
    pieces = []
    for na, nb, _ in sorted(SECTIONS, key=lambda sec: sec[2]):
        for k in range(NDEV):
            lo, hi = max(na, k * n), min(nb, (k + 1) * n)
            if lo < hi:
                pieces.append(ga[k][:, lo - k * n:hi - k * n])
    pieces.append(jnp.zeros((ga.shape[1], NPJ - NNAT), ga.dtype))
    return jnp.concatenate(pieces, axis=1)


def _blocks_from_perm(gp, n):
    blocks = []
    for k in range(NDEV):
        pieces = []
        for na, nb, po in SECTIONS:
            lo, hi = max(na, k * n), min(nb, (k + 1) * n)
            if lo < hi:
                pieces.append(gp[:, po + lo - na:po + hi - na])
        blocks.append(jnp.concatenate(pieces, axis=1))
    return jnp.stack(blocks)


def _padded(n, row_align):
    unit = row_align * D
    return -(-n // unit) * unit if row_align else n


def _slab(arrs, rows, row_align=0):
    parts = []
    for a in arrs:
        f = a.reshape(-1)
        parts.append(jnp.pad(f, (0, _padded(f.shape[0], row_align) - f.shape[0])))
    flat = jnp.concatenate(parts)
    flat = jnp.pad(flat, (0, rows * D - flat.shape[0]))
    return flat.reshape(rows, D)


def _unslab(slab, shapes, row_align=0):
    out, off = [], 0
    for shp in shapes:
        n = 1
        for s in shp:
            n *= s
        r0, r1 = off // D, -(-(off + n) // D)
        out.append(slab[r0:r1].reshape(-1)[off - r0 * D:off - r0 * D + n].reshape(shp))
        off += _padded(n, row_align)
    return out


def _row(v):
    return v.reshape(1, -1)


def _t(a):
    return jnp.swapaxes(a, 0, 1)


def _pad_rows(a, rows):
    return jnp.pad(a, ((0, rows - a.shape[0]), (0, 0)))


BIG = ["w_in", "w_ssd_proj", "w_gm_proj", "w_out", "w_ff1", "w_ff3", "w_ff2"]
BIG_ROWS = 2304
BIG_ALIGN = 16
REPL = ["c_ctx", "ln0_g", "ln0_b", "b_ada", "conv_b", "dt_bias", "a_log", "d_skip", "ssd_norm_g",
        "gm_norm_g", "gm_norm_b", "w_spatial", "b_spatial", "b_gate", "ln1_g", "ln1_b", "ln2_g", "ln2_b"]
SMALL_ROWS = 160
WEIGHTS = ["c_ctx", "ln0_g", "ln0_b", "w_ada", "b_ada", "w_in", "conv_w", "conv_b", "dt_bias", "a_log",
           "d_skip", "ssd_norm_g", "gm_norm_g", "gm_norm_b", "w_spatial", "b_spatial", "b_gate",
           "w_ssd_proj", "w_gm_proj", "w_out", "ln1_g", "ln1_b", "w_ff1", "w_ff3", "w_ff2", "ln2_g", "ln2_b"]


def kernel(x, c, ctx, c_ctx, ln0_g, ln0_b, w_ada, b_ada, w_in, conv_w, conv_b, dt_bias, a_log, d_skip, ssd_norm_g, gm_norm_g, gm_norm_b, w_spatial, b_spatial, b_gate, w_ssd_proj, w_gm_proj, w_out, ln1_g, ln1_b, w_ff1, w_ff3, w_ff2, ln2_g, ln2_b, loss_target, m_c_ctx, m_ln0_g, m_ln0_b, m_w_ada, m_b_ada, m_w_in, m_conv_w, m_conv_b, m_dt_bias, m_a_log, m_d_skip, m_ssd_norm_g, m_gm_norm_g, m_gm_norm_b, m_w_spatial, m_b_spatial, m_b_gate, m_w_ssd_proj, m_w_gm_proj, m_w_out, m_ln1_g, m_ln1_b, m_w_ff1, m_w_ff3, m_w_ff2, m_ln2_g, m_ln2_b, v_c_ctx, v_ln0_g, v_ln0_b, v_w_ada, v_b_ada, v_w_in, v_conv_w, v_conv_b, v_dt_bias, v_a_log, v_d_skip, v_ssd_norm_g, v_gm_norm_g, v_gm_norm_b, v_w_spatial, v_b_spatial, v_b_gate, v_w_ssd_proj, v_w_gm_proj, v_w_out, v_ln1_g, v_ln1_b, v_w_ff1, v_w_ff3, v_w_ff2, v_ln2_g, v_ln2_b):
    W = dict(c_ctx=c_ctx, ln0_g=ln0_g, ln0_b=ln0_b, w_ada=w_ada, b_ada=b_ada, w_in=w_in, conv_w=conv_w,
             conv_b=conv_b, dt_bias=dt_bias, a_log=a_log, d_skip=d_skip, ssd_norm_g=ssd_norm_g,
             gm_norm_g=gm_norm_g, gm_norm_b=gm_norm_b, w_spatial=w_spatial, b_spatial=b_spatial,
             b_gate=b_gate, w_ssd_proj=w_ssd_proj, w_gm_proj=w_gm_proj, w_out=w_out, ln1_g=ln1_g,
             ln1_b=ln1_b, w_ff1=w_ff1, w_ff3=w_ff3, w_ff2=w_ff2, ln2_g=ln2_g, ln2_b=ln2_b)
    M = dict(c_ctx=m_c_ctx, ln0_g=m_ln0_g, ln0_b=m_ln0_b, w_ada=m_w_ada, b_ada=m_b_ada, w_in=m_w_in,
             conv_w=m_conv_w, conv_b=m_conv_b, dt_bias=m_dt_bias, a_log=m_a_log, d_skip=m_d_skip,
             ssd_norm_g=m_ssd_norm_g, gm_norm_g=m_gm_norm_g, gm_norm_b=m_gm_norm_b,
             w_spatial=m_w_spatial, b_spatial=m_b_spatial, b_gate=m_b_gate, w_ssd_proj=m_w_ssd_proj,
             w_gm_proj=m_w_gm_proj, w_out=m_w_out, ln1_g=m_ln1_g, ln1_b=m_ln1_b, w_ff1=m_w_ff1,
             w_ff3=m_w_ff3, w_ff2=m_w_ff2, ln2_g=m_ln2_g, ln2_b=m_ln2_b)
    V = dict(c_ctx=v_c_ctx, ln0_g=v_ln0_g, ln0_b=v_ln0_b, w_ada=v_w_ada, b_ada=v_b_ada, w_in=v_w_in,
             conv_w=v_conv_w, conv_b=v_conv_b, dt_bias=v_dt_bias, a_log=v_a_log, d_skip=v_d_skip,
             ssd_norm_g=v_ssd_norm_g, gm_norm_g=v_gm_norm_g, gm_norm_b=v_gm_norm_b,
             w_spatial=v_w_spatial, b_spatial=v_b_spatial, b_gate=v_b_gate, w_ssd_proj=v_w_ssd_proj,
             w_gm_proj=v_w_gm_proj, w_out=v_w_out, ln1_g=v_ln1_g, ln1_b=v_ln1_b, w_ff1=v_w_ff1,
             w_ff3=v_w_ff3, w_ff2=v_w_ff2, ln2_g=v_ln2_g, ln2_b=v_ln2_b)

    me = 4 * lax.axis_index("x") + 2 * lax.axis_index("y") + lax.axis_index("c")
    xl, cx, tgt = x[0], ctx[0], loss_target[0]
    L = xl.shape[0]
    assert cx.shape[0] == TL and L % TL == 0
    ada_n = w_ada.shape[2]
    cw_n = conv_w.shape[2]

    small1 = _pad_rows(jnp.concatenate([c, _slab([conv_w[0]], 1)], axis=0), 8)
    g1 = _all_gather(small1, "ag_small")
    c_all = g1[:, 0, :]
    conv_w_full = g1[:, 1, :5 * cw_n].reshape(NDEV, 5, cw_n).transpose(1, 0, 2).reshape(5, NDEV * cw_n)
    sq = w_ssd_proj.shape[1]
    ffr = w_ff2.shape[1]
    ffc = w_ff1.shape[2]
    late = [jnp.concatenate([w_ssd_proj[0], w_gm_proj[0], w_out[0], w_ff2[0]], axis=0).astype(_MXU),
            _t(w_ff1[0]).astype(_MXU), _t(w_ff3[0]).astype(_MXU)]

    c16 = _pad_rows(jnp.concatenate([c_all, _row(c_ctx)], axis=0), 16)
    b_ada_sh = lax.dynamic_slice(b_ada, (0, ada_n * me), (1, ada_n))
    modp = _ada_fwd(c16, w_ada[0], b_ada_sh)
    mod16 = _all_gather(modp, "ag_mod").transpose(1, 0, 2).reshape(16, NDEV * ada_n)

    ga, = _all_gather_multi([w_in[0].astype(_MXU)], "ag_w_in")
    ga, late, mod16 = lax.optimization_barrier((ga, late, mod16))
    lw_send, lw_recv, lw_src, lw_land, lw_token = _exchange_start(late, "ag_late_start", gather=True)
    w_in_p = _perm_from_blocks(ga)
    modx = _pad_rows(lax.dynamic_slice(mod16, (me, 0), (1, 6 * D)).reshape(6, D), 8) + lw_token[0, 0]
    modc = _pad_rows(mod16[8].reshape(6, D), 8)

    g0, b0 = _row(ln0_g), _row(ln0_b)
    xn, h1 = _ln0_fwd(xl, cx, g0, b0, modx, modc)
    p = _mm(h1, w_in_p, "nn", F32, "mm_p")
    conv_w8 = _pad_rows(conv_w_full, 8)
    xbc = _conv_fwd(p, conv_w8, conv_b)
    prm = _pad_rows(jnp.pad(jnp.stack([dt_bias.reshape(32), a_log.reshape(32)]), ((0, 0), (0, 96))), 8)
    yf, yb, hpf, hpb = _ssd2_fwd(xbc, p, prm)
    lw_land = _exchange_wait(lw_send, lw_recv, lw_src, lw_land, yf, "ag_late_wait", gather=True)
    fw_send, fw_recv, lw_land, fw_token = _forward_start(lw_land, "ag_fwd_start")
    dsk = _row(jnp.repeat(d_skip[0, 0] + d_skip[0, 1], HP)) + fw_token[0:1, 0:1]
    ws_m = w_spatial[0].astype(_MXU)
    bsT = jnp.pad(b_spatial[0].T, ((0, 0), (0, 120)))
    mixp = (dsk, ssd_norm_g, gm_norm_g, gm_norm_b, ws_m, bsT)
    yssd, ygm = _mix_fwd(yf, yb, p, xbc, *mixp)
    gb, gc1, gc2 = _forward_wait(fw_send, fw_recv, lw_land, yssd, "ag_fwd_wait")

    def with_own(g, mine, k):
        return jnp.where(me == k, mine, g[k])

    gb = jnp.stack([with_own(gb, late[0], k) for k in range(NDEV)])
    w_ssd_f = gb[:, 0:sq].reshape(NDEV * sq, D)
    w_gm_f = gb[:, sq:2 * sq].reshape(NDEV * sq, D)
    w_out_f = gb[:, 2 * sq:3 * sq].reshape(NDEV * sq, D)
    w_ff2_f = gb[:, 3 * sq:3 * sq + ffr].reshape(NDEV * ffr, D)
    assert HFF == (NDEV // 2) * ffc
    halves = (range(0, NDEV // 2), range(NDEV // 2, NDEV))
    w13i = jnp.concatenate([with_own(g, mine, k) for ks in halves for g, mine in ((gc1, late[1]), (gc2, late[2]))
                            for k in ks], axis=0)
    a1 = _mm(yssd, w_ssd_f, "nn", F32, "mm_a1")
    a2 = _mm(ygm, w_gm_f, "nn", F32, "mm_a2")
    merged = _gate_fwd(a1, a2, p, b_gate)
    out = _mm(merged, w_out_f, "nn", F32, "mm_out")
    r1, h2 = _res1_fwd(xn, out, modx, ln1_g, ln1_b)
    f13, ff = _mm_f13_glu(h2, w13i)
    o2 = _mm(ff, w_ff2_f, "nn", F32, "mm_o2")

    dr2, do2, st2, loss_slab = _res2(r1, o2, tgt, modx, ln1_g, ln1_b, ln2_g, ln2_b)
    loss = lax.psum(loss_slab[0, 0], ("x", "y", "c"))
    df13 = _mm_dff_glu(do2, w_ff2_f, f13)
    dh2 = _mm(df13, w13i, "nn", F32, "mm_dh2")
    dw_ff2 = _mm(ff, do2, "tn", _MXU, "mm_dw_ff2")
    dw13i = _mm(df13, h2, "tn", _MXU, "mm_dw13")

    def owner_blocks(first):
        return jnp.concatenate([dw13i[t * 2 * HFF + first:t * 2 * HFF + first + HFF].reshape(NDEV // 2, ffc, D)
                                for t in range(2)], axis=0)

    xff = [dw_ff2.reshape(NDEV, ffr, D), owner_blocks(0), owner_blocks(HFF)]
    ff_send, ff_recv, ff_src, ff_land, ff_token = _exchange_start(xff, "xchg_ff_start")
    modx = modx + ff_token[0, 0]
    dr1, dout, st1 = _res1_bwd(dr2, dh2, r1, out, modx, ln1_g, ln1_b)
    dmg = _mm(dout, w_out_f, "nt", F32, "mm_dmerged")
    dw_out = _mm(merged, dout, "tn", _MXU, "mm_dw_out")
    dp = jnp.zeros((L + TL, NPJ), _MXU)
    dp, da1, da2, stg = _gate_bwd(dmg, a1, a2, p, b_gate, dp)
    dys = _mm(da1, w_ssd_f, "nt", F32, "mm_dyssd")
    dym = _mm(da2, w_gm_f, "nt", F32, "mm_dygm")
    dw_ssd = _mm(yssd, da1, "tn", _MXU, "mm_dw_ssd")
    dw_gm = _mm(ygm, da2, "tn", _MXU, "mm_dw_gm")
    xsq = [jnp.concatenate([dw_ssd.reshape(NDEV, sq, D), dw_gm.reshape(NDEV, sq, D),
                            dw_out.reshape(NDEV, sq, D)], axis=1)]
    sq_send, sq_recv, sq_src, sq_land, sq_token = _exchange_start(xsq, "xchg_sq_start")
    mixp = (dsk + sq_token[0:1, 0:1],) + mixp[1:]
    dp, dyd, stm, dws, dbsT = _mix_bwd(dys, dym, yf, yb, p, xbc, dp, *mixp)
    dxf, dxb, ddf, ddb, sts = _ssd2_bwd(xbc, p, prm, dsk, dyd, hpf, hpb)
    dp, dcw, dcb = _conv_bwd(dxf, dxb, p, conv_w8, conv_b, dp)
    dp, std = _dt_bwd(ddf, ddb, dp)
    dh1 = _mm(dp, w_in_p, "nt", F32, "mm_dh1")
    dw_in_p = _mm(h1, dp, "tn", _MXU, "mm_dw_in")
    xin = [_blocks_from_perm(dw_in_p, w_in.shape[2])]
    in_send, in_recv, in_src, in_land, in_token = _exchange_start(xin, "xchg_in_start")
    modx = modx + in_token[0, 0]
    grad_x, st0 = _ln0_bwd(dh1, dr1, xl, cx, g0, b0, modx, modc)

    zero = jnp.zeros((D,), F32)
    dmod = jnp.stack([jnp.concatenate([st0[0], st0[1], st1[4], st1[1], st1[0], st2[2]]),
                      jnp.concatenate([st0[2], st0[3], zero, zero, zero, zero])])
    g16 = _all_gather(_pad_rows(dmod, 8), "ag_dmod")[:, 0:2, :].reshape(16, 6 * D)
    g16_sh = lax.dynamic_slice(g16, (0, ada_n * me), (16, ada_n))
    c16b = jnp.stack([c_all, jnp.broadcast_to(_row(c_ctx), (NDEV, D))], axis=1).reshape(16, D)
    dw_ada, db_ada8, dcc8 = _ada_bwd(c16b, g16, g16_sh, w_ada[0])

    part = dict(
        c_ctx=dcc8[0], ln0_g=st0[4], ln0_b=st0[5], conv_w=dcw[0:5], conv_b=dcb[0],
        dt_bias=std[0, 0:32], a_log=sts[0, 0:32], d_skip=jnp.tile(sts[1, 0:16], 2),
        ssd_norm_g=stm[0], gm_norm_g=stm[1], gm_norm_b=stm[2], w_spatial=dws,
        b_spatial=dbsT[:, 0:8].T, b_gate=stg[0], ln1_g=st1[2], ln1_b=st1[3], ln2_g=st2[0], ln2_b=st2[1])
    pnames = list(part)
    psum8 = _sum8(_all_gather(_slab([part[n] for n in pnames], SMALL_ROWS), "ag_smallgrads"), "sum_smallgrads")
    small = dict(zip(pnames, _unslab(psum8, [part[n].shape for n in pnames])))
    grads = {n: small[n].reshape(W[n].shape) for n in pnames if n != "conv_w"}
    grads["conv_w"] = lax.dynamic_slice(small["conv_w"], (0, cw_n * me), (5, cw_n)).reshape(conv_w.shape)
    grads["b_ada"] = db_ada8[0:1]
    grads["w_ada"] = dw_ada.reshape(w_ada.shape)

    delta, new_m, new_v = {}, {}, {}

    def adam_group(names, rows, tag, align=0):
        shapes = [W[n].shape for n in names]
        outs = _adamw(*[_slab([src[n] for n in names], rows, align) for src in (grads, W, M, V)], tag)
        for res, slab in zip((delta, new_m, new_v), outs):
            for n, a in zip(names, _unslab(slab, shapes, align)):
                res[n] = a

    adam_group(REPL + ["conv_w"], SMALL_ROWS, "adamw_small")
    res = _adamw(grads["w_ada"][0], w_ada[0], m_w_ada[0], v_w_ada[0], "adamw_w_ada")
    delta["w_ada"], new_m["w_ada"], new_v["w_ada"] = [a[None] for a in res]

    rff = _exchange_wait(ff_send, ff_recv, ff_src, ff_land, st0, "xchg_ff_wait")
    rsq = _exchange_wait(sq_send, sq_recv, sq_src, sq_land, rff[0], "xchg_sq_wait")
    rin = _exchange_wait(in_send, in_recv, in_src, in_land, delta["ln2_b"], "xchg_in_wait")

    def own(blocks):
        return lax.dynamic_index_in_dim(blocks, me, 0, keepdims=False)

    for n, r8, mine, row0, tr in (
            ("w_ff2", rff[0], own(xff[0]), 0, ffr // 2), ("w_ssd_proj", rsq[0], own(xsq[0]), 0, sq),
            ("w_gm_proj", rsq[0], own(xsq[0]), sq, sq), ("w_out", rsq[0], own(xsq[0]), 2 * sq, sq),
            ("w_in", rin[0], own(xin[0]), 0, 256)):
        res = _adamw_sum(r8, mine, W[n][0], M[n][0], V[n][0], row0, tr, "adamw_" + n)
        grads[n], delta[n], new_m[n], new_v[n] = [a[None] for a in res]
    for n, r8, mine in (("w_ff1", rff[1], own(xff[1])), ("w_ff3", rff[2], own(xff[2]))):
        res = _adamw_sum(r8, mine, _t(W[n][0]), _t(M[n][0]), _t(V[n][0]), 0, ffc // 2, "adamw_" + n)
        grads[n], delta[n], new_m[n], new_v[n] = [_t(a)[None] for a in res]

    return (loss, grad_x[None], *[grads[n] for n in WEIGHTS], *[delta[n] for n in WEIGHTS],
            *[new_m[n] for n in WEIGHTS], *[new_v[n] for n in WEIGHTS])
```

```python
import functools

import jax
import jax.numpy as jnp
from jax import lax
from jax.experimental import pallas as pl
from jax.experimental.pallas import tpu as pltpu

_MXU = jnp.bfloat16
F32 = jnp.float32
D = 1024
TL = 256
Q = 128
NH, HP, NS, HPG = 16, 64, 128, 8
DFF = 2816
ALPHA = 2.0 ** 0.25
EPS = 1e-5
OZ, OU, OV, OXS, OG, OB, OC, ODT, NPJ = 0, 1024, 2048, 3072, 4096, 6144, 6400, 6656, 6912
NNAT = 6688
NDEV = 8
ADAM_LR, ADAM_B1, ADAM_B2, ADAM_EPS, ADAM_WD, ADAM_STEP = 1e-3, 0.9, 0.999, 1e-8, 0.01, 10
VMEM_LIMIT = 48 * 1024 * 1024

NN = ((1,), (0,))
NT = ((1,), (1,))
TN = ((0,), (0,))
MESH = pl.DeviceIdType.MESH


def _dot(a, b, dims):
    return lax.dot_general(a.astype(_MXU), b.astype(_MXU), (dims, ((), ())),
                           preferred_element_type=F32)


def _tile(n, cands):
    for c in cands:
        if n % c == 0:
            return c
    return n


def _divisor_tile(n, cap, mult):
    best = n
    for t in range(mult, min(n, cap) + 1, mult):
        if n % t == 0:
            best = t
    return best


def _params(sem):
    return pltpu.CompilerParams(dimension_semantics=sem, vmem_limit_bytes=VMEM_LIMIT)


def _cst(shape):
    nd = len(shape)
    return pl.BlockSpec(shape, lambda *_: (0,) * nd)


def _rt(w, cb=0, rows=TL):
    return pl.BlockSpec((rows, w), lambda i: (i, cb))


def _rtc(w, nt, cb=0):
    return pl.BlockSpec((TL, w), lambda i: (jnp.minimum(i, nt - 1), cb))


def _sig(x):
    return jax.nn.sigmoid(x)


def _softplus(x):
    return jnp.maximum(x, 0.0) + jnp.log1p(jnp.exp(-jnp.abs(x)))


_G0, _G1 = 0.7978845608028654, 0.044715


def _gelu(x):
    t = jnp.tanh(_G0 * (x + _G1 * x * x * x))
    return 0.5 * x * (1.0 + t), t


def _gelu_grad(x, t):
    return 0.5 * (1.0 + t) + 0.5 * x * (1.0 - t * t) * _G0 * (1.0 + 3.0 * _G1 * x * x)


def _ln(r):
    mu = jnp.mean(r, axis=-1, keepdims=True)
    xc = r - mu
    var = jnp.mean(xc * xc, axis=-1, keepdims=True)
    rstd = lax.rsqrt(var + EPS)
    return xc * rstd, rstd


def _ln_bwd(dyh, xhat, rstd):
    return rstd * (dyh - jnp.mean(dyh, axis=-1, keepdims=True)
                   - xhat * jnp.mean(dyh * xhat, axis=-1, keepdims=True))


def _colsum(v):
    return jnp.sum(v, axis=0, keepdims=True)


def _sum11(v):
    return jnp.sum(jnp.sum(v, axis=1, keepdims=True), axis=0, keepdims=True)


def _cumsum_rows(a, rev):
    n = a.shape[0]
    row = lax.broadcasted_iota(jnp.int32, a.shape, 0)
    s = 1
    while s < n:
        if rev:
            a = a + jnp.where(row < n - s, pltpu.roll(a, n - s, 0), 0.0)
        else:
            a = a + jnp.where(row >= s, pltpu.roll(a, s, 0), 0.0)
        s *= 2
    return a


def _mm(a, b, mode, out_dtype, name):
    if mode == "tn":
        K, M = a.shape
    else:
        M, K = a.shape
    N = b.shape[0] if mode == "nt" else b.shape[1]
    tm = _divisor_tile(M, 1408, 128) if mode == "tn" else _divisor_tile(M, 1088, 16)
    tn = _divisor_tile(N, 1408, 128)
    tk = _divisor_tile(K, 2304, 128)
    nk = K // tk
    dims = {"nn": NN, "nt": NT, "tn": TN}[mode]
    use_acc = nk > 1 and out_dtype != F32

    def body(a_ref, b_ref, o_ref, *acc):
        prod = _dot(a_ref[...], b_ref[...], dims)
        if nk == 1:
            o_ref[...] = prod.astype(o_ref.dtype)
            return
        acc_ref = acc[0] if use_acc else o_ref
        k = pl.program_id(2)

        @pl.when(k == 0)
        def _():
            acc_ref[...] = prod

        if use_acc:
            @pl.when((k > 0) & (k < nk - 1))
            def _():
                acc_ref[...] += prod

            @pl.when(k == nk - 1)
            def _():
                o_ref[...] = (acc_ref[...] + prod).astype(o_ref.dtype)
        else:
            @pl.when(k > 0)
            def _():
                o_ref[...] += prod

    if mode == "tn":
        a_spec = pl.BlockSpec((tk, tm), lambda i, j, k: (k, i))
    else:
        a_spec = pl.BlockSpec((tm, tk), lambda i, j, k: (i, k))
    if mode == "nt":
        b_spec = pl.BlockSpec((tn, tk), lambda i, j, k: (j, k))
    else:
        b_spec = pl.BlockSpec((tk, tn), lambda i, j, k: (k, j))
    return pl.pallas_call(
        body, name=name, grid=(M // tm, N // tn, nk),
        in_specs=[a_spec, b_spec],
        out_specs=pl.BlockSpec((tm, tn), lambda i, j, k: (i, j)),
        out_shape=jax.ShapeDtypeStruct((M, N), out_dtype),
        scratch_shapes=[pltpu.VMEM((tm, tn), F32)] if use_acc else [],
        compiler_params=_params(("parallel", "parallel", "arbitrary")),
    )(a, b)


def _all_gather(x, name):
    def body(x_ref, out_ref, send_sems, recv_sems, local_sem):
        mx, my, mc = lax.axis_index("x"), lax.axis_index("y"), lax.axis_index("c")
        me, sibling = (mx, my, mc), (mx, my, 1 - mc)
        chips = [(1 - mx, my), (mx, 1 - my), (1 - mx, 1 - my)]

        def slot(px, py, pc):
            return out_ref.at[4 * px + 2 * py + pc]

        def copy(k, block, to, src=None):
            return pltpu.make_async_remote_copy(
                src_ref=slot(*block) if src is None else src, dst_ref=slot(*block),
                send_sem=send_sems.at[k], recv_sem=recv_sems.at[k],
                device_id=to, device_id_type=MESH)

        mine = pltpu.make_async_copy(x_ref, slot(*me), local_sem)
        mine.start()
        first = [copy(0, me, sibling, src=x_ref)]
        first += [copy(1 + j, me, (*chip, mc), src=x_ref) for j, chip in enumerate(chips)]
        for cp in first:
            cp.start()
        passed = [copy(4 + j, (*chip, mc), sibling) for j, chip in enumerate(chips)]
        for j, chip in enumerate(chips):
            copy(1 + j, (*chip, mc), me).wait_recv()
            passed[j].start()
        copy(0, sibling, me).wait_recv()
        for j, chip in enumerate(chips):
            copy(4 + j, (*chip, 1 - mc), me).wait_recv()
        for cp in first + passed:
            cp.wait_send()
        mine.wait()

    return pl.pallas_call(
        body, name=name,
        out_shape=jax.ShapeDtypeStruct((NDEV,) + x.shape, x.dtype),
        in_specs=[pl.BlockSpec(memory_space=pl.ANY)],
        out_specs=pl.BlockSpec(memory_space=pl.ANY),
        scratch_shapes=[pltpu.SemaphoreType.DMA((7,)), pltpu.SemaphoreType.DMA((7,)),
                        pltpu.SemaphoreType.DMA],
    )(x)


def _owner_exchange(g, name):
    def body(g_ref, out_ref, send_sems, recv_sems, local_sem):
        mx, my, mc = lax.axis_index("x"), lax.axis_index("y"), lax.axis_index("c")
        local = pltpu.make_async_copy(g_ref.at[4 * mx + 2 * my + mc], out_ref.at[0], local_sem)
        local.start()
        copies = []
        for f in range(1, NDEV):
            px = 1 - mx if (f >> 2) & 1 else mx
            py = 1 - my if (f >> 1) & 1 else my
            pc = 1 - mc if f & 1 else mc
            cp = pltpu.make_async_remote_copy(
                src_ref=g_ref.at[4 * px + 2 * py + pc], dst_ref=out_ref.at[f],
                send_sem=send_sems.at[f - 1], recv_sem=recv_sems.at[f - 1],
                device_id=(px, py, pc), device_id_type=MESH)
            cp.start()
            copies.append(cp)
        for cp in copies:
            cp.wait_recv()
        for cp in copies:
            cp.wait_send()
        local.wait()

    return pl.pallas_call(
        body, name=name,
        out_shape=jax.ShapeDtypeStruct(g.shape, g.dtype),
        in_specs=[pl.BlockSpec(memory_space=pl.ANY)],
        out_specs=pl.BlockSpec(memory_space=pl.ANY),
        scratch_shapes=[pltpu.SemaphoreType.DMA((7,)), pltpu.SemaphoreType.DMA((7,)),
                        pltpu.SemaphoreType.DMA],
    )(g)


def _any_specs(n):
    return [pl.BlockSpec(memory_space=pl.ANY)] * n


def _all_gather_multi(xs, name):
    na = len(xs)

    def body(*refs):
        x_refs, out_refs = refs[:na], refs[na:2 * na]
        send_sems, recv_sems, local_sems = refs[2 * na:]
        mx, my, mc = lax.axis_index("x"), lax.axis_index("y"), lax.axis_index("c")
        me, sibling = (mx, my, mc), (mx, my, 1 - mc)
        chips = [(1 - mx, my), (mx, 1 - my), (1 - mx, 1 - my)]

        def copy(a, k, block, to, src=None):
            slot = out_refs[a].at[4 * block[0] + 2 * block[1] + block[2]]
            return pltpu.make_async_remote_copy(
                src_ref=slot if src is None else src, dst_ref=slot,
                send_sem=send_sems.at[7 * a + k], recv_sem=recv_sems.at[7 * a + k],
                device_id=to, device_id_type=MESH)

        mine = [pltpu.make_async_copy(x_refs[a], out_refs[a].at[4 * mx + 2 * my + mc], local_sems.at[a])
                for a in range(na)]
        for cp in mine:
            cp.start()
        first = []
        for a in range(na):
            first.append(copy(a, 0, me, sibling, src=x_refs[a]))
            first += [copy(a, 1 + j, me, (*chip, mc), src=x_refs[a]) for j, chip in enumerate(chips)]
        for cp in first:
            cp.start()
        passed = []
        for a in range(na):
            for j, chip in enumerate(chips):
                copy(a, 1 + j, (*chip, mc), me).wait_recv()
                fwd = copy(a, 4 + j, (*chip, mc), sibling)
                fwd.start()
                passed.append(fwd)
        for a in range(na):
            copy(a, 0, sibling, me).wait_recv()
            for j, chip in enumerate(chips):
                copy(a, 4 + j, (*chip, 1 - mc), me).wait_recv()
        for cp in first + passed:
            cp.wait_send()
        for cp in mine:
            cp.wait()

    return pl.pallas_call(
        body, name=name,
        out_shape=[jax.ShapeDtypeStruct((NDEV,) + x.shape, x.dtype) for x in xs],
        in_specs=_any_specs(na), out_specs=_any_specs(na),
        scratch_shapes=[pltpu.SemaphoreType.DMA((7 * na,)), pltpu.SemaphoreType.DMA((7 * na,)),
                        pltpu.SemaphoreType.DMA((na,))],
    )(*xs)


def _owner_exchange_multi(gs, name):
    na = len(gs)

    def body(*refs):
        g_refs, out_refs = refs[:na], refs[na:2 * na]
        send_sems, recv_sems, local_sems = refs[2 * na:]
        mx, my, mc = lax.axis_index("x"), lax.axis_index("y"), lax.axis_index("c")
        locals_ = [pltpu.make_async_copy(g_refs[a].at[4 * mx + 2 * my + mc], out_refs[a].at[0], local_sems.at[a])
                   for a in range(na)]
        for cp in locals_:
            cp.start()
        copies = []
        for a in range(na):
            for f in range(1, NDEV):
                px = 1 - mx if (f >> 2) & 1 else mx
                py = 1 - my if (f >> 1) & 1 else my
                pc = 1 - mc if f & 1 else mc
                cp = pltpu.make_async_remote_copy(
                    src_ref=g_refs[a].at[4 * px + 2 * py + pc], dst_ref=out_refs[a].at[f],
                    send_sem=send_sems.at[7 * a + f - 1], recv_sem=recv_sems.at[7 * a + f - 1],
                    device_id=(px, py, pc), device_id_type=MESH)
                cp.start()
                copies.append(cp)
        for cp in copies:
            cp.wait_recv()
        for cp in copies:
            cp.wait_send()
        for cp in locals_:
            cp.wait()

    return pl.pallas_call(
        body, name=name,
        out_shape=[jax.ShapeDtypeStruct(g.shape, g.dtype) for g in gs],
        in_specs=_any_specs(na), out_specs=_any_specs(na),
        scratch_shapes=[pltpu.SemaphoreType.DMA((7 * na,)), pltpu.SemaphoreType.DMA((7 * na,)),
                        pltpu.SemaphoreType.DMA((na,))],
    )(*gs)


def _adamw_sum(r8, own, w, m, v, row0, tr, name):
    R, C = w.shape
    assert row0 % tr == 0
    blk0 = row0 // tr
    bc1 = 1.0 - ADAM_B1 ** ADAM_STEP
    bc2 = 1.0 - ADAM_B2 ** ADAM_STEP

    def body(r_ref, *refs):
        if own is None:
            gg = r_ref[0].astype(F32)
        else:
            gg = refs[0][...].astype(F32)
            refs = refs[1:]
        w_ref, m_ref, v_ref, g_ref, d_ref, mo_ref, vo_ref = refs
        for k in range(1, NDEV):
            gg = gg + r_ref[k].astype(F32)
        mn = ADAM_B1 * m_ref[...] + (1.0 - ADAM_B1) * gg
        vn = ADAM_B2 * v_ref[...] + (1.0 - ADAM_B2) * (gg * gg)
        mh = mn / bc1
        vh = vn / bc2
        g_ref[...] = gg
        d_ref[...] = -ADAM_LR * (mh / (jnp.sqrt(vh) + ADAM_EPS) + ADAM_WD * w_ref[...])
        mo_ref[...] = mn
        vo_ref[...] = vn

    spec = pl.BlockSpec((tr, C), lambda i: (i, 0))
    sh = jax.ShapeDtypeStruct((R, C), F32)
    own_ops = [] if own is None else [own]
    own_specs = [] if own is None else [pl.BlockSpec((tr, C), lambda i: (i + blk0, 0))]
    return pl.pallas_call(
        body, name=name, grid=(R // tr,),
        in_specs=[pl.BlockSpec((NDEV, tr, C), lambda i: (0, i + blk0, 0))] + own_specs + [spec, spec, spec],
        out_specs=[spec] * 4, out_shape=[sh] * 4, compiler_params=_params(("parallel",)),
    )(r8, *own_ops, w, m, v)


_HBM = pl.BlockSpec(memory_space=pltpu.HBM)
_SEM = pl.BlockSpec(memory_space=pltpu.SEMAPHORE)
_EFFECT = pltpu.SideEffectType.DATAFLOW_SIDE_EFFECTING


def _exchange_copies(g_refs, land_refs, send_sems, recv_sems, gather):
    mx, my, mc = lax.axis_index("x"), lax.axis_index("y"), lax.axis_index("c")
    copies = []
    for a in range(len(g_refs)):
        for f in ((1, 2, 4, 6) if gather else range(1, NDEV)):
            px = 1 - mx if (f >> 2) & 1 else mx
            py = 1 - my if (f >> 1) & 1 else my
            pc = 1 - mc if f & 1 else mc
            src = g_refs[a] if gather else g_refs[a].at[4 * px + 2 * py + pc]
            dst = land_refs[a].at[4 * mx + 2 * my + mc] if gather else land_refs[a].at[f]
            copies.append(pltpu.make_async_remote_copy(
                src_ref=src, dst_ref=dst,
                send_sem=send_sems.at[7 * a + f - 1], recv_sem=recv_sems.at[7 * a + f - 1],
                device_id=(px, py, pc), device_id_type=MESH))
    return copies


def _exchange_start(gs, name, gather=False):
    na = len(gs)

    def body(*refs):
        for cp in _exchange_copies(refs[:na], refs[na:2 * na], refs[2 * na], refs[2 * na + 1], gather):
            cp.start()
        refs[-1][...] = jnp.zeros_like(refs[-1])

    hbm = [pltpu.HBM(g.shape, g.dtype) for g in gs]
    land_shapes = [((NDEV,) + g.shape) if gather else g.shape for g in gs]
    lands = [pltpu.with_memory_space_constraint(lax.empty(shp, g.dtype), pltpu.HBM)
             for shp, g in zip(land_shapes, gs)]
    hbm_land = [pltpu.HBM(shp, g.dtype) for shp, g in zip(land_shapes, gs)]
    outs = pl.pallas_call(
        body, name=name,
        out_shape=(pltpu.SemaphoreType.DMA((7 * na,)), pltpu.SemaphoreType.DMA((7 * na,)), *hbm, *hbm_land,
                   jax.ShapeDtypeStruct((8, 128), F32)),
        in_specs=[_HBM] * (2 * na),
        out_specs=(_SEM, _SEM, *([_HBM] * (2 * na)), pl.BlockSpec(memory_space=pltpu.VMEM)),
        input_output_aliases={i: 2 + i for i in range(2 * na)},
        compiler_params=pltpu.CompilerParams(has_side_effects=_EFFECT),
    )(*[pltpu.with_memory_space_constraint(g, pltpu.HBM) for g in gs], *lands)
    return outs[0], outs[1], outs[2:2 + na], outs[2 + na:2 + 2 * na], outs[-1]


def _forward_copies(land_refs, send_sems, recv_sems):
    mx, my, mc = lax.axis_index("x"), lax.axis_index("y"), lax.axis_index("c")
    copies = []
    for a in range(len(land_refs)):
        for j, (fx, fy) in enumerate(((0, 1), (1, 0), (1, 1))):
            px = 1 - mx if fx else mx
            py = 1 - my if fy else my
            blk = land_refs[a].at[4 * px + 2 * py + mc]
            copies.append(pltpu.make_async_remote_copy(
                src_ref=blk, dst_ref=blk, send_sem=send_sems.at[3 * a + j], recv_sem=recv_sems.at[3 * a + j],
                device_id=(mx, my, 1 - mc), device_id_type=MESH))
    return copies


def _forward_start(lands, name):
    na = len(lands)

    def body(*refs):
        for cp in _forward_copies(refs[:na], refs[na], refs[na + 1]):
            cp.start()
        refs[-1][...] = jnp.zeros_like(refs[-1])

    outs = pl.pallas_call(
        body, name=name,
        out_shape=(pltpu.SemaphoreType.DMA((3 * na,)), pltpu.SemaphoreType.DMA((3 * na,)),
                   *[pltpu.HBM(g.shape, g.dtype) for g in lands], jax.ShapeDtypeStruct((8, 128), F32)),
        in_specs=[_HBM] * na,
        out_specs=(_SEM, _SEM, *([_HBM] * na), pl.BlockSpec(memory_space=pltpu.VMEM)),
        input_output_aliases={i: 2 + i for i in range(na)},
        compiler_params=pltpu.CompilerParams(has_side_effects=_EFFECT),
    )(*lands)
    return outs[0], outs[1], outs[2:2 + na], outs[-1]


def _forward_wait(send_sems, recv_sems, lands, after, name):
    na = len(lands)

    def body(*refs):
        for cp in _forward_copies(refs[:na], refs[na], refs[na + 1]):
            cp.wait_send()
            cp.wait_recv()

    return pl.pallas_call(
        body, name=name,
        out_shape=tuple(pltpu.HBM(g.shape, g.dtype) for g in lands),
        in_specs=[_HBM] * na + [_SEM, _SEM, pl.BlockSpec(memory_space=pl.ANY)],
        out_specs=tuple([_HBM] * na),
        input_output_aliases={i: i for i in range(na)},
        compiler_params=pltpu.CompilerParams(has_side_effects=_EFFECT),
    )(*lands, send_sems, recv_sems, after)


def _exchange_wait(send_sems, recv_sems, g_thru, land_thru, after, name, gather=False):
    na = len(g_thru)

    def body(*refs):
        for cp in _exchange_copies(refs[:na], refs[na:2 * na], refs[2 * na], refs[2 * na + 1], gather):
            cp.wait_send()
            cp.wait_recv()

    outs = pl.pallas_call(
        body, name=name,
        out_shape=tuple(pltpu.HBM(g.shape, g.dtype) for g in list(g_thru) + list(land_thru)),
        in_specs=[_HBM] * (2 * na) + [_SEM, _SEM, pl.BlockSpec(memory_space=pl.ANY)],
        out_specs=tuple([_HBM] * (2 * na)),
        input_output_aliases={i: i for i in range(2 * na)},
        compiler_params=pltpu.CompilerParams(has_side_effects=_EFFECT),
    )(*g_thru, *land_thru, send_sems, recv_sems, after)
    return outs[na:]


def _sum8(r, name):
    _, R, C = r.shape
    tr = _tile(R, (256, 160, 128, 64, 32, 16, 8))

    def body(r_ref, o_ref):
        acc = r_ref[0].astype(F32)
        for k in range(1, NDEV):
            acc = acc + r_ref[k].astype(F32)
        o_ref[...] = acc

    return pl.pallas_call(
        body, name=name, grid=(R // tr,),
        in_specs=[pl.BlockSpec((NDEV, tr, C), lambda i: (0, i, 0))],
        out_specs=pl.BlockSpec((tr, C), lambda i: (i, 0)),
        out_shape=jax.ShapeDtypeStruct((R, C), F32),
        compiler_params=_params(("parallel",)),
    )(r)


def _adamw(g, w, m, v, name):
    R, C = g.shape
    tr = _tile(R, (256, 160, 128, 64, 32, 16, 8))
    bc1 = 1.0 - ADAM_B1 ** ADAM_STEP
    bc2 = 1.0 - ADAM_B2 ** ADAM_STEP

    def body(g_ref, w_ref, m_ref, v_ref, d_ref, mo_ref, vo_ref):
        gg = g_ref[...]
        mn = ADAM_B1 * m_ref[...] + (1.0 - ADAM_B1) * gg
        vn = ADAM_B2 * v_ref[...] + (1.0 - ADAM_B2) * (gg * gg)
        mh = mn / bc1
        vh = vn / bc2
        d_ref[...] = -ADAM_LR * (mh / (jnp.sqrt(vh) + ADAM_EPS) + ADAM_WD * w_ref[...])
        mo_ref[...] = mn
        vo_ref[...] = vn

    spec = pl.BlockSpec((tr, C), lambda i: (i, 0))
    sh = jax.ShapeDtypeStruct((R, C), F32)
    return pl.pallas_call(
        body, name=name, grid=(R // tr,), in_specs=[spec] * 4, out_specs=[spec] * 3,
        out_shape=[sh] * 3, compiler_params=_params(("parallel",)),
    )(g, w, m, v)


def _ada_fwd(c16, w_sh, b_sh):
    def body(c_ref, w_ref, b_ref, o_ref):
        c = c_ref[...]
        o_ref[...] = _dot(c * _sig(c), w_ref[...], NN) + b_ref[...]

    return pl.pallas_call(
        body, name="ada_fwd", out_shape=jax.ShapeDtypeStruct((16, w_sh.shape[1]), F32),
        compiler_params=pltpu.CompilerParams(vmem_limit_bytes=VMEM_LIMIT),
    )(c16, w_sh, b_sh)


def _ada_bwd(c16, g16, g16_sh, w_sh):
    ncol = w_sh.shape[1]

    def body(c_ref, g_ref, gs_ref, w_ref, dw_ref, db_ref, dc_ref):
        c = c_ref[...]
        s = _sig(c)
        gs = gs_ref[...]
        dw_ref[...] = _dot(c * s, gs, TN)
        db_ref[...] = jnp.broadcast_to(_colsum(g_ref[...]), db_ref.shape)
        odd = lax.broadcasted_iota(jnp.int32, gs.shape, 0) % 2 == 1
        gc = _colsum(jnp.where(odd, gs, 0.0))
        ds = _dot(jnp.broadcast_to(gc, (8, ncol)), w_ref[...], NT)
        c1 = c[1:2, :]
        s1 = s[1:2, :]
        dc_ref[...] = ds * (s1 * (1.0 + c1 * (1.0 - s1)))

    return pl.pallas_call(
        body, name="ada_bwd",
        out_shape=[jax.ShapeDtypeStruct(w_sh.shape, F32),
                   jax.ShapeDtypeStruct((8, g16.shape[1]), F32),
                   jax.ShapeDtypeStruct((8, D), F32)],
        compiler_params=pltpu.CompilerParams(vmem_limit_bytes=VMEM_LIMIT),
    )(c16, g16, g16_sh, w_sh)


def _ln0_fwd(x, ctx, g, b, modx, modc):
    L = x.shape[0]
    nt = L // TL

    def body(x_ref, c_ref, g_ref, b_ref, mx_ref, mc_ref, xn_ref, h_ref):
        isc = pl.program_id(0) == nt
        xin = jnp.where(isc, c_ref[...], x_ref[...])
        sh = jnp.where(isc, mc_ref[0:1, :], mx_ref[0:1, :])
        sc = jnp.where(isc, mc_ref[1:2, :], mx_ref[1:2, :])
        xhat, _ = _ln(xin)
        xn = xhat * g_ref[...] + b_ref[...]
        xn_ref[...] = xn
        h_ref[...] = (xn * (1.0 + sc) + sh).astype(h_ref.dtype)

    return pl.pallas_call(
        body, name="ln0_fwd", grid=(nt + 1,),
        in_specs=[_rtc(D, nt), _cst((TL, D)), _cst((1, D)), _cst((1, D)), _cst((8, D)), _cst((8, D))],
        out_specs=[_rt(D), _rt(D)],
        out_shape=[jax.ShapeDtypeStruct((L + TL, D), F32), jax.ShapeDtypeStruct((L + TL, D), _MXU)],
        compiler_params=_params(("parallel",)),
    )(x, ctx, g, b, modx, modc)


def _xbc_colblk(j):
    return jnp.where(j < 8, OXS // 128 + j, OB // 128 + j - 8)


def _conv_taps(p_ref, r0, first, last):
    main = p_ref[pl.ds(r0, TL), :]
    zero = jnp.zeros((8, main.shape[1]), F32)
    prev = zero if first else p_ref[pl.ds(r0 - 8, 8), :]
    nxt = zero if last else p_ref[pl.ds(r0 + TL, 8), :]
    ext = jnp.concatenate([prev, main, nxt], axis=0)
    n = TL + 16
    return [pltpu.roll(ext, (2 - k) % n, 0)[8:8 + TL] for k in range(5)]


def _seq_chunks(L):
    nt = L // TL
    return [(r * TL, r == 0, r == nt - 1) for r in range(nt)] + [(L, True, True)]


def _conv_fwd(p, conv_w8, conv_b):
    RT = p.shape[0]
    L = RT - TL
    chunks = _seq_chunks(L)

    def body(p_ref, w_ref, b_ref, o_ref):
        w = w_ref[...]
        bias = b_ref[...]
        for r0, first, last in chunks:
            taps = _conv_taps(p_ref, r0, first, last)
            pre = bias + sum(w[k:k + 1, :] * taps[k] for k in range(5))
            o_ref[pl.ds(r0, TL), :] = pre * _sig(pre)

    return pl.pallas_call(
        body, name="conv_fwd", grid=(12,),
        in_specs=[pl.BlockSpec((RT, 128), lambda j: (0, _xbc_colblk(j))),
                  pl.BlockSpec((8, 128), lambda j: (0, j)),
                  pl.BlockSpec((1, 128), lambda j: (0, j))],
        out_specs=pl.BlockSpec((RT, 128), lambda j: (0, j)),
        out_shape=jax.ShapeDtypeStruct((RT, 1536), F32),
        compiler_params=_params(("parallel",)),
    )(p, conv_w8, conv_b)


def _ssd_common(dtraw, dtb, a32, rev):
    dt = _softplus(dtraw + dtb)
    acum = _cumsum_rows(dt * a32, rev)
    ii = lax.broadcasted_iota(jnp.int32, (Q, Q), 0)
    jj = lax.broadcasted_iota(jnp.int32, (Q, Q), 1)
    mask = (ii <= jj) if rev else (ii >= jj)
    return dt, acum, acum.T, dt.T, mask


def _ssd_orders(ncl, ncc):
    nc = ncl + ncc

    def cf(s):
        return jnp.where(s < ncc, ncl + s, s - ncc)

    def cb(s):
        return nc - 1 - s

    return cf, cb


def _ssd_fwd(xbc, p, prm):
    RT = xbc.shape[0]
    nc = RT // Q
    ncc = TL // Q
    cf, cb = _ssd_orders(nc - ncc, ncc)

    def one_dir(x_ref, dt_ref, prm_ref, y_ref, hp_ref, H_ref, d):
        rev = d == 1
        a32 = -jnp.exp(prm_ref[1:2, :])
        dt, acum, acumT, dtT, mask = _ssd_common(dt_ref[...], prm_ref[0:1, :], a32, rev)
        end = 0 if rev else Q - 1
        for g in range(2):
            Bg = x_ref[:, D + g * NS:D + (g + 1) * NS]
            Cg = x_ref[:, D + 2 * NS + g * NS:D + 2 * NS + (g + 1) * NS]
            CB = _dot(Cg, Bg, NT)
            for hh in range(HPG):
                h = g * HPG + hh
                ln = 16 * d + h
                col = acum[:, ln:ln + 1]
                rowv = acumT[ln:ln + 1, :]
                a_end = rowv[:, end:end + 1]
                Lm = jnp.exp(jnp.where(mask, col - rowv, -1e30))
                W = CB * Lm * dtT[ln:ln + 1, :]
                Xh = x_ref[:, h * HP:(h + 1) * HP]
                Hp = H_ref[h * HP:(h + 1) * HP, :]
                y = _dot(W, Xh, NN) + jnp.exp(col) * _dot(Cg, Hp, NT)
                y_ref[:, h * HP:(h + 1) * HP] = y
                dcol = jnp.exp(a_end - col) * dt[:, ln:ln + 1]
                hp_ref[0, h * HP:(h + 1) * HP, :] = Hp
                H_ref[h * HP:(h + 1) * HP, :] = jnp.exp(a_end) * Hp + _dot(Xh * dcol, Bg, TN)

    def body(xf_ref, xb_ref, df_ref, db_ref, prm_ref, yf_ref, yb_ref, hf_ref, hb_ref, Hf, Hb):
        @pl.when(pl.program_id(0) == 0)
        def _():
            Hf[...] = jnp.zeros_like(Hf)
            Hb[...] = jnp.zeros_like(Hb)

        one_dir(xf_ref, df_ref, prm_ref, yf_ref, hf_ref, Hf, 0)
        one_dir(xb_ref, db_ref, prm_ref, yb_ref, hb_ref, Hb, 1)

    ysh = jax.ShapeDtypeStruct((RT, D), F32)
    hsh = jax.ShapeDtypeStruct((nc, NH * HP, NS), F32)
    hspec = pl.BlockSpec((1, NH * HP, NS), lambda s: (s, 0, 0))
    return pl.pallas_call(
        body, name="ssd_fwd", grid=(nc,),
        in_specs=[pl.BlockSpec((Q, 1536), lambda s: (cf(s), 0)),
                  pl.BlockSpec((Q, 1536), lambda s: (cb(s), 0)),
                  pl.BlockSpec((Q, 128), lambda s: (cf(s), ODT // 128)),
                  pl.BlockSpec((Q, 128), lambda s: (cb(s), ODT // 128)),
                  _cst((8, 128))],
        out_specs=[pl.BlockSpec((Q, D), lambda s: (cf(s), 0)),
                   pl.BlockSpec((Q, D), lambda s: (cb(s), 0)), hspec, hspec],
        out_shape=[ysh, ysh, hsh, hsh],
        scratch_shapes=[pltpu.VMEM((NH * HP, NS), F32), pltpu.VMEM((NH * HP, NS), F32)],
        compiler_params=_params(("arbitrary",)),
    )(xbc, xbc, p, p, prm)


def _ssd_bwd(xbc, p, prm, dsk, dyd, hpf, hpb):
    RT = xbc.shape[0]
    nc = RT // Q
    ncc = TL // Q
    ncl = nc - ncc
    cf, cb = _ssd_orders(ncl, ncc)

    def rs(t):
        return nc - 1 - t

    def one_dir(x_ref, dt_ref, prm_ref, dsk_ref, dy_ref, is_ctx, hp_ref, dH_ref,
                dx_ref, ddt_ref, st_ref, d):
        rev = d == 1
        a32 = -jnp.exp(prm_ref[1:2, :])
        dtraw = dt_ref[...]
        dtb = prm_ref[0:1, :]
        dt, acum, acumT, dtT, mask = _ssd_common(dtraw, dtb, a32, rev)
        end = 0 if rev else Q - 1
        lane = lax.broadcasted_iota(jnp.int32, (Q, 128), 1)
        srow = lax.broadcasted_iota(jnp.int32, (Q, 128), 0)
        dyscale = jnp.where(is_ctx, 0.0, 1.0)
        c_dacum = jnp.zeros((Q, 128), F32)
        r_dacum = jnp.zeros((Q, 128), F32)
        c_ddt = jnp.zeros((Q, 128), F32)
        r_ddt = jnp.zeros((Q, 128), F32)
        dskacc = jnp.zeros((1, 128), F32)
        for g in range(2):
            Bg = x_ref[:, D + g * NS:D + (g + 1) * NS]
            Cg = x_ref[:, D + 2 * NS + g * NS:D + 2 * NS + (g + 1) * NS]
            CB = _dot(Cg, Bg, NT)
            dCB = jnp.zeros((Q, Q), F32)
            dBg = jnp.zeros((Q, NS), F32)
            dCg = jnp.zeros((Q, NS), F32)
            for hh in range(HPG):
                h = g * HPG + hh
                ln = 16 * d + h
                hs = slice(h * HP, (h + 1) * HP)
                col = acum[:, ln:ln + 1]
                rowv = acumT[ln:ln + 1, :]
                dtr = dtT[ln:ln + 1, :]
                dtc = dt[:, ln:ln + 1]
                a_end = rowv[:, end:end + 1]
                Lm = jnp.exp(jnp.where(mask, col - rowv, -1e30))
                E = jnp.exp(col)
                ecol = jnp.exp(a_end - col)
                dcol = ecol * dtc
                Xh = x_ref[:, hs]
                dY = dy_ref[:, hs] * dyscale
                Hp = hp_ref[0, hs, :]
                dHn = dH_ref[hs, :]
                W = CB * Lm * dtr
                dW = _dot(dY, Xh, NT)
                Mm = dW * CB * Lm
                T = Mm * dtr
                dCB = dCB + dW * Lm * dtr
                BdH = _dot(Bg, dHn, NT)
                dX = _dot(W, dY, TN) + dcol * BdH
                if d == 0:
                    dX = dX + dY * dsk_ref[:, hs]
                    dskacc = dskacc + jnp.where(lane[0:1, :] == h, _sum11(dY * Xh), 0.0)
                dx_ref[:, hs] = dX
                xb = jnp.sum(Xh * BdH, axis=1, keepdims=True)
                scol = dcol * xb
                G = _dot(dY, Hp, NN)
                dCg = dCg + E * G
                qcol = E * jnp.sum(G * Cg, axis=1, keepdims=True)
                dBg = dBg + _dot(Xh * dcol, dHn, NN)
                dH_ref[hs, :] = jnp.exp(a_end) * dHn + _dot(dY * E, Cg, TN)
                eterm = jnp.exp(a_end) * _sum11(dHn * Hp) + _sum11(scol)
                cvec = jnp.sum(T, axis=1, keepdims=True) + qcol - scol
                cvec = cvec + jnp.where(srow[:, 0:1] == end, eterm, 0.0)
                c_dacum = c_dacum + jnp.where(lane == ln, cvec, 0.0)
                r_dacum = r_dacum - jnp.where(srow == ln, _colsum(T), 0.0)
                c_ddt = c_ddt + jnp.where(lane == ln, ecol * xb, 0.0)
                r_ddt = r_ddt + jnp.where(srow == ln, _colsum(Mm), 0.0)
            dBg = dBg + _dot(dCB, Cg, TN)
            dCg = dCg + _dot(dCB, Bg, NN)
            dx_ref[:, D + g * NS:D + (g + 1) * NS] = dBg
            dx_ref[:, D + 2 * NS + g * NS:D + 2 * NS + (g + 1) * NS] = dCg
        dacum = c_dacum + r_dacum.T
        da = _cumsum_rows(dacum, not rev)
        mine = (lane >= 16 * d) & (lane < 16 * d + 16)
        ddt = jnp.where(mine, c_ddt + r_ddt.T + da * a32, 0.0)
        ddt_ref[...] = ddt * _sig(dtraw + dtb)
        st_ref[0:1, :] += _colsum(jnp.where(mine, da * dt, 0.0))
        if d == 0:
            st_ref[1:2, :] += dskacc

    def body(xf_ref, xb_ref, df_ref, db_ref, prm_ref, dsk_ref, dyf_ref, dyb_ref, hf_ref, hb_ref,
             dxf_ref, dxb_ref, ddf_ref, ddb_ref, st_ref, dHf, dHb):
        t = pl.program_id(0)

        @pl.when(t == 0)
        def _():
            dHf[...] = jnp.zeros_like(dHf)
            dHb[...] = jnp.zeros_like(dHb)
            st_ref[...] = jnp.zeros_like(st_ref)

        s = rs(t)
        one_dir(xf_ref, df_ref, prm_ref, dsk_ref, dyf_ref, cf(s) >= ncl, hf_ref, dHf,
                dxf_ref, ddf_ref, st_ref, 0)
        one_dir(xb_ref, db_ref, prm_ref, dsk_ref, dyb_ref, cb(s) >= ncl, hb_ref, dHb,
                dxb_ref, ddb_ref, st_ref, 1)

        @pl.when(t == nc - 1)
        def _():
            st_ref[0:1, :] = -jnp.exp(prm_ref[1:2, :]) * st_ref[0:1, :]

    def lat(c):
        return jnp.minimum(c, ncl - 1)

    xsh = jax.ShapeDtypeStruct((RT, 1536), F32)
    dsh = jax.ShapeDtypeStruct((RT, 128), F32)
    hspec = pl.BlockSpec((1, NH * HP, NS), lambda t: (rs(t), 0, 0))
    return pl.pallas_call(
        body, name="ssd_bwd", grid=(nc,),
        in_specs=[pl.BlockSpec((Q, 1536), lambda t: (cf(rs(t)), 0)),
                  pl.BlockSpec((Q, 1536), lambda t: (cb(rs(t)), 0)),
                  pl.BlockSpec((Q, 128), lambda t: (cf(rs(t)), ODT // 128)),
                  pl.BlockSpec((Q, 128), lambda t: (cb(rs(t)), ODT // 128)),
                  _cst((8, 128)), _cst((1, D)),
                  pl.BlockSpec((Q, D), lambda t: (lat(cf(rs(t))), 0)),
                  pl.BlockSpec((Q, D), lambda t: (lat(cb(rs(t))), 0)),
                  hspec, hspec],
        out_specs=[pl.BlockSpec((Q, 1536), lambda t: (cf(rs(t)), 0)),
                   pl.BlockSpec((Q, 1536), lambda t: (cb(rs(t)), 0)),
                   pl.BlockSpec((Q, 128), lambda t: (cf(rs(t)), 0)),
                   pl.BlockSpec((Q, 128), lambda t: (cb(rs(t)), 0)),
                   _cst((8, 128))],
        out_shape=[xsh, xsh, dsh, dsh, jax.ShapeDtypeStruct((8, 128), F32)],
        scratch_shapes=[pltpu.VMEM((NH * HP, NS), F32), pltpu.VMEM((NH * HP, NS), F32)],
        compiler_params=_params(("arbitrary",)),
    )(xbc, xbc, p, p, prm, dsk, dyd, dyd, hpf, hpb)


def _lane_bcast(v, ln):
    return jnp.broadcast_to(v[:, ln:ln + 1], v.shape)


def _halves(v, lo, axis):
    return jnp.concatenate([jnp.where(lo, v, 0.0), jnp.where(lo, 0.0, v)], axis=axis)


def _ssd2_fwd(xbc, p, prm):
    RT = xbc.shape[0]
    nc = RT // Q
    ncc = TL // Q
    cf, cb = _ssd_orders(nc - ncc, ncc)

    def one_dir(x_ref, dt_ref, prm_ref, y_ref, hp_ref, HT_ref, d):
        rev = d == 1
        a32 = -jnp.exp(prm_ref[1:2, :])
        dt, acum, acumT, dtT, mask = _ssd_common(dt_ref[...], prm_ref[0:1, :], a32, rev)
        end = 0 if rev else Q - 1
        lo = lax.broadcasted_iota(jnp.int32, (Q, 128), 1) < HP
        for g in range(2):
            Bg = x_ref[:, D + g * NS:D + (g + 1) * NS]
            Cg = x_ref[:, D + 2 * NS + g * NS:D + 2 * NS + (g + 1) * NS]
            CB = _dot(Cg, Bg, NT)
            xds, svs = [], []
            for q in range(HPG // 2):
                pi = g * (HPG // 2) + q
                ps = slice(pi * 128, (pi + 1) * 128)
                Xp = x_ref[:, ps]
                HTp = HT_ref[:, ps]
                lhs, dcs, sv = [], [], []
                ces = []
                for h in (2 * pi, 2 * pi + 1):
                    ln = 16 * d + h
                    colB = _lane_bcast(acum, ln)
                    rowv = acumT[ln:ln + 1, :]
                    aend = colB[end:end + 1, :]
                    Lm = jnp.exp(jnp.where(mask, colB - rowv, -1e30))
                    lhs.append(CB * Lm * dtT[ln:ln + 1, :])
                    ces.append(Cg * jnp.exp(colB))
                    dcs.append(jnp.exp(aend - colB) * _lane_bcast(dt, ln))
                    sv.append(jnp.exp(aend))
                lhs = jnp.concatenate(lhs + ces, axis=1)
                rhs = jnp.concatenate([_halves(Xp, lo, 0), _halves(HTp, lo, 0)], axis=0)
                y_ref[:, ps] = _dot(lhs, rhs, NN)
                xds.append(Xp * jnp.where(lo, dcs[0], dcs[1]))
                svs.append(jnp.where(lo[0:1, :], sv[0], sv[1]))
            gs = slice(g * 512, (g + 1) * 512)
            HTg = HT_ref[:, gs]
            hp_ref[0, :, gs] = HTg
            st = _dot(Bg.T, jnp.concatenate(xds, axis=1), NN)
            HT_ref[:, gs] = jnp.concatenate(svs, axis=1) * HTg + st

    def body(xf_ref, xb_ref, df_ref, db_ref, prm_ref, yf_ref, yb_ref, hf_ref, hb_ref, Hf, Hb):
        @pl.when(pl.program_id(0) == 0)
        def _():
            Hf[...] = jnp.zeros_like(Hf)
            Hb[...] = jnp.zeros_like(Hb)

        one_dir(xf_ref, df_ref, prm_ref, yf_ref, hf_ref, Hf, 0)
        one_dir(xb_ref, db_ref, prm_ref, yb_ref, hb_ref, Hb, 1)

    ysh = jax.ShapeDtypeStruct((RT, D), F32)
    hsh = jax.ShapeDtypeStruct((nc, NS, NH * HP), F32)
    hspec = pl.BlockSpec((1, NS, NH * HP), lambda s: (s, 0, 0))
    return pl.pallas_call(
        body, name="ssd_fwd", grid=(nc,),
        in_specs=[pl.BlockSpec((Q, 1536), lambda s: (cf(s), 0)),
                  pl.BlockSpec((Q, 1536), lambda s: (cb(s), 0)),
                  pl.BlockSpec((Q, 128), lambda s: (cf(s), ODT // 128)),
                  pl.BlockSpec((Q, 128), lambda s: (cb(s), ODT // 128)),
                  _cst((8, 128))],
        out_specs=[pl.BlockSpec((Q, D), lambda s: (cf(s), 0)),
                   pl.BlockSpec((Q, D), lambda s: (cb(s), 0)), hspec, hspec],
        out_shape=[ysh, ysh, hsh, hsh],
        scratch_shapes=[pltpu.VMEM((NS, NH * HP), F32), pltpu.VMEM((NS, NH * HP), F32)],
        compiler_params=_params(("arbitrary",)),
    )(xbc, xbc, p, p, prm)


def _ssd2_bwd(xbc, p, prm, dsk, dyd, hpf, hpb):
    RT = xbc.shape[0]
    nc = RT // Q
    ncc = TL // Q
    ncl = nc - ncc
    cf, cb = _ssd_orders(ncl, ncc)

    def rs(t):
        return nc - 1 - t

    def one_dir(x_ref, dt_ref, prm_ref, dsk_ref, dy_ref, is_ctx, hp_ref, dHT_ref,
                dx_ref, ddt_ref, st_ref, d):
        rev = d == 1
        a32 = -jnp.exp(prm_ref[1:2, :])
        dtraw = dt_ref[...]
        dtb = prm_ref[0:1, :]
        dt, acum, acumT, _, _ = _ssd_common(dtraw, dtb, a32, rev)
        end = 0 if rev else Q - 1
        lane = lax.broadcasted_iota(jnp.int32, (Q, 128), 1)
        srow = lax.broadcasted_iota(jnp.int32, (Q, 128), 0)
        maskT = (lane <= srow) if rev else (lane >= srow)
        lo = lane < HP
        lo1 = lo[0:1, :]
        dyscale = jnp.where(is_ctx, 0.0, 1.0)
        c_dacum = jnp.zeros((Q, 128), F32)
        r_dacum = jnp.zeros((Q, 128), F32)
        c_ddt = jnp.zeros((Q, 128), F32)
        dskacc = jnp.zeros((1, 128), F32)
        for g in range(2):
            gs = slice(g * 512, (g + 1) * 512)
            Bg = x_ref[:, D + g * NS:D + (g + 1) * NS]
            Cg = x_ref[:, D + 2 * NS + g * NS:D + 2 * NS + (g + 1) * NS]
            CBT = _dot(Bg, Cg, NT)
            HTg = hp_ref[0, :, gs]
            dHTg = dHT_ref[:, gs]
            BdHg = _dot(Bg, dHTg, NN)
            dCBT = jnp.zeros((Q, Q), F32)
            dCg = jnp.zeros((Q, NS), F32)
            xds, dyes, svs = [], [], []
            for q in range(HPG // 2):
                pi = g * (HPG // 2) + q
                ps = slice(pi * 128, (pi + 1) * 128)
                qs = slice(q * 128, (q + 1) * 128)
                Xp = x_ref[:, ps]
                dYp = dy_ref[:, ps] * dyscale
                HTp = HTg[:, qs]
                BdHp = BdHg[:, qs]
                dY2 = _halves(dYp, lo, 0)
                dWT2 = _dot(_halves(Xp, lo, 0), dYp.T, NN)
                G2 = _dot(dY2, HTp, NT)
                XB = Xp * BdHp
                hh = _colsum(dHTg[:, qs] * HTp)
                yx = _colsum(dYp * Xp)
                wts, dcs, ebs, sv = [], [], [], []
                for k, h in enumerate((2 * pi, 2 * pi + 1)):
                    ln = 16 * d + h
                    half = lo if k == 0 else jnp.logical_not(lo)
                    half1 = half[0:1, :]
                    colB = _lane_bcast(acum, ln)
                    dtcB = _lane_bcast(dt, ln)
                    rowv = acumT[ln:ln + 1, :]
                    aend = colB[end:end + 1, :]
                    LmT = jnp.exp(jnp.where(maskT, rowv - colB, -1e30))
                    WT = CBT * LmT * dtcB
                    dWT = dWT2[k * Q:(k + 1) * Q, :]
                    U = dWT * LmT
                    MT = U * CBT
                    rM = jnp.sum(MT, axis=1, keepdims=True)
                    rT = _colsum(MT * dtcB)
                    dCBT = dCBT + U * dtcB
                    ecol = jnp.exp(aend - colB)
                    EB = jnp.exp(colB)
                    Gk = G2[k * Q:(k + 1) * Q, :]
                    dCg = dCg + EB * Gk
                    qcol = jnp.sum(EB * Gk * Cg, axis=1, keepdims=True)
                    xb = jnp.sum(jnp.where(half, XB, 0.0), axis=1, keepdims=True)
                    e1 = ecol[:, 0:1]
                    dt1 = dtcB[:, 0:1]
                    scol = e1 * dt1 * xb
                    sA = jnp.exp(aend)
                    eterm = sA[:, 0:1] * jnp.sum(jnp.where(half1, hh, 0.0), axis=1, keepdims=True) \
                        + _colsum(scol)
                    cvec = qcol - dt1 * rM - scol + jnp.where(srow[:, 0:1] == end, eterm, 0.0)
                    c_dacum = jnp.where(lane == ln, cvec, c_dacum)
                    r_dacum = jnp.where(srow == ln, rT, r_dacum)
                    c_ddt = jnp.where(lane == ln, rM + e1 * xb, c_ddt)
                    if d == 0:
                        dskacc = dskacc + jnp.where(
                            lane[0:1, :] == h, jnp.sum(jnp.where(half1, yx, 0.0), axis=1, keepdims=True), 0.0)
                    wts.append(WT)
                    dcs.append(ecol * dtcB)
                    ebs.append(EB)
                    sv.append(sA)
                dcp = jnp.where(lo, dcs[0], dcs[1])
                dX = _dot(jnp.concatenate(wts, axis=1), dY2, NN) + dcp * BdHp
                if d == 0:
                    dX = dX + dYp * dsk_ref[:, ps]
                dx_ref[:, ps] = dX
                xds.append(Xp * dcp)
                dyes.append(dYp * jnp.where(lo, ebs[0], ebs[1]))
                svs.append(jnp.where(lo1, sv[0], sv[1]))
            dx_ref[:, D + g * NS:D + (g + 1) * NS] = (
                _dot(jnp.concatenate(xds, axis=1), dHTg, NT) + _dot(dCBT, Cg, NN))
            dx_ref[:, D + 2 * NS + g * NS:D + 2 * NS + (g + 1) * NS] = dCg + _dot(dCBT, Bg, TN)
            dHT_ref[:, gs] = (jnp.concatenate(svs, axis=1) * dHTg
                              + _dot(Cg.T, jnp.concatenate(dyes, axis=1), NN))
        dacum = c_dacum + r_dacum.T
        da = _cumsum_rows(dacum, not rev)
        mine = (lane >= 16 * d) & (lane < 16 * d + 16)
        ddt = jnp.where(mine, c_ddt + da * a32, 0.0)
        ddt_ref[...] = ddt * _sig(dtraw + dtb)
        st_ref[0:1, :] += _colsum(jnp.where(mine, da * dt, 0.0))
        if d == 0:
            st_ref[1:2, :] += dskacc

    def body(xf_ref, xb_ref, df_ref, db_ref, prm_ref, dsk_ref, dyf_ref, dyb_ref, hf_ref, hb_ref,
             dxf_ref, dxb_ref, ddf_ref, ddb_ref, st_ref, dHf, dHb):
        t = pl.program_id(0)

        @pl.when(t == 0)
        def _():
            dHf[...] = jnp.zeros_like(dHf)
            dHb[...] = jnp.zeros_like(dHb)
            st_ref[...] = jnp.zeros_like(st_ref)

        s = rs(t)
        one_dir(xf_ref, df_ref, prm_ref, dsk_ref, dyf_ref, cf(s) >= ncl, hf_ref, dHf,
                dxf_ref, ddf_ref, st_ref, 0)
        one_dir(xb_ref, db_ref, prm_ref, dsk_ref, dyb_ref, cb(s) >= ncl, hb_ref, dHb,
                dxb_ref, ddb_ref, st_ref, 1)

        @pl.when(t == nc - 1)
        def _():
            st_ref[0:1, :] = -jnp.exp(prm_ref[1:2, :]) * st_ref[0:1, :]

    def lat(c):
        return jnp.minimum(c, ncl - 1)

    xsh = jax.ShapeDtypeStruct((RT, 1536), F32)
    dsh = jax.ShapeDtypeStruct((RT, 128), F32)
    hspec = pl.BlockSpec((1, NS, NH * HP), lambda t: (rs(t), 0, 0))
    return pl.pallas_call(
        body, name="ssd_bwd", grid=(nc,),
        in_specs=[pl.BlockSpec((Q, 1536), lambda t: (cf(rs(t)), 0)),
                  pl.BlockSpec((Q, 1536), lambda t: (cb(rs(t)), 0)),
                  pl.BlockSpec((Q, 128), lambda t: (cf(rs(t)), ODT // 128)),
                  pl.BlockSpec((Q, 128), lambda t: (cb(rs(t)), ODT // 128)),
                  _cst((8, 128)), _cst((1, D)),
                  pl.BlockSpec((Q, D), lambda t: (lat(cf(rs(t))), 0)),
                  pl.BlockSpec((Q, D), lambda t: (lat(cb(rs(t))), 0)),
                  hspec, hspec],
        out_specs=[pl.BlockSpec((Q, 1536), lambda t: (cf(rs(t)), 0)),
                   pl.BlockSpec((Q, 1536), lambda t: (cb(rs(t)), 0)),
                   pl.BlockSpec((Q, 128), lambda t: (cf(rs(t)), 0)),
                   pl.BlockSpec((Q, 128), lambda t: (cb(rs(t)), 0)),
                   _cst((8, 128))],
        out_shape=[xsh, xsh, dsh, dsh, jax.ShapeDtypeStruct((8, 128), F32)],
        scratch_shapes=[pltpu.VMEM((NS, NH * HP), F32), pltpu.VMEM((NS, NH * HP), F32)],
        compiler_params=_params(("arbitrary",)),
    )(xbc, xbc, p, p, prm, dsk, dyd, dyd, hpf, hpb)


RB = 32


def _ssd3_bwd(xbc, p, prm, dsk, dyd, hpf, hpb):
    RT = xbc.shape[0]
    nc = RT // Q
    ncc = TL // Q
    ncl = nc - ncc
    cf, cb = _ssd_orders(ncl, ncc)
    npair = HPG // 2

    def rs(t):
        return nc - 1 - t

    def one_dir(x_ref, dt_ref, prm_ref, dsk_ref, dy_ref, is_ctx, hp_ref, dHT_ref,
                dx_ref, ddt_ref, st_ref, s_dwt, s_g, s_wt, s_xd, s_dye, s_dcbt, s_dcg, s_cd, s_cdt, d):
        rev = d == 1
        a32 = -jnp.exp(prm_ref[1:2, :])
        dtraw = dt_ref[...]
        dtb = prm_ref[0:1, :]
        dt, acum, acumT, _, _ = _ssd_common(dtraw, dtb, a32, rev)
        end = 0 if rev else Q - 1
        lane = lax.broadcasted_iota(jnp.int32, (RB, 128), 1)
        srow0 = lax.broadcasted_iota(jnp.int32, (RB, 128), 0)
        lo = lane < HP
        lo1 = lo[0:1, :]
        lane1 = lane[0:1, :]
        dyscale = jnp.where(is_ctx, 0.0, 1.0)
        aend_row = acum[end:end + 1, :]
        s_cd[...] = jnp.zeros_like(s_cd)
        s_cdt[...] = jnp.zeros_like(s_cdt)
        r_rows = jnp.zeros((Q, 128), F32)
        srowQ = lax.broadcasted_iota(jnp.int32, (Q, 128), 0)
        dskacc = jnp.zeros((1, 128), F32)
        for g in range(2):
            gs = slice(g * 512, (g + 1) * 512)
            Bg = x_ref[:, D + g * NS:D + (g + 1) * NS]
            Cg = x_ref[:, D + 2 * NS + g * NS:D + 2 * NS + (g + 1) * NS]
            CBT = _dot(Bg, Cg, NT)
            HTg = hp_ref[0, :, gs]
            dHTg = dHT_ref[:, gs]
            BdHg = _dot(Bg, dHTg, NN)
            hhs, yxs = [], []
            for q in range(npair):
                pi = g * npair + q
                ps = slice(pi * 128, (pi + 1) * 128)
                qs = slice(q * 128, (q + 1) * 128)
                Xp = x_ref[:, ps]
                dYp = dy_ref[:, ps] * dyscale
                s_dwt[q] = _dot(_halves(Xp, lo_full(), 0), dYp.T, NN)
                s_g[q] = _dot(_halves(dYp, lo_full(), 0), HTg[:, qs], NT)
                hhs.append(_colsum(dHTg[:, qs] * HTg[:, qs]))
                yxs.append(_colsum(dYp * Xp))
            rparts = [jnp.zeros((8, 128), F32) for _ in range(HPG)]
            ssum = [jnp.zeros((1, 1), F32) for _ in range(HPG)]
            for rb in range(Q // RB):
                r0 = rb * RB
                rows = slice(r0, r0 + RB)
                srow = srow0 + r0
                maskT = (lane <= srow) if rev else (lane >= srow)
                acum_rb = acum[rows, :]
                dt_rb = dt[rows, :]
                CBT_rb = CBT[rows, :]
                Cg_rb = Cg[rows, :]
                dcbt = jnp.zeros((RB, Q), F32)
                dcg = jnp.zeros((RB, NS), F32)
                cd = s_cd[rows, :]
                cdt = s_cdt[rows, :]
                for q in range(npair):
                    pi = g * npair + q
                    ps = slice(pi * 128, (pi + 1) * 128)
                    qs = slice(q * 128, (q + 1) * 128)
                    Xp = x_ref[rows, ps]
                    dYp = dy_ref[rows, ps] * dyscale
                    BdHp = BdHg[rows, qs]
                    XB = Xp * BdHp
                    dcs, ebs = [], []
                    for k in range(2):
                        hh = 2 * q + k
                        ln = 16 * d + g * HPG + hh
                        half = lo if k == 0 else jnp.logical_not(lo)
                        colB = _lane_bcast(acum_rb, ln)
                        dtcB = _lane_bcast(dt_rb, ln)
                        rowv = acumT[ln:ln + 1, :]
                        aend = _lane_bcast(aend_row, ln)
                        LmT = jnp.exp(jnp.where(maskT, rowv - colB, -1e30))
                        s_wt[q, rows, k * Q:(k + 1) * Q] = (CBT_rb * LmT * dtcB).astype(s_wt.dtype)
                        U = s_dwt[q, k * Q + r0:k * Q + r0 + RB, :] * LmT
                        MT = U * CBT_rb
                        rM = jnp.sum(MT, axis=1, keepdims=True)
                        TT = MT * dtcB
                        rparts[hh] = rparts[hh] + (TT[0:8] + TT[8:16] + TT[16:24] + TT[24:32])
                        dcbt = dcbt + U * dtcB
                        ecol = jnp.exp(aend - colB)
                        EB = jnp.exp(colB)
                        EG = EB * s_g[q, k * Q + r0:k * Q + r0 + RB, :]
                        dcg = dcg + EG
                        qcol = jnp.sum(EG * Cg_rb, axis=1, keepdims=True)
                        xb = jnp.sum(jnp.where(half, XB, 0.0), axis=1, keepdims=True)
                        e1 = ecol[:, 0:1]
                        dt1 = dtcB[:, 0:1]
                        scol = e1 * dt1 * xb
                        ssum[hh] = ssum[hh] + _colsum(scol)
                        cd = jnp.where(lane == ln, qcol - dt1 * rM - scol, cd)
                        cdt = jnp.where(lane == ln, rM + e1 * xb, cdt)
                        dcs.append(ecol * dtcB)
                        ebs.append(EB)
                    dcp = jnp.where(lo, dcs[0], dcs[1])
                    dxo = dcp * BdHp
                    if d == 0:
                        dxo = dxo + dYp * dsk_ref[:, ps]
                    dx_ref[rows, ps] = dxo
                    s_xd[rows, qs] = (Xp * dcp).astype(s_xd.dtype)
                    s_dye[rows, qs] = (dYp * jnp.where(lo, ebs[0], ebs[1])).astype(s_dye.dtype)
                s_dcbt[rows, :] = dcbt
                s_dcg[rows, :] = dcg
                s_cd[rows, :] = cd
                s_cdt[rows, :] = cdt
            erow = jnp.zeros((1, 128), F32)
            svs = []
            for q in range(npair):
                pi = g * npair + q
                ps = slice(pi * 128, (pi + 1) * 128)
                sv = []
                for k in range(2):
                    hh = 2 * q + k
                    h = g * HPG + hh
                    ln = 16 * d + h
                    half1 = lo1 if k == 0 else jnp.logical_not(lo1)
                    sA = jnp.exp(_lane_bcast(aend_row, ln))
                    hsum = jnp.sum(jnp.where(half1, hhs[q], 0.0), axis=1, keepdims=True)
                    erow = erow + jnp.where(lane1 == ln, sA[:, 0:1] * hsum + ssum[hh], 0.0)
                    rp = rparts[hh]
                    r_rows = jnp.where(srowQ == ln, _colsum(rp), r_rows)
                    if d == 0:
                        dskacc = dskacc + jnp.where(
                            lane1 == h, jnp.sum(jnp.where(half1, yxs[q], 0.0), axis=1, keepdims=True), 0.0)
                    sv.append(sA)
                svs.append(jnp.where(lo1, sv[0], sv[1]))
                dY2 = _halves(dy_ref[:, ps] * dyscale, lo_full(), 0)
                dx_ref[:, ps] += _dot(s_wt[q], dY2, NN)
            s_cd[end:end + 1, :] += erow
            dcbt_g = s_dcbt[...]
            dx_ref[:, D + g * NS:D + (g + 1) * NS] = _dot(s_xd[...], dHTg, NT) + _dot(dcbt_g, Cg, NN)
            dx_ref[:, D + 2 * NS + g * NS:D + 2 * NS + (g + 1) * NS] = s_dcg[...] + _dot(dcbt_g, Bg, TN)
            dHT_ref[:, gs] = jnp.concatenate(svs, axis=1) * dHTg + _dot(Cg.T, s_dye[...], NN)
        dacum = s_cd[...] + r_rows.T
        da = _cumsum_rows(dacum, not rev)
        laneQ = lax.broadcasted_iota(jnp.int32, (Q, 128), 1)
        mine = (laneQ >= 16 * d) & (laneQ < 16 * d + 16)
        ddt = jnp.where(mine, s_cdt[...] + da * a32, 0.0)
        ddt_ref[...] = ddt * _sig(dtraw + dtb)
        st_ref[0:1, :] += _colsum(jnp.where(mine, da * dt, 0.0))
        if d == 0:
            st_ref[1:2, :] += dskacc

    def lo_full():
        return lax.broadcasted_iota(jnp.int32, (Q, 128), 1) < HP

    def body(xf_ref, xb_ref, df_ref, db_ref, prm_ref, dsk_ref, dyf_ref, dyb_ref, hf_ref, hb_ref,
             dxf_ref, dxb_ref, ddf_ref, ddb_ref, st_ref, dHf, dHb, *scr):
        t = pl.program_id(0)

        @pl.when(t == 0)
        def _():
            dHf[...] = jnp.zeros_like(dHf)
            dHb[...] = jnp.zeros_like(dHb)
            st_ref[...] = jnp.zeros_like(st_ref)

        s = rs(t)
        one_dir(xf_ref, df_ref, prm_ref, dsk_ref, dyf_ref, cf(s) >= ncl, hf_ref, dHf,
                dxf_ref, ddf_ref, st_ref, *scr, 0)
        one_dir(xb_ref, db_ref, prm_ref, dsk_ref, dyb_ref, cb(s) >= ncl, hb_ref, dHb,
                dxb_ref, ddb_ref, st_ref, *scr, 1)

        @pl.when(t == nc - 1)
        def _():
            st_ref[0:1, :] = -jnp.exp(prm_ref[1:2, :]) * st_ref[0:1, :]

    def lat(c):
        return jnp.minimum(c, ncl - 1)

    xsh = jax.ShapeDtypeStruct((RT, 1536), F32)
    dsh = jax.ShapeDtypeStruct((RT, 128), F32)
    hspec = pl.BlockSpec((1, NS, NH * HP), lambda t: (rs(t), 0, 0))
    return pl.pallas_call(
        body, name="ssd_bwd", grid=(nc,),
        in_specs=[pl.BlockSpec((Q, 1536), lambda t: (cf(rs(t)), 0)),
                  pl.BlockSpec((Q, 1536), lambda t: (cb(rs(t)), 0)),
                  pl.BlockSpec((Q, 128), lambda t: (cf(rs(t)), ODT // 128)),
                  pl.BlockSpec((Q, 128), lambda t: (cb(rs(t)), ODT // 128)),
                  _cst((8, 128)), _cst((1, D)),
                  pl.BlockSpec((Q, D), lambda t: (lat(cf(rs(t))), 0)),
                  pl.BlockSpec((Q, D), lambda t: (lat(cb(rs(t))), 0)),
                  hspec, hspec],
        out_specs=[pl.BlockSpec((Q, 1536), lambda t: (cf(rs(t)), 0)),
                   pl.BlockSpec((Q, 1536), lambda t: (cb(rs(t)), 0)),
                   pl.BlockSpec((Q, 128), lambda t: (cf(rs(t)), 0)),
                   pl.BlockSpec((Q, 128), lambda t: (cb(rs(t)), 0)),
                   _cst((8, 128))],
        out_shape=[xsh, xsh, dsh, dsh, jax.ShapeDtypeStruct((8, 128), F32)],
        scratch_shapes=[pltpu.VMEM((NS, NH * HP), F32), pltpu.VMEM((NS, NH * HP), F32),
                        pltpu.VMEM((npair, 2 * Q, Q), F32), pltpu.VMEM((npair, 2 * Q, NS), F32),
                        pltpu.VMEM((npair, Q, 2 * Q), _MXU), pltpu.VMEM((Q, 512), _MXU),
                        pltpu.VMEM((Q, 512), _MXU), pltpu.VMEM((Q, Q), F32), pltpu.VMEM((Q, NS), F32),
                        pltpu.VMEM((Q, 128), F32), pltpu.VMEM((Q, 128), F32)],
        compiler_params=_params(("arbitrary",)),
    )(xbc, xbc, p, p, prm, dsk, dyd, dyd, hpf, hpb)


def _mix_fwd_vals(yf, yb, z, xs, u, v, dsk, sg, gg, gb):
    y = yf + yb + xs * dsk
    sz = _sig(z)
    hh = y * z * sz
    r = lax.rsqrt(jnp.mean(hh * hh, axis=-1, keepdims=True) + EPS)
    nh = hh * r
    ug, tu = _gelu(u)
    vg, tv = _gelu(v)
    vhat, vrstd = _ln(vg)
    vn = vhat * gg + gb
    return y, sz, r, nh, ug, tu, vg, tv, vhat, vrstd, vn


def _mix_fwd(yf, yb, p, xbc, dsk, sg, gg, gb, ws, bsT):
    L = yf.shape[0] - TL
    nt = L // TL

    def body(yf_ref, yb_ref, z_ref, xs_ref, u_ref, v_ref, dsk_ref, sg_ref, gg_ref, gb_ref,
             ws_ref, bs_ref, ys_ref, ym_ref):
        _, _, _, nh, ug, _, _, _, _, _, vn = _mix_fwd_vals(
            yf_ref[...], yb_ref[...], z_ref[...], xs_ref[...], u_ref[...], v_ref[...],
            dsk_ref[...], sg_ref[...], gg_ref[...], gb_ref[...])
        ys_ref[...] = (nh * sg_ref[...]).astype(ys_ref.dtype)
        for n in range(TL // Q):
            rs_ = slice(n * Q, (n + 1) * Q)
            for g in range(8):
                cs = slice(g * 128, (g + 1) * 128)
                mixed = _dot(ws_ref[g], vn[rs_, cs], NN) + bs_ref[:, g:g + 1]
                ym_ref[rs_, cs] = (ug[rs_, cs] * mixed).astype(ym_ref.dtype)

    return pl.pallas_call(
        body, name="mix_fwd", grid=(nt,),
        in_specs=[_rt(D), _rt(D), _rt(D, OZ // D), _rt(D, 0), _rt(D, OU // D), _rt(D, OV // D),
                  _cst((1, D)), _cst((1, D)), _cst((1, D)), _cst((1, D)),
                  _cst((8, 128, 128)), _cst((128, 128))],
        out_specs=[_rt(D), _rt(D)],
        out_shape=[jax.ShapeDtypeStruct((L, D), _MXU), jax.ShapeDtypeStruct((L, D), _MXU)],
        compiler_params=_params(("parallel",)),
    )(yf, yb, p, xbc, p, p, dsk, sg, gg, gb, ws, bsT)


def _mix_bwd(dys, dym, yf, yb, p, xbc, dp, dsk, sg, gg, gb, ws, bsT):
    L = dys.shape[0]
    nt = L // TL

    def body(dys_ref, dym_ref, yf_ref, yb_ref, z_ref, xs_ref, u_ref, v_ref, dsk_ref, sg_ref,
             gg_ref, gb_ref, ws_ref, bs_ref, dp_any, dzuv_ref, dy_ref, st_ref,
             dws_ref, dbs_ref, dvn_s):
        del dp_any
        dz_ref = dzuv_ref.at[:, OZ:OZ + D]
        du_ref = dzuv_ref.at[:, OU:OU + D]
        dv_ref = dzuv_ref.at[:, OV:OV + D]

        @pl.when(pl.program_id(0) == 0)
        def _():
            st_ref[...] = jnp.zeros_like(st_ref)
            dws_ref[...] = jnp.zeros_like(dws_ref)
            dbs_ref[...] = jnp.zeros_like(dbs_ref)

        z = z_ref[...]
        u = u_ref[...]
        v = v_ref[...]
        y, sz, r, nh, ug, tu, vg, tv, vhat, vrstd, vn = _mix_fwd_vals(
            yf_ref[...], yb_ref[...], z, xs_ref[...], u, v,
            dsk_ref[...], sg_ref[...], gg_ref[...], gb_ref[...])
        dys = dys_ref[...]
        st_ref[0:1, :] += _colsum(dys * nh)
        dn = dys * sg_ref[...]
        dhh = r * (dn - nh * jnp.mean(dn * nh, axis=-1, keepdims=True))
        dy_ref[...] = dhh * z * sz
        dz_ref[...] = (dhh * y * (sz * (1.0 + z * (1.0 - sz)))).astype(dz_ref.dtype)
        dym = dym_ref[...]
        lane = lax.broadcasted_iota(jnp.int32, (Q, 128), 1)
        dbs = jnp.zeros((Q, 128), F32)
        gu = _gelu_grad(u, tu)
        for n in range(TL // Q):
            rs_ = slice(n * Q, (n + 1) * Q)
            for g in range(8):
                cs = slice(g * 128, (g + 1) * 128)
                vb = vn[rs_, cs]
                mixed = _dot(ws_ref[g], vb, NN) + bs_ref[:, g:g + 1]
                dyb = dym[rs_, cs]
                dmx = dyb * ug[rs_, cs]
                du_ref[rs_, cs] = (dyb * mixed * gu[rs_, cs]).astype(du_ref.dtype)
                dvn_s[rs_, cs] = _dot(ws_ref[g], dmx, TN)
                dws_ref[g] += _dot(dmx, vb, NT)
                dbs = dbs + jnp.where(lane == g, jnp.sum(dmx, axis=1, keepdims=True), 0.0)
        dbs_ref[...] += dbs
        dvn = dvn_s[...]
        st_ref[1:2, :] += _colsum(dvn * vhat)
        st_ref[2:3, :] += _colsum(dvn)
        dvg = _ln_bwd(dvn * gg_ref[...], vhat, vrstd)
        dv_ref[...] = (dvg * _gelu_grad(v, tv)).astype(dv_ref.dtype)

    outs = pl.pallas_call(
        body, name="mix_bwd", grid=(nt,),
        in_specs=[_rt(D), _rt(D), _rt(D), _rt(D), _rt(D, OZ // D), _rt(D, 0), _rt(D, OU // D),
                  _rt(D, OV // D), _cst((1, D)), _cst((1, D)), _cst((1, D)), _cst((1, D)),
                  _cst((8, 128, 128)), _cst((128, 128)), pl.BlockSpec(memory_space=pl.ANY)],
        out_specs=[_rt(3 * D, 0), _rt(D), _cst((8, D)),
                   _cst((8, 128, 128)), _cst((128, 128))],
        out_shape=[jax.ShapeDtypeStruct(dp.shape, dp.dtype),
                   jax.ShapeDtypeStruct((L, D), F32), jax.ShapeDtypeStruct((8, D), F32),
                   jax.ShapeDtypeStruct((8, 128, 128), F32), jax.ShapeDtypeStruct((128, 128), F32)],
        scratch_shapes=[pltpu.VMEM((TL, D), F32)],
        input_output_aliases={14: 0},
        compiler_params=_params(("arbitrary",)),
    )(dys, dym, yf, yb, p, xbc, p, p, dsk, sg, gg, gb, ws, bsT, dp)
    return outs


def _gate_fwd(a1, a2, p, bg):
    L = a1.shape[0]

    def body(a1_ref, a2_ref, g_ref, bg_ref, m_ref):
        gt = _sig(g_ref[...] + bg_ref[...])
        m_ref[...] = (gt[:, :D] * a1_ref[...] + gt[:, D:] * a2_ref[...]).astype(m_ref.dtype)

    return pl.pallas_call(
        body, name="gate_fwd", grid=(L // TL,),
        in_specs=[_rt(D), _rt(D), _rt(2 * D, OG // (2 * D)), _cst((1, 2 * D))],
        out_specs=_rt(D), out_shape=jax.ShapeDtypeStruct((L, D), _MXU),
        compiler_params=_params(("parallel",)),
    )(a1, a2, p, bg)


def _gate_bwd(dmg, a1, a2, p, bg, dp):
    L = a1.shape[0]

    def body(dm_ref, a1_ref, a2_ref, g_ref, bg_ref, dp_any, dg_ref, da1_ref, da2_ref, st_ref):
        del dp_any

        @pl.when(pl.program_id(0) == 0)
        def _():
            st_ref[...] = jnp.zeros_like(st_ref)

        gt = _sig(g_ref[...] + bg_ref[...])
        g1 = gt[:, :D]
        g2 = gt[:, D:]
        dm = dm_ref[...]
        da1_ref[...] = (dm * g1).astype(da1_ref.dtype)
        da2_ref[...] = (dm * g2).astype(da2_ref.dtype)
        dg1 = dm * a1_ref[...] * g1 * (1.0 - g1)
        dg2 = dm * a2_ref[...] * g2 * (1.0 - g2)
        st_ref[0:1, 0:D] += _colsum(dg1)
        st_ref[0:1, D:2 * D] += _colsum(dg2)
        dg_ref[:, 0:D] = dg1.astype(dg_ref.dtype)
        dg_ref[:, D:2 * D] = dg2.astype(dg_ref.dtype)

    return pl.pallas_call(
        body, name="gate_bwd", grid=(L // TL,),
        in_specs=[_rt(D), _rt(D), _rt(D), _rt(2 * D, OG // (2 * D)), _cst((1, 2 * D)),
                  pl.BlockSpec(memory_space=pl.ANY)],
        out_specs=[_rt(2 * D, OG // (2 * D)), _rt(D), _rt(D), _cst((8, 2 * D))],
        out_shape=[jax.ShapeDtypeStruct(dp.shape, dp.dtype), jax.ShapeDtypeStruct((L, D), _MXU),
                   jax.ShapeDtypeStruct((L, D), _MXU), jax.ShapeDtypeStruct((8, 2 * D), F32)],
        input_output_aliases={5: 0},
        compiler_params=_params(("arbitrary",)),
    )(dmg, a1, a2, p, bg, dp)


def _res1_fwd(xn, out, modx, g, b):
    L = out.shape[0]

    def body(xn_ref, o_ref, mx_ref, g_ref, b_ref, r1_ref, h2_ref):
        r1 = ALPHA * xn_ref[...] + mx_ref[2:3, :] * o_ref[...]
        xhat, _ = _ln(r1)
        x1 = xhat * g_ref[...] + b_ref[...]
        r1_ref[...] = r1
        h2_ref[...] = (x1 * (1.0 + mx_ref[4:5, :]) + mx_ref[3:4, :]).astype(h2_ref.dtype)

    return pl.pallas_call(
        body, name="res1_fwd", grid=(L // TL,),
        in_specs=[_rt(D), _rt(D), _cst((8, D)), _cst((1, D)), _cst((1, D))],
        out_specs=[_rt(D), _rt(D)],
        out_shape=[jax.ShapeDtypeStruct((L, D), F32), jax.ShapeDtypeStruct((L, D), _MXU)],
        compiler_params=_params(("parallel",)),
    )(xn, out, modx, g, b)


HFF = DFF // 2


def _mm_f13_glu(h2, w13i):
    L = h2.shape[0]
    tm = 512

    def body(a_ref, b_ref, f_ref, ff_ref):
        f = _dot(a_ref[...], b_ref[...], NT)
        f_ref[...] = f
        f1 = f[:, :HFF]
        ff_ref[...] = (f1 * _sig(f1) * f[:, HFF:]).astype(ff_ref.dtype)

    return pl.pallas_call(
        body, name="mm_f13_glu", grid=(DFF // HFF, L // tm),
        in_specs=[pl.BlockSpec((tm, D), lambda j, i: (i, 0)), pl.BlockSpec((2 * HFF, D), lambda j, i: (j, 0))],
        out_specs=[pl.BlockSpec((tm, 2 * HFF), lambda j, i: (i, j)), pl.BlockSpec((tm, HFF), lambda j, i: (i, j))],
        out_shape=[jax.ShapeDtypeStruct((L, 2 * DFF), F32), jax.ShapeDtypeStruct((L, DFF), _MXU)],
        compiler_params=_params(("parallel", "parallel")),
    )(h2, w13i)


def _mm_dff_glu(do2, w_ff2_f, f13i):
    L = do2.shape[0]
    tm = 512

    def body(a_ref, b_ref, f_ref, o_ref):
        d = _dot(a_ref[...], b_ref[...], NT)
        f1 = f_ref[:, :HFF]
        s = _sig(f1)
        o_ref[:, :HFF] = (d * f_ref[:, HFF:] * (s * (1.0 + f1 * (1.0 - s)))).astype(o_ref.dtype)
        o_ref[:, HFF:] = (d * f1 * s).astype(o_ref.dtype)

    return pl.pallas_call(
        body, name="mm_dff_glu", grid=(DFF // HFF, L // tm),
        in_specs=[pl.BlockSpec((tm, D), lambda j, i: (i, 0)), pl.BlockSpec((HFF, D), lambda j, i: (j, 0)),
                  pl.BlockSpec((tm, 2 * HFF), lambda j, i: (i, j))],
        out_specs=pl.BlockSpec((tm, 2 * HFF), lambda j, i: (i, j)),
        out_shape=jax.ShapeDtypeStruct((L, 2 * DFF), _MXU),
        compiler_params=_params(("parallel", "parallel")),
    )(do2, w_ff2_f, f13i)


def _glu_fwd(f13):
    L = f13.shape[0]

    def body(f1_ref, f3_ref, o_ref):
        f1 = f1_ref[...]
        o_ref[...] = (f1 * _sig(f1) * f3_ref[...]).astype(o_ref.dtype)

    return pl.pallas_call(
        body, name="glu_fwd", grid=(L // TL,),
        in_specs=[_rt(DFF, 0), _rt(DFF, 1)], out_specs=_rt(DFF),
        out_shape=jax.ShapeDtypeStruct((L, DFF), _MXU),
        compiler_params=_params(("parallel",)),
    )(f13, f13)


def _glu_bwd(dff, f13):
    L = f13.shape[0]

    def body(d_ref, f1_ref, f3_ref, o_ref):
        f1 = f1_ref[...]
        s = _sig(f1)
        d = d_ref[...]
        o_ref[:, 0:DFF] = (d * f3_ref[...] * (s * (1.0 + f1 * (1.0 - s)))).astype(o_ref.dtype)
        o_ref[:, DFF:2 * DFF] = (d * f1 * s).astype(o_ref.dtype)

    return pl.pallas_call(
        body, name="glu_bwd", grid=(L // TL,),
        in_specs=[_rt(DFF), _rt(DFF, 0), _rt(DFF, 1)], out_specs=_rt(2 * DFF),
        out_shape=jax.ShapeDtypeStruct((L, 2 * DFF), _MXU),
        compiler_params=_params(("parallel",)),
    )(dff, f13, f13)


def _res2(r1, o2, tgt, modx, g1, b1, g2, b2):
    L = r1.shape[0]

    def body(r1_ref, o2_ref, t_ref, mx_ref, g1_ref, b1_ref, g2_ref, b2_ref,
             dr2_ref, do2_ref, st_ref, loss_ref):
        @pl.when(pl.program_id(0) == 0)
        def _():
            st_ref[...] = jnp.zeros_like(st_ref)
            loss_ref[...] = jnp.zeros_like(loss_ref)

        xh1, _ = _ln(r1_ref[...])
        x1 = xh1 * g1_ref[...] + b1_ref[...]
        o2 = o2_ref[...]
        g2x = mx_ref[5:6, :]
        xh2, rstd2 = _ln(ALPHA * x1 + g2x * o2)
        err = xh2 * g2_ref[...] + b2_ref[...] - t_ref[...]
        per_tok = jnp.mean(err * err, axis=-1, keepdims=True)
        loss_ref[...] += 0.5 * jnp.sum(per_tok, axis=0, keepdims=True)
        dy = err * (1.0 / D)
        st_ref[0:1, :] += _colsum(dy * xh2)
        st_ref[1:2, :] += _colsum(dy)
        dr2 = _ln_bwd(dy * g2_ref[...], xh2, rstd2)
        st_ref[2:3, :] += _colsum(dr2 * o2)
        dr2_ref[...] = dr2
        do2_ref[...] = (g2x * dr2).astype(do2_ref.dtype)

    return pl.pallas_call(
        body, name="res2", grid=(L // TL,),
        in_specs=[_rt(D), _rt(D), _rt(D), _cst((8, D))] + [_cst((1, D))] * 4,
        out_specs=[_rt(D), _rt(D), _cst((8, D)), _cst((8, 128))],
        out_shape=[jax.ShapeDtypeStruct((L, D), F32), jax.ShapeDtypeStruct((L, D), _MXU),
                   jax.ShapeDtypeStruct((8, D), F32), jax.ShapeDtypeStruct((8, 128), F32)],
        compiler_params=_params(("arbitrary",)),
    )(r1, o2, tgt, modx, g1, b1, g2, b2)


def _res1_bwd(dr2, dh2, r1, out, modx, g1, b1):
    L = r1.shape[0]

    def body(dr2_ref, dh2_ref, r1_ref, o_ref, mx_ref, g_ref, b_ref, dr1_ref, do_ref, st_ref):
        @pl.when(pl.program_id(0) == 0)
        def _():
            st_ref[...] = jnp.zeros_like(st_ref)

        xh1, rstd1 = _ln(r1_ref[...])
        x1 = xh1 * g_ref[...] + b_ref[...]
        dh2 = dh2_ref[...]
        dx1 = ALPHA * dr2_ref[...] + dh2 * (1.0 + mx_ref[4:5, :])
        st_ref[0:1, :] += _colsum(dh2 * x1)
        st_ref[1:2, :] += _colsum(dh2)
        st_ref[2:3, :] += _colsum(dx1 * xh1)
        st_ref[3:4, :] += _colsum(dx1)
        dr1 = _ln_bwd(dx1 * g_ref[...], xh1, rstd1)
        st_ref[4:5, :] += _colsum(dr1 * o_ref[...])
        dr1_ref[...] = dr1
        do_ref[...] = (mx_ref[2:3, :] * dr1).astype(do_ref.dtype)

    return pl.pallas_call(
        body, name="res1_bwd", grid=(L // TL,),
        in_specs=[_rt(D), _rt(D), _rt(D), _rt(D), _cst((8, D)), _cst((1, D)), _cst((1, D))],
        out_specs=[_rt(D), _rt(D), _cst((8, D))],
        out_shape=[jax.ShapeDtypeStruct((L, D), F32), jax.ShapeDtypeStruct((L, D), _MXU),
                   jax.ShapeDtypeStruct((8, D), F32)],
        compiler_params=_params(("arbitrary",)),
    )(dr2, dh2, r1, out, modx, g1, b1)


def _conv_bwd(dxf, dxb, p, conv_w8, conv_b, dp):
    RT = p.shape[0]
    chunks = _seq_chunks(RT - TL)

    def body(df_ref, db_ref, p_ref, w_ref, b_ref, dp_any, o_ref, dw_ref, dbias_ref, dpre_s):
        del dp_any
        w = w_ref[...]
        bias = b_ref[...]
        srow = lax.broadcasted_iota(jnp.int32, (8, 128), 0)
        dwacc = jnp.zeros((8, 128), F32)
        dbacc = jnp.zeros((1, 128), F32)
        for r0, first, last in chunks:
            taps = _conv_taps(p_ref, r0, first, last)
            pre = bias + sum(w[k:k + 1, :] * taps[k] for k in range(5))
            s = _sig(pre)
            dpre = (df_ref[pl.ds(r0, TL), :] + db_ref[pl.ds(r0, TL), :]) * (s * (1.0 + pre * (1.0 - s)))
            dpre_s[pl.ds(r0, TL), :] = dpre
            dbacc = dbacc + _colsum(dpre)
            for k in range(5):
                dwacc = dwacc + jnp.where(srow == k, _colsum(dpre * taps[k]), 0.0)
        for r0, first, last in chunks:
            taps = _conv_taps(dpre_s, r0, first, last)
            dx = sum(w[k:k + 1, :] * taps[4 - k] for k in range(5))
            o_ref[pl.ds(r0, TL), :] = dx.astype(o_ref.dtype)
        dw_ref[...] = dwacc
        dbias_ref[...] = jnp.broadcast_to(dbacc, (8, 128))

    cspec = pl.BlockSpec((RT, 128), lambda j: (0, j))
    wspec = pl.BlockSpec((8, 128), lambda j: (0, j))
    return pl.pallas_call(
        body, name="conv_bwd", grid=(12,),
        in_specs=[cspec, cspec, pl.BlockSpec((RT, 128), lambda j: (0, _xbc_colblk(j))),
                  wspec, pl.BlockSpec((1, 128), lambda j: (0, j)), pl.BlockSpec(memory_space=pl.ANY)],
        out_specs=[pl.BlockSpec((RT, 128), lambda j: (0, _xbc_colblk(j))), wspec, wspec],
        out_shape=[jax.ShapeDtypeStruct(dp.shape, dp.dtype), jax.ShapeDtypeStruct((8, 1536), F32),
                   jax.ShapeDtypeStruct((8, 1536), F32)],
        scratch_shapes=[pltpu.VMEM((RT, 128), F32)],
        input_output_aliases={5: 0},
        compiler_params=_params(("parallel",)),
    )(dxf, dxb, p, conv_w8, conv_b, dp)


def _dt_bwd(ddf, ddb, dp):
    RT = ddf.shape[0]

    def body(f_ref, b_ref, dp_any, o_ref, st_ref):
        del dp_any

        @pl.when(pl.program_id(0) == 0)
        def _():
            st_ref[...] = jnp.zeros_like(st_ref)

        s = f_ref[...] + b_ref[...]
        o_ref[...] = s.astype(o_ref.dtype)
        st_ref[0:1, :] += _colsum(s)

    return pl.pallas_call(
        body, name="dt_bwd", grid=(RT // TL,),
        in_specs=[_rt(128), _rt(128), pl.BlockSpec(memory_space=pl.ANY)],
        out_specs=[_rt(128, ODT // 128), _cst((8, 128))],
        out_shape=[jax.ShapeDtypeStruct(dp.shape, dp.dtype), jax.ShapeDtypeStruct((8, 128), F32)],
        input_output_aliases={2: 0},
        compiler_params=_params(("arbitrary",)),
    )(ddf, ddb, dp)


def _ln0_bwd(dh1, dr1, x, ctx, g, b, modx, modc):
    L = x.shape[0]
    nt = L // TL

    def body(dh_ref, dr1_ref, x_ref, c_ref, g_ref, b_ref, mx_ref, mc_ref, gx_ref, st_ref):
        i = pl.program_id(0)
        isc = i == nt

        @pl.when(i == 0)
        def _():
            st_ref[...] = jnp.zeros_like(st_ref)

        xin = jnp.where(isc, c_ref[...], x_ref[...])
        xhat, rstd = _ln(xin)
        xn = xhat * g_ref[...] + b_ref[...]
        sc = jnp.where(isc, mc_ref[1:2, :], mx_ref[1:2, :])
        dh = dh_ref[...]
        lat = jnp.where(isc, 0.0, 1.0)
        dxn = dh * (1.0 + sc) + (lat * ALPHA) * dr1_ref[...]
        tsh = _colsum(dh)
        tsc = _colsum(dh * xn)
        st_ref[0:1, :] += lat * tsh
        st_ref[1:2, :] += lat * tsc
        st_ref[2:3, :] += (1.0 - lat) * tsh
        st_ref[3:4, :] += (1.0 - lat) * tsc
        st_ref[4:5, :] += _colsum(dxn * xhat)
        st_ref[5:6, :] += _colsum(dxn)

        @pl.when(i < nt)
        def _():
            gx_ref[...] = _ln_bwd(dxn * g_ref[...], xhat, rstd)

    return pl.pallas_call(
        body, name="ln0_bwd", grid=(nt + 1,),
        in_specs=[_rt(D), _rtc(D, nt), _rtc(D, nt), _cst((TL, D)), _cst((1, D)), _cst((1, D)),
                  _cst((8, D)), _cst((8, D))],
        out_specs=[_rtc(D, nt), _cst((8, D))],
        out_shape=[jax.ShapeDtypeStruct((L, D), F32), jax.ShapeDtypeStruct((8, D), F32)],
        compiler_params=_params(("arbitrary",)),
    )(dh1, dr1, x, ctx, g, b, modx, modc)


def _perm_cols(w):
    pad = jnp.zeros((w.shape[0], NPJ - NNAT), w.dtype)
    return jnp.concatenate([w[:, 0:1024], w[:, 2592:3616], w[:, 3616:4640], w[:, 1024:2048],
                            w[:, 4640:6688], w[:, 2048:2304], w[:, 2304:2560], w[:, 2560:2592], pad],
                           axis=1)


SECTIONS = ((0, 1024, OZ), (1024, 2048, OXS), (2048, 2304, OB), (2304, 2560, OC), (2560, 2592, ODT),
            (2592, 3616, OU), (3616, 4640, OV), (4640, 6688, OG))


def _perm_from_blocks(ga):
    n = ga.shape[2]
    pieces = []
    for na, nb, _ in sorted(SECTIONS, key=lambda sec: sec[2]):
        for k in range(NDEV):
            lo, hi = max(na, k * n), min(nb, (k + 1) * n)
            if lo < hi:
                pieces.append(ga[k][:, lo - k * n:hi - k * n])
    pieces.append(jnp.zeros((ga.shape[1], NPJ - NNAT), ga.dtype))
    return jnp.concatenate(pieces, axis=1)


def _blocks_from_perm(gp, n):
    blocks = []
    for k in range(NDEV):
        pieces = []
        for na, nb, po in SECTIONS:
            lo, hi = max(na, k * n), min(nb, (k + 1) * n)
            if lo < hi:
                pieces.append(gp[:, po + lo - na:po + hi - na])
        blocks.append(jnp.concatenate(pieces, axis=1))
    return jnp.stack(blocks)


def _padded(n, row_align):
    unit = row_align * D
    return -(-n // unit) * unit if row_align else n


def _slab(arrs, rows, row_align=0):
    parts = []
    for a in arrs:
        f = a.reshape(-1)
        parts.append(jnp.pad(f, (0, _padded(f.shape[0], row_align) - f.shape[0])))
    flat = jnp.concatenate(parts)
    flat = jnp.pad(flat, (0, rows * D - flat.shape[0]))
    return flat.reshape(rows, D)


def _unslab(slab, shapes, row_align=0):
    out, off = [], 0
    for shp in shapes:
        n = 1
        for s in shp:
            n *= s
        r0, r1 = off // D, -(-(off + n) // D)
        out.append(slab[r0:r1].reshape(-1)[off - r0 * D:off - r0 * D + n].reshape(shp))
        off += _padded(n, row_align)
    return out


def _row(v):
    return v.reshape(1, -1)


def _t(a):
    return jnp.swapaxes(a, 0, 1)


def _pad_rows(a, rows):
    return jnp.pad(a, ((0, rows - a.shape[0]), (0, 0)))


BIG = ["w_in", "w_ssd_proj", "w_gm_proj", "w_out", "w_ff1", "w_ff3", "w_ff2"]
BIG_ROWS = 2304
BIG_ALIGN = 16
REPL = ["c_ctx", "ln0_g", "ln0_b", "b_ada", "conv_b", "dt_bias", "a_log", "d_skip", "ssd_norm_g",
        "gm_norm_g", "gm_norm_b", "w_spatial", "b_spatial", "b_gate", "ln1_g", "ln1_b", "ln2_g", "ln2_b"]
SMALL_ROWS = 160
WEIGHTS = ["c_ctx", "ln0_g", "ln0_b", "w_ada", "b_ada", "w_in", "conv_w", "conv_b", "dt_bias", "a_log",
           "d_skip", "ssd_norm_g", "gm_norm_g", "gm_norm_b", "w_spatial", "b_spatial", "b_gate",
           "w_ssd_proj", "w_gm_proj", "w_out", "ln1_g", "ln1_b", "w_ff1", "w_ff3", "w_ff2", "ln2_g", "ln2_b"]


def kernel(x, c, ctx, c_ctx, ln0_g, ln0_b, w_ada, b_ada, w_in, conv_w, conv_b, dt_bias, a_log, d_skip, ssd_norm_g, gm_norm_g, gm_norm_b, w_spatial, b_spatial, b_gate, w_ssd_proj, w_gm_proj, w_out, ln1_g, ln1_b, w_ff1, w_ff3, w_ff2, ln2_g, ln2_b, loss_target, m_c_ctx, m_ln0_g, m_ln0_b, m_w_ada, m_b_ada, m_w_in, m_conv_w, m_conv_b, m_dt_bias, m_a_log, m_d_skip, m_ssd_norm_g, m_gm_norm_g, m_gm_norm_b, m_w_spatial, m_b_spatial, m_b_gate, m_w_ssd_proj, m_w_gm_proj, m_w_out, m_ln1_g, m_ln1_b, m_w_ff1, m_w_ff3, m_w_ff2, m_ln2_g, m_ln2_b, v_c_ctx, v_ln0_g, v_ln0_b, v_w_ada, v_b_ada, v_w_in, v_conv_w, v_conv_b, v_dt_bias, v_a_log, v_d_skip, v_ssd_norm_g, v_gm_norm_g, v_gm_norm_b, v_w_spatial, v_b_spatial, v_b_gate, v_w_ssd_proj, v_w_gm_proj, v_w_out, v_ln1_g, v_ln1_b, v_w_ff1, v_w_ff3, v_w_ff2, v_ln2_g, v_ln2_b):
    W = dict(c_ctx=c_ctx, ln0_g=ln0_g, ln0_b=ln0_b, w_ada=w_ada, b_ada=b_ada, w_in=w_in, conv_w=conv_w,
             conv_b=conv_b, dt_bias=dt_bias, a_log=a_log, d_skip=d_skip, ssd_norm_g=ssd_norm_g,
             gm_norm_g=gm_norm_g, gm_norm_b=gm_norm_b, w_spatial=w_spatial, b_spatial=b_spatial,
             b_gate=b_gate, w_ssd_proj=w_ssd_proj, w_gm_proj=w_gm_proj, w_out=w_out, ln1_g=ln1_g,
             ln1_b=ln1_b, w_ff1=w_ff1, w_ff3=w_ff3, w_ff2=w_ff2, ln2_g=ln2_g, ln2_b=ln2_b)
    M = dict(c_ctx=m_c_ctx, ln0_g=m_ln0_g, ln0_b=m_ln0_b, w_ada=m_w_ada, b_ada=m_b_ada, w_in=m_w_in,
             conv_w=m_conv_w, conv_b=m_conv_b, dt_bias=m_dt_bias, a_log=m_a_log, d_skip=m_d_skip,
             ssd_norm_g=m_ssd_norm_g, gm_norm_g=m_gm_norm_g, gm_norm_b=m_gm_norm_b,
             w_spatial=m_w_spatial, b_spatial=m_b_spatial, b_gate=m_b_gate, w_ssd_proj=m_w_ssd_proj,
             w_gm_proj=m_w_gm_proj, w_out=m_w_out, ln1_g=m_ln1_g, ln1_b=m_ln1_b, w_ff1=m_w_ff1,
             w_ff3=m_w_ff3, w_ff2=m_w_ff2, ln2_g=m_ln2_g, ln2_b=m_ln2_b)
    V = dict(c_ctx=v_c_ctx, ln0_g=v_ln0_g, ln0_b=v_ln0_b, w_ada=v_w_ada, b_ada=v_b_ada, w_in=v_w_in,
             conv_w=v_conv_w, conv_b=v_conv_b, dt_bias=v_dt_bias, a_log=v_a_log, d_skip=v_d_skip,
             ssd_norm_g=v_ssd_norm_g, gm_norm_g=v_gm_norm_g, gm_norm_b=v_gm_norm_b,
             w_spatial=v_w_spatial, b_spatial=v_b_spatial, b_gate=v_b_gate, w_ssd_proj=v_w_ssd_proj,
             w_gm_proj=v_w_gm_proj, w_out=v_w_out, ln1_g=v_ln1_g, ln1_b=v_ln1_b, w_ff1=v_w_ff1,
             w_ff3=v_w_ff3, w_ff2=v_w_ff2, ln2_g=v_ln2_g, ln2_b=v_ln2_b)

    me = 4 * lax.axis_index("x") + 2 * lax.axis_index("y") + lax.axis_index("c")
    xl, cx, tgt = x[0], ctx[0], loss_target[0]
    L = xl.shape[0]
    assert cx.shape[0] == TL and L % TL == 0
    ada_n = w_ada.shape[2]
    cw_n = conv_w.shape[2]

    small1 = _pad_rows(jnp.concatenate([c, _slab([conv_w[0]], 1)], axis=0), 8)
    g1 = _all_gather(small1, "ag_small")
    c_all = g1[:, 0, :]
    conv_w_full = g1[:, 1, :5 * cw_n].reshape(NDEV, 5, cw_n).transpose(1, 0, 2).reshape(5, NDEV * cw_n)
    sq = w_ssd_proj.shape[1]
    ffr = w_ff2.shape[1]
    ffc = w_ff1.shape[2]
    late = [jnp.concatenate([w_ssd_proj[0], w_gm_proj[0], w_out[0], w_ff2[0]], axis=0).astype(_MXU),
            _t(w_ff1[0]).astype(_MXU), _t(w_ff3[0]).astype(_MXU)]

    c16 = _pad_rows(jnp.concatenate([c_all, _row(c_ctx)], axis=0), 16)
    b_ada_sh = lax.dynamic_slice(b_ada, (0, ada_n * me), (1, ada_n))
    modp = _ada_fwd(c16, w_ada[0], b_ada_sh)
    mod16 = _all_gather(modp, "ag_mod").transpose(1, 0, 2).reshape(16, NDEV * ada_n)

    ga, = _all_gather_multi([w_in[0].astype(_MXU)], "ag_w_in")
    ga, late, mod16 = lax.optimization_barrier((ga, late, mod16))
    lw_send, lw_recv, lw_src, lw_land, lw_token = _exchange_start(late, "ag_late_start", gather=True)
    w_in_p = _perm_from_blocks(ga)
    modx = _pad_rows(lax.dynamic_slice(mod16, (me, 0), (1, 6 * D)).reshape(6, D), 8) + lw_token[0, 0]
    modc = _pad_rows(mod16[8].reshape(6, D), 8)

    g0, b0 = _row(ln0_g), _row(ln0_b)
    xn, h1 = _ln0_fwd(xl, cx, g0, b0, modx, modc)
    p = _mm(h1, w_in_p, "nn", F32, "mm_p")
    conv_w8 = _pad_rows(conv_w_full, 8)
    xbc = _conv_fwd(p, conv_w8, conv_b)
    prm = _pad_rows(jnp.pad(jnp.stack([dt_bias.reshape(32), a_log.reshape(32)]), ((0, 0), (0, 96))), 8)
    yf, yb, hpf, hpb = _ssd2_fwd(xbc, p, prm)
    lw_land = _exchange_wait(lw_send, lw_recv, lw_src, lw_land, yf, "ag_late_wait", gather=True)
    fw_send, fw_recv, lw_land, fw_token = _forward_start(lw_land, "ag_fwd_start")
    dsk = _row(jnp.repeat(d_skip[0, 0] + d_skip[0, 1], HP)) + fw_token[0:1, 0:1]
    ws_m = w_spatial[0].astype(_MXU)
    bsT = jnp.pad(b_spatial[0].T, ((0, 0), (0, 120)))
    mixp = (dsk, ssd_norm_g, gm_norm_g, gm_norm_b, ws_m, bsT)
    yssd, ygm = _mix_fwd(yf, yb, p, xbc, *mixp)
    gb, gc1, gc2 = _forward_wait(fw_send, fw_recv, lw_land, yssd, "ag_fwd_wait")

    def with_own(g, mine, k):
        return jnp.where(me == k, mine, g[k])

    gb = jnp.stack([with_own(gb, late[0], k) for k in range(NDEV)])
    w_ssd_f = gb[:, 0:sq].reshape(NDEV * sq, D)
    w_gm_f = gb[:, sq:2 * sq].reshape(NDEV * sq, D)
    w_out_f = gb[:, 2 * sq:3 * sq].reshape(NDEV * sq, D)
    w_ff2_f = gb[:, 3 * sq:3 * sq + ffr].reshape(NDEV * ffr, D)
    assert HFF == (NDEV // 2) * ffc
    halves = (range(0, NDEV // 2), range(NDEV // 2, NDEV))
    w13i = jnp.concatenate([with_own(g, mine, k) for ks in halves for g, mine in ((gc1, late[1]), (gc2, late[2]))
                            for k in ks], axis=0)
    a1 = _mm(yssd, w_ssd_f, "nn", F32, "mm_a1")
    a2 = _mm(ygm, w_gm_f, "nn", F32, "mm_a2")
    merged = _gate_fwd(a1, a2, p, b_gate)
    out = _mm(merged, w_out_f, "nn", F32, "mm_out")
    r1, h2 = _res1_fwd(xn, out, modx, ln1_g, ln1_b)
    f13, ff = _mm_f13_glu(h2, w13i)
    o2 = _mm(ff, w_ff2_f, "nn", F32, "mm_o2")

    dr2, do2, st2, loss_slab = _res2(r1, o2, tgt, modx, ln1_g, ln1_b, ln2_g, ln2_b)
    loss = lax.psum(loss_slab[0, 0], ("x", "y", "c"))
    df13 = _mm_dff_glu(do2, w_ff2_f, f13)
    dh2 = _mm(df13, w13i, "nn", F32, "mm_dh2")
    dw_ff2 = _mm(ff, do2, "tn", _MXU, "mm_dw_ff2")
    dw13i = _mm(df13, h2, "tn", _MXU, "mm_dw13")

    def owner_blocks(first):
        return jnp.concatenate([dw13i[t * 2 * HFF + first:t * 2 * HFF + first + HFF].reshape(NDEV // 2, ffc, D)
                                for t in range(2)], axis=0)

    xff = [dw_ff2.reshape(NDEV, ffr, D), owner_blocks(0), owner_blocks(HFF)]
    ff_send, ff_recv, ff_src, ff_land, ff_token = _exchange_start(xff, "xchg_ff_start")
    modx = modx + ff_token[0, 0]
    dr1, dout, st1 = _res1_bwd(dr2, dh2, r1, out, modx, ln1_g, ln1_b)
    dmg = _mm(dout, w_out_f, "nt", F32, "mm_dmerged")
    dw_out = _mm(merged, dout, "tn", _MXU, "mm_dw_out")
    dp = jnp.zeros((L + TL, NPJ), _MXU)
    dp, da1, da2, stg = _gate_bwd(dmg, a1, a2, p, b_gate, dp)
    dys = _mm(da1, w_ssd_f, "nt", F32, "mm_dyssd")
    dym = _mm(da2, w_gm_f, "nt", F32, "mm_dygm")
    dw_ssd = _mm(yssd, da1, "tn", _MXU, "mm_dw_ssd")
    dw_gm = _mm(ygm, da2, "tn", _MXU, "mm_dw_gm")
    xsq = [jnp.concatenate([dw_ssd.reshape(NDEV, sq, D), dw_gm.reshape(NDEV, sq, D),
                            dw_out.reshape(NDEV, sq, D)], axis=1)]
    sq_send, sq_recv, sq_src, sq_land, sq_token = _exchange_start(xsq, "xchg_sq_start")
    mixp = (dsk + sq_token[0:1, 0:1],) + mixp[1:]
    dp, dyd, stm, dws, dbsT = _mix_bwd(dys, dym, yf, yb, p, xbc, dp, *mixp)
    dxf, dxb, ddf, ddb, sts = _ssd2_bwd(xbc, p, prm, dsk, dyd, hpf, hpb)
    dp, dcw, dcb = _conv_bwd(dxf, dxb, p, conv_w8, conv_b, dp)
    dp, std = _dt_bwd(ddf, ddb, dp)
    hw = D // 2
    xin_a = [_blocks_from_perm(_mm(h1[:, :hw], dp, "tn", _MXU, "mm_dw_in_a"), w_in.shape[2])]
    ina_send, ina_recv, ina_src, ina_land, ina_token = _exchange_start(xin_a, "xchg_in_a_start")
    h1b, ina_token = lax.optimization_barrier((h1[:, hw:], ina_token))
    xin_b = [_blocks_from_perm(_mm(h1b, dp, "tn", _MXU, "mm_dw_in_b"), w_in.shape[2])]
    inb_send, inb_recv, inb_src, inb_land, inb_token = _exchange_start(xin_b, "xchg_in_b_start")
    dp, inb_token = lax.optimization_barrier((dp, inb_token))
    dh1 = _mm(dp, w_in_p, "nt", F32, "mm_dh1")
    modx = modx + (ina_token[0, 0] + inb_token[0, 0])
    grad_x, st0 = _ln0_bwd(dh1, dr1, xl, cx, g0, b0, modx, modc)

    zero = jnp.zeros((D,), F32)
    dmod = jnp.stack([jnp.concatenate([st0[0], st0[1], st1[4], st1[1], st1[0], st2[2]]),
                      jnp.concatenate([st0[2], st0[3], zero, zero, zero, zero])])
    g16 = _all_gather(_pad_rows(dmod, 8), "ag_dmod")[:, 0:2, :].reshape(16, 6 * D)
    g16_sh = lax.dynamic_slice(g16, (0, ada_n * me), (16, ada_n))
    c16b = jnp.stack([c_all, jnp.broadcast_to(_row(c_ctx), (NDEV, D))], axis=1).reshape(16, D)
    dw_ada, db_ada8, dcc8 = _ada_bwd(c16b, g16, g16_sh, w_ada[0])

    part = dict(
        c_ctx=dcc8[0], ln0_g=st0[4], ln0_b=st0[5], conv_w=dcw[0:5], conv_b=dcb[0],
        dt_bias=std[0, 0:32], a_log=sts[0, 0:32], d_skip=jnp.tile(sts[1, 0:16], 2),
        ssd_norm_g=stm[0], gm_norm_g=stm[1], gm_norm_b=stm[2], w_spatial=dws,
        b_spatial=dbsT[:, 0:8].T, b_gate=stg[0], ln1_g=st1[2], ln1_b=st1[3], ln2_g=st2[0], ln2_b=st2[1])
    pnames = list(part)
    psum8 = _sum8(_all_gather(_slab([part[n] for n in pnames], SMALL_ROWS), "ag_smallgrads"), "sum_smallgrads")
    small = dict(zip(pnames, _unslab(psum8, [part[n].shape for n in pnames])))
    grads = {n: small[n].reshape(W[n].shape) for n in pnames if n != "conv_w"}
    grads["conv_w"] = lax.dynamic_slice(small["conv_w"], (0, cw_n * me), (5, cw_n)).reshape(conv_w.shape)
    grads["b_ada"] = db_ada8[0:1]
    grads["w_ada"] = dw_ada.reshape(w_ada.shape)

    delta, new_m, new_v = {}, {}, {}

    def adam_group(names, rows, tag, align=0):
        shapes = [W[n].shape for n in names]
        outs = _adamw(*[_slab([src[n] for n in names], rows, align) for src in (grads, W, M, V)], tag)
        for res, slab in zip((delta, new_m, new_v), outs):
            for n, a in zip(names, _unslab(slab, shapes, align)):
                res[n] = a

    adam_group(REPL + ["conv_w"], SMALL_ROWS, "adamw_small")
    res = _adamw(grads["w_ada"][0], w_ada[0], m_w_ada[0], v_w_ada[0], "adamw_w_ada")
    delta["w_ada"], new_m["w_ada"], new_v["w_ada"] = [a[None] for a in res]

    rff = _exchange_wait(ff_send, ff_recv, ff_src, ff_land, st0, "xchg_ff_wait")
    rsq = _exchange_wait(sq_send, sq_recv, sq_src, sq_land, rff[0], "xchg_sq_wait")
    rin_a = _exchange_wait(ina_send, ina_recv, ina_src, ina_land, delta["ln2_b"], "xchg_in_a_wait")
    rin_b = _exchange_wait(inb_send, inb_recv, inb_src, inb_land, rin_a[0], "xchg_in_b_wait")
    rin = jnp.concatenate([rin_a[0], rin_b[0]], axis=1)

    def own(blocks):
        return lax.dynamic_index_in_dim(blocks, me, 0, keepdims=False)

    own_in = jnp.concatenate([own(xin_a[0]), own(xin_b[0])], axis=0)
    for n, r8, mine, row0, tr in (
            ("w_ff2", rff[0], own(xff[0]), 0, ffr // 2), ("w_ssd_proj", rsq[0], own(xsq[0]), 0, sq),
            ("w_gm_proj", rsq[0], own(xsq[0]), sq, sq), ("w_out", rsq[0], own(xsq[0]), 2 * sq, sq),
            ("w_in", rin, own_in, 0, 256)):
        res = _adamw_sum(r8, mine, W[n][0], M[n][0], V[n][0], row0, tr, "adamw_" + n)
        grads[n], delta[n], new_m[n], new_v[n] = [a[None] for a in res]
    for n, r8, mine in (("w_ff1", rff[1], own(xff[1])), ("w_ff3", rff[2], own(xff[2]))):
        res = _adamw_sum(r8, mine, _t(W[n][0]), _t(M[n][0]), _t(V[n][0]), 0, ffc // 2, "adamw_" + n)
        grads[n], delta[n], new_m[n], new_v[n] = [_t(a)[None] for a in res]

    return (loss, grad_x[None], *[grads[n] for n in WEIGHTS], *[delta[n] for n in WEIGHTS],
            *[new_m[n] for n in WEIGHTS], *[new_v[n] for n in WEIGHTS])
```

```python
import functools

import jax
import jax.numpy as jnp
from jax import lax
from jax.experimental import pallas as pl
from jax.experimental.pallas import tpu as pltpu

_MXU = jnp.bfloat16
F32 = jnp.float32
D = 1024
TL = 256
Q = 128
NH, HP, NS, HPG = 16, 64, 128, 8
DFF = 2816
ALPHA = 2.0 ** 0.25
EPS = 1e-5
OZ, OU, OV, OXS, OG, OB, OC, ODT, NPJ = 0, 1024, 2048, 3072, 4096, 6144, 6400, 6656, 6912
NNAT = 6688
NDEV = 8
ADAM_LR, ADAM_B1, ADAM_B2, ADAM_EPS, ADAM_WD, ADAM_STEP = 1e-3, 0.9, 0.999, 1e-8, 0.01, 10
VMEM_LIMIT = 48 * 1024 * 1024

NN = ((1,), (0,))
NT = ((1,), (1,))
TN = ((0,), (0,))
MESH = pl.DeviceIdType.MESH


def _dot(a, b, dims):
    return lax.dot_general(a.astype(_MXU), b.astype(_MXU), (dims, ((), ())),
                           preferred_element_type=F32)


def _tile(n, cands):
    for c in cands:
        if n % c == 0:
            return c
    return n


def _divisor_tile(n, cap, mult):
    best = n
    for t in range(mult, min(n, cap) + 1, mult):
        if n % t == 0:
            best = t
    return best


def _params(sem):
    return pltpu.CompilerParams(dimension_semantics=sem, vmem_limit_bytes=VMEM_LIMIT)


def _cst(shape):
    nd = len(shape)
    return pl.BlockSpec(shape, lambda *_: (0,) * nd)


def _rt(w, cb=0, rows=TL):
    return pl.BlockSpec((rows, w), lambda i: (i, cb))


def _rtc(w, nt, cb=0):
    return pl.BlockSpec((TL, w), lambda i: (jnp.minimum(i, nt - 1), cb))


def _sig(x):
    return jax.nn.sigmoid(x)


def _softplus(x):
    return jnp.maximum(x, 0.0) + jnp.log1p(jnp.exp(-jnp.abs(x)))


_G0, _G1 = 0.7978845608028654, 0.044715


def _gelu(x):
    t = jnp.tanh(_G0 * (x + _G1 * x * x * x))
    return 0.5 * x * (1.0 + t), t


def _gelu_grad(x, t):
    return 0.5 * (1.0 + t) + 0.5 * x * (1.0 - t * t) * _G0 * (1.0 + 3.0 * _G1 * x * x)


def _ln(r):
    mu = jnp.mean(r, axis=-1, keepdims=True)
    xc = r - mu
    var = jnp.mean(xc * xc, axis=-1, keepdims=True)
    rstd = lax.rsqrt(var + EPS)
    return xc * rstd, rstd


def _ln_bwd(dyh, xhat, rstd):
    return rstd * (dyh - jnp.mean(dyh, axis=-1, keepdims=True)
                   - xhat * jnp.mean(dyh * xhat, axis=-1, keepdims=True))


def _colsum(v):
    return jnp.sum(v, axis=0, keepdims=True)


def _sum11(v):
    return jnp.sum(jnp.sum(v, axis=1, keepdims=True), axis=0, keepdims=True)


def _cumsum_rows(a, rev):
    n = a.shape[0]
    row = lax.broadcasted_iota(jnp.int32, a.shape, 0)
    s = 1
    while s < n:
        if rev:
            a = a + jnp.where(row < n - s, pltpu.roll(a, n - s, 0), 0.0)
        else:
            a = a + jnp.where(row >= s, pltpu.roll(a, s, 0), 0.0)
        s *= 2
    return a


def _mm(a, b, mode, out_dtype, name):
    if mode == "tn":
        K, M = a.shape
    else:
        M, K = a.shape
    N = b.shape[0] if mode == "nt" else b.shape[1]
    tm = _divisor_tile(M, 1408, 128) if mode == "tn" else _divisor_tile(M, 1088, 16)
    tn = _divisor_tile(N, 1408, 128)
    tk = _divisor_tile(K, 2304, 128)
    nk = K // tk
    dims = {"nn": NN, "nt": NT, "tn": TN}[mode]
    use_acc = nk > 1 and out_dtype != F32

    def body(a_ref, b_ref, o_ref, *acc):
        prod = _dot(a_ref[...], b_ref[...], dims)
        if nk == 1:
            o_ref[...] = prod.astype(o_ref.dtype)
            return
        acc_ref = acc[0] if use_acc else o_ref
        k = pl.program_id(2)

        @pl.when(k == 0)
        def _():
            acc_ref[...] = prod

        if use_acc:
            @pl.when((k > 0) & (k < nk - 1))
            def _():
                acc_ref[...] += prod

            @pl.when(k == nk - 1)
            def _():
                o_ref[...] = (acc_ref[...] + prod).astype(o_ref.dtype)
        else:
            @pl.when(k > 0)
            def _():
                o_ref[...] += prod

    if mode == "tn":
        a_spec = pl.BlockSpec((tk, tm), lambda i, j, k: (k, i))
    else:
        a_spec = pl.BlockSpec((tm, tk), lambda i, j, k: (i, k))
    if mode == "nt":
        b_spec = pl.BlockSpec((tn, tk), lambda i, j, k: (j, k))
    else:
        b_spec = pl.BlockSpec((tk, tn), lambda i, j, k: (k, j))
    return pl.pallas_call(
        body, name=name, grid=(M // tm, N // tn, nk),
        in_specs=[a_spec, b_spec],
        out_specs=pl.BlockSpec((tm, tn), lambda i, j, k: (i, j)),
        out_shape=jax.ShapeDtypeStruct((M, N), out_dtype),
        scratch_shapes=[pltpu.VMEM((tm, tn), F32)] if use_acc else [],
        compiler_params=_params(("parallel", "parallel", "arbitrary")),
    )(a, b)


def _all_gather(x, name):
    def body(x_ref, out_ref, send_sems, recv_sems, local_sem):
        mx, my, mc = lax.axis_index("x"), lax.axis_index("y"), lax.axis_index("c")
        me, sibling = (mx, my, mc), (mx, my, 1 - mc)
        chips = [(1 - mx, my), (mx, 1 - my), (1 - mx, 1 - my)]

        def slot(px, py, pc):
            return out_ref.at[4 * px + 2 * py + pc]

        def copy(k, block, to, src=None):
            return pltpu.make_async_remote_copy(
                src_ref=slot(*block) if src is None else src, dst_ref=slot(*block),
                send_sem=send_sems.at[k], recv_sem=recv_sems.at[k],
                device_id=to, device_id_type=MESH)

        mine = pltpu.make_async_copy(x_ref, slot(*me), local_sem)
        mine.start()
        first = [copy(0, me, sibling, src=x_ref)]
        first += [copy(1 + j, me, (*chip, mc), src=x_ref) for j, chip in enumerate(chips)]
        for cp in first:
            cp.start()
        passed = [copy(4 + j, (*chip, mc), sibling) for j, chip in enumerate(chips)]
        for j, chip in enumerate(chips):
            copy(1 + j, (*chip, mc), me).wait_recv()
            passed[j].start()
        copy(0, sibling, me).wait_recv()
        for j, chip in enumerate(chips):
            copy(4 + j, (*chip, 1 - mc), me).wait_recv()
        for cp in first + passed:
            cp.wait_send()
        mine.wait()

    return pl.pallas_call(
        body, name=name,
        out_shape=jax.ShapeDtypeStruct((NDEV,) + x.shape, x.dtype),
        in_specs=[pl.BlockSpec(memory_space=pl.ANY)],
        out_specs=pl.BlockSpec(memory_space=pl.ANY),
        scratch_shapes=[pltpu.SemaphoreType.DMA((7,)), pltpu.SemaphoreType.DMA((7,)),
                        pltpu.SemaphoreType.DMA],
    )(x)


def _owner_exchange(g, name):
    def body(g_ref, out_ref, send_sems, recv_sems, local_sem):
        mx, my, mc = lax.axis_index("x"), lax.axis_index("y"), lax.axis_index("c")
        local = pltpu.make_async_copy(g_ref.at[4 * mx + 2 * my + mc], out_ref.at[0], local_sem)
        local.start()
        copies = []
        for f in range(1, NDEV):
            px = 1 - mx if (f >> 2) & 1 else mx
            py = 1 - my if (f >> 1) & 1 else my
            pc = 1 - mc if f & 1 else mc
            cp = pltpu.make_async_remote_copy(
                src_ref=g_ref.at[4 * px + 2 * py + pc], dst_ref=out_ref.at[f],
                send_sem=send_sems.at[f - 1], recv_sem=recv_sems.at[f - 1],
                device_id=(px, py, pc), device_id_type=MESH)
            cp.start()
            copies.append(cp)
        for cp in copies:
            cp.wait_recv()
        for cp in copies:
            cp.wait_send()
        local.wait()

    return pl.pallas_call(
        body, name=name,
        out_shape=jax.ShapeDtypeStruct(g.shape, g.dtype),
        in_specs=[pl.BlockSpec(memory_space=pl.ANY)],
        out_specs=pl.BlockSpec(memory_space=pl.ANY),
        scratch_shapes=[pltpu.SemaphoreType.DMA((7,)), pltpu.SemaphoreType.DMA((7,)),
                        pltpu.SemaphoreType.DMA],
    )(g)


def _any_specs(n):
    return [pl.BlockSpec(memory_space=pl.ANY)] * n


def _all_gather_multi(xs, name):
    na = len(xs)

    def body(*refs):
        x_refs, out_refs = refs[:na], refs[na:2 * na]
        send_sems, recv_sems, local_sems = refs[2 * na:]
        mx, my, mc = lax.axis_index("x"), lax.axis_index("y"), lax.axis_index("c")
        me, sibling = (mx, my, mc), (mx, my, 1 - mc)
        chips = [(1 - mx, my), (mx, 1 - my), (1 - mx, 1 - my)]

        def copy(a, k, block, to, src=None):
            slot = out_refs[a].at[4 * block[0] + 2 * block[1] + block[2]]
            return pltpu.make_async_remote_copy(
                src_ref=slot if src is None else src, dst_ref=slot,
                send_sem=send_sems.at[7 * a + k], recv_sem=recv_sems.at[7 * a + k],
                device_id=to, device_id_type=MESH)

        mine = [pltpu.make_async_copy(x_refs[a], out_refs[a].at[4 * mx + 2 * my + mc], local_sems.at[a])
                for a in range(na)]
        for cp in mine:
            cp.start()
        first = []
        for a in range(na):
            first.append(copy(a, 0, me, sibling, src=x_refs[a]))
            first += [copy(a, 1 + j, me, (*chip, mc), src=x_refs[a]) for j, chip in enumerate(chips)]
        for cp in first:
            cp.start()
        passed = []
        for a in range(na):
            for j, chip in enumerate(chips):
                copy(a, 1 + j, (*chip, mc), me).wait_recv()
                fwd = copy(a, 4 + j, (*chip, mc), sibling)
                fwd.start()
                passed.append(fwd)
        for a in range(na):
            copy(a, 0, sibling, me).wait_recv()
            for j, chip in enumerate(chips):
                copy(a, 4 + j, (*chip, 1 - mc), me).wait_recv()
        for cp in first + passed:
            cp.wait_send()
        for cp in mine:
            cp.wait()

    return pl.pallas_call(
        body, name=name,
        out_shape=[jax.ShapeDtypeStruct((NDEV,) + x.shape, x.dtype) for x in xs],
        in_specs=_any_specs(na), out_specs=_any_specs(na),
        scratch_shapes=[pltpu.SemaphoreType.DMA((7 * na,)), pltpu.SemaphoreType.DMA((7 * na,)),
                        pltpu.SemaphoreType.DMA((na,))],
    )(*xs)


def _owner_exchange_multi(gs, name):
    na = len(gs)

    def body(*refs):
        g_refs, out_refs = refs[:na], refs[na:2 * na]
        send_sems, recv_sems, local_sems = refs[2 * na:]
        mx, my, mc = lax.axis_index("x"), lax.axis_index("y"), lax.axis_index("c")
        locals_ = [pltpu.make_async_copy(g_refs[a].at[4 * mx + 2 * my + mc], out_refs[a].at[0], local_sems.at[a])
                   for a in range(na)]
        for cp in locals_:
            cp.start()
        copies = []
        for a in range(na):
            for f in range(1, NDEV):
                px = 1 - mx if (f >> 2) & 1 else mx
                py = 1 - my if (f >> 1) & 1 else my
                pc = 1 - mc if f & 1 else mc
                cp = pltpu.make_async_remote_copy(
                    src_ref=g_refs[a].at[4 * px + 2 * py + pc], dst_ref=out_refs[a].at[f],
                    send_sem=send_sems.at[7 * a + f - 1], recv_sem=recv_sems.at[7 * a + f - 1],
                    device_id=(px, py, pc), device_id_type=MESH)
                cp.start()
                copies.append(cp)
        for cp in copies:
            cp.wait_recv()
        for cp in copies:
            cp.wait_send()
        for cp in locals_:
            cp.wait()

    return pl.pallas_call(
        body, name=name,
        out_shape=[jax.ShapeDtypeStruct(g.shape, g.dtype) for g in gs],
        in_specs=_any_specs(na), out_specs=_any_specs(na),
        scratch_shapes=[pltpu.SemaphoreType.DMA((7 * na,)), pltpu.SemaphoreType.DMA((7 * na,)),
                        pltpu.SemaphoreType.DMA((na,))],
    )(*gs)


def _adamw_sum(r8, own, w, m, v, row0, tr, name):
    R, C = w.shape
    assert row0 % tr == 0
    blk0 = row0 // tr
    bc1 = 1.0 - ADAM_B1 ** ADAM_STEP
    bc2 = 1.0 - ADAM_B2 ** ADAM_STEP

    def body(r_ref, *refs):
        if own is None:
            gg = r_ref[0].astype(F32)
        else:
            gg = refs[0][...].astype(F32)
            refs = refs[1:]
        w_ref, m_ref, v_ref, g_ref, d_ref, mo_ref, vo_ref = refs
        for k in range(1, NDEV):
            gg = gg + r_ref[k].astype(F32)
        mn = ADAM_B1 * m_ref[...] + (1.0 - ADAM_B1) * gg
        vn = ADAM_B2 * v_ref[...] + (1.0 - ADAM_B2) * (gg * gg)
        mh = mn / bc1
        vh = vn / bc2
        g_ref[...] = gg
        d_ref[...] = -ADAM_LR * (mh / (jnp.sqrt(vh) + ADAM_EPS) + ADAM_WD * w_ref[...])
        mo_ref[...] = mn
        vo_ref[...] = vn

    spec = pl.BlockSpec((tr, C), lambda i: (i, 0))
    sh = jax.ShapeDtypeStruct((R, C), F32)
    own_ops = [] if own is None else [own]
    own_specs = [] if own is None else [pl.BlockSpec((tr, C), lambda i: (i + blk0, 0))]
    return pl.pallas_call(
        body, name=name, grid=(R // tr,),
        in_specs=[pl.BlockSpec((NDEV, tr, C), lambda i: (0, i + blk0, 0))] + own_specs + [spec, spec, spec],
        out_specs=[spec] * 4, out_shape=[sh] * 4, compiler_params=_params(("parallel",)),
    )(r8, *own_ops, w, m, v)


_HBM = pl.BlockSpec(memory_space=pltpu.HBM)
_SEM = pl.BlockSpec(memory_space=pltpu.SEMAPHORE)
_EFFECT = pltpu.SideEffectType.DATAFLOW_SIDE_EFFECTING


def _exchange_copies(g_refs, land_refs, send_sems, recv_sems, gather):
    mx, my, mc = lax.axis_index("x"), lax.axis_index("y"), lax.axis_index("c")
    copies = []
    for a in range(len(g_refs)):
        for f in ((1, 2, 4, 6) if gather else range(1, NDEV)):
            px = 1 - mx if (f >> 2) & 1 else mx
            py = 1 - my if (f >> 1) & 1 else my
            pc = 1 - mc if f & 1 else mc
            src = g_refs[a] if gather else g_refs[a].at[4 * px + 2 * py + pc]
            dst = land_refs[a].at[4 * mx + 2 * my + mc] if gather else land_refs[a].at[f]
            copies.append(pltpu.make_async_remote_copy(
                src_ref=src, dst_ref=dst,
                send_sem=send_sems.at[7 * a + f - 1], recv_sem=recv_sems.at[7 * a + f - 1],
                device_id=(px, py, pc), device_id_type=MESH))
    return copies


def _exchange_start(gs, name, gather=False):
    na = len(gs)

    def body(*refs):
        for cp in _exchange_copies(refs[:na], refs[na:2 * na], refs[2 * na], refs[2 * na + 1], gather):
            cp.start()
        refs[-1][...] = jnp.zeros_like(refs[-1])

    hbm = [pltpu.HBM(g.shape, g.dtype) for g in gs]
    land_shapes = [((NDEV,) + g.shape) if gather else g.shape for g in gs]
    lands = [pltpu.with_memory_space_constraint(lax.empty(shp, g.dtype), pltpu.HBM)
             for shp, g in zip(land_shapes, gs)]
    hbm_land = [pltpu.HBM(shp, g.dtype) for shp, g in zip(land_shapes, gs)]
    outs = pl.pallas_call(
        body, name=name,
        out_shape=(pltpu.SemaphoreType.DMA((7 * na,)), pltpu.SemaphoreType.DMA((7 * na,)), *hbm, *hbm_land,
                   jax.ShapeDtypeStruct((8, 128), F32)),
        in_specs=[_HBM] * (2 * na),
        out_specs=(_SEM, _SEM, *([_HBM] * (2 * na)), pl.BlockSpec(memory_space=pltpu.VMEM)),
        input_output_aliases={i: 2 + i for i in range(2 * na)},
        compiler_params=pltpu.CompilerParams(has_side_effects=_EFFECT),
    )(*[pltpu.with_memory_space_constraint(g, pltpu.HBM) for g in gs], *lands)
    return outs[0], outs[1], outs[2:2 + na], outs[2 + na:2 + 2 * na], outs[-1]


def _forward_copies(land_refs, send_sems, recv_sems):
    mx, my, mc = lax.axis_index("x"), lax.axis_index("y"), lax.axis_index("c")
    copies = []
    for a in range(len(land_refs)):
        for j, (fx, fy) in enumerate(((0, 1), (1, 0), (1, 1))):
            px = 1 - mx if fx else mx
            py = 1 - my if fy else my
            blk = land_refs[a].at[4 * px + 2 * py + mc]
            copies.append(pltpu.make_async_remote_copy(
                src_ref=blk, dst_ref=blk, send_sem=send_sems.at[3 * a + j], recv_sem=recv_sems.at[3 * a + j],
                device_id=(mx, my, 1 - mc), device_id_type=MESH))
    return copies


def _forward_start(lands, name):
    na = len(lands)

    def body(*refs):
        for cp in _forward_copies(refs[:na], refs[na], refs[na + 1]):
            cp.start()
        refs[-1][...] = jnp.zeros_like(refs[-1])

    outs = pl.pallas_call(
        body, name=name,
        out_shape=(pltpu.SemaphoreType.DMA((3 * na,)), pltpu.SemaphoreType.DMA((3 * na,)),
                   *[pltpu.HBM(g.shape, g.dtype) for g in lands], jax.ShapeDtypeStruct((8, 128), F32)),
        in_specs=[_HBM] * na,
        out_specs=(_SEM, _SEM, *([_HBM] * na), pl.BlockSpec(memory_space=pltpu.VMEM)),
        input_output_aliases={i: 2 + i for i in range(na)},
        compiler_params=pltpu.CompilerParams(has_side_effects=_EFFECT),
    )(*lands)
    return outs[0], outs[1], outs[2:2 + na], outs[-1]


def _forward_wait(send_sems, recv_sems, lands, after, name):
    na = len(lands)

    def body(*refs):
        for cp in _forward_copies(refs[:na], refs[na], refs[na + 1]):
            cp.wait_send()
            cp.wait_recv()

    return pl.pallas_call(
        body, name=name,
        out_shape=tuple(pltpu.HBM(g.shape, g.dtype) for g in lands),
        in_specs=[_HBM] * na + [_SEM, _SEM, pl.BlockSpec(memory_space=pl.ANY)],
        out_specs=tuple([_HBM] * na),
        input_output_aliases={i: i for i in range(na)},
        compiler_params=pltpu.CompilerParams(has_side_effects=_EFFECT),
    )(*lands, send_sems, recv_sems, after)


def _exchange_wait(send_sems, recv_sems, g_thru, land_thru, after, name, gather=False):
    na = len(g_thru)

    def body(*refs):
        for cp in _exchange_copies(refs[:na], refs[na:2 * na], refs[2 * na], refs[2 * na + 1], gather):
            cp.wait_send()
            cp.wait_recv()

    outs = pl.pallas_call(
        body, name=name,
        out_shape=tuple(pltpu.HBM(g.shape, g.dtype) for g in list(g_thru) + list(land_thru)),
        in_specs=[_HBM] * (2 * na) + [_SEM, _SEM, pl.BlockSpec(memory_space=pl.ANY)],
        out_specs=tuple([_HBM] * (2 * na)),
        input_output_aliases={i: i for i in range(2 * na)},
        compiler_params=pltpu.CompilerParams(has_side_effects=_EFFECT),
    )(*g_thru, *land_thru, send_sems, recv_sems, after)
    return outs[na:]


def _sum8(r, name):
    _, R, C = r.shape
    tr = _tile(R, (256, 160, 128, 64, 32, 16, 8))

    def body(r_ref, o_ref):
        acc = r_ref[0].astype(F32)
        for k in range(1, NDEV):
            acc = acc + r_ref[k].astype(F32)
        o_ref[...] = acc

    return pl.pallas_call(
        body, name=name, grid=(R // tr,),
        in_specs=[pl.BlockSpec((NDEV, tr, C), lambda i: (0, i, 0))],
        out_specs=pl.BlockSpec((tr, C), lambda i: (i, 0)),
        out_shape=jax.ShapeDtypeStruct((R, C), F32),
        compiler_params=_params(("parallel",)),
    )(r)


def _adamw(g, w, m, v, name):
    R, C = g.shape
    tr = _tile(R, (256, 160, 128, 64, 32, 16, 8))
    bc1 = 1.0 - ADAM_B1 ** ADAM_STEP
    bc2 = 1.0 - ADAM_B2 ** ADAM_STEP

    def body(g_ref, w_ref, m_ref, v_ref, d_ref, mo_ref, vo_ref):
        gg = g_ref[...]
        mn = ADAM_B1 * m_ref[...] + (1.0 - ADAM_B1) * gg
        vn = ADAM_B2 * v_ref[...] + (1.0 - ADAM_B2) * (gg * gg)
        mh = mn / bc1
        vh = vn / bc2
        d_ref[...] = -ADAM_LR * (mh / (jnp.sqrt(vh) + ADAM_EPS) + ADAM_WD * w_ref[...])
        mo_ref[...] = mn
        vo_ref[...] = vn

    spec = pl.BlockSpec((tr, C), lambda i: (i, 0))
    sh = jax.ShapeDtypeStruct((R, C), F32)
    return pl.pallas_call(
        body, name=name, grid=(R // tr,), in_specs=[spec] * 4, out_specs=[spec] * 3,
        out_shape=[sh] * 3, compiler_params=_params(("parallel",)),
    )(g, w, m, v)


def _ada_fwd(c16, w_sh, b_sh):
    def body(c_ref, w_ref, b_ref, o_ref):
        c = c_ref[...]
        o_ref[...] = _dot(c * _sig(c), w_ref[...], NN) + b_ref[...]

    return pl.pallas_call(
        body, name="ada_fwd", out_shape=jax.ShapeDtypeStruct((16, w_sh.shape[1]), F32),
        compiler_params=pltpu.CompilerParams(vmem_limit_bytes=VMEM_LIMIT),
    )(c16, w_sh, b_sh)


def _ada_bwd(c16, g16, g16_sh, w_sh):
    ncol = w_sh.shape[1]

    def body(c_ref, g_ref, gs_ref, w_ref, dw_ref, db_ref, dc_ref):
        c = c_ref[...]
        s = _sig(c)
        gs = gs_ref[...]
        dw_ref[...] = _dot(c * s, gs, TN)
        db_ref[...] = jnp.broadcast_to(_colsum(g_ref[...]), db_ref.shape)
        odd = lax.broadcasted_iota(jnp.int32, gs.shape, 0) % 2 == 1
        gc = _colsum(jnp.where(odd, gs, 0.0))
        ds = _dot(jnp.broadcast_to(gc, (8, ncol)), w_ref[...], NT)
        c1 = c[1:2, :]
        s1 = s[1:2, :]
        dc_ref[...] = ds * (s1 * (1.0 + c1 * (1.0 - s1)))

    return pl.pallas_call(
        body, name="ada_bwd",
        out_shape=[jax.ShapeDtypeStruct(w_sh.shape, F32),
                   jax.ShapeDtypeStruct((8, g16.shape[1]), F32),
                   jax.ShapeDtypeStruct((8, D), F32)],
        compiler_params=pltpu.CompilerParams(vmem_limit_bytes=VMEM_LIMIT),
    )(c16, g16, g16_sh, w_sh)


def _ln0_fwd(x, ctx, g, b, modx, modc):
    L = x.shape[0]
    nt = L // TL

    def body(x_ref, c_ref, g_ref, b_ref, mx_ref, mc_ref, xn_ref, h_ref):
        isc = pl.program_id(0) == nt
        xin = jnp.where(isc, c_ref[...], x_ref[...])
        sh = jnp.where(isc, mc_ref[0:1, :], mx_ref[0:1, :])
        sc = jnp.where(isc, mc_ref[1:2, :], mx_ref[1:2, :])
        xhat, _ = _ln(xin)
        xn = xhat * g_ref[...] + b_ref[...]
        xn_ref[...] = xn
        h_ref[...] = (xn * (1.0 + sc) + sh).astype(h_ref.dtype)

    return pl.pallas_call(
        body, name="ln0_fwd", grid=(nt + 1,),
        in_specs=[_rtc(D, nt), _cst((TL, D)), _cst((1, D)), _cst((1, D)), _cst((8, D)), _cst((8, D))],
        out_specs=[_rt(D), _rt(D)],
        out_shape=[jax.ShapeDtypeStruct((L + TL, D), F32), jax.ShapeDtypeStruct((L + TL, D), _MXU)],
        compiler_params=_params(("parallel",)),
    )(x, ctx, g, b, modx, modc)


def _xbc_colblk(j):
    return jnp.where(j < 8, OXS // 128 + j, OB // 128 + j - 8)


def _conv_taps(p_ref, r0, first, last):
    main = p_ref[pl.ds(r0, TL), :]
    zero = jnp.zeros((8, main.shape[1]), F32)
    prev = zero if first else p_ref[pl.ds(r0 - 8, 8), :]
    nxt = zero if last else p_ref[pl.ds(r0 + TL, 8), :]
    ext = jnp.concatenate([prev, main, nxt], axis=0)
    n = TL + 16
    return [pltpu.roll(ext, (2 - k) % n, 0)[8:8 + TL] for k in range(5)]


def _seq_chunks(L):
    nt = L // TL
    return [(r * TL, r == 0, r == nt - 1) for r in range(nt)] + [(L, True, True)]


def _conv_fwd(p, conv_w8, conv_b):
    RT = p.shape[0]
    L = RT - TL
    chunks = _seq_chunks(L)

    def body(p_ref, w_ref, b_ref, o_ref):
        w = w_ref[...]
        bias = b_ref[...]
        for r0, first, last in chunks:
            taps = _conv_taps(p_ref, r0, first, last)
            pre = bias + sum(w[k:k + 1, :] * taps[k] for k in range(5))
            o_ref[pl.ds(r0, TL), :] = pre * _sig(pre)

    return pl.pallas_call(
        body, name="conv_fwd", grid=(12,),
        in_specs=[pl.BlockSpec((RT, 128), lambda j: (0, _xbc_colblk(j))),
                  pl.BlockSpec((8, 128), lambda j: (0, j)),
                  pl.BlockSpec((1, 128), lambda j: (0, j))],
        out_specs=pl.BlockSpec((RT, 128), lambda j: (0, j)),
        out_shape=jax.ShapeDtypeStruct((RT, 1536), F32),
        compiler_params=_params(("parallel",)),
    )(p, conv_w8, conv_b)


def _ssd_common(dtraw, dtb, a32, rev):
    dt = _softplus(dtraw + dtb)
    acum = _cumsum_rows(dt * a32, rev)
    ii = lax.broadcasted_iota(jnp.int32, (Q, Q), 0)
    jj = lax.broadcasted_iota(jnp.int32, (Q, Q), 1)
    mask = (ii <= jj) if rev else (ii >= jj)
    return dt, acum, acum.T, dt.T, mask


def _ssd_orders(ncl, ncc):
    nc = ncl + ncc

    def cf(s):
        return jnp.where(s < ncc, ncl + s, s - ncc)

    def cb(s):
        return nc - 1 - s

    return cf, cb


def _ssd_fwd(xbc, p, prm):
    RT = xbc.shape[0]
    nc = RT // Q
    ncc = TL // Q
    cf, cb = _ssd_orders(nc - ncc, ncc)

    def one_dir(x_ref, dt_ref, prm_ref, y_ref, hp_ref, H_ref, d):
        rev = d == 1
        a32 = -jnp.exp(prm_ref[1:2, :])
        dt, acum, acumT, dtT, mask = _ssd_common(dt_ref[...], prm_ref[0:1, :], a32, rev)
        end = 0 if rev else Q - 1
        for g in range(2):
            Bg = x_ref[:, D + g * NS:D + (g + 1) * NS]
            Cg = x_ref[:, D + 2 * NS + g * NS:D + 2 * NS + (g + 1) * NS]
            CB = _dot(Cg, Bg, NT)
            for hh in range(HPG):
                h = g * HPG + hh
                ln = 16 * d + h
                col = acum[:, ln:ln + 1]
                rowv = acumT[ln:ln + 1, :]
                a_end = rowv[:, end:end + 1]
                Lm = jnp.exp(jnp.where(mask, col - rowv, -1e30))
                W = CB * Lm * dtT[ln:ln + 1, :]
                Xh = x_ref[:, h * HP:(h + 1) * HP]
                Hp = H_ref[h * HP:(h + 1) * HP, :]
                y = _dot(W, Xh, NN) + jnp.exp(col) * _dot(Cg, Hp, NT)
                y_ref[:, h * HP:(h + 1) * HP] = y
                dcol = jnp.exp(a_end - col) * dt[:, ln:ln + 1]
                hp_ref[0, h * HP:(h + 1) * HP, :] = Hp
                H_ref[h * HP:(h + 1) * HP, :] = jnp.exp(a_end) * Hp + _dot(Xh * dcol, Bg, TN)

    def body(xf_ref, xb_ref, df_ref, db_ref, prm_ref, yf_ref, yb_ref, hf_ref, hb_ref, Hf, Hb):
        @pl.when(pl.program_id(0) == 0)
        def _():
            Hf[...] = jnp.zeros_like(Hf)
            Hb[...] = jnp.zeros_like(Hb)

        one_dir(xf_ref, df_ref, prm_ref, yf_ref, hf_ref, Hf, 0)
        one_dir(xb_ref, db_ref, prm_ref, yb_ref, hb_ref, Hb, 1)

    ysh = jax.ShapeDtypeStruct((RT, D), F32)
    hsh = jax.ShapeDtypeStruct((nc, NH * HP, NS), F32)
    hspec = pl.BlockSpec((1, NH * HP, NS), lambda s: (s, 0, 0))
    return pl.pallas_call(
        body, name="ssd_fwd", grid=(nc,),
        in_specs=[pl.BlockSpec((Q, 1536), lambda s: (cf(s), 0)),
                  pl.BlockSpec((Q, 1536), lambda s: (cb(s), 0)),
                  pl.BlockSpec((Q, 128), lambda s: (cf(s), ODT // 128)),
                  pl.BlockSpec((Q, 128), lambda s: (cb(s), ODT // 128)),
                  _cst((8, 128))],
        out_specs=[pl.BlockSpec((Q, D), lambda s: (cf(s), 0)),
                   pl.BlockSpec((Q, D), lambda s: (cb(s), 0)), hspec, hspec],
        out_shape=[ysh, ysh, hsh, hsh],
        scratch_shapes=[pltpu.VMEM((NH * HP, NS), F32), pltpu.VMEM((NH * HP, NS), F32)],
        compiler_params=_params(("arbitrary",)),
    )(xbc, xbc, p, p, prm)


def _ssd_bwd(xbc, p, prm, dsk, dyd, hpf, hpb):
    RT = xbc.shape[0]
    nc = RT // Q
    ncc = TL // Q
    ncl = nc - ncc
    cf, cb = _ssd_orders(ncl, ncc)

    def rs(t):
        return nc - 1 - t

    def one_dir(x_ref, dt_ref, prm_ref, dsk_ref, dy_ref, is_ctx, hp_ref, dH_ref,
                dx_ref, ddt_ref, st_ref, d):
        rev = d == 1
        a32 = -jnp.exp(prm_ref[1:2, :])
        dtraw = dt_ref[...]
        dtb = prm_ref[0:1, :]
        dt, acum, acumT, dtT, mask = _ssd_common(dtraw, dtb, a32, rev)
        end = 0 if rev else Q - 1
        lane = lax.broadcasted_iota(jnp.int32, (Q, 128), 1)
        srow = lax.broadcasted_iota(jnp.int32, (Q, 128), 0)
        dyscale = jnp.where(is_ctx, 0.0, 1.0)
        c_dacum = jnp.zeros((Q, 128), F32)
        r_dacum = jnp.zeros((Q, 128), F32)
        c_ddt = jnp.zeros((Q, 128), F32)
        r_ddt = jnp.zeros((Q, 128), F32)
        dskacc = jnp.zeros((1, 128), F32)
        for g in range(2):
            Bg = x_ref[:, D + g * NS:D + (g + 1) * NS]
            Cg = x_ref[:, D + 2 * NS + g * NS:D + 2 * NS + (g + 1) * NS]
            CB = _dot(Cg, Bg, NT)
            dCB = jnp.zeros((Q, Q), F32)
            dBg = jnp.zeros((Q, NS), F32)
            dCg = jnp.zeros((Q, NS), F32)
            for hh in range(HPG):
                h = g * HPG + hh
                ln = 16 * d + h
                hs = slice(h * HP, (h + 1) * HP)
                col = acum[:, ln:ln + 1]
                rowv = acumT[ln:ln + 1, :]
                dtr = dtT[ln:ln + 1, :]
                dtc = dt[:, ln:ln + 1]
                a_end = rowv[:, end:end + 1]
                Lm = jnp.exp(jnp.where(mask, col - rowv, -1e30))
                E = jnp.exp(col)
                ecol = jnp.exp(a_end - col)
                dcol = ecol * dtc
                Xh = x_ref[:, hs]
                dY = dy_ref[:, hs] * dyscale
                Hp = hp_ref[0, hs, :]
                dHn = dH_ref[hs, :]
                W = CB * Lm * dtr
                dW = _dot(dY, Xh, NT)
                Mm = dW * CB * Lm
                T = Mm * dtr
                dCB = dCB + dW * Lm * dtr
                BdH = _dot(Bg, dHn, NT)
                dX = _dot(W, dY, TN) + dcol * BdH
                if d == 0:
                    dX = dX + dY * dsk_ref[:, hs]
                    dskacc = dskacc + jnp.where(lane[0:1, :] == h, _sum11(dY * Xh), 0.0)
                dx_ref[:, hs] = dX
                xb = jnp.sum(Xh * BdH, axis=1, keepdims=True)
                scol = dcol * xb
                G = _dot(dY, Hp, NN)
                dCg = dCg + E * G
                qcol = E * jnp.sum(G * Cg, axis=1, keepdims=True)
                dBg = dBg + _dot(Xh * dcol, dHn, NN)
                dH_ref[hs, :] = jnp.exp(a_end) * dHn + _dot(dY * E, Cg, TN)
                eterm = jnp.exp(a_end) * _sum11(dHn * Hp) + _sum11(scol)
                cvec = jnp.sum(T, axis=1, keepdims=True) + qcol - scol
                cvec = cvec + jnp.where(srow[:, 0:1] == end, eterm, 0.0)
                c_dacum = c_dacum + jnp.where(lane == ln, cvec, 0.0)
                r_dacum = r_dacum - jnp.where(srow == ln, _colsum(T), 0.0)
                c_ddt = c_ddt + jnp.where(lane == ln, ecol * xb, 0.0)
                r_ddt = r_ddt + jnp.where(srow == ln, _colsum(Mm), 0.0)
            dBg = dBg + _dot(dCB, Cg, TN)
            dCg = dCg + _dot(dCB, Bg, NN)
            dx_ref[:, D + g * NS:D + (g + 1) * NS] = dBg
            dx_ref[:, D + 2 * NS + g * NS:D + 2 * NS + (g + 1) * NS] = dCg
        dacum = c_dacum + r_dacum.T
        da = _cumsum_rows(dacum, not rev)
        mine = (lane >= 16 * d) & (lane < 16 * d + 16)
        ddt = jnp.where(mine, c_ddt + r_ddt.T + da * a32, 0.0)
        ddt_ref[...] = ddt * _sig(dtraw + dtb)
        st_ref[0:1, :] += _colsum(jnp.where(mine, da * dt, 0.0))
        if d == 0:
            st_ref[1:2, :] += dskacc

    def body(xf_ref, xb_ref, df_ref, db_ref, prm_ref, dsk_ref, dyf_ref, dyb_ref, hf_ref, hb_ref,
             dxf_ref, dxb_ref, ddf_ref, ddb_ref, st_ref, dHf, dHb):
        t = pl.program_id(0)

        @pl.when(t == 0)
        def _():
            dHf[...] = jnp.zeros_like(dHf)
            dHb[...] = jnp.zeros_like(dHb)
            st_ref[...] = jnp.zeros_like(st_ref)

        s = rs(t)
        one_dir(xf_ref, df_ref, prm_ref, dsk_ref, dyf_ref, cf(s) >= ncl, hf_ref, dHf,
                dxf_ref, ddf_ref, st_ref, 0)
        one_dir(xb_ref, db_ref, prm_ref, dsk_ref, dyb_ref, cb(s) >= ncl, hb_ref, dHb,
                dxb_ref, ddb_ref, st_ref, 1)

        @pl.when(t == nc - 1)
        def _():
            st_ref[0:1, :] = -jnp.exp(prm_ref[1:2, :]) * st_ref[0:1, :]

    def lat(c):
        return jnp.minimum(c, ncl - 1)

    xsh = jax.ShapeDtypeStruct((RT, 1536), F32)
    dsh = jax.ShapeDtypeStruct((RT, 128), F32)
    hspec = pl.BlockSpec((1, NH * HP, NS), lambda t: (rs(t), 0, 0))
    return pl.pallas_call(
        body, name="ssd_bwd", grid=(nc,),
        in_specs=[pl.BlockSpec((Q, 1536), lambda t: (cf(rs(t)), 0)),
                  pl.BlockSpec((Q, 1536), lambda t: (cb(rs(t)), 0)),
                  pl.BlockSpec((Q, 128), lambda t: (cf(rs(t)), ODT // 128)),
                  pl.BlockSpec((Q, 128), lambda t: (cb(rs(t)), ODT // 128)),
                  _cst((8, 128)), _cst((1, D)),
                  pl.BlockSpec((Q, D), lambda t: (lat(cf(rs(t))), 0)),
                  pl.BlockSpec((Q, D), lambda t: (lat(cb(rs(t))), 0)),
                  hspec, hspec],
        out_specs=[pl.BlockSpec((Q, 1536), lambda t: (cf(rs(t)), 0)),
                   pl.BlockSpec((Q, 1536), lambda t: (cb(rs(t)), 0)),
                   pl.BlockSpec((Q, 128), lambda t: (cf(rs(t)), 0)),
                   pl.BlockSpec((Q, 128), lambda t: (cb(rs(t)), 0)),
                   _cst((8, 128))],
        out_shape=[xsh, xsh, dsh, dsh, jax.ShapeDtypeStruct((8, 128), F32)],
        scratch_shapes=[pltpu.VMEM((NH * HP, NS), F32), pltpu.VMEM((NH * HP, NS), F32)],
        compiler_params=_params(("arbitrary",)),
    )(xbc, xbc, p, p, prm, dsk, dyd, dyd, hpf, hpb)


def _lane_bcast(v, ln):
    return jnp.broadcast_to(v[:, ln:ln + 1], v.shape)


def _halves(v, lo, axis):
    return jnp.concatenate([jnp.where(lo, v, 0.0), jnp.where(lo, 0.0, v)], axis=axis)


def _ssd2_fwd(xbc, p, prm):
    RT = xbc.shape[0]
    nc = RT // Q
    ncc = TL // Q
    cf, cb = _ssd_orders(nc - ncc, ncc)

    def one_dir(x_ref, dt_ref, prm_ref, y_ref, hp_ref, HT_ref, d):
        rev = d == 1
        a32 = -jnp.exp(prm_ref[1:2, :])
        dt, acum, acumT, dtT, mask = _ssd_common(dt_ref[...], prm_ref[0:1, :], a32, rev)
        end = 0 if rev else Q - 1
        lo = lax.broadcasted_iota(jnp.int32, (Q, 128), 1) < HP
        for g in range(2):
            Bg = x_ref[:, D + g * NS:D + (g + 1) * NS]
            Cg = x_ref[:, D + 2 * NS + g * NS:D + 2 * NS + (g + 1) * NS]
            CB = _dot(Cg, Bg, NT)
            xds, svs = [], []
            for q in range(HPG // 2):
                pi = g * (HPG // 2) + q
                ps = slice(pi * 128, (pi + 1) * 128)
                Xp = x_ref[:, ps]
                HTp = HT_ref[:, ps]
                lhs, dcs, sv = [], [], []
                ces = []
                for h in (2 * pi, 2 * pi + 1):
                    ln = 16 * d + h
                    colB = _lane_bcast(acum, ln)
                    rowv = acumT[ln:ln + 1, :]
                    aend = colB[end:end + 1, :]
                    Lm = jnp.exp(jnp.where(mask, colB - rowv, -1e30))
                    lhs.append(CB * Lm * dtT[ln:ln + 1, :])
                    ces.append(Cg * jnp.exp(colB))
                    dcs.append(jnp.exp(aend - colB) * _lane_bcast(dt, ln))
                    sv.append(jnp.exp(aend))
                lhs = jnp.concatenate(lhs + ces, axis=1)
                rhs = jnp.concatenate([_halves(Xp, lo, 0), _halves(HTp, lo, 0)], axis=0)
                y_ref[:, ps] = _dot(lhs, rhs, NN)
                xds.append(Xp * jnp.where(lo, dcs[0], dcs[1]))
                svs.append(jnp.where(lo[0:1, :], sv[0], sv[1]))
            gs = slice(g * 512, (g + 1) * 512)
            HTg = HT_ref[:, gs]
            hp_ref[0, :, gs] = HTg
            st = _dot(Bg.T, jnp.concatenate(xds, axis=1), NN)
            HT_ref[:, gs] = jnp.concatenate(svs, axis=1) * HTg + st

    def body(xf_ref, xb_ref, df_ref, db_ref, prm_ref, yf_ref, yb_ref, hf_ref, hb_ref, Hf, Hb):
        @pl.when(pl.program_id(0) == 0)
        def _():
            Hf[...] = jnp.zeros_like(Hf)
            Hb[...] = jnp.zeros_like(Hb)

        one_dir(xf_ref, df_ref, prm_ref, yf_ref, hf_ref, Hf, 0)
        one_dir(xb_ref, db_ref, prm_ref, yb_ref, hb_ref, Hb, 1)

    ysh = jax.ShapeDtypeStruct((RT, D), F32)
    hsh = jax.ShapeDtypeStruct((nc, NS, NH * HP), F32)
    hspec = pl.BlockSpec((1, NS, NH * HP), lambda s: (s, 0, 0))
    return pl.pallas_call(
        body, name="ssd_fwd", grid=(nc,),
        in_specs=[pl.BlockSpec((Q, 1536), lambda s: (cf(s), 0)),
                  pl.BlockSpec((Q, 1536), lambda s: (cb(s), 0)),
                  pl.BlockSpec((Q, 128), lambda s: (cf(s), ODT // 128)),
                  pl.BlockSpec((Q, 128), lambda s: (cb(s), ODT // 128)),
                  _cst((8, 128))],
        out_specs=[pl.BlockSpec((Q, D), lambda s: (cf(s), 0)),
                   pl.BlockSpec((Q, D), lambda s: (cb(s), 0)), hspec, hspec],
        out_shape=[ysh, ysh, hsh, hsh],
        scratch_shapes=[pltpu.VMEM((NS, NH * HP), F32), pltpu.VMEM((NS, NH * HP), F32)],
        compiler_params=_params(("arbitrary",)),
    )(xbc, xbc, p, p, prm)


def _ssd2_bwd(xbc, p, prm, dsk, dyd, hpf, hpb):
    RT = xbc.shape[0]
    nc = RT // Q
    ncc = TL // Q
    ncl = nc - ncc
    cf, cb = _ssd_orders(ncl, ncc)

    def rs(t):
        return nc - 1 - t

    def one_dir(x_ref, dt_ref, prm_ref, dsk_ref, dy_ref, is_ctx, hp_ref, dHT_ref,
                dx_ref, ddt_ref, st_ref, d):
        rev = d == 1
        a32 = -jnp.exp(prm_ref[1:2, :])
        dtraw = dt_ref[...]
        dtb = prm_ref[0:1, :]
        dt, acum, acumT, _, _ = _ssd_common(dtraw, dtb, a32, rev)
        end = 0 if rev else Q - 1
        lane = lax.broadcasted_iota(jnp.int32, (Q, 128), 1)
        srow = lax.broadcasted_iota(jnp.int32, (Q, 128), 0)
        maskT = (lane <= srow) if rev else (lane >= srow)
        lo = lane < HP
        lo1 = lo[0:1, :]
        dyscale = jnp.where(is_ctx, 0.0, 1.0)
        c_dacum = jnp.zeros((Q, 128), F32)
        r_dacum = jnp.zeros((Q, 128), F32)
        c_ddt = jnp.zeros((Q, 128), F32)
        dskacc = jnp.zeros((1, 128), F32)
        for g in range(2):
            gs = slice(g * 512, (g + 1) * 512)
            Bg = x_ref[:, D + g * NS:D + (g + 1) * NS]
            Cg = x_ref[:, D + 2 * NS + g * NS:D + 2 * NS + (g + 1) * NS]
            CBT = _dot(Bg, Cg, NT)
            HTg = hp_ref[0, :, gs]
            dHTg = dHT_ref[:, gs]
            BdHg = _dot(Bg, dHTg, NN)
            dCBT = jnp.zeros((Q, Q), F32)
            dCg = jnp.zeros((Q, NS), F32)
            xds, dyes, svs = [], [], []
            for q in range(HPG // 2):
                pi = g * (HPG // 2) + q
                ps = slice(pi * 128, (pi + 1) * 128)
                qs = slice(q * 128, (q + 1) * 128)
                Xp = x_ref[:, ps]
                dYp = dy_ref[:, ps] * dyscale
                HTp = HTg[:, qs]
                BdHp = BdHg[:, qs]
                dY2 = _halves(dYp, lo, 0)
                dWT2 = _dot(_halves(Xp, lo, 0), dYp.T, NN)
                G2 = _dot(dY2, HTp, NT)
                XB = Xp * BdHp
                hh = _colsum(dHTg[:, qs] * HTp)
                yx = _colsum(dYp * Xp)
                wts, dcs, ebs, sv = [], [], [], []
                for k, h in enumerate((2 * pi, 2 * pi + 1)):
                    ln = 16 * d + h
                    half = lo if k == 0 else jnp.logical_not(lo)
                    half1 = half[0:1, :]
                    colB = _lane_bcast(acum, ln)
                    dtcB = _lane_bcast(dt, ln)
                    rowv = acumT[ln:ln + 1, :]
                    aend = colB[end:end + 1, :]
                    LmT = jnp.exp(jnp.where(maskT, rowv - colB, -1e30))
                    WT = CBT * LmT * dtcB
                    dWT = dWT2[k * Q:(k + 1) * Q, :]
                    U = dWT * LmT
                    MT = U * CBT
                    rM = jnp.sum(MT, axis=1, keepdims=True)
                    rT = _colsum(MT * dtcB)
                    dCBT = dCBT + U * dtcB
                    ecol = jnp.exp(aend - colB)
                    EB = jnp.exp(colB)
                    Gk = G2[k * Q:(k + 1) * Q, :]
                    dCg = dCg + EB * Gk
                    qcol = jnp.sum(EB * Gk * Cg, axis=1, keepdims=True)
                    xb = jnp.sum(jnp.where(half, XB, 0.0), axis=1, keepdims=True)
                    e1 = ecol[:, 0:1]
                    dt1 = dtcB[:, 0:1]
                    scol = e1 * dt1 * xb
                    sA = jnp.exp(aend)
                    eterm = sA[:, 0:1] * jnp.sum(jnp.where(half1, hh, 0.0), axis=1, keepdims=True) \
                        + _colsum(scol)
                    cvec = qcol - dt1 * rM - scol + jnp.where(srow[:, 0:1] == end, eterm, 0.0)
                    c_dacum = jnp.where(lane == ln, cvec, c_dacum)
                    r_dacum = jnp.where(srow == ln, rT, r_dacum)
                    c_ddt = jnp.where(lane == ln, rM + e1 * xb, c_ddt)
                    if d == 0:
                        dskacc = dskacc + jnp.where(
                            lane[0:1, :] == h, jnp.sum(jnp.where(half1, yx, 0.0), axis=1, keepdims=True), 0.0)
                    wts.append(WT)
                    dcs.append(ecol * dtcB)
                    ebs.append(EB)
                    sv.append(sA)
                dcp = jnp.where(lo, dcs[0], dcs[1])
                dX = _dot(jnp.concatenate(wts, axis=1), dY2, NN) + dcp * BdHp
                if d == 0:
                    dX = dX + dYp * dsk_ref[:, ps]
                dx_ref[:, ps] = dX
                xds.append(Xp * dcp)
                dyes.append(dYp * jnp.where(lo, ebs[0], ebs[1]))
                svs.append(jnp.where(lo1, sv[0], sv[1]))
            dx_ref[:, D + g * NS:D + (g + 1) * NS] = (
                _dot(jnp.concatenate(xds, axis=1), dHTg, NT) + _dot(dCBT, Cg, NN))
            dx_ref[:, D + 2 * NS + g * NS:D + 2 * NS + (g + 1) * NS] = dCg + _dot(dCBT, Bg, TN)
            dHT_ref[:, gs] = (jnp.concatenate(svs, axis=1) * dHTg
                              + _dot(Cg.T, jnp.concatenate(dyes, axis=1), NN))
        dacum = c_dacum + r_dacum.T
        da = _cumsum_rows(dacum, not rev)
        mine = (lane >= 16 * d) & (lane < 16 * d + 16)
        ddt = jnp.where(mine, c_ddt + da * a32, 0.0)
        ddt_ref[...] = ddt * _sig(dtraw + dtb)
        st_ref[0:1, :] += _colsum(jnp.where(mine, da * dt, 0.0))
        if d == 0:
            st_ref[1:2, :] += dskacc

    def body(xf_ref, xb_ref, df_ref, db_ref, prm_ref, dsk_ref, dyf_ref, dyb_ref, hf_ref, hb_ref,
             dxf_ref, dxb_ref, ddf_ref, ddb_ref, st_ref, dHf, dHb):
        t = pl.program_id(0)

        @pl.when(t == 0)
        def _():
            dHf[...] = jnp.zeros_like(dHf)
            dHb[...] = jnp.zeros_like(dHb)
            st_ref[...] = jnp.zeros_like(st_ref)

        s = rs(t)
        one_dir(xf_ref, df_ref, prm_ref, dsk_ref, dyf_ref, cf(s) >= ncl, hf_ref, dHf,
                dxf_ref, ddf_ref, st_ref, 0)
        one_dir(xb_ref, db_ref, prm_ref, dsk_ref, dyb_ref, cb(s) >= ncl, hb_ref, dHb,
                dxb_ref, ddb_ref, st_ref, 1)

        @pl.when(t == nc - 1)
        def _():
            st_ref[0:1, :] = -jnp.exp(prm_ref[1:2, :]) * st_ref[0:1, :]

    def lat(c):
        return jnp.minimum(c, ncl - 1)

    xsh = jax.ShapeDtypeStruct((RT, 1536), F32)
    dsh = jax.ShapeDtypeStruct((RT, 128), F32)
    hspec = pl.BlockSpec((1, NS, NH * HP), lambda t: (rs(t), 0, 0))
    return pl.pallas_call(
        body, name="ssd_bwd", grid=(nc,),
        in_specs=[pl.BlockSpec((Q, 1536), lambda t: (cf(rs(t)), 0)),
                  pl.BlockSpec((Q, 1536), lambda t: (cb(rs(t)), 0)),
                  pl.BlockSpec((Q, 128), lambda t: (cf(rs(t)), ODT // 128)),
                  pl.BlockSpec((Q, 128), lambda t: (cb(rs(t)), ODT // 128)),
                  _cst((8, 128)), _cst((1, D)),
                  pl.BlockSpec((Q, D), lambda t: (lat(cf(rs(t))), 0)),
                  pl.BlockSpec((Q, D), lambda t: (lat(cb(rs(t))), 0)),
                  hspec, hspec],
        out_specs=[pl.BlockSpec((Q, 1536), lambda t: (cf(rs(t)), 0)),
                   pl.BlockSpec((Q, 1536), lambda t: (cb(rs(t)), 0)),
                   pl.BlockSpec((Q, 128), lambda t: (cf(rs(t)), 0)),
                   pl.BlockSpec((Q, 128), lambda t: (cb(rs(t)), 0)),
                   _cst((8, 128))],
        out_shape=[xsh, xsh, dsh, dsh, jax.ShapeDtypeStruct((8, 128), F32)],
        scratch_shapes=[pltpu.VMEM((NS, NH * HP), F32), pltpu.VMEM((NS, NH * HP), F32)],
        compiler_params=_params(("arbitrary",)),
    )(xbc, xbc, p, p, prm, dsk, dyd, dyd, hpf, hpb)


RB = 32


def _ssd3_bwd(xbc, p, prm, dsk, dyd, hpf, hpb):
    RT = xbc.shape[0]
    nc = RT // Q
    ncc = TL // Q
    ncl = nc - ncc
    cf, cb = _ssd_orders(ncl, ncc)
    npair = HPG // 2

    def rs(t):
        return nc - 1 - t

    def one_dir(x_ref, dt_ref, prm_ref, dsk_ref, dy_ref, is_ctx, hp_ref, dHT_ref,
                dx_ref, ddt_ref, st_ref, s_dwt, s_g, s_wt, s_xd, s_dye, s_dcbt, s_dcg, s_cd, s_cdt, d):
        rev = d == 1
        a32 = -jnp.exp(prm_ref[1:2, :])
        dtraw = dt_ref[...]
        dtb = prm_ref[0:1, :]
        dt, acum, acumT, _, _ = _ssd_common(dtraw, dtb, a32, rev)
        end = 0 if rev else Q - 1
        lane = lax.broadcasted_iota(jnp.int32, (RB, 128), 1)
        srow0 = lax.broadcasted_iota(jnp.int32, (RB, 128), 0)
        lo = lane < HP
        lo1 = lo[0:1, :]
        lane1 = lane[0:1, :]
        dyscale = jnp.where(is_ctx, 0.0, 1.0)
        aend_row = acum[end:end + 1, :]
        s_cd[...] = jnp.zeros_like(s_cd)
        s_cdt[...] = jnp.zeros_like(s_cdt)
        r_rows = jnp.zeros((Q, 128), F32)
        srowQ = lax.broadcasted_iota(jnp.int32, (Q, 128), 0)
        dskacc = jnp.zeros((1, 128), F32)
        for g in range(2):
            gs = slice(g * 512, (g + 1) * 512)
            Bg = x_ref[:, D + g * NS:D + (g + 1) * NS]
            Cg = x_ref[:, D + 2 * NS + g * NS:D + 2 * NS + (g + 1) * NS]
            CBT = _dot(Bg, Cg, NT)
            HTg = hp_ref[0, :, gs]
            dHTg = dHT_ref[:, gs]
            BdHg = _dot(Bg, dHTg, NN)
            hhs, yxs = [], []
            for q in range(npair):
                pi = g * npair + q
                ps = slice(pi * 128, (pi + 1) * 128)
                qs = slice(q * 128, (q + 1) * 128)
                Xp = x_ref[:, ps]
                dYp = dy_ref[:, ps] * dyscale
                s_dwt[q] = _dot(_halves(Xp, lo_full(), 0), dYp.T, NN)
                s_g[q] = _dot(_halves(dYp, lo_full(), 0), HTg[:, qs], NT)
                hhs.append(_colsum(dHTg[:, qs] * HTg[:, qs]))
                yxs.append(_colsum(dYp * Xp))
            rparts = [jnp.zeros((8, 128), F32) for _ in range(HPG)]
            ssum = [jnp.zeros((1, 1), F32) for _ in range(HPG)]
            for rb in range(Q // RB):
                r0 = rb * RB
                rows = slice(r0, r0 + RB)
                srow = srow0 + r0
                maskT = (lane <= srow) if rev else (lane >= srow)
                acum_rb = acum[rows, :]
                dt_rb = dt[rows, :]
                CBT_rb = CBT[rows, :]
                Cg_rb = Cg[rows, :]
                dcbt = jnp.zeros((RB, Q), F32)
                dcg = jnp.zeros((RB, NS), F32)
                cd = s_cd[rows, :]
                cdt = s_cdt[rows, :]
                for q in range(npair):
                    pi = g * npair + q
                    ps = slice(pi * 128, (pi + 1) * 128)
                    qs = slice(q * 128, (q + 1) * 128)
                    Xp = x_ref[rows, ps]
                    dYp = dy_ref[rows, ps] * dyscale
                    BdHp = BdHg[rows, qs]
                    XB = Xp * BdHp
                    dcs, ebs = [], []
                    for k in range(2):
                        hh = 2 * q + k
                        ln = 16 * d + g * HPG + hh
                        half = lo if k == 0 else jnp.logical_not(lo)
                        colB = _lane_bcast(acum_rb, ln)
                        dtcB = _lane_bcast(dt_rb, ln)
                        rowv = acumT[ln:ln + 1, :]
                        aend = _lane_bcast(aend_row, ln)
                        LmT = jnp.exp(jnp.where(maskT, rowv - colB, -1e30))
                        s_wt[q, rows, k * Q:(k + 1) * Q] = (CBT_rb * LmT * dtcB).astype(s_wt.dtype)
                        U = s_dwt[q, k * Q + r0:k * Q + r0 + RB, :] * LmT
                        MT = U * CBT_rb
                        rM = jnp.sum(MT, axis=1, keepdims=True)
                        TT = MT * dtcB
                        rparts[hh] = rparts[hh] + (TT[0:8] + TT[8:16] + TT[16:24] + TT[24:32])
                        dcbt = dcbt + U * dtcB
                        ecol = jnp.exp(aend - colB)
                        EB = jnp.exp(colB)
                        EG = EB * s_g[q, k * Q + r0:k * Q + r0 + RB, :]
                        dcg = dcg + EG
                        qcol = jnp.sum(EG * Cg_rb, axis=1, keepdims=True)
                        xb = jnp.sum(jnp.where(half, XB, 0.0), axis=1, keepdims=True)
                        e1 = ecol[:, 0:1]
                        dt1 = dtcB[:, 0:1]
                        scol = e1 * dt1 * xb
                        ssum[hh] = ssum[hh] + _colsum(scol)
                        cd = jnp.where(lane == ln, qcol - dt1 * rM - scol, cd)
                        cdt = jnp.where(lane == ln, rM + e1 * xb, cdt)
                        dcs.append(ecol * dtcB)
                        ebs.append(EB)
                    dcp = jnp.where(lo, dcs[0], dcs[1])
                    dxo = dcp * BdHp
                    if d == 0:
                        dxo = dxo + dYp * dsk_ref[:, ps]
                    dx_ref[rows, ps] = dxo
                    s_xd[rows, qs] = (Xp * dcp).astype(s_xd.dtype)
                    s_dye[rows, qs] = (dYp * jnp.where(lo, ebs[0], ebs[1])).astype(s_dye.dtype)
                s_dcbt[rows, :] = dcbt
                s_dcg[rows, :] = dcg
                s_cd[rows, :] = cd
                s_cdt[rows, :] = cdt
            erow = jnp.zeros((1, 128), F32)
            svs = []
            for q in range(npair):
                pi = g * npair + q
                ps = slice(pi * 128, (pi + 1) * 128)
                sv = []
                for k in range(2):
                    hh = 2 * q + k
                    h = g * HPG + hh
                    ln = 16 * d + h
                    half1 = lo1 if k == 0 else jnp.logical_not(lo1)
                    sA = jnp.exp(_lane_bcast(aend_row, ln))
                    hsum = jnp.sum(jnp.where(half1, hhs[q], 0.0), axis=1, keepdims=True)
                    erow = erow + jnp.where(lane1 == ln, sA[:, 0:1] * hsum + ssum[hh], 0.0)
                    rp = rparts[hh]
                    r_rows = jnp.where(srowQ == ln, _colsum(rp), r_rows)
                    if d == 0:
                        dskacc = dskacc + jnp.where(
                            lane1 == h, jnp.sum(jnp.where(half1, yxs[q], 0.0), axis=1, keepdims=True), 0.0)
                    sv.append(sA)
                svs.append(jnp.where(lo1, sv[0], sv[1]))
                dY2 = _halves(dy_ref[:, ps] * dyscale, lo_full(), 0)
                dx_ref[:, ps] += _dot(s_wt[q], dY2, NN)
            s_cd[end:end + 1, :] += erow
            dcbt_g = s_dcbt[...]
            dx_ref[:, D + g * NS:D + (g + 1) * NS] = _dot(s_xd[...], dHTg, NT) + _dot(dcbt_g, Cg, NN)
            dx_ref[:, D + 2 * NS + g * NS:D + 2 * NS + (g + 1) * NS] = s_dcg[...] + _dot(dcbt_g, Bg, TN)
            dHT_ref[:, gs] = jnp.concatenate(svs, axis=1) * dHTg + _dot(Cg.T, s_dye[...], NN)
        dacum = s_cd[...] + r_rows.T
        da = _cumsum_rows(dacum, not rev)
        laneQ = lax.broadcasted_iota(jnp.int32, (Q, 128), 1)
        mine = (laneQ >= 16 * d) & (laneQ < 16 * d + 16)
        ddt = jnp.where(mine, s_cdt[...] + da * a32, 0.0)
        ddt_ref[...] = ddt * _sig(dtraw + dtb)
        st_ref[0:1, :] += _colsum(jnp.where(mine, da * dt, 0.0))
        if d == 0:
            st_ref[1:2, :] += dskacc

    def lo_full():
        return lax.broadcasted_iota(jnp.int32, (Q, 128), 1) < HP

    def body(xf_ref, xb_ref, df_ref, db_ref, prm_ref, dsk_ref, dyf_ref, dyb_ref, hf_ref, hb_ref,
             dxf_ref, dxb_ref, ddf_ref, ddb_ref, st_ref, dHf, dHb, *scr):
        t = pl.program_id(0)

        @pl.when(t == 0)
        def _():
            dHf[...] = jnp.zeros_like(dHf)
            dHb[...] = jnp.zeros_like(dHb)
            st_ref[...] = jnp.zeros_like(st_ref)

        s = rs(t)
        one_dir(xf_ref, df_ref, prm_ref, dsk_ref, dyf_ref, cf(s) >= ncl, hf_ref, dHf,
                dxf_ref, ddf_ref, st_ref, *scr, 0)
        one_dir(xb_ref, db_ref, prm_ref, dsk_ref, dyb_ref, cb(s) >= ncl, hb_ref, dHb,
                dxb_ref, ddb_ref, st_ref, *scr, 1)

        @pl.when(t == nc - 1)
        def _():
            st_ref[0:1, :] = -jnp.exp(prm_ref[1:2, :]) * st_ref[0:1, :]

    def lat(c):
        return jnp.minimum(c, ncl - 1)

    xsh = jax.ShapeDtypeStruct((RT, 1536), F32)
    dsh = jax.ShapeDtypeStruct((RT, 128), F32)
    hspec = pl.BlockSpec((1, NS, NH * HP), lambda t: (rs(t), 0, 0))
    return pl.pallas_call(
        body, name="ssd_bwd", grid=(nc,),
        in_specs=[pl.BlockSpec((Q, 1536), lambda t: (cf(rs(t)), 0)),
                  pl.BlockSpec((Q, 1536), lambda t: (cb(rs(t)), 0)),
                  pl.BlockSpec((Q, 128), lambda t: (cf(rs(t)), ODT // 128)),
                  pl.BlockSpec((Q, 128), lambda t: (cb(rs(t)), ODT // 128)),
                  _cst((8, 128)), _cst((1, D)),
                  pl.BlockSpec((Q, D), lambda t: (lat(cf(rs(t))), 0)),
                  pl.BlockSpec((Q, D), lambda t: (lat(cb(rs(t))), 0)),
                  hspec, hspec],
        out_specs=[pl.BlockSpec((Q, 1536), lambda t: (cf(rs(t)), 0)),
                   pl.BlockSpec((Q, 1536), lambda t: (cb(rs(t)), 0)),
                   pl.BlockSpec((Q, 128), lambda t: (cf(rs(t)), 0)),
                   pl.BlockSpec((Q, 128), lambda t: (cb(rs(t)), 0)),
                   _cst((8, 128))],
        out_shape=[xsh, xsh, dsh, dsh, jax.ShapeDtypeStruct((8, 128), F32)],
        scratch_shapes=[pltpu.VMEM((NS, NH * HP), F32), pltpu.VMEM((NS, NH * HP), F32),
                        pltpu.VMEM((npair, 2 * Q, Q), F32), pltpu.VMEM((npair, 2 * Q, NS), F32),
                        pltpu.VMEM((npair, Q, 2 * Q), _MXU), pltpu.VMEM((Q, 512), _MXU),
                        pltpu.VMEM((Q, 512), _MXU), pltpu.VMEM((Q, Q), F32), pltpu.VMEM((Q, NS), F32),
                        pltpu.VMEM((Q, 128), F32), pltpu.VMEM((Q, 128), F32)],
        compiler_params=_params(("arbitrary",)),
    )(xbc, xbc, p, p, prm, dsk, dyd, dyd, hpf, hpb)


def _mix_fwd_vals(yf, yb, z, xs, u, v, dsk, sg, gg, gb):
    y = yf + yb + xs * dsk
    sz = _sig(z)
    hh = y * z * sz
    r = lax.rsqrt(jnp.mean(hh * hh, axis=-1, keepdims=True) + EPS)
    nh = hh * r
    ug, tu = _gelu(u)
    vg, tv = _gelu(v)
    vhat, vrstd = _ln(vg)
    vn = vhat * gg + gb
    return y, sz, r, nh, ug, tu, vg, tv, vhat, vrstd, vn


def _mix_fwd(yf, yb, p, xbc, dsk, sg, gg, gb, ws, bsT):
    L = yf.shape[0] - TL
    nt = L // TL

    def body(yf_ref, yb_ref, z_ref, xs_ref, u_ref, v_ref, dsk_ref, sg_ref, gg_ref, gb_ref,
             ws_ref, bs_ref, ys_ref, ym_ref):
        _, _, _, nh, ug, _, _, _, _, _, vn = _mix_fwd_vals(
            yf_ref[...], yb_ref[...], z_ref[...], xs_ref[...], u_ref[...], v_ref[...],
            dsk_ref[...], sg_ref[...], gg_ref[...], gb_ref[...])
        ys_ref[...] = (nh * sg_ref[...]).astype(ys_ref.dtype)
        for n in range(TL // Q):
            rs_ = slice(n * Q, (n + 1) * Q)
            for g in range(8):
                cs = slice(g * 128, (g + 1) * 128)
                mixed = _dot(ws_ref[g], vn[rs_, cs], NN) + bs_ref[:, g:g + 1]
                ym_ref[rs_, cs] = (ug[rs_, cs] * mixed).astype(ym_ref.dtype)

    return pl.pallas_call(
        body, name="mix_fwd", grid=(nt,),
        in_specs=[_rt(D), _rt(D), _rt(D, OZ // D), _rt(D, 0), _rt(D, OU // D), _rt(D, OV // D),
                  _cst((1, D)), _cst((1, D)), _cst((1, D)), _cst((1, D)),
                  _cst((8, 128, 128)), _cst((128, 128))],
        out_specs=[_rt(D), _rt(D)],
        out_shape=[jax.ShapeDtypeStruct((L, D), _MXU), jax.ShapeDtypeStruct((L, D), _MXU)],
        compiler_params=_params(("parallel",)),
    )(yf, yb, p, xbc, p, p, dsk, sg, gg, gb, ws, bsT)


def _mix_bwd(dys, dym, yf, yb, p, xbc, dp, dsk, sg, gg, gb, ws, bsT):
    L = dys.shape[0]
    nt = L // TL

    def body(dys_ref, dym_ref, yf_ref, yb_ref, z_ref, xs_ref, u_ref, v_ref, dsk_ref, sg_ref,
             gg_ref, gb_ref, ws_ref, bs_ref, dp_any, dzuv_ref, dy_ref, st_ref,
             dws_ref, dbs_ref, dvn_s):
        del dp_any
        dz_ref = dzuv_ref.at[:, OZ:OZ + D]
        du_ref = dzuv_ref.at[:, OU:OU + D]
        dv_ref = dzuv_ref.at[:, OV:OV + D]

        @pl.when(pl.program_id(0) == 0)
        def _():
            st_ref[...] = jnp.zeros_like(st_ref)
            dws_ref[...] = jnp.zeros_like(dws_ref)
            dbs_ref[...] = jnp.zeros_like(dbs_ref)

        z = z_ref[...]
        u = u_ref[...]
        v = v_ref[...]
        y, sz, r, nh, ug, tu, vg, tv, vhat, vrstd, vn = _mix_fwd_vals(
            yf_ref[...], yb_ref[...], z, xs_ref[...], u, v,
            dsk_ref[...], sg_ref[...], gg_ref[...], gb_ref[...])
        dys = dys_ref[...]
        st_ref[0:1, :] += _colsum(dys * nh)
        dn = dys * sg_ref[...]
        dhh = r * (dn - nh * jnp.mean(dn * nh, axis=-1, keepdims=True))
        dy_ref[...] = dhh * z * sz
        dz_ref[...] = (dhh * y * (sz * (1.0 + z * (1.0 - sz)))).astype(dz_ref.dtype)
        dym = dym_ref[...]
        lane = lax.broadcasted_iota(jnp.int32, (Q, 128), 1)
        dbs = jnp.zeros((Q, 128), F32)
        gu = _gelu_grad(u, tu)
        for n in range(TL // Q):
            rs_ = slice(n * Q, (n + 1) * Q)
            for g in range(8):
                cs = slice(g * 128, (g + 1) * 128)
                vb = vn[rs_, cs]
                mixed = _dot(ws_ref[g], vb, NN) + bs_ref[:, g:g + 1]
                dyb = dym[rs_, cs]
                dmx = dyb * ug[rs_, cs]
                du_ref[rs_, cs] = (dyb * mixed * gu[rs_, cs]).astype(du_ref.dtype)
                dvn_s[rs_, cs] = _dot(ws_ref[g], dmx, TN)
                dws_ref[g] += _dot(dmx, vb, NT)
                dbs = dbs + jnp.where(lane == g, jnp.sum(dmx, axis=1, keepdims=True), 0.0)
        dbs_ref[...] += dbs
        dvn = dvn_s[...]
        st_ref[1:2, :] += _colsum(dvn * vhat)
        st_ref[2:3, :] += _colsum(dvn)
        dvg = _ln_bwd(dvn * gg_ref[...], vhat, vrstd)
        dv_ref[...] = (dvg * _gelu_grad(v, tv)).astype(dv_ref.dtype)

    outs = pl.pallas_call(
        body, name="mix_bwd", grid=(nt,),
        in_specs=[_rt(D), _rt(D), _rt(D), _rt(D), _rt(D, OZ // D), _rt(D, 0), _rt(D, OU // D),
                  _rt(D, OV // D), _cst((1, D)), _cst((1, D)), _cst((1, D)), _cst((1, D)),
                  _cst((8, 128, 128)), _cst((128, 128)), pl.BlockSpec(memory_space=pl.ANY)],
        out_specs=[_rt(3 * D, 0), _rt(D), _cst((8, D)),
                   _cst((8, 128, 128)), _cst((128, 128))],
        out_shape=[jax.ShapeDtypeStruct(dp.shape, dp.dtype),
                   jax.ShapeDtypeStruct((L, D), F32), jax.ShapeDtypeStruct((8, D), F32),
                   jax.ShapeDtypeStruct((8, 128, 128), F32), jax.ShapeDtypeStruct((128, 128), F32)],
        scratch_shapes=[pltpu.VMEM((TL, D), F32)],
        input_output_aliases={14: 0},
        compiler_params=_params(("arbitrary",)),
    )(dys, dym, yf, yb, p, xbc, p, p, dsk, sg, gg, gb, ws, bsT, dp)
    return outs


def _gate_fwd(a1, a2, p, bg):
    L = a1.shape[0]

    def body(a1_ref, a2_ref, g_ref, bg_ref, m_ref):
        gt = _sig(g_ref[...] + bg_ref[...])
        m_ref[...] = (gt[:, :D] * a1_ref[...] + gt[:, D:] * a2_ref[...]).astype(m_ref.dtype)

    return pl.pallas_call(
        body, name="gate_fwd", grid=(L // TL,),
        in_specs=[_rt(D), _rt(D), _rt(2 * D, OG // (2 * D)), _cst((1, 2 * D))],
        out_specs=_rt(D), out_shape=jax.ShapeDtypeStruct((L, D), _MXU),
        compiler_params=_params(("parallel",)),
    )(a1, a2, p, bg)


def _gate_bwd(dmg, a1, a2, p, bg, dp):
    L = a1.shape[0]

    def body(dm_ref, a1_ref, a2_ref, g_ref, bg_ref, dp_any, dg_ref, da1_ref, da2_ref, st_ref):
        del dp_any

        @pl.when(pl.program_id(0) == 0)
        def _():
            st_ref[...] = jnp.zeros_like(st_ref)

        gt = _sig(g_ref[...] + bg_ref[...])
        g1 = gt[:, :D]
        g2 = gt[:, D:]
        dm = dm_ref[...]
        da1_ref[...] = (dm * g1).astype(da1_ref.dtype)
        da2_ref[...] = (dm * g2).astype(da2_ref.dtype)
        dg1 = dm * a1_ref[...] * g1 * (1.0 - g1)
        dg2 = dm * a2_ref[...] * g2 * (1.0 - g2)
        st_ref[0:1, 0:D] += _colsum(dg1)
        st_ref[0:1, D:2 * D] += _colsum(dg2)
        dg_ref[:, 0:D] = dg1.astype(dg_ref.dtype)
        dg_ref[:, D:2 * D] = dg2.astype(dg_ref.dtype)

    return pl.pallas_call(
        body, name="gate_bwd", grid=(L // TL,),
        in_specs=[_rt(D), _rt(D), _rt(D), _rt(2 * D, OG // (2 * D)), _cst((1, 2 * D)),
                  pl.BlockSpec(memory_space=pl.ANY)],
        out_specs=[_rt(2 * D, OG // (2 * D)), _rt(D), _rt(D), _cst((8, 2 * D))],
        out_shape=[jax.ShapeDtypeStruct(dp.shape, dp.dtype), jax.ShapeDtypeStruct((L, D), _MXU),
                   jax.ShapeDtypeStruct((L, D), _MXU), jax.ShapeDtypeStruct((8, 2 * D), F32)],
        input_output_aliases={5: 0},
        compiler_params=_params(("arbitrary",)),
    )(dmg, a1, a2, p, bg, dp)


def _merge_fwd(yssd, ygm, p, bg, ws, wg, wo):
    L = yssd.shape[0]
    tm = TL

    def body(ys_ref, yg_ref, g_ref, bg_ref, ws_ref, wg_ref, wo_ref, a1_ref, a2_ref, m_ref, o_ref):
        a1 = _dot(ys_ref[...], ws_ref[...], NN)
        a2 = _dot(yg_ref[...], wg_ref[...], NN)
        gt = _sig(g_ref[...] + bg_ref[...])
        mg = gt[:, :D] * a1 + gt[:, D:] * a2
        a1_ref[...] = a1
        a2_ref[...] = a2
        m_ref[...] = mg.astype(m_ref.dtype)
        o_ref[...] = _dot(mg, wo_ref[...], NN)

    rows = pl.BlockSpec((tm, D), lambda i: (i, 0))
    f32s = jax.ShapeDtypeStruct((L, D), F32)
    return pl.pallas_call(
        body, name="merge_fwd", grid=(L // tm,),
        in_specs=[rows, rows, pl.BlockSpec((tm, 2 * D), lambda i: (i, OG // (2 * D))), _cst((1, 2 * D)),
                  _cst((D, D)), _cst((D, D)), _cst((D, D))],
        out_specs=[rows, rows, rows, rows],
        out_shape=[f32s, f32s, jax.ShapeDtypeStruct((L, D), _MXU), f32s],
        compiler_params=_params(("parallel",)),
    )(yssd, ygm, p, bg, ws, wg, wo)


def _merge_bwd(dout, a1, a2, p, bg, wo, ws, wg, dp):
    L = a1.shape[0]
    tm = TL

    def body(do_ref, a1_ref, a2_ref, g_ref, bg_ref, wo_ref, ws_ref, wg_ref, dp_any,
             dg_ref, da1_ref, da2_ref, st_ref, dys_ref, dym_ref):
        del dp_any

        @pl.when(pl.program_id(0) == 0)
        def _():
            st_ref[...] = jnp.zeros_like(st_ref)

        dm = _dot(do_ref[...], wo_ref[...], NT)
        gt = _sig(g_ref[...] + bg_ref[...])
        g1 = gt[:, :D]
        g2 = gt[:, D:]
        da1 = (dm * g1).astype(da1_ref.dtype)
        da2 = (dm * g2).astype(da2_ref.dtype)
        da1_ref[...] = da1
        da2_ref[...] = da2
        dg1 = dm * a1_ref[...] * g1 * (1.0 - g1)
        dg2 = dm * a2_ref[...] * g2 * (1.0 - g2)
        st_ref[0:1, 0:D] += _colsum(dg1)
        st_ref[0:1, D:2 * D] += _colsum(dg2)
        dg_ref[:, 0:D] = dg1.astype(dg_ref.dtype)
        dg_ref[:, D:2 * D] = dg2.astype(dg_ref.dtype)
        dys_ref[...] = _dot(da1, ws_ref[...], NT)
        dym_ref[...] = _dot(da2, wg_ref[...], NT)

    rows = pl.BlockSpec((tm, D), lambda i: (i, 0))
    gates = pl.BlockSpec((tm, 2 * D), lambda i: (i, OG // (2 * D)))
    f32s = jax.ShapeDtypeStruct((L, D), F32)
    mxus = jax.ShapeDtypeStruct((L, D), _MXU)
    return pl.pallas_call(
        body, name="merge_bwd", grid=(L // tm,),
        in_specs=[rows, rows, rows, gates, _cst((1, 2 * D)), _cst((D, D)), _cst((D, D)), _cst((D, D)),
                  pl.BlockSpec(memory_space=pl.ANY)],
        out_specs=[gates, rows, rows, _cst((8, 2 * D)), rows, rows],
        out_shape=[jax.ShapeDtypeStruct(dp.shape, dp.dtype), mxus, mxus,
                   jax.ShapeDtypeStruct((8, 2 * D), F32), f32s, f32s],
        input_output_aliases={8: 0},
        compiler_params=_params(("arbitrary",)),
    )(dout, a1, a2, p, bg, wo, ws, wg, dp)


def _res1_fwd(xn, out, modx, g, b):
    L = out.shape[0]

    def body(xn_ref, o_ref, mx_ref, g_ref, b_ref, r1_ref, h2_ref):
        r1 = ALPHA * xn_ref[...] + mx_ref[2:3, :] * o_ref[...]
        xhat, _ = _ln(r1)
        x1 = xhat * g_ref[...] + b_ref[...]
        r1_ref[...] = r1
        h2_ref[...] = (x1 * (1.0 + mx_ref[4:5, :]) + mx_ref[3:4, :]).astype(h2_ref.dtype)

    return pl.pallas_call(
        body, name="res1_fwd", grid=(L // TL,),
        in_specs=[_rt(D), _rt(D), _cst((8, D)), _cst((1, D)), _cst((1, D))],
        out_specs=[_rt(D), _rt(D)],
        out_shape=[jax.ShapeDtypeStruct((L, D), F32), jax.ShapeDtypeStruct((L, D), _MXU)],
        compiler_params=_params(("parallel",)),
    )(xn, out, modx, g, b)


HFF = DFF // 2


def _mm_f13_glu(h2, w13i):
    L = h2.shape[0]
    tm = 512

    def body(a_ref, b_ref, f_ref, ff_ref):
        f = _dot(a_ref[...], b_ref[...], NT)
        f_ref[...] = f
        f1 = f[:, :HFF]
        ff_ref[...] = (f1 * _sig(f1) * f[:, HFF:]).astype(ff_ref.dtype)

    return pl.pallas_call(
        body, name="mm_f13_glu", grid=(DFF // HFF, L // tm),
        in_specs=[pl.BlockSpec((tm, D), lambda j, i: (i, 0)), pl.BlockSpec((2 * HFF, D), lambda j, i: (j, 0))],
        out_specs=[pl.BlockSpec((tm, 2 * HFF), lambda j, i: (i, j)), pl.BlockSpec((tm, HFF), lambda j, i: (i, j))],
        out_shape=[jax.ShapeDtypeStruct((L, 2 * DFF), F32), jax.ShapeDtypeStruct((L, DFF), _MXU)],
        compiler_params=_params(("parallel", "parallel")),
    )(h2, w13i)


def _mm_dff_glu(do2, w_ff2_f, f13i):
    L = do2.shape[0]
    tm = 512

    def body(a_ref, b_ref, f_ref, o_ref):
        d = _dot(a_ref[...], b_ref[...], NT)
        f1 = f_ref[:, :HFF]
        s = _sig(f1)
        o_ref[:, :HFF] = (d * f_ref[:, HFF:] * (s * (1.0 + f1 * (1.0 - s)))).astype(o_ref.dtype)
        o_ref[:, HFF:] = (d * f1 * s).astype(o_ref.dtype)

    return pl.pallas_call(
        body, name="mm_dff_glu", grid=(DFF // HFF, L // tm),
        in_specs=[pl.BlockSpec((tm, D), lambda j, i: (i, 0)), pl.BlockSpec((HFF, D), lambda j, i: (j, 0)),
                  pl.BlockSpec((tm, 2 * HFF), lambda j, i: (i, j))],
        out_specs=pl.BlockSpec((tm, 2 * HFF), lambda j, i: (i, j)),
        out_shape=jax.ShapeDtypeStruct((L, 2 * DFF), _MXU),
        compiler_params=_params(("parallel", "parallel")),
    )(do2, w_ff2_f, f13i)


def _glu_fwd(f13):
    L = f13.shape[0]

    def body(f1_ref, f3_ref, o_ref):
        f1 = f1_ref[...]
        o_ref[...] = (f1 * _sig(f1) * f3_ref[...]).astype(o_ref.dtype)

    return pl.pallas_call(
        body, name="glu_fwd", grid=(L // TL,),
        in_specs=[_rt(DFF, 0), _rt(DFF, 1)], out_specs=_rt(DFF),
        out_shape=jax.ShapeDtypeStruct((L, DFF), _MXU),
        compiler_params=_params(("parallel",)),
    )(f13, f13)


def _glu_bwd(dff, f13):
    L = f13.shape[0]

    def body(d_ref, f1_ref, f3_ref, o_ref):
        f1 = f1_ref[...]
        s = _sig(f1)
        d = d_ref[...]
        o_ref[:, 0:DFF] = (d * f3_ref[...] * (s * (1.0 + f1 * (1.0 - s)))).astype(o_ref.dtype)
        o_ref[:, DFF:2 * DFF] = (d * f1 * s).astype(o_ref.dtype)

    return pl.pallas_call(
        body, name="glu_bwd", grid=(L // TL,),
        in_specs=[_rt(DFF), _rt(DFF, 0), _rt(DFF, 1)], out_specs=_rt(2 * DFF),
        out_shape=jax.ShapeDtypeStruct((L, 2 * DFF), _MXU),
        compiler_params=_params(("parallel",)),
    )(dff, f13, f13)


def _res2(r1, o2, tgt, modx, g1, b1, g2, b2):
    L = r1.shape[0]

    def body(r1_ref, o2_ref, t_ref, mx_ref, g1_ref, b1_ref, g2_ref, b2_ref,
             dr2_ref, do2_ref, st_ref, loss_ref):
        @pl.when(pl.program_id(0) == 0)
        def _():
            st_ref[...] = jnp.zeros_like(st_ref)
            loss_ref[...] = jnp.zeros_like(loss_ref)

        xh1, _ = _ln(r1_ref[...])
        x1 = xh1 * g1_ref[...] + b1_ref[...]
        o2 = o2_ref[...]
        g2x = mx_ref[5:6, :]
        xh2, rstd2 = _ln(ALPHA * x1 + g2x * o2)
        err = xh2 * g2_ref[...] + b2_ref[...] - t_ref[...]
        per_tok = jnp.mean(err * err, axis=-1, keepdims=True)
        loss_ref[...] += 0.5 * jnp.sum(per_tok, axis=0, keepdims=True)
        dy = err * (1.0 / D)
        st_ref[0:1, :] += _colsum(dy * xh2)
        st_ref[1:2, :] += _colsum(dy)
        dr2 = _ln_bwd(dy * g2_ref[...], xh2, rstd2)
        st_ref[2:3, :] += _colsum(dr2 * o2)
        dr2_ref[...] = dr2
        do2_ref[...] = (g2x * dr2).astype(do2_ref.dtype)

    return pl.pallas_call(
        body, name="res2", grid=(L // TL,),
        in_specs=[_rt(D), _rt(D), _rt(D), _cst((8, D))] + [_cst((1, D))] * 4,
        out_specs=[_rt(D), _rt(D), _cst((8, D)), _cst((8, 128))],
        out_shape=[jax.ShapeDtypeStruct((L, D), F32), jax.ShapeDtypeStruct((L, D), _MXU),
                   jax.ShapeDtypeStruct((8, D), F32), jax.ShapeDtypeStruct((8, 128), F32)],
        compiler_params=_params(("arbitrary",)),
    )(r1, o2, tgt, modx, g1, b1, g2, b2)


def _res1_bwd(dr2, dh2, r1, out, modx, g1, b1):
    L = r1.shape[0]

    def body(dr2_ref, dh2_ref, r1_ref, o_ref, mx_ref, g_ref, b_ref, dr1_ref, do_ref, st_ref):
        @pl.when(pl.program_id(0) == 0)
        def _():
            st_ref[...] = jnp.zeros_like(st_ref)

        xh1, rstd1 = _ln(r1_ref[...])
        x1 = xh1 * g_ref[...] + b_ref[...]
        dh2 = dh2_ref[...]
        dx1 = ALPHA * dr2_ref[...] + dh2 * (1.0 + mx_ref[4:5, :])
        st_ref[0:1, :] += _colsum(dh2 * x1)
        st_ref[1:2, :] += _colsum(dh2)
        st_ref[2:3, :] += _colsum(dx1 * xh1)
        st_ref[3:4, :] += _colsum(dx1)
        dr1 = _ln_bwd(dx1 * g_ref[...], xh1, rstd1)
        st_ref[4:5, :] += _colsum(dr1 * o_ref[...])
        dr1_ref[...] = dr1
        do_ref[...] = (mx_ref[2:3, :] * dr1).astype(do_ref.dtype)

    return pl.pallas_call(
        body, name="res1_bwd", grid=(L // TL,),
        in_specs=[_rt(D), _rt(D), _rt(D), _rt(D), _cst((8, D)), _cst((1, D)), _cst((1, D))],
        out_specs=[_rt(D), _rt(D), _cst((8, D))],
        out_shape=[jax.ShapeDtypeStruct((L, D), F32), jax.ShapeDtypeStruct((L, D), _MXU),
                   jax.ShapeDtypeStruct((8, D), F32)],
        compiler_params=_params(("arbitrary",)),
    )(dr2, dh2, r1, out, modx, g1, b1)


def _conv_bwd(dxf, dxb, p, conv_w8, conv_b, dp):
    RT = p.shape[0]
    chunks = _seq_chunks(RT - TL)

    def body(df_ref, db_ref, p_ref, w_ref, b_ref, dp_any, o_ref, dw_ref, dbias_ref, dpre_s):
        del dp_any
        w = w_ref[...]
        bias = b_ref[...]
        srow = lax.broadcasted_iota(jnp.int32, (8, 128), 0)
        dwacc = jnp.zeros((8, 128), F32)
        dbacc = jnp.zeros((1, 128), F32)
        for r0, first, last in chunks:
            taps = _conv_taps(p_ref, r0, first, last)
            pre = bias + sum(w[k:k + 1, :] * taps[k] for k in range(5))
            s = _sig(pre)
            dpre = (df_ref[pl.ds(r0, TL), :] + db_ref[pl.ds(r0, TL), :]) * (s * (1.0 + pre * (1.0 - s)))
            dpre_s[pl.ds(r0, TL), :] = dpre
            dbacc = dbacc + _colsum(dpre)
            for k in range(5):
                dwacc = dwacc + jnp.where(srow == k, _colsum(dpre * taps[k]), 0.0)
        for r0, first, last in chunks:
            taps = _conv_taps(dpre_s, r0, first, last)
            dx = sum(w[k:k + 1, :] * taps[4 - k] for k in range(5))
            o_ref[pl.ds(r0, TL), :] = dx.astype(o_ref.dtype)
        dw_ref[...] = dwacc
        dbias_ref[...] = jnp.broadcast_to(dbacc, (8, 128))

    cspec = pl.BlockSpec((RT, 128), lambda j: (0, j))
    wspec = pl.BlockSpec((8, 128), lambda j: (0, j))
    return pl.pallas_call(
        body, name="conv_bwd", grid=(12,),
        in_specs=[cspec, cspec, pl.BlockSpec((RT, 128), lambda j: (0, _xbc_colblk(j))),
                  wspec, pl.BlockSpec((1, 128), lambda j: (0, j)), pl.BlockSpec(memory_space=pl.ANY)],
        out_specs=[pl.BlockSpec((RT, 128), lambda j: (0, _xbc_colblk(j))), wspec, wspec],
        out_shape=[jax.ShapeDtypeStruct(dp.shape, dp.dtype), jax.ShapeDtypeStruct((8, 1536), F32),
                   jax.ShapeDtypeStruct((8, 1536), F32)],
        scratch_shapes=[pltpu.VMEM((RT, 128), F32)],
        input_output_aliases={5: 0},
        compiler_params=_params(("parallel",)),
    )(dxf, dxb, p, conv_w8, conv_b, dp)


def _dt_bwd(ddf, ddb, dp):
    RT = ddf.shape[0]

    def body(f_ref, b_ref, dp_any, o_ref, st_ref):
        del dp_any

        @pl.when(pl.program_id(0) == 0)
        def _():
            st_ref[...] = jnp.zeros_like(st_ref)

        s = f_ref[...] + b_ref[...]
        o_ref[...] = s.astype(o_ref.dtype)
        st_ref[0:1, :] += _colsum(s)

    return pl.pallas_call(
        body, name="dt_bwd", grid=(RT // TL,),
        in_specs=[_rt(128), _rt(128), pl.BlockSpec(memory_space=pl.ANY)],
        out_specs=[_rt(128, ODT // 128), _cst((8, 128))],
        out_shape=[jax.ShapeDtypeStruct(dp.shape, dp.dtype), jax.ShapeDtypeStruct((8, 128), F32)],
        input_output_aliases={2: 0},
        compiler_params=_params(("arbitrary",)),
    )(ddf, ddb, dp)


def _ln0_bwd(dh1, dr1, x, ctx, g, b, modx, modc):
    L = x.shape[0]
    nt = L // TL

    def body(dh_ref, dr1_ref, x_ref, c_ref, g_ref, b_ref, mx_ref, mc_ref, gx_ref, st_ref):
        i = pl.program_id(0)
        isc = i == nt

        @pl.when(i == 0)
        def _():
            st_ref[...] = jnp.zeros_like(st_ref)

        xin = jnp.where(isc, c_ref[...], x_ref[...])
        xhat, rstd = _ln(xin)
        xn = xhat * g_ref[...] + b_ref[...]
        sc = jnp.where(isc, mc_ref[1:2, :], mx_ref[1:2, :])
        dh = dh_ref[...]
        lat = jnp.where(isc, 0.0, 1.0)
        dxn = dh * (1.0 + sc) + (lat * ALPHA) * dr1_ref[...]
        tsh = _colsum(dh)
        tsc = _colsum(dh * xn)
        st_ref[0:1, :] += lat * tsh
        st_ref[1:2, :] += lat * tsc
        st_ref[2:3, :] += (1.0 - lat) * tsh
        st_ref[3:4, :] += (1.0 - lat) * tsc
        st_ref[4:5, :] += _colsum(dxn * xhat)
        st_ref[5:6, :] += _colsum(dxn)

        @pl.when(i < nt)
        def _():
            gx_ref[...] = _ln_bwd(dxn * g_ref[...], xhat, rstd)

    return pl.pallas_call(
        body, name="ln0_bwd", grid=(nt + 1,),
        in_specs=[_rt(D), _rtc(D, nt), _rtc(D, nt), _cst((TL, D)), _cst((1, D)), _cst((1, D)),
                  _cst((8, D)), _cst((8, D))],
        out_specs=[_rtc(D, nt), _cst((8, D))],
        out_shape=[jax.ShapeDtypeStruct((L, D), F32), jax.ShapeDtypeStruct((8, D), F32)],
        compiler_params=_params(("arbitrary",)),
    )(dh1, dr1, x, ctx, g, b, modx, modc)


def _perm_cols(w):
    pad = jnp.zeros((w.shape[0], NPJ - NNAT), w.dtype)
    return jnp.concatenate([w[:, 0:1024], w[:, 2592:3616], w[:, 3616:4640], w[:, 1024:2048],
                            w[:, 4640:6688], w[:, 2048:2304], w[:, 2304:2560], w[:, 2560:2592], pad],
                           axis=1)


SECTIONS = ((0, 1024, OZ), (1024, 2048, OXS), (2048, 2304, OB), (2304, 2560, OC), (2560, 2592, ODT),
            (2592, 3616, OU), (3616, 4640, OV), (4640, 6688, OG))


def _perm_from_blocks(ga):
    n = ga.shape[2]
    pieces = []
    for na, nb, _ in sorted(SECTIONS, key=lambda sec: sec[2]):
        for k in range(NDEV):
            lo, hi = max(na, k * n), min(nb, (k + 1) * n)
            if lo < hi:
                pieces.append(ga[k][:, lo - k * n:hi - k * n])
    pieces.append(jnp.zeros((ga.shape[1], NPJ - NNAT), ga.dtype))
    return jnp.concatenate(pieces, axis=1)


def _blocks_from_perm(gp, n):
    blocks = []
    for k in range(NDEV):
        pieces = []
        for na, nb, po in SECTIONS:
            lo, hi = max(na, k * n), min(nb, (k + 1) * n)
            if lo < hi:
                pieces.append(gp[:, po + lo - na:po + hi - na])
        blocks.append(jnp.concatenate(pieces, axis=1))
    return jnp.stack(blocks)


def _padded(n, row_align):
    unit = row_align * D
    return -(-n // unit) * unit if row_align else n


def _slab(arrs, rows, row_align=0):
    parts = []
    for a in arrs:
        f = a.reshape(-1)
        parts.append(jnp.pad(f, (0, _padded(f.shape[0], row_align) - f.shape[0])))
    flat = jnp.concatenate(parts)
    flat = jnp.pad(flat, (0, rows * D - flat.shape[0]))
    return flat.reshape(rows, D)


def _unslab(slab, shapes, row_align=0):
    out, off = [], 0
    for shp in shapes:
        n = 1
        for s in shp:
            n *= s
        r0, r1 = off // D, -(-(off + n) // D)
        out.append(slab[r0:r1].reshape(-1)[off - r0 * D:off - r0 * D + n].reshape(shp))
        off += _padded(n, row_align)
    return out


def _row(v):
    return v.reshape(1, -1)


def _t(a):
    return jnp.swapaxes(a, 0, 1)


def _pad_rows(a, rows):
    return jnp.pad(a, ((0, rows - a.shape[0]), (0, 0)))


BIG = ["w_in", "w_ssd_proj", "w_gm_proj", "w_out", "w_ff1", "w_ff3", "w_ff2"]
BIG_ROWS = 2304
BIG_ALIGN = 16
REPL = ["c_ctx", "ln0_g", "ln0_b", "b_ada", "conv_b", "dt_bias", "a_log", "d_skip", "ssd_norm_g",
        "gm_norm_g", "gm_norm_b", "w_spatial", "b_spatial", "b_gate", "ln1_g", "ln1_b", "ln2_g", "ln2_b"]
SMALL_ROWS = 160
WEIGHTS = ["c_ctx", "ln0_g", "ln0_b", "w_ada", "b_ada", "w_in", "conv_w", "conv_b", "dt_bias", "a_log",
           "d_skip", "ssd_norm_g", "gm_norm_g", "gm_norm_b", "w_spatial", "b_spatial", "b_gate",
           "w_ssd_proj", "w_gm_proj", "w_out", "ln1_g", "ln1_b", "w_ff1", "w_ff3", "w_ff2", "ln2_g", "ln2_b"]


def kernel(x, c, ctx, c_ctx, ln0_g, ln0_b, w_ada, b_ada, w_in, conv_w, conv_b, dt_bias, a_log, d_skip, ssd_norm_g, gm_norm_g, gm_norm_b, w_spatial, b_spatial, b_gate, w_ssd_proj, w_gm_proj, w_out, ln1_g, ln1_b, w_ff1, w_ff3, w_ff2, ln2_g, ln2_b, loss_target, m_c_ctx, m_ln0_g, m_ln0_b, m_w_ada, m_b_ada, m_w_in, m_conv_w, m_conv_b, m_dt_bias, m_a_log, m_d_skip, m_ssd_norm_g, m_gm_norm_g, m_gm_norm_b, m_w_spatial, m_b_spatial, m_b_gate, m_w_ssd_proj, m_w_gm_proj, m_w_out, m_ln1_g, m_ln1_b, m_w_ff1, m_w_ff3, m_w_ff2, m_ln2_g, m_ln2_b, v_c_ctx, v_ln0_g, v_ln0_b, v_w_ada, v_b_ada, v_w_in, v_conv_w, v_conv_b, v_dt_bias, v_a_log, v_d_skip, v_ssd_norm_g, v_gm_norm_g, v_gm_norm_b, v_w_spatial, v_b_spatial, v_b_gate, v_w_ssd_proj, v_w_gm_proj, v_w_out, v_ln1_g, v_ln1_b, v_w_ff1, v_w_ff3, v_w_ff2, v_ln2_g, v_ln2_b):
    W = dict(c_ctx=c_ctx, ln0_g=ln0_g, ln0_b=ln0_b, w_ada=w_ada, b_ada=b_ada, w_in=w_in, conv_w=conv_w,
             conv_b=conv_b, dt_bias=dt_bias, a_log=a_log, d_skip=d_skip, ssd_norm_g=ssd_norm_g,
             gm_norm_g=gm_norm_g, gm_norm_b=gm_norm_b, w_spatial=w_spatial, b_spatial=b_spatial,
             b_gate=b_gate, w_ssd_proj=w_ssd_proj, w_gm_proj=w_gm_proj, w_out=w_out, ln1_g=ln1_g,
             ln1_b=ln1_b, w_ff1=w_ff1, w_ff3=w_ff3, w_ff2=w_ff2, ln2_g=ln2_g, ln2_b=ln2_b)
    M = dict(c_ctx=m_c_ctx, ln0_g=m_ln0_g, ln0_b=m_ln0_b, w_ada=m_w_ada, b_ada=m_b_ada, w_in=m_w_in,
             conv_w=m_conv_w, conv_b=m_conv_b, dt_bias=m_dt_bias, a_log=m_a_log, d_skip=m_d_skip,
             ssd_norm_g=m_ssd_norm_g, gm_norm_g=m_gm_norm_g, gm_norm_b=m_gm_norm_b,
             w_spatial=m_w_spatial, b_spatial=m_b_spatial, b_gate=m_b_gate, w_ssd_proj=m_w_ssd_proj,
             w_gm_proj=m_w_gm_proj, w_out=m_w_out, ln1_g=m_ln1_g, ln1_b=m_ln1_b, w_ff1=m_w_ff1,
             w_ff3=m_w_ff3, w_ff2=m_w_ff2, ln2_g=m_ln2_g, ln2_b=m_ln2_b)
    V = dict(c_ctx=v_c_ctx, ln0_g=v_ln0_g, ln0_b=v_ln0_b, w_ada=v_w_ada, b_ada=v_b_ada, w_in=v_w_in,
             conv_w=v_conv_w, conv_b=v_conv_b, dt_bias=v_dt_bias, a_log=v_a_log, d_skip=v_d_skip,
             ssd_norm_g=v_ssd_norm_g, gm_norm_g=v_gm_norm_g, gm_norm_b=v_gm_norm_b,
             w_spatial=v_w_spatial, b_spatial=v_b_spatial, b_gate=v_b_gate, w_ssd_proj=v_w_ssd_proj,
             w_gm_proj=v_w_gm_proj, w_out=v_w_out, ln1_g=v_ln1_g, ln1_b=v_ln1_b, w_ff1=v_w_ff1,
             w_ff3=v_w_ff3, w_ff2=v_w_ff2, ln2_g=v_ln2_g, ln2_b=v_ln2_b)

    me = 4 * lax.axis_index("x") + 2 * lax.axis_index("y") + lax.axis_index("c")
    xl, cx, tgt = x[0], ctx[0], loss_target[0]
    L = xl.shape[0]
    assert cx.shape[0] == TL and L % TL == 0
    ada_n = w_ada.shape[2]
    cw_n = conv_w.shape[2]

    small1 = _pad_rows(jnp.concatenate([c, _slab([conv_w[0]], 1)], axis=0), 8)
    g1 = _all_gather(small1, "ag_small")
    c_all = g1[:, 0, :]
    conv_w_full = g1[:, 1, :5 * cw_n].reshape(NDEV, 5, cw_n).transpose(1, 0, 2).reshape(5, NDEV * cw_n)
    sq = w_ssd_proj.shape[1]
    ffr = w_ff2.shape[1]
    ffc = w_ff1.shape[2]
    late = [jnp.concatenate([w_ssd_proj[0], w_gm_proj[0], w_out[0], w_ff2[0]], axis=0).astype(_MXU),
            _t(w_ff1[0]).astype(_MXU), _t(w_ff3[0]).astype(_MXU)]

    c16 = _pad_rows(jnp.concatenate([c_all, _row(c_ctx)], axis=0), 16)
    b_ada_sh = lax.dynamic_slice(b_ada, (0, ada_n * me), (1, ada_n))
    modp = _ada_fwd(c16, w_ada[0], b_ada_sh)
    mod16 = _all_gather(modp, "ag_mod").transpose(1, 0, 2).reshape(16, NDEV * ada_n)

    ga, = _all_gather_multi([w_in[0].astype(_MXU)], "ag_w_in")
    ga, late, mod16 = lax.optimization_barrier((ga, late, mod16))
    lw_send, lw_recv, lw_src, lw_land, lw_token = _exchange_start(late, "ag_late_start", gather=True)
    w_in_p = _perm_from_blocks(ga)
    modx = _pad_rows(lax.dynamic_slice(mod16, (me, 0), (1, 6 * D)).reshape(6, D), 8) + lw_token[0, 0]
    modc = _pad_rows(mod16[8].reshape(6, D), 8)

    g0, b0 = _row(ln0_g), _row(ln0_b)
    xn, h1 = _ln0_fwd(xl, cx, g0, b0, modx, modc)
    p = _mm(h1, w_in_p, "nn", F32, "mm_p")
    conv_w8 = _pad_rows(conv_w_full, 8)
    xbc = _conv_fwd(p, conv_w8, conv_b)
    prm = _pad_rows(jnp.pad(jnp.stack([dt_bias.reshape(32), a_log.reshape(32)]), ((0, 0), (0, 96))), 8)
    yf, yb, hpf, hpb = _ssd2_fwd(xbc, p, prm)
    lw_land = _exchange_wait(lw_send, lw_recv, lw_src, lw_land, yf, "ag_late_wait", gather=True)
    fw_send, fw_recv, lw_land, fw_token = _forward_start(lw_land, "ag_fwd_start")
    dsk = _row(jnp.repeat(d_skip[0, 0] + d_skip[0, 1], HP)) + fw_token[0:1, 0:1]
    ws_m = w_spatial[0].astype(_MXU)
    bsT = jnp.pad(b_spatial[0].T, ((0, 0), (0, 120)))
    mixp = (dsk, ssd_norm_g, gm_norm_g, gm_norm_b, ws_m, bsT)
    yssd, ygm = _mix_fwd(yf, yb, p, xbc, *mixp)
    gb, gc1, gc2 = _forward_wait(fw_send, fw_recv, lw_land, yssd, "ag_fwd_wait")

    def with_own(g, mine, k):
        return jnp.where(me == k, mine, g[k])

    gb = jnp.stack([with_own(gb, late[0], k) for k in range(NDEV)])
    w_ssd_f = gb[:, 0:sq].reshape(NDEV * sq, D)
    w_gm_f = gb[:, sq:2 * sq].reshape(NDEV * sq, D)
    w_out_f = gb[:, 2 * sq:3 * sq].reshape(NDEV * sq, D)
    w_ff2_f = gb[:, 3 * sq:3 * sq + ffr].reshape(NDEV * ffr, D)
    assert HFF == (NDEV // 2) * ffc
    halves = (range(0, NDEV // 2), range(NDEV // 2, NDEV))
    w13i = jnp.concatenate([with_own(g, mine, k) for ks in halves for g, mine in ((gc1, late[1]), (gc2, late[2]))
                            for k in ks], axis=0)
    a1, a2, merged, out = _merge_fwd(yssd, ygm, p, b_gate, w_ssd_f, w_gm_f, w_out_f)
    r1, h2 = _res1_fwd(xn, out, modx, ln1_g, ln1_b)
    f13, ff = _mm_f13_glu(h2, w13i)
    o2 = _mm(ff, w_ff2_f, "nn", F32, "mm_o2")

    dr2, do2, st2, loss_slab = _res2(r1, o2, tgt, modx, ln1_g, ln1_b, ln2_g, ln2_b)
    loss = lax.psum(loss_slab[0, 0], ("x", "y", "c"))
    df13 = _mm_dff_glu(do2, w_ff2_f, f13)
    dh2 = _mm(df13, w13i, "nn", F32, "mm_dh2")
    dw_ff2 = _mm(ff, do2, "tn", _MXU, "mm_dw_ff2")
    dw13i = _mm(df13, h2, "tn", _MXU, "mm_dw13")

    def owner_blocks(first):
        return jnp.concatenate([dw13i[t * 2 * HFF + first:t * 2 * HFF + first + HFF].reshape(NDEV // 2, ffc, D)
                                for t in range(2)], axis=0)

    xff = [dw_ff2.reshape(NDEV, ffr, D), owner_blocks(0), owner_blocks(HFF)]
    ff_send, ff_recv, ff_src, ff_land, ff_token = _exchange_start(xff, "xchg_ff_start")
    modx = modx + ff_token[0, 0]
    dr1, dout, st1 = _res1_bwd(dr2, dh2, r1, out, modx, ln1_g, ln1_b)
    dw_out = _mm(merged, dout, "tn", _MXU, "mm_dw_out")
    dp = jnp.zeros((L + TL, NPJ), _MXU)
    dp, da1, da2, stg, dys, dym = _merge_bwd(dout, a1, a2, p, b_gate, w_out_f, w_ssd_f, w_gm_f, dp)
    dw_ssd = _mm(yssd, da1, "tn", _MXU, "mm_dw_ssd")
    dw_gm = _mm(ygm, da2, "tn", _MXU, "mm_dw_gm")
    xsq = [jnp.concatenate([dw_ssd.reshape(NDEV, sq, D), dw_gm.reshape(NDEV, sq, D),
                            dw_out.reshape(NDEV, sq, D)], axis=1)]
    sq_send, sq_recv, sq_src, sq_land, sq_token = _exchange_start(xsq, "xchg_sq_start")
    mixp = (dsk + sq_token[0:1, 0:1],) + mixp[1:]
    dp, dyd, stm, dws, dbsT = _mix_bwd(dys, dym, yf, yb, p, xbc, dp, *mixp)
    dxf, dxb, ddf, ddb, sts = _ssd2_bwd(xbc, p, prm, dsk, dyd, hpf, hpb)
    dp, dcw, dcb = _conv_bwd(dxf, dxb, p, conv_w8, conv_b, dp)
    dp, std = _dt_bwd(ddf, ddb, dp)
    hw = D // 2
    xin_a = [_blocks_from_perm(_mm(h1[:, :hw], dp, "tn", _MXU, "mm_dw_in_a"), w_in.shape[2])]
    ina_send, ina_recv, ina_src, ina_land, ina_token = _exchange_start(xin_a, "xchg_in_a_start")
    h1b, ina_token = lax.optimization_barrier((h1[:, hw:], ina_token))
    xin_b = [_blocks_from_perm(_mm(h1b, dp, "tn", _MXU, "mm_dw_in_b"), w_in.shape[2])]
    inb_send, inb_recv, inb_src, inb_land, inb_token = _exchange_start(xin_b, "xchg_in_b_start")
    dp, inb_token = lax.optimization_barrier((dp, inb_token))
    dh1 = _mm(dp, w_in_p, "nt", F32, "mm_dh1")
    modx = modx + (ina_token[0, 0] + inb_token[0, 0])
    grad_x, st0 = _ln0_bwd(dh1, dr1, xl, cx, g0, b0, modx, modc)

    zero = jnp.zeros((D,), F32)
    dmod = jnp.stack([jnp.concatenate([st0[0], st0[1], st1[4], st1[1], st1[0], st2[2]]),
                      jnp.concatenate([st0[2], st0[3], zero, zero, zero, zero])])
    g16 = _all_gather(_pad_rows(dmod, 8), "ag_dmod")[:, 0:2, :].reshape(16, 6 * D)
    g16_sh = lax.dynamic_slice(g16, (0, ada_n * me), (16, ada_n))
    c16b = jnp.stack([c_all, jnp.broadcast_to(_row(c_ctx), (NDEV, D))], axis=1).reshape(16, D)
    dw_ada, db_ada8, dcc8 = _ada_bwd(c16b, g16, g16_sh, w_ada[0])

    part = dict(
        c_ctx=dcc8[0], ln0_g=st0[4], ln0_b=st0[5], conv_w=dcw[0:5], conv_b=dcb[0],
        dt_bias=std[0, 0:32], a_log=sts[0, 0:32], d_skip=jnp.tile(sts[1, 0:16], 2),
        ssd_norm_g=stm[0], gm_norm_g=stm[1], gm_norm_b=stm[2], w_spatial=dws,
        b_spatial=dbsT[:, 0:8].T, b_gate=stg[0], ln1_g=st1[2], ln1_b=st1[3], ln2_g=st2[0], ln2_b=st2[1])
    pnames = list(part)
    psum8 = _sum8(_all_gather(_slab([part[n] for n in pnames], SMALL_ROWS), "ag_smallgrads"), "sum_smallgrads")
    small = dict(zip(pnames, _unslab(psum8, [part[n].shape for n in pnames])))
    grads = {n: small[n].reshape(W[n].shape) for n in pnames if n != "conv_w"}
    grads["conv_w"] = lax.dynamic_slice(small["conv_w"], (0, cw_n * me), (5, cw_n)).reshape(conv_w.shape)
    grads["b_ada"] = db_ada8[0:1]
    grads["w_ada"] = dw_ada.reshape(w_ada.shape)

    delta, new_m, new_v = {}, {}, {}

    def adam_group(names, rows, tag, align=0):
        shapes = [W[n].shape for n in names]
        outs = _adamw(*[_slab([src[n] for n in names], rows, align) for src in (grads, W, M, V)], tag)
        for res, slab in zip((delta, new_m, new_v), outs):
            for n, a in zip(names, _unslab(slab, shapes, align)):
                res[n] = a

    adam_group(REPL + ["conv_w"], SMALL_ROWS, "adamw_small")
    res = _adamw(grads["w_ada"][0], w_ada[0], m_w_ada[0], v_w_ada[0], "adamw_w_ada")
    delta["w_ada"], new_m["w_ada"], new_v["w_ada"] = [a[None] for a in res]

    rff = _exchange_wait(ff_send, ff_recv, ff_src, ff_land, st0, "xchg_ff_wait")
    rsq = _exchange_wait(sq_send, sq_recv, sq_src, sq_land, rff[0], "xchg_sq_wait")
    rin_a = _exchange_wait(ina_send, ina_recv, ina_src, ina_land, delta["ln2_b"], "xchg_in_a_wait")
    rin_b = _exchange_wait(inb_send, inb_recv, inb_src, inb_land, rin_a[0], "xchg_in_b_wait")
    rin = jnp.concatenate([rin_a[0], rin_b[0]], axis=1)

    def own(blocks):
        return lax.dynamic_index_in_dim(blocks, me, 0, keepdims=False)

    own_in = jnp.concatenate([own(xin_a[0]), own(xin_b[0])], axis=0)
    for n, r8, mine, row0, tr in (
            ("w_ff2", rff[0], own(xff[0]), 0, ffr // 2), ("w_ssd_proj", rsq[0], own(xsq[0]), 0, sq),
            ("w_gm_proj", rsq[0], own(xsq[0]), sq, sq), ("w_out", rsq[0], own(xsq[0]), 2 * sq, sq),
            ("w_in", rin, own_in, 0, 256)):
        res = _adamw_sum(r8, mine, W[n][0], M[n][0], V[n][0], row0, tr, "adamw_" + n)
        grads[n], delta[n], new_m[n], new_v[n] = [a[None] for a in res]
    for n, r8, mine in (("w_ff1", rff[1], own(xff[1])), ("w_ff3", rff[2], own(xff[2]))):
        res = _adamw_sum(r8, mine, _t(W[n][0]), _t(M[n][0]), _t(V[n][0]), 0, ffc // 2, "adamw_" + n)
        grads[n], delta[n], new_m[n], new_v[n] = [_t(a)[None] for a in res]

    return (loss, grad_x[None], *[grads[n] for n in WEIGHTS], *[delta[n] for n in WEIGHTS],
            *[new_m[n] for n in WEIGHTS], *[new_v[n] for n in WEIGHTS])
```

```python
import functools

import jax
import jax.numpy as jnp
from jax import lax
from jax.experimental import pallas as pl
from jax.experimental.pallas import tpu as pltpu

_MXU = jnp.bfloat16
F32 = jnp.float32
D = 1024
TL = 256
Q = 128
NH, HP, NS, HPG = 16, 64, 128, 8
DFF = 2816
ALPHA = 2.0 ** 0.25
EPS = 1e-5
OZ, OU, OV, OXS, OG, OB, OC, ODT, NPJ = 0, 1024, 2048, 3072, 4096, 6144, 6400, 6656, 6912
NNAT = 6688
NDEV = 8
ADAM_LR, ADAM_B1, ADAM_B2, ADAM_EPS, ADAM_WD, ADAM_STEP = 1e-3, 0.9, 0.999, 1e-8, 0.01, 10
VMEM_LIMIT = 48 * 1024 * 1024

NN = ((1,), (0,))
NT = ((1,), (1,))
TN = ((0,), (0,))
MESH = pl.DeviceIdType.MESH


def _dot(a, b, dims):
    return lax.dot_general(a.astype(_MXU), b.astype(_MXU), (dims, ((), ())),
                           preferred_element_type=F32)


def _tile(n, cands):
    for c in cands:
        if n % c == 0:
            return c
    return n


def _divisor_tile(n, cap, mult):
    best = n
    for t in range(mult, min(n, cap) + 1, mult):
        if n % t == 0:
            best = t
    return best


def _params(sem):
    return pltpu.CompilerParams(dimension_semantics=sem, vmem_limit_bytes=VMEM_LIMIT)


def _cst(shape):
    nd = len(shape)
    return pl.BlockSpec(shape, lambda *_: (0,) * nd)


def _rt(w, cb=0, rows=TL):
    return pl.BlockSpec((rows, w), lambda i: (i, cb))


def _rtc(w, nt, cb=0):
    return pl.BlockSpec((TL, w), lambda i: (jnp.minimum(i, nt - 1), cb))


def _sig(x):
    return jax.nn.sigmoid(x)


def _softplus(x):
    return jnp.maximum(x, 0.0) + jnp.log1p(jnp.exp(-jnp.abs(x)))


_G0, _G1 = 0.7978845608028654, 0.044715


def _gelu(x):
    t = jnp.tanh(_G0 * (x + _G1 * x * x * x))
    return 0.5 * x * (1.0 + t), t


def _gelu_grad(x, t):
    return 0.5 * (1.0 + t) + 0.5 * x * (1.0 - t * t) * _G0 * (1.0 + 3.0 * _G1 * x * x)


def _ln(r):
    mu = jnp.mean(r, axis=-1, keepdims=True)
    xc = r - mu
    var = jnp.mean(xc * xc, axis=-1, keepdims=True)
    rstd = lax.rsqrt(var + EPS)
    return xc * rstd, rstd


def _ln_bwd(dyh, xhat, rstd):
    return rstd * (dyh - jnp.mean(dyh, axis=-1, keepdims=True)
                   - xhat * jnp.mean(dyh * xhat, axis=-1, keepdims=True))


def _colsum(v):
    return jnp.sum(v, axis=0, keepdims=True)


def _sum11(v):
    return jnp.sum(jnp.sum(v, axis=1, keepdims=True), axis=0, keepdims=True)


def _cumsum_rows(a, rev):
    n = a.shape[0]
    row = lax.broadcasted_iota(jnp.int32, a.shape, 0)
    s = 1
    while s < n:
        if rev:
            a = a + jnp.where(row < n - s, pltpu.roll(a, n - s, 0), 0.0)
        else:
            a = a + jnp.where(row >= s, pltpu.roll(a, s, 0), 0.0)
        s *= 2
    return a


def _mm(a, b, mode, out_dtype, name):
    if mode == "tn":
        K, M = a.shape
    else:
        M, K = a.shape
    N = b.shape[0] if mode == "nt" else b.shape[1]
    tm = _divisor_tile(M, 1408, 128) if mode == "tn" else _divisor_tile(M, 1088, 16)
    tn = _divisor_tile(N, 1408, 128)
    tk = _divisor_tile(K, 2304, 128)
    nk = K // tk
    dims = {"nn": NN, "nt": NT, "tn": TN}[mode]
    use_acc = nk > 1 and out_dtype != F32

    def body(a_ref, b_ref, o_ref, *acc):
        prod = _dot(a_ref[...], b_ref[...], dims)
        if nk == 1:
            o_ref[...] = prod.astype(o_ref.dtype)
            return
        acc_ref = acc[0] if use_acc else o_ref
        k = pl.program_id(2)

        @pl.when(k == 0)
        def _():
            acc_ref[...] = prod

        if use_acc:
            @pl.when((k > 0) & (k < nk - 1))
            def _():
                acc_ref[...] += prod

            @pl.when(k == nk - 1)
            def _():
                o_ref[...] = (acc_ref[...] + prod).astype(o_ref.dtype)
        else:
            @pl.when(k > 0)
            def _():
                o_ref[...] += prod

    if mode == "tn":
        a_spec = pl.BlockSpec((tk, tm), lambda i, j, k: (k, i))
    else:
        a_spec = pl.BlockSpec((tm, tk), lambda i, j, k: (i, k))
    if mode == "nt":
        b_spec = pl.BlockSpec((tn, tk), lambda i, j, k: (j, k))
    else:
        b_spec = pl.BlockSpec((tk, tn), lambda i, j, k: (k, j))
    return pl.pallas_call(
        body, name=name, grid=(M // tm, N // tn, nk),
        in_specs=[a_spec, b_spec],
        out_specs=pl.BlockSpec((tm, tn), lambda i, j, k: (i, j)),
        out_shape=jax.ShapeDtypeStruct((M, N), out_dtype),
        scratch_shapes=[pltpu.VMEM((tm, tn), F32)] if use_acc else [],
        compiler_params=_params(("parallel", "parallel", "arbitrary")),
    )(a, b)


def _all_gather(x, name):
    def body(x_ref, out_ref, send_sems, recv_sems, local_sem):
        mx, my, mc = lax.axis_index("x"), lax.axis_index("y"), lax.axis_index("c")
        me, sibling = (mx, my, mc), (mx, my, 1 - mc)
        chips = [(1 - mx, my), (mx, 1 - my), (1 - mx, 1 - my)]

        def slot(px, py, pc):
            return out_ref.at[4 * px + 2 * py + pc]

        def copy(k, block, to, src=None):
            return pltpu.make_async_remote_copy(
                src_ref=slot(*block) if src is None else src, dst_ref=slot(*block),
                send_sem=send_sems.at[k], recv_sem=recv_sems.at[k],
                device_id=to, device_id_type=MESH)

        mine = pltpu.make_async_copy(x_ref, slot(*me), local_sem)
        mine.start()
        first = [copy(0, me, sibling, src=x_ref)]
        first += [copy(1 + j, me, (*chip, mc), src=x_ref) for j, chip in enumerate(chips)]
        for cp in first:
            cp.start()
        passed = [copy(4 + j, (*chip, mc), sibling) for j, chip in enumerate(chips)]
        for j, chip in enumerate(chips):
            copy(1 + j, (*chip, mc), me).wait_recv()
            passed[j].start()
        copy(0, sibling, me).wait_recv()
        for j, chip in enumerate(chips):
            copy(4 + j, (*chip, 1 - mc), me).wait_recv()
        for cp in first + passed:
            cp.wait_send()
        mine.wait()

    return pl.pallas_call(
        body, name=name,
        out_shape=jax.ShapeDtypeStruct((NDEV,) + x.shape, x.dtype),
        in_specs=[pl.BlockSpec(memory_space=pl.ANY)],
        out_specs=pl.BlockSpec(memory_space=pl.ANY),
        scratch_shapes=[pltpu.SemaphoreType.DMA((7,)), pltpu.SemaphoreType.DMA((7,)),
                        pltpu.SemaphoreType.DMA],
    )(x)


def _owner_exchange(g, name):
    def body(g_ref, out_ref, send_sems, recv_sems, local_sem):
        mx, my, mc = lax.axis_index("x"), lax.axis_index("y"), lax.axis_index("c")
        local = pltpu.make_async_copy(g_ref.at[4 * mx + 2 * my + mc], out_ref.at[0], local_sem)
        local.start()
        copies = []
        for f in range(1, NDEV):
            px = 1 - mx if (f >> 2) & 1 else mx
            py = 1 - my if (f >> 1) & 1 else my
            pc = 1 - mc if f & 1 else mc
            cp = pltpu.make_async_remote_copy(
                src_ref=g_ref.at[4 * px + 2 * py + pc], dst_ref=out_ref.at[f],
                send_sem=send_sems.at[f - 1], recv_sem=recv_sems.at[f - 1],
                device_id=(px, py, pc), device_id_type=MESH)
            cp.start()
            copies.append(cp)
        for cp in copies:
            cp.wait_recv()
        for cp in copies:
            cp.wait_send()
        local.wait()

    return pl.pallas_call(
        body, name=name,
        out_shape=jax.ShapeDtypeStruct(g.shape, g.dtype),
        in_specs=[pl.BlockSpec(memory_space=pl.ANY)],
        out_specs=pl.BlockSpec(memory_space=pl.ANY),
        scratch_shapes=[pltpu.SemaphoreType.DMA((7,)), pltpu.SemaphoreType.DMA((7,)),
                        pltpu.SemaphoreType.DMA],
    )(g)


def _any_specs(n):
    return [pl.BlockSpec(memory_space=pl.ANY)] * n


def _all_gather_multi(xs, name):
    na = len(xs)

    def body(*refs):
        x_refs, out_refs = refs[:na], refs[na:2 * na]
        send_sems, recv_sems, local_sems = refs[2 * na:]
        mx, my, mc = lax.axis_index("x"), lax.axis_index("y"), lax.axis_index("c")
        me, sibling = (mx, my, mc), (mx, my, 1 - mc)
        chips = [(1 - mx, my), (mx, 1 - my), (1 - mx, 1 - my)]

        def copy(a, k, block, to, src=None):
            slot = out_refs[a].at[4 * block[0] + 2 * block[1] + block[2]]
            return pltpu.make_async_remote_copy(
                src_ref=slot if src is None else src, dst_ref=slot,
                send_sem=send_sems.at[7 * a + k], recv_sem=recv_sems.at[7 * a + k],
                device_id=to, device_id_type=MESH)

        mine = [pltpu.make_async_copy(x_refs[a], out_refs[a].at[4 * mx + 2 * my + mc], local_sems.at[a])
                for a in range(na)]
        for cp in mine:
            cp.start()
        first = []
        for a in range(na):
            first.append(copy(a, 0, me, sibling, src=x_refs[a]))
            first += [copy(a, 1 + j, me, (*chip, mc), src=x_refs[a]) for j, chip in enumerate(chips)]
        for cp in first:
            cp.start()
        passed = []
        for a in range(na):
            for j, chip in enumerate(chips):
                copy(a, 1 + j, (*chip, mc), me).wait_recv()
                fwd = copy(a, 4 + j, (*chip, mc), sibling)
                fwd.start()
                passed.append(fwd)
        for a in range(na):
            copy(a, 0, sibling, me).wait_recv()
            for j, chip in enumerate(chips):
                copy(a, 4 + j, (*chip, 1 - mc), me).wait_recv()
        for cp in first + passed:
            cp.wait_send()
        for cp in mine:
            cp.wait()

    return pl.pallas_call(
        body, name=name,
        out_shape=[jax.ShapeDtypeStruct((NDEV,) + x.shape, x.dtype) for x in xs],
        in_specs=_any_specs(na), out_specs=_any_specs(na),
        scratch_shapes=[pltpu.SemaphoreType.DMA((7 * na,)), pltpu.SemaphoreType.DMA((7 * na,)),
                        pltpu.SemaphoreType.DMA((na,))],
    )(*xs)


def _owner_exchange_multi(gs, name):
    na = len(gs)

    def body(*refs):
        g_refs, out_refs = refs[:na], refs[na:2 * na]
        send_sems, recv_sems, local_sems = refs[2 * na:]
        mx, my, mc = lax.axis_index("x"), lax.axis_index("y"), lax.axis_index("c")
        locals_ = [pltpu.make_async_copy(g_refs[a].at[4 * mx + 2 * my + mc], out_refs[a].at[0], local_sems.at[a])
                   for a in range(na)]
        for cp in locals_:
            cp.start()
        copies = []
        for a in range(na):
            for f in range(1, NDEV):
                px = 1 - mx if (f >> 2) & 1 else mx
                py = 1 - my if (f >> 1) & 1 else my
                pc = 1 - mc if f & 1 else mc
                cp = pltpu.make_async_remote_copy(
                    src_ref=g_refs[a].at[4 * px + 2 * py + pc], dst_ref=out_refs[a].at[f],
                    send_sem=send_sems.at[7 * a + f - 1], recv_sem=recv_sems.at[7 * a + f - 1],
                    device_id=(px, py, pc), device_id_type=MESH)
                cp.start()
                copies.append(cp)
        for cp in copies:
            cp.wait_recv()
        for cp in copies:
            cp.wait_send()
        for cp in locals_:
            cp.wait()

    return pl.pallas_call(
        body, name=name,
        out_shape=[jax.ShapeDtypeStruct(g.shape, g.dtype) for g in gs],
        in_specs=_any_specs(na), out_specs=_any_specs(na),
        scratch_shapes=[pltpu.SemaphoreType.DMA((7 * na,)), pltpu.SemaphoreType.DMA((7 * na,)),
                        pltpu.SemaphoreType.DMA((na,))],
    )(*gs)


def _adamw_sum(r8, own, w, m, v, row0, tr, name):
    R, C = w.shape
    assert row0 % tr == 0
    blk0 = row0 // tr
    bc1 = 1.0 - ADAM_B1 ** ADAM_STEP
    bc2 = 1.0 - ADAM_B2 ** ADAM_STEP

    def body(r_ref, *refs):
        if own is None:
            gg = r_ref[0].astype(F32)
        else:
            gg = refs[0][...].astype(F32)
            refs = refs[1:]
        w_ref, m_ref, v_ref, g_ref, d_ref, mo_ref, vo_ref = refs
        for k in range(1, NDEV):
            gg = gg + r_ref[k].astype(F32)
        mn = ADAM_B1 * m_ref[...] + (1.0 - ADAM_B1) * gg
        vn = ADAM_B2 * v_ref[...] + (1.0 - ADAM_B2) * (gg * gg)
        mh = mn / bc1
        vh = vn / bc2
        g_ref[...] = gg
        d_ref[...] = -ADAM_LR * (mh / (jnp.sqrt(vh) + ADAM_EPS) + ADAM_WD * w_ref[...])
        mo_ref[...] = mn
        vo_ref[...] = vn

    spec = pl.BlockSpec((tr, C), lambda i: (i, 0))
    sh = jax.ShapeDtypeStruct((R, C), F32)
    own_ops = [] if own is None else [own]
    own_specs = [] if own is None else [pl.BlockSpec((tr, C), lambda i: (i + blk0, 0))]
    return pl.pallas_call(
        body, name=name, grid=(R // tr,),
        in_specs=[pl.BlockSpec((NDEV, tr, C), lambda i: (0, i + blk0, 0))] + own_specs + [spec, spec, spec],
        out_specs=[spec] * 4, out_shape=[sh] * 4, compiler_params=_params(("parallel",)),
    )(r8, *own_ops, w, m, v)


_HBM = pl.BlockSpec(memory_space=pltpu.HBM)
_SEM = pl.BlockSpec(memory_space=pltpu.SEMAPHORE)
_EFFECT = pltpu.SideEffectType.DATAFLOW_SIDE_EFFECTING


def _exchange_copies(g_refs, land_refs, send_sems, recv_sems, gather):
    mx, my, mc = lax.axis_index("x"), lax.axis_index("y"), lax.axis_index("c")
    copies = []
    for a in range(len(g_refs)):
        for f in ((1, 2, 4, 6) if gather else range(1, NDEV)):
            px = 1 - mx if (f >> 2) & 1 else mx
            py = 1 - my if (f >> 1) & 1 else my
            pc = 1 - mc if f & 1 else mc
            src = g_refs[a] if gather else g_refs[a].at[4 * px + 2 * py + pc]
            dst = land_refs[a].at[4 * mx + 2 * my + mc] if gather else land_refs[a].at[f]
            copies.append(pltpu.make_async_remote_copy(
                src_ref=src, dst_ref=dst,
                send_sem=send_sems.at[7 * a + f - 1], recv_sem=recv_sems.at[7 * a + f - 1],
                device_id=(px, py, pc), device_id_type=MESH))
    return copies


def _exchange_start(gs, name, gather=False):
    na = len(gs)

    def body(*refs):
        for cp in _exchange_copies(refs[:na], refs[na:2 * na], refs[2 * na], refs[2 * na + 1], gather):
            cp.start()
        refs[-1][...] = jnp.zeros_like(refs[-1])

    hbm = [pltpu.HBM(g.shape, g.dtype) for g in gs]
    land_shapes = [((NDEV,) + g.shape) if gather else g.shape for g in gs]
    lands = [pltpu.with_memory_space_constraint(lax.empty(shp, g.dtype), pltpu.HBM)
             for shp, g in zip(land_shapes, gs)]
    hbm_land = [pltpu.HBM(shp, g.dtype) for shp, g in zip(land_shapes, gs)]
    outs = pl.pallas_call(
        body, name=name,
        out_shape=(pltpu.SemaphoreType.DMA((7 * na,)), pltpu.SemaphoreType.DMA((7 * na,)), *hbm, *hbm_land,
                   jax.ShapeDtypeStruct((8, 128), F32)),
        in_specs=[_HBM] * (2 * na),
        out_specs=(_SEM, _SEM, *([_HBM] * (2 * na)), pl.BlockSpec(memory_space=pltpu.VMEM)),
        input_output_aliases={i: 2 + i for i in range(2 * na)},
        compiler_params=pltpu.CompilerParams(has_side_effects=_EFFECT),
    )(*[pltpu.with_memory_space_constraint(g, pltpu.HBM) for g in gs], *lands)
    return outs[0], outs[1], outs[2:2 + na], outs[2 + na:2 + 2 * na], outs[-1]


def _forward_copies(land_refs, send_sems, recv_sems):
    mx, my, mc = lax.axis_index("x"), lax.axis_index("y"), lax.axis_index("c")
    copies = []
    for a in range(len(land_refs)):
        for j, (fx, fy) in enumerate(((0, 1), (1, 0), (1, 1))):
            px = 1 - mx if fx else mx
            py = 1 - my if fy else my
            blk = land_refs[a].at[4 * px + 2 * py + mc]
            copies.append(pltpu.make_async_remote_copy(
                src_ref=blk, dst_ref=blk, send_sem=send_sems.at[3 * a + j], recv_sem=recv_sems.at[3 * a + j],
                device_id=(mx, my, 1 - mc), device_id_type=MESH))
    return copies


def _forward_start(lands, name):
    na = len(lands)

    def body(*refs):
        for cp in _forward_copies(refs[:na], refs[na], refs[na + 1]):
            cp.start()
        refs[-1][...] = jnp.zeros_like(refs[-1])

    outs = pl.pallas_call(
        body, name=name,
        out_shape=(pltpu.SemaphoreType.DMA((3 * na,)), pltpu.SemaphoreType.DMA((3 * na,)),
                   *[pltpu.HBM(g.shape, g.dtype) for g in lands], jax.ShapeDtypeStruct((8, 128), F32)),
        in_specs=[_HBM] * na,
        out_specs=(_SEM, _SEM, *([_HBM] * na), pl.BlockSpec(memory_space=pltpu.VMEM)),
        input_output_aliases={i: 2 + i for i in range(na)},
        compiler_params=pltpu.CompilerParams(has_side_effects=_EFFECT),
    )(*lands)
    return outs[0], outs[1], outs[2:2 + na], outs[-1]


def _forward_wait(send_sems, recv_sems, lands, after, name):
    na = len(lands)

    def body(*refs):
        for cp in _forward_copies(refs[:na], refs[na], refs[na + 1]):
            cp.wait_send()
            cp.wait_recv()

    return pl.pallas_call(
        body, name=name,
        out_shape=tuple(pltpu.HBM(g.shape, g.dtype) for g in lands),
        in_specs=[_HBM] * na + [_SEM, _SEM, pl.BlockSpec(memory_space=pl.ANY)],
        out_specs=tuple([_HBM] * na),
        input_output_aliases={i: i for i in range(na)},
        compiler_params=pltpu.CompilerParams(has_side_effects=_EFFECT),
    )(*lands, send_sems, recv_sems, after)


def _exchange_wait(send_sems, recv_sems, g_thru, land_thru, after, name, gather=False):
    na = len(g_thru)

    def body(*refs):
        for cp in _exchange_copies(refs[:na], refs[na:2 * na], refs[2 * na], refs[2 * na + 1], gather):
            cp.wait_send()
            cp.wait_recv()

    outs = pl.pallas_call(
        body, name=name,
        out_shape=tuple(pltpu.HBM(g.shape, g.dtype) for g in list(g_thru) + list(land_thru)),
        in_specs=[_HBM] * (2 * na) + [_SEM, _SEM, pl.BlockSpec(memory_space=pl.ANY)],
        out_specs=tuple([_HBM] * (2 * na)),
        input_output_aliases={i: i for i in range(2 * na)},
        compiler_params=pltpu.CompilerParams(has_side_effects=_EFFECT),
    )(*g_thru, *land_thru, send_sems, recv_sems, after)
    return outs[na:]


def _sum8(r, name):
    _, R, C = r.shape
    tr = _tile(R, (256, 160, 128, 64, 32, 16, 8))

    def body(r_ref, o_ref):
        acc = r_ref[0].astype(F32)
        for k in range(1, NDEV):
            acc = acc + r_ref[k].astype(F32)
        o_ref[...] = acc

    return pl.pallas_call(
        body, name=name, grid=(R // tr,),
        in_specs=[pl.BlockSpec((NDEV, tr, C), lambda i: (0, i, 0))],
        out_specs=pl.BlockSpec((tr, C), lambda i: (i, 0)),
        out_shape=jax.ShapeDtypeStruct((R, C), F32),
        compiler_params=_params(("parallel",)),
    )(r)


def _adamw(g, w, m, v, name):
    R, C = g.shape
    tr = _tile(R, (256, 160, 128, 64, 32, 16, 8))
    bc1 = 1.0 - ADAM_B1 ** ADAM_STEP
    bc2 = 1.0 - ADAM_B2 ** ADAM_STEP

    def body(g_ref, w_ref, m_ref, v_ref, d_ref, mo_ref, vo_ref):
        gg = g_ref[...]
        mn = ADAM_B1 * m_ref[...] + (1.0 - ADAM_B1) * gg
        vn = ADAM_B2 * v_ref[...] + (1.0 - ADAM_B2) * (gg * gg)
        mh = mn / bc1
        vh = vn / bc2
        d_ref[...] = -ADAM_LR * (mh / (jnp.sqrt(vh) + ADAM_EPS) + ADAM_WD * w_ref[...])
        mo_ref[...] = mn
        vo_ref[...] = vn

    spec = pl.BlockSpec((tr, C), lambda i: (i, 0))
    sh = jax.ShapeDtypeStruct((R, C), F32)
    return pl.pallas_call(
        body, name=name, grid=(R // tr,), in_specs=[spec] * 4, out_specs=[spec] * 3,
        out_shape=[sh] * 3, compiler_params=_params(("parallel",)),
    )(g, w, m, v)


def _ada_fwd(c16, w_sh, b_sh):
    def body(c_ref, w_ref, b_ref, o_ref):
        c = c_ref[...]
        o_ref[...] = _dot(c * _sig(c), w_ref[...], NN) + b_ref[...]

    return pl.pallas_call(
        body, name="ada_fwd", out_shape=jax.ShapeDtypeStruct((16, w_sh.shape[1]), F32),
        compiler_params=pltpu.CompilerParams(vmem_limit_bytes=VMEM_LIMIT),
    )(c16, w_sh, b_sh)


def _ada_bwd(c16, g16, g16_sh, w_sh):
    ncol = w_sh.shape[1]

    def body(c_ref, g_ref, gs_ref, w_ref, dw_ref, db_ref, dc_ref):
        c = c_ref[...]
        s = _sig(c)
        gs = gs_ref[...]
        dw_ref[...] = _dot(c * s, gs, TN)
        db_ref[...] = jnp.broadcast_to(_colsum(g_ref[...]), db_ref.shape)
        odd = lax.broadcasted_iota(jnp.int32, gs.shape, 0) % 2 == 1
        gc = _colsum(jnp.where(odd, gs, 0.0))
        ds = _dot(jnp.broadcast_to(gc, (8, ncol)), w_ref[...], NT)
        c1 = c[1:2, :]
        s1 = s[1:2, :]
        dc_ref[...] = ds * (s1 * (1.0 + c1 * (1.0 - s1)))

    return pl.pallas_call(
        body, name="ada_bwd",
        out_shape=[jax.ShapeDtypeStruct(w_sh.shape, F32),
                   jax.ShapeDtypeStruct((8, g16.shape[1]), F32),
                   jax.ShapeDtypeStruct((8, D), F32)],
        compiler_params=pltpu.CompilerParams(vmem_limit_bytes=VMEM_LIMIT),
    )(c16, g16, g16_sh, w_sh)


def _ln0_fwd(x, ctx, g, b, modx, modc):
    L = x.shape[0]
    nt = L // TL

    def body(x_ref, c_ref, g_ref, b_ref, mx_ref, mc_ref, xn_ref, h_ref):
        isc = pl.program_id(0) == nt
        xin = jnp.where(isc, c_ref[...], x_ref[...])
        sh = jnp.where(isc, mc_ref[0:1, :], mx_ref[0:1, :])
        sc = jnp.where(isc, mc_ref[1:2, :], mx_ref[1:2, :])
        xhat, _ = _ln(xin)
        xn = xhat * g_ref[...] + b_ref[...]
        xn_ref[...] = xn
        h_ref[...] = (xn * (1.0 + sc) + sh).astype(h_ref.dtype)

    return pl.pallas_call(
        body, name="ln0_fwd", grid=(nt + 1,),
        in_specs=[_rtc(D, nt), _cst((TL, D)), _cst((1, D)), _cst((1, D)), _cst((8, D)), _cst((8, D))],
        out_specs=[_rt(D), _rt(D)],
        out_shape=[jax.ShapeDtypeStruct((L + TL, D), F32), jax.ShapeDtypeStruct((L + TL, D), _MXU)],
        compiler_params=_params(("parallel",)),
    )(x, ctx, g, b, modx, modc)


def _xbc_colblk(j):
    return jnp.where(j < 8, OXS // 128 + j, OB // 128 + j - 8)


def _conv_taps(p_ref, r0, first, last):
    main = p_ref[pl.ds(r0, TL), :]
    zero = jnp.zeros((8, main.shape[1]), F32)
    prev = zero if first else p_ref[pl.ds(r0 - 8, 8), :]
    nxt = zero if last else p_ref[pl.ds(r0 + TL, 8), :]
    ext = jnp.concatenate([prev, main, nxt], axis=0)
    n = TL + 16
    return [pltpu.roll(ext, (2 - k) % n, 0)[8:8 + TL] for k in range(5)]


def _seq_chunks(L):
    nt = L // TL
    return [(r * TL, r == 0, r == nt - 1) for r in range(nt)] + [(L, True, True)]


def _conv_fwd(p, conv_w8, conv_b):
    RT = p.shape[0]
    L = RT - TL
    chunks = _seq_chunks(L)

    def body(p_ref, w_ref, b_ref, o_ref):
        w = w_ref[...]
        bias = b_ref[...]
        for r0, first, last in chunks:
            taps = _conv_taps(p_ref, r0, first, last)
            pre = bias + sum(w[k:k + 1, :] * taps[k] for k in range(5))
            o_ref[pl.ds(r0, TL), :] = pre * _sig(pre)

    return pl.pallas_call(
        body, name="conv_fwd", grid=(12,),
        in_specs=[pl.BlockSpec((RT, 128), lambda j: (0, _xbc_colblk(j))),
                  pl.BlockSpec((8, 128), lambda j: (0, j)),
                  pl.BlockSpec((1, 128), lambda j: (0, j))],
        out_specs=pl.BlockSpec((RT, 128), lambda j: (0, j)),
        out_shape=jax.ShapeDtypeStruct((RT, 1536), F32),
        compiler_params=_params(("parallel",)),
    )(p, conv_w8, conv_b)


def _ssd_common(dtraw, dtb, a32, rev):
    dt = _softplus(dtraw + dtb)
    acum = _cumsum_rows(dt * a32, rev)
    ii = lax.broadcasted_iota(jnp.int32, (Q, Q), 0)
    jj = lax.broadcasted_iota(jnp.int32, (Q, Q), 1)
    mask = (ii <= jj) if rev else (ii >= jj)
    return dt, acum, acum.T, dt.T, mask


def _ssd_orders(ncl, ncc):
    nc = ncl + ncc

    def cf(s):
        return jnp.where(s < ncc, ncl + s, s - ncc)

    def cb(s):
        return nc - 1 - s

    return cf, cb


def _ssd_fwd(xbc, p, prm):
    RT = xbc.shape[0]
    nc = RT // Q
    ncc = TL // Q
    cf, cb = _ssd_orders(nc - ncc, ncc)

    def one_dir(x_ref, dt_ref, prm_ref, y_ref, hp_ref, H_ref, d):
        rev = d == 1
        a32 = -jnp.exp(prm_ref[1:2, :])
        dt, acum, acumT, dtT, mask = _ssd_common(dt_ref[...], prm_ref[0:1, :], a32, rev)
        end = 0 if rev else Q - 1
        for g in range(2):
            Bg = x_ref[:, D + g * NS:D + (g + 1) * NS]
            Cg = x_ref[:, D + 2 * NS + g * NS:D + 2 * NS + (g + 1) * NS]
            CB = _dot(Cg, Bg, NT)
            for hh in range(HPG):
                h = g * HPG + hh
                ln = 16 * d + h
                col = acum[:, ln:ln + 1]
                rowv = acumT[ln:ln + 1, :]
                a_end = rowv[:, end:end + 1]
                Lm = jnp.exp(jnp.where(mask, col - rowv, -1e30))
                W = CB * Lm * dtT[ln:ln + 1, :]
                Xh = x_ref[:, h * HP:(h + 1) * HP]
                Hp = H_ref[h * HP:(h + 1) * HP, :]
                y = _dot(W, Xh, NN) + jnp.exp(col) * _dot(Cg, Hp, NT)
                y_ref[:, h * HP:(h + 1) * HP] = y
                dcol = jnp.exp(a_end - col) * dt[:, ln:ln + 1]
                hp_ref[0, h * HP:(h + 1) * HP, :] = Hp
                H_ref[h * HP:(h + 1) * HP, :] = jnp.exp(a_end) * Hp + _dot(Xh * dcol, Bg, TN)

    def body(xf_ref, xb_ref, df_ref, db_ref, prm_ref, yf_ref, yb_ref, hf_ref, hb_ref, Hf, Hb):
        @pl.when(pl.program_id(0) == 0)
        def _():
            Hf[...] = jnp.zeros_like(Hf)
            Hb[...] = jnp.zeros_like(Hb)

        one_dir(xf_ref, df_ref, prm_ref, yf_ref, hf_ref, Hf, 0)
        one_dir(xb_ref, db_ref, prm_ref, yb_ref, hb_ref, Hb, 1)

    ysh = jax.ShapeDtypeStruct((RT, D), F32)
    hsh = jax.ShapeDtypeStruct((nc, NH * HP, NS), F32)
    hspec = pl.BlockSpec((1, NH * HP, NS), lambda s: (s, 0, 0))
    return pl.pallas_call(
        body, name="ssd_fwd", grid=(nc,),
        in_specs=[pl.BlockSpec((Q, 1536), lambda s: (cf(s), 0)),
                  pl.BlockSpec((Q, 1536), lambda s: (cb(s), 0)),
                  pl.BlockSpec((Q, 128), lambda s: (cf(s), ODT // 128)),
                  pl.BlockSpec((Q, 128), lambda s: (cb(s), ODT // 128)),
                  _cst((8, 128))],
        out_specs=[pl.BlockSpec((Q, D), lambda s: (cf(s), 0)),
                   pl.BlockSpec((Q, D), lambda s: (cb(s), 0)), hspec, hspec],
        out_shape=[ysh, ysh, hsh, hsh],
        scratch_shapes=[pltpu.VMEM((NH * HP, NS), F32), pltpu.VMEM((NH * HP, NS), F32)],
        compiler_params=_params(("arbitrary",)),
    )(xbc, xbc, p, p, prm)


def _ssd_bwd(xbc, p, prm, dsk, dyd, hpf, hpb):
    RT = xbc.shape[0]
    nc = RT // Q
    ncc = TL // Q
    ncl = nc - ncc
    cf, cb = _ssd_orders(ncl, ncc)

    def rs(t):
        return nc - 1 - t

    def one_dir(x_ref, dt_ref, prm_ref, dsk_ref, dy_ref, is_ctx, hp_ref, dH_ref,
                dx_ref, ddt_ref, st_ref, d):
        rev = d == 1
        a32 = -jnp.exp(prm_ref[1:2, :])
        dtraw = dt_ref[...]
        dtb = prm_ref[0:1, :]
        dt, acum, acumT, dtT, mask = _ssd_common(dtraw, dtb, a32, rev)
        end = 0 if rev else Q - 1
        lane = lax.broadcasted_iota(jnp.int32, (Q, 128), 1)
        srow = lax.broadcasted_iota(jnp.int32, (Q, 128), 0)
        dyscale = jnp.where(is_ctx, 0.0, 1.0)
        c_dacum = jnp.zeros((Q, 128), F32)
        r_dacum = jnp.zeros((Q, 128), F32)
        c_ddt = jnp.zeros((Q, 128), F32)
        r_ddt = jnp.zeros((Q, 128), F32)
        dskacc = jnp.zeros((1, 128), F32)
        for g in range(2):
            Bg = x_ref[:, D + g * NS:D + (g + 1) * NS]
            Cg = x_ref[:, D + 2 * NS + g * NS:D + 2 * NS + (g + 1) * NS]
            CB = _dot(Cg, Bg, NT)
            dCB = jnp.zeros((Q, Q), F32)
            dBg = jnp.zeros((Q, NS), F32)
            dCg = jnp.zeros((Q, NS), F32)
            for hh in range(HPG):
                h = g * HPG + hh
                ln = 16 * d + h
                hs = slice(h * HP, (h + 1) * HP)
                col = acum[:, ln:ln + 1]
                rowv = acumT[ln:ln + 1, :]
                dtr = dtT[ln:ln + 1, :]
                dtc = dt[:, ln:ln + 1]
                a_end = rowv[:, end:end + 1]
                Lm = jnp.exp(jnp.where(mask, col - rowv, -1e30))
                E = jnp.exp(col)
                ecol = jnp.exp(a_end - col)
                dcol = ecol * dtc
                Xh = x_ref[:, hs]
                dY = dy_ref[:, hs] * dyscale
                Hp = hp_ref[0, hs, :]
                dHn = dH_ref[hs, :]
                W = CB * Lm * dtr
                dW = _dot(dY, Xh, NT)
                Mm = dW * CB * Lm
                T = Mm * dtr
                dCB = dCB + dW * Lm * dtr
                BdH = _dot(Bg, dHn, NT)
                dX = _dot(W, dY, TN) + dcol * BdH
                if d == 0:
                    dX = dX + dY * dsk_ref[:, hs]
                    dskacc = dskacc + jnp.where(lane[0:1, :] == h, _sum11(dY * Xh), 0.0)
                dx_ref[:, hs] = dX
                xb = jnp.sum(Xh * BdH, axis=1, keepdims=True)
                scol = dcol * xb
                G = _dot(dY, Hp, NN)
                dCg = dCg + E * G
                qcol = E * jnp.sum(G * Cg, axis=1, keepdims=True)
                dBg = dBg + _dot(Xh * dcol, dHn, NN)
                dH_ref[hs, :] = jnp.exp(a_end) * dHn + _dot(dY * E, Cg, TN)
                eterm = jnp.exp(a_end) * _sum11(dHn * Hp) + _sum11(scol)
                cvec = jnp.sum(T, axis=1, keepdims=True) + qcol - scol
                cvec = cvec + jnp.where(srow[:, 0:1] == end, eterm, 0.0)
                c_dacum = c_dacum + jnp.where(lane == ln, cvec, 0.0)
                r_dacum = r_dacum - jnp.where(srow == ln, _colsum(T), 0.0)
                c_ddt = c_ddt + jnp.where(lane == ln, ecol * xb, 0.0)
                r_ddt = r_ddt + jnp.where(srow == ln, _colsum(Mm), 0.0)
            dBg = dBg + _dot(dCB, Cg, TN)
            dCg = dCg + _dot(dCB, Bg, NN)
            dx_ref[:, D + g * NS:D + (g + 1) * NS] = dBg
            dx_ref[:, D + 2 * NS + g * NS:D + 2 * NS + (g + 1) * NS] = dCg
        dacum = c_dacum + r_dacum.T
        da = _cumsum_rows(dacum, not rev)
        mine = (lane >= 16 * d) & (lane < 16 * d + 16)
        ddt = jnp.where(mine, c_ddt + r_ddt.T + da * a32, 0.0)
        ddt_ref[...] = ddt * _sig(dtraw + dtb)
        st_ref[0:1, :] += _colsum(jnp.where(mine, da * dt, 0.0))
        if d == 0:
            st_ref[1:2, :] += dskacc

    def body(xf_ref, xb_ref, df_ref, db_ref, prm_ref, dsk_ref, dyf_ref, dyb_ref, hf_ref, hb_ref,
             dxf_ref, dxb_ref, ddf_ref, ddb_ref, st_ref, dHf, dHb):
        t = pl.program_id(0)

        @pl.when(t == 0)
        def _():
            dHf[...] = jnp.zeros_like(dHf)
            dHb[...] = jnp.zeros_like(dHb)
            st_ref[...] = jnp.zeros_like(st_ref)

        s = rs(t)
        one_dir(xf_ref, df_ref, prm_ref, dsk_ref, dyf_ref, cf(s) >= ncl, hf_ref, dHf,
                dxf_ref, ddf_ref, st_ref, 0)
        one_dir(xb_ref, db_ref, prm_ref, dsk_ref, dyb_ref, cb(s) >= ncl, hb_ref, dHb,
                dxb_ref, ddb_ref, st_ref, 1)

        @pl.when(t == nc - 1)
        def _():
            st_ref[0:1, :] = -jnp.exp(prm_ref[1:2, :]) * st_ref[0:1, :]

    def lat(c):
        return jnp.minimum(c, ncl - 1)

    xsh = jax.ShapeDtypeStruct((RT, 1536), F32)
    dsh = jax.ShapeDtypeStruct((RT, 128), F32)
    hspec = pl.BlockSpec((1, NH * HP, NS), lambda t: (rs(t), 0, 0))
    return pl.pallas_call(
        body, name="ssd_bwd", grid=(nc,),
        in_specs=[pl.BlockSpec((Q, 1536), lambda t: (cf(rs(t)), 0)),
                  pl.BlockSpec((Q, 1536), lambda t: (cb(rs(t)), 0)),
                  pl.BlockSpec((Q, 128), lambda t: (cf(rs(t)), ODT // 128)),
                  pl.BlockSpec((Q, 128), lambda t: (cb(rs(t)), ODT // 128)),
                  _cst((8, 128)), _cst((1, D)),
                  pl.BlockSpec((Q, D), lambda t: (lat(cf(rs(t))), 0)),
                  pl.BlockSpec((Q, D), lambda t: (lat(cb(rs(t))), 0)),
                  hspec, hspec],
        out_specs=[pl.BlockSpec((Q, 1536), lambda t: (cf(rs(t)), 0)),
                   pl.BlockSpec((Q, 1536), lambda t: (cb(rs(t)), 0)),
                   pl.BlockSpec((Q, 128), lambda t: (cf(rs(t)), 0)),
                   pl.BlockSpec((Q, 128), lambda t: (cb(rs(t)), 0)),
                   _cst((8, 128))],
        out_shape=[xsh, xsh, dsh, dsh, jax.ShapeDtypeStruct((8, 128), F32)],
        scratch_shapes=[pltpu.VMEM((NH * HP, NS), F32), pltpu.VMEM((NH * HP, NS), F32)],
        compiler_params=_params(("arbitrary",)),
    )(xbc, xbc, p, p, prm, dsk, dyd, dyd, hpf, hpb)


def _lane_bcast(v, ln):
    return jnp.broadcast_to(v[:, ln:ln + 1], v.shape)


def _halves(v, lo, axis):
    return jnp.concatenate([jnp.where(lo, v, 0.0), jnp.where(lo, 0.0, v)], axis=axis)


def _ssd2_fwd(xbc, p, prm):
    RT = xbc.shape[0]
    nc = RT // Q
    ncc = TL // Q
    cf, cb = _ssd_orders(nc - ncc, ncc)

    def one_dir(x_ref, dt_ref, prm_ref, y_ref, hp_ref, HT_ref, d):
        rev = d == 1
        a32 = -jnp.exp(prm_ref[1:2, :])
        dt, acum, acumT, dtT, mask = _ssd_common(dt_ref[...], prm_ref[0:1, :], a32, rev)
        end = 0 if rev else Q - 1
        lo = lax.broadcasted_iota(jnp.int32, (Q, 128), 1) < HP
        for g in range(2):
            Bg = x_ref[:, D + g * NS:D + (g + 1) * NS]
            Cg = x_ref[:, D + 2 * NS + g * NS:D + 2 * NS + (g + 1) * NS]
            CB = _dot(Cg, Bg, NT)
            xds, svs = [], []
            for q in range(HPG // 2):
                pi = g * (HPG // 2) + q
                ps = slice(pi * 128, (pi + 1) * 128)
                Xp = x_ref[:, ps]
                HTp = HT_ref[:, ps]
                lhs, dcs, sv = [], [], []
                ces = []
                for h in (2 * pi, 2 * pi + 1):
                    ln = 16 * d + h
                    colB = _lane_bcast(acum, ln)
                    rowv = acumT[ln:ln + 1, :]
                    aend = colB[end:end + 1, :]
                    Lm = jnp.exp(jnp.where(mask, colB - rowv, -1e30))
                    lhs.append(CB * Lm * dtT[ln:ln + 1, :])
                    ces.append(Cg * jnp.exp(colB))
                    dcs.append(jnp.exp(aend - colB) * _lane_bcast(dt, ln))
                    sv.append(jnp.exp(aend))
                lhs = jnp.concatenate(lhs + ces, axis=1)
                rhs = jnp.concatenate([_halves(Xp, lo, 0), _halves(HTp, lo, 0)], axis=0)
                y_ref[:, ps] = _dot(lhs, rhs, NN)
                xds.append(Xp * jnp.where(lo, dcs[0], dcs[1]))
                svs.append(jnp.where(lo[0:1, :], sv[0], sv[1]))
            gs = slice(g * 512, (g + 1) * 512)
            HTg = HT_ref[:, gs]
            hp_ref[0, :, gs] = HTg
            st = _dot(Bg.T, jnp.concatenate(xds, axis=1), NN)
            HT_ref[:, gs] = jnp.concatenate(svs, axis=1) * HTg + st

    def body(xf_ref, xb_ref, df_ref, db_ref, prm_ref, yf_ref, yb_ref, hf_ref, hb_ref, Hf, Hb):
        @pl.when(pl.program_id(0) == 0)
        def _():
            Hf[...] = jnp.zeros_like(Hf)
            Hb[...] = jnp.zeros_like(Hb)

        one_dir(xf_ref, df_ref, prm_ref, yf_ref, hf_ref, Hf, 0)
        one_dir(xb_ref, db_ref, prm_ref, yb_ref, hb_ref, Hb, 1)

    ysh = jax.ShapeDtypeStruct((RT, D), F32)
    hsh = jax.ShapeDtypeStruct((nc, NS, NH * HP), F32)
    hspec = pl.BlockSpec((1, NS, NH * HP), lambda s: (s, 0, 0))
    return pl.pallas_call(
        body, name="ssd_fwd", grid=(nc,),
        in_specs=[pl.BlockSpec((Q, 1536), lambda s: (cf(s), 0)),
                  pl.BlockSpec((Q, 1536), lambda s: (cb(s), 0)),
                  pl.BlockSpec((Q, 128), lambda s: (cf(s), ODT // 128)),
                  pl.BlockSpec((Q, 128), lambda s: (cb(s), ODT // 128)),
                  _cst((8, 128))],
        out_specs=[pl.BlockSpec((Q, D), lambda s: (cf(s), 0)),
                   pl.BlockSpec((Q, D), lambda s: (cb(s), 0)), hspec, hspec],
        out_shape=[ysh, ysh, hsh, hsh],
        scratch_shapes=[pltpu.VMEM((NS, NH * HP), F32), pltpu.VMEM((NS, NH * HP), F32)],
        compiler_params=_params(("arbitrary",)),
    )(xbc, xbc, p, p, prm)


def _ssd2_bwd(xbc, p, prm, dsk, dyd, hpf, hpb):
    RT = xbc.shape[0]
    nc = RT // Q
    ncc = TL // Q
    ncl = nc - ncc
    cf, cb = _ssd_orders(ncl, ncc)

    def rs(t):
        return nc - 1 - t

    def one_dir(x_ref, dt_ref, prm_ref, dsk_ref, dy_ref, is_ctx, hp_ref, dHT_ref,
                dx_ref, ddt_ref, st_ref, d):
        rev = d == 1
        a32 = -jnp.exp(prm_ref[1:2, :])
        dtraw = dt_ref[...]
        dtb = prm_ref[0:1, :]
        dt, acum, acumT, _, _ = _ssd_common(dtraw, dtb, a32, rev)
        end = 0 if rev else Q - 1
        lane = lax.broadcasted_iota(jnp.int32, (Q, 128), 1)
        srow = lax.broadcasted_iota(jnp.int32, (Q, 128), 0)
        maskT = (lane <= srow) if rev else (lane >= srow)
        lo = lane < HP
        lo1 = lo[0:1, :]
        dyscale = jnp.where(is_ctx, 0.0, 1.0)
        c_dacum = jnp.zeros((Q, 128), F32)
        r_dacum = jnp.zeros((Q, 128), F32)
        c_ddt = jnp.zeros((Q, 128), F32)
        dskacc = jnp.zeros((1, 128), F32)
        for g in range(2):
            gs = slice(g * 512, (g + 1) * 512)
            Bg = x_ref[:, D + g * NS:D + (g + 1) * NS]
            Cg = x_ref[:, D + 2 * NS + g * NS:D + 2 * NS + (g + 1) * NS]
            CBT = _dot(Bg, Cg, NT)
            HTg = hp_ref[0, :, gs]
            dHTg = dHT_ref[:, gs]
            BdHg = _dot(Bg, dHTg, NN)
            dCBT = jnp.zeros((Q, Q), F32)
            dCg = jnp.zeros((Q, NS), F32)
            xds, dyes, svs = [], [], []
            for q in range(HPG // 2):
                pi = g * (HPG // 2) + q
                ps = slice(pi * 128, (pi + 1) * 128)
                qs = slice(q * 128, (q + 1) * 128)
                Xp = x_ref[:, ps]
                dYp = dy_ref[:, ps] * dyscale
                HTp = HTg[:, qs]
                BdHp = BdHg[:, qs]
                dY2 = _halves(dYp, lo, 0)
                dWT2 = _dot(_halves(Xp, lo, 0), dYp.T, NN)
                G2 = _dot(dY2, HTp, NT)
                XB = Xp * BdHp
                hh = _colsum(dHTg[:, qs] * HTp)
                yx = _colsum(dYp * Xp)
                wts, dcs, ebs, sv = [], [], [], []
                for k, h in enumerate((2 * pi, 2 * pi + 1)):
                    ln = 16 * d + h
                    half = lo if k == 0 else jnp.logical_not(lo)
                    half1 = half[0:1, :]
                    colB = _lane_bcast(acum, ln)
                    dtcB = _lane_bcast(dt, ln)
                    rowv = acumT[ln:ln + 1, :]
                    aend = colB[end:end + 1, :]
                    LmT = jnp.exp(jnp.where(maskT, rowv - colB, -1e30))
                    WT = CBT * LmT * dtcB
                    dWT = dWT2[k * Q:(k + 1) * Q, :]
                    U = dWT * LmT
                    MT = U * CBT
                    rM = jnp.sum(MT, axis=1, keepdims=True)
                    rT = _colsum(MT * dtcB)
                    dCBT = dCBT + U * dtcB
                    ecol = jnp.exp(aend - colB)
                    EB = jnp.exp(colB)
                    Gk = G2[k * Q:(k + 1) * Q, :]
                    dCg = dCg + EB * Gk
                    qcol = jnp.sum(EB * Gk * Cg, axis=1, keepdims=True)
                    xb = jnp.sum(jnp.where(half, XB, 0.0), axis=1, keepdims=True)
                    e1 = ecol[:, 0:1]
                    dt1 = dtcB[:, 0:1]
                    scol = e1 * dt1 * xb
                    sA = jnp.exp(aend)
                    eterm = sA[:, 0:1] * jnp.sum(jnp.where(half1, hh, 0.0), axis=1, keepdims=True) \
                        + _colsum(scol)
                    cvec = qcol - dt1 * rM - scol + jnp.where(srow[:, 0:1] == end, eterm, 0.0)
                    c_dacum = jnp.where(lane == ln, cvec, c_dacum)
                    r_dacum = jnp.where(srow == ln, rT, r_dacum)
                    c_ddt = jnp.where(lane == ln, rM + e1 * xb, c_ddt)
                    if d == 0:
                        dskacc = dskacc + jnp.where(
                            lane[0:1, :] == h, jnp.sum(jnp.where(half1, yx, 0.0), axis=1, keepdims=True), 0.0)
                    wts.append(WT)
                    dcs.append(ecol * dtcB)
                    ebs.append(EB)
                    sv.append(sA)
                dcp = jnp.where(lo, dcs[0], dcs[1])
                dX = _dot(jnp.concatenate(wts, axis=1), dY2, NN) + dcp * BdHp
                if d == 0:
                    dX = dX + dYp * dsk_ref[:, ps]
                dx_ref[:, ps] = dX
                xds.append(Xp * dcp)
                dyes.append(dYp * jnp.where(lo, ebs[0], ebs[1]))
                svs.append(jnp.where(lo1, sv[0], sv[1]))
            dx_ref[:, D + g * NS:D + (g + 1) * NS] = (
                _dot(jnp.concatenate(xds, axis=1), dHTg, NT) + _dot(dCBT, Cg, NN))
            dx_ref[:, D + 2 * NS + g * NS:D + 2 * NS + (g + 1) * NS] = dCg + _dot(dCBT, Bg, TN)
            dHT_ref[:, gs] = (jnp.concatenate(svs, axis=1) * dHTg
                              + _dot(Cg.T, jnp.concatenate(dyes, axis=1), NN))
        dacum = c_dacum + r_dacum.T
        da = _cumsum_rows(dacum, not rev)
        mine = (lane >= 16 * d) & (lane < 16 * d + 16)
        ddt = jnp.where(mine, c_ddt + da * a32, 0.0)
        ddt_ref[...] = ddt * _sig(dtraw + dtb)
        st_ref[0:1, :] += _colsum(jnp.where(mine, da * dt, 0.0))
        if d == 0:
            st_ref[1:2, :] += dskacc

    def body(xf_ref, xb_ref, df_ref, db_ref, prm_ref, dsk_ref, dyf_ref, dyb_ref, hf_ref, hb_ref,
             dxf_ref, dxb_ref, ddf_ref, ddb_ref, st_ref, dHf, dHb):
        t = pl.program_id(0)

        @pl.when(t == 0)
        def _():
            dHf[...] = jnp.zeros_like(dHf)
            dHb[...] = jnp.zeros_like(dHb)
            st_ref[...] = jnp.zeros_like(st_ref)

        s = rs(t)
        one_dir(xf_ref, df_ref, prm_ref, dsk_ref, dyf_ref, cf(s) >= ncl, hf_ref, dHf,
                dxf_ref, ddf_ref, st_ref, 0)
        one_dir(xb_ref, db_ref, prm_ref, dsk_ref, dyb_ref, cb(s) >= ncl, hb_ref, dHb,
                dxb_ref, ddb_ref, st_ref, 1)

        @pl.when(t == nc - 1)
        def _():
            st_ref[0:1, :] = -jnp.exp(prm_ref[1:2, :]) * st_ref[0:1, :]

    def lat(c):
        return jnp.minimum(c, ncl - 1)

    xsh = jax.ShapeDtypeStruct((RT, 1536), F32)
    dsh = jax.ShapeDtypeStruct((RT, 128), F32)
    hspec = pl.BlockSpec((1, NS, NH * HP), lambda t: (rs(t), 0, 0))
    return pl.pallas_call(
        body, name="ssd_bwd", grid=(nc,),
        in_specs=[pl.BlockSpec((Q, 1536), lambda t: (cf(rs(t)), 0)),
                  pl.BlockSpec((Q, 1536), lambda t: (cb(rs(t)), 0)),
                  pl.BlockSpec((Q, 128), lambda t: (cf(rs(t)), ODT // 128)),
                  pl.BlockSpec((Q, 128), lambda t: (cb(rs(t)), ODT // 128)),
                  _cst((8, 128)), _cst((1, D)),
                  pl.BlockSpec((Q, D), lambda t: (lat(cf(rs(t))), 0)),
                  pl.BlockSpec((Q, D), lambda t: (lat(cb(rs(t))), 0)),
                  hspec, hspec],
        out_specs=[pl.BlockSpec((Q, 1536), lambda t: (cf(rs(t)), 0)),
                   pl.BlockSpec((Q, 1536), lambda t: (cb(rs(t)), 0)),
                   pl.BlockSpec((Q, 128), lambda t: (cf(rs(t)), 0)),
                   pl.BlockSpec((Q, 128), lambda t: (cb(rs(t)), 0)),
                   _cst((8, 128))],
        out_shape=[xsh, xsh, dsh, dsh, jax.ShapeDtypeStruct((8, 128), F32)],
        scratch_shapes=[pltpu.VMEM((NS, NH * HP), F32), pltpu.VMEM((NS, NH * HP), F32)],
        compiler_params=_params(("arbitrary",)),
    )(xbc, xbc, p, p, prm, dsk, dyd, dyd, hpf, hpb)


RB = 32


def _ssd3_bwd(xbc, p, prm, dsk, dyd, hpf, hpb):
    RT = xbc.shape[0]
    nc = RT // Q
    ncc = TL // Q
    ncl = nc - ncc
    cf, cb = _ssd_orders(ncl, ncc)
    npair = HPG // 2

    def rs(t):
        return nc - 1 - t

    def one_dir(x_ref, dt_ref, prm_ref, dsk_ref, dy_ref, is_ctx, hp_ref, dHT_ref,
                dx_ref, ddt_ref, st_ref, s_dwt, s_g, s_wt, s_xd, s_dye, s_dcbt, s_dcg, s_cd, s_cdt, d):
        rev = d == 1
        a32 = -jnp.exp(prm_ref[1:2, :])
        dtraw = dt_ref[...]
        dtb = prm_ref[0:1, :]
        dt, acum, acumT, _, _ = _ssd_common(dtraw, dtb, a32, rev)
        end = 0 if rev else Q - 1
        lane = lax.broadcasted_iota(jnp.int32, (RB, 128), 1)
        srow0 = lax.broadcasted_iota(jnp.int32, (RB, 128), 0)
        lo = lane < HP
        lo1 = lo[0:1, :]
        lane1 = lane[0:1, :]
        dyscale = jnp.where(is_ctx, 0.0, 1.0)
        aend_row = acum[end:end + 1, :]
        s_cd[...] = jnp.zeros_like(s_cd)
        s_cdt[...] = jnp.zeros_like(s_cdt)
        r_rows = jnp.zeros((Q, 128), F32)
        srowQ = lax.broadcasted_iota(jnp.int32, (Q, 128), 0)
        dskacc = jnp.zeros((1, 128), F32)
        for g in range(2):
            gs = slice(g * 512, (g + 1) * 512)
            Bg = x_ref[:, D + g * NS:D + (g + 1) * NS]
            Cg = x_ref[:, D + 2 * NS + g * NS:D + 2 * NS + (g + 1) * NS]
            CBT = _dot(Bg, Cg, NT)
            HTg = hp_ref[0, :, gs]
            dHTg = dHT_ref[:, gs]
            BdHg = _dot(Bg, dHTg, NN)
            hhs, yxs = [], []
            for q in range(npair):
                pi = g * npair + q
                ps = slice(pi * 128, (pi + 1) * 128)
                qs = slice(q * 128, (q + 1) * 128)
                Xp = x_ref[:, ps]
                dYp = dy_ref[:, ps] * dyscale
                s_dwt[q] = _dot(_halves(Xp, lo_full(), 0), dYp.T, NN)
                s_g[q] = _dot(_halves(dYp, lo_full(), 0), HTg[:, qs], NT)
                hhs.append(_colsum(dHTg[:, qs] * HTg[:, qs]))
                yxs.append(_colsum(dYp * Xp))
            rparts = [jnp.zeros((8, 128), F32) for _ in range(HPG)]
            ssum = [jnp.zeros((1, 1), F32) for _ in range(HPG)]
            for rb in range(Q // RB):
                r0 = rb * RB
                rows = slice(r0, r0 + RB)
                srow = srow0 + r0
                maskT = (lane <= srow) if rev else (lane >= srow)
                acum_rb = acum[rows, :]
                dt_rb = dt[rows, :]
                CBT_rb = CBT[rows, :]
                Cg_rb = Cg[rows, :]
                dcbt = jnp.zeros((RB, Q), F32)
                dcg = jnp.zeros((RB, NS), F32)
                cd = s_cd[rows, :]
                cdt = s_cdt[rows, :]
                for q in range(npair):
                    pi = g * npair + q
                    ps = slice(pi * 128, (pi + 1) * 128)
                    qs = slice(q * 128, (q + 1) * 128)
                    Xp = x_ref[rows, ps]
                    dYp = dy_ref[rows, ps] * dyscale
                    BdHp = BdHg[rows, qs]
                    XB = Xp * BdHp
                    dcs, ebs = [], []
                    for k in range(2):
                        hh = 2 * q + k
                        ln = 16 * d + g * HPG + hh
                        half = lo if k == 0 else jnp.logical_not(lo)
                        colB = _lane_bcast(acum_rb, ln)
                        dtcB = _lane_bcast(dt_rb, ln)
                        rowv = acumT[ln:ln + 1, :]
                        aend = _lane_bcast(aend_row, ln)
                        LmT = jnp.exp(jnp.where(maskT, rowv - colB, -1e30))
                        s_wt[q, rows, k * Q:(k + 1) * Q] = (CBT_rb * LmT * dtcB).astype(s_wt.dtype)
                        U = s_dwt[q, k * Q + r0:k * Q + r0 + RB, :] * LmT
                        MT = U * CBT_rb
                        rM = jnp.sum(MT, axis=1, keepdims=True)
                        TT = MT * dtcB
                        rparts[hh] = rparts[hh] + (TT[0:8] + TT[8:16] + TT[16:24] + TT[24:32])
                        dcbt = dcbt + U * dtcB
                        ecol = jnp.exp(aend - colB)
                        EB = jnp.exp(colB)
                        EG = EB * s_g[q, k * Q + r0:k * Q + r0 + RB, :]
                        dcg = dcg + EG
                        qcol = jnp.sum(EG * Cg_rb, axis=1, keepdims=True)
                        xb = jnp.sum(jnp.where(half, XB, 0.0), axis=1, keepdims=True)
                        e1 = ecol[:, 0:1]
                        dt1 = dtcB[:, 0:1]
                        scol = e1 * dt1 * xb
                        ssum[hh] = ssum[hh] + _colsum(scol)
                        cd = jnp.where(lane == ln, qcol - dt1 * rM - scol, cd)
                        cdt = jnp.where(lane == ln, rM + e1 * xb, cdt)
                        dcs.append(ecol * dtcB)
                        ebs.append(EB)
                    dcp = jnp.where(lo, dcs[0], dcs[1])
                    dxo = dcp * BdHp
                    if d == 0:
                        dxo = dxo + dYp * dsk_ref[:, ps]
                    dx_ref[rows, ps] = dxo
                    s_xd[rows, qs] = (Xp * dcp).astype(s_xd.dtype)
                    s_dye[rows, qs] = (dYp * jnp.where(lo, ebs[0], ebs[1])).astype(s_dye.dtype)
                s_dcbt[rows, :] = dcbt
                s_dcg[rows, :] = dcg
                s_cd[rows, :] = cd
                s_cdt[rows, :] = cdt
            erow = jnp.zeros((1, 128), F32)
            svs = []
            for q in range(npair):
                pi = g * npair + q
                ps = slice(pi * 128, (pi + 1) * 128)
                sv = []
                for k in range(2):
                    hh = 2 * q + k
                    h = g * HPG + hh
                    ln = 16 * d + h
                    half1 = lo1 if k == 0 else jnp.logical_not(lo1)
                    sA = jnp.exp(_lane_bcast(aend_row, ln))
                    hsum = jnp.sum(jnp.where(half1, hhs[q], 0.0), axis=1, keepdims=True)
                    erow = erow + jnp.where(lane1 == ln, sA[:, 0:1] * hsum + ssum[hh], 0.0)
                    rp = rparts[hh]
                    r_rows = jnp.where(srowQ == ln, _colsum(rp), r_rows)
                    if d == 0:
                        dskacc = dskacc + jnp.where(
                            lane1 == h, jnp.sum(jnp.where(half1, yxs[q], 0.0), axis=1, keepdims=True), 0.0)
                    sv.append(sA)
                svs.append(jnp.where(lo1, sv[0], sv[1]))
                dY2 = _halves(dy_ref[:, ps] * dyscale, lo_full(), 0)
                dx_ref[:, ps] += _dot(s_wt[q], dY2, NN)
            s_cd[end:end + 1, :] += erow
            dcbt_g = s_dcbt[...]
            dx_ref[:, D + g * NS:D + (g + 1) * NS] = _dot(s_xd[...], dHTg, NT) + _dot(dcbt_g, Cg, NN)
            dx_ref[:, D + 2 * NS + g * NS:D + 2 * NS + (g + 1) * NS] = s_dcg[...] + _dot(dcbt_g, Bg, TN)
            dHT_ref[:, gs] = jnp.concatenate(svs, axis=1) * dHTg + _dot(Cg.T, s_dye[...], NN)
        dacum = s_cd[...] + r_rows.T
        da = _cumsum_rows(dacum, not rev)
        laneQ = lax.broadcasted_iota(jnp.int32, (Q, 128), 1)
        mine = (laneQ >= 16 * d) & (laneQ < 16 * d + 16)
        ddt = jnp.where(mine, s_cdt[...] + da * a32, 0.0)
        ddt_ref[...] = ddt * _sig(dtraw + dtb)
        st_ref[0:1, :] += _colsum(jnp.where(mine, da * dt, 0.0))
        if d == 0:
            st_ref[1:2, :] += dskacc

    def lo_full():
        return lax.broadcasted_iota(jnp.int32, (Q, 128), 1) < HP

    def body(xf_ref, xb_ref, df_ref, db_ref, prm_ref, dsk_ref, dyf_ref, dyb_ref, hf_ref, hb_ref,
             dxf_ref, dxb_ref, ddf_ref, ddb_ref, st_ref, dHf, dHb, *scr):
        t = pl.program_id(0)

        @pl.when(t == 0)
        def _():
            dHf[...] = jnp.zeros_like(dHf)
            dHb[...] = jnp.zeros_like(dHb)
            st_ref[...] = jnp.zeros_like(st_ref)

        s = rs(t)
        one_dir(xf_ref, df_ref, prm_ref, dsk_ref, dyf_ref, cf(s) >= ncl, hf_ref, dHf,
                dxf_ref, ddf_ref, st_ref, *scr, 0)
        one_dir(xb_ref, db_ref, prm_ref, dsk_ref, dyb_ref, cb(s) >= ncl, hb_ref, dHb,
                dxb_ref, ddb_ref, st_ref, *scr, 1)

        @pl.when(t == nc - 1)
        def _():
            st_ref[0:1, :] = -jnp.exp(prm_ref[1:2, :]) * st_ref[0:1, :]

    def lat(c):
        return jnp.minimum(c, ncl - 1)

    xsh = jax.ShapeDtypeStruct((RT, 1536), F32)
    dsh = jax.ShapeDtypeStruct((RT, 128), F32)
    hspec = pl.BlockSpec((1, NS, NH * HP), lambda t: (rs(t), 0, 0))
    return pl.pallas_call(
        body, name="ssd_bwd", grid=(nc,),
        in_specs=[pl.BlockSpec((Q, 1536), lambda t: (cf(rs(t)), 0)),
                  pl.BlockSpec((Q, 1536), lambda t: (cb(rs(t)), 0)),
                  pl.BlockSpec((Q, 128), lambda t: (cf(rs(t)), ODT // 128)),
                  pl.BlockSpec((Q, 128), lambda t: (cb(rs(t)), ODT // 128)),
                  _cst((8, 128)), _cst((1, D)),
                  pl.BlockSpec((Q, D), lambda t: (lat(cf(rs(t))), 0)),
                  pl.BlockSpec((Q, D), lambda t: (lat(cb(rs(t))), 0)),
                  hspec, hspec],
        out_specs=[pl.BlockSpec((Q, 1536), lambda t: (cf(rs(t)), 0)),
                   pl.BlockSpec((Q, 1536), lambda t: (cb(rs(t)), 0)),
                   pl.BlockSpec((Q, 128), lambda t: (cf(rs(t)), 0)),
                   pl.BlockSpec((Q, 128), lambda t: (cb(rs(t)), 0)),
                   _cst((8, 128))],
        out_shape=[xsh, xsh, dsh, dsh, jax.ShapeDtypeStruct((8, 128), F32)],
        scratch_shapes=[pltpu.VMEM((NS, NH * HP), F32), pltpu.VMEM((NS, NH * HP), F32),
                        pltpu.VMEM((npair, 2 * Q, Q), F32), pltpu.VMEM((npair, 2 * Q, NS), F32),
                        pltpu.VMEM((npair, Q, 2 * Q), _MXU), pltpu.VMEM((Q, 512), _MXU),
                        pltpu.VMEM((Q, 512), _MXU), pltpu.VMEM((Q, Q), F32), pltpu.VMEM((Q, NS), F32),
                        pltpu.VMEM((Q, 128), F32), pltpu.VMEM((Q, 128), F32)],
        compiler_params=_params(("arbitrary",)),
    )(xbc, xbc, p, p, prm, dsk, dyd, dyd, hpf, hpb)


def _mix_fwd_vals(yf, yb, z, xs, u, v, dsk, sg, gg, gb):
    y = yf + yb + xs * dsk
    sz = _sig(z)
    hh = y * z * sz
    r = lax.rsqrt(jnp.mean(hh * hh, axis=-1, keepdims=True) + EPS)
    nh = hh * r
    ug, tu = _gelu(u)
    vg, tv = _gelu(v)
    vhat, vrstd = _ln(vg)
    vn = vhat * gg + gb
    return y, sz, r, nh, ug, tu, vg, tv, vhat, vrstd, vn


def _mix_fwd(yf, yb, p, xbc, dsk, sg, gg, gb, ws, bsT):
    L = yf.shape[0] - TL
    nt = L // TL

    def body(yf_ref, yb_ref, z_ref, xs_ref, u_ref, v_ref, dsk_ref, sg_ref, gg_ref, gb_ref,
             ws_ref, bs_ref, ys_ref, ym_ref):
        _, _, _, nh, ug, _, _, _, _, _, vn = _mix_fwd_vals(
            yf_ref[...], yb_ref[...], z_ref[...], xs_ref[...], u_ref[...], v_ref[...],
            dsk_ref[...], sg_ref[...], gg_ref[...], gb_ref[...])
        ys_ref[...] = (nh * sg_ref[...]).astype(ys_ref.dtype)
        for n in range(TL // Q):
            rs_ = slice(n * Q, (n + 1) * Q)
            for g in range(8):
                cs = slice(g * 128, (g + 1) * 128)
                mixed = _dot(ws_ref[g], vn[rs_, cs], NN) + bs_ref[:, g:g + 1]
                ym_ref[rs_, cs] = (ug[rs_, cs] * mixed).astype(ym_ref.dtype)

    return pl.pallas_call(
        body, name="mix_fwd", grid=(nt,),
        in_specs=[_rt(D), _rt(D), _rt(D, OZ // D), _rt(D, 0), _rt(D, OU // D), _rt(D, OV // D),
                  _cst((1, D)), _cst((1, D)), _cst((1, D)), _cst((1, D)),
                  _cst((8, 128, 128)), _cst((128, 128))],
        out_specs=[_rt(D), _rt(D)],
        out_shape=[jax.ShapeDtypeStruct((L, D), _MXU), jax.ShapeDtypeStruct((L, D), _MXU)],
        compiler_params=_params(("parallel",)),
    )(yf, yb, p, xbc, p, p, dsk, sg, gg, gb, ws, bsT)


def _mix_bwd(dys, dym, yf, yb, p, xbc, dp, dsk, sg, gg, gb, ws, bsT):
    L = dys.shape[0]
    nt = L // TL

    def body(dys_ref, dym_ref, yf_ref, yb_ref, z_ref, xs_ref, u_ref, v_ref, dsk_ref, sg_ref,
             gg_ref, gb_ref, ws_ref, bs_ref, dp_any, dzuv_ref, dy_ref, st_ref,
             dws_ref, dbs_ref, dvn_s):
        del dp_any
        dz_ref = dzuv_ref.at[:, OZ:OZ + D]
        du_ref = dzuv_ref.at[:, OU:OU + D]
        dv_ref = dzuv_ref.at[:, OV:OV + D]

        @pl.when(pl.program_id(0) == 0)
        def _():
            st_ref[...] = jnp.zeros_like(st_ref)
            dws_ref[...] = jnp.zeros_like(dws_ref)
            dbs_ref[...] = jnp.zeros_like(dbs_ref)

        z = z_ref[...]
        u = u_ref[...]
        v = v_ref[...]
        y, sz, r, nh, ug, tu, vg, tv, vhat, vrstd, vn = _mix_fwd_vals(
            yf_ref[...], yb_ref[...], z, xs_ref[...], u, v,
            dsk_ref[...], sg_ref[...], gg_ref[...], gb_ref[...])
        dys = dys_ref[...]
        st_ref[0:1, :] += _colsum(dys * nh)
        dn = dys * sg_ref[...]
        dhh = r * (dn - nh * jnp.mean(dn * nh, axis=-1, keepdims=True))
        dy_ref[...] = dhh * z * sz
        dz_ref[...] = (dhh * y * (sz * (1.0 + z * (1.0 - sz)))).astype(dz_ref.dtype)
        dym = dym_ref[...]
        lane = lax.broadcasted_iota(jnp.int32, (Q, 128), 1)
        dbs = jnp.zeros((Q, 128), F32)
        gu = _gelu_grad(u, tu)
        for n in range(TL // Q):
            rs_ = slice(n * Q, (n + 1) * Q)
            for g in range(8):
                cs = slice(g * 128, (g + 1) * 128)
                vb = vn[rs_, cs]
                mixed = _dot(ws_ref[g], vb, NN) + bs_ref[:, g:g + 1]
                dyb = dym[rs_, cs]
                dmx = dyb * ug[rs_, cs]
                du_ref[rs_, cs] = (dyb * mixed * gu[rs_, cs]).astype(du_ref.dtype)
                dvn_s[rs_, cs] = _dot(ws_ref[g], dmx, TN)
                dws_ref[g] += _dot(dmx, vb, NT)
                dbs = dbs + jnp.where(lane == g, jnp.sum(dmx, axis=1, keepdims=True), 0.0)
        dbs_ref[...] += dbs
        dvn = dvn_s[...]
        st_ref[1:2, :] += _colsum(dvn * vhat)
        st_ref[2:3, :] += _colsum(dvn)
        dvg = _ln_bwd(dvn * gg_ref[...], vhat, vrstd)
        dv_ref[...] = (dvg * _gelu_grad(v, tv)).astype(dv_ref.dtype)

    outs = pl.pallas_call(
        body, name="mix_bwd", grid=(nt,),
        in_specs=[_rt(D), _rt(D), _rt(D), _rt(D), _rt(D, OZ // D), _rt(D, 0), _rt(D, OU // D),
                  _rt(D, OV // D), _cst((1, D)), _cst((1, D)), _cst((1, D)), _cst((1, D)),
                  _cst((8, 128, 128)), _cst((128, 128)), pl.BlockSpec(memory_space=pl.ANY)],
        out_specs=[_rt(3 * D, 0), _rt(D), _cst((8, D)),
                   _cst((8, 128, 128)), _cst((128, 128))],
        out_shape=[jax.ShapeDtypeStruct(dp.shape, dp.dtype),
                   jax.ShapeDtypeStruct((L, D), F32), jax.ShapeDtypeStruct((8, D), F32),
                   jax.ShapeDtypeStruct((8, 128, 128), F32), jax.ShapeDtypeStruct((128, 128), F32)],
        scratch_shapes=[pltpu.VMEM((TL, D), F32)],
        input_output_aliases={14: 0},
        compiler_params=_params(("arbitrary",)),
    )(dys, dym, yf, yb, p, xbc, p, p, dsk, sg, gg, gb, ws, bsT, dp)
    return outs


def _gate_fwd(a1, a2, p, bg):
    L = a1.shape[0]

    def body(a1_ref, a2_ref, g_ref, bg_ref, m_ref):
        gt = _sig(g_ref[...] + bg_ref[...])
        m_ref[...] = (gt[:, :D] * a1_ref[...] + gt[:, D:] * a2_ref[...]).astype(m_ref.dtype)

    return pl.pallas_call(
        body, name="gate_fwd", grid=(L // TL,),
        in_specs=[_rt(D), _rt(D), _rt(2 * D, OG // (2 * D)), _cst((1, 2 * D))],
        out_specs=_rt(D), out_shape=jax.ShapeDtypeStruct((L, D), _MXU),
        compiler_params=_params(("parallel",)),
    )(a1, a2, p, bg)


def _gate_bwd(dmg, a1, a2, p, bg, dp):
    L = a1.shape[0]

    def body(dm_ref, a1_ref, a2_ref, g_ref, bg_ref, dp_any, dg_ref, da1_ref, da2_ref, st_ref):
        del dp_any

        @pl.when(pl.program_id(0) == 0)
        def _():
            st_ref[...] = jnp.zeros_like(st_ref)

        gt = _sig(g_ref[...] + bg_ref[...])
        g1 = gt[:, :D]
        g2 = gt[:, D:]
        dm = dm_ref[...]
        da1_ref[...] = (dm * g1).astype(da1_ref.dtype)
        da2_ref[...] = (dm * g2).astype(da2_ref.dtype)
        dg1 = dm * a1_ref[...] * g1 * (1.0 - g1)
        dg2 = dm * a2_ref[...] * g2 * (1.0 - g2)
        st_ref[0:1, 0:D] += _colsum(dg1)
        st_ref[0:1, D:2 * D] += _colsum(dg2)
        dg_ref[:, 0:D] = dg1.astype(dg_ref.dtype)
        dg_ref[:, D:2 * D] = dg2.astype(dg_ref.dtype)

    return pl.pallas_call(
        body, name="gate_bwd", grid=(L // TL,),
        in_specs=[_rt(D), _rt(D), _rt(D), _rt(2 * D, OG // (2 * D)), _cst((1, 2 * D)),
                  pl.BlockSpec(memory_space=pl.ANY)],
        out_specs=[_rt(2 * D, OG // (2 * D)), _rt(D), _rt(D), _cst((8, 2 * D))],
        out_shape=[jax.ShapeDtypeStruct(dp.shape, dp.dtype), jax.ShapeDtypeStruct((L, D), _MXU),
                   jax.ShapeDtypeStruct((L, D), _MXU), jax.ShapeDtypeStruct((8, 2 * D), F32)],
        input_output_aliases={5: 0},
        compiler_params=_params(("arbitrary",)),
    )(dmg, a1, a2, p, bg, dp)


def _merge_fwd(yssd, ygm, p, bg, ws, wg, wo, xn, modx, g1, b1):
    L = yssd.shape[0]
    tm = TL

    def body(ys_ref, yg_ref, g_ref, bg_ref, ws_ref, wg_ref, wo_ref, xn_ref, mx_ref, g1_ref, b1_ref,
             a1_ref, a2_ref, m_ref, o_ref, r1_ref, h2_ref):
        a1 = _dot(ys_ref[...], ws_ref[...], NN)
        a2 = _dot(yg_ref[...], wg_ref[...], NN)
        gt = _sig(g_ref[...] + bg_ref[...])
        mg = gt[:, :D] * a1 + gt[:, D:] * a2
        a1_ref[...] = a1
        a2_ref[...] = a2
        m_ref[...] = mg.astype(m_ref.dtype)
        out = _dot(mg, wo_ref[...], NN)
        o_ref[...] = out
        r1 = ALPHA * xn_ref[...] + mx_ref[2:3, :] * out
        xhat, _ = _ln(r1)
        x1 = xhat * g1_ref[...] + b1_ref[...]
        r1_ref[...] = r1
        h2_ref[...] = (x1 * (1.0 + mx_ref[4:5, :]) + mx_ref[3:4, :]).astype(h2_ref.dtype)

    rows = pl.BlockSpec((tm, D), lambda i: (i, 0))
    f32s = jax.ShapeDtypeStruct((L, D), F32)
    mxus = jax.ShapeDtypeStruct((L, D), _MXU)
    return pl.pallas_call(
        body, name="merge_fwd", grid=(L // tm,),
        in_specs=[rows, rows, pl.BlockSpec((tm, 2 * D), lambda i: (i, OG // (2 * D))), _cst((1, 2 * D)),
                  _cst((D, D)), _cst((D, D)), _cst((D, D)), rows, _cst((8, D)), _cst((1, D)), _cst((1, D))],
        out_specs=[rows] * 6,
        out_shape=[f32s, f32s, mxus, f32s, f32s, mxus],
        compiler_params=_params(("parallel",)),
    )(yssd, ygm, p, bg, ws, wg, wo, xn, modx, g1, b1)


def _mm_o2_res2(ff, w2, r1, tgt, modx, g1, b1, g2, b2):
    L, K = ff.shape
    tm = 2 * TL
    tk = K // 2
    nk = K // tk

    def body(a_ref, b_ref, r1_ref, t_ref, mx_ref, g1_ref, b1_ref, g2_ref, b2_ref,
             dr2_ref, do2_ref, st_ref, loss_ref, acc_ref):
        i, k = pl.program_id(0), pl.program_id(1)

        @pl.when((i == 0) & (k == 0))
        def _():
            st_ref[...] = jnp.zeros_like(st_ref)
            loss_ref[...] = jnp.zeros_like(loss_ref)

        prod = _dot(a_ref[...], b_ref[...], NN)

        @pl.when(k == 0)
        def _():
            acc_ref[...] = prod

        @pl.when(k == nk - 1)
        def _():
            o2 = acc_ref[...] + prod
            xh1, _ = _ln(r1_ref[...])
            x1 = xh1 * g1_ref[...] + b1_ref[...]
            g2x = mx_ref[5:6, :]
            xh2, rstd2 = _ln(ALPHA * x1 + g2x * o2)
            err = xh2 * g2_ref[...] + b2_ref[...] - t_ref[...]
            per_tok = jnp.mean(err * err, axis=-1, keepdims=True)
            loss_ref[...] += 0.5 * jnp.sum(per_tok, axis=0, keepdims=True)
            dy = err * (1.0 / D)
            st_ref[0:1, :] += _colsum(dy * xh2)
            st_ref[1:2, :] += _colsum(dy)
            dr2 = _ln_bwd(dy * g2_ref[...], xh2, rstd2)
            st_ref[2:3, :] += _colsum(dr2 * o2)
            dr2_ref[...] = dr2
            do2_ref[...] = (g2x * dr2).astype(do2_ref.dtype)

    assert nk == 2
    rows = pl.BlockSpec((tm, D), lambda i, k: (i, 0))
    vec = pl.BlockSpec((1, D), lambda i, k: (0, 0))
    return pl.pallas_call(
        body, name="mm_o2_res2", grid=(L // tm, nk),
        in_specs=[pl.BlockSpec((tm, tk), lambda i, k: (i, k)), pl.BlockSpec((tk, D), lambda i, k: (k, 0)),
                  rows, rows, pl.BlockSpec((8, D), lambda i, k: (0, 0)), vec, vec, vec, vec],
        out_specs=[rows, rows, pl.BlockSpec((8, D), lambda i, k: (0, 0)),
                   pl.BlockSpec((8, 128), lambda i, k: (0, 0))],
        out_shape=[jax.ShapeDtypeStruct((L, D), F32), jax.ShapeDtypeStruct((L, D), _MXU),
                   jax.ShapeDtypeStruct((8, D), F32), jax.ShapeDtypeStruct((8, 128), F32)],
        scratch_shapes=[pltpu.VMEM((tm, D), F32)],
        compiler_params=_params(("arbitrary", "arbitrary")),
    )(ff, w2, r1, tgt, modx, g1, b1, g2, b2)


def _mm_dh2_res1bwd(df13, w13i, dr2, r1, out, modx, g1, b1):
    L, K = df13.shape
    tm = 2 * TL
    tk = HFF
    nk = K // tk

    def body(a_ref, b_ref, dr2_ref, r1_ref, o_ref, mx_ref, g_ref, bb_ref, dr1_ref, do_ref, st_ref, acc_ref):
        i, k = pl.program_id(0), pl.program_id(1)

        @pl.when((i == 0) & (k == 0))
        def _():
            st_ref[...] = jnp.zeros_like(st_ref)

        prod = _dot(a_ref[...], b_ref[...], NN)

        @pl.when(k == 0)
        def _():
            acc_ref[...] = prod

        @pl.when((k > 0) & (k < nk - 1))
        def _():
            acc_ref[...] += prod

        @pl.when(k == nk - 1)
        def _():
            dh2 = acc_ref[...] + prod
            xh1, rstd1 = _ln(r1_ref[...])
            x1 = xh1 * g_ref[...] + bb_ref[...]
            dx1 = ALPHA * dr2_ref[...] + dh2 * (1.0 + mx_ref[4:5, :])
            st_ref[0:1, :] += _colsum(dh2 * x1)
            st_ref[1:2, :] += _colsum(dh2)
            st_ref[2:3, :] += _colsum(dx1 * xh1)
            st_ref[3:4, :] += _colsum(dx1)
            dr1 = _ln_bwd(dx1 * g_ref[...], xh1, rstd1)
            st_ref[4:5, :] += _colsum(dr1 * o_ref[...])
            dr1_ref[...] = dr1
            do_ref[...] = (mx_ref[2:3, :] * dr1).astype(do_ref.dtype)

    assert nk >= 2
    rows = pl.BlockSpec((tm, D), lambda i, k: (i, 0))
    vec = pl.BlockSpec((1, D), lambda i, k: (0, 0))
    return pl.pallas_call(
        body, name="mm_dh2_res1bwd", grid=(L // tm, nk),
        in_specs=[pl.BlockSpec((tm, tk), lambda i, k: (i, k)), pl.BlockSpec((tk, D), lambda i, k: (k, 0)),
                  rows, rows, rows, pl.BlockSpec((8, D), lambda i, k: (0, 0)), vec, vec],
        out_specs=[rows, rows, pl.BlockSpec((8, D), lambda i, k: (0, 0))],
        out_shape=[jax.ShapeDtypeStruct((L, D), F32), jax.ShapeDtypeStruct((L, D), _MXU),
                   jax.ShapeDtypeStruct((8, D), F32)],
        scratch_shapes=[pltpu.VMEM((tm, D), F32)],
        compiler_params=_params(("arbitrary", "arbitrary")),
    )(df13, w13i, dr2, r1, out, modx, g1, b1)


def _merge_bwd(dout, a1, a2, p, bg, wo, ws, wg, dp):
    L = a1.shape[0]
    tm = TL

    def body(do_ref, a1_ref, a2_ref, g_ref, bg_ref, wo_ref, ws_ref, wg_ref, dp_any,
             dg_ref, da1_ref, da2_ref, st_ref, dys_ref, dym_ref):
        del dp_any

        @pl.when(pl.program_id(0) == 0)
        def _():
            st_ref[...] = jnp.zeros_like(st_ref)

        dm = _dot(do_ref[...], wo_ref[...], NT)
        gt = _sig(g_ref[...] + bg_ref[...])
        g1 = gt[:, :D]
        g2 = gt[:, D:]
        da1 = (dm * g1).astype(da1_ref.dtype)
        da2 = (dm * g2).astype(da2_ref.dtype)
        da1_ref[...] = da1
        da2_ref[...] = da2
        dg1 = dm * a1_ref[...] * g1 * (1.0 - g1)
        dg2 = dm * a2_ref[...] * g2 * (1.0 - g2)
        st_ref[0:1, 0:D] += _colsum(dg1)
        st_ref[0:1, D:2 * D] += _colsum(dg2)
        dg_ref[:, 0:D] = dg1.astype(dg_ref.dtype)
        dg_ref[:, D:2 * D] = dg2.astype(dg_ref.dtype)
        dys_ref[...] = _dot(da1, ws_ref[...], NT)
        dym_ref[...] = _dot(da2, wg_ref[...], NT)

    rows = pl.BlockSpec((tm, D), lambda i: (i, 0))
    gates = pl.BlockSpec((tm, 2 * D), lambda i: (i, OG // (2 * D)))
    f32s = jax.ShapeDtypeStruct((L, D), F32)
    mxus = jax.ShapeDtypeStruct((L, D), _MXU)
    return pl.pallas_call(
        body, name="merge_bwd", grid=(L // tm,),
        in_specs=[rows, rows, rows, gates, _cst((1, 2 * D)), _cst((D, D)), _cst((D, D)), _cst((D, D)),
                  pl.BlockSpec(memory_space=pl.ANY)],
        out_specs=[gates, rows, rows, _cst((8, 2 * D)), rows, rows],
        out_shape=[jax.ShapeDtypeStruct(dp.shape, dp.dtype), mxus, mxus,
                   jax.ShapeDtypeStruct((8, 2 * D), F32), f32s, f32s],
        input_output_aliases={8: 0},
        compiler_params=_params(("arbitrary",)),
    )(dout, a1, a2, p, bg, wo, ws, wg, dp)


def _res1_fwd(xn, out, modx, g, b):
    L = out.shape[0]

    def body(xn_ref, o_ref, mx_ref, g_ref, b_ref, r1_ref, h2_ref):
        r1 = ALPHA * xn_ref[...] + mx_ref[2:3, :] * o_ref[...]
        xhat, _ = _ln(r1)
        x1 = xhat * g_ref[...] + b_ref[...]
        r1_ref[...] = r1
        h2_ref[...] = (x1 * (1.0 + mx_ref[4:5, :]) + mx_ref[3:4, :]).astype(h2_ref.dtype)

    return pl.pallas_call(
        body, name="res1_fwd", grid=(L // TL,),
        in_specs=[_rt(D), _rt(D), _cst((8, D)), _cst((1, D)), _cst((1, D))],
        out_specs=[_rt(D), _rt(D)],
        out_shape=[jax.ShapeDtypeStruct((L, D), F32), jax.ShapeDtypeStruct((L, D), _MXU)],
        compiler_params=_params(("parallel",)),
    )(xn, out, modx, g, b)


HFF = DFF // 2


def _mm_f13_glu(h2, w13i):
    L = h2.shape[0]
    tm = 512

    def body(a_ref, b_ref, f_ref, ff_ref):
        f = _dot(a_ref[...], b_ref[...], NT)
        f_ref[...] = f
        f1 = f[:, :HFF]
        ff_ref[...] = (f1 * _sig(f1) * f[:, HFF:]).astype(ff_ref.dtype)

    return pl.pallas_call(
        body, name="mm_f13_glu", grid=(DFF // HFF, L // tm),
        in_specs=[pl.BlockSpec((tm, D), lambda j, i: (i, 0)), pl.BlockSpec((2 * HFF, D), lambda j, i: (j, 0))],
        out_specs=[pl.BlockSpec((tm, 2 * HFF), lambda j, i: (i, j)), pl.BlockSpec((tm, HFF), lambda j, i: (i, j))],
        out_shape=[jax.ShapeDtypeStruct((L, 2 * DFF), F32), jax.ShapeDtypeStruct((L, DFF), _MXU)],
        compiler_params=_params(("parallel", "parallel")),
    )(h2, w13i)


def _mm_dff_glu(do2, w_ff2_f, f13i):
    L = do2.shape[0]
    tm = 512

    def body(a_ref, b_ref, f_ref, o_ref):
        d = _dot(a_ref[...], b_ref[...], NT)
        f1 = f_ref[:, :HFF]
        s = _sig(f1)
        o_ref[:, :HFF] = (d * f_ref[:, HFF:] * (s * (1.0 + f1 * (1.0 - s)))).astype(o_ref.dtype)
        o_ref[:, HFF:] = (d * f1 * s).astype(o_ref.dtype)

    return pl.pallas_call(
        body, name="mm_dff_glu", grid=(DFF // HFF, L // tm),
        in_specs=[pl.BlockSpec((tm, D), lambda j, i: (i, 0)), pl.BlockSpec((HFF, D), lambda j, i: (j, 0)),
                  pl.BlockSpec((tm, 2 * HFF), lambda j, i: (i, j))],
        out_specs=pl.BlockSpec((tm, 2 * HFF), lambda j, i: (i, j)),
        out_shape=jax.ShapeDtypeStruct((L, 2 * DFF), _MXU),
        compiler_params=_params(("parallel", "parallel")),
    )(do2, w_ff2_f, f13i)


def _glu_fwd(f13):
    L = f13.shape[0]

    def body(f1_ref, f3_ref, o_ref):
        f1 = f1_ref[...]
        o_ref[...] = (f1 * _sig(f1) * f3_ref[...]).astype(o_ref.dtype)

    return pl.pallas_call(
        body, name="glu_fwd", grid=(L // TL,),
        in_specs=[_rt(DFF, 0), _rt(DFF, 1)], out_specs=_rt(DFF),
        out_shape=jax.ShapeDtypeStruct((L, DFF), _MXU),
        compiler_params=_params(("parallel",)),
    )(f13, f13)


def _glu_bwd(dff, f13):
    L = f13.shape[0]

    def body(d_ref, f1_ref, f3_ref, o_ref):
        f1 = f1_ref[...]
        s = _sig(f1)
        d = d_ref[...]
        o_ref[:, 0:DFF] = (d * f3_ref[...] * (s * (1.0 + f1 * (1.0 - s)))).astype(o_ref.dtype)
        o_ref[:, DFF:2 * DFF] = (d * f1 * s).astype(o_ref.dtype)

    return pl.pallas_call(
        body, name="glu_bwd", grid=(L // TL,),
        in_specs=[_rt(DFF), _rt(DFF, 0), _rt(DFF, 1)], out_specs=_rt(2 * DFF),
        out_shape=jax.ShapeDtypeStruct((L, 2 * DFF), _MXU),
        compiler_params=_params(("parallel",)),
    )(dff, f13, f13)


def _res2(r1, o2, tgt, modx, g1, b1, g2, b2):
    L = r1.shape[0]

    def body(r1_ref, o2_ref, t_ref, mx_ref, g1_ref, b1_ref, g2_ref, b2_ref,
             dr2_ref, do2_ref, st_ref, loss_ref):
        @pl.when(pl.program_id(0) == 0)
        def _():
            st_ref[...] = jnp.zeros_like(st_ref)
            loss_ref[...] = jnp.zeros_like(loss_ref)

        xh1, _ = _ln(r1_ref[...])
        x1 = xh1 * g1_ref[...] + b1_ref[...]
        o2 = o2_ref[...]
        g2x = mx_ref[5:6, :]
        xh2, rstd2 = _ln(ALPHA * x1 + g2x * o2)
        err = xh2 * g2_ref[...] + b2_ref[...] - t_ref[...]
        per_tok = jnp.mean(err * err, axis=-1, keepdims=True)
        loss_ref[...] += 0.5 * jnp.sum(per_tok, axis=0, keepdims=True)
        dy = err * (1.0 / D)
        st_ref[0:1, :] += _colsum(dy * xh2)
        st_ref[1:2, :] += _colsum(dy)
        dr2 = _ln_bwd(dy * g2_ref[...], xh2, rstd2)
        st_ref[2:3, :] += _colsum(dr2 * o2)
        dr2_ref[...] = dr2
        do2_ref[...] = (g2x * dr2).astype(do2_ref.dtype)

    return pl.pallas_call(
        body, name="res2", grid=(L // TL,),
        in_specs=[_rt(D), _rt(D), _rt(D), _cst((8, D))] + [_cst((1, D))] * 4,
        out_specs=[_rt(D), _rt(D), _cst((8, D)), _cst((8, 128))],
        out_shape=[jax.ShapeDtypeStruct((L, D), F32), jax.ShapeDtypeStruct((L, D), _MXU),
                   jax.ShapeDtypeStruct((8, D), F32), jax.ShapeDtypeStruct((8, 128), F32)],
        compiler_params=_params(("arbitrary",)),
    )(r1, o2, tgt, modx, g1, b1, g2, b2)


def _res1_bwd(dr2, dh2, r1, out, modx, g1, b1):
    L = r1.shape[0]

    def body(dr2_ref, dh2_ref, r1_ref, o_ref, mx_ref, g_ref, b_ref, dr1_ref, do_ref, st_ref):
        @pl.when(pl.program_id(0) == 0)
        def _():
            st_ref[...] = jnp.zeros_like(st_ref)

        xh1, rstd1 = _ln(r1_ref[...])
        x1 = xh1 * g_ref[...] + b_ref[...]
        dh2 = dh2_ref[...]
        dx1 = ALPHA * dr2_ref[...] + dh2 * (1.0 + mx_ref[4:5, :])
        st_ref[0:1, :] += _colsum(dh2 * x1)
        st_ref[1:2, :] += _colsum(dh2)
        st_ref[2:3, :] += _colsum(dx1 * xh1)
        st_ref[3:4, :] += _colsum(dx1)
        dr1 = _ln_bwd(dx1 * g_ref[...], xh1, rstd1)
        st_ref[4:5, :] += _colsum(dr1 * o_ref[...])
        dr1_ref[...] = dr1
        do_ref[...] = (mx_ref[2:3, :] * dr1).astype(do_ref.dtype)

    return pl.pallas_call(
        body, name="res1_bwd", grid=(L // TL,),
        in_specs=[_rt(D), _rt(D), _rt(D), _rt(D), _cst((8, D)), _cst((1, D)), _cst((1, D))],
        out_specs=[_rt(D), _rt(D), _cst((8, D))],
        out_shape=[jax.ShapeDtypeStruct((L, D), F32), jax.ShapeDtypeStruct((L, D), _MXU),
                   jax.ShapeDtypeStruct((8, D), F32)],
        compiler_params=_params(("arbitrary",)),
    )(dr2, dh2, r1, out, modx, g1, b1)


def _conv_bwd(dxf, dxb, p, conv_w8, conv_b, dp):
    RT = p.shape[0]
    chunks = _seq_chunks(RT - TL)

    def body(df_ref, db_ref, p_ref, w_ref, b_ref, dp_any, o_ref, dw_ref, dbias_ref, dpre_s):
        del dp_any
        w = w_ref[...]
        bias = b_ref[...]
        srow = lax.broadcasted_iota(jnp.int32, (8, 128), 0)
        dwacc = jnp.zeros((8, 128), F32)
        dbacc = jnp.zeros((1, 128), F32)
        for r0, first, last in chunks:
            taps = _conv_taps(p_ref, r0, first, last)
            pre = bias + sum(w[k:k + 1, :] * taps[k] for k in range(5))
            s = _sig(pre)
            dpre = (df_ref[pl.ds(r0, TL), :] + db_ref[pl.ds(r0, TL), :]) * (s * (1.0 + pre * (1.0 - s)))
            dpre_s[pl.ds(r0, TL), :] = dpre
            dbacc = dbacc + _colsum(dpre)
            for k in range(5):
                dwacc = dwacc + jnp.where(srow == k, _colsum(dpre * taps[k]), 0.0)
        for r0, first, last in chunks:
            taps = _conv_taps(dpre_s, r0, first, last)
            dx = sum(w[k:k + 1, :] * taps[4 - k] for k in range(5))
            o_ref[pl.ds(r0, TL), :] = dx.astype(o_ref.dtype)
        dw_ref[...] = dwacc
        dbias_ref[...] = jnp.broadcast_to(dbacc, (8, 128))

    cspec = pl.BlockSpec((RT, 128), lambda j: (0, j))
    wspec = pl.BlockSpec((8, 128), lambda j: (0, j))
    return pl.pallas_call(
        body, name="conv_bwd", grid=(12,),
        in_specs=[cspec, cspec, pl.BlockSpec((RT, 128), lambda j: (0, _xbc_colblk(j))),
                  wspec, pl.BlockSpec((1, 128), lambda j: (0, j)), pl.BlockSpec(memory_space=pl.ANY)],
        out_specs=[pl.BlockSpec((RT, 128), lambda j: (0, _xbc_colblk(j))), wspec, wspec],
        out_shape=[jax.ShapeDtypeStruct(dp.shape, dp.dtype), jax.ShapeDtypeStruct((8, 1536), F32),
                   jax.ShapeDtypeStruct((8, 1536), F32)],
        scratch_shapes=[pltpu.VMEM((RT, 128), F32)],
        input_output_aliases={5: 0},
        compiler_params=_params(("parallel",)),
    )(dxf, dxb, p, conv_w8, conv_b, dp)


def _dt_bwd(ddf, ddb, dp):
    RT = ddf.shape[0]

    def body(f_ref, b_ref, dp_any, o_ref, st_ref):
        del dp_any

        @pl.when(pl.program_id(0) == 0)
        def _():
            st_ref[...] = jnp.zeros_like(st_ref)

        s = f_ref[...] + b_ref[...]
        o_ref[...] = s.astype(o_ref.dtype)
        st_ref[0:1, :] += _colsum(s)

    return pl.pallas_call(
        body, name="dt_bwd", grid=(RT // TL,),
        in_specs=[_rt(128), _rt(128), pl.BlockSpec(memory_space=pl.ANY)],
        out_specs=[_rt(128, ODT // 128), _cst((8, 128))],
        out_shape=[jax.ShapeDtypeStruct(dp.shape, dp.dtype), jax.ShapeDtypeStruct((8, 128), F32)],
        input_output_aliases={2: 0},
        compiler_params=_params(("arbitrary",)),
    )(ddf, ddb, dp)


def _ln0_bwd(dh1, dr1, x, ctx, g, b, modx, modc):
    L = x.shape[0]
    nt = L // TL

    def body(dh_ref, dr1_ref, x_ref, c_ref, g_ref, b_ref, mx_ref, mc_ref, gx_ref, st_ref):
        i = pl.program_id(0)
        isc = i == nt

        @pl.when(i == 0)
        def _():
            st_ref[...] = jnp.zeros_like(st_ref)

        xin = jnp.where(isc, c_ref[...], x_ref[...])
        xhat, rstd = _ln(xin)
        xn = xhat * g_ref[...] + b_ref[...]
        sc = jnp.where(isc, mc_ref[1:2, :], mx_ref[1:2, :])
        dh = dh_ref[...]
        lat = jnp.where(isc, 0.0, 1.0)
        dxn = dh * (1.0 + sc) + (lat * ALPHA) * dr1_ref[...]
        tsh = _colsum(dh)
        tsc = _colsum(dh * xn)
        st_ref[0:1, :] += lat * tsh
        st_ref[1:2, :] += lat * tsc
        st_ref[2:3, :] += (1.0 - lat) * tsh
        st_ref[3:4, :] += (1.0 - lat) * tsc
        st_ref[4:5, :] += _colsum(dxn * xhat)
        st_ref[5:6, :] += _colsum(dxn)

        @pl.when(i < nt)
        def _():
            gx_ref[...] = _ln_bwd(dxn * g_ref[...], xhat, rstd)

    return pl.pallas_call(
        body, name="ln0_bwd", grid=(nt + 1,),
        in_specs=[_rt(D), _rtc(D, nt), _rtc(D, nt), _cst((TL, D)), _cst((1, D)), _cst((1, D)),
                  _cst((8, D)), _cst((8, D))],
        out_specs=[_rtc(D, nt), _cst((8, D))],
        out_shape=[jax.ShapeDtypeStruct((L, D), F32), jax.ShapeDtypeStruct((8, D), F32)],
        compiler_params=_params(("arbitrary",)),
    )(dh1, dr1, x, ctx, g, b, modx, modc)


def _perm_cols(w):
    pad = jnp.zeros((w.shape[0], NPJ - NNAT), w.dtype)
    return jnp.concatenate([w[:, 0:1024], w[:, 2592:3616], w[:, 3616:4640], w[:, 1024:2048],
                            w[:, 4640:6688], w[:, 2048:2304], w[:, 2304:2560], w[:, 2560:2592], pad],
                           axis=1)


SECTIONS = ((0, 1024, OZ), (1024, 2048, OXS), (2048, 2304, OB), (2304, 2560, OC), (2560, 2592, ODT),
            (2592, 3616, OU), (3616, 4640, OV), (4640, 6688, OG))


def _perm_from_blocks(ga):
    n = ga.shape[2]
    pieces = []
    for na, nb, _ in sorted(SECTIONS, key=lambda sec: sec[2]):
        for k in range(NDEV):
            lo, hi = max(na, k * n), min(nb, (k + 1) * n)
            if lo < hi:
                pieces.append(ga[k][:, lo - k * n:hi - k * n])
    pieces.append(jnp.zeros((ga.shape[1], NPJ - NNAT), ga.dtype))
    return jnp.concatenate(pieces, axis=1)


def _blocks_from_perm(gp, n):
    blocks = []
    for k in range(NDEV):
        pieces = []
        for na, nb, po in SECTIONS:
            lo, hi = max(na, k * n), min(nb, (k + 1) * n)
            if lo < hi:
                pieces.append(gp[:, po + lo - na:po + hi - na])
        blocks.append(jnp.concatenate(pieces, axis=1))
    return jnp.stack(blocks)


def _padded(n, row_align):
    unit = row_align * D
    return -(-n // unit) * unit if row_align else n


def _slab(arrs, rows, row_align=0):
    parts = []
    for a in arrs:
        f = a.reshape(-1)
        parts.append(jnp.pad(f, (0, _padded(f.shape[0], row_align) - f.shape[0])))
    flat = jnp.concatenate(parts)
    flat = jnp.pad(flat, (0, rows * D - flat.shape[0]))
    return flat.reshape(rows, D)


def _unslab(slab, shapes, row_align=0):
    out, off = [], 0
    for shp in shapes:
        n = 1
        for s in shp:
            n *= s
        r0, r1 = off // D, -(-(off + n) // D)
        out.append(slab[r0:r1].reshape(-1)[off - r0 * D:off - r0 * D + n].reshape(shp))
        off += _padded(n, row_align)
    return out


def _row(v):
    return v.reshape(1, -1)


def _t(a):
    return jnp.swapaxes(a, 0, 1)


def _pad_rows(a, rows):
    return jnp.pad(a, ((0, rows - a.shape[0]), (0, 0)))


BIG = ["w_in", "w_ssd_proj", "w_gm_proj", "w_out", "w_ff1", "w_ff3", "w_ff2"]
BIG_ROWS = 2304
BIG_ALIGN = 16
REPL = ["c_ctx", "ln0_g", "ln0_b", "b_ada", "conv_b", "dt_bias", "a_log", "d_skip", "ssd_norm_g",
        "gm_norm_g", "gm_norm_b", "w_spatial", "b_spatial", "b_gate", "ln1_g", "ln1_b", "ln2_g", "ln2_b"]
SMALL_ROWS = 160
WEIGHTS = ["c_ctx", "ln0_g", "ln0_b", "w_ada", "b_ada", "w_in", "conv_w", "conv_b", "dt_bias", "a_log",
           "d_skip", "ssd_norm_g", "gm_norm_g", "gm_norm_b", "w_spatial", "b_spatial", "b_gate",
           "w_ssd_proj", "w_gm_proj", "w_out", "ln1_g", "ln1_b", "w_ff1", "w_ff3", "w_ff2", "ln2_g", "ln2_b"]


def kernel(x, c, ctx, c_ctx, ln0_g, ln0_b, w_ada, b_ada, w_in, conv_w, conv_b, dt_bias, a_log, d_skip, ssd_norm_g, gm_norm_g, gm_norm_b, w_spatial, b_spatial, b_gate, w_ssd_proj, w_gm_proj, w_out, ln1_g, ln1_b, w_ff1, w_ff3, w_ff2, ln2_g, ln2_b, loss_target, m_c_ctx, m_ln0_g, m_ln0_b, m_w_ada, m_b_ada, m_w_in, m_conv_w, m_conv_b, m_dt_bias, m_a_log, m_d_skip, m_ssd_norm_g, m_gm_norm_g, m_gm_norm_b, m_w_spatial, m_b_spatial, m_b_gate, m_w_ssd_proj, m_w_gm_proj, m_w_out, m_ln1_g, m_ln1_b, m_w_ff1, m_w_ff3, m_w_ff2, m_ln2_g, m_ln2_b, v_c_ctx, v_ln0_g, v_ln0_b, v_w_ada, v_b_ada, v_w_in, v_conv_w, v_conv_b, v_dt_bias, v_a_log, v_d_skip, v_ssd_norm_g, v_gm_norm_g, v_gm_norm_b, v_w_spatial, v_b_spatial, v_b_gate, v_w_ssd_proj, v_w_gm_proj, v_w_out, v_ln1_g, v_ln1_b, v_w_ff1, v_w_ff3, v_w_ff2, v_ln2_g, v_ln2_b):
    W = dict(c_ctx=c_ctx, ln0_g=ln0_g, ln0_b=ln0_b, w_ada=w_ada, b_ada=b_ada, w_in=w_in, conv_w=conv_w,
             conv_b=conv_b, dt_bias=dt_bias, a_log=a_log, d_skip=d_skip, ssd_norm_g=ssd_norm_g,
             gm_norm_g=gm_norm_g, gm_norm_b=gm_norm_b, w_spatial=w_spatial, b_spatial=b_spatial,
             b_gate=b_gate, w_ssd_proj=w_ssd_proj, w_gm_proj=w_gm_proj, w_out=w_out, ln1_g=ln1_g,
             ln1_b=ln1_b, w_ff1=w_ff1, w_ff3=w_ff3, w_ff2=w_ff2, ln2_g=ln2_g, ln2_b=ln2_b)
    M = dict(c_ctx=m_c_ctx, ln0_g=m_ln0_g, ln0_b=m_ln0_b, w_ada=m_w_ada, b_ada=m_b_ada, w_in=m_w_in,
             conv_w=m_conv_w, conv_b=m_conv_b, dt_bias=m_dt_bias, a_log=m_a_log, d_skip=m_d_skip,
             ssd_norm_g=m_ssd_norm_g, gm_norm_g=m_gm_norm_g, gm_norm_b=m_gm_norm_b,
             w_spatial=m_w_spatial, b_spatial=m_b_spatial, b_gate=m_b_gate, w_ssd_proj=m_w_ssd_proj,
             w_gm_proj=m_w_gm_proj, w_out=m_w_out, ln1_g=m_ln1_g, ln1_b=m_ln1_b, w_ff1=m_w_ff1,
             w_ff3=m_w_ff3, w_ff2=m_w_ff2, ln2_g=m_ln2_g, ln2_b=m_ln2_b)
    V = dict(c_ctx=v_c_ctx, ln0_g=v_ln0_g, ln0_b=v_ln0_b, w_ada=v_w_ada, b_ada=v_b_ada, w_in=v_w_in,
             conv_w=v_conv_w, conv_b=v_conv_b, dt_bias=v_dt_bias, a_log=v_a_log, d_skip=v_d_skip,
             ssd_norm_g=v_ssd_norm_g, gm_norm_g=v_gm_norm_g, gm_norm_b=v_gm_norm_b,
             w_spatial=v_w_spatial, b_spatial=v_b_spatial, b_gate=v_b_gate, w_ssd_proj=v_w_ssd_proj,
             w_gm_proj=v_w_gm_proj, w_out=v_w_out, ln1_g=v_ln1_g, ln1_b=v_ln1_b, w_ff1=v_w_ff1,
             w_ff3=v_w_ff3, w_ff2=v_w_ff2, ln2_g=v_ln2_g, ln2_b=v_ln2_b)

    me = 4 * lax.axis_index("x") + 2 * lax.axis_index("y") + lax.axis_index("c")
    xl, cx, tgt = x[0], ctx[0], loss_target[0]
    L = xl.shape[0]
    assert cx.shape[0] == TL and L % TL == 0
    ada_n = w_ada.shape[2]
    cw_n = conv_w.shape[2]

    small1 = _pad_rows(jnp.concatenate([c, _slab([conv_w[0]], 1)], axis=0), 8)
    g1 = _all_gather(small1, "ag_small")
    c_all = g1[:, 0, :]
    conv_w_full = g1[:, 1, :5 * cw_n].reshape(NDEV, 5, cw_n).transpose(1, 0, 2).reshape(5, NDEV * cw_n)
    sq = w_ssd_proj.shape[1]
    ffr = w_ff2.shape[1]
    ffc = w_ff1.shape[2]
    late = [jnp.concatenate([w_ssd_proj[0], w_gm_proj[0], w_out[0], w_ff2[0]], axis=0).astype(_MXU),
            _t(w_ff1[0]).astype(_MXU), _t(w_ff3[0]).astype(_MXU)]

    c16 = _pad_rows(jnp.concatenate([c_all, _row(c_ctx)], axis=0), 16)
    b_ada_sh = lax.dynamic_slice(b_ada, (0, ada_n * me), (1, ada_n))
    modp = _ada_fwd(c16, w_ada[0], b_ada_sh)
    mod16 = _all_gather(modp, "ag_mod").transpose(1, 0, 2).reshape(16, NDEV * ada_n)

    ga, = _all_gather_multi([w_in[0].astype(_MXU)], "ag_w_in")
    ga, late, mod16 = lax.optimization_barrier((ga, late, mod16))
    lw_send, lw_recv, lw_src, lw_land, lw_token = _exchange_start(late, "ag_late_start", gather=True)
    w_in_p = _perm_from_blocks(ga)
    modx = _pad_rows(lax.dynamic_slice(mod16, (me, 0), (1, 6 * D)).reshape(6, D), 8) + lw_token[0, 0]
    modc = _pad_rows(mod16[8].reshape(6, D), 8)

    g0, b0 = _row(ln0_g), _row(ln0_b)
    xn, h1 = _ln0_fwd(xl, cx, g0, b0, modx, modc)
    p = _mm(h1, w_in_p, "nn", F32, "mm_p")
    conv_w8 = _pad_rows(conv_w_full, 8)
    xbc = _conv_fwd(p, conv_w8, conv_b)
    prm = _pad_rows(jnp.pad(jnp.stack([dt_bias.reshape(32), a_log.reshape(32)]), ((0, 0), (0, 96))), 8)
    yf, yb, hpf, hpb = _ssd2_fwd(xbc, p, prm)
    lw_land = _exchange_wait(lw_send, lw_recv, lw_src, lw_land, yf, "ag_late_wait", gather=True)
    fw_send, fw_recv, lw_land, fw_token = _forward_start(lw_land, "ag_fwd_start")
    dsk = _row(jnp.repeat(d_skip[0, 0] + d_skip[0, 1], HP)) + fw_token[0:1, 0:1]
    ws_m = w_spatial[0].astype(_MXU)
    bsT = jnp.pad(b_spatial[0].T, ((0, 0), (0, 120)))
    mixp = (dsk, ssd_norm_g, gm_norm_g, gm_norm_b, ws_m, bsT)
    yssd, ygm = _mix_fwd(yf, yb, p, xbc, *mixp)
    gb, gc1, gc2 = _forward_wait(fw_send, fw_recv, lw_land, yssd, "ag_fwd_wait")

    gb, gc1, gc2 = [lax.dynamic_update_index_in_dim(g, mine, me, 0) for g, mine in zip((gb, gc1, gc2), late)]
    w_ssd_f = gb[:, 0:sq].reshape(NDEV * sq, D)
    w_gm_f = gb[:, sq:2 * sq].reshape(NDEV * sq, D)
    w_out_f = gb[:, 2 * sq:3 * sq].reshape(NDEV * sq, D)
    w_ff2_f = gb[:, 3 * sq:3 * sq + ffr].reshape(NDEV * ffr, D)
    assert HFF == (NDEV // 2) * ffc
    hd = NDEV // 2
    w13i = jnp.concatenate([g[t * hd:(t + 1) * hd].reshape(HFF, D) for t in range(2) for g in (gc1, gc2)],
                           axis=0)
    a1, a2, merged, out, r1, h2 = _merge_fwd(yssd, ygm, p, b_gate, w_ssd_f, w_gm_f, w_out_f,
                                             xn, modx, ln1_g, ln1_b)
    f13, ff = _mm_f13_glu(h2, w13i)

    dr2, do2, st2, loss_slab = _mm_o2_res2(ff, w_ff2_f, r1, tgt, modx, ln1_g, ln1_b, ln2_g, ln2_b)
    loss = lax.psum(loss_slab[0, 0], ("x", "y", "c"))
    df13 = _mm_dff_glu(do2, w_ff2_f, f13)
    dw_ff2 = _mm(ff, do2, "tn", _MXU, "mm_dw_ff2")
    dw13i = _mm(df13, h2, "tn", _MXU, "mm_dw13")

    def owner_blocks(first):
        return jnp.concatenate([dw13i[t * 2 * HFF + first:t * 2 * HFF + first + HFF].reshape(NDEV // 2, ffc, D)
                                for t in range(2)], axis=0)

    xff = [dw_ff2.reshape(NDEV, ffr, D), owner_blocks(0), owner_blocks(HFF)]
    ff_send, ff_recv, ff_src, ff_land, ff_token = _exchange_start(xff, "xchg_ff_start")
    modx = modx + ff_token[0, 0]
    dr1, dout, st1 = _mm_dh2_res1bwd(df13, w13i, dr2, r1, out, modx, ln1_g, ln1_b)
    dw_out = _mm(merged, dout, "tn", _MXU, "mm_dw_out")
    dp = jnp.zeros((L + TL, NPJ), _MXU)
    dp, da1, da2, stg, dys, dym = _merge_bwd(dout, a1, a2, p, b_gate, w_out_f, w_ssd_f, w_gm_f, dp)
    dw_ssd = _mm(yssd, da1, "tn", _MXU, "mm_dw_ssd")
    dw_gm = _mm(ygm, da2, "tn", _MXU, "mm_dw_gm")
    xsq = [jnp.concatenate([dw_ssd.reshape(NDEV, sq, D), dw_gm.reshape(NDEV, sq, D),
                            dw_out.reshape(NDEV, sq, D)], axis=1)]
    sq_send, sq_recv, sq_src, sq_land, sq_token = _exchange_start(xsq, "xchg_sq_start")
    mixp = (dsk + sq_token[0:1, 0:1],) + mixp[1:]
    dp, dyd, stm, dws, dbsT = _mix_bwd(dys, dym, yf, yb, p, xbc, dp, *mixp)
    dxf, dxb, ddf, ddb, sts = _ssd2_bwd(xbc, p, prm, dsk, dyd, hpf, hpb)
    dp, dcw, dcb = _conv_bwd(dxf, dxb, p, conv_w8, conv_b, dp)
    dp, std = _dt_bwd(ddf, ddb, dp)
    hw = D // 2
    xin_a = [_blocks_from_perm(_mm(h1[:, :hw], dp, "tn", _MXU, "mm_dw_in_a"), w_in.shape[2])]
    ina_send, ina_recv, ina_src, ina_land, ina_token = _exchange_start(xin_a, "xchg_in_a_start")
    h1b, ina_token = lax.optimization_barrier((h1[:, hw:], ina_token))
    xin_b = [_blocks_from_perm(_mm(h1b, dp, "tn", _MXU, "mm_dw_in_b"), w_in.shape[2])]
    inb_send, inb_recv, inb_src, inb_land, inb_token = _exchange_start(xin_b, "xchg_in_b_start")
    dp, inb_token = lax.optimization_barrier((dp, inb_token))
    dh1 = _mm(dp, w_in_p, "nt", F32, "mm_dh1")
    modx = modx + (ina_token[0, 0] + inb_token[0, 0])
    grad_x, st0 = _ln0_bwd(dh1, dr1, xl, cx, g0, b0, modx, modc)

    zero = jnp.zeros((D,), F32)
    dmod = jnp.stack([jnp.concatenate([st0[0], st0[1], st1[4], st1[1], st1[0], st2[2]]),
                      jnp.concatenate([st0[2], st0[3], zero, zero, zero, zero])])
    g16 = _all_gather(_pad_rows(dmod, 8), "ag_dmod")[:, 0:2, :].reshape(16, 6 * D)
    g16_sh = lax.dynamic_slice(g16, (0, ada_n * me), (16, ada_n))
    c16b = jnp.stack([c_all, jnp.broadcast_to(_row(c_ctx), (NDEV, D))], axis=1).reshape(16, D)
    dw_ada, db_ada8, dcc8 = _ada_bwd(c16b, g16, g16_sh, w_ada[0])

    part = dict(
        c_ctx=dcc8[0], ln0_g=st0[4], ln0_b=st0[5], conv_w=dcw[0:5], conv_b=dcb[0],
        dt_bias=std[0, 0:32], a_log=sts[0, 0:32], d_skip=jnp.tile(sts[1, 0:16], 2),
        ssd_norm_g=stm[0], gm_norm_g=stm[1], gm_norm_b=stm[2], w_spatial=dws,
        b_spatial=dbsT[:, 0:8].T, b_gate=stg[0], ln1_g=st1[2], ln1_b=st1[3], ln2_g=st2[0], ln2_b=st2[1])
    pnames = list(part)
    psum8 = _sum8(_all_gather(_slab([part[n] for n in pnames], SMALL_ROWS), "ag_smallgrads"), "sum_smallgrads")
    small = dict(zip(pnames, _unslab(psum8, [part[n].shape for n in pnames])))
    grads = {n: small[n].reshape(W[n].shape) for n in pnames if n != "conv_w"}
    grads["conv_w"] = lax.dynamic_slice(small["conv_w"], (0, cw_n * me), (5, cw_n)).reshape(conv_w.shape)
    grads["b_ada"] = db_ada8[0:1]
    grads["w_ada"] = dw_ada.reshape(w_ada.shape)

    delta, new_m, new_v = {}, {}, {}

    def adam_group(names, rows, tag, align=0):
        shapes = [W[n].shape for n in names]
        outs = _adamw(*[_slab([src[n] for n in names], rows, align) for src in (grads, W, M, V)], tag)
        for res, slab in zip((delta, new_m, new_v), outs):
            for n, a in zip(names, _unslab(slab, shapes, align)):
                res[n] = a

    adam_group(REPL + ["conv_w"], SMALL_ROWS, "adamw_small")
    res = _adamw(grads["w_ada"][0], w_ada[0], m_w_ada[0], v_w_ada[0], "adamw_w_ada")
    delta["w_ada"], new_m["w_ada"], new_v["w_ada"] = [a[None] for a in res]

    rff = _exchange_wait(ff_send, ff_recv, ff_src, ff_land, st0, "xchg_ff_wait")
    rsq = _exchange_wait(sq_send, sq_recv, sq_src, sq_land, rff[0], "xchg_sq_wait")
    rin_a = _exchange_wait(ina_send, ina_recv, ina_src, ina_land, delta["ln2_b"], "xchg_in_a_wait")
    rin_b = _exchange_wait(inb_send, inb_recv, inb_src, inb_land, rin_a[0], "xchg_in_b_wait")
    rin = jnp.concatenate([rin_a[0], rin_b[0]], axis=1)

    def own(blocks):
        return lax.dynamic_index_in_dim(blocks, me, 0, keepdims=False)

    own_in = jnp.concatenate([own(xin_a[0]), own(xin_b[0])], axis=0)
    for n, r8, mine, row0, tr in (
            ("w_ff2", rff[0], own(xff[0]), 0, ffr // 2), ("w_ssd_proj", rsq[0], own(xsq[0]), 0, sq),
            ("w_gm_proj", rsq[0], own(xsq[0]), sq, sq), ("w_out", rsq[0], own(xsq[0]), 2 * sq, sq),
            ("w_in", rin, own_in, 0, 256)):
        res = _adamw_sum(r8, mine, W[n][0], M[n][0], V[n][0], row0, tr, "adamw_" + n)
        grads[n], delta[n], new_m[n], new_v[n] = [a[None] for a in res]
    for n, r8, mine in (("w_ff1", rff[1], own(xff[1])), ("w_ff3", rff[2], own(xff[2]))):
        res = _adamw_sum(r8, mine, _t(W[n][0]), _t(M[n][0]), _t(V[n][0]), 0, ffc // 2, "adamw_" + n)
        grads[n], delta[n], new_m[n], new_v[n] = [_t(a)[None] for a in res]

    return (loss, grad_x[None], *[grads[n] for n in WEIGHTS], *[delta[n] for n in WEIGHTS],
            *[new_m[n] for n in WEIGHTS], *[new_v[n] for n in WEIGHTS])
```

```python
import functools

import jax
import jax.numpy as jnp
from jax import lax
from jax.experimental import pallas as pl
from jax.experimental.pallas import tpu as pltpu

_MXU = jnp.bfloat16
F32 = jnp.float32
D = 1024
TL = 256
Q = 128
NH, HP, NS, HPG = 16, 64, 128, 8
DFF = 2816
ALPHA = 2.0 ** 0.25
EPS = 1e-5
OZ, OU, OV, OXS, OG, OB, OC, ODT, NPJ = 0, 1024, 2048, 3072, 4096, 6144, 6400, 6656, 6912
NNAT = 6688
NDEV = 8
ADAM_LR, ADAM_B1, ADAM_B2, ADAM_EPS, ADAM_WD, ADAM_STEP = 1e-3, 0.9, 0.999, 1e-8, 0.01, 10
VMEM_LIMIT = 48 * 1024 * 1024

NN = ((1,), (0,))
NT = ((1,), (1,))
TN = ((0,), (0,))
MESH = pl.DeviceIdType.MESH


def _dot(a, b, dims):
    return lax.dot_general(a.astype(_MXU), b.astype(_MXU), (dims, ((), ())),
                           preferred_element_type=F32)


def _tile(n, cands):
    for c in cands:
        if n % c == 0:
            return c
    return n


def _divisor_tile(n, cap, mult):
    best = n
    for t in range(mult, min(n, cap) + 1, mult):
        if n % t == 0:
            best = t
    return best


def _params(sem):
    return pltpu.CompilerParams(dimension_semantics=sem, vmem_limit_bytes=VMEM_LIMIT)


def _cst(shape):
    nd = len(shape)
    return pl.BlockSpec(shape, lambda *_: (0,) * nd)


def _rt(w, cb=0, rows=TL):
    return pl.BlockSpec((rows, w), lambda i: (i, cb))


def _rtc(w, nt, cb=0):
    return pl.BlockSpec((TL, w), lambda i: (jnp.minimum(i, nt - 1), cb))


def _sig(x):
    return jax.nn.sigmoid(x)


def _softplus(x):
    return jnp.maximum(x, 0.0) + jnp.log1p(jnp.exp(-jnp.abs(x)))


_G0, _G1 = 0.7978845608028654, 0.044715


def _gelu(x):
    t = jnp.tanh(_G0 * (x + _G1 * x * x * x))
    return 0.5 * x * (1.0 + t), t


def _gelu_grad(x, t):
    return 0.5 * (1.0 + t) + 0.5 * x * (1.0 - t * t) * _G0 * (1.0 + 3.0 * _G1 * x * x)


def _ln(r):
    mu = jnp.mean(r, axis=-1, keepdims=True)
    xc = r - mu
    var = jnp.mean(xc * xc, axis=-1, keepdims=True)
    rstd = lax.rsqrt(var + EPS)
    return xc * rstd, rstd


def _ln_bwd(dyh, xhat, rstd):
    return rstd * (dyh - jnp.mean(dyh, axis=-1, keepdims=True)
                   - xhat * jnp.mean(dyh * xhat, axis=-1, keepdims=True))


def _colsum(v):
    return jnp.sum(v, axis=0, keepdims=True)


def _sum11(v):
    return jnp.sum(jnp.sum(v, axis=1, keepdims=True), axis=0, keepdims=True)


def _cumsum_rows(a, rev):
    n = a.shape[0]
    row = lax.broadcasted_iota(jnp.int32, a.shape, 0)
    s = 1
    while s < n:
        if rev:
            a = a + jnp.where(row < n - s, pltpu.roll(a, n - s, 0), 0.0)
        else:
            a = a + jnp.where(row >= s, pltpu.roll(a, s, 0), 0.0)
        s *= 2
    return a


def _mm(a, b, mode, out_dtype, name):
    if mode == "tn":
        K, M = a.shape
    else:
        M, K = a.shape
    N = b.shape[0] if mode == "nt" else b.shape[1]
    tm = _divisor_tile(M, 1408, 128) if mode == "tn" else _divisor_tile(M, 1088, 16)
    tn = _divisor_tile(N, 1408, 128)
    tk = _divisor_tile(K, 2304, 128)
    nk = K // tk
    dims = {"nn": NN, "nt": NT, "tn": TN}[mode]
    use_acc = nk > 1 and out_dtype != F32

    def body(a_ref, b_ref, o_ref, *acc):
        prod = _dot(a_ref[...], b_ref[...], dims)
        if nk == 1:
            o_ref[...] = prod.astype(o_ref.dtype)
            return
        acc_ref = acc[0] if use_acc else o_ref
        k = pl.program_id(2)

        @pl.when(k == 0)
        def _():
            acc_ref[...] = prod

        if use_acc:
            @pl.when((k > 0) & (k < nk - 1))
            def _():
                acc_ref[...] += prod

            @pl.when(k == nk - 1)
            def _():
                o_ref[...] = (acc_ref[...] + prod).astype(o_ref.dtype)
        else:
            @pl.when(k > 0)
            def _():
                o_ref[...] += prod

    if mode == "tn":
        a_spec = pl.BlockSpec((tk, tm), lambda i, j, k: (k, i))
    else:
        a_spec = pl.BlockSpec((tm, tk), lambda i, j, k: (i, k))
    if mode == "nt":
        b_spec = pl.BlockSpec((tn, tk), lambda i, j, k: (j, k))
    else:
        b_spec = pl.BlockSpec((tk, tn), lambda i, j, k: (k, j))
    return pl.pallas_call(
        body, name=name, grid=(M // tm, N // tn, nk),
        in_specs=[a_spec, b_spec],
        out_specs=pl.BlockSpec((tm, tn), lambda i, j, k: (i, j)),
        out_shape=jax.ShapeDtypeStruct((M, N), out_dtype),
        scratch_shapes=[pltpu.VMEM((tm, tn), F32)] if use_acc else [],
        compiler_params=_params(("parallel", "parallel", "arbitrary")),
    )(a, b)


def _all_gather(x, name):
    def body(x_ref, out_ref, send_sems, recv_sems, local_sem):
        mx, my, mc = lax.axis_index("x"), lax.axis_index("y"), lax.axis_index("c")
        me, sibling = (mx, my, mc), (mx, my, 1 - mc)
        chips = [(1 - mx, my), (mx, 1 - my), (1 - mx, 1 - my)]

        def slot(px, py, pc):
            return out_ref.at[4 * px + 2 * py + pc]

        def copy(k, block, to, src=None):
            return pltpu.make_async_remote_copy(
                src_ref=slot(*block) if src is None else src, dst_ref=slot(*block),
                send_sem=send_sems.at[k], recv_sem=recv_sems.at[k],
                device_id=to, device_id_type=MESH)

        mine = pltpu.make_async_copy(x_ref, slot(*me), local_sem)
        mine.start()
        first = [copy(0, me, sibling, src=x_ref)]
        first += [copy(1 + j, me, (*chip, mc), src=x_ref) for j, chip in enumerate(chips)]
        for cp in first:
            cp.start()
        passed = [copy(4 + j, (*chip, mc), sibling) for j, chip in enumerate(chips)]
        for j, chip in enumerate(chips):
            copy(1 + j, (*chip, mc), me).wait_recv()
            passed[j].start()
        copy(0, sibling, me).wait_recv()
        for j, chip in enumerate(chips):
            copy(4 + j, (*chip, 1 - mc), me).wait_recv()
        for cp in first + passed:
            cp.wait_send()
        mine.wait()

    return pl.pallas_call(
        body, name=name,
        out_shape=jax.ShapeDtypeStruct((NDEV,) + x.shape, x.dtype),
        in_specs=[pl.BlockSpec(memory_space=pl.ANY)],
        out_specs=pl.BlockSpec(memory_space=pl.ANY),
        scratch_shapes=[pltpu.SemaphoreType.DMA((7,)), pltpu.SemaphoreType.DMA((7,)),
                        pltpu.SemaphoreType.DMA],
    )(x)


def _owner_exchange(g, name):
    def body(g_ref, out_ref, send_sems, recv_sems, local_sem):
        mx, my, mc = lax.axis_index("x"), lax.axis_index("y"), lax.axis_index("c")
        local = pltpu.make_async_copy(g_ref.at[4 * mx + 2 * my + mc], out_ref.at[0], local_sem)
        local.start()
        copies = []
        for f in range(1, NDEV):
            px = 1 - mx if (f >> 2) & 1 else mx
            py = 1 - my if (f >> 1) & 1 else my
            pc = 1 - mc if f & 1 else mc
            cp = pltpu.make_async_remote_copy(
                src_ref=g_ref.at[4 * px + 2 * py + pc], dst_ref=out_ref.at[f],
                send_sem=send_sems.at[f - 1], recv_sem=recv_sems.at[f - 1],
                device_id=(px, py, pc), device_id_type=MESH)
            cp.start()
            copies.append(cp)
        for cp in copies:
            cp.wait_recv()
        for cp in copies:
            cp.wait_send()
        local.wait()

    return pl.pallas_call(
        body, name=name,
        out_shape=jax.ShapeDtypeStruct(g.shape, g.dtype),
        in_specs=[pl.BlockSpec(memory_space=pl.ANY)],
        out_specs=pl.BlockSpec(memory_space=pl.ANY),
        scratch_shapes=[pltpu.SemaphoreType.DMA((7,)), pltpu.SemaphoreType.DMA((7,)),
                        pltpu.SemaphoreType.DMA],
    )(g)


def _any_specs(n):
    return [pl.BlockSpec(memory_space=pl.ANY)] * n


def _all_gather_multi(xs, name):
    na = len(xs)

    def body(*refs):
        x_refs, out_refs = refs[:na], refs[na:2 * na]
        send_sems, recv_sems, local_sems = refs[2 * na:]
        mx, my, mc = lax.axis_index("x"), lax.axis_index("y"), lax.axis_index("c")
        me, sibling = (mx, my, mc), (mx, my, 1 - mc)
        chips = [(1 - mx, my), (mx, 1 - my), (1 - mx, 1 - my)]

        def copy(a, k, block, to, src=None):
            slot = out_refs[a].at[4 * block[0] + 2 * block[1] + block[2]]
            return pltpu.make_async_remote_copy(
                src_ref=slot if src is None else src, dst_ref=slot,
                send_sem=send_sems.at[7 * a + k], recv_sem=recv_sems.at[7 * a + k],
                device_id=to, device_id_type=MESH)

        mine = [pltpu.make_async_copy(x_refs[a], out_refs[a].at[4 * mx + 2 * my + mc], local_sems.at[a])
                for a in range(na)]
        for cp in mine:
            cp.start()
        first = []
        for a in range(na):
            first.append(copy(a, 0, me, sibling, src=x_refs[a]))
            first += [copy(a, 1 + j, me, (*chip, mc), src=x_refs[a]) for j, chip in enumerate(chips)]
        for cp in first:
            cp.start()
        passed = []
        for a in range(na):
            for j, chip in enumerate(chips):
                copy(a, 1 + j, (*chip, mc), me).wait_recv()
                fwd = copy(a, 4 + j, (*chip, mc), sibling)
                fwd.start()
                passed.append(fwd)
        for a in range(na):
            copy(a, 0, sibling, me).wait_recv()
            for j, chip in enumerate(chips):
                copy(a, 4 + j, (*chip, 1 - mc), me).wait_recv()
        for cp in first + passed:
            cp.wait_send()
        for cp in mine:
            cp.wait()

    return pl.pallas_call(
        body, name=name,
        out_shape=[jax.ShapeDtypeStruct((NDEV,) + x.shape, x.dtype) for x in xs],
        in_specs=_any_specs(na), out_specs=_any_specs(na),
        scratch_shapes=[pltpu.SemaphoreType.DMA((7 * na,)), pltpu.SemaphoreType.DMA((7 * na,)),
                        pltpu.SemaphoreType.DMA((na,))],
    )(*xs)


def _owner_exchange_multi(gs, name):
    na = len(gs)

    def body(*refs):
        g_refs, out_refs = refs[:na], refs[na:2 * na]
        send_sems, recv_sems, local_sems = refs[2 * na:]
        mx, my, mc = lax.axis_index("x"), lax.axis_index("y"), lax.axis_index("c")
        locals_ = [pltpu.make_async_copy(g_refs[a].at[4 * mx + 2 * my + mc], out_refs[a].at[0], local_sems.at[a])
                   for a in range(na)]
        for cp in locals_:
            cp.start()
        copies = []
        for a in range(na):
            for f in range(1, NDEV):
                px = 1 - mx if (f >> 2) & 1 else mx
                py = 1 - my if (f >> 1) & 1 else my
                pc = 1 - mc if f & 1 else mc
                cp = pltpu.make_async_remote_copy(
                    src_ref=g_refs[a].at[4 * px + 2 * py + pc], dst_ref=out_refs[a].at[f],
                    send_sem=send_sems.at[7 * a + f - 1], recv_sem=recv_sems.at[7 * a + f - 1],
                    device_id=(px, py, pc), device_id_type=MESH)
                cp.start()
                copies.append(cp)
        for cp in copies:
            cp.wait_recv()
        for cp in copies:
            cp.wait_send()
        for cp in locals_:
            cp.wait()

    return pl.pallas_call(
        body, name=name,
        out_shape=[jax.ShapeDtypeStruct(g.shape, g.dtype) for g in gs],
        in_specs=_any_specs(na), out_specs=_any_specs(na),
        scratch_shapes=[pltpu.SemaphoreType.DMA((7 * na,)), pltpu.SemaphoreType.DMA((7 * na,)),
                        pltpu.SemaphoreType.DMA((na,))],
    )(*gs)


def _adamw_sum(r8, own, w, m, v, row0, tr, name):
    R, C = w.shape
    assert row0 % tr == 0
    blk0 = row0 // tr
    bc1 = 1.0 - ADAM_B1 ** ADAM_STEP
    bc2 = 1.0 - ADAM_B2 ** ADAM_STEP

    def body(r_ref, *refs):
        if own is None:
            gg = r_ref[0].astype(F32)
        else:
            gg = refs[0][...].astype(F32)
            refs = refs[1:]
        w_ref, m_ref, v_ref, g_ref, d_ref, mo_ref, vo_ref = refs
        for k in range(1, NDEV):
            gg = gg + r_ref[k].astype(F32)
        mn = ADAM_B1 * m_ref[...] + (1.0 - ADAM_B1) * gg
        vn = ADAM_B2 * v_ref[...] + (1.0 - ADAM_B2) * (gg * gg)
        mh = mn / bc1
        vh = vn / bc2
        g_ref[...] = gg
        d_ref[...] = -ADAM_LR * (mh / (jnp.sqrt(vh) + ADAM_EPS) + ADAM_WD * w_ref[...])
        mo_ref[...] = mn
        vo_ref[...] = vn

    spec = pl.BlockSpec((tr, C), lambda i: (i, 0))
    sh = jax.ShapeDtypeStruct((R, C), F32)
    own_ops = [] if own is None else [own]
    own_specs = [] if own is None else [pl.BlockSpec((tr, C), lambda i: (i + blk0, 0))]
    return pl.pallas_call(
        body, name=name, grid=(R // tr,),
        in_specs=[pl.BlockSpec((NDEV, tr, C), lambda i: (0, i + blk0, 0))] + own_specs + [spec, spec, spec],
        out_specs=[spec] * 4, out_shape=[sh] * 4, compiler_params=_params(("parallel",)),
    )(r8, *own_ops, w, m, v)


_HBM = pl.BlockSpec(memory_space=pltpu.HBM)
_SEM = pl.BlockSpec(memory_space=pltpu.SEMAPHORE)
_EFFECT = pltpu.SideEffectType.DATAFLOW_SIDE_EFFECTING


def _exchange_copies(g_refs, land_refs, send_sems, recv_sems, gather):
    mx, my, mc = lax.axis_index("x"), lax.axis_index("y"), lax.axis_index("c")
    copies = []
    for a in range(len(g_refs)):
        for f in ((1, 2, 4, 6) if gather else range(1, NDEV)):
            px = 1 - mx if (f >> 2) & 1 else mx
            py = 1 - my if (f >> 1) & 1 else my
            pc = 1 - mc if f & 1 else mc
            src = g_refs[a] if gather else g_refs[a].at[4 * px + 2 * py + pc]
            dst = land_refs[a].at[4 * mx + 2 * my + mc] if gather else land_refs[a].at[f]
            copies.append(pltpu.make_async_remote_copy(
                src_ref=src, dst_ref=dst,
                send_sem=send_sems.at[7 * a + f - 1], recv_sem=recv_sems.at[7 * a + f - 1],
                device_id=(px, py, pc), device_id_type=MESH))
    return copies


def _exchange_start(gs, name, gather=False):
    na = len(gs)

    def body(*refs):
        for cp in _exchange_copies(refs[:na], refs[na:2 * na], refs[2 * na], refs[2 * na + 1], gather):
            cp.start()
        refs[-1][...] = jnp.zeros_like(refs[-1])

    hbm = [pltpu.HBM(g.shape, g.dtype) for g in gs]
    land_shapes = [((NDEV,) + g.shape) if gather else g.shape for g in gs]
    lands = [pltpu.with_memory_space_constraint(lax.empty(shp, g.dtype), pltpu.HBM)
             for shp, g in zip(land_shapes, gs)]
    hbm_land = [pltpu.HBM(shp, g.dtype) for shp, g in zip(land_shapes, gs)]
    outs = pl.pallas_call(
        body, name=name,
        out_shape=(pltpu.SemaphoreType.DMA((7 * na,)), pltpu.SemaphoreType.DMA((7 * na,)), *hbm, *hbm_land,
                   jax.ShapeDtypeStruct((8, 128), F32)),
        in_specs=[_HBM] * (2 * na),
        out_specs=(_SEM, _SEM, *([_HBM] * (2 * na)), pl.BlockSpec(memory_space=pltpu.VMEM)),
        input_output_aliases={i: 2 + i for i in range(2 * na)},
        compiler_params=pltpu.CompilerParams(has_side_effects=_EFFECT),
    )(*[pltpu.with_memory_space_constraint(g, pltpu.HBM) for g in gs], *lands)
    return outs[0], outs[1], outs[2:2 + na], outs[2 + na:2 + 2 * na], outs[-1]


def _forward_copies(land_refs, send_sems, recv_sems):
    mx, my, mc = lax.axis_index("x"), lax.axis_index("y"), lax.axis_index("c")
    copies = []
    for a in range(len(land_refs)):
        for j, (fx, fy) in enumerate(((0, 1), (1, 0), (1, 1))):
            px = 1 - mx if fx else mx
            py = 1 - my if fy else my
            blk = land_refs[a].at[4 * px + 2 * py + mc]
            copies.append(pltpu.make_async_remote_copy(
                src_ref=blk, dst_ref=blk, send_sem=send_sems.at[3 * a + j], recv_sem=recv_sems.at[3 * a + j],
                device_id=(mx, my, 1 - mc), device_id_type=MESH))
    return copies


def _forward_start(lands, name):
    na = len(lands)

    def body(*refs):
        for cp in _forward_copies(refs[:na], refs[na], refs[na + 1]):
            cp.start()
        refs[-1][...] = jnp.zeros_like(refs[-1])

    outs = pl.pallas_call(
        body, name=name,
        out_shape=(pltpu.SemaphoreType.DMA((3 * na,)), pltpu.SemaphoreType.DMA((3 * na,)),
                   *[pltpu.HBM(g.shape, g.dtype) for g in lands], jax.ShapeDtypeStruct((8, 128), F32)),
        in_specs=[_HBM] * na,
        out_specs=(_SEM, _SEM, *([_HBM] * na), pl.BlockSpec(memory_space=pltpu.VMEM)),
        input_output_aliases={i: 2 + i for i in range(na)},
        compiler_params=pltpu.CompilerParams(has_side_effects=_EFFECT),
    )(*lands)
    return outs[0], outs[1], outs[2:2 + na], outs[-1]


def _forward_wait(send_sems, recv_sems, lands, after, name):
    na = len(lands)

    def body(*refs):
        for cp in _forward_copies(refs[:na], refs[na], refs[na + 1]):
            cp.wait_send()
            cp.wait_recv()

    return pl.pallas_call(
        body, name=name,
        out_shape=tuple(pltpu.HBM(g.shape, g.dtype) for g in lands),
        in_specs=[_HBM] * na + [_SEM, _SEM, pl.BlockSpec(memory_space=pl.ANY)],
        out_specs=tuple([_HBM] * na),
        input_output_aliases={i: i for i in range(na)},
        compiler_params=pltpu.CompilerParams(has_side_effects=_EFFECT),
    )(*lands, send_sems, recv_sems, after)


def _exchange_wait(send_sems, recv_sems, g_thru, land_thru, after, name, gather=False):
    na = len(g_thru)

    def body(*refs):
        for cp in _exchange_copies(refs[:na], refs[na:2 * na], refs[2 * na], refs[2 * na + 1], gather):
            cp.wait_send()
            cp.wait_recv()

    outs = pl.pallas_call(
        body, name=name,
        out_shape=tuple(pltpu.HBM(g.shape, g.dtype) for g in list(g_thru) + list(land_thru)),
        in_specs=[_HBM] * (2 * na) + [_SEM, _SEM, pl.BlockSpec(memory_space=pl.ANY)],
        out_specs=tuple([_HBM] * (2 * na)),
        input_output_aliases={i: i for i in range(2 * na)},
        compiler_params=pltpu.CompilerParams(has_side_effects=_EFFECT),
    )(*g_thru, *land_thru, send_sems, recv_sems, after)
    return outs[na:]


def _sum8(r, name):
    _, R, C = r.shape
    tr = _tile(R, (256, 160, 128, 64, 32, 16, 8))

    def body(r_ref, o_ref):
        acc = r_ref[0].astype(F32)
        for k in range(1, NDEV):
            acc = acc + r_ref[k].astype(F32)
        o_ref[...] = acc

    return pl.pallas_call(
        body, name=name, grid=(R // tr,),
        in_specs=[pl.BlockSpec((NDEV, tr, C), lambda i: (0, i, 0))],
        out_specs=pl.BlockSpec((tr, C), lambda i: (i, 0)),
        out_shape=jax.ShapeDtypeStruct((R, C), F32),
        compiler_params=_params(("parallel",)),
    )(r)


def _adamw(g, w, m, v, name):
    R, C = g.shape
    tr = _tile(R, (256, 160, 128, 64, 32, 16, 8))
    bc1 = 1.0 - ADAM_B1 ** ADAM_STEP
    bc2 = 1.0 - ADAM_B2 ** ADAM_STEP

    def body(g_ref, w_ref, m_ref, v_ref, d_ref, mo_ref, vo_ref):
        gg = g_ref[...]
        mn = ADAM_B1 * m_ref[...] + (1.0 - ADAM_B1) * gg
        vn = ADAM_B2 * v_ref[...] + (1.0 - ADAM_B2) * (gg * gg)
        mh = mn / bc1
        vh = vn / bc2
        d_ref[...] = -ADAM_LR * (mh / (jnp.sqrt(vh) + ADAM_EPS) + ADAM_WD * w_ref[...])
        mo_ref[...] = mn
        vo_ref[...] = vn

    spec = pl.BlockSpec((tr, C), lambda i: (i, 0))
    sh = jax.ShapeDtypeStruct((R, C), F32)
    return pl.pallas_call(
        body, name=name, grid=(R // tr,), in_specs=[spec] * 4, out_specs=[spec] * 3,
        out_shape=[sh] * 3, compiler_params=_params(("parallel",)),
    )(g, w, m, v)


def _ada_fwd(c16, w_sh, b_sh):
    def body(c_ref, w_ref, b_ref, o_ref):
        c = c_ref[...]
        o_ref[...] = _dot(c * _sig(c), w_ref[...], NN) + b_ref[...]

    return pl.pallas_call(
        body, name="ada_fwd", out_shape=jax.ShapeDtypeStruct((16, w_sh.shape[1]), F32),
        compiler_params=pltpu.CompilerParams(vmem_limit_bytes=VMEM_LIMIT),
    )(c16, w_sh, b_sh)


def _ada_bwd(c16, g16, g16_sh, w_sh):
    ncol = w_sh.shape[1]

    def body(c_ref, g_ref, gs_ref, w_ref, dw_ref, db_ref, dc_ref):
        c = c_ref[...]
        s = _sig(c)
        gs = gs_ref[...]
        dw_ref[...] = _dot(c * s, gs, TN)
        db_ref[...] = jnp.broadcast_to(_colsum(g_ref[...]), db_ref.shape)
        odd = lax.broadcasted_iota(jnp.int32, gs.shape, 0) % 2 == 1
        gc = _colsum(jnp.where(odd, gs, 0.0))
        ds = _dot(jnp.broadcast_to(gc, (8, ncol)), w_ref[...], NT)
        c1 = c[1:2, :]
        s1 = s[1:2, :]
        dc_ref[...] = ds * (s1 * (1.0 + c1 * (1.0 - s1)))

    return pl.pallas_call(
        body, name="ada_bwd",
        out_shape=[jax.ShapeDtypeStruct(w_sh.shape, F32),
                   jax.ShapeDtypeStruct((8, g16.shape[1]), F32),
                   jax.ShapeDtypeStruct((8, D), F32)],
        compiler_params=pltpu.CompilerParams(vmem_limit_bytes=VMEM_LIMIT),
    )(c16, g16, g16_sh, w_sh)


def _ln0_fwd(x, ctx, g, b, modx, modc):
    L = x.shape[0]
    nt = L // TL

    def body(x_ref, c_ref, g_ref, b_ref, mx_ref, mc_ref, xn_ref, h_ref):
        isc = pl.program_id(0) == nt
        xin = jnp.where(isc, c_ref[...], x_ref[...])
        sh = jnp.where(isc, mc_ref[0:1, :], mx_ref[0:1, :])
        sc = jnp.where(isc, mc_ref[1:2, :], mx_ref[1:2, :])
        xhat, _ = _ln(xin)
        xn = xhat * g_ref[...] + b_ref[...]
        xn_ref[...] = xn
        h_ref[...] = (xn * (1.0 + sc) + sh).astype(h_ref.dtype)

    return pl.pallas_call(
        body, name="ln0_fwd", grid=(nt + 1,),
        in_specs=[_rtc(D, nt), _cst((TL, D)), _cst((1, D)), _cst((1, D)), _cst((8, D)), _cst((8, D))],
        out_specs=[_rt(D), _rt(D)],
        out_shape=[jax.ShapeDtypeStruct((L + TL, D), F32), jax.ShapeDtypeStruct((L + TL, D), _MXU)],
        compiler_params=_params(("parallel",)),
    )(x, ctx, g, b, modx, modc)


def _xbc_colblk(j):
    return jnp.where(j < 8, OXS // 128 + j, OB // 128 + j - 8)


def _conv_taps(p_ref, r0, first, last):
    main = p_ref[pl.ds(r0, TL), :]
    zero = jnp.zeros((8, main.shape[1]), F32)
    prev = zero if first else p_ref[pl.ds(r0 - 8, 8), :]
    nxt = zero if last else p_ref[pl.ds(r0 + TL, 8), :]
    ext = jnp.concatenate([prev, main, nxt], axis=0)
    n = TL + 16
    return [pltpu.roll(ext, (2 - k) % n, 0)[8:8 + TL] for k in range(5)]


def _seq_chunks(L):
    nt = L // TL
    return [(r * TL, r == 0, r == nt - 1) for r in range(nt)] + [(L, True, True)]


def _conv_fwd(p, conv_w8, conv_b):
    RT = p.shape[0]
    L = RT - TL
    chunks = _seq_chunks(L)

    def body(p_ref, w_ref, b_ref, o_ref):
        w = w_ref[...]
        bias = b_ref[...]
        for r0, first, last in chunks:
            taps = _conv_taps(p_ref, r0, first, last)
            pre = bias + sum(w[k:k + 1, :] * taps[k] for k in range(5))
            o_ref[pl.ds(r0, TL), :] = pre * _sig(pre)

    return pl.pallas_call(
        body, name="conv_fwd", grid=(12,),
        in_specs=[pl.BlockSpec((RT, 128), lambda j: (0, _xbc_colblk(j))),
                  pl.BlockSpec((8, 128), lambda j: (0, j)),
                  pl.BlockSpec((1, 128), lambda j: (0, j))],
        out_specs=pl.BlockSpec((RT, 128), lambda j: (0, j)),
        out_shape=jax.ShapeDtypeStruct((RT, 1536), F32),
        compiler_params=_params(("parallel",)),
    )(p, conv_w8, conv_b)


def _ssd_common(dtraw, dtb, a32, rev):
    dt = _softplus(dtraw + dtb)
    acum = _cumsum_rows(dt * a32, rev)
    ii = lax.broadcasted_iota(jnp.int32, (Q, Q), 0)
    jj = lax.broadcasted_iota(jnp.int32, (Q, Q), 1)
    mask = (ii <= jj) if rev else (ii >= jj)
    return dt, acum, acum.T, dt.T, mask


def _ssd_orders(ncl, ncc):
    nc = ncl + ncc

    def cf(s):
        return jnp.where(s < ncc, ncl + s, s - ncc)

    def cb(s):
        return nc - 1 - s

    return cf, cb


def _ssd_fwd(xbc, p, prm):
    RT = xbc.shape[0]
    nc = RT // Q
    ncc = TL // Q
    cf, cb = _ssd_orders(nc - ncc, ncc)

    def one_dir(x_ref, dt_ref, prm_ref, y_ref, hp_ref, H_ref, d):
        rev = d == 1
        a32 = -jnp.exp(prm_ref[1:2, :])
        dt, acum, acumT, dtT, mask = _ssd_common(dt_ref[...], prm_ref[0:1, :], a32, rev)
        end = 0 if rev else Q - 1
        for g in range(2):
            Bg = x_ref[:, D + g * NS:D + (g + 1) * NS]
            Cg = x_ref[:, D + 2 * NS + g * NS:D + 2 * NS + (g + 1) * NS]
            CB = _dot(Cg, Bg, NT)
            for hh in range(HPG):
                h = g * HPG + hh
                ln = 16 * d + h
                col = acum[:, ln:ln + 1]
                rowv = acumT[ln:ln + 1, :]
                a_end = rowv[:, end:end + 1]
                Lm = jnp.exp(jnp.where(mask, col - rowv, -1e30))
                W = CB * Lm * dtT[ln:ln + 1, :]
                Xh = x_ref[:, h * HP:(h + 1) * HP]
                Hp = H_ref[h * HP:(h + 1) * HP, :]
                y = _dot(W, Xh, NN) + jnp.exp(col) * _dot(Cg, Hp, NT)
                y_ref[:, h * HP:(h + 1) * HP] = y
                dcol = jnp.exp(a_end - col) * dt[:, ln:ln + 1]
                hp_ref[0, h * HP:(h + 1) * HP, :] = Hp
                H_ref[h * HP:(h + 1) * HP, :] = jnp.exp(a_end) * Hp + _dot(Xh * dcol, Bg, TN)

    def body(xf_ref, xb_ref, df_ref, db_ref, prm_ref, yf_ref, yb_ref, hf_ref, hb_ref, Hf, Hb):
        @pl.when(pl.program_id(0) == 0)
        def _():
            Hf[...] = jnp.zeros_like(Hf)
            Hb[...] = jnp.zeros_like(Hb)

        one_dir(xf_ref, df_ref, prm_ref, yf_ref, hf_ref, Hf, 0)
        one_dir(xb_ref, db_ref, prm_ref, yb_ref, hb_ref, Hb, 1)

    ysh = jax.ShapeDtypeStruct((RT, D), F32)
    hsh = jax.ShapeDtypeStruct((nc, NH * HP, NS), F32)
    hspec = pl.BlockSpec((1, NH * HP, NS), lambda s: (s, 0, 0))
    return pl.pallas_call(
        body, name="ssd_fwd", grid=(nc,),
        in_specs=[pl.BlockSpec((Q, 1536), lambda s: (cf(s), 0)),
                  pl.BlockSpec((Q, 1536), lambda s: (cb(s), 0)),
                  pl.BlockSpec((Q, 128), lambda s: (cf(s), ODT // 128)),
                  pl.BlockSpec((Q, 128), lambda s: (cb(s), ODT // 128)),
                  _cst((8, 128))],
        out_specs=[pl.BlockSpec((Q, D), lambda s: (cf(s), 0)),
                   pl.BlockSpec((Q, D), lambda s: (cb(s), 0)), hspec, hspec],
        out_shape=[ysh, ysh, hsh, hsh],
        scratch_shapes=[pltpu.VMEM((NH * HP, NS), F32), pltpu.VMEM((NH * HP, NS), F32)],
        compiler_params=_params(("arbitrary",)),
    )(xbc, xbc, p, p, prm)


def _ssd_bwd(xbc, p, prm, dsk, dyd, hpf, hpb):
    RT = xbc.shape[0]
    nc = RT // Q
    ncc = TL // Q
    ncl = nc - ncc
    cf, cb = _ssd_orders(ncl, ncc)

    def rs(t):
        return nc - 1 - t

    def one_dir(x_ref, dt_ref, prm_ref, dsk_ref, dy_ref, is_ctx, hp_ref, dH_ref,
                dx_ref, ddt_ref, st_ref, d):
        rev = d == 1
        a32 = -jnp.exp(prm_ref[1:2, :])
        dtraw = dt_ref[...]
        dtb = prm_ref[0:1, :]
        dt, acum, acumT, dtT, mask = _ssd_common(dtraw, dtb, a32, rev)
        end = 0 if rev else Q - 1
        lane = lax.broadcasted_iota(jnp.int32, (Q, 128), 1)
        srow = lax.broadcasted_iota(jnp.int32, (Q, 128), 0)
        dyscale = jnp.where(is_ctx, 0.0, 1.0)
        c_dacum = jnp.zeros((Q, 128), F32)
        r_dacum = jnp.zeros((Q, 128), F32)
        c_ddt = jnp.zeros((Q, 128), F32)
        r_ddt = jnp.zeros((Q, 128), F32)
        dskacc = jnp.zeros((1, 128), F32)
        for g in range(2):
            Bg = x_ref[:, D + g * NS:D + (g + 1) * NS]
            Cg = x_ref[:, D + 2 * NS + g * NS:D + 2 * NS + (g + 1) * NS]
            CB = _dot(Cg, Bg, NT)
            dCB = jnp.zeros((Q, Q), F32)
            dBg = jnp.zeros((Q, NS), F32)
            dCg = jnp.zeros((Q, NS), F32)
            for hh in range(HPG):
                h = g * HPG + hh
                ln = 16 * d + h
                hs = slice(h * HP, (h + 1) * HP)
                col = acum[:, ln:ln + 1]
                rowv = acumT[ln:ln + 1, :]
                dtr = dtT[ln:ln + 1, :]
                dtc = dt[:, ln:ln + 1]
                a_end = rowv[:, end:end + 1]
                Lm = jnp.exp(jnp.where(mask, col - rowv, -1e30))
                E = jnp.exp(col)
                ecol = jnp.exp(a_end - col)
                dcol = ecol * dtc
                Xh = x_ref[:, hs]
                dY = dy_ref[:, hs] * dyscale
                Hp = hp_ref[0, hs, :]
                dHn = dH_ref[hs, :]
                W = CB * Lm * dtr
                dW = _dot(dY, Xh, NT)
                Mm = dW * CB * Lm
                T = Mm * dtr
                dCB = dCB + dW * Lm * dtr
                BdH = _dot(Bg, dHn, NT)
                dX = _dot(W, dY, TN) + dcol * BdH
                if d == 0:
                    dX = dX + dY * dsk_ref[:, hs]
                    dskacc = dskacc + jnp.where(lane[0:1, :] == h, _sum11(dY * Xh), 0.0)
                dx_ref[:, hs] = dX
                xb = jnp.sum(Xh * BdH, axis=1, keepdims=True)
                scol = dcol * xb
                G = _dot(dY, Hp, NN)
                dCg = dCg + E * G
                qcol = E * jnp.sum(G * Cg, axis=1, keepdims=True)
                dBg = dBg + _dot(Xh * dcol, dHn, NN)
                dH_ref[hs, :] = jnp.exp(a_end) * dHn + _dot(dY * E, Cg, TN)
                eterm = jnp.exp(a_end) * _sum11(dHn * Hp) + _sum11(scol)
                cvec = jnp.sum(T, axis=1, keepdims=True) + qcol - scol
                cvec = cvec + jnp.where(srow[:, 0:1] == end, eterm, 0.0)
                c_dacum = c_dacum + jnp.where(lane == ln, cvec, 0.0)
                r_dacum = r_dacum - jnp.where(srow == ln, _colsum(T), 0.0)
                c_ddt = c_ddt + jnp.where(lane == ln, ecol * xb, 0.0)
                r_ddt = r_ddt + jnp.where(srow == ln, _colsum(Mm), 0.0)
            dBg = dBg + _dot(dCB, Cg, TN)
            dCg = dCg + _dot(dCB, Bg, NN)
            dx_ref[:, D + g * NS:D + (g + 1) * NS] = dBg
            dx_ref[:, D + 2 * NS + g * NS:D + 2 * NS + (g + 1) * NS] = dCg
        dacum = c_dacum + r_dacum.T
        da = _cumsum_rows(dacum, not rev)
        mine = (lane >= 16 * d) & (lane < 16 * d + 16)
        ddt = jnp.where(mine, c_ddt + r_ddt.T + da * a32, 0.0)
        ddt_ref[...] = ddt * _sig(dtraw + dtb)
        st_ref[0:1, :] += _colsum(jnp.where(mine, da * dt, 0.0))
        if d == 0:
            st_ref[1:2, :] += dskacc

    def body(xf_ref, xb_ref, df_ref, db_ref, prm_ref, dsk_ref, dyf_ref, dyb_ref, hf_ref, hb_ref,
             dxf_ref, dxb_ref, ddf_ref, ddb_ref, st_ref, dHf, dHb):
        t = pl.program_id(0)

        @pl.when(t == 0)
        def _():
            dHf[...] = jnp.zeros_like(dHf)
            dHb[...] = jnp.zeros_like(dHb)
            st_ref[...] = jnp.zeros_like(st_ref)

        s = rs(t)
        one_dir(xf_ref, df_ref, prm_ref, dsk_ref, dyf_ref, cf(s) >= ncl, hf_ref, dHf,
                dxf_ref, ddf_ref, st_ref, 0)
        one_dir(xb_ref, db_ref, prm_ref, dsk_ref, dyb_ref, cb(s) >= ncl, hb_ref, dHb,
                dxb_ref, ddb_ref, st_ref, 1)

        @pl.when(t == nc - 1)
        def _():
            st_ref[0:1, :] = -jnp.exp(prm_ref[1:2, :]) * st_ref[0:1, :]

    def lat(c):
        return jnp.minimum(c, ncl - 1)

    xsh = jax.ShapeDtypeStruct((RT, 1536), F32)
    dsh = jax.ShapeDtypeStruct((RT, 128), F32)
    hspec = pl.BlockSpec((1, NH * HP, NS), lambda t: (rs(t), 0, 0))
    return pl.pallas_call(
        body, name="ssd_bwd", grid=(nc,),
        in_specs=[pl.BlockSpec((Q, 1536), lambda t: (cf(rs(t)), 0)),
                  pl.BlockSpec((Q, 1536), lambda t: (cb(rs(t)), 0)),
                  pl.BlockSpec((Q, 128), lambda t: (cf(rs(t)), ODT // 128)),
                  pl.BlockSpec((Q, 128), lambda t: (cb(rs(t)), ODT // 128)),
                  _cst((8, 128)), _cst((1, D)),
                  pl.BlockSpec((Q, D), lambda t: (lat(cf(rs(t))), 0)),
                  pl.BlockSpec((Q, D), lambda t: (lat(cb(rs(t))), 0)),
                  hspec, hspec],
        out_specs=[pl.BlockSpec((Q, 1536), lambda t: (cf(rs(t)), 0)),
                   pl.BlockSpec((Q, 1536), lambda t: (cb(rs(t)), 0)),
                   pl.BlockSpec((Q, 128), lambda t: (cf(rs(t)), 0)),
                   pl.BlockSpec((Q, 128), lambda t: (cb(rs(t)), 0)),
                   _cst((8, 128))],
        out_shape=[xsh, xsh, dsh, dsh, jax.ShapeDtypeStruct((8, 128), F32)],
        scratch_shapes=[pltpu.VMEM((NH * HP, NS), F32), pltpu.VMEM((NH * HP, NS), F32)],
        compiler_params=_params(("arbitrary",)),
    )(xbc, xbc, p, p, prm, dsk, dyd, dyd, hpf, hpb)


def _lane_bcast(v, ln):
    return jnp.broadcast_to(v[:, ln:ln + 1], v.shape)


def _halves(v, lo, axis):
    return jnp.concatenate([jnp.where(lo, v, 0.0), jnp.where(lo, 0.0, v)], axis=axis)


def _ssd2_fwd(xbc, p, prm):
    RT = xbc.shape[0]
    nc = RT // Q
    ncc = TL // Q
    cf, cb = _ssd_orders(nc - ncc, ncc)

    def one_dir(x_ref, dt_ref, prm_ref, y_ref, hp_ref, HT_ref, d):
        rev = d == 1
        a32 = -jnp.exp(prm_ref[1:2, :])
        dt, acum, acumT, dtT, mask = _ssd_common(dt_ref[...], prm_ref[0:1, :], a32, rev)
        end = 0 if rev else Q - 1
        lo = lax.broadcasted_iota(jnp.int32, (Q, 128), 1) < HP
        for g in range(2):
            Bg = x_ref[:, D + g * NS:D + (g + 1) * NS]
            Cg = x_ref[:, D + 2 * NS + g * NS:D + 2 * NS + (g + 1) * NS]
            CB = _dot(Cg, Bg, NT)
            xds, svs = [], []
            for q in range(HPG // 2):
                pi = g * (HPG // 2) + q
                ps = slice(pi * 128, (pi + 1) * 128)
                Xp = x_ref[:, ps]
                HTp = HT_ref[:, ps]
                lhs, dcs, sv = [], [], []
                ces = []
                for h in (2 * pi, 2 * pi + 1):
                    ln = 16 * d + h
                    colB = _lane_bcast(acum, ln)
                    rowv = acumT[ln:ln + 1, :]
                    aend = colB[end:end + 1, :]
                    Lm = jnp.exp(jnp.where(mask, colB - rowv, -1e30))
                    lhs.append(CB * Lm * dtT[ln:ln + 1, :])
                    ces.append(Cg * jnp.exp(colB))
                    dcs.append(jnp.exp(aend - colB) * _lane_bcast(dt, ln))
                    sv.append(jnp.exp(aend))
                lhs = jnp.concatenate(lhs + ces, axis=1)
                rhs = jnp.concatenate([_halves(Xp, lo, 0), _halves(HTp, lo, 0)], axis=0)
                y_ref[:, ps] = _dot(lhs, rhs, NN)
                xds.append(Xp * jnp.where(lo, dcs[0], dcs[1]))
                svs.append(jnp.where(lo[0:1, :], sv[0], sv[1]))
            gs = slice(g * 512, (g + 1) * 512)
            HTg = HT_ref[:, gs]
            hp_ref[0, :, gs] = HTg
            st = _dot(Bg.T, jnp.concatenate(xds, axis=1), NN)
            HT_ref[:, gs] = jnp.concatenate(svs, axis=1) * HTg + st

    def body(xf_ref, xb_ref, df_ref, db_ref, prm_ref, yf_ref, yb_ref, hf_ref, hb_ref, Hf, Hb):
        @pl.when(pl.program_id(0) == 0)
        def _():
            Hf[...] = jnp.zeros_like(Hf)
            Hb[...] = jnp.zeros_like(Hb)

        one_dir(xf_ref, df_ref, prm_ref, yf_ref, hf_ref, Hf, 0)
        one_dir(xb_ref, db_ref, prm_ref, yb_ref, hb_ref, Hb, 1)

    ysh = jax.ShapeDtypeStruct((RT, D), F32)
    hsh = jax.ShapeDtypeStruct((nc, NS, NH * HP), F32)
    hspec = pl.BlockSpec((1, NS, NH * HP), lambda s: (s, 0, 0))
    return pl.pallas_call(
        body, name="ssd_fwd", grid=(nc,),
        in_specs=[pl.BlockSpec((Q, 1536), lambda s: (cf(s), 0)),
                  pl.BlockSpec((Q, 1536), lambda s: (cb(s), 0)),
                  pl.BlockSpec((Q, 128), lambda s: (cf(s), ODT // 128)),
                  pl.BlockSpec((Q, 128), lambda s: (cb(s), ODT // 128)),
                  _cst((8, 128))],
        out_specs=[pl.BlockSpec((Q, D), lambda s: (cf(s), 0)),
                   pl.BlockSpec((Q, D), lambda s: (cb(s), 0)), hspec, hspec],
        out_shape=[ysh, ysh, hsh, hsh],
        scratch_shapes=[pltpu.VMEM((NS, NH * HP), F32), pltpu.VMEM((NS, NH * HP), F32)],
        compiler_params=_params(("arbitrary",)),
    )(xbc, xbc, p, p, prm)


def _ssd2_bwd(xbc, p, prm, dsk, dyd, hpf, hpb):
    RT = xbc.shape[0]
    nc = RT // Q
    ncc = TL // Q
    ncl = nc - ncc
    cf, cb = _ssd_orders(ncl, ncc)

    def rs(t):
        return nc - 1 - t

    def one_dir(x_ref, dt_ref, prm_ref, dsk_ref, dy_ref, is_ctx, hp_ref, dHT_ref,
                dx_ref, ddt_ref, st_ref, d):
        rev = d == 1
        a32 = -jnp.exp(prm_ref[1:2, :])
        dtraw = dt_ref[...]
        dtb = prm_ref[0:1, :]
        dt, acum, acumT, _, _ = _ssd_common(dtraw, dtb, a32, rev)
        end = 0 if rev else Q - 1
        lane = lax.broadcasted_iota(jnp.int32, (Q, 128), 1)
        srow = lax.broadcasted_iota(jnp.int32, (Q, 128), 0)
        maskT = (lane <= srow) if rev else (lane >= srow)
        lo = lane < HP
        lo1 = lo[0:1, :]
        dyscale = jnp.where(is_ctx, 0.0, 1.0)
        c_dacum = jnp.zeros((Q, 128), F32)
        r_dacum = jnp.zeros((Q, 128), F32)
        c_ddt = jnp.zeros((Q, 128), F32)
        dskacc = jnp.zeros((1, 128), F32)
        for g in range(2):
            gs = slice(g * 512, (g + 1) * 512)
            Bg = x_ref[:, D + g * NS:D + (g + 1) * NS]
            Cg = x_ref[:, D + 2 * NS + g * NS:D + 2 * NS + (g + 1) * NS]
            CBT = _dot(Bg, Cg, NT)
            HTg = hp_ref[0, :, gs]
            dHTg = dHT_ref[:, gs]
            BdHg = _dot(Bg, dHTg, NN)
            dCBT = jnp.zeros((Q, Q), F32)
            dCg = jnp.zeros((Q, NS), F32)
            xds, dyes, svs = [], [], []
            for q in range(HPG // 2):
                pi = g * (HPG // 2) + q
                ps = slice(pi * 128, (pi + 1) * 128)
                qs = slice(q * 128, (q + 1) * 128)
                Xp = x_ref[:, ps]
                dYp = dy_ref[:, ps] * dyscale
                HTp = HTg[:, qs]
                BdHp = BdHg[:, qs]
                dY2 = _halves(dYp, lo, 0)
                dWT2 = _dot(_halves(Xp, lo, 0), dYp.T, NN)
                G2 = _dot(dY2, HTp, NT)
                XB = Xp * BdHp
                hh = _colsum(dHTg[:, qs] * HTp)
                yx = _colsum(dYp * Xp)
                wts, dcs, ebs, sv = [], [], [], []
                for k, h in enumerate((2 * pi, 2 * pi + 1)):
                    ln = 16 * d + h
                    half = lo if k == 0 else jnp.logical_not(lo)
                    half1 = half[0:1, :]
                    colB = _lane_bcast(acum, ln)
                    dtcB = _lane_bcast(dt, ln)
                    rowv = acumT[ln:ln + 1, :]
                    aend = colB[end:end + 1, :]
                    LmT = jnp.exp(jnp.where(maskT, rowv - colB, -1e30))
                    WT = CBT * LmT * dtcB
                    dWT = dWT2[k * Q:(k + 1) * Q, :]
                    U = dWT * LmT
                    MT = U * CBT
                    rM = jnp.sum(MT, axis=1, keepdims=True)
                    rT = _colsum(MT * dtcB)
                    dCBT = dCBT + U * dtcB
                    ecol = jnp.exp(aend - colB)
                    EB = jnp.exp(colB)
                    Gk = G2[k * Q:(k + 1) * Q, :]
                    dCg = dCg + EB * Gk
                    qcol = jnp.sum(EB * Gk * Cg, axis=1, keepdims=True)
                    xb = jnp.sum(jnp.where(half, XB, 0.0), axis=1, keepdims=True)
                    e1 = ecol[:, 0:1]
                    dt1 = dtcB[:, 0:1]
                    scol = e1 * dt1 * xb
                    sA = jnp.exp(aend)
                    eterm = sA[:, 0:1] * jnp.sum(jnp.where(half1, hh, 0.0), axis=1, keepdims=True) \
                        + _colsum(scol)
                    cvec = qcol - dt1 * rM - scol + jnp.where(srow[:, 0:1] == end, eterm, 0.0)
                    c_dacum = jnp.where(lane == ln, cvec, c_dacum)
                    r_dacum = jnp.where(srow == ln, rT, r_dacum)
                    c_ddt = jnp.where(lane == ln, rM + e1 * xb, c_ddt)
                    if d == 0:
                        dskacc = dskacc + jnp.where(
                            lane[0:1, :] == h, jnp.sum(jnp.where(half1, yx, 0.0), axis=1, keepdims=True), 0.0)
                    wts.append(WT)
                    dcs.append(ecol * dtcB)
                    ebs.append(EB)
                    sv.append(sA)
                dcp = jnp.where(lo, dcs[0], dcs[1])
                dX = _dot(jnp.concatenate(wts, axis=1), dY2, NN) + dcp * BdHp
                if d == 0:
                    dX = dX + dYp * dsk_ref[:, ps]
                dx_ref[:, ps] = dX
                xds.append(Xp * dcp)
                dyes.append(dYp * jnp.where(lo, ebs[0], ebs[1]))
                svs.append(jnp.where(lo1, sv[0], sv[1]))
            dx_ref[:, D + g * NS:D + (g + 1) * NS] = (
                _dot(jnp.concatenate(xds, axis=1), dHTg, NT) + _dot(dCBT, Cg, NN))
            dx_ref[:, D + 2 * NS + g * NS:D + 2 * NS + (g + 1) * NS] = dCg + _dot(dCBT, Bg, TN)
            dHT_ref[:, gs] = (jnp.concatenate(svs, axis=1) * dHTg
                              + _dot(Cg.T, jnp.concatenate(dyes, axis=1), NN))
        dacum = c_dacum + r_dacum.T
        da = _cumsum_rows(dacum, not rev)
        mine = (lane >= 16 * d) & (lane < 16 * d + 16)
        ddt = jnp.where(mine, c_ddt + da * a32, 0.0)
        ddt_ref[...] = ddt * _sig(dtraw + dtb)
        st_ref[0:1, :] += _colsum(jnp.where(mine, da * dt, 0.0))
        if d == 0:
            st_ref[1:2, :] += dskacc

    def body(xf_ref, xb_ref, df_ref, db_ref, prm_ref, dsk_ref, dyf_ref, dyb_ref, hf_ref, hb_ref,
             dxf_ref, dxb_ref, ddf_ref, ddb_ref, st_ref, dHf, dHb):
        t = pl.program_id(0)

        @pl.when(t == 0)
        def _():
            dHf[...] = jnp.zeros_like(dHf)
            dHb[...] = jnp.zeros_like(dHb)
            st_ref[...] = jnp.zeros_like(st_ref)

        s = rs(t)
        one_dir(xf_ref, df_ref, prm_ref, dsk_ref, dyf_ref, cf(s) >= ncl, hf_ref, dHf,
                dxf_ref, ddf_ref, st_ref, 0)
        one_dir(xb_ref, db_ref, prm_ref, dsk_ref, dyb_ref, cb(s) >= ncl, hb_ref, dHb,
                dxb_ref, ddb_ref, st_ref, 1)

        @pl.when(t == nc - 1)
        def _():
            st_ref[0:1, :] = -jnp.exp(prm_ref[1:2, :]) * st_ref[0:1, :]

    def lat(c):
        return jnp.minimum(c, ncl - 1)

    xsh = jax.ShapeDtypeStruct((RT, 1536), F32)
    dsh = jax.ShapeDtypeStruct((RT, 128), F32)
    hspec = pl.BlockSpec((1, NS, NH * HP), lambda t: (rs(t), 0, 0))
    return pl.pallas_call(
        body, name="ssd_bwd", grid=(nc,),
        in_specs=[pl.BlockSpec((Q, 1536), lambda t: (cf(rs(t)), 0)),
                  pl.BlockSpec((Q, 1536), lambda t: (cb(rs(t)), 0)),
                  pl.BlockSpec((Q, 128), lambda t: (cf(rs(t)), ODT // 128)),
                  pl.BlockSpec((Q, 128), lambda t: (cb(rs(t)), ODT // 128)),
                  _cst((8, 128)), _cst((1, D)),
                  pl.BlockSpec((Q, D), lambda t: (lat(cf(rs(t))), 0)),
                  pl.BlockSpec((Q, D), lambda t: (lat(cb(rs(t))), 0)),
                  hspec, hspec],
        out_specs=[pl.BlockSpec((Q, 1536), lambda t: (cf(rs(t)), 0)),
                   pl.BlockSpec((Q, 1536), lambda t: (cb(rs(t)), 0)),
                   pl.BlockSpec((Q, 128), lambda t: (cf(rs(t)), 0)),
                   pl.BlockSpec((Q, 128), lambda t: (cb(rs(t)), 0)),
                   _cst((8, 128))],
        out_shape=[xsh, xsh, dsh, dsh, jax.ShapeDtypeStruct((8, 128), F32)],
        scratch_shapes=[pltpu.VMEM((NS, NH * HP), F32), pltpu.VMEM((NS, NH * HP), F32)],
        compiler_params=_params(("arbitrary",)),
    )(xbc, xbc, p, p, prm, dsk, dyd, dyd, hpf, hpb)


RB = 32


def _ssd3_bwd(xbc, p, prm, dsk, dyd, hpf, hpb):
    RT = xbc.shape[0]
    nc = RT // Q
    ncc = TL // Q
    ncl = nc - ncc
    cf, cb = _ssd_orders(ncl, ncc)
    npair = HPG // 2

    def rs(t):
        return nc - 1 - t

    def one_dir(x_ref, dt_ref, prm_ref, dsk_ref, dy_ref, is_ctx, hp_ref, dHT_ref,
                dx_ref, ddt_ref, st_ref, s_dwt, s_g, s_wt, s_xd, s_dye, s_dcbt, s_dcg, s_cd, s_cdt, d):
        rev = d == 1
        a32 = -jnp.exp(prm_ref[1:2, :])
        dtraw = dt_ref[...]
        dtb = prm_ref[0:1, :]
        dt, acum, acumT, _, _ = _ssd_common(dtraw, dtb, a32, rev)
        end = 0 if rev else Q - 1
        lane = lax.broadcasted_iota(jnp.int32, (RB, 128), 1)
        srow0 = lax.broadcasted_iota(jnp.int32, (RB, 128), 0)
        lo = lane < HP
        lo1 = lo[0:1, :]
        lane1 = lane[0:1, :]
        dyscale = jnp.where(is_ctx, 0.0, 1.0)
        aend_row = acum[end:end + 1, :]
        s_cd[...] = jnp.zeros_like(s_cd)
        s_cdt[...] = jnp.zeros_like(s_cdt)
        r_rows = jnp.zeros((Q, 128), F32)
        srowQ = lax.broadcasted_iota(jnp.int32, (Q, 128), 0)
        dskacc = jnp.zeros((1, 128), F32)
        for g in range(2):
            gs = slice(g * 512, (g + 1) * 512)
            Bg = x_ref[:, D + g * NS:D + (g + 1) * NS]
            Cg = x_ref[:, D + 2 * NS + g * NS:D + 2 * NS + (g + 1) * NS]
            CBT = _dot(Bg, Cg, NT)
            HTg = hp_ref[0, :, gs]
            dHTg = dHT_ref[:, gs]
            BdHg = _dot(Bg, dHTg, NN)
            hhs, yxs = [], []
            for q in range(npair):
                pi = g * npair + q
                ps = slice(pi * 128, (pi + 1) * 128)
                qs = slice(q * 128, (q + 1) * 128)
                Xp = x_ref[:, ps]
                dYp = dy_ref[:, ps] * dyscale
                s_dwt[q] = _dot(_halves(Xp, lo_full(), 0), dYp.T, NN)
                s_g[q] = _dot(_halves(dYp, lo_full(), 0), HTg[:, qs], NT)
                hhs.append(_colsum(dHTg[:, qs] * HTg[:, qs]))
                yxs.append(_colsum(dYp * Xp))
            rparts = [jnp.zeros((8, 128), F32) for _ in range(HPG)]
            ssum = [jnp.zeros((1, 1), F32) for _ in range(HPG)]
            for rb in range(Q // RB):
                r0 = rb * RB
                rows = slice(r0, r0 + RB)
                srow = srow0 + r0
                maskT = (lane <= srow) if rev else (lane >= srow)
                acum_rb = acum[rows, :]
                dt_rb = dt[rows, :]
                CBT_rb = CBT[rows, :]
                Cg_rb = Cg[rows, :]
                dcbt = jnp.zeros((RB, Q), F32)
                dcg = jnp.zeros((RB, NS), F32)
                cd = s_cd[rows, :]
                cdt = s_cdt[rows, :]
                for q in range(npair):
                    pi = g * npair + q
                    ps = slice(pi * 128, (pi + 1) * 128)
                    qs = slice(q * 128, (q + 1) * 128)
                    Xp = x_ref[rows, ps]
                    dYp = dy_ref[rows, ps] * dyscale
                    BdHp = BdHg[rows, qs]
                    XB = Xp * BdHp
                    dcs, ebs = [], []
                    for k in range(2):
                        hh = 2 * q + k
                        ln = 16 * d + g * HPG + hh
                        half = lo if k == 0 else jnp.logical_not(lo)
                        colB = _lane_bcast(acum_rb, ln)
                        dtcB = _lane_bcast(dt_rb, ln)
                        rowv = acumT[ln:ln + 1, :]
                        aend = _lane_bcast(aend_row, ln)
                        LmT = jnp.exp(jnp.where(maskT, rowv - colB, -1e30))
                        s_wt[q, rows, k * Q:(k + 1) * Q] = (CBT_rb * LmT * dtcB).astype(s_wt.dtype)
                        U = s_dwt[q, k * Q + r0:k * Q + r0 + RB, :] * LmT
                        MT = U * CBT_rb
                        rM = jnp.sum(MT, axis=1, keepdims=True)
                        TT = MT * dtcB
                        rparts[hh] = rparts[hh] + (TT[0:8] + TT[8:16] + TT[16:24] + TT[24:32])
                        dcbt = dcbt + U * dtcB
                        ecol = jnp.exp(aend - colB)
                        EB = jnp.exp(colB)
                        EG = EB * s_g[q, k * Q + r0:k * Q + r0 + RB, :]
                        dcg = dcg + EG
                        qcol = jnp.sum(EG * Cg_rb, axis=1, keepdims=True)
                        xb = jnp.sum(jnp.where(half, XB, 0.0), axis=1, keepdims=True)
                        e1 = ecol[:, 0:1]
                        dt1 = dtcB[:, 0:1]
                        scol = e1 * dt1 * xb
                        ssum[hh] = ssum[hh] + _colsum(scol)
                        cd = jnp.where(lane == ln, qcol - dt1 * rM - scol, cd)
                        cdt = jnp.where(lane == ln, rM + e1 * xb, cdt)
                        dcs.append(ecol * dtcB)
                        ebs.append(EB)
                    dcp = jnp.where(lo, dcs[0], dcs[1])
                    dxo = dcp * BdHp
                    if d == 0:
                        dxo = dxo + dYp * dsk_ref[:, ps]
                    dx_ref[rows, ps] = dxo
                    s_xd[rows, qs] = (Xp * dcp).astype(s_xd.dtype)
                    s_dye[rows, qs] = (dYp * jnp.where(lo, ebs[0], ebs[1])).astype(s_dye.dtype)
                s_dcbt[rows, :] = dcbt
                s_dcg[rows, :] = dcg
                s_cd[rows, :] = cd
                s_cdt[rows, :] = cdt
            erow = jnp.zeros((1, 128), F32)
            svs = []
            for q in range(npair):
                pi = g * npair + q
                ps = slice(pi * 128, (pi + 1) * 128)
                sv = []
                for k in range(2):
                    hh = 2 * q + k
                    h = g * HPG + hh
                    ln = 16 * d + h
                    half1 = lo1 if k == 0 else jnp.logical_not(lo1)
                    sA = jnp.exp(_lane_bcast(aend_row, ln))
                    hsum = jnp.sum(jnp.where(half1, hhs[q], 0.0), axis=1, keepdims=True)
                    erow = erow + jnp.where(lane1 == ln, sA[:, 0:1] * hsum + ssum[hh], 0.0)
                    rp = rparts[hh]
                    r_rows = jnp.where(srowQ == ln, _colsum(rp), r_rows)
                    if d == 0:
                        dskacc = dskacc + jnp.where(
                            lane1 == h, jnp.sum(jnp.where(half1, yxs[q], 0.0), axis=1, keepdims=True), 0.0)
                    sv.append(sA)
                svs.append(jnp.where(lo1, sv[0], sv[1]))
                dY2 = _halves(dy_ref[:, ps] * dyscale, lo_full(), 0)
                dx_ref[:, ps] += _dot(s_wt[q], dY2, NN)
            s_cd[end:end + 1, :] += erow
            dcbt_g = s_dcbt[...]
            dx_ref[:, D + g * NS:D + (g + 1) * NS] = _dot(s_xd[...], dHTg, NT) + _dot(dcbt_g, Cg, NN)
            dx_ref[:, D + 2 * NS + g * NS:D + 2 * NS + (g + 1) * NS] = s_dcg[...] + _dot(dcbt_g, Bg, TN)
            dHT_ref[:, gs] = jnp.concatenate(svs, axis=1) * dHTg + _dot(Cg.T, s_dye[...], NN)
        dacum = s_cd[...] + r_rows.T
        da = _cumsum_rows(dacum, not rev)
        laneQ = lax.broadcasted_iota(jnp.int32, (Q, 128), 1)
        mine = (laneQ >= 16 * d) & (laneQ < 16 * d + 16)
        ddt = jnp.where(mine, s_cdt[...] + da * a32, 0.0)
        ddt_ref[...] = ddt * _sig(dtraw + dtb)
        st_ref[0:1, :] += _colsum(jnp.where(mine, da * dt, 0.0))
        if d == 0:
            st_ref[1:2, :] += dskacc

    def lo_full():
        return lax.broadcasted_iota(jnp.int32, (Q, 128), 1) < HP

    def body(xf_ref, xb_ref, df_ref, db_ref, prm_ref, dsk_ref, dyf_ref, dyb_ref, hf_ref, hb_ref,
             dxf_ref, dxb_ref, ddf_ref, ddb_ref, st_ref, dHf, dHb, *scr):
        t = pl.program_id(0)

        @pl.when(t == 0)
        def _():
            dHf[...] = jnp.zeros_like(dHf)
            dHb[...] = jnp.zeros_like(dHb)
            st_ref[...] = jnp.zeros_like(st_ref)

        s = rs(t)
        one_dir(xf_ref, df_ref, prm_ref, dsk_ref, dyf_ref, cf(s) >= ncl, hf_ref, dHf,
                dxf_ref, ddf_ref, st_ref, *scr, 0)
        one_dir(xb_ref, db_ref, prm_ref, dsk_ref, dyb_ref, cb(s) >= ncl, hb_ref, dHb,
                dxb_ref, ddb_ref, st_ref, *scr, 1)

        @pl.when(t == nc - 1)
        def _():
            st_ref[0:1, :] = -jnp.exp(prm_ref[1:2, :]) * st_ref[0:1, :]

    def lat(c):
        return jnp.minimum(c, ncl - 1)

    xsh = jax.ShapeDtypeStruct((RT, 1536), F32)
    dsh = jax.ShapeDtypeStruct((RT, 128), F32)
    hspec = pl.BlockSpec((1, NS, NH * HP), lambda t: (rs(t), 0, 0))
    return pl.pallas_call(
        body, name="ssd_bwd", grid=(nc,),
        in_specs=[pl.BlockSpec((Q, 1536), lambda t: (cf(rs(t)), 0)),
                  pl.BlockSpec((Q, 1536), lambda t: (cb(rs(t)), 0)),
                  pl.BlockSpec((Q, 128), lambda t: (cf(rs(t)), ODT // 128)),
                  pl.BlockSpec((Q, 128), lambda t: (cb(rs(t)), ODT // 128)),
                  _cst((8, 128)), _cst((1, D)),
                  pl.BlockSpec((Q, D), lambda t: (lat(cf(rs(t))), 0)),
                  pl.BlockSpec((Q, D), lambda t: (lat(cb(rs(t))), 0)),
                  hspec, hspec],
        out_specs=[pl.BlockSpec((Q, 1536), lambda t: (cf(rs(t)), 0)),
                   pl.BlockSpec((Q, 1536), lambda t: (cb(rs(t)), 0)),
                   pl.BlockSpec((Q, 128), lambda t: (cf(rs(t)), 0)),
                   pl.BlockSpec((Q, 128), lambda t: (cb(rs(t)), 0)),
                   _cst((8, 128))],
        out_shape=[xsh, xsh, dsh, dsh, jax.ShapeDtypeStruct((8, 128), F32)],
        scratch_shapes=[pltpu.VMEM((NS, NH * HP), F32), pltpu.VMEM((NS, NH * HP), F32),
                        pltpu.VMEM((npair, 2 * Q, Q), F32), pltpu.VMEM((npair, 2 * Q, NS), F32),
                        pltpu.VMEM((npair, Q, 2 * Q), _MXU), pltpu.VMEM((Q, 512), _MXU),
                        pltpu.VMEM((Q, 512), _MXU), pltpu.VMEM((Q, Q), F32), pltpu.VMEM((Q, NS), F32),
                        pltpu.VMEM((Q, 128), F32), pltpu.VMEM((Q, 128), F32)],
        compiler_params=_params(("arbitrary",)),
    )(xbc, xbc, p, p, prm, dsk, dyd, dyd, hpf, hpb)


def _mix_fwd_vals(yf, yb, z, xs, u, v, dsk, sg, gg, gb):
    y = yf + yb + xs * dsk
    sz = _sig(z)
    hh = y * z * sz
    r = lax.rsqrt(jnp.mean(hh * hh, axis=-1, keepdims=True) + EPS)
    nh = hh * r
    ug, tu = _gelu(u)
    vg, tv = _gelu(v)
    vhat, vrstd = _ln(vg)
    vn = vhat * gg + gb
    return y, sz, r, nh, ug, tu, vg, tv, vhat, vrstd, vn


def _mix_fwd(yf, yb, p, xbc, dsk, sg, gg, gb, ws, bsT):
    L = yf.shape[0] - TL
    nt = L // TL

    def body(yf_ref, yb_ref, z_ref, xs_ref, u_ref, v_ref, dsk_ref, sg_ref, gg_ref, gb_ref,
             ws_ref, bs_ref, ys_ref, ym_ref):
        _, _, _, nh, ug, _, _, _, _, _, vn = _mix_fwd_vals(
            yf_ref[...], yb_ref[...], z_ref[...], xs_ref[...], u_ref[...], v_ref[...],
            dsk_ref[...], sg_ref[...], gg_ref[...], gb_ref[...])
        ys_ref[...] = (nh * sg_ref[...]).astype(ys_ref.dtype)
        for n in range(TL // Q):
            rs_ = slice(n * Q, (n + 1) * Q)
            for g in range(8):
                cs = slice(g * 128, (g + 1) * 128)
                mixed = _dot(ws_ref[g], vn[rs_, cs], NN) + bs_ref[:, g:g + 1]
                ym_ref[rs_, cs] = (ug[rs_, cs] * mixed).astype(ym_ref.dtype)

    return pl.pallas_call(
        body, name="mix_fwd", grid=(nt,),
        in_specs=[_rt(D), _rt(D), _rt(D, OZ // D), _rt(D, 0), _rt(D, OU // D), _rt(D, OV // D),
                  _cst((1, D)), _cst((1, D)), _cst((1, D)), _cst((1, D)),
                  _cst((8, 128, 128)), _cst((128, 128))],
        out_specs=[_rt(D), _rt(D)],
        out_shape=[jax.ShapeDtypeStruct((L, D), _MXU), jax.ShapeDtypeStruct((L, D), _MXU)],
        compiler_params=_params(("parallel",)),
    )(yf, yb, p, xbc, p, p, dsk, sg, gg, gb, ws, bsT)


def _mix_bwd(dys, dym, yf, yb, p, xbc, dp, dsk, sg, gg, gb, ws, bsT):
    L = dys.shape[0]
    nt = L // TL

    def body(dys_ref, dym_ref, yf_ref, yb_ref, z_ref, xs_ref, u_ref, v_ref, dsk_ref, sg_ref,
             gg_ref, gb_ref, ws_ref, bs_ref, dp_any, dzuv_ref, dy_ref, st_ref,
             dws_ref, dbs_ref, dvn_s):
        del dp_any
        dz_ref = dzuv_ref.at[:, OZ:OZ + D]
        du_ref = dzuv_ref.at[:, OU:OU + D]
        dv_ref = dzuv_ref.at[:, OV:OV + D]

        @pl.when(pl.program_id(0) == 0)
        def _():
            st_ref[...] = jnp.zeros_like(st_ref)
            dws_ref[...] = jnp.zeros_like(dws_ref)
            dbs_ref[...] = jnp.zeros_like(dbs_ref)

        z = z_ref[...]
        u = u_ref[...]
        v = v_ref[...]
        y, sz, r, nh, ug, tu, vg, tv, vhat, vrstd, vn = _mix_fwd_vals(
            yf_ref[...], yb_ref[...], z, xs_ref[...], u, v,
            dsk_ref[...], sg_ref[...], gg_ref[...], gb_ref[...])
        dys = dys_ref[...]
        st_ref[0:1, :] += _colsum(dys * nh)
        dn = dys * sg_ref[...]
        dhh = r * (dn - nh * jnp.mean(dn * nh, axis=-1, keepdims=True))
        dy_ref[...] = dhh * z * sz
        dz_ref[...] = (dhh * y * (sz * (1.0 + z * (1.0 - sz)))).astype(dz_ref.dtype)
        dym = dym_ref[...]
        lane = lax.broadcasted_iota(jnp.int32, (Q, 128), 1)
        dbs = jnp.zeros((Q, 128), F32)
        gu = _gelu_grad(u, tu)
        for n in range(TL // Q):
            rs_ = slice(n * Q, (n + 1) * Q)
            for g in range(8):
                cs = slice(g * 128, (g + 1) * 128)
                vb = vn[rs_, cs]
                mixed = _dot(ws_ref[g], vb, NN) + bs_ref[:, g:g + 1]
                dyb = dym[rs_, cs]
                dmx = dyb * ug[rs_, cs]
                du_ref[rs_, cs] = (dyb * mixed * gu[rs_, cs]).astype(du_ref.dtype)
                dvn_s[rs_, cs] = _dot(ws_ref[g], dmx, TN)
                dws_ref[g] += _dot(dmx, vb, NT)
                dbs = dbs + jnp.where(lane == g, jnp.sum(dmx, axis=1, keepdims=True), 0.0)
        dbs_ref[...] += dbs
        dvn = dvn_s[...]
        st_ref[1:2, :] += _colsum(dvn * vhat)
        st_ref[2:3, :] += _colsum(dvn)
        dvg = _ln_bwd(dvn * gg_ref[...], vhat, vrstd)
        dv_ref[...] = (dvg * _gelu_grad(v, tv)).astype(dv_ref.dtype)

    outs = pl.pallas_call(
        body, name="mix_bwd", grid=(nt,),
        in_specs=[_rt(D), _rt(D), _rt(D), _rt(D), _rt(D, OZ // D), _rt(D, 0), _rt(D, OU // D),
                  _rt(D, OV // D), _cst((1, D)), _cst((1, D)), _cst((1, D)), _cst((1, D)),
                  _cst((8, 128, 128)), _cst((128, 128)), pl.BlockSpec(memory_space=pl.ANY)],
        out_specs=[_rt(3 * D, 0), _rt(D), _cst((8, D)),
                   _cst((8, 128, 128)), _cst((128, 128))],
        out_shape=[jax.ShapeDtypeStruct(dp.shape, dp.dtype),
                   jax.ShapeDtypeStruct((L, D), F32), jax.ShapeDtypeStruct((8, D), F32),
                   jax.ShapeDtypeStruct((8, 128, 128), F32), jax.ShapeDtypeStruct((128, 128), F32)],
        scratch_shapes=[pltpu.VMEM((TL, D), F32)],
        input_output_aliases={14: 0},
        compiler_params=_params(("arbitrary",)),
    )(dys, dym, yf, yb, p, xbc, p, p, dsk, sg, gg, gb, ws, bsT, dp)
    return outs


def _gate_fwd(a1, a2, p, bg):
    L = a1.shape[0]

    def body(a1_ref, a2_ref, g_ref, bg_ref, m_ref):
        gt = _sig(g_ref[...] + bg_ref[...])
        m_ref[...] = (gt[:, :D] * a1_ref[...] + gt[:, D:] * a2_ref[...]).astype(m_ref.dtype)

    return pl.pallas_call(
        body, name="gate_fwd", grid=(L // TL,),
        in_specs=[_rt(D), _rt(D), _rt(2 * D, OG // (2 * D)), _cst((1, 2 * D))],
        out_specs=_rt(D), out_shape=jax.ShapeDtypeStruct((L, D), _MXU),
        compiler_params=_params(("parallel",)),
    )(a1, a2, p, bg)


def _gate_bwd(dmg, a1, a2, p, bg, dp):
    L = a1.shape[0]

    def body(dm_ref, a1_ref, a2_ref, g_ref, bg_ref, dp_any, dg_ref, da1_ref, da2_ref, st_ref):
        del dp_any

        @pl.when(pl.program_id(0) == 0)
        def _():
            st_ref[...] = jnp.zeros_like(st_ref)

        gt = _sig(g_ref[...] + bg_ref[...])
        g1 = gt[:, :D]
        g2 = gt[:, D:]
        dm = dm_ref[...]
        da1_ref[...] = (dm * g1).astype(da1_ref.dtype)
        da2_ref[...] = (dm * g2).astype(da2_ref.dtype)
        dg1 = dm * a1_ref[...] * g1 * (1.0 - g1)
        dg2 = dm * a2_ref[...] * g2 * (1.0 - g2)
        st_ref[0:1, 0:D] += _colsum(dg1)
        st_ref[0:1, D:2 * D] += _colsum(dg2)
        dg_ref[:, 0:D] = dg1.astype(dg_ref.dtype)
        dg_ref[:, D:2 * D] = dg2.astype(dg_ref.dtype)

    return pl.pallas_call(
        body, name="gate_bwd", grid=(L // TL,),
        in_specs=[_rt(D), _rt(D), _rt(D), _rt(2 * D, OG // (2 * D)), _cst((1, 2 * D)),
                  pl.BlockSpec(memory_space=pl.ANY)],
        out_specs=[_rt(2 * D, OG // (2 * D)), _rt(D), _rt(D), _cst((8, 2 * D))],
        out_shape=[jax.ShapeDtypeStruct(dp.shape, dp.dtype), jax.ShapeDtypeStruct((L, D), _MXU),
                   jax.ShapeDtypeStruct((L, D), _MXU), jax.ShapeDtypeStruct((8, 2 * D), F32)],
        input_output_aliases={5: 0},
        compiler_params=_params(("arbitrary",)),
    )(dmg, a1, a2, p, bg, dp)


def _merge_fwd(yssd, ygm, p, bg, ws, wg, wo, xn, modx, g1, b1):
    L = yssd.shape[0]
    tm = TL

    def body(ys_ref, yg_ref, g_ref, bg_ref, ws_ref, wg_ref, wo_ref, xn_ref, mx_ref, g1_ref, b1_ref,
             a1_ref, a2_ref, m_ref, o_ref, r1_ref, h2_ref):
        a1 = _dot(ys_ref[...], ws_ref[...], NN)
        a2 = _dot(yg_ref[...], wg_ref[...], NN)
        gt = _sig(g_ref[...] + bg_ref[...])
        mg = gt[:, :D] * a1 + gt[:, D:] * a2
        a1_ref[...] = a1
        a2_ref[...] = a2
        m_ref[...] = mg.astype(m_ref.dtype)
        out = _dot(mg, wo_ref[...], NN)
        o_ref[...] = out
        r1 = ALPHA * xn_ref[...] + mx_ref[2:3, :] * out
        xhat, _ = _ln(r1)
        x1 = xhat * g1_ref[...] + b1_ref[...]
        r1_ref[...] = r1
        h2_ref[...] = (x1 * (1.0 + mx_ref[4:5, :]) + mx_ref[3:4, :]).astype(h2_ref.dtype)

    rows = pl.BlockSpec((tm, D), lambda i: (i, 0))
    f32s = jax.ShapeDtypeStruct((L, D), F32)
    mxus = jax.ShapeDtypeStruct((L, D), _MXU)
    return pl.pallas_call(
        body, name="merge_fwd", grid=(L // tm,),
        in_specs=[rows, rows, pl.BlockSpec((tm, 2 * D), lambda i: (i, OG // (2 * D))), _cst((1, 2 * D)),
                  _cst((D, D)), _cst((D, D)), _cst((D, D)), rows, _cst((8, D)), _cst((1, D)), _cst((1, D))],
        out_specs=[rows] * 6,
        out_shape=[f32s, f32s, mxus, f32s, f32s, mxus],
        compiler_params=_params(("parallel",)),
    )(yssd, ygm, p, bg, ws, wg, wo, xn, modx, g1, b1)


def _mm_o2_res2(ff, w2, r1, tgt, modx, g1, b1, g2, b2):
    L, K = ff.shape
    tm = 2 * TL
    tk = K // 2
    nk = K // tk

    def body(a_ref, b_ref, r1_ref, t_ref, mx_ref, g1_ref, b1_ref, g2_ref, b2_ref,
             dr2_ref, do2_ref, st_ref, loss_ref, acc_ref):
        i, k = pl.program_id(0), pl.program_id(1)

        @pl.when((i == 0) & (k == 0))
        def _():
            st_ref[...] = jnp.zeros_like(st_ref)
            loss_ref[...] = jnp.zeros_like(loss_ref)

        prod = _dot(a_ref[...], b_ref[...], NN)

        @pl.when(k == 0)
        def _():
            acc_ref[...] = prod

        @pl.when(k == nk - 1)
        def _():
            o2 = acc_ref[...] + prod
            xh1, _ = _ln(r1_ref[...])
            x1 = xh1 * g1_ref[...] + b1_ref[...]
            g2x = mx_ref[5:6, :]
            xh2, rstd2 = _ln(ALPHA * x1 + g2x * o2)
            err = xh2 * g2_ref[...] + b2_ref[...] - t_ref[...]
            per_tok = jnp.mean(err * err, axis=-1, keepdims=True)
            loss_ref[...] += 0.5 * jnp.sum(per_tok, axis=0, keepdims=True)
            dy = err * (1.0 / D)
            st_ref[0:1, :] += _colsum(dy * xh2)
            st_ref[1:2, :] += _colsum(dy)
            dr2 = _ln_bwd(dy * g2_ref[...], xh2, rstd2)
            st_ref[2:3, :] += _colsum(dr2 * o2)
            dr2_ref[...] = dr2
            do2_ref[...] = (g2x * dr2).astype(do2_ref.dtype)

    assert nk == 2
    rows = pl.BlockSpec((tm, D), lambda i, k: (i, 0))
    vec = pl.BlockSpec((1, D), lambda i, k: (0, 0))
    return pl.pallas_call(
        body, name="mm_o2_res2", grid=(L // tm, nk),
        in_specs=[pl.BlockSpec((tm, tk), lambda i, k: (i, k)), pl.BlockSpec((tk, D), lambda i, k: (k, 0)),
                  rows, rows, pl.BlockSpec((8, D), lambda i, k: (0, 0)), vec, vec, vec, vec],
        out_specs=[rows, rows, pl.BlockSpec((8, D), lambda i, k: (0, 0)),
                   pl.BlockSpec((8, 128), lambda i, k: (0, 0))],
        out_shape=[jax.ShapeDtypeStruct((L, D), F32), jax.ShapeDtypeStruct((L, D), _MXU),
                   jax.ShapeDtypeStruct((8, D), F32), jax.ShapeDtypeStruct((8, 128), F32)],
        scratch_shapes=[pltpu.VMEM((tm, D), F32)],
        compiler_params=_params(("arbitrary", "arbitrary")),
    )(ff, w2, r1, tgt, modx, g1, b1, g2, b2)


def _mm_dh2_res1bwd(df13, w13i, dr2, r1, out, modx, g1, b1):
    L, K = df13.shape
    tm = 2 * TL
    tk = HFF
    nk = K // tk

    def body(a_ref, b_ref, dr2_ref, r1_ref, o_ref, mx_ref, g_ref, bb_ref, dr1_ref, do_ref, st_ref, acc_ref):
        i, k = pl.program_id(0), pl.program_id(1)

        @pl.when((i == 0) & (k == 0))
        def _():
            st_ref[...] = jnp.zeros_like(st_ref)

        prod = _dot(a_ref[...], b_ref[...], NN)

        @pl.when(k == 0)
        def _():
            acc_ref[...] = prod

        @pl.when((k > 0) & (k < nk - 1))
        def _():
            acc_ref[...] += prod

        @pl.when(k == nk - 1)
        def _():
            dh2 = acc_ref[...] + prod
            xh1, rstd1 = _ln(r1_ref[...])
            x1 = xh1 * g_ref[...] + bb_ref[...]
            dx1 = ALPHA * dr2_ref[...] + dh2 * (1.0 + mx_ref[4:5, :])
            st_ref[0:1, :] += _colsum(dh2 * x1)
            st_ref[1:2, :] += _colsum(dh2)
            st_ref[2:3, :] += _colsum(dx1 * xh1)
            st_ref[3:4, :] += _colsum(dx1)
            dr1 = _ln_bwd(dx1 * g_ref[...], xh1, rstd1)
            st_ref[4:5, :] += _colsum(dr1 * o_ref[...])
            dr1_ref[...] = dr1
            do_ref[...] = (mx_ref[2:3, :] * dr1).astype(do_ref.dtype)

    assert nk >= 2
    rows = pl.BlockSpec((tm, D), lambda i, k: (i, 0))
    vec = pl.BlockSpec((1, D), lambda i, k: (0, 0))
    return pl.pallas_call(
        body, name="mm_dh2_res1bwd", grid=(L // tm, nk),
        in_specs=[pl.BlockSpec((tm, tk), lambda i, k: (i, k)), pl.BlockSpec((tk, D), lambda i, k: (k, 0)),
                  rows, rows, rows, pl.BlockSpec((8, D), lambda i, k: (0, 0)), vec, vec],
        out_specs=[rows, rows, pl.BlockSpec((8, D), lambda i, k: (0, 0))],
        out_shape=[jax.ShapeDtypeStruct((L, D), F32), jax.ShapeDtypeStruct((L, D), _MXU),
                   jax.ShapeDtypeStruct((8, D), F32)],
        scratch_shapes=[pltpu.VMEM((tm, D), F32)],
        compiler_params=_params(("arbitrary", "arbitrary")),
    )(df13, w13i, dr2, r1, out, modx, g1, b1)


def _merge_bwd(dout, a1, a2, p, bg, wo, ws, wg, dp):
    L = a1.shape[0]
    tm = TL

    def body(do_ref, a1_ref, a2_ref, g_ref, bg_ref, wo_ref, ws_ref, wg_ref, dp_any,
             dg_ref, da1_ref, da2_ref, st_ref, dys_ref, dym_ref):
        del dp_any

        @pl.when(pl.program_id(0) == 0)
        def _():
            st_ref[...] = jnp.zeros_like(st_ref)

        dm = _dot(do_ref[...], wo_ref[...], NT)
        gt = _sig(g_ref[...] + bg_ref[...])
        g1 = gt[:, :D]
        g2 = gt[:, D:]
        da1 = (dm * g1).astype(da1_ref.dtype)
        da2 = (dm * g2).astype(da2_ref.dtype)
        da1_ref[...] = da1
        da2_ref[...] = da2
        dg1 = dm * a1_ref[...] * g1 * (1.0 - g1)
        dg2 = dm * a2_ref[...] * g2 * (1.0 - g2)
        st_ref[0:1, 0:D] += _colsum(dg1)
        st_ref[0:1, D:2 * D] += _colsum(dg2)
        dg_ref[:, 0:D] = dg1.astype(dg_ref.dtype)
        dg_ref[:, D:2 * D] = dg2.astype(dg_ref.dtype)
        dys_ref[...] = _dot(da1, ws_ref[...], NT)
        dym_ref[...] = _dot(da2, wg_ref[...], NT)

    rows = pl.BlockSpec((tm, D), lambda i: (i, 0))
    gates = pl.BlockSpec((tm, 2 * D), lambda i: (i, OG // (2 * D)))
    f32s = jax.ShapeDtypeStruct((L, D), F32)
    mxus = jax.ShapeDtypeStruct((L, D), _MXU)
    return pl.pallas_call(
        body, name="merge_bwd", grid=(L // tm,),
        in_specs=[rows, rows, rows, gates, _cst((1, 2 * D)), _cst((D, D)), _cst((D, D)), _cst((D, D)),
                  pl.BlockSpec(memory_space=pl.ANY)],
        out_specs=[gates, rows, rows, _cst((8, 2 * D)), rows, rows],
        out_shape=[jax.ShapeDtypeStruct(dp.shape, dp.dtype), mxus, mxus,
                   jax.ShapeDtypeStruct((8, 2 * D), F32), f32s, f32s],
        input_output_aliases={8: 0},
        compiler_params=_params(("arbitrary",)),
    )(dout, a1, a2, p, bg, wo, ws, wg, dp)


def _res1_fwd(xn, out, modx, g, b):
    L = out.shape[0]

    def body(xn_ref, o_ref, mx_ref, g_ref, b_ref, r1_ref, h2_ref):
        r1 = ALPHA * xn_ref[...] + mx_ref[2:3, :] * o_ref[...]
        xhat, _ = _ln(r1)
        x1 = xhat * g_ref[...] + b_ref[...]
        r1_ref[...] = r1
        h2_ref[...] = (x1 * (1.0 + mx_ref[4:5, :]) + mx_ref[3:4, :]).astype(h2_ref.dtype)

    return pl.pallas_call(
        body, name="res1_fwd", grid=(L // TL,),
        in_specs=[_rt(D), _rt(D), _cst((8, D)), _cst((1, D)), _cst((1, D))],
        out_specs=[_rt(D), _rt(D)],
        out_shape=[jax.ShapeDtypeStruct((L, D), F32), jax.ShapeDtypeStruct((L, D), _MXU)],
        compiler_params=_params(("parallel",)),
    )(xn, out, modx, g, b)


HFF = DFF // 2


def _mm_f13_glu(h2, w13i):
    L = h2.shape[0]
    tm = 512

    def body(a_ref, b_ref, f_ref, ff_ref):
        f = _dot(a_ref[...], b_ref[...], NT)
        f_ref[...] = f
        f1 = f[:, :HFF]
        ff_ref[...] = (f1 * _sig(f1) * f[:, HFF:]).astype(ff_ref.dtype)

    return pl.pallas_call(
        body, name="mm_f13_glu", grid=(DFF // HFF, L // tm),
        in_specs=[pl.BlockSpec((tm, D), lambda j, i: (i, 0)), pl.BlockSpec((2 * HFF, D), lambda j, i: (j, 0))],
        out_specs=[pl.BlockSpec((tm, 2 * HFF), lambda j, i: (i, j)), pl.BlockSpec((tm, HFF), lambda j, i: (i, j))],
        out_shape=[jax.ShapeDtypeStruct((L, 2 * DFF), F32), jax.ShapeDtypeStruct((L, DFF), _MXU)],
        compiler_params=_params(("parallel", "parallel")),
    )(h2, w13i)


def _mm_dff_glu(do2, w_ff2_f, f13i):
    L = do2.shape[0]
    tm = 512

    def body(a_ref, b_ref, f_ref, o_ref):
        d = _dot(a_ref[...], b_ref[...], NT)
        f1 = f_ref[:, :HFF]
        s = _sig(f1)
        o_ref[:, :HFF] = (d * f_ref[:, HFF:] * (s * (1.0 + f1 * (1.0 - s)))).astype(o_ref.dtype)
        o_ref[:, HFF:] = (d * f1 * s).astype(o_ref.dtype)

    return pl.pallas_call(
        body, name="mm_dff_glu", grid=(DFF // HFF, L // tm),
        in_specs=[pl.BlockSpec((tm, D), lambda j, i: (i, 0)), pl.BlockSpec((HFF, D), lambda j, i: (j, 0)),
                  pl.BlockSpec((tm, 2 * HFF), lambda j, i: (i, j))],
        out_specs=pl.BlockSpec((tm, 2 * HFF), lambda j, i: (i, j)),
        out_shape=jax.ShapeDtypeStruct((L, 2 * DFF), _MXU),
        compiler_params=_params(("parallel", "parallel")),
    )(do2, w_ff2_f, f13i)


def _glu_fwd(f13):
    L = f13.shape[0]

    def body(f1_ref, f3_ref, o_ref):
        f1 = f1_ref[...]
        o_ref[...] = (f1 * _sig(f1) * f3_ref[...]).astype(o_ref.dtype)

    return pl.pallas_call(
        body, name="glu_fwd", grid=(L // TL,),
        in_specs=[_rt(DFF, 0), _rt(DFF, 1)], out_specs=_rt(DFF),
        out_shape=jax.ShapeDtypeStruct((L, DFF), _MXU),
        compiler_params=_params(("parallel",)),
    )(f13, f13)


def _glu_bwd(dff, f13):
    L = f13.shape[0]

    def body(d_ref, f1_ref, f3_ref, o_ref):
        f1 = f1_ref[...]
        s = _sig(f1)
        d = d_ref[...]
        o_ref[:, 0:DFF] = (d * f3_ref[...] * (s * (1.0 + f1 * (1.0 - s)))).astype(o_ref.dtype)
        o_ref[:, DFF:2 * DFF] = (d * f1 * s).astype(o_ref.dtype)

    return pl.pallas_call(
        body, name="glu_bwd", grid=(L // TL,),
        in_specs=[_rt(DFF), _rt(DFF, 0), _rt(DFF, 1)], out_specs=_rt(2 * DFF),
        out_shape=jax.ShapeDtypeStruct((L, 2 * DFF), _MXU),
        compiler_params=_params(("parallel",)),
    )(dff, f13, f13)


def _res2(r1, o2, tgt, modx, g1, b1, g2, b2):
    L = r1.shape[0]

    def body(r1_ref, o2_ref, t_ref, mx_ref, g1_ref, b1_ref, g2_ref, b2_ref,
             dr2_ref, do2_ref, st_ref, loss_ref):
        @pl.when(pl.program_id(0) == 0)
        def _():
            st_ref[...] = jnp.zeros_like(st_ref)
            loss_ref[...] = jnp.zeros_like(loss_ref)

        xh1, _ = _ln(r1_ref[...])
        x1 = xh1 * g1_ref[...] + b1_ref[...]
        o2 = o2_ref[...]
        g2x = mx_ref[5:6, :]
        xh2, rstd2 = _ln(ALPHA * x1 + g2x * o2)
        err = xh2 * g2_ref[...] + b2_ref[...] - t_ref[...]
        per_tok = jnp.mean(err * err, axis=-1, keepdims=True)
        loss_ref[...] += 0.5 * jnp.sum(per_tok, axis=0, keepdims=True)
        dy = err * (1.0 / D)
        st_ref[0:1, :] += _colsum(dy * xh2)
        st_ref[1:2, :] += _colsum(dy)
        dr2 = _ln_bwd(dy * g2_ref[...], xh2, rstd2)
        st_ref[2:3, :] += _colsum(dr2 * o2)
        dr2_ref[...] = dr2
        do2_ref[...] = (g2x * dr2).astype(do2_ref.dtype)

    return pl.pallas_call(
        body, name="res2", grid=(L // TL,),
        in_specs=[_rt(D), _rt(D), _rt(D), _cst((8, D))] + [_cst((1, D))] * 4,
        out_specs=[_rt(D), _rt(D), _cst((8, D)), _cst((8, 128))],
        out_shape=[jax.ShapeDtypeStruct((L, D), F32), jax.ShapeDtypeStruct((L, D), _MXU),
                   jax.ShapeDtypeStruct((8, D), F32), jax.ShapeDtypeStruct((8, 128), F32)],
        compiler_params=_params(("arbitrary",)),
    )(r1, o2, tgt, modx, g1, b1, g2, b2)


def _res1_bwd(dr2, dh2, r1, out, modx, g1, b1):
    L = r1.shape[0]

    def body(dr2_ref, dh2_ref, r1_ref, o_ref, mx_ref, g_ref, b_ref, dr1_ref, do_ref, st_ref):
        @pl.when(pl.program_id(0) == 0)
        def _():
            st_ref[...] = jnp.zeros_like(st_ref)

        xh1, rstd1 = _ln(r1_ref[...])
        x1 = xh1 * g_ref[...] + b_ref[...]
        dh2 = dh2_ref[...]
        dx1 = ALPHA * dr2_ref[...] + dh2 * (1.0 + mx_ref[4:5, :])
        st_ref[0:1, :] += _colsum(dh2 * x1)
        st_ref[1:2, :] += _colsum(dh2)
        st_ref[2:3, :] += _colsum(dx1 * xh1)
        st_ref[3:4, :] += _colsum(dx1)
        dr1 = _ln_bwd(dx1 * g_ref[...], xh1, rstd1)
        st_ref[4:5, :] += _colsum(dr1 * o_ref[...])
        dr1_ref[...] = dr1
        do_ref[...] = (mx_ref[2:3, :] * dr1).astype(do_ref.dtype)

    return pl.pallas_call(
        body, name="res1_bwd", grid=(L // TL,),
        in_specs=[_rt(D), _rt(D), _rt(D), _rt(D), _cst((8, D)), _cst((1, D)), _cst((1, D))],
        out_specs=[_rt(D), _rt(D), _cst((8, D))],
        out_shape=[jax.ShapeDtypeStruct((L, D), F32), jax.ShapeDtypeStruct((L, D), _MXU),
                   jax.ShapeDtypeStruct((8, D), F32)],
        compiler_params=_params(("arbitrary",)),
    )(dr2, dh2, r1, out, modx, g1, b1)


def _conv_bwd(dxf, dxb, p, conv_w8, conv_b, dp):
    RT = p.shape[0]
    chunks = _seq_chunks(RT - TL)

    def body(df_ref, db_ref, p_ref, w_ref, b_ref, dp_any, o_ref, dw_ref, dbias_ref, dpre_s):
        del dp_any
        w = w_ref[...]
        bias = b_ref[...]
        srow = lax.broadcasted_iota(jnp.int32, (8, 128), 0)
        dwacc = jnp.zeros((8, 128), F32)
        dbacc = jnp.zeros((1, 128), F32)
        for r0, first, last in chunks:
            taps = _conv_taps(p_ref, r0, first, last)
            pre = bias + sum(w[k:k + 1, :] * taps[k] for k in range(5))
            s = _sig(pre)
            dpre = (df_ref[pl.ds(r0, TL), :] + db_ref[pl.ds(r0, TL), :]) * (s * (1.0 + pre * (1.0 - s)))
            dpre_s[pl.ds(r0, TL), :] = dpre
            dbacc = dbacc + _colsum(dpre)
            for k in range(5):
                dwacc = dwacc + jnp.where(srow == k, _colsum(dpre * taps[k]), 0.0)
        for r0, first, last in chunks:
            taps = _conv_taps(dpre_s, r0, first, last)
            dx = sum(w[k:k + 1, :] * taps[4 - k] for k in range(5))
            o_ref[pl.ds(r0, TL), :] = dx.astype(o_ref.dtype)
        dw_ref[...] = dwacc
        dbias_ref[...] = jnp.broadcast_to(dbacc, (8, 128))

    cspec = pl.BlockSpec((RT, 128), lambda j: (0, j))
    wspec = pl.BlockSpec((8, 128), lambda j: (0, j))
    return pl.pallas_call(
        body, name="conv_bwd", grid=(12,),
        in_specs=[cspec, cspec, pl.BlockSpec((RT, 128), lambda j: (0, _xbc_colblk(j))),
                  wspec, pl.BlockSpec((1, 128), lambda j: (0, j)), pl.BlockSpec(memory_space=pl.ANY)],
        out_specs=[pl.BlockSpec((RT, 128), lambda j: (0, _xbc_colblk(j))), wspec, wspec],
        out_shape=[jax.ShapeDtypeStruct(dp.shape, dp.dtype), jax.ShapeDtypeStruct((8, 1536), F32),
                   jax.ShapeDtypeStruct((8, 1536), F32)],
        scratch_shapes=[pltpu.VMEM((RT, 128), F32)],
        input_output_aliases={5: 0},
        compiler_params=_params(("parallel",)),
    )(dxf, dxb, p, conv_w8, conv_b, dp)


def _dt_bwd(ddf, ddb, dp):
    RT = ddf.shape[0]

    def body(f_ref, b_ref, dp_any, o_ref, st_ref):
        del dp_any

        @pl.when(pl.program_id(0) == 0)
        def _():
            st_ref[...] = jnp.zeros_like(st_ref)

        s = f_ref[...] + b_ref[...]
        o_ref[...] = s.astype(o_ref.dtype)
        st_ref[0:1, :] += _colsum(s)

    return pl.pallas_call(
        body, name="dt_bwd", grid=(RT // TL,),
        in_specs=[_rt(128), _rt(128), pl.BlockSpec(memory_space=pl.ANY)],
        out_specs=[_rt(128, ODT // 128), _cst((8, 128))],
        out_shape=[jax.ShapeDtypeStruct(dp.shape, dp.dtype), jax.ShapeDtypeStruct((8, 128), F32)],
        input_output_aliases={2: 0},
        compiler_params=_params(("arbitrary",)),
    )(ddf, ddb, dp)


def _ln0_bwd(dh1, dr1, x, ctx, g, b, modx, modc):
    L = x.shape[0]
    nt = L // TL

    def body(dh_ref, dr1_ref, x_ref, c_ref, g_ref, b_ref, mx_ref, mc_ref, gx_ref, st_ref):
        i = pl.program_id(0)
        isc = i == nt

        @pl.when(i == 0)
        def _():
            st_ref[...] = jnp.zeros_like(st_ref)

        xin = jnp.where(isc, c_ref[...], x_ref[...])
        xhat, rstd = _ln(xin)
        xn = xhat * g_ref[...] + b_ref[...]
        sc = jnp.where(isc, mc_ref[1:2, :], mx_ref[1:2, :])
        dh = dh_ref[...]
        lat = jnp.where(isc, 0.0, 1.0)
        dxn = dh * (1.0 + sc) + (lat * ALPHA) * dr1_ref[...]
        tsh = _colsum(dh)
        tsc = _colsum(dh * xn)
        st_ref[0:1, :] += lat * tsh
        st_ref[1:2, :] += lat * tsc
        st_ref[2:3, :] += (1.0 - lat) * tsh
        st_ref[3:4, :] += (1.0 - lat) * tsc
        st_ref[4:5, :] += _colsum(dxn * xhat)
        st_ref[5:6, :] += _colsum(dxn)

        @pl.when(i < nt)
        def _():
            gx_ref[...] = _ln_bwd(dxn * g_ref[...], xhat, rstd)

    return pl.pallas_call(
        body, name="ln0_bwd", grid=(nt + 1,),
        in_specs=[_rt(D), _rtc(D, nt), _rtc(D, nt), _cst((TL, D)), _cst((1, D)), _cst((1, D)),
                  _cst((8, D)), _cst((8, D))],
        out_specs=[_rtc(D, nt), _cst((8, D))],
        out_shape=[jax.ShapeDtypeStruct((L, D), F32), jax.ShapeDtypeStruct((8, D), F32)],
        compiler_params=_params(("arbitrary",)),
    )(dh1, dr1, x, ctx, g, b, modx, modc)


def _perm_cols(w):
    pad = jnp.zeros((w.shape[0], NPJ - NNAT), w.dtype)
    return jnp.concatenate([w[:, 0:1024], w[:, 2592:3616], w[:, 3616:4640], w[:, 1024:2048],
                            w[:, 4640:6688], w[:, 2048:2304], w[:, 2304:2560], w[:, 2560:2592], pad],
                           axis=1)


SECTIONS = ((0, 1024, OZ), (1024, 2048, OXS), (2048, 2304, OB), (2304, 2560, OC), (2560, 2592, ODT),
            (2592, 3616, OU), (3616, 4640, OV), (4640, 6688, OG))


def _perm_from_blocks(ga):
    n = ga.shape[2]
    pieces = []
    for na, nb, _ in sorted(SECTIONS, key=lambda sec: sec[2]):
        for k in range(NDEV):
            lo, hi = max(na, k * n), min(nb, (k + 1) * n)
            if lo < hi:
                pieces.append(ga[k][:, lo - k * n:hi - k * n])
    pieces.append(jnp.zeros((ga.shape[1], NPJ - NNAT), ga.dtype))
    return jnp.concatenate(pieces, axis=1)


def _blocks_from_perm(gp, n):
    blocks = []
    for k in range(NDEV):
        pieces = []
        for na, nb, po in SECTIONS:
            lo, hi = max(na, k * n), min(nb, (k + 1) * n)
            if lo < hi:
                pieces.append(gp[:, po + lo - na:po + hi - na])
        blocks.append(jnp.concatenate(pieces, axis=1))
    return jnp.stack(blocks)


def _padded(n, row_align):
    unit = row_align * D
    return -(-n // unit) * unit if row_align else n


def _slab(arrs, rows, row_align=0):
    parts = []
    for a in arrs:
        f = a.reshape(-1)
        parts.append(jnp.pad(f, (0, _padded(f.shape[0], row_align) - f.shape[0])))
    flat = jnp.concatenate(parts)
    flat = jnp.pad(flat, (0, rows * D - flat.shape[0]))
    return flat.reshape(rows, D)


def _unslab(slab, shapes, row_align=0):
    out, off = [], 0
    for shp in shapes:
        n = 1
        for s in shp:
            n *= s
        r0, r1 = off // D, -(-(off + n) // D)
        out.append(slab[r0:r1].reshape(-1)[off - r0 * D:off - r0 * D + n].reshape(shp))
        off += _padded(n, row_align)
    return out


def _row(v):
    return v.reshape(1, -1)


def _t(a):
    return jnp.swapaxes(a, 0, 1)


def _pad_rows(a, rows):
    return jnp.pad(a, ((0, rows - a.shape[0]), (0, 0)))


BIG = ["w_in", "w_ssd_proj", "w_gm_proj", "w_out", "w_ff1", "w_ff3", "w_ff2"]
BIG_ROWS = 2304
BIG_ALIGN = 16
REPL = ["c_ctx", "ln0_g", "ln0_b", "b_ada", "conv_b", "dt_bias", "a_log", "d_skip", "ssd_norm_g",
        "gm_norm_g", "gm_norm_b", "w_spatial", "b_spatial", "b_gate", "ln1_g", "ln1_b", "ln2_g", "ln2_b"]
SMALL_ROWS = 160
WEIGHTS = ["c_ctx", "ln0_g", "ln0_b", "w_ada", "b_ada", "w_in", "conv_w", "conv_b", "dt_bias", "a_log",
           "d_skip", "ssd_norm_g", "gm_norm_g", "gm_norm_b", "w_spatial", "b_spatial", "b_gate",
           "w_ssd_proj", "w_gm_proj", "w_out", "ln1_g", "ln1_b", "w_ff1", "w_ff3", "w_ff2", "ln2_g", "ln2_b"]


def kernel(x, c, ctx, c_ctx, ln0_g, ln0_b, w_ada, b_ada, w_in, conv_w, conv_b, dt_bias, a_log, d_skip, ssd_norm_g, gm_norm_g, gm_norm_b, w_spatial, b_spatial, b_gate, w_ssd_proj, w_gm_proj, w_out, ln1_g, ln1_b, w_ff1, w_ff3, w_ff2, ln2_g, ln2_b, loss_target, m_c_ctx, m_ln0_g, m_ln0_b, m_w_ada, m_b_ada, m_w_in, m_conv_w, m_conv_b, m_dt_bias, m_a_log, m_d_skip, m_ssd_norm_g, m_gm_norm_g, m_gm_norm_b, m_w_spatial, m_b_spatial, m_b_gate, m_w_ssd_proj, m_w_gm_proj, m_w_out, m_ln1_g, m_ln1_b, m_w_ff1, m_w_ff3, m_w_ff2, m_ln2_g, m_ln2_b, v_c_ctx, v_ln0_g, v_ln0_b, v_w_ada, v_b_ada, v_w_in, v_conv_w, v_conv_b, v_dt_bias, v_a_log, v_d_skip, v_ssd_norm_g, v_gm_norm_g, v_gm_norm_b, v_w_spatial, v_b_spatial, v_b_gate, v_w_ssd_proj, v_w_gm_proj, v_w_out, v_ln1_g, v_ln1_b, v_w_ff1, v_w_ff3, v_w_ff2, v_ln2_g, v_ln2_b):
    W = dict(c_ctx=c_ctx, ln0_g=ln0_g, ln0_b=ln0_b, w_ada=w_ada, b_ada=b_ada, w_in=w_in, conv_w=conv_w,
             conv_b=conv_b, dt_bias=dt_bias, a_log=a_log, d_skip=d_skip, ssd_norm_g=ssd_norm_g,
             gm_norm_g=gm_norm_g, gm_norm_b=gm_norm_b, w_spatial=w_spatial, b_spatial=b_spatial,
             b_gate=b_gate, w_ssd_proj=w_ssd_proj, w_gm_proj=w_gm_proj, w_out=w_out, ln1_g=ln1_g,
             ln1_b=ln1_b, w_ff1=w_ff1, w_ff3=w_ff3, w_ff2=w_ff2, ln2_g=ln2_g, ln2_b=ln2_b)
    M = dict(c_ctx=m_c_ctx, ln0_g=m_ln0_g, ln0_b=m_ln0_b, w_ada=m_w_ada, b_ada=m_b_ada, w_in=m_w_in,
             conv_w=m_conv_w, conv_b=m_conv_b, dt_bias=m_dt_bias, a_log=m_a_log, d_skip=m_d_skip,
             ssd_norm_g=m_ssd_norm_g, gm_norm_g=m_gm_norm_g, gm_norm_b=m_gm_norm_b,
             w_spatial=m_w_spatial, b_spatial=m_b_spatial, b_gate=m_b_gate, w_ssd_proj=m_w_ssd_proj,
             w_gm_proj=m_w_gm_proj, w_out=m_w_out, ln1_g=m_ln1_g, ln1_b=m_ln1_b, w_ff1=m_w_ff1,
             w_ff3=m_w_ff3, w_ff2=m_w_ff2, ln2_g=m_ln2_g, ln2_b=m_ln2_b)
    V = dict(c_ctx=v_c_ctx, ln0_g=v_ln0_g, ln0_b=v_ln0_b, w_ada=v_w_ada, b_ada=v_b_ada, w_in=v_w_in,
             conv_w=v_conv_w, conv_b=v_conv_b, dt_bias=v_dt_bias, a_log=v_a_log, d_skip=v_d_skip,
             ssd_norm_g=v_ssd_norm_g, gm_norm_g=v_gm_norm_g, gm_norm_b=v_gm_norm_b,
             w_spatial=v_w_spatial, b_spatial=v_b_spatial, b_gate=v_b_gate, w_ssd_proj=v_w_ssd_proj,
             w_gm_proj=v_w_gm_proj, w_out=v_w_out, ln1_g=v_ln1_g, ln1_b=v_ln1_b, w_ff1=v_w_ff1,
             w_ff3=v_w_ff3, w_ff2=v_w_ff2, ln2_g=v_ln2_g, ln2_b=v_ln2_b)

    me = 4 * lax.axis_index("x") + 2 * lax.axis_index("y") + lax.axis_index("c")
    xl, cx, tgt = x[0], ctx[0], loss_target[0]
    L = xl.shape[0]
    assert cx.shape[0] == TL and L % TL == 0
    ada_n = w_ada.shape[2]
    cw_n = conv_w.shape[2]

    small1 = _pad_rows(jnp.concatenate([c, _slab([conv_w[0]], 1)], axis=0), 8)
    g1 = _all_gather(small1, "ag_small")
    c_all = g1[:, 0, :]
    conv_w_full = g1[:, 1, :5 * cw_n].reshape(NDEV, 5, cw_n).transpose(1, 0, 2).reshape(5, NDEV * cw_n)
    sq = w_ssd_proj.shape[1]
    ffr = w_ff2.shape[1]
    ffc = w_ff1.shape[2]
    late = [jnp.concatenate([w_ssd_proj[0], w_gm_proj[0], w_out[0], w_ff2[0]], axis=0).astype(_MXU),
            _t(w_ff1[0]).astype(_MXU), _t(w_ff3[0]).astype(_MXU)]

    c16 = _pad_rows(jnp.concatenate([c_all, _row(c_ctx)], axis=0), 16)
    b_ada_sh = lax.dynamic_slice(b_ada, (0, ada_n * me), (1, ada_n))
    modp = _ada_fwd(c16, w_ada[0], b_ada_sh)
    mod16 = _all_gather(modp, "ag_mod").transpose(1, 0, 2).reshape(16, NDEV * ada_n)

    ga, = _all_gather_multi([w_in[0].astype(_MXU)], "ag_w_in")
    ga, late, mod16 = lax.optimization_barrier((ga, late, mod16))
    lw_send, lw_recv, lw_src, lw_land, lw_token = _exchange_start(late, "ag_late_start", gather=True)
    w_in_p = _perm_from_blocks(ga)
    modx = _pad_rows(lax.dynamic_slice(mod16, (me, 0), (1, 6 * D)).reshape(6, D), 8) + lw_token[0, 0]
    modc = _pad_rows(mod16[8].reshape(6, D), 8)

    g0, b0 = _row(ln0_g), _row(ln0_b)
    xn, h1 = _ln0_fwd(xl, cx, g0, b0, modx, modc)
    p = _mm(h1, w_in_p, "nn", F32, "mm_p")
    conv_w8 = _pad_rows(conv_w_full, 8)
    xbc = _conv_fwd(p, conv_w8, conv_b)
    prm = _pad_rows(jnp.pad(jnp.stack([dt_bias.reshape(32), a_log.reshape(32)]), ((0, 0), (0, 96))), 8)
    yf, yb, hpf, hpb = _ssd2_fwd(xbc, p, prm)
    lw_land = _exchange_wait(lw_send, lw_recv, lw_src, lw_land, yf, "ag_late_wait", gather=True)
    fw_send, fw_recv, lw_land, fw_token = _forward_start(lw_land, "ag_fwd_start")
    dsk = _row(jnp.repeat(d_skip[0, 0] + d_skip[0, 1], HP)) + fw_token[0:1, 0:1]
    ws_m = w_spatial[0].astype(_MXU)
    bsT = jnp.pad(b_spatial[0].T, ((0, 0), (0, 120)))
    mixp = (dsk, ssd_norm_g, gm_norm_g, gm_norm_b, ws_m, bsT)
    yssd, ygm = _mix_fwd(yf, yb, p, xbc, *mixp)
    gb, gc1, gc2 = _forward_wait(fw_send, fw_recv, lw_land, yssd, "ag_fwd_wait")

    gb, gc1, gc2 = lax.optimization_barrier(
        [lax.dynamic_update_index_in_dim(g, mine, me, 0) for g, mine in zip((gb, gc1, gc2), late)])
    w_ssd_f = gb[:, 0:sq].reshape(NDEV * sq, D)
    w_gm_f = gb[:, sq:2 * sq].reshape(NDEV * sq, D)
    w_out_f = gb[:, 2 * sq:3 * sq].reshape(NDEV * sq, D)
    w_ff2_f = gb[:, 3 * sq:3 * sq + ffr].reshape(NDEV * ffr, D)
    assert HFF == (NDEV // 2) * ffc
    hd = NDEV // 2
    w13i = jnp.concatenate([g[t * hd:(t + 1) * hd].reshape(HFF, D) for t in range(2) for g in (gc1, gc2)],
                           axis=0)
    a1, a2, merged, out, r1, h2 = _merge_fwd(yssd, ygm, p, b_gate, w_ssd_f, w_gm_f, w_out_f,
                                             xn, modx, ln1_g, ln1_b)
    f13, ff = _mm_f13_glu(h2, w13i)

    dr2, do2, st2, loss_slab = _mm_o2_res2(ff, w_ff2_f, r1, tgt, modx, ln1_g, ln1_b, ln2_g, ln2_b)
    loss = lax.psum(loss_slab[0, 0], ("x", "y", "c"))
    df13 = _mm_dff_glu(do2, w_ff2_f, f13)
    dw_ff2 = _mm(ff, do2, "tn", _MXU, "mm_dw_ff2")
    dw13i = _mm(df13, h2, "tn", _MXU, "mm_dw13")

    def owner_blocks(first):
        return jnp.concatenate([dw13i[t * 2 * HFF + first:t * 2 * HFF + first + HFF].reshape(NDEV // 2, ffc, D)
                                for t in range(2)], axis=0)

    xff = [dw_ff2.reshape(NDEV, ffr, D), owner_blocks(0), owner_blocks(HFF)]
    ff_send, ff_recv, ff_src, ff_land, ff_token = _exchange_start(xff, "xchg_ff_start")
    modx = modx + ff_token[0, 0]
    dr1, dout, st1 = _mm_dh2_res1bwd(df13, w13i, dr2, r1, out, modx, ln1_g, ln1_b)
    dw_out = _mm(merged, dout, "tn", _MXU, "mm_dw_out")
    dp = jnp.zeros((L + TL, NPJ), _MXU)
    dp, da1, da2, stg, dys, dym = _merge_bwd(dout, a1, a2, p, b_gate, w_out_f, w_ssd_f, w_gm_f, dp)
    dw_ssd = _mm(yssd, da1, "tn", _MXU, "mm_dw_ssd")
    dw_gm = _mm(ygm, da2, "tn", _MXU, "mm_dw_gm")
    xsq = [jnp.concatenate([dw_ssd.reshape(NDEV, sq, D), dw_gm.reshape(NDEV, sq, D),
                            dw_out.reshape(NDEV, sq, D)], axis=1)]
    sq_send, sq_recv, sq_src, sq_land, sq_token = _exchange_start(xsq, "xchg_sq_start")
    mixp = (dsk + sq_token[0:1, 0:1],) + mixp[1:]
    dp, dyd, stm, dws, dbsT = _mix_bwd(dys, dym, yf, yb, p, xbc, dp, *mixp)
    dxf, dxb, ddf, ddb, sts = _ssd2_bwd(xbc, p, prm, dsk, dyd, hpf, hpb)
    dp, dcw, dcb = _conv_bwd(dxf, dxb, p, conv_w8, conv_b, dp)
    dp, std = _dt_bwd(ddf, ddb, dp)
    hw = D // 2
    xin_a = [_blocks_from_perm(_mm(h1[:, :hw], dp, "tn", _MXU, "mm_dw_in_a"), w_in.shape[2])]
    ina_send, ina_recv, ina_src, ina_land, ina_token = _exchange_start(xin_a, "xchg_in_a_start")
    h1b, ina_token = lax.optimization_barrier((h1[:, hw:], ina_token))
    xin_b = [_blocks_from_perm(_mm(h1b, dp, "tn", _MXU, "mm_dw_in_b"), w_in.shape[2])]
    inb_send, inb_recv, inb_src, inb_land, inb_token = _exchange_start(xin_b, "xchg_in_b_start")
    dp, inb_token = lax.optimization_barrier((dp, inb_token))
    dh1 = _mm(dp, w_in_p, "nt", F32, "mm_dh1")
    modx = modx + (ina_token[0, 0] + inb_token[0, 0])
    grad_x, st0 = _ln0_bwd(dh1, dr1, xl, cx, g0, b0, modx, modc)

    zero = jnp.zeros((D,), F32)
    dmod = jnp.stack([jnp.concatenate([st0[0], st0[1], st1[4], st1[1], st1[0], st2[2]]),
                      jnp.concatenate([st0[2], st0[3], zero, zero, zero, zero])])
    g16 = _all_gather(_pad_rows(dmod, 8), "ag_dmod")[:, 0:2, :].reshape(16, 6 * D)
    g16_sh = lax.dynamic_slice(g16, (0, ada_n * me), (16, ada_n))
    c16b = jnp.stack([c_all, jnp.broadcast_to(_row(c_ctx), (NDEV, D))], axis=1).reshape(16, D)
    dw_ada, db_ada8, dcc8 = _ada_bwd(c16b, g16, g16_sh, w_ada[0])

    part = dict(
        c_ctx=dcc8[0], ln0_g=st0[4], ln0_b=st0[5], conv_w=dcw[0:5], conv_b=dcb[0],
        dt_bias=std[0, 0:32], a_log=sts[0, 0:32], d_skip=jnp.tile(sts[1, 0:16], 2),
        ssd_norm_g=stm[0], gm_norm_g=stm[1], gm_norm_b=stm[2], w_spatial=dws,
        b_spatial=dbsT[:, 0:8].T, b_gate=stg[0], ln1_g=st1[2], ln1_b=st1[3], ln2_g=st2[0], ln2_b=st2[1])
    pnames = list(part)
    psum8 = _sum8(_all_gather(_slab([part[n] for n in pnames], SMALL_ROWS), "ag_smallgrads"), "sum_smallgrads")
    small = dict(zip(pnames, _unslab(psum8, [part[n].shape for n in pnames])))
    grads = {n: small[n].reshape(W[n].shape) for n in pnames if n != "conv_w"}
    grads["conv_w"] = lax.dynamic_slice(small["conv_w"], (0, cw_n * me), (5, cw_n)).reshape(conv_w.shape)
    grads["b_ada"] = db_ada8[0:1]
    grads["w_ada"] = dw_ada.reshape(w_ada.shape)

    delta, new_m, new_v = {}, {}, {}

    def adam_group(names, rows, tag, align=0):
        shapes = [W[n].shape for n in names]
        outs = _adamw(*[_slab([src[n] for n in names], rows, align) for src in (grads, W, M, V)], tag)
        for res, slab in zip((delta, new_m, new_v), outs):
            for n, a in zip(names, _unslab(slab, shapes, align)):
                res[n] = a

    adam_group(REPL + ["conv_w"], SMALL_ROWS, "adamw_small")
    res = _adamw(grads["w_ada"][0], w_ada[0], m_w_ada[0], v_w_ada[0], "adamw_w_ada")
    delta["w_ada"], new_m["w_ada"], new_v["w_ada"] = [a[None] for a in res]

    rff = _exchange_wait(ff_send, ff_recv, ff_src, ff_land, st0, "xchg_ff_wait")
    rsq = _exchange_wait(sq_send, sq_recv, sq_src, sq_land, rff[0], "xchg_sq_wait")
    rin_a = _exchange_wait(ina_send, ina_recv, ina_src, ina_land, delta["ln2_b"], "xchg_in_a_wait")
    rin_b = _exchange_wait(inb_send, inb_recv, inb_src, inb_land, rin_a[0], "xchg_in_b_wait")
    rin = jnp.concatenate([rin_a[0], rin_b[0]], axis=1)

    def own(blocks):
        return lax.dynamic_index_in_dim(blocks, me, 0, keepdims=False)

    own_in = jnp.concatenate([own(xin_a[0]), own(xin_b[0])], axis=0)
    for n, r8, mine, row0, tr in (
            ("w_ff2", rff[0], own(xff[0]), 0, ffr // 2), ("w_ssd_proj", rsq[0], own(xsq[0]), 0, sq),
            ("w_gm_proj", rsq[0], own(xsq[0]), sq, sq), ("w_out", rsq[0], own(xsq[0]), 2 * sq, sq),
            ("w_in", rin, own_in, 0, 256)):
        res = _adamw_sum(r8, mine, W[n][0], M[n][0], V[n][0], row0, tr, "adamw_" + n)
        grads[n], delta[n], new_m[n], new_v[n] = [a[None] for a in res]
    for n, r8, mine in (("w_ff1", rff[1], own(xff[1])), ("w_ff3", rff[2], own(xff[2]))):
        res = _adamw_sum(r8, mine, _t(W[n][0]), _t(M[n][0]), _t(V[n][0]), 0, ffc // 2, "adamw_" + n)
        grads[n], delta[n], new_m[n], new_v[n] = [_t(a)[None] for a in res]

    return (loss, grad_x[None], *[grads[n] for n in WEIGHTS], *[delta[n] for n in WEIGHTS],
            *[new_m[n] for n in WEIGHTS], *[new_v[n] for n in WEIGHTS])
```

```python
import functools

import jax
import jax.numpy as jnp
from jax import lax
from jax.experimental import pallas as pl
from jax.experimental.pallas import tpu as pltpu

_MXU = jnp.bfloat16
F32 = jnp.float32
D = 1024
TL = 256
Q = 128
NH, HP, NS, HPG = 16, 64, 128, 8
DFF = 2816
ALPHA = 2.0 ** 0.25
EPS = 1e-5
OZ, OU, OV, OXS, OG, OB, OC, ODT, NPJ = 0, 1024, 2048, 3072, 4096, 6144, 6400, 6656, 6912
NNAT = 6688
NDEV = 8
ADAM_LR, ADAM_B1, ADAM_B2, ADAM_EPS, ADAM_WD, ADAM_STEP = 1e-3, 0.9, 0.999, 1e-8, 0.01, 10
VMEM_LIMIT = 48 * 1024 * 1024

NN = ((1,), (0,))
NT = ((1,), (1,))
TN = ((0,), (0,))
MESH = pl.DeviceIdType.MESH


def _dot(a, b, dims):
    return lax.dot_general(a.astype(_MXU), b.astype(_MXU), (dims, ((), ())),
                           preferred_element_type=F32)


def _tile(n, cands):
    for c in cands:
        if n % c == 0:
            return c
    return n


def _divisor_tile(n, cap, mult):
    best = n
    for t in range(mult, min(n, cap) + 1, mult):
        if n % t == 0:
            best = t
    return best


def _params(sem):
    return pltpu.CompilerParams(dimension_semantics=sem, vmem_limit_bytes=VMEM_LIMIT)


def _cst(shape):
    nd = len(shape)
    return pl.BlockSpec(shape, lambda *_: (0,) * nd)


def _rt(w, cb=0, rows=TL):
    return pl.BlockSpec((rows, w), lambda i: (i, cb))


def _rtc(w, nt, cb=0):
    return pl.BlockSpec((TL, w), lambda i: (jnp.minimum(i, nt - 1), cb))


def _sig(x):
    return jax.nn.sigmoid(x)


def _softplus(x):
    return jnp.maximum(x, 0.0) + jnp.log1p(jnp.exp(-jnp.abs(x)))


_G0, _G1 = 0.7978845608028654, 0.044715


def _gelu(x):
    t = jnp.tanh(_G0 * (x + _G1 * x * x * x))
    return 0.5 * x * (1.0 + t), t


def _gelu_grad(x, t):
    return 0.5 * (1.0 + t) + 0.5 * x * (1.0 - t * t) * _G0 * (1.0 + 3.0 * _G1 * x * x)


def _ln(r):
    mu = jnp.mean(r, axis=-1, keepdims=True)
    xc = r - mu
    var = jnp.mean(xc * xc, axis=-1, keepdims=True)
    rstd = lax.rsqrt(var + EPS)
    return xc * rstd, rstd


def _ln_bwd(dyh, xhat, rstd):
    return rstd * (dyh - jnp.mean(dyh, axis=-1, keepdims=True)
                   - xhat * jnp.mean(dyh * xhat, axis=-1, keepdims=True))


def _colsum(v):
    return jnp.sum(v, axis=0, keepdims=True)


def _sum11(v):
    return jnp.sum(jnp.sum(v, axis=1, keepdims=True), axis=0, keepdims=True)


def _cumsum_rows(a, rev):
    n = a.shape[0]
    row = lax.broadcasted_iota(jnp.int32, a.shape, 0)
    s = 1
    while s < n:
        if rev:
            a = a + jnp.where(row < n - s, pltpu.roll(a, n - s, 0), 0.0)
        else:
            a = a + jnp.where(row >= s, pltpu.roll(a, s, 0), 0.0)
        s *= 2
    return a


def _mm(a, b, mode, out_dtype, name):
    if mode == "tn":
        K, M = a.shape
    else:
        M, K = a.shape
    N = b.shape[0] if mode == "nt" else b.shape[1]
    tm = _divisor_tile(M, 1408, 128) if mode == "tn" else _divisor_tile(M, 1088, 16)
    tn = _divisor_tile(N, 1408, 128)
    tk = _divisor_tile(K, 2304, 128)
    nk = K // tk
    dims = {"nn": NN, "nt": NT, "tn": TN}[mode]
    use_acc = nk > 1 and out_dtype != F32

    def body(a_ref, b_ref, o_ref, *acc):
        prod = _dot(a_ref[...], b_ref[...], dims)
        if nk == 1:
            o_ref[...] = prod.astype(o_ref.dtype)
            return
        acc_ref = acc[0] if use_acc else o_ref
        k = pl.program_id(2)

        @pl.when(k == 0)
        def _():
            acc_ref[...] = prod

        if use_acc:
            @pl.when((k > 0) & (k < nk - 1))
            def _():
                acc_ref[...] += prod

            @pl.when(k == nk - 1)
            def _():
                o_ref[...] = (acc_ref[...] + prod).astype(o_ref.dtype)
        else:
            @pl.when(k > 0)
            def _():
                o_ref[...] += prod

    if mode == "tn":
        a_spec = pl.BlockSpec((tk, tm), lambda i, j, k: (k, i))
    else:
        a_spec = pl.BlockSpec((tm, tk), lambda i, j, k: (i, k))
    if mode == "nt":
        b_spec = pl.BlockSpec((tn, tk), lambda i, j, k: (j, k))
    else:
        b_spec = pl.BlockSpec((tk, tn), lambda i, j, k: (k, j))
    return pl.pallas_call(
        body, name=name, grid=(M // tm, N // tn, nk),
        in_specs=[a_spec, b_spec],
        out_specs=pl.BlockSpec((tm, tn), lambda i, j, k: (i, j)),
        out_shape=jax.ShapeDtypeStruct((M, N), out_dtype),
        scratch_shapes=[pltpu.VMEM((tm, tn), F32)] if use_acc else [],
        compiler_params=_params(("parallel", "parallel", "arbitrary")),
    )(a, b)


def _all_gather(x, name):
    def body(x_ref, out_ref, send_sems, recv_sems, local_sem):
        mx, my, mc = lax.axis_index("x"), lax.axis_index("y"), lax.axis_index("c")
        me, sibling = (mx, my, mc), (mx, my, 1 - mc)
        chips = [(1 - mx, my), (mx, 1 - my), (1 - mx, 1 - my)]

        def slot(px, py, pc):
            return out_ref.at[4 * px + 2 * py + pc]

        def copy(k, block, to, src=None):
            return pltpu.make_async_remote_copy(
                src_ref=slot(*block) if src is None else src, dst_ref=slot(*block),
                send_sem=send_sems.at[k], recv_sem=recv_sems.at[k],
                device_id=to, device_id_type=MESH)

        mine = pltpu.make_async_copy(x_ref, slot(*me), local_sem)
        mine.start()
        first = [copy(0, me, sibling, src=x_ref)]
        first += [copy(1 + j, me, (*chip, mc), src=x_ref) for j, chip in enumerate(chips)]
        for cp in first:
            cp.start()
        passed = [copy(4 + j, (*chip, mc), sibling) for j, chip in enumerate(chips)]
        for j, chip in enumerate(chips):
            copy(1 + j, (*chip, mc), me).wait_recv()
            passed[j].start()
        copy(0, sibling, me).wait_recv()
        for j, chip in enumerate(chips):
            copy(4 + j, (*chip, 1 - mc), me).wait_recv()
        for cp in first + passed:
            cp.wait_send()
        mine.wait()

    return pl.pallas_call(
        body, name=name,
        out_shape=jax.ShapeDtypeStruct((NDEV,) + x.shape, x.dtype),
        in_specs=[pl.BlockSpec(memory_space=pl.ANY)],
        out_specs=pl.BlockSpec(memory_space=pl.ANY),
        scratch_shapes=[pltpu.SemaphoreType.DMA((7,)), pltpu.SemaphoreType.DMA((7,)),
                        pltpu.SemaphoreType.DMA],
    )(x)


def _owner_exchange(g, name):
    def body(g_ref, out_ref, send_sems, recv_sems, local_sem):
        mx, my, mc = lax.axis_index("x"), lax.axis_index("y"), lax.axis_index("c")
        local = pltpu.make_async_copy(g_ref.at[4 * mx + 2 * my + mc], out_ref.at[0], local_sem)
        local.start()
        copies = []
        for f in range(1, NDEV):
            px = 1 - mx if (f >> 2) & 1 else mx
            py = 1 - my if (f >> 1) & 1 else my
            pc = 1 - mc if f & 1 else mc
            cp = pltpu.make_async_remote_copy(
                src_ref=g_ref.at[4 * px + 2 * py + pc], dst_ref=out_ref.at[f],
                send_sem=send_sems.at[f - 1], recv_sem=recv_sems.at[f - 1],
                device_id=(px, py, pc), device_id_type=MESH)
            cp.start()
            copies.append(cp)
        for cp in copies:
            cp.wait_recv()
        for cp in copies:
            cp.wait_send()
        local.wait()

    return pl.pallas_call(
        body, name=name,
        out_shape=jax.ShapeDtypeStruct(g.shape, g.dtype),
        in_specs=[pl.BlockSpec(memory_space=pl.ANY)],
        out_specs=pl.BlockSpec(memory_space=pl.ANY),
        scratch_shapes=[pltpu.SemaphoreType.DMA((7,)), pltpu.SemaphoreType.DMA((7,)),
                        pltpu.SemaphoreType.DMA],
    )(g)


def _any_specs(n):
    return [pl.BlockSpec(memory_space=pl.ANY)] * n


def _all_gather_multi(xs, name):
    na = len(xs)

    def body(*refs):
        x_refs, out_refs = refs[:na], refs[na:2 * na]
        send_sems, recv_sems, local_sems = refs[2 * na:]
        mx, my, mc = lax.axis_index("x"), lax.axis_index("y"), lax.axis_index("c")
        me, sibling = (mx, my, mc), (mx, my, 1 - mc)
        chips = [(1 - mx, my), (mx, 1 - my), (1 - mx, 1 - my)]

        def copy(a, k, block, to, src=None):
            slot = out_refs[a].at[4 * block[0] + 2 * block[1] + block[2]]
            return pltpu.make_async_remote_copy(
                src_ref=slot if src is None else src, dst_ref=slot,
                send_sem=send_sems.at[7 * a + k], recv_sem=recv_sems.at[7 * a + k],
                device_id=to, device_id_type=MESH)

        mine = [pltpu.make_async_copy(x_refs[a], out_refs[a].at[4 * mx + 2 * my + mc], local_sems.at[a])
                for a in range(na)]
        for cp in mine:
            cp.start()
        first = []
        for a in range(na):
            first.append(copy(a, 0, me, sibling, src=x_refs[a]))
            first += [copy(a, 1 + j, me, (*chip, mc), src=x_refs[a]) for j, chip in enumerate(chips)]
        for cp in first:
            cp.start()
        passed = []
        for a in range(na):
            for j, chip in enumerate(chips):
                copy(a, 1 + j, (*chip, mc), me).wait_recv()
                fwd = copy(a, 4 + j, (*chip, mc), sibling)
                fwd.start()
                passed.append(fwd)
        for a in range(na):
            copy(a, 0, sibling, me).wait_recv()
            for j, chip in enumerate(chips):
                copy(a, 4 + j, (*chip, 1 - mc), me).wait_recv()
        for cp in first + passed:
            cp.wait_send()
        for cp in mine:
            cp.wait()

    return pl.pallas_call(
        body, name=name,
        out_shape=[jax.ShapeDtypeStruct((NDEV,) + x.shape, x.dtype) for x in xs],
        in_specs=_any_specs(na), out_specs=_any_specs(na),
        scratch_shapes=[pltpu.SemaphoreType.DMA((7 * na,)), pltpu.SemaphoreType.DMA((7 * na,)),
                        pltpu.SemaphoreType.DMA((na,))],
    )(*xs)


def _owner_exchange_multi(gs, name):
    na = len(gs)

    def body(*refs):
        g_refs, out_refs = refs[:na], refs[na:2 * na]
        send_sems, recv_sems, local_sems = refs[2 * na:]
        mx, my, mc = lax.axis_index("x"), lax.axis_index("y"), lax.axis_index("c")
        locals_ = [pltpu.make_async_copy(g_refs[a].at[4 * mx + 2 * my + mc], out_refs[a].at[0], local_sems.at[a])
                   for a in range(na)]
        for cp in locals_:
            cp.start()
        copies = []
        for a in range(na):
            for f in range(1, NDEV):
                px = 1 - mx if (f >> 2) & 1 else mx
                py = 1 - my if (f >> 1) & 1 else my
                pc = 1 - mc if f & 1 else mc
                cp = pltpu.make_async_remote_copy(
                    src_ref=g_refs[a].at[4 * px + 2 * py + pc], dst_ref=out_refs[a].at[f],
                    send_sem=send_sems.at[7 * a + f - 1], recv_sem=recv_sems.at[7 * a + f - 1],
                    device_id=(px, py, pc), device_id_type=MESH)
                cp.start()
                copies.append(cp)
        for cp in copies:
            cp.wait_recv()
        for cp in copies:
            cp.wait_send()
        for cp in locals_:
            cp.wait()

    return pl.pallas_call(
        body, name=name,
        out_shape=[jax.ShapeDtypeStruct(g.shape, g.dtype) for g in gs],
        in_specs=_any_specs(na), out_specs=_any_specs(na),
        scratch_shapes=[pltpu.SemaphoreType.DMA((7 * na,)), pltpu.SemaphoreType.DMA((7 * na,)),
                        pltpu.SemaphoreType.DMA((na,))],
    )(*gs)


def _adamw_sum(r8, own, w, m, v, row0, tr, name):
    R, C = w.shape
    assert row0 % tr == 0
    blk0 = row0 // tr
    bc1 = 1.0 - ADAM_B1 ** ADAM_STEP
    bc2 = 1.0 - ADAM_B2 ** ADAM_STEP

    def body(r_ref, *refs):
        if own is None:
            gg = r_ref[0].astype(F32)
        else:
            gg = refs[0][...].astype(F32)
            refs = refs[1:]
        w_ref, m_ref, v_ref, g_ref, d_ref, mo_ref, vo_ref = refs
        for k in range(1, NDEV):
            gg = gg + r_ref[k].astype(F32)
        mn = ADAM_B1 * m_ref[...] + (1.0 - ADAM_B1) * gg
        vn = ADAM_B2 * v_ref[...] + (1.0 - ADAM_B2) * (gg * gg)
        mh = mn / bc1
        vh = vn / bc2
        g_ref[...] = gg
        d_ref[...] = -ADAM_LR * (mh / (jnp.sqrt(vh) + ADAM_EPS) + ADAM_WD * w_ref[...])
        mo_ref[...] = mn
        vo_ref[...] = vn

    spec = pl.BlockSpec((tr, C), lambda i: (i, 0))
    sh = jax.ShapeDtypeStruct((R, C), F32)
    own_ops = [] if own is None else [own]
    own_specs = [] if own is None else [pl.BlockSpec((tr, C), lambda i: (i + blk0, 0))]
    return pl.pallas_call(
        body, name=name, grid=(R // tr,),
        in_specs=[pl.BlockSpec((NDEV, tr, C), lambda i: (0, i + blk0, 0))] + own_specs + [spec, spec, spec],
        out_specs=[spec] * 4, out_shape=[sh] * 4, compiler_params=_params(("parallel",)),
    )(r8, *own_ops, w, m, v)


_HBM = pl.BlockSpec(memory_space=pltpu.HBM)
_SEM = pl.BlockSpec(memory_space=pltpu.SEMAPHORE)
_EFFECT = pltpu.SideEffectType.DATAFLOW_SIDE_EFFECTING


def _exchange_copies(g_refs, land_refs, send_sems, recv_sems, gather):
    mx, my, mc = lax.axis_index("x"), lax.axis_index("y"), lax.axis_index("c")
    copies = []
    for a in range(len(g_refs)):
        for f in ((1, 2, 4, 6) if gather else range(1, NDEV)):
            px = 1 - mx if (f >> 2) & 1 else mx
            py = 1 - my if (f >> 1) & 1 else my
            pc = 1 - mc if f & 1 else mc
            src = g_refs[a] if gather else g_refs[a].at[4 * px + 2 * py + pc]
            dst = land_refs[a].at[4 * mx + 2 * my + mc] if gather else land_refs[a].at[f]
            copies.append(pltpu.make_async_remote_copy(
                src_ref=src, dst_ref=dst,
                send_sem=send_sems.at[7 * a + f - 1], recv_sem=recv_sems.at[7 * a + f - 1],
                device_id=(px, py, pc), device_id_type=MESH))
    return copies


def _exchange_start(gs, name, gather=False):
    na = len(gs)

    def body(*refs):
        for cp in _exchange_copies(refs[:na], refs[na:2 * na], refs[2 * na], refs[2 * na + 1], gather):
            cp.start()
        refs[-1][...] = jnp.zeros_like(refs[-1])

    hbm = [pltpu.HBM(g.shape, g.dtype) for g in gs]
    land_shapes = [((NDEV,) + g.shape) if gather else g.shape for g in gs]
    lands = [pltpu.with_memory_space_constraint(lax.empty(shp, g.dtype), pltpu.HBM)
             for shp, g in zip(land_shapes, gs)]
    hbm_land = [pltpu.HBM(shp, g.dtype) for shp, g in zip(land_shapes, gs)]
    outs = pl.pallas_call(
        body, name=name,
        out_shape=(pltpu.SemaphoreType.DMA((7 * na,)), pltpu.SemaphoreType.DMA((7 * na,)), *hbm, *hbm_land,
                   jax.ShapeDtypeStruct((8, 128), F32)),
        in_specs=[_HBM] * (2 * na),
        out_specs=(_SEM, _SEM, *([_HBM] * (2 * na)), pl.BlockSpec(memory_space=pltpu.VMEM)),
        input_output_aliases={i: 2 + i for i in range(2 * na)},
        compiler_params=pltpu.CompilerParams(has_side_effects=_EFFECT),
    )(*[pltpu.with_memory_space_constraint(g, pltpu.HBM) for g in gs], *lands)
    return outs[0], outs[1], outs[2:2 + na], outs[2 + na:2 + 2 * na], outs[-1]


def _forward_copies(land_refs, send_sems, recv_sems):
    mx, my, mc = lax.axis_index("x"), lax.axis_index("y"), lax.axis_index("c")
    copies = []
    for a in range(len(land_refs)):
        for j, (fx, fy) in enumerate(((0, 1), (1, 0), (1, 1))):
            px = 1 - mx if fx else mx
            py = 1 - my if fy else my
            blk = land_refs[a].at[4 * px + 2 * py + mc]
            copies.append(pltpu.make_async_remote_copy(
                src_ref=blk, dst_ref=blk, send_sem=send_sems.at[3 * a + j], recv_sem=recv_sems.at[3 * a + j],
                device_id=(mx, my, 1 - mc), device_id_type=MESH))
    return copies


def _forward_start(lands, name):
    na = len(lands)

    def body(*refs):
        for cp in _forward_copies(refs[:na], refs[na], refs[na + 1]):
            cp.start()
        refs[-1][...] = jnp.zeros_like(refs[-1])

    outs = pl.pallas_call(
        body, name=name,
        out_shape=(pltpu.SemaphoreType.DMA((3 * na,)), pltpu.SemaphoreType.DMA((3 * na,)),
                   *[pltpu.HBM(g.shape, g.dtype) for g in lands], jax.ShapeDtypeStruct((8, 128), F32)),
        in_specs=[_HBM] * na,
        out_specs=(_SEM, _SEM, *([_HBM] * na), pl.BlockSpec(memory_space=pltpu.VMEM)),
        input_output_aliases={i: 2 + i for i in range(na)},
        compiler_params=pltpu.CompilerParams(has_side_effects=_EFFECT),
    )(*lands)
    return outs[0], outs[1], outs[2:2 + na], outs[-1]


def _forward_wait(send_sems, recv_sems, lands, after, name):
    na = len(lands)

    def body(*refs):
        for cp in _forward_copies(refs[:na], refs[na], refs[na + 1]):
            cp.wait_send()
            cp.wait_recv()

    return pl.pallas_call(
        body, name=name,
        out_shape=tuple(pltpu.HBM(g.shape, g.dtype) for g in lands),
        in_specs=[_HBM] * na + [_SEM, _SEM, pl.BlockSpec(memory_space=pl.ANY)],
        out_specs=tuple([_HBM] * na),
        input_output_aliases={i: i for i in range(na)},
        compiler_params=pltpu.CompilerParams(has_side_effects=_EFFECT),
    )(*lands, send_sems, recv_sems, after)


def _exchange_wait(send_sems, recv_sems, g_thru, land_thru, after, name, gather=False):
    na = len(g_thru)

    def body(*refs):
        for cp in _exchange_copies(refs[:na], refs[na:2 * na], refs[2 * na], refs[2 * na + 1], gather):
            cp.wait_send()
            cp.wait_recv()

    outs = pl.pallas_call(
        body, name=name,
        out_shape=tuple(pltpu.HBM(g.shape, g.dtype) for g in list(g_thru) + list(land_thru)),
        in_specs=[_HBM] * (2 * na) + [_SEM, _SEM, pl.BlockSpec(memory_space=pl.ANY)],
        out_specs=tuple([_HBM] * (2 * na)),
        input_output_aliases={i: i for i in range(2 * na)},
        compiler_params=pltpu.CompilerParams(has_side_effects=_EFFECT),
    )(*g_thru, *land_thru, send_sems, recv_sems, after)
    return outs[na:]


def _sum8(r, name):
    _, R, C = r.shape
    tr = _tile(R, (256, 160, 128, 64, 32, 16, 8))

    def body(r_ref, o_ref):
        acc = r_ref[0].astype(F32)
        for k in range(1, NDEV):
            acc = acc + r_ref[k].astype(F32)
        o_ref[...] = acc

    return pl.pallas_call(
        body, name=name, grid=(R // tr,),
        in_specs=[pl.BlockSpec((NDEV, tr, C), lambda i: (0, i, 0))],
        out_specs=pl.BlockSpec((tr, C), lambda i: (i, 0)),
        out_shape=jax.ShapeDtypeStruct((R, C), F32),
        compiler_params=_params(("parallel",)),
    )(r)


def _adamw(g, w, m, v, name):
    R, C = g.shape
    tr = _tile(R, (256, 160, 128, 64, 32, 16, 8))
    bc1 = 1.0 - ADAM_B1 ** ADAM_STEP
    bc2 = 1.0 - ADAM_B2 ** ADAM_STEP

    def body(g_ref, w_ref, m_ref, v_ref, d_ref, mo_ref, vo_ref):
        gg = g_ref[...]
        mn = ADAM_B1 * m_ref[...] + (1.0 - ADAM_B1) * gg
        vn = ADAM_B2 * v_ref[...] + (1.0 - ADAM_B2) * (gg * gg)
        mh = mn / bc1
        vh = vn / bc2
        d_ref[...] = -ADAM_LR * (mh / (jnp.sqrt(vh) + ADAM_EPS) + ADAM_WD * w_ref[...])
        mo_ref[...] = mn
        vo_ref[...] = vn

    spec = pl.BlockSpec((tr, C), lambda i: (i, 0))
    sh = jax.ShapeDtypeStruct((R, C), F32)
    return pl.pallas_call(
        body, name=name, grid=(R // tr,), in_specs=[spec] * 4, out_specs=[spec] * 3,
        out_shape=[sh] * 3, compiler_params=_params(("parallel",)),
    )(g, w, m, v)


def _ada_fwd(c16, w_sh, b_sh):
    def body(c_ref, w_ref, b_ref, o_ref):
        c = c_ref[...]
        o_ref[...] = _dot(c * _sig(c), w_ref[...], NN) + b_ref[...]

    return pl.pallas_call(
        body, name="ada_fwd", out_shape=jax.ShapeDtypeStruct((16, w_sh.shape[1]), F32),
        compiler_params=pltpu.CompilerParams(vmem_limit_bytes=VMEM_LIMIT),
    )(c16, w_sh, b_sh)


def _ada_bwd(c16, g16, g16_sh, w_sh):
    ncol = w_sh.shape[1]

    def body(c_ref, g_ref, gs_ref, w_ref, dw_ref, db_ref, dc_ref):
        c = c_ref[...]
        s = _sig(c)
        gs = gs_ref[...]
        dw_ref[...] = _dot(c * s, gs, TN)
        db_ref[...] = jnp.broadcast_to(_colsum(g_ref[...]), db_ref.shape)
        odd = lax.broadcasted_iota(jnp.int32, gs.shape, 0) % 2 == 1
        gc = _colsum(jnp.where(odd, gs, 0.0))
        ds = _dot(jnp.broadcast_to(gc, (8, ncol)), w_ref[...], NT)
        c1 = c[1:2, :]
        s1 = s[1:2, :]
        dc_ref[...] = ds * (s1 * (1.0 + c1 * (1.0 - s1)))

    return pl.pallas_call(
        body, name="ada_bwd",
        out_shape=[jax.ShapeDtypeStruct(w_sh.shape, F32),
                   jax.ShapeDtypeStruct((8, g16.shape[1]), F32),
                   jax.ShapeDtypeStruct((8, D), F32)],
        compiler_params=pltpu.CompilerParams(vmem_limit_bytes=VMEM_LIMIT),
    )(c16, g16, g16_sh, w_sh)


def _ln0_fwd(x, ctx, g, b, modx, modc):
    L = x.shape[0]
    nt = L // TL

    def body(x_ref, c_ref, g_ref, b_ref, mx_ref, mc_ref, xn_ref, h_ref):
        isc = pl.program_id(0) == nt
        xin = jnp.where(isc, c_ref[...], x_ref[...])
        sh = jnp.where(isc, mc_ref[0:1, :], mx_ref[0:1, :])
        sc = jnp.where(isc, mc_ref[1:2, :], mx_ref[1:2, :])
        xhat, _ = _ln(xin)
        xn = xhat * g_ref[...] + b_ref[...]
        xn_ref[...] = xn
        h_ref[...] = (xn * (1.0 + sc) + sh).astype(h_ref.dtype)

    return pl.pallas_call(
        body, name="ln0_fwd", grid=(nt + 1,),
        in_specs=[_rtc(D, nt), _cst((TL, D)), _cst((1, D)), _cst((1, D)), _cst((8, D)), _cst((8, D))],
        out_specs=[_rt(D), _rt(D)],
        out_shape=[jax.ShapeDtypeStruct((L + TL, D), F32), jax.ShapeDtypeStruct((L + TL, D), _MXU)],
        compiler_params=_params(("parallel",)),
    )(x, ctx, g, b, modx, modc)


def _xbc_colblk(j):
    return jnp.where(j < 8, OXS // 128 + j, OB // 128 + j - 8)


def _conv_taps(p_ref, r0, first, last):
    main = p_ref[pl.ds(r0, TL), :]
    zero = jnp.zeros((8, main.shape[1]), F32)
    prev = zero if first else p_ref[pl.ds(r0 - 8, 8), :]
    nxt = zero if last else p_ref[pl.ds(r0 + TL, 8), :]
    ext = jnp.concatenate([prev, main, nxt], axis=0)
    n = TL + 16
    return [pltpu.roll(ext, (2 - k) % n, 0)[8:8 + TL] for k in range(5)]


def _seq_chunks(L):
    nt = L // TL
    return [(r * TL, r == 0, r == nt - 1) for r in range(nt)] + [(L, True, True)]


def _conv_fwd(p, conv_w8, conv_b):
    RT = p.shape[0]
    L = RT - TL
    chunks = _seq_chunks(L)

    def body(p_ref, w_ref, b_ref, o_ref):
        w = w_ref[...]
        bias = b_ref[...]
        for r0, first, last in chunks:
            taps = _conv_taps(p_ref, r0, first, last)
            pre = bias + sum(w[k:k + 1, :] * taps[k] for k in range(5))
            o_ref[pl.ds(r0, TL), :] = pre * _sig(pre)

    return pl.pallas_call(
        body, name="conv_fwd", grid=(12,),
        in_specs=[pl.BlockSpec((RT, 128), lambda j: (0, _xbc_colblk(j))),
                  pl.BlockSpec((8, 128), lambda j: (0, j)),
                  pl.BlockSpec((1, 128), lambda j: (0, j))],
        out_specs=pl.BlockSpec((RT, 128), lambda j: (0, j)),
        out_shape=jax.ShapeDtypeStruct((RT, 1536), F32),
        compiler_params=_params(("parallel",)),
    )(p, conv_w8, conv_b)


def _ssd_common(dtraw, dtb, a32, rev):
    dt = _softplus(dtraw + dtb)
    acum = _cumsum_rows(dt * a32, rev)
    ii = lax.broadcasted_iota(jnp.int32, (Q, Q), 0)
    jj = lax.broadcasted_iota(jnp.int32, (Q, Q), 1)
    mask = (ii <= jj) if rev else (ii >= jj)
    return dt, acum, acum.T, dt.T, mask


def _ssd_orders(ncl, ncc):
    nc = ncl + ncc

    def cf(s):
        return jnp.where(s < ncc, ncl + s, s - ncc)

    def cb(s):
        return nc - 1 - s

    return cf, cb


def _ssd_fwd(xbc, p, prm):
    RT = xbc.shape[0]
    nc = RT // Q
    ncc = TL // Q
    cf, cb = _ssd_orders(nc - ncc, ncc)

    def one_dir(x_ref, dt_ref, prm_ref, y_ref, hp_ref, H_ref, d):
        rev = d == 1
        a32 = -jnp.exp(prm_ref[1:2, :])
        dt, acum, acumT, dtT, mask = _ssd_common(dt_ref[...], prm_ref[0:1, :], a32, rev)
        end = 0 if rev else Q - 1
        for g in range(2):
            Bg = x_ref[:, D + g * NS:D + (g + 1) * NS]
            Cg = x_ref[:, D + 2 * NS + g * NS:D + 2 * NS + (g + 1) * NS]
            CB = _dot(Cg, Bg, NT)
            for hh in range(HPG):
                h = g * HPG + hh
                ln = 16 * d + h
                col = acum[:, ln:ln + 1]
                rowv = acumT[ln:ln + 1, :]
                a_end = rowv[:, end:end + 1]
                Lm = jnp.exp(jnp.where(mask, col - rowv, -1e30))
                W = CB * Lm * dtT[ln:ln + 1, :]
                Xh = x_ref[:, h * HP:(h + 1) * HP]
                Hp = H_ref[h * HP:(h + 1) * HP, :]
                y = _dot(W, Xh, NN) + jnp.exp(col) * _dot(Cg, Hp, NT)
                y_ref[:, h * HP:(h + 1) * HP] = y
                dcol = jnp.exp(a_end - col) * dt[:, ln:ln + 1]
                hp_ref[0, h * HP:(h + 1) * HP, :] = Hp
                H_ref[h * HP:(h + 1) * HP, :] = jnp.exp(a_end) * Hp + _dot(Xh * dcol, Bg, TN)

    def body(xf_ref, xb_ref, df_ref, db_ref, prm_ref, yf_ref, yb_ref, hf_ref, hb_ref, Hf, Hb):
        @pl.when(pl.program_id(0) == 0)
        def _():
            Hf[...] = jnp.zeros_like(Hf)
            Hb[...] = jnp.zeros_like(Hb)

        one_dir(xf_ref, df_ref, prm_ref, yf_ref, hf_ref, Hf, 0)
        one_dir(xb_ref, db_ref, prm_ref, yb_ref, hb_ref, Hb, 1)

    ysh = jax.ShapeDtypeStruct((RT, D), F32)
    hsh = jax.ShapeDtypeStruct((nc, NH * HP, NS), F32)
    hspec = pl.BlockSpec((1, NH * HP, NS), lambda s: (s, 0, 0))
    return pl.pallas_call(
        body, name="ssd_fwd", grid=(nc,),
        in_specs=[pl.BlockSpec((Q, 1536), lambda s: (cf(s), 0)),
                  pl.BlockSpec((Q, 1536), lambda s: (cb(s), 0)),
                  pl.BlockSpec((Q, 128), lambda s: (cf(s), ODT // 128)),
                  pl.BlockSpec((Q, 128), lambda s: (cb(s), ODT // 128)),
                  _cst((8, 128))],
        out_specs=[pl.BlockSpec((Q, D), lambda s: (cf(s), 0)),
                   pl.BlockSpec((Q, D), lambda s: (cb(s), 0)), hspec, hspec],
        out_shape=[ysh, ysh, hsh, hsh],
        scratch_shapes=[pltpu.VMEM((NH * HP, NS), F32), pltpu.VMEM((NH * HP, NS), F32)],
        compiler_params=_params(("arbitrary",)),
    )(xbc, xbc, p, p, prm)


def _ssd_bwd(xbc, p, prm, dsk, dyd, hpf, hpb):
    RT = xbc.shape[0]
    nc = RT // Q
    ncc = TL // Q
    ncl = nc - ncc
    cf, cb = _ssd_orders(ncl, ncc)

    def rs(t):
        return nc - 1 - t

    def one_dir(x_ref, dt_ref, prm_ref, dsk_ref, dy_ref, is_ctx, hp_ref, dH_ref,
                dx_ref, ddt_ref, st_ref, d):
        rev = d == 1
        a32 = -jnp.exp(prm_ref[1:2, :])
        dtraw = dt_ref[...]
        dtb = prm_ref[0:1, :]
        dt, acum, acumT, dtT, mask = _ssd_common(dtraw, dtb, a32, rev)
        end = 0 if rev else Q - 1
        lane = lax.broadcasted_iota(jnp.int32, (Q, 128), 1)
        srow = lax.broadcasted_iota(jnp.int32, (Q, 128), 0)
        dyscale = jnp.where(is_ctx, 0.0, 1.0)
        c_dacum = jnp.zeros((Q, 128), F32)
        r_dacum = jnp.zeros((Q, 128), F32)
        c_ddt = jnp.zeros((Q, 128), F32)
        r_ddt = jnp.zeros((Q, 128), F32)
        dskacc = jnp.zeros((1, 128), F32)
        for g in range(2):
            Bg = x_ref[:, D + g * NS:D + (g + 1) * NS]
            Cg = x_ref[:, D + 2 * NS + g * NS:D + 2 * NS + (g + 1) * NS]
            CB = _dot(Cg, Bg, NT)
            dCB = jnp.zeros((Q, Q), F32)
            dBg = jnp.zeros((Q, NS), F32)
            dCg = jnp.zeros((Q, NS), F32)
            for hh in range(HPG):
                h = g * HPG + hh
                ln = 16 * d + h
                hs = slice(h * HP, (h + 1) * HP)
                col = acum[:, ln:ln + 1]
                rowv = acumT[ln:ln + 1, :]
                dtr = dtT[ln:ln + 1, :]
                dtc = dt[:, ln:ln + 1]
                a_end = rowv[:, end:end + 1]
                Lm = jnp.exp(jnp.where(mask, col - rowv, -1e30))
                E = jnp.exp(col)
                ecol = jnp.exp(a_end - col)
                dcol = ecol * dtc
                Xh = x_ref[:, hs]
                dY = dy_ref[:, hs] * dyscale
                Hp = hp_ref[0, hs, :]
                dHn = dH_ref[hs, :]
                W = CB * Lm * dtr
                dW = _dot(dY, Xh, NT)
                Mm = dW * CB * Lm
                T = Mm * dtr
                dCB = dCB + dW * Lm * dtr
                BdH = _dot(Bg, dHn, NT)
                dX = _dot(W, dY, TN) + dcol * BdH
                if d == 0:
                    dX = dX + dY * dsk_ref[:, hs]
                    dskacc = dskacc + jnp.where(lane[0:1, :] == h, _sum11(dY * Xh), 0.0)
                dx_ref[:, hs] = dX
                xb = jnp.sum(Xh * BdH, axis=1, keepdims=True)
                scol = dcol * xb
                G = _dot(dY, Hp, NN)
                dCg = dCg + E * G
                qcol = E * jnp.sum(G * Cg, axis=1, keepdims=True)
                dBg = dBg + _dot(Xh * dcol, dHn, NN)
                dH_ref[hs, :] = jnp.exp(a_end) * dHn + _dot(dY * E, Cg, TN)
                eterm = jnp.exp(a_end) * _sum11(dHn * Hp) + _sum11(scol)
                cvec = jnp.sum(T, axis=1, keepdims=True) + qcol - scol
                cvec = cvec + jnp.where(srow[:, 0:1] == end, eterm, 0.0)
                c_dacum = c_dacum + jnp.where(lane == ln, cvec, 0.0)
                r_dacum = r_dacum - jnp.where(srow == ln, _colsum(T), 0.0)
                c_ddt = c_ddt + jnp.where(lane == ln, ecol * xb, 0.0)
                r_ddt = r_ddt + jnp.where(srow == ln, _colsum(Mm), 0.0)
            dBg = dBg + _dot(dCB, Cg, TN)
            dCg = dCg + _dot(dCB, Bg, NN)
            dx_ref[:, D + g * NS:D + (g + 1) * NS] = dBg
            dx_ref[:, D + 2 * NS + g * NS:D + 2 * NS + (g + 1) * NS] = dCg
        dacum = c_dacum + r_dacum.T
        da = _cumsum_rows(dacum, not rev)
        mine = (lane >= 16 * d) & (lane < 16 * d + 16)
        ddt = jnp.where(mine, c_ddt + r_ddt.T + da * a32, 0.0)
        ddt_ref[...] = ddt * _sig(dtraw + dtb)
        st_ref[0:1, :] += _colsum(jnp.where(mine, da * dt, 0.0))
        if d == 0:
            st_ref[1:2, :] += dskacc

    def body(xf_ref, xb_ref, df_ref, db_ref, prm_ref, dsk_ref, dyf_ref, dyb_ref, hf_ref, hb_ref,
             dxf_ref, dxb_ref, ddf_ref, ddb_ref, st_ref, dHf, dHb):
        t = pl.program_id(0)

        @pl.when(t == 0)
        def _():
            dHf[...] = jnp.zeros_like(dHf)
            dHb[...] = jnp.zeros_like(dHb)
            st_ref[...] = jnp.zeros_like(st_ref)

        s = rs(t)
        one_dir(xf_ref, df_ref, prm_ref, dsk_ref, dyf_ref, cf(s) >= ncl, hf_ref, dHf,
                dxf_ref, ddf_ref, st_ref, 0)
        one_dir(xb_ref, db_ref, prm_ref, dsk_ref, dyb_ref, cb(s) >= ncl, hb_ref, dHb,
                dxb_ref, ddb_ref, st_ref, 1)

        @pl.when(t == nc - 1)
        def _():
            st_ref[0:1, :] = -jnp.exp(prm_ref[1:2, :]) * st_ref[0:1, :]

    def lat(c):
        return jnp.minimum(c, ncl - 1)

    xsh = jax.ShapeDtypeStruct((RT, 1536), F32)
    dsh = jax.ShapeDtypeStruct((RT, 128), F32)
    hspec = pl.BlockSpec((1, NH * HP, NS), lambda t: (rs(t), 0, 0))
    return pl.pallas_call(
        body, name="ssd_bwd", grid=(nc,),
        in_specs=[pl.BlockSpec((Q, 1536), lambda t: (cf(rs(t)), 0)),
                  pl.BlockSpec((Q, 1536), lambda t: (cb(rs(t)), 0)),
                  pl.BlockSpec((Q, 128), lambda t: (cf(rs(t)), ODT // 128)),
                  pl.BlockSpec((Q, 128), lambda t: (cb(rs(t)), ODT // 128)),
                  _cst((8, 128)), _cst((1, D)),
                  pl.BlockSpec((Q, D), lambda t: (lat(cf(rs(t))), 0)),
                  pl.BlockSpec((Q, D), lambda t: (lat(cb(rs(t))), 0)),
                  hspec, hspec],
        out_specs=[pl.BlockSpec((Q, 1536), lambda t: (cf(rs(t)), 0)),
                   pl.BlockSpec((Q, 1536), lambda t: (cb(rs(t)), 0)),
                   pl.BlockSpec((Q, 128), lambda t: (cf(rs(t)), 0)),
                   pl.BlockSpec((Q, 128), lambda t: (cb(rs(t)), 0)),
                   _cst((8, 128))],
        out_shape=[xsh, xsh, dsh, dsh, jax.ShapeDtypeStruct((8, 128), F32)],
        scratch_shapes=[pltpu.VMEM((NH * HP, NS), F32), pltpu.VMEM((NH * HP, NS), F32)],
        compiler_params=_params(("arbitrary",)),
    )(xbc, xbc, p, p, prm, dsk, dyd, dyd, hpf, hpb)


def _lane_bcast(v, ln):
    return jnp.broadcast_to(v[:, ln:ln + 1], v.shape)


def _halves(v, lo, axis):
    return jnp.concatenate([jnp.where(lo, v, 0.0), jnp.where(lo, 0.0, v)], axis=axis)


def _ssd2_fwd(xbc, p, prm):
    RT = xbc.shape[0]
    nc = RT // Q
    ncc = TL // Q
    cf, cb = _ssd_orders(nc - ncc, ncc)

    def one_dir(x_ref, dt_ref, prm_ref, y_ref, hp_ref, HT_ref, d):
        rev = d == 1
        a32 = -jnp.exp(prm_ref[1:2, :])
        dt, acum, acumT, dtT, mask = _ssd_common(dt_ref[...], prm_ref[0:1, :], a32, rev)
        end = 0 if rev else Q - 1
        lo = lax.broadcasted_iota(jnp.int32, (Q, 128), 1) < HP
        for g in range(2):
            Bg = x_ref[:, D + g * NS:D + (g + 1) * NS]
            Cg = x_ref[:, D + 2 * NS + g * NS:D + 2 * NS + (g + 1) * NS]
            CB = _dot(Cg, Bg, NT)
            xds, svs = [], []
            for q in range(HPG // 2):
                pi = g * (HPG // 2) + q
                ps = slice(pi * 128, (pi + 1) * 128)
                Xp = x_ref[:, ps]
                HTp = HT_ref[:, ps]
                lhs, dcs, sv = [], [], []
                ces = []
                for h in (2 * pi, 2 * pi + 1):
                    ln = 16 * d + h
                    colB = _lane_bcast(acum, ln)
                    rowv = acumT[ln:ln + 1, :]
                    aend = colB[end:end + 1, :]
                    Lm = jnp.exp(jnp.where(mask, colB - rowv, -1e30))
                    lhs.append(CB * Lm * dtT[ln:ln + 1, :])
                    ces.append(Cg * jnp.exp(colB))
                    dcs.append(jnp.exp(aend - colB) * _lane_bcast(dt, ln))
                    sv.append(jnp.exp(aend))
                lhs = jnp.concatenate(lhs + ces, axis=1)
                rhs = jnp.concatenate([_halves(Xp, lo, 0), _halves(HTp, lo, 0)], axis=0)
                y_ref[:, ps] = _dot(lhs, rhs, NN)
                xds.append(Xp * jnp.where(lo, dcs[0], dcs[1]))
                svs.append(jnp.where(lo[0:1, :], sv[0], sv[1]))
            gs = slice(g * 512, (g + 1) * 512)
            HTg = HT_ref[:, gs]
            hp_ref[0, :, gs] = HTg
            st = _dot(Bg.T, jnp.concatenate(xds, axis=1), NN)
            HT_ref[:, gs] = jnp.concatenate(svs, axis=1) * HTg + st

    def body(xf_ref, xb_ref, df_ref, db_ref, prm_ref, yf_ref, yb_ref, hf_ref, hb_ref, Hf, Hb):
        @pl.when(pl.program_id(0) == 0)
        def _():
            Hf[...] = jnp.zeros_like(Hf)
            Hb[...] = jnp.zeros_like(Hb)

        one_dir(xf_ref, df_ref, prm_ref, yf_ref, hf_ref, Hf, 0)
        one_dir(xb_ref, db_ref, prm_ref, yb_ref, hb_ref, Hb, 1)

    ysh = jax.ShapeDtypeStruct((RT, D), F32)
    hsh = jax.ShapeDtypeStruct((nc, NS, NH * HP), F32)
    hspec = pl.BlockSpec((1, NS, NH * HP), lambda s: (s, 0, 0))
    return pl.pallas_call(
        body, name="ssd_fwd", grid=(nc,),
        in_specs=[pl.BlockSpec((Q, 1536), lambda s: (cf(s), 0)),
                  pl.BlockSpec((Q, 1536), lambda s: (cb(s), 0)),
                  pl.BlockSpec((Q, 128), lambda s: (cf(s), ODT // 128)),
                  pl.BlockSpec((Q, 128), lambda s: (cb(s), ODT // 128)),
                  _cst((8, 128))],
        out_specs=[pl.BlockSpec((Q, D), lambda s: (cf(s), 0)),
                   pl.BlockSpec((Q, D), lambda s: (cb(s), 0)), hspec, hspec],
        out_shape=[ysh, ysh, hsh, hsh],
        scratch_shapes=[pltpu.VMEM((NS, NH * HP), F32), pltpu.VMEM((NS, NH * HP), F32)],
        compiler_params=_params(("arbitrary",)),
    )(xbc, xbc, p, p, prm)


def _ssd2_bwd(xbc, p, prm, dsk, dyd, hpf, hpb):
    RT = xbc.shape[0]
    nc = RT // Q
    ncc = TL // Q
    ncl = nc - ncc
    cf, cb = _ssd_orders(ncl, ncc)

    def rs(t):
        return nc - 1 - t

    def one_dir(x_ref, dt_ref, prm_ref, dsk_ref, dy_ref, is_ctx, hp_ref, dHT_ref,
                dx_ref, ddt_ref, st_ref, d):
        rev = d == 1
        a32 = -jnp.exp(prm_ref[1:2, :])
        dtraw = dt_ref[...]
        dtb = prm_ref[0:1, :]
        dt, acum, acumT, _, _ = _ssd_common(dtraw, dtb, a32, rev)
        end = 0 if rev else Q - 1
        lane = lax.broadcasted_iota(jnp.int32, (Q, 128), 1)
        srow = lax.broadcasted_iota(jnp.int32, (Q, 128), 0)
        maskT = (lane <= srow) if rev else (lane >= srow)
        lo = lane < HP
        lo1 = lo[0:1, :]
        dyscale = jnp.where(is_ctx, 0.0, 1.0)
        c_dacum = jnp.zeros((Q, 128), F32)
        r_dacum = jnp.zeros((Q, 128), F32)
        c_ddt = jnp.zeros((Q, 128), F32)
        dskacc = jnp.zeros((1, 128), F32)
        for g in range(2):
            gs = slice(g * 512, (g + 1) * 512)
            Bg = x_ref[:, D + g * NS:D + (g + 1) * NS]
            Cg = x_ref[:, D + 2 * NS + g * NS:D + 2 * NS + (g + 1) * NS]
            CBT = _dot(Bg, Cg, NT)
            HTg = hp_ref[0, :, gs]
            dHTg = dHT_ref[:, gs]
            BdHg = _dot(Bg, dHTg, NN)
            dCBT = jnp.zeros((Q, Q), F32)
            dCg = jnp.zeros((Q, NS), F32)
            xds, dyes, svs = [], [], []
            for q in range(HPG // 2):
                pi = g * (HPG // 2) + q
                ps = slice(pi * 128, (pi + 1) * 128)
                qs = slice(q * 128, (q + 1) * 128)
                Xp = x_ref[:, ps]
                dYp = dy_ref[:, ps] * dyscale
                HTp = HTg[:, qs]
                BdHp = BdHg[:, qs]
                dY2 = _halves(dYp, lo, 0)
                dWT2 = _dot(_halves(Xp, lo, 0), dYp.T, NN)
                G2 = _dot(dY2, HTp, NT)
                XB = Xp * BdHp
                hh = _colsum(dHTg[:, qs] * HTp)
                yx = _colsum(dYp * Xp)
                wts, dcs, ebs, sv = [], [], [], []
                for k, h in enumerate((2 * pi, 2 * pi + 1)):
                    ln = 16 * d + h
                    half = lo if k == 0 else jnp.logical_not(lo)
                    half1 = half[0:1, :]
                    colB = _lane_bcast(acum, ln)
                    dtcB = _lane_bcast(dt, ln)
                    rowv = acumT[ln:ln + 1, :]
                    aend = colB[end:end + 1, :]
                    LmT = jnp.exp(jnp.where(maskT, rowv - colB, -1e30))
                    WT = CBT * LmT * dtcB
                    dWT = dWT2[k * Q:(k + 1) * Q, :]
                    U = dWT * LmT
                    MT = U * CBT
                    rM = jnp.sum(MT, axis=1, keepdims=True)
                    rT = _colsum(MT * dtcB)
                    dCBT = dCBT + U * dtcB
                    ecol = jnp.exp(aend - colB)
                    EB = jnp.exp(colB)
                    Gk = G2[k * Q:(k + 1) * Q, :]
                    dCg = dCg + EB * Gk
                    qcol = jnp.sum(EB * Gk * Cg, axis=1, keepdims=True)
                    xb = jnp.sum(jnp.where(half, XB, 0.0), axis=1, keepdims=True)
                    e1 = ecol[:, 0:1]
                    dt1 = dtcB[:, 0:1]
                    scol = e1 * dt1 * xb
                    sA = jnp.exp(aend)
                    eterm = sA[:, 0:1] * jnp.sum(jnp.where(half1, hh, 0.0), axis=1, keepdims=True) \
                        + _colsum(scol)
                    cvec = qcol - dt1 * rM - scol + jnp.where(srow[:, 0:1] == end, eterm, 0.0)
                    c_dacum = jnp.where(lane == ln, cvec, c_dacum)
                    r_dacum = jnp.where(srow == ln, rT, r_dacum)
                    c_ddt = jnp.where(lane == ln, rM + e1 * xb, c_ddt)
                    if d == 0:
                        dskacc = dskacc + jnp.where(
                            lane[0:1, :] == h, jnp.sum(jnp.where(half1, yx, 0.0), axis=1, keepdims=True), 0.0)
                    wts.append(WT)
                    dcs.append(ecol * dtcB)
                    ebs.append(EB)
                    sv.append(sA)
                dcp = jnp.where(lo, dcs[0], dcs[1])
                dX = _dot(jnp.concatenate(wts, axis=1), dY2, NN) + dcp * BdHp
                if d == 0:
                    dX = dX + dYp * dsk_ref[:, ps]
                dx_ref[:, ps] = dX
                xds.append(Xp * dcp)
                dyes.append(dYp * jnp.where(lo, ebs[0], ebs[1]))
                svs.append(jnp.where(lo1, sv[0], sv[1]))
            dx_ref[:, D + g * NS:D + (g + 1) * NS] = (
                _dot(jnp.concatenate(xds, axis=1), dHTg, NT) + _dot(dCBT, Cg, NN))
            dx_ref[:, D + 2 * NS + g * NS:D + 2 * NS + (g + 1) * NS] = dCg + _dot(dCBT, Bg, TN)
            dHT_ref[:, gs] = (jnp.concatenate(svs, axis=1) * dHTg
                              + _dot(Cg.T, jnp.concatenate(dyes, axis=1), NN))
        dacum = c_dacum + r_dacum.T
        da = _cumsum_rows(dacum, not rev)
        mine = (lane >= 16 * d) & (lane < 16 * d + 16)
        ddt = jnp.where(mine, c_ddt + da * a32, 0.0)
        ddt_ref[...] = ddt * _sig(dtraw + dtb)
        st_ref[0:1, :] += _colsum(jnp.where(mine, da * dt, 0.0))
        if d == 0:
            st_ref[1:2, :] += dskacc

    def body(xf_ref, xb_ref, df_ref, db_ref, prm_ref, dsk_ref, dyf_ref, dyb_ref, hf_ref, hb_ref,
             dxf_ref, dxb_ref, ddf_ref, ddb_ref, st_ref, dHf, dHb):
        t = pl.program_id(0)

        @pl.when(t == 0)
        def _():
            dHf[...] = jnp.zeros_like(dHf)
            dHb[...] = jnp.zeros_like(dHb)
            st_ref[...] = jnp.zeros_like(st_ref)

        s = rs(t)
        one_dir(xf_ref, df_ref, prm_ref, dsk_ref, dyf_ref, cf(s) >= ncl, hf_ref, dHf,
                dxf_ref, ddf_ref, st_ref, 0)
        one_dir(xb_ref, db_ref, prm_ref, dsk_ref, dyb_ref, cb(s) >= ncl, hb_ref, dHb,
                dxb_ref, ddb_ref, st_ref, 1)

        @pl.when(t == nc - 1)
        def _():
            st_ref[0:1, :] = -jnp.exp(prm_ref[1:2, :]) * st_ref[0:1, :]

    def lat(c):
        return jnp.minimum(c, ncl - 1)

    xsh = jax.ShapeDtypeStruct((RT, 1536), F32)
    dsh = jax.ShapeDtypeStruct((RT, 128), F32)
    hspec = pl.BlockSpec((1, NS, NH * HP), lambda t: (rs(t), 0, 0))
    return pl.pallas_call(
        body, name="ssd_bwd", grid=(nc,),
        in_specs=[pl.BlockSpec((Q, 1536), lambda t: (cf(rs(t)), 0)),
                  pl.BlockSpec((Q, 1536), lambda t: (cb(rs(t)), 0)),
                  pl.BlockSpec((Q, 128), lambda t: (cf(rs(t)), ODT // 128)),
                  pl.BlockSpec((Q, 128), lambda t: (cb(rs(t)), ODT // 128)),
                  _cst((8, 128)), _cst((1, D)),
                  pl.BlockSpec((Q, D), lambda t: (lat(cf(rs(t))), 0)),
                  pl.BlockSpec((Q, D), lambda t: (lat(cb(rs(t))), 0)),
                  hspec, hspec],
        out_specs=[pl.BlockSpec((Q, 1536), lambda t: (cf(rs(t)), 0)),
                   pl.BlockSpec((Q, 1536), lambda t: (cb(rs(t)), 0)),
                   pl.BlockSpec((Q, 128), lambda t: (cf(rs(t)), 0)),
                   pl.BlockSpec((Q, 128), lambda t: (cb(rs(t)), 0)),
                   _cst((8, 128))],
        out_shape=[xsh, xsh, dsh, dsh, jax.ShapeDtypeStruct((8, 128), F32)],
        scratch_shapes=[pltpu.VMEM((NS, NH * HP), F32), pltpu.VMEM((NS, NH * HP), F32)],
        compiler_params=_params(("arbitrary",)),
    )(xbc, xbc, p, p, prm, dsk, dyd, dyd, hpf, hpb)


RB = 32


def _ssd3_bwd(xbc, p, prm, dsk, dyd, hpf, hpb):
    RT = xbc.shape[0]
    nc = RT // Q
    ncc = TL // Q
    ncl = nc - ncc
    cf, cb = _ssd_orders(ncl, ncc)
    npair = HPG // 2

    def rs(t):
        return nc - 1 - t

    def one_dir(x_ref, dt_ref, prm_ref, dsk_ref, dy_ref, is_ctx, hp_ref, dHT_ref,
                dx_ref, ddt_ref, st_ref, s_dwt, s_g, s_wt, s_xd, s_dye, s_dcbt, s_dcg, s_cd, s_cdt, d):
        rev = d == 1
        a32 = -jnp.exp(prm_ref[1:2, :])
        dtraw = dt_ref[...]
        dtb = prm_ref[0:1, :]
        dt, acum, acumT, _, _ = _ssd_common(dtraw, dtb, a32, rev)
        end = 0 if rev else Q - 1
        lane = lax.broadcasted_iota(jnp.int32, (RB, 128), 1)
        srow0 = lax.broadcasted_iota(jnp.int32, (RB, 128), 0)
        lo = lane < HP
        lo1 = lo[0:1, :]
        lane1 = lane[0:1, :]
        dyscale = jnp.where(is_ctx, 0.0, 1.0)
        aend_row = acum[end:end + 1, :]
        s_cd[...] = jnp.zeros_like(s_cd)
        s_cdt[...] = jnp.zeros_like(s_cdt)
        r_rows = jnp.zeros((Q, 128), F32)
        srowQ = lax.broadcasted_iota(jnp.int32, (Q, 128), 0)
        dskacc = jnp.zeros((1, 128), F32)
        for g in range(2):
            gs = slice(g * 512, (g + 1) * 512)
            Bg = x_ref[:, D + g * NS:D + (g + 1) * NS]
            Cg = x_ref[:, D + 2 * NS + g * NS:D + 2 * NS + (g + 1) * NS]
            CBT = _dot(Bg, Cg, NT)
            HTg = hp_ref[0, :, gs]
            dHTg = dHT_ref[:, gs]
            BdHg = _dot(Bg, dHTg, NN)
            hhs, yxs = [], []
            for q in range(npair):
                pi = g * npair + q
                ps = slice(pi * 128, (pi + 1) * 128)
                qs = slice(q * 128, (q + 1) * 128)
                Xp = x_ref[:, ps]
                dYp = dy_ref[:, ps] * dyscale
                s_dwt[q] = _dot(_halves(Xp, lo_full(), 0), dYp.T, NN)
                s_g[q] = _dot(_halves(dYp, lo_full(), 0), HTg[:, qs], NT)
                hhs.append(_colsum(dHTg[:, qs] * HTg[:, qs]))
                yxs.append(_colsum(dYp * Xp))
            rparts = [jnp.zeros((8, 128), F32) for _ in range(HPG)]
            ssum = [jnp.zeros((1, 1), F32) for _ in range(HPG)]
            for rb in range(Q // RB):
                r0 = rb * RB
                rows = slice(r0, r0 + RB)
                srow = srow0 + r0
                maskT = (lane <= srow) if rev else (lane >= srow)
                acum_rb = acum[rows, :]
                dt_rb = dt[rows, :]
                CBT_rb = CBT[rows, :]
                Cg_rb = Cg[rows, :]
                dcbt = jnp.zeros((RB, Q), F32)
                dcg = jnp.zeros((RB, NS), F32)
                cd = s_cd[rows, :]
                cdt = s_cdt[rows, :]
                for q in range(npair):
                    pi = g * npair + q
                    ps = slice(pi * 128, (pi + 1) * 128)
                    qs = slice(q * 128, (q + 1) * 128)
                    Xp = x_ref[rows, ps]
                    dYp = dy_ref[rows, ps] * dyscale
                    BdHp = BdHg[rows, qs]
                    XB = Xp * BdHp
                    dcs, ebs = [], []
                    for k in range(2):
                        hh = 2 * q + k
                        ln = 16 * d + g * HPG + hh
                        half = lo if k == 0 else jnp.logical_not(lo)
                        colB = _lane_bcast(acum_rb, ln)
                        dtcB = _lane_bcast(dt_rb, ln)
                        rowv = acumT[ln:ln + 1, :]
                        aend = _lane_bcast(aend_row, ln)
                        LmT = jnp.exp(jnp.where(maskT, rowv - colB, -1e30))
                        s_wt[q, rows, k * Q:(k + 1) * Q] = (CBT_rb * LmT * dtcB).astype(s_wt.dtype)
                        U = s_dwt[q, k * Q + r0:k * Q + r0 + RB, :] * LmT
                        MT = U * CBT_rb
                        rM = jnp.sum(MT, axis=1, keepdims=True)
                        TT = MT * dtcB
                        rparts[hh] = rparts[hh] + (TT[0:8] + TT[8:16] + TT[16:24] + TT[24:32])
                        dcbt = dcbt + U * dtcB
                        ecol = jnp.exp(aend - colB)
                        EB = jnp.exp(colB)
                        EG = EB * s_g[q, k * Q + r0:k * Q + r0 + RB, :]
                        dcg = dcg + EG
                        qcol = jnp.sum(EG * Cg_rb, axis=1, keepdims=True)
                        xb = jnp.sum(jnp.where(half, XB, 0.0), axis=1, keepdims=True)
                        e1 = ecol[:, 0:1]
                        dt1 = dtcB[:, 0:1]
                        scol = e1 * dt1 * xb
                        ssum[hh] = ssum[hh] + _colsum(scol)
                        cd = jnp.where(lane == ln, qcol - dt1 * rM - scol, cd)
                        cdt = jnp.where(lane == ln, rM + e1 * xb, cdt)
                        dcs.append(ecol * dtcB)
                        ebs.append(EB)
                    dcp = jnp.where(lo, dcs[0], dcs[1])
                    dxo = dcp * BdHp
                    if d == 0:
                        dxo = dxo + dYp * dsk_ref[:, ps]
                    dx_ref[rows, ps] = dxo
                    s_xd[rows, qs] = (Xp * dcp).astype(s_xd.dtype)
                    s_dye[rows, qs] = (dYp * jnp.where(lo, ebs[0], ebs[1])).astype(s_dye.dtype)
                s_dcbt[rows, :] = dcbt
                s_dcg[rows, :] = dcg
                s_cd[rows, :] = cd
                s_cdt[rows, :] = cdt
            erow = jnp.zeros((1, 128), F32)
            svs = []
            for q in range(npair):
                pi = g * npair + q
                ps = slice(pi * 128, (pi + 1) * 128)
                sv = []
                for k in range(2):
                    hh = 2 * q + k
                    h = g * HPG + hh
                    ln = 16 * d + h
                    half1 = lo1 if k == 0 else jnp.logical_not(lo1)
                    sA = jnp.exp(_lane_bcast(aend_row, ln))
                    hsum = jnp.sum(jnp.where(half1, hhs[q], 0.0), axis=1, keepdims=True)
                    erow = erow + jnp.where(lane1 == ln, sA[:, 0:1] * hsum + ssum[hh], 0.0)
                    rp = rparts[hh]
                    r_rows = jnp.where(srowQ == ln, _colsum(rp), r_rows)
                    if d == 0:
                        dskacc = dskacc + jnp.where(
                            lane1 == h, jnp.sum(jnp.where(half1, yxs[q], 0.0), axis=1, keepdims=True), 0.0)
                    sv.append(sA)
                svs.append(jnp.where(lo1, sv[0], sv[1]))
                dY2 = _halves(dy_ref[:, ps] * dyscale, lo_full(), 0)
                dx_ref[:, ps] += _dot(s_wt[q], dY2, NN)
            s_cd[end:end + 1, :] += erow
            dcbt_g = s_dcbt[...]
            dx_ref[:, D + g * NS:D + (g + 1) * NS] = _dot(s_xd[...], dHTg, NT) + _dot(dcbt_g, Cg, NN)
            dx_ref[:, D + 2 * NS + g * NS:D + 2 * NS + (g + 1) * NS] = s_dcg[...] + _dot(dcbt_g, Bg, TN)
            dHT_ref[:, gs] = jnp.concatenate(svs, axis=1) * dHTg + _dot(Cg.T, s_dye[...], NN)
        dacum = s_cd[...] + r_rows.T
        da = _cumsum_rows(dacum, not rev)
        laneQ = lax.broadcasted_iota(jnp.int32, (Q, 128), 1)
        mine = (laneQ >= 16 * d) & (laneQ < 16 * d + 16)
        ddt = jnp.where(mine, s_cdt[...] + da * a32, 0.0)
        ddt_ref[...] = ddt * _sig(dtraw + dtb)
        st_ref[0:1, :] += _colsum(jnp.where(mine, da * dt, 0.0))
        if d == 0:
            st_ref[1:2, :] += dskacc

    def lo_full():
        return lax.broadcasted_iota(jnp.int32, (Q, 128), 1) < HP

    def body(xf_ref, xb_ref, df_ref, db_ref, prm_ref, dsk_ref, dyf_ref, dyb_ref, hf_ref, hb_ref,
             dxf_ref, dxb_ref, ddf_ref, ddb_ref, st_ref, dHf, dHb, *scr):
        t = pl.program_id(0)

        @pl.when(t == 0)
        def _():
            dHf[...] = jnp.zeros_like(dHf)
            dHb[...] = jnp.zeros_like(dHb)
            st_ref[...] = jnp.zeros_like(st_ref)

        s = rs(t)
        one_dir(xf_ref, df_ref, prm_ref, dsk_ref, dyf_ref, cf(s) >= ncl, hf_ref, dHf,
                dxf_ref, ddf_ref, st_ref, *scr, 0)
        one_dir(xb_ref, db_ref, prm_ref, dsk_ref, dyb_ref, cb(s) >= ncl, hb_ref, dHb,
                dxb_ref, ddb_ref, st_ref, *scr, 1)

        @pl.when(t == nc - 1)
        def _():
            st_ref[0:1, :] = -jnp.exp(prm_ref[1:2, :]) * st_ref[0:1, :]

    def lat(c):
        return jnp.minimum(c, ncl - 1)

    xsh = jax.ShapeDtypeStruct((RT, 1536), F32)
    dsh = jax.ShapeDtypeStruct((RT, 128), F32)
    hspec = pl.BlockSpec((1, NS, NH * HP), lambda t: (rs(t), 0, 0))
    return pl.pallas_call(
        body, name="ssd_bwd", grid=(nc,),
        in_specs=[pl.BlockSpec((Q, 1536), lambda t: (cf(rs(t)), 0)),
                  pl.BlockSpec((Q, 1536), lambda t: (cb(rs(t)), 0)),
                  pl.BlockSpec((Q, 128), lambda t: (cf(rs(t)), ODT // 128)),
                  pl.BlockSpec((Q, 128), lambda t: (cb(rs(t)), ODT // 128)),
                  _cst((8, 128)), _cst((1, D)),
                  pl.BlockSpec((Q, D), lambda t: (lat(cf(rs(t))), 0)),
                  pl.BlockSpec((Q, D), lambda t: (lat(cb(rs(t))), 0)),
                  hspec, hspec],
        out_specs=[pl.BlockSpec((Q, 1536), lambda t: (cf(rs(t)), 0)),
                   pl.BlockSpec((Q, 1536), lambda t: (cb(rs(t)), 0)),
                   pl.BlockSpec((Q, 128), lambda t: (cf(rs(t)), 0)),
                   pl.BlockSpec((Q, 128), lambda t: (cb(rs(t)), 0)),
                   _cst((8, 128))],
        out_shape=[xsh, xsh, dsh, dsh, jax.ShapeDtypeStruct((8, 128), F32)],
        scratch_shapes=[pltpu.VMEM((NS, NH * HP), F32), pltpu.VMEM((NS, NH * HP), F32),
                        pltpu.VMEM((npair, 2 * Q, Q), F32), pltpu.VMEM((npair, 2 * Q, NS), F32),
                        pltpu.VMEM((npair, Q, 2 * Q), _MXU), pltpu.VMEM((Q, 512), _MXU),
                        pltpu.VMEM((Q, 512), _MXU), pltpu.VMEM((Q, Q), F32), pltpu.VMEM((Q, NS), F32),
                        pltpu.VMEM((Q, 128), F32), pltpu.VMEM((Q, 128), F32)],
        compiler_params=_params(("arbitrary",)),
    )(xbc, xbc, p, p, prm, dsk, dyd, dyd, hpf, hpb)


def _mix_fwd_vals(yf, yb, z, xs, u, v, dsk, sg, gg, gb):
    y = yf + yb + xs * dsk
    sz = _sig(z)
    hh = y * z * sz
    r = lax.rsqrt(jnp.mean(hh * hh, axis=-1, keepdims=True) + EPS)
    nh = hh * r
    ug, tu = _gelu(u)
    vg, tv = _gelu(v)
    vhat, vrstd = _ln(vg)
    vn = vhat * gg + gb
    return y, sz, r, nh, ug, tu, vg, tv, vhat, vrstd, vn


def _mix_fwd(yf, yb, p, xbc, dsk, sg, gg, gb, ws, bsT):
    L = yf.shape[0] - TL
    nt = L // TL

    def body(yf_ref, yb_ref, z_ref, xs_ref, u_ref, v_ref, dsk_ref, sg_ref, gg_ref, gb_ref,
             ws_ref, bs_ref, ys_ref, ym_ref):
        _, _, _, nh, ug, _, _, _, _, _, vn = _mix_fwd_vals(
            yf_ref[...], yb_ref[...], z_ref[...], xs_ref[...], u_ref[...], v_ref[...],
            dsk_ref[...], sg_ref[...], gg_ref[...], gb_ref[...])
        ys_ref[...] = (nh * sg_ref[...]).astype(ys_ref.dtype)
        for n in range(TL // Q):
            rs_ = slice(n * Q, (n + 1) * Q)
            for g in range(8):
                cs = slice(g * 128, (g + 1) * 128)
                mixed = _dot(ws_ref[g], vn[rs_, cs], NN) + bs_ref[:, g:g + 1]
                ym_ref[rs_, cs] = (ug[rs_, cs] * mixed).astype(ym_ref.dtype)

    return pl.pallas_call(
        body, name="mix_fwd", grid=(nt,),
        in_specs=[_rt(D), _rt(D), _rt(D, OZ // D), _rt(D, 0), _rt(D, OU // D), _rt(D, OV // D),
                  _cst((1, D)), _cst((1, D)), _cst((1, D)), _cst((1, D)),
                  _cst((8, 128, 128)), _cst((128, 128))],
        out_specs=[_rt(D), _rt(D)],
        out_shape=[jax.ShapeDtypeStruct((L, D), _MXU), jax.ShapeDtypeStruct((L, D), _MXU)],
        compiler_params=_params(("parallel",)),
    )(yf, yb, p, xbc, p, p, dsk, sg, gg, gb, ws, bsT)


def _mix_bwd(dys, dym, yf, yb, p, xbc, dp, dsk, sg, gg, gb, ws, bsT):
    L = dys.shape[0]
    nt = L // TL

    def body(dys_ref, dym_ref, yf_ref, yb_ref, z_ref, xs_ref, u_ref, v_ref, dsk_ref, sg_ref,
             gg_ref, gb_ref, ws_ref, bs_ref, dp_any, dzuv_ref, dy_ref, st_ref,
             dws_ref, dbs_ref, dvn_s):
        del dp_any
        dz_ref = dzuv_ref.at[:, OZ:OZ + D]
        du_ref = dzuv_ref.at[:, OU:OU + D]
        dv_ref = dzuv_ref.at[:, OV:OV + D]

        @pl.when(pl.program_id(0) == 0)
        def _():
            st_ref[...] = jnp.zeros_like(st_ref)
            dws_ref[...] = jnp.zeros_like(dws_ref)
            dbs_ref[...] = jnp.zeros_like(dbs_ref)

        z = z_ref[...]
        u = u_ref[...]
        v = v_ref[...]
        y, sz, r, nh, ug, tu, vg, tv, vhat, vrstd, vn = _mix_fwd_vals(
            yf_ref[...], yb_ref[...], z, xs_ref[...], u, v,
            dsk_ref[...], sg_ref[...], gg_ref[...], gb_ref[...])
        dys = dys_ref[...]
        st_ref[0:1, :] += _colsum(dys * nh)
        dn = dys * sg_ref[...]
        dhh = r * (dn - nh * jnp.mean(dn * nh, axis=-1, keepdims=True))
        dy_ref[...] = dhh * z * sz
        dz_ref[...] = (dhh * y * (sz * (1.0 + z * (1.0 - sz)))).astype(dz_ref.dtype)
        dym = dym_ref[...]
        lane = lax.broadcasted_iota(jnp.int32, (Q, 128), 1)
        dbs = jnp.zeros((Q, 128), F32)
        gu = _gelu_grad(u, tu)
        for n in range(TL // Q):
            rs_ = slice(n * Q, (n + 1) * Q)
            for g in range(8):
                cs = slice(g * 128, (g + 1) * 128)
                vb = vn[rs_, cs]
                mixed = _dot(ws_ref[g], vb, NN) + bs_ref[:, g:g + 1]
                dyb = dym[rs_, cs]
                dmx = dyb * ug[rs_, cs]
                du_ref[rs_, cs] = (dyb * mixed * gu[rs_, cs]).astype(du_ref.dtype)
                dvn_s[rs_, cs] = _dot(ws_ref[g], dmx, TN)
                dws_ref[g] += _dot(dmx, vb, NT)
                dbs = dbs + jnp.where(lane == g, jnp.sum(dmx, axis=1, keepdims=True), 0.0)
        dbs_ref[...] += dbs
        dvn = dvn_s[...]
        st_ref[1:2, :] += _colsum(dvn * vhat)
        st_ref[2:3, :] += _colsum(dvn)
        dvg = _ln_bwd(dvn * gg_ref[...], vhat, vrstd)
        dv_ref[...] = (dvg * _gelu_grad(v, tv)).astype(dv_ref.dtype)

    outs = pl.pallas_call(
        body, name="mix_bwd", grid=(nt,),
        in_specs=[_rt(D), _rt(D), _rt(D), _rt(D), _rt(D, OZ // D), _rt(D, 0), _rt(D, OU // D),
                  _rt(D, OV // D), _cst((1, D)), _cst((1, D)), _cst((1, D)), _cst((1, D)),
                  _cst((8, 128, 128)), _cst((128, 128)), pl.BlockSpec(memory_space=pl.ANY)],
        out_specs=[_rt(3 * D, 0), _rt(D), _cst((8, D)),
                   _cst((8, 128, 128)), _cst((128, 128))],
        out_shape=[jax.ShapeDtypeStruct(dp.shape, dp.dtype),
                   jax.ShapeDtypeStruct((L, D), F32), jax.ShapeDtypeStruct((8, D), F32),
                   jax.ShapeDtypeStruct((8, 128, 128), F32), jax.ShapeDtypeStruct((128, 128), F32)],
        scratch_shapes=[pltpu.VMEM((TL, D), F32)],
        input_output_aliases={14: 0},
        compiler_params=_params(("arbitrary",)),
    )(dys, dym, yf, yb, p, xbc, p, p, dsk, sg, gg, gb, ws, bsT, dp)
    return outs


def _gate_fwd(a1, a2, p, bg):
    L = a1.shape[0]

    def body(a1_ref, a2_ref, g_ref, bg_ref, m_ref):
        gt = _sig(g_ref[...] + bg_ref[...])
        m_ref[...] = (gt[:, :D] * a1_ref[...] + gt[:, D:] * a2_ref[...]).astype(m_ref.dtype)

    return pl.pallas_call(
        body, name="gate_fwd", grid=(L // TL,),
        in_specs=[_rt(D), _rt(D), _rt(2 * D, OG // (2 * D)), _cst((1, 2 * D))],
        out_specs=_rt(D), out_shape=jax.ShapeDtypeStruct((L, D), _MXU),
        compiler_params=_params(("parallel",)),
    )(a1, a2, p, bg)


def _gate_bwd(dmg, a1, a2, p, bg, dp):
    L = a1.shape[0]

    def body(dm_ref, a1_ref, a2_ref, g_ref, bg_ref, dp_any, dg_ref, da1_ref, da2_ref, st_ref):
        del dp_any

        @pl.when(pl.program_id(0) == 0)
        def _():
            st_ref[...] = jnp.zeros_like(st_ref)

        gt = _sig(g_ref[...] + bg_ref[...])
        g1 = gt[:, :D]
        g2 = gt[:, D:]
        dm = dm_ref[...]
        da1_ref[...] = (dm * g1).astype(da1_ref.dtype)
        da2_ref[...] = (dm * g2).astype(da2_ref.dtype)
        dg1 = dm * a1_ref[...] * g1 * (1.0 - g1)
        dg2 = dm * a2_ref[...] * g2 * (1.0 - g2)
        st_ref[0:1, 0:D] += _colsum(dg1)
        st_ref[0:1, D:2 * D] += _colsum(dg2)
        dg_ref[:, 0:D] = dg1.astype(dg_ref.dtype)
        dg_ref[:, D:2 * D] = dg2.astype(dg_ref.dtype)

    return pl.pallas_call(
        body, name="gate_bwd", grid=(L // TL,),
        in_specs=[_rt(D), _rt(D), _rt(D), _rt(2 * D, OG // (2 * D)), _cst((1, 2 * D)),
                  pl.BlockSpec(memory_space=pl.ANY)],
        out_specs=[_rt(2 * D, OG // (2 * D)), _rt(D), _rt(D), _cst((8, 2 * D))],
        out_shape=[jax.ShapeDtypeStruct(dp.shape, dp.dtype), jax.ShapeDtypeStruct((L, D), _MXU),
                   jax.ShapeDtypeStruct((L, D), _MXU), jax.ShapeDtypeStruct((8, 2 * D), F32)],
        input_output_aliases={5: 0},
        compiler_params=_params(("arbitrary",)),
    )(dmg, a1, a2, p, bg, dp)


def _merge_fwd(yssd, ygm, p, bg, ws, wg, wo, xn, modx, g1, b1):
    L = yssd.shape[0]
    tm = TL

    def body(ys_ref, yg_ref, g_ref, bg_ref, ws_ref, wg_ref, wo_ref, xn_ref, mx_ref, g1_ref, b1_ref,
             a1_ref, a2_ref, m_ref, o_ref, r1_ref, h2_ref):
        a1 = _dot(ys_ref[...], ws_ref[...], NN)
        a2 = _dot(yg_ref[...], wg_ref[...], NN)
        gt = _sig(g_ref[...] + bg_ref[...])
        mg = gt[:, :D] * a1 + gt[:, D:] * a2
        a1_ref[...] = a1
        a2_ref[...] = a2
        m_ref[...] = mg.astype(m_ref.dtype)
        out = _dot(mg, wo_ref[...], NN)
        o_ref[...] = out
        r1 = ALPHA * xn_ref[...] + mx_ref[2:3, :] * out
        xhat, _ = _ln(r1)
        x1 = xhat * g1_ref[...] + b1_ref[...]
        r1_ref[...] = r1
        h2_ref[...] = (x1 * (1.0 + mx_ref[4:5, :]) + mx_ref[3:4, :]).astype(h2_ref.dtype)

    rows = pl.BlockSpec((tm, D), lambda i: (i, 0))
    f32s = jax.ShapeDtypeStruct((L, D), F32)
    mxus = jax.ShapeDtypeStruct((L, D), _MXU)
    return pl.pallas_call(
        body, name="merge_fwd", grid=(L // tm,),
        in_specs=[rows, rows, pl.BlockSpec((tm, 2 * D), lambda i: (i, OG // (2 * D))), _cst((1, 2 * D)),
                  _cst((D, D)), _cst((D, D)), _cst((D, D)), rows, _cst((8, D)), _cst((1, D)), _cst((1, D))],
        out_specs=[rows] * 6,
        out_shape=[f32s, f32s, mxus, f32s, f32s, mxus],
        compiler_params=_params(("parallel",)),
    )(yssd, ygm, p, bg, ws, wg, wo, xn, modx, g1, b1)


def _mm_o2_res2(ff, w2, r1, tgt, modx, g1, b1, g2, b2):
    L, K = ff.shape
    tm = 2 * TL
    tk = K // 2
    nk = K // tk

    def body(a_ref, b_ref, r1_ref, t_ref, mx_ref, g1_ref, b1_ref, g2_ref, b2_ref,
             dr2_ref, do2_ref, st_ref, loss_ref, acc_ref):
        i, k = pl.program_id(0), pl.program_id(1)

        @pl.when((i == 0) & (k == 0))
        def _():
            st_ref[...] = jnp.zeros_like(st_ref)
            loss_ref[...] = jnp.zeros_like(loss_ref)

        prod = _dot(a_ref[...], b_ref[...], NN)

        @pl.when(k == 0)
        def _():
            acc_ref[...] = prod

        @pl.when(k == nk - 1)
        def _():
            o2 = acc_ref[...] + prod
            xh1, _ = _ln(r1_ref[...])
            x1 = xh1 * g1_ref[...] + b1_ref[...]
            g2x = mx_ref[5:6, :]
            xh2, rstd2 = _ln(ALPHA * x1 + g2x * o2)
            err = xh2 * g2_ref[...] + b2_ref[...] - t_ref[...]
            per_tok = jnp.mean(err * err, axis=-1, keepdims=True)
            loss_ref[...] += 0.5 * jnp.sum(per_tok, axis=0, keepdims=True)
            dy = err * (1.0 / D)
            st_ref[0:1, :] += _colsum(dy * xh2)
            st_ref[1:2, :] += _colsum(dy)
            dr2 = _ln_bwd(dy * g2_ref[...], xh2, rstd2)
            st_ref[2:3, :] += _colsum(dr2 * o2)
            dr2_ref[...] = dr2
            do2_ref[...] = (g2x * dr2).astype(do2_ref.dtype)

    assert nk == 2
    rows = pl.BlockSpec((tm, D), lambda i, k: (i, 0))
    vec = pl.BlockSpec((1, D), lambda i, k: (0, 0))
    return pl.pallas_call(
        body, name="mm_o2_res2", grid=(L // tm, nk),
        in_specs=[pl.BlockSpec((tm, tk), lambda i, k: (i, k)), pl.BlockSpec((tk, D), lambda i, k: (k, 0)),
                  rows, rows, pl.BlockSpec((8, D), lambda i, k: (0, 0)), vec, vec, vec, vec],
        out_specs=[rows, rows, pl.BlockSpec((8, D), lambda i, k: (0, 0)),
                   pl.BlockSpec((8, 128), lambda i, k: (0, 0))],
        out_shape=[jax.ShapeDtypeStruct((L, D), F32), jax.ShapeDtypeStruct((L, D), _MXU),
                   jax.ShapeDtypeStruct((8, D), F32), jax.ShapeDtypeStruct((8, 128), F32)],
        scratch_shapes=[pltpu.VMEM((tm, D), F32)],
        compiler_params=_params(("arbitrary", "arbitrary")),
    )(ff, w2, r1, tgt, modx, g1, b1, g2, b2)


def _mm_dh2_res1bwd(df13, w13i, dr2, r1, out, modx, g1, b1):
    L, K = df13.shape
    tm = 2 * TL
    tk = HFF
    nk = K // tk

    def body(a_ref, b_ref, dr2_ref, r1_ref, o_ref, mx_ref, g_ref, bb_ref, dr1_ref, do_ref, st_ref, acc_ref):
        i, k = pl.program_id(0), pl.program_id(1)

        @pl.when((i == 0) & (k == 0))
        def _():
            st_ref[...] = jnp.zeros_like(st_ref)

        prod = _dot(a_ref[...], b_ref[...], NN)

        @pl.when(k == 0)
        def _():
            acc_ref[...] = prod

        @pl.when((k > 0) & (k < nk - 1))
        def _():
            acc_ref[...] += prod

        @pl.when(k == nk - 1)
        def _():
            dh2 = acc_ref[...] + prod
            xh1, rstd1 = _ln(r1_ref[...])
            x1 = xh1 * g_ref[...] + bb_ref[...]
            dx1 = ALPHA * dr2_ref[...] + dh2 * (1.0 + mx_ref[4:5, :])
            st_ref[0:1, :] += _colsum(dh2 * x1)
            st_ref[1:2, :] += _colsum(dh2)
            st_ref[2:3, :] += _colsum(dx1 * xh1)
            st_ref[3:4, :] += _colsum(dx1)
            dr1 = _ln_bwd(dx1 * g_ref[...], xh1, rstd1)
            st_ref[4:5, :] += _colsum(dr1 * o_ref[...])
            dr1_ref[...] = dr1
            do_ref[...] = (mx_ref[2:3, :] * dr1).astype(do_ref.dtype)

    assert nk >= 2
    rows = pl.BlockSpec((tm, D), lambda i, k: (i, 0))
    vec = pl.BlockSpec((1, D), lambda i, k: (0, 0))
    return pl.pallas_call(
        body, name="mm_dh2_res1bwd", grid=(L // tm, nk),
        in_specs=[pl.BlockSpec((tm, tk), lambda i, k: (i, k)), pl.BlockSpec((tk, D), lambda i, k: (k, 0)),
                  rows, rows, rows, pl.BlockSpec((8, D), lambda i, k: (0, 0)), vec, vec],
        out_specs=[rows, rows, pl.BlockSpec((8, D), lambda i, k: (0, 0))],
        out_shape=[jax.ShapeDtypeStruct((L, D), F32), jax.ShapeDtypeStruct((L, D), _MXU),
                   jax.ShapeDtypeStruct((8, D), F32)],
        scratch_shapes=[pltpu.VMEM((tm, D), F32)],
        compiler_params=_params(("arbitrary", "arbitrary")),
    )(df13, w13i, dr2, r1, out, modx, g1, b1)


def _merge_bwd(dout, a1, a2, p, bg, wo, ws, wg, dp):
    L = a1.shape[0]
    tm = TL

    def body(do_ref, a1_ref, a2_ref, g_ref, bg_ref, wo_ref, ws_ref, wg_ref, dp_any,
             dg_ref, da1_ref, da2_ref, st_ref, dys_ref, dym_ref):
        del dp_any

        @pl.when(pl.program_id(0) == 0)
        def _():
            st_ref[...] = jnp.zeros_like(st_ref)

        dm = _dot(do_ref[...], wo_ref[...], NT)
        gt = _sig(g_ref[...] + bg_ref[...])
        g1 = gt[:, :D]
        g2 = gt[:, D:]
        da1 = (dm * g1).astype(da1_ref.dtype)
        da2 = (dm * g2).astype(da2_ref.dtype)
        da1_ref[...] = da1
        da2_ref[...] = da2
        dg1 = dm * a1_ref[...] * g1 * (1.0 - g1)
        dg2 = dm * a2_ref[...] * g2 * (1.0 - g2)
        st_ref[0:1, 0:D] += _colsum(dg1)
        st_ref[0:1, D:2 * D] += _colsum(dg2)
        dg_ref[:, 0:D] = dg1.astype(dg_ref.dtype)
        dg_ref[:, D:2 * D] = dg2.astype(dg_ref.dtype)
        dys_ref[...] = _dot(da1, ws_ref[...], NT)
        dym_ref[...] = _dot(da2, wg_ref[...], NT)

    rows = pl.BlockSpec((tm, D), lambda i: (i, 0))
    gates = pl.BlockSpec((tm, 2 * D), lambda i: (i, OG // (2 * D)))
    f32s = jax.ShapeDtypeStruct((L, D), F32)
    mxus = jax.ShapeDtypeStruct((L, D), _MXU)
    return pl.pallas_call(
        body, name="merge_bwd", grid=(L // tm,),
        in_specs=[rows, rows, rows, gates, _cst((1, 2 * D)), _cst((D, D)), _cst((D, D)), _cst((D, D)),
                  pl.BlockSpec(memory_space=pl.ANY)],
        out_specs=[gates, rows, rows, _cst((8, 2 * D)), rows, rows],
        out_shape=[jax.ShapeDtypeStruct(dp.shape, dp.dtype), mxus, mxus,
                   jax.ShapeDtypeStruct((8, 2 * D), F32), f32s, f32s],
        input_output_aliases={8: 0},
        compiler_params=_params(("arbitrary",)),
    )(dout, a1, a2, p, bg, wo, ws, wg, dp)


def _res1_fwd(xn, out, modx, g, b):
    L = out.shape[0]

    def body(xn_ref, o_ref, mx_ref, g_ref, b_ref, r1_ref, h2_ref):
        r1 = ALPHA * xn_ref[...] + mx_ref[2:3, :] * o_ref[...]
        xhat, _ = _ln(r1)
        x1 = xhat * g_ref[...] + b_ref[...]
        r1_ref[...] = r1
        h2_ref[...] = (x1 * (1.0 + mx_ref[4:5, :]) + mx_ref[3:4, :]).astype(h2_ref.dtype)

    return pl.pallas_call(
        body, name="res1_fwd", grid=(L // TL,),
        in_specs=[_rt(D), _rt(D), _cst((8, D)), _cst((1, D)), _cst((1, D))],
        out_specs=[_rt(D), _rt(D)],
        out_shape=[jax.ShapeDtypeStruct((L, D), F32), jax.ShapeDtypeStruct((L, D), _MXU)],
        compiler_params=_params(("parallel",)),
    )(xn, out, modx, g, b)


HFF = DFF // 2


def _mm_f13_glu(h2, w13i):
    L = h2.shape[0]
    tm = 512

    def body(a_ref, b_ref, f_ref, ff_ref):
        f = _dot(a_ref[...], b_ref[...], NT)
        f_ref[...] = f
        f1 = f[:, :HFF]
        ff_ref[...] = (f1 * _sig(f1) * f[:, HFF:]).astype(ff_ref.dtype)

    return pl.pallas_call(
        body, name="mm_f13_glu", grid=(DFF // HFF, L // tm),
        in_specs=[pl.BlockSpec((tm, D), lambda j, i: (i, 0)), pl.BlockSpec((2 * HFF, D), lambda j, i: (j, 0))],
        out_specs=[pl.BlockSpec((tm, 2 * HFF), lambda j, i: (i, j)), pl.BlockSpec((tm, HFF), lambda j, i: (i, j))],
        out_shape=[jax.ShapeDtypeStruct((L, 2 * DFF), F32), jax.ShapeDtypeStruct((L, DFF), _MXU)],
        compiler_params=_params(("parallel", "parallel")),
    )(h2, w13i)


def _mm_dff_glu(do2, w_ff2_f, f13i):
    L = do2.shape[0]
    tm = 512

    def body(a_ref, b_ref, f_ref, o_ref):
        d = _dot(a_ref[...], b_ref[...], NT)
        f1 = f_ref[:, :HFF]
        s = _sig(f1)
        o_ref[:, :HFF] = (d * f_ref[:, HFF:] * (s * (1.0 + f1 * (1.0 - s)))).astype(o_ref.dtype)
        o_ref[:, HFF:] = (d * f1 * s).astype(o_ref.dtype)

    return pl.pallas_call(
        body, name="mm_dff_glu", grid=(DFF // HFF, L // tm),
        in_specs=[pl.BlockSpec((tm, D), lambda j, i: (i, 0)), pl.BlockSpec((HFF, D), lambda j, i: (j, 0)),
                  pl.BlockSpec((tm, 2 * HFF), lambda j, i: (i, j))],
        out_specs=pl.BlockSpec((tm, 2 * HFF), lambda j, i: (i, j)),
        out_shape=jax.ShapeDtypeStruct((L, 2 * DFF), _MXU),
        compiler_params=_params(("parallel", "parallel")),
    )(do2, w_ff2_f, f13i)


def _glu_fwd(f13):
    L = f13.shape[0]

    def body(f1_ref, f3_ref, o_ref):
        f1 = f1_ref[...]
        o_ref[...] = (f1 * _sig(f1) * f3_ref[...]).astype(o_ref.dtype)

    return pl.pallas_call(
        body, name="glu_fwd", grid=(L // TL,),
        in_specs=[_rt(DFF, 0), _rt(DFF, 1)], out_specs=_rt(DFF),
        out_shape=jax.ShapeDtypeStruct((L, DFF), _MXU),
        compiler_params=_params(("parallel",)),
    )(f13, f13)


def _glu_bwd(dff, f13):
    L = f13.shape[0]

    def body(d_ref, f1_ref, f3_ref, o_ref):
        f1 = f1_ref[...]
        s = _sig(f1)
        d = d_ref[...]
        o_ref[:, 0:DFF] = (d * f3_ref[...] * (s * (1.0 + f1 * (1.0 - s)))).astype(o_ref.dtype)
        o_ref[:, DFF:2 * DFF] = (d * f1 * s).astype(o_ref.dtype)

    return pl.pallas_call(
        body, name="glu_bwd", grid=(L // TL,),
        in_specs=[_rt(DFF), _rt(DFF, 0), _rt(DFF, 1)], out_specs=_rt(2 * DFF),
        out_shape=jax.ShapeDtypeStruct((L, 2 * DFF), _MXU),
        compiler_params=_params(("parallel",)),
    )(dff, f13, f13)


def _res2(r1, o2, tgt, modx, g1, b1, g2, b2):
    L = r1.shape[0]

    def body(r1_ref, o2_ref, t_ref, mx_ref, g1_ref, b1_ref, g2_ref, b2_ref,
             dr2_ref, do2_ref, st_ref, loss_ref):
        @pl.when(pl.program_id(0) == 0)
        def _():
            st_ref[...] = jnp.zeros_like(st_ref)
            loss_ref[...] = jnp.zeros_like(loss_ref)

        xh1, _ = _ln(r1_ref[...])
        x1 = xh1 * g1_ref[...] + b1_ref[...]
        o2 = o2_ref[...]
        g2x = mx_ref[5:6, :]
        xh2, rstd2 = _ln(ALPHA * x1 + g2x * o2)
        err = xh2 * g2_ref[...] + b2_ref[...] - t_ref[...]
        per_tok = jnp.mean(err * err, axis=-1, keepdims=True)
        loss_ref[...] += 0.5 * jnp.sum(per_tok, axis=0, keepdims=True)
        dy = err * (1.0 / D)
        st_ref[0:1, :] += _colsum(dy * xh2)
        st_ref[1:2, :] += _colsum(dy)
        dr2 = _ln_bwd(dy * g2_ref[...], xh2, rstd2)
        st_ref[2:3, :] += _colsum(dr2 * o2)
        dr2_ref[...] = dr2
        do2_ref[...] = (g2x * dr2).astype(do2_ref.dtype)

    return pl.pallas_call(
        body, name="res2", grid=(L // TL,),
        in_specs=[_rt(D), _rt(D), _rt(D), _cst((8, D))] + [_cst((1, D))] * 4,
        out_specs=[_rt(D), _rt(D), _cst((8, D)), _cst((8, 128))],
        out_shape=[jax.ShapeDtypeStruct((L, D), F32), jax.ShapeDtypeStruct((L, D), _MXU),
                   jax.ShapeDtypeStruct((8, D), F32), jax.ShapeDtypeStruct((8, 128), F32)],
        compiler_params=_params(("arbitrary",)),
    )(r1, o2, tgt, modx, g1, b1, g2, b2)


def _res1_bwd(dr2, dh2, r1, out, modx, g1, b1):
    L = r1.shape[0]

    def body(dr2_ref, dh2_ref, r1_ref, o_ref, mx_ref, g_ref, b_ref, dr1_ref, do_ref, st_ref):
        @pl.when(pl.program_id(0) == 0)
        def _():
            st_ref[...] = jnp.zeros_like(st_ref)

        xh1, rstd1 = _ln(r1_ref[...])
        x1 = xh1 * g_ref[...] + b_ref[...]
        dh2 = dh2_ref[...]
        dx1 = ALPHA * dr2_ref[...] + dh2 * (1.0 + mx_ref[4:5, :])
        st_ref[0:1, :] += _colsum(dh2 * x1)
        st_ref[1:2, :] += _colsum(dh2)
        st_ref[2:3, :] += _colsum(dx1 * xh1)
        st_ref[3:4, :] += _colsum(dx1)
        dr1 = _ln_bwd(dx1 * g_ref[...], xh1, rstd1)
        st_ref[4:5, :] += _colsum(dr1 * o_ref[...])
        dr1_ref[...] = dr1
        do_ref[...] = (mx_ref[2:3, :] * dr1).astype(do_ref.dtype)

    return pl.pallas_call(
        body, name="res1_bwd", grid=(L // TL,),
        in_specs=[_rt(D), _rt(D), _rt(D), _rt(D), _cst((8, D)), _cst((1, D)), _cst((1, D))],
        out_specs=[_rt(D), _rt(D), _cst((8, D))],
        out_shape=[jax.ShapeDtypeStruct((L, D), F32), jax.ShapeDtypeStruct((L, D), _MXU),
                   jax.ShapeDtypeStruct((8, D), F32)],
        compiler_params=_params(("arbitrary",)),
    )(dr2, dh2, r1, out, modx, g1, b1)


def _conv_bwd(dxf, dxb, p, conv_w8, conv_b, dp):
    RT = p.shape[0]
    chunks = _seq_chunks(RT - TL)

    def body(df_ref, db_ref, p_ref, w_ref, b_ref, dp_any, o_ref, dw_ref, dbias_ref, dpre_s):
        del dp_any
        w = w_ref[...]
        bias = b_ref[...]
        srow = lax.broadcasted_iota(jnp.int32, (8, 128), 0)
        dwacc = jnp.zeros((8, 128), F32)
        dbacc = jnp.zeros((1, 128), F32)
        for r0, first, last in chunks:
            taps = _conv_taps(p_ref, r0, first, last)
            pre = bias + sum(w[k:k + 1, :] * taps[k] for k in range(5))
            s = _sig(pre)
            dpre = (df_ref[pl.ds(r0, TL), :] + db_ref[pl.ds(r0, TL), :]) * (s * (1.0 + pre * (1.0 - s)))
            dpre_s[pl.ds(r0, TL), :] = dpre
            dbacc = dbacc + _colsum(dpre)
            for k in range(5):
                dwacc = dwacc + jnp.where(srow == k, _colsum(dpre * taps[k]), 0.0)
        for r0, first, last in chunks:
            taps = _conv_taps(dpre_s, r0, first, last)
            dx = sum(w[k:k + 1, :] * taps[4 - k] for k in range(5))
            o_ref[pl.ds(r0, TL), :] = dx.astype(o_ref.dtype)
        dw_ref[...] = dwacc
        dbias_ref[...] = jnp.broadcast_to(dbacc, (8, 128))

    cspec = pl.BlockSpec((RT, 128), lambda j: (0, j))
    wspec = pl.BlockSpec((8, 128), lambda j: (0, j))
    return pl.pallas_call(
        body, name="conv_bwd", grid=(12,),
        in_specs=[cspec, cspec, pl.BlockSpec((RT, 128), lambda j: (0, _xbc_colblk(j))),
                  wspec, pl.BlockSpec((1, 128), lambda j: (0, j)), pl.BlockSpec(memory_space=pl.ANY)],
        out_specs=[pl.BlockSpec((RT, 128), lambda j: (0, _xbc_colblk(j))), wspec, wspec],
        out_shape=[jax.ShapeDtypeStruct(dp.shape, dp.dtype), jax.ShapeDtypeStruct((8, 1536), F32),
                   jax.ShapeDtypeStruct((8, 1536), F32)],
        scratch_shapes=[pltpu.VMEM((RT, 128), F32)],
        input_output_aliases={5: 0},
        compiler_params=_params(("parallel",)),
    )(dxf, dxb, p, conv_w8, conv_b, dp)


def _dt_bwd(ddf, ddb, dp):
    RT = ddf.shape[0]

    def body(f_ref, b_ref, dp_any, o_ref, st_ref):
        del dp_any

        @pl.when(pl.program_id(0) == 0)
        def _():
            st_ref[...] = jnp.zeros_like(st_ref)

        s = f_ref[...] + b_ref[...]
        o_ref[...] = s.astype(o_ref.dtype)
        st_ref[0:1, :] += _colsum(s)

    return pl.pallas_call(
        body, name="dt_bwd", grid=(RT // TL,),
        in_specs=[_rt(128), _rt(128), pl.BlockSpec(memory_space=pl.ANY)],
        out_specs=[_rt(128, ODT // 128), _cst((8, 128))],
        out_shape=[jax.ShapeDtypeStruct(dp.shape, dp.dtype), jax.ShapeDtypeStruct((8, 128), F32)],
        input_output_aliases={2: 0},
        compiler_params=_params(("arbitrary",)),
    )(ddf, ddb, dp)


def _ln0_bwd(dh1, dr1, x, ctx, g, b, modx, modc):
    L = x.shape[0]
    nt = L // TL

    def body(dh_ref, dr1_ref, x_ref, c_ref, g_ref, b_ref, mx_ref, mc_ref, gx_ref, st_ref):
        i = pl.program_id(0)
        isc = i == nt

        @pl.when(i == 0)
        def _():
            st_ref[...] = jnp.zeros_like(st_ref)

        xin = jnp.where(isc, c_ref[...], x_ref[...])
        xhat, rstd = _ln(xin)
        xn = xhat * g_ref[...] + b_ref[...]
        sc = jnp.where(isc, mc_ref[1:2, :], mx_ref[1:2, :])
        dh = dh_ref[...]
        lat = jnp.where(isc, 0.0, 1.0)
        dxn = dh * (1.0 + sc) + (lat * ALPHA) * dr1_ref[...]
        tsh = _colsum(dh)
        tsc = _colsum(dh * xn)
        st_ref[0:1, :] += lat * tsh
        st_ref[1:2, :] += lat * tsc
        st_ref[2:3, :] += (1.0 - lat) * tsh
        st_ref[3:4, :] += (1.0 - lat) * tsc
        st_ref[4:5, :] += _colsum(dxn * xhat)
        st_ref[5:6, :] += _colsum(dxn)

        @pl.when(i < nt)
        def _():
            gx_ref[...] = _ln_bwd(dxn * g_ref[...], xhat, rstd)

    return pl.pallas_call(
        body, name="ln0_bwd", grid=(nt + 1,),
        in_specs=[_rt(D), _rtc(D, nt), _rtc(D, nt), _cst((TL, D)), _cst((1, D)), _cst((1, D)),
                  _cst((8, D)), _cst((8, D))],
        out_specs=[_rtc(D, nt), _cst((8, D))],
        out_shape=[jax.ShapeDtypeStruct((L, D), F32), jax.ShapeDtypeStruct((8, D), F32)],
        compiler_params=_params(("arbitrary",)),
    )(dh1, dr1, x, ctx, g, b, modx, modc)


def _perm_cols(w):
    pad = jnp.zeros((w.shape[0], NPJ - NNAT), w.dtype)
    return jnp.concatenate([w[:, 0:1024], w[:, 2592:3616], w[:, 3616:4640], w[:, 1024:2048],
                            w[:, 4640:6688], w[:, 2048:2304], w[:, 2304:2560], w[:, 2560:2592], pad],
                           axis=1)


SECTIONS = ((0, 1024, OZ), (1024, 2048, OXS), (2048, 2304, OB), (2304, 2560, OC), (2560, 2592, ODT),
            (2592, 3616, OU), (3616, 4640, OV), (4640, 6688, OG))


def _perm_from_blocks(ga):
    n = ga.shape[2]
    pieces = []
    for na, nb, _ in sorted(SECTIONS, key=lambda sec: sec[2]):
        for k in range(NDEV):
            lo, hi = max(na, k * n), min(nb, (k + 1) * n)
            if lo < hi:
                pieces.append(ga[k][:, lo - k * n:hi - k * n])
    pieces.append(jnp.zeros((ga.shape[1], NPJ - NNAT), ga.dtype))
    return jnp.concatenate(pieces, axis=1)


def _blocks_from_perm(gp, n):
    blocks = []
    for k in range(NDEV):
        pieces = []
        for na, nb, po in SECTIONS:
            lo, hi = max(na, k * n), min(nb, (k + 1) * n)
            if lo < hi:
                pieces.append(gp[:, po + lo - na:po + hi - na])
        blocks.append(jnp.concatenate(pieces, axis=1))
    return jnp.stack(blocks)


def _padded(n, row_align):
    unit = row_align * D
    return -(-n // unit) * unit if row_align else n


def _slab(arrs, rows, row_align=0):
    parts = []
    for a in arrs:
        f = a.reshape(-1)
        parts.append(jnp.pad(f, (0, _padded(f.shape[0], row_align) - f.shape[0])))
    flat = jnp.concatenate(parts)
    flat = jnp.pad(flat, (0, rows * D - flat.shape[0]))
    return flat.reshape(rows, D)


def _unslab(slab, shapes, row_align=0):
    out, off = [], 0
    for shp in shapes:
        n = 1
        for s in shp:
            n *= s
        r0, r1 = off // D, -(-(off + n) // D)
        out.append(slab[r0:r1].reshape(-1)[off - r0 * D:off - r0 * D + n].reshape(shp))
        off += _padded(n, row_align)
    return out


def _row(v):
    return v.reshape(1, -1)


def _t(a):
    return jnp.swapaxes(a, 0, 1)


def _pad_rows(a, rows):
    return jnp.pad(a, ((0, rows - a.shape[0]), (0, 0)))


BIG = ["w_in", "w_ssd_proj", "w_gm_proj", "w_out", "w_ff1", "w_ff3", "w_ff2"]
BIG_ROWS = 2304
BIG_ALIGN = 16
REPL = ["c_ctx", "ln0_g", "ln0_b", "b_ada", "conv_b", "dt_bias", "a_log", "d_skip", "ssd_norm_g",
        "gm_norm_g", "gm_norm_b", "w_spatial", "b_spatial", "b_gate", "ln1_g", "ln1_b", "ln2_g", "ln2_b"]
SMALL_ROWS = 160
WEIGHTS = ["c_ctx", "ln0_g", "ln0_b", "w_ada", "b_ada", "w_in", "conv_w", "conv_b", "dt_bias", "a_log",
           "d_skip", "ssd_norm_g", "gm_norm_g", "gm_norm_b", "w_spatial", "b_spatial", "b_gate",
           "w_ssd_proj", "w_gm_proj", "w_out", "ln1_g", "ln1_b", "w_ff1", "w_ff3", "w_ff2", "ln2_g", "ln2_b"]


def kernel(x, c, ctx, c_ctx, ln0_g, ln0_b, w_ada, b_ada, w_in, conv_w, conv_b, dt_bias, a_log, d_skip, ssd_norm_g, gm_norm_g, gm_norm_b, w_spatial, b_spatial, b_gate, w_ssd_proj, w_gm_proj, w_out, ln1_g, ln1_b, w_ff1, w_ff3, w_ff2, ln2_g, ln2_b, loss_target, m_c_ctx, m_ln0_g, m_ln0_b, m_w_ada, m_b_ada, m_w_in, m_conv_w, m_conv_b, m_dt_bias, m_a_log, m_d_skip, m_ssd_norm_g, m_gm_norm_g, m_gm_norm_b, m_w_spatial, m_b_spatial, m_b_gate, m_w_ssd_proj, m_w_gm_proj, m_w_out, m_ln1_g, m_ln1_b, m_w_ff1, m_w_ff3, m_w_ff2, m_ln2_g, m_ln2_b, v_c_ctx, v_ln0_g, v_ln0_b, v_w_ada, v_b_ada, v_w_in, v_conv_w, v_conv_b, v_dt_bias, v_a_log, v_d_skip, v_ssd_norm_g, v_gm_norm_g, v_gm_norm_b, v_w_spatial, v_b_spatial, v_b_gate, v_w_ssd_proj, v_w_gm_proj, v_w_out, v_ln1_g, v_ln1_b, v_w_ff1, v_w_ff3, v_w_ff2, v_ln2_g, v_ln2_b):
    W = dict(c_ctx=c_ctx, ln0_g=ln0_g, ln0_b=ln0_b, w_ada=w_ada, b_ada=b_ada, w_in=w_in, conv_w=conv_w,
             conv_b=conv_b, dt_bias=dt_bias, a_log=a_log, d_skip=d_skip, ssd_norm_g=ssd_norm_g,
             gm_norm_g=gm_norm_g, gm_norm_b=gm_norm_b, w_spatial=w_spatial, b_spatial=b_spatial,
             b_gate=b_gate, w_ssd_proj=w_ssd_proj, w_gm_proj=w_gm_proj, w_out=w_out, ln1_g=ln1_g,
             ln1_b=ln1_b, w_ff1=w_ff1, w_ff3=w_ff3, w_ff2=w_ff2, ln2_g=ln2_g, ln2_b=ln2_b)
    M = dict(c_ctx=m_c_ctx, ln0_g=m_ln0_g, ln0_b=m_ln0_b, w_ada=m_w_ada, b_ada=m_b_ada, w_in=m_w_in,
             conv_w=m_conv_w, conv_b=m_conv_b, dt_bias=m_dt_bias, a_log=m_a_log, d_skip=m_d_skip,
             ssd_norm_g=m_ssd_norm_g, gm_norm_g=m_gm_norm_g, gm_norm_b=m_gm_norm_b,
             w_spatial=m_w_spatial, b_spatial=m_b_spatial, b_gate=m_b_gate, w_ssd_proj=m_w_ssd_proj,
             w_gm_proj=m_w_gm_proj, w_out=m_w_out, ln1_g=m_ln1_g, ln1_b=m_ln1_b, w_ff1=m_w_ff1,
             w_ff3=m_w_ff3, w_ff2=m_w_ff2, ln2_g=m_ln2_g, ln2_b=m_ln2_b)
    V = dict(c_ctx=v_c_ctx, ln0_g=v_ln0_g, ln0_b=v_ln0_b, w_ada=v_w_ada, b_ada=v_b_ada, w_in=v_w_in,
             conv_w=v_conv_w, conv_b=v_conv_b, dt_bias=v_dt_bias, a_log=v_a_log, d_skip=v_d_skip,
             ssd_norm_g=v_ssd_norm_g, gm_norm_g=v_gm_norm_g, gm_norm_b=v_gm_norm_b,
             w_spatial=v_w_spatial, b_spatial=v_b_spatial, b_gate=v_b_gate, w_ssd_proj=v_w_ssd_proj,
             w_gm_proj=v_w_gm_proj, w_out=v_w_out, ln1_g=v_ln1_g, ln1_b=v_ln1_b, w_ff1=v_w_ff1,
             w_ff3=v_w_ff3, w_ff2=v_w_ff2, ln2_g=v_ln2_g, ln2_b=v_ln2_b)

    me = 4 * lax.axis_index("x") + 2 * lax.axis_index("y") + lax.axis_index("c")
    xl, cx, tgt = x[0], ctx[0], loss_target[0]
    L = xl.shape[0]
    assert cx.shape[0] == TL and L % TL == 0
    ada_n = w_ada.shape[2]
    cw_n = conv_w.shape[2]

    small1 = _pad_rows(jnp.concatenate([c, _slab([conv_w[0]], 1)], axis=0), 8)
    g1 = _all_gather(small1, "ag_small")
    c_all = g1[:, 0, :]
    conv_w_full = g1[:, 1, :5 * cw_n].reshape(NDEV, 5, cw_n).transpose(1, 0, 2).reshape(5, NDEV * cw_n)
    sq = w_ssd_proj.shape[1]
    ffr = w_ff2.shape[1]
    ffc = w_ff1.shape[2]
    late = [jnp.concatenate([w_ssd_proj[0], w_gm_proj[0], w_out[0], w_ff2[0]], axis=0).astype(_MXU),
            _t(w_ff1[0]).astype(_MXU), _t(w_ff3[0]).astype(_MXU)]

    c16 = _pad_rows(jnp.concatenate([c_all, _row(c_ctx)], axis=0), 16)
    b_ada_sh = lax.dynamic_slice(b_ada, (0, ada_n * me), (1, ada_n))
    modp = _ada_fwd(c16, w_ada[0], b_ada_sh)
    mod16 = _all_gather(modp, "ag_mod").transpose(1, 0, 2).reshape(16, NDEV * ada_n)

    ga, = _all_gather_multi([w_in[0].astype(_MXU)], "ag_w_in")
    ga, late, mod16 = lax.optimization_barrier((ga, late, mod16))
    lw_send, lw_recv, lw_src, lw_land, lw_token = _exchange_start(late, "ag_late_start", gather=True)
    w_in_p = _perm_from_blocks(ga)
    modx = _pad_rows(lax.dynamic_slice(mod16, (me, 0), (1, 6 * D)).reshape(6, D), 8) + lw_token[0, 0]
    modc = _pad_rows(mod16[8].reshape(6, D), 8)

    g0, b0 = _row(ln0_g), _row(ln0_b)
    xn, h1 = _ln0_fwd(xl, cx, g0, b0, modx, modc)
    p = _mm(h1, w_in_p, "nn", F32, "mm_p")
    conv_w8 = _pad_rows(conv_w_full, 8)
    xbc = _conv_fwd(p, conv_w8, conv_b)
    prm = _pad_rows(jnp.pad(jnp.stack([dt_bias.reshape(32), a_log.reshape(32)]), ((0, 0), (0, 96))), 8)
    yf, yb, hpf, hpb = _ssd2_fwd(xbc, p, prm)
    lw_land = _exchange_wait(lw_send, lw_recv, lw_src, lw_land, yf, "ag_late_wait", gather=True)
    fw_send, fw_recv, lw_land, fw_token = _forward_start(lw_land, "ag_fwd_start")
    dsk = _row(jnp.repeat(d_skip[0, 0] + d_skip[0, 1], HP)) + fw_token[0:1, 0:1]
    ws_m = w_spatial[0].astype(_MXU)
    bsT = jnp.pad(b_spatial[0].T, ((0, 0), (0, 120)))
    mixp = (dsk, ssd_norm_g, gm_norm_g, gm_norm_b, ws_m, bsT)
    yssd, ygm = _mix_fwd(yf, yb, p, xbc, *mixp)
    gb, gc1, gc2 = _forward_wait(fw_send, fw_recv, lw_land, yssd, "ag_fwd_wait")

    gb, gc1, gc2 = lax.optimization_barrier(
        [lax.dynamic_update_index_in_dim(g, mine, me, 0) for g, mine in zip((gb, gc1, gc2), late)])
    w_ssd_f = gb[:, 0:sq].reshape(NDEV * sq, D)
    w_gm_f = gb[:, sq:2 * sq].reshape(NDEV * sq, D)
    w_out_f = gb[:, 2 * sq:3 * sq].reshape(NDEV * sq, D)
    w_ff2_f = gb[:, 3 * sq:3 * sq + ffr].reshape(NDEV * ffr, D)
    assert HFF == (NDEV // 2) * ffc
    hd = NDEV // 2
    w13i = jnp.concatenate([g[k] for t in range(2) for g in (gc1, gc2) for k in range(t * hd, (t + 1) * hd)],
                           axis=0)
    a1, a2, merged, out, r1, h2 = _merge_fwd(yssd, ygm, p, b_gate, w_ssd_f, w_gm_f, w_out_f,
                                             xn, modx, ln1_g, ln1_b)
    f13, ff = _mm_f13_glu(h2, w13i)

    dr2, do2, st2, loss_slab = _mm_o2_res2(ff, w_ff2_f, r1, tgt, modx, ln1_g, ln1_b, ln2_g, ln2_b)
    loss = lax.psum(loss_slab[0, 0], ("x", "y", "c"))
    df13 = _mm_dff_glu(do2, w_ff2_f, f13)
    dw_ff2 = _mm(ff, do2, "tn", _MXU, "mm_dw_ff2")
    dw13i = _mm(df13, h2, "tn", _MXU, "mm_dw13")

    def owner_blocks(first):
        return jnp.concatenate([dw13i[t * 2 * HFF + first:t * 2 * HFF + first + HFF].reshape(NDEV // 2, ffc, D)
                                for t in range(2)], axis=0)

    xff = [dw_ff2.reshape(NDEV, ffr, D), owner_blocks(0), owner_blocks(HFF)]
    ff_send, ff_recv, ff_src, ff_land, ff_token = _exchange_start(xff, "xchg_ff_start")
    modx = modx + ff_token[0, 0]
    dr1, dout, st1 = _mm_dh2_res1bwd(df13, w13i, dr2, r1, out, modx, ln1_g, ln1_b)
    dw_out = _mm(merged, dout, "tn", _MXU, "mm_dw_out")
    dp = jnp.zeros((L + TL, NPJ), _MXU)
    dp, da1, da2, stg, dys, dym = _merge_bwd(dout, a1, a2, p, b_gate, w_out_f, w_ssd_f, w_gm_f, dp)
    dw_ssd = _mm(yssd, da1, "tn", _MXU, "mm_dw_ssd")
    dw_gm = _mm(ygm, da2, "tn", _MXU, "mm_dw_gm")
    xsq = [jnp.concatenate([dw_ssd.reshape(NDEV, sq, D), dw_gm.reshape(NDEV, sq, D),
                            dw_out.reshape(NDEV, sq, D)], axis=1)]
    sq_send, sq_recv, sq_src, sq_land, sq_token = _exchange_start(xsq, "xchg_sq_start")
    mixp = (dsk + sq_token[0:1, 0:1],) + mixp[1:]
    dp, dyd, stm, dws, dbsT = _mix_bwd(dys, dym, yf, yb, p, xbc, dp, *mixp)
    dxf, dxb, ddf, ddb, sts = _ssd2_bwd(xbc, p, prm, dsk, dyd, hpf, hpb)
    dp, dcw, dcb = _conv_bwd(dxf, dxb, p, conv_w8, conv_b, dp)
    dp, std = _dt_bwd(ddf, ddb, dp)
    hw = D // 2
    xin_a = [_blocks_from_perm(_mm(h1[:, :hw], dp, "tn", _MXU, "mm_dw_in_a"), w_in.shape[2])]
    ina_send, ina_recv, ina_src, ina_land, ina_token = _exchange_start(xin_a, "xchg_in_a_start")
    h1b, ina_token = lax.optimization_barrier((h1[:, hw:], ina_token))
    xin_b = [_blocks_from_perm(_mm(h1b, dp, "tn", _MXU, "mm_dw_in_b"), w_in.shape[2])]
    inb_send, inb_recv, inb_src, inb_land, inb_token = _exchange_start(xin_b, "xchg_in_b_start")
    dp, inb_token = lax.optimization_barrier((dp, inb_token))
    dh1 = _mm(dp, w_in_p, "nt", F32, "mm_dh1")
    modx = modx + (ina_token[0, 0] + inb_token[0, 0])
    grad_x, st0 = _ln0_bwd(dh1, dr1, xl, cx, g0, b0, modx, modc)

    zero = jnp.zeros((D,), F32)
    dmod = jnp.stack([jnp.concatenate([st0[0], st0[1], st1[4], st1[1], st1[0], st2[2]]),
                      jnp.concatenate([st0[2], st0[3], zero, zero, zero, zero])])
    g16 = _all_gather(_pad_rows(dmod, 8), "ag_dmod")[:, 0:2, :].reshape(16, 6 * D)
    g16_sh = lax.dynamic_slice(g16, (0, ada_n * me), (16, ada_n))
    c16b = jnp.stack([c_all, jnp.broadcast_to(_row(c_ctx), (NDEV, D))], axis=1).reshape(16, D)
    dw_ada, db_ada8, dcc8 = _ada_bwd(c16b, g16, g16_sh, w_ada[0])

    part = dict(
        c_ctx=dcc8[0], ln0_g=st0[4], ln0_b=st0[5], conv_w=dcw[0:5], conv_b=dcb[0],
        dt_bias=std[0, 0:32], a_log=sts[0, 0:32], d_skip=jnp.tile(sts[1, 0:16], 2),
        ssd_norm_g=stm[0], gm_norm_g=stm[1], gm_norm_b=stm[2], w_spatial=dws,
        b_spatial=dbsT[:, 0:8].T, b_gate=stg[0], ln1_g=st1[2], ln1_b=st1[3], ln2_g=st2[0], ln2_b=st2[1])
    pnames = list(part)
    psum8 = _sum8(_all_gather(_slab([part[n] for n in pnames], SMALL_ROWS), "ag_smallgrads"), "sum_smallgrads")
    small = dict(zip(pnames, _unslab(psum8, [part[n].shape for n in pnames])))
    grads = {n: small[n].reshape(W[n].shape) for n in pnames if n != "conv_w"}
    grads["conv_w"] = lax.dynamic_slice(small["conv_w"], (0, cw_n * me), (5, cw_n)).reshape(conv_w.shape)
    grads["b_ada"] = db_ada8[0:1]
    grads["w_ada"] = dw_ada.reshape(w_ada.shape)

    delta, new_m, new_v = {}, {}, {}

    def adam_group(names, rows, tag, align=0):
        shapes = [W[n].shape for n in names]
        outs = _adamw(*[_slab([src[n] for n in names], rows, align) for src in (grads, W, M, V)], tag)
        for res, slab in zip((delta, new_m, new_v), outs):
            for n, a in zip(names, _unslab(slab, shapes, align)):
                res[n] = a

    adam_group(REPL + ["conv_w"], SMALL_ROWS, "adamw_small")
    res = _adamw(grads["w_ada"][0], w_ada[0], m_w_ada[0], v_w_ada[0], "adamw_w_ada")
    delta["w_ada"], new_m["w_ada"], new_v["w_ada"] = [a[None] for a in res]

    rff = _exchange_wait(ff_send, ff_recv, ff_src, ff_land, st0, "xchg_ff_wait")
    rsq = _exchange_wait(sq_send, sq_recv, sq_src, sq_land, rff[0], "xchg_sq_wait")
    rin_a = _exchange_wait(ina_send, ina_recv, ina_src, ina_land, delta["ln2_b"], "xchg_in_a_wait")
    rin_b = _exchange_wait(inb_send, inb_recv, inb_src, inb_land, rin_a[0], "xchg_in_b_wait")
    rin = jnp.concatenate([rin_a[0], rin_b[0]], axis=1)

    def own(blocks):
        return lax.dynamic_index_in_dim(blocks, me, 0, keepdims=False)

    own_in = jnp.concatenate([own(xin_a[0]), own(xin_b[0])], axis=0)
    for n, r8, mine, row0, tr in (
            ("w_ff2", rff[0], own(xff[0]), 0, ffr // 2), ("w_ssd_proj", rsq[0], own(xsq[0]), 0, sq),
            ("w_gm_proj", rsq[0], own(xsq[0]), sq, sq), ("w_out", rsq[0], own(xsq[0]), 2 * sq, sq),
            ("w_in", rin, own_in, 0, 256)):
        res = _adamw_sum(r8, mine, W[n][0], M[n][0], V[n][0], row0, tr, "adamw_" + n)
        grads[n], delta[n], new_m[n], new_v[n] = [a[None] for a in res]
    for n, r8, mine in (("w_ff1", rff[1], own(xff[1])), ("w_ff3", rff[2], own(xff[2]))):
        res = _adamw_sum(r8, mine, _t(W[n][0]), _t(M[n][0]), _t(V[n][0]), 0, ffc // 2, "adamw_" + n)
        grads[n], delta[n], new_m[n], new_v[n] = [_t(a)[None] for a in res]

    return (loss, grad_x[None], *[grads[n] for n in WEIGHTS], *[delta[n] for n in WEIGHTS],
            *[new_m[n] for n in WEIGHTS], *[new_v[n] for n in WEIGHTS])
```

```python
import functools

import jax
import jax.numpy as jnp
from jax import lax
from jax.experimental import pallas as pl
from jax.experimental.pallas import tpu as pltpu

_MXU = jnp.bfloat16
F32 = jnp.float32
D = 1024
TL = 256
Q = 128
NH, HP, NS, HPG = 16, 64, 128, 8
DFF = 2816
ALPHA = 2.0 ** 0.25
EPS = 1e-5
OZ, OU, OV, OXS, OG, OB, OC, ODT, NPJ = 0, 1024, 2048, 3072, 4096, 6144, 6400, 6656, 6912
NNAT = 6688
NDEV = 8
ADAM_LR, ADAM_B1, ADAM_B2, ADAM_EPS, ADAM_WD, ADAM_STEP = 1e-3, 0.9, 0.999, 1e-8, 0.01, 10
VMEM_LIMIT = 48 * 1024 * 1024

NN = ((1,), (0,))
NT = ((1,), (1,))
TN = ((0,), (0,))
MESH = pl.DeviceIdType.MESH


def _dot(a, b, dims):
    return lax.dot_general(a.astype(_MXU), b.astype(_MXU), (dims, ((), ())),
                           preferred_element_type=F32)


def _tile(n, cands):
    for c in cands:
        if n % c == 0:
            return c
    return n


def _divisor_tile(n, cap, mult):
    best = n
    for t in range(mult, min(n, cap) + 1, mult):
        if n % t == 0:
            best = t
    return best


def _params(sem):
    return pltpu.CompilerParams(dimension_semantics=sem, vmem_limit_bytes=VMEM_LIMIT)


def _cst(shape):
    nd = len(shape)
    return pl.BlockSpec(shape, lambda *_: (0,) * nd)


def _rt(w, cb=0, rows=TL):
    return pl.BlockSpec((rows, w), lambda i: (i, cb))


def _rtc(w, nt, cb=0):
    return pl.BlockSpec((TL, w), lambda i: (jnp.minimum(i, nt - 1), cb))


def _sig(x):
    return jax.nn.sigmoid(x)


def _softplus(x):
    return jnp.maximum(x, 0.0) + jnp.log1p(jnp.exp(-jnp.abs(x)))


_G0, _G1 = 0.7978845608028654, 0.044715


def _gelu(x):
    t = jnp.tanh(_G0 * (x + _G1 * x * x * x))
    return 0.5 * x * (1.0 + t), t


def _gelu_grad(x, t):
    return 0.5 * (1.0 + t) + 0.5 * x * (1.0 - t * t) * _G0 * (1.0 + 3.0 * _G1 * x * x)


def _ln(r):
    mu = jnp.mean(r, axis=-1, keepdims=True)
    xc = r - mu
    var = jnp.mean(xc * xc, axis=-1, keepdims=True)
    rstd = lax.rsqrt(var + EPS)
    return xc * rstd, rstd


def _ln_bwd(dyh, xhat, rstd):
    return rstd * (dyh - jnp.mean(dyh, axis=-1, keepdims=True)
                   - xhat * jnp.mean(dyh * xhat, axis=-1, keepdims=True))


def _colsum(v):
    return jnp.sum(v, axis=0, keepdims=True)


def _sum11(v):
    return jnp.sum(jnp.sum(v, axis=1, keepdims=True), axis=0, keepdims=True)


def _cumsum_rows(a, rev):
    n = a.shape[0]
    row = lax.broadcasted_iota(jnp.int32, a.shape, 0)
    s = 1
    while s < n:
        if rev:
            a = a + jnp.where(row < n - s, pltpu.roll(a, n - s, 0), 0.0)
        else:
            a = a + jnp.where(row >= s, pltpu.roll(a, s, 0), 0.0)
        s *= 2
    return a


def _mm(a, b, mode, out_dtype, name):
    if mode == "tn":
        K, M = a.shape
    else:
        M, K = a.shape
    N = b.shape[0] if mode == "nt" else b.shape[1]
    tm = _divisor_tile(M, 1408, 128) if mode == "tn" else _divisor_tile(M, 1088, 16)
    tn = _divisor_tile(N, 1408, 128)
    tk = _divisor_tile(K, 2304, 128)
    nk = K // tk
    dims = {"nn": NN, "nt": NT, "tn": TN}[mode]
    use_acc = nk > 1 and out_dtype != F32

    def body(a_ref, b_ref, o_ref, *acc):
        prod = _dot(a_ref[...], b_ref[...], dims)
        if nk == 1:
            o_ref[...] = prod.astype(o_ref.dtype)
            return
        acc_ref = acc[0] if use_acc else o_ref
        k = pl.program_id(2)

        @pl.when(k == 0)
        def _():
            acc_ref[...] = prod

        if use_acc:
            @pl.when((k > 0) & (k < nk - 1))
            def _():
                acc_ref[...] += prod

            @pl.when(k == nk - 1)
            def _():
                o_ref[...] = (acc_ref[...] + prod).astype(o_ref.dtype)
        else:
            @pl.when(k > 0)
            def _():
                o_ref[...] += prod

    if mode == "tn":
        a_spec = pl.BlockSpec((tk, tm), lambda i, j, k: (k, i))
    else:
        a_spec = pl.BlockSpec((tm, tk), lambda i, j, k: (i, k))
    if mode == "nt":
        b_spec = pl.BlockSpec((tn, tk), lambda i, j, k: (j, k))
    else:
        b_spec = pl.BlockSpec((tk, tn), lambda i, j, k: (k, j))
    return pl.pallas_call(
        body, name=name, grid=(M // tm, N // tn, nk),
        in_specs=[a_spec, b_spec],
        out_specs=pl.BlockSpec((tm, tn), lambda i, j, k: (i, j)),
        out_shape=jax.ShapeDtypeStruct((M, N), out_dtype),
        scratch_shapes=[pltpu.VMEM((tm, tn), F32)] if use_acc else [],
        compiler_params=_params(("parallel", "parallel", "arbitrary")),
    )(a, b)


def _all_gather(x, name):
    def body(x_ref, out_ref, send_sems, recv_sems, local_sem):
        mx, my, mc = lax.axis_index("x"), lax.axis_index("y"), lax.axis_index("c")
        me, sibling = (mx, my, mc), (mx, my, 1 - mc)
        chips = [(1 - mx, my), (mx, 1 - my), (1 - mx, 1 - my)]

        def slot(px, py, pc):
            return out_ref.at[4 * px + 2 * py + pc]

        def copy(k, block, to, src=None):
            return pltpu.make_async_remote_copy(
                src_ref=slot(*block) if src is None else src, dst_ref=slot(*block),
                send_sem=send_sems.at[k], recv_sem=recv_sems.at[k],
                device_id=to, device_id_type=MESH)

        mine = pltpu.make_async_copy(x_ref, slot(*me), local_sem)
        mine.start()
        first = [copy(0, me, sibling, src=x_ref)]
        first += [copy(1 + j, me, (*chip, mc), src=x_ref) for j, chip in enumerate(chips)]
        for cp in first:
            cp.start()
        passed = [copy(4 + j, (*chip, mc), sibling) for j, chip in enumerate(chips)]
        for j, chip in enumerate(chips):
            copy(1 + j, (*chip, mc), me).wait_recv()
            passed[j].start()
        copy(0, sibling, me).wait_recv()
        for j, chip in enumerate(chips):
            copy(4 + j, (*chip, 1 - mc), me).wait_recv()
        for cp in first + passed:
            cp.wait_send()
        mine.wait()

    return pl.pallas_call(
        body, name=name,
        out_shape=jax.ShapeDtypeStruct((NDEV,) + x.shape, x.dtype),
        in_specs=[pl.BlockSpec(memory_space=pl.ANY)],
        out_specs=pl.BlockSpec(memory_space=pl.ANY),
        scratch_shapes=[pltpu.SemaphoreType.DMA((7,)), pltpu.SemaphoreType.DMA((7,)),
                        pltpu.SemaphoreType.DMA],
    )(x)


def _owner_exchange(g, name):
    def body(g_ref, out_ref, send_sems, recv_sems, local_sem):
        mx, my, mc = lax.axis_index("x"), lax.axis_index("y"), lax.axis_index("c")
        local = pltpu.make_async_copy(g_ref.at[4 * mx + 2 * my + mc], out_ref.at[0], local_sem)
        local.start()
        copies = []
        for f in range(1, NDEV):
            px = 1 - mx if (f >> 2) & 1 else mx
            py = 1 - my if (f >> 1) & 1 else my
            pc = 1 - mc if f & 1 else mc
            cp = pltpu.make_async_remote_copy(
                src_ref=g_ref.at[4 * px + 2 * py + pc], dst_ref=out_ref.at[f],
                send_sem=send_sems.at[f - 1], recv_sem=recv_sems.at[f - 1],
                device_id=(px, py, pc), device_id_type=MESH)
            cp.start()
            copies.append(cp)
        for cp in copies:
            cp.wait_recv()
        for cp in copies:
            cp.wait_send()
        local.wait()

    return pl.pallas_call(
        body, name=name,
        out_shape=jax.ShapeDtypeStruct(g.shape, g.dtype),
        in_specs=[pl.BlockSpec(memory_space=pl.ANY)],
        out_specs=pl.BlockSpec(memory_space=pl.ANY),
        scratch_shapes=[pltpu.SemaphoreType.DMA((7,)), pltpu.SemaphoreType.DMA((7,)),
                        pltpu.SemaphoreType.DMA],
    )(g)


def _any_specs(n):
    return [pl.BlockSpec(memory_space=pl.ANY)] * n


def _all_gather_multi(xs, name):
    na = len(xs)

    def body(*refs):
        x_refs, out_refs = refs[:na], refs[na:2 * na]
        send_sems, recv_sems, local_sems = refs[2 * na:]
        mx, my, mc = lax.axis_index("x"), lax.axis_index("y"), lax.axis_index("c")
        me, sibling = (mx, my, mc), (mx, my, 1 - mc)
        chips = [(1 - mx, my), (mx, 1 - my), (1 - mx, 1 - my)]

        def copy(a, k, block, to, src=None):
            slot = out_refs[a].at[4 * block[0] + 2 * block[1] + block[2]]
            return pltpu.make_async_remote_copy(
                src_ref=slot if src is None else src, dst_ref=slot,
                send_sem=send_sems.at[7 * a + k], recv_sem=recv_sems.at[7 * a + k],
                device_id=to, device_id_type=MESH)

        mine = [pltpu.make_async_copy(x_refs[a], out_refs[a].at[4 * mx + 2 * my + mc], local_sems.at[a])
                for a in range(na)]
        for cp in mine:
            cp.start()
        first = []
        for a in range(na):
            first.append(copy(a, 0, me, sibling, src=x_refs[a]))
            first += [copy(a, 1 + j, me, (*chip, mc), src=x_refs[a]) for j, chip in enumerate(chips)]
        for cp in first:
            cp.start()
        passed = []
        for a in range(na):
            for j, chip in enumerate(chips):
                copy(a, 1 + j, (*chip, mc), me).wait_recv()
                fwd = copy(a, 4 + j, (*chip, mc), sibling)
                fwd.start()
                passed.append(fwd)
        for a in range(na):
            copy(a, 0, sibling, me).wait_recv()
            for j, chip in enumerate(chips):
                copy(a, 4 + j, (*chip, 1 - mc), me).wait_recv()
        for cp in first + passed:
            cp.wait_send()
        for cp in mine:
            cp.wait()

    return pl.pallas_call(
        body, name=name,
        out_shape=[jax.ShapeDtypeStruct((NDEV,) + x.shape, x.dtype) for x in xs],
        in_specs=_any_specs(na), out_specs=_any_specs(na),
        scratch_shapes=[pltpu.SemaphoreType.DMA((7 * na,)), pltpu.SemaphoreType.DMA((7 * na,)),
                        pltpu.SemaphoreType.DMA((na,))],
    )(*xs)


def _owner_exchange_multi(gs, name):
    na = len(gs)

    def body(*refs):
        g_refs, out_refs = refs[:na], refs[na:2 * na]
        send_sems, recv_sems, local_sems = refs[2 * na:]
        mx, my, mc = lax.axis_index("x"), lax.axis_index("y"), lax.axis_index("c")
        locals_ = [pltpu.make_async_copy(g_refs[a].at[4 * mx + 2 * my + mc], out_refs[a].at[0], local_sems.at[a])
                   for a in range(na)]
        for cp in locals_:
            cp.start()
        copies = []
        for a in range(na):
            for f in range(1, NDEV):
                px = 1 - mx if (f >> 2) & 1 else mx
                py = 1 - my if (f >> 1) & 1 else my
                pc = 1 - mc if f & 1 else mc
                cp = pltpu.make_async_remote_copy(
                    src_ref=g_refs[a].at[4 * px + 2 * py + pc], dst_ref=out_refs[a].at[f],
                    send_sem=send_sems.at[7 * a + f - 1], recv_sem=recv_sems.at[7 * a + f - 1],
                    device_id=(px, py, pc), device_id_type=MESH)
                cp.start()
                copies.append(cp)
        for cp in copies:
            cp.wait_recv()
        for cp in copies:
            cp.wait_send()
        for cp in locals_:
            cp.wait()

    return pl.pallas_call(
        body, name=name,
        out_shape=[jax.ShapeDtypeStruct(g.shape, g.dtype) for g in gs],
        in_specs=_any_specs(na), out_specs=_any_specs(na),
        scratch_shapes=[pltpu.SemaphoreType.DMA((7 * na,)), pltpu.SemaphoreType.DMA((7 * na,)),
                        pltpu.SemaphoreType.DMA((na,))],
    )(*gs)


def _adamw_sum(r8, own, w, m, v, row0, tr, name):
    R, C = w.shape
    assert row0 % tr == 0
    blk0 = row0 // tr
    bc1 = 1.0 - ADAM_B1 ** ADAM_STEP
    bc2 = 1.0 - ADAM_B2 ** ADAM_STEP

    def body(r_ref, *refs):
        if own is None:
            gg = r_ref[0].astype(F32)
        else:
            gg = refs[0][...].astype(F32)
            refs = refs[1:]
        w_ref, m_ref, v_ref, g_ref, d_ref, mo_ref, vo_ref = refs
        for k in range(1, NDEV):
            gg = gg + r_ref[k].astype(F32)
        mn = ADAM_B1 * m_ref[...] + (1.0 - ADAM_B1) * gg
        vn = ADAM_B2 * v_ref[...] + (1.0 - ADAM_B2) * (gg * gg)
        mh = mn / bc1
        vh = vn / bc2
        g_ref[...] = gg
        d_ref[...] = -ADAM_LR * (mh / (jnp.sqrt(vh) + ADAM_EPS) + ADAM_WD * w_ref[...])
        mo_ref[...] = mn
        vo_ref[...] = vn

    spec = pl.BlockSpec((tr, C), lambda i: (i, 0))
    sh = jax.ShapeDtypeStruct((R, C), F32)
    own_ops = [] if own is None else [own]
    own_specs = [] if own is None else [pl.BlockSpec((tr, C), lambda i: (i + blk0, 0))]
    return pl.pallas_call(
        body, name=name, grid=(R // tr,),
        in_specs=[pl.BlockSpec((NDEV, tr, C), lambda i: (0, i + blk0, 0))] + own_specs + [spec, spec, spec],
        out_specs=[spec] * 4, out_shape=[sh] * 4, compiler_params=_params(("parallel",)),
    )(r8, *own_ops, w, m, v)


_HBM = pl.BlockSpec(memory_space=pltpu.HBM)
_SEM = pl.BlockSpec(memory_space=pltpu.SEMAPHORE)
_EFFECT = pltpu.SideEffectType.DATAFLOW_SIDE_EFFECTING


def _exchange_copies(g_refs, land_refs, send_sems, recv_sems, gather):
    mx, my, mc = lax.axis_index("x"), lax.axis_index("y"), lax.axis_index("c")
    copies = []
    for a in range(len(g_refs)):
        for f in ((1, 2, 4, 6) if gather else range(1, NDEV)):
            px = 1 - mx if (f >> 2) & 1 else mx
            py = 1 - my if (f >> 1) & 1 else my
            pc = 1 - mc if f & 1 else mc
            src = g_refs[a] if gather else g_refs[a].at[4 * px + 2 * py + pc]
            dst = land_refs[a].at[4 * mx + 2 * my + mc] if gather else land_refs[a].at[f]
            copies.append(pltpu.make_async_remote_copy(
                src_ref=src, dst_ref=dst,
                send_sem=send_sems.at[7 * a + f - 1], recv_sem=recv_sems.at[7 * a + f - 1],
                device_id=(px, py, pc), device_id_type=MESH))
    return copies


def _exchange_start(gs, name, gather=False):
    na = len(gs)

    def body(*refs):
        for cp in _exchange_copies(refs[:na], refs[na:2 * na], refs[2 * na], refs[2 * na + 1], gather):
            cp.start()
        refs[-1][...] = jnp.zeros_like(refs[-1])

    hbm = [pltpu.HBM(g.shape, g.dtype) for g in gs]
    land_shapes = [((NDEV,) + g.shape) if gather else g.shape for g in gs]
    lands = [pltpu.with_memory_space_constraint(lax.empty(shp, g.dtype), pltpu.HBM)
             for shp, g in zip(land_shapes, gs)]
    hbm_land = [pltpu.HBM(shp, g.dtype) for shp, g in zip(land_shapes, gs)]
    outs = pl.pallas_call(
        body, name=name,
        out_shape=(pltpu.SemaphoreType.DMA((7 * na,)), pltpu.SemaphoreType.DMA((7 * na,)), *hbm, *hbm_land,
                   jax.ShapeDtypeStruct((8, 128), F32)),
        in_specs=[_HBM] * (2 * na),
        out_specs=(_SEM, _SEM, *([_HBM] * (2 * na)), pl.BlockSpec(memory_space=pltpu.VMEM)),
        input_output_aliases={i: 2 + i for i in range(2 * na)},
        compiler_params=pltpu.CompilerParams(has_side_effects=_EFFECT),
    )(*[pltpu.with_memory_space_constraint(g, pltpu.HBM) for g in gs], *lands)
    return outs[0], outs[1], outs[2:2 + na], outs[2 + na:2 + 2 * na], outs[-1]


def _forward_copies(land_refs, send_sems, recv_sems):
    mx, my, mc = lax.axis_index("x"), lax.axis_index("y"), lax.axis_index("c")
    copies = []
    for a in range(len(land_refs)):
        for j, (fx, fy) in enumerate(((0, 1), (1, 0), (1, 1))):
            px = 1 - mx if fx else mx
            py = 1 - my if fy else my
            blk = land_refs[a].at[4 * px + 2 * py + mc]
            copies.append(pltpu.make_async_remote_copy(
                src_ref=blk, dst_ref=blk, send_sem=send_sems.at[3 * a + j], recv_sem=recv_sems.at[3 * a + j],
                device_id=(mx, my, 1 - mc), device_id_type=MESH))
    return copies


def _forward_start(lands, name):
    na = len(lands)

    def body(*refs):
        for cp in _forward_copies(refs[:na], refs[na], refs[na + 1]):
            cp.start()
        refs[-1][...] = jnp.zeros_like(refs[-1])

    outs = pl.pallas_call(
        body, name=name,
        out_shape=(pltpu.SemaphoreType.DMA((3 * na,)), pltpu.SemaphoreType.DMA((3 * na,)),
                   *[pltpu.HBM(g.shape, g.dtype) for g in lands], jax.ShapeDtypeStruct((8, 128), F32)),
        in_specs=[_HBM] * na,
        out_specs=(_SEM, _SEM, *([_HBM] * na), pl.BlockSpec(memory_space=pltpu.VMEM)),
        input_output_aliases={i: 2 + i for i in range(na)},
        compiler_params=pltpu.CompilerParams(has_side_effects=_EFFECT),
    )(*lands)
    return outs[0], outs[1], outs[2:2 + na], outs[-1]


def _forward_wait(send_sems, recv_sems, lands, after, name):
    na = len(lands)

    def body(*refs):
        for cp in _forward_copies(refs[:na], refs[na], refs[na + 1]):
            cp.wait_send()
            cp.wait_recv()

    return pl.pallas_call(
        body, name=name,
        out_shape=tuple(pltpu.HBM(g.shape, g.dtype) for g in lands),
        in_specs=[_HBM] * na + [_SEM, _SEM, pl.BlockSpec(memory_space=pl.ANY)],
        out_specs=tuple([_HBM] * na),
        input_output_aliases={i: i for i in range(na)},
        compiler_params=pltpu.CompilerParams(has_side_effects=_EFFECT),
    )(*lands, send_sems, recv_sems, after)


def _exchange_wait(send_sems, recv_sems, g_thru, land_thru, after, name, gather=False):
    na = len(g_thru)

    def body(*refs):
        for cp in _exchange_copies(refs[:na], refs[na:2 * na], refs[2 * na], refs[2 * na + 1], gather):
            cp.wait_send()
            cp.wait_recv()

    outs = pl.pallas_call(
        body, name=name,
        out_shape=tuple(pltpu.HBM(g.shape, g.dtype) for g in list(g_thru) + list(land_thru)),
        in_specs=[_HBM] * (2 * na) + [_SEM, _SEM, pl.BlockSpec(memory_space=pl.ANY)],
        out_specs=tuple([_HBM] * (2 * na)),
        input_output_aliases={i: i for i in range(2 * na)},
        compiler_params=pltpu.CompilerParams(has_side_effects=_EFFECT),
    )(*g_thru, *land_thru, send_sems, recv_sems, after)
    return outs[na:]


def _sum8(r, name):
    _, R, C = r.shape
    tr = _tile(R, (256, 160, 128, 64, 32, 16, 8))

    def body(r_ref, o_ref):
        acc = r_ref[0].astype(F32)
        for k in range(1, NDEV):
            acc = acc + r_ref[k].astype(F32)
        o_ref[...] = acc

    return pl.pallas_call(
        body, name=name, grid=(R // tr,),
        in_specs=[pl.BlockSpec((NDEV, tr, C), lambda i: (0, i, 0))],
        out_specs=pl.BlockSpec((tr, C), lambda i: (i, 0)),
        out_shape=jax.ShapeDtypeStruct((R, C), F32),
        compiler_params=_params(("parallel",)),
    )(r)


def _adamw(g, w, m, v, name):
    R, C = g.shape
    tr = _tile(R, (256, 160, 128, 64, 32, 16, 8))
    bc1 = 1.0 - ADAM_B1 ** ADAM_STEP
    bc2 = 1.0 - ADAM_B2 ** ADAM_STEP

    def body(g_ref, w_ref, m_ref, v_ref, d_ref, mo_ref, vo_ref):
        gg = g_ref[...]
        mn = ADAM_B1 * m_ref[...] + (1.0 - ADAM_B1) * gg
        vn = ADAM_B2 * v_ref[...] + (1.0 - ADAM_B2) * (gg * gg)
        mh = mn / bc1
        vh = vn / bc2
        d_ref[...] = -ADAM_LR * (mh / (jnp.sqrt(vh) + ADAM_EPS) + ADAM_WD * w_ref[...])
        mo_ref[...] = mn
        vo_ref[...] = vn

    spec = pl.BlockSpec((tr, C), lambda i: (i, 0))
    sh = jax.ShapeDtypeStruct((R, C), F32)
    return pl.pallas_call(
        body, name=name, grid=(R // tr,), in_specs=[spec] * 4, out_specs=[spec] * 3,
        out_shape=[sh] * 3, compiler_params=_params(("parallel",)),
    )(g, w, m, v)


def _ada_fwd(c16, w_sh, b_sh):
    def body(c_ref, w_ref, b_ref, o_ref):
        c = c_ref[...]
        o_ref[...] = _dot(c * _sig(c), w_ref[...], NN) + b_ref[...]

    return pl.pallas_call(
        body, name="ada_fwd", out_shape=jax.ShapeDtypeStruct((16, w_sh.shape[1]), F32),
        compiler_params=pltpu.CompilerParams(vmem_limit_bytes=VMEM_LIMIT),
    )(c16, w_sh, b_sh)


def _ada_bwd(c16, g16, g16_sh, w_sh):
    ncol = w_sh.shape[1]

    def body(c_ref, g_ref, gs_ref, w_ref, dw_ref, db_ref, dc_ref):
        c = c_ref[...]
        s = _sig(c)
        gs = gs_ref[...]
        dw_ref[...] = _dot(c * s, gs, TN)
        db_ref[...] = jnp.broadcast_to(_colsum(g_ref[...]), db_ref.shape)
        odd = lax.broadcasted_iota(jnp.int32, gs.shape, 0) % 2 == 1
        gc = _colsum(jnp.where(odd, gs, 0.0))
        ds = _dot(jnp.broadcast_to(gc, (8, ncol)), w_ref[...], NT)
        c1 = c[1:2, :]
        s1 = s[1:2, :]
        dc_ref[...] = ds * (s1 * (1.0 + c1 * (1.0 - s1)))

    return pl.pallas_call(
        body, name="ada_bwd",
        out_shape=[jax.ShapeDtypeStruct(w_sh.shape, F32),
                   jax.ShapeDtypeStruct((8, g16.shape[1]), F32),
                   jax.ShapeDtypeStruct((8, D), F32)],
        compiler_params=pltpu.CompilerParams(vmem_limit_bytes=VMEM_LIMIT),
    )(c16, g16, g16_sh, w_sh)


def _ln0_fwd(x, ctx, g, b, modx, modc):
    L = x.shape[0]
    nt = L // TL

    def body(x_ref, c_ref, g_ref, b_ref, mx_ref, mc_ref, xn_ref, h_ref):
        isc = pl.program_id(0) == nt
        xin = jnp.where(isc, c_ref[...], x_ref[...])
        sh = jnp.where(isc, mc_ref[0:1, :], mx_ref[0:1, :])
        sc = jnp.where(isc, mc_ref[1:2, :], mx_ref[1:2, :])
        xhat, _ = _ln(xin)
        xn = xhat * g_ref[...] + b_ref[...]
        xn_ref[...] = xn
        h_ref[...] = (xn * (1.0 + sc) + sh).astype(h_ref.dtype)

    return pl.pallas_call(
        body, name="ln0_fwd", grid=(nt + 1,),
        in_specs=[_rtc(D, nt), _cst((TL, D)), _cst((1, D)), _cst((1, D)), _cst((8, D)), _cst((8, D))],
        out_specs=[_rt(D), _rt(D)],
        out_shape=[jax.ShapeDtypeStruct((L + TL, D), F32), jax.ShapeDtypeStruct((L + TL, D), _MXU)],
        compiler_params=_params(("parallel",)),
    )(x, ctx, g, b, modx, modc)


def _xbc_colblk(j):
    return jnp.where(j < 8, OXS // 128 + j, OB // 128 + j - 8)


def _conv_taps(p_ref, r0, first, last):
    main = p_ref[pl.ds(r0, TL), :]
    zero = jnp.zeros((8, main.shape[1]), F32)
    prev = zero if first else p_ref[pl.ds(r0 - 8, 8), :]
    nxt = zero if last else p_ref[pl.ds(r0 + TL, 8), :]
    ext = jnp.concatenate([prev, main, nxt], axis=0)
    n = TL + 16
    return [pltpu.roll(ext, (2 - k) % n, 0)[8:8 + TL] for k in range(5)]


def _seq_chunks(L):
    nt = L // TL
    return [(r * TL, r == 0, r == nt - 1) for r in range(nt)] + [(L, True, True)]


def _conv_fwd(p, conv_w8, conv_b):
    RT = p.shape[0]
    L = RT - TL
    chunks = _seq_chunks(L)

    def body(p_ref, w_ref, b_ref, o_ref):
        w = w_ref[...]
        bias = b_ref[...]
        for r0, first, last in chunks:
            taps = _conv_taps(p_ref, r0, first, last)
            pre = bias + sum(w[k:k + 1, :] * taps[k] for k in range(5))
            o_ref[pl.ds(r0, TL), :] = pre * _sig(pre)

    return pl.pallas_call(
        body, name="conv_fwd", grid=(12,),
        in_specs=[pl.BlockSpec((RT, 128), lambda j: (0, _xbc_colblk(j))),
                  pl.BlockSpec((8, 128), lambda j: (0, j)),
                  pl.BlockSpec((1, 128), lambda j: (0, j))],
        out_specs=pl.BlockSpec((RT, 128), lambda j: (0, j)),
        out_shape=jax.ShapeDtypeStruct((RT, 1536), F32),
        compiler_params=_params(("parallel",)),
    )(p, conv_w8, conv_b)


def _ssd_common(dtraw, dtb, a32, rev):
    dt = _softplus(dtraw + dtb)
    acum = _cumsum_rows(dt * a32, rev)
    ii = lax.broadcasted_iota(jnp.int32, (Q, Q), 0)
    jj = lax.broadcasted_iota(jnp.int32, (Q, Q), 1)
    mask = (ii <= jj) if rev else (ii >= jj)
    return dt, acum, acum.T, dt.T, mask


def _ssd_orders(ncl, ncc):
    nc = ncl + ncc

    def cf(s):
        return jnp.where(s < ncc, ncl + s, s - ncc)

    def cb(s):
        return nc - 1 - s

    return cf, cb


def _ssd_fwd(xbc, p, prm):
    RT = xbc.shape[0]
    nc = RT // Q
    ncc = TL // Q
    cf, cb = _ssd_orders(nc - ncc, ncc)

    def one_dir(x_ref, dt_ref, prm_ref, y_ref, hp_ref, H_ref, d):
        rev = d == 1
        a32 = -jnp.exp(prm_ref[1:2, :])
        dt, acum, acumT, dtT, mask = _ssd_common(dt_ref[...], prm_ref[0:1, :], a32, rev)
        end = 0 if rev else Q - 1
        for g in range(2):
            Bg = x_ref[:, D + g * NS:D + (g + 1) * NS]
            Cg = x_ref[:, D + 2 * NS + g * NS:D + 2 * NS + (g + 1) * NS]
            CB = _dot(Cg, Bg, NT)
            for hh in range(HPG):
                h = g * HPG + hh
                ln = 16 * d + h
                col = acum[:, ln:ln + 1]
                rowv = acumT[ln:ln + 1, :]
                a_end = rowv[:, end:end + 1]
                Lm = jnp.exp(jnp.where(mask, col - rowv, -1e30))
                W = CB * Lm * dtT[ln:ln + 1, :]
                Xh = x_ref[:, h * HP:(h + 1) * HP]
                Hp = H_ref[h * HP:(h + 1) * HP, :]
                y = _dot(W, Xh, NN) + jnp.exp(col) * _dot(Cg, Hp, NT)
                y_ref[:, h * HP:(h + 1) * HP] = y
                dcol = jnp.exp(a_end - col) * dt[:, ln:ln + 1]
                hp_ref[0, h * HP:(h + 1) * HP, :] = Hp
                H_ref[h * HP:(h + 1) * HP, :] = jnp.exp(a_end) * Hp + _dot(Xh * dcol, Bg, TN)

    def body(xf_ref, xb_ref, df_ref, db_ref, prm_ref, yf_ref, yb_ref, hf_ref, hb_ref, Hf, Hb):
        @pl.when(pl.program_id(0) == 0)
        def _():
            Hf[...] = jnp.zeros_like(Hf)
            Hb[...] = jnp.zeros_like(Hb)

        one_dir(xf_ref, df_ref, prm_ref, yf_ref, hf_ref, Hf, 0)
        one_dir(xb_ref, db_ref, prm_ref, yb_ref, hb_ref, Hb, 1)

    ysh = jax.ShapeDtypeStruct((RT, D), F32)
    hsh = jax.ShapeDtypeStruct((nc, NH * HP, NS), F32)
    hspec = pl.BlockSpec((1, NH * HP, NS), lambda s: (s, 0, 0))
    return pl.pallas_call(
        body, name="ssd_fwd", grid=(nc,),
        in_specs=[pl.BlockSpec((Q, 1536), lambda s: (cf(s), 0)),
                  pl.BlockSpec((Q, 1536), lambda s: (cb(s), 0)),
                  pl.BlockSpec((Q, 128), lambda s: (cf(s), ODT // 128)),
                  pl.BlockSpec((Q, 128), lambda s: (cb(s), ODT // 128)),
                  _cst((8, 128))],
        out_specs=[pl.BlockSpec((Q, D), lambda s: (cf(s), 0)),
                   pl.BlockSpec((Q, D), lambda s: (cb(s), 0)), hspec, hspec],
        out_shape=[ysh, ysh, hsh, hsh],
        scratch_shapes=[pltpu.VMEM((NH * HP, NS), F32), pltpu.VMEM((NH * HP, NS), F32)],
        compiler_params=_params(("arbitrary",)),
    )(xbc, xbc, p, p, prm)


def _ssd_bwd(xbc, p, prm, dsk, dyd, hpf, hpb):
    RT = xbc.shape[0]
    nc = RT // Q
    ncc = TL // Q
    ncl = nc - ncc
    cf, cb = _ssd_orders(ncl, ncc)

    def rs(t):
        return nc - 1 - t

    def one_dir(x_ref, dt_ref, prm_ref, dsk_ref, dy_ref, is_ctx, hp_ref, dH_ref,
                dx_ref, ddt_ref, st_ref, d):
        rev = d == 1
        a32 = -jnp.exp(prm_ref[1:2, :])
        dtraw = dt_ref[...]
        dtb = prm_ref[0:1, :]
        dt, acum, acumT, dtT, mask = _ssd_common(dtraw, dtb, a32, rev)
        end = 0 if rev else Q - 1
        lane = lax.broadcasted_iota(jnp.int32, (Q, 128), 1)
        srow = lax.broadcasted_iota(jnp.int32, (Q, 128), 0)
        dyscale = jnp.where(is_ctx, 0.0, 1.0)
        c_dacum = jnp.zeros((Q, 128), F32)
        r_dacum = jnp.zeros((Q, 128), F32)
        c_ddt = jnp.zeros((Q, 128), F32)
        r_ddt = jnp.zeros((Q, 128), F32)
        dskacc = jnp.zeros((1, 128), F32)
        for g in range(2):
            Bg = x_ref[:, D + g * NS:D + (g + 1) * NS]
            Cg = x_ref[:, D + 2 * NS + g * NS:D + 2 * NS + (g + 1) * NS]
            CB = _dot(Cg, Bg, NT)
            dCB = jnp.zeros((Q, Q), F32)
            dBg = jnp.zeros((Q, NS), F32)
            dCg = jnp.zeros((Q, NS), F32)
            for hh in range(HPG):
                h = g * HPG + hh
                ln = 16 * d + h
                hs = slice(h * HP, (h + 1) * HP)
                col = acum[:, ln:ln + 1]
                rowv = acumT[ln:ln + 1, :]
                dtr = dtT[ln:ln + 1, :]
                dtc = dt[:, ln:ln + 1]
                a_end = rowv[:, end:end + 1]
                Lm = jnp.exp(jnp.where(mask, col - rowv, -1e30))
                E = jnp.exp(col)
                ecol = jnp.exp(a_end - col)
                dcol = ecol * dtc
                Xh = x_ref[:, hs]
                dY = dy_ref[:, hs] * dyscale
                Hp = hp_ref[0, hs, :]
                dHn = dH_ref[hs, :]
                W = CB * Lm * dtr
                dW = _dot(dY, Xh, NT)
                Mm = dW * CB * Lm
                T = Mm * dtr
                dCB = dCB + dW * Lm * dtr
                BdH = _dot(Bg, dHn, NT)
                dX = _dot(W, dY, TN) + dcol * BdH
                if d == 0:
                    dX = dX + dY * dsk_ref[:, hs]
                    dskacc = dskacc + jnp.where(lane[0:1, :] == h, _sum11(dY * Xh), 0.0)
                dx_ref[:, hs] = dX
                xb = jnp.sum(Xh * BdH, axis=1, keepdims=True)
                scol = dcol * xb
                G = _dot(dY, Hp, NN)
                dCg = dCg + E * G
                qcol = E * jnp.sum(G * Cg, axis=1, keepdims=True)
                dBg = dBg + _dot(Xh * dcol, dHn, NN)
                dH_ref[hs, :] = jnp.exp(a_end) * dHn + _dot(dY * E, Cg, TN)
                eterm = jnp.exp(a_end) * _sum11(dHn * Hp) + _sum11(scol)
                cvec = jnp.sum(T, axis=1, keepdims=True) + qcol - scol
                cvec = cvec + jnp.where(srow[:, 0:1] == end, eterm, 0.0)
                c_dacum = c_dacum + jnp.where(lane == ln, cvec, 0.0)
                r_dacum = r_dacum - jnp.where(srow == ln, _colsum(T), 0.0)
                c_ddt = c_ddt + jnp.where(lane == ln, ecol * xb, 0.0)
                r_ddt = r_ddt + jnp.where(srow == ln, _colsum(Mm), 0.0)
            dBg = dBg + _dot(dCB, Cg, TN)
            dCg = dCg + _dot(dCB, Bg, NN)
            dx_ref[:, D + g * NS:D + (g + 1) * NS] = dBg
            dx_ref[:, D + 2 * NS + g * NS:D + 2 * NS + (g + 1) * NS] = dCg
        dacum = c_dacum + r_dacum.T
        da = _cumsum_rows(dacum, not rev)
        mine = (lane >= 16 * d) & (lane < 16 * d + 16)
        ddt = jnp.where(mine, c_ddt + r_ddt.T + da * a32, 0.0)
        ddt_ref[...] = ddt * _sig(dtraw + dtb)
        st_ref[0:1, :] += _colsum(jnp.where(mine, da * dt, 0.0))
        if d == 0:
            st_ref[1:2, :] += dskacc

    def body(xf_ref, xb_ref, df_ref, db_ref, prm_ref, dsk_ref, dyf_ref, dyb_ref, hf_ref, hb_ref,
             dxf_ref, dxb_ref, ddf_ref, ddb_ref, st_ref, dHf, dHb):
        t = pl.program_id(0)

        @pl.when(t == 0)
        def _():
            dHf[...] = jnp.zeros_like(dHf)
            dHb[...] = jnp.zeros_like(dHb)
            st_ref[...] = jnp.zeros_like(st_ref)

        s = rs(t)
        one_dir(xf_ref, df_ref, prm_ref, dsk_ref, dyf_ref, cf(s) >= ncl, hf_ref, dHf,
                dxf_ref, ddf_ref, st_ref, 0)
        one_dir(xb_ref, db_ref, prm_ref, dsk_ref, dyb_ref, cb(s) >= ncl, hb_ref, dHb,
                dxb_ref, ddb_ref, st_ref, 1)

        @pl.when(t == nc - 1)
        def _():
            st_ref[0:1, :] = -jnp.exp(prm_ref[1:2, :]) * st_ref[0:1, :]

    def lat(c):
        return jnp.minimum(c, ncl - 1)

    xsh = jax.ShapeDtypeStruct((RT, 1536), F32)
    dsh = jax.ShapeDtypeStruct((RT, 128), F32)
    hspec = pl.BlockSpec((1, NH * HP, NS), lambda t: (rs(t), 0, 0))
    return pl.pallas_call(
        body, name="ssd_bwd", grid=(nc,),
        in_specs=[pl.BlockSpec((Q, 1536), lambda t: (cf(rs(t)), 0)),
                  pl.BlockSpec((Q, 1536), lambda t: (cb(rs(t)), 0)),
                  pl.BlockSpec((Q, 128), lambda t: (cf(rs(t)), ODT // 128)),
                  pl.BlockSpec((Q, 128), lambda t: (cb(rs(t)), ODT // 128)),
                  _cst((8, 128)), _cst((1, D)),
                  pl.BlockSpec((Q, D), lambda t: (lat(cf(rs(t))), 0)),
                  pl.BlockSpec((Q, D), lambda t: (lat(cb(rs(t))), 0)),
                  hspec, hspec],
        out_specs=[pl.BlockSpec((Q, 1536), lambda t: (cf(rs(t)), 0)),
                   pl.BlockSpec((Q, 1536), lambda t: (cb(rs(t)), 0)),
                   pl.BlockSpec((Q, 128), lambda t: (cf(rs(t)), 0)),
                   pl.BlockSpec((Q, 128), lambda t: (cb(rs(t)), 0)),
                   _cst((8, 128))],
        out_shape=[xsh, xsh, dsh, dsh, jax.ShapeDtypeStruct((8, 128), F32)],
        scratch_shapes=[pltpu.VMEM((NH * HP, NS), F32), pltpu.VMEM((NH * HP, NS), F32)],
        compiler_params=_params(("arbitrary",)),
    )(xbc, xbc, p, p, prm, dsk, dyd, dyd, hpf, hpb)


def _lane_bcast(v, ln):
    return jnp.broadcast_to(v[:, ln:ln + 1], v.shape)


def _halves(v, lo, axis):
    return jnp.concatenate([jnp.where(lo, v, 0.0), jnp.where(lo, 0.0, v)], axis=axis)


def _ssd2_fwd(xbc, p, prm):
    RT = xbc.shape[0]
    nc = RT // Q
    ncc = TL // Q
    cf, cb = _ssd_orders(nc - ncc, ncc)

    def one_dir(x_ref, dt_ref, prm_ref, y_ref, hp_ref, HT_ref, d):
        rev = d == 1
        a32 = -jnp.exp(prm_ref[1:2, :])
        dt, acum, acumT, dtT, mask = _ssd_common(dt_ref[...], prm_ref[0:1, :], a32, rev)
        end = 0 if rev else Q - 1
        lo = lax.broadcasted_iota(jnp.int32, (Q, 128), 1) < HP
        for g in range(2):
            Bg = x_ref[:, D + g * NS:D + (g + 1) * NS]
            Cg = x_ref[:, D + 2 * NS + g * NS:D + 2 * NS + (g + 1) * NS]
            CB = _dot(Cg, Bg, NT)
            xds, svs = [], []
            for q in range(HPG // 2):
                pi = g * (HPG // 2) + q
                ps = slice(pi * 128, (pi + 1) * 128)
                Xp = x_ref[:, ps]
                HTp = HT_ref[:, ps]
                lhs, dcs, sv = [], [], []
                ces = []
                for h in (2 * pi, 2 * pi + 1):
                    ln = 16 * d + h
                    colB = _lane_bcast(acum, ln)
                    rowv = acumT[ln:ln + 1, :]
                    aend = colB[end:end + 1, :]
                    Lm = jnp.exp(jnp.where(mask, colB - rowv, -1e30))
                    lhs.append(CB * Lm * dtT[ln:ln + 1, :])
                    ces.append(Cg * jnp.exp(colB))
                    dcs.append(jnp.exp(aend - colB) * _lane_bcast(dt, ln))
                    sv.append(jnp.exp(aend))
                lhs = jnp.concatenate(lhs + ces, axis=1)
                rhs = jnp.concatenate([_halves(Xp, lo, 0), _halves(HTp, lo, 0)], axis=0)
                y_ref[:, ps] = _dot(lhs, rhs, NN)
                xds.append(Xp * jnp.where(lo, dcs[0], dcs[1]))
                svs.append(jnp.where(lo[0:1, :], sv[0], sv[1]))
            gs = slice(g * 512, (g + 1) * 512)
            HTg = HT_ref[:, gs]
            hp_ref[0, :, gs] = HTg
            st = _dot(Bg.T, jnp.concatenate(xds, axis=1), NN)
            HT_ref[:, gs] = jnp.concatenate(svs, axis=1) * HTg + st

    def body(xf_ref, xb_ref, df_ref, db_ref, prm_ref, yf_ref, yb_ref, hf_ref, hb_ref, Hf, Hb):
        @pl.when(pl.program_id(0) == 0)
        def _():
            Hf[...] = jnp.zeros_like(Hf)
            Hb[...] = jnp.zeros_like(Hb)

        one_dir(xf_ref, df_ref, prm_ref, yf_ref, hf_ref, Hf, 0)
        one_dir(xb_ref, db_ref, prm_ref, yb_ref, hb_ref, Hb, 1)

    ysh = jax.ShapeDtypeStruct((RT, D), F32)
    hsh = jax.ShapeDtypeStruct((nc, NS, NH * HP), F32)
    hspec = pl.BlockSpec((1, NS, NH * HP), lambda s: (s, 0, 0))
    return pl.pallas_call(
        body, name="ssd_fwd", grid=(nc,),
        in_specs=[pl.BlockSpec((Q, 1536), lambda s: (cf(s), 0)),
                  pl.BlockSpec((Q, 1536), lambda s: (cb(s), 0)),
                  pl.BlockSpec((Q, 128), lambda s: (cf(s), ODT // 128)),
                  pl.BlockSpec((Q, 128), lambda s: (cb(s), ODT // 128)),
                  _cst((8, 128))],
        out_specs=[pl.BlockSpec((Q, D), lambda s: (cf(s), 0)),
                   pl.BlockSpec((Q, D), lambda s: (cb(s), 0)), hspec, hspec],
        out_shape=[ysh, ysh, hsh, hsh],
        scratch_shapes=[pltpu.VMEM((NS, NH * HP), F32), pltpu.VMEM((NS, NH * HP), F32)],
        compiler_params=_params(("arbitrary",)),
    )(xbc, xbc, p, p, prm)


def _ssd2_bwd(xbc, p, prm, dsk, dyd, hpf, hpb):
    RT = xbc.shape[0]
    nc = RT // Q
    ncc = TL // Q
    ncl = nc - ncc
    cf, cb = _ssd_orders(ncl, ncc)

    def rs(t):
        return nc - 1 - t

    def one_dir(x_ref, dt_ref, prm_ref, dsk_ref, dy_ref, is_ctx, hp_ref, dHT_ref,
                dx_ref, ddt_ref, st_ref, d):
        rev = d == 1
        a32 = -jnp.exp(prm_ref[1:2, :])
        dtraw = dt_ref[...]
        dtb = prm_ref[0:1, :]
        dt, acum, acumT, _, _ = _ssd_common(dtraw, dtb, a32, rev)
        end = 0 if rev else Q - 1
        lane = lax.broadcasted_iota(jnp.int32, (Q, 128), 1)
        srow = lax.broadcasted_iota(jnp.int32, (Q, 128), 0)
        maskT = (lane <= srow) if rev else (lane >= srow)
        lo = lane < HP
        lo1 = lo[0:1, :]
        dyscale = jnp.where(is_ctx, 0.0, 1.0)
        c_dacum = jnp.zeros((Q, 128), F32)
        r_dacum = jnp.zeros((Q, 128), F32)
        c_ddt = jnp.zeros((Q, 128), F32)
        dskacc = jnp.zeros((1, 128), F32)
        for g in range(2):
            gs = slice(g * 512, (g + 1) * 512)
            Bg = x_ref[:, D + g * NS:D + (g + 1) * NS]
            Cg = x_ref[:, D + 2 * NS + g * NS:D + 2 * NS + (g + 1) * NS]
            CBT = _dot(Bg, Cg, NT)
            HTg = hp_ref[0, :, gs]
            dHTg = dHT_ref[:, gs]
            BdHg = _dot(Bg, dHTg, NN)
            dCBT = jnp.zeros((Q, Q), F32)
            dCg = jnp.zeros((Q, NS), F32)
            xds, dyes, svs = [], [], []
            for q in range(HPG // 2):
                pi = g * (HPG // 2) + q
                ps = slice(pi * 128, (pi + 1) * 128)
                qs = slice(q * 128, (q + 1) * 128)
                Xp = x_ref[:, ps]
                dYp = dy_ref[:, ps] * dyscale
                HTp = HTg[:, qs]
                BdHp = BdHg[:, qs]
                dY2 = _halves(dYp, lo, 0)
                dWT2 = _dot(_halves(Xp, lo, 0), dYp.T, NN)
                G2 = _dot(dY2, HTp, NT)
                XB = Xp * BdHp
                hh = _colsum(dHTg[:, qs] * HTp)
                yx = _colsum(dYp * Xp)
                wts, dcs, ebs, sv = [], [], [], []
                for k, h in enumerate((2 * pi, 2 * pi + 1)):
                    ln = 16 * d + h
                    half = lo if k == 0 else jnp.logical_not(lo)
                    half1 = half[0:1, :]
                    colB = _lane_bcast(acum, ln)
                    dtcB = _lane_bcast(dt, ln)
                    rowv = acumT[ln:ln + 1, :]
                    aend = colB[end:end + 1, :]
                    LmT = jnp.exp(jnp.where(maskT, rowv - colB, -1e30))
                    WT = CBT * LmT * dtcB
                    dWT = dWT2[k * Q:(k + 1) * Q, :]
                    U = dWT * LmT
                    MT = U * CBT
                    rM = jnp.sum(MT, axis=1, keepdims=True)
                    rT = _colsum(MT * dtcB)
                    dCBT = dCBT + U * dtcB
                    ecol = jnp.exp(aend - colB)
                    EB = jnp.exp(colB)
                    Gk = G2[k * Q:(k + 1) * Q, :]
                    dCg = dCg + EB * Gk
                    qcol = jnp.sum(EB * Gk * Cg, axis=1, keepdims=True)
                    xb = jnp.sum(jnp.where(half, XB, 0.0), axis=1, keepdims=True)
                    e1 = ecol[:, 0:1]
                    dt1 = dtcB[:, 0:1]
                    scol = e1 * dt1 * xb
                    sA = jnp.exp(aend)
                    eterm = sA[:, 0:1] * jnp.sum(jnp.where(half1, hh, 0.0), axis=1, keepdims=True) \
                        + _colsum(scol)
                    cvec = qcol - dt1 * rM - scol + jnp.where(srow[:, 0:1] == end, eterm, 0.0)
                    c_dacum = jnp.where(lane == ln, cvec, c_dacum)
                    r_dacum = jnp.where(srow == ln, rT, r_dacum)
                    c_ddt = jnp.where(lane == ln, rM + e1 * xb, c_ddt)
                    if d == 0:
                        dskacc = dskacc + jnp.where(
                            lane[0:1, :] == h, jnp.sum(jnp.where(half1, yx, 0.0), axis=1, keepdims=True), 0.0)
                    wts.append(WT)
                    dcs.append(ecol * dtcB)
                    ebs.append(EB)
                    sv.append(sA)
                dcp = jnp.where(lo, dcs[0], dcs[1])
                dX = _dot(jnp.concatenate(wts, axis=1), dY2, NN) + dcp * BdHp
                if d == 0:
                    dX = dX + dYp * dsk_ref[:, ps]
                dx_ref[:, ps] = dX
                xds.append(Xp * dcp)
                dyes.append(dYp * jnp.where(lo, ebs[0], ebs[1]))
                svs.append(jnp.where(lo1, sv[0], sv[1]))
            dx_ref[:, D + g * NS:D + (g + 1) * NS] = (
                _dot(jnp.concatenate(xds, axis=1), dHTg, NT) + _dot(dCBT, Cg, NN))
            dx_ref[:, D + 2 * NS + g * NS:D + 2 * NS + (g + 1) * NS] = dCg + _dot(dCBT, Bg, TN)
            dHT_ref[:, gs] = (jnp.concatenate(svs, axis=1) * dHTg
                              + _dot(Cg.T, jnp.concatenate(dyes, axis=1), NN))
        dacum = c_dacum + r_dacum.T
        da = _cumsum_rows(dacum, not rev)
        mine = (lane >= 16 * d) & (lane < 16 * d + 16)
        ddt = jnp.where(mine, c_ddt + da * a32, 0.0)
        ddt_ref[...] = ddt * _sig(dtraw + dtb)
        st_ref[0:1, :] += _colsum(jnp.where(mine, da * dt, 0.0))
        if d == 0:
            st_ref[1:2, :] += dskacc

    def body(xf_ref, xb_ref, df_ref, db_ref, prm_ref, dsk_ref, dyf_ref, dyb_ref, hf_ref, hb_ref,
             dxf_ref, dxb_ref, ddf_ref, ddb_ref, st_ref, dHf, dHb):
        t = pl.program_id(0)

        @pl.when(t == 0)
        def _():
            dHf[...] = jnp.zeros_like(dHf)
            dHb[...] = jnp.zeros_like(dHb)
            st_ref[...] = jnp.zeros_like(st_ref)

        s = rs(t)
        one_dir(xf_ref, df_ref, prm_ref, dsk_ref, dyf_ref, cf(s) >= ncl, hf_ref, dHf,
                dxf_ref, ddf_ref, st_ref, 0)
        one_dir(xb_ref, db_ref, prm_ref, dsk_ref, dyb_ref, cb(s) >= ncl, hb_ref, dHb,
                dxb_ref, ddb_ref, st_ref, 1)

        @pl.when(t == nc - 1)
        def _():
            st_ref[0:1, :] = -jnp.exp(prm_ref[1:2, :]) * st_ref[0:1, :]

    def lat(c):
        return jnp.minimum(c, ncl - 1)

    xsh = jax.ShapeDtypeStruct((RT, 1536), F32)
    dsh = jax.ShapeDtypeStruct((RT, 128), F32)
    hspec = pl.BlockSpec((1, NS, NH * HP), lambda t: (rs(t), 0, 0))
    return pl.pallas_call(
        body, name="ssd_bwd", grid=(nc,),
        in_specs=[pl.BlockSpec((Q, 1536), lambda t: (cf(rs(t)), 0)),
                  pl.BlockSpec((Q, 1536), lambda t: (cb(rs(t)), 0)),
                  pl.BlockSpec((Q, 128), lambda t: (cf(rs(t)), ODT // 128)),
                  pl.BlockSpec((Q, 128), lambda t: (cb(rs(t)), ODT // 128)),
                  _cst((8, 128)), _cst((1, D)),
                  pl.BlockSpec((Q, D), lambda t: (lat(cf(rs(t))), 0)),
                  pl.BlockSpec((Q, D), lambda t: (lat(cb(rs(t))), 0)),
                  hspec, hspec],
        out_specs=[pl.BlockSpec((Q, 1536), lambda t: (cf(rs(t)), 0)),
                   pl.BlockSpec((Q, 1536), lambda t: (cb(rs(t)), 0)),
                   pl.BlockSpec((Q, 128), lambda t: (cf(rs(t)), 0)),
                   pl.BlockSpec((Q, 128), lambda t: (cb(rs(t)), 0)),
                   _cst((8, 128))],
        out_shape=[xsh, xsh, dsh, dsh, jax.ShapeDtypeStruct((8, 128), F32)],
        scratch_shapes=[pltpu.VMEM((NS, NH * HP), F32), pltpu.VMEM((NS, NH * HP), F32)],
        compiler_params=_params(("arbitrary",)),
    )(xbc, xbc, p, p, prm, dsk, dyd, dyd, hpf, hpb)


RB = 32


def _ssd3_bwd(xbc, p, prm, dsk, dyd, hpf, hpb):
    RT = xbc.shape[0]
    nc = RT // Q
    ncc = TL // Q
    ncl = nc - ncc
    cf, cb = _ssd_orders(ncl, ncc)
    npair = HPG // 2

    def rs(t):
        return nc - 1 - t

    def one_dir(x_ref, dt_ref, prm_ref, dsk_ref, dy_ref, is_ctx, hp_ref, dHT_ref,
                dx_ref, ddt_ref, st_ref, s_dwt, s_g, s_wt, s_xd, s_dye, s_dcbt, s_dcg, s_cd, s_cdt, d):
        rev = d == 1
        a32 = -jnp.exp(prm_ref[1:2, :])
        dtraw = dt_ref[...]
        dtb = prm_ref[0:1, :]
        dt, acum, acumT, _, _ = _ssd_common(dtraw, dtb, a32, rev)
        end = 0 if rev else Q - 1
        lane = lax.broadcasted_iota(jnp.int32, (RB, 128), 1)
        srow0 = lax.broadcasted_iota(jnp.int32, (RB, 128), 0)
        lo = lane < HP
        lo1 = lo[0:1, :]
        lane1 = lane[0:1, :]
        dyscale = jnp.where(is_ctx, 0.0, 1.0)
        aend_row = acum[end:end + 1, :]
        s_cd[...] = jnp.zeros_like(s_cd)
        s_cdt[...] = jnp.zeros_like(s_cdt)
        r_rows = jnp.zeros((Q, 128), F32)
        srowQ = lax.broadcasted_iota(jnp.int32, (Q, 128), 0)
        dskacc = jnp.zeros((1, 128), F32)
        for g in range(2):
            gs = slice(g * 512, (g + 1) * 512)
            Bg = x_ref[:, D + g * NS:D + (g + 1) * NS]
            Cg = x_ref[:, D + 2 * NS + g * NS:D + 2 * NS + (g + 1) * NS]
            CBT = _dot(Bg, Cg, NT)
            HTg = hp_ref[0, :, gs]
            dHTg = dHT_ref[:, gs]
            BdHg = _dot(Bg, dHTg, NN)
            hhs, yxs = [], []
            for q in range(npair):
                pi = g * npair + q
                ps = slice(pi * 128, (pi + 1) * 128)
                qs = slice(q * 128, (q + 1) * 128)
                Xp = x_ref[:, ps]
                dYp = dy_ref[:, ps] * dyscale
                s_dwt[q] = _dot(_halves(Xp, lo_full(), 0), dYp.T, NN)
                s_g[q] = _dot(_halves(dYp, lo_full(), 0), HTg[:, qs], NT)
                hhs.append(_colsum(dHTg[:, qs] * HTg[:, qs]))
                yxs.append(_colsum(dYp * Xp))
            rparts = [jnp.zeros((8, 128), F32) for _ in range(HPG)]
            ssum = [jnp.zeros((1, 1), F32) for _ in range(HPG)]
            for rb in range(Q // RB):
                r0 = rb * RB
                rows = slice(r0, r0 + RB)
                srow = srow0 + r0
                maskT = (lane <= srow) if rev else (lane >= srow)
                acum_rb = acum[rows, :]
                dt_rb = dt[rows, :]
                CBT_rb = CBT[rows, :]
                Cg_rb = Cg[rows, :]
                dcbt = jnp.zeros((RB, Q), F32)
                dcg = jnp.zeros((RB, NS), F32)
                cd = s_cd[rows, :]
                cdt = s_cdt[rows, :]
                for q in range(npair):
                    pi = g * npair + q
                    ps = slice(pi * 128, (pi + 1) * 128)
                    qs = slice(q * 128, (q + 1) * 128)
                    Xp = x_ref[rows, ps]
                    dYp = dy_ref[rows, ps] * dyscale
                    BdHp = BdHg[rows, qs]
                    XB = Xp * BdHp
                    dcs, ebs = [], []
                    for k in range(2):
                        hh = 2 * q + k
                        ln = 16 * d + g * HPG + hh
                        half = lo if k == 0 else jnp.logical_not(lo)
                        colB = _lane_bcast(acum_rb, ln)
                        dtcB = _lane_bcast(dt_rb, ln)
                        rowv = acumT[ln:ln + 1, :]
                        aend = _lane_bcast(aend_row, ln)
                        LmT = jnp.exp(jnp.where(maskT, rowv - colB, -1e30))
                        s_wt[q, rows, k * Q:(k + 1) * Q] = (CBT_rb * LmT * dtcB).astype(s_wt.dtype)
                        U = s_dwt[q, k * Q + r0:k * Q + r0 + RB, :] * LmT
                        MT = U * CBT_rb
                        rM = jnp.sum(MT, axis=1, keepdims=True)
                        TT = MT * dtcB
                        rparts[hh] = rparts[hh] + (TT[0:8] + TT[8:16] + TT[16:24] + TT[24:32])
                        dcbt = dcbt + U * dtcB
                        ecol = jnp.exp(aend - colB)
                        EB = jnp.exp(colB)
                        EG = EB * s_g[q, k * Q + r0:k * Q + r0 + RB, :]
                        dcg = dcg + EG
                        qcol = jnp.sum(EG * Cg_rb, axis=1, keepdims=True)
                        xb = jnp.sum(jnp.where(half, XB, 0.0), axis=1, keepdims=True)
                        e1 = ecol[:, 0:1]
                        dt1 = dtcB[:, 0:1]
                        scol = e1 * dt1 * xb
                        ssum[hh] = ssum[hh] + _colsum(scol)
                        cd = jnp.where(lane == ln, qcol - dt1 * rM - scol, cd)
                        cdt = jnp.where(lane == ln, rM + e1 * xb, cdt)
                        dcs.append(ecol * dtcB)
                        ebs.append(EB)
                    dcp = jnp.where(lo, dcs[0], dcs[1])
                    dxo = dcp * BdHp
                    if d == 0:
                        dxo = dxo + dYp * dsk_ref[:, ps]
                    dx_ref[rows, ps] = dxo
                    s_xd[rows, qs] = (Xp * dcp).astype(s_xd.dtype)
                    s_dye[rows, qs] = (dYp * jnp.where(lo, ebs[0], ebs[1])).astype(s_dye.dtype)
                s_dcbt[rows, :] = dcbt
                s_dcg[rows, :] = dcg
                s_cd[rows, :] = cd
                s_cdt[rows, :] = cdt
            erow = jnp.zeros((1, 128), F32)
            svs = []
            for q in range(npair):
                pi = g * npair + q
                ps = slice(pi * 128, (pi + 1) * 128)
                sv = []
                for k in range(2):
                    hh = 2 * q + k
                    h = g * HPG + hh
                    ln = 16 * d + h
                    half1 = lo1 if k == 0 else jnp.logical_not(lo1)
                    sA = jnp.exp(_lane_bcast(aend_row, ln))
                    hsum = jnp.sum(jnp.where(half1, hhs[q], 0.0), axis=1, keepdims=True)
                    erow = erow + jnp.where(lane1 == ln, sA[:, 0:1] * hsum + ssum[hh], 0.0)
                    rp = rparts[hh]
                    r_rows = jnp.where(srowQ == ln, _colsum(rp), r_rows)
                    if d == 0:
                        dskacc = dskacc + jnp.where(
                            lane1 == h, jnp.sum(jnp.where(half1, yxs[q], 0.0), axis=1, keepdims=True), 0.0)
                    sv.append(sA)
                svs.append(jnp.where(lo1, sv[0], sv[1]))
                dY2 = _halves(dy_ref[:, ps] * dyscale, lo_full(), 0)
                dx_ref[:, ps] += _dot(s_wt[q], dY2, NN)
            s_cd[end:end + 1, :] += erow
            dcbt_g = s_dcbt[...]
            dx_ref[:, D + g * NS:D + (g + 1) * NS] = _dot(s_xd[...], dHTg, NT) + _dot(dcbt_g, Cg, NN)
            dx_ref[:, D + 2 * NS + g * NS:D + 2 * NS + (g + 1) * NS] = s_dcg[...] + _dot(dcbt_g, Bg, TN)
            dHT_ref[:, gs] = jnp.concatenate(svs, axis=1) * dHTg + _dot(Cg.T, s_dye[...], NN)
        dacum = s_cd[...] + r_rows.T
        da = _cumsum_rows(dacum, not rev)
        laneQ = lax.broadcasted_iota(jnp.int32, (Q, 128), 1)
        mine = (laneQ >= 16 * d) & (laneQ < 16 * d + 16)
        ddt = jnp.where(mine, s_cdt[...] + da * a32, 0.0)
        ddt_ref[...] = ddt * _sig(dtraw + dtb)
        st_ref[0:1, :] += _colsum(jnp.where(mine, da * dt, 0.0))
        if d == 0:
            st_ref[1:2, :] += dskacc

    def lo_full():
        return lax.broadcasted_iota(jnp.int32, (Q, 128), 1) < HP

    def body(xf_ref, xb_ref, df_ref, db_ref, prm_ref, dsk_ref, dyf_ref, dyb_ref, hf_ref, hb_ref,
             dxf_ref, dxb_ref, ddf_ref, ddb_ref, st_ref, dHf, dHb, *scr):
        t = pl.program_id(0)

        @pl.when(t == 0)
        def _():
            dHf[...] = jnp.zeros_like(dHf)
            dHb[...] = jnp.zeros_like(dHb)
            st_ref[...] = jnp.zeros_like(st_ref)

        s = rs(t)
        one_dir(xf_ref, df_ref, prm_ref, dsk_ref, dyf_ref, cf(s) >= ncl, hf_ref, dHf,
                dxf_ref, ddf_ref, st_ref, *scr, 0)
        one_dir(xb_ref, db_ref, prm_ref, dsk_ref, dyb_ref, cb(s) >= ncl, hb_ref, dHb,
                dxb_ref, ddb_ref, st_ref, *scr, 1)

        @pl.when(t == nc - 1)
        def _():
            st_ref[0:1, :] = -jnp.exp(prm_ref[1:2, :]) * st_ref[0:1, :]

    def lat(c):
        return jnp.minimum(c, ncl - 1)

    xsh = jax.ShapeDtypeStruct((RT, 1536), F32)
    dsh = jax.ShapeDtypeStruct((RT, 128), F32)
    hspec = pl.BlockSpec((1, NS, NH * HP), lambda t: (rs(t), 0, 0))
    return pl.pallas_call(
        body, name="ssd_bwd", grid=(nc,),
        in_specs=[pl.BlockSpec((Q, 1536), lambda t: (cf(rs(t)), 0)),
                  pl.BlockSpec((Q, 1536), lambda t: (cb(rs(t)), 0)),
                  pl.BlockSpec((Q, 128), lambda t: (cf(rs(t)), ODT // 128)),
                  pl.BlockSpec((Q, 128), lambda t: (cb(rs(t)), ODT // 128)),
                  _cst((8, 128)), _cst((1, D)),
                  pl.BlockSpec((Q, D), lambda t: (lat(cf(rs(t))), 0)),
                  pl.BlockSpec((Q, D), lambda t: (lat(cb(rs(t))), 0)),
                  hspec, hspec],
        out_specs=[pl.BlockSpec((Q, 1536), lambda t: (cf(rs(t)), 0)),
                   pl.BlockSpec((Q, 1536), lambda t: (cb(rs(t)), 0)),
                   pl.BlockSpec((Q, 128), lambda t: (cf(rs(t)), 0)),
                   pl.BlockSpec((Q, 128), lambda t: (cb(rs(t)), 0)),
                   _cst((8, 128))],
        out_shape=[xsh, xsh, dsh, dsh, jax.ShapeDtypeStruct((8, 128), F32)],
        scratch_shapes=[pltpu.VMEM((NS, NH * HP), F32), pltpu.VMEM((NS, NH * HP), F32),
                        pltpu.VMEM((npair, 2 * Q, Q), F32), pltpu.VMEM((npair, 2 * Q, NS), F32),
                        pltpu.VMEM((npair, Q, 2 * Q), _MXU), pltpu.VMEM((Q, 512), _MXU),
                        pltpu.VMEM((Q, 512), _MXU), pltpu.VMEM((Q, Q), F32), pltpu.VMEM((Q, NS), F32),
                        pltpu.VMEM((Q, 128), F32), pltpu.VMEM((Q, 128), F32)],
        compiler_params=_params(("arbitrary",)),
    )(xbc, xbc, p, p, prm, dsk, dyd, dyd, hpf, hpb)


def _mix_fwd_vals(yf, yb, z, xs, u, v, dsk, sg, gg, gb):
    y = yf + yb + xs * dsk
    sz = _sig(z)
    hh = y * z * sz
    r = lax.rsqrt(jnp.mean(hh * hh, axis=-1, keepdims=True) + EPS)
    nh = hh * r
    ug, tu = _gelu(u)
    vg, tv = _gelu(v)
    vhat, vrstd = _ln(vg)
    vn = vhat * gg + gb
    return y, sz, r, nh, ug, tu, vg, tv, vhat, vrstd, vn


def _mix_fwd(yf, yb, p, xbc, dsk, sg, gg, gb, ws, bsT):
    L = yf.shape[0] - TL
    nt = L // TL

    def body(yf_ref, yb_ref, z_ref, xs_ref, u_ref, v_ref, dsk_ref, sg_ref, gg_ref, gb_ref,
             ws_ref, bs_ref, ys_ref, ym_ref):
        _, _, _, nh, ug, _, _, _, _, _, vn = _mix_fwd_vals(
            yf_ref[...], yb_ref[...], z_ref[...], xs_ref[...], u_ref[...], v_ref[...],
            dsk_ref[...], sg_ref[...], gg_ref[...], gb_ref[...])
        ys_ref[...] = (nh * sg_ref[...]).astype(ys_ref.dtype)
        for n in range(TL // Q):
            rs_ = slice(n * Q, (n + 1) * Q)
            for g in range(8):
                cs = slice(g * 128, (g + 1) * 128)
                mixed = _dot(ws_ref[g], vn[rs_, cs], NN) + bs_ref[:, g:g + 1]
                ym_ref[rs_, cs] = (ug[rs_, cs] * mixed).astype(ym_ref.dtype)

    return pl.pallas_call(
        body, name="mix_fwd", grid=(nt,),
        in_specs=[_rt(D), _rt(D), _rt(D, OZ // D), _rt(D, 0), _rt(D, OU // D), _rt(D, OV // D),
                  _cst((1, D)), _cst((1, D)), _cst((1, D)), _cst((1, D)),
                  _cst((8, 128, 128)), _cst((128, 128))],
        out_specs=[_rt(D), _rt(D)],
        out_shape=[jax.ShapeDtypeStruct((L, D), _MXU), jax.ShapeDtypeStruct((L, D), _MXU)],
        compiler_params=_params(("parallel",)),
    )(yf, yb, p, xbc, p, p, dsk, sg, gg, gb, ws, bsT)


def _mix_bwd(dys, dym, yf, yb, p, xbc, dp, dsk, sg, gg, gb, ws, bsT):
    L = dys.shape[0]
    nt = L // TL

    def body(dys_ref, dym_ref, yf_ref, yb_ref, z_ref, xs_ref, u_ref, v_ref, dsk_ref, sg_ref,
             gg_ref, gb_ref, ws_ref, bs_ref, dp_any, dzuv_ref, dy_ref, st_ref,
             dws_ref, dbs_ref, dvn_s):
        del dp_any
        dz_ref = dzuv_ref.at[:, OZ:OZ + D]
        du_ref = dzuv_ref.at[:, OU:OU + D]
        dv_ref = dzuv_ref.at[:, OV:OV + D]

        @pl.when(pl.program_id(0) == 0)
        def _():
            st_ref[...] = jnp.zeros_like(st_ref)
            dws_ref[...] = jnp.zeros_like(dws_ref)
            dbs_ref[...] = jnp.zeros_like(dbs_ref)

        z = z_ref[...]
        u = u_ref[...]
        v = v_ref[...]
        y, sz, r, nh, ug, tu, vg, tv, vhat, vrstd, vn = _mix_fwd_vals(
            yf_ref[...], yb_ref[...], z, xs_ref[...], u, v,
            dsk_ref[...], sg_ref[...], gg_ref[...], gb_ref[...])
        dys = dys_ref[...]
        st_ref[0:1, :] += _colsum(dys * nh)
        dn = dys * sg_ref[...]
        dhh = r * (dn - nh * jnp.mean(dn * nh, axis=-1, keepdims=True))
        dy_ref[...] = dhh * z * sz
        dz_ref[...] = (dhh * y * (sz * (1.0 + z * (1.0 - sz)))).astype(dz_ref.dtype)
        dym = dym_ref[...]
        lane = lax.broadcasted_iota(jnp.int32, (Q, 128), 1)
        dbs = jnp.zeros((Q, 128), F32)
        gu = _gelu_grad(u, tu)
        for n in range(TL // Q):
            rs_ = slice(n * Q, (n + 1) * Q)
            for g in range(8):
                cs = slice(g * 128, (g + 1) * 128)
                vb = vn[rs_, cs]
                mixed = _dot(ws_ref[g], vb, NN) + bs_ref[:, g:g + 1]
                dyb = dym[rs_, cs]
                dmx = dyb * ug[rs_, cs]
                du_ref[rs_, cs] = (dyb * mixed * gu[rs_, cs]).astype(du_ref.dtype)
                dvn_s[rs_, cs] = _dot(ws_ref[g], dmx, TN)
                dws_ref[g] += _dot(dmx, vb, NT)
                dbs = dbs + jnp.where(lane == g, jnp.sum(dmx, axis=1, keepdims=True), 0.0)
        dbs_ref[...] += dbs
        dvn = dvn_s[...]
        st_ref[1:2, :] += _colsum(dvn * vhat)
        st_ref[2:3, :] += _colsum(dvn)
        dvg = _ln_bwd(dvn * gg_ref[...], vhat, vrstd)
        dv_ref[...] = (dvg * _gelu_grad(v, tv)).astype(dv_ref.dtype)

    outs = pl.pallas_call(
        body, name="mix_bwd", grid=(nt,),
        in_specs=[_rt(D), _rt(D), _rt(D), _rt(D), _rt(D, OZ // D), _rt(D, 0), _rt(D, OU // D),
                  _rt(D, OV // D), _cst((1, D)), _cst((1, D)), _cst((1, D)), _cst((1, D)),
                  _cst((8, 128, 128)), _cst((128, 128)), pl.BlockSpec(memory_space=pl.ANY)],
        out_specs=[_rt(3 * D, 0), _rt(D), _cst((8, D)),
                   _cst((8, 128, 128)), _cst((128, 128))],
        out_shape=[jax.ShapeDtypeStruct(dp.shape, dp.dtype),
                   jax.ShapeDtypeStruct((L, D), F32), jax.ShapeDtypeStruct((8, D), F32),
                   jax.ShapeDtypeStruct((8, 128, 128), F32), jax.ShapeDtypeStruct((128, 128), F32)],
        scratch_shapes=[pltpu.VMEM((TL, D), F32)],
        input_output_aliases={14: 0},
        compiler_params=_params(("arbitrary",)),
    )(dys, dym, yf, yb, p, xbc, p, p, dsk, sg, gg, gb, ws, bsT, dp)
    return outs


def _gate_fwd(a1, a2, p, bg):
    L = a1.shape[0]

    def body(a1_ref, a2_ref, g_ref, bg_ref, m_ref):
        gt = _sig(g_ref[...] + bg_ref[...])
        m_ref[...] = (gt[:, :D] * a1_ref[...] + gt[:, D:] * a2_ref[...]).astype(m_ref.dtype)

    return pl.pallas_call(
        body, name="gate_fwd", grid=(L // TL,),
        in_specs=[_rt(D), _rt(D), _rt(2 * D, OG // (2 * D)), _cst((1, 2 * D))],
        out_specs=_rt(D), out_shape=jax.ShapeDtypeStruct((L, D), _MXU),
        compiler_params=_params(("parallel",)),
    )(a1, a2, p, bg)


def _gate_bwd(dmg, a1, a2, p, bg, dp):
    L = a1.shape[0]

    def body(dm_ref, a1_ref, a2_ref, g_ref, bg_ref, dp_any, dg_ref, da1_ref, da2_ref, st_ref):
        del dp_any

        @pl.when(pl.program_id(0) == 0)
        def _():
            st_ref[...] = jnp.zeros_like(st_ref)

        gt = _sig(g_ref[...] + bg_ref[...])
        g1 = gt[:, :D]
        g2 = gt[:, D:]
        dm = dm_ref[...]
        da1_ref[...] = (dm * g1).astype(da1_ref.dtype)
        da2_ref[...] = (dm * g2).astype(da2_ref.dtype)
        dg1 = dm * a1_ref[...] * g1 * (1.0 - g1)
        dg2 = dm * a2_ref[...] * g2 * (1.0 - g2)
        st_ref[0:1, 0:D] += _colsum(dg1)
        st_ref[0:1, D:2 * D] += _colsum(dg2)
        dg_ref[:, 0:D] = dg1.astype(dg_ref.dtype)
        dg_ref[:, D:2 * D] = dg2.astype(dg_ref.dtype)

    return pl.pallas_call(
        body, name="gate_bwd", grid=(L // TL,),
        in_specs=[_rt(D), _rt(D), _rt(D), _rt(2 * D, OG // (2 * D)), _cst((1, 2 * D)),
                  pl.BlockSpec(memory_space=pl.ANY)],
        out_specs=[_rt(2 * D, OG // (2 * D)), _rt(D), _rt(D), _cst((8, 2 * D))],
        out_shape=[jax.ShapeDtypeStruct(dp.shape, dp.dtype), jax.ShapeDtypeStruct((L, D), _MXU),
                   jax.ShapeDtypeStruct((L, D), _MXU), jax.ShapeDtypeStruct((8, 2 * D), F32)],
        input_output_aliases={5: 0},
        compiler_params=_params(("arbitrary",)),
    )(dmg, a1, a2, p, bg, dp)


def _merge_fwd(yssd, ygm, p, bg, ws, wg, wo, xn, modx, g1, b1):
    L = yssd.shape[0]
    tm = TL

    def body(ys_ref, yg_ref, g_ref, bg_ref, ws_ref, wg_ref, wo_ref, xn_ref, mx_ref, g1_ref, b1_ref,
             a1_ref, a2_ref, m_ref, o_ref, r1_ref, h2_ref):
        a1 = _dot(ys_ref[...], ws_ref[...], NN)
        a2 = _dot(yg_ref[...], wg_ref[...], NN)
        gt = _sig(g_ref[...] + bg_ref[...])
        mg = gt[:, :D] * a1 + gt[:, D:] * a2
        a1_ref[...] = a1
        a2_ref[...] = a2
        m_ref[...] = mg.astype(m_ref.dtype)
        out = _dot(mg, wo_ref[...], NN)
        o_ref[...] = out
        r1 = ALPHA * xn_ref[...] + mx_ref[2:3, :] * out
        xhat, _ = _ln(r1)
        x1 = xhat * g1_ref[...] + b1_ref[...]
        r1_ref[...] = r1
        h2_ref[...] = (x1 * (1.0 + mx_ref[4:5, :]) + mx_ref[3:4, :]).astype(h2_ref.dtype)

    rows = pl.BlockSpec((tm, D), lambda i: (i, 0))
    f32s = jax.ShapeDtypeStruct((L, D), F32)
    mxus = jax.ShapeDtypeStruct((L, D), _MXU)
    return pl.pallas_call(
        body, name="merge_fwd", grid=(L // tm,),
        in_specs=[rows, rows, pl.BlockSpec((tm, 2 * D), lambda i: (i, OG // (2 * D))), _cst((1, 2 * D)),
                  _cst((D, D)), _cst((D, D)), _cst((D, D)), rows, _cst((8, D)), _cst((1, D)), _cst((1, D))],
        out_specs=[rows] * 6,
        out_shape=[f32s, f32s, mxus, f32s, f32s, mxus],
        compiler_params=_params(("parallel",)),
    )(yssd, ygm, p, bg, ws, wg, wo, xn, modx, g1, b1)


def _mm_o2_res2(ff, w2, r1, tgt, modx, g1, b1, g2, b2):
    L, K = ff.shape
    tm = 2 * TL
    tk = K // 2
    nk = K // tk

    def body(a_ref, b_ref, r1_ref, t_ref, mx_ref, g1_ref, b1_ref, g2_ref, b2_ref,
             dr2_ref, do2_ref, st_ref, loss_ref, acc_ref):
        i, k = pl.program_id(0), pl.program_id(1)

        @pl.when((i == 0) & (k == 0))
        def _():
            st_ref[...] = jnp.zeros_like(st_ref)
            loss_ref[...] = jnp.zeros_like(loss_ref)

        prod = _dot(a_ref[...], b_ref[...], NN)

        @pl.when(k == 0)
        def _():
            acc_ref[...] = prod

        @pl.when(k == nk - 1)
        def _():
            o2 = acc_ref[...] + prod
            xh1, _ = _ln(r1_ref[...])
            x1 = xh1 * g1_ref[...] + b1_ref[...]
            g2x = mx_ref[5:6, :]
            xh2, rstd2 = _ln(ALPHA * x1 + g2x * o2)
            err = xh2 * g2_ref[...] + b2_ref[...] - t_ref[...]
            per_tok = jnp.mean(err * err, axis=-1, keepdims=True)
            loss_ref[...] += 0.5 * jnp.sum(per_tok, axis=0, keepdims=True)
            dy = err * (1.0 / D)
            st_ref[0:1, :] += _colsum(dy * xh2)
            st_ref[1:2, :] += _colsum(dy)
            dr2 = _ln_bwd(dy * g2_ref[...], xh2, rstd2)
            st_ref[2:3, :] += _colsum(dr2 * o2)
            dr2_ref[...] = dr2
            do2_ref[...] = (g2x * dr2).astype(do2_ref.dtype)

    assert nk == 2
    rows = pl.BlockSpec((tm, D), lambda i, k: (i, 0))
    vec = pl.BlockSpec((1, D), lambda i, k: (0, 0))
    return pl.pallas_call(
        body, name="mm_o2_res2", grid=(L // tm, nk),
        in_specs=[pl.BlockSpec((tm, tk), lambda i, k: (i, k)), pl.BlockSpec((tk, D), lambda i, k: (k, 0)),
                  rows, rows, pl.BlockSpec((8, D), lambda i, k: (0, 0)), vec, vec, vec, vec],
        out_specs=[rows, rows, pl.BlockSpec((8, D), lambda i, k: (0, 0)),
                   pl.BlockSpec((8, 128), lambda i, k: (0, 0))],
        out_shape=[jax.ShapeDtypeStruct((L, D), F32), jax.ShapeDtypeStruct((L, D), _MXU),
                   jax.ShapeDtypeStruct((8, D), F32), jax.ShapeDtypeStruct((8, 128), F32)],
        scratch_shapes=[pltpu.VMEM((tm, D), F32)],
        compiler_params=_params(("arbitrary", "arbitrary")),
    )(ff, w2, r1, tgt, modx, g1, b1, g2, b2)


def _mm_dh2_res1bwd(df13, w13i, dr2, r1, out, modx, g1, b1):
    L, K = df13.shape
    tm = 2 * TL
    tk = HFF
    nk = K // tk

    def body(a_ref, b_ref, dr2_ref, r1_ref, o_ref, mx_ref, g_ref, bb_ref, dr1_ref, do_ref, st_ref, acc_ref):
        i, k = pl.program_id(0), pl.program_id(1)

        @pl.when((i == 0) & (k == 0))
        def _():
            st_ref[...] = jnp.zeros_like(st_ref)

        prod = _dot(a_ref[...], b_ref[...], NN)

        @pl.when(k == 0)
        def _():
            acc_ref[...] = prod

        @pl.when((k > 0) & (k < nk - 1))
        def _():
            acc_ref[...] += prod

        @pl.when(k == nk - 1)
        def _():
            dh2 = acc_ref[...] + prod
            xh1, rstd1 = _ln(r1_ref[...])
            x1 = xh1 * g_ref[...] + bb_ref[...]
            dx1 = ALPHA * dr2_ref[...] + dh2 * (1.0 + mx_ref[4:5, :])
            st_ref[0:1, :] += _colsum(dh2 * x1)
            st_ref[1:2, :] += _colsum(dh2)
            st_ref[2:3, :] += _colsum(dx1 * xh1)
            st_ref[3:4, :] += _colsum(dx1)
            dr1 = _ln_bwd(dx1 * g_ref[...], xh1, rstd1)
            st_ref[4:5, :] += _colsum(dr1 * o_ref[...])
            dr1_ref[...] = dr1
            do_ref[...] = (mx_ref[2:3, :] * dr1).astype(do_ref.dtype)

    assert nk >= 2
    rows = pl.BlockSpec((tm, D), lambda i, k: (i, 0))
    vec = pl.BlockSpec((1, D), lambda i, k: (0, 0))
    return pl.pallas_call(
        body, name="mm_dh2_res1bwd", grid=(L // tm, nk),
        in_specs=[pl.BlockSpec((tm, tk), lambda i, k: (i, k)), pl.BlockSpec((tk, D), lambda i, k: (k, 0)),
                  rows, rows, rows, pl.BlockSpec((8, D), lambda i, k: (0, 0)), vec, vec],
        out_specs=[rows, rows, pl.BlockSpec((8, D), lambda i, k: (0, 0))],
        out_shape=[jax.ShapeDtypeStruct((L, D), F32), jax.ShapeDtypeStruct((L, D), _MXU),
                   jax.ShapeDtypeStruct((8, D), F32)],
        scratch_shapes=[pltpu.VMEM((tm, D), F32)],
        compiler_params=_params(("arbitrary", "arbitrary")),
    )(df13, w13i, dr2, r1, out, modx, g1, b1)


def _merge_bwd(dout, a1, a2, p, bg, wo, ws, wg, dp):
    L = a1.shape[0]
    tm = TL

    def body(do_ref, a1_ref, a2_ref, g_ref, bg_ref, wo_ref, ws_ref, wg_ref, dp_any,
             dg_ref, da1_ref, da2_ref, st_ref, dys_ref, dym_ref):
        del dp_any

        @pl.when(pl.program_id(0) == 0)
        def _():
            st_ref[...] = jnp.zeros_like(st_ref)

        dm = _dot(do_ref[...], wo_ref[...], NT)
        gt = _sig(g_ref[...] + bg_ref[...])
        g1 = gt[:, :D]
        g2 = gt[:, D:]
        da1 = (dm * g1).astype(da1_ref.dtype)
        da2 = (dm * g2).astype(da2_ref.dtype)
        da1_ref[...] = da1
        da2_ref[...] = da2
        dg1 = dm * a1_ref[...] * g1 * (1.0 - g1)
        dg2 = dm * a2_ref[...] * g2 * (1.0 - g2)
        st_ref[0:1, 0:D] += _colsum(dg1)
        st_ref[0:1, D:2 * D] += _colsum(dg2)
        dg_ref[:, 0:D] = dg1.astype(dg_ref.dtype)
        dg_ref[:, D:2 * D] = dg2.astype(dg_ref.dtype)
        dys_ref[...] = _dot(da1, ws_ref[...], NT)
        dym_ref[...] = _dot(da2, wg_ref[...], NT)

    rows = pl.BlockSpec((tm, D), lambda i: (i, 0))
    gates = pl.BlockSpec((tm, 2 * D), lambda i: (i, OG // (2 * D)))
    f32s = jax.ShapeDtypeStruct((L, D), F32)
    mxus = jax.ShapeDtypeStruct((L, D), _MXU)
    return pl.pallas_call(
        body, name="merge_bwd", grid=(L // tm,),
        in_specs=[rows, rows, rows, gates, _cst((1, 2 * D)), _cst((D, D)), _cst((D, D)), _cst((D, D)),
                  pl.BlockSpec(memory_space=pl.ANY)],
        out_specs=[gates, rows, rows, _cst((8, 2 * D)), rows, rows],
        out_shape=[jax.ShapeDtypeStruct(dp.shape, dp.dtype), mxus, mxus,
                   jax.ShapeDtypeStruct((8, 2 * D), F32), f32s, f32s],
        input_output_aliases={8: 0},
        compiler_params=_params(("arbitrary",)),
    )(dout, a1, a2, p, bg, wo, ws, wg, dp)


def _res1_fwd(xn, out, modx, g, b):
    L = out.shape[0]

    def body(xn_ref, o_ref, mx_ref, g_ref, b_ref, r1_ref, h2_ref):
        r1 = ALPHA * xn_ref[...] + mx_ref[2:3, :] * o_ref[...]
        xhat, _ = _ln(r1)
        x1 = xhat * g_ref[...] + b_ref[...]
        r1_ref[...] = r1
        h2_ref[...] = (x1 * (1.0 + mx_ref[4:5, :]) + mx_ref[3:4, :]).astype(h2_ref.dtype)

    return pl.pallas_call(
        body, name="res1_fwd", grid=(L // TL,),
        in_specs=[_rt(D), _rt(D), _cst((8, D)), _cst((1, D)), _cst((1, D))],
        out_specs=[_rt(D), _rt(D)],
        out_shape=[jax.ShapeDtypeStruct((L, D), F32), jax.ShapeDtypeStruct((L, D), _MXU)],
        compiler_params=_params(("parallel",)),
    )(xn, out, modx, g, b)


HFF = DFF // 2


def _mm_f13_glu(h2, w13i):
    L = h2.shape[0]
    tm = 512

    def body(a_ref, b_ref, f_ref, ff_ref):
        f = _dot(a_ref[...], b_ref[...], NT)
        f_ref[...] = f
        f1 = f[:, :HFF]
        ff_ref[...] = (f1 * _sig(f1) * f[:, HFF:]).astype(ff_ref.dtype)

    return pl.pallas_call(
        body, name="mm_f13_glu", grid=(DFF // HFF, L // tm),
        in_specs=[pl.BlockSpec((tm, D), lambda j, i: (i, 0)), pl.BlockSpec((2 * HFF, D), lambda j, i: (j, 0))],
        out_specs=[pl.BlockSpec((tm, 2 * HFF), lambda j, i: (i, j)), pl.BlockSpec((tm, HFF), lambda j, i: (i, j))],
        out_shape=[jax.ShapeDtypeStruct((L, 2 * DFF), F32), jax.ShapeDtypeStruct((L, DFF), _MXU)],
        compiler_params=_params(("parallel", "parallel")),
    )(h2, w13i)


def _mm_dff_glu(do2, w_ff2_f, f13i):
    L = do2.shape[0]
    tm = 512

    def body(a_ref, b_ref, f_ref, o_ref):
        d = _dot(a_ref[...], b_ref[...], NT)
        f1 = f_ref[:, :HFF]
        s = _sig(f1)
        o_ref[:, :HFF] = (d * f_ref[:, HFF:] * (s * (1.0 + f1 * (1.0 - s)))).astype(o_ref.dtype)
        o_ref[:, HFF:] = (d * f1 * s).astype(o_ref.dtype)

    return pl.pallas_call(
        body, name="mm_dff_glu", grid=(DFF // HFF, L // tm),
        in_specs=[pl.BlockSpec((tm, D), lambda j, i: (i, 0)), pl.BlockSpec((HFF, D), lambda j, i: (j, 0)),
                  pl.BlockSpec((tm, 2 * HFF), lambda j, i: (i, j))],
        out_specs=pl.BlockSpec((tm, 2 * HFF), lambda j, i: (i, j)),
        out_shape=jax.ShapeDtypeStruct((L, 2 * DFF), _MXU),
        compiler_params=_params(("parallel", "parallel")),
    )(do2, w_ff2_f, f13i)


def _glu_fwd(f13):
    L = f13.shape[0]

    def body(f1_ref, f3_ref, o_ref):
        f1 = f1_ref[...]
        o_ref[...] = (f1 * _sig(f1) * f3_ref[...]).astype(o_ref.dtype)

    return pl.pallas_call(
        body, name="glu_fwd", grid=(L // TL,),
        in_specs=[_rt(DFF, 0), _rt(DFF, 1)], out_specs=_rt(DFF),
        out_shape=jax.ShapeDtypeStruct((L, DFF), _MXU),
        compiler_params=_params(("parallel",)),
    )(f13, f13)


def _glu_bwd(dff, f13):
    L = f13.shape[0]

    def body(d_ref, f1_ref, f3_ref, o_ref):
        f1 = f1_ref[...]
        s = _sig(f1)
        d = d_ref[...]
        o_ref[:, 0:DFF] = (d * f3_ref[...] * (s * (1.0 + f1 * (1.0 - s)))).astype(o_ref.dtype)
        o_ref[:, DFF:2 * DFF] = (d * f1 * s).astype(o_ref.dtype)

    return pl.pallas_call(
        body, name="glu_bwd", grid=(L // TL,),
        in_specs=[_rt(DFF), _rt(DFF, 0), _rt(DFF, 1)], out_specs=_rt(2 * DFF),
        out_shape=jax.ShapeDtypeStruct((L, 2 * DFF), _MXU),
        compiler_params=_params(("parallel",)),
    )(dff, f13, f13)


def _res2(r1, o2, tgt, modx, g1, b1, g2, b2):
    L = r1.shape[0]

    def body(r1_ref, o2_ref, t_ref, mx_ref, g1_ref, b1_ref, g2_ref, b2_ref,
             dr2_ref, do2_ref, st_ref, loss_ref):
        @pl.when(pl.program_id(0) == 0)
        def _():
            st_ref[...] = jnp.zeros_like(st_ref)
            loss_ref[...] = jnp.zeros_like(loss_ref)

        xh1, _ = _ln(r1_ref[...])
        x1 = xh1 * g1_ref[...] + b1_ref[...]
        o2 = o2_ref[...]
        g2x = mx_ref[5:6, :]
        xh2, rstd2 = _ln(ALPHA * x1 + g2x * o2)
        err = xh2 * g2_ref[...] + b2_ref[...] - t_ref[...]
        per_tok = jnp.mean(err * err, axis=-1, keepdims=True)
        loss_ref[...] += 0.5 * jnp.sum(per_tok, axis=0, keepdims=True)
        dy = err * (1.0 / D)
        st_ref[0:1, :] += _colsum(dy * xh2)
        st_ref[1:2, :] += _colsum(dy)
        dr2 = _ln_bwd(dy * g2_ref[...], xh2, rstd2)
        st_ref[2:3, :] += _colsum(dr2 * o2)
        dr2_ref[...] = dr2
        do2_ref[...] = (g2x * dr2).astype(do2_ref.dtype)

    return pl.pallas_call(
        body, name="res2", grid=(L // TL,),
        in_specs=[_rt(D), _rt(D), _rt(D), _cst((8, D))] + [_cst((1, D))] * 4,
        out_specs=[_rt(D), _rt(D), _cst((8, D)), _cst((8, 128))],
        out_shape=[jax.ShapeDtypeStruct((L, D), F32), jax.ShapeDtypeStruct((L, D), _MXU),
                   jax.ShapeDtypeStruct((8, D), F32), jax.ShapeDtypeStruct((8, 128), F32)],
        compiler_params=_params(("arbitrary",)),
    )(r1, o2, tgt, modx, g1, b1, g2, b2)


def _res1_bwd(dr2, dh2, r1, out, modx, g1, b1):
    L = r1.shape[0]

    def body(dr2_ref, dh2_ref, r1_ref, o_ref, mx_ref, g_ref, b_ref, dr1_ref, do_ref, st_ref):
        @pl.when(pl.program_id(0) == 0)
        def _():
            st_ref[...] = jnp.zeros_like(st_ref)

        xh1, rstd1 = _ln(r1_ref[...])
        x1 = xh1 * g_ref[...] + b_ref[...]
        dh2 = dh2_ref[...]
        dx1 = ALPHA * dr2_ref[...] + dh2 * (1.0 + mx_ref[4:5, :])
        st_ref[0:1, :] += _colsum(dh2 * x1)
        st_ref[1:2, :] += _colsum(dh2)
        st_ref[2:3, :] += _colsum(dx1 * xh1)
        st_ref[3:4, :] += _colsum(dx1)
        dr1 = _ln_bwd(dx1 * g_ref[...], xh1, rstd1)
        st_ref[4:5, :] += _colsum(dr1 * o_ref[...])
        dr1_ref[...] = dr1
        do_ref[...] = (mx_ref[2:3, :] * dr1).astype(do_ref.dtype)

    return pl.pallas_call(
        body, name="res1_bwd", grid=(L // TL,),
        in_specs=[_rt(D), _rt(D), _rt(D), _rt(D), _cst((8, D)), _cst((1, D)), _cst((1, D))],
        out_specs=[_rt(D), _rt(D), _cst((8, D))],
        out_shape=[jax.ShapeDtypeStruct((L, D), F32), jax.ShapeDtypeStruct((L, D), _MXU),
                   jax.ShapeDtypeStruct((8, D), F32)],
        compiler_params=_params(("arbitrary",)),
    )(dr2, dh2, r1, out, modx, g1, b1)


def _conv_bwd(dxf, dxb, p, conv_w8, conv_b, dp):
    RT = p.shape[0]
    chunks = _seq_chunks(RT - TL)

    def body(df_ref, db_ref, p_ref, w_ref, b_ref, dp_any, o_ref, dw_ref, dbias_ref, dpre_s):
        del dp_any
        w = w_ref[...]
        bias = b_ref[...]
        srow = lax.broadcasted_iota(jnp.int32, (8, 128), 0)
        dwacc = jnp.zeros((8, 128), F32)
        dbacc = jnp.zeros((1, 128), F32)
        for r0, first, last in chunks:
            taps = _conv_taps(p_ref, r0, first, last)
            pre = bias + sum(w[k:k + 1, :] * taps[k] for k in range(5))
            s = _sig(pre)
            dpre = (df_ref[pl.ds(r0, TL), :] + db_ref[pl.ds(r0, TL), :]) * (s * (1.0 + pre * (1.0 - s)))
            dpre_s[pl.ds(r0, TL), :] = dpre
            dbacc = dbacc + _colsum(dpre)
            for k in range(5):
                dwacc = dwacc + jnp.where(srow == k, _colsum(dpre * taps[k]), 0.0)
        for r0, first, last in chunks:
            taps = _conv_taps(dpre_s, r0, first, last)
            dx = sum(w[k:k + 1, :] * taps[4 - k] for k in range(5))
            o_ref[pl.ds(r0, TL), :] = dx.astype(o_ref.dtype)
        dw_ref[...] = dwacc
        dbias_ref[...] = jnp.broadcast_to(dbacc, (8, 128))

    cspec = pl.BlockSpec((RT, 128), lambda j: (0, j))
    wspec = pl.BlockSpec((8, 128), lambda j: (0, j))
    return pl.pallas_call(
        body, name="conv_bwd", grid=(12,),
        in_specs=[cspec, cspec, pl.BlockSpec((RT, 128), lambda j: (0, _xbc_colblk(j))),
                  wspec, pl.BlockSpec((1, 128), lambda j: (0, j)), pl.BlockSpec(memory_space=pl.ANY)],
        out_specs=[pl.BlockSpec((RT, 128), lambda j: (0, _xbc_colblk(j))), wspec, wspec],
        out_shape=[jax.ShapeDtypeStruct(dp.shape, dp.dtype), jax.ShapeDtypeStruct((8, 1536), F32),
                   jax.ShapeDtypeStruct((8, 1536), F32)],
        scratch_shapes=[pltpu.VMEM((RT, 128), F32)],
        input_output_aliases={5: 0},
        compiler_params=_params(("parallel",)),
    )(dxf, dxb, p, conv_w8, conv_b, dp)


def _dp_fill(dp):
    nrow = dp.shape[0] // TL

    def body(dp_any, o_ref):
        del dp_any
        o_ref[...] = jnp.zeros_like(o_ref)

    return pl.pallas_call(
        body, name="dp_fill", grid=(1,),
        in_specs=[pl.BlockSpec(memory_space=pl.ANY)],
        out_specs=pl.BlockSpec((TL, OB), lambda i: (nrow - 1, 0)),
        out_shape=jax.ShapeDtypeStruct(dp.shape, dp.dtype),
        input_output_aliases={0: 0},
        compiler_params=_params(("arbitrary",)),
    )(dp)


def _dt_bwd(ddf, ddb, dp):
    RT = ddf.shape[0]
    assert NPJ - ODT == 256

    def body(f_ref, b_ref, dp_any, o_ref, st_ref):
        del dp_any

        @pl.when(pl.program_id(0) == 0)
        def _():
            st_ref[...] = jnp.zeros_like(st_ref)

        s = f_ref[...] + b_ref[...]
        o_ref[:, 0:128] = s.astype(o_ref.dtype)
        o_ref[:, 128:256] = jnp.zeros((TL, 128), o_ref.dtype)
        st_ref[0:1, :] += _colsum(s)

    return pl.pallas_call(
        body, name="dt_bwd", grid=(RT // TL,),
        in_specs=[_rt(128), _rt(128), pl.BlockSpec(memory_space=pl.ANY)],
        out_specs=[_rt(256, ODT // 256), _cst((8, 128))],
        out_shape=[jax.ShapeDtypeStruct(dp.shape, dp.dtype), jax.ShapeDtypeStruct((8, 128), F32)],
        input_output_aliases={2: 0},
        compiler_params=_params(("arbitrary",)),
    )(ddf, ddb, dp)


def _ln0_bwd(dh1, dr1, x, ctx, g, b, modx, modc):
    L = x.shape[0]
    nt = L // TL

    def body(dh_ref, dr1_ref, x_ref, c_ref, g_ref, b_ref, mx_ref, mc_ref, gx_ref, st_ref):
        i = pl.program_id(0)
        isc = i == nt

        @pl.when(i == 0)
        def _():
            st_ref[...] = jnp.zeros_like(st_ref)

        xin = jnp.where(isc, c_ref[...], x_ref[...])
        xhat, rstd = _ln(xin)
        xn = xhat * g_ref[...] + b_ref[...]
        sc = jnp.where(isc, mc_ref[1:2, :], mx_ref[1:2, :])
        dh = dh_ref[...]
        lat = jnp.where(isc, 0.0, 1.0)
        dxn = dh * (1.0 + sc) + (lat * ALPHA) * dr1_ref[...]
        tsh = _colsum(dh)
        tsc = _colsum(dh * xn)
        st_ref[0:1, :] += lat * tsh
        st_ref[1:2, :] += lat * tsc
        st_ref[2:3, :] += (1.0 - lat) * tsh
        st_ref[3:4, :] += (1.0 - lat) * tsc
        st_ref[4:5, :] += _colsum(dxn * xhat)
        st_ref[5:6, :] += _colsum(dxn)

        @pl.when(i < nt)
        def _():
            gx_ref[...] = _ln_bwd(dxn * g_ref[...], xhat, rstd)

    return pl.pallas_call(
        body, name="ln0_bwd", grid=(nt + 1,),
        in_specs=[_rt(D), _rtc(D, nt), _rtc(D, nt), _cst((TL, D)), _cst((1, D)), _cst((1, D)),
                  _cst((8, D)), _cst((8, D))],
        out_specs=[_rtc(D, nt), _cst((8, D))],
        out_shape=[jax.ShapeDtypeStruct((L, D), F32), jax.ShapeDtypeStruct((8, D), F32)],
        compiler_params=_params(("arbitrary",)),
    )(dh1, dr1, x, ctx, g, b, modx, modc)


def _perm_cols(w):
    pad = jnp.zeros((w.shape[0], NPJ - NNAT), w.dtype)
    return jnp.concatenate([w[:, 0:1024], w[:, 2592:3616], w[:, 3616:4640], w[:, 1024:2048],
                            w[:, 4640:6688], w[:, 2048:2304], w[:, 2304:2560], w[:, 2560:2592], pad],
                           axis=1)


SECTIONS = ((0, 1024, OZ), (1024, 2048, OXS), (2048, 2304, OB), (2304, 2560, OC), (2560, 2592, ODT),
            (2592, 3616, OU), (3616, 4640, OV), (4640, 6688, OG))


def _perm_from_blocks(ga):
    n = ga.shape[2]
    pieces = []
    for na, nb, _ in sorted(SECTIONS, key=lambda sec: sec[2]):
        for k in range(NDEV):
            lo, hi = max(na, k * n), min(nb, (k + 1) * n)
            if lo < hi:
                pieces.append(ga[k][:, lo - k * n:hi - k * n])
    pieces.append(jnp.zeros((ga.shape[1], NPJ - NNAT), ga.dtype))
    return jnp.concatenate(pieces, axis=1)


def _blocks_from_perm(gp, n):
    blocks = []
    for k in range(NDEV):
        pieces = []
        for na, nb, po in SECTIONS:
            lo, hi = max(na, k * n), min(nb, (k + 1) * n)
            if lo < hi:
                pieces.append(gp[:, po + lo - na:po + hi - na])
        blocks.append(jnp.concatenate(pieces, axis=1))
    return jnp.stack(blocks)


def _padded(n, row_align):
    unit = row_align * D
    return -(-n // unit) * unit if row_align else n


def _slab(arrs, rows, row_align=0):
    parts = []
    for a in arrs:
        f = a.reshape(-1)
        parts.append(jnp.pad(f, (0, _padded(f.shape[0], row_align) - f.shape[0])))
    flat = jnp.concatenate(parts)
    flat = jnp.pad(flat, (0, rows * D - flat.shape[0]))
    return flat.reshape(rows, D)


def _unslab(slab, shapes, row_align=0):
    out, off = [], 0
    for shp in shapes:
        n = 1
        for s in shp:
            n *= s
        r0, r1 = off // D, -(-(off + n) // D)
        out.append(slab[r0:r1].reshape(-1)[off - r0 * D:off - r0 * D + n].reshape(shp))
        off += _padded(n, row_align)
    return out


def _row(v):
    return v.reshape(1, -1)


def _t(a):
    return jnp.swapaxes(a, 0, 1)


def _pad_rows(a, rows):
    return jnp.pad(a, ((0, rows - a.shape[0]), (0, 0)))


BIG = ["w_in", "w_ssd_proj", "w_gm_proj", "w_out", "w_ff1", "w_ff3", "w_ff2"]
BIG_ROWS = 2304
BIG_ALIGN = 16
REPL = ["c_ctx", "ln0_g", "ln0_b", "b_ada", "conv_b", "dt_bias", "a_log", "d_skip", "ssd_norm_g",
        "gm_norm_g", "gm_norm_b", "w_spatial", "b_spatial", "b_gate", "ln1_g", "ln1_b", "ln2_g", "ln2_b"]
SMALL_ROWS = 160
WEIGHTS = ["c_ctx", "ln0_g", "ln0_b", "w_ada", "b_ada", "w_in", "conv_w", "conv_b", "dt_bias", "a_log",
           "d_skip", "ssd_norm_g", "gm_norm_g", "gm_norm_b", "w_spatial", "b_spatial", "b_gate",
           "w_ssd_proj", "w_gm_proj", "w_out", "ln1_g", "ln1_b", "w_ff1", "w_ff3", "w_ff2", "ln2_g", "ln2_b"]


def kernel(x, c, ctx, c_ctx, ln0_g, ln0_b, w_ada, b_ada, w_in, conv_w, conv_b, dt_bias, a_log, d_skip, ssd_norm_g, gm_norm_g, gm_norm_b, w_spatial, b_spatial, b_gate, w_ssd_proj, w_gm_proj, w_out, ln1_g, ln1_b, w_ff1, w_ff3, w_ff2, ln2_g, ln2_b, loss_target, m_c_ctx, m_ln0_g, m_ln0_b, m_w_ada, m_b_ada, m_w_in, m_conv_w, m_conv_b, m_dt_bias, m_a_log, m_d_skip, m_ssd_norm_g, m_gm_norm_g, m_gm_norm_b, m_w_spatial, m_b_spatial, m_b_gate, m_w_ssd_proj, m_w_gm_proj, m_w_out, m_ln1_g, m_ln1_b, m_w_ff1, m_w_ff3, m_w_ff2, m_ln2_g, m_ln2_b, v_c_ctx, v_ln0_g, v_ln0_b, v_w_ada, v_b_ada, v_w_in, v_conv_w, v_conv_b, v_dt_bias, v_a_log, v_d_skip, v_ssd_norm_g, v_gm_norm_g, v_gm_norm_b, v_w_spatial, v_b_spatial, v_b_gate, v_w_ssd_proj, v_w_gm_proj, v_w_out, v_ln1_g, v_ln1_b, v_w_ff1, v_w_ff3, v_w_ff2, v_ln2_g, v_ln2_b):
    W = dict(c_ctx=c_ctx, ln0_g=ln0_g, ln0_b=ln0_b, w_ada=w_ada, b_ada=b_ada, w_in=w_in, conv_w=conv_w,
             conv_b=conv_b, dt_bias=dt_bias, a_log=a_log, d_skip=d_skip, ssd_norm_g=ssd_norm_g,
             gm_norm_g=gm_norm_g, gm_norm_b=gm_norm_b, w_spatial=w_spatial, b_spatial=b_spatial,
             b_gate=b_gate, w_ssd_proj=w_ssd_proj, w_gm_proj=w_gm_proj, w_out=w_out, ln1_g=ln1_g,
             ln1_b=ln1_b, w_ff1=w_ff1, w_ff3=w_ff3, w_ff2=w_ff2, ln2_g=ln2_g, ln2_b=ln2_b)
    M = dict(c_ctx=m_c_ctx, ln0_g=m_ln0_g, ln0_b=m_ln0_b, w_ada=m_w_ada, b_ada=m_b_ada, w_in=m_w_in,
             conv_w=m_conv_w, conv_b=m_conv_b, dt_bias=m_dt_bias, a_log=m_a_log, d_skip=m_d_skip,
             ssd_norm_g=m_ssd_norm_g, gm_norm_g=m_gm_norm_g, gm_norm_b=m_gm_norm_b,
             w_spatial=m_w_spatial, b_spatial=m_b_spatial, b_gate=m_b_gate, w_ssd_proj=m_w_ssd_proj,
             w_gm_proj=m_w_gm_proj, w_out=m_w_out, ln1_g=m_ln1_g, ln1_b=m_ln1_b, w_ff1=m_w_ff1,
             w_ff3=m_w_ff3, w_ff2=m_w_ff2, ln2_g=m_ln2_g, ln2_b=m_ln2_b)
    V = dict(c_ctx=v_c_ctx, ln0_g=v_ln0_g, ln0_b=v_ln0_b, w_ada=v_w_ada, b_ada=v_b_ada, w_in=v_w_in,
             conv_w=v_conv_w, conv_b=v_conv_b, dt_bias=v_dt_bias, a_log=v_a_log, d_skip=v_d_skip,
             ssd_norm_g=v_ssd_norm_g, gm_norm_g=v_gm_norm_g, gm_norm_b=v_gm_norm_b,
             w_spatial=v_w_spatial, b_spatial=v_b_spatial, b_gate=v_b_gate, w_ssd_proj=v_w_ssd_proj,
             w_gm_proj=v_w_gm_proj, w_out=v_w_out, ln1_g=v_ln1_g, ln1_b=v_ln1_b, w_ff1=v_w_ff1,
             w_ff3=v_w_ff3, w_ff2=v_w_ff2, ln2_g=v_ln2_g, ln2_b=v_ln2_b)

    me = 4 * lax.axis_index("x") + 2 * lax.axis_index("y") + lax.axis_index("c")
    xl, cx, tgt = x[0], ctx[0], loss_target[0]
    L = xl.shape[0]
    assert cx.shape[0] == TL and L % TL == 0
    ada_n = w_ada.shape[2]
    cw_n = conv_w.shape[2]

    small1 = _pad_rows(jnp.concatenate([c, _slab([conv_w[0]], 1)], axis=0), 8)
    g1 = _all_gather(small1, "ag_small")
    c_all = g1[:, 0, :]
    conv_w_full = g1[:, 1, :5 * cw_n].reshape(NDEV, 5, cw_n).transpose(1, 0, 2).reshape(5, NDEV * cw_n)
    sq = w_ssd_proj.shape[1]
    ffr = w_ff2.shape[1]
    ffc = w_ff1.shape[2]
    late = [jnp.concatenate([w_ssd_proj[0], w_gm_proj[0], w_out[0], w_ff2[0]], axis=0).astype(_MXU),
            _t(w_ff1[0]).astype(_MXU), _t(w_ff3[0]).astype(_MXU)]

    c16 = _pad_rows(jnp.concatenate([c_all, _row(c_ctx)], axis=0), 16)
    b_ada_sh = lax.dynamic_slice(b_ada, (0, ada_n * me), (1, ada_n))
    modp = _ada_fwd(c16, w_ada[0], b_ada_sh)
    mod16 = _all_gather(modp, "ag_mod").transpose(1, 0, 2).reshape(16, NDEV * ada_n)

    ga, = _all_gather_multi([w_in[0].astype(_MXU)], "ag_w_in")
    ga, late, mod16 = lax.optimization_barrier((ga, late, mod16))
    lw_send, lw_recv, lw_src, lw_land, lw_token = _exchange_start(late, "ag_late_start", gather=True)
    w_in_p = _perm_from_blocks(ga)
    modx = _pad_rows(lax.dynamic_slice(mod16, (me, 0), (1, 6 * D)).reshape(6, D), 8) + lw_token[0, 0]
    modc = _pad_rows(mod16[8].reshape(6, D), 8)

    g0, b0 = _row(ln0_g), _row(ln0_b)
    xn, h1 = _ln0_fwd(xl, cx, g0, b0, modx, modc)
    p = _mm(h1, w_in_p, "nn", F32, "mm_p")
    conv_w8 = _pad_rows(conv_w_full, 8)
    xbc = _conv_fwd(p, conv_w8, conv_b)
    prm = _pad_rows(jnp.pad(jnp.stack([dt_bias.reshape(32), a_log.reshape(32)]), ((0, 0), (0, 96))), 8)
    yf, yb, hpf, hpb = _ssd2_fwd(xbc, p, prm)
    lw_land = _exchange_wait(lw_send, lw_recv, lw_src, lw_land, yf, "ag_late_wait", gather=True)
    fw_send, fw_recv, lw_land, fw_token = _forward_start(lw_land, "ag_fwd_start")
    dsk = _row(jnp.repeat(d_skip[0, 0] + d_skip[0, 1], HP)) + fw_token[0:1, 0:1]
    ws_m = w_spatial[0].astype(_MXU)
    bsT = jnp.pad(b_spatial[0].T, ((0, 0), (0, 120)))
    mixp = (dsk, ssd_norm_g, gm_norm_g, gm_norm_b, ws_m, bsT)
    yssd, ygm = _mix_fwd(yf, yb, p, xbc, *mixp)
    gb, gc1, gc2 = _forward_wait(fw_send, fw_recv, lw_land, yssd, "ag_fwd_wait")

    gb, gc1, gc2 = lax.optimization_barrier(
        [lax.dynamic_update_index_in_dim(g, mine, me, 0) for g, mine in zip((gb, gc1, gc2), late)])
    w_ssd_f = gb[:, 0:sq].reshape(NDEV * sq, D)
    w_gm_f = gb[:, sq:2 * sq].reshape(NDEV * sq, D)
    w_out_f = gb[:, 2 * sq:3 * sq].reshape(NDEV * sq, D)
    w_ff2_f = gb[:, 3 * sq:3 * sq + ffr].reshape(NDEV * ffr, D)
    assert HFF == (NDEV // 2) * ffc
    hd = NDEV // 2
    w13i = jnp.concatenate([g[k] for t in range(2) for g in (gc1, gc2) for k in range(t * hd, (t + 1) * hd)],
                           axis=0)
    a1, a2, merged, out, r1, h2 = _merge_fwd(yssd, ygm, p, b_gate, w_ssd_f, w_gm_f, w_out_f,
                                             xn, modx, ln1_g, ln1_b)
    f13, ff = _mm_f13_glu(h2, w13i)

    dr2, do2, st2, loss_slab = _mm_o2_res2(ff, w_ff2_f, r1, tgt, modx, ln1_g, ln1_b, ln2_g, ln2_b)
    loss = lax.psum(loss_slab[0, 0], ("x", "y", "c"))
    df13 = _mm_dff_glu(do2, w_ff2_f, f13)
    dw_ff2 = _mm(ff, do2, "tn", _MXU, "mm_dw_ff2")
    dw13i = _mm(df13, h2, "tn", _MXU, "mm_dw13")

    def owner_blocks(first):
        return jnp.concatenate([dw13i[t * 2 * HFF + first:t * 2 * HFF + first + HFF].reshape(NDEV // 2, ffc, D)
                                for t in range(2)], axis=0)

    xff = [dw_ff2.reshape(NDEV, ffr, D), owner_blocks(0), owner_blocks(HFF)]
    ff_send, ff_recv, ff_src, ff_land, ff_token = _exchange_start(xff, "xchg_ff_start")
    modx = modx + ff_token[0, 0]
    dr1, dout, st1 = _mm_dh2_res1bwd(df13, w13i, dr2, r1, out, modx, ln1_g, ln1_b)
    dw_out = _mm(merged, dout, "tn", _MXU, "mm_dw_out")
    dp = _dp_fill(lax.empty((L + TL, NPJ), _MXU))
    dp, da1, da2, stg, dys, dym = _merge_bwd(dout, a1, a2, p, b_gate, w_out_f, w_ssd_f, w_gm_f, dp)
    dw_ssd = _mm(yssd, da1, "tn", _MXU, "mm_dw_ssd")
    dw_gm = _mm(ygm, da2, "tn", _MXU, "mm_dw_gm")
    xsq = [jnp.concatenate([dw_ssd.reshape(NDEV, sq, D), dw_gm.reshape(NDEV, sq, D),
                            dw_out.reshape(NDEV, sq, D)], axis=1)]
    sq_send, sq_recv, sq_src, sq_land, sq_token = _exchange_start(xsq, "xchg_sq_start")
    mixp = (dsk + sq_token[0:1, 0:1],) + mixp[1:]
    dp, dyd, stm, dws, dbsT = _mix_bwd(dys, dym, yf, yb, p, xbc, dp, *mixp)
    dxf, dxb, ddf, ddb, sts = _ssd2_bwd(xbc, p, prm, dsk, dyd, hpf, hpb)
    dp, dcw, dcb = _conv_bwd(dxf, dxb, p, conv_w8, conv_b, dp)
    dp, std = _dt_bwd(ddf, ddb, dp)
    hw = D // 2
    xin_a = [_blocks_from_perm(_mm(h1[:, :hw], dp, "tn", _MXU, "mm_dw_in_a"), w_in.shape[2])]
    ina_send, ina_recv, ina_src, ina_land, ina_token = _exchange_start(xin_a, "xchg_in_a_start")
    h1b, ina_token = lax.optimization_barrier((h1[:, hw:], ina_token))
    xin_b = [_blocks_from_perm(_mm(h1b, dp, "tn", _MXU, "mm_dw_in_b"), w_in.shape[2])]
    inb_send, inb_recv, inb_src, inb_land, inb_token = _exchange_start(xin_b, "xchg_in_b_start")
    dp, inb_token = lax.optimization_barrier((dp, inb_token))
    dh1 = _mm(dp, w_in_p, "nt", F32, "mm_dh1")
    modx = modx + (ina_token[0, 0] + inb_token[0, 0])
    grad_x, st0 = _ln0_bwd(dh1, dr1, xl, cx, g0, b0, modx, modc)

    zero = jnp.zeros((D,), F32)
    dmod = jnp.stack([jnp.concatenate([st0[0], st0[1], st1[4], st1[1], st1[0], st2[2]]),
                      jnp.concatenate([st0[2], st0[3], zero, zero, zero, zero])])
    g16 = _all_gather(_pad_rows(dmod, 8), "ag_dmod")[:, 0:2, :].reshape(16, 6 * D)
    g16_sh = lax.dynamic_slice(g16, (0, ada_n * me), (16, ada_n))
    c16b = jnp.stack([c_all, jnp.broadcast_to(_row(c_ctx), (NDEV, D))], axis=1).reshape(16, D)
    dw_ada, db_ada8, dcc8 = _ada_bwd(c16b, g16, g16_sh, w_ada[0])

    part = dict(
        c_ctx=dcc8[0], ln0_g=st0[4], ln0_b=st0[5], conv_w=dcw[0:5], conv_b=dcb[0],
        dt_bias=std[0, 0:32], a_log=sts[0, 0:32], d_skip=jnp.tile(sts[1, 0:16], 2),
        ssd_norm_g=stm[0], gm_norm_g=stm[1], gm_norm_b=stm[2], w_spatial=dws,
        b_spatial=dbsT[:, 0:8].T, b_gate=stg[0], ln1_g=st1[2], ln1_b=st1[3], ln2_g=st2[0], ln2_b=st2[1])
    pnames = list(part)
    psum8 = _sum8(_all_gather(_slab([part[n] for n in pnames], SMALL_ROWS), "ag_smallgrads"), "sum_smallgrads")
    small = dict(zip(pnames, _unslab(psum8, [part[n].shape for n in pnames])))
    grads = {n: small[n].reshape(W[n].shape) for n in pnames if n != "conv_w"}
    grads["conv_w"] = lax.dynamic_slice(small["conv_w"], (0, cw_n * me), (5, cw_n)).reshape(conv_w.shape)
    grads["b_ada"] = db_ada8[0:1]
    grads["w_ada"] = dw_ada.reshape(w_ada.shape)

    delta, new_m, new_v = {}, {}, {}

    def adam_group(names, rows, tag, align=0):
        shapes = [W[n].shape for n in names]
        outs = _adamw(*[_slab([src[n] for n in names], rows, align) for src in (grads, W, M, V)], tag)
        for res, slab in zip((delta, new_m, new_v), outs):
            for n, a in zip(names, _unslab(slab, shapes, align)):
                res[n] = a

    adam_group(REPL + ["conv_w"], SMALL_ROWS, "adamw_small")
    res = _adamw(grads["w_ada"][0], w_ada[0], m_w_ada[0], v_w_ada[0], "adamw_w_ada")
    delta["w_ada"], new_m["w_ada"], new_v["w_ada"] = [a[None] for a in res]

    rff = _exchange_wait(ff_send, ff_recv, ff_src, ff_land, st0, "xchg_ff_wait")
    rsq = _exchange_wait(sq_send, sq_recv, sq_src, sq_land, rff[0], "xchg_sq_wait")
    rin_a = _exchange_wait(ina_send, ina_recv, ina_src, ina_land, delta["ln2_b"], "xchg_in_a_wait")
    rin_b = _exchange_wait(inb_send, inb_recv, inb_src, inb_land, rin_a[0], "xchg_in_b_wait")
    rin = jnp.concatenate([rin_a[0], rin_b[0]], axis=1)

    def own(blocks):
        return lax.dynamic_index_in_dim(blocks, me, 0, keepdims=False)

    own_in = jnp.concatenate([own(xin_a[0]), own(xin_b[0])], axis=0)
    for n, r8, mine, row0, tr in (
            ("w_ff2", rff[0], own(xff[0]), 0, ffr // 2), ("w_ssd_proj", rsq[0], own(xsq[0]), 0, sq),
            ("w_gm_proj", rsq[0], own(xsq[0]), sq, sq), ("w_out", rsq[0], own(xsq[0]), 2 * sq, sq),
            ("w_in", rin, own_in, 0, 256)):
        res = _adamw_sum(r8, mine, W[n][0], M[n][0], V[n][0], row0, tr, "adamw_" + n)
        grads[n], delta[n], new_m[n], new_v[n] = [a[None] for a in res]
    for n, r8, mine in (("w_ff1", rff[1], own(xff[1])), ("w_ff3", rff[2], own(xff[2]))):
        res = _adamw_sum(r8, mine, _t(W[n][0]), _t(M[n][0]), _t(V[n][0]), 0, ffc // 2, "adamw_" + n)
        grads[n], delta[n], new_m[n], new_v[n] = [_t(a)[None] for a in res]

    return (loss, grad_x[None], *[grads[n] for n in WEIGHTS], *[delta[n] for n in WEIGHTS],
            *[new_m[n] for n in WEIGHTS], *[new_v[n] for n in WEIGHTS])
```

```python
import jax
import jax.numpy as jnp
from jax import lax
from jax.experimental import pallas as pl
from jax.experimental.pallas import tpu as pltpu

_MXU = jnp.bfloat16
F32 = jnp.float32
D = 1024
TL = 256
Q = 128
NH, HP, NS, HPG = 16, 64, 128, 8
DFF = 2816
ALPHA = 2.0 ** 0.25
EPS = 1e-5
OZ, OU, OV, OXS, OG, OB, OC, ODT, NPJ = 0, 1024, 2048, 3072, 4096, 6144, 6400, 6656, 6912
NNAT = 6688
NDEV = 8
ADAM_LR, ADAM_B1, ADAM_B2, ADAM_EPS, ADAM_WD, ADAM_STEP = 1e-3, 0.9, 0.999, 1e-8, 0.01, 10
VMEM_LIMIT = 48 * 1024 * 1024

NN = ((1,), (0,))
NT = ((1,), (1,))
TN = ((0,), (0,))
MESH = pl.DeviceIdType.MESH


def _dot(a, b, dims):
    return lax.dot_general(a.astype(_MXU), b.astype(_MXU), (dims, ((), ())),
                           preferred_element_type=F32)


def _tile(n, cands):
    for c in cands:
        if n % c == 0:
            return c
    return n


def _divisor_tile(n, cap, mult):
    best = n
    for t in range(mult, min(n, cap) + 1, mult):
        if n % t == 0:
            best = t
    return best


def _params(sem):
    return pltpu.CompilerParams(dimension_semantics=sem, vmem_limit_bytes=VMEM_LIMIT)


def _cst(shape):
    nd = len(shape)
    return pl.BlockSpec(shape, lambda *_: (0,) * nd)


def _rt(w, cb=0, rows=TL):
    return pl.BlockSpec((rows, w), lambda i: (i, cb))


def _rtc(w, nt, cb=0):
    return pl.BlockSpec((TL, w), lambda i: (jnp.minimum(i, nt - 1), cb))


def _sig(x):
    return jax.nn.sigmoid(x)


def _softplus(x):
    return jnp.maximum(x, 0.0) + jnp.log1p(jnp.exp(-jnp.abs(x)))


_G0, _G1 = 0.7978845608028654, 0.044715


def _gelu(x):
    t = jnp.tanh(_G0 * (x + _G1 * x * x * x))
    return 0.5 * x * (1.0 + t), t


def _gelu_grad(x, t):
    return 0.5 * (1.0 + t) + 0.5 * x * (1.0 - t * t) * _G0 * (1.0 + 3.0 * _G1 * x * x)


def _ln(r):
    mu = jnp.mean(r, axis=-1, keepdims=True)
    xc = r - mu
    var = jnp.mean(xc * xc, axis=-1, keepdims=True)
    rstd = lax.rsqrt(var + EPS)
    return xc * rstd, rstd


def _ln_bwd(dyh, xhat, rstd):
    return rstd * (dyh - jnp.mean(dyh, axis=-1, keepdims=True)
                   - xhat * jnp.mean(dyh * xhat, axis=-1, keepdims=True))


def _colsum(v):
    return jnp.sum(v, axis=0, keepdims=True)


def _cumsum_rows(a, rev):
    n = a.shape[0]
    row = lax.broadcasted_iota(jnp.int32, a.shape, 0)
    s = 1
    while s < n:
        if rev:
            a = a + jnp.where(row < n - s, pltpu.roll(a, n - s, 0), 0.0)
        else:
            a = a + jnp.where(row >= s, pltpu.roll(a, s, 0), 0.0)
        s *= 2
    return a


def _mm(a, b, mode, out_dtype, name):
    if mode == "tn":
        K, M = a.shape
    else:
        M, K = a.shape
    N = b.shape[0] if mode == "nt" else b.shape[1]
    tm = _divisor_tile(M, 1408, 128) if mode == "tn" else _divisor_tile(M, 1088, 16)
    tn = _divisor_tile(N, 1408, 128)
    tk = _divisor_tile(K, 2304, 128)
    nk = K // tk
    dims = {"nn": NN, "nt": NT, "tn": TN}[mode]
    use_acc = nk > 1 and out_dtype != F32

    def body(a_ref, b_ref, o_ref, *acc):
        prod = _dot(a_ref[...], b_ref[...], dims)
        if nk == 1:
            o_ref[...] = prod.astype(o_ref.dtype)
            return
        acc_ref = acc[0] if use_acc else o_ref
        k = pl.program_id(2)

        @pl.when(k == 0)
        def _():
            acc_ref[...] = prod

        if use_acc:
            @pl.when((k > 0) & (k < nk - 1))
            def _():
                acc_ref[...] += prod

            @pl.when(k == nk - 1)
            def _():
                o_ref[...] = (acc_ref[...] + prod).astype(o_ref.dtype)
        else:
            @pl.when(k > 0)
            def _():
                o_ref[...] += prod

    if mode == "tn":
        a_spec = pl.BlockSpec((tk, tm), lambda i, j, k: (k, i))
    else:
        a_spec = pl.BlockSpec((tm, tk), lambda i, j, k: (i, k))
    if mode == "nt":
        b_spec = pl.BlockSpec((tn, tk), lambda i, j, k: (j, k))
    else:
        b_spec = pl.BlockSpec((tk, tn), lambda i, j, k: (k, j))
    return pl.pallas_call(
        body, name=name, grid=(M // tm, N // tn, nk),
        in_specs=[a_spec, b_spec],
        out_specs=pl.BlockSpec((tm, tn), lambda i, j, k: (i, j)),
        out_shape=jax.ShapeDtypeStruct((M, N), out_dtype),
        scratch_shapes=[pltpu.VMEM((tm, tn), F32)] if use_acc else [],
        compiler_params=_params(("parallel", "parallel", "arbitrary")),
    )(a, b)


def _all_gather(x, name):
    def body(x_ref, out_ref, send_sems, recv_sems, local_sem):
        mx, my, mc = lax.axis_index("x"), lax.axis_index("y"), lax.axis_index("c")
        me, sibling = (mx, my, mc), (mx, my, 1 - mc)
        chips = [(1 - mx, my), (mx, 1 - my), (1 - mx, 1 - my)]

        def slot(px, py, pc):
            return out_ref.at[4 * px + 2 * py + pc]

        def copy(k, block, to, src=None):
            return pltpu.make_async_remote_copy(
                src_ref=slot(*block) if src is None else src, dst_ref=slot(*block),
                send_sem=send_sems.at[k], recv_sem=recv_sems.at[k],
                device_id=to, device_id_type=MESH)

        mine = pltpu.make_async_copy(x_ref, slot(*me), local_sem)
        mine.start()
        first = [copy(0, me, sibling, src=x_ref)]
        first += [copy(1 + j, me, (*chip, mc), src=x_ref) for j, chip in enumerate(chips)]
        for cp in first:
            cp.start()
        passed = [copy(4 + j, (*chip, mc), sibling) for j, chip in enumerate(chips)]
        for j, chip in enumerate(chips):
            copy(1 + j, (*chip, mc), me).wait_recv()
            passed[j].start()
        copy(0, sibling, me).wait_recv()
        for j, chip in enumerate(chips):
            copy(4 + j, (*chip, 1 - mc), me).wait_recv()
        for cp in first + passed:
            cp.wait_send()
        mine.wait()

    return pl.pallas_call(
        body, name=name,
        out_shape=jax.ShapeDtypeStruct((NDEV,) + x.shape, x.dtype),
        in_specs=[pl.BlockSpec(memory_space=pl.ANY)],
        out_specs=pl.BlockSpec(memory_space=pl.ANY),
        scratch_shapes=[pltpu.SemaphoreType.DMA((7,)), pltpu.SemaphoreType.DMA((7,)),
                        pltpu.SemaphoreType.DMA],
    )(x)


def _any_specs(n):
    return [pl.BlockSpec(memory_space=pl.ANY)] * n


def _all_gather_multi(xs, name):
    na = len(xs)

    def body(*refs):
        x_refs, out_refs = refs[:na], refs[na:2 * na]
        send_sems, recv_sems, local_sems = refs[2 * na:]
        mx, my, mc = lax.axis_index("x"), lax.axis_index("y"), lax.axis_index("c")
        me, sibling = (mx, my, mc), (mx, my, 1 - mc)
        chips = [(1 - mx, my), (mx, 1 - my), (1 - mx, 1 - my)]

        def copy(a, k, block, to, src=None):
            slot = out_refs[a].at[4 * block[0] + 2 * block[1] + block[2]]
            return pltpu.make_async_remote_copy(
                src_ref=slot if src is None else src, dst_ref=slot,
                send_sem=send_sems.at[7 * a + k], recv_sem=recv_sems.at[7 * a + k],
                device_id=to, device_id_type=MESH)

        mine = [pltpu.make_async_copy(x_refs[a], out_refs[a].at[4 * mx + 2 * my + mc], local_sems.at[a])
                for a in range(na)]
        for cp in mine:
            cp.start()
        first = []
        for a in range(na):
            first.append(copy(a, 0, me, sibling, src=x_refs[a]))
            first += [copy(a, 1 + j, me, (*chip, mc), src=x_refs[a]) for j, chip in enumerate(chips)]
        for cp in first:
            cp.start()
        passed = []
        for a in range(na):
            for j, chip in enumerate(chips):
                copy(a, 1 + j, (*chip, mc), me).wait_recv()
                fwd = copy(a, 4 + j, (*chip, mc), sibling)
                fwd.start()
                passed.append(fwd)
        for a in range(na):
            copy(a, 0, sibling, me).wait_recv()
            for j, chip in enumerate(chips):
                copy(a, 4 + j, (*chip, 1 - mc), me).wait_recv()
        for cp in first + passed:
            cp.wait_send()
        for cp in mine:
            cp.wait()

    return pl.pallas_call(
        body, name=name,
        out_shape=[jax.ShapeDtypeStruct((NDEV,) + x.shape, x.dtype) for x in xs],
        in_specs=_any_specs(na), out_specs=_any_specs(na),
        scratch_shapes=[pltpu.SemaphoreType.DMA((7 * na,)), pltpu.SemaphoreType.DMA((7 * na,)),
                        pltpu.SemaphoreType.DMA((na,))],
    )(*xs)


def _adamw_sum(r8, own, w, m, v, row0, tr, name):
    R, C = w.shape
    assert row0 % tr == 0
    blk0 = row0 // tr
    bc1 = 1.0 - ADAM_B1 ** ADAM_STEP
    bc2 = 1.0 - ADAM_B2 ** ADAM_STEP

    def body(r_ref, *refs):
        if own is None:
            gg = r_ref[0].astype(F32)
        else:
            gg = refs[0][...].astype(F32)
            refs = refs[1:]
        w_ref, m_ref, v_ref, g_ref, d_ref, mo_ref, vo_ref = refs
        for k in range(1, NDEV):
            gg = gg + r_ref[k].astype(F32)
        mn = ADAM_B1 * m_ref[...] + (1.0 - ADAM_B1) * gg
        vn = ADAM_B2 * v_ref[...] + (1.0 - ADAM_B2) * (gg * gg)
        mh = mn / bc1
        vh = vn / bc2
        g_ref[...] = gg
        d_ref[...] = -ADAM_LR * (mh / (jnp.sqrt(vh) + ADAM_EPS) + ADAM_WD * w_ref[...])
        mo_ref[...] = mn
        vo_ref[...] = vn

    spec = pl.BlockSpec((tr, C), lambda i: (i, 0))
    sh = jax.ShapeDtypeStruct((R, C), F32)
    own_ops = [] if own is None else [own]
    own_specs = [] if own is None else [pl.BlockSpec((tr, C), lambda i: (i + blk0, 0))]
    return pl.pallas_call(
        body, name=name, grid=(R // tr,),
        in_specs=[pl.BlockSpec((NDEV, tr, C), lambda i: (0, i + blk0, 0))] + own_specs + [spec, spec, spec],
        out_specs=[spec] * 4, out_shape=[sh] * 4, compiler_params=_params(("parallel",)),
    )(r8, *own_ops, w, m, v)


_HBM = pl.BlockSpec(memory_space=pltpu.HBM)
_SEM = pl.BlockSpec(memory_space=pltpu.SEMAPHORE)
_EFFECT = pltpu.SideEffectType.DATAFLOW_SIDE_EFFECTING


def _exchange_copies(g_refs, land_refs, send_sems, recv_sems, gather):
    mx, my, mc = lax.axis_index("x"), lax.axis_index("y"), lax.axis_index("c")
    copies = []
    for a in range(len(g_refs)):
        for f in ((1, 2, 4, 6) if gather else range(1, NDEV)):
            px = 1 - mx if (f >> 2) & 1 else mx
            py = 1 - my if (f >> 1) & 1 else my
            pc = 1 - mc if f & 1 else mc
            src = g_refs[a] if gather else g_refs[a].at[4 * px + 2 * py + pc]
            dst = land_refs[a].at[4 * mx + 2 * my + mc] if gather else land_refs[a].at[f]
            copies.append(pltpu.make_async_remote_copy(
                src_ref=src, dst_ref=dst,
                send_sem=send_sems.at[7 * a + f - 1], recv_sem=recv_sems.at[7 * a + f - 1],
                device_id=(px, py, pc), device_id_type=MESH))
    return copies


def _exchange_start(gs, name, gather=False):
    na = len(gs)

    def body(*refs):
        for cp in _exchange_copies(refs[:na], refs[na:2 * na], refs[2 * na], refs[2 * na + 1], gather):
            cp.start()
        refs[-1][...] = jnp.zeros_like(refs[-1])

    hbm = [pltpu.HBM(g.shape, g.dtype) for g in gs]
    land_shapes = [((NDEV,) + g.shape) if gather else g.shape for g in gs]
    lands = [pltpu.with_memory_space_constraint(lax.empty(shp, g.dtype), pltpu.HBM)
             for shp, g in zip(land_shapes, gs)]
    hbm_land = [pltpu.HBM(shp, g.dtype) for shp, g in zip(land_shapes, gs)]
    outs = pl.pallas_call(
        body, name=name,
        out_shape=(pltpu.SemaphoreType.DMA((7 * na,)), pltpu.SemaphoreType.DMA((7 * na,)), *hbm, *hbm_land,
                   jax.ShapeDtypeStruct((8, 128), F32)),
        in_specs=[_HBM] * (2 * na),
        out_specs=(_SEM, _SEM, *([_HBM] * (2 * na)), pl.BlockSpec(memory_space=pltpu.VMEM)),
        input_output_aliases={i: 2 + i for i in range(2 * na)},
        compiler_params=pltpu.CompilerParams(has_side_effects=_EFFECT),
    )(*[pltpu.with_memory_space_constraint(g, pltpu.HBM) for g in gs], *lands)
    return outs[0], outs[1], outs[2:2 + na], outs[2 + na:2 + 2 * na], outs[-1]


def _forward_copies(land_refs, send_sems, recv_sems):
    mx, my, mc = lax.axis_index("x"), lax.axis_index("y"), lax.axis_index("c")
    copies = []
    for a in range(len(land_refs)):
        for j, (fx, fy) in enumerate(((0, 1), (1, 0), (1, 1))):
            px = 1 - mx if fx else mx
            py = 1 - my if fy else my
            blk = land_refs[a].at[4 * px + 2 * py + mc]
            copies.append(pltpu.make_async_remote_copy(
                src_ref=blk, dst_ref=blk, send_sem=send_sems.at[3 * a + j], recv_sem=recv_sems.at[3 * a + j],
                device_id=(mx, my, 1 - mc), device_id_type=MESH))
    return copies


def _forward_start(lands, name):
    na = len(lands)

    def body(*refs):
        for cp in _forward_copies(refs[:na], refs[na], refs[na + 1]):
            cp.start()
        refs[-1][...] = jnp.zeros_like(refs[-1])

    outs = pl.pallas_call(
        body, name=name,
        out_shape=(pltpu.SemaphoreType.DMA((3 * na,)), pltpu.SemaphoreType.DMA((3 * na,)),
                   *[pltpu.HBM(g.shape, g.dtype) for g in lands], jax.ShapeDtypeStruct((8, 128), F32)),
        in_specs=[_HBM] * na,
        out_specs=(_SEM, _SEM, *([_HBM] * na), pl.BlockSpec(memory_space=pltpu.VMEM)),
        input_output_aliases={i: 2 + i for i in range(na)},
        compiler_params=pltpu.CompilerParams(has_side_effects=_EFFECT),
    )(*lands)
    return outs[0], outs[1], outs[2:2 + na], outs[-1]


def _forward_wait(send_sems, recv_sems, lands, after, name):
    na = len(lands)

    def body(*refs):
        for cp in _forward_copies(refs[:na], refs[na], refs[na + 1]):
            cp.wait_send()
            cp.wait_recv()

    return pl.pallas_call(
        body, name=name,
        out_shape=tuple(pltpu.HBM(g.shape, g.dtype) for g in lands),
        in_specs=[_HBM] * na + [_SEM, _SEM, pl.BlockSpec(memory_space=pl.ANY)],
        out_specs=tuple([_HBM] * na),
        input_output_aliases={i: i for i in range(na)},
        compiler_params=pltpu.CompilerParams(has_side_effects=_EFFECT),
    )(*lands, send_sems, recv_sems, after)


def _exchange_wait(send_sems, recv_sems, g_thru, land_thru, after, name, gather=False):
    na = len(g_thru)

    def body(*refs):
        for cp in _exchange_copies(refs[:na], refs[na:2 * na], refs[2 * na], refs[2 * na + 1], gather):
            cp.wait_send()
            cp.wait_recv()

    outs = pl.pallas_call(
        body, name=name,
        out_shape=tuple(pltpu.HBM(g.shape, g.dtype) for g in list(g_thru) + list(land_thru)),
        in_specs=[_HBM] * (2 * na) + [_SEM, _SEM, pl.BlockSpec(memory_space=pl.ANY)],
        out_specs=tuple([_HBM] * (2 * na)),
        input_output_aliases={i: i for i in range(2 * na)},
        compiler_params=pltpu.CompilerParams(has_side_effects=_EFFECT),
    )(*g_thru, *land_thru, send_sems, recv_sems, after)
    return outs[na:]


def _sum8(r, name):
    _, R, C = r.shape
    tr = _tile(R, (256, 160, 128, 64, 32, 16, 8))

    def body(r_ref, o_ref):
        acc = r_ref[0].astype(F32)
        for k in range(1, NDEV):
            acc = acc + r_ref[k].astype(F32)
        o_ref[...] = acc

    return pl.pallas_call(
        body, name=name, grid=(R // tr,),
        in_specs=[pl.BlockSpec((NDEV, tr, C), lambda i: (0, i, 0))],
        out_specs=pl.BlockSpec((tr, C), lambda i: (i, 0)),
        out_shape=jax.ShapeDtypeStruct((R, C), F32),
        compiler_params=_params(("parallel",)),
    )(r)


def _adamw(g, w, m, v, name):
    R, C = g.shape
    tr = _tile(R, (256, 160, 128, 64, 32, 16, 8))
    bc1 = 1.0 - ADAM_B1 ** ADAM_STEP
    bc2 = 1.0 - ADAM_B2 ** ADAM_STEP

    def body(g_ref, w_ref, m_ref, v_ref, d_ref, mo_ref, vo_ref):
        gg = g_ref[...]
        mn = ADAM_B1 * m_ref[...] + (1.0 - ADAM_B1) * gg
        vn = ADAM_B2 * v_ref[...] + (1.0 - ADAM_B2) * (gg * gg)
        mh = mn / bc1
        vh = vn / bc2
        d_ref[...] = -ADAM_LR * (mh / (jnp.sqrt(vh) + ADAM_EPS) + ADAM_WD * w_ref[...])
        mo_ref[...] = mn
        vo_ref[...] = vn

    spec = pl.BlockSpec((tr, C), lambda i: (i, 0))
    sh = jax.ShapeDtypeStruct((R, C), F32)
    return pl.pallas_call(
        body, name=name, grid=(R // tr,), in_specs=[spec] * 4, out_specs=[spec] * 3,
        out_shape=[sh] * 3, compiler_params=_params(("parallel",)),
    )(g, w, m, v)


def _ada_fwd(c16, w_sh, b_sh):
    def body(c_ref, w_ref, b_ref, o_ref):
        c = c_ref[...]
        o_ref[...] = _dot(c * _sig(c), w_ref[...], NN) + b_ref[...]

    return pl.pallas_call(
        body, name="ada_fwd", out_shape=jax.ShapeDtypeStruct((16, w_sh.shape[1]), F32),
        compiler_params=pltpu.CompilerParams(vmem_limit_bytes=VMEM_LIMIT),
    )(c16, w_sh, b_sh)


def _ada_bwd(c16, g16, g16_sh, w_sh):
    ncol = w_sh.shape[1]

    def body(c_ref, g_ref, gs_ref, w_ref, dw_ref, db_ref, dc_ref):
        c = c_ref[...]
        s = _sig(c)
        gs = gs_ref[...]
        dw_ref[...] = _dot(c * s, gs, TN)
        db_ref[...] = jnp.broadcast_to(_colsum(g_ref[...]), db_ref.shape)
        odd = lax.broadcasted_iota(jnp.int32, gs.shape, 0) % 2 == 1
        gc = _colsum(jnp.where(odd, gs, 0.0))
        ds = _dot(jnp.broadcast_to(gc, (8, ncol)), w_ref[...], NT)
        c1 = c[1:2, :]
        s1 = s[1:2, :]
        dc_ref[...] = ds * (s1 * (1.0 + c1 * (1.0 - s1)))

    return pl.pallas_call(
        body, name="ada_bwd",
        out_shape=[jax.ShapeDtypeStruct(w_sh.shape, F32),
                   jax.ShapeDtypeStruct((8, g16.shape[1]), F32),
                   jax.ShapeDtypeStruct((8, D), F32)],
        compiler_params=pltpu.CompilerParams(vmem_limit_bytes=VMEM_LIMIT),
    )(c16, g16, g16_sh, w_sh)


def _ln0_fwd(x, ctx, g, b, modx, modc):
    L = x.shape[0]
    nt = L // TL

    def body(x_ref, c_ref, g_ref, b_ref, mx_ref, mc_ref, xn_ref, h_ref):
        isc = pl.program_id(0) == nt
        xin = jnp.where(isc, c_ref[...], x_ref[...])
        sh = jnp.where(isc, mc_ref[0:1, :], mx_ref[0:1, :])
        sc = jnp.where(isc, mc_ref[1:2, :], mx_ref[1:2, :])
        xhat, _ = _ln(xin)
        xn = xhat * g_ref[...] + b_ref[...]
        xn_ref[...] = xn
        h_ref[...] = (xn * (1.0 + sc) + sh).astype(h_ref.dtype)

    return pl.pallas_call(
        body, name="ln0_fwd", grid=(nt + 1,),
        in_specs=[_rtc(D, nt), _cst((TL, D)), _cst((1, D)), _cst((1, D)), _cst((8, D)), _cst((8, D))],
        out_specs=[_rt(D), _rt(D)],
        out_shape=[jax.ShapeDtypeStruct((L + TL, D), F32), jax.ShapeDtypeStruct((L + TL, D), _MXU)],
        compiler_params=_params(("parallel",)),
    )(x, ctx, g, b, modx, modc)


def _xbc_colblk(j):
    return jnp.where(j < 8, OXS // 128 + j, OB // 128 + j - 8)


def _conv_taps(p_ref, r0, first, last):
    main = p_ref[pl.ds(r0, TL), :]
    zero = jnp.zeros((8, main.shape[1]), F32)
    prev = zero if first else p_ref[pl.ds(r0 - 8, 8), :]
    nxt = zero if last else p_ref[pl.ds(r0 + TL, 8), :]
    ext = jnp.concatenate([prev, main, nxt], axis=0)
    n = TL + 16
    return [pltpu.roll(ext, (2 - k) % n, 0)[8:8 + TL] for k in range(5)]


def _seq_chunks(L):
    nt = L // TL
    return [(r * TL, r == 0, r == nt - 1) for r in range(nt)] + [(L, True, True)]


def _conv_fwd(p, conv_w8, conv_b):
    RT = p.shape[0]
    L = RT - TL
    chunks = _seq_chunks(L)

    def body(p_ref, w_ref, b_ref, o_ref):
        w = w_ref[...]
        bias = b_ref[...]
        for r0, first, last in chunks:
            taps = _conv_taps(p_ref, r0, first, last)
            pre = bias + sum(w[k:k + 1, :] * taps[k] for k in range(5))
            o_ref[pl.ds(r0, TL), :] = pre * _sig(pre)

    return pl.pallas_call(
        body, name="conv_fwd", grid=(12,),
        in_specs=[pl.BlockSpec((RT, 128), lambda j: (0, _xbc_colblk(j))),
                  pl.BlockSpec((8, 128), lambda j: (0, j)),
                  pl.BlockSpec((1, 128), lambda j: (0, j))],
        out_specs=pl.BlockSpec((RT, 128), lambda j: (0, j)),
        out_shape=jax.ShapeDtypeStruct((RT, 1536), F32),
        compiler_params=_params(("parallel",)),
    )(p, conv_w8, conv_b)


def _ssd_common(dtraw, dtb, a32, rev):
    dt = _softplus(dtraw + dtb)
    acum = _cumsum_rows(dt * a32, rev)
    ii = lax.broadcasted_iota(jnp.int32, (Q, Q), 0)
    jj = lax.broadcasted_iota(jnp.int32, (Q, Q), 1)
    mask = (ii <= jj) if rev else (ii >= jj)
    return dt, acum, acum.T, dt.T, mask


def _ssd_orders(ncl, ncc):
    nc = ncl + ncc

    def cf(s):
        return jnp.where(s < ncc, ncl + s, s - ncc)

    def cb(s):
        return nc - 1 - s

    return cf, cb


def _lane_bcast(v, ln):
    return jnp.broadcast_to(v[:, ln:ln + 1], v.shape)


def _halves(v, lo, axis):
    return jnp.concatenate([jnp.where(lo, v, 0.0), jnp.where(lo, 0.0, v)], axis=axis)


def _ssd2_fwd(xbc, p, prm):
    RT = xbc.shape[0]
    nc = RT // Q
    ncc = TL // Q
    cf, cb = _ssd_orders(nc - ncc, ncc)

    def one_dir(x_ref, dt_ref, prm_ref, y_ref, hp_ref, HT_ref, d):
        rev = d == 1
        a32 = -jnp.exp(prm_ref[1:2, :])
        dt, acum, acumT, dtT, mask = _ssd_common(dt_ref[...], prm_ref[0:1, :], a32, rev)
        end = 0 if rev else Q - 1
        lo = lax.broadcasted_iota(jnp.int32, (Q, 128), 1) < HP
        for g in range(2):
            Bg = x_ref[:, D + g * NS:D + (g + 1) * NS]
            Cg = x_ref[:, D + 2 * NS + g * NS:D + 2 * NS + (g + 1) * NS]
            CB = _dot(Cg, Bg, NT)
            xds, svs = [], []
            for q in range(HPG // 2):
                pi = g * (HPG // 2) + q
                ps = slice(pi * 128, (pi + 1) * 128)
                Xp = x_ref[:, ps]
                HTp = HT_ref[:, ps]
                lhs, dcs, sv = [], [], []
                ces = []
                for h in (2 * pi, 2 * pi + 1):
                    ln = 16 * d + h
                    colB = _lane_bcast(acum, ln)
                    rowv = acumT[ln:ln + 1, :]
                    aend = colB[end:end + 1, :]
                    Lm = jnp.exp(jnp.where(mask, colB - rowv, -1e30))
                    lhs.append(CB * Lm * dtT[ln:ln + 1, :])
                    ces.append(Cg * jnp.exp(colB))
                    dcs.append(jnp.exp(aend - colB) * _lane_bcast(dt, ln))
                    sv.append(jnp.exp(aend))
                lhs = jnp.concatenate(lhs + ces, axis=1)
                rhs = jnp.concatenate([_halves(Xp, lo, 0), _halves(HTp, lo, 0)], axis=0)
                y_ref[:, ps] = _dot(lhs, rhs, NN)
                xds.append(Xp * jnp.where(lo, dcs[0], dcs[1]))
                svs.append(jnp.where(lo[0:1, :], sv[0], sv[1]))
            gs = slice(g * 512, (g + 1) * 512)
            HTg = HT_ref[:, gs]
            hp_ref[0, :, gs] = HTg
            st = _dot(Bg.T, jnp.concatenate(xds, axis=1), NN)
            HT_ref[:, gs] = jnp.concatenate(svs, axis=1) * HTg + st

    def body(xf_ref, xb_ref, df_ref, db_ref, prm_ref, yf_ref, yb_ref, hf_ref, hb_ref, Hf, Hb):
        @pl.when(pl.program_id(0) == 0)
        def _():
            Hf[...] = jnp.zeros_like(Hf)
            Hb[...] = jnp.zeros_like(Hb)

        one_dir(xf_ref, df_ref, prm_ref, yf_ref, hf_ref, Hf, 0)
        one_dir(xb_ref, db_ref, prm_ref, yb_ref, hb_ref, Hb, 1)

    ysh = jax.ShapeDtypeStruct((RT, D), F32)
    hsh = jax.ShapeDtypeStruct((nc, NS, NH * HP), F32)
    hspec = pl.BlockSpec((1, NS, NH * HP), lambda s: (s, 0, 0))
    return pl.pallas_call(
        body, name="ssd_fwd", grid=(nc,),
        in_specs=[pl.BlockSpec((Q, 1536), lambda s: (cf(s), 0)),
                  pl.BlockSpec((Q, 1536), lambda s: (cb(s), 0)),
                  pl.BlockSpec((Q, 128), lambda s: (cf(s), ODT // 128)),
                  pl.BlockSpec((Q, 128), lambda s: (cb(s), ODT // 128)),
                  _cst((8, 128))],
        out_specs=[pl.BlockSpec((Q, D), lambda s: (cf(s), 0)),
                   pl.BlockSpec((Q, D), lambda s: (cb(s), 0)), hspec, hspec],
        out_shape=[ysh, ysh, hsh, hsh],
        scratch_shapes=[pltpu.VMEM((NS, NH * HP), F32), pltpu.VMEM((NS, NH * HP), F32)],
        compiler_params=_params(("arbitrary",)),
    )(xbc, xbc, p, p, prm)


def _ssd2_bwd(xbc, p, prm, dsk, dyd, hpf, hpb):
    RT = xbc.shape[0]
    nc = RT // Q
    ncc = TL // Q
    ncl = nc - ncc
    cf, cb = _ssd_orders(ncl, ncc)

    def rs(t):
        return nc - 1 - t

    def one_dir(x_ref, dt_ref, prm_ref, dsk_ref, dy_ref, is_ctx, hp_ref, dHT_ref,
                dx_ref, ddt_ref, st_ref, d):
        rev = d == 1
        a32 = -jnp.exp(prm_ref[1:2, :])
        dtraw = dt_ref[...]
        dtb = prm_ref[0:1, :]
        dt, acum, acumT, _, _ = _ssd_common(dtraw, dtb, a32, rev)
        end = 0 if rev else Q - 1
        lane = lax.broadcasted_iota(jnp.int32, (Q, 128), 1)
        srow = lax.broadcasted_iota(jnp.int32, (Q, 128), 0)
        maskT = (lane <= srow) if rev else (lane >= srow)
        lo = lane < HP
        lo1 = lo[0:1, :]
        dyscale = jnp.where(is_ctx, 0.0, 1.0)
        c_dacum = jnp.zeros((Q, 128), F32)
        r_dacum = jnp.zeros((Q, 128), F32)
        c_ddt = jnp.zeros((Q, 128), F32)
        dskacc = jnp.zeros((1, 128), F32)
        for g in range(2):
            gs = slice(g * 512, (g + 1) * 512)
            Bg = x_ref[:, D + g * NS:D + (g + 1) * NS]
            Cg = x_ref[:, D + 2 * NS + g * NS:D + 2 * NS + (g + 1) * NS]
            CBT = _dot(Bg, Cg, NT)
            HTg = hp_ref[0, :, gs]
            dHTg = dHT_ref[:, gs]
            BdHg = _dot(Bg, dHTg, NN)
            dCBT = jnp.zeros((Q, Q), F32)
            dCg = jnp.zeros((Q, NS), F32)
            xds, dyes, svs = [], [], []
            for q in range(HPG // 2):
                pi = g * (HPG // 2) + q
                ps = slice(pi * 128, (pi + 1) * 128)
                qs = slice(q * 128, (q + 1) * 128)
                Xp = x_ref[:, ps]
                dYp = dy_ref[:, ps] * dyscale
                HTp = HTg[:, qs]
                BdHp = BdHg[:, qs]
                dY2 = _halves(dYp, lo, 0)
                dWT2 = _dot(_halves(Xp, lo, 0), dYp.T, NN)
                G2 = _dot(dY2, HTp, NT)
                XB = Xp * BdHp
                hh = _colsum(dHTg[:, qs] * HTp)
                yx = _colsum(dYp * Xp)
                wts, dcs, ebs, sv = [], [], [], []
                for k, h in enumerate((2 * pi, 2 * pi + 1)):
                    ln = 16 * d + h
                    half = lo if k == 0 else jnp.logical_not(lo)
                    half1 = half[0:1, :]
                    colB = _lane_bcast(acum, ln)
                    dtcB = _lane_bcast(dt, ln)
                    rowv = acumT[ln:ln + 1, :]
                    aend = colB[end:end + 1, :]
                    LmT = jnp.exp(jnp.where(maskT, rowv - colB, -1e30))
                    WT = CBT * LmT * dtcB
                    dWT = dWT2[k * Q:(k + 1) * Q, :]
                    U = dWT * LmT
                    MT = U * CBT
                    rM = jnp.sum(MT, axis=1, keepdims=True)
                    rT = _colsum(MT * dtcB)
                    dCBT = dCBT + U * dtcB
                    ecol = jnp.exp(aend - colB)
                    EB = jnp.exp(colB)
                    Gk = G2[k * Q:(k + 1) * Q, :]
                    dCg = dCg + EB * Gk
                    qcol = jnp.sum(EB * Gk * Cg, axis=1, keepdims=True)
                    xb = jnp.sum(jnp.where(half, XB, 0.0), axis=1, keepdims=True)
                    e1 = ecol[:, 0:1]
                    dt1 = dtcB[:, 0:1]
                    scol = e1 * dt1 * xb
                    sA = jnp.exp(aend)
                    eterm = sA[:, 0:1] * jnp.sum(jnp.where(half1, hh, 0.0), axis=1, keepdims=True) \
                        + _colsum(scol)
                    cvec = qcol - dt1 * rM - scol + jnp.where(srow[:, 0:1] == end, eterm, 0.0)
                    c_dacum = jnp.where(lane == ln, cvec, c_dacum)
                    r_dacum = jnp.where(srow == ln, rT, r_dacum)
                    c_ddt = jnp.where(lane == ln, rM + e1 * xb, c_ddt)
                    if d == 0:
                        dskacc = dskacc + jnp.where(
                            lane[0:1, :] == h, jnp.sum(jnp.where(half1, yx, 0.0), axis=1, keepdims=True), 0.0)
                    wts.append(WT)
                    dcs.append(ecol * dtcB)
                    ebs.append(EB)
                    sv.append(sA)
                dcp = jnp.where(lo, dcs[0], dcs[1])
                dX = _dot(jnp.concatenate(wts, axis=1), dY2, NN) + dcp * BdHp
                if d == 0:
                    dX = dX + dYp * dsk_ref[:, ps]
                dx_ref[:, ps] = dX
                xds.append(Xp * dcp)
                dyes.append(dYp * jnp.where(lo, ebs[0], ebs[1]))
                svs.append(jnp.where(lo1, sv[0], sv[1]))
            dx_ref[:, D + g * NS:D + (g + 1) * NS] = (
                _dot(jnp.concatenate(xds, axis=1), dHTg, NT) + _dot(dCBT, Cg, NN))
            dx_ref[:, D + 2 * NS + g * NS:D + 2 * NS + (g + 1) * NS] = dCg + _dot(dCBT, Bg, TN)
            dHT_ref[:, gs] = (jnp.concatenate(svs, axis=1) * dHTg
                              + _dot(Cg.T, jnp.concatenate(dyes, axis=1), NN))
        dacum = c_dacum + r_dacum.T
        da = _cumsum_rows(dacum, not rev)
        mine = (lane >= 16 * d) & (lane < 16 * d + 16)
        ddt = jnp.where(mine, c_ddt + da * a32, 0.0)
        ddt_ref[...] = ddt * _sig(dtraw + dtb)
        st_ref[0:1, :] += _colsum(jnp.where(mine, da * dt, 0.0))
        if d == 0:
            st_ref[1:2, :] += dskacc

    def body(xf_ref, xb_ref, df_ref, db_ref, prm_ref, dsk_ref, dyf_ref, dyb_ref, hf_ref, hb_ref,
             dxf_ref, dxb_ref, ddf_ref, ddb_ref, st_ref, dHf, dHb):
        t = pl.program_id(0)

        @pl.when(t == 0)
        def _():
            dHf[...] = jnp.zeros_like(dHf)
            dHb[...] = jnp.zeros_like(dHb)
            st_ref[...] = jnp.zeros_like(st_ref)

        s = rs(t)
        one_dir(xf_ref, df_ref, prm_ref, dsk_ref, dyf_ref, cf(s) >= ncl, hf_ref, dHf,
                dxf_ref, ddf_ref, st_ref, 0)
        one_dir(xb_ref, db_ref, prm_ref, dsk_ref, dyb_ref, cb(s) >= ncl, hb_ref, dHb,
                dxb_ref, ddb_ref, st_ref, 1)

        @pl.when(t == nc - 1)
        def _():
            st_ref[0:1, :] = -jnp.exp(prm_ref[1:2, :]) * st_ref[0:1, :]

    def lat(c):
        return jnp.minimum(c, ncl - 1)

    xsh = jax.ShapeDtypeStruct((RT, 1536), F32)
    dsh = jax.ShapeDtypeStruct((RT, 128), F32)
    hspec = pl.BlockSpec((1, NS, NH * HP), lambda t: (rs(t), 0, 0))
    return pl.pallas_call(
        body, name="ssd_bwd", grid=(nc,),
        in_specs=[pl.BlockSpec((Q, 1536), lambda t: (cf(rs(t)), 0)),
                  pl.BlockSpec((Q, 1536), lambda t: (cb(rs(t)), 0)),
                  pl.BlockSpec((Q, 128), lambda t: (cf(rs(t)), ODT // 128)),
                  pl.BlockSpec((Q, 128), lambda t: (cb(rs(t)), ODT // 128)),
                  _cst((8, 128)), _cst((1, D)),
                  pl.BlockSpec((Q, D), lambda t: (lat(cf(rs(t))), 0)),
                  pl.BlockSpec((Q, D), lambda t: (lat(cb(rs(t))), 0)),
                  hspec, hspec],
        out_specs=[pl.BlockSpec((Q, 1536), lambda t: (cf(rs(t)), 0)),
                   pl.BlockSpec((Q, 1536), lambda t: (cb(rs(t)), 0)),
                   pl.BlockSpec((Q, 128), lambda t: (cf(rs(t)), 0)),
                   pl.BlockSpec((Q, 128), lambda t: (cb(rs(t)), 0)),
                   _cst((8, 128))],
        out_shape=[xsh, xsh, dsh, dsh, jax.ShapeDtypeStruct((8, 128), F32)],
        scratch_shapes=[pltpu.VMEM((NS, NH * HP), F32), pltpu.VMEM((NS, NH * HP), F32)],
        compiler_params=_params(("arbitrary",)),
    )(xbc, xbc, p, p, prm, dsk, dyd, dyd, hpf, hpb)


def _mix_fwd_vals(yf, yb, z, xs, u, v, dsk, sg, gg, gb):
    y = yf + yb + xs * dsk
    sz = _sig(z)
    hh = y * z * sz
    r = lax.rsqrt(jnp.mean(hh * hh, axis=-1, keepdims=True) + EPS)
    nh = hh * r
    ug, tu = _gelu(u)
    vg, tv = _gelu(v)
    vhat, vrstd = _ln(vg)
    vn = vhat * gg + gb
    return y, sz, r, nh, ug, tu, vg, tv, vhat, vrstd, vn


def _mix_fwd(yf, yb, p, xbc, dsk, sg, gg, gb, ws, bsT):
    L = yf.shape[0] - TL
    nt = L // TL

    def body(yf_ref, yb_ref, z_ref, xs_ref, u_ref, v_ref, dsk_ref, sg_ref, gg_ref, gb_ref,
             ws_ref, bs_ref, ys_ref, ym_ref):
        _, _, _, nh, ug, _, _, _, _, _, vn = _mix_fwd_vals(
            yf_ref[...], yb_ref[...], z_ref[...], xs_ref[...], u_ref[...], v_ref[...],
            dsk_ref[...], sg_ref[...], gg_ref[...], gb_ref[...])
        ys_ref[...] = (nh * sg_ref[...]).astype(ys_ref.dtype)
        for n in range(TL // Q):
            rs_ = slice(n * Q, (n + 1) * Q)
            for g in range(8):
                cs = slice(g * 128, (g + 1) * 128)
                mixed = _dot(ws_ref[g], vn[rs_, cs], NN) + bs_ref[:, g:g + 1]
                ym_ref[rs_, cs] = (ug[rs_, cs] * mixed).astype(ym_ref.dtype)

    return pl.pallas_call(
        body, name="mix_fwd", grid=(nt,),
        in_specs=[_rt(D), _rt(D), _rt(D, OZ // D), _rt(D, 0), _rt(D, OU // D), _rt(D, OV // D),
                  _cst((1, D)), _cst((1, D)), _cst((1, D)), _cst((1, D)),
                  _cst((8, 128, 128)), _cst((128, 128))],
        out_specs=[_rt(D), _rt(D)],
        out_shape=[jax.ShapeDtypeStruct((L, D), _MXU), jax.ShapeDtypeStruct((L, D), _MXU)],
        compiler_params=_params(("parallel",)),
    )(yf, yb, p, xbc, p, p, dsk, sg, gg, gb, ws, bsT)


def _mix_bwd(dys, dym, yf, yb, p, xbc, dp, dsk, sg, gg, gb, ws, bsT):
    L = dys.shape[0]
    nt = L // TL

    def body(dys_ref, dym_ref, yf_ref, yb_ref, z_ref, xs_ref, u_ref, v_ref, dsk_ref, sg_ref,
             gg_ref, gb_ref, ws_ref, bs_ref, dp_any, dzuv_ref, dy_ref, st_ref,
             dws_ref, dbs_ref, dvn_s):
        del dp_any
        dz_ref = dzuv_ref.at[:, OZ:OZ + D]
        du_ref = dzuv_ref.at[:, OU:OU + D]
        dv_ref = dzuv_ref.at[:, OV:OV + D]

        @pl.when(pl.program_id(0) == 0)
        def _():
            st_ref[...] = jnp.zeros_like(st_ref)
            dws_ref[...] = jnp.zeros_like(dws_ref)
            dbs_ref[...] = jnp.zeros_like(dbs_ref)

        z = z_ref[...]
        u = u_ref[...]
        v = v_ref[...]
        y, sz, r, nh, ug, tu, vg, tv, vhat, vrstd, vn = _mix_fwd_vals(
            yf_ref[...], yb_ref[...], z, xs_ref[...], u, v,
            dsk_ref[...], sg_ref[...], gg_ref[...], gb_ref[...])
        dys = dys_ref[...]
        st_ref[0:1, :] += _colsum(dys * nh)
        dn = dys * sg_ref[...]
        dhh = r * (dn - nh * jnp.mean(dn * nh, axis=-1, keepdims=True))
        dy_ref[...] = dhh * z * sz
        dz_ref[...] = (dhh * y * (sz * (1.0 + z * (1.0 - sz)))).astype(dz_ref.dtype)
        dym = dym_ref[...]
        lane = lax.broadcasted_iota(jnp.int32, (Q, 128), 1)
        dbs = jnp.zeros((Q, 128), F32)
        gu = _gelu_grad(u, tu)
        for n in range(TL // Q):
            rs_ = slice(n * Q, (n + 1) * Q)
            for g in range(8):
                cs = slice(g * 128, (g + 1) * 128)
                vb = vn[rs_, cs]
                mixed = _dot(ws_ref[g], vb, NN) + bs_ref[:, g:g + 1]
                dyb = dym[rs_, cs]
                dmx = dyb * ug[rs_, cs]
                du_ref[rs_, cs] = (dyb * mixed * gu[rs_, cs]).astype(du_ref.dtype)
                dvn_s[rs_, cs] = _dot(ws_ref[g], dmx, TN)
                dws_ref[g] += _dot(dmx, vb, NT)
                dbs = dbs + jnp.where(lane == g, jnp.sum(dmx, axis=1, keepdims=True), 0.0)
        dbs_ref[...] += dbs
        dvn = dvn_s[...]
        st_ref[1:2, :] += _colsum(dvn * vhat)
        st_ref[2:3, :] += _colsum(dvn)
        dvg = _ln_bwd(dvn * gg_ref[...], vhat, vrstd)
        dv_ref[...] = (dvg * _gelu_grad(v, tv)).astype(dv_ref.dtype)

    outs = pl.pallas_call(
        body, name="mix_bwd", grid=(nt,),
        in_specs=[_rt(D), _rt(D), _rt(D), _rt(D), _rt(D, OZ // D), _rt(D, 0), _rt(D, OU // D),
                  _rt(D, OV // D), _cst((1, D)), _cst((1, D)), _cst((1, D)), _cst((1, D)),
                  _cst((8, 128, 128)), _cst((128, 128)), pl.BlockSpec(memory_space=pl.ANY)],
        out_specs=[_rt(3 * D, 0), _rt(D), _cst((8, D)),
                   _cst((8, 128, 128)), _cst((128, 128))],
        out_shape=[jax.ShapeDtypeStruct(dp.shape, dp.dtype),
                   jax.ShapeDtypeStruct((L, D), F32), jax.ShapeDtypeStruct((8, D), F32),
                   jax.ShapeDtypeStruct((8, 128, 128), F32), jax.ShapeDtypeStruct((128, 128), F32)],
        scratch_shapes=[pltpu.VMEM((TL, D), F32)],
        input_output_aliases={14: 0},
        compiler_params=_params(("arbitrary",)),
    )(dys, dym, yf, yb, p, xbc, p, p, dsk, sg, gg, gb, ws, bsT, dp)
    return outs


def _merge_fwd(yssd, ygm, p, bg, ws, wg, wo, xn, modx, g1, b1):
    L = yssd.shape[0]
    tm = TL

    def body(ys_ref, yg_ref, g_ref, bg_ref, ws_ref, wg_ref, wo_ref, xn_ref, mx_ref, g1_ref, b1_ref,
             a1_ref, a2_ref, m_ref, o_ref, r1_ref, h2_ref):
        a1 = _dot(ys_ref[...], ws_ref[...], NN)
        a2 = _dot(yg_ref[...], wg_ref[...], NN)
        gt = _sig(g_ref[...] + bg_ref[...])
        mg = gt[:, :D] * a1 + gt[:, D:] * a2
        a1_ref[...] = a1
        a2_ref[...] = a2
        m_ref[...] = mg.astype(m_ref.dtype)
        out = _dot(mg, wo_ref[...], NN)
        o_ref[...] = out
        r1 = ALPHA * xn_ref[...] + mx_ref[2:3, :] * out
        xhat, _ = _ln(r1)
        x1 = xhat * g1_ref[...] + b1_ref[...]
        r1_ref[...] = r1
        h2_ref[...] = (x1 * (1.0 + mx_ref[4:5, :]) + mx_ref[3:4, :]).astype(h2_ref.dtype)

    rows = pl.BlockSpec((tm, D), lambda i: (i, 0))
    f32s = jax.ShapeDtypeStruct((L, D), F32)
    mxus = jax.ShapeDtypeStruct((L, D), _MXU)
    return pl.pallas_call(
        body, name="merge_fwd", grid=(L // tm,),
        in_specs=[rows, rows, pl.BlockSpec((tm, 2 * D), lambda i: (i, OG // (2 * D))), _cst((1, 2 * D)),
                  _cst((D, D)), _cst((D, D)), _cst((D, D)), rows, _cst((8, D)), _cst((1, D)), _cst((1, D))],
        out_specs=[rows] * 6,
        out_shape=[f32s, f32s, mxus, f32s, f32s, mxus],
        compiler_params=_params(("parallel",)),
    )(yssd, ygm, p, bg, ws, wg, wo, xn, modx, g1, b1)


def _mm_o2_res2(ff, w2, r1, tgt, modx, g1, b1, g2, b2):
    L, K = ff.shape
    tm = 2 * TL

    def body(a_ref, b_ref, r1_ref, t_ref, mx_ref, g1_ref, b1_ref, g2_ref, b2_ref,
             dr2_ref, do2_ref, st_ref, loss_ref):
        @pl.when(pl.program_id(0) == 0)
        def _():
            st_ref[...] = jnp.zeros_like(st_ref)
            loss_ref[...] = jnp.zeros_like(loss_ref)

        o2 = _dot(a_ref[...], b_ref[...], NN)
        xh1, _ = _ln(r1_ref[...])
        x1 = xh1 * g1_ref[...] + b1_ref[...]
        g2x = mx_ref[5:6, :]
        xh2, rstd2 = _ln(ALPHA * x1 + g2x * o2)
        err = xh2 * g2_ref[...] + b2_ref[...] - t_ref[...]
        per_tok = jnp.mean(err * err, axis=-1, keepdims=True)
        loss_ref[...] += 0.5 * jnp.sum(per_tok, axis=0, keepdims=True)
        dy = err * (1.0 / D)
        st_ref[0:1, :] += _colsum(dy * xh2)
        st_ref[1:2, :] += _colsum(dy)
        dr2 = _ln_bwd(dy * g2_ref[...], xh2, rstd2)
        st_ref[2:3, :] += _colsum(dr2 * o2)
        dr2_ref[...] = dr2
        do2_ref[...] = (g2x * dr2).astype(do2_ref.dtype)

    return pl.pallas_call(
        body, name="mm_o2_res2", grid=(L // tm,),
        in_specs=[_rt(K, rows=tm), _cst((K, D)), _rt(D, rows=tm), _rt(D, rows=tm), _cst((8, D))]
        + [_cst((1, D))] * 4,
        out_specs=[_rt(D, rows=tm), _rt(D, rows=tm), _cst((8, D)), _cst((8, 128))],
        out_shape=[jax.ShapeDtypeStruct((L, D), F32), jax.ShapeDtypeStruct((L, D), _MXU),
                   jax.ShapeDtypeStruct((8, D), F32), jax.ShapeDtypeStruct((8, 128), F32)],
        compiler_params=_params(("arbitrary",)),
    )(ff, w2, r1, tgt, modx, g1, b1, g2, b2)


def _mm_dh2_res1bwd(df13, w13i, dr2, r1, out, modx, g1, b1):
    L, K = df13.shape

    def body(a_ref, b_ref, dr2_ref, r1_ref, o_ref, mx_ref, g_ref, bb_ref, dr1_ref, do_ref, st_ref):
        @pl.when(pl.program_id(0) == 0)
        def _():
            st_ref[...] = jnp.zeros_like(st_ref)

        dh2 = _dot(a_ref[...], b_ref[...], NN)
        xh1, rstd1 = _ln(r1_ref[...])
        x1 = xh1 * g_ref[...] + bb_ref[...]
        dx1 = ALPHA * dr2_ref[...] + dh2 * (1.0 + mx_ref[4:5, :])
        st_ref[0:1, :] += _colsum(dh2 * x1)
        st_ref[1:2, :] += _colsum(dh2)
        st_ref[2:3, :] += _colsum(dx1 * xh1)
        st_ref[3:4, :] += _colsum(dx1)
        dr1 = _ln_bwd(dx1 * g_ref[...], xh1, rstd1)
        st_ref[4:5, :] += _colsum(dr1 * o_ref[...])
        dr1_ref[...] = dr1
        do_ref[...] = (mx_ref[2:3, :] * dr1).astype(do_ref.dtype)

    return pl.pallas_call(
        body, name="mm_dh2_res1bwd", grid=(L // TL,),
        in_specs=[_rt(K), _cst((K, D)), _rt(D), _rt(D), _rt(D), _cst((8, D)), _cst((1, D)), _cst((1, D))],
        out_specs=[_rt(D), _rt(D), _cst((8, D))],
        out_shape=[jax.ShapeDtypeStruct((L, D), F32), jax.ShapeDtypeStruct((L, D), _MXU),
                   jax.ShapeDtypeStruct((8, D), F32)],
        compiler_params=_params(("arbitrary",)),
    )(df13, w13i, dr2, r1, out, modx, g1, b1)


def _merge_bwd(dout, a1, a2, p, bg, wo, ws, wg, dp):
    L = a1.shape[0]
    tm = TL

    def body(do_ref, a1_ref, a2_ref, g_ref, bg_ref, wo_ref, ws_ref, wg_ref, dp_any,
             dg_ref, da1_ref, da2_ref, st_ref, dys_ref, dym_ref):
        del dp_any

        @pl.when(pl.program_id(0) == 0)
        def _():
            st_ref[...] = jnp.zeros_like(st_ref)

        dm = _dot(do_ref[...], wo_ref[...], NT)
        gt = _sig(g_ref[...] + bg_ref[...])
        g1 = gt[:, :D]
        g2 = gt[:, D:]
        da1 = (dm * g1).astype(da1_ref.dtype)
        da2 = (dm * g2).astype(da2_ref.dtype)
        da1_ref[...] = da1
        da2_ref[...] = da2
        dg1 = dm * a1_ref[...] * g1 * (1.0 - g1)
        dg2 = dm * a2_ref[...] * g2 * (1.0 - g2)
        st_ref[0:1, 0:D] += _colsum(dg1)
        st_ref[0:1, D:2 * D] += _colsum(dg2)
        dg_ref[:, 0:D] = dg1.astype(dg_ref.dtype)
        dg_ref[:, D:2 * D] = dg2.astype(dg_ref.dtype)
        dys_ref[...] = _dot(da1, ws_ref[...], NT)
        dym_ref[...] = _dot(da2, wg_ref[...], NT)

    rows = pl.BlockSpec((tm, D), lambda i: (i, 0))
    gates = pl.BlockSpec((tm, 2 * D), lambda i: (i, OG // (2 * D)))
    f32s = jax.ShapeDtypeStruct((L, D), F32)
    mxus = jax.ShapeDtypeStruct((L, D), _MXU)
    return pl.pallas_call(
        body, name="merge_bwd", grid=(L // tm,),
        in_specs=[rows, rows, rows, gates, _cst((1, 2 * D)), _cst((D, D)), _cst((D, D)), _cst((D, D)),
                  pl.BlockSpec(memory_space=pl.ANY)],
        out_specs=[gates, rows, rows, _cst((8, 2 * D)), rows, rows],
        out_shape=[jax.ShapeDtypeStruct(dp.shape, dp.dtype), mxus, mxus,
                   jax.ShapeDtypeStruct((8, 2 * D), F32), f32s, f32s],
        input_output_aliases={8: 0},
        compiler_params=_params(("arbitrary",)),
    )(dout, a1, a2, p, bg, wo, ws, wg, dp)


HFF = DFF // 2


def _mm_f13_glu(h2, w13i):
    L = h2.shape[0]
    tm = 512

    def body(a_ref, b_ref, f_ref, ff_ref):
        f = _dot(a_ref[...], b_ref[...], NT)
        f_ref[...] = f
        f1 = f[:, :HFF]
        ff_ref[...] = (f1 * _sig(f1) * f[:, HFF:]).astype(ff_ref.dtype)

    return pl.pallas_call(
        body, name="mm_f13_glu", grid=(DFF // HFF, L // tm),
        in_specs=[pl.BlockSpec((tm, D), lambda j, i: (i, 0)), pl.BlockSpec((2 * HFF, D), lambda j, i: (j, 0))],
        out_specs=[pl.BlockSpec((tm, 2 * HFF), lambda j, i: (i, j)), pl.BlockSpec((tm, HFF), lambda j, i: (i, j))],
        out_shape=[jax.ShapeDtypeStruct((L, 2 * DFF), F32), jax.ShapeDtypeStruct((L, DFF), _MXU)],
        compiler_params=_params(("parallel", "parallel")),
    )(h2, w13i)


def _mm_dff_glu(do2, w_ff2_f, f13i):
    L = do2.shape[0]
    tm = 512

    def body(a_ref, b_ref, f_ref, o_ref):
        d = _dot(a_ref[...], b_ref[...], NT)
        f1 = f_ref[:, :HFF]
        s = _sig(f1)
        o_ref[:, :HFF] = (d * f_ref[:, HFF:] * (s * (1.0 + f1 * (1.0 - s)))).astype(o_ref.dtype)
        o_ref[:, HFF:] = (d * f1 * s).astype(o_ref.dtype)

    return pl.pallas_call(
        body, name="mm_dff_glu", grid=(DFF // HFF, L // tm),
        in_specs=[pl.BlockSpec((tm, D), lambda j, i: (i, 0)), pl.BlockSpec((HFF, D), lambda j, i: (j, 0)),
                  pl.BlockSpec((tm, 2 * HFF), lambda j, i: (i, j))],
        out_specs=pl.BlockSpec((tm, 2 * HFF), lambda j, i: (i, j)),
        out_shape=jax.ShapeDtypeStruct((L, 2 * DFF), _MXU),
        compiler_params=_params(("parallel", "parallel")),
    )(do2, w_ff2_f, f13i)


def _conv_bwd(dxf, dxb, p, conv_w8, conv_b, dp):
    RT = p.shape[0]
    chunks = _seq_chunks(RT - TL)

    def body(df_ref, db_ref, p_ref, w_ref, b_ref, dp_any, o_ref, dw_ref, dbias_ref, dpre_s):
        del dp_any
        w = w_ref[...]
        bias = b_ref[...]
        srow = lax.broadcasted_iota(jnp.int32, (8, 128), 0)
        dwacc = jnp.zeros((8, 128), F32)
        dbacc = jnp.zeros((1, 128), F32)
        for r0, first, last in chunks:
            taps = _conv_taps(p_ref, r0, first, last)
            pre = bias + sum(w[k:k + 1, :] * taps[k] for k in range(5))
            s = _sig(pre)
            dpre = (df_ref[pl.ds(r0, TL), :] + db_ref[pl.ds(r0, TL), :]) * (s * (1.0 + pre * (1.0 - s)))
            dpre_s[pl.ds(r0, TL), :] = dpre
            dbacc = dbacc + _colsum(dpre)
            for k in range(5):
                dwacc = dwacc + jnp.where(srow == k, _colsum(dpre * taps[k]), 0.0)
        for r0, first, last in chunks:
            taps = _conv_taps(dpre_s, r0, first, last)
            dx = sum(w[k:k + 1, :] * taps[4 - k] for k in range(5))
            o_ref[pl.ds(r0, TL), :] = dx.astype(o_ref.dtype)
        dw_ref[...] = dwacc
        dbias_ref[...] = jnp.broadcast_to(dbacc, (8, 128))

    cspec = pl.BlockSpec((RT, 128), lambda j: (0, j))
    wspec = pl.BlockSpec((8, 128), lambda j: (0, j))
    return pl.pallas_call(
        body, name="conv_bwd", grid=(12,),
        in_specs=[cspec, cspec, pl.BlockSpec((RT, 128), lambda j: (0, _xbc_colblk(j))),
                  wspec, pl.BlockSpec((1, 128), lambda j: (0, j)), pl.BlockSpec(memory_space=pl.ANY)],
        out_specs=[pl.BlockSpec((RT, 128), lambda j: (0, _xbc_colblk(j))), wspec, wspec],
        out_shape=[jax.ShapeDtypeStruct(dp.shape, dp.dtype), jax.ShapeDtypeStruct((8, 1536), F32),
                   jax.ShapeDtypeStruct((8, 1536), F32)],
        scratch_shapes=[pltpu.VMEM((RT, 128), F32)],
        input_output_aliases={5: 0},
        compiler_params=_params(("parallel",)),
    )(dxf, dxb, p, conv_w8, conv_b, dp)


def _dp_fill(dp):
    nrow = dp.shape[0] // TL

    def body(dp_any, o_ref):
        del dp_any
        o_ref[...] = jnp.zeros_like(o_ref)

    return pl.pallas_call(
        body, name="dp_fill", grid=(1,),
        in_specs=[pl.BlockSpec(memory_space=pl.ANY)],
        out_specs=pl.BlockSpec((TL, OB), lambda i: (nrow - 1, 0)),
        out_shape=jax.ShapeDtypeStruct(dp.shape, dp.dtype),
        input_output_aliases={0: 0},
        compiler_params=_params(("arbitrary",)),
    )(dp)


def _dt_bwd(ddf, ddb, dp):
    RT = ddf.shape[0]
    assert NPJ - ODT == 256

    def body(f_ref, b_ref, dp_any, o_ref, st_ref):
        del dp_any

        @pl.when(pl.program_id(0) == 0)
        def _():
            st_ref[...] = jnp.zeros_like(st_ref)

        s = f_ref[...] + b_ref[...]
        o_ref[:, 0:128] = s.astype(o_ref.dtype)
        o_ref[:, 128:256] = jnp.zeros((TL, 128), o_ref.dtype)
        st_ref[0:1, :] += _colsum(s)

    return pl.pallas_call(
        body, name="dt_bwd", grid=(RT // TL,),
        in_specs=[_rt(128), _rt(128), pl.BlockSpec(memory_space=pl.ANY)],
        out_specs=[_rt(256, ODT // 256), _cst((8, 128))],
        out_shape=[jax.ShapeDtypeStruct(dp.shape, dp.dtype), jax.ShapeDtypeStruct((8, 128), F32)],
        input_output_aliases={2: 0},
        compiler_params=_params(("arbitrary",)),
    )(ddf, ddb, dp)


def _ln0_bwd(dh1, dr1, x, ctx, g, b, modx, modc):
    L = x.shape[0]
    nt = L // TL

    def body(dh_ref, dr1_ref, x_ref, c_ref, g_ref, b_ref, mx_ref, mc_ref, gx_ref, st_ref):
        i = pl.program_id(0)
        isc = i == nt

        @pl.when(i == 0)
        def _():
            st_ref[...] = jnp.zeros_like(st_ref)

        xin = jnp.where(isc, c_ref[...], x_ref[...])
        xhat, rstd = _ln(xin)
        xn = xhat * g_ref[...] + b_ref[...]
        sc = jnp.where(isc, mc_ref[1:2, :], mx_ref[1:2, :])
        dh = dh_ref[...]
        lat = jnp.where(isc, 0.0, 1.0)
        dxn = dh * (1.0 + sc) + (lat * ALPHA) * dr1_ref[...]
        tsh = _colsum(dh)
        tsc = _colsum(dh * xn)
        st_ref[0:1, :] += lat * tsh
        st_ref[1:2, :] += lat * tsc
        st_ref[2:3, :] += (1.0 - lat) * tsh
        st_ref[3:4, :] += (1.0 - lat) * tsc
        st_ref[4:5, :] += _colsum(dxn * xhat)
        st_ref[5:6, :] += _colsum(dxn)

        @pl.when(i < nt)
        def _():
            gx_ref[...] = _ln_bwd(dxn * g_ref[...], xhat, rstd)

    return pl.pallas_call(
        body, name="ln0_bwd", grid=(nt + 1,),
        in_specs=[_rt(D), _rtc(D, nt), _rtc(D, nt), _cst((TL, D)), _cst((1, D)), _cst((1, D)),
                  _cst((8, D)), _cst((8, D))],
        out_specs=[_rtc(D, nt), _cst((8, D))],
        out_shape=[jax.ShapeDtypeStruct((L, D), F32), jax.ShapeDtypeStruct((8, D), F32)],
        compiler_params=_params(("arbitrary",)),
    )(dh1, dr1, x, ctx, g, b, modx, modc)


SECTIONS = ((0, 1024, OZ), (1024, 2048, OXS), (2048, 2304, OB), (2304, 2560, OC), (2560, 2592, ODT),
            (2592, 3616, OU), (3616, 4640, OV), (4640, 6688, OG))


def _perm_from_blocks(ga):
    n = ga.shape[2]
    pieces = []
    for na, nb, _ in sorted(SECTIONS, key=lambda sec: sec[2]):
        for k in range(NDEV):
            lo, hi = max(na, k * n), min(nb, (k + 1) * n)
            if lo < hi:
                pieces.append(ga[k][:, lo - k * n:hi - k * n])
    pieces.append(jnp.zeros((ga.shape[1], NPJ - NNAT), ga.dtype))
    return jnp.concatenate(pieces, axis=1)


def _blocks_from_perm(gp, n):
    blocks = []
    for k in range(NDEV):
        pieces = []
        for na, nb, po in SECTIONS:
            lo, hi = max(na, k * n), min(nb, (k + 1) * n)
            if lo < hi:
                pieces.append(gp[:, po + lo - na:po + hi - na])
        blocks.append(jnp.concatenate(pieces, axis=1))
    return jnp.stack(blocks)


def _padded(n, row_align):
    unit = row_align * D
    return -(-n // unit) * unit if row_align else n


def _slab(arrs, rows, row_align=0):
    parts = []
    for a in arrs:
        f = a.reshape(-1)
        parts.append(jnp.pad(f, (0, _padded(f.shape[0], row_align) - f.shape[0])))
    flat = jnp.concatenate(parts)
    flat = jnp.pad(flat, (0, rows * D - flat.shape[0]))
    return flat.reshape(rows, D)


def _unslab(slab, shapes, row_align=0):
    out, off = [], 0
    for shp in shapes:
        n = 1
        for s in shp:
            n *= s
        r0, r1 = off // D, -(-(off + n) // D)
        out.append(slab[r0:r1].reshape(-1)[off - r0 * D:off - r0 * D + n].reshape(shp))
        off += _padded(n, row_align)
    return out


def _row(v):
    return v.reshape(1, -1)


def _t(a):
    return jnp.swapaxes(a, 0, 1)


def _pad_rows(a, rows):
    return jnp.pad(a, ((0, rows - a.shape[0]), (0, 0)))


REPL = ["c_ctx", "ln0_g", "ln0_b", "b_ada", "conv_b", "dt_bias", "a_log", "d_skip", "ssd_norm_g",
        "gm_norm_g", "gm_norm_b", "w_spatial", "b_spatial", "b_gate", "ln1_g", "ln1_b", "ln2_g", "ln2_b"]
SMALL_ROWS = 160
WEIGHTS = ["c_ctx", "ln0_g", "ln0_b", "w_ada", "b_ada", "w_in", "conv_w", "conv_b", "dt_bias", "a_log",
           "d_skip", "ssd_norm_g", "gm_norm_g", "gm_norm_b", "w_spatial", "b_spatial", "b_gate",
           "w_ssd_proj", "w_gm_proj", "w_out", "ln1_g", "ln1_b", "w_ff1", "w_ff3", "w_ff2", "ln2_g", "ln2_b"]


def kernel(x, c, ctx, c_ctx, ln0_g, ln0_b, w_ada, b_ada, w_in, conv_w, conv_b, dt_bias, a_log, d_skip, ssd_norm_g, gm_norm_g, gm_norm_b, w_spatial, b_spatial, b_gate, w_ssd_proj, w_gm_proj, w_out, ln1_g, ln1_b, w_ff1, w_ff3, w_ff2, ln2_g, ln2_b, loss_target, m_c_ctx, m_ln0_g, m_ln0_b, m_w_ada, m_b_ada, m_w_in, m_conv_w, m_conv_b, m_dt_bias, m_a_log, m_d_skip, m_ssd_norm_g, m_gm_norm_g, m_gm_norm_b, m_w_spatial, m_b_spatial, m_b_gate, m_w_ssd_proj, m_w_gm_proj, m_w_out, m_ln1_g, m_ln1_b, m_w_ff1, m_w_ff3, m_w_ff2, m_ln2_g, m_ln2_b, v_c_ctx, v_ln0_g, v_ln0_b, v_w_ada, v_b_ada, v_w_in, v_conv_w, v_conv_b, v_dt_bias, v_a_log, v_d_skip, v_ssd_norm_g, v_gm_norm_g, v_gm_norm_b, v_w_spatial, v_b_spatial, v_b_gate, v_w_ssd_proj, v_w_gm_proj, v_w_out, v_ln1_g, v_ln1_b, v_w_ff1, v_w_ff3, v_w_ff2, v_ln2_g, v_ln2_b):
    W = dict(c_ctx=c_ctx, ln0_g=ln0_g, ln0_b=ln0_b, w_ada=w_ada, b_ada=b_ada, w_in=w_in, conv_w=conv_w,
             conv_b=conv_b, dt_bias=dt_bias, a_log=a_log, d_skip=d_skip, ssd_norm_g=ssd_norm_g,
             gm_norm_g=gm_norm_g, gm_norm_b=gm_norm_b, w_spatial=w_spatial, b_spatial=b_spatial,
             b_gate=b_gate, w_ssd_proj=w_ssd_proj, w_gm_proj=w_gm_proj, w_out=w_out, ln1_g=ln1_g,
             ln1_b=ln1_b, w_ff1=w_ff1, w_ff3=w_ff3, w_ff2=w_ff2, ln2_g=ln2_g, ln2_b=ln2_b)
    M = dict(c_ctx=m_c_ctx, ln0_g=m_ln0_g, ln0_b=m_ln0_b, w_ada=m_w_ada, b_ada=m_b_ada, w_in=m_w_in,
             conv_w=m_conv_w, conv_b=m_conv_b, dt_bias=m_dt_bias, a_log=m_a_log, d_skip=m_d_skip,
             ssd_norm_g=m_ssd_norm_g, gm_norm_g=m_gm_norm_g, gm_norm_b=m_gm_norm_b,
             w_spatial=m_w_spatial, b_spatial=m_b_spatial, b_gate=m_b_gate, w_ssd_proj=m_w_ssd_proj,
             w_gm_proj=m_w_gm_proj, w_out=m_w_out, ln1_g=m_ln1_g, ln1_b=m_ln1_b, w_ff1=m_w_ff1,
             w_ff3=m_w_ff3, w_ff2=m_w_ff2, ln2_g=m_ln2_g, ln2_b=m_ln2_b)
    V = dict(c_ctx=v_c_ctx, ln0_g=v_ln0_g, ln0_b=v_ln0_b, w_ada=v_w_ada, b_ada=v_b_ada, w_in=v_w_in,
             conv_w=v_conv_w, conv_b=v_conv_b, dt_bias=v_dt_bias, a_log=v_a_log, d_skip=v_d_skip,
             ssd_norm_g=v_ssd_norm_g, gm_norm_g=v_gm_norm_g, gm_norm_b=v_gm_norm_b,
             w_spatial=v_w_spatial, b_spatial=v_b_spatial, b_gate=v_b_gate, w_ssd_proj=v_w_ssd_proj,
             w_gm_proj=v_w_gm_proj, w_out=v_w_out, ln1_g=v_ln1_g, ln1_b=v_ln1_b, w_ff1=v_w_ff1,
             w_ff3=v_w_ff3, w_ff2=v_w_ff2, ln2_g=v_ln2_g, ln2_b=v_ln2_b)

    me = 4 * lax.axis_index("x") + 2 * lax.axis_index("y") + lax.axis_index("c")
    xl, cx, tgt = x[0], ctx[0], loss_target[0]
    L = xl.shape[0]
    assert cx.shape[0] == TL and L % TL == 0
    ada_n = w_ada.shape[2]
    cw_n = conv_w.shape[2]

    small1 = _pad_rows(jnp.concatenate([c, _slab([conv_w[0]], 1)], axis=0), 8)
    g1 = _all_gather(small1, "ag_small")
    c_all = g1[:, 0, :]
    conv_w_full = g1[:, 1, :5 * cw_n].reshape(NDEV, 5, cw_n).transpose(1, 0, 2).reshape(5, NDEV * cw_n)
    sq = w_ssd_proj.shape[1]
    ffr = w_ff2.shape[1]
    ffc = w_ff1.shape[2]
    late = [jnp.concatenate([w_ssd_proj[0], w_gm_proj[0], w_out[0], w_ff2[0]], axis=0).astype(_MXU),
            _t(w_ff1[0]).astype(_MXU), _t(w_ff3[0]).astype(_MXU)]

    c16 = _pad_rows(jnp.concatenate([c_all, _row(c_ctx)], axis=0), 16)
    b_ada_sh = lax.dynamic_slice(b_ada, (0, ada_n * me), (1, ada_n))
    modp = _ada_fwd(c16, w_ada[0], b_ada_sh)
    mod16 = _all_gather(modp, "ag_mod").transpose(1, 0, 2).reshape(16, NDEV * ada_n)

    ga, = _all_gather_multi([w_in[0].astype(_MXU)], "ag_w_in")
    ga, late, mod16 = lax.optimization_barrier((ga, late, mod16))
    lw_send, lw_recv, lw_src, lw_land, lw_token = _exchange_start(late, "ag_late_start", gather=True)
    w_in_p = _perm_from_blocks(ga)
    modx = _pad_rows(lax.dynamic_slice(mod16, (me, 0), (1, 6 * D)).reshape(6, D), 8) + lw_token[0, 0]
    modc = _pad_rows(mod16[8].reshape(6, D), 8)

    g0, b0 = _row(ln0_g), _row(ln0_b)
    xn, h1 = _ln0_fwd(xl, cx, g0, b0, modx, modc)
    p = _mm(h1, w_in_p, "nn", F32, "mm_p")
    conv_w8 = _pad_rows(conv_w_full, 8)
    xbc = _conv_fwd(p, conv_w8, conv_b)
    prm = _pad_rows(jnp.pad(jnp.stack([dt_bias.reshape(32), a_log.reshape(32)]), ((0, 0), (0, 96))), 8)
    yf, yb, hpf, hpb = _ssd2_fwd(xbc, p, prm)
    lw_land = _exchange_wait(lw_send, lw_recv, lw_src, lw_land, yf, "ag_late_wait", gather=True)
    fw_send, fw_recv, lw_land, fw_token = _forward_start(lw_land, "ag_fwd_start")
    dsk = _row(jnp.repeat(d_skip[0, 0] + d_skip[0, 1], HP)) + fw_token[0:1, 0:1]
    ws_m = w_spatial[0].astype(_MXU)
    bsT = jnp.pad(b_spatial[0].T, ((0, 0), (0, 120)))
    mixp = (dsk, ssd_norm_g, gm_norm_g, gm_norm_b, ws_m, bsT)
    yssd, ygm = _mix_fwd(yf, yb, p, xbc, *mixp)
    gb, gc1, gc2 = _forward_wait(fw_send, fw_recv, lw_land, yssd, "ag_fwd_wait")

    gb, gc1, gc2 = lax.optimization_barrier(
        [lax.dynamic_update_index_in_dim(g, mine, me, 0) for g, mine in zip((gb, gc1, gc2), late)])
    w_ssd_f = gb[:, 0:sq].reshape(NDEV * sq, D)
    w_gm_f = gb[:, sq:2 * sq].reshape(NDEV * sq, D)
    w_out_f = gb[:, 2 * sq:3 * sq].reshape(NDEV * sq, D)
    w_ff2_f = gb[:, 3 * sq:3 * sq + ffr].reshape(NDEV * ffr, D)
    assert HFF == (NDEV // 2) * ffc
    hd = NDEV // 2
    w13i = jnp.concatenate([g[k] for t in range(2) for g in (gc1, gc2) for k in range(t * hd, (t + 1) * hd)],
                           axis=0)
    a1, a2, merged, out, r1, h2 = _merge_fwd(yssd, ygm, p, b_gate, w_ssd_f, w_gm_f, w_out_f,
                                             xn, modx, ln1_g, ln1_b)
    f13, ff = _mm_f13_glu(h2, w13i)

    dr2, do2, st2, loss_slab = _mm_o2_res2(ff, w_ff2_f, r1, tgt, modx, ln1_g, ln1_b, ln2_g, ln2_b)
    loss = lax.psum(loss_slab[0, 0], ("x", "y", "c"))
    df13 = _mm_dff_glu(do2, w_ff2_f, f13)
    dw_ff2 = _mm(ff, do2, "tn", _MXU, "mm_dw_ff2")
    dw13i = _mm(df13, h2, "tn", _MXU, "mm_dw13")

    def owner_blocks(first):
        return jnp.concatenate([dw13i[t * 2 * HFF + first:t * 2 * HFF + first + HFF].reshape(NDEV // 2, ffc, D)
                                for t in range(2)], axis=0)

    xff = [dw_ff2.reshape(NDEV, ffr, D), owner_blocks(0), owner_blocks(HFF)]
    ff_send, ff_recv, ff_src, ff_land, ff_token = _exchange_start(xff, "xchg_ff_start")
    modx = modx + ff_token[0, 0]
    dr1, dout, st1 = _mm_dh2_res1bwd(df13, w13i, dr2, r1, out, modx, ln1_g, ln1_b)
    dw_out = _mm(merged, dout, "tn", _MXU, "mm_dw_out")
    dp = _dp_fill(lax.empty((L + TL, NPJ), _MXU))
    dp, da1, da2, stg, dys, dym = _merge_bwd(dout, a1, a2, p, b_gate, w_out_f, w_ssd_f, w_gm_f, dp)
    dw_ssd = _mm(yssd, da1, "tn", _MXU, "mm_dw_ssd")
    dw_gm = _mm(ygm, da2, "tn", _MXU, "mm_dw_gm")
    xsq = [jnp.concatenate([dw_ssd.reshape(NDEV, sq, D), dw_gm.reshape(NDEV, sq, D),
                            dw_out.reshape(NDEV, sq, D)], axis=1)]
    sq_send, sq_recv, sq_src, sq_land, sq_token = _exchange_start(xsq, "xchg_sq_start")
    mixp = (dsk + sq_token[0:1, 0:1],) + mixp[1:]
    dp, dyd, stm, dws, dbsT = _mix_bwd(dys, dym, yf, yb, p, xbc, dp, *mixp)
    dxf, dxb, ddf, ddb, sts = _ssd2_bwd(xbc, p, prm, dsk, dyd, hpf, hpb)
    dp, dcw, dcb = _conv_bwd(dxf, dxb, p, conv_w8, conv_b, dp)
    dp, std = _dt_bwd(ddf, ddb, dp)
    hw = D // 2
    xin_a = [_blocks_from_perm(_mm(h1[:, :hw], dp, "tn", _MXU, "mm_dw_in_a"), w_in.shape[2])]
    ina_send, ina_recv, ina_src, ina_land, ina_token = _exchange_start(xin_a, "xchg_in_a_start")
    h1b, ina_token = lax.optimization_barrier((h1[:, hw:], ina_token))
    xin_b = [_blocks_from_perm(_mm(h1b, dp, "tn", _MXU, "mm_dw_in_b"), w_in.shape[2])]
    inb_send, inb_recv, inb_src, inb_land, inb_token = _exchange_start(xin_b, "xchg_in_b_start")
    dp, inb_token = lax.optimization_barrier((dp, inb_token))
    dh1 = _mm(dp, w_in_p, "nt", F32, "mm_dh1")
    modx = modx + (ina_token[0, 0] + inb_token[0, 0])
    grad_x, st0 = _ln0_bwd(dh1, dr1, xl, cx, g0, b0, modx, modc)

    zero = jnp.zeros((D,), F32)
    dmod = jnp.stack([jnp.concatenate([st0[0], st0[1], st1[4], st1[1], st1[0], st2[2]]),
                      jnp.concatenate([st0[2], st0[3], zero, zero, zero, zero])])
    g16 = _all_gather(_pad_rows(dmod, 8), "ag_dmod")[:, 0:2, :].reshape(16, 6 * D)
    g16_sh = lax.dynamic_slice(g16, (0, ada_n * me), (16, ada_n))
    c16b = jnp.stack([c_all, jnp.broadcast_to(_row(c_ctx), (NDEV, D))], axis=1).reshape(16, D)
    dw_ada, db_ada8, dcc8 = _ada_bwd(c16b, g16, g16_sh, w_ada[0])

    part = dict(
        c_ctx=dcc8[0], ln0_g=st0[4], ln0_b=st0[5], conv_w=dcw[0:5], conv_b=dcb[0],
        dt_bias=std[0, 0:32], a_log=sts[0, 0:32], d_skip=jnp.tile(sts[1, 0:16], 2),
        ssd_norm_g=stm[0], gm_norm_g=stm[1], gm_norm_b=stm[2], w_spatial=dws,
        b_spatial=dbsT[:, 0:8].T, b_gate=stg[0], ln1_g=st1[2], ln1_b=st1[3], ln2_g=st2[0], ln2_b=st2[1])
    pnames = list(part)
    psum8 = _sum8(_all_gather(_slab([part[n] for n in pnames], SMALL_ROWS), "ag_smallgrads"), "sum_smallgrads")
    small = dict(zip(pnames, _unslab(psum8, [part[n].shape for n in pnames])))
    grads = {n: small[n].reshape(W[n].shape) for n in pnames if n != "conv_w"}
    grads["conv_w"] = lax.dynamic_slice(small["conv_w"], (0, cw_n * me), (5, cw_n)).reshape(conv_w.shape)
    grads["b_ada"] = db_ada8[0:1]
    grads["w_ada"] = dw_ada.reshape(w_ada.shape)

    delta, new_m, new_v = {}, {}, {}

    def adam_group(names, rows, tag, align=0):
        shapes = [W[n].shape for n in names]
        outs = _adamw(*[_slab([src[n] for n in names], rows, align) for src in (grads, W, M, V)], tag)
        for res, slab in zip((delta, new_m, new_v), outs):
            for n, a in zip(names, _unslab(slab, shapes, align)):
                res[n] = a

    adam_group(REPL + ["conv_w"], SMALL_ROWS, "adamw_small")
    res = _adamw(grads["w_ada"][0], w_ada[0], m_w_ada[0], v_w_ada[0], "adamw_w_ada")
    delta["w_ada"], new_m["w_ada"], new_v["w_ada"] = [a[None] for a in res]

    rff = _exchange_wait(ff_send, ff_recv, ff_src, ff_land, st0, "xchg_ff_wait")
    rsq = _exchange_wait(sq_send, sq_recv, sq_src, sq_land, rff[0], "xchg_sq_wait")
    rin_a = _exchange_wait(ina_send, ina_recv, ina_src, ina_land, delta["ln2_b"], "xchg_in_a_wait")
    rin_b = _exchange_wait(inb_send, inb_recv, inb_src, inb_land, rin_a[0], "xchg_in_b_wait")
    rin = jnp.concatenate([rin_a[0], rin_b[0]], axis=1)

    def own(blocks):
        return lax.dynamic_index_in_dim(blocks, me, 0, keepdims=False)

    own_in = jnp.concatenate([own(xin_a[0]), own(xin_b[0])], axis=0)
    for n, r8, mine, row0, tr in (
            ("w_ff2", rff[0], own(xff[0]), 0, ffr // 2), ("w_ssd_proj", rsq[0], own(xsq[0]), 0, sq),
            ("w_gm_proj", rsq[0], own(xsq[0]), sq, sq), ("w_out", rsq[0], own(xsq[0]), 2 * sq, sq),
            ("w_in", rin, own_in, 0, 256)):
        res = _adamw_sum(r8, mine, W[n][0], M[n][0], V[n][0], row0, tr, "adamw_" + n)
        grads[n], delta[n], new_m[n], new_v[n] = [a[None] for a in res]
    for n, r8, mine in (("w_ff1", rff[1], own(xff[1])), ("w_ff3", rff[2], own(xff[2]))):
        res = _adamw_sum(r8, mine, _t(W[n][0]), _t(M[n][0]), _t(V[n][0]), 0, ffc // 2, "adamw_" + n)
        grads[n], delta[n], new_m[n], new_v[n] = [_t(a)[None] for a in res]

    return (loss, grad_x[None], *[grads[n] for n in WEIGHTS], *[delta[n] for n in WEIGHTS],
            *[new_m[n] for n in WEIGHTS], *[new_v[n] for n in WEIGHTS])
```

```python
import jax
import jax.numpy as jnp
from jax import lax
from jax.experimental import pallas as pl
from jax.experimental.pallas import tpu as pltpu

_MXU = jnp.bfloat16
F32 = jnp.float32
D = 1024
TL = 256
Q = 128
NH, HP, NS, HPG = 16, 64, 128, 8
DFF = 2816
ALPHA = 2.0 ** 0.25
EPS = 1e-5
OZ, OU, OV, OXS, OG, OB, OC, ODT, NPJ = 0, 1024, 2048, 3072, 4096, 6144, 6400, 6656, 6912
NNAT = 6688
NDEV = 8
ADAM_LR, ADAM_B1, ADAM_B2, ADAM_EPS, ADAM_WD, ADAM_STEP = 1e-3, 0.9, 0.999, 1e-8, 0.01, 10
VMEM_LIMIT = 48 * 1024 * 1024

NN = ((1,), (0,))
NT = ((1,), (1,))
TN = ((0,), (0,))
MESH = pl.DeviceIdType.MESH


def _dot(a, b, dims):
    return lax.dot_general(a.astype(_MXU), b.astype(_MXU), (dims, ((), ())),
                           preferred_element_type=F32)


def _tile(n, cands):
    for c in cands:
        if n % c == 0:
            return c
    return n


def _divisor_tile(n, cap, mult):
    best = n
    for t in range(mult, min(n, cap) + 1, mult):
        if n % t == 0:
            best = t
    return best


def _params(sem):
    return pltpu.CompilerParams(dimension_semantics=sem, vmem_limit_bytes=VMEM_LIMIT)


def _cst(shape):
    nd = len(shape)
    return pl.BlockSpec(shape, lambda *_: (0,) * nd)


def _rt(w, cb=0, rows=TL):
    return pl.BlockSpec((rows, w), lambda i: (i, cb))


def _rtc(w, nt, cb=0):
    return pl.BlockSpec((TL, w), lambda i: (jnp.minimum(i, nt - 1), cb))


def _sig(x):
    return jax.nn.sigmoid(x)


def _softplus(x):
    return jnp.maximum(x, 0.0) + jnp.log1p(jnp.exp(-jnp.abs(x)))


_G0, _G1 = 0.7978845608028654, 0.044715


def _gelu(x):
    t = jnp.tanh(_G0 * (x + _G1 * x * x * x))
    return 0.5 * x * (1.0 + t), t


def _gelu_grad(x, t):
    return 0.5 * (1.0 + t) + 0.5 * x * (1.0 - t * t) * _G0 * (1.0 + 3.0 * _G1 * x * x)


def _ln(r):
    mu = jnp.mean(r, axis=-1, keepdims=True)
    xc = r - mu
    var = jnp.mean(xc * xc, axis=-1, keepdims=True)
    rstd = lax.rsqrt(var + EPS)
    return xc * rstd, rstd


def _ln_bwd(dyh, xhat, rstd):
    return rstd * (dyh - jnp.mean(dyh, axis=-1, keepdims=True)
                   - xhat * jnp.mean(dyh * xhat, axis=-1, keepdims=True))


def _colsum(v):
    return jnp.sum(v, axis=0, keepdims=True)


def _cumsum_rows(a, rev):
    n = a.shape[0]
    row = lax.broadcasted_iota(jnp.int32, a.shape, 0)
    s = 1
    while s < n:
        if rev:
            a = a + jnp.where(row < n - s, pltpu.roll(a, n - s, 0), 0.0)
        else:
            a = a + jnp.where(row >= s, pltpu.roll(a, s, 0), 0.0)
        s *= 2
    return a


def _mm(a, b, mode, out_dtype, name):
    if mode == "tn":
        K, M = a.shape
    else:
        M, K = a.shape
    N = b.shape[0] if mode == "nt" else b.shape[1]
    tm = _divisor_tile(M, 1408, 128) if mode == "tn" else _divisor_tile(M, 1088, 16)
    tn = _divisor_tile(N, 1408, 128)
    tk = _divisor_tile(K, 2304, 128)
    nk = K // tk
    dims = {"nn": NN, "nt": NT, "tn": TN}[mode]
    use_acc = nk > 1 and out_dtype != F32

    def body(a_ref, b_ref, o_ref, *acc):
        prod = _dot(a_ref[...], b_ref[...], dims)
        if nk == 1:
            o_ref[...] = prod.astype(o_ref.dtype)
            return
        acc_ref = acc[0] if use_acc else o_ref
        k = pl.program_id(2)

        @pl.when(k == 0)
        def _():
            acc_ref[...] = prod

        if use_acc:
            @pl.when((k > 0) & (k < nk - 1))
            def _():
                acc_ref[...] += prod

            @pl.when(k == nk - 1)
            def _():
                o_ref[...] = (acc_ref[...] + prod).astype(o_ref.dtype)
        else:
            @pl.when(k > 0)
            def _():
                o_ref[...] += prod

    if mode == "tn":
        a_spec = pl.BlockSpec((tk, tm), lambda i, j, k: (k, i))
    else:
        a_spec = pl.BlockSpec((tm, tk), lambda i, j, k: (i, k))
    if mode == "nt":
        b_spec = pl.BlockSpec((tn, tk), lambda i, j, k: (j, k))
    else:
        b_spec = pl.BlockSpec((tk, tn), lambda i, j, k: (k, j))
    return pl.pallas_call(
        body, name=name, grid=(M // tm, N // tn, nk),
        in_specs=[a_spec, b_spec],
        out_specs=pl.BlockSpec((tm, tn), lambda i, j, k: (i, j)),
        out_shape=jax.ShapeDtypeStruct((M, N), out_dtype),
        scratch_shapes=[pltpu.VMEM((tm, tn), F32)] if use_acc else [],
        compiler_params=_params(("parallel", "parallel", "arbitrary")),
    )(a, b)


def _all_gather(x, name):
    def body(x_ref, out_ref, send_sems, recv_sems, local_sem):
        mx, my, mc = lax.axis_index("x"), lax.axis_index("y"), lax.axis_index("c")
        me, sibling = (mx, my, mc), (mx, my, 1 - mc)
        chips = [(1 - mx, my), (mx, 1 - my), (1 - mx, 1 - my)]

        def slot(px, py, pc):
            return out_ref.at[4 * px + 2 * py + pc]

        def copy(k, block, to, src=None):
            return pltpu.make_async_remote_copy(
                src_ref=slot(*block) if src is None else src, dst_ref=slot(*block),
                send_sem=send_sems.at[k], recv_sem=recv_sems.at[k],
                device_id=to, device_id_type=MESH)

        mine = pltpu.make_async_copy(x_ref, slot(*me), local_sem)
        mine.start()
        first = [copy(0, me, sibling, src=x_ref)]
        first += [copy(1 + j, me, (*chip, mc), src=x_ref) for j, chip in enumerate(chips)]
        for cp in first:
            cp.start()
        passed = [copy(4 + j, (*chip, mc), sibling) for j, chip in enumerate(chips)]
        for j, chip in enumerate(chips):
            copy(1 + j, (*chip, mc), me).wait_recv()
            passed[j].start()
        copy(0, sibling, me).wait_recv()
        for j, chip in enumerate(chips):
            copy(4 + j, (*chip, 1 - mc), me).wait_recv()
        for cp in first + passed:
            cp.wait_send()
        mine.wait()

    return pl.pallas_call(
        body, name=name,
        out_shape=jax.ShapeDtypeStruct((NDEV,) + x.shape, x.dtype),
        in_specs=[pl.BlockSpec(memory_space=pl.ANY)],
        out_specs=pl.BlockSpec(memory_space=pl.ANY),
        scratch_shapes=[pltpu.SemaphoreType.DMA((7,)), pltpu.SemaphoreType.DMA((7,)),
                        pltpu.SemaphoreType.DMA],
    )(x)


def _any_specs(n):
    return [pl.BlockSpec(memory_space=pl.ANY)] * n


def _all_gather_multi(xs, name):
    na = len(xs)

    def body(*refs):
        x_refs, out_refs = refs[:na], refs[na:2 * na]
        send_sems, recv_sems, local_sems = refs[2 * na:]
        mx, my, mc = lax.axis_index("x"), lax.axis_index("y"), lax.axis_index("c")
        me, sibling = (mx, my, mc), (mx, my, 1 - mc)
        chips = [(1 - mx, my), (mx, 1 - my), (1 - mx, 1 - my)]

        def copy(a, k, block, to, src=None):
            slot = out_refs[a].at[4 * block[0] + 2 * block[1] + block[2]]
            return pltpu.make_async_remote_copy(
                src_ref=slot if src is None else src, dst_ref=slot,
                send_sem=send_sems.at[7 * a + k], recv_sem=recv_sems.at[7 * a + k],
                device_id=to, device_id_type=MESH)

        mine = [pltpu.make_async_copy(x_refs[a], out_refs[a].at[4 * mx + 2 * my + mc], local_sems.at[a])
                for a in range(na)]
        for cp in mine:
            cp.start()
        first = []
        for a in range(na):
            first.append(copy(a, 0, me, sibling, src=x_refs[a]))
            first += [copy(a, 1 + j, me, (*chip, mc), src=x_refs[a]) for j, chip in enumerate(chips)]
        for cp in first:
            cp.start()
        passed = []
        for a in range(na):
            for j, chip in enumerate(chips):
                copy(a, 1 + j, (*chip, mc), me).wait_recv()
                fwd = copy(a, 4 + j, (*chip, mc), sibling)
                fwd.start()
                passed.append(fwd)
        for a in range(na):
            copy(a, 0, sibling, me).wait_recv()
            for j, chip in enumerate(chips):
                copy(a, 4 + j, (*chip, 1 - mc), me).wait_recv()
        for cp in first + passed:
            cp.wait_send()
        for cp in mine:
            cp.wait()

    return pl.pallas_call(
        body, name=name,
        out_shape=[jax.ShapeDtypeStruct((NDEV,) + x.shape, x.dtype) for x in xs],
        in_specs=_any_specs(na), out_specs=_any_specs(na),
        scratch_shapes=[pltpu.SemaphoreType.DMA((7 * na,)), pltpu.SemaphoreType.DMA((7 * na,)),
                        pltpu.SemaphoreType.DMA((na,))],
    )(*xs)


def _adamw_sum(r8, own, w, m, v, row0, tr, name):
    R, C = w.shape
    assert row0 % tr == 0
    blk0 = row0 // tr
    bc1 = 1.0 - ADAM_B1 ** ADAM_STEP
    bc2 = 1.0 - ADAM_B2 ** ADAM_STEP

    def body(r_ref, *refs):
        if own is None:
            gg = r_ref[0].astype(F32)
        else:
            gg = refs[0][...].astype(F32)
            refs = refs[1:]
        w_ref, m_ref, v_ref, g_ref, d_ref, mo_ref, vo_ref = refs
        for k in range(1, NDEV):
            gg = gg + r_ref[k].astype(F32)
        mn = ADAM_B1 * m_ref[...] + (1.0 - ADAM_B1) * gg
        vn = ADAM_B2 * v_ref[...] + (1.0 - ADAM_B2) * (gg * gg)
        mh = mn / bc1
        vh = vn / bc2
        g_ref[...] = gg
        d_ref[...] = -ADAM_LR * (mh / (jnp.sqrt(vh) + ADAM_EPS) + ADAM_WD * w_ref[...])
        mo_ref[...] = mn
        vo_ref[...] = vn

    spec = pl.BlockSpec((tr, C), lambda i: (i, 0))
    sh = jax.ShapeDtypeStruct((R, C), F32)
    own_ops = [] if own is None else [own]
    own_specs = [] if own is None else [pl.BlockSpec((tr, C), lambda i: (i + blk0, 0))]
    return pl.pallas_call(
        body, name=name, grid=(R // tr,),
        in_specs=[pl.BlockSpec((NDEV, tr, C), lambda i: (0, i + blk0, 0))] + own_specs + [spec, spec, spec],
        out_specs=[spec] * 4, out_shape=[sh] * 4, compiler_params=_params(("parallel",)),
    )(r8, *own_ops, w, m, v)


_HBM = pl.BlockSpec(memory_space=pltpu.HBM)
_SEM = pl.BlockSpec(memory_space=pltpu.SEMAPHORE)
_EFFECT = pltpu.SideEffectType.DATAFLOW_SIDE_EFFECTING


def _exchange_copies(g_refs, land_refs, send_sems, recv_sems, gather):
    mx, my, mc = lax.axis_index("x"), lax.axis_index("y"), lax.axis_index("c")
    copies = []
    for a in range(len(g_refs)):
        for f in ((1, 2, 4, 6) if gather else range(1, NDEV)):
            px = 1 - mx if (f >> 2) & 1 else mx
            py = 1 - my if (f >> 1) & 1 else my
            pc = 1 - mc if f & 1 else mc
            src = g_refs[a] if gather else g_refs[a].at[4 * px + 2 * py + pc]
            dst = land_refs[a].at[4 * mx + 2 * my + mc] if gather else land_refs[a].at[f]
            copies.append(pltpu.make_async_remote_copy(
                src_ref=src, dst_ref=dst,
                send_sem=send_sems.at[7 * a + f - 1], recv_sem=recv_sems.at[7 * a + f - 1],
                device_id=(px, py, pc), device_id_type=MESH))
    return copies


def _exchange_start(gs, name, gather=False):
    na = len(gs)

    def body(*refs):
        for cp in _exchange_copies(refs[:na], refs[na:2 * na], refs[2 * na], refs[2 * na + 1], gather):
            cp.start()
        refs[-1][...] = jnp.zeros_like(refs[-1])

    hbm = [pltpu.HBM(g.shape, g.dtype) for g in gs]
    land_shapes = [((NDEV,) + g.shape) if gather else g.shape for g in gs]
    lands = [pltpu.with_memory_space_constraint(lax.empty(shp, g.dtype), pltpu.HBM)
             for shp, g in zip(land_shapes, gs)]
    hbm_land = [pltpu.HBM(shp, g.dtype) for shp, g in zip(land_shapes, gs)]
    outs = pl.pallas_call(
        body, name=name,
        out_shape=(pltpu.SemaphoreType.DMA((7 * na,)), pltpu.SemaphoreType.DMA((7 * na,)), *hbm, *hbm_land,
                   jax.ShapeDtypeStruct((8, 128), F32)),
        in_specs=[_HBM] * (2 * na),
        out_specs=(_SEM, _SEM, *([_HBM] * (2 * na)), pl.BlockSpec(memory_space=pltpu.VMEM)),
        input_output_aliases={i: 2 + i for i in range(2 * na)},
        compiler_params=pltpu.CompilerParams(has_side_effects=_EFFECT),
    )(*[pltpu.with_memory_space_constraint(g, pltpu.HBM) for g in gs], *lands)
    return outs[0], outs[1], outs[2:2 + na], outs[2 + na:2 + 2 * na], outs[-1]


def _forward_copies(land_refs, send_sems, recv_sems):
    mx, my, mc = lax.axis_index("x"), lax.axis_index("y"), lax.axis_index("c")
    copies = []
    for a in range(len(land_refs)):
        for j, (fx, fy) in enumerate(((0, 1), (1, 0), (1, 1))):
            px = 1 - mx if fx else mx
            py = 1 - my if fy else my
            blk = land_refs[a].at[4 * px + 2 * py + mc]
            copies.append(pltpu.make_async_remote_copy(
                src_ref=blk, dst_ref=blk, send_sem=send_sems.at[3 * a + j], recv_sem=recv_sems.at[3 * a + j],
                device_id=(mx, my, 1 - mc), device_id_type=MESH))
    return copies


def _forward_start(lands, name):
    na = len(lands)

    def body(*refs):
        for cp in _forward_copies(refs[:na], refs[na], refs[na + 1]):
            cp.start()
        refs[-1][...] = jnp.zeros_like(refs[-1])

    outs = pl.pallas_call(
        body, name=name,
        out_shape=(pltpu.SemaphoreType.DMA((3 * na,)), pltpu.SemaphoreType.DMA((3 * na,)),
                   *[pltpu.HBM(g.shape, g.dtype) for g in lands], jax.ShapeDtypeStruct((8, 128), F32)),
        in_specs=[_HBM] * na,
        out_specs=(_SEM, _SEM, *([_HBM] * na), pl.BlockSpec(memory_space=pltpu.VMEM)),
        input_output_aliases={i: 2 + i for i in range(na)},
        compiler_params=pltpu.CompilerParams(has_side_effects=_EFFECT),
    )(*lands)
    return outs[0], outs[1], outs[2:2 + na], outs[-1]


def _forward_wait(send_sems, recv_sems, lands, after, name):
    na = len(lands)

    def body(*refs):
        for cp in _forward_copies(refs[:na], refs[na], refs[na + 1]):
            cp.wait_send()
            cp.wait_recv()

    return pl.pallas_call(
        body, name=name,
        out_shape=tuple(pltpu.HBM(g.shape, g.dtype) for g in lands),
        in_specs=[_HBM] * na + [_SEM, _SEM, pl.BlockSpec(memory_space=pl.ANY)],
        out_specs=tuple([_HBM] * na),
        input_output_aliases={i: i for i in range(na)},
        compiler_params=pltpu.CompilerParams(has_side_effects=_EFFECT),
    )(*lands, send_sems, recv_sems, after)


def _exchange_wait(send_sems, recv_sems, g_thru, land_thru, after, name, gather=False):
    na = len(g_thru)

    def body(*refs):
        for cp in _exchange_copies(refs[:na], refs[na:2 * na], refs[2 * na], refs[2 * na + 1], gather):
            cp.wait_send()
            cp.wait_recv()

    outs = pl.pallas_call(
        body, name=name,
        out_shape=tuple(pltpu.HBM(g.shape, g.dtype) for g in list(g_thru) + list(land_thru)),
        in_specs=[_HBM] * (2 * na) + [_SEM, _SEM, pl.BlockSpec(memory_space=pl.ANY)],
        out_specs=tuple([_HBM] * (2 * na)),
        input_output_aliases={i: i for i in range(2 * na)},
        compiler_params=pltpu.CompilerParams(has_side_effects=_EFFECT),
    )(*g_thru, *land_thru, send_sems, recv_sems, after)
    return outs[na:]


def _sum8(r, name):
    _, R, C = r.shape
    tr = _tile(R, (256, 160, 128, 64, 32, 16, 8))

    def body(r_ref, o_ref):
        acc = r_ref[0].astype(F32)
        for k in range(1, NDEV):
            acc = acc + r_ref[k].astype(F32)
        o_ref[...] = acc

    return pl.pallas_call(
        body, name=name, grid=(R // tr,),
        in_specs=[pl.BlockSpec((NDEV, tr, C), lambda i: (0, i, 0))],
        out_specs=pl.BlockSpec((tr, C), lambda i: (i, 0)),
        out_shape=jax.ShapeDtypeStruct((R, C), F32),
        compiler_params=_params(("parallel",)),
    )(r)


def _adamw(g, w, m, v, name):
    R, C = g.shape
    tr = _tile(R, (256, 160, 128, 64, 32, 16, 8))
    bc1 = 1.0 - ADAM_B1 ** ADAM_STEP
    bc2 = 1.0 - ADAM_B2 ** ADAM_STEP

    def body(g_ref, w_ref, m_ref, v_ref, d_ref, mo_ref, vo_ref):
        gg = g_ref[...]
        mn = ADAM_B1 * m_ref[...] + (1.0 - ADAM_B1) * gg
        vn = ADAM_B2 * v_ref[...] + (1.0 - ADAM_B2) * (gg * gg)
        mh = mn / bc1
        vh = vn / bc2
        d_ref[...] = -ADAM_LR * (mh / (jnp.sqrt(vh) + ADAM_EPS) + ADAM_WD * w_ref[...])
        mo_ref[...] = mn
        vo_ref[...] = vn

    spec = pl.BlockSpec((tr, C), lambda i: (i, 0))
    sh = jax.ShapeDtypeStruct((R, C), F32)
    return pl.pallas_call(
        body, name=name, grid=(R // tr,), in_specs=[spec] * 4, out_specs=[spec] * 3,
        out_shape=[sh] * 3, compiler_params=_params(("parallel",)),
    )(g, w, m, v)


def _ada_fwd(c16, w_sh, b_sh):
    def body(c_ref, w_ref, b_ref, o_ref):
        c = c_ref[...]
        o_ref[...] = _dot(c * _sig(c), w_ref[...], NN) + b_ref[...]

    return pl.pallas_call(
        body, name="ada_fwd", out_shape=jax.ShapeDtypeStruct((16, w_sh.shape[1]), F32),
        compiler_params=pltpu.CompilerParams(vmem_limit_bytes=VMEM_LIMIT),
    )(c16, w_sh, b_sh)


def _ada_bwd(c16, g16, g16_sh, w_sh):
    ncol = w_sh.shape[1]

    def body(c_ref, g_ref, gs_ref, w_ref, dw_ref, db_ref, dc_ref):
        c = c_ref[...]
        s = _sig(c)
        gs = gs_ref[...]
        dw_ref[...] = _dot(c * s, gs, TN)
        db_ref[...] = jnp.broadcast_to(_colsum(g_ref[...]), db_ref.shape)
        odd = lax.broadcasted_iota(jnp.int32, gs.shape, 0) % 2 == 1
        gc = _colsum(jnp.where(odd, gs, 0.0))
        ds = _dot(jnp.broadcast_to(gc, (8, ncol)), w_ref[...], NT)
        c1 = c[1:2, :]
        s1 = s[1:2, :]
        dc_ref[...] = ds * (s1 * (1.0 + c1 * (1.0 - s1)))

    return pl.pallas_call(
        body, name="ada_bwd",
        out_shape=[jax.ShapeDtypeStruct(w_sh.shape, F32),
                   jax.ShapeDtypeStruct((8, g16.shape[1]), F32),
                   jax.ShapeDtypeStruct((8, D), F32)],
        compiler_params=pltpu.CompilerParams(vmem_limit_bytes=VMEM_LIMIT),
    )(c16, g16, g16_sh, w_sh)


def _ln0_fwd(x, ctx, g, b, modx, modc):
    L = x.shape[0]
    nt = L // TL

    def body(x_ref, c_ref, g_ref, b_ref, mx_ref, mc_ref, xn_ref, h_ref):
        isc = pl.program_id(0) == nt
        xin = jnp.where(isc, c_ref[...], x_ref[...])
        sh = jnp.where(isc, mc_ref[0:1, :], mx_ref[0:1, :])
        sc = jnp.where(isc, mc_ref[1:2, :], mx_ref[1:2, :])
        xhat, _ = _ln(xin)
        xn = xhat * g_ref[...] + b_ref[...]
        xn_ref[...] = xn
        h_ref[...] = (xn * (1.0 + sc) + sh).astype(h_ref.dtype)

    return pl.pallas_call(
        body, name="ln0_fwd", grid=(nt + 1,),
        in_specs=[_rtc(D, nt), _cst((TL, D)), _cst((1, D)), _cst((1, D)), _cst((8, D)), _cst((8, D))],
        out_specs=[_rt(D), _rt(D)],
        out_shape=[jax.ShapeDtypeStruct((L + TL, D), F32), jax.ShapeDtypeStruct((L + TL, D), _MXU)],
        compiler_params=_params(("parallel",)),
    )(x, ctx, g, b, modx, modc)


def _xbc_colblk(j):
    return jnp.where(j < 8, OXS // 128 + j, OB // 128 + j - 8)


def _conv_taps(p_ref, r0, first, last):
    main = p_ref[pl.ds(r0, TL), :]
    zero = jnp.zeros((8, main.shape[1]), F32)
    prev = zero if first else p_ref[pl.ds(r0 - 8, 8), :]
    nxt = zero if last else p_ref[pl.ds(r0 + TL, 8), :]
    ext = jnp.concatenate([prev, main, nxt], axis=0)
    n = TL + 16
    return [pltpu.roll(ext, (2 - k) % n, 0)[8:8 + TL] for k in range(5)]


def _seq_chunks(L):
    nt = L // TL
    return [(r * TL, r == 0, r == nt - 1) for r in range(nt)] + [(L, True, True)]


def _conv_fwd(p, conv_w8, conv_b):
    RT = p.shape[0]
    L = RT - TL
    chunks = _seq_chunks(L)

    def body(p_ref, w_ref, b_ref, o_ref):
        w = w_ref[...]
        bias = b_ref[...]
        for r0, first, last in chunks:
            taps = _conv_taps(p_ref, r0, first, last)
            pre = bias + sum(w[k:k + 1, :] * taps[k] for k in range(5))
            o_ref[pl.ds(r0, TL), :] = pre * _sig(pre)

    return pl.pallas_call(
        body, name="conv_fwd", grid=(12,),
        in_specs=[pl.BlockSpec((RT, 128), lambda j: (0, _xbc_colblk(j))),
                  pl.BlockSpec((8, 128), lambda j: (0, j)),
                  pl.BlockSpec((1, 128), lambda j: (0, j))],
        out_specs=pl.BlockSpec((RT, 128), lambda j: (0, j)),
        out_shape=jax.ShapeDtypeStruct((RT, 1536), F32),
        compiler_params=_params(("parallel",)),
    )(p, conv_w8, conv_b)


def _ssd_common(dtraw, dtb, a32, rev):
    dt = _softplus(dtraw + dtb)
    acum = _cumsum_rows(dt * a32, rev)
    ii = lax.broadcasted_iota(jnp.int32, (Q, Q), 0)
    jj = lax.broadcasted_iota(jnp.int32, (Q, Q), 1)
    mask = (ii <= jj) if rev else (ii >= jj)
    return dt, acum, acum.T, dt.T, mask


def _ssd_orders(ncl, ncc):
    nc = ncl + ncc

    def cf(s):
        return jnp.where(s < ncc, ncl + s, s - ncc)

    def cb(s):
        return nc - 1 - s

    return cf, cb


def _lane_bcast(v, ln):
    return jnp.broadcast_to(v[:, ln:ln + 1], v.shape)


def _halves(v, lo, axis):
    return jnp.concatenate([jnp.where(lo, v, 0.0), jnp.where(lo, 0.0, v)], axis=axis)


def _ssd2_fwd(xbc, p, prm):
    RT = xbc.shape[0]
    nc = RT // Q
    ncc = TL // Q
    cf, cb = _ssd_orders(nc - ncc, ncc)

    def one_dir(x_ref, dt_ref, prm_ref, y_ref, hp_ref, HT_ref, d):
        rev = d == 1
        a32 = -jnp.exp(prm_ref[1:2, :])
        dt, acum, acumT, dtT, mask = _ssd_common(dt_ref[...], prm_ref[0:1, :], a32, rev)
        end = 0 if rev else Q - 1
        lo = lax.broadcasted_iota(jnp.int32, (Q, 128), 1) < HP
        for g in range(2):
            Bg = x_ref[:, D + g * NS:D + (g + 1) * NS]
            Cg = x_ref[:, D + 2 * NS + g * NS:D + 2 * NS + (g + 1) * NS]
            CB = _dot(Cg, Bg, NT)
            xds, svs = [], []
            for q in range(HPG // 2):
                pi = g * (HPG // 2) + q
                ps = slice(pi * 128, (pi + 1) * 128)
                Xp = x_ref[:, ps]
                HTp = HT_ref[:, ps]
                lhs, dcs, sv = [], [], []
                ces = []
                for h in (2 * pi, 2 * pi + 1):
                    ln = 16 * d + h
                    colB = _lane_bcast(acum, ln)
                    rowv = acumT[ln:ln + 1, :]
                    aend = colB[end:end + 1, :]
                    Lm = jnp.exp(jnp.where(mask, colB - rowv, -1e30))
                    lhs.append(CB * Lm * dtT[ln:ln + 1, :])
                    ces.append(Cg * jnp.exp(colB))
                    dcs.append(jnp.exp(aend - colB) * _lane_bcast(dt, ln))
                    sv.append(jnp.exp(aend))
                lhs = jnp.concatenate(lhs + ces, axis=1)
                rhs = jnp.concatenate([_halves(Xp, lo, 0), _halves(HTp, lo, 0)], axis=0)
                y_ref[:, ps] = _dot(lhs, rhs, NN)
                xds.append(Xp * jnp.where(lo, dcs[0], dcs[1]))
                svs.append(jnp.where(lo[0:1, :], sv[0], sv[1]))
            gs = slice(g * 512, (g + 1) * 512)
            HTg = HT_ref[:, gs]
            hp_ref[0, :, gs] = HTg
            st = _dot(Bg.T, jnp.concatenate(xds, axis=1), NN)
            HT_ref[:, gs] = jnp.concatenate(svs, axis=1) * HTg + st

    def body(xf_ref, xb_ref, df_ref, db_ref, prm_ref, yf_ref, yb_ref, hf_ref, hb_ref, Hf, Hb):
        @pl.when(pl.program_id(0) == 0)
        def _():
            Hf[...] = jnp.zeros_like(Hf)
            Hb[...] = jnp.zeros_like(Hb)

        one_dir(xf_ref, df_ref, prm_ref, yf_ref, hf_ref, Hf, 0)
        one_dir(xb_ref, db_ref, prm_ref, yb_ref, hb_ref, Hb, 1)

    ysh = jax.ShapeDtypeStruct((RT, D), F32)
    hsh = jax.ShapeDtypeStruct((nc, NS, NH * HP), F32)
    hspec = pl.BlockSpec((1, NS, NH * HP), lambda s: (s, 0, 0))
    return pl.pallas_call(
        body, name="ssd_fwd", grid=(nc,),
        in_specs=[pl.BlockSpec((Q, 1536), lambda s: (cf(s), 0)),
                  pl.BlockSpec((Q, 1536), lambda s: (cb(s), 0)),
                  pl.BlockSpec((Q, 128), lambda s: (cf(s), ODT // 128)),
                  pl.BlockSpec((Q, 128), lambda s: (cb(s), ODT // 128)),
                  _cst((8, 128))],
        out_specs=[pl.BlockSpec((Q, D), lambda s: (cf(s), 0)),
                   pl.BlockSpec((Q, D), lambda s: (cb(s), 0)), hspec, hspec],
        out_shape=[ysh, ysh, hsh, hsh],
        scratch_shapes=[pltpu.VMEM((NS, NH * HP), F32), pltpu.VMEM((NS, NH * HP), F32)],
        compiler_params=_params(("arbitrary",)),
    )(xbc, xbc, p, p, prm)


def _ssd2_bwd(xbc, p, prm, dsk, dyd, hpf, hpb):
    RT = xbc.shape[0]
    nc = RT // Q
    ncc = TL // Q
    ncl = nc - ncc
    cf, cb = _ssd_orders(ncl, ncc)

    def rs(t):
        return nc - 1 - t

    def one_dir(x_ref, dt_ref, prm_ref, dsk_ref, dy_ref, is_ctx, hp_ref, dHT_ref,
                dx_ref, ddt_ref, st_ref, d):
        rev = d == 1
        a32 = -jnp.exp(prm_ref[1:2, :])
        dtraw = dt_ref[...]
        dtb = prm_ref[0:1, :]
        dt, acum, acumT, _, _ = _ssd_common(dtraw, dtb, a32, rev)
        end = 0 if rev else Q - 1
        lane = lax.broadcasted_iota(jnp.int32, (Q, 128), 1)
        srow = lax.broadcasted_iota(jnp.int32, (Q, 128), 0)
        maskT = (lane <= srow) if rev else (lane >= srow)
        lo = lane < HP
        lo1 = lo[0:1, :]
        dyscale = jnp.where(is_ctx, 0.0, 1.0)
        c_dacum = jnp.zeros((Q, 128), F32)
        r_dacum = jnp.zeros((Q, 128), F32)
        c_ddt = jnp.zeros((Q, 128), F32)
        dskacc = jnp.zeros((1, 128), F32)
        for g in range(2):
            gs = slice(g * 512, (g + 1) * 512)
            Bg = x_ref[:, D + g * NS:D + (g + 1) * NS]
            Cg = x_ref[:, D + 2 * NS + g * NS:D + 2 * NS + (g + 1) * NS]
            CBT = _dot(Bg, Cg, NT)
            HTg = hp_ref[0, :, gs]
            dHTg = dHT_ref[:, gs]
            BdHg = _dot(Bg, dHTg, NN)
            dCBT = jnp.zeros((Q, Q), F32)
            dCg = jnp.zeros((Q, NS), F32)
            xds, dyes, svs = [], [], []
            for q in range(HPG // 2):
                pi = g * (HPG // 2) + q
                ps = slice(pi * 128, (pi + 1) * 128)
                qs = slice(q * 128, (q + 1) * 128)
                Xp = x_ref[:, ps]
                dYp = dy_ref[:, ps] * dyscale
                HTp = HTg[:, qs]
                BdHp = BdHg[:, qs]
                dY2 = _halves(dYp, lo, 0)
                dWT2 = _dot(_halves(Xp, lo, 0), dYp.T, NN)
                G2 = _dot(dY2, HTp, NT)
                XB = Xp * BdHp
                hh = _colsum(dHTg[:, qs] * HTp)
                yx = _colsum(dYp * Xp)
                wts, dcs, ebs, sv = [], [], [], []
                for k, h in enumerate((2 * pi, 2 * pi + 1)):
                    ln = 16 * d + h
                    half = lo if k == 0 else jnp.logical_not(lo)
                    half1 = half[0:1, :]
                    colB = _lane_bcast(acum, ln)
                    dtcB = _lane_bcast(dt, ln)
                    rowv = acumT[ln:ln + 1, :]
                    aend = colB[end:end + 1, :]
                    LmT = jnp.exp(jnp.where(maskT, rowv - colB, -1e30))
                    WT = CBT * LmT * dtcB
                    dWT = dWT2[k * Q:(k + 1) * Q, :]
                    U = dWT * LmT
                    MT = U * CBT
                    rM = jnp.sum(MT, axis=1, keepdims=True)
                    rT = _colsum(MT * dtcB)
                    dCBT = dCBT + U * dtcB
                    ecol = jnp.exp(aend - colB)
                    EB = jnp.exp(colB)
                    Gk = G2[k * Q:(k + 1) * Q, :]
                    dCg = dCg + EB * Gk
                    qcol = jnp.sum(EB * Gk * Cg, axis=1, keepdims=True)
                    xb = jnp.sum(jnp.where(half, XB, 0.0), axis=1, keepdims=True)
                    e1 = ecol[:, 0:1]
                    dt1 = dtcB[:, 0:1]
                    scol = e1 * dt1 * xb
                    sA = jnp.exp(aend)
                    eterm = sA[:, 0:1] * jnp.sum(jnp.where(half1, hh, 0.0), axis=1, keepdims=True) \
                        + _colsum(scol)
                    cvec = qcol - dt1 * rM - scol + jnp.where(srow[:, 0:1] == end, eterm, 0.0)
                    c_dacum = jnp.where(lane == ln, cvec, c_dacum)
                    r_dacum = jnp.where(srow == ln, rT, r_dacum)
                    c_ddt = jnp.where(lane == ln, rM + e1 * xb, c_ddt)
                    if d == 0:
                        dskacc = dskacc + jnp.where(
                            lane[0:1, :] == h, jnp.sum(jnp.where(half1, yx, 0.0), axis=1, keepdims=True), 0.0)
                    wts.append(WT)
                    dcs.append(ecol * dtcB)
                    ebs.append(EB)
                    sv.append(sA)
                dcp = jnp.where(lo, dcs[0], dcs[1])
                dX = _dot(jnp.concatenate(wts, axis=1), dY2, NN) + dcp * BdHp
                if d == 0:
                    dX = dX + dYp * dsk_ref[:, ps]
                dx_ref[:, ps] = dX
                xds.append(Xp * dcp)
                dyes.append(dYp * jnp.where(lo, ebs[0], ebs[1]))
                svs.append(jnp.where(lo1, sv[0], sv[1]))
            dx_ref[:, D + g * NS:D + (g + 1) * NS] = (
                _dot(jnp.concatenate(xds, axis=1), dHTg, NT) + _dot(dCBT, Cg, NN))
            dx_ref[:, D + 2 * NS + g * NS:D + 2 * NS + (g + 1) * NS] = dCg + _dot(dCBT, Bg, TN)
            dHT_ref[:, gs] = (jnp.concatenate(svs, axis=1) * dHTg
                              + _dot(Cg.T, jnp.concatenate(dyes, axis=1), NN))
        dacum = c_dacum + r_dacum.T
        da = _cumsum_rows(dacum, not rev)
        mine = (lane >= 16 * d) & (lane < 16 * d + 16)
        ddt = jnp.where(mine, c_ddt + da * a32, 0.0)
        ddt_ref[...] = ddt * _sig(dtraw + dtb)
        st_ref[0:1, :] += _colsum(jnp.where(mine, da * dt, 0.0))
        if d == 0:
            st_ref[1:2, :] += dskacc

    def body(xf_ref, xb_ref, df_ref, db_ref, prm_ref, dsk_ref, dyf_ref, dyb_ref, hf_ref, hb_ref,
             dxf_ref, dxb_ref, ddf_ref, ddb_ref, st_ref, dHf, dHb):
        t = pl.program_id(0)

        @pl.when(t == 0)
        def _():
            dHf[...] = jnp.zeros_like(dHf)
            dHb[...] = jnp.zeros_like(dHb)
            st_ref[...] = jnp.zeros_like(st_ref)

        s = rs(t)
        one_dir(xf_ref, df_ref, prm_ref, dsk_ref, dyf_ref, cf(s) >= ncl, hf_ref, dHf,
                dxf_ref, ddf_ref, st_ref, 0)
        one_dir(xb_ref, db_ref, prm_ref, dsk_ref, dyb_ref, cb(s) >= ncl, hb_ref, dHb,
                dxb_ref, ddb_ref, st_ref, 1)

        @pl.when(t == nc - 1)
        def _():
            st_ref[0:1, :] = -jnp.exp(prm_ref[1:2, :]) * st_ref[0:1, :]

    def lat(c):
        return jnp.minimum(c, ncl - 1)

    xsh = jax.ShapeDtypeStruct((RT, 1536), F32)
    dsh = jax.ShapeDtypeStruct((RT, 128), F32)
    hspec = pl.BlockSpec((1, NS, NH * HP), lambda t: (rs(t), 0, 0))
    return pl.pallas_call(
        body, name="ssd_bwd", grid=(nc,),
        in_specs=[pl.BlockSpec((Q, 1536), lambda t: (cf(rs(t)), 0)),
                  pl.BlockSpec((Q, 1536), lambda t: (cb(rs(t)), 0)),
                  pl.BlockSpec((Q, 128), lambda t: (cf(rs(t)), ODT // 128)),
                  pl.BlockSpec((Q, 128), lambda t: (cb(rs(t)), ODT // 128)),
                  _cst((8, 128)), _cst((1, D)),
                  pl.BlockSpec((Q, D), lambda t: (lat(cf(rs(t))), 0)),
                  pl.BlockSpec((Q, D), lambda t: (lat(cb(rs(t))), 0)),
                  hspec, hspec],
        out_specs=[pl.BlockSpec((Q, 1536), lambda t: (cf(rs(t)), 0)),
                   pl.BlockSpec((Q, 1536), lambda t: (cb(rs(t)), 0)),
                   pl.BlockSpec((Q, 128), lambda t: (cf(rs(t)), 0)),
                   pl.BlockSpec((Q, 128), lambda t: (cb(rs(t)), 0)),
                   _cst((8, 128))],
        out_shape=[xsh, xsh, dsh, dsh, jax.ShapeDtypeStruct((8, 128), F32)],
        scratch_shapes=[pltpu.VMEM((NS, NH * HP), F32), pltpu.VMEM((NS, NH * HP), F32)],
        compiler_params=_params(("arbitrary",)),
    )(xbc, xbc, p, p, prm, dsk, dyd, dyd, hpf, hpb)


def _mix_fwd_vals(yf, yb, z, xs, u, v, dsk, sg, gg, gb):
    y = yf + yb + xs * dsk
    sz = _sig(z)
    hh = y * z * sz
    r = lax.rsqrt(jnp.mean(hh * hh, axis=-1, keepdims=True) + EPS)
    nh = hh * r
    ug, tu = _gelu(u)
    vg, tv = _gelu(v)
    vhat, vrstd = _ln(vg)
    vn = vhat * gg + gb
    return y, sz, r, nh, ug, tu, vg, tv, vhat, vrstd, vn


def _mix_fwd(yf, yb, p, xbc, dsk, sg, gg, gb, ws, bsT):
    L = yf.shape[0] - TL
    nt = L // TL

    def body(yf_ref, yb_ref, z_ref, xs_ref, u_ref, v_ref, dsk_ref, sg_ref, gg_ref, gb_ref,
             ws_ref, bs_ref, ys_ref, ym_ref):
        _, _, _, nh, ug, _, _, _, _, _, vn = _mix_fwd_vals(
            yf_ref[...], yb_ref[...], z_ref[...], xs_ref[...], u_ref[...], v_ref[...],
            dsk_ref[...], sg_ref[...], gg_ref[...], gb_ref[...])
        ys_ref[...] = (nh * sg_ref[...]).astype(ys_ref.dtype)
        for n in range(TL // Q):
            rs_ = slice(n * Q, (n + 1) * Q)
            for g in range(8):
                cs = slice(g * 128, (g + 1) * 128)
                mixed = _dot(ws_ref[g], vn[rs_, cs], NN) + bs_ref[:, g:g + 1]
                ym_ref[rs_, cs] = (ug[rs_, cs] * mixed).astype(ym_ref.dtype)

    return pl.pallas_call(
        body, name="mix_fwd", grid=(nt,),
        in_specs=[_rt(D), _rt(D), _rt(D, OZ // D), _rt(D, 0), _rt(D, OU // D), _rt(D, OV // D),
                  _cst((1, D)), _cst((1, D)), _cst((1, D)), _cst((1, D)),
                  _cst((8, 128, 128)), _cst((128, 128))],
        out_specs=[_rt(D), _rt(D)],
        out_shape=[jax.ShapeDtypeStruct((L, D), _MXU), jax.ShapeDtypeStruct((L, D), _MXU)],
        compiler_params=_params(("parallel",)),
    )(yf, yb, p, xbc, p, p, dsk, sg, gg, gb, ws, bsT)


def _mix_bwd(dys, dym, yf, yb, p, xbc, dp, dsk, sg, gg, gb, ws, bsT):
    L = dys.shape[0]
    nt = L // TL

    def body(dys_ref, dym_ref, yf_ref, yb_ref, z_ref, xs_ref, u_ref, v_ref, dsk_ref, sg_ref,
             gg_ref, gb_ref, ws_ref, bs_ref, dp_any, dzuv_ref, dy_ref, st_ref,
             dws_ref, dbs_ref, dvn_s):
        del dp_any
        dz_ref = dzuv_ref.at[:, OZ:OZ + D]
        du_ref = dzuv_ref.at[:, OU:OU + D]
        dv_ref = dzuv_ref.at[:, OV:OV + D]

        @pl.when(pl.program_id(0) == 0)
        def _():
            st_ref[...] = jnp.zeros_like(st_ref)
            dws_ref[...] = jnp.zeros_like(dws_ref)
            dbs_ref[...] = jnp.zeros_like(dbs_ref)

        z = z_ref[...]
        u = u_ref[...]
        v = v_ref[...]
        y, sz, r, nh, ug, tu, vg, tv, vhat, vrstd, vn = _mix_fwd_vals(
            yf_ref[...], yb_ref[...], z, xs_ref[...], u, v,
            dsk_ref[...], sg_ref[...], gg_ref[...], gb_ref[...])
        dys = dys_ref[...]
        st_ref[0:1, :] += _colsum(dys * nh)
        dn = dys * sg_ref[...]
        dhh = r * (dn - nh * jnp.mean(dn * nh, axis=-1, keepdims=True))
        dy_ref[...] = dhh * z * sz
        dz_ref[...] = (dhh * y * (sz * (1.0 + z * (1.0 - sz)))).astype(dz_ref.dtype)
        dym = dym_ref[...]
        lane = lax.broadcasted_iota(jnp.int32, (Q, 128), 1)
        dbs = jnp.zeros((Q, 128), F32)
        gu = _gelu_grad(u, tu)
        for n in range(TL // Q):
            rs_ = slice(n * Q, (n + 1) * Q)
            for g in range(8):
                cs = slice(g * 128, (g + 1) * 128)
                vb = vn[rs_, cs]
                mixed = _dot(ws_ref[g], vb, NN) + bs_ref[:, g:g + 1]
                dyb = dym[rs_, cs]
                dmx = dyb * ug[rs_, cs]
                du_ref[rs_, cs] = (dyb * mixed * gu[rs_, cs]).astype(du_ref.dtype)
                dvn_s[rs_, cs] = _dot(ws_ref[g], dmx, TN)
                dws_ref[g] += _dot(dmx, vb, NT)
                dbs = dbs + jnp.where(lane == g, jnp.sum(dmx, axis=1, keepdims=True), 0.0)
        dbs_ref[...] += dbs
        dvn = dvn_s[...]
        st_ref[1:2, :] += _colsum(dvn * vhat)
        st_ref[2:3, :] += _colsum(dvn)
        dvg = _ln_bwd(dvn * gg_ref[...], vhat, vrstd)
        dv_ref[...] = (dvg * _gelu_grad(v, tv)).astype(dv_ref.dtype)

    outs = pl.pallas_call(
        body, name="mix_bwd", grid=(nt,),
        in_specs=[_rt(D), _rt(D), _rt(D), _rt(D), _rt(D, OZ // D), _rt(D, 0), _rt(D, OU // D),
                  _rt(D, OV // D), _cst((1, D)), _cst((1, D)), _cst((1, D)), _cst((1, D)),
                  _cst((8, 128, 128)), _cst((128, 128)), pl.BlockSpec(memory_space=pl.ANY)],
        out_specs=[_rt(3 * D, 0), _rt(D), _cst((8, D)),
                   _cst((8, 128, 128)), _cst((128, 128))],
        out_shape=[jax.ShapeDtypeStruct(dp.shape, dp.dtype),
                   jax.ShapeDtypeStruct((L, D), F32), jax.ShapeDtypeStruct((8, D), F32),
                   jax.ShapeDtypeStruct((8, 128, 128), F32), jax.ShapeDtypeStruct((128, 128), F32)],
        scratch_shapes=[pltpu.VMEM((TL, D), F32)],
        input_output_aliases={14: 0},
        compiler_params=_params(("arbitrary",)),
    )(dys, dym, yf, yb, p, xbc, p, p, dsk, sg, gg, gb, ws, bsT, dp)
    return outs


def _merge_fwd(yssd, ygm, p, bg, ws, wg, wo, xn, modx, g1, b1):
    L = yssd.shape[0]
    tm = TL

    def body(ys_ref, yg_ref, g_ref, bg_ref, ws_ref, wg_ref, wo_ref, xn_ref, mx_ref, g1_ref, b1_ref,
             a1_ref, a2_ref, m_ref, o_ref, r1_ref, h2_ref):
        a1 = _dot(ys_ref[...], ws_ref[...], NN)
        a2 = _dot(yg_ref[...], wg_ref[...], NN)
        gt = _sig(g_ref[...] + bg_ref[...])
        mg = gt[:, :D] * a1 + gt[:, D:] * a2
        a1_ref[...] = a1
        a2_ref[...] = a2
        m_ref[...] = mg.astype(m_ref.dtype)
        out = _dot(mg, wo_ref[...], NN)
        o_ref[...] = out
        r1 = ALPHA * xn_ref[...] + mx_ref[2:3, :] * out
        xhat, _ = _ln(r1)
        x1 = xhat * g1_ref[...] + b1_ref[...]
        r1_ref[...] = r1
        h2_ref[...] = (x1 * (1.0 + mx_ref[4:5, :]) + mx_ref[3:4, :]).astype(h2_ref.dtype)

    rows = pl.BlockSpec((tm, D), lambda i: (i, 0))
    f32s = jax.ShapeDtypeStruct((L, D), F32)
    mxus = jax.ShapeDtypeStruct((L, D), _MXU)
    return pl.pallas_call(
        body, name="merge_fwd", grid=(L // tm,),
        in_specs=[rows, rows, pl.BlockSpec((tm, 2 * D), lambda i: (i, OG // (2 * D))), _cst((1, 2 * D)),
                  _cst((D, D)), _cst((D, D)), _cst((D, D)), rows, _cst((8, D)), _cst((1, D)), _cst((1, D))],
        out_specs=[rows] * 6,
        out_shape=[f32s, f32s, mxus, f32s, f32s, mxus],
        compiler_params=_params(("parallel",)),
    )(yssd, ygm, p, bg, ws, wg, wo, xn, modx, g1, b1)


def _mm_o2_res2(ff, w2, r1, tgt, modx, g1, b1, g2, b2):
    L, K = ff.shape
    tm = 2 * TL

    def body(a_ref, b_ref, r1_ref, t_ref, mx_ref, g1_ref, b1_ref, g2_ref, b2_ref,
             dr2_ref, do2_ref, st_ref, loss_ref):
        @pl.when(pl.program_id(0) == 0)
        def _():
            st_ref[...] = jnp.zeros_like(st_ref)
            loss_ref[...] = jnp.zeros_like(loss_ref)

        o2 = _dot(a_ref[...], b_ref[...], NN)
        xh1, _ = _ln(r1_ref[...])
        x1 = xh1 * g1_ref[...] + b1_ref[...]
        g2x = mx_ref[5:6, :]
        xh2, rstd2 = _ln(ALPHA * x1 + g2x * o2)
        err = xh2 * g2_ref[...] + b2_ref[...] - t_ref[...]
        per_tok = jnp.mean(err * err, axis=-1, keepdims=True)
        loss_ref[...] += 0.5 * jnp.sum(per_tok, axis=0, keepdims=True)
        dy = err * (1.0 / D)
        st_ref[0:1, :] += _colsum(dy * xh2)
        st_ref[1:2, :] += _colsum(dy)
        dr2 = _ln_bwd(dy * g2_ref[...], xh2, rstd2)
        st_ref[2:3, :] += _colsum(dr2 * o2)
        dr2_ref[...] = dr2
        do2_ref[...] = (g2x * dr2).astype(do2_ref.dtype)

    return pl.pallas_call(
        body, name="mm_o2_res2", grid=(L // tm,),
        in_specs=[_rt(K, rows=tm), _cst((K, D)), _rt(D, rows=tm), _rt(D, rows=tm), _cst((8, D))]
        + [_cst((1, D))] * 4,
        out_specs=[_rt(D, rows=tm), _rt(D, rows=tm), _cst((8, D)), _cst((8, 128))],
        out_shape=[jax.ShapeDtypeStruct((L, D), F32), jax.ShapeDtypeStruct((L, D), _MXU),
                   jax.ShapeDtypeStruct((8, D), F32), jax.ShapeDtypeStruct((8, 128), F32)],
        compiler_params=_params(("arbitrary",)),
    )(ff, w2, r1, tgt, modx, g1, b1, g2, b2)


def _mm_dh2_res1bwd(df13, w13i, dr2, r1, out, modx, g1, b1):
    L, K = df13.shape

    def body(a_ref, b_ref, dr2_ref, r1_ref, o_ref, mx_ref, g_ref, bb_ref, dr1_ref, do_ref, st_ref):
        @pl.when(pl.program_id(0) == 0)
        def _():
            st_ref[...] = jnp.zeros_like(st_ref)

        dh2 = _dot(a_ref[...], b_ref[...], NN)
        xh1, rstd1 = _ln(r1_ref[...])
        x1 = xh1 * g_ref[...] + bb_ref[...]
        dx1 = ALPHA * dr2_ref[...] + dh2 * (1.0 + mx_ref[4:5, :])
        st_ref[0:1, :] += _colsum(dh2 * x1)
        st_ref[1:2, :] += _colsum(dh2)
        st_ref[2:3, :] += _colsum(dx1 * xh1)
        st_ref[3:4, :] += _colsum(dx1)
        dr1 = _ln_bwd(dx1 * g_ref[...], xh1, rstd1)
        st_ref[4:5, :] += _colsum(dr1 * o_ref[...])
        dr1_ref[...] = dr1
        do_ref[...] = (mx_ref[2:3, :] * dr1).astype(do_ref.dtype)

    return pl.pallas_call(
        body, name="mm_dh2_res1bwd", grid=(L // TL,),
        in_specs=[_rt(K), _cst((K, D)), _rt(D), _rt(D), _rt(D), _cst((8, D)), _cst((1, D)), _cst((1, D))],
        out_specs=[_rt(D), _rt(D), _cst((8, D))],
        out_shape=[jax.ShapeDtypeStruct((L, D), F32), jax.ShapeDtypeStruct((L, D), _MXU),
                   jax.ShapeDtypeStruct((8, D), F32)],
        compiler_params=_params(("arbitrary",)),
    )(df13, w13i, dr2, r1, out, modx, g1, b1)


def _merge_bwd(dout, a1, a2, p, bg, wo, ws, wg, dp):
    L = a1.shape[0]
    tm = TL

    def body(do_ref, a1_ref, a2_ref, g_ref, bg_ref, wo_ref, ws_ref, wg_ref, dp_any,
             dg_ref, da1_ref, da2_ref, st_ref, dys_ref, dym_ref):
        del dp_any

        @pl.when(pl.program_id(0) == 0)
        def _():
            st_ref[...] = jnp.zeros_like(st_ref)

        dm = _dot(do_ref[...], wo_ref[...], NT)
        gt = _sig(g_ref[...] + bg_ref[...])
        g1 = gt[:, :D]
        g2 = gt[:, D:]
        da1 = (dm * g1).astype(da1_ref.dtype)
        da2 = (dm * g2).astype(da2_ref.dtype)
        da1_ref[...] = da1
        da2_ref[...] = da2
        dg1 = dm * a1_ref[...] * g1 * (1.0 - g1)
        dg2 = dm * a2_ref[...] * g2 * (1.0 - g2)
        st_ref[0:1, 0:D] += _colsum(dg1)
        st_ref[0:1, D:2 * D] += _colsum(dg2)
        dg_ref[:, 0:D] = dg1.astype(dg_ref.dtype)
        dg_ref[:, D:2 * D] = dg2.astype(dg_ref.dtype)
        dys_ref[...] = _dot(da1, ws_ref[...], NT)
        dym_ref[...] = _dot(da2, wg_ref[...], NT)

    rows = pl.BlockSpec((tm, D), lambda i: (i, 0))
    gates = pl.BlockSpec((tm, 2 * D), lambda i: (i, OG // (2 * D)))
    f32s = jax.ShapeDtypeStruct((L, D), F32)
    mxus = jax.ShapeDtypeStruct((L, D), _MXU)
    return pl.pallas_call(
        body, name="merge_bwd", grid=(L // tm,),
        in_specs=[rows, rows, rows, gates, _cst((1, 2 * D)), _cst((D, D)), _cst((D, D)), _cst((D, D)),
                  pl.BlockSpec(memory_space=pl.ANY)],
        out_specs=[gates, rows, rows, _cst((8, 2 * D)), rows, rows],
        out_shape=[jax.ShapeDtypeStruct(dp.shape, dp.dtype), mxus, mxus,
                   jax.ShapeDtypeStruct((8, 2 * D), F32), f32s, f32s],
        input_output_aliases={8: 0},
        compiler_params=_params(("arbitrary",)),
    )(dout, a1, a2, p, bg, wo, ws, wg, dp)


HFF = DFF // 2


def _mm_f13_glu(h2, w13i):
    L = h2.shape[0]
    tm = 512

    def body(a_ref, b_ref, f_ref, ff_ref):
        f = _dot(a_ref[...], b_ref[...], NT)
        f_ref[...] = f
        f1 = f[:, :HFF]
        ff_ref[...] = (f1 * _sig(f1) * f[:, HFF:]).astype(ff_ref.dtype)

    return pl.pallas_call(
        body, name="mm_f13_glu", grid=(DFF // HFF, L // tm),
        in_specs=[pl.BlockSpec((tm, D), lambda j, i: (i, 0)), pl.BlockSpec((2 * HFF, D), lambda j, i: (j, 0))],
        out_specs=[pl.BlockSpec((tm, 2 * HFF), lambda j, i: (i, j)), pl.BlockSpec((tm, HFF), lambda j, i: (i, j))],
        out_shape=[jax.ShapeDtypeStruct((L, 2 * DFF), F32), jax.ShapeDtypeStruct((L, DFF), _MXU)],
        compiler_params=_params(("parallel", "parallel")),
    )(h2, w13i)


def _mm_dff_glu(do2, w_ff2_f, f13i):
    L = do2.shape[0]
    tm = 512

    def body(a_ref, b_ref, f_ref, o_ref):
        d = _dot(a_ref[...], b_ref[...], NT)
        f1 = f_ref[:, :HFF]
        s = _sig(f1)
        o_ref[:, :HFF] = (d * f_ref[:, HFF:] * (s * (1.0 + f1 * (1.0 - s)))).astype(o_ref.dtype)
        o_ref[:, HFF:] = (d * f1 * s).astype(o_ref.dtype)

    return pl.pallas_call(
        body, name="mm_dff_glu", grid=(DFF // HFF, L // tm),
        in_specs=[pl.BlockSpec((tm, D), lambda j, i: (i, 0)), pl.BlockSpec((HFF, D), lambda j, i: (j, 0)),
                  pl.BlockSpec((tm, 2 * HFF), lambda j, i: (i, j))],
        out_specs=pl.BlockSpec((tm, 2 * HFF), lambda j, i: (i, j)),
        out_shape=jax.ShapeDtypeStruct((L, 2 * DFF), _MXU),
        compiler_params=_params(("parallel", "parallel")),
    )(do2, w_ff2_f, f13i)


def _conv_bwd(dxf, dxb, p, conv_w8, conv_b, dp):
    RT = p.shape[0]
    chunks = _seq_chunks(RT - TL)

    def body(df_ref, db_ref, p_ref, w_ref, b_ref, dp_any, o_ref, dw_ref, dbias_ref, dpre_s):
        del dp_any
        w = w_ref[...]
        bias = b_ref[...]
        srow = lax.broadcasted_iota(jnp.int32, (8, 128), 0)
        dwacc = jnp.zeros((8, 128), F32)
        dbacc = jnp.zeros((1, 128), F32)
        for r0, first, last in chunks:
            taps = _conv_taps(p_ref, r0, first, last)
            pre = bias + sum(w[k:k + 1, :] * taps[k] for k in range(5))
            s = _sig(pre)
            dpre = (df_ref[pl.ds(r0, TL), :] + db_ref[pl.ds(r0, TL), :]) * (s * (1.0 + pre * (1.0 - s)))
            dpre_s[pl.ds(r0, TL), :] = dpre
            dbacc = dbacc + _colsum(dpre)
            for k in range(5):
                dwacc = dwacc + jnp.where(srow == k, _colsum(dpre * taps[k]), 0.0)
        for r0, first, last in chunks:
            taps = _conv_taps(dpre_s, r0, first, last)
            dx = sum(w[k:k + 1, :] * taps[4 - k] for k in range(5))
            o_ref[pl.ds(r0, TL), :] = dx.astype(o_ref.dtype)
        dw_ref[...] = dwacc
        dbias_ref[...] = jnp.broadcast_to(dbacc, (8, 128))

    cspec = pl.BlockSpec((RT, 128), lambda j: (0, j))
    wspec = pl.BlockSpec((8, 128), lambda j: (0, j))
    return pl.pallas_call(
        body, name="conv_bwd", grid=(12,),
        in_specs=[cspec, cspec, pl.BlockSpec((RT, 128), lambda j: (0, _xbc_colblk(j))),
                  wspec, pl.BlockSpec((1, 128), lambda j: (0, j)), pl.BlockSpec(memory_space=pl.ANY)],
        out_specs=[pl.BlockSpec((RT, 128), lambda j: (0, _xbc_colblk(j))), wspec, wspec],
        out_shape=[jax.ShapeDtypeStruct(dp.shape, dp.dtype), jax.ShapeDtypeStruct((8, 1536), F32),
                   jax.ShapeDtypeStruct((8, 1536), F32)],
        scratch_shapes=[pltpu.VMEM((RT, 128), F32)],
        input_output_aliases={5: 0},
        compiler_params=_params(("parallel",)),
    )(dxf, dxb, p, conv_w8, conv_b, dp)


def _dp_fill(dp):
    nrow = dp.shape[0] // TL

    def body(dp_any, o_ref):
        del dp_any
        o_ref[...] = jnp.zeros_like(o_ref)

    return pl.pallas_call(
        body, name="dp_fill", grid=(1,),
        in_specs=[pl.BlockSpec(memory_space=pl.ANY)],
        out_specs=pl.BlockSpec((TL, OB), lambda i: (nrow - 1, 0)),
        out_shape=jax.ShapeDtypeStruct(dp.shape, dp.dtype),
        input_output_aliases={0: 0},
        compiler_params=_params(("arbitrary",)),
    )(dp)


def _dt_bwd(ddf, ddb, dp):
    RT = ddf.shape[0]
    assert NPJ - ODT == 256

    def body(f_ref, b_ref, dp_any, o_ref, st_ref):
        del dp_any

        @pl.when(pl.program_id(0) == 0)
        def _():
            st_ref[...] = jnp.zeros_like(st_ref)

        s = f_ref[...] + b_ref[...]
        o_ref[:, 0:128] = s.astype(o_ref.dtype)
        o_ref[:, 128:256] = jnp.zeros((TL, 128), o_ref.dtype)
        st_ref[0:1, :] += _colsum(s)

    return pl.pallas_call(
        body, name="dt_bwd", grid=(RT // TL,),
        in_specs=[_rt(128), _rt(128), pl.BlockSpec(memory_space=pl.ANY)],
        out_specs=[_rt(256, ODT // 256), _cst((8, 128))],
        out_shape=[jax.ShapeDtypeStruct(dp.shape, dp.dtype), jax.ShapeDtypeStruct((8, 128), F32)],
        input_output_aliases={2: 0},
        compiler_params=_params(("arbitrary",)),
    )(ddf, ddb, dp)


def _mm_dh1_ln0bwd(dp, w_in_p, dr1, x, ctx, g, b, modx, modc):
    L = x.shape[0]
    nt = L // TL

    def body(dp_ref, w_ref, dr1_ref, x_ref, c_ref, g_ref, b_ref, mx_ref, mc_ref, gx_ref, st_ref):
        i = pl.program_id(0)
        isc = i == nt

        @pl.when(i == 0)
        def _():
            st_ref[...] = jnp.zeros_like(st_ref)

        xin = jnp.where(isc, c_ref[...], x_ref[...])
        xhat, rstd = _ln(xin)
        xn = xhat * g_ref[...] + b_ref[...]
        sc = jnp.where(isc, mc_ref[1:2, :], mx_ref[1:2, :])
        dh = _dot(dp_ref[...], w_ref[...], NT)
        lat = jnp.where(isc, 0.0, 1.0)
        dxn = dh * (1.0 + sc) + (lat * ALPHA) * dr1_ref[...]
        tsh = _colsum(dh)
        tsc = _colsum(dh * xn)
        st_ref[0:1, :] += lat * tsh
        st_ref[1:2, :] += lat * tsc
        st_ref[2:3, :] += (1.0 - lat) * tsh
        st_ref[3:4, :] += (1.0 - lat) * tsc
        st_ref[4:5, :] += _colsum(dxn * xhat)
        st_ref[5:6, :] += _colsum(dxn)

        @pl.when(i < nt)
        def _():
            gx_ref[...] = _ln_bwd(dxn * g_ref[...], xhat, rstd)

    return pl.pallas_call(
        body, name="mm_dh1_ln0bwd", grid=(nt + 1,),
        in_specs=[_rt(NPJ), _cst((D, NPJ)), _rtc(D, nt), _rtc(D, nt), _cst((TL, D)), _cst((1, D)), _cst((1, D)),
                  _cst((8, D)), _cst((8, D))],
        out_specs=[_rtc(D, nt), _cst((8, D))],
        out_shape=[jax.ShapeDtypeStruct((L, D), F32), jax.ShapeDtypeStruct((8, D), F32)],
        compiler_params=pltpu.CompilerParams(dimension_semantics=("arbitrary",),
                                             vmem_limit_bytes=VMEM_LIMIT + 8 * 1024 * 1024),
    )(dp, w_in_p, dr1, x, ctx, g, b, modx, modc)


SECTIONS = ((0, 1024, OZ), (1024, 2048, OXS), (2048, 2304, OB), (2304, 2560, OC), (2560, 2592, ODT),
            (2592, 3616, OU), (3616, 4640, OV), (4640, 6688, OG))


def _perm_from_blocks(ga):
    n = ga.shape[2]
    pieces = []
    for na, nb, _ in sorted(SECTIONS, key=lambda sec: sec[2]):
        for k in range(NDEV):
            lo, hi = max(na, k * n), min(nb, (k + 1) * n)
            if lo < hi:
                pieces.append(ga[k][:, lo - k * n:hi - k * n])
    pieces.append(jnp.zeros((ga.shape[1], NPJ - NNAT), ga.dtype))
    return jnp.concatenate(pieces, axis=1)


def _blocks_from_perm(gp, n):
    blocks = []
    for k in range(NDEV):
        pieces = []
        for na, nb, po in SECTIONS:
            lo, hi = max(na, k * n), min(nb, (k + 1) * n)
            if lo < hi:
                pieces.append(gp[:, po + lo - na:po + hi - na])
        blocks.append(jnp.concatenate(pieces, axis=1))
    return jnp.stack(blocks)


def _padded(n, row_align):
    unit = row_align * D
    return -(-n // unit) * unit if row_align else n


def _slab(arrs, rows, row_align=0):
    parts = []
    for a in arrs:
        f = a.reshape(-1)
        parts.append(jnp.pad(f, (0, _padded(f.shape[0], row_align) - f.shape[0])))
    flat = jnp.concatenate(parts)
    flat = jnp.pad(flat, (0, rows * D - flat.shape[0]))
    return flat.reshape(rows, D)


def _unslab(slab, shapes, row_align=0):
    out, off = [], 0
    for shp in shapes:
        n = 1
        for s in shp:
            n *= s
        r0, r1 = off // D, -(-(off + n) // D)
        out.append(slab[r0:r1].reshape(-1)[off - r0 * D:off - r0 * D + n].reshape(shp))
        off += _padded(n, row_align)
    return out


def _row(v):
    return v.reshape(1, -1)


def _t(a):
    return jnp.swapaxes(a, 0, 1)


def _pad_rows(a, rows):
    return jnp.pad(a, ((0, rows - a.shape[0]), (0, 0)))


REPL = ["c_ctx", "ln0_g", "ln0_b", "b_ada", "conv_b", "dt_bias", "a_log", "d_skip", "ssd_norm_g",
        "gm_norm_g", "gm_norm_b", "w_spatial", "b_spatial", "b_gate", "ln1_g", "ln1_b", "ln2_g", "ln2_b"]
SMALL_ROWS = 160
WEIGHTS = ["c_ctx", "ln0_g", "ln0_b", "w_ada", "b_ada", "w_in", "conv_w", "conv_b", "dt_bias", "a_log",
           "d_skip", "ssd_norm_g", "gm_norm_g", "gm_norm_b", "w_spatial", "b_spatial", "b_gate",
           "w_ssd_proj", "w_gm_proj", "w_out", "ln1_g", "ln1_b", "w_ff1", "w_ff3", "w_ff2", "ln2_g", "ln2_b"]


def kernel(x, c, ctx, c_ctx, ln0_g, ln0_b, w_ada, b_ada, w_in, conv_w, conv_b, dt_bias, a_log, d_skip, ssd_norm_g, gm_norm_g, gm_norm_b, w_spatial, b_spatial, b_gate, w_ssd_proj, w_gm_proj, w_out, ln1_g, ln1_b, w_ff1, w_ff3, w_ff2, ln2_g, ln2_b, loss_target, m_c_ctx, m_ln0_g, m_ln0_b, m_w_ada, m_b_ada, m_w_in, m_conv_w, m_conv_b, m_dt_bias, m_a_log, m_d_skip, m_ssd_norm_g, m_gm_norm_g, m_gm_norm_b, m_w_spatial, m_b_spatial, m_b_gate, m_w_ssd_proj, m_w_gm_proj, m_w_out, m_ln1_g, m_ln1_b, m_w_ff1, m_w_ff3, m_w_ff2, m_ln2_g, m_ln2_b, v_c_ctx, v_ln0_g, v_ln0_b, v_w_ada, v_b_ada, v_w_in, v_conv_w, v_conv_b, v_dt_bias, v_a_log, v_d_skip, v_ssd_norm_g, v_gm_norm_g, v_gm_norm_b, v_w_spatial, v_b_spatial, v_b_gate, v_w_ssd_proj, v_w_gm_proj, v_w_out, v_ln1_g, v_ln1_b, v_w_ff1, v_w_ff3, v_w_ff2, v_ln2_g, v_ln2_b):
    W = dict(c_ctx=c_ctx, ln0_g=ln0_g, ln0_b=ln0_b, w_ada=w_ada, b_ada=b_ada, w_in=w_in, conv_w=conv_w,
             conv_b=conv_b, dt_bias=dt_bias, a_log=a_log, d_skip=d_skip, ssd_norm_g=ssd_norm_g,
             gm_norm_g=gm_norm_g, gm_norm_b=gm_norm_b, w_spatial=w_spatial, b_spatial=b_spatial,
             b_gate=b_gate, w_ssd_proj=w_ssd_proj, w_gm_proj=w_gm_proj, w_out=w_out, ln1_g=ln1_g,
             ln1_b=ln1_b, w_ff1=w_ff1, w_ff3=w_ff3, w_ff2=w_ff2, ln2_g=ln2_g, ln2_b=ln2_b)
    M = dict(c_ctx=m_c_ctx, ln0_g=m_ln0_g, ln0_b=m_ln0_b, w_ada=m_w_ada, b_ada=m_b_ada, w_in=m_w_in,
             conv_w=m_conv_w, conv_b=m_conv_b, dt_bias=m_dt_bias, a_log=m_a_log, d_skip=m_d_skip,
             ssd_norm_g=m_ssd_norm_g, gm_norm_g=m_gm_norm_g, gm_norm_b=m_gm_norm_b,
             w_spatial=m_w_spatial, b_spatial=m_b_spatial, b_gate=m_b_gate, w_ssd_proj=m_w_ssd_proj,
             w_gm_proj=m_w_gm_proj, w_out=m_w_out, ln1_g=m_ln1_g, ln1_b=m_ln1_b, w_ff1=m_w_ff1,
             w_ff3=m_w_ff3, w_ff2=m_w_ff2, ln2_g=m_ln2_g, ln2_b=m_ln2_b)
    V = dict(c_ctx=v_c_ctx, ln0_g=v_ln0_g, ln0_b=v_ln0_b, w_ada=v_w_ada, b_ada=v_b_ada, w_in=v_w_in,
             conv_w=v_conv_w, conv_b=v_conv_b, dt_bias=v_dt_bias, a_log=v_a_log, d_skip=v_d_skip,
             ssd_norm_g=v_ssd_norm_g, gm_norm_g=v_gm_norm_g, gm_norm_b=v_gm_norm_b,
             w_spatial=v_w_spatial, b_spatial=v_b_spatial, b_gate=v_b_gate, w_ssd_proj=v_w_ssd_proj,
             w_gm_proj=v_w_gm_proj, w_out=v_w_out, ln1_g=v_ln1_g, ln1_b=v_ln1_b, w_ff1=v_w_ff1,
             w_ff3=v_w_ff3, w_ff2=v_w_ff2, ln2_g=v_ln2_g, ln2_b=v_ln2_b)

    me = 4 * lax.axis_index("x") + 2 * lax.axis_index("y") + lax.axis_index("c")
    xl, cx, tgt = x[0], ctx[0], loss_target[0]
    L = xl.shape[0]
    assert cx.shape[0] == TL and L % TL == 0
    ada_n = w_ada.shape[2]
    cw_n = conv_w.shape[2]

    small1 = _pad_rows(jnp.concatenate([c, _slab([conv_w[0]], 1)], axis=0), 8)
    g1 = _all_gather(small1, "ag_small")
    c_all = g1[:, 0, :]
    conv_w_full = g1[:, 1, :5 * cw_n].reshape(NDEV, 5, cw_n).transpose(1, 0, 2).reshape(5, NDEV * cw_n)
    sq = w_ssd_proj.shape[1]
    ffr = w_ff2.shape[1]
    ffc = w_ff1.shape[2]
    late = [jnp.concatenate([w_ssd_proj[0], w_gm_proj[0], w_out[0], w_ff2[0]], axis=0).astype(_MXU),
            _t(w_ff1[0]).astype(_MXU), _t(w_ff3[0]).astype(_MXU)]

    c16 = _pad_rows(jnp.concatenate([c_all, _row(c_ctx)], axis=0), 16)
    b_ada_sh = lax.dynamic_slice(b_ada, (0, ada_n * me), (1, ada_n))
    modp = _ada_fwd(c16, w_ada[0], b_ada_sh)
    mod16 = _all_gather(modp, "ag_mod").transpose(1, 0, 2).reshape(16, NDEV * ada_n)

    ga, = _all_gather_multi([w_in[0].astype(_MXU)], "ag_w_in")
    ga, late, mod16 = lax.optimization_barrier((ga, late, mod16))
    lw_send, lw_recv, lw_src, lw_land, lw_token = _exchange_start(late, "ag_late_start", gather=True)
    w_in_p = _perm_from_blocks(ga)
    modx = _pad_rows(lax.dynamic_slice(mod16, (me, 0), (1, 6 * D)).reshape(6, D), 8) + lw_token[0, 0]
    modc = _pad_rows(mod16[8].reshape(6, D), 8)

    g0, b0 = _row(ln0_g), _row(ln0_b)
    xn, h1 = _ln0_fwd(xl, cx, g0, b0, modx, modc)
    p = _mm(h1, w_in_p, "nn", F32, "mm_p")
    conv_w8 = _pad_rows(conv_w_full, 8)
    xbc = _conv_fwd(p, conv_w8, conv_b)
    prm = _pad_rows(jnp.pad(jnp.stack([dt_bias.reshape(32), a_log.reshape(32)]), ((0, 0), (0, 96))), 8)
    yf, yb, hpf, hpb = _ssd2_fwd(xbc, p, prm)
    lw_land = _exchange_wait(lw_send, lw_recv, lw_src, lw_land, yf, "ag_late_wait", gather=True)
    fw_send, fw_recv, lw_land, fw_token = _forward_start(lw_land, "ag_fwd_start")
    dsk = _row(jnp.repeat(d_skip[0, 0] + d_skip[0, 1], HP)) + fw_token[0:1, 0:1]
    ws_m = w_spatial[0].astype(_MXU)
    bsT = jnp.pad(b_spatial[0].T, ((0, 0), (0, 120)))
    mixp = (dsk, ssd_norm_g, gm_norm_g, gm_norm_b, ws_m, bsT)
    yssd, ygm = _mix_fwd(yf, yb, p, xbc, *mixp)
    gb, gc1, gc2 = _forward_wait(fw_send, fw_recv, lw_land, yssd, "ag_fwd_wait")

    gb, gc1, gc2 = lax.optimization_barrier(
        [lax.dynamic_update_index_in_dim(g, mine, me, 0) for g, mine in zip((gb, gc1, gc2), late)])
    w_ssd_f = gb[:, 0:sq].reshape(NDEV * sq, D)
    w_gm_f = gb[:, sq:2 * sq].reshape(NDEV * sq, D)
    w_out_f = gb[:, 2 * sq:3 * sq].reshape(NDEV * sq, D)
    w_ff2_f = gb[:, 3 * sq:3 * sq + ffr].reshape(NDEV * ffr, D)
    assert HFF == (NDEV // 2) * ffc
    hd = NDEV // 2
    w13i = jnp.concatenate([g[k] for t in range(2) for g in (gc1, gc2) for k in range(t * hd, (t + 1) * hd)],
                           axis=0)
    a1, a2, merged, out, r1, h2 = _merge_fwd(yssd, ygm, p, b_gate, w_ssd_f, w_gm_f, w_out_f,
                                             xn, modx, ln1_g, ln1_b)
    f13, ff = _mm_f13_glu(h2, w13i)

    dr2, do2, st2, loss_slab = _mm_o2_res2(ff, w_ff2_f, r1, tgt, modx, ln1_g, ln1_b, ln2_g, ln2_b)
    loss = lax.psum(loss_slab[0, 0], ("x", "y", "c"))
    df13 = _mm_dff_glu(do2, w_ff2_f, f13)
    dw_ff2 = _mm(ff, do2, "tn", _MXU, "mm_dw_ff2")
    dw13i = _mm(df13, h2, "tn", _MXU, "mm_dw13")

    def owner_blocks(first):
        return jnp.concatenate([dw13i[t * 2 * HFF + first:t * 2 * HFF + first + HFF].reshape(NDEV // 2, ffc, D)
                                for t in range(2)], axis=0)

    xff = [dw_ff2.reshape(NDEV, ffr, D), owner_blocks(0), owner_blocks(HFF)]
    ff_send, ff_recv, ff_src, ff_land, ff_token = _exchange_start(xff, "xchg_ff_start")
    modx = modx + ff_token[0, 0]
    dr1, dout, st1 = _mm_dh2_res1bwd(df13, w13i, dr2, r1, out, modx, ln1_g, ln1_b)
    dw_out = _mm(merged, dout, "tn", _MXU, "mm_dw_out")
    dp = _dp_fill(lax.empty((L + TL, NPJ), _MXU))
    dp, da1, da2, stg, dys, dym = _merge_bwd(dout, a1, a2, p, b_gate, w_out_f, w_ssd_f, w_gm_f, dp)
    dw_ssd = _mm(yssd, da1, "tn", _MXU, "mm_dw_ssd")
    dw_gm = _mm(ygm, da2, "tn", _MXU, "mm_dw_gm")
    xsq = [jnp.concatenate([dw_ssd.reshape(NDEV, sq, D), dw_gm.reshape(NDEV, sq, D),
                            dw_out.reshape(NDEV, sq, D)], axis=1)]
    sq_send, sq_recv, sq_src, sq_land, sq_token = _exchange_start(xsq, "xchg_sq_start")
    mixp = (dsk + sq_token[0:1, 0:1],) + mixp[1:]
    dp, dyd, stm, dws, dbsT = _mix_bwd(dys, dym, yf, yb, p, xbc, dp, *mixp)
    dxf, dxb, ddf, ddb, sts = _ssd2_bwd(xbc, p, prm, dsk, dyd, hpf, hpb)
    dp, dcw, dcb = _conv_bwd(dxf, dxb, p, conv_w8, conv_b, dp)
    dp, std = _dt_bwd(ddf, ddb, dp)
    hw = D // 2
    xin_a = [_blocks_from_perm(_mm(h1[:, :hw], dp, "tn", _MXU, "mm_dw_in_a"), w_in.shape[2])]
    ina_send, ina_recv, ina_src, ina_land, ina_token = _exchange_start(xin_a, "xchg_in_a_start")
    h1b, ina_token = lax.optimization_barrier((h1[:, hw:], ina_token))
    xin_b = [_blocks_from_perm(_mm(h1b, dp, "tn", _MXU, "mm_dw_in_b"), w_in.shape[2])]
    inb_send, inb_recv, inb_src, inb_land, inb_token = _exchange_start(xin_b, "xchg_in_b_start")
    modx = modx + (ina_token[0, 0] + inb_token[0, 0])
    grad_x, st0 = _mm_dh1_ln0bwd(dp, w_in_p, dr1, xl, cx, g0, b0, modx, modc)

    zero = jnp.zeros((D,), F32)
    dmod = jnp.stack([jnp.concatenate([st0[0], st0[1], st1[4], st1[1], st1[0], st2[2]]),
                      jnp.concatenate([st0[2], st0[3], zero, zero, zero, zero])])
    g16 = _all_gather(_pad_rows(dmod, 8), "ag_dmod")[:, 0:2, :].reshape(16, 6 * D)
    g16_sh = lax.dynamic_slice(g16, (0, ada_n * me), (16, ada_n))
    c16b = jnp.stack([c_all, jnp.broadcast_to(_row(c_ctx), (NDEV, D))], axis=1).reshape(16, D)
    dw_ada, db_ada8, dcc8 = _ada_bwd(c16b, g16, g16_sh, w_ada[0])

    part = dict(
        c_ctx=dcc8[0], ln0_g=st0[4], ln0_b=st0[5], conv_w=dcw[0:5], conv_b=dcb[0],
        dt_bias=std[0, 0:32], a_log=sts[0, 0:32], d_skip=jnp.tile(sts[1, 0:16], 2),
        ssd_norm_g=stm[0], gm_norm_g=stm[1], gm_norm_b=stm[2], w_spatial=dws,
        b_spatial=dbsT[:, 0:8].T, b_gate=stg[0], ln1_g=st1[2], ln1_b=st1[3], ln2_g=st2[0], ln2_b=st2[1])
    pnames = list(part)
    psum8 = _sum8(_all_gather(_slab([part[n] for n in pnames], SMALL_ROWS), "ag_smallgrads"), "sum_smallgrads")
    small = dict(zip(pnames, _unslab(psum8, [part[n].shape for n in pnames])))
    grads = {n: small[n].reshape(W[n].shape) for n in pnames if n != "conv_w"}
    grads["conv_w"] = lax.dynamic_slice(small["conv_w"], (0, cw_n * me), (5, cw_n)).reshape(conv_w.shape)
    grads["b_ada"] = db_ada8[0:1]
    grads["w_ada"] = dw_ada.reshape(w_ada.shape)

    delta, new_m, new_v = {}, {}, {}

    def adam_group(names, rows, tag, align=0):
        shapes = [W[n].shape for n in names]
        outs = _adamw(*[_slab([src[n] for n in names], rows, align) for src in (grads, W, M, V)], tag)
        for res, slab in zip((delta, new_m, new_v), outs):
            for n, a in zip(names, _unslab(slab, shapes, align)):
                res[n] = a

    adam_group(REPL + ["conv_w"], SMALL_ROWS, "adamw_small")
    res = _adamw(grads["w_ada"][0], w_ada[0], m_w_ada[0], v_w_ada[0], "adamw_w_ada")
    delta["w_ada"], new_m["w_ada"], new_v["w_ada"] = [a[None] for a in res]

    rff = _exchange_wait(ff_send, ff_recv, ff_src, ff_land, st0, "xchg_ff_wait")
    rsq = _exchange_wait(sq_send, sq_recv, sq_src, sq_land, rff[0], "xchg_sq_wait")
    rin_a = _exchange_wait(ina_send, ina_recv, ina_src, ina_land, delta["ln2_b"], "xchg_in_a_wait")
    rin_b = _exchange_wait(inb_send, inb_recv, inb_src, inb_land, rin_a[0], "xchg_in_b_wait")
    rin = jnp.concatenate([rin_a[0], rin_b[0]], axis=1)

    def own(blocks):
        return lax.dynamic_index_in_dim(blocks, me, 0, keepdims=False)

    own_in = jnp.concatenate([own(xin_a[0]), own(xin_b[0])], axis=0)
    for n, r8, mine, row0, tr in (
            ("w_ff2", rff[0], own(xff[0]), 0, ffr // 2), ("w_ssd_proj", rsq[0], own(xsq[0]), 0, sq),
            ("w_gm_proj", rsq[0], own(xsq[0]), sq, sq), ("w_out", rsq[0], own(xsq[0]), 2 * sq, sq),
            ("w_in", rin, own_in, 0, 256)):
        res = _adamw_sum(r8, mine, W[n][0], M[n][0], V[n][0], row0, tr, "adamw_" + n)
        grads[n], delta[n], new_m[n], new_v[n] = [a[None] for a in res]
    for n, r8, mine in (("w_ff1", rff[1], own(xff[1])), ("w_ff3", rff[2], own(xff[2]))):
        res = _adamw_sum(r8, mine, _t(W[n][0]), _t(M[n][0]), _t(V[n][0]), 0, ffc // 2, "adamw_" + n)
        grads[n], delta[n], new_m[n], new_v[n] = [_t(a)[None] for a in res]

    return (loss, grad_x[None], *[grads[n] for n in WEIGHTS], *[delta[n] for n in WEIGHTS],
            *[new_m[n] for n in WEIGHTS], *[new_v[n] for n in WEIGHTS])
```

```python
import jax
import jax.numpy as jnp
from jax import lax
from jax.experimental import pallas as pl
from jax.experimental.pallas import tpu as pltpu

_MXU = jnp.bfloat16
F32 = jnp.float32
D = 1024
TL = 256
Q = 128
NH, HP, NS, HPG = 16, 64, 128, 8
DFF = 2816
ALPHA = 2.0 ** 0.25
EPS = 1e-5
OZ, OU, OV, OXS, OG, OB, OC, ODT, NPJ = 0, 1024, 2048, 3072, 4096, 6144, 6400, 6656, 6912
NNAT = 6688
NDEV = 8
ADAM_LR, ADAM_B1, ADAM_B2, ADAM_EPS, ADAM_WD, ADAM_STEP = 1e-3, 0.9, 0.999, 1e-8, 0.01, 10
VMEM_LIMIT = 48 * 1024 * 1024

NN = ((1,), (0,))
NT = ((1,), (1,))
TN = ((0,), (0,))
MESH = pl.DeviceIdType.MESH


def _dot(a, b, dims):
    return lax.dot_general(a.astype(_MXU), b.astype(_MXU), (dims, ((), ())),
                           preferred_element_type=F32)


def _tile(n, cands):
    for c in cands:
        if n % c == 0:
            return c
    return n


def _divisor_tile(n, cap, mult):
    best = n
    for t in range(mult, min(n, cap) + 1, mult):
        if n % t == 0:
            best = t
    return best


def _params(sem):
    return pltpu.CompilerParams(dimension_semantics=sem, vmem_limit_bytes=VMEM_LIMIT)


def _cst(shape):
    nd = len(shape)
    return pl.BlockSpec(shape, lambda *_: (0,) * nd)


def _rt(w, cb=0, rows=TL):
    return pl.BlockSpec((rows, w), lambda i: (i, cb))


def _rtc(w, nt, cb=0):
    return pl.BlockSpec((TL, w), lambda i: (jnp.minimum(i, nt - 1), cb))


def _sig(x):
    return jax.nn.sigmoid(x)


def _softplus(x):
    return jnp.maximum(x, 0.0) + jnp.log1p(jnp.exp(-jnp.abs(x)))


_G0, _G1 = 0.7978845608028654, 0.044715


def _gelu(x):
    t = jnp.tanh(_G0 * (x + _G1 * x * x * x))
    return 0.5 * x * (1.0 + t), t


def _gelu_grad(x, t):
    return 0.5 * (1.0 + t) + 0.5 * x * (1.0 - t * t) * _G0 * (1.0 + 3.0 * _G1 * x * x)


def _ln(r):
    mu = jnp.mean(r, axis=-1, keepdims=True)
    xc = r - mu
    var = jnp.mean(xc * xc, axis=-1, keepdims=True)
    rstd = lax.rsqrt(var + EPS)
    return xc * rstd, rstd


def _ln_bwd(dyh, xhat, rstd):
    return rstd * (dyh - jnp.mean(dyh, axis=-1, keepdims=True)
                   - xhat * jnp.mean(dyh * xhat, axis=-1, keepdims=True))


def _colsum(v):
    return jnp.sum(v, axis=0, keepdims=True)


def _cumsum_rows(a, rev):
    n = a.shape[0]
    row = lax.broadcasted_iota(jnp.int32, a.shape, 0)
    s = 1
    while s < n:
        if rev:
            a = a + jnp.where(row < n - s, pltpu.roll(a, n - s, 0), 0.0)
        else:
            a = a + jnp.where(row >= s, pltpu.roll(a, s, 0), 0.0)
        s *= 2
    return a


def _mm(a, b, mode, out_dtype, name, tm_cap=1088, tk_cap=2304):
    if mode == "tn":
        K, M = a.shape
    else:
        M, K = a.shape
    N = b.shape[0] if mode == "nt" else b.shape[1]
    tm = _divisor_tile(M, 1408, 128) if mode == "tn" else _divisor_tile(M, tm_cap, 16)
    tn = _divisor_tile(N, 1408, 128)
    tk = _divisor_tile(K, tk_cap, 128)
    nk = K // tk
    dims = {"nn": NN, "nt": NT, "tn": TN}[mode]
    use_acc = nk > 1 and out_dtype != F32

    def body(a_ref, b_ref, o_ref, *acc):
        prod = _dot(a_ref[...], b_ref[...], dims)
        if nk == 1:
            o_ref[...] = prod.astype(o_ref.dtype)
            return
        acc_ref = acc[0] if use_acc else o_ref
        k = pl.program_id(2)

        @pl.when(k == 0)
        def _():
            acc_ref[...] = prod

        if use_acc:
            @pl.when((k > 0) & (k < nk - 1))
            def _():
                acc_ref[...] += prod

            @pl.when(k == nk - 1)
            def _():
                o_ref[...] = (acc_ref[...] + prod).astype(o_ref.dtype)
        else:
            @pl.when(k > 0)
            def _():
                o_ref[...] += prod

    if mode == "tn":
        a_spec = pl.BlockSpec((tk, tm), lambda i, j, k: (k, i))
    else:
        a_spec = pl.BlockSpec((tm, tk), lambda i, j, k: (i, k))
    if mode == "nt":
        b_spec = pl.BlockSpec((tn, tk), lambda i, j, k: (j, k))
    else:
        b_spec = pl.BlockSpec((tk, tn), lambda i, j, k: (k, j))
    return pl.pallas_call(
        body, name=name, grid=(M // tm, N // tn, nk),
        in_specs=[a_spec, b_spec],
        out_specs=pl.BlockSpec((tm, tn), lambda i, j, k: (i, j)),
        out_shape=jax.ShapeDtypeStruct((M, N), out_dtype),
        scratch_shapes=[pltpu.VMEM((tm, tn), F32)] if use_acc else [],
        compiler_params=_params(("parallel", "parallel", "arbitrary")),
    )(a, b)


def _all_gather(x, name):
    def body(x_ref, out_ref, send_sems, recv_sems, local_sem):
        mx, my, mc = lax.axis_index("x"), lax.axis_index("y"), lax.axis_index("c")
        me, sibling = (mx, my, mc), (mx, my, 1 - mc)
        chips = [(1 - mx, my), (mx, 1 - my), (1 - mx, 1 - my)]

        def slot(px, py, pc):
            return out_ref.at[4 * px + 2 * py + pc]

        def copy(k, block, to, src=None):
            return pltpu.make_async_remote_copy(
                src_ref=slot(*block) if src is None else src, dst_ref=slot(*block),
                send_sem=send_sems.at[k], recv_sem=recv_sems.at[k],
                device_id=to, device_id_type=MESH)

        mine = pltpu.make_async_copy(x_ref, slot(*me), local_sem)
        mine.start()
        first = [copy(0, me, sibling, src=x_ref)]
        first += [copy(1 + j, me, (*chip, mc), src=x_ref) for j, chip in enumerate(chips)]
        for cp in first:
            cp.start()
        passed = [copy(4 + j, (*chip, mc), sibling) for j, chip in enumerate(chips)]
        for j, chip in enumerate(chips):
            copy(1 + j, (*chip, mc), me).wait_recv()
            passed[j].start()
        copy(0, sibling, me).wait_recv()
        for j, chip in enumerate(chips):
            copy(4 + j, (*chip, 1 - mc), me).wait_recv()
        for cp in first + passed:
            cp.wait_send()
        mine.wait()

    return pl.pallas_call(
        body, name=name,
        out_shape=jax.ShapeDtypeStruct((NDEV,) + x.shape, x.dtype),
        in_specs=[pl.BlockSpec(memory_space=pl.ANY)],
        out_specs=pl.BlockSpec(memory_space=pl.ANY),
        scratch_shapes=[pltpu.SemaphoreType.DMA((7,)), pltpu.SemaphoreType.DMA((7,)),
                        pltpu.SemaphoreType.DMA],
    )(x)


def _any_specs(n):
    return [pl.BlockSpec(memory_space=pl.ANY)] * n


def _all_gather_multi(xs, name):
    na = len(xs)

    def body(*refs):
        x_refs, out_refs = refs[:na], refs[na:2 * na]
        send_sems, recv_sems, local_sems = refs[2 * na:]
        mx, my, mc = lax.axis_index("x"), lax.axis_index("y"), lax.axis_index("c")
        me, sibling = (mx, my, mc), (mx, my, 1 - mc)
        chips = [(1 - mx, my), (mx, 1 - my), (1 - mx, 1 - my)]

        def copy(a, k, block, to, src=None):
            slot = out_refs[a].at[4 * block[0] + 2 * block[1] + block[2]]
            return pltpu.make_async_remote_copy(
                src_ref=slot if src is None else src, dst_ref=slot,
                send_sem=send_sems.at[7 * a + k], recv_sem=recv_sems.at[7 * a + k],
                device_id=to, device_id_type=MESH)

        mine = [pltpu.make_async_copy(x_refs[a], out_refs[a].at[4 * mx + 2 * my + mc], local_sems.at[a])
                for a in range(na)]
        for cp in mine:
            cp.start()
        first = []
        for a in range(na):
            first.append(copy(a, 0, me, sibling, src=x_refs[a]))
            first += [copy(a, 1 + j, me, (*chip, mc), src=x_refs[a]) for j, chip in enumerate(chips)]
        for cp in first:
            cp.start()
        passed = []
        for a in range(na):
            for j, chip in enumerate(chips):
                copy(a, 1 + j, (*chip, mc), me).wait_recv()
                fwd = copy(a, 4 + j, (*chip, mc), sibling)
                fwd.start()
                passed.append(fwd)
        for a in range(na):
            copy(a, 0, sibling, me).wait_recv()
            for j, chip in enumerate(chips):
                copy(a, 4 + j, (*chip, 1 - mc), me).wait_recv()
        for cp in first + passed:
            cp.wait_send()
        for cp in mine:
            cp.wait()

    return pl.pallas_call(
        body, name=name,
        out_shape=[jax.ShapeDtypeStruct((NDEV,) + x.shape, x.dtype) for x in xs],
        in_specs=_any_specs(na), out_specs=_any_specs(na),
        scratch_shapes=[pltpu.SemaphoreType.DMA((7 * na,)), pltpu.SemaphoreType.DMA((7 * na,)),
                        pltpu.SemaphoreType.DMA((na,))],
    )(*xs)


def _adamw_sum(r8, own, w, m, v, row0, tr, name):
    R, C = w.shape
    assert row0 % tr == 0
    blk0 = row0 // tr
    bc1 = 1.0 - ADAM_B1 ** ADAM_STEP
    bc2 = 1.0 - ADAM_B2 ** ADAM_STEP

    def body(r_ref, *refs):
        if own is None:
            gg = r_ref[0].astype(F32)
        else:
            gg = refs[0][...].astype(F32)
            refs = refs[1:]
        w_ref, m_ref, v_ref, g_ref, d_ref, mo_ref, vo_ref = refs
        for k in range(1, NDEV):
            gg = gg + r_ref[k].astype(F32)
        mn = ADAM_B1 * m_ref[...] + (1.0 - ADAM_B1) * gg
        vn = ADAM_B2 * v_ref[...] + (1.0 - ADAM_B2) * (gg * gg)
        mh = mn / bc1
        vh = vn / bc2
        g_ref[...] = gg
        d_ref[...] = -ADAM_LR * (mh / (jnp.sqrt(vh) + ADAM_EPS) + ADAM_WD * w_ref[...])
        mo_ref[...] = mn
        vo_ref[...] = vn

    spec = pl.BlockSpec((tr, C), lambda i: (i, 0))
    sh = jax.ShapeDtypeStruct((R, C), F32)
    own_ops = [] if own is None else [own]
    own_specs = [] if own is None else [pl.BlockSpec((tr, C), lambda i: (i + blk0, 0))]
    return pl.pallas_call(
        body, name=name, grid=(R // tr,),
        in_specs=[pl.BlockSpec((NDEV, tr, C), lambda i: (0, i + blk0, 0))] + own_specs + [spec, spec, spec],
        out_specs=[spec] * 4, out_shape=[sh] * 4, compiler_params=_params(("parallel",)),
    )(r8, *own_ops, w, m, v)


_HBM = pl.BlockSpec(memory_space=pltpu.HBM)
_SEM = pl.BlockSpec(memory_space=pltpu.SEMAPHORE)
_EFFECT = pltpu.SideEffectType.DATAFLOW_SIDE_EFFECTING


def _exchange_copies(g_refs, land_refs, send_sems, recv_sems, gather):
    mx, my, mc = lax.axis_index("x"), lax.axis_index("y"), lax.axis_index("c")
    copies = []
    for a in range(len(g_refs)):
        for f in ((1, 2, 4, 6) if gather else range(1, NDEV)):
            px = 1 - mx if (f >> 2) & 1 else mx
            py = 1 - my if (f >> 1) & 1 else my
            pc = 1 - mc if f & 1 else mc
            src = g_refs[a] if gather else g_refs[a].at[4 * px + 2 * py + pc]
            dst = land_refs[a].at[4 * mx + 2 * my + mc] if gather else land_refs[a].at[f]
            copies.append(pltpu.make_async_remote_copy(
                src_ref=src, dst_ref=dst,
                send_sem=send_sems.at[7 * a + f - 1], recv_sem=recv_sems.at[7 * a + f - 1],
                device_id=(px, py, pc), device_id_type=MESH))
    return copies


def _exchange_start(gs, name, gather=False):
    na = len(gs)

    def body(*refs):
        for cp in _exchange_copies(refs[:na], refs[na:2 * na], refs[2 * na], refs[2 * na + 1], gather):
            cp.start()
        refs[-1][...] = jnp.zeros_like(refs[-1])

    hbm = [pltpu.HBM(g.shape, g.dtype) for g in gs]
    land_shapes = [((NDEV,) + g.shape) if gather else g.shape for g in gs]
    lands = [pltpu.with_memory_space_constraint(lax.empty(shp, g.dtype), pltpu.HBM)
             for shp, g in zip(land_shapes, gs)]
    hbm_land = [pltpu.HBM(shp, g.dtype) for shp, g in zip(land_shapes, gs)]
    outs = pl.pallas_call(
        body, name=name,
        out_shape=(pltpu.SemaphoreType.DMA((7 * na,)), pltpu.SemaphoreType.DMA((7 * na,)), *hbm, *hbm_land,
                   jax.ShapeDtypeStruct((8, 128), F32)),
        in_specs=[_HBM] * (2 * na),
        out_specs=(_SEM, _SEM, *([_HBM] * (2 * na)), pl.BlockSpec(memory_space=pltpu.VMEM)),
        input_output_aliases={i: 2 + i for i in range(2 * na)},
        compiler_params=pltpu.CompilerParams(has_side_effects=_EFFECT),
    )(*[pltpu.with_memory_space_constraint(g, pltpu.HBM) for g in gs], *lands)
    return outs[0], outs[1], outs[2:2 + na], outs[2 + na:2 + 2 * na], outs[-1]


def _forward_copies(land_refs, send_sems, recv_sems):
    mx, my, mc = lax.axis_index("x"), lax.axis_index("y"), lax.axis_index("c")
    copies = []
    for a in range(len(land_refs)):
        for j, (fx, fy) in enumerate(((0, 1), (1, 0), (1, 1))):
            px = 1 - mx if fx else mx
            py = 1 - my if fy else my
            blk = land_refs[a].at[4 * px + 2 * py + mc]
            copies.append(pltpu.make_async_remote_copy(
                src_ref=blk, dst_ref=blk, send_sem=send_sems.at[3 * a + j], recv_sem=recv_sems.at[3 * a + j],
                device_id=(mx, my, 1 - mc), device_id_type=MESH))
    return copies


def _forward_start(lands, name):
    na = len(lands)

    def body(*refs):
        for cp in _forward_copies(refs[:na], refs[na], refs[na + 1]):
            cp.start()
        refs[-1][...] = jnp.zeros_like(refs[-1])

    outs = pl.pallas_call(
        body, name=name,
        out_shape=(pltpu.SemaphoreType.DMA((3 * na,)), pltpu.SemaphoreType.DMA((3 * na,)),
                   *[pltpu.HBM(g.shape, g.dtype) for g in lands], jax.ShapeDtypeStruct((8, 128), F32)),
        in_specs=[_HBM] * na,
        out_specs=(_SEM, _SEM, *([_HBM] * na), pl.BlockSpec(memory_space=pltpu.VMEM)),
        input_output_aliases={i: 2 + i for i in range(na)},
        compiler_params=pltpu.CompilerParams(has_side_effects=_EFFECT),
    )(*lands)
    return outs[0], outs[1], outs[2:2 + na], outs[-1]


def _forward_wait(send_sems, recv_sems, lands, after, name):
    na = len(lands)

    def body(*refs):
        for cp in _forward_copies(refs[:na], refs[na], refs[na + 1]):
            cp.wait_send()
            cp.wait_recv()

    return pl.pallas_call(
        body, name=name,
        out_shape=tuple(pltpu.HBM(g.shape, g.dtype) for g in lands),
        in_specs=[_HBM] * na + [_SEM, _SEM, pl.BlockSpec(memory_space=pl.ANY)],
        out_specs=tuple([_HBM] * na),
        input_output_aliases={i: i for i in range(na)},
        compiler_params=pltpu.CompilerParams(has_side_effects=_EFFECT),
    )(*lands, send_sems, recv_sems, after)


def _exchange_wait(send_sems, recv_sems, g_thru, land_thru, after, name, gather=False):
    na = len(g_thru)

    def body(*refs):
        for cp in _exchange_copies(refs[:na], refs[na:2 * na], refs[2 * na], refs[2 * na + 1], gather):
            cp.wait_send()
            cp.wait_recv()

    outs = pl.pallas_call(
        body, name=name,
        out_shape=tuple(pltpu.HBM(g.shape, g.dtype) for g in list(g_thru) + list(land_thru)),
        in_specs=[_HBM] * (2 * na) + [_SEM, _SEM, pl.BlockSpec(memory_space=pl.ANY)],
        out_specs=tuple([_HBM] * (2 * na)),
        input_output_aliases={i: i for i in range(2 * na)},
        compiler_params=pltpu.CompilerParams(has_side_effects=_EFFECT),
    )(*g_thru, *land_thru, send_sems, recv_sems, after)
    return outs[na:]


def _sum8(r, name):
    _, R, C = r.shape
    tr = _tile(R, (256, 160, 128, 64, 32, 16, 8))

    def body(r_ref, o_ref):
        acc = r_ref[0].astype(F32)
        for k in range(1, NDEV):
            acc = acc + r_ref[k].astype(F32)
        o_ref[...] = acc

    return pl.pallas_call(
        body, name=name, grid=(R // tr,),
        in_specs=[pl.BlockSpec((NDEV, tr, C), lambda i: (0, i, 0))],
        out_specs=pl.BlockSpec((tr, C), lambda i: (i, 0)),
        out_shape=jax.ShapeDtypeStruct((R, C), F32),
        compiler_params=_params(("parallel",)),
    )(r)


def _adamw(g, w, m, v, name):
    R, C = g.shape
    tr = _tile(R, (256, 160, 128, 64, 32, 16, 8))
    bc1 = 1.0 - ADAM_B1 ** ADAM_STEP
    bc2 = 1.0 - ADAM_B2 ** ADAM_STEP

    def body(g_ref, w_ref, m_ref, v_ref, d_ref, mo_ref, vo_ref):
        gg = g_ref[...]
        mn = ADAM_B1 * m_ref[...] + (1.0 - ADAM_B1) * gg
        vn = ADAM_B2 * v_ref[...] + (1.0 - ADAM_B2) * (gg * gg)
        mh = mn / bc1
        vh = vn / bc2
        d_ref[...] = -ADAM_LR * (mh / (jnp.sqrt(vh) + ADAM_EPS) + ADAM_WD * w_ref[...])
        mo_ref[...] = mn
        vo_ref[...] = vn

    spec = pl.BlockSpec((tr, C), lambda i: (i, 0))
    sh = jax.ShapeDtypeStruct((R, C), F32)
    return pl.pallas_call(
        body, name=name, grid=(R // tr,), in_specs=[spec] * 4, out_specs=[spec] * 3,
        out_shape=[sh] * 3, compiler_params=_params(("parallel",)),
    )(g, w, m, v)


def _ada_fwd(c16, w_sh, b_sh):
    def body(c_ref, w_ref, b_ref, o_ref):
        c = c_ref[...]
        o_ref[...] = _dot(c * _sig(c), w_ref[...], NN) + b_ref[...]

    return pl.pallas_call(
        body, name="ada_fwd", out_shape=jax.ShapeDtypeStruct((16, w_sh.shape[1]), F32),
        compiler_params=pltpu.CompilerParams(vmem_limit_bytes=VMEM_LIMIT),
    )(c16, w_sh, b_sh)


def _ada_bwd(c16, g16, g16_sh, w_sh):
    ncol = w_sh.shape[1]

    def body(c_ref, g_ref, gs_ref, w_ref, dw_ref, db_ref, dc_ref):
        c = c_ref[...]
        s = _sig(c)
        gs = gs_ref[...]
        dw_ref[...] = _dot(c * s, gs, TN)
        db_ref[...] = jnp.broadcast_to(_colsum(g_ref[...]), db_ref.shape)
        odd = lax.broadcasted_iota(jnp.int32, gs.shape, 0) % 2 == 1
        gc = _colsum(jnp.where(odd, gs, 0.0))
        ds = _dot(jnp.broadcast_to(gc, (8, ncol)), w_ref[...], NT)
        c1 = c[1:2, :]
        s1 = s[1:2, :]
        dc_ref[...] = ds * (s1 * (1.0 + c1 * (1.0 - s1)))

    return pl.pallas_call(
        body, name="ada_bwd",
        out_shape=[jax.ShapeDtypeStruct(w_sh.shape, F32),
                   jax.ShapeDtypeStruct((8, g16.shape[1]), F32),
                   jax.ShapeDtypeStruct((8, D), F32)],
        compiler_params=pltpu.CompilerParams(vmem_limit_bytes=VMEM_LIMIT),
    )(c16, g16, g16_sh, w_sh)


def _ln0_fwd(x, ctx, g, b, modx, modc):
    L = x.shape[0]
    nt = L // TL

    def body(x_ref, c_ref, g_ref, b_ref, mx_ref, mc_ref, xn_ref, h_ref):
        isc = pl.program_id(0) == nt
        xin = jnp.where(isc, c_ref[...], x_ref[...])
        sh = jnp.where(isc, mc_ref[0:1, :], mx_ref[0:1, :])
        sc = jnp.where(isc, mc_ref[1:2, :], mx_ref[1:2, :])
        xhat, _ = _ln(xin)
        xn = xhat * g_ref[...] + b_ref[...]
        xn_ref[...] = xn
        h_ref[...] = (xn * (1.0 + sc) + sh).astype(h_ref.dtype)

    return pl.pallas_call(
        body, name="ln0_fwd", grid=(nt + 1,),
        in_specs=[_rtc(D, nt), _cst((TL, D)), _cst((1, D)), _cst((1, D)), _cst((8, D)), _cst((8, D))],
        out_specs=[_rt(D), _rt(D)],
        out_shape=[jax.ShapeDtypeStruct((L + TL, D), F32), jax.ShapeDtypeStruct((L + TL, D), _MXU)],
        compiler_params=_params(("parallel",)),
    )(x, ctx, g, b, modx, modc)


def _xbc_colblk(j):
    return jnp.where(j < 8, OXS // 128 + j, OB // 128 + j - 8)


def _conv_taps(p_ref, r0, first, last):
    main = p_ref[pl.ds(r0, TL), :]
    zero = jnp.zeros((8, main.shape[1]), F32)
    prev = zero if first else p_ref[pl.ds(r0 - 8, 8), :]
    nxt = zero if last else p_ref[pl.ds(r0 + TL, 8), :]
    ext = jnp.concatenate([prev, main, nxt], axis=0)
    n = TL + 16
    return [pltpu.roll(ext, (2 - k) % n, 0)[8:8 + TL] for k in range(5)]


def _seq_chunks(L):
    nt = L // TL
    return [(r * TL, r == 0, r == nt - 1) for r in range(nt)] + [(L, True, True)]


def _conv_fwd(p, conv_w8, conv_b):
    RT = p.shape[0]
    L = RT - TL
    chunks = _seq_chunks(L)

    def body(p_ref, w_ref, b_ref, o_ref):
        w = w_ref[...]
        bias = b_ref[...]
        for r0, first, last in chunks:
            taps = _conv_taps(p_ref, r0, first, last)
            pre = bias + sum(w[k:k + 1, :] * taps[k] for k in range(5))
            o_ref[pl.ds(r0, TL), :] = pre * _sig(pre)

    return pl.pallas_call(
        body, name="conv_fwd", grid=(12,),
        in_specs=[pl.BlockSpec((RT, 128), lambda j: (0, _xbc_colblk(j))),
                  pl.BlockSpec((8, 128), lambda j: (0, j)),
                  pl.BlockSpec((1, 128), lambda j: (0, j))],
        out_specs=pl.BlockSpec((RT, 128), lambda j: (0, j)),
        out_shape=jax.ShapeDtypeStruct((RT, 1536), F32),
        compiler_params=_params(("parallel",)),
    )(p, conv_w8, conv_b)


def _ssd_common(dtraw, dtb, a32, rev):
    dt = _softplus(dtraw + dtb)
    acum = _cumsum_rows(dt * a32, rev)
    ii = lax.broadcasted_iota(jnp.int32, (Q, Q), 0)
    jj = lax.broadcasted_iota(jnp.int32, (Q, Q), 1)
    mask = (ii <= jj) if rev else (ii >= jj)
    return dt, acum, acum.T, dt.T, mask


def _ssd_orders(ncl, ncc):
    nc = ncl + ncc

    def cf(s):
        return jnp.where(s < ncc, ncl + s, s - ncc)

    def cb(s):
        return nc - 1 - s

    return cf, cb


def _lane_bcast(v, ln):
    return jnp.broadcast_to(v[:, ln:ln + 1], v.shape)


def _halves(v, lo, axis):
    return jnp.concatenate([jnp.where(lo, v, 0.0), jnp.where(lo, 0.0, v)], axis=axis)


def _ssd2_fwd(xbc, p, prm):
    RT = xbc.shape[0]
    nc = RT // Q
    ncc = TL // Q
    cf, cb = _ssd_orders(nc - ncc, ncc)

    def one_dir(x_ref, dt_ref, prm_ref, y_ref, hp_ref, HT_ref, d):
        rev = d == 1
        a32 = -jnp.exp(prm_ref[1:2, :])
        dt, acum, acumT, dtT, mask = _ssd_common(dt_ref[...], prm_ref[0:1, :], a32, rev)
        end = 0 if rev else Q - 1
        lo = lax.broadcasted_iota(jnp.int32, (Q, 128), 1) < HP
        for g in range(2):
            Bg = x_ref[:, D + g * NS:D + (g + 1) * NS]
            Cg = x_ref[:, D + 2 * NS + g * NS:D + 2 * NS + (g + 1) * NS]
            CB = _dot(Cg, Bg, NT)
            xds, svs = [], []
            for q in range(HPG // 2):
                pi = g * (HPG // 2) + q
                ps = slice(pi * 128, (pi + 1) * 128)
                Xp = x_ref[:, ps]
                HTp = HT_ref[:, ps]
                lhs, dcs, sv = [], [], []
                ces = []
                for h in (2 * pi, 2 * pi + 1):
                    ln = 16 * d + h
                    colB = _lane_bcast(acum, ln)
                    rowv = acumT[ln:ln + 1, :]
                    aend = colB[end:end + 1, :]
                    Lm = jnp.exp(jnp.where(mask, colB - rowv, -1e30))
                    lhs.append(CB * Lm * dtT[ln:ln + 1, :])
                    ces.append(Cg * jnp.exp(colB))
                    dcs.append(jnp.exp(aend - colB) * _lane_bcast(dt, ln))
                    sv.append(jnp.exp(aend))
                lhs = jnp.concatenate(lhs + ces, axis=1)
                rhs = jnp.concatenate([_halves(Xp, lo, 0), _halves(HTp, lo, 0)], axis=0)
                y_ref[:, ps] = _dot(lhs, rhs, NN)
                xds.append(Xp * jnp.where(lo, dcs[0], dcs[1]))
                svs.append(jnp.where(lo[0:1, :], sv[0], sv[1]))
            gs = slice(g * 512, (g + 1) * 512)
            HTg = HT_ref[:, gs]
            hp_ref[0, :, gs] = HTg
            st = _dot(Bg.T, jnp.concatenate(xds, axis=1), NN)
            HT_ref[:, gs] = jnp.concatenate(svs, axis=1) * HTg + st

    def body(xf_ref, xb_ref, df_ref, db_ref, prm_ref, yf_ref, yb_ref, hf_ref, hb_ref, Hf, Hb):
        @pl.when(pl.program_id(0) == 0)
        def _():
            Hf[...] = jnp.zeros_like(Hf)
            Hb[...] = jnp.zeros_like(Hb)

        one_dir(xf_ref, df_ref, prm_ref, yf_ref, hf_ref, Hf, 0)
        one_dir(xb_ref, db_ref, prm_ref, yb_ref, hb_ref, Hb, 1)

    ysh = jax.ShapeDtypeStruct((RT, D), F32)
    hsh = jax.ShapeDtypeStruct((nc, NS, NH * HP), F32)
    hspec = pl.BlockSpec((1, NS, NH * HP), lambda s: (s, 0, 0))
    return pl.pallas_call(
        body, name="ssd_fwd", grid=(nc,),
        in_specs=[pl.BlockSpec((Q, 1536), lambda s: (cf(s), 0)),
                  pl.BlockSpec((Q, 1536), lambda s: (cb(s), 0)),
                  pl.BlockSpec((Q, 128), lambda s: (cf(s), ODT // 128)),
                  pl.BlockSpec((Q, 128), lambda s: (cb(s), ODT // 128)),
                  _cst((8, 128))],
        out_specs=[pl.BlockSpec((Q, D), lambda s: (cf(s), 0)),
                   pl.BlockSpec((Q, D), lambda s: (cb(s), 0)), hspec, hspec],
        out_shape=[ysh, ysh, hsh, hsh],
        scratch_shapes=[pltpu.VMEM((NS, NH * HP), F32), pltpu.VMEM((NS, NH * HP), F32)],
        compiler_params=_params(("arbitrary",)),
    )(xbc, xbc, p, p, prm)


def _ssd2_bwd(xbc, p, prm, dsk, dyd, hpf, hpb):
    RT = xbc.shape[0]
    nc = RT // Q
    ncc = TL // Q
    ncl = nc - ncc
    cf, cb = _ssd_orders(ncl, ncc)

    def rs(t):
        return nc - 1 - t

    def one_dir(x_ref, dt_ref, prm_ref, dsk_ref, dy_ref, is_ctx, hp_ref, dHT_ref,
                dx_ref, ddt_ref, st_ref, d):
        rev = d == 1
        a32 = -jnp.exp(prm_ref[1:2, :])
        dtraw = dt_ref[...]
        dtb = prm_ref[0:1, :]
        dt, acum, acumT, _, _ = _ssd_common(dtraw, dtb, a32, rev)
        end = 0 if rev else Q - 1
        lane = lax.broadcasted_iota(jnp.int32, (Q, 128), 1)
        srow = lax.broadcasted_iota(jnp.int32, (Q, 128), 0)
        maskT = (lane <= srow) if rev else (lane >= srow)
        lo = lane < HP
        lo1 = lo[0:1, :]
        dyscale = jnp.where(is_ctx, 0.0, 1.0)
        c_dacum = jnp.zeros((Q, 128), F32)
        r_dacum = jnp.zeros((Q, 128), F32)
        c_ddt = jnp.zeros((Q, 128), F32)
        dskacc = jnp.zeros((1, 128), F32)
        for g in range(2):
            gs = slice(g * 512, (g + 1) * 512)
            Bg = x_ref[:, D + g * NS:D + (g + 1) * NS]
            Cg = x_ref[:, D + 2 * NS + g * NS:D + 2 * NS + (g + 1) * NS]
            CBT = _dot(Bg, Cg, NT)
            HTg = hp_ref[0, :, gs]
            dHTg = dHT_ref[:, gs]
            BdHg = _dot(Bg, dHTg, NN)
            dCBT = jnp.zeros((Q, Q), F32)
            dCg = jnp.zeros((Q, NS), F32)
            xds, dyes, svs = [], [], []
            for q in range(HPG // 2):
                pi = g * (HPG // 2) + q
                ps = slice(pi * 128, (pi + 1) * 128)
                qs = slice(q * 128, (q + 1) * 128)
                Xp = x_ref[:, ps]
                dYp = dy_ref[:, ps] * dyscale
                HTp = HTg[:, qs]
                BdHp = BdHg[:, qs]
                dY2 = _halves(dYp, lo, 0)
                dWT2 = _dot(_halves(Xp, lo, 0), dYp.T, NN)
                G2 = _dot(dY2, HTp, NT)
                XB = Xp * BdHp
                hh = _colsum(dHTg[:, qs] * HTp)
                yx = _colsum(dYp * Xp)
                wts, dcs, ebs, sv = [], [], [], []
                for k, h in enumerate((2 * pi, 2 * pi + 1)):
                    ln = 16 * d + h
                    half = lo if k == 0 else jnp.logical_not(lo)
                    half1 = half[0:1, :]
                    colB = _lane_bcast(acum, ln)
                    dtcB = _lane_bcast(dt, ln)
                    rowv = acumT[ln:ln + 1, :]
                    aend = colB[end:end + 1, :]
                    LmT = jnp.exp(jnp.where(maskT, rowv - colB, -1e30))
                    WT = CBT * LmT * dtcB
                    dWT = dWT2[k * Q:(k + 1) * Q, :]
                    U = dWT * LmT
                    MT = U * CBT
                    rM = jnp.sum(MT, axis=1, keepdims=True)
                    rT = _colsum(MT * dtcB)
                    dCBT = dCBT + U * dtcB
                    ecol = jnp.exp(aend - colB)
                    EB = jnp.exp(colB)
                    Gk = G2[k * Q:(k + 1) * Q, :]
                    dCg = dCg + EB * Gk
                    qcol = jnp.sum(EB * Gk * Cg, axis=1, keepdims=True)
                    xb = jnp.sum(jnp.where(half, XB, 0.0), axis=1, keepdims=True)
                    e1 = ecol[:, 0:1]
                    dt1 = dtcB[:, 0:1]
                    scol = e1 * dt1 * xb
                    sA = jnp.exp(aend)
                    eterm = sA[:, 0:1] * jnp.sum(jnp.where(half1, hh, 0.0), axis=1, keepdims=True) \
                        + _colsum(scol)
                    cvec = qcol - dt1 * rM - scol + jnp.where(srow[:, 0:1] == end, eterm, 0.0)
                    c_dacum = jnp.where(lane == ln, cvec, c_dacum)
                    r_dacum = jnp.where(srow == ln, rT, r_dacum)
                    c_ddt = jnp.where(lane == ln, rM + e1 * xb, c_ddt)
                    if d == 0:
                        dskacc = dskacc + jnp.where(
                            lane[0:1, :] == h, jnp.sum(jnp.where(half1, yx, 0.0), axis=1, keepdims=True), 0.0)
                    wts.append(WT)
                    dcs.append(ecol * dtcB)
                    ebs.append(EB)
                    sv.append(sA)
                dcp = jnp.where(lo, dcs[0], dcs[1])
                dX = _dot(jnp.concatenate(wts, axis=1), dY2, NN) + dcp * BdHp
                if d == 0:
                    dX = dX + dYp * dsk_ref[:, ps]
                dx_ref[:, ps] = dX
                xds.append(Xp * dcp)
                dyes.append(dYp * jnp.where(lo, ebs[0], ebs[1]))
                svs.append(jnp.where(lo1, sv[0], sv[1]))
            dx_ref[:, D + g * NS:D + (g + 1) * NS] = (
                _dot(jnp.concatenate(xds, axis=1), dHTg, NT) + _dot(dCBT, Cg, NN))
            dx_ref[:, D + 2 * NS + g * NS:D + 2 * NS + (g + 1) * NS] = dCg + _dot(dCBT, Bg, TN)
            dHT_ref[:, gs] = (jnp.concatenate(svs, axis=1) * dHTg
                              + _dot(Cg.T, jnp.concatenate(dyes, axis=1), NN))
        dacum = c_dacum + r_dacum.T
        da = _cumsum_rows(dacum, not rev)
        mine = (lane >= 16 * d) & (lane < 16 * d + 16)
        ddt = jnp.where(mine, c_ddt + da * a32, 0.0)
        ddt_ref[...] = ddt * _sig(dtraw + dtb)
        st_ref[0:1, :] += _colsum(jnp.where(mine, da * dt, 0.0))
        if d == 0:
            st_ref[1:2, :] += dskacc

    def body(xf_ref, xb_ref, df_ref, db_ref, prm_ref, dsk_ref, dyf_ref, dyb_ref, hf_ref, hb_ref,
             dxf_ref, dxb_ref, ddf_ref, ddb_ref, st_ref, dHf, dHb):
        t = pl.program_id(0)

        @pl.when(t == 0)
        def _():
            dHf[...] = jnp.zeros_like(dHf)
            dHb[...] = jnp.zeros_like(dHb)
            st_ref[...] = jnp.zeros_like(st_ref)

        s = rs(t)
        one_dir(xf_ref, df_ref, prm_ref, dsk_ref, dyf_ref, cf(s) >= ncl, hf_ref, dHf,
                dxf_ref, ddf_ref, st_ref, 0)
        one_dir(xb_ref, db_ref, prm_ref, dsk_ref, dyb_ref, cb(s) >= ncl, hb_ref, dHb,
                dxb_ref, ddb_ref, st_ref, 1)

        @pl.when(t == nc - 1)
        def _():
            st_ref[0:1, :] = -jnp.exp(prm_ref[1:2, :]) * st_ref[0:1, :]

    def lat(c):
        return jnp.minimum(c, ncl - 1)

    xsh = jax.ShapeDtypeStruct((RT, 1536), F32)
    dsh = jax.ShapeDtypeStruct((RT, 128), F32)
    hspec = pl.BlockSpec((1, NS, NH * HP), lambda t: (rs(t), 0, 0))
    return pl.pallas_call(
        body, name="ssd_bwd", grid=(nc,),
        in_specs=[pl.BlockSpec((Q, 1536), lambda t: (cf(rs(t)), 0)),
                  pl.BlockSpec((Q, 1536), lambda t: (cb(rs(t)), 0)),
                  pl.BlockSpec((Q, 128), lambda t: (cf(rs(t)), ODT // 128)),
                  pl.BlockSpec((Q, 128), lambda t: (cb(rs(t)), ODT // 128)),
                  _cst((8, 128)), _cst((1, D)),
                  pl.BlockSpec((Q, D), lambda t: (lat(cf(rs(t))), 0)),
                  pl.BlockSpec((Q, D), lambda t: (lat(cb(rs(t))), 0)),
                  hspec, hspec],
        out_specs=[pl.BlockSpec((Q, 1536), lambda t: (cf(rs(t)), 0)),
                   pl.BlockSpec((Q, 1536), lambda t: (cb(rs(t)), 0)),
                   pl.BlockSpec((Q, 128), lambda t: (cf(rs(t)), 0)),
                   pl.BlockSpec((Q, 128), lambda t: (cb(rs(t)), 0)),
                   _cst((8, 128))],
        out_shape=[xsh, xsh, dsh, dsh, jax.ShapeDtypeStruct((8, 128), F32)],
        scratch_shapes=[pltpu.VMEM((NS, NH * HP), F32), pltpu.VMEM((NS, NH * HP), F32)],
        compiler_params=_params(("arbitrary",)),
    )(xbc, xbc, p, p, prm, dsk, dyd, dyd, hpf, hpb)


def _mix_fwd_vals(yf, yb, z, xs, u, v, dsk, sg, gg, gb):
    y = yf + yb + xs * dsk
    sz = _sig(z)
    hh = y * z * sz
    r = lax.rsqrt(jnp.mean(hh * hh, axis=-1, keepdims=True) + EPS)
    nh = hh * r
    ug, tu = _gelu(u)
    vg, tv = _gelu(v)
    vhat, vrstd = _ln(vg)
    vn = vhat * gg + gb
    return y, sz, r, nh, ug, tu, vg, tv, vhat, vrstd, vn


def _mix_fwd(yf, yb, p, xbc, dsk, sg, gg, gb, ws, bsT):
    L = yf.shape[0] - TL
    nt = L // TL

    def body(yf_ref, yb_ref, z_ref, xs_ref, u_ref, v_ref, dsk_ref, sg_ref, gg_ref, gb_ref,
             ws_ref, bs_ref, ys_ref, ym_ref):
        _, _, _, nh, ug, _, _, _, _, _, vn = _mix_fwd_vals(
            yf_ref[...], yb_ref[...], z_ref[...], xs_ref[...], u_ref[...], v_ref[...],
            dsk_ref[...], sg_ref[...], gg_ref[...], gb_ref[...])
        ys_ref[...] = (nh * sg_ref[...]).astype(ys_ref.dtype)
        for n in range(TL // Q):
            rs_ = slice(n * Q, (n + 1) * Q)
            for g in range(8):
                cs = slice(g * 128, (g + 1) * 128)
                mixed = _dot(ws_ref[g], vn[rs_, cs], NN) + bs_ref[:, g:g + 1]
                ym_ref[rs_, cs] = (ug[rs_, cs] * mixed).astype(ym_ref.dtype)

    return pl.pallas_call(
        body, name="mix_fwd", grid=(nt,),
        in_specs=[_rt(D), _rt(D), _rt(D, OZ // D), _rt(D, 0), _rt(D, OU // D), _rt(D, OV // D),
                  _cst((1, D)), _cst((1, D)), _cst((1, D)), _cst((1, D)),
                  _cst((8, 128, 128)), _cst((128, 128))],
        out_specs=[_rt(D), _rt(D)],
        out_shape=[jax.ShapeDtypeStruct((L, D), _MXU), jax.ShapeDtypeStruct((L, D), _MXU)],
        compiler_params=_params(("parallel",)),
    )(yf, yb, p, xbc, p, p, dsk, sg, gg, gb, ws, bsT)


def _mix_bwd(dys, dym, yf, yb, p, xbc, dp, dsk, sg, gg, gb, ws, bsT):
    L = dys.shape[0]
    nt = L // TL

    def body(dys_ref, dym_ref, yf_ref, yb_ref, z_ref, xs_ref, u_ref, v_ref, dsk_ref, sg_ref,
             gg_ref, gb_ref, ws_ref, bs_ref, dp_any, dzuv_ref, dy_ref, st_ref,
             dws_ref, dbs_ref, dvn_s):
        del dp_any
        dz_ref = dzuv_ref.at[:, OZ:OZ + D]
        du_ref = dzuv_ref.at[:, OU:OU + D]
        dv_ref = dzuv_ref.at[:, OV:OV + D]

        @pl.when(pl.program_id(0) == 0)
        def _():
            st_ref[...] = jnp.zeros_like(st_ref)
            dws_ref[...] = jnp.zeros_like(dws_ref)
            dbs_ref[...] = jnp.zeros_like(dbs_ref)

        z = z_ref[...]
        u = u_ref[...]
        v = v_ref[...]
        y, sz, r, nh, ug, tu, vg, tv, vhat, vrstd, vn = _mix_fwd_vals(
            yf_ref[...], yb_ref[...], z, xs_ref[...], u, v,
            dsk_ref[...], sg_ref[...], gg_ref[...], gb_ref[...])
        dys = dys_ref[...]
        st_ref[0:1, :] += _colsum(dys * nh)
        dn = dys * sg_ref[...]
        dhh = r * (dn - nh * jnp.mean(dn * nh, axis=-1, keepdims=True))
        dy_ref[...] = dhh * z * sz
        dz_ref[...] = (dhh * y * (sz * (1.0 + z * (1.0 - sz)))).astype(dz_ref.dtype)
        dym = dym_ref[...]
        lane = lax.broadcasted_iota(jnp.int32, (Q, 128), 1)
        dbs = jnp.zeros((Q, 128), F32)
        gu = _gelu_grad(u, tu)
        for n in range(TL // Q):
            rs_ = slice(n * Q, (n + 1) * Q)
            for g in range(8):
                cs = slice(g * 128, (g + 1) * 128)
                vb = vn[rs_, cs]
                mixed = _dot(ws_ref[g], vb, NN) + bs_ref[:, g:g + 1]
                dyb = dym[rs_, cs]
                dmx = dyb * ug[rs_, cs]
                du_ref[rs_, cs] = (dyb * mixed * gu[rs_, cs]).astype(du_ref.dtype)
                dvn_s[rs_, cs] = _dot(ws_ref[g], dmx, TN)
                dws_ref[g] += _dot(dmx, vb, NT)
                dbs = dbs + jnp.where(lane == g, jnp.sum(dmx, axis=1, keepdims=True), 0.0)
        dbs_ref[...] += dbs
        dvn = dvn_s[...]
        st_ref[1:2, :] += _colsum(dvn * vhat)
        st_ref[2:3, :] += _colsum(dvn)
        dvg = _ln_bwd(dvn * gg_ref[...], vhat, vrstd)
        dv_ref[...] = (dvg * _gelu_grad(v, tv)).astype(dv_ref.dtype)

    outs = pl.pallas_call(
        body, name="mix_bwd", grid=(nt,),
        in_specs=[_rt(D), _rt(D), _rt(D), _rt(D), _rt(D, OZ // D), _rt(D, 0), _rt(D, OU // D),
                  _rt(D, OV // D), _cst((1, D)), _cst((1, D)), _cst((1, D)), _cst((1, D)),
                  _cst((8, 128, 128)), _cst((128, 128)), pl.BlockSpec(memory_space=pl.ANY)],
        out_specs=[_rt(3 * D, 0), _rt(D), _cst((8, D)),
                   _cst((8, 128, 128)), _cst((128, 128))],
        out_shape=[jax.ShapeDtypeStruct(dp.shape, dp.dtype),
                   jax.ShapeDtypeStruct((L, D), F32), jax.ShapeDtypeStruct((8, D), F32),
                   jax.ShapeDtypeStruct((8, 128, 128), F32), jax.ShapeDtypeStruct((128, 128), F32)],
        scratch_shapes=[pltpu.VMEM((TL, D), F32)],
        input_output_aliases={14: 0},
        compiler_params=_params(("arbitrary",)),
    )(dys, dym, yf, yb, p, xbc, p, p, dsk, sg, gg, gb, ws, bsT, dp)
    return outs


def _merge_fwd(yssd, ygm, p, bg, ws, wg, wo, xn, modx, g1, b1):
    L = yssd.shape[0]
    tm = TL

    def body(ys_ref, yg_ref, g_ref, bg_ref, ws_ref, wg_ref, wo_ref, xn_ref, mx_ref, g1_ref, b1_ref,
             a1_ref, a2_ref, m_ref, o_ref, r1_ref, h2_ref):
        a1 = _dot(ys_ref[...], ws_ref[...], NN)
        a2 = _dot(yg_ref[...], wg_ref[...], NN)
        gt = _sig(g_ref[...] + bg_ref[...])
        mg = gt[:, :D] * a1 + gt[:, D:] * a2
        a1_ref[...] = a1
        a2_ref[...] = a2
        m_ref[...] = mg.astype(m_ref.dtype)
        out = _dot(mg, wo_ref[...], NN)
        o_ref[...] = out
        r1 = ALPHA * xn_ref[...] + mx_ref[2:3, :] * out
        xhat, _ = _ln(r1)
        x1 = xhat * g1_ref[...] + b1_ref[...]
        r1_ref[...] = r1
        h2_ref[...] = (x1 * (1.0 + mx_ref[4:5, :]) + mx_ref[3:4, :]).astype(h2_ref.dtype)

    rows = pl.BlockSpec((tm, D), lambda i: (i, 0))
    f32s = jax.ShapeDtypeStruct((L, D), F32)
    mxus = jax.ShapeDtypeStruct((L, D), _MXU)
    return pl.pallas_call(
        body, name="merge_fwd", grid=(L // tm,),
        in_specs=[rows, rows, pl.BlockSpec((tm, 2 * D), lambda i: (i, OG // (2 * D))), _cst((1, 2 * D)),
                  _cst((D, D)), _cst((D, D)), _cst((D, D)), rows, _cst((8, D)), _cst((1, D)), _cst((1, D))],
        out_specs=[rows] * 6,
        out_shape=[f32s, f32s, mxus, f32s, f32s, mxus],
        compiler_params=_params(("parallel",)),
    )(yssd, ygm, p, bg, ws, wg, wo, xn, modx, g1, b1)


def _mm_o2_res2(ff, w2, r1, tgt, modx, g1, b1, g2, b2):
    L, K = ff.shape
    tm = 2 * TL

    def body(a_ref, b_ref, r1_ref, t_ref, mx_ref, g1_ref, b1_ref, g2_ref, b2_ref,
             dr2_ref, do2_ref, st_ref, loss_ref):
        @pl.when(pl.program_id(0) == 0)
        def _():
            st_ref[...] = jnp.zeros_like(st_ref)
            loss_ref[...] = jnp.zeros_like(loss_ref)

        o2 = _dot(a_ref[...], b_ref[...], NN)
        xh1, _ = _ln(r1_ref[...])
        x1 = xh1 * g1_ref[...] + b1_ref[...]
        g2x = mx_ref[5:6, :]
        xh2, rstd2 = _ln(ALPHA * x1 + g2x * o2)
        err = xh2 * g2_ref[...] + b2_ref[...] - t_ref[...]
        per_tok = jnp.mean(err * err, axis=-1, keepdims=True)
        loss_ref[...] += 0.5 * jnp.sum(per_tok, axis=0, keepdims=True)
        dy = err * (1.0 / D)
        st_ref[0:1, :] += _colsum(dy * xh2)
        st_ref[1:2, :] += _colsum(dy)
        dr2 = _ln_bwd(dy * g2_ref[...], xh2, rstd2)
        st_ref[2:3, :] += _colsum(dr2 * o2)
        dr2_ref[...] = dr2
        do2_ref[...] = (g2x * dr2).astype(do2_ref.dtype)

    return pl.pallas_call(
        body, name="mm_o2_res2", grid=(L // tm,),
        in_specs=[_rt(K, rows=tm), _cst((K, D)), _rt(D, rows=tm), _rt(D, rows=tm), _cst((8, D))]
        + [_cst((1, D))] * 4,
        out_specs=[_rt(D, rows=tm), _rt(D, rows=tm), _cst((8, D)), _cst((8, 128))],
        out_shape=[jax.ShapeDtypeStruct((L, D), F32), jax.ShapeDtypeStruct((L, D), _MXU),
                   jax.ShapeDtypeStruct((8, D), F32), jax.ShapeDtypeStruct((8, 128), F32)],
        compiler_params=_params(("arbitrary",)),
    )(ff, w2, r1, tgt, modx, g1, b1, g2, b2)


def _mm_dh2_res1bwd(df13, w13i, dr2, r1, out, modx, g1, b1):
    L, K = df13.shape

    def body(a_ref, b_ref, dr2_ref, r1_ref, o_ref, mx_ref, g_ref, bb_ref, dr1_ref, do_ref, st_ref):
        @pl.when(pl.program_id(0) == 0)
        def _():
            st_ref[...] = jnp.zeros_like(st_ref)

        dh2 = _dot(a_ref[...], b_ref[...], NN)
        xh1, rstd1 = _ln(r1_ref[...])
        x1 = xh1 * g_ref[...] + bb_ref[...]
        dx1 = ALPHA * dr2_ref[...] + dh2 * (1.0 + mx_ref[4:5, :])
        st_ref[0:1, :] += _colsum(dh2 * x1)
        st_ref[1:2, :] += _colsum(dh2)
        st_ref[2:3, :] += _colsum(dx1 * xh1)
        st_ref[3:4, :] += _colsum(dx1)
        dr1 = _ln_bwd(dx1 * g_ref[...], xh1, rstd1)
        st_ref[4:5, :] += _colsum(dr1 * o_ref[...])
        dr1_ref[...] = dr1
        do_ref[...] = (mx_ref[2:3, :] * dr1).astype(do_ref.dtype)

    return pl.pallas_call(
        body, name="mm_dh2_res1bwd", grid=(L // TL,),
        in_specs=[_rt(K), _cst((K, D)), _rt(D), _rt(D), _rt(D), _cst((8, D)), _cst((1, D)), _cst((1, D))],
        out_specs=[_rt(D), _rt(D), _cst((8, D))],
        out_shape=[jax.ShapeDtypeStruct((L, D), F32), jax.ShapeDtypeStruct((L, D), _MXU),
                   jax.ShapeDtypeStruct((8, D), F32)],
        compiler_params=_params(("arbitrary",)),
    )(df13, w13i, dr2, r1, out, modx, g1, b1)


def _merge_bwd(dout, a1, a2, p, bg, wo, ws, wg, dp):
    L = a1.shape[0]
    tm = TL

    def body(do_ref, a1_ref, a2_ref, g_ref, bg_ref, wo_ref, ws_ref, wg_ref, dp_any,
             dg_ref, da1_ref, da2_ref, st_ref, dys_ref, dym_ref):
        del dp_any

        @pl.when(pl.program_id(0) == 0)
        def _():
            st_ref[...] = jnp.zeros_like(st_ref)

        dm = _dot(do_ref[...], wo_ref[...], NT)
        gt = _sig(g_ref[...] + bg_ref[...])
        g1 = gt[:, :D]
        g2 = gt[:, D:]
        da1 = (dm * g1).astype(da1_ref.dtype)
        da2 = (dm * g2).astype(da2_ref.dtype)
        da1_ref[...] = da1
        da2_ref[...] = da2
        dg1 = dm * a1_ref[...] * g1 * (1.0 - g1)
        dg2 = dm * a2_ref[...] * g2 * (1.0 - g2)
        st_ref[0:1, 0:D] += _colsum(dg1)
        st_ref[0:1, D:2 * D] += _colsum(dg2)
        dg_ref[:, 0:D] = dg1.astype(dg_ref.dtype)
        dg_ref[:, D:2 * D] = dg2.astype(dg_ref.dtype)
        dys_ref[...] = _dot(da1, ws_ref[...], NT)
        dym_ref[...] = _dot(da2, wg_ref[...], NT)

    rows = pl.BlockSpec((tm, D), lambda i: (i, 0))
    gates = pl.BlockSpec((tm, 2 * D), lambda i: (i, OG // (2 * D)))
    f32s = jax.ShapeDtypeStruct((L, D), F32)
    mxus = jax.ShapeDtypeStruct((L, D), _MXU)
    return pl.pallas_call(
        body, name="merge_bwd", grid=(L // tm,),
        in_specs=[rows, rows, rows, gates, _cst((1, 2 * D)), _cst((D, D)), _cst((D, D)), _cst((D, D)),
                  pl.BlockSpec(memory_space=pl.ANY)],
        out_specs=[gates, rows, rows, _cst((8, 2 * D)), rows, rows],
        out_shape=[jax.ShapeDtypeStruct(dp.shape, dp.dtype), mxus, mxus,
                   jax.ShapeDtypeStruct((8, 2 * D), F32), f32s, f32s],
        input_output_aliases={8: 0},
        compiler_params=_params(("arbitrary",)),
    )(dout, a1, a2, p, bg, wo, ws, wg, dp)


HFF = DFF // 2


def _mm_f13_glu(h2, w13i):
    L = h2.shape[0]
    tm = 512

    def body(a_ref, b_ref, f_ref, ff_ref):
        f = _dot(a_ref[...], b_ref[...], NT)
        f_ref[...] = f
        f1 = f[:, :HFF]
        ff_ref[...] = (f1 * _sig(f1) * f[:, HFF:]).astype(ff_ref.dtype)

    return pl.pallas_call(
        body, name="mm_f13_glu", grid=(DFF // HFF, L // tm),
        in_specs=[pl.BlockSpec((tm, D), lambda j, i: (i, 0)), pl.BlockSpec((2 * HFF, D), lambda j, i: (j, 0))],
        out_specs=[pl.BlockSpec((tm, 2 * HFF), lambda j, i: (i, j)), pl.BlockSpec((tm, HFF), lambda j, i: (i, j))],
        out_shape=[jax.ShapeDtypeStruct((L, 2 * DFF), F32), jax.ShapeDtypeStruct((L, DFF), _MXU)],
        compiler_params=_params(("parallel", "parallel")),
    )(h2, w13i)


def _mm_dff_glu(do2, w_ff2_f, f13i):
    L = do2.shape[0]
    tm = 512

    def body(a_ref, b_ref, f_ref, o_ref):
        d = _dot(a_ref[...], b_ref[...], NT)
        f1 = f_ref[:, :HFF]
        s = _sig(f1)
        o_ref[:, :HFF] = (d * f_ref[:, HFF:] * (s * (1.0 + f1 * (1.0 - s)))).astype(o_ref.dtype)
        o_ref[:, HFF:] = (d * f1 * s).astype(o_ref.dtype)

    return pl.pallas_call(
        body, name="mm_dff_glu", grid=(DFF // HFF, L // tm),
        in_specs=[pl.BlockSpec((tm, D), lambda j, i: (i, 0)), pl.BlockSpec((HFF, D), lambda j, i: (j, 0)),
                  pl.BlockSpec((tm, 2 * HFF), lambda j, i: (i, j))],
        out_specs=pl.BlockSpec((tm, 2 * HFF), lambda j, i: (i, j)),
        out_shape=jax.ShapeDtypeStruct((L, 2 * DFF), _MXU),
        compiler_params=_params(("parallel", "parallel")),
    )(do2, w_ff2_f, f13i)


def _conv_bwd(dxf, dxb, p, conv_w8, conv_b, dp):
    RT = p.shape[0]
    chunks = _seq_chunks(RT - TL)

    def body(df_ref, db_ref, p_ref, w_ref, b_ref, dp_any, o_ref, dw_ref, dbias_ref, dpre_s):
        del dp_any
        w = w_ref[...]
        bias = b_ref[...]
        srow = lax.broadcasted_iota(jnp.int32, (8, 128), 0)
        dwacc = jnp.zeros((8, 128), F32)
        dbacc = jnp.zeros((1, 128), F32)
        for r0, first, last in chunks:
            taps = _conv_taps(p_ref, r0, first, last)
            pre = bias + sum(w[k:k + 1, :] * taps[k] for k in range(5))
            s = _sig(pre)
            dpre = (df_ref[pl.ds(r0, TL), :] + db_ref[pl.ds(r0, TL), :]) * (s * (1.0 + pre * (1.0 - s)))
            dpre_s[pl.ds(r0, TL), :] = dpre
            dbacc = dbacc + _colsum(dpre)
            for k in range(5):
                dwacc = dwacc + jnp.where(srow == k, _colsum(dpre * taps[k]), 0.0)
        for r0, first, last in chunks:
            taps = _conv_taps(dpre_s, r0, first, last)
            dx = sum(w[k:k + 1, :] * taps[4 - k] for k in range(5))
            o_ref[pl.ds(r0, TL), :] = dx.astype(o_ref.dtype)
        dw_ref[...] = dwacc
        dbias_ref[...] = jnp.broadcast_to(dbacc, (8, 128))

    cspec = pl.BlockSpec((RT, 128), lambda j: (0, j))
    wspec = pl.BlockSpec((8, 128), lambda j: (0, j))
    return pl.pallas_call(
        body, name="conv_bwd", grid=(12,),
        in_specs=[cspec, cspec, pl.BlockSpec((RT, 128), lambda j: (0, _xbc_colblk(j))),
                  wspec, pl.BlockSpec((1, 128), lambda j: (0, j)), pl.BlockSpec(memory_space=pl.ANY)],
        out_specs=[pl.BlockSpec((RT, 128), lambda j: (0, _xbc_colblk(j))), wspec, wspec],
        out_shape=[jax.ShapeDtypeStruct(dp.shape, dp.dtype), jax.ShapeDtypeStruct((8, 1536), F32),
                   jax.ShapeDtypeStruct((8, 1536), F32)],
        scratch_shapes=[pltpu.VMEM((RT, 128), F32)],
        input_output_aliases={5: 0},
        compiler_params=_params(("parallel",)),
    )(dxf, dxb, p, conv_w8, conv_b, dp)


def _dp_fill(dp):
    nrow = dp.shape[0] // TL

    def body(dp_any, o_ref):
        del dp_any
        o_ref[...] = jnp.zeros_like(o_ref)

    return pl.pallas_call(
        body, name="dp_fill", grid=(1,),
        in_specs=[pl.BlockSpec(memory_space=pl.ANY)],
        out_specs=pl.BlockSpec((TL, OB), lambda i: (nrow - 1, 0)),
        out_shape=jax.ShapeDtypeStruct(dp.shape, dp.dtype),
        input_output_aliases={0: 0},
        compiler_params=_params(("arbitrary",)),
    )(dp)


def _dt_bwd(ddf, ddb, dp):
    RT = ddf.shape[0]
    assert NPJ - ODT == 256

    def body(f_ref, b_ref, dp_any, o_ref, st_ref):
        del dp_any

        @pl.when(pl.program_id(0) == 0)
        def _():
            st_ref[...] = jnp.zeros_like(st_ref)

        s = f_ref[...] + b_ref[...]
        o_ref[:, 0:128] = s.astype(o_ref.dtype)
        o_ref[:, 128:256] = jnp.zeros((TL, 128), o_ref.dtype)
        st_ref[0:1, :] += _colsum(s)

    return pl.pallas_call(
        body, name="dt_bwd", grid=(RT // TL,),
        in_specs=[_rt(128), _rt(128), pl.BlockSpec(memory_space=pl.ANY)],
        out_specs=[_rt(256, ODT // 256), _cst((8, 128))],
        out_shape=[jax.ShapeDtypeStruct(dp.shape, dp.dtype), jax.ShapeDtypeStruct((8, 128), F32)],
        input_output_aliases={2: 0},
        compiler_params=_params(("arbitrary",)),
    )(ddf, ddb, dp)


def _mm_dh1_ln0bwd(dp, w_in_p, dr1, x, ctx, g, b, modx, modc):
    L = x.shape[0]
    nt = L // TL

    def body(dp_ref, w_ref, dr1_ref, x_ref, c_ref, g_ref, b_ref, mx_ref, mc_ref, gx_ref, st_ref):
        i = pl.program_id(0)
        isc = i == nt

        @pl.when(i == 0)
        def _():
            st_ref[...] = jnp.zeros_like(st_ref)

        xin = jnp.where(isc, c_ref[...], x_ref[...])
        xhat, rstd = _ln(xin)
        xn = xhat * g_ref[...] + b_ref[...]
        sc = jnp.where(isc, mc_ref[1:2, :], mx_ref[1:2, :])
        dh = _dot(dp_ref[...], w_ref[...], NT)
        lat = jnp.where(isc, 0.0, 1.0)
        dxn = dh * (1.0 + sc) + (lat * ALPHA) * dr1_ref[...]
        tsh = _colsum(dh)
        tsc = _colsum(dh * xn)
        st_ref[0:1, :] += lat * tsh
        st_ref[1:2, :] += lat * tsc
        st_ref[2:3, :] += (1.0 - lat) * tsh
        st_ref[3:4, :] += (1.0 - lat) * tsc
        st_ref[4:5, :] += _colsum(dxn * xhat)
        st_ref[5:6, :] += _colsum(dxn)

        @pl.when(i < nt)
        def _():
            gx_ref[...] = _ln_bwd(dxn * g_ref[...], xhat, rstd)

    return pl.pallas_call(
        body, name="mm_dh1_ln0bwd", grid=(nt + 1,),
        in_specs=[_rt(NPJ), _cst((D, NPJ)), _rtc(D, nt), _rtc(D, nt), _cst((TL, D)), _cst((1, D)), _cst((1, D)),
                  _cst((8, D)), _cst((8, D))],
        out_specs=[_rtc(D, nt), _cst((8, D))],
        out_shape=[jax.ShapeDtypeStruct((L, D), F32), jax.ShapeDtypeStruct((8, D), F32)],
        compiler_params=pltpu.CompilerParams(dimension_semantics=("arbitrary",),
                                             vmem_limit_bytes=VMEM_LIMIT + 8 * 1024 * 1024),
    )(dp, w_in_p, dr1, x, ctx, g, b, modx, modc)


SECTIONS = ((0, 1024, OZ), (1024, 2048, OXS), (2048, 2304, OB), (2304, 2560, OC), (2560, 2592, ODT),
            (2592, 3616, OU), (3616, 4640, OV), (4640, 6688, OG))


def _perm_from_blocks(ga):
    n = ga.shape[2]
    pieces = []
    for na, nb, _ in sorted(SECTIONS, key=lambda sec: sec[2]):
        for k in range(NDEV):
            lo, hi = max(na, k * n), min(nb, (k + 1) * n)
            if lo < hi:
                pieces.append(ga[k][:, lo - k * n:hi - k * n])
    pieces.append(jnp.zeros((ga.shape[1], NPJ - NNAT), ga.dtype))
    return jnp.concatenate(pieces, axis=1)


def _blocks_from_perm(gp, n):
    blocks = []
    for k in range(NDEV):
        pieces = []
        for na, nb, po in SECTIONS:
            lo, hi = max(na, k * n), min(nb, (k + 1) * n)
            if lo < hi:
                pieces.append(gp[:, po + lo - na:po + hi - na])
        blocks.append(jnp.concatenate(pieces, axis=1))
    return jnp.stack(blocks)


def _padded(n, row_align):
    unit = row_align * D
    return -(-n // unit) * unit if row_align else n


def _slab(arrs, rows, row_align=0):
    parts = []
    for a in arrs:
        f = a.reshape(-1)
        parts.append(jnp.pad(f, (0, _padded(f.shape[0], row_align) - f.shape[0])))
    flat = jnp.concatenate(parts)
    flat = jnp.pad(flat, (0, rows * D - flat.shape[0]))
    return flat.reshape(rows, D)


def _unslab(slab, shapes, row_align=0):
    out, off = [], 0
    for shp in shapes:
        n = 1
        for s in shp:
            n *= s
        r0, r1 = off // D, -(-(off + n) // D)
        out.append(slab[r0:r1].reshape(-1)[off - r0 * D:off - r0 * D + n].reshape(shp))
        off += _padded(n, row_align)
    return out


def _row(v):
    return v.reshape(1, -1)


def _t(a):
    return jnp.swapaxes(a, 0, 1)


def _pad_rows(a, rows):
    return jnp.pad(a, ((0, rows - a.shape[0]), (0, 0)))


REPL = ["c_ctx", "ln0_g", "ln0_b", "b_ada", "conv_b", "dt_bias", "a_log", "d_skip", "ssd_norm_g",
        "gm_norm_g", "gm_norm_b", "w_spatial", "b_spatial", "b_gate", "ln1_g", "ln1_b", "ln2_g", "ln2_b"]
SMALL_ROWS = 160
WEIGHTS = ["c_ctx", "ln0_g", "ln0_b", "w_ada", "b_ada", "w_in", "conv_w", "conv_b", "dt_bias", "a_log",
           "d_skip", "ssd_norm_g", "gm_norm_g", "gm_norm_b", "w_spatial", "b_spatial", "b_gate",
           "w_ssd_proj", "w_gm_proj", "w_out", "ln1_g", "ln1_b", "w_ff1", "w_ff3", "w_ff2", "ln2_g", "ln2_b"]


def kernel(x, c, ctx, c_ctx, ln0_g, ln0_b, w_ada, b_ada, w_in, conv_w, conv_b, dt_bias, a_log, d_skip, ssd_norm_g, gm_norm_g, gm_norm_b, w_spatial, b_spatial, b_gate, w_ssd_proj, w_gm_proj, w_out, ln1_g, ln1_b, w_ff1, w_ff3, w_ff2, ln2_g, ln2_b, loss_target, m_c_ctx, m_ln0_g, m_ln0_b, m_w_ada, m_b_ada, m_w_in, m_conv_w, m_conv_b, m_dt_bias, m_a_log, m_d_skip, m_ssd_norm_g, m_gm_norm_g, m_gm_norm_b, m_w_spatial, m_b_spatial, m_b_gate, m_w_ssd_proj, m_w_gm_proj, m_w_out, m_ln1_g, m_ln1_b, m_w_ff1, m_w_ff3, m_w_ff2, m_ln2_g, m_ln2_b, v_c_ctx, v_ln0_g, v_ln0_b, v_w_ada, v_b_ada, v_w_in, v_conv_w, v_conv_b, v_dt_bias, v_a_log, v_d_skip, v_ssd_norm_g, v_gm_norm_g, v_gm_norm_b, v_w_spatial, v_b_spatial, v_b_gate, v_w_ssd_proj, v_w_gm_proj, v_w_out, v_ln1_g, v_ln1_b, v_w_ff1, v_w_ff3, v_w_ff2, v_ln2_g, v_ln2_b):
    W = dict(c_ctx=c_ctx, ln0_g=ln0_g, ln0_b=ln0_b, w_ada=w_ada, b_ada=b_ada, w_in=w_in, conv_w=conv_w,
             conv_b=conv_b, dt_bias=dt_bias, a_log=a_log, d_skip=d_skip, ssd_norm_g=ssd_norm_g,
             gm_norm_g=gm_norm_g, gm_norm_b=gm_norm_b, w_spatial=w_spatial, b_spatial=b_spatial,
             b_gate=b_gate, w_ssd_proj=w_ssd_proj, w_gm_proj=w_gm_proj, w_out=w_out, ln1_g=ln1_g,
             ln1_b=ln1_b, w_ff1=w_ff1, w_ff3=w_ff3, w_ff2=w_ff2, ln2_g=ln2_g, ln2_b=ln2_b)
    M = dict(c_ctx=m_c_ctx, ln0_g=m_ln0_g, ln0_b=m_ln0_b, w_ada=m_w_ada, b_ada=m_b_ada, w_in=m_w_in,
             conv_w=m_conv_w, conv_b=m_conv_b, dt_bias=m_dt_bias, a_log=m_a_log, d_skip=m_d_skip,
             ssd_norm_g=m_ssd_norm_g, gm_norm_g=m_gm_norm_g, gm_norm_b=m_gm_norm_b,
             w_spatial=m_w_spatial, b_spatial=m_b_spatial, b_gate=m_b_gate, w_ssd_proj=m_w_ssd_proj,
             w_gm_proj=m_w_gm_proj, w_out=m_w_out, ln1_g=m_ln1_g, ln1_b=m_ln1_b, w_ff1=m_w_ff1,
             w_ff3=m_w_ff3, w_ff2=m_w_ff2, ln2_g=m_ln2_g, ln2_b=m_ln2_b)
    V = dict(c_ctx=v_c_ctx, ln0_g=v_ln0_g, ln0_b=v_ln0_b, w_ada=v_w_ada, b_ada=v_b_ada, w_in=v_w_in,
             conv_w=v_conv_w, conv_b=v_conv_b, dt_bias=v_dt_bias, a_log=v_a_log, d_skip=v_d_skip,
             ssd_norm_g=v_ssd_norm_g, gm_norm_g=v_gm_norm_g, gm_norm_b=v_gm_norm_b,
             w_spatial=v_w_spatial, b_spatial=v_b_spatial, b_gate=v_b_gate, w_ssd_proj=v_w_ssd_proj,
             w_gm_proj=v_w_gm_proj, w_out=v_w_out, ln1_g=v_ln1_g, ln1_b=v_ln1_b, w_ff1=v_w_ff1,
             w_ff3=v_w_ff3, w_ff2=v_w_ff2, ln2_g=v_ln2_g, ln2_b=v_ln2_b)

    me = 4 * lax.axis_index("x") + 2 * lax.axis_index("y") + lax.axis_index("c")
    xl, cx, tgt = x[0], ctx[0], loss_target[0]
    L = xl.shape[0]
    assert cx.shape[0] == TL and L % TL == 0
    ada_n = w_ada.shape[2]
    cw_n = conv_w.shape[2]

    small1 = _pad_rows(jnp.concatenate([c, _slab([conv_w[0]], 1)], axis=0), 8)
    g1 = _all_gather(small1, "ag_small")
    c_all = g1[:, 0, :]
    conv_w_full = g1[:, 1, :5 * cw_n].reshape(NDEV, 5, cw_n).transpose(1, 0, 2).reshape(5, NDEV * cw_n)
    sq = w_ssd_proj.shape[1]
    ffr = w_ff2.shape[1]
    ffc = w_ff1.shape[2]
    late = [jnp.concatenate([w_ssd_proj[0], w_gm_proj[0], w_out[0], w_ff2[0]], axis=0).astype(_MXU),
            _t(w_ff1[0]).astype(_MXU), _t(w_ff3[0]).astype(_MXU)]

    c16 = _pad_rows(jnp.concatenate([c_all, _row(c_ctx)], axis=0), 16)
    b_ada_sh = lax.dynamic_slice(b_ada, (0, ada_n * me), (1, ada_n))
    modp = _ada_fwd(c16, w_ada[0], b_ada_sh)
    mod16 = _all_gather(modp, "ag_mod").transpose(1, 0, 2).reshape(16, NDEV * ada_n)

    ga, = _all_gather_multi([w_in[0].astype(_MXU)], "ag_w_in")
    ga, late, mod16 = lax.optimization_barrier((ga, late, mod16))
    lw_send, lw_recv, lw_src, lw_land, lw_token = _exchange_start(late, "ag_late_start", gather=True)
    w_in_p = _perm_from_blocks(ga)
    modx = _pad_rows(lax.dynamic_slice(mod16, (me, 0), (1, 6 * D)).reshape(6, D), 8) + lw_token[0, 0]
    modc = _pad_rows(mod16[8].reshape(6, D), 8)

    g0, b0 = _row(ln0_g), _row(ln0_b)
    xn, h1 = _ln0_fwd(xl, cx, g0, b0, modx, modc)
    p = _mm(h1, w_in_p, "nn", F32, "mm_p", tm_cap=(L + TL) // 2)
    conv_w8 = _pad_rows(conv_w_full, 8)
    xbc = _conv_fwd(p, conv_w8, conv_b)
    prm = _pad_rows(jnp.pad(jnp.stack([dt_bias.reshape(32), a_log.reshape(32)]), ((0, 0), (0, 96))), 8)
    yf, yb, hpf, hpb = _ssd2_fwd(xbc, p, prm)
    lw_land = _exchange_wait(lw_send, lw_recv, lw_src, lw_land, yf, "ag_late_wait", gather=True)
    fw_send, fw_recv, lw_land, fw_token = _forward_start(lw_land, "ag_fwd_start")
    dsk = _row(jnp.repeat(d_skip[0, 0] + d_skip[0, 1], HP)) + fw_token[0:1, 0:1]
    ws_m = w_spatial[0].astype(_MXU)
    bsT = jnp.pad(b_spatial[0].T, ((0, 0), (0, 120)))
    mixp = (dsk, ssd_norm_g, gm_norm_g, gm_norm_b, ws_m, bsT)
    yssd, ygm = _mix_fwd(yf, yb, p, xbc, *mixp)
    gb, gc1, gc2 = _forward_wait(fw_send, fw_recv, lw_land, yssd, "ag_fwd_wait")

    gb, gc1, gc2 = lax.optimization_barrier(
        [lax.dynamic_update_index_in_dim(g, mine, me, 0) for g, mine in zip((gb, gc1, gc2), late)])
    w_ssd_f = gb[:, 0:sq].reshape(NDEV * sq, D)
    w_gm_f = gb[:, sq:2 * sq].reshape(NDEV * sq, D)
    w_out_f = gb[:, 2 * sq:3 * sq].reshape(NDEV * sq, D)
    w_ff2_f = gb[:, 3 * sq:3 * sq + ffr].reshape(NDEV * ffr, D)
    assert HFF == (NDEV // 2) * ffc
    hd = NDEV // 2
    w13i = jnp.concatenate([g[k] for t in range(2) for g in (gc1, gc2) for k in range(t * hd, (t + 1) * hd)],
                           axis=0)
    a1, a2, merged, out, r1, h2 = _merge_fwd(yssd, ygm, p, b_gate, w_ssd_f, w_gm_f, w_out_f,
                                             xn, modx, ln1_g, ln1_b)
    f13, ff = _mm_f13_glu(h2, w13i)

    dr2, do2, st2, loss_slab = _mm_o2_res2(ff, w_ff2_f, r1, tgt, modx, ln1_g, ln1_b, ln2_g, ln2_b)
    loss = lax.psum(loss_slab[0, 0], ("x", "y", "c"))
    df13 = _mm_dff_glu(do2, w_ff2_f, f13)
    dw_ff2 = _mm(ff, do2, "tn", _MXU, "mm_dw_ff2")
    dw13i = _mm(df13, h2, "tn", _MXU, "mm_dw13")

    def owner_blocks(first):
        return jnp.concatenate([dw13i[t * 2 * HFF + first:t * 2 * HFF + first + HFF].reshape(NDEV // 2, ffc, D)
                                for t in range(2)], axis=0)

    xff = [dw_ff2.reshape(NDEV, ffr, D), owner_blocks(0), owner_blocks(HFF)]
    ff_send, ff_recv, ff_src, ff_land, ff_token = _exchange_start(xff, "xchg_ff_start")
    modx = modx + ff_token[0, 0]
    dr1, dout, st1 = _mm_dh2_res1bwd(df13, w13i, dr2, r1, out, modx, ln1_g, ln1_b)
    dw_out = _mm(merged, dout, "tn", _MXU, "mm_dw_out")
    dp = _dp_fill(lax.empty((L + TL, NPJ), _MXU))
    dp, da1, da2, stg, dys, dym = _merge_bwd(dout, a1, a2, p, b_gate, w_out_f, w_ssd_f, w_gm_f, dp)
    dw_ssd = _mm(yssd, da1, "tn", _MXU, "mm_dw_ssd")
    dw_gm = _mm(ygm, da2, "tn", _MXU, "mm_dw_gm")
    xsq = [jnp.concatenate([dw_ssd.reshape(NDEV, sq, D), dw_gm.reshape(NDEV, sq, D),
                            dw_out.reshape(NDEV, sq, D)], axis=1)]
    sq_send, sq_recv, sq_src, sq_land, sq_token = _exchange_start(xsq, "xchg_sq_start")
    mixp = (dsk + sq_token[0:1, 0:1],) + mixp[1:]
    dp, dyd, stm, dws, dbsT = _mix_bwd(dys, dym, yf, yb, p, xbc, dp, *mixp)
    dxf, dxb, ddf, ddb, sts = _ssd2_bwd(xbc, p, prm, dsk, dyd, hpf, hpb)
    dp, dcw, dcb = _conv_bwd(dxf, dxb, p, conv_w8, conv_b, dp)
    dp, std = _dt_bwd(ddf, ddb, dp)
    hw = D // 2
    xin_a = [_blocks_from_perm(_mm(h1[:, :hw], dp, "tn", _MXU, "mm_dw_in_a", tk_cap=L + TL), w_in.shape[2])]
    ina_send, ina_recv, ina_src, ina_land, ina_token = _exchange_start(xin_a, "xchg_in_a_start")
    h1b, ina_token = lax.optimization_barrier((h1[:, hw:], ina_token))
    xin_b = [_blocks_from_perm(_mm(h1b, dp, "tn", _MXU, "mm_dw_in_b", tk_cap=L + TL), w_in.shape[2])]
    inb_send, inb_recv, inb_src, inb_land, inb_token = _exchange_start(xin_b, "xchg_in_b_start")
    modx = modx + (ina_token[0, 0] + inb_token[0, 0])
    grad_x, st0 = _mm_dh1_ln0bwd(dp, w_in_p, dr1, xl, cx, g0, b0, modx, modc)

    zero = jnp.zeros((D,), F32)
    dmod = jnp.stack([jnp.concatenate([st0[0], st0[1], st1[4], st1[1], st1[0], st2[2]]),
                      jnp.concatenate([st0[2], st0[3], zero, zero, zero, zero])])
    g16 = _all_gather(_pad_rows(dmod, 8), "ag_dmod")[:, 0:2, :].reshape(16, 6 * D)
    g16_sh = lax.dynamic_slice(g16, (0, ada_n * me), (16, ada_n))
    c16b = jnp.stack([c_all, jnp.broadcast_to(_row(c_ctx), (NDEV, D))], axis=1).reshape(16, D)
    dw_ada, db_ada8, dcc8 = _ada_bwd(c16b, g16, g16_sh, w_ada[0])

    part = dict(
        c_ctx=dcc8[0], ln0_g=st0[4], ln0_b=st0[5], conv_w=dcw[0:5], conv_b=dcb[0],
        dt_bias=std[0, 0:32], a_log=sts[0, 0:32], d_skip=jnp.tile(sts[1, 0:16], 2),
        ssd_norm_g=stm[0], gm_norm_g=stm[1], gm_norm_b=stm[2], w_spatial=dws,
        b_spatial=dbsT[:, 0:8].T, b_gate=stg[0], ln1_g=st1[2], ln1_b=st1[3], ln2_g=st2[0], ln2_b=st2[1])
    pnames = list(part)
    psum8 = _sum8(_all_gather(_slab([part[n] for n in pnames], SMALL_ROWS), "ag_smallgrads"), "sum_smallgrads")
    small = dict(zip(pnames, _unslab(psum8, [part[n].shape for n in pnames])))
    grads = {n: small[n].reshape(W[n].shape) for n in pnames if n != "conv_w"}
    grads["conv_w"] = lax.dynamic_slice(small["conv_w"], (0, cw_n * me), (5, cw_n)).reshape(conv_w.shape)
    grads["b_ada"] = db_ada8[0:1]
    grads["w_ada"] = dw_ada.reshape(w_ada.shape)

    delta, new_m, new_v = {}, {}, {}

    def adam_group(names, rows, tag, align=0):
        shapes = [W[n].shape for n in names]
        outs = _adamw(*[_slab([src[n] for n in names], rows, align) for src in (grads, W, M, V)], tag)
        for res, slab in zip((delta, new_m, new_v), outs):
            for n, a in zip(names, _unslab(slab, shapes, align)):
                res[n] = a

    adam_group(REPL + ["conv_w"], SMALL_ROWS, "adamw_small")
    res = _adamw(grads["w_ada"][0], w_ada[0], m_w_ada[0], v_w_ada[0], "adamw_w_ada")
    delta["w_ada"], new_m["w_ada"], new_v["w_ada"] = [a[None] for a in res]

    rff = _exchange_wait(ff_send, ff_recv, ff_src, ff_land, st0, "xchg_ff_wait")
    rsq = _exchange_wait(sq_send, sq_recv, sq_src, sq_land, rff[0], "xchg_sq_wait")
    rin_a = _exchange_wait(ina_send, ina_recv, ina_src, ina_land, delta["ln2_b"], "xchg_in_a_wait")
    rin_b = _exchange_wait(inb_send, inb_recv, inb_src, inb_land, rin_a[0], "xchg_in_b_wait")
    rin = jnp.concatenate([rin_a[0], rin_b[0]], axis=1)

    def own(blocks):
        return lax.dynamic_index_in_dim(blocks, me, 0, keepdims=False)

    own_in = jnp.concatenate([own(xin_a[0]), own(xin_b[0])], axis=0)
    for n, r8, mine, row0, tr in (
            ("w_ff2", rff[0], own(xff[0]), 0, ffr // 2), ("w_ssd_proj", rsq[0], own(xsq[0]), 0, sq),
            ("w_gm_proj", rsq[0], own(xsq[0]), sq, sq), ("w_out", rsq[0], own(xsq[0]), 2 * sq, sq),
            ("w_in", rin, own_in, 0, 256)):
        res = _adamw_sum(r8, mine, W[n][0], M[n][0], V[n][0], row0, tr, "adamw_" + n)
        grads[n], delta[n], new_m[n], new_v[n] = [a[None] for a in res]
    for n, r8, mine in (("w_ff1", rff[1], own(xff[1])), ("w_ff3", rff[2], own(xff[2]))):
        res = _adamw_sum(r8, mine, _t(W[n][0]), _t(M[n][0]), _t(V[n][0]), 0, ffc // 2, "adamw_" + n)
        grads[n], delta[n], new_m[n], new_v[n] = [_t(a)[None] for a in res]

    return (loss, grad_x[None], *[grads[n] for n in WEIGHTS], *[delta[n] for n in WEIGHTS],
            *[new_m[n] for n in WEIGHTS], *[new_v[n] for n in WEIGHTS])
```

```python
import jax
import jax.numpy as jnp
from jax import lax
from jax.experimental import pallas as pl
from jax.experimental.pallas import tpu as pltpu

_MXU = jnp.bfloat16
F32 = jnp.float32
D = 1024
TL = 256
Q = 128
NH, HP, NS, HPG = 16, 64, 128, 8
DFF = 2816
ALPHA = 2.0 ** 0.25
EPS = 1e-5
OZ, OU, OV, OXS, OG, OB, OC, ODT, NPJ = 0, 1024, 2048, 3072, 4096, 6144, 6400, 6656, 6912
NNAT = 6688
NDEV = 8
ADAM_LR, ADAM_B1, ADAM_B2, ADAM_EPS, ADAM_WD, ADAM_STEP = 1e-3, 0.9, 0.999, 1e-8, 0.01, 10
VMEM_LIMIT = 48 * 1024 * 1024

NN = ((1,), (0,))
NT = ((1,), (1,))
TN = ((0,), (0,))
MESH = pl.DeviceIdType.MESH


def _dot(a, b, dims):
    return lax.dot_general(a.astype(_MXU), b.astype(_MXU), (dims, ((), ())),
                           preferred_element_type=F32)


def _tile(n, cands):
    for c in cands:
        if n % c == 0:
            return c
    return n


def _divisor_tile(n, cap, mult):
    best = n
    for t in range(mult, min(n, cap) + 1, mult):
        if n % t == 0:
            best = t
    return best


def _params(sem):
    return pltpu.CompilerParams(dimension_semantics=sem, vmem_limit_bytes=VMEM_LIMIT)


def _cst(shape):
    nd = len(shape)
    return pl.BlockSpec(shape, lambda *_: (0,) * nd)


def _rt(w, cb=0, rows=TL):
    return pl.BlockSpec((rows, w), lambda i: (i, cb))


def _rtc(w, nt, cb=0):
    return pl.BlockSpec((TL, w), lambda i: (jnp.minimum(i, nt - 1), cb))


def _sig(x):
    return jax.nn.sigmoid(x)


def _softplus(x):
    return jnp.maximum(x, 0.0) + jnp.log1p(jnp.exp(-jnp.abs(x)))


_G0, _G1 = 0.7978845608028654, 0.044715


def _gelu(x):
    t = jnp.tanh(_G0 * (x + _G1 * x * x * x))
    return 0.5 * x * (1.0 + t), t


def _gelu_grad(x, t):
    return 0.5 * (1.0 + t) + 0.5 * x * (1.0 - t * t) * _G0 * (1.0 + 3.0 * _G1 * x * x)


def _ln(r):
    mu = jnp.mean(r, axis=-1, keepdims=True)
    xc = r - mu
    var = jnp.mean(xc * xc, axis=-1, keepdims=True)
    rstd = lax.rsqrt(var + EPS)
    return xc * rstd, rstd


def _ln_bwd(dyh, xhat, rstd):
    return rstd * (dyh - jnp.mean(dyh, axis=-1, keepdims=True)
                   - xhat * jnp.mean(dyh * xhat, axis=-1, keepdims=True))


def _colsum(v):
    return jnp.sum(v, axis=0, keepdims=True)


def _cumsum_rows(a, rev):
    n = a.shape[0]
    row = lax.broadcasted_iota(jnp.int32, a.shape, 0)
    s = 1
    while s < n:
        if rev:
            a = a + jnp.where(row < n - s, pltpu.roll(a, n - s, 0), 0.0)
        else:
            a = a + jnp.where(row >= s, pltpu.roll(a, s, 0), 0.0)
        s *= 2
    return a


def _mm(a, b, mode, out_dtype, name, tm_cap=None, tk_cap=2304):
    if mode == "tn":
        K, M = a.shape
    else:
        M, K = a.shape
    N = b.shape[0] if mode == "nt" else b.shape[1]
    if mode == "tn":
        tm = _divisor_tile(M, tm_cap or 1408, 128)
    else:
        tm = _divisor_tile(M, tm_cap or 1088, 16)
    tn = _divisor_tile(N, 1408, 128)
    tk = _divisor_tile(K, tk_cap, 128)
    nk = K // tk
    dims = {"nn": NN, "nt": NT, "tn": TN}[mode]
    use_acc = nk > 1 and out_dtype != F32

    def body(a_ref, b_ref, o_ref, *acc):
        prod = _dot(a_ref[...], b_ref[...], dims)
        if nk == 1:
            o_ref[...] = prod.astype(o_ref.dtype)
            return
        acc_ref = acc[0] if use_acc else o_ref
        k = pl.program_id(2)

        @pl.when(k == 0)
        def _():
            acc_ref[...] = prod

        if use_acc:
            @pl.when((k > 0) & (k < nk - 1))
            def _():
                acc_ref[...] += prod

            @pl.when(k == nk - 1)
            def _():
                o_ref[...] = (acc_ref[...] + prod).astype(o_ref.dtype)
        else:
            @pl.when(k > 0)
            def _():
                o_ref[...] += prod

    if mode == "tn":
        a_spec = pl.BlockSpec((tk, tm), lambda i, j, k: (k, i))
    else:
        a_spec = pl.BlockSpec((tm, tk), lambda i, j, k: (i, k))
    if mode == "nt":
        b_spec = pl.BlockSpec((tn, tk), lambda i, j, k: (j, k))
    else:
        b_spec = pl.BlockSpec((tk, tn), lambda i, j, k: (k, j))
    return pl.pallas_call(
        body, name=name, grid=(M // tm, N // tn, nk),
        in_specs=[a_spec, b_spec],
        out_specs=pl.BlockSpec((tm, tn), lambda i, j, k: (i, j)),
        out_shape=jax.ShapeDtypeStruct((M, N), out_dtype),
        scratch_shapes=[pltpu.VMEM((tm, tn), F32)] if use_acc else [],
        compiler_params=_params(("parallel", "parallel", "arbitrary")),
    )(a, b)


def _all_gather(x, name):
    def body(x_ref, out_ref, send_sems, recv_sems, local_sem):
        mx, my, mc = lax.axis_index("x"), lax.axis_index("y"), lax.axis_index("c")
        me, sibling = (mx, my, mc), (mx, my, 1 - mc)
        chips = [(1 - mx, my), (mx, 1 - my), (1 - mx, 1 - my)]

        def slot(px, py, pc):
            return out_ref.at[4 * px + 2 * py + pc]

        def copy(k, block, to, src=None):
            return pltpu.make_async_remote_copy(
                src_ref=slot(*block) if src is None else src, dst_ref=slot(*block),
                send_sem=send_sems.at[k], recv_sem=recv_sems.at[k],
                device_id=to, device_id_type=MESH)

        mine = pltpu.make_async_copy(x_ref, slot(*me), local_sem)
        mine.start()
        first = [copy(0, me, sibling, src=x_ref)]
        first += [copy(1 + j, me, (*chip, mc), src=x_ref) for j, chip in enumerate(chips)]
        for cp in first:
            cp.start()
        passed = [copy(4 + j, (*chip, mc), sibling) for j, chip in enumerate(chips)]
        for j, chip in enumerate(chips):
            copy(1 + j, (*chip, mc), me).wait_recv()
            passed[j].start()
        copy(0, sibling, me).wait_recv()
        for j, chip in enumerate(chips):
            copy(4 + j, (*chip, 1 - mc), me).wait_recv()
        for cp in first + passed:
            cp.wait_send()
        mine.wait()

    return pl.pallas_call(
        body, name=name,
        out_shape=jax.ShapeDtypeStruct((NDEV,) + x.shape, x.dtype),
        in_specs=[pl.BlockSpec(memory_space=pl.ANY)],
        out_specs=pl.BlockSpec(memory_space=pl.ANY),
        scratch_shapes=[pltpu.SemaphoreType.DMA((7,)), pltpu.SemaphoreType.DMA((7,)),
                        pltpu.SemaphoreType.DMA],
    )(x)


def _any_specs(n):
    return [pl.BlockSpec(memory_space=pl.ANY)] * n


def _all_gather_multi(xs, name):
    na = len(xs)

    def body(*refs):
        x_refs, out_refs = refs[:na], refs[na:2 * na]
        send_sems, recv_sems, local_sems = refs[2 * na:]
        mx, my, mc = lax.axis_index("x"), lax.axis_index("y"), lax.axis_index("c")
        me, sibling = (mx, my, mc), (mx, my, 1 - mc)
        chips = [(1 - mx, my), (mx, 1 - my), (1 - mx, 1 - my)]

        def copy(a, k, block, to, src=None):
            slot = out_refs[a].at[4 * block[0] + 2 * block[1] + block[2]]
            return pltpu.make_async_remote_copy(
                src_ref=slot if src is None else src, dst_ref=slot,
                send_sem=send_sems.at[7 * a + k], recv_sem=recv_sems.at[7 * a + k],
                device_id=to, device_id_type=MESH)

        mine = [pltpu.make_async_copy(x_refs[a], out_refs[a].at[4 * mx + 2 * my + mc], local_sems.at[a])
                for a in range(na)]
        for cp in mine:
            cp.start()
        first = []
        for a in range(na):
            first.append(copy(a, 0, me, sibling, src=x_refs[a]))
            first += [copy(a, 1 + j, me, (*chip, mc), src=x_refs[a]) for j, chip in enumerate(chips)]
        for cp in first:
            cp.start()
        passed = []
        for a in range(na):
            for j, chip in enumerate(chips):
                copy(a, 1 + j, (*chip, mc), me).wait_recv()
                fwd = copy(a, 4 + j, (*chip, mc), sibling)
                fwd.start()
                passed.append(fwd)
        for a in range(na):
            copy(a, 0, sibling, me).wait_recv()
            for j, chip in enumerate(chips):
                copy(a, 4 + j, (*chip, 1 - mc), me).wait_recv()
        for cp in first + passed:
            cp.wait_send()
        for cp in mine:
            cp.wait()

    return pl.pallas_call(
        body, name=name,
        out_shape=[jax.ShapeDtypeStruct((NDEV,) + x.shape, x.dtype) for x in xs],
        in_specs=_any_specs(na), out_specs=_any_specs(na),
        scratch_shapes=[pltpu.SemaphoreType.DMA((7 * na,)), pltpu.SemaphoreType.DMA((7 * na,)),
                        pltpu.SemaphoreType.DMA((na,))],
    )(*xs)


def _adamw_sum(r8, own, w, m, v, row0, tr, name):
    R, C = w.shape
    assert row0 % tr == 0
    blk0 = row0 // tr
    bc1 = 1.0 - ADAM_B1 ** ADAM_STEP
    bc2 = 1.0 - ADAM_B2 ** ADAM_STEP

    def body(r_ref, *refs):
        if own is None:
            gg = r_ref[0].astype(F32)
        else:
            gg = refs[0][...].astype(F32)
            refs = refs[1:]
        w_ref, m_ref, v_ref, g_ref, d_ref, mo_ref, vo_ref = refs
        for k in range(1, NDEV):
            gg = gg + r_ref[k].astype(F32)
        mn = ADAM_B1 * m_ref[...] + (1.0 - ADAM_B1) * gg
        vn = ADAM_B2 * v_ref[...] + (1.0 - ADAM_B2) * (gg * gg)
        mh = mn / bc1
        vh = vn / bc2
        g_ref[...] = gg
        d_ref[...] = -ADAM_LR * (mh / (jnp.sqrt(vh) + ADAM_EPS) + ADAM_WD * w_ref[...])
        mo_ref[...] = mn
        vo_ref[...] = vn

    spec = pl.BlockSpec((tr, C), lambda i: (i, 0))
    sh = jax.ShapeDtypeStruct((R, C), F32)
    own_ops = [] if own is None else [own]
    own_specs = [] if own is None else [pl.BlockSpec((tr, C), lambda i: (i + blk0, 0))]
    return pl.pallas_call(
        body, name=name, grid=(R // tr,),
        in_specs=[pl.BlockSpec((NDEV, tr, C), lambda i: (0, i + blk0, 0))] + own_specs + [spec, spec, spec],
        out_specs=[spec] * 4, out_shape=[sh] * 4, compiler_params=_params(("parallel",)),
    )(r8, *own_ops, w, m, v)


_HBM = pl.BlockSpec(memory_space=pltpu.HBM)
_SEM = pl.BlockSpec(memory_space=pltpu.SEMAPHORE)
_EFFECT = pltpu.SideEffectType.DATAFLOW_SIDE_EFFECTING


def _exchange_copies(g_refs, land_refs, send_sems, recv_sems, gather):
    mx, my, mc = lax.axis_index("x"), lax.axis_index("y"), lax.axis_index("c")
    copies = []
    for a in range(len(g_refs)):
        for f in ((1, 2, 4, 6) if gather else range(1, NDEV)):
            px = 1 - mx if (f >> 2) & 1 else mx
            py = 1 - my if (f >> 1) & 1 else my
            pc = 1 - mc if f & 1 else mc
            src = g_refs[a] if gather else g_refs[a].at[4 * px + 2 * py + pc]
            dst = land_refs[a].at[4 * mx + 2 * my + mc] if gather else land_refs[a].at[f]
            copies.append(pltpu.make_async_remote_copy(
                src_ref=src, dst_ref=dst,
                send_sem=send_sems.at[7 * a + f - 1], recv_sem=recv_sems.at[7 * a + f - 1],
                device_id=(px, py, pc), device_id_type=MESH))
    return copies


def _exchange_start(gs, name, gather=False):
    na = len(gs)

    def body(*refs):
        for cp in _exchange_copies(refs[:na], refs[na:2 * na], refs[2 * na], refs[2 * na + 1], gather):
            cp.start()
        refs[-1][...] = jnp.zeros_like(refs[-1])

    hbm = [pltpu.HBM(g.shape, g.dtype) for g in gs]
    land_shapes = [((NDEV,) + g.shape) if gather else g.shape for g in gs]
    lands = [pltpu.with_memory_space_constraint(lax.empty(shp, g.dtype), pltpu.HBM)
             for shp, g in zip(land_shapes, gs)]
    hbm_land = [pltpu.HBM(shp, g.dtype) for shp, g in zip(land_shapes, gs)]
    outs = pl.pallas_call(
        body, name=name,
        out_shape=(pltpu.SemaphoreType.DMA((7 * na,)), pltpu.SemaphoreType.DMA((7 * na,)), *hbm, *hbm_land,
                   jax.ShapeDtypeStruct((8, 128), F32)),
        in_specs=[_HBM] * (2 * na),
        out_specs=(_SEM, _SEM, *([_HBM] * (2 * na)), pl.BlockSpec(memory_space=pltpu.VMEM)),
        input_output_aliases={i: 2 + i for i in range(2 * na)},
        compiler_params=pltpu.CompilerParams(has_side_effects=_EFFECT),
    )(*[pltpu.with_memory_space_constraint(g, pltpu.HBM) for g in gs], *lands)
    return outs[0], outs[1], outs[2:2 + na], outs[2 + na:2 + 2 * na], outs[-1]


def _forward_copies(land_refs, send_sems, recv_sems):
    mx, my, mc = lax.axis_index("x"), lax.axis_index("y"), lax.axis_index("c")
    copies = []
    for a in range(len(land_refs)):
        for j, (fx, fy) in enumerate(((0, 1), (1, 0), (1, 1))):
            px = 1 - mx if fx else mx
            py = 1 - my if fy else my
            blk = land_refs[a].at[4 * px + 2 * py + mc]
            copies.append(pltpu.make_async_remote_copy(
                src_ref=blk, dst_ref=blk, send_sem=send_sems.at[3 * a + j], recv_sem=recv_sems.at[3 * a + j],
                device_id=(mx, my, 1 - mc), device_id_type=MESH))
    return copies


def _forward_start(lands, name):
    na = len(lands)

    def body(*refs):
        for cp in _forward_copies(refs[:na], refs[na], refs[na + 1]):
            cp.start()
        refs[-1][...] = jnp.zeros_like(refs[-1])

    outs = pl.pallas_call(
        body, name=name,
        out_shape=(pltpu.SemaphoreType.DMA((3 * na,)), pltpu.SemaphoreType.DMA((3 * na,)),
                   *[pltpu.HBM(g.shape, g.dtype) for g in lands], jax.ShapeDtypeStruct((8, 128), F32)),
        in_specs=[_HBM] * na,
        out_specs=(_SEM, _SEM, *([_HBM] * na), pl.BlockSpec(memory_space=pltpu.VMEM)),
        input_output_aliases={i: 2 + i for i in range(na)},
        compiler_params=pltpu.CompilerParams(has_side_effects=_EFFECT),
    )(*lands)
    return outs[0], outs[1], outs[2:2 + na], outs[-1]


def _forward_wait(send_sems, recv_sems, lands, after, name):
    na = len(lands)

    def body(*refs):
        for cp in _forward_copies(refs[:na], refs[na], refs[na + 1]):
            cp.wait_send()
            cp.wait_recv()

    return pl.pallas_call(
        body, name=name,
        out_shape=tuple(pltpu.HBM(g.shape, g.dtype) for g in lands),
        in_specs=[_HBM] * na + [_SEM, _SEM, pl.BlockSpec(memory_space=pl.ANY)],
        out_specs=tuple([_HBM] * na),
        input_output_aliases={i: i for i in range(na)},
        compiler_params=pltpu.CompilerParams(has_side_effects=_EFFECT),
    )(*lands, send_sems, recv_sems, after)


def _exchange_wait(send_sems, recv_sems, g_thru, land_thru, after, name, gather=False):
    na = len(g_thru)

    def body(*refs):
        for cp in _exchange_copies(refs[:na], refs[na:2 * na], refs[2 * na], refs[2 * na + 1], gather):
            cp.wait_send()
            cp.wait_recv()

    outs = pl.pallas_call(
        body, name=name,
        out_shape=tuple(pltpu.HBM(g.shape, g.dtype) for g in list(g_thru) + list(land_thru)),
        in_specs=[_HBM] * (2 * na) + [_SEM, _SEM, pl.BlockSpec(memory_space=pl.ANY)],
        out_specs=tuple([_HBM] * (2 * na)),
        input_output_aliases={i: i for i in range(2 * na)},
        compiler_params=pltpu.CompilerParams(has_side_effects=_EFFECT),
    )(*g_thru, *land_thru, send_sems, recv_sems, after)
    return outs[na:]


def _sum8(r, name):
    _, R, C = r.shape
    tr = _tile(R, (256, 160, 128, 64, 32, 16, 8))

    def body(r_ref, o_ref):
        acc = r_ref[0].astype(F32)
        for k in range(1, NDEV):
            acc = acc + r_ref[k].astype(F32)
        o_ref[...] = acc

    return pl.pallas_call(
        body, name=name, grid=(R // tr,),
        in_specs=[pl.BlockSpec((NDEV, tr, C), lambda i: (0, i, 0))],
        out_specs=pl.BlockSpec((tr, C), lambda i: (i, 0)),
        out_shape=jax.ShapeDtypeStruct((R, C), F32),
        compiler_params=_params(("parallel",)),
    )(r)


def _adamw(g, w, m, v, name):
    R, C = g.shape
    tr = _tile(R, (256, 160, 128, 64, 32, 16, 8))
    bc1 = 1.0 - ADAM_B1 ** ADAM_STEP
    bc2 = 1.0 - ADAM_B2 ** ADAM_STEP

    def body(g_ref, w_ref, m_ref, v_ref, d_ref, mo_ref, vo_ref):
        gg = g_ref[...]
        mn = ADAM_B1 * m_ref[...] + (1.0 - ADAM_B1) * gg
        vn = ADAM_B2 * v_ref[...] + (1.0 - ADAM_B2) * (gg * gg)
        mh = mn / bc1
        vh = vn / bc2
        d_ref[...] = -ADAM_LR * (mh / (jnp.sqrt(vh) + ADAM_EPS) + ADAM_WD * w_ref[...])
        mo_ref[...] = mn
        vo_ref[...] = vn

    spec = pl.BlockSpec((tr, C), lambda i: (i, 0))
    sh = jax.ShapeDtypeStruct((R, C), F32)
    return pl.pallas_call(
        body, name=name, grid=(R // tr,), in_specs=[spec] * 4, out_specs=[spec] * 3,
        out_shape=[sh] * 3, compiler_params=_params(("parallel",)),
    )(g, w, m, v)


def _ada_fwd(c16, w_sh, b_sh):
    def body(c_ref, w_ref, b_ref, o_ref):
        c = c_ref[...]
        o_ref[...] = _dot(c * _sig(c), w_ref[...], NN) + b_ref[...]

    return pl.pallas_call(
        body, name="ada_fwd", out_shape=jax.ShapeDtypeStruct((16, w_sh.shape[1]), F32),
        compiler_params=pltpu.CompilerParams(vmem_limit_bytes=VMEM_LIMIT),
    )(c16, w_sh, b_sh)


def _ada_bwd(c16, g16, g16_sh, w_sh):
    ncol = w_sh.shape[1]

    def body(c_ref, g_ref, gs_ref, w_ref, dw_ref, db_ref, dc_ref):
        c = c_ref[...]
        s = _sig(c)
        gs = gs_ref[...]
        dw_ref[...] = _dot(c * s, gs, TN)
        db_ref[...] = jnp.broadcast_to(_colsum(g_ref[...]), db_ref.shape)
        odd = lax.broadcasted_iota(jnp.int32, gs.shape, 0) % 2 == 1
        gc = _colsum(jnp.where(odd, gs, 0.0))
        ds = _dot(jnp.broadcast_to(gc, (8, ncol)), w_ref[...], NT)
        c1 = c[1:2, :]
        s1 = s[1:2, :]
        dc_ref[...] = ds * (s1 * (1.0 + c1 * (1.0 - s1)))

    return pl.pallas_call(
        body, name="ada_bwd",
        out_shape=[jax.ShapeDtypeStruct(w_sh.shape, F32),
                   jax.ShapeDtypeStruct((8, g16.shape[1]), F32),
                   jax.ShapeDtypeStruct((8, D), F32)],
        compiler_params=pltpu.CompilerParams(vmem_limit_bytes=VMEM_LIMIT),
    )(c16, g16, g16_sh, w_sh)


def _ln0_fwd(x, ctx, g, b, modx, modc):
    L = x.shape[0]
    nt = L // TL

    def body(x_ref, c_ref, g_ref, b_ref, mx_ref, mc_ref, xn_ref, h_ref):
        isc = pl.program_id(0) == nt
        xin = jnp.where(isc, c_ref[...], x_ref[...])
        sh = jnp.where(isc, mc_ref[0:1, :], mx_ref[0:1, :])
        sc = jnp.where(isc, mc_ref[1:2, :], mx_ref[1:2, :])
        xhat, _ = _ln(xin)
        xn = xhat * g_ref[...] + b_ref[...]
        xn_ref[...] = xn
        h_ref[...] = (xn * (1.0 + sc) + sh).astype(h_ref.dtype)

    return pl.pallas_call(
        body, name="ln0_fwd", grid=(nt + 1,),
        in_specs=[_rtc(D, nt), _cst((TL, D)), _cst((1, D)), _cst((1, D)), _cst((8, D)), _cst((8, D))],
        out_specs=[_rt(D), _rt(D)],
        out_shape=[jax.ShapeDtypeStruct((L + TL, D), F32), jax.ShapeDtypeStruct((L + TL, D), _MXU)],
        compiler_params=_params(("parallel",)),
    )(x, ctx, g, b, modx, modc)


def _xbc_colblk(j):
    return jnp.where(j < 8, OXS // 128 + j, OB // 128 + j - 8)


def _conv_taps(p_ref, r0, first, last):
    main = p_ref[pl.ds(r0, TL), :]
    zero = jnp.zeros((8, main.shape[1]), F32)
    prev = zero if first else p_ref[pl.ds(r0 - 8, 8), :]
    nxt = zero if last else p_ref[pl.ds(r0 + TL, 8), :]
    ext = jnp.concatenate([prev, main, nxt], axis=0)
    n = TL + 16
    return [pltpu.roll(ext, (2 - k) % n, 0)[8:8 + TL] for k in range(5)]


def _seq_chunks(L):
    nt = L // TL
    return [(r * TL, r == 0, r == nt - 1) for r in range(nt)] + [(L, True, True)]


def _conv_fwd(p, conv_w8, conv_b):
    RT = p.shape[0]
    L = RT - TL
    chunks = _seq_chunks(L)

    def body(p_ref, w_ref, b_ref, o_ref):
        w = w_ref[...]
        bias = b_ref[...]
        for r0, first, last in chunks:
            taps = _conv_taps(p_ref, r0, first, last)
            pre = bias + sum(w[k:k + 1, :] * taps[k] for k in range(5))
            o_ref[pl.ds(r0, TL), :] = pre * _sig(pre)

    return pl.pallas_call(
        body, name="conv_fwd", grid=(12,),
        in_specs=[pl.BlockSpec((RT, 128), lambda j: (0, _xbc_colblk(j))),
                  pl.BlockSpec((8, 128), lambda j: (0, j)),
                  pl.BlockSpec((1, 128), lambda j: (0, j))],
        out_specs=pl.BlockSpec((RT, 128), lambda j: (0, j)),
        out_shape=jax.ShapeDtypeStruct((RT, 1536), F32),
        compiler_params=_params(("parallel",)),
    )(p, conv_w8, conv_b)


def _ssd_common(dtraw, dtb, a32, rev):
    dt = _softplus(dtraw + dtb)
    acum = _cumsum_rows(dt * a32, rev)
    ii = lax.broadcasted_iota(jnp.int32, (Q, Q), 0)
    jj = lax.broadcasted_iota(jnp.int32, (Q, Q), 1)
    mask = (ii <= jj) if rev else (ii >= jj)
    return dt, acum, acum.T, dt.T, mask


def _ssd_orders(ncl, ncc):
    nc = ncl + ncc

    def cf(s):
        return jnp.where(s < ncc, ncl + s, s - ncc)

    def cb(s):
        return nc - 1 - s

    return cf, cb


def _lane_bcast(v, ln):
    return jnp.broadcast_to(v[:, ln:ln + 1], v.shape)


def _halves(v, lo, axis):
    return jnp.concatenate([jnp.where(lo, v, 0.0), jnp.where(lo, 0.0, v)], axis=axis)


def _ssd2_fwd(xbc, p, prm):
    RT = xbc.shape[0]
    nc = RT // Q
    ncc = TL // Q
    cf, cb = _ssd_orders(nc - ncc, ncc)

    def one_dir(x_ref, dt_ref, prm_ref, y_ref, hp_ref, HT_ref, d):
        rev = d == 1
        a32 = -jnp.exp(prm_ref[1:2, :])
        dt, acum, acumT, dtT, mask = _ssd_common(dt_ref[...], prm_ref[0:1, :], a32, rev)
        end = 0 if rev else Q - 1
        lo = lax.broadcasted_iota(jnp.int32, (Q, 128), 1) < HP
        for g in range(2):
            Bg = x_ref[:, D + g * NS:D + (g + 1) * NS]
            Cg = x_ref[:, D + 2 * NS + g * NS:D + 2 * NS + (g + 1) * NS]
            CB = _dot(Cg, Bg, NT)
            xds, svs = [], []
            for q in range(HPG // 2):
                pi = g * (HPG // 2) + q
                ps = slice(pi * 128, (pi + 1) * 128)
                Xp = x_ref[:, ps]
                HTp = HT_ref[:, ps]
                lhs, dcs, sv = [], [], []
                ces = []
                for h in (2 * pi, 2 * pi + 1):
                    ln = 16 * d + h
                    colB = _lane_bcast(acum, ln)
                    rowv = acumT[ln:ln + 1, :]
                    aend = colB[end:end + 1, :]
                    Lm = jnp.exp(jnp.where(mask, colB - rowv, -1e30))
                    lhs.append(CB * Lm * dtT[ln:ln + 1, :])
                    ces.append(Cg * jnp.exp(colB))
                    dcs.append(jnp.exp(aend - colB) * _lane_bcast(dt, ln))
                    sv.append(jnp.exp(aend))
                lhs = jnp.concatenate(lhs + ces, axis=1)
                rhs = jnp.concatenate([_halves(Xp, lo, 0), _halves(HTp, lo, 0)], axis=0)
                y_ref[:, ps] = _dot(lhs, rhs, NN)
                xds.append(Xp * jnp.where(lo, dcs[0], dcs[1]))
                svs.append(jnp.where(lo[0:1, :], sv[0], sv[1]))
            gs = slice(g * 512, (g + 1) * 512)
            HTg = HT_ref[:, gs]
            hp_ref[0, :, gs] = HTg
            st = _dot(Bg.T, jnp.concatenate(xds, axis=1), NN)
            HT_ref[:, gs] = jnp.concatenate(svs, axis=1) * HTg + st

    def body(xf_ref, xb_ref, df_ref, db_ref, prm_ref, yf_ref, yb_ref, hf_ref, hb_ref, Hf, Hb):
        @pl.when(pl.program_id(0) == 0)
        def _():
            Hf[...] = jnp.zeros_like(Hf)
            Hb[...] = jnp.zeros_like(Hb)

        one_dir(xf_ref, df_ref, prm_ref, yf_ref, hf_ref, Hf, 0)
        one_dir(xb_ref, db_ref, prm_ref, yb_ref, hb_ref, Hb, 1)

    ysh = jax.ShapeDtypeStruct((RT, D), F32)
    hsh = jax.ShapeDtypeStruct((nc, NS, NH * HP), F32)
    hspec = pl.BlockSpec((1, NS, NH * HP), lambda s: (s, 0, 0))
    return pl.pallas_call(
        body, name="ssd_fwd", grid=(nc,),
        in_specs=[pl.BlockSpec((Q, 1536), lambda s: (cf(s), 0)),
                  pl.BlockSpec((Q, 1536), lambda s: (cb(s), 0)),
                  pl.BlockSpec((Q, 128), lambda s: (cf(s), ODT // 128)),
                  pl.BlockSpec((Q, 128), lambda s: (cb(s), ODT // 128)),
                  _cst((8, 128))],
        out_specs=[pl.BlockSpec((Q, D), lambda s: (cf(s), 0)),
                   pl.BlockSpec((Q, D), lambda s: (cb(s), 0)), hspec, hspec],
        out_shape=[ysh, ysh, hsh, hsh],
        scratch_shapes=[pltpu.VMEM((NS, NH * HP), F32), pltpu.VMEM((NS, NH * HP), F32)],
        compiler_params=_params(("arbitrary",)),
    )(xbc, xbc, p, p, prm)


def _ssd2_bwd(xbc, p, prm, dsk, dyd, hpf, hpb):
    RT = xbc.shape[0]
    nc = RT // Q
    ncc = TL // Q
    ncl = nc - ncc
    cf, cb = _ssd_orders(ncl, ncc)

    def rs(t):
        return nc - 1 - t

    def one_dir(x_ref, dt_ref, prm_ref, dsk_ref, dy_ref, is_ctx, hp_ref, dHT_ref,
                dx_ref, ddt_ref, st_ref, d):
        rev = d == 1
        a32 = -jnp.exp(prm_ref[1:2, :])
        dtraw = dt_ref[...]
        dtb = prm_ref[0:1, :]
        dt, acum, acumT, _, _ = _ssd_common(dtraw, dtb, a32, rev)
        end = 0 if rev else Q - 1
        lane = lax.broadcasted_iota(jnp.int32, (Q, 128), 1)
        srow = lax.broadcasted_iota(jnp.int32, (Q, 128), 0)
        maskT = (lane <= srow) if rev else (lane >= srow)
        lo = lane < HP
        lo1 = lo[0:1, :]
        dyscale = jnp.where(is_ctx, 0.0, 1.0)
        c_dacum = jnp.zeros((Q, 128), F32)
        r_dacum = jnp.zeros((Q, 128), F32)
        c_ddt = jnp.zeros((Q, 128), F32)
        dskacc = jnp.zeros((1, 128), F32)
        for g in range(2):
            gs = slice(g * 512, (g + 1) * 512)
            Bg = x_ref[:, D + g * NS:D + (g + 1) * NS]
            Cg = x_ref[:, D + 2 * NS + g * NS:D + 2 * NS + (g + 1) * NS]
            CBT = _dot(Bg, Cg, NT)
            HTg = hp_ref[0, :, gs]
            dHTg = dHT_ref[:, gs]
            BdHg = _dot(Bg, dHTg, NN)
            dCBT = jnp.zeros((Q, Q), F32)
            dCg = jnp.zeros((Q, NS), F32)
            xds, dyes, svs = [], [], []
            for q in range(HPG // 2):
                pi = g * (HPG // 2) + q
                ps = slice(pi * 128, (pi + 1) * 128)
                qs = slice(q * 128, (q + 1) * 128)
                Xp = x_ref[:, ps]
                dYp = dy_ref[:, ps] * dyscale
                HTp = HTg[:, qs]
                BdHp = BdHg[:, qs]
                dY2 = _halves(dYp, lo, 0)
                dWT2 = _dot(_halves(Xp, lo, 0), dYp.T, NN)
                G2 = _dot(dY2, HTp, NT)
                XB = Xp * BdHp
                hh = _colsum(dHTg[:, qs] * HTp)
                yx = _colsum(dYp * Xp)
                wts, dcs, ebs, sv = [], [], [], []
                for k, h in enumerate((2 * pi, 2 * pi + 1)):
                    ln = 16 * d + h
                    half = lo if k == 0 else jnp.logical_not(lo)
                    half1 = half[0:1, :]
                    colB = _lane_bcast(acum, ln)
                    dtcB = _lane_bcast(dt, ln)
                    rowv = acumT[ln:ln + 1, :]
                    aend = colB[end:end + 1, :]
                    LmT = jnp.exp(jnp.where(maskT, rowv - colB, -1e30))
                    WT = CBT * LmT * dtcB
                    dWT = dWT2[k * Q:(k + 1) * Q, :]
                    U = dWT * LmT
                    MT = U * CBT
                    rM = jnp.sum(MT, axis=1, keepdims=True)
                    rT = _colsum(MT * dtcB)
                    dCBT = dCBT + U * dtcB
                    ecol = jnp.exp(aend - colB)
                    EB = jnp.exp(colB)
                    Gk = G2[k * Q:(k + 1) * Q, :]
                    dCg = dCg + EB * Gk
                    qcol = jnp.sum(EB * Gk * Cg, axis=1, keepdims=True)
                    xb = jnp.sum(jnp.where(half, XB, 0.0), axis=1, keepdims=True)
                    e1 = ecol[:, 0:1]
                    dt1 = dtcB[:, 0:1]
                    scol = e1 * dt1 * xb
                    sA = jnp.exp(aend)
                    eterm = sA[:, 0:1] * jnp.sum(jnp.where(half1, hh, 0.0), axis=1, keepdims=True) \
                        + _colsum(scol)
                    cvec = qcol - dt1 * rM - scol + jnp.where(srow[:, 0:1] == end, eterm, 0.0)
                    c_dacum = jnp.where(lane == ln, cvec, c_dacum)
                    r_dacum = jnp.where(srow == ln, rT, r_dacum)
                    c_ddt = jnp.where(lane == ln, rM + e1 * xb, c_ddt)
                    if d == 0:
                        dskacc = dskacc + jnp.where(
                            lane[0:1, :] == h, jnp.sum(jnp.where(half1, yx, 0.0), axis=1, keepdims=True), 0.0)
                    wts.append(WT)
                    dcs.append(ecol * dtcB)
                    ebs.append(EB)
                    sv.append(sA)
                dcp = jnp.where(lo, dcs[0], dcs[1])
                dX = _dot(jnp.concatenate(wts, axis=1), dY2, NN) + dcp * BdHp
                if d == 0:
                    dX = dX + dYp * dsk_ref[:, ps]
                dx_ref[:, ps] = dX
                xds.append(Xp * dcp)
                dyes.append(dYp * jnp.where(lo, ebs[0], ebs[1]))
                svs.append(jnp.where(lo1, sv[0], sv[1]))
            dx_ref[:, D + g * NS:D + (g + 1) * NS] = (
                _dot(jnp.concatenate(xds, axis=1), dHTg, NT) + _dot(dCBT, Cg, NN))
            dx_ref[:, D + 2 * NS + g * NS:D + 2 * NS + (g + 1) * NS] = dCg + _dot(dCBT, Bg, TN)
            dHT_ref[:, gs] = (jnp.concatenate(svs, axis=1) * dHTg
                              + _dot(Cg.T, jnp.concatenate(dyes, axis=1), NN))
        dacum = c_dacum + r_dacum.T
        da = _cumsum_rows(dacum, not rev)
        mine = (lane >= 16 * d) & (lane < 16 * d + 16)
        ddt = jnp.where(mine, c_ddt + da * a32, 0.0)
        ddt_ref[...] = ddt * _sig(dtraw + dtb)
        st_ref[0:1, :] += _colsum(jnp.where(mine, da * dt, 0.0))
        if d == 0:
            st_ref[1:2, :] += dskacc

    def body(xf_ref, xb_ref, df_ref, db_ref, prm_ref, dsk_ref, dyf_ref, dyb_ref, hf_ref, hb_ref,
             dxf_ref, dxb_ref, ddf_ref, ddb_ref, st_ref, dHf, dHb):
        t = pl.program_id(0)

        @pl.when(t == 0)
        def _():
            dHf[...] = jnp.zeros_like(dHf)
            dHb[...] = jnp.zeros_like(dHb)
            st_ref[...] = jnp.zeros_like(st_ref)

        s = rs(t)
        one_dir(xf_ref, df_ref, prm_ref, dsk_ref, dyf_ref, cf(s) >= ncl, hf_ref, dHf,
                dxf_ref, ddf_ref, st_ref, 0)
        one_dir(xb_ref, db_ref, prm_ref, dsk_ref, dyb_ref, cb(s) >= ncl, hb_ref, dHb,
                dxb_ref, ddb_ref, st_ref, 1)

        @pl.when(t == nc - 1)
        def _():
            st_ref[0:1, :] = -jnp.exp(prm_ref[1:2, :]) * st_ref[0:1, :]

    def lat(c):
        return jnp.minimum(c, ncl - 1)

    xsh = jax.ShapeDtypeStruct((RT, 1536), F32)
    dsh = jax.ShapeDtypeStruct((RT, 128), F32)
    hspec = pl.BlockSpec((1, NS, NH * HP), lambda t: (rs(t), 0, 0))
    return pl.pallas_call(
        body, name="ssd_bwd", grid=(nc,),
        in_specs=[pl.BlockSpec((Q, 1536), lambda t: (cf(rs(t)), 0)),
                  pl.BlockSpec((Q, 1536), lambda t: (cb(rs(t)), 0)),
                  pl.BlockSpec((Q, 128), lambda t: (cf(rs(t)), ODT // 128)),
                  pl.BlockSpec((Q, 128), lambda t: (cb(rs(t)), ODT // 128)),
                  _cst((8, 128)), _cst((1, D)),
                  pl.BlockSpec((Q, D), lambda t: (lat(cf(rs(t))), 0)),
                  pl.BlockSpec((Q, D), lambda t: (lat(cb(rs(t))), 0)),
                  hspec, hspec],
        out_specs=[pl.BlockSpec((Q, 1536), lambda t: (cf(rs(t)), 0)),
                   pl.BlockSpec((Q, 1536), lambda t: (cb(rs(t)), 0)),
                   pl.BlockSpec((Q, 128), lambda t: (cf(rs(t)), 0)),
                   pl.BlockSpec((Q, 128), lambda t: (cb(rs(t)), 0)),
                   _cst((8, 128))],
        out_shape=[xsh, xsh, dsh, dsh, jax.ShapeDtypeStruct((8, 128), F32)],
        scratch_shapes=[pltpu.VMEM((NS, NH * HP), F32), pltpu.VMEM((NS, NH * HP), F32)],
        compiler_params=_params(("arbitrary",)),
    )(xbc, xbc, p, p, prm, dsk, dyd, dyd, hpf, hpb)


def _mix_fwd_vals(yf, yb, z, xs, u, v, dsk, sg, gg, gb):
    y = yf + yb + xs * dsk
    sz = _sig(z)
    hh = y * z * sz
    r = lax.rsqrt(jnp.mean(hh * hh, axis=-1, keepdims=True) + EPS)
    nh = hh * r
    ug, tu = _gelu(u)
    vg, tv = _gelu(v)
    vhat, vrstd = _ln(vg)
    vn = vhat * gg + gb
    return y, sz, r, nh, ug, tu, vg, tv, vhat, vrstd, vn


def _mix_fwd(yf, yb, p, xbc, dsk, sg, gg, gb, ws, bsT):
    L = yf.shape[0] - TL
    nt = L // TL

    def body(yf_ref, yb_ref, z_ref, xs_ref, u_ref, v_ref, dsk_ref, sg_ref, gg_ref, gb_ref,
             ws_ref, bs_ref, ys_ref, ym_ref):
        _, _, _, nh, ug, _, _, _, _, _, vn = _mix_fwd_vals(
            yf_ref[...], yb_ref[...], z_ref[...], xs_ref[...], u_ref[...], v_ref[...],
            dsk_ref[...], sg_ref[...], gg_ref[...], gb_ref[...])
        ys_ref[...] = (nh * sg_ref[...]).astype(ys_ref.dtype)
        for n in range(TL // Q):
            rs_ = slice(n * Q, (n + 1) * Q)
            for g in range(8):
                cs = slice(g * 128, (g + 1) * 128)
                mixed = _dot(ws_ref[g], vn[rs_, cs], NN) + bs_ref[:, g:g + 1]
                ym_ref[rs_, cs] = (ug[rs_, cs] * mixed).astype(ym_ref.dtype)

    return pl.pallas_call(
        body, name="mix_fwd", grid=(nt,),
        in_specs=[_rt(D), _rt(D), _rt(D, OZ // D), _rt(D, 0), _rt(D, OU // D), _rt(D, OV // D),
                  _cst((1, D)), _cst((1, D)), _cst((1, D)), _cst((1, D)),
                  _cst((8, 128, 128)), _cst((128, 128))],
        out_specs=[_rt(D), _rt(D)],
        out_shape=[jax.ShapeDtypeStruct((L, D), _MXU), jax.ShapeDtypeStruct((L, D), _MXU)],
        compiler_params=_params(("parallel",)),
    )(yf, yb, p, xbc, p, p, dsk, sg, gg, gb, ws, bsT)


def _mix_bwd(dys, dym, yf, yb, p, xbc, dp, dsk, sg, gg, gb, ws, bsT):
    L = dys.shape[0]
    nt = L // TL

    def body(dys_ref, dym_ref, yf_ref, yb_ref, z_ref, xs_ref, u_ref, v_ref, dsk_ref, sg_ref,
             gg_ref, gb_ref, ws_ref, bs_ref, dp_any, dzuv_ref, dy_ref, st_ref,
             dws_ref, dbs_ref, dvn_s):
        del dp_any
        dz_ref = dzuv_ref.at[:, OZ:OZ + D]
        du_ref = dzuv_ref.at[:, OU:OU + D]
        dv_ref = dzuv_ref.at[:, OV:OV + D]

        @pl.when(pl.program_id(0) == 0)
        def _():
            st_ref[...] = jnp.zeros_like(st_ref)
            dws_ref[...] = jnp.zeros_like(dws_ref)
            dbs_ref[...] = jnp.zeros_like(dbs_ref)

        z = z_ref[...]
        u = u_ref[...]
        v = v_ref[...]
        y, sz, r, nh, ug, tu, vg, tv, vhat, vrstd, vn = _mix_fwd_vals(
            yf_ref[...], yb_ref[...], z, xs_ref[...], u, v,
            dsk_ref[...], sg_ref[...], gg_ref[...], gb_ref[...])
        dys = dys_ref[...]
        st_ref[0:1, :] += _colsum(dys * nh)
        dn = dys * sg_ref[...]
        dhh = r * (dn - nh * jnp.mean(dn * nh, axis=-1, keepdims=True))
        dy_ref[...] = dhh * z * sz
        dz_ref[...] = (dhh * y * (sz * (1.0 + z * (1.0 - sz)))).astype(dz_ref.dtype)
        dym = dym_ref[...]
        lane = lax.broadcasted_iota(jnp.int32, (Q, 128), 1)
        dbs = jnp.zeros((Q, 128), F32)
        gu = _gelu_grad(u, tu)
        for n in range(TL // Q):
            rs_ = slice(n * Q, (n + 1) * Q)
            for g in range(8):
                cs = slice(g * 128, (g + 1) * 128)
                vb = vn[rs_, cs]
                mixed = _dot(ws_ref[g], vb, NN) + bs_ref[:, g:g + 1]
                dyb = dym[rs_, cs]
                dmx = dyb * ug[rs_, cs]
                du_ref[rs_, cs] = (dyb * mixed * gu[rs_, cs]).astype(du_ref.dtype)
                dvn_s[rs_, cs] = _dot(ws_ref[g], dmx, TN)
                dws_ref[g] += _dot(dmx, vb, NT)
                dbs = dbs + jnp.where(lane == g, jnp.sum(dmx, axis=1, keepdims=True), 0.0)
        dbs_ref[...] += dbs
        dvn = dvn_s[...]
        st_ref[1:2, :] += _colsum(dvn * vhat)
        st_ref[2:3, :] += _colsum(dvn)
        dvg = _ln_bwd(dvn * gg_ref[...], vhat, vrstd)
        dv_ref[...] = (dvg * _gelu_grad(v, tv)).astype(dv_ref.dtype)

    outs = pl.pallas_call(
        body, name="mix_bwd", grid=(nt,),
        in_specs=[_rt(D), _rt(D), _rt(D), _rt(D), _rt(D, OZ // D), _rt(D, 0), _rt(D, OU // D),
                  _rt(D, OV // D), _cst((1, D)), _cst((1, D)), _cst((1, D)), _cst((1, D)),
                  _cst((8, 128, 128)), _cst((128, 128)), pl.BlockSpec(memory_space=pl.ANY)],
        out_specs=[_rt(3 * D, 0), _rt(D), _cst((8, D)),
                   _cst((8, 128, 128)), _cst((128, 128))],
        out_shape=[jax.ShapeDtypeStruct(dp.shape, dp.dtype),
                   jax.ShapeDtypeStruct((L, D), F32), jax.ShapeDtypeStruct((8, D), F32),
                   jax.ShapeDtypeStruct((8, 128, 128), F32), jax.ShapeDtypeStruct((128, 128), F32)],
        scratch_shapes=[pltpu.VMEM((TL, D), F32)],
        input_output_aliases={14: 0},
        compiler_params=_params(("arbitrary",)),
    )(dys, dym, yf, yb, p, xbc, p, p, dsk, sg, gg, gb, ws, bsT, dp)
    return outs


def _merge_fwd(yssd, ygm, p, bg, ws, wg, wo, xn, modx, g1, b1):
    L = yssd.shape[0]
    tm = TL

    def body(ys_ref, yg_ref, g_ref, bg_ref, ws_ref, wg_ref, wo_ref, xn_ref, mx_ref, g1_ref, b1_ref,
             a1_ref, a2_ref, m_ref, o_ref, r1_ref, h2_ref):
        a1 = _dot(ys_ref[...], ws_ref[...], NN)
        a2 = _dot(yg_ref[...], wg_ref[...], NN)
        gt = _sig(g_ref[...] + bg_ref[...])
        mg = gt[:, :D] * a1 + gt[:, D:] * a2
        a1_ref[...] = a1
        a2_ref[...] = a2
        m_ref[...] = mg.astype(m_ref.dtype)
        out = _dot(mg, wo_ref[...], NN)
        o_ref[...] = out
        r1 = ALPHA * xn_ref[...] + mx_ref[2:3, :] * out
        xhat, _ = _ln(r1)
        x1 = xhat * g1_ref[...] + b1_ref[...]
        r1_ref[...] = r1
        h2_ref[...] = (x1 * (1.0 + mx_ref[4:5, :]) + mx_ref[3:4, :]).astype(h2_ref.dtype)

    rows = pl.BlockSpec((tm, D), lambda i: (i, 0))
    f32s = jax.ShapeDtypeStruct((L, D), F32)
    mxus = jax.ShapeDtypeStruct((L, D), _MXU)
    return pl.pallas_call(
        body, name="merge_fwd", grid=(L // tm,),
        in_specs=[rows, rows, pl.BlockSpec((tm, 2 * D), lambda i: (i, OG // (2 * D))), _cst((1, 2 * D)),
                  _cst((D, D)), _cst((D, D)), _cst((D, D)), rows, _cst((8, D)), _cst((1, D)), _cst((1, D))],
        out_specs=[rows] * 6,
        out_shape=[f32s, f32s, mxus, f32s, f32s, mxus],
        compiler_params=_params(("parallel",)),
    )(yssd, ygm, p, bg, ws, wg, wo, xn, modx, g1, b1)


def _mm_o2_res2(ff, w2, r1, tgt, modx, g1, b1, g2, b2):
    L, K = ff.shape
    tm = 2 * TL

    def body(a_ref, b_ref, r1_ref, t_ref, mx_ref, g1_ref, b1_ref, g2_ref, b2_ref,
             dr2_ref, do2_ref, st_ref, loss_ref):
        @pl.when(pl.program_id(0) == 0)
        def _():
            st_ref[...] = jnp.zeros_like(st_ref)
            loss_ref[...] = jnp.zeros_like(loss_ref)

        o2 = _dot(a_ref[...], b_ref[...], NN)
        xh1, _ = _ln(r1_ref[...])
        x1 = xh1 * g1_ref[...] + b1_ref[...]
        g2x = mx_ref[5:6, :]
        xh2, rstd2 = _ln(ALPHA * x1 + g2x * o2)
        err = xh2 * g2_ref[...] + b2_ref[...] - t_ref[...]
        per_tok = jnp.mean(err * err, axis=-1, keepdims=True)
        loss_ref[...] += 0.5 * jnp.sum(per_tok, axis=0, keepdims=True)
        dy = err * (1.0 / D)
        st_ref[0:1, :] += _colsum(dy * xh2)
        st_ref[1:2, :] += _colsum(dy)
        dr2 = _ln_bwd(dy * g2_ref[...], xh2, rstd2)
        st_ref[2:3, :] += _colsum(dr2 * o2)
        dr2_ref[...] = dr2
        do2_ref[...] = (g2x * dr2).astype(do2_ref.dtype)

    return pl.pallas_call(
        body, name="mm_o2_res2", grid=(L // tm,),
        in_specs=[_rt(K, rows=tm), _cst((K, D)), _rt(D, rows=tm), _rt(D, rows=tm), _cst((8, D))]
        + [_cst((1, D))] * 4,
        out_specs=[_rt(D, rows=tm), _rt(D, rows=tm), _cst((8, D)), _cst((8, 128))],
        out_shape=[jax.ShapeDtypeStruct((L, D), F32), jax.ShapeDtypeStruct((L, D), _MXU),
                   jax.ShapeDtypeStruct((8, D), F32), jax.ShapeDtypeStruct((8, 128), F32)],
        compiler_params=_params(("arbitrary",)),
    )(ff, w2, r1, tgt, modx, g1, b1, g2, b2)


def _mm_dh2_res1bwd(df13, w13i, dr2, r1, out, modx, g1, b1):
    L, K = df13.shape

    def body(a_ref, b_ref, dr2_ref, r1_ref, o_ref, mx_ref, g_ref, bb_ref, dr1_ref, do_ref, st_ref):
        @pl.when(pl.program_id(0) == 0)
        def _():
            st_ref[...] = jnp.zeros_like(st_ref)

        dh2 = _dot(a_ref[...], b_ref[...], NN)
        xh1, rstd1 = _ln(r1_ref[...])
        x1 = xh1 * g_ref[...] + bb_ref[...]
        dx1 = ALPHA * dr2_ref[...] + dh2 * (1.0 + mx_ref[4:5, :])
        st_ref[0:1, :] += _colsum(dh2 * x1)
        st_ref[1:2, :] += _colsum(dh2)
        st_ref[2:3, :] += _colsum(dx1 * xh1)
        st_ref[3:4, :] += _colsum(dx1)
        dr1 = _ln_bwd(dx1 * g_ref[...], xh1, rstd1)
        st_ref[4:5, :] += _colsum(dr1 * o_ref[...])
        dr1_ref[...] = dr1
        do_ref[...] = (mx_ref[2:3, :] * dr1).astype(do_ref.dtype)

    return pl.pallas_call(
        body, name="mm_dh2_res1bwd", grid=(L // TL,),
        in_specs=[_rt(K), _cst((K, D)), _rt(D), _rt(D), _rt(D), _cst((8, D)), _cst((1, D)), _cst((1, D))],
        out_specs=[_rt(D), _rt(D), _cst((8, D))],
        out_shape=[jax.ShapeDtypeStruct((L, D), F32), jax.ShapeDtypeStruct((L, D), _MXU),
                   jax.ShapeDtypeStruct((8, D), F32)],
        compiler_params=_params(("arbitrary",)),
    )(df13, w13i, dr2, r1, out, modx, g1, b1)


def _merge_bwd(dout, a1, a2, p, bg, wo, ws, wg, dp):
    L = a1.shape[0]
    tm = TL

    def body(do_ref, a1_ref, a2_ref, g_ref, bg_ref, wo_ref, ws_ref, wg_ref, dp_any,
             dg_ref, da1_ref, da2_ref, st_ref, dys_ref, dym_ref):
        del dp_any

        @pl.when(pl.program_id(0) == 0)
        def _():
            st_ref[...] = jnp.zeros_like(st_ref)

        dm = _dot(do_ref[...], wo_ref[...], NT)
        gt = _sig(g_ref[...] + bg_ref[...])
        g1 = gt[:, :D]
        g2 = gt[:, D:]
        da1 = (dm * g1).astype(da1_ref.dtype)
        da2 = (dm * g2).astype(da2_ref.dtype)
        da1_ref[...] = da1
        da2_ref[...] = da2
        dg1 = dm * a1_ref[...] * g1 * (1.0 - g1)
        dg2 = dm * a2_ref[...] * g2 * (1.0 - g2)
        st_ref[0:1, 0:D] += _colsum(dg1)
        st_ref[0:1, D:2 * D] += _colsum(dg2)
        dg_ref[:, 0:D] = dg1.astype(dg_ref.dtype)
        dg_ref[:, D:2 * D] = dg2.astype(dg_ref.dtype)
        dys_ref[...] = _dot(da1, ws_ref[...], NT)
        dym_ref[...] = _dot(da2, wg_ref[...], NT)

    rows = pl.BlockSpec((tm, D), lambda i: (i, 0))
    gates = pl.BlockSpec((tm, 2 * D), lambda i: (i, OG // (2 * D)))
    f32s = jax.ShapeDtypeStruct((L, D), F32)
    mxus = jax.ShapeDtypeStruct((L, D), _MXU)
    return pl.pallas_call(
        body, name="merge_bwd", grid=(L // tm,),
        in_specs=[rows, rows, rows, gates, _cst((1, 2 * D)), _cst((D, D)), _cst((D, D)), _cst((D, D)),
                  pl.BlockSpec(memory_space=pl.ANY)],
        out_specs=[gates, rows, rows, _cst((8, 2 * D)), rows, rows],
        out_shape=[jax.ShapeDtypeStruct(dp.shape, dp.dtype), mxus, mxus,
                   jax.ShapeDtypeStruct((8, 2 * D), F32), f32s, f32s],
        input_output_aliases={8: 0},
        compiler_params=_params(("arbitrary",)),
    )(dout, a1, a2, p, bg, wo, ws, wg, dp)


HFF = DFF // 2


def _mm_f13_glu(h2, w13i):
    L = h2.shape[0]
    tm = 512

    def body(a_ref, b_ref, f_ref, ff_ref):
        f = _dot(a_ref[...], b_ref[...], NT)
        f_ref[...] = f
        f1 = f[:, :HFF]
        ff_ref[...] = (f1 * _sig(f1) * f[:, HFF:]).astype(ff_ref.dtype)

    return pl.pallas_call(
        body, name="mm_f13_glu", grid=(DFF // HFF, L // tm),
        in_specs=[pl.BlockSpec((tm, D), lambda j, i: (i, 0)), pl.BlockSpec((2 * HFF, D), lambda j, i: (j, 0))],
        out_specs=[pl.BlockSpec((tm, 2 * HFF), lambda j, i: (i, j)), pl.BlockSpec((tm, HFF), lambda j, i: (i, j))],
        out_shape=[jax.ShapeDtypeStruct((L, 2 * DFF), F32), jax.ShapeDtypeStruct((L, DFF), _MXU)],
        compiler_params=_params(("parallel", "parallel")),
    )(h2, w13i)


def _mm_dff_glu(do2, w_ff2_f, f13i):
    L = do2.shape[0]
    tm = 512

    def body(a_ref, b_ref, f_ref, o_ref):
        d = _dot(a_ref[...], b_ref[...], NT)
        f1 = f_ref[:, :HFF]
        s = _sig(f1)
        o_ref[:, :HFF] = (d * f_ref[:, HFF:] * (s * (1.0 + f1 * (1.0 - s)))).astype(o_ref.dtype)
        o_ref[:, HFF:] = (d * f1 * s).astype(o_ref.dtype)

    return pl.pallas_call(
        body, name="mm_dff_glu", grid=(DFF // HFF, L // tm),
        in_specs=[pl.BlockSpec((tm, D), lambda j, i: (i, 0)), pl.BlockSpec((HFF, D), lambda j, i: (j, 0)),
                  pl.BlockSpec((tm, 2 * HFF), lambda j, i: (i, j))],
        out_specs=pl.BlockSpec((tm, 2 * HFF), lambda j, i: (i, j)),
        out_shape=jax.ShapeDtypeStruct((L, 2 * DFF), _MXU),
        compiler_params=_params(("parallel", "parallel")),
    )(do2, w_ff2_f, f13i)


def _conv_bwd(dxf, dxb, p, conv_w8, conv_b, dp):
    RT = p.shape[0]
    chunks = _seq_chunks(RT - TL)

    def body(df_ref, db_ref, p_ref, w_ref, b_ref, dp_any, o_ref, dw_ref, dbias_ref, dpre_s):
        del dp_any
        w = w_ref[...]
        bias = b_ref[...]
        srow = lax.broadcasted_iota(jnp.int32, (8, 128), 0)
        dwacc = jnp.zeros((8, 128), F32)
        dbacc = jnp.zeros((1, 128), F32)
        for r0, first, last in chunks:
            taps = _conv_taps(p_ref, r0, first, last)
            pre = bias + sum(w[k:k + 1, :] * taps[k] for k in range(5))
            s = _sig(pre)
            dpre = (df_ref[pl.ds(r0, TL), :] + db_ref[pl.ds(r0, TL), :]) * (s * (1.0 + pre * (1.0 - s)))
            dpre_s[pl.ds(r0, TL), :] = dpre
            dbacc = dbacc + _colsum(dpre)
            for k in range(5):
                dwacc = dwacc + jnp.where(srow == k, _colsum(dpre * taps[k]), 0.0)
        for r0, first, last in chunks:
            taps = _conv_taps(dpre_s, r0, first, last)
            dx = sum(w[k:k + 1, :] * taps[4 - k] for k in range(5))
            o_ref[pl.ds(r0, TL), :] = dx.astype(o_ref.dtype)
        dw_ref[...] = dwacc
        dbias_ref[...] = jnp.broadcast_to(dbacc, (8, 128))

    cspec = pl.BlockSpec((RT, 128), lambda j: (0, j))
    wspec = pl.BlockSpec((8, 128), lambda j: (0, j))
    return pl.pallas_call(
        body, name="conv_bwd", grid=(12,),
        in_specs=[cspec, cspec, pl.BlockSpec((RT, 128), lambda j: (0, _xbc_colblk(j))),
                  wspec, pl.BlockSpec((1, 128), lambda j: (0, j)), pl.BlockSpec(memory_space=pl.ANY)],
        out_specs=[pl.BlockSpec((RT, 128), lambda j: (0, _xbc_colblk(j))), wspec, wspec],
        out_shape=[jax.ShapeDtypeStruct(dp.shape, dp.dtype), jax.ShapeDtypeStruct((8, 1536), F32),
                   jax.ShapeDtypeStruct((8, 1536), F32)],
        scratch_shapes=[pltpu.VMEM((RT, 128), F32)],
        input_output_aliases={5: 0},
        compiler_params=_params(("parallel",)),
    )(dxf, dxb, p, conv_w8, conv_b, dp)


def _dp_fill(dp):
    nrow = dp.shape[0] // TL

    def body(dp_any, o_ref):
        del dp_any
        o_ref[...] = jnp.zeros_like(o_ref)

    return pl.pallas_call(
        body, name="dp_fill", grid=(1,),
        in_specs=[pl.BlockSpec(memory_space=pl.ANY)],
        out_specs=pl.BlockSpec((TL, OB), lambda i: (nrow - 1, 0)),
        out_shape=jax.ShapeDtypeStruct(dp.shape, dp.dtype),
        input_output_aliases={0: 0},
        compiler_params=_params(("arbitrary",)),
    )(dp)


def _dt_bwd(ddf, ddb, dp):
    RT = ddf.shape[0]
    assert NPJ - ODT == 256

    def body(f_ref, b_ref, dp_any, o_ref, st_ref):
        del dp_any

        @pl.when(pl.program_id(0) == 0)
        def _():
            st_ref[...] = jnp.zeros_like(st_ref)

        s = f_ref[...] + b_ref[...]
        o_ref[:, 0:128] = s.astype(o_ref.dtype)
        o_ref[:, 128:256] = jnp.zeros((TL, 128), o_ref.dtype)
        st_ref[0:1, :] += _colsum(s)

    return pl.pallas_call(
        body, name="dt_bwd", grid=(RT // TL,),
        in_specs=[_rt(128), _rt(128), pl.BlockSpec(memory_space=pl.ANY)],
        out_specs=[_rt(256, ODT // 256), _cst((8, 128))],
        out_shape=[jax.ShapeDtypeStruct(dp.shape, dp.dtype), jax.ShapeDtypeStruct((8, 128), F32)],
        input_output_aliases={2: 0},
        compiler_params=_params(("arbitrary",)),
    )(ddf, ddb, dp)


def _mm_dh1_ln0bwd(dp, w_in_p, dr1, x, ctx, g, b, modx, modc):
    L = x.shape[0]
    nt = L // TL

    def body(dp_ref, w_ref, dr1_ref, x_ref, c_ref, g_ref, b_ref, mx_ref, mc_ref, gx_ref, st_ref):
        i = pl.program_id(0)
        isc = i == nt

        @pl.when(i == 0)
        def _():
            st_ref[...] = jnp.zeros_like(st_ref)

        xin = jnp.where(isc, c_ref[...], x_ref[...])
        xhat, rstd = _ln(xin)
        xn = xhat * g_ref[...] + b_ref[...]
        sc = jnp.where(isc, mc_ref[1:2, :], mx_ref[1:2, :])
        dh = _dot(dp_ref[...], w_ref[...], NT)
        lat = jnp.where(isc, 0.0, 1.0)
        dxn = dh * (1.0 + sc) + (lat * ALPHA) * dr1_ref[...]
        tsh = _colsum(dh)
        tsc = _colsum(dh * xn)
        st_ref[0:1, :] += lat * tsh
        st_ref[1:2, :] += lat * tsc
        st_ref[2:3, :] += (1.0 - lat) * tsh
        st_ref[3:4, :] += (1.0 - lat) * tsc
        st_ref[4:5, :] += _colsum(dxn * xhat)
        st_ref[5:6, :] += _colsum(dxn)

        @pl.when(i < nt)
        def _():
            gx_ref[...] = _ln_bwd(dxn * g_ref[...], xhat, rstd)

    return pl.pallas_call(
        body, name="mm_dh1_ln0bwd", grid=(nt + 1,),
        in_specs=[_rt(NPJ), _cst((D, NPJ)), _rtc(D, nt), _rtc(D, nt), _cst((TL, D)), _cst((1, D)), _cst((1, D)),
                  _cst((8, D)), _cst((8, D))],
        out_specs=[_rtc(D, nt), _cst((8, D))],
        out_shape=[jax.ShapeDtypeStruct((L, D), F32), jax.ShapeDtypeStruct((8, D), F32)],
        compiler_params=pltpu.CompilerParams(dimension_semantics=("arbitrary",),
                                             vmem_limit_bytes=VMEM_LIMIT + 8 * 1024 * 1024),
    )(dp, w_in_p, dr1, x, ctx, g, b, modx, modc)


SECTIONS = ((0, 1024, OZ), (1024, 2048, OXS), (2048, 2304, OB), (2304, 2560, OC), (2560, 2592, ODT),
            (2592, 3616, OU), (3616, 4640, OV), (4640, 6688, OG))


def _perm_from_blocks(ga):
    n = ga.shape[2]
    pieces = []
    for na, nb, _ in sorted(SECTIONS, key=lambda sec: sec[2]):
        for k in range(NDEV):
            lo, hi = max(na, k * n), min(nb, (k + 1) * n)
            if lo < hi:
                pieces.append(ga[k][:, lo - k * n:hi - k * n])
    pieces.append(jnp.zeros((ga.shape[1], NPJ - NNAT), ga.dtype))
    return jnp.concatenate(pieces, axis=1)


def _blocks_from_perm(gp, n):
    blocks = []
    for k in range(NDEV):
        pieces = []
        for na, nb, po in SECTIONS:
            lo, hi = max(na, k * n), min(nb, (k + 1) * n)
            if lo < hi:
                pieces.append(gp[:, po + lo - na:po + hi - na])
        blocks.append(jnp.concatenate(pieces, axis=1))
    return jnp.stack(blocks)


def _padded(n, row_align):
    unit = row_align * D
    return -(-n // unit) * unit if row_align else n


def _slab(arrs, rows, row_align=0):
    parts = []
    for a in arrs:
        f = a.reshape(-1)
        parts.append(jnp.pad(f, (0, _padded(f.shape[0], row_align) - f.shape[0])))
    flat = jnp.concatenate(parts)
    flat = jnp.pad(flat, (0, rows * D - flat.shape[0]))
    return flat.reshape(rows, D)


def _unslab(slab, shapes, row_align=0):
    out, off = [], 0
    for shp in shapes:
        n = 1
        for s in shp:
            n *= s
        r0, r1 = off // D, -(-(off + n) // D)
        out.append(slab[r0:r1].reshape(-1)[off - r0 * D:off - r0 * D + n].reshape(shp))
        off += _padded(n, row_align)
    return out


def _row(v):
    return v.reshape(1, -1)


def _t(a):
    return jnp.swapaxes(a, 0, 1)


def _pad_rows(a, rows):
    return jnp.pad(a, ((0, rows - a.shape[0]), (0, 0)))


REPL = ["c_ctx", "ln0_g", "ln0_b", "b_ada", "conv_b", "dt_bias", "a_log", "d_skip", "ssd_norm_g",
        "gm_norm_g", "gm_norm_b", "w_spatial", "b_spatial", "b_gate", "ln1_g", "ln1_b", "ln2_g", "ln2_b"]
SMALL_ROWS = 160
WEIGHTS = ["c_ctx", "ln0_g", "ln0_b", "w_ada", "b_ada", "w_in", "conv_w", "conv_b", "dt_bias", "a_log",
           "d_skip", "ssd_norm_g", "gm_norm_g", "gm_norm_b", "w_spatial", "b_spatial", "b_gate",
           "w_ssd_proj", "w_gm_proj", "w_out", "ln1_g", "ln1_b", "w_ff1", "w_ff3", "w_ff2", "ln2_g", "ln2_b"]


def kernel(x, c, ctx, c_ctx, ln0_g, ln0_b, w_ada, b_ada, w_in, conv_w, conv_b, dt_bias, a_log, d_skip, ssd_norm_g, gm_norm_g, gm_norm_b, w_spatial, b_spatial, b_gate, w_ssd_proj, w_gm_proj, w_out, ln1_g, ln1_b, w_ff1, w_ff3, w_ff2, ln2_g, ln2_b, loss_target, m_c_ctx, m_ln0_g, m_ln0_b, m_w_ada, m_b_ada, m_w_in, m_conv_w, m_conv_b, m_dt_bias, m_a_log, m_d_skip, m_ssd_norm_g, m_gm_norm_g, m_gm_norm_b, m_w_spatial, m_b_spatial, m_b_gate, m_w_ssd_proj, m_w_gm_proj, m_w_out, m_ln1_g, m_ln1_b, m_w_ff1, m_w_ff3, m_w_ff2, m_ln2_g, m_ln2_b, v_c_ctx, v_ln0_g, v_ln0_b, v_w_ada, v_b_ada, v_w_in, v_conv_w, v_conv_b, v_dt_bias, v_a_log, v_d_skip, v_ssd_norm_g, v_gm_norm_g, v_gm_norm_b, v_w_spatial, v_b_spatial, v_b_gate, v_w_ssd_proj, v_w_gm_proj, v_w_out, v_ln1_g, v_ln1_b, v_w_ff1, v_w_ff3, v_w_ff2, v_ln2_g, v_ln2_b):
    W = dict(c_ctx=c_ctx, ln0_g=ln0_g, ln0_b=ln0_b, w_ada=w_ada, b_ada=b_ada, w_in=w_in, conv_w=conv_w,
             conv_b=conv_b, dt_bias=dt_bias, a_log=a_log, d_skip=d_skip, ssd_norm_g=ssd_norm_g,
             gm_norm_g=gm_norm_g, gm_norm_b=gm_norm_b, w_spatial=w_spatial, b_spatial=b_spatial,
             b_gate=b_gate, w_ssd_proj=w_ssd_proj, w_gm_proj=w_gm_proj, w_out=w_out, ln1_g=ln1_g,
             ln1_b=ln1_b, w_ff1=w_ff1, w_ff3=w_ff3, w_ff2=w_ff2, ln2_g=ln2_g, ln2_b=ln2_b)
    M = dict(c_ctx=m_c_ctx, ln0_g=m_ln0_g, ln0_b=m_ln0_b, w_ada=m_w_ada, b_ada=m_b_ada, w_in=m_w_in,
             conv_w=m_conv_w, conv_b=m_conv_b, dt_bias=m_dt_bias, a_log=m_a_log, d_skip=m_d_skip,
             ssd_norm_g=m_ssd_norm_g, gm_norm_g=m_gm_norm_g, gm_norm_b=m_gm_norm_b,
             w_spatial=m_w_spatial, b_spatial=m_b_spatial, b_gate=m_b_gate, w_ssd_proj=m_w_ssd_proj,
             w_gm_proj=m_w_gm_proj, w_out=m_w_out, ln1_g=m_ln1_g, ln1_b=m_ln1_b, w_ff1=m_w_ff1,
             w_ff3=m_w_ff3, w_ff2=m_w_ff2, ln2_g=m_ln2_g, ln2_b=m_ln2_b)
    V = dict(c_ctx=v_c_ctx, ln0_g=v_ln0_g, ln0_b=v_ln0_b, w_ada=v_w_ada, b_ada=v_b_ada, w_in=v_w_in,
             conv_w=v_conv_w, conv_b=v_conv_b, dt_bias=v_dt_bias, a_log=v_a_log, d_skip=v_d_skip,
             ssd_norm_g=v_ssd_norm_g, gm_norm_g=v_gm_norm_g, gm_norm_b=v_gm_norm_b,
             w_spatial=v_w_spatial, b_spatial=v_b_spatial, b_gate=v_b_gate, w_ssd_proj=v_w_ssd_proj,
             w_gm_proj=v_w_gm_proj, w_out=v_w_out, ln1_g=v_ln1_g, ln1_b=v_ln1_b, w_ff1=v_w_ff1,
             w_ff3=v_w_ff3, w_ff2=v_w_ff2, ln2_g=v_ln2_g, ln2_b=v_ln2_b)

    me = 4 * lax.axis_index("x") + 2 * lax.axis_index("y") + lax.axis_index("c")
    xl, cx, tgt = x[0], ctx[0], loss_target[0]
    L = xl.shape[0]
    assert cx.shape[0] == TL and L % TL == 0
    ada_n = w_ada.shape[2]
    cw_n = conv_w.shape[2]

    small1 = _pad_rows(jnp.concatenate([c, _slab([conv_w[0]], 1)], axis=0), 8)
    g1 = _all_gather(small1, "ag_small")
    c_all = g1[:, 0, :]
    conv_w_full = g1[:, 1, :5 * cw_n].reshape(NDEV, 5, cw_n).transpose(1, 0, 2).reshape(5, NDEV * cw_n)
    sq = w_ssd_proj.shape[1]
    ffr = w_ff2.shape[1]
    ffc = w_ff1.shape[2]
    late = [jnp.concatenate([w_ssd_proj[0], w_gm_proj[0], w_out[0], w_ff2[0]], axis=0).astype(_MXU),
            _t(w_ff1[0]).astype(_MXU), _t(w_ff3[0]).astype(_MXU)]

    c16 = _pad_rows(jnp.concatenate([c_all, _row(c_ctx)], axis=0), 16)
    b_ada_sh = lax.dynamic_slice(b_ada, (0, ada_n * me), (1, ada_n))
    modp = _ada_fwd(c16, w_ada[0], b_ada_sh)
    mod16 = _all_gather(modp, "ag_mod").transpose(1, 0, 2).reshape(16, NDEV * ada_n)

    ga, = _all_gather_multi([w_in[0].astype(_MXU)], "ag_w_in")
    ga, late, mod16 = lax.optimization_barrier((ga, late, mod16))
    lw_send, lw_recv, lw_src, lw_land, lw_token = _exchange_start(late, "ag_late_start", gather=True)
    w_in_p = _perm_from_blocks(ga)
    modx = _pad_rows(lax.dynamic_slice(mod16, (me, 0), (1, 6 * D)).reshape(6, D), 8) + lw_token[0, 0]
    modc = _pad_rows(mod16[8].reshape(6, D), 8)

    g0, b0 = _row(ln0_g), _row(ln0_b)
    xn, h1 = _ln0_fwd(xl, cx, g0, b0, modx, modc)
    p = _mm(h1, w_in_p, "nn", F32, "mm_p", tm_cap=(L + TL) // 2)
    conv_w8 = _pad_rows(conv_w_full, 8)
    xbc = _conv_fwd(p, conv_w8, conv_b)
    prm = _pad_rows(jnp.pad(jnp.stack([dt_bias.reshape(32), a_log.reshape(32)]), ((0, 0), (0, 96))), 8)
    yf, yb, hpf, hpb = _ssd2_fwd(xbc, p, prm)
    lw_land = _exchange_wait(lw_send, lw_recv, lw_src, lw_land, yf, "ag_late_wait", gather=True)
    fw_send, fw_recv, lw_land, fw_token = _forward_start(lw_land, "ag_fwd_start")
    dsk = _row(jnp.repeat(d_skip[0, 0] + d_skip[0, 1], HP)) + fw_token[0:1, 0:1]
    ws_m = w_spatial[0].astype(_MXU)
    bsT = jnp.pad(b_spatial[0].T, ((0, 0), (0, 120)))
    mixp = (dsk, ssd_norm_g, gm_norm_g, gm_norm_b, ws_m, bsT)
    yssd, ygm = _mix_fwd(yf, yb, p, xbc, *mixp)
    gb, gc1, gc2 = _forward_wait(fw_send, fw_recv, lw_land, yssd, "ag_fwd_wait")

    gb, gc1, gc2 = lax.optimization_barrier(
        [lax.dynamic_update_index_in_dim(g, mine, me, 0) for g, mine in zip((gb, gc1, gc2), late)])
    w_ssd_f = gb[:, 0:sq].reshape(NDEV * sq, D)
    w_gm_f = gb[:, sq:2 * sq].reshape(NDEV * sq, D)
    w_out_f = gb[:, 2 * sq:3 * sq].reshape(NDEV * sq, D)
    w_ff2_f = gb[:, 3 * sq:3 * sq + ffr].reshape(NDEV * ffr, D)
    assert HFF == (NDEV // 2) * ffc
    hd = NDEV // 2
    w13i = jnp.concatenate([g[k] for t in range(2) for g in (gc1, gc2) for k in range(t * hd, (t + 1) * hd)],
                           axis=0)
    a1, a2, merged, out, r1, h2 = _merge_fwd(yssd, ygm, p, b_gate, w_ssd_f, w_gm_f, w_out_f,
                                             xn, modx, ln1_g, ln1_b)
    f13, ff = _mm_f13_glu(h2, w13i)

    dr2, do2, st2, loss_slab = _mm_o2_res2(ff, w_ff2_f, r1, tgt, modx, ln1_g, ln1_b, ln2_g, ln2_b)
    loss = lax.psum(loss_slab[0, 0], ("x", "y", "c"))
    df13 = _mm_dff_glu(do2, w_ff2_f, f13)
    dw_ff2 = _mm(ff, do2, "tn", _MXU, "mm_dw_ff2", tm_cap=512, tk_cap=L)
    dw13i = _mm(df13, h2, "tn", _MXU, "mm_dw13", tm_cap=512, tk_cap=L)

    def owner_blocks(first):
        return jnp.concatenate([dw13i[t * 2 * HFF + first:t * 2 * HFF + first + HFF].reshape(NDEV // 2, ffc, D)
                                for t in range(2)], axis=0)

    xff = [dw_ff2.reshape(NDEV, ffr, D), owner_blocks(0), owner_blocks(HFF)]
    ff_send, ff_recv, ff_src, ff_land, ff_token = _exchange_start(xff, "xchg_ff_start")
    modx = modx + ff_token[0, 0]
    dr1, dout, st1 = _mm_dh2_res1bwd(df13, w13i, dr2, r1, out, modx, ln1_g, ln1_b)
    dw_out = _mm(merged, dout, "tn", _MXU, "mm_dw_out")
    dp = _dp_fill(lax.empty((L + TL, NPJ), _MXU))
    dp, da1, da2, stg, dys, dym = _merge_bwd(dout, a1, a2, p, b_gate, w_out_f, w_ssd_f, w_gm_f, dp)
    dw_ssd = _mm(yssd, da1, "tn", _MXU, "mm_dw_ssd")
    dw_gm = _mm(ygm, da2, "tn", _MXU, "mm_dw_gm")
    xsq = [jnp.concatenate([dw_ssd.reshape(NDEV, sq, D), dw_gm.reshape(NDEV, sq, D),
                            dw_out.reshape(NDEV, sq, D)], axis=1)]
    sq_send, sq_recv, sq_src, sq_land, sq_token = _exchange_start(xsq, "xchg_sq_start")
    mixp = (dsk + sq_token[0:1, 0:1],) + mixp[1:]
    dp, dyd, stm, dws, dbsT = _mix_bwd(dys, dym, yf, yb, p, xbc, dp, *mixp)
    dxf, dxb, ddf, ddb, sts = _ssd2_bwd(xbc, p, prm, dsk, dyd, hpf, hpb)
    dp, dcw, dcb = _conv_bwd(dxf, dxb, p, conv_w8, conv_b, dp)
    dp, std = _dt_bwd(ddf, ddb, dp)
    hw = D // 2
    xin_a = [_blocks_from_perm(_mm(h1[:, :hw], dp, "tn", _MXU, "mm_dw_in_a", tk_cap=L + TL), w_in.shape[2])]
    ina_send, ina_recv, ina_src, ina_land, ina_token = _exchange_start(xin_a, "xchg_in_a_start")
    h1b, ina_token = lax.optimization_barrier((h1[:, hw:], ina_token))
    xin_b = [_blocks_from_perm(_mm(h1b, dp, "tn", _MXU, "mm_dw_in_b", tk_cap=L + TL), w_in.shape[2])]
    inb_send, inb_recv, inb_src, inb_land, inb_token = _exchange_start(xin_b, "xchg_in_b_start")
    modx = modx + (ina_token[0, 0] + inb_token[0, 0])
    grad_x, st0 = _mm_dh1_ln0bwd(dp, w_in_p, dr1, xl, cx, g0, b0, modx, modc)

    zero = jnp.zeros((D,), F32)
    dmod = jnp.stack([jnp.concatenate([st0[0], st0[1], st1[4], st1[1], st1[0], st2[2]]),
                      jnp.concatenate([st0[2], st0[3], zero, zero, zero, zero])])
    g16 = _all_gather(_pad_rows(dmod, 8), "ag_dmod")[:, 0:2, :].reshape(16, 6 * D)
    g16_sh = lax.dynamic_slice(g16, (0, ada_n * me), (16, ada_n))
    c16b = jnp.stack([c_all, jnp.broadcast_to(_row(c_ctx), (NDEV, D))], axis=1).reshape(16, D)
    dw_ada, db_ada8, dcc8 = _ada_bwd(c16b, g16, g16_sh, w_ada[0])

    part = dict(
        c_ctx=dcc8[0], ln0_g=st0[4], ln0_b=st0[5], conv_w=dcw[0:5], conv_b=dcb[0],
        dt_bias=std[0, 0:32], a_log=sts[0, 0:32], d_skip=jnp.tile(sts[1, 0:16], 2),
        ssd_norm_g=stm[0], gm_norm_g=stm[1], gm_norm_b=stm[2], w_spatial=dws,
        b_spatial=dbsT[:, 0:8].T, b_gate=stg[0], ln1_g=st1[2], ln1_b=st1[3], ln2_g=st2[0], ln2_b=st2[1])
    pnames = list(part)
    psum8 = _sum8(_all_gather(_slab([part[n] for n in pnames], SMALL_ROWS), "ag_smallgrads"), "sum_smallgrads")
    small = dict(zip(pnames, _unslab(psum8, [part[n].shape for n in pnames])))
    grads = {n: small[n].reshape(W[n].shape) for n in pnames if n != "conv_w"}
    grads["conv_w"] = lax.dynamic_slice(small["conv_w"], (0, cw_n * me), (5, cw_n)).reshape(conv_w.shape)
    grads["b_ada"] = db_ada8[0:1]
    grads["w_ada"] = dw_ada.reshape(w_ada.shape)

    delta, new_m, new_v = {}, {}, {}

    def adam_group(names, rows, tag, align=0):
        shapes = [W[n].shape for n in names]
        outs = _adamw(*[_slab([src[n] for n in names], rows, align) for src in (grads, W, M, V)], tag)
        for res, slab in zip((delta, new_m, new_v), outs):
            for n, a in zip(names, _unslab(slab, shapes, align)):
                res[n] = a

    adam_group(REPL + ["conv_w"], SMALL_ROWS, "adamw_small")
    res = _adamw(grads["w_ada"][0], w_ada[0], m_w_ada[0], v_w_ada[0], "adamw_w_ada")
    delta["w_ada"], new_m["w_ada"], new_v["w_ada"] = [a[None] for a in res]

    rff = _exchange_wait(ff_send, ff_recv, ff_src, ff_land, st0, "xchg_ff_wait")
    rsq = _exchange_wait(sq_send, sq_recv, sq_src, sq_land, rff[0], "xchg_sq_wait")
    rin_a = _exchange_wait(ina_send, ina_recv, ina_src, ina_land, delta["ln2_b"], "xchg_in_a_wait")
    rin_b = _exchange_wait(inb_send, inb_recv, inb_src, inb_land, rin_a[0], "xchg_in_b_wait")
    rin = jnp.concatenate([rin_a[0], rin_b[0]], axis=1)

    def own(blocks):
        return lax.dynamic_index_in_dim(blocks, me, 0, keepdims=False)

    own_in = jnp.concatenate([own(xin_a[0]), own(xin_b[0])], axis=0)
    for n, r8, mine, row0, tr in (
            ("w_ff2", rff[0], own(xff[0]), 0, ffr // 2), ("w_ssd_proj", rsq[0], own(xsq[0]), 0, sq),
            ("w_gm_proj", rsq[0], own(xsq[0]), sq, sq), ("w_out", rsq[0], own(xsq[0]), 2 * sq, sq),
            ("w_in", rin, own_in, 0, 256)):
        res = _adamw_sum(r8, mine, W[n][0], M[n][0], V[n][0], row0, tr, "adamw_" + n)
        grads[n], delta[n], new_m[n], new_v[n] = [a[None] for a in res]
    for n, r8, mine in (("w_ff1", rff[1], own(xff[1])), ("w_ff3", rff[2], own(xff[2]))):
        res = _adamw_sum(r8, mine, _t(W[n][0]), _t(M[n][0]), _t(V[n][0]), 0, ffc // 2, "adamw_" + n)
        grads[n], delta[n], new_m[n], new_v[n] = [_t(a)[None] for a in res]

    return (loss, grad_x[None], *[grads[n] for n in WEIGHTS], *[delta[n] for n in WEIGHTS],
            *[new_m[n] for n in WEIGHTS], *[new_v[n] for n in WEIGHTS])
```

```python
import jax
import jax.numpy as jnp
from jax import lax
from jax.experimental import pallas as pl
from jax.experimental.pallas import tpu as pltpu

_MXU = jnp.bfloat16
F32 = jnp.float32
D = 1024
TL = 256
Q = 128
NH, HP, NS, HPG = 16, 64, 128, 8
DFF = 2816
ALPHA = 2.0 ** 0.25
EPS = 1e-5
OZ, OU, OV, OXS, OG, OB, OC, ODT, NPJ = 0, 1024, 2048, 3072, 4096, 6144, 6400, 6656, 6912
NNAT = 6688
NDEV = 8
ADAM_LR, ADAM_B1, ADAM_B2, ADAM_EPS, ADAM_WD, ADAM_STEP = 1e-3, 0.9, 0.999, 1e-8, 0.01, 10
VMEM_LIMIT = 48 * 1024 * 1024

NN = ((1,), (0,))
NT = ((1,), (1,))
TN = ((0,), (0,))
MESH = pl.DeviceIdType.MESH


def _dot(a, b, dims):
    return lax.dot_general(a.astype(_MXU), b.astype(_MXU), (dims, ((), ())),
                           preferred_element_type=F32)


def _tile(n, cands):
    for c in cands:
        if n % c == 0:
            return c
    return n


def _divisor_tile(n, cap, mult):
    best = n
    for t in range(mult, min(n, cap) + 1, mult):
        if n % t == 0:
            best = t
    return best


def _params(sem):
    return pltpu.CompilerParams(dimension_semantics=sem, vmem_limit_bytes=VMEM_LIMIT)


def _cst(shape):
    nd = len(shape)
    return pl.BlockSpec(shape, lambda *_: (0,) * nd)


def _rt(w, cb=0, rows=TL):
    return pl.BlockSpec((rows, w), lambda i: (i, cb))


def _rtc(w, nt, cb=0):
    return pl.BlockSpec((TL, w), lambda i: (jnp.minimum(i, nt - 1), cb))


def _sig(x):
    return jax.nn.sigmoid(x)


def _softplus(x):
    return jnp.maximum(x, 0.0) + jnp.log1p(jnp.exp(-jnp.abs(x)))


_G0, _G1 = 0.7978845608028654, 0.044715


def _gelu(x):
    t = jnp.tanh(_G0 * (x + _G1 * x * x * x))
    return 0.5 * x * (1.0 + t), t


def _gelu_grad(x, t):
    return 0.5 * (1.0 + t) + 0.5 * x * (1.0 - t * t) * _G0 * (1.0 + 3.0 * _G1 * x * x)


def _ln(r):
    mu = jnp.mean(r, axis=-1, keepdims=True)
    xc = r - mu
    var = jnp.mean(xc * xc, axis=-1, keepdims=True)
    rstd = lax.rsqrt(var + EPS)
    return xc * rstd, rstd


def _ln_bwd(dyh, xhat, rstd):
    return rstd * (dyh - jnp.mean(dyh, axis=-1, keepdims=True)
                   - xhat * jnp.mean(dyh * xhat, axis=-1, keepdims=True))


def _colsum(v):
    return jnp.sum(v, axis=0, keepdims=True)


def _cumsum_rows(a, rev):
    n = a.shape[0]
    row = lax.broadcasted_iota(jnp.int32, a.shape, 0)
    s = 1
    while s < n:
        if rev:
            a = a + jnp.where(row < n - s, pltpu.roll(a, n - s, 0), 0.0)
        else:
            a = a + jnp.where(row >= s, pltpu.roll(a, s, 0), 0.0)
        s *= 2
    return a


def _mm(a, b, mode, out_dtype, name, tm_cap=None, tk_cap=2304):
    if mode == "tn":
        K, M = a.shape
    else:
        M, K = a.shape
    N = b.shape[0] if mode == "nt" else b.shape[1]
    if mode == "tn":
        tm = _divisor_tile(M, tm_cap or 1408, 128)
    else:
        tm = _divisor_tile(M, tm_cap or 1088, 16)
    tn = _divisor_tile(N, 1408, 128)
    tk = _divisor_tile(K, tk_cap, 128)
    nk = K // tk
    dims = {"nn": NN, "nt": NT, "tn": TN}[mode]
    use_acc = nk > 1 and out_dtype != F32

    def body(a_ref, b_ref, o_ref, *acc):
        prod = _dot(a_ref[...], b_ref[...], dims)
        if nk == 1:
            o_ref[...] = prod.astype(o_ref.dtype)
            return
        acc_ref = acc[0] if use_acc else o_ref
        k = pl.program_id(2)

        @pl.when(k == 0)
        def _():
            acc_ref[...] = prod

        if use_acc:
            @pl.when((k > 0) & (k < nk - 1))
            def _():
                acc_ref[...] += prod

            @pl.when(k == nk - 1)
            def _():
                o_ref[...] = (acc_ref[...] + prod).astype(o_ref.dtype)
        else:
            @pl.when(k > 0)
            def _():
                o_ref[...] += prod

    if mode == "tn":
        a_spec = pl.BlockSpec((tk, tm), lambda i, j, k: (k, i))
    else:
        a_spec = pl.BlockSpec((tm, tk), lambda i, j, k: (i, k))
    if mode == "nt":
        b_spec = pl.BlockSpec((tn, tk), lambda i, j, k: (j, k))
    else:
        b_spec = pl.BlockSpec((tk, tn), lambda i, j, k: (k, j))
    return pl.pallas_call(
        body, name=name, grid=(M // tm, N // tn, nk),
        in_specs=[a_spec, b_spec],
        out_specs=pl.BlockSpec((tm, tn), lambda i, j, k: (i, j)),
        out_shape=jax.ShapeDtypeStruct((M, N), out_dtype),
        scratch_shapes=[pltpu.VMEM((tm, tn), F32)] if use_acc else [],
        compiler_params=_params(("parallel", "parallel", "arbitrary")),
    )(a, b)


def _all_gather(x, name):
    def body(x_ref, out_ref, send_sems, recv_sems, local_sem):
        mx, my, mc = lax.axis_index("x"), lax.axis_index("y"), lax.axis_index("c")
        me, sibling = (mx, my, mc), (mx, my, 1 - mc)
        chips = [(1 - mx, my), (mx, 1 - my), (1 - mx, 1 - my)]

        def slot(px, py, pc):
            return out_ref.at[4 * px + 2 * py + pc]

        def copy(k, block, to, src=None):
            return pltpu.make_async_remote_copy(
                src_ref=slot(*block) if src is None else src, dst_ref=slot(*block),
                send_sem=send_sems.at[k], recv_sem=recv_sems.at[k],
                device_id=to, device_id_type=MESH)

        mine = pltpu.make_async_copy(x_ref, slot(*me), local_sem)
        mine.start()
        first = [copy(0, me, sibling, src=x_ref)]
        first += [copy(1 + j, me, (*chip, mc), src=x_ref) for j, chip in enumerate(chips)]
        for cp in first:
            cp.start()
        passed = [copy(4 + j, (*chip, mc), sibling) for j, chip in enumerate(chips)]
        for j, chip in enumerate(chips):
            copy(1 + j, (*chip, mc), me).wait_recv()
            passed[j].start()
        copy(0, sibling, me).wait_recv()
        for j, chip in enumerate(chips):
            copy(4 + j, (*chip, 1 - mc), me).wait_recv()
        for cp in first + passed:
            cp.wait_send()
        mine.wait()

    return pl.pallas_call(
        body, name=name,
        out_shape=jax.ShapeDtypeStruct((NDEV,) + x.shape, x.dtype),
        in_specs=[pl.BlockSpec(memory_space=pl.ANY)],
        out_specs=pl.BlockSpec(memory_space=pl.ANY),
        scratch_shapes=[pltpu.SemaphoreType.DMA((7,)), pltpu.SemaphoreType.DMA((7,)),
                        pltpu.SemaphoreType.DMA],
    )(x)


def _any_specs(n):
    return [pl.BlockSpec(memory_space=pl.ANY)] * n


def _all_gather_multi(xs, name):
    na = len(xs)

    def body(*refs):
        x_refs, out_refs = refs[:na], refs[na:2 * na]
        send_sems, recv_sems, local_sems = refs[2 * na:]
        mx, my, mc = lax.axis_index("x"), lax.axis_index("y"), lax.axis_index("c")
        me, sibling = (mx, my, mc), (mx, my, 1 - mc)
        chips = [(1 - mx, my), (mx, 1 - my), (1 - mx, 1 - my)]

        def copy(a, k, block, to, src=None):
            slot = out_refs[a].at[4 * block[0] + 2 * block[1] + block[2]]
            return pltpu.make_async_remote_copy(
                src_ref=slot if src is None else src, dst_ref=slot,
                send_sem=send_sems.at[7 * a + k], recv_sem=recv_sems.at[7 * a + k],
                device_id=to, device_id_type=MESH)

        mine = [pltpu.make_async_copy(x_refs[a], out_refs[a].at[4 * mx + 2 * my + mc], local_sems.at[a])
                for a in range(na)]
        for cp in mine:
            cp.start()
        first = []
        for a in range(na):
            first.append(copy(a, 0, me, sibling, src=x_refs[a]))
            first += [copy(a, 1 + j, me, (*chip, mc), src=x_refs[a]) for j, chip in enumerate(chips)]
        for cp in first:
            cp.start()
        passed = []
        for a in range(na):
            for j, chip in enumerate(chips):
                copy(a, 1 + j, (*chip, mc), me).wait_recv()
                fwd = copy(a, 4 + j, (*chip, mc), sibling)
                fwd.start()
                passed.append(fwd)
        for a in range(na):
            copy(a, 0, sibling, me).wait_recv()
            for j, chip in enumerate(chips):
                copy(a, 4 + j, (*chip, 1 - mc), me).wait_recv()
        for cp in first + passed:
            cp.wait_send()
        for cp in mine:
            cp.wait()

    return pl.pallas_call(
        body, name=name,
        out_shape=[jax.ShapeDtypeStruct((NDEV,) + x.shape, x.dtype) for x in xs],
        in_specs=_any_specs(na), out_specs=_any_specs(na),
        scratch_shapes=[pltpu.SemaphoreType.DMA((7 * na,)), pltpu.SemaphoreType.DMA((7 * na,)),
                        pltpu.SemaphoreType.DMA((na,))],
    )(*xs)


def _adamw_sum(r8, own, w, m, v, row0, tr, name):
    R, C = w.shape
    assert row0 % tr == 0
    blk0 = row0 // tr
    bc1 = 1.0 - ADAM_B1 ** ADAM_STEP
    bc2 = 1.0 - ADAM_B2 ** ADAM_STEP

    def body(r_ref, *refs):
        if own is None:
            gg = r_ref[0].astype(F32)
        else:
            gg = refs[0][...].astype(F32)
            refs = refs[1:]
        w_ref, m_ref, v_ref, g_ref, d_ref, mo_ref, vo_ref = refs
        for k in range(1, NDEV):
            gg = gg + r_ref[k].astype(F32)
        mn = ADAM_B1 * m_ref[...] + (1.0 - ADAM_B1) * gg
        vn = ADAM_B2 * v_ref[...] + (1.0 - ADAM_B2) * (gg * gg)
        mh = mn / bc1
        vh = vn / bc2
        g_ref[...] = gg
        d_ref[...] = -ADAM_LR * (mh / (jnp.sqrt(vh) + ADAM_EPS) + ADAM_WD * w_ref[...])
        mo_ref[...] = mn
        vo_ref[...] = vn

    spec = pl.BlockSpec((tr, C), lambda i: (i, 0))
    sh = jax.ShapeDtypeStruct((R, C), F32)
    own_ops = [] if own is None else [own]
    own_specs = [] if own is None else [pl.BlockSpec((tr, C), lambda i: (i + blk0, 0))]
    return pl.pallas_call(
        body, name=name, grid=(R // tr,),
        in_specs=[pl.BlockSpec((NDEV, tr, C), lambda i: (0, i + blk0, 0))] + own_specs + [spec, spec, spec],
        out_specs=[spec] * 4, out_shape=[sh] * 4, compiler_params=_params(("parallel",)),
    )(r8, *own_ops, w, m, v)


_HBM = pl.BlockSpec(memory_space=pltpu.HBM)
_SEM = pl.BlockSpec(memory_space=pltpu.SEMAPHORE)
_EFFECT = pltpu.SideEffectType.DATAFLOW_SIDE_EFFECTING


def _exchange_copies(g_refs, land_refs, send_sems, recv_sems, gather):
    mx, my, mc = lax.axis_index("x"), lax.axis_index("y"), lax.axis_index("c")
    copies = []
    for a in range(len(g_refs)):
        for f in ((1, 2, 4, 6) if gather else range(1, NDEV)):
            px = 1 - mx if (f >> 2) & 1 else mx
            py = 1 - my if (f >> 1) & 1 else my
            pc = 1 - mc if f & 1 else mc
            src = g_refs[a] if gather else g_refs[a].at[4 * px + 2 * py + pc]
            dst = land_refs[a].at[4 * mx + 2 * my + mc] if gather else land_refs[a].at[f]
            copies.append(pltpu.make_async_remote_copy(
                src_ref=src, dst_ref=dst,
                send_sem=send_sems.at[7 * a + f - 1], recv_sem=recv_sems.at[7 * a + f - 1],
                device_id=(px, py, pc), device_id_type=MESH))
    return copies


def _exchange_start(gs, name, gather=False):
    na = len(gs)

    def body(*refs):
        for cp in _exchange_copies(refs[:na], refs[na:2 * na], refs[2 * na], refs[2 * na + 1], gather):
            cp.start()
        refs[-1][...] = jnp.zeros_like(refs[-1])

    hbm = [pltpu.HBM(g.shape, g.dtype) for g in gs]
    land_shapes = [((NDEV,) + g.shape) if gather else g.shape for g in gs]
    lands = [pltpu.with_memory_space_constraint(lax.empty(shp, g.dtype), pltpu.HBM)
             for shp, g in zip(land_shapes, gs)]
    hbm_land = [pltpu.HBM(shp, g.dtype) for shp, g in zip(land_shapes, gs)]
    outs = pl.pallas_call(
        body, name=name,
        out_shape=(pltpu.SemaphoreType.DMA((7 * na,)), pltpu.SemaphoreType.DMA((7 * na,)), *hbm, *hbm_land,
                   jax.ShapeDtypeStruct((8, 128), F32)),
        in_specs=[_HBM] * (2 * na),
        out_specs=(_SEM, _SEM, *([_HBM] * (2 * na)), pl.BlockSpec(memory_space=pltpu.VMEM)),
        input_output_aliases={i: 2 + i for i in range(2 * na)},
        compiler_params=pltpu.CompilerParams(has_side_effects=_EFFECT),
    )(*[pltpu.with_memory_space_constraint(g, pltpu.HBM) for g in gs], *lands)
    return outs[0], outs[1], outs[2:2 + na], outs[2 + na:2 + 2 * na], outs[-1]


def _forward_copies(land_refs, send_sems, recv_sems):
    mx, my, mc = lax.axis_index("x"), lax.axis_index("y"), lax.axis_index("c")
    copies = []
    for a in range(len(land_refs)):
        for j, (fx, fy) in enumerate(((0, 1), (1, 0), (1, 1))):
            px = 1 - mx if fx else mx
            py = 1 - my if fy else my
            blk = land_refs[a].at[4 * px + 2 * py + mc]
            copies.append(pltpu.make_async_remote_copy(
                src_ref=blk, dst_ref=blk, send_sem=send_sems.at[3 * a + j], recv_sem=recv_sems.at[3 * a + j],
                device_id=(mx, my, 1 - mc), device_id_type=MESH))
    return copies


def _forward_start(lands, name):
    na = len(lands)

    def body(*refs):
        for cp in _forward_copies(refs[:na], refs[na], refs[na + 1]):
            cp.start()
        refs[-1][...] = jnp.zeros_like(refs[-1])

    outs = pl.pallas_call(
        body, name=name,
        out_shape=(pltpu.SemaphoreType.DMA((3 * na,)), pltpu.SemaphoreType.DMA((3 * na,)),
                   *[pltpu.HBM(g.shape, g.dtype) for g in lands], jax.ShapeDtypeStruct((8, 128), F32)),
        in_specs=[_HBM] * na,
        out_specs=(_SEM, _SEM, *([_HBM] * na), pl.BlockSpec(memory_space=pltpu.VMEM)),
        input_output_aliases={i: 2 + i for i in range(na)},
        compiler_params=pltpu.CompilerParams(has_side_effects=_EFFECT),
    )(*lands)
    return outs[0], outs[1], outs[2:2 + na], outs[-1]


def _forward_wait(send_sems, recv_sems, lands, after, name):
    na = len(lands)

    def body(*refs):
        for cp in _forward_copies(refs[:na], refs[na], refs[na + 1]):
            cp.wait_send()
            cp.wait_recv()

    return pl.pallas_call(
        body, name=name,
        out_shape=tuple(pltpu.HBM(g.shape, g.dtype) for g in lands),
        in_specs=[_HBM] * na + [_SEM, _SEM, pl.BlockSpec(memory_space=pl.ANY)],
        out_specs=tuple([_HBM] * na),
        input_output_aliases={i: i for i in range(na)},
        compiler_params=pltpu.CompilerParams(has_side_effects=_EFFECT),
    )(*lands, send_sems, recv_sems, after)


def _exchange_wait(send_sems, recv_sems, g_thru, land_thru, after, name, gather=False):
    na = len(g_thru)

    def body(*refs):
        for cp in _exchange_copies(refs[:na], refs[na:2 * na], refs[2 * na], refs[2 * na + 1], gather):
            cp.wait_send()
            cp.wait_recv()

    outs = pl.pallas_call(
        body, name=name,
        out_shape=tuple(pltpu.HBM(g.shape, g.dtype) for g in list(g_thru) + list(land_thru)),
        in_specs=[_HBM] * (2 * na) + [_SEM, _SEM, pl.BlockSpec(memory_space=pl.ANY)],
        out_specs=tuple([_HBM] * (2 * na)),
        input_output_aliases={i: i for i in range(2 * na)},
        compiler_params=pltpu.CompilerParams(has_side_effects=_EFFECT),
    )(*g_thru, *land_thru, send_sems, recv_sems, after)
    return outs[na:]


def _sum8(r, name):
    _, R, C = r.shape
    tr = _tile(R, (256, 160, 128, 64, 32, 16, 8))

    def body(r_ref, o_ref):
        acc = r_ref[0].astype(F32)
        for k in range(1, NDEV):
            acc = acc + r_ref[k].astype(F32)
        o_ref[...] = acc

    return pl.pallas_call(
        body, name=name, grid=(R // tr,),
        in_specs=[pl.BlockSpec((NDEV, tr, C), lambda i: (0, i, 0))],
        out_specs=pl.BlockSpec((tr, C), lambda i: (i, 0)),
        out_shape=jax.ShapeDtypeStruct((R, C), F32),
        compiler_params=_params(("parallel",)),
    )(r)


def _adamw(g, w, m, v, name):
    R, C = g.shape
    tr = _tile(R, (256, 160, 128, 64, 32, 16, 8))
    bc1 = 1.0 - ADAM_B1 ** ADAM_STEP
    bc2 = 1.0 - ADAM_B2 ** ADAM_STEP

    def body(g_ref, w_ref, m_ref, v_ref, d_ref, mo_ref, vo_ref):
        gg = g_ref[...]
        mn = ADAM_B1 * m_ref[...] + (1.0 - ADAM_B1) * gg
        vn = ADAM_B2 * v_ref[...] + (1.0 - ADAM_B2) * (gg * gg)
        mh = mn / bc1
        vh = vn / bc2
        d_ref[...] = -ADAM_LR * (mh / (jnp.sqrt(vh) + ADAM_EPS) + ADAM_WD * w_ref[...])
        mo_ref[...] = mn
        vo_ref[...] = vn

    spec = pl.BlockSpec((tr, C), lambda i: (i, 0))
    sh = jax.ShapeDtypeStruct((R, C), F32)
    return pl.pallas_call(
        body, name=name, grid=(R // tr,), in_specs=[spec] * 4, out_specs=[spec] * 3,
        out_shape=[sh] * 3, compiler_params=_params(("parallel",)),
    )(g, w, m, v)


def _ada_fwd(c16, w_sh, b_sh):
    def body(c_ref, w_ref, b_ref, o_ref):
        c = c_ref[...]
        o_ref[...] = _dot(c * _sig(c), w_ref[...], NN) + b_ref[...]

    return pl.pallas_call(
        body, name="ada_fwd", out_shape=jax.ShapeDtypeStruct((16, w_sh.shape[1]), F32),
        compiler_params=pltpu.CompilerParams(vmem_limit_bytes=VMEM_LIMIT),
    )(c16, w_sh, b_sh)


def _ada_bwd(c16, g16, g16_sh, w_sh):
    ncol = w_sh.shape[1]

    def body(c_ref, g_ref, gs_ref, w_ref, dw_ref, db_ref, dc_ref):
        c = c_ref[...]
        s = _sig(c)
        gs = gs_ref[...]
        dw_ref[...] = _dot(c * s, gs, TN)
        db_ref[...] = jnp.broadcast_to(_colsum(g_ref[...]), db_ref.shape)
        odd = lax.broadcasted_iota(jnp.int32, gs.shape, 0) % 2 == 1
        gc = _colsum(jnp.where(odd, gs, 0.0))
        ds = _dot(jnp.broadcast_to(gc, (8, ncol)), w_ref[...], NT)
        c1 = c[1:2, :]
        s1 = s[1:2, :]
        dc_ref[...] = ds * (s1 * (1.0 + c1 * (1.0 - s1)))

    return pl.pallas_call(
        body, name="ada_bwd",
        out_shape=[jax.ShapeDtypeStruct(w_sh.shape, F32),
                   jax.ShapeDtypeStruct((8, g16.shape[1]), F32),
                   jax.ShapeDtypeStruct((8, D), F32)],
        compiler_params=pltpu.CompilerParams(vmem_limit_bytes=VMEM_LIMIT),
    )(c16, g16, g16_sh, w_sh)


def _ln0_fwd(x, ctx, g, b, modx, modc):
    L = x.shape[0]
    nt = L // TL

    def body(x_ref, c_ref, g_ref, b_ref, mx_ref, mc_ref, xn_ref, h_ref):
        isc = pl.program_id(0) == nt
        xin = jnp.where(isc, c_ref[...], x_ref[...])
        sh = jnp.where(isc, mc_ref[0:1, :], mx_ref[0:1, :])
        sc = jnp.where(isc, mc_ref[1:2, :], mx_ref[1:2, :])
        xhat, _ = _ln(xin)
        xn = xhat * g_ref[...] + b_ref[...]
        xn_ref[...] = xn
        h_ref[...] = (xn * (1.0 + sc) + sh).astype(h_ref.dtype)

    return pl.pallas_call(
        body, name="ln0_fwd", grid=(nt + 1,),
        in_specs=[_rtc(D, nt), _cst((TL, D)), _cst((1, D)), _cst((1, D)), _cst((8, D)), _cst((8, D))],
        out_specs=[_rt(D), _rt(D)],
        out_shape=[jax.ShapeDtypeStruct((L + TL, D), F32), jax.ShapeDtypeStruct((L + TL, D), _MXU)],
        compiler_params=_params(("parallel",)),
    )(x, ctx, g, b, modx, modc)


def _xbc_colblk(j):
    return jnp.where(j < 8, OXS // 128 + j, OB // 128 + j - 8)


def _conv_taps(p_ref, r0, first, last):
    main = p_ref[pl.ds(r0, TL), :]
    zero = jnp.zeros((8, main.shape[1]), F32)
    prev = zero if first else p_ref[pl.ds(r0 - 8, 8), :]
    nxt = zero if last else p_ref[pl.ds(r0 + TL, 8), :]
    ext = jnp.concatenate([prev, main, nxt], axis=0)
    n = TL + 16
    return [pltpu.roll(ext, (2 - k) % n, 0)[8:8 + TL] for k in range(5)]


def _seq_chunks(L):
    nt = L // TL
    return [(r * TL, r == 0, r == nt - 1) for r in range(nt)] + [(L, True, True)]


def _conv_fwd(p, conv_w8, conv_b):
    RT = p.shape[0]
    L = RT - TL
    chunks = _seq_chunks(L)

    def body(p_ref, w_ref, b_ref, o_ref):
        w = w_ref[...]
        bias = b_ref[...]
        for r0, first, last in chunks:
            taps = _conv_taps(p_ref, r0, first, last)
            pre = bias + sum(w[k:k + 1, :] * taps[k] for k in range(5))
            o_ref[pl.ds(r0, TL), :] = pre * _sig(pre)

    return pl.pallas_call(
        body, name="conv_fwd", grid=(12,),
        in_specs=[pl.BlockSpec((RT, 128), lambda j: (0, _xbc_colblk(j))),
                  pl.BlockSpec((8, 128), lambda j: (0, j)),
                  pl.BlockSpec((1, 128), lambda j: (0, j))],
        out_specs=pl.BlockSpec((RT, 128), lambda j: (0, j)),
        out_shape=jax.ShapeDtypeStruct((RT, 1536), F32),
        compiler_params=_params(("parallel",)),
    )(p, conv_w8, conv_b)


def _ssd_common(dtraw, dtb, a32, rev):
    dt = _softplus(dtraw + dtb)
    acum = _cumsum_rows(dt * a32, rev)
    ii = lax.broadcasted_iota(jnp.int32, (Q, Q), 0)
    jj = lax.broadcasted_iota(jnp.int32, (Q, Q), 1)
    mask = (ii <= jj) if rev else (ii >= jj)
    return dt, acum, acum.T, dt.T, mask


def _ssd_orders(ncl, ncc):
    nc = ncl + ncc

    def cf(s):
        return jnp.where(s < ncc, ncl + s, s - ncc)

    def cb(s):
        return nc - 1 - s

    return cf, cb


def _lane_bcast(v, ln):
    return jnp.broadcast_to(v[:, ln:ln + 1], v.shape)


def _halves(v, lo, axis):
    return jnp.concatenate([jnp.where(lo, v, 0.0), jnp.where(lo, 0.0, v)], axis=axis)


def _ssd2_fwd(xbc, p, prm):
    RT = xbc.shape[0]
    nc = RT // Q
    ncc = TL // Q
    cf, cb = _ssd_orders(nc - ncc, ncc)

    def one_dir(x_ref, dt_ref, prm_ref, y_ref, hp_ref, HT_ref, d):
        rev = d == 1
        a32 = -jnp.exp(prm_ref[1:2, :])
        dt, acum, acumT, dtT, mask = _ssd_common(dt_ref[...], prm_ref[0:1, :], a32, rev)
        end = 0 if rev else Q - 1
        lo = lax.broadcasted_iota(jnp.int32, (Q, 128), 1) < HP
        for g in range(2):
            Bg = x_ref[:, D + g * NS:D + (g + 1) * NS]
            Cg = x_ref[:, D + 2 * NS + g * NS:D + 2 * NS + (g + 1) * NS]
            CB = _dot(Cg, Bg, NT)
            xds, svs = [], []
            for q in range(HPG // 2):
                pi = g * (HPG // 2) + q
                ps = slice(pi * 128, (pi + 1) * 128)
                Xp = x_ref[:, ps]
                HTp = HT_ref[:, ps]
                lhs, dcs, sv = [], [], []
                ces = []
                for h in (2 * pi, 2 * pi + 1):
                    ln = 16 * d + h
                    colB = _lane_bcast(acum, ln)
                    rowv = acumT[ln:ln + 1, :]
                    aend = colB[end:end + 1, :]
                    Lm = jnp.exp(jnp.where(mask, colB - rowv, -1e30))
                    lhs.append(CB * Lm * dtT[ln:ln + 1, :])
                    ces.append(Cg * jnp.exp(colB))
                    dcs.append(jnp.exp(aend - colB) * _lane_bcast(dt, ln))
                    sv.append(jnp.exp(aend))
                lhs = jnp.concatenate(lhs + ces, axis=1)
                rhs = jnp.concatenate([_halves(Xp, lo, 0), _halves(HTp, lo, 0)], axis=0)
                y_ref[:, ps] = _dot(lhs, rhs, NN)
                xds.append(Xp * jnp.where(lo, dcs[0], dcs[1]))
                svs.append(jnp.where(lo[0:1, :], sv[0], sv[1]))
            gs = slice(g * 512, (g + 1) * 512)
            HTg = HT_ref[:, gs]
            hp_ref[0, :, gs] = HTg
            st = _dot(Bg.T, jnp.concatenate(xds, axis=1), NN)
            HT_ref[:, gs] = jnp.concatenate(svs, axis=1) * HTg + st

    def body(xf_ref, xb_ref, df_ref, db_ref, prm_ref, yf_ref, yb_ref, hf_ref, hb_ref, Hf, Hb):
        @pl.when(pl.program_id(0) == 0)
        def _():
            Hf[...] = jnp.zeros_like(Hf)
            Hb[...] = jnp.zeros_like(Hb)

        one_dir(xf_ref, df_ref, prm_ref, yf_ref, hf_ref, Hf, 0)
        one_dir(xb_ref, db_ref, prm_ref, yb_ref, hb_ref, Hb, 1)

    ysh = jax.ShapeDtypeStruct((RT, D), F32)
    hsh = jax.ShapeDtypeStruct((nc, NS, NH * HP), F32)
    hspec = pl.BlockSpec((1, NS, NH * HP), lambda s: (s, 0, 0))
    return pl.pallas_call(
        body, name="ssd_fwd", grid=(nc,),
        in_specs=[pl.BlockSpec((Q, 1536), lambda s: (cf(s), 0)),
                  pl.BlockSpec((Q, 1536), lambda s: (cb(s), 0)),
                  pl.BlockSpec((Q, 128), lambda s: (cf(s), ODT // 128)),
                  pl.BlockSpec((Q, 128), lambda s: (cb(s), ODT // 128)),
                  _cst((8, 128))],
        out_specs=[pl.BlockSpec((Q, D), lambda s: (cf(s), 0)),
                   pl.BlockSpec((Q, D), lambda s: (cb(s), 0)), hspec, hspec],
        out_shape=[ysh, ysh, hsh, hsh],
        scratch_shapes=[pltpu.VMEM((NS, NH * HP), F32), pltpu.VMEM((NS, NH * HP), F32)],
        compiler_params=_params(("arbitrary",)),
    )(xbc, xbc, p, p, prm)


def _ssd2_bwd(xbc, p, prm, dsk, dyd, hpf, hpb):
    RT = xbc.shape[0]
    nc = RT // Q
    ncc = TL // Q
    ncl = nc - ncc
    cf, cb = _ssd_orders(ncl, ncc)

    def rs(t):
        return nc - 1 - t

    def one_dir(x_ref, dt_ref, prm_ref, dsk_ref, dy_ref, is_ctx, hp_ref, dHT_ref,
                dx_ref, ddt_ref, st_ref, d):
        rev = d == 1
        a32 = -jnp.exp(prm_ref[1:2, :])
        dtraw = dt_ref[...]
        dtb = prm_ref[0:1, :]
        dt, acum, acumT, _, _ = _ssd_common(dtraw, dtb, a32, rev)
        end = 0 if rev else Q - 1
        lane = lax.broadcasted_iota(jnp.int32, (Q, 128), 1)
        srow = lax.broadcasted_iota(jnp.int32, (Q, 128), 0)
        maskT = (lane <= srow) if rev else (lane >= srow)
        lo = lane < HP
        lo1 = lo[0:1, :]
        dyscale = jnp.where(is_ctx, 0.0, 1.0)
        c_dacum = jnp.zeros((Q, 128), F32)
        r_dacum = jnp.zeros((Q, 128), F32)
        c_ddt = jnp.zeros((Q, 128), F32)
        dskacc = jnp.zeros((1, 128), F32)
        for g in range(2):
            gs = slice(g * 512, (g + 1) * 512)
            Bg = x_ref[:, D + g * NS:D + (g + 1) * NS]
            Cg = x_ref[:, D + 2 * NS + g * NS:D + 2 * NS + (g + 1) * NS]
            CBT = _dot(Bg, Cg, NT)
            HTg = hp_ref[0, :, gs]
            dHTg = dHT_ref[:, gs]
            BdHg = _dot(Bg, dHTg, NN)
            dCBT = jnp.zeros((Q, Q), F32)
            dCg = jnp.zeros((Q, NS), F32)
            xds, dyes, svs = [], [], []
            for q in range(HPG // 2):
                pi = g * (HPG // 2) + q
                ps = slice(pi * 128, (pi + 1) * 128)
                qs = slice(q * 128, (q + 1) * 128)
                Xp = x_ref[:, ps]
                dYp = dy_ref[:, ps] * dyscale
                HTp = HTg[:, qs]
                BdHp = BdHg[:, qs]
                dY2 = _halves(dYp, lo, 0)
                dWT2 = _dot(_halves(Xp, lo, 0), dYp.T, NN)
                G2 = _dot(dY2, HTp, NT)
                XB = Xp * BdHp
                hh = _colsum(dHTg[:, qs] * HTp)
                yx = _colsum(dYp * Xp)
                wts, dcs, ebs, sv = [], [], [], []
                for k, h in enumerate((2 * pi, 2 * pi + 1)):
                    ln = 16 * d + h
                    half = lo if k == 0 else jnp.logical_not(lo)
                    half1 = half[0:1, :]
                    colB = _lane_bcast(acum, ln)
                    dtcB = _lane_bcast(dt, ln)
                    rowv = acumT[ln:ln + 1, :]
                    aend = colB[end:end + 1, :]
                    LmT = jnp.exp(jnp.where(maskT, rowv - colB, -1e30))
                    WT = CBT * LmT * dtcB
                    dWT = dWT2[k * Q:(k + 1) * Q, :]
                    U = dWT * LmT
                    MT = U * CBT
                    rM = jnp.sum(MT, axis=1, keepdims=True)
                    rT = _colsum(MT * dtcB)
                    dCBT = dCBT + U * dtcB
                    ecol = jnp.exp(aend - colB)
                    EB = jnp.exp(colB)
                    Gk = G2[k * Q:(k + 1) * Q, :]
                    dCg = dCg + EB * Gk
                    qcol = jnp.sum(EB * Gk * Cg, axis=1, keepdims=True)
                    xb = jnp.sum(jnp.where(half, XB, 0.0), axis=1, keepdims=True)
                    e1 = ecol[:, 0:1]
                    dt1 = dtcB[:, 0:1]
                    scol = e1 * dt1 * xb
                    sA = jnp.exp(aend)
                    eterm = sA[:, 0:1] * jnp.sum(jnp.where(half1, hh, 0.0), axis=1, keepdims=True) \
                        + _colsum(scol)
                    cvec = qcol - dt1 * rM - scol + jnp.where(srow[:, 0:1] == end, eterm, 0.0)
                    c_dacum = jnp.where(lane == ln, cvec, c_dacum)
                    r_dacum = jnp.where(srow == ln, rT, r_dacum)
                    c_ddt = jnp.where(lane == ln, rM + e1 * xb, c_ddt)
                    if d == 0:
                        dskacc = dskacc + jnp.where(
                            lane[0:1, :] == h, jnp.sum(jnp.where(half1, yx, 0.0), axis=1, keepdims=True), 0.0)
                    wts.append(WT)
                    dcs.append(ecol * dtcB)
                    ebs.append(EB)
                    sv.append(sA)
                dcp = jnp.where(lo, dcs[0], dcs[1])
                dX = _dot(jnp.concatenate(wts, axis=1), dY2, NN) + dcp * BdHp
                if d == 0:
                    dX = dX + dYp * dsk_ref[:, ps]
                dx_ref[:, ps] = dX
                xds.append(Xp * dcp)
                dyes.append(dYp * jnp.where(lo, ebs[0], ebs[1]))
                svs.append(jnp.where(lo1, sv[0], sv[1]))
            dx_ref[:, D + g * NS:D + (g + 1) * NS] = (
                _dot(jnp.concatenate(xds, axis=1), dHTg, NT) + _dot(dCBT, Cg, NN))
            dx_ref[:, D + 2 * NS + g * NS:D + 2 * NS + (g + 1) * NS] = dCg + _dot(dCBT, Bg, TN)
            dHT_ref[:, gs] = (jnp.concatenate(svs, axis=1) * dHTg
                              + _dot(Cg.T, jnp.concatenate(dyes, axis=1), NN))
        dacum = c_dacum + r_dacum.T
        da = _cumsum_rows(dacum, not rev)
        mine = (lane >= 16 * d) & (lane < 16 * d + 16)
        ddt = jnp.where(mine, c_ddt + da * a32, 0.0)
        ddt_ref[...] = ddt * _sig(dtraw + dtb)
        st_ref[0:1, :] += _colsum(jnp.where(mine, da * dt, 0.0))
        if d == 0:
            st_ref[1:2, :] += dskacc

    def body(xf_ref, xb_ref, df_ref, db_ref, prm_ref, dsk_ref, dyf_ref, dyb_ref, hf_ref, hb_ref,
             dxf_ref, dxb_ref, ddf_ref, ddb_ref, st_ref, dHf, dHb):
        t = pl.program_id(0)

        @pl.when(t == 0)
        def _():
            dHf[...] = jnp.zeros_like(dHf)
            dHb[...] = jnp.zeros_like(dHb)
            st_ref[...] = jnp.zeros_like(st_ref)

        s = rs(t)
        one_dir(xf_ref, df_ref, prm_ref, dsk_ref, dyf_ref, cf(s) >= ncl, hf_ref, dHf,
                dxf_ref, ddf_ref, st_ref, 0)
        one_dir(xb_ref, db_ref, prm_ref, dsk_ref, dyb_ref, cb(s) >= ncl, hb_ref, dHb,
                dxb_ref, ddb_ref, st_ref, 1)

        @pl.when(t == nc - 1)
        def _():
            st_ref[0:1, :] = -jnp.exp(prm_ref[1:2, :]) * st_ref[0:1, :]

    def lat(c):
        return jnp.minimum(c, ncl - 1)

    xsh = jax.ShapeDtypeStruct((RT, 1536), F32)
    dsh = jax.ShapeDtypeStruct((RT, 128), F32)
    hspec = pl.BlockSpec((1, NS, NH * HP), lambda t: (rs(t), 0, 0))
    return pl.pallas_call(
        body, name="ssd_bwd", grid=(nc,),
        in_specs=[pl.BlockSpec((Q, 1536), lambda t: (cf(rs(t)), 0)),
                  pl.BlockSpec((Q, 1536), lambda t: (cb(rs(t)), 0)),
                  pl.BlockSpec((Q, 128), lambda t: (cf(rs(t)), ODT // 128)),
                  pl.BlockSpec((Q, 128), lambda t: (cb(rs(t)), ODT // 128)),
                  _cst((8, 128)), _cst((1, D)),
                  pl.BlockSpec((Q, D), lambda t: (lat(cf(rs(t))), 0)),
                  pl.BlockSpec((Q, D), lambda t: (lat(cb(rs(t))), 0)),
                  hspec, hspec],
        out_specs=[pl.BlockSpec((Q, 1536), lambda t: (cf(rs(t)), 0)),
                   pl.BlockSpec((Q, 1536), lambda t: (cb(rs(t)), 0)),
                   pl.BlockSpec((Q, 128), lambda t: (cf(rs(t)), 0)),
                   pl.BlockSpec((Q, 128), lambda t: (cb(rs(t)), 0)),
                   _cst((8, 128))],
        out_shape=[xsh, xsh, dsh, dsh, jax.ShapeDtypeStruct((8, 128), F32)],
        scratch_shapes=[pltpu.VMEM((NS, NH * HP), F32), pltpu.VMEM((NS, NH * HP), F32)],
        compiler_params=_params(("arbitrary",)),
    )(xbc, xbc, p, p, prm, dsk, dyd, dyd, hpf, hpb)


def _mix_fwd_vals(yf, yb, z, xs, u, v, dsk, sg, gg, gb):
    y = yf + yb + xs * dsk
    sz = _sig(z)
    hh = y * z * sz
    r = lax.rsqrt(jnp.mean(hh * hh, axis=-1, keepdims=True) + EPS)
    nh = hh * r
    ug, tu = _gelu(u)
    vg, tv = _gelu(v)
    vhat, vrstd = _ln(vg)
    vn = vhat * gg + gb
    return y, sz, r, nh, ug, tu, vg, tv, vhat, vrstd, vn


def _mix_fwd(yf, yb, p, xbc, dsk, sg, gg, gb, ws, bsT):
    L = yf.shape[0] - TL
    nt = L // TL

    def body(yf_ref, yb_ref, z_ref, xs_ref, u_ref, v_ref, dsk_ref, sg_ref, gg_ref, gb_ref,
             ws_ref, bs_ref, ys_ref, ym_ref):
        _, _, _, nh, ug, _, _, _, _, _, vn = _mix_fwd_vals(
            yf_ref[...], yb_ref[...], z_ref[...], xs_ref[...], u_ref[...], v_ref[...],
            dsk_ref[...], sg_ref[...], gg_ref[...], gb_ref[...])
        ys_ref[...] = (nh * sg_ref[...]).astype(ys_ref.dtype)
        for n in range(TL // Q):
            rs_ = slice(n * Q, (n + 1) * Q)
            for g in range(8):
                cs = slice(g * 128, (g + 1) * 128)
                mixed = _dot(ws_ref[g], vn[rs_, cs], NN) + bs_ref[:, g:g + 1]
                ym_ref[rs_, cs] = (ug[rs_, cs] * mixed).astype(ym_ref.dtype)

    return pl.pallas_call(
        body, name="mix_fwd", grid=(nt,),
        in_specs=[_rt(D), _rt(D), _rt(D, OZ // D), _rt(D, 0), _rt(D, OU // D), _rt(D, OV // D),
                  _cst((1, D)), _cst((1, D)), _cst((1, D)), _cst((1, D)),
                  _cst((8, 128, 128)), _cst((128, 128))],
        out_specs=[_rt(D), _rt(D)],
        out_shape=[jax.ShapeDtypeStruct((L, D), _MXU), jax.ShapeDtypeStruct((L, D), _MXU)],
        compiler_params=_params(("parallel",)),
    )(yf, yb, p, xbc, p, p, dsk, sg, gg, gb, ws, bsT)


def _mix_bwd(dys, dym, yf, yb, p, xbc, dp, dsk, sg, gg, gb, ws, bsT):
    L = dys.shape[0]
    nt = L // TL

    def body(dys_ref, dym_ref, yf_ref, yb_ref, z_ref, xs_ref, u_ref, v_ref, dsk_ref, sg_ref,
             gg_ref, gb_ref, ws_ref, bs_ref, dp_any, dzuv_ref, dy_ref, st_ref,
             dws_ref, dbs_ref, dvn_s):
        del dp_any
        dz_ref = dzuv_ref.at[:, OZ:OZ + D]
        du_ref = dzuv_ref.at[:, OU:OU + D]
        dv_ref = dzuv_ref.at[:, OV:OV + D]

        @pl.when(pl.program_id(0) == 0)
        def _():
            st_ref[...] = jnp.zeros_like(st_ref)
            dws_ref[...] = jnp.zeros_like(dws_ref)
            dbs_ref[...] = jnp.zeros_like(dbs_ref)

        z = z_ref[...]
        u = u_ref[...]
        v = v_ref[...]
        y, sz, r, nh, ug, tu, vg, tv, vhat, vrstd, vn = _mix_fwd_vals(
            yf_ref[...], yb_ref[...], z, xs_ref[...], u, v,
            dsk_ref[...], sg_ref[...], gg_ref[...], gb_ref[...])
        dys = dys_ref[...]
        st_ref[0:1, :] += _colsum(dys * nh)
        dn = dys * sg_ref[...]
        dhh = r * (dn - nh * jnp.mean(dn * nh, axis=-1, keepdims=True))
        dy_ref[...] = dhh * z * sz
        dz_ref[...] = (dhh * y * (sz * (1.0 + z * (1.0 - sz)))).astype(dz_ref.dtype)
        dym = dym_ref[...]
        lane = lax.broadcasted_iota(jnp.int32, (Q, 128), 1)
        dbs = jnp.zeros((Q, 128), F32)
        gu = _gelu_grad(u, tu)
        for n in range(TL // Q):
            rs_ = slice(n * Q, (n + 1) * Q)
            for g in range(8):
                cs = slice(g * 128, (g + 1) * 128)
                vb = vn[rs_, cs]
                mixed = _dot(ws_ref[g], vb, NN) + bs_ref[:, g:g + 1]
                dyb = dym[rs_, cs]
                dmx = dyb * ug[rs_, cs]
                du_ref[rs_, cs] = (dyb * mixed * gu[rs_, cs]).astype(du_ref.dtype)
                dvn_s[rs_, cs] = _dot(ws_ref[g], dmx, TN)
                dws_ref[g] += _dot(dmx, vb, NT)
                dbs = dbs + jnp.where(lane == g, jnp.sum(dmx, axis=1, keepdims=True), 0.0)
        dbs_ref[...] += dbs
        dvn = dvn_s[...]
        st_ref[1:2, :] += _colsum(dvn * vhat)
        st_ref[2:3, :] += _colsum(dvn)
        dvg = _ln_bwd(dvn * gg_ref[...], vhat, vrstd)
        dv_ref[...] = (dvg * _gelu_grad(v, tv)).astype(dv_ref.dtype)

    outs = pl.pallas_call(
        body, name="mix_bwd", grid=(nt,),
        in_specs=[_rt(D), _rt(D), _rt(D), _rt(D), _rt(D, OZ // D), _rt(D, 0), _rt(D, OU // D),
                  _rt(D, OV // D), _cst((1, D)), _cst((1, D)), _cst((1, D)), _cst((1, D)),
                  _cst((8, 128, 128)), _cst((128, 128)), pl.BlockSpec(memory_space=pl.ANY)],
        out_specs=[_rt(3 * D, 0), _rt(D), _cst((8, D)),
                   _cst((8, 128, 128)), _cst((128, 128))],
        out_shape=[jax.ShapeDtypeStruct(dp.shape, dp.dtype),
                   jax.ShapeDtypeStruct((L, D), F32), jax.ShapeDtypeStruct((8, D), F32),
                   jax.ShapeDtypeStruct((8, 128, 128), F32), jax.ShapeDtypeStruct((128, 128), F32)],
        scratch_shapes=[pltpu.VMEM((TL, D), F32)],
        input_output_aliases={14: 0},
        compiler_params=_params(("arbitrary",)),
    )(dys, dym, yf, yb, p, xbc, p, p, dsk, sg, gg, gb, ws, bsT, dp)
    return outs


def _merge_fwd(yssd, ygm, p, bg, ws, wg, wo, xn, modx, g1, b1):
    L = yssd.shape[0]
    tm = TL

    def body(ys_ref, yg_ref, g_ref, bg_ref, ws_ref, wg_ref, wo_ref, xn_ref, mx_ref, g1_ref, b1_ref,
             a1_ref, a2_ref, m_ref, o_ref, r1_ref, h2_ref):
        a1 = _dot(ys_ref[...], ws_ref[...], NN)
        a2 = _dot(yg_ref[...], wg_ref[...], NN)
        gt = _sig(g_ref[...] + bg_ref[...])
        mg = gt[:, :D] * a1 + gt[:, D:] * a2
        a1_ref[...] = a1
        a2_ref[...] = a2
        m_ref[...] = mg.astype(m_ref.dtype)
        out = _dot(mg, wo_ref[...], NN)
        o_ref[...] = out
        r1 = ALPHA * xn_ref[...] + mx_ref[2:3, :] * out
        xhat, _ = _ln(r1)
        x1 = xhat * g1_ref[...] + b1_ref[...]
        r1_ref[...] = r1
        h2_ref[...] = (x1 * (1.0 + mx_ref[4:5, :]) + mx_ref[3:4, :]).astype(h2_ref.dtype)

    rows = pl.BlockSpec((tm, D), lambda i: (i, 0))
    f32s = jax.ShapeDtypeStruct((L, D), F32)
    mxus = jax.ShapeDtypeStruct((L, D), _MXU)
    return pl.pallas_call(
        body, name="merge_fwd", grid=(L // tm,),
        in_specs=[rows, rows, pl.BlockSpec((tm, 2 * D), lambda i: (i, OG // (2 * D))), _cst((1, 2 * D)),
                  _cst((D, D)), _cst((D, D)), _cst((D, D)), rows, _cst((8, D)), _cst((1, D)), _cst((1, D))],
        out_specs=[rows] * 6,
        out_shape=[f32s, f32s, mxus, f32s, f32s, mxus],
        compiler_params=_params(("parallel",)),
    )(yssd, ygm, p, bg, ws, wg, wo, xn, modx, g1, b1)


def _mm_o2_res2(ff, w2, r1, tgt, modx, g1, b1, g2, b2):
    L, K = ff.shape
    tm = 2 * TL

    def body(a_ref, b_ref, r1_ref, t_ref, mx_ref, g1_ref, b1_ref, g2_ref, b2_ref,
             dr2_ref, do2_ref, st_ref, loss_ref):
        @pl.when(pl.program_id(0) == 0)
        def _():
            st_ref[...] = jnp.zeros_like(st_ref)
            loss_ref[...] = jnp.zeros_like(loss_ref)

        o2 = _dot(a_ref[...], b_ref[...], NN)
        xh1, _ = _ln(r1_ref[...])
        x1 = xh1 * g1_ref[...] + b1_ref[...]
        g2x = mx_ref[5:6, :]
        xh2, rstd2 = _ln(ALPHA * x1 + g2x * o2)
        err = xh2 * g2_ref[...] + b2_ref[...] - t_ref[...]
        per_tok = jnp.mean(err * err, axis=-1, keepdims=True)
        loss_ref[...] += 0.5 * jnp.sum(per_tok, axis=0, keepdims=True)
        dy = err * (1.0 / D)
        st_ref[0:1, :] += _colsum(dy * xh2)
        st_ref[1:2, :] += _colsum(dy)
        dr2 = _ln_bwd(dy * g2_ref[...], xh2, rstd2)
        st_ref[2:3, :] += _colsum(dr2 * o2)
        dr2_ref[...] = dr2
        do2_ref[...] = (g2x * dr2).astype(do2_ref.dtype)

    return pl.pallas_call(
        body, name="mm_o2_res2", grid=(L // tm,),
        in_specs=[_rt(K, rows=tm), _cst((K, D)), _rt(D, rows=tm), _rt(D, rows=tm), _cst((8, D))]
        + [_cst((1, D))] * 4,
        out_specs=[_rt(D, rows=tm), _rt(D, rows=tm), _cst((8, D)), _cst((8, 128))],
        out_shape=[jax.ShapeDtypeStruct((L, D), F32), jax.ShapeDtypeStruct((L, D), _MXU),
                   jax.ShapeDtypeStruct((8, D), F32), jax.ShapeDtypeStruct((8, 128), F32)],
        compiler_params=_params(("arbitrary",)),
    )(ff, w2, r1, tgt, modx, g1, b1, g2, b2)


def _mm_dh2_res1bwd(df13, w13i, dr2, r1, out, modx, g1, b1):
    L, K = df13.shape

    def body(a_ref, b_ref, dr2_ref, r1_ref, o_ref, mx_ref, g_ref, bb_ref, dr1_ref, do_ref, st_ref):
        @pl.when(pl.program_id(0) == 0)
        def _():
            st_ref[...] = jnp.zeros_like(st_ref)

        dh2 = _dot(a_ref[...], b_ref[...], NN)
        xh1, rstd1 = _ln(r1_ref[...])
        x1 = xh1 * g_ref[...] + bb_ref[...]
        dx1 = ALPHA * dr2_ref[...] + dh2 * (1.0 + mx_ref[4:5, :])
        st_ref[0:1, :] += _colsum(dh2 * x1)
        st_ref[1:2, :] += _colsum(dh2)
        st_ref[2:3, :] += _colsum(dx1 * xh1)
        st_ref[3:4, :] += _colsum(dx1)
        dr1 = _ln_bwd(dx1 * g_ref[...], xh1, rstd1)
        st_ref[4:5, :] += _colsum(dr1 * o_ref[...])
        dr1_ref[...] = dr1
        do_ref[...] = (mx_ref[2:3, :] * dr1).astype(do_ref.dtype)

    return pl.pallas_call(
        body, name="mm_dh2_res1bwd", grid=(L // TL,),
        in_specs=[_rt(K), _cst((K, D)), _rt(D), _rt(D), _rt(D), _cst((8, D)), _cst((1, D)), _cst((1, D))],
        out_specs=[_rt(D), _rt(D), _cst((8, D))],
        out_shape=[jax.ShapeDtypeStruct((L, D), F32), jax.ShapeDtypeStruct((L, D), _MXU),
                   jax.ShapeDtypeStruct((8, D), F32)],
        compiler_params=_params(("arbitrary",)),
    )(df13, w13i, dr2, r1, out, modx, g1, b1)


def _merge_bwd(dout, a1, a2, p, bg, wo, ws, wg, dp):
    L = a1.shape[0]
    tm = TL

    def body(do_ref, a1_ref, a2_ref, g_ref, bg_ref, wo_ref, ws_ref, wg_ref, dp_any,
             dg_ref, da1_ref, da2_ref, st_ref, dys_ref, dym_ref):
        del dp_any

        @pl.when(pl.program_id(0) == 0)
        def _():
            st_ref[...] = jnp.zeros_like(st_ref)

        dm = _dot(do_ref[...], wo_ref[...], NT)
        gt = _sig(g_ref[...] + bg_ref[...])
        g1 = gt[:, :D]
        g2 = gt[:, D:]
        da1 = (dm * g1).astype(da1_ref.dtype)
        da2 = (dm * g2).astype(da2_ref.dtype)
        da1_ref[...] = da1
        da2_ref[...] = da2
        dg1 = dm * a1_ref[...] * g1 * (1.0 - g1)
        dg2 = dm * a2_ref[...] * g2 * (1.0 - g2)
        st_ref[0:1, 0:D] += _colsum(dg1)
        st_ref[0:1, D:2 * D] += _colsum(dg2)
        dg_ref[:, 0:D] = dg1.astype(dg_ref.dtype)
        dg_ref[:, D:2 * D] = dg2.astype(dg_ref.dtype)
        dys_ref[...] = _dot(da1, ws_ref[...], NT)
        dym_ref[...] = _dot(da2, wg_ref[...], NT)

    rows = pl.BlockSpec((tm, D), lambda i: (i, 0))
    gates = pl.BlockSpec((tm, 2 * D), lambda i: (i, OG // (2 * D)))
    f32s = jax.ShapeDtypeStruct((L, D), F32)
    mxus = jax.ShapeDtypeStruct((L, D), _MXU)
    return pl.pallas_call(
        body, name="merge_bwd", grid=(L // tm,),
        in_specs=[rows, rows, rows, gates, _cst((1, 2 * D)), _cst((D, D)), _cst((D, D)), _cst((D, D)),
                  pl.BlockSpec(memory_space=pl.ANY)],
        out_specs=[gates, rows, rows, _cst((8, 2 * D)), rows, rows],
        out_shape=[jax.ShapeDtypeStruct(dp.shape, dp.dtype), mxus, mxus,
                   jax.ShapeDtypeStruct((8, 2 * D), F32), f32s, f32s],
        input_output_aliases={8: 0},
        compiler_params=_params(("arbitrary",)),
    )(dout, a1, a2, p, bg, wo, ws, wg, dp)


HFF = DFF // 2


def _mm_f13_glu(h2, w13i):
    L = h2.shape[0]
    tm = 512

    def body(a_ref, b_ref, f_ref, ff_ref):
        f = _dot(a_ref[...], b_ref[...], NT)
        f_ref[...] = f
        f1 = f[:, :HFF]
        ff_ref[...] = (f1 * _sig(f1) * f[:, HFF:]).astype(ff_ref.dtype)

    return pl.pallas_call(
        body, name="mm_f13_glu", grid=(DFF // HFF, L // tm),
        in_specs=[pl.BlockSpec((tm, D), lambda j, i: (i, 0)), pl.BlockSpec((2 * HFF, D), lambda j, i: (j, 0))],
        out_specs=[pl.BlockSpec((tm, 2 * HFF), lambda j, i: (i, j)), pl.BlockSpec((tm, HFF), lambda j, i: (i, j))],
        out_shape=[jax.ShapeDtypeStruct((L, 2 * DFF), F32), jax.ShapeDtypeStruct((L, DFF), _MXU)],
        compiler_params=_params(("parallel", "parallel")),
    )(h2, w13i)


def _mm_dff_glu(do2, w_ff2_f, f13i):
    L = do2.shape[0]
    tm = 512

    def body(a_ref, b_ref, f_ref, o_ref):
        d = _dot(a_ref[...], b_ref[...], NT)
        f1 = f_ref[:, :HFF]
        s = _sig(f1)
        o_ref[:, :HFF] = (d * f_ref[:, HFF:] * (s * (1.0 + f1 * (1.0 - s)))).astype(o_ref.dtype)
        o_ref[:, HFF:] = (d * f1 * s).astype(o_ref.dtype)

    return pl.pallas_call(
        body, name="mm_dff_glu", grid=(DFF // HFF, L // tm),
        in_specs=[pl.BlockSpec((tm, D), lambda j, i: (i, 0)), pl.BlockSpec((HFF, D), lambda j, i: (j, 0)),
                  pl.BlockSpec((tm, 2 * HFF), lambda j, i: (i, j))],
        out_specs=pl.BlockSpec((tm, 2 * HFF), lambda j, i: (i, j)),
        out_shape=jax.ShapeDtypeStruct((L, 2 * DFF), _MXU),
        compiler_params=_params(("parallel", "parallel")),
    )(do2, w_ff2_f, f13i)


def _conv_bwd(dxf, dxb, p, conv_w8, conv_b, dp):
    RT = p.shape[0]
    chunks = _seq_chunks(RT - TL)

    def body(df_ref, db_ref, p_ref, w_ref, b_ref, dp_any, o_ref, dw_ref, dbias_ref, dpre_s):
        del dp_any
        w = w_ref[...]
        bias = b_ref[...]
        srow = lax.broadcasted_iota(jnp.int32, (8, 128), 0)
        dwacc = jnp.zeros((8, 128), F32)
        dbacc = jnp.zeros((1, 128), F32)
        for r0, first, last in chunks:
            taps = _conv_taps(p_ref, r0, first, last)
            pre = bias + sum(w[k:k + 1, :] * taps[k] for k in range(5))
            s = _sig(pre)
            dpre = (df_ref[pl.ds(r0, TL), :] + db_ref[pl.ds(r0, TL), :]) * (s * (1.0 + pre * (1.0 - s)))
            dpre_s[pl.ds(r0, TL), :] = dpre
            dbacc = dbacc + _colsum(dpre)
            for k in range(5):
                dwacc = dwacc + jnp.where(srow == k, _colsum(dpre * taps[k]), 0.0)
        for r0, first, last in chunks:
            taps = _conv_taps(dpre_s, r0, first, last)
            dx = sum(w[k:k + 1, :] * taps[4 - k] for k in range(5))
            o_ref[pl.ds(r0, TL), :] = dx.astype(o_ref.dtype)
        dw_ref[...] = dwacc
        dbias_ref[...] = jnp.broadcast_to(dbacc, (8, 128))

    cspec = pl.BlockSpec((RT, 128), lambda j: (0, j))
    wspec = pl.BlockSpec((8, 128), lambda j: (0, j))
    return pl.pallas_call(
        body, name="conv_bwd", grid=(12,),
        in_specs=[cspec, cspec, pl.BlockSpec((RT, 128), lambda j: (0, _xbc_colblk(j))),
                  wspec, pl.BlockSpec((1, 128), lambda j: (0, j)), pl.BlockSpec(memory_space=pl.ANY)],
        out_specs=[pl.BlockSpec((RT, 128), lambda j: (0, _xbc_colblk(j))), wspec, wspec],
        out_shape=[jax.ShapeDtypeStruct(dp.shape, dp.dtype), jax.ShapeDtypeStruct((8, 1536), F32),
                   jax.ShapeDtypeStruct((8, 1536), F32)],
        scratch_shapes=[pltpu.VMEM((RT, 128), F32)],
        input_output_aliases={5: 0},
        compiler_params=_params(("parallel",)),
    )(dxf, dxb, p, conv_w8, conv_b, dp)


def _dp_fill(dp):
    nrow = dp.shape[0] // TL

    def body(dp_any, o_ref):
        del dp_any
        o_ref[...] = jnp.zeros_like(o_ref)

    return pl.pallas_call(
        body, name="dp_fill", grid=(1,),
        in_specs=[pl.BlockSpec(memory_space=pl.ANY)],
        out_specs=pl.BlockSpec((TL, OB), lambda i: (nrow - 1, 0)),
        out_shape=jax.ShapeDtypeStruct(dp.shape, dp.dtype),
        input_output_aliases={0: 0},
        compiler_params=_params(("arbitrary",)),
    )(dp)


def _dt_bwd(ddf, ddb, dp):
    RT = ddf.shape[0]
    assert NPJ - ODT == 256

    def body(f_ref, b_ref, dp_any, o_ref, st_ref):
        del dp_any

        @pl.when(pl.program_id(0) == 0)
        def _():
            st_ref[...] = jnp.zeros_like(st_ref)

        s = f_ref[...] + b_ref[...]
        o_ref[:, 0:128] = s.astype(o_ref.dtype)
        o_ref[:, 128:256] = jnp.zeros((TL, 128), o_ref.dtype)
        st_ref[0:1, :] += _colsum(s)

    return pl.pallas_call(
        body, name="dt_bwd", grid=(RT // TL,),
        in_specs=[_rt(128), _rt(128), pl.BlockSpec(memory_space=pl.ANY)],
        out_specs=[_rt(256, ODT // 256), _cst((8, 128))],
        out_shape=[jax.ShapeDtypeStruct(dp.shape, dp.dtype), jax.ShapeDtypeStruct((8, 128), F32)],
        input_output_aliases={2: 0},
        compiler_params=_params(("arbitrary",)),
    )(ddf, ddb, dp)


def _mm_dh1_ln0bwd(dp, w_in_p, dr1, x, ctx, g, b, modx, modc):
    L = x.shape[0]
    nt = L // TL

    def body(dp_ref, w_ref, dr1_ref, x_ref, c_ref, g_ref, b_ref, mx_ref, mc_ref, gx_ref, st_ref):
        i = pl.program_id(0)
        isc = i == nt

        @pl.when(i == 0)
        def _():
            st_ref[...] = jnp.zeros_like(st_ref)

        xin = jnp.where(isc, c_ref[...], x_ref[...])
        xhat, rstd = _ln(xin)
        xn = xhat * g_ref[...] + b_ref[...]
        sc = jnp.where(isc, mc_ref[1:2, :], mx_ref[1:2, :])
        dh = _dot(dp_ref[...], w_ref[...], NT)
        lat = jnp.where(isc, 0.0, 1.0)
        dxn = dh * (1.0 + sc) + (lat * ALPHA) * dr1_ref[...]
        tsh = _colsum(dh)
        tsc = _colsum(dh * xn)
        st_ref[0:1, :] += lat * tsh
        st_ref[1:2, :] += lat * tsc
        st_ref[2:3, :] += (1.0 - lat) * tsh
        st_ref[3:4, :] += (1.0 - lat) * tsc
        st_ref[4:5, :] += _colsum(dxn * xhat)
        st_ref[5:6, :] += _colsum(dxn)

        @pl.when(i < nt)
        def _():
            gx_ref[...] = _ln_bwd(dxn * g_ref[...], xhat, rstd)

    return pl.pallas_call(
        body, name="mm_dh1_ln0bwd", grid=(nt + 1,),
        in_specs=[_rt(NPJ), _cst((D, NPJ)), _rtc(D, nt), _rtc(D, nt), _cst((TL, D)), _cst((1, D)), _cst((1, D)),
                  _cst((8, D)), _cst((8, D))],
        out_specs=[_rtc(D, nt), _cst((8, D))],
        out_shape=[jax.ShapeDtypeStruct((L, D), F32), jax.ShapeDtypeStruct((8, D), F32)],
        compiler_params=pltpu.CompilerParams(dimension_semantics=("arbitrary",),
                                             vmem_limit_bytes=VMEM_LIMIT + 8 * 1024 * 1024),
    )(dp, w_in_p, dr1, x, ctx, g, b, modx, modc)


SECTIONS = ((0, 1024, OZ), (1024, 2048, OXS), (2048, 2304, OB), (2304, 2560, OC), (2560, 2592, ODT),
            (2592, 3616, OU), (3616, 4640, OV), (4640, 6688, OG))


def _perm_from_blocks(ga):
    n = ga.shape[2]
    pieces = []
    for na, nb, _ in sorted(SECTIONS, key=lambda sec: sec[2]):
        for k in range(NDEV):
            lo, hi = max(na, k * n), min(nb, (k + 1) * n)
            if lo < hi:
                pieces.append(ga[k][:, lo - k * n:hi - k * n])
    pieces.append(jnp.zeros((ga.shape[1], NPJ - NNAT), ga.dtype))
    return jnp.concatenate(pieces, axis=1)


def _blocks_from_perm(gp, n):
    blocks = []
    for k in range(NDEV):
        pieces = []
        for na, nb, po in SECTIONS:
            lo, hi = max(na, k * n), min(nb, (k + 1) * n)
            if lo < hi:
                pieces.append(gp[:, po + lo - na:po + hi - na])
        blocks.append(jnp.concatenate(pieces, axis=1))
    return jnp.stack(blocks)


def _padded(n, row_align):
    unit = row_align * D
    return -(-n // unit) * unit if row_align else n


def _slab(arrs, rows, row_align=0):
    parts = []
    for a in arrs:
        f = a.reshape(-1)
        parts.append(jnp.pad(f, (0, _padded(f.shape[0], row_align) - f.shape[0])))
    flat = jnp.concatenate(parts)
    flat = jnp.pad(flat, (0, rows * D - flat.shape[0]))
    return flat.reshape(rows, D)


def _unslab(slab, shapes, row_align=0):
    out, off = [], 0
    for shp in shapes:
        n = 1
        for s in shp:
            n *= s
        r0, r1 = off // D, -(-(off + n) // D)
        out.append(slab[r0:r1].reshape(-1)[off - r0 * D:off - r0 * D + n].reshape(shp))
        off += _padded(n, row_align)
    return out


def _row(v):
    return v.reshape(1, -1)


def _t(a):
    return jnp.swapaxes(a, 0, 1)


def _pad_rows(a, rows):
    return jnp.pad(a, ((0, rows - a.shape[0]), (0, 0)))


REPL = ["c_ctx", "ln0_g", "ln0_b", "b_ada", "conv_b", "dt_bias", "a_log", "d_skip", "ssd_norm_g",
        "gm_norm_g", "gm_norm_b", "w_spatial", "b_spatial", "b_gate", "ln1_g", "ln1_b", "ln2_g", "ln2_b"]
SMALL_ROWS = 160
WEIGHTS = ["c_ctx", "ln0_g", "ln0_b", "w_ada", "b_ada", "w_in", "conv_w", "conv_b", "dt_bias", "a_log",
           "d_skip", "ssd_norm_g", "gm_norm_g", "gm_norm_b", "w_spatial", "b_spatial", "b_gate",
           "w_ssd_proj", "w_gm_proj", "w_out", "ln1_g", "ln1_b", "w_ff1", "w_ff3", "w_ff2", "ln2_g", "ln2_b"]


def kernel(x, c, ctx, c_ctx, ln0_g, ln0_b, w_ada, b_ada, w_in, conv_w, conv_b, dt_bias, a_log, d_skip, ssd_norm_g, gm_norm_g, gm_norm_b, w_spatial, b_spatial, b_gate, w_ssd_proj, w_gm_proj, w_out, ln1_g, ln1_b, w_ff1, w_ff3, w_ff2, ln2_g, ln2_b, loss_target, m_c_ctx, m_ln0_g, m_ln0_b, m_w_ada, m_b_ada, m_w_in, m_conv_w, m_conv_b, m_dt_bias, m_a_log, m_d_skip, m_ssd_norm_g, m_gm_norm_g, m_gm_norm_b, m_w_spatial, m_b_spatial, m_b_gate, m_w_ssd_proj, m_w_gm_proj, m_w_out, m_ln1_g, m_ln1_b, m_w_ff1, m_w_ff3, m_w_ff2, m_ln2_g, m_ln2_b, v_c_ctx, v_ln0_g, v_ln0_b, v_w_ada, v_b_ada, v_w_in, v_conv_w, v_conv_b, v_dt_bias, v_a_log, v_d_skip, v_ssd_norm_g, v_gm_norm_g, v_gm_norm_b, v_w_spatial, v_b_spatial, v_b_gate, v_w_ssd_proj, v_w_gm_proj, v_w_out, v_ln1_g, v_ln1_b, v_w_ff1, v_w_ff3, v_w_ff2, v_ln2_g, v_ln2_b):
    W = dict(c_ctx=c_ctx, ln0_g=ln0_g, ln0_b=ln0_b, w_ada=w_ada, b_ada=b_ada, w_in=w_in, conv_w=conv_w,
             conv_b=conv_b, dt_bias=dt_bias, a_log=a_log, d_skip=d_skip, ssd_norm_g=ssd_norm_g,
             gm_norm_g=gm_norm_g, gm_norm_b=gm_norm_b, w_spatial=w_spatial, b_spatial=b_spatial,
             b_gate=b_gate, w_ssd_proj=w_ssd_proj, w_gm_proj=w_gm_proj, w_out=w_out, ln1_g=ln1_g,
             ln1_b=ln1_b, w_ff1=w_ff1, w_ff3=w_ff3, w_ff2=w_ff2, ln2_g=ln2_g, ln2_b=ln2_b)
    M = dict(c_ctx=m_c_ctx, ln0_g=m_ln0_g, ln0_b=m_ln0_b, w_ada=m_w_ada, b_ada=m_b_ada, w_in=m_w_in,
             conv_w=m_conv_w, conv_b=m_conv_b, dt_bias=m_dt_bias, a_log=m_a_log, d_skip=m_d_skip,
             ssd_norm_g=m_ssd_norm_g, gm_norm_g=m_gm_norm_g, gm_norm_b=m_gm_norm_b,
             w_spatial=m_w_spatial, b_spatial=m_b_spatial, b_gate=m_b_gate, w_ssd_proj=m_w_ssd_proj,
             w_gm_proj=m_w_gm_proj, w_out=m_w_out, ln1_g=m_ln1_g, ln1_b=m_ln1_b, w_ff1=m_w_ff1,
             w_ff3=m_w_ff3, w_ff2=m_w_ff2, ln2_g=m_ln2_g, ln2_b=m_ln2_b)
    V = dict(c_ctx=v_c_ctx, ln0_g=v_ln0_g, ln0_b=v_ln0_b, w_ada=v_w_ada, b_ada=v_b_ada, w_in=v_w_in,
             conv_w=v_conv_w, conv_b=v_conv_b, dt_bias=v_dt_bias, a_log=v_a_log, d_skip=v_d_skip,
             ssd_norm_g=v_ssd_norm_g, gm_norm_g=v_gm_norm_g, gm_norm_b=v_gm_norm_b,
             w_spatial=v_w_spatial, b_spatial=v_b_spatial, b_gate=v_b_gate, w_ssd_proj=v_w_ssd_proj,
             w_gm_proj=v_w_gm_proj, w_out=v_w_out, ln1_g=v_ln1_g, ln1_b=v_ln1_b, w_ff1=v_w_ff1,
             w_ff3=v_w_ff3, w_ff2=v_w_ff2, ln2_g=v_ln2_g, ln2_b=v_ln2_b)

    me = 4 * lax.axis_index("x") + 2 * lax.axis_index("y") + lax.axis_index("c")
    xl, cx, tgt = x[0], ctx[0], loss_target[0]
    L = xl.shape[0]
    assert cx.shape[0] == TL and L % TL == 0
    ada_n = w_ada.shape[2]
    cw_n = conv_w.shape[2]

    small1 = _pad_rows(jnp.concatenate([c, _slab([conv_w[0]], 1)], axis=0), 8)
    g1 = _all_gather(small1, "ag_small")
    c_all = g1[:, 0, :]
    conv_w_full = g1[:, 1, :5 * cw_n].reshape(NDEV, 5, cw_n).transpose(1, 0, 2).reshape(5, NDEV * cw_n)
    sq = w_ssd_proj.shape[1]
    ffr = w_ff2.shape[1]
    ffc = w_ff1.shape[2]
    late = [jnp.concatenate([w_ssd_proj[0], w_gm_proj[0], w_out[0], w_ff2[0]], axis=0).astype(_MXU),
            _t(w_ff1[0]).astype(_MXU), _t(w_ff3[0]).astype(_MXU)]

    c16 = _pad_rows(jnp.concatenate([c_all, _row(c_ctx)], axis=0), 16)
    b_ada_sh = lax.dynamic_slice(b_ada, (0, ada_n * me), (1, ada_n))
    modp = _ada_fwd(c16, w_ada[0], b_ada_sh)
    mod16 = _all_gather(modp, "ag_mod").transpose(1, 0, 2).reshape(16, NDEV * ada_n)

    ga, = _all_gather_multi([w_in[0].astype(_MXU)], "ag_w_in")
    ga, late, mod16 = lax.optimization_barrier((ga, late, mod16))
    lw_send, lw_recv, lw_src, lw_land, lw_token = _exchange_start(late, "ag_late_start", gather=True)
    w_in_p = _perm_from_blocks(ga)
    modx = _pad_rows(lax.dynamic_slice(mod16, (me, 0), (1, 6 * D)).reshape(6, D), 8) + lw_token[0, 0]
    modc = _pad_rows(mod16[8].reshape(6, D), 8)

    g0, b0 = _row(ln0_g), _row(ln0_b)
    xn, h1 = _ln0_fwd(xl, cx, g0, b0, modx, modc)
    p = _mm(h1, w_in_p, "nn", F32, "mm_p", tm_cap=(L + TL) // 2)
    conv_w8 = _pad_rows(conv_w_full, 8)
    xbc = _conv_fwd(p, conv_w8, conv_b)
    prm = _pad_rows(jnp.pad(jnp.stack([dt_bias.reshape(32), a_log.reshape(32)]), ((0, 0), (0, 96))), 8)
    yf, yb, hpf, hpb = _ssd2_fwd(xbc, p, prm)
    lw_land = _exchange_wait(lw_send, lw_recv, lw_src, lw_land, yf, "ag_late_wait", gather=True)
    fw_send, fw_recv, lw_land, fw_token = _forward_start(lw_land, "ag_fwd_start")
    dsk = _row(jnp.repeat(d_skip[0, 0] + d_skip[0, 1], HP)) + fw_token[0:1, 0:1]
    ws_m = w_spatial[0].astype(_MXU)
    bsT = jnp.pad(b_spatial[0].T, ((0, 0), (0, 120)))
    mixp = (dsk, ssd_norm_g, gm_norm_g, gm_norm_b, ws_m, bsT)
    yssd, ygm = _mix_fwd(yf, yb, p, xbc, *mixp)
    gb, gc1, gc2 = _forward_wait(fw_send, fw_recv, lw_land, yssd, "ag_fwd_wait")

    gb, gc1, gc2 = lax.optimization_barrier(
        [lax.dynamic_update_index_in_dim(g, mine, me, 0) for g, mine in zip((gb, gc1, gc2), late)])
    w_ssd_f = gb[:, 0:sq].reshape(NDEV * sq, D)
    w_gm_f = gb[:, sq:2 * sq].reshape(NDEV * sq, D)
    w_out_f = gb[:, 2 * sq:3 * sq].reshape(NDEV * sq, D)
    w_ff2_f = gb[:, 3 * sq:3 * sq + ffr].reshape(NDEV * ffr, D)
    assert HFF == (NDEV // 2) * ffc
    hd = NDEV // 2
    w13i = jnp.concatenate([g[k] for t in range(2) for g in (gc1, gc2) for k in range(t * hd, (t + 1) * hd)],
                           axis=0)
    a1, a2, merged, out, r1, h2 = _merge_fwd(yssd, ygm, p, b_gate, w_ssd_f, w_gm_f, w_out_f,
                                             xn, modx, ln1_g, ln1_b)
    f13, ff = _mm_f13_glu(h2, w13i)

    dr2, do2, st2, loss_slab = _mm_o2_res2(ff, w_ff2_f, r1, tgt, modx, ln1_g, ln1_b, ln2_g, ln2_b)
    loss = lax.psum(loss_slab[0, 0], ("x", "y", "c"))
    df13 = _mm_dff_glu(do2, w_ff2_f, f13)
    dw_ff2 = _mm(ff, do2, "tn", _MXU, "mm_dw_ff2", tm_cap=512, tk_cap=L)
    dw13i = _mm(df13, h2, "tn", _MXU, "mm_dw13", tm_cap=512, tk_cap=L)

    def owner_blocks(first):
        return jnp.concatenate([dw13i[t * 2 * HFF + first:t * 2 * HFF + first + HFF].reshape(NDEV // 2, ffc, D)
                                for t in range(2)], axis=0)

    xff = [dw_ff2.reshape(NDEV, ffr, D), owner_blocks(0), owner_blocks(HFF)]
    ff_send, ff_recv, ff_src, ff_land, ff_token = _exchange_start(xff, "xchg_ff_start")
    modx = modx + ff_token[0, 0]
    dr1, dout, st1 = _mm_dh2_res1bwd(df13, w13i, dr2, r1, out, modx, ln1_g, ln1_b)
    dw_out = _mm(merged, dout, "tn", _MXU, "mm_dw_out")
    dp = _dp_fill(lax.empty((L + TL, NPJ), _MXU))
    dp, da1, da2, stg, dys, dym = _merge_bwd(dout, a1, a2, p, b_gate, w_out_f, w_ssd_f, w_gm_f, dp)
    dw_ssd = _mm(yssd, da1, "tn", _MXU, "mm_dw_ssd")
    dw_gm = _mm(ygm, da2, "tn", _MXU, "mm_dw_gm")
    xsq = [jnp.concatenate([dw_ssd.reshape(NDEV, sq, D), dw_gm.reshape(NDEV, sq, D),
                            dw_out.reshape(NDEV, sq, D)], axis=1)]
    sq_send, sq_recv, sq_src, sq_land, sq_token = _exchange_start(xsq, "xchg_sq_start")
    mixp = (dsk + sq_token[0:1, 0:1],) + mixp[1:]
    dp, dyd, stm, dws, dbsT = _mix_bwd(dys, dym, yf, yb, p, xbc, dp, *mixp)
    dxf, dxb, ddf, ddb, sts = _ssd2_bwd(xbc, p, prm, dsk, dyd, hpf, hpb)
    dp, dcw, dcb = _conv_bwd(dxf, dxb, p, conv_w8, conv_b, dp)
    dp, std = _dt_bwd(ddf, ddb, dp)
    hw = D // 2
    xin_a = [_blocks_from_perm(_mm(h1[:, :hw], dp, "tn", _MXU, "mm_dw_in_a", tk_cap=L + TL), w_in.shape[2])]
    ina_send, ina_recv, ina_src, ina_land, ina_token = _exchange_start(xin_a, "xchg_in_a_start")
    h1b, ina_token = lax.optimization_barrier((h1[:, hw:], ina_token))
    xin_b = [_blocks_from_perm(_mm(h1b, dp, "tn", _MXU, "mm_dw_in_b", tk_cap=L + TL), w_in.shape[2])]
    inb_send, inb_recv, inb_src, inb_land, inb_token = _exchange_start(xin_b, "xchg_in_b_start")
    modx = modx + (ina_token[0, 0] + inb_token[0, 0])
    grad_x, st0 = _mm_dh1_ln0bwd(dp, w_in_p, dr1, xl, cx, g0, b0, modx, modc)

    zero = jnp.zeros((D,), F32)
    dmod = jnp.stack([jnp.concatenate([st0[0], st0[1], st1[4], st1[1], st1[0], st2[2]]),
                      jnp.concatenate([st0[2], st0[3], zero, zero, zero, zero])])
    g16 = _all_gather(_pad_rows(dmod, 8), "ag_dmod")[:, 0:2, :].reshape(16, 6 * D)
    g16_sh = lax.dynamic_slice(g16, (0, ada_n * me), (16, ada_n))
    c16b = jnp.stack([c_all, jnp.broadcast_to(_row(c_ctx), (NDEV, D))], axis=1).reshape(16, D)
    dw_ada, db_ada8, dcc8 = _ada_bwd(c16b, g16, g16_sh, w_ada[0])

    part = dict(
        c_ctx=dcc8[0], ln0_g=st0[4], ln0_b=st0[5], conv_w=dcw[0:5], conv_b=dcb[0],
        dt_bias=std[0, 0:32], a_log=sts[0, 0:32], d_skip=jnp.tile(sts[1, 0:16], 2),
        ssd_norm_g=stm[0], gm_norm_g=stm[1], gm_norm_b=stm[2], w_spatial=dws,
        b_spatial=dbsT[:, 0:8].T, b_gate=stg[0], ln1_g=st1[2], ln1_b=st1[3], ln2_g=st2[0], ln2_b=st2[1])
    pnames = [n for n in part if n != "w_spatial"]
    g_small, g_ws = _all_gather_multi(
        [_slab([part[n] for n in pnames], 24), dws.reshape(Q, D).astype(_MXU)], "ag_smallgrads")
    small = dict(zip(pnames, _unslab(_sum8(g_small, "sum_smallgrads"), [part[n].shape for n in pnames])))
    small["w_spatial"] = _sum8(g_ws, "sum_w_spatial").reshape(dws.shape)
    grads = {n: small[n].reshape(W[n].shape) for n in small if n != "conv_w"}
    grads["conv_w"] = lax.dynamic_slice(small["conv_w"], (0, cw_n * me), (5, cw_n)).reshape(conv_w.shape)
    grads["b_ada"] = db_ada8[0:1]
    grads["w_ada"] = dw_ada.reshape(w_ada.shape)

    delta, new_m, new_v = {}, {}, {}

    def adam_group(names, rows, tag, align=0):
        shapes = [W[n].shape for n in names]
        outs = _adamw(*[_slab([src[n] for n in names], rows, align) for src in (grads, W, M, V)], tag)
        for res, slab in zip((delta, new_m, new_v), outs):
            for n, a in zip(names, _unslab(slab, shapes, align)):
                res[n] = a

    adam_group(REPL + ["conv_w"], SMALL_ROWS, "adamw_small")
    res = _adamw(grads["w_ada"][0], w_ada[0], m_w_ada[0], v_w_ada[0], "adamw_w_ada")
    delta["w_ada"], new_m["w_ada"], new_v["w_ada"] = [a[None] for a in res]

    rff = _exchange_wait(ff_send, ff_recv, ff_src, ff_land, st0, "xchg_ff_wait")
    rsq = _exchange_wait(sq_send, sq_recv, sq_src, sq_land, rff[0], "xchg_sq_wait")
    rin_a = _exchange_wait(ina_send, ina_recv, ina_src, ina_land, delta["ln2_b"], "xchg_in_a_wait")
    rin_b = _exchange_wait(inb_send, inb_recv, inb_src, inb_land, rin_a[0], "xchg_in_b_wait")
    rin = jnp.concatenate([rin_a[0], rin_b[0]], axis=1)

    def own(blocks):
        return lax.dynamic_index_in_dim(blocks, me, 0, keepdims=False)

    own_in = jnp.concatenate([own(xin_a[0]), own(xin_b[0])], axis=0)
    for n, r8, mine, row0, tr in (
            ("w_ff2", rff[0], own(xff[0]), 0, ffr // 2), ("w_ssd_proj", rsq[0], own(xsq[0]), 0, sq),
            ("w_gm_proj", rsq[0], own(xsq[0]), sq, sq), ("w_out", rsq[0], own(xsq[0]), 2 * sq, sq),
            ("w_in", rin, own_in, 0, 256)):
        res = _adamw_sum(r8, mine, W[n][0], M[n][0], V[n][0], row0, tr, "adamw_" + n)
        grads[n], delta[n], new_m[n], new_v[n] = [a[None] for a in res]
    for n, r8, mine in (("w_ff1", rff[1], own(xff[1])), ("w_ff3", rff[2], own(xff[2]))):
        res = _adamw_sum(r8, mine, _t(W[n][0]), _t(M[n][0]), _t(V[n][0]), 0, ffc // 2, "adamw_" + n)
        grads[n], delta[n], new_m[n], new_v[n] = [_t(a)[None] for a in res]

    return (loss, grad_x[None], *[grads[n] for n in WEIGHTS], *[delta[n] for n in WEIGHTS],
            *[new_m[n] for n in WEIGHTS], *[new_v[n] for n in WEIGHTS])
```

```python
import jax
import jax.numpy as jnp
from jax import lax
from jax.experimental import pallas as pl
from jax.experimental.pallas import tpu as pltpu

_MXU = jnp.bfloat16
F32 = jnp.float32
D = 1024
TL = 256
Q = 128
NH, HP, NS, HPG = 16, 64, 128, 8
DFF = 2816
ALPHA = 2.0 ** 0.25
EPS = 1e-5
OZ, OU, OV, OXS, OG, OB, OC, ODT, NPJ = 0, 1024, 2048, 3072, 4096, 6144, 6400, 6656, 6912
NNAT = 6688
NDEV = 8
ADAM_LR, ADAM_B1, ADAM_B2, ADAM_EPS, ADAM_WD, ADAM_STEP = 1e-3, 0.9, 0.999, 1e-8, 0.01, 10
VMEM_LIMIT = 48 * 1024 * 1024

NN = ((1,), (0,))
NT = ((1,), (1,))
TN = ((0,), (0,))
MESH = pl.DeviceIdType.MESH


def _dot(a, b, dims):
    return lax.dot_general(a.astype(_MXU), b.astype(_MXU), (dims, ((), ())),
                           preferred_element_type=F32)


def _tile(n, cands):
    for c in cands:
        if n % c == 0:
            return c
    return n


def _divisor_tile(n, cap, mult):
    best = n
    for t in range(mult, min(n, cap) + 1, mult):
        if n % t == 0:
            best = t
    return best


def _params(sem):
    return pltpu.CompilerParams(dimension_semantics=sem, vmem_limit_bytes=VMEM_LIMIT)


def _cst(shape):
    nd = len(shape)
    return pl.BlockSpec(shape, lambda *_: (0,) * nd)


def _rt(w, cb=0, rows=TL):
    return pl.BlockSpec((rows, w), lambda i: (i, cb))


def _rtc(w, nt, cb=0):
    return pl.BlockSpec((TL, w), lambda i: (jnp.minimum(i, nt - 1), cb))


def _sig(x):
    return jax.nn.sigmoid(x)


def _softplus(x):
    return jnp.maximum(x, 0.0) + jnp.log1p(jnp.exp(-jnp.abs(x)))


_G0, _G1 = 0.7978845608028654, 0.044715


def _gelu(x):
    t = jnp.tanh(_G0 * (x + _G1 * x * x * x))
    return 0.5 * x * (1.0 + t), t


def _gelu_grad(x, t):
    return 0.5 * (1.0 + t) + 0.5 * x * (1.0 - t * t) * _G0 * (1.0 + 3.0 * _G1 * x * x)


def _ln(r):
    mu = jnp.mean(r, axis=-1, keepdims=True)
    xc = r - mu
    var = jnp.mean(xc * xc, axis=-1, keepdims=True)
    rstd = lax.rsqrt(var + EPS)
    return xc * rstd, rstd


def _ln_bwd(dyh, xhat, rstd):
    return rstd * (dyh - jnp.mean(dyh, axis=-1, keepdims=True)
                   - xhat * jnp.mean(dyh * xhat, axis=-1, keepdims=True))


def _colsum(v):
    return jnp.sum(v, axis=0, keepdims=True)


def _cumsum_rows(a, rev):
    n = a.shape[0]
    row = lax.broadcasted_iota(jnp.int32, a.shape, 0)
    s = 1
    while s < n:
        if rev:
            a = a + jnp.where(row < n - s, pltpu.roll(a, n - s, 0), 0.0)
        else:
            a = a + jnp.where(row >= s, pltpu.roll(a, s, 0), 0.0)
        s *= 2
    return a


def _mm(a, b, mode, out_dtype, name, tm_cap=None, tk_cap=2304):
    if mode == "tn":
        K, M = a.shape
    else:
        M, K = a.shape
    N = b.shape[0] if mode == "nt" else b.shape[1]
    if mode == "tn":
        tm = _divisor_tile(M, tm_cap or 1408, 128)
    else:
        tm = _divisor_tile(M, tm_cap or 1088, 16)
    tn = _divisor_tile(N, 1408, 128)
    tk = _divisor_tile(K, tk_cap, 128)
    nk = K // tk
    dims = {"nn": NN, "nt": NT, "tn": TN}[mode]
    use_acc = nk > 1 and out_dtype != F32

    def body(a_ref, b_ref, o_ref, *acc):
        prod = _dot(a_ref[...], b_ref[...], dims)
        if nk == 1:
            o_ref[...] = prod.astype(o_ref.dtype)
            return
        acc_ref = acc[0] if use_acc else o_ref
        k = pl.program_id(2)

        @pl.when(k == 0)
        def _():
            acc_ref[...] = prod

        if use_acc:
            @pl.when((k > 0) & (k < nk - 1))
            def _():
                acc_ref[...] += prod

            @pl.when(k == nk - 1)
            def _():
                o_ref[...] = (acc_ref[...] + prod).astype(o_ref.dtype)
        else:
            @pl.when(k > 0)
            def _():
                o_ref[...] += prod

    if mode == "tn":
        a_spec = pl.BlockSpec((tk, tm), lambda i, j, k: (k, i))
    else:
        a_spec = pl.BlockSpec((tm, tk), lambda i, j, k: (i, k))
    if mode == "nt":
        b_spec = pl.BlockSpec((tn, tk), lambda i, j, k: (j, k))
    else:
        b_spec = pl.BlockSpec((tk, tn), lambda i, j, k: (k, j))
    return pl.pallas_call(
        body, name=name, grid=(M // tm, N // tn, nk),
        in_specs=[a_spec, b_spec],
        out_specs=pl.BlockSpec((tm, tn), lambda i, j, k: (i, j)),
        out_shape=jax.ShapeDtypeStruct((M, N), out_dtype),
        scratch_shapes=[pltpu.VMEM((tm, tn), F32)] if use_acc else [],
        compiler_params=_params(("parallel", "parallel", "arbitrary")),
    )(a, b)


def _all_gather(x, name):
    def body(x_ref, out_ref, send_sems, recv_sems, local_sem):
        mx, my, mc = lax.axis_index("x"), lax.axis_index("y"), lax.axis_index("c")
        me, sibling = (mx, my, mc), (mx, my, 1 - mc)
        chips = [(1 - mx, my), (mx, 1 - my), (1 - mx, 1 - my)]

        def slot(px, py, pc):
            return out_ref.at[4 * px + 2 * py + pc]

        def copy(k, block, to, src=None):
            return pltpu.make_async_remote_copy(
                src_ref=slot(*block) if src is None else src, dst_ref=slot(*block),
                send_sem=send_sems.at[k], recv_sem=recv_sems.at[k],
                device_id=to, device_id_type=MESH)

        mine = pltpu.make_async_copy(x_ref, slot(*me), local_sem)
        mine.start()
        first = [copy(0, me, sibling, src=x_ref)]
        first += [copy(1 + j, me, (*chip, mc), src=x_ref) for j, chip in enumerate(chips)]
        for cp in first:
            cp.start()
        passed = [copy(4 + j, (*chip, mc), sibling) for j, chip in enumerate(chips)]
        for j, chip in enumerate(chips):
            copy(1 + j, (*chip, mc), me).wait_recv()
            passed[j].start()
        copy(0, sibling, me).wait_recv()
        for j, chip in enumerate(chips):
            copy(4 + j, (*chip, 1 - mc), me).wait_recv()
        for cp in first + passed:
            cp.wait_send()
        mine.wait()

    return pl.pallas_call(
        body, name=name,
        out_shape=jax.ShapeDtypeStruct((NDEV,) + x.shape, x.dtype),
        in_specs=[pl.BlockSpec(memory_space=pl.ANY)],
        out_specs=pl.BlockSpec(memory_space=pl.ANY),
        scratch_shapes=[pltpu.SemaphoreType.DMA((7,)), pltpu.SemaphoreType.DMA((7,)),
                        pltpu.SemaphoreType.DMA],
    )(x)


def _any_specs(n):
    return [pl.BlockSpec(memory_space=pl.ANY)] * n


def _all_gather_multi(xs, name):
    na = len(xs)

    def body(*refs):
        x_refs, out_refs = refs[:na], refs[na:2 * na]
        send_sems, recv_sems, local_sems = refs[2 * na:]
        mx, my, mc = lax.axis_index("x"), lax.axis_index("y"), lax.axis_index("c")
        me, sibling = (mx, my, mc), (mx, my, 1 - mc)
        chips = [(1 - mx, my), (mx, 1 - my), (1 - mx, 1 - my)]

        def copy(a, k, block, to, src=None):
            slot = out_refs[a].at[4 * block[0] + 2 * block[1] + block[2]]
            return pltpu.make_async_remote_copy(
                src_ref=slot if src is None else src, dst_ref=slot,
                send_sem=send_sems.at[7 * a + k], recv_sem=recv_sems.at[7 * a + k],
                device_id=to, device_id_type=MESH)

        mine = [pltpu.make_async_copy(x_refs[a], out_refs[a].at[4 * mx + 2 * my + mc], local_sems.at[a])
                for a in range(na)]
        for cp in mine:
            cp.start()
        first = []
        for a in range(na):
            first.append(copy(a, 0, me, sibling, src=x_refs[a]))
            first += [copy(a, 1 + j, me, (*chip, mc), src=x_refs[a]) for j, chip in enumerate(chips)]
        for cp in first:
            cp.start()
        passed = []
        for a in range(na):
            for j, chip in enumerate(chips):
                copy(a, 1 + j, (*chip, mc), me).wait_recv()
                fwd = copy(a, 4 + j, (*chip, mc), sibling)
                fwd.start()
                passed.append(fwd)
        for a in range(na):
            copy(a, 0, sibling, me).wait_recv()
            for j, chip in enumerate(chips):
                copy(a, 4 + j, (*chip, 1 - mc), me).wait_recv()
        for cp in first + passed:
            cp.wait_send()
        for cp in mine:
            cp.wait()

    return pl.pallas_call(
        body, name=name,
        out_shape=[jax.ShapeDtypeStruct((NDEV,) + x.shape, x.dtype) for x in xs],
        in_specs=_any_specs(na), out_specs=_any_specs(na),
        scratch_shapes=[pltpu.SemaphoreType.DMA((7 * na,)), pltpu.SemaphoreType.DMA((7 * na,)),
                        pltpu.SemaphoreType.DMA((na,))],
    )(*xs)


def _adamw_sum(r8, own, w, m, v, row0, tr, name):
    R, C = w.shape
    assert row0 % tr == 0
    blk0 = row0 // tr
    bc1 = 1.0 - ADAM_B1 ** ADAM_STEP
    bc2 = 1.0 - ADAM_B2 ** ADAM_STEP

    def body(r_ref, *refs):
        if own is None:
            gg = r_ref[0].astype(F32)
        else:
            gg = refs[0][...].astype(F32)
            refs = refs[1:]
        w_ref, m_ref, v_ref, g_ref, d_ref, mo_ref, vo_ref = refs
        for k in range(1, NDEV):
            gg = gg + r_ref[k].astype(F32)
        mn = ADAM_B1 * m_ref[...] + (1.0 - ADAM_B1) * gg
        vn = ADAM_B2 * v_ref[...] + (1.0 - ADAM_B2) * (gg * gg)
        mh = mn / bc1
        vh = vn / bc2
        g_ref[...] = gg
        d_ref[...] = -ADAM_LR * (mh / (jnp.sqrt(vh) + ADAM_EPS) + ADAM_WD * w_ref[...])
        mo_ref[...] = mn
        vo_ref[...] = vn

    spec = pl.BlockSpec((tr, C), lambda i: (i, 0))
    sh = jax.ShapeDtypeStruct((R, C), F32)
    own_ops = [] if own is None else [own]
    own_specs = [] if own is None else [pl.BlockSpec((tr, C), lambda i: (i + blk0, 0))]
    return pl.pallas_call(
        body, name=name, grid=(R // tr,),
        in_specs=[pl.BlockSpec((NDEV, tr, C), lambda i: (0, i + blk0, 0))] + own_specs + [spec, spec, spec],
        out_specs=[spec] * 4, out_shape=[sh] * 4, compiler_params=_params(("parallel",)),
    )(r8, *own_ops, w, m, v)


_HBM = pl.BlockSpec(memory_space=pltpu.HBM)
_SEM = pl.BlockSpec(memory_space=pltpu.SEMAPHORE)
_EFFECT = pltpu.SideEffectType.DATAFLOW_SIDE_EFFECTING


def _exchange_copies(g_refs, land_refs, send_sems, recv_sems, gather):
    mx, my, mc = lax.axis_index("x"), lax.axis_index("y"), lax.axis_index("c")
    copies = []
    for a in range(len(g_refs)):
        for f in ((1, 2, 4, 6) if gather else range(1, NDEV)):
            px = 1 - mx if (f >> 2) & 1 else mx
            py = 1 - my if (f >> 1) & 1 else my
            pc = 1 - mc if f & 1 else mc
            src = g_refs[a] if gather else g_refs[a].at[4 * px + 2 * py + pc]
            dst = land_refs[a].at[4 * mx + 2 * my + mc] if gather else land_refs[a].at[f]
            copies.append(pltpu.make_async_remote_copy(
                src_ref=src, dst_ref=dst,
                send_sem=send_sems.at[7 * a + f - 1], recv_sem=recv_sems.at[7 * a + f - 1],
                device_id=(px, py, pc), device_id_type=MESH))
    return copies


def _exchange_start(gs, name, gather=False):
    na = len(gs)

    def body(*refs):
        for cp in _exchange_copies(refs[:na], refs[na:2 * na], refs[2 * na], refs[2 * na + 1], gather):
            cp.start()
        refs[-1][...] = jnp.zeros_like(refs[-1])

    hbm = [pltpu.HBM(g.shape, g.dtype) for g in gs]
    land_shapes = [((NDEV,) + g.shape) if gather else g.shape for g in gs]
    lands = [pltpu.with_memory_space_constraint(lax.empty(shp, g.dtype), pltpu.HBM)
             for shp, g in zip(land_shapes, gs)]
    hbm_land = [pltpu.HBM(shp, g.dtype) for shp, g in zip(land_shapes, gs)]
    outs = pl.pallas_call(
        body, name=name,
        out_shape=(pltpu.SemaphoreType.DMA((7 * na,)), pltpu.SemaphoreType.DMA((7 * na,)), *hbm, *hbm_land,
                   jax.ShapeDtypeStruct((8, 128), F32)),
        in_specs=[_HBM] * (2 * na),
        out_specs=(_SEM, _SEM, *([_HBM] * (2 * na)), pl.BlockSpec(memory_space=pltpu.VMEM)),
        input_output_aliases={i: 2 + i for i in range(2 * na)},
        compiler_params=pltpu.CompilerParams(has_side_effects=_EFFECT),
    )(*[pltpu.with_memory_space_constraint(g, pltpu.HBM) for g in gs], *lands)
    return outs[0], outs[1], outs[2:2 + na], outs[2 + na:2 + 2 * na], outs[-1]


def _forward_copies(land_refs, send_sems, recv_sems):
    mx, my, mc = lax.axis_index("x"), lax.axis_index("y"), lax.axis_index("c")
    copies = []
    for a in range(len(land_refs)):
        for j, (fx, fy) in enumerate(((0, 1), (1, 0), (1, 1))):
            px = 1 - mx if fx else mx
            py = 1 - my if fy else my
            blk = land_refs[a].at[4 * px + 2 * py + mc]
            copies.append(pltpu.make_async_remote_copy(
                src_ref=blk, dst_ref=blk, send_sem=send_sems.at[3 * a + j], recv_sem=recv_sems.at[3 * a + j],
                device_id=(mx, my, 1 - mc), device_id_type=MESH))
    return copies


def _forward_start(lands, name):
    na = len(lands)

    def body(*refs):
        for cp in _forward_copies(refs[:na], refs[na], refs[na + 1]):
            cp.start()
        refs[-1][...] = jnp.zeros_like(refs[-1])

    outs = pl.pallas_call(
        body, name=name,
        out_shape=(pltpu.SemaphoreType.DMA((3 * na,)), pltpu.SemaphoreType.DMA((3 * na,)),
                   *[pltpu.HBM(g.shape, g.dtype) for g in lands], jax.ShapeDtypeStruct((8, 128), F32)),
        in_specs=[_HBM] * na,
        out_specs=(_SEM, _SEM, *([_HBM] * na), pl.BlockSpec(memory_space=pltpu.VMEM)),
        input_output_aliases={i: 2 + i for i in range(na)},
        compiler_params=pltpu.CompilerParams(has_side_effects=_EFFECT),
    )(*lands)
    return outs[0], outs[1], outs[2:2 + na], outs[-1]


def _forward_wait(send_sems, recv_sems, lands, after, name):
    na = len(lands)

    def body(*refs):
        for cp in _forward_copies(refs[:na], refs[na], refs[na + 1]):
            cp.wait_send()
            cp.wait_recv()

    return pl.pallas_call(
        body, name=name,
        out_shape=tuple(pltpu.HBM(g.shape, g.dtype) for g in lands),
        in_specs=[_HBM] * na + [_SEM, _SEM, pl.BlockSpec(memory_space=pl.ANY)],
        out_specs=tuple([_HBM] * na),
        input_output_aliases={i: i for i in range(na)},
        compiler_params=pltpu.CompilerParams(has_side_effects=_EFFECT),
    )(*lands, send_sems, recv_sems, after)


def _exchange_wait(send_sems, recv_sems, g_thru, land_thru, after, name, gather=False):
    na = len(g_thru)

    def body(*refs):
        for cp in _exchange_copies(refs[:na], refs[na:2 * na], refs[2 * na], refs[2 * na + 1], gather):
            cp.wait_send()
            cp.wait_recv()

    outs = pl.pallas_call(
        body, name=name,
        out_shape=tuple(pltpu.HBM(g.shape, g.dtype) for g in list(g_thru) + list(land_thru)),
        in_specs=[_HBM] * (2 * na) + [_SEM, _SEM, pl.BlockSpec(memory_space=pl.ANY)],
        out_specs=tuple([_HBM] * (2 * na)),
        input_output_aliases={i: i for i in range(2 * na)},
        compiler_params=pltpu.CompilerParams(has_side_effects=_EFFECT),
    )(*g_thru, *land_thru, send_sems, recv_sems, after)
    return outs[na:]


def _sum8(r, name):
    _, R, C = r.shape
    tr = _tile(R, (256, 160, 128, 64, 32, 16, 8))

    def body(r_ref, o_ref):
        acc = r_ref[0].astype(F32)
        for k in range(1, NDEV):
            acc = acc + r_ref[k].astype(F32)
        o_ref[...] = acc

    return pl.pallas_call(
        body, name=name, grid=(R // tr,),
        in_specs=[pl.BlockSpec((NDEV, tr, C), lambda i: (0, i, 0))],
        out_specs=pl.BlockSpec((tr, C), lambda i: (i, 0)),
        out_shape=jax.ShapeDtypeStruct((R, C), F32),
        compiler_params=_params(("parallel",)),
    )(r)


def _adamw(g, w, m, v, name):
    R, C = g.shape
    tr = _tile(R, (256, 160, 128, 64, 32, 16, 8))
    bc1 = 1.0 - ADAM_B1 ** ADAM_STEP
    bc2 = 1.0 - ADAM_B2 ** ADAM_STEP

    def body(g_ref, w_ref, m_ref, v_ref, d_ref, mo_ref, vo_ref):
        gg = g_ref[...]
        mn = ADAM_B1 * m_ref[...] + (1.0 - ADAM_B1) * gg
        vn = ADAM_B2 * v_ref[...] + (1.0 - ADAM_B2) * (gg * gg)
        mh = mn / bc1
        vh = vn / bc2
        d_ref[...] = -ADAM_LR * (mh / (jnp.sqrt(vh) + ADAM_EPS) + ADAM_WD * w_ref[...])
        mo_ref[...] = mn
        vo_ref[...] = vn

    spec = pl.BlockSpec((tr, C), lambda i: (i, 0))
    sh = jax.ShapeDtypeStruct((R, C), F32)
    return pl.pallas_call(
        body, name=name, grid=(R // tr,), in_specs=[spec] * 4, out_specs=[spec] * 3,
        out_shape=[sh] * 3, compiler_params=_params(("parallel",)),
    )(g, w, m, v)


def _ada_fwd(c16, w_sh, b_sh):
    def body(c_ref, w_ref, b_ref, o_ref):
        c = c_ref[...]
        o_ref[...] = _dot(c * _sig(c), w_ref[...], NN) + b_ref[...]

    return pl.pallas_call(
        body, name="ada_fwd", out_shape=jax.ShapeDtypeStruct((16, w_sh.shape[1]), F32),
        compiler_params=pltpu.CompilerParams(vmem_limit_bytes=VMEM_LIMIT),
    )(c16, w_sh, b_sh)


def _ada_bwd(c16, g16, g16_sh, w_sh):
    ncol = w_sh.shape[1]

    def body(c_ref, g_ref, gs_ref, w_ref, dw_ref, db_ref, dc_ref):
        c = c_ref[...]
        s = _sig(c)
        gs = gs_ref[...]
        dw_ref[...] = _dot(c * s, gs, TN)
        db_ref[...] = jnp.broadcast_to(_colsum(g_ref[...]), db_ref.shape)
        odd = lax.broadcasted_iota(jnp.int32, gs.shape, 0) % 2 == 1
        gc = _colsum(jnp.where(odd, gs, 0.0))
        ds = _dot(jnp.broadcast_to(gc, (8, ncol)), w_ref[...], NT)
        c1 = c[1:2, :]
        s1 = s[1:2, :]
        dc_ref[...] = ds * (s1 * (1.0 + c1 * (1.0 - s1)))

    return pl.pallas_call(
        body, name="ada_bwd",
        out_shape=[jax.ShapeDtypeStruct(w_sh.shape, F32),
                   jax.ShapeDtypeStruct((8, g16.shape[1]), F32),
                   jax.ShapeDtypeStruct((8, D), F32)],
        compiler_params=pltpu.CompilerParams(vmem_limit_bytes=VMEM_LIMIT),
    )(c16, g16, g16_sh, w_sh)


def _ln0_fwd(x, ctx, g, b, modx, modc):
    L = x.shape[0]
    nt = L // TL

    def body(x_ref, c_ref, g_ref, b_ref, mx_ref, mc_ref, xn_ref, h_ref):
        isc = pl.program_id(0) == nt
        xin = jnp.where(isc, c_ref[...], x_ref[...])
        sh = jnp.where(isc, mc_ref[0:1, :], mx_ref[0:1, :])
        sc = jnp.where(isc, mc_ref[1:2, :], mx_ref[1:2, :])
        xhat, _ = _ln(xin)
        xn = xhat * g_ref[...] + b_ref[...]
        xn_ref[...] = xn
        h_ref[...] = (xn * (1.0 + sc) + sh).astype(h_ref.dtype)

    return pl.pallas_call(
        body, name="ln0_fwd", grid=(nt + 1,),
        in_specs=[_rtc(D, nt), _cst((TL, D)), _cst((1, D)), _cst((1, D)), _cst((8, D)), _cst((8, D))],
        out_specs=[_rt(D), _rt(D)],
        out_shape=[jax.ShapeDtypeStruct((L + TL, D), F32), jax.ShapeDtypeStruct((L + TL, D), _MXU)],
        compiler_params=_params(("parallel",)),
    )(x, ctx, g, b, modx, modc)


def _xbc_colblk(j):
    return jnp.where(j < 8, OXS // 128 + j, OB // 128 + j - 8)


def _conv_taps(p_ref, r0, first, last):
    main = p_ref[pl.ds(r0, TL), :]
    zero = jnp.zeros((8, main.shape[1]), F32)
    prev = zero if first else p_ref[pl.ds(r0 - 8, 8), :]
    nxt = zero if last else p_ref[pl.ds(r0 + TL, 8), :]
    ext = jnp.concatenate([prev, main, nxt], axis=0)
    n = TL + 16
    return [pltpu.roll(ext, (2 - k) % n, 0)[8:8 + TL] for k in range(5)]


def _seq_chunks(L):
    nt = L // TL
    return [(r * TL, r == 0, r == nt - 1) for r in range(nt)] + [(L, True, True)]


def _conv_fwd(p, conv_w8, conv_b):
    RT = p.shape[0]
    L = RT - TL
    chunks = _seq_chunks(L)

    def body(p_ref, w_ref, b_ref, o_ref):
        w = w_ref[...]
        bias = b_ref[...]
        for r0, first, last in chunks:
            taps = _conv_taps(p_ref, r0, first, last)
            pre = bias + sum(w[k:k + 1, :] * taps[k] for k in range(5))
            o_ref[pl.ds(r0, TL), :] = pre * _sig(pre)

    return pl.pallas_call(
        body, name="conv_fwd", grid=(12,),
        in_specs=[pl.BlockSpec((RT, 128), lambda j: (0, _xbc_colblk(j))),
                  pl.BlockSpec((8, 128), lambda j: (0, j)),
                  pl.BlockSpec((1, 128), lambda j: (0, j))],
        out_specs=pl.BlockSpec((RT, 128), lambda j: (0, j)),
        out_shape=jax.ShapeDtypeStruct((RT, 1536), F32),
        compiler_params=_params(("parallel",)),
    )(p, conv_w8, conv_b)


def _ssd_common(dtraw, dtb, a32, rev):
    dt = _softplus(dtraw + dtb)
    acum = _cumsum_rows(dt * a32, rev)
    ii = lax.broadcasted_iota(jnp.int32, (Q, Q), 0)
    jj = lax.broadcasted_iota(jnp.int32, (Q, Q), 1)
    mask = (ii <= jj) if rev else (ii >= jj)
    return dt, acum, acum.T, dt.T, mask


def _ssd_orders(ncl, ncc):
    nc = ncl + ncc

    def cf(s):
        return jnp.where(s < ncc, ncl + s, s - ncc)

    def cb(s):
        return nc - 1 - s

    return cf, cb


def _lane_bcast(v, ln):
    return jnp.broadcast_to(v[:, ln:ln + 1], v.shape)


def _halves(v, lo, axis):
    return jnp.concatenate([jnp.where(lo, v, 0.0), jnp.where(lo, 0.0, v)], axis=axis)


def _ssd2_fwd(xbc, p, prm):
    RT = xbc.shape[0]
    nc = RT // Q
    ncc = TL // Q
    cf, cb = _ssd_orders(nc - ncc, ncc)

    def one_dir(x_ref, dt_ref, prm_ref, y_ref, hp_ref, HT_ref, d):
        rev = d == 1
        a32 = -jnp.exp(prm_ref[1:2, :])
        dt, acum, acumT, dtT, mask = _ssd_common(dt_ref[...], prm_ref[0:1, :], a32, rev)
        end = 0 if rev else Q - 1
        lo = lax.broadcasted_iota(jnp.int32, (Q, 128), 1) < HP
        for g in range(2):
            Bg = x_ref[:, D + g * NS:D + (g + 1) * NS]
            Cg = x_ref[:, D + 2 * NS + g * NS:D + 2 * NS + (g + 1) * NS]
            CB = _dot(Cg, Bg, NT)
            xds, svs = [], []
            for q in range(HPG // 2):
                pi = g * (HPG // 2) + q
                ps = slice(pi * 128, (pi + 1) * 128)
                Xp = x_ref[:, ps]
                HTp = HT_ref[:, ps]
                lhs, dcs, sv = [], [], []
                ces = []
                for h in (2 * pi, 2 * pi + 1):
                    ln = 16 * d + h
                    colB = _lane_bcast(acum, ln)
                    rowv = acumT[ln:ln + 1, :]
                    aend = colB[end:end + 1, :]
                    Lm = jnp.exp(jnp.where(mask, colB - rowv, -1e30))
                    lhs.append(CB * Lm * dtT[ln:ln + 1, :])
                    ces.append(Cg * jnp.exp(colB))
                    dcs.append(jnp.exp(aend - colB) * _lane_bcast(dt, ln))
                    sv.append(jnp.exp(aend))
                lhs = jnp.concatenate(lhs + ces, axis=1)
                rhs = jnp.concatenate([_halves(Xp, lo, 0), _halves(HTp, lo, 0)], axis=0)
                y_ref[:, ps] = _dot(lhs, rhs, NN)
                xds.append(Xp * jnp.where(lo, dcs[0], dcs[1]))
                svs.append(jnp.where(lo[0:1, :], sv[0], sv[1]))
            gs = slice(g * 512, (g + 1) * 512)
            HTg = HT_ref[:, gs]
            hp_ref[0, :, gs] = HTg
            st = _dot(Bg.T, jnp.concatenate(xds, axis=1), NN)
            HT_ref[:, gs] = jnp.concatenate(svs, axis=1) * HTg + st

    def body(xf_ref, xb_ref, df_ref, db_ref, prm_ref, yf_ref, yb_ref, hf_ref, hb_ref, Hf, Hb):
        @pl.when(pl.program_id(0) == 0)
        def _():
            Hf[...] = jnp.zeros_like(Hf)
            Hb[...] = jnp.zeros_like(Hb)

        one_dir(xf_ref, df_ref, prm_ref, yf_ref, hf_ref, Hf, 0)
        one_dir(xb_ref, db_ref, prm_ref, yb_ref, hb_ref, Hb, 1)

    ysh = jax.ShapeDtypeStruct((RT, D), F32)
    hsh = jax.ShapeDtypeStruct((nc, NS, NH * HP), F32)
    hspec = pl.BlockSpec((1, NS, NH * HP), lambda s: (s, 0, 0))
    return pl.pallas_call(
        body, name="ssd_fwd", grid=(nc,),
        in_specs=[pl.BlockSpec((Q, 1536), lambda s: (cf(s), 0)),
                  pl.BlockSpec((Q, 1536), lambda s: (cb(s), 0)),
                  pl.BlockSpec((Q, 128), lambda s: (cf(s), ODT // 128)),
                  pl.BlockSpec((Q, 128), lambda s: (cb(s), ODT // 128)),
                  _cst((8, 128))],
        out_specs=[pl.BlockSpec((Q, D), lambda s: (cf(s), 0)),
                   pl.BlockSpec((Q, D), lambda s: (cb(s), 0)), hspec, hspec],
        out_shape=[ysh, ysh, hsh, hsh],
        scratch_shapes=[pltpu.VMEM((NS, NH * HP), F32), pltpu.VMEM((NS, NH * HP), F32)],
        compiler_params=_params(("arbitrary",)),
    )(xbc, xbc, p, p, prm)


def _ssd2_bwd(xbc, p, prm, dsk, dyd, hpf, hpb):
    RT = xbc.shape[0]
    nc = RT // Q
    ncc = TL // Q
    ncl = nc - ncc
    cf, cb = _ssd_orders(ncl, ncc)

    def rs(t):
        return nc - 1 - t

    def one_dir(x_ref, dt_ref, prm_ref, dsk_ref, dy_ref, is_ctx, hp_ref, dHT_ref,
                dx_ref, ddt_ref, st_ref, d):
        rev = d == 1
        a32 = -jnp.exp(prm_ref[1:2, :])
        dtraw = dt_ref[...]
        dtb = prm_ref[0:1, :]
        dt, acum, acumT, _, _ = _ssd_common(dtraw, dtb, a32, rev)
        end = 0 if rev else Q - 1
        lane = lax.broadcasted_iota(jnp.int32, (Q, 128), 1)
        srow = lax.broadcasted_iota(jnp.int32, (Q, 128), 0)
        maskT = (lane <= srow) if rev else (lane >= srow)
        lo = lane < HP
        lo1 = lo[0:1, :]
        dyscale = jnp.where(is_ctx, 0.0, 1.0)
        c_dacum = jnp.zeros((Q, 128), F32)
        r_dacum = jnp.zeros((Q, 128), F32)
        c_ddt = jnp.zeros((Q, 128), F32)
        dskacc = jnp.zeros((1, 128), F32)
        for g in range(2):
            gs = slice(g * 512, (g + 1) * 512)
            Bg = x_ref[:, D + g * NS:D + (g + 1) * NS]
            Cg = x_ref[:, D + 2 * NS + g * NS:D + 2 * NS + (g + 1) * NS]
            CBT = _dot(Bg, Cg, NT)
            HTg = hp_ref[0, :, gs]
            dHTg = dHT_ref[:, gs]
            BdHg = _dot(Bg, dHTg, NN)
            dCBT = jnp.zeros((Q, Q), F32)
            dCg = jnp.zeros((Q, NS), F32)
            xds, dyes, svs = [], [], []
            for q in range(HPG // 2):
                pi = g * (HPG // 2) + q
                ps = slice(pi * 128, (pi + 1) * 128)
                qs = slice(q * 128, (q + 1) * 128)
                Xp = x_ref[:, ps]
                dYp = dy_ref[:, ps] * dyscale
                HTp = HTg[:, qs]
                BdHp = BdHg[:, qs]
                dY2 = _halves(dYp, lo, 0)
                dWT2 = _dot(_halves(Xp, lo, 0), dYp.T, NN)
                G2 = _dot(dY2, HTp, NT)
                XB = Xp * BdHp
                hh = _colsum(dHTg[:, qs] * HTp)
                yx = _colsum(dYp * Xp)
                wts, dcs, ebs, sv = [], [], [], []
                for k, h in enumerate((2 * pi, 2 * pi + 1)):
                    ln = 16 * d + h
                    half = lo if k == 0 else jnp.logical_not(lo)
                    half1 = half[0:1, :]
                    colB = _lane_bcast(acum, ln)
                    dtcB = _lane_bcast(dt, ln)
                    rowv = acumT[ln:ln + 1, :]
                    aend = colB[end:end + 1, :]
                    LmT = jnp.exp(jnp.where(maskT, rowv - colB, -1e30))
                    WT = CBT * LmT * dtcB
                    dWT = dWT2[k * Q:(k + 1) * Q, :]
                    U = dWT * LmT
                    MT = U * CBT
                    rM = jnp.sum(MT, axis=1, keepdims=True)
                    rT = _colsum(MT * dtcB)
                    dCBT = dCBT + U * dtcB
                    ecol = jnp.exp(aend - colB)
                    EB = jnp.exp(colB)
                    Gk = G2[k * Q:(k + 1) * Q, :]
                    dCg = dCg + EB * Gk
                    qcol = jnp.sum(EB * Gk * Cg, axis=1, keepdims=True)
                    xb = jnp.sum(jnp.where(half, XB, 0.0), axis=1, keepdims=True)
                    e1 = ecol[:, 0:1]
                    dt1 = dtcB[:, 0:1]
                    scol = e1 * dt1 * xb
                    sA = jnp.exp(aend)
                    eterm = sA[:, 0:1] * jnp.sum(jnp.where(half1, hh, 0.0), axis=1, keepdims=True) \
                        + _colsum(scol)
                    cvec = qcol - dt1 * rM - scol + jnp.where(srow[:, 0:1] == end, eterm, 0.0)
                    c_dacum = jnp.where(lane == ln, cvec, c_dacum)
                    r_dacum = jnp.where(srow == ln, rT, r_dacum)
                    c_ddt = jnp.where(lane == ln, rM + e1 * xb, c_ddt)
                    if d == 0:
                        dskacc = dskacc + jnp.where(
                            lane[0:1, :] == h, jnp.sum(jnp.where(half1, yx, 0.0), axis=1, keepdims=True), 0.0)
                    wts.append(WT)
                    dcs.append(ecol * dtcB)
                    ebs.append(EB)
                    sv.append(sA)
                dcp = jnp.where(lo, dcs[0], dcs[1])
                dX = _dot(jnp.concatenate(wts, axis=1), dY2, NN) + dcp * BdHp
                if d == 0:
                    dX = dX + dYp * dsk_ref[:, ps]
                dx_ref[:, ps] = dX
                xds.append(Xp * dcp)
                dyes.append(dYp * jnp.where(lo, ebs[0], ebs[1]))
                svs.append(jnp.where(lo1, sv[0], sv[1]))
            dx_ref[:, D + g * NS:D + (g + 1) * NS] = (
                _dot(jnp.concatenate(xds, axis=1), dHTg, NT) + _dot(dCBT, Cg, NN))
            dx_ref[:, D + 2 * NS + g * NS:D + 2 * NS + (g + 1) * NS] = dCg + _dot(dCBT, Bg, TN)
            dHT_ref[:, gs] = (jnp.concatenate(svs, axis=1) * dHTg
                              + _dot(Cg.T, jnp.concatenate(dyes, axis=1), NN))
        dacum = c_dacum + r_dacum.T
        da = _cumsum_rows(dacum, not rev)
        mine = (lane >= 16 * d) & (lane < 16 * d + 16)
        ddt = jnp.where(mine, c_ddt + da * a32, 0.0)
        ddt_ref[...] = ddt * _sig(dtraw + dtb)
        st_ref[0:1, :] += _colsum(jnp.where(mine, da * dt, 0.0))
        if d == 0:
            st_ref[1:2, :] += dskacc

    def body(xf_ref, xb_ref, df_ref, db_ref, prm_ref, dsk_ref, dyf_ref, dyb_ref, hf_ref, hb_ref,
             dxf_ref, dxb_ref, ddf_ref, ddb_ref, st_ref, dHf, dHb):
        t = pl.program_id(0)

        @pl.when(t == 0)
        def _():
            dHf[...] = jnp.zeros_like(dHf)
            dHb[...] = jnp.zeros_like(dHb)
            st_ref[...] = jnp.zeros_like(st_ref)

        s = rs(t)
        one_dir(xf_ref, df_ref, prm_ref, dsk_ref, dyf_ref, cf(s) >= ncl, hf_ref, dHf,
                dxf_ref, ddf_ref, st_ref, 0)
        one_dir(xb_ref, db_ref, prm_ref, dsk_ref, dyb_ref, cb(s) >= ncl, hb_ref, dHb,
                dxb_ref, ddb_ref, st_ref, 1)

        @pl.when(t == nc - 1)
        def _():
            st_ref[0:1, :] = -jnp.exp(prm_ref[1:2, :]) * st_ref[0:1, :]

    def lat(c):
        return jnp.minimum(c, ncl - 1)

    xsh = jax.ShapeDtypeStruct((RT, 1536), F32)
    dsh = jax.ShapeDtypeStruct((RT, 128), F32)
    hspec = pl.BlockSpec((1, NS, NH * HP), lambda t: (rs(t), 0, 0))
    return pl.pallas_call(
        body, name="ssd_bwd", grid=(nc,),
        in_specs=[pl.BlockSpec((Q, 1536), lambda t: (cf(rs(t)), 0)),
                  pl.BlockSpec((Q, 1536), lambda t: (cb(rs(t)), 0)),
                  pl.BlockSpec((Q, 128), lambda t: (cf(rs(t)), ODT // 128)),
                  pl.BlockSpec((Q, 128), lambda t: (cb(rs(t)), ODT // 128)),
                  _cst((8, 128)), _cst((1, D)),
                  pl.BlockSpec((Q, D), lambda t: (lat(cf(rs(t))), 0)),
                  pl.BlockSpec((Q, D), lambda t: (lat(cb(rs(t))), 0)),
                  hspec, hspec],
        out_specs=[pl.BlockSpec((Q, 1536), lambda t: (cf(rs(t)), 0)),
                   pl.BlockSpec((Q, 1536), lambda t: (cb(rs(t)), 0)),
                   pl.BlockSpec((Q, 128), lambda t: (cf(rs(t)), 0)),
                   pl.BlockSpec((Q, 128), lambda t: (cb(rs(t)), 0)),
                   _cst((8, 128))],
        out_shape=[xsh, xsh, dsh, dsh, jax.ShapeDtypeStruct((8, 128), F32)],
        scratch_shapes=[pltpu.VMEM((NS, NH * HP), F32), pltpu.VMEM((NS, NH * HP), F32)],
        compiler_params=_params(("arbitrary",)),
    )(xbc, xbc, p, p, prm, dsk, dyd, dyd, hpf, hpb)


def _mix_fwd_vals(yf, yb, z, xs, u, v, dsk, sg, gg, gb):
    y = yf + yb + xs * dsk
    sz = _sig(z)
    hh = y * z * sz
    r = lax.rsqrt(jnp.mean(hh * hh, axis=-1, keepdims=True) + EPS)
    nh = hh * r
    ug, tu = _gelu(u)
    vg, tv = _gelu(v)
    vhat, vrstd = _ln(vg)
    vn = vhat * gg + gb
    return y, sz, r, nh, ug, tu, vg, tv, vhat, vrstd, vn


def _mix_fwd(yf, yb, p, xbc, dsk, sg, gg, gb, ws, bsT):
    L = yf.shape[0] - TL
    nt = L // TL

    def body(yf_ref, yb_ref, z_ref, xs_ref, u_ref, v_ref, dsk_ref, sg_ref, gg_ref, gb_ref,
             ws_ref, bs_ref, ys_ref, ym_ref):
        _, _, _, nh, ug, _, _, _, _, _, vn = _mix_fwd_vals(
            yf_ref[...], yb_ref[...], z_ref[...], xs_ref[...], u_ref[...], v_ref[...],
            dsk_ref[...], sg_ref[...], gg_ref[...], gb_ref[...])
        ys_ref[...] = (nh * sg_ref[...]).astype(ys_ref.dtype)
        for n in range(TL // Q):
            rs_ = slice(n * Q, (n + 1) * Q)
            for g in range(8):
                cs = slice(g * 128, (g + 1) * 128)
                mixed = _dot(ws_ref[g], vn[rs_, cs], NN) + bs_ref[:, g:g + 1]
                ym_ref[rs_, cs] = (ug[rs_, cs] * mixed).astype(ym_ref.dtype)

    return pl.pallas_call(
        body, name="mix_fwd", grid=(nt,),
        in_specs=[_rt(D), _rt(D), _rt(D, OZ // D), _rt(D, 0), _rt(D, OU // D), _rt(D, OV // D),
                  _cst((1, D)), _cst((1, D)), _cst((1, D)), _cst((1, D)),
                  _cst((8, 128, 128)), _cst((128, 128))],
        out_specs=[_rt(D), _rt(D)],
        out_shape=[jax.ShapeDtypeStruct((L, D), _MXU), jax.ShapeDtypeStruct((L, D), _MXU)],
        compiler_params=_params(("parallel",)),
    )(yf, yb, p, xbc, p, p, dsk, sg, gg, gb, ws, bsT)


def _mix_bwd(dys, dym, yf, yb, p, xbc, dp, dsk, sg, gg, gb, ws, bsT):
    L = dys.shape[0]
    nt = L // TL

    def body(dys_ref, dym_ref, yf_ref, yb_ref, z_ref, xs_ref, u_ref, v_ref, dsk_ref, sg_ref,
             gg_ref, gb_ref, ws_ref, bs_ref, dp_any, dzuv_ref, dy_ref, st_ref,
             dws_ref, dbs_ref, dvn_s):
        del dp_any
        dz_ref = dzuv_ref.at[:, OZ:OZ + D]
        du_ref = dzuv_ref.at[:, OU:OU + D]
        dv_ref = dzuv_ref.at[:, OV:OV + D]

        @pl.when(pl.program_id(0) == 0)
        def _():
            st_ref[...] = jnp.zeros_like(st_ref)
            dws_ref[...] = jnp.zeros_like(dws_ref)
            dbs_ref[...] = jnp.zeros_like(dbs_ref)

        z = z_ref[...]
        u = u_ref[...]
        v = v_ref[...]
        y, sz, r, nh, ug, tu, vg, tv, vhat, vrstd, vn = _mix_fwd_vals(
            yf_ref[...], yb_ref[...], z, xs_ref[...], u, v,
            dsk_ref[...], sg_ref[...], gg_ref[...], gb_ref[...])
        dys = dys_ref[...]
        st_ref[0:1, :] += _colsum(dys * nh)
        dn = dys * sg_ref[...]
        dhh = r * (dn - nh * jnp.mean(dn * nh, axis=-1, keepdims=True))
        dy_ref[...] = dhh * z * sz
        dz_ref[...] = (dhh * y * (sz * (1.0 + z * (1.0 - sz)))).astype(dz_ref.dtype)
        dym = dym_ref[...]
        lane = lax.broadcasted_iota(jnp.int32, (Q, 128), 1)
        dbs = jnp.zeros((Q, 128), F32)
        gu = _gelu_grad(u, tu)
        for n in range(TL // Q):
            rs_ = slice(n * Q, (n + 1) * Q)
            for g in range(8):
                cs = slice(g * 128, (g + 1) * 128)
                vb = vn[rs_, cs]
                mixed = _dot(ws_ref[g], vb, NN) + bs_ref[:, g:g + 1]
                dyb = dym[rs_, cs]
                dmx = dyb * ug[rs_, cs]
                du_ref[rs_, cs] = (dyb * mixed * gu[rs_, cs]).astype(du_ref.dtype)
                dvn_s[rs_, cs] = _dot(ws_ref[g], dmx, TN)
                dws_ref[g] += _dot(dmx, vb, NT)
                dbs = dbs + jnp.where(lane == g, jnp.sum(dmx, axis=1, keepdims=True), 0.0)
        dbs_ref[...] += dbs
        dvn = dvn_s[...]
        st_ref[1:2, :] += _colsum(dvn * vhat)
        st_ref[2:3, :] += _colsum(dvn)
        dvg = _ln_bwd(dvn * gg_ref[...], vhat, vrstd)
        dv_ref[...] = (dvg * _gelu_grad(v, tv)).astype(dv_ref.dtype)

    outs = pl.pallas_call(
        body, name="mix_bwd", grid=(nt,),
        in_specs=[_rt(D), _rt(D), _rt(D), _rt(D), _rt(D, OZ // D), _rt(D, 0), _rt(D, OU // D),
                  _rt(D, OV // D), _cst((1, D)), _cst((1, D)), _cst((1, D)), _cst((1, D)),
                  _cst((8, 128, 128)), _cst((128, 128)), pl.BlockSpec(memory_space=pl.ANY)],
        out_specs=[_rt(3 * D, 0), _rt(D), _cst((8, D)),
                   _cst((8, 128, 128)), _cst((128, 128))],
        out_shape=[jax.ShapeDtypeStruct(dp.shape, dp.dtype),
                   jax.ShapeDtypeStruct((L, D), F32), jax.ShapeDtypeStruct((8, D), F32),
                   jax.ShapeDtypeStruct((8, 128, 128), F32), jax.ShapeDtypeStruct((128, 128), F32)],
        scratch_shapes=[pltpu.VMEM((TL, D), F32)],
        input_output_aliases={14: 0},
        compiler_params=_params(("arbitrary",)),
    )(dys, dym, yf, yb, p, xbc, p, p, dsk, sg, gg, gb, ws, bsT, dp)
    return outs


def _merge_fwd(yssd, ygm, p, bg, ws, wg, wo, xn, modx, g1, b1):
    L = yssd.shape[0]
    tm = TL

    def body(ys_ref, yg_ref, g_ref, bg_ref, ws_ref, wg_ref, wo_ref, xn_ref, mx_ref, g1_ref, b1_ref,
             a1_ref, a2_ref, m_ref, o_ref, r1_ref, h2_ref):
        a1 = _dot(ys_ref[...], ws_ref[...], NN)
        a2 = _dot(yg_ref[...], wg_ref[...], NN)
        gt = _sig(g_ref[...] + bg_ref[...])
        mg = gt[:, :D] * a1 + gt[:, D:] * a2
        a1_ref[...] = a1
        a2_ref[...] = a2
        m_ref[...] = mg.astype(m_ref.dtype)
        out = _dot(mg, wo_ref[...], NN)
        o_ref[...] = out
        r1 = ALPHA * xn_ref[...] + mx_ref[2:3, :] * out
        xhat, _ = _ln(r1)
        x1 = xhat * g1_ref[...] + b1_ref[...]
        r1_ref[...] = r1
        h2_ref[...] = (x1 * (1.0 + mx_ref[4:5, :]) + mx_ref[3:4, :]).astype(h2_ref.dtype)

    rows = pl.BlockSpec((tm, D), lambda i: (i, 0))
    f32s = jax.ShapeDtypeStruct((L, D), F32)
    mxus = jax.ShapeDtypeStruct((L, D), _MXU)
    return pl.pallas_call(
        body, name="merge_fwd", grid=(L // tm,),
        in_specs=[rows, rows, pl.BlockSpec((tm, 2 * D), lambda i: (i, OG // (2 * D))), _cst((1, 2 * D)),
                  _cst((D, D)), _cst((D, D)), _cst((D, D)), rows, _cst((8, D)), _cst((1, D)), _cst((1, D))],
        out_specs=[rows] * 6,
        out_shape=[f32s, f32s, mxus, f32s, f32s, mxus],
        compiler_params=_params(("parallel",)),
    )(yssd, ygm, p, bg, ws, wg, wo, xn, modx, g1, b1)


def _mm_o2_res2(ff, w2, r1, tgt, modx, g1, b1, g2, b2):
    L, K = ff.shape
    tm = 2 * TL

    def body(a_ref, b_ref, r1_ref, t_ref, mx_ref, g1_ref, b1_ref, g2_ref, b2_ref,
             dr2_ref, do2_ref, st_ref, loss_ref):
        @pl.when(pl.program_id(0) == 0)
        def _():
            st_ref[...] = jnp.zeros_like(st_ref)
            loss_ref[...] = jnp.zeros_like(loss_ref)

        o2 = _dot(a_ref[...], b_ref[...], NN)
        xh1, _ = _ln(r1_ref[...])
        x1 = xh1 * g1_ref[...] + b1_ref[...]
        g2x = mx_ref[5:6, :]
        xh2, rstd2 = _ln(ALPHA * x1 + g2x * o2)
        err = xh2 * g2_ref[...] + b2_ref[...] - t_ref[...]
        per_tok = jnp.mean(err * err, axis=-1, keepdims=True)
        loss_ref[...] += 0.5 * jnp.sum(per_tok, axis=0, keepdims=True)
        dy = err * (1.0 / D)
        st_ref[0:1, :] += _colsum(dy * xh2)
        st_ref[1:2, :] += _colsum(dy)
        dr2 = _ln_bwd(dy * g2_ref[...], xh2, rstd2)
        st_ref[2:3, :] += _colsum(dr2 * o2)
        dr2_ref[...] = dr2
        do2_ref[...] = (g2x * dr2).astype(do2_ref.dtype)

    return pl.pallas_call(
        body, name="mm_o2_res2", grid=(L // tm,),
        in_specs=[_rt(K, rows=tm), _cst((K, D)), _rt(D, rows=tm), _rt(D, rows=tm), _cst((8, D))]
        + [_cst((1, D))] * 4,
        out_specs=[_rt(D, rows=tm), _rt(D, rows=tm), _cst((8, D)), _cst((8, 128))],
        out_shape=[jax.ShapeDtypeStruct((L, D), F32), jax.ShapeDtypeStruct((L, D), _MXU),
                   jax.ShapeDtypeStruct((8, D), F32), jax.ShapeDtypeStruct((8, 128), F32)],
        compiler_params=_params(("arbitrary",)),
    )(ff, w2, r1, tgt, modx, g1, b1, g2, b2)


def _mm_dh2_res1bwd(df13, w13i, dr2, r1, out, modx, g1, b1):
    L, K = df13.shape

    def body(a_ref, b_ref, dr2_ref, r1_ref, o_ref, mx_ref, g_ref, bb_ref, dr1_ref, do_ref, st_ref):
        @pl.when(pl.program_id(0) == 0)
        def _():
            st_ref[...] = jnp.zeros_like(st_ref)

        dh2 = _dot(a_ref[...], b_ref[...], NN)
        xh1, rstd1 = _ln(r1_ref[...])
        x1 = xh1 * g_ref[...] + bb_ref[...]
        dx1 = ALPHA * dr2_ref[...] + dh2 * (1.0 + mx_ref[4:5, :])
        st_ref[0:1, :] += _colsum(dh2 * x1)
        st_ref[1:2, :] += _colsum(dh2)
        st_ref[2:3, :] += _colsum(dx1 * xh1)
        st_ref[3:4, :] += _colsum(dx1)
        dr1 = _ln_bwd(dx1 * g_ref[...], xh1, rstd1)
        st_ref[4:5, :] += _colsum(dr1 * o_ref[...])
        dr1_ref[...] = dr1
        do_ref[...] = (mx_ref[2:3, :] * dr1).astype(do_ref.dtype)

    return pl.pallas_call(
        body, name="mm_dh2_res1bwd", grid=(L // TL,),
        in_specs=[_rt(K), _cst((K, D)), _rt(D), _rt(D), _rt(D), _cst((8, D)), _cst((1, D)), _cst((1, D))],
        out_specs=[_rt(D), _rt(D), _cst((8, D))],
        out_shape=[jax.ShapeDtypeStruct((L, D), F32), jax.ShapeDtypeStruct((L, D), _MXU),
                   jax.ShapeDtypeStruct((8, D), F32)],
        compiler_params=_params(("arbitrary",)),
    )(df13, w13i, dr2, r1, out, modx, g1, b1)


def _merge_bwd(dout, a1, a2, p, bg, wo, ws, wg, dp):
    L = a1.shape[0]
    tm = TL

    def body(do_ref, a1_ref, a2_ref, g_ref, bg_ref, wo_ref, ws_ref, wg_ref, dp_any,
             dg_ref, da1_ref, da2_ref, st_ref, dys_ref, dym_ref):
        del dp_any

        @pl.when(pl.program_id(0) == 0)
        def _():
            st_ref[...] = jnp.zeros_like(st_ref)

        dm = _dot(do_ref[...], wo_ref[...], NT)
        gt = _sig(g_ref[...] + bg_ref[...])
        g1 = gt[:, :D]
        g2 = gt[:, D:]
        da1 = (dm * g1).astype(da1_ref.dtype)
        da2 = (dm * g2).astype(da2_ref.dtype)
        da1_ref[...] = da1
        da2_ref[...] = da2
        dg1 = dm * a1_ref[...] * g1 * (1.0 - g1)
        dg2 = dm * a2_ref[...] * g2 * (1.0 - g2)
        st_ref[0:1, 0:D] += _colsum(dg1)
        st_ref[0:1, D:2 * D] += _colsum(dg2)
        dg_ref[:, 0:D] = dg1.astype(dg_ref.dtype)
        dg_ref[:, D:2 * D] = dg2.astype(dg_ref.dtype)
        dys_ref[...] = _dot(da1, ws_ref[...], NT)
        dym_ref[...] = _dot(da2, wg_ref[...], NT)

    rows = pl.BlockSpec((tm, D), lambda i: (i, 0))
    gates = pl.BlockSpec((tm, 2 * D), lambda i: (i, OG // (2 * D)))
    f32s = jax.ShapeDtypeStruct((L, D), F32)
    mxus = jax.ShapeDtypeStruct((L, D), _MXU)
    return pl.pallas_call(
        body, name="merge_bwd", grid=(L // tm,),
        in_specs=[rows, rows, rows, gates, _cst((1, 2 * D)), _cst((D, D)), _cst((D, D)), _cst((D, D)),
                  pl.BlockSpec(memory_space=pl.ANY)],
        out_specs=[gates, rows, rows, _cst((8, 2 * D)), rows, rows],
        out_shape=[jax.ShapeDtypeStruct(dp.shape, dp.dtype), mxus, mxus,
                   jax.ShapeDtypeStruct((8, 2 * D), F32), f32s, f32s],
        input_output_aliases={8: 0},
        compiler_params=_params(("arbitrary",)),
    )(dout, a1, a2, p, bg, wo, ws, wg, dp)


HFF = DFF // 2


def _mm_f13_glu(h2, w13i):
    L = h2.shape[0]
    tm = 512

    def body(a_ref, b_ref, f_ref, ff_ref):
        f = _dot(a_ref[...], b_ref[...], NT)
        f_ref[...] = f
        f1 = f[:, :HFF]
        ff_ref[...] = (f1 * _sig(f1) * f[:, HFF:]).astype(ff_ref.dtype)

    return pl.pallas_call(
        body, name="mm_f13_glu", grid=(DFF // HFF, L // tm),
        in_specs=[pl.BlockSpec((tm, D), lambda j, i: (i, 0)), pl.BlockSpec((2 * HFF, D), lambda j, i: (j, 0))],
        out_specs=[pl.BlockSpec((tm, 2 * HFF), lambda j, i: (i, j)), pl.BlockSpec((tm, HFF), lambda j, i: (i, j))],
        out_shape=[jax.ShapeDtypeStruct((L, 2 * DFF), F32), jax.ShapeDtypeStruct((L, DFF), _MXU)],
        compiler_params=_params(("parallel", "parallel")),
    )(h2, w13i)


def _mm_dff_glu(do2, w_ff2_f, f13i):
    L = do2.shape[0]
    tm = 512

    def body(a_ref, b_ref, f_ref, o_ref):
        d = _dot(a_ref[...], b_ref[...], NT)
        f1 = f_ref[:, :HFF]
        s = _sig(f1)
        o_ref[:, :HFF] = (d * f_ref[:, HFF:] * (s * (1.0 + f1 * (1.0 - s)))).astype(o_ref.dtype)
        o_ref[:, HFF:] = (d * f1 * s).astype(o_ref.dtype)

    return pl.pallas_call(
        body, name="mm_dff_glu", grid=(DFF // HFF, L // tm),
        in_specs=[pl.BlockSpec((tm, D), lambda j, i: (i, 0)), pl.BlockSpec((HFF, D), lambda j, i: (j, 0)),
                  pl.BlockSpec((tm, 2 * HFF), lambda j, i: (i, j))],
        out_specs=pl.BlockSpec((tm, 2 * HFF), lambda j, i: (i, j)),
        out_shape=jax.ShapeDtypeStruct((L, 2 * DFF), _MXU),
        compiler_params=_params(("parallel", "parallel")),
    )(do2, w_ff2_f, f13i)


def _conv_bwd(dxf, dxb, p, conv_w8, conv_b, dp):
    RT = p.shape[0]
    chunks = _seq_chunks(RT - TL)

    def body(df_ref, db_ref, p_ref, w_ref, b_ref, dp_any, o_ref, dw_ref, dbias_ref, dpre_s):
        del dp_any
        w = w_ref[...]
        bias = b_ref[...]
        srow = lax.broadcasted_iota(jnp.int32, (8, 128), 0)
        dwacc = jnp.zeros((8, 128), F32)
        dbacc = jnp.zeros((1, 128), F32)
        for r0, first, last in chunks:
            taps = _conv_taps(p_ref, r0, first, last)
            pre = bias + sum(w[k:k + 1, :] * taps[k] for k in range(5))
            s = _sig(pre)
            dpre = (df_ref[pl.ds(r0, TL), :] + db_ref[pl.ds(r0, TL), :]) * (s * (1.0 + pre * (1.0 - s)))
            dpre_s[pl.ds(r0, TL), :] = dpre
            dbacc = dbacc + _colsum(dpre)
            for k in range(5):
                dwacc = dwacc + jnp.where(srow == k, _colsum(dpre * taps[k]), 0.0)
        for r0, first, last in chunks:
            taps = _conv_taps(dpre_s, r0, first, last)
            dx = sum(w[k:k + 1, :] * taps[4 - k] for k in range(5))
            o_ref[pl.ds(r0, TL), :] = dx.astype(o_ref.dtype)
        dw_ref[...] = dwacc
        dbias_ref[...] = jnp.broadcast_to(dbacc, (8, 128))

    cspec = pl.BlockSpec((RT, 128), lambda j: (0, j))
    wspec = pl.BlockSpec((8, 128), lambda j: (0, j))
    return pl.pallas_call(
        body, name="conv_bwd", grid=(12,),
        in_specs=[cspec, cspec, pl.BlockSpec((RT, 128), lambda j: (0, _xbc_colblk(j))),
                  wspec, pl.BlockSpec((1, 128), lambda j: (0, j)), pl.BlockSpec(memory_space=pl.ANY)],
        out_specs=[pl.BlockSpec((RT, 128), lambda j: (0, _xbc_colblk(j))), wspec, wspec],
        out_shape=[jax.ShapeDtypeStruct(dp.shape, dp.dtype), jax.ShapeDtypeStruct((8, 1536), F32),
                   jax.ShapeDtypeStruct((8, 1536), F32)],
        scratch_shapes=[pltpu.VMEM((RT, 128), F32)],
        input_output_aliases={5: 0},
        compiler_params=_params(("parallel",)),
    )(dxf, dxb, p, conv_w8, conv_b, dp)


def _dp_fill(dp):
    nrow = dp.shape[0] // TL

    def body(dp_any, o_ref):
        del dp_any
        o_ref[...] = jnp.zeros_like(o_ref)

    return pl.pallas_call(
        body, name="dp_fill", grid=(1,),
        in_specs=[pl.BlockSpec(memory_space=pl.ANY)],
        out_specs=pl.BlockSpec((TL, OB), lambda i: (nrow - 1, 0)),
        out_shape=jax.ShapeDtypeStruct(dp.shape, dp.dtype),
        input_output_aliases={0: 0},
        compiler_params=_params(("arbitrary",)),
    )(dp)


def _dt_bwd(ddf, ddb, dp):
    RT = ddf.shape[0]
    assert NPJ - ODT == 256

    def body(f_ref, b_ref, dp_any, o_ref, st_ref):
        del dp_any

        @pl.when(pl.program_id(0) == 0)
        def _():
            st_ref[...] = jnp.zeros_like(st_ref)

        s = f_ref[...] + b_ref[...]
        o_ref[:, 0:128] = s.astype(o_ref.dtype)
        o_ref[:, 128:256] = jnp.zeros((TL, 128), o_ref.dtype)
        st_ref[0:1, :] += _colsum(s)

    return pl.pallas_call(
        body, name="dt_bwd", grid=(RT // TL,),
        in_specs=[_rt(128), _rt(128), pl.BlockSpec(memory_space=pl.ANY)],
        out_specs=[_rt(256, ODT // 256), _cst((8, 128))],
        out_shape=[jax.ShapeDtypeStruct(dp.shape, dp.dtype), jax.ShapeDtypeStruct((8, 128), F32)],
        input_output_aliases={2: 0},
        compiler_params=_params(("arbitrary",)),
    )(ddf, ddb, dp)


def _mm_dh1_ln0bwd(dp, w_in_p, dr1, x, ctx, g, b, modx, modc):
    L = x.shape[0]
    nt = L // TL

    def body(dp_ref, w_ref, dr1_ref, x_ref, c_ref, g_ref, b_ref, mx_ref, mc_ref, gx_ref, st_ref):
        i = pl.program_id(0)
        isc = i == nt

        @pl.when(i == 0)
        def _():
            st_ref[...] = jnp.zeros_like(st_ref)

        xin = jnp.where(isc, c_ref[...], x_ref[...])
        xhat, rstd = _ln(xin)
        xn = xhat * g_ref[...] + b_ref[...]
        sc = jnp.where(isc, mc_ref[1:2, :], mx_ref[1:2, :])
        dh = _dot(dp_ref[...], w_ref[...], NT)
        lat = jnp.where(isc, 0.0, 1.0)
        dxn = dh * (1.0 + sc) + (lat * ALPHA) * dr1_ref[...]
        tsh = _colsum(dh)
        tsc = _colsum(dh * xn)
        st_ref[0:1, :] += lat * tsh
        st_ref[1:2, :] += lat * tsc
        st_ref[2:3, :] += (1.0 - lat) * tsh
        st_ref[3:4, :] += (1.0 - lat) * tsc
        st_ref[4:5, :] += _colsum(dxn * xhat)
        st_ref[5:6, :] += _colsum(dxn)

        @pl.when(i < nt)
        def _():
            gx_ref[...] = _ln_bwd(dxn * g_ref[...], xhat, rstd)

    return pl.pallas_call(
        body, name="mm_dh1_ln0bwd", grid=(nt + 1,),
        in_specs=[_rt(NPJ), _cst((D, NPJ)), _rtc(D, nt), _rtc(D, nt), _cst((TL, D)), _cst((1, D)), _cst((1, D)),
                  _cst((8, D)), _cst((8, D))],
        out_specs=[_rtc(D, nt), _cst((8, D))],
        out_shape=[jax.ShapeDtypeStruct((L, D), F32), jax.ShapeDtypeStruct((8, D), F32)],
        compiler_params=pltpu.CompilerParams(dimension_semantics=("arbitrary",),
                                             vmem_limit_bytes=VMEM_LIMIT + 8 * 1024 * 1024),
    )(dp, w_in_p, dr1, x, ctx, g, b, modx, modc)


SECTIONS = ((0, 1024, OZ), (1024, 2048, OXS), (2048, 2304, OB), (2304, 2560, OC), (2560, 2592, ODT),
            (2592, 3616, OU), (3616, 4640, OV), (4640, 6688, OG))


def _perm_from_blocks(ga):
    n = ga.shape[2]
    pieces = []
    for na, nb, _ in sorted(SECTIONS, key=lambda sec: sec[2]):
        for k in range(NDEV):
            lo, hi = max(na, k * n), min(nb, (k + 1) * n)
            if lo < hi:
                pieces.append(ga[k][:, lo - k * n:hi - k * n])
    pieces.append(jnp.zeros((ga.shape[1], NPJ - NNAT), ga.dtype))
    return jnp.concatenate(pieces, axis=1)


def _blocks_from_perm(gp, n):
    blocks = []
    for k in range(NDEV):
        pieces = []
        for na, nb, po in SECTIONS:
            lo, hi = max(na, k * n), min(nb, (k + 1) * n)
            if lo < hi:
                pieces.append(gp[:, po + lo - na:po + hi - na])
        blocks.append(jnp.concatenate(pieces, axis=1))
    return jnp.stack(blocks)


def _padded(n, row_align):
    unit = row_align * D
    return -(-n // unit) * unit if row_align else n


def _slab(arrs, rows, row_align=0):
    parts = []
    for a in arrs:
        f = a.reshape(-1)
        parts.append(jnp.pad(f, (0, _padded(f.shape[0], row_align) - f.shape[0])))
    flat = jnp.concatenate(parts)
    flat = jnp.pad(flat, (0, rows * D - flat.shape[0]))
    return flat.reshape(rows, D)


def _unslab(slab, shapes, row_align=0):
    out, off = [], 0
    for shp in shapes:
        n = 1
        for s in shp:
            n *= s
        r0, r1 = off // D, -(-(off + n) // D)
        out.append(slab[r0:r1].reshape(-1)[off - r0 * D:off - r0 * D + n].reshape(shp))
        off += _padded(n, row_align)
    return out


def _row(v):
    return v.reshape(1, -1)


def _t(a):
    return jnp.swapaxes(a, 0, 1)


def _pad_rows(a, rows):
    return jnp.pad(a, ((0, rows - a.shape[0]), (0, 0)))


REPL = ["c_ctx", "ln0_g", "ln0_b", "b_ada", "conv_b", "dt_bias", "a_log", "d_skip", "ssd_norm_g",
        "gm_norm_g", "gm_norm_b", "w_spatial", "b_spatial", "b_gate", "ln1_g", "ln1_b", "ln2_g", "ln2_b"]
SMALL_ROWS = 160
WEIGHTS = ["c_ctx", "ln0_g", "ln0_b", "w_ada", "b_ada", "w_in", "conv_w", "conv_b", "dt_bias", "a_log",
           "d_skip", "ssd_norm_g", "gm_norm_g", "gm_norm_b", "w_spatial", "b_spatial", "b_gate",
           "w_ssd_proj", "w_gm_proj", "w_out", "ln1_g", "ln1_b", "w_ff1", "w_ff3", "w_ff2", "ln2_g", "ln2_b"]


def kernel(x, c, ctx, c_ctx, ln0_g, ln0_b, w_ada, b_ada, w_in, conv_w, conv_b, dt_bias, a_log, d_skip, ssd_norm_g, gm_norm_g, gm_norm_b, w_spatial, b_spatial, b_gate, w_ssd_proj, w_gm_proj, w_out, ln1_g, ln1_b, w_ff1, w_ff3, w_ff2, ln2_g, ln2_b, loss_target, m_c_ctx, m_ln0_g, m_ln0_b, m_w_ada, m_b_ada, m_w_in, m_conv_w, m_conv_b, m_dt_bias, m_a_log, m_d_skip, m_ssd_norm_g, m_gm_norm_g, m_gm_norm_b, m_w_spatial, m_b_spatial, m_b_gate, m_w_ssd_proj, m_w_gm_proj, m_w_out, m_ln1_g, m_ln1_b, m_w_ff1, m_w_ff3, m_w_ff2, m_ln2_g, m_ln2_b, v_c_ctx, v_ln0_g, v_ln0_b, v_w_ada, v_b_ada, v_w_in, v_conv_w, v_conv_b, v_dt_bias, v_a_log, v_d_skip, v_ssd_norm_g, v_gm_norm_g, v_gm_norm_b, v_w_spatial, v_b_spatial, v_b_gate, v_w_ssd_proj, v_w_gm_proj, v_w_out, v_ln1_g, v_ln1_b, v_w_ff1, v_w_ff3, v_w_ff2, v_ln2_g, v_ln2_b):
    W = dict(c_ctx=c_ctx, ln0_g=ln0_g, ln0_b=ln0_b, w_ada=w_ada, b_ada=b_ada, w_in=w_in, conv_w=conv_w,
             conv_b=conv_b, dt_bias=dt_bias, a_log=a_log, d_skip=d_skip, ssd_norm_g=ssd_norm_g,
             gm_norm_g=gm_norm_g, gm_norm_b=gm_norm_b, w_spatial=w_spatial, b_spatial=b_spatial,
             b_gate=b_gate, w_ssd_proj=w_ssd_proj, w_gm_proj=w_gm_proj, w_out=w_out, ln1_g=ln1_g,
             ln1_b=ln1_b, w_ff1=w_ff1, w_ff3=w_ff3, w_ff2=w_ff2, ln2_g=ln2_g, ln2_b=ln2_b)
    M = dict(c_ctx=m_c_ctx, ln0_g=m_ln0_g, ln0_b=m_ln0_b, w_ada=m_w_ada, b_ada=m_b_ada, w_in=m_w_in,
             conv_w=m_conv_w, conv_b=m_conv_b, dt_bias=m_dt_bias, a_log=m_a_log, d_skip=m_d_skip,
             ssd_norm_g=m_ssd_norm_g, gm_norm_g=m_gm_norm_g, gm_norm_b=m_gm_norm_b,
             w_spatial=m_w_spatial, b_spatial=m_b_spatial, b_gate=m_b_gate, w_ssd_proj=m_w_ssd_proj,
             w_gm_proj=m_w_gm_proj, w_out=m_w_out, ln1_g=m_ln1_g, ln1_b=m_ln1_b, w_ff1=m_w_ff1,
             w_ff3=m_w_ff3, w_ff2=m_w_ff2, ln2_g=m_ln2_g, ln2_b=m_ln2_b)
    V = dict(c_ctx=v_c_ctx, ln0_g=v_ln0_g, ln0_b=v_ln0_b, w_ada=v_w_ada, b_ada=v_b_ada, w_in=v_w_in,
             conv_w=v_conv_w, conv_b=v_conv_b, dt_bias=v_dt_bias, a_log=v_a_log, d_skip=v_d_skip,
             ssd_norm_g=v_ssd_norm_g, gm_norm_g=v_gm_norm_g, gm_norm_b=v_gm_norm_b,
             w_spatial=v_w_spatial, b_spatial=v_b_spatial, b_gate=v_b_gate, w_ssd_proj=v_w_ssd_proj,
             w_gm_proj=v_w_gm_proj, w_out=v_w_out, ln1_g=v_ln1_g, ln1_b=v_ln1_b, w_ff1=v_w_ff1,
             w_ff3=v_w_ff3, w_ff2=v_w_ff2, ln2_g=v_ln2_g, ln2_b=v_ln2_b)

    me = 4 * lax.axis_index("x") + 2 * lax.axis_index("y") + lax.axis_index("c")
    xl, cx, tgt = x[0], ctx[0], loss_target[0]
    L = xl.shape[0]
    assert cx.shape[0] == TL and L % TL == 0
    ada_n = w_ada.shape[2]
    cw_n = conv_w.shape[2]

    small1 = _pad_rows(jnp.concatenate([c, _slab([conv_w[0]], 1)], axis=0), 8)
    g1 = _all_gather(small1, "ag_small")
    c_all = g1[:, 0, :]
    conv_w_full = g1[:, 1, :5 * cw_n].reshape(NDEV, 5, cw_n).transpose(1, 0, 2).reshape(5, NDEV * cw_n)
    sq = w_ssd_proj.shape[1]
    ffr = w_ff2.shape[1]
    ffc = w_ff1.shape[2]
    late = [jnp.concatenate([w_ssd_proj[0], w_gm_proj[0], w_out[0], w_ff2[0]], axis=0).astype(_MXU),
            _t(w_ff1[0]).astype(_MXU), _t(w_ff3[0]).astype(_MXU)]

    c16 = _pad_rows(jnp.concatenate([c_all, _row(c_ctx)], axis=0), 16)
    b_ada_sh = lax.dynamic_slice(b_ada, (0, ada_n * me), (1, ada_n))
    modp = _ada_fwd(c16, w_ada[0], b_ada_sh)
    mod16 = _all_gather(modp, "ag_mod").transpose(1, 0, 2).reshape(16, NDEV * ada_n)

    ga, = _all_gather_multi([w_in[0].astype(_MXU)], "ag_w_in")
    ga, late, mod16 = lax.optimization_barrier((ga, late, mod16))
    lw_send, lw_recv, lw_src, lw_land, lw_token = _exchange_start(late, "ag_late_start", gather=True)
    w_in_p = _perm_from_blocks(ga)
    modx = _pad_rows(lax.dynamic_slice(mod16, (me, 0), (1, 6 * D)).reshape(6, D), 8) + lw_token[0, 0]
    modc = _pad_rows(mod16[8].reshape(6, D), 8)

    g0, b0 = _row(ln0_g), _row(ln0_b)
    xn, h1 = _ln0_fwd(xl, cx, g0, b0, modx, modc)
    p = _mm(h1, w_in_p, "nn", F32, "mm_p", tm_cap=(L + TL) // 2)
    conv_w8 = _pad_rows(conv_w_full, 8)
    xbc = _conv_fwd(p, conv_w8, conv_b)
    prm = _pad_rows(jnp.pad(jnp.stack([dt_bias.reshape(32), a_log.reshape(32)]), ((0, 0), (0, 96))), 8)
    yf, yb, hpf, hpb = _ssd2_fwd(xbc, p, prm)
    lw_land = _exchange_wait(lw_send, lw_recv, lw_src, lw_land, yf, "ag_late_wait", gather=True)
    fw_send, fw_recv, lw_land, fw_token = _forward_start(lw_land, "ag_fwd_start")
    dsk = _row(jnp.repeat(d_skip[0, 0] + d_skip[0, 1], HP)) + fw_token[0:1, 0:1]
    ws_m = w_spatial[0].astype(_MXU)
    bsT = jnp.pad(b_spatial[0].T, ((0, 0), (0, 120)))
    mixp = (dsk, ssd_norm_g, gm_norm_g, gm_norm_b, ws_m, bsT)
    yssd, ygm = _mix_fwd(yf, yb, p, xbc, *mixp)
    gb, gc1, gc2 = _forward_wait(fw_send, fw_recv, lw_land, yssd, "ag_fwd_wait")

    gb, gc1, gc2 = lax.optimization_barrier(
        [lax.dynamic_update_index_in_dim(g, mine, me, 0) for g, mine in zip((gb, gc1, gc2), late)])
    w_ssd_f = gb[:, 0:sq].reshape(NDEV * sq, D)
    w_gm_f = gb[:, sq:2 * sq].reshape(NDEV * sq, D)
    w_out_f = gb[:, 2 * sq:3 * sq].reshape(NDEV * sq, D)
    w_ff2_f = gb[:, 3 * sq:3 * sq + ffr].reshape(NDEV * ffr, D)
    assert HFF == (NDEV // 2) * ffc
    hd = NDEV // 2
    w13i = jnp.concatenate([g[k] for t in range(2) for g in (gc1, gc2) for k in range(t * hd, (t + 1) * hd)],
                           axis=0)
    a1, a2, merged, out, r1, h2 = _merge_fwd(yssd, ygm, p, b_gate, w_ssd_f, w_gm_f, w_out_f,
                                             xn, modx, ln1_g, ln1_b)
    f13, ff = _mm_f13_glu(h2, w13i)

    dr2, do2, st2, loss_slab = _mm_o2_res2(ff, w_ff2_f, r1, tgt, modx, ln1_g, ln1_b, ln2_g, ln2_b)
    df13 = _mm_dff_glu(do2, w_ff2_f, f13)
    dw_ff2 = _mm(ff, do2, "tn", _MXU, "mm_dw_ff2", tm_cap=512, tk_cap=L)
    dw13i = _mm(df13, h2, "tn", _MXU, "mm_dw13", tm_cap=512, tk_cap=L)

    def owner_blocks(first):
        return jnp.concatenate([dw13i[t * 2 * HFF + first:t * 2 * HFF + first + HFF].reshape(NDEV // 2, ffc, D)
                                for t in range(2)], axis=0)

    xff = [dw_ff2.reshape(NDEV, ffr, D), owner_blocks(0), owner_blocks(HFF)]
    ff_send, ff_recv, ff_src, ff_land, ff_token = _exchange_start(xff, "xchg_ff_start")
    modx = modx + ff_token[0, 0]
    dr1, dout, st1 = _mm_dh2_res1bwd(df13, w13i, dr2, r1, out, modx, ln1_g, ln1_b)
    dw_out = _mm(merged, dout, "tn", _MXU, "mm_dw_out")
    dp = _dp_fill(lax.empty((L + TL, NPJ), _MXU))
    dp, da1, da2, stg, dys, dym = _merge_bwd(dout, a1, a2, p, b_gate, w_out_f, w_ssd_f, w_gm_f, dp)
    dw_ssd = _mm(yssd, da1, "tn", _MXU, "mm_dw_ssd")
    dw_gm = _mm(ygm, da2, "tn", _MXU, "mm_dw_gm")
    xsq = [jnp.concatenate([dw_ssd.reshape(NDEV, sq, D), dw_gm.reshape(NDEV, sq, D),
                            dw_out.reshape(NDEV, sq, D)], axis=1)]
    sq_send, sq_recv, sq_src, sq_land, sq_token = _exchange_start(xsq, "xchg_sq_start")
    mixp = (dsk + sq_token[0:1, 0:1],) + mixp[1:]
    dp, dyd, stm, dws, dbsT = _mix_bwd(dys, dym, yf, yb, p, xbc, dp, *mixp)
    dxf, dxb, ddf, ddb, sts = _ssd2_bwd(xbc, p, prm, dsk, dyd, hpf, hpb)
    dp, dcw, dcb = _conv_bwd(dxf, dxb, p, conv_w8, conv_b, dp)
    dp, std = _dt_bwd(ddf, ddb, dp)
    hw = D // 2
    xin_a = [_blocks_from_perm(_mm(h1[:, :hw], dp, "tn", _MXU, "mm_dw_in_a", tk_cap=L + TL), w_in.shape[2])]
    ina_send, ina_recv, ina_src, ina_land, ina_token = _exchange_start(xin_a, "xchg_in_a_start")
    h1b, ina_token = lax.optimization_barrier((h1[:, hw:], ina_token))
    xin_b = [_blocks_from_perm(_mm(h1b, dp, "tn", _MXU, "mm_dw_in_b", tk_cap=L + TL), w_in.shape[2])]
    inb_send, inb_recv, inb_src, inb_land, inb_token = _exchange_start(xin_b, "xchg_in_b_start")
    modx = modx + (ina_token[0, 0] + inb_token[0, 0])
    grad_x, st0 = _mm_dh1_ln0bwd(dp, w_in_p, dr1, xl, cx, g0, b0, modx, modc)

    zero = jnp.zeros((D,), F32)
    dmod = jnp.stack([jnp.concatenate([st0[0], st0[1], st1[4], st1[1], st1[0], st2[2]]),
                      jnp.concatenate([st0[2], st0[3], zero, zero, zero, zero])])
    g16 = _all_gather(_pad_rows(dmod, 8), "ag_dmod")[:, 0:2, :].reshape(16, 6 * D)
    g16_sh = lax.dynamic_slice(g16, (0, ada_n * me), (16, ada_n))
    c16b = jnp.stack([c_all, jnp.broadcast_to(_row(c_ctx), (NDEV, D))], axis=1).reshape(16, D)
    dw_ada, db_ada8, dcc8 = _ada_bwd(c16b, g16, g16_sh, w_ada[0])

    part = dict(
        c_ctx=dcc8[0], ln0_g=st0[4], ln0_b=st0[5], conv_w=dcw[0:5], conv_b=dcb[0],
        dt_bias=std[0, 0:32], a_log=sts[0, 0:32], d_skip=jnp.tile(sts[1, 0:16], 2),
        ssd_norm_g=stm[0], gm_norm_g=stm[1], gm_norm_b=stm[2], w_spatial=dws,
        b_spatial=dbsT[:, 0:8].T, b_gate=stg[0], ln1_g=st1[2], ln1_b=st1[3], ln2_g=st2[0], ln2_b=st2[1],
        loss=loss_slab[0, 0:1])
    pnames = [n for n in part if n != "w_spatial"]
    g_small, g_ws = _all_gather_multi(
        [_slab([part[n] for n in pnames], 24), dws.reshape(Q, D).astype(_MXU)], "ag_smallgrads")
    small = dict(zip(pnames, _unslab(_sum8(g_small, "sum_smallgrads"), [part[n].shape for n in pnames])))
    small["w_spatial"] = _sum8(g_ws, "sum_w_spatial").reshape(dws.shape)
    loss = small["loss"][0]
    grads = {n: small[n].reshape(W[n].shape) for n in small if n not in ("conv_w", "loss")}
    grads["conv_w"] = lax.dynamic_slice(small["conv_w"], (0, cw_n * me), (5, cw_n)).reshape(conv_w.shape)
    grads["b_ada"] = db_ada8[0:1]
    grads["w_ada"] = dw_ada.reshape(w_ada.shape)

    delta, new_m, new_v = {}, {}, {}

    def adam_group(names, rows, tag, align=0):
        shapes = [W[n].shape for n in names]
        outs = _adamw(*[_slab([src[n] for n in names], rows, align) for src in (grads, W, M, V)], tag)
        for res, slab in zip((delta, new_m, new_v), outs):
            for n, a in zip(names, _unslab(slab, shapes, align)):
                res[n] = a

    adam_group(REPL + ["conv_w"], SMALL_ROWS, "adamw_small")
    res = _adamw(grads["w_ada"][0], w_ada[0], m_w_ada[0], v_w_ada[0], "adamw_w_ada")
    delta["w_ada"], new_m["w_ada"], new_v["w_ada"] = [a[None] for a in res]

    rff = _exchange_wait(ff_send, ff_recv, ff_src, ff_land, st0, "xchg_ff_wait")
    rsq = _exchange_wait(sq_send, sq_recv, sq_src, sq_land, rff[0], "xchg_sq_wait")
    rin_a = _exchange_wait(ina_send, ina_recv, ina_src, ina_land, delta["ln2_b"], "xchg_in_a_wait")
    rin_b = _exchange_wait(inb_send, inb_recv, inb_src, inb_land, rin_a[0], "xchg_in_b_wait")
    rin = jnp.concatenate([rin_a[0], rin_b[0]], axis=1)

    def own(blocks):
        return lax.dynamic_index_in_dim(blocks, me, 0, keepdims=False)

    own_in = jnp.concatenate([own(xin_a[0]), own(xin_b[0])], axis=0)
    for n, r8, mine, row0, tr in (
            ("w_ff2", rff[0], own(xff[0]), 0, ffr // 2), ("w_ssd_proj", rsq[0], own(xsq[0]), 0, sq),
            ("w_gm_proj", rsq[0], own(xsq[0]), sq, sq), ("w_out", rsq[0], own(xsq[0]), 2 * sq, sq),
            ("w_in", rin, own_in, 0, 256)):
        res = _adamw_sum(r8, mine, W[n][0], M[n][0], V[n][0], row0, tr, "adamw_" + n)
        grads[n], delta[n], new_m[n], new_v[n] = [a[None] for a in res]
    for n, r8, mine in (("w_ff1", rff[1], own(xff[1])), ("w_ff3", rff[2], own(xff[2]))):
        res = _adamw_sum(r8, mine, _t(W[n][0]), _t(M[n][0]), _t(V[n][0]), 0, ffc // 2, "adamw_" + n)
        grads[n], delta[n], new_m[n], new_v[n] = [_t(a)[None] for a in res]

    return (loss, grad_x[None], *[grads[n] for n in WEIGHTS], *[delta[n] for n in WEIGHTS],
            *[new_m[n] for n in WEIGHTS], *[new_v[n] for n in WEIGHTS])
```
